```python
import math
import jax, jax.numpy as jnp
from jax import lax
import numpy as np

D_MODEL = 1024
BATCH = 8
SEQ = 2048
DEPTH = 2

N_EVEN = (DEPTH + 1) // 2
N_ODD = DEPTH // 2
NORM_EPS = 1e-6

S5_WIDTH = D_MODEL // 2
S5_GROUP = 16
S5_GROUPS = S5_WIDTH // S5_GROUP
S5_STATE = 64
S5_DT_MIN = 1e-3
S5_DT_MAX = 1e-1

HG_WIDTH = D_MODEL // 2
HG_HEADS = 4
HG_KDIM = HG_WIDTH // HG_HEADS
HG_VDIM = HG_WIDTH // HG_HEADS
HG_CHUNK = 64

IN0_COLS = S5_WIDTH + 2 * HG_HEADS * HG_KDIM + 2 * HG_WIDTH

ATT_HEAD_DIM = 64
ATT_HEADS_PER_GROUP = 8
ATT_BRANCHES = ((128, 1), (512, 4), (2048, 16))
N_BRANCH = len(ATT_BRANCHES)
ATT_BLOCK = 128
ATT_GROUP_WIDTH = ATT_HEADS_PER_GROUP * ATT_HEAD_DIM
IN1_COLS = 3 * N_BRANCH * ATT_GROUP_WIDTH
ROT_DIM = ATT_HEAD_DIM // 4
ROPE_THETA = 500000.0

D_FF = 2816
CONV_W = 3

kernel_name = "hybrid_s5_hgrn2_dilated_convffn"


def _rmsnorm(x, g):
    xf = x.astype(jnp.float32)
    y = xf * lax.rsqrt(jnp.mean(xf * xf, axis=-1, keepdims=True) + NORM_EPS)
    return (y * g.astype(jnp.float32)).astype(x.dtype)


def _cplx_scan_op(e1, e2):
    a1r, a1i, b1r, b1i = e1
    a2r, a2i, b2r, b2i = e2
    ar = a2r * a1r - a2i * a1i
    ai = a2r * a1i + a2i * a1r
    br = a2r * b1r - a2i * b1i + b2r
    bi = a2r * b1i + a2i * b1r + b2i
    return (ar, ai, br, bi)


def _s5_mixer(u, A_re, A_im, log_dt, B_re, B_im, C_re, C_im, Dd, glu_w, glu_b):
    Bsz, L, _ = u.shape
    f32 = jnp.float32
    A_re, A_im = A_re.astype(f32), A_im.astype(f32)
    B_re, B_im = B_re.astype(f32), B_im.astype(f32)
    C_re, C_im = C_re.astype(f32), C_im.astype(f32)
    ug = u.astype(f32).reshape(Bsz, L, S5_GROUPS, S5_GROUP)
    dt = jnp.exp(log_dt.astype(f32))[:, None]
    mag = jnp.exp(A_re * dt)
    ab_re = mag * jnp.cos(A_im * dt)
    ab_im = mag * jnp.sin(A_im * dt)
    den = A_re * A_re + A_im * A_im
    nr, ni = ab_re - 1.0, ab_im
    c_re = (nr * A_re + ni * A_im) / den
    c_im = (ni * A_re - nr * A_im) / den
    Bb_re = c_re[..., None] * B_re - c_im[..., None] * B_im
    Bb_im = c_re[..., None] * B_im + c_im[..., None] * B_re
    bu_re = jnp.einsum('gpc,blgc->blgp', Bb_re, ug)
    bu_im = jnp.einsum('gpc,blgc->blgp', Bb_im, ug)
    a_re = jnp.broadcast_to(ab_re[None, None], (1, L, S5_GROUPS, S5_STATE))
    a_im = jnp.broadcast_to(ab_im[None, None], (1, L, S5_GROUPS, S5_STATE))
    _, _, x_re, x_im = lax.associative_scan(_cplx_scan_op, (a_re, a_im, bu_re, bu_im), axis=1)
    y = (jnp.einsum('gcp,blgp->blgc', C_re, x_re)
         - jnp.einsum('gcp,blgp->blgc', C_im, x_im)
         + Dd.astype(f32) * ug)
    z = jax.nn.gelu(y.reshape(Bsz, L, S5_WIDTH))
    return z * jax.nn.sigmoid(z @ glu_w.astype(f32) + glu_b.astype(f32))


def _hgrn2_mixer(xq, xf, xi, xg, lb, norm_g):
    Bsz, L, _ = xq.shape
    f32 = jnp.float32
    nc = L // HG_CHUNK
    q = jax.nn.silu(xq.astype(f32)).reshape(Bsz, L, HG_HEADS, HG_KDIM)
    f = lb + (1.0 - lb) * jax.nn.sigmoid(xf.astype(f32))
    k = (1.0 - f).reshape(Bsz, L, HG_HEADS, HG_KDIM)
    logf = jnp.log(f).reshape(Bsz, L, HG_HEADS, HG_KDIM)
    v = xi.astype(f32).reshape(Bsz, L, HG_HEADS, HG_VDIM)

    def to_chunks(t):
        return t.reshape(Bsz, nc, HG_CHUNK, HG_HEADS, -1).transpose(1, 0, 3, 2, 4)

    causal = jnp.tril(jnp.ones((HG_CHUNK, HG_CHUNK), dtype=bool))[None, None, :, :, None]

    def step(S, inp):
        qc, kc, gc, vc = inp
        b = jnp.cumsum(gc, axis=2)
        o_inter = jnp.einsum('bhtk,bhkv->bhtv', qc * jnp.exp(b), S)
        diff = b[:, :, :, None, :] - b[:, :, None, :, :]
        decay = jnp.exp(jnp.where(causal, diff, -jnp.inf))
        att = jnp.einsum('bhtk,bhsk,bhtsk->bhts', qc, kc, decay)
        o_intra = jnp.einsum('bhts,bhsv->bhtv', att, vc)
        b_last = b[:, :, -1, :]
        S_new = (jnp.exp(b_last)[..., None] * S
                 + jnp.einsum('bhsk,bhsv->bhkv', kc * jnp.exp(b_last[:, :, None, :] - b), vc))
        return S_new, o_inter + o_intra

    S0 = jnp.zeros((Bsz, HG_HEADS, HG_KDIM, HG_VDIM), f32)
    _, o = lax.scan(step, S0, (to_chunks(q), to_chunks(k), to_chunks(logf), to_chunks(v)))
    o = o.transpose(1, 0, 3, 2, 4).reshape(Bsz, L, HG_HEADS, HG_VDIM)
    o = o * lax.rsqrt(jnp.mean(o * o, axis=-1, keepdims=True) + NORM_EPS)
    o = o * norm_g.astype(f32).reshape(HG_HEADS, HG_VDIM)
    return o.reshape(Bsz, L, HG_WIDTH) * jax.nn.silu(xg.astype(f32))


def _partial_rotary(t, positions):
    half = ROT_DIM // 2
    inv_freq = ROPE_THETA ** (-jnp.arange(half, dtype=jnp.float32) * 2.0 / ROT_DIM)
    ang = positions.astype(jnp.float32)[..., None] * inv_freq
    cos = jnp.cos(ang)[:, :, None, :]
    sin = jnp.sin(ang)[:, :, None, :]
    x1 = t[..., :half]
    x2 = t[..., half:ROT_DIM]
    return jnp.concatenate([x1 * cos - x2 * sin, x2 * cos + x1 * sin, t[..., ROT_DIM:]], axis=-1)


def _dilated_branch(q, k, v, dil, steps):
    Bsz, L, H, E = q.shape
    M = L // dil
    nb = -(-M // ATT_BLOCK)
    Mp = nb * ATT_BLOCK

    def to_blocks(t):
        t = t.reshape(Bsz, M, dil, H, E).transpose(0, 2, 3, 1, 4)
        t = jnp.pad(t, ((0, 0), (0, 0), (0, 0), (0, Mp - M), (0, 0)))
        return t.reshape(Bsz, dil, H, nb, ATT_BLOCK, E)

    def with_prev(t):
        prev = jnp.pad(t[:, :, :, :-1], ((0, 0), (0, 0), (0, 0), (1, 0), (0, 0), (0, 0)))
        return jnp.concatenate([prev, t], axis=-2)

    qb = to_blocks(q) * (E ** -0.5)
    kc = with_prev(to_blocks(k))
    vc = with_prev(to_blocks(v))
    s = jnp.einsum('bdhnqe,bdhnke->bdhnqk', qb, kc)
    qi = jnp.arange(ATT_BLOCK)[:, None] + ATT_BLOCK
    kj = jnp.arange(2 * ATT_BLOCK)[None, :]
    back = qi - kj
    in_range = (jnp.arange(nb)[:, None, None] * ATT_BLOCK - ATT_BLOCK + kj[None]) >= 0
    valid = (back >= 0) & (back <= steps) & in_range
    s = jnp.where(valid, s, -jnp.inf)
    m = jnp.max(s, axis=-1, keepdims=True)
    p = jnp.exp(s - m)
    den = jnp.sum(p, axis=-1, keepdims=True)
    o = jnp.einsum('bdhnqk,bdhnke->bdhnqe', p, vc) / den
    lse = (m + jnp.log(den))[..., 0]
    o = o.reshape(Bsz, dil, H, Mp, E)[:, :, :, :M].transpose(0, 3, 1, 2, 4).reshape(Bsz, L, H, E)
    lse = lse.reshape(Bsz, dil, H, Mp)[..., :M].transpose(0, 3, 1, 2).reshape(Bsz, L, H)
    return o, lse


def _dilated_attention(h, positions, w_qkv, w_o):
    Bsz, L, _ = h.shape
    f32 = jnp.float32
    qkv = (h @ w_qkv).astype(f32).reshape(Bsz, L, 3, N_BRANCH * ATT_HEADS_PER_GROUP, ATT_HEAD_DIM)
    q = _partial_rotary(qkv[:, :, 0], positions)
    k = _partial_rotary(qkv[:, :, 1], positions)
    v = qkv[:, :, 2]
    outs, lses = [], []
    for g, (win, dil) in enumerate(ATT_BRANCHES):
        sl = slice(g * ATT_HEADS_PER_GROUP, (g + 1) * ATT_HEADS_PER_GROUP)
        o_g, lse_g = _dilated_branch(q[:, :, sl], k[:, :, sl], v[:, :, sl], dil, win // dil)
        outs.append(o_g)
        lses.append(lse_g)
    alpha = jax.nn.softmax(jnp.stack(lses, axis=0), axis=0)
    o = jnp.sum(alpha[..., None] * jnp.stack(outs, axis=0), axis=0)
    return o.reshape(Bsz, L, ATT_GROUP_WIDTH).astype(h.dtype) @ w_o


def _conv_ffn(h, w_in, conv_w, conv_b, w_out):
    hu = h @ w_in
    C = hu.shape[-1]
    hu = lax.conv_general_dilated(
        hu, conv_w.astype(hu.dtype)[:, None, :], window_strides=(1,),
        padding=[(CONV_W - 1, 0)], dimension_numbers=('NWC', 'WIO', 'NWC'),
        feature_group_count=C) + conv_b.astype(hu.dtype)
    a, b = hu[..., :D_FF], hu[..., D_FF:]
    return (jax.nn.silu(a) * b) @ w_out


def _fwd_setup_inputs(seed: int = 0) -> dict:
    key = jax.random.key(seed)
    ks = jax.random.split(key, 24)
    f32 = jnp.float32

    def nrm(k, shape, scale):
        return jax.random.normal(k, shape, f32) * scale

    mix_width0 = S5_WIDTH + HG_WIDTH
    n_idx = jnp.arange(S5_STATE, dtype=f32)
    return {
        "x": nrm(ks[0], (BATCH, SEQ, D_MODEL), 1.0),
        "positions": jnp.broadcast_to(jnp.arange(SEQ, dtype=jnp.int32), (BATCH, SEQ)),
        "norm_mix": 1.0 + nrm(ks[1], (DEPTH, D_MODEL), 0.02),
        "norm_ffn": 1.0 + nrm(ks[2], (DEPTH, D_MODEL), 0.02),
        "norm_final": 1.0 + nrm(ks[3], (D_MODEL,), 0.02),
        "mix_w_in": nrm(ks[4], (N_EVEN, D_MODEL, IN0_COLS), D_MODEL ** -0.5),
        "mix_w_out": nrm(ks[5], (N_EVEN, mix_width0, D_MODEL), mix_width0 ** -0.5),
        "s5_A_re": -0.5 + nrm(ks[6], (N_EVEN, S5_GROUPS, S5_STATE), 0.01),
        "s5_A_im": math.pi * n_idx + nrm(ks[7], (N_EVEN, S5_GROUPS, S5_STATE), 0.01),
        "s5_log_dt": jax.random.uniform(ks[8], (N_EVEN, S5_GROUPS), f32,
                                        math.log(S5_DT_MIN), math.log(S5_DT_MAX)),
        "s5_B_re": nrm(ks[9], (N_EVEN, S5_GROUPS, S5_STATE, S5_GROUP), (2 * S5_GROUP) ** -0.5),
        "s5_B_im": nrm(ks[10], (N_EVEN, S5_GROUPS, S5_STATE, S5_GROUP), (2 * S5_GROUP) ** -0.5),
        "s5_C_re": nrm(ks[11], (N_EVEN, S5_GROUPS, S5_GROUP, S5_STATE), (2 * S5_STATE) ** -0.5),
        "s5_C_im": nrm(ks[12], (N_EVEN, S5_GROUPS, S5_GROUP, S5_STATE), (2 * S5_STATE) ** -0.5),
        "s5_D": nrm(ks[13], (N_EVEN, S5_GROUPS, S5_GROUP), 1.0),
        "s5_glu_w": nrm(ks[14], (N_EVEN, S5_WIDTH, S5_WIDTH), S5_WIDTH ** -0.5),
        "s5_glu_b": nrm(ks[15], (N_EVEN, S5_WIDTH), 0.01),
        "hgrn_gamma": nrm(ks[16], (N_EVEN + 1, HG_HEADS * HG_KDIM), 0.1),
        "hgrn_norm": 1.0 + nrm(ks[17], (N_EVEN, HG_WIDTH), 0.02),
        "att_w_qkv": nrm(ks[18], (N_ODD, D_MODEL, IN1_COLS), D_MODEL ** -0.5),
        "att_w_o": nrm(ks[19], (N_ODD, ATT_GROUP_WIDTH, D_MODEL), ATT_GROUP_WIDTH ** -0.5),
        "ffn_w_in": nrm(ks[20], (DEPTH, D_MODEL, 2 * D_FF), D_MODEL ** -0.5),
        "ffn_conv_w": nrm(ks[21], (DEPTH, CONV_W, 2 * D_FF), CONV_W ** -0.5),
        "ffn_conv_b": nrm(ks[22], (DEPTH, 2 * D_FF), 0.01),
        "ffn_w_out": nrm(ks[23], (DEPTH, D_FF, D_MODEL), D_FF ** -0.5),
    }


def _fwd_reference(x, positions, norm_mix, norm_ffn, norm_final, mix_w_in, mix_w_out,
              s5_A_re, s5_A_im, s5_log_dt, s5_B_re, s5_B_im, s5_C_re, s5_C_im, s5_D,
              s5_glu_w, s5_glu_b, hgrn_gamma, hgrn_norm, att_w_qkv, att_w_o,
              ffn_w_in, ffn_conv_w, ffn_conv_b, ffn_w_out):
    lb_all = jnp.cumsum(jax.nn.softmax(hgrn_gamma.astype(jnp.float32), axis=0), axis=0)
    h = x
    c_q = S5_WIDTH
    c_f = c_q + HG_HEADS * HG_KDIM
    c_i = c_f + HG_HEADS * HG_KDIM
    c_g = c_i + HG_WIDTH
    for layer in range(DEPTH):
        hn = _rmsnorm(h, norm_mix[layer])
        j = layer // 2
        if layer % 2 == 0:
            proj = hn @ mix_w_in[j]
            oa = _s5_mixer(proj[..., :c_q], s5_A_re[j], s5_A_im[j], s5_log_dt[j],
                           s5_B_re[j], s5_B_im[j], s5_C_re[j], s5_C_im[j], s5_D[j],
                           s5_glu_w[j], s5_glu_b[j])
            ob = _hgrn2_mixer(proj[..., c_q:c_f], proj[..., c_f:c_i], proj[..., c_i:c_g],
                              proj[..., c_g:], lb_all[j], hgrn_norm[j])
            mix = jnp.concatenate([oa, ob], axis=-1).astype(h.dtype) @ mix_w_out[j]
        else:
            mix = _dilated_attention(hn, positions, att_w_qkv[j], att_w_o[j])
        h = h + mix.astype(h.dtype)
        ff = _conv_ffn(_rmsnorm(h, norm_ffn[layer]), ffn_w_in[layer], ffn_conv_w[layer],
                       ffn_conv_b[layer], ffn_w_out[layer])
        h = h + ff.astype(h.dtype)
    return _rmsnorm(h, norm_final)


import jax as _jax
import jax.numpy as _jnp

TWIN_FORMAT = 'train_step'
FWD_PARAMS = ['x', 'positions', 'norm_mix', 'norm_ffn', 'norm_final', 'mix_w_in', 'mix_w_out', 's5_A_re', 's5_A_im', 's5_log_dt', 's5_B_re', 's5_B_im', 's5_C_re', 's5_C_im', 's5_D', 's5_glu_w', 's5_glu_b', 'hgrn_gamma', 'hgrn_norm', 'att_w_qkv', 'att_w_o', 'ffn_w_in', 'ffn_conv_w', 'ffn_conv_b', 'ffn_w_out']
TWIN_WEIGHTS = ['norm_mix', 'norm_ffn', 'norm_final', 'mix_w_in', 'mix_w_out', 's5_A_re', 's5_A_im', 's5_log_dt', 's5_B_re', 's5_B_im', 's5_C_re', 's5_C_im', 's5_D', 's5_glu_w', 's5_glu_b', 'hgrn_gamma', 'hgrn_norm', 'att_w_qkv', 'att_w_o', 'ffn_w_in', 'ffn_conv_w', 'ffn_conv_b', 'ffn_w_out']
TWIN_DIFF_INPUT = 'x'
TWIN_INPUTS = ['x', 'positions', 'norm_mix', 'norm_ffn', 'norm_final', 'mix_w_in', 'mix_w_out', 's5_A_re', 's5_A_im', 's5_log_dt', 's5_B_re', 's5_B_im', 's5_C_re', 's5_C_im', 's5_D', 's5_glu_w', 's5_glu_b', 'hgrn_gamma', 'hgrn_norm', 'att_w_qkv', 'att_w_o', 'ffn_w_in', 'ffn_conv_w', 'ffn_conv_b', 'ffn_w_out', 'loss_target', 'm_norm_mix', 'm_norm_ffn', 'm_norm_final', 'm_mix_w_in', 'm_mix_w_out', 'm_s5_A_re', 'm_s5_A_im', 'm_s5_log_dt', 'm_s5_B_re', 'm_s5_B_im', 'm_s5_C_re', 'm_s5_C_im', 'm_s5_D', 'm_s5_glu_w', 'm_s5_glu_b', 'm_hgrn_gamma', 'm_hgrn_norm', 'm_att_w_qkv', 'm_att_w_o', 'm_ffn_w_in', 'm_ffn_conv_w', 'm_ffn_conv_b', 'm_ffn_w_out', 'v_norm_mix', 'v_norm_ffn', 'v_norm_final', 'v_mix_w_in', 'v_mix_w_out', 'v_s5_A_re', 'v_s5_A_im', 'v_s5_log_dt', 'v_s5_B_re', 'v_s5_B_im', 'v_s5_C_re', 'v_s5_C_im', 'v_s5_D', 'v_s5_glu_w', 'v_s5_glu_b', 'v_hgrn_gamma', 'v_hgrn_norm', 'v_att_w_qkv', 'v_att_w_o', 'v_ffn_w_in', 'v_ffn_conv_w', 'v_ffn_conv_b', 'v_ffn_w_out']
TWIN_OUTPUTS = ['loss', 'grad_x', 'grad_norm_mix', 'grad_norm_ffn', 'grad_norm_final', 'grad_mix_w_in', 'grad_mix_w_out', 'grad_s5_A_re', 'grad_s5_A_im', 'grad_s5_log_dt', 'grad_s5_B_re', 'grad_s5_B_im', 'grad_s5_C_re', 'grad_s5_C_im', 'grad_s5_D', 'grad_s5_glu_w', 'grad_s5_glu_b', 'grad_hgrn_gamma', 'grad_hgrn_norm', 'grad_att_w_qkv', 'grad_att_w_o', 'grad_ffn_w_in', 'grad_ffn_conv_w', 'grad_ffn_conv_b', 'grad_ffn_w_out', 'delta_norm_mix', 'delta_norm_ffn', 'delta_norm_final', 'delta_mix_w_in', 'delta_mix_w_out', 'delta_s5_A_re', 'delta_s5_A_im', 'delta_s5_log_dt', 'delta_s5_B_re', 'delta_s5_B_im', 'delta_s5_C_re', 'delta_s5_C_im', 'delta_s5_D', 'delta_s5_glu_w', 'delta_s5_glu_b', 'delta_hgrn_gamma', 'delta_hgrn_norm', 'delta_att_w_qkv', 'delta_att_w_o', 'delta_ffn_w_in', 'delta_ffn_conv_w', 'delta_ffn_conv_b', 'delta_ffn_w_out', 'new_m_norm_mix', 'new_m_norm_ffn', 'new_m_norm_final', 'new_m_mix_w_in', 'new_m_mix_w_out', 'new_m_s5_A_re', 'new_m_s5_A_im', 'new_m_s5_log_dt', 'new_m_s5_B_re', 'new_m_s5_B_im', 'new_m_s5_C_re', 'new_m_s5_C_im', 'new_m_s5_D', 'new_m_s5_glu_w', 'new_m_s5_glu_b', 'new_m_hgrn_gamma', 'new_m_hgrn_norm', 'new_m_att_w_qkv', 'new_m_att_w_o', 'new_m_ffn_w_in', 'new_m_ffn_conv_w', 'new_m_ffn_conv_b', 'new_m_ffn_w_out', 'new_v_norm_mix', 'new_v_norm_ffn', 'new_v_norm_final', 'new_v_mix_w_in', 'new_v_mix_w_out', 'new_v_s5_A_re', 'new_v_s5_A_im', 'new_v_s5_log_dt', 'new_v_s5_B_re', 'new_v_s5_B_im', 'new_v_s5_C_re', 'new_v_s5_C_im', 'new_v_s5_D', 'new_v_s5_glu_w', 'new_v_s5_glu_b', 'new_v_hgrn_gamma', 'new_v_hgrn_norm', 'new_v_att_w_qkv', 'new_v_att_w_o', 'new_v_ffn_w_in', 'new_v_ffn_conv_w', 'new_v_ffn_conv_b', 'new_v_ffn_w_out']
TWIN_LEAF_KINDS = {'loss': 'loss', 'grad_x': 'grad_x', 'grad_norm_mix': 'grad_w', 'grad_norm_ffn': 'grad_w', 'grad_norm_final': 'grad_w', 'grad_mix_w_in': 'grad_w', 'grad_mix_w_out': 'grad_w', 'grad_s5_A_re': 'grad_w', 'grad_s5_A_im': 'grad_w', 'grad_s5_log_dt': 'grad_w', 'grad_s5_B_re': 'grad_w', 'grad_s5_B_im': 'grad_w', 'grad_s5_C_re': 'grad_w', 'grad_s5_C_im': 'grad_w', 'grad_s5_D': 'grad_w', 'grad_s5_glu_w': 'grad_w', 'grad_s5_glu_b': 'grad_w', 'grad_hgrn_gamma': 'grad_w', 'grad_hgrn_norm': 'grad_w', 'grad_att_w_qkv': 'grad_w', 'grad_att_w_o': 'grad_w', 'grad_ffn_w_in': 'grad_w', 'grad_ffn_conv_w': 'grad_w', 'grad_ffn_conv_b': 'grad_w', 'grad_ffn_w_out': 'grad_w', 'delta_norm_mix': 'delta_w', 'delta_norm_ffn': 'delta_w', 'delta_norm_final': 'delta_w', 'delta_mix_w_in': 'delta_w', 'delta_mix_w_out': 'delta_w', 'delta_s5_A_re': 'delta_w', 'delta_s5_A_im': 'delta_w', 'delta_s5_log_dt': 'delta_w', 'delta_s5_B_re': 'delta_w', 'delta_s5_B_im': 'delta_w', 'delta_s5_C_re': 'delta_w', 'delta_s5_C_im': 'delta_w', 'delta_s5_D': 'delta_w', 'delta_s5_glu_w': 'delta_w', 'delta_s5_glu_b': 'delta_w', 'delta_hgrn_gamma': 'delta_w', 'delta_hgrn_norm': 'delta_w', 'delta_att_w_qkv': 'delta_w', 'delta_att_w_o': 'delta_w', 'delta_ffn_w_in': 'delta_w', 'delta_ffn_conv_w': 'delta_w', 'delta_ffn_conv_b': 'delta_w', 'delta_ffn_w_out': 'delta_w', 'new_m_norm_mix': 'new_m', 'new_m_norm_ffn': 'new_m', 'new_m_norm_final': 'new_m', 'new_m_mix_w_in': 'new_m', 'new_m_mix_w_out': 'new_m', 'new_m_s5_A_re': 'new_m', 'new_m_s5_A_im': 'new_m', 'new_m_s5_log_dt': 'new_m', 'new_m_s5_B_re': 'new_m', 'new_m_s5_B_im': 'new_m', 'new_m_s5_C_re': 'new_m', 'new_m_s5_C_im': 'new_m', 'new_m_s5_D': 'new_m', 'new_m_s5_glu_w': 'new_m', 'new_m_s5_glu_b': 'new_m', 'new_m_hgrn_gamma': 'new_m', 'new_m_hgrn_norm': 'new_m', 'new_m_att_w_qkv': 'new_m', 'new_m_att_w_o': 'new_m', 'new_m_ffn_w_in': 'new_m', 'new_m_ffn_conv_w': 'new_m', 'new_m_ffn_conv_b': 'new_m', 'new_m_ffn_w_out': 'new_m', 'new_v_norm_mix': 'new_v', 'new_v_norm_ffn': 'new_v', 'new_v_norm_final': 'new_v', 'new_v_mix_w_in': 'new_v', 'new_v_mix_w_out': 'new_v', 'new_v_s5_A_re': 'new_v', 'new_v_s5_A_im': 'new_v', 'new_v_s5_log_dt': 'new_v', 'new_v_s5_B_re': 'new_v', 'new_v_s5_B_im': 'new_v', 'new_v_s5_C_re': 'new_v', 'new_v_s5_C_im': 'new_v', 'new_v_s5_D': 'new_v', 'new_v_s5_glu_w': 'new_v', 'new_v_s5_glu_b': 'new_v', 'new_v_hgrn_gamma': 'new_v', 'new_v_hgrn_norm': 'new_v', 'new_v_att_w_qkv': 'new_v', 'new_v_att_w_o': 'new_v', 'new_v_ffn_w_in': 'new_v', 'new_v_ffn_conv_w': 'new_v', 'new_v_ffn_conv_b': 'new_v', 'new_v_ffn_w_out': 'new_v'}


def _forward(args):
    return _fwd_reference(*[args[k] for k in FWD_PARAMS])


def _output_shape():
    out = _jax.eval_shape(lambda: _forward(_fwd_setup_inputs(0)))
    return out.shape, out.dtype

N_MICROBATCH = 1
ADAM_LR = 0.001
ADAM_B1 = 0.9
ADAM_B2 = 0.999
ADAM_EPS = 1e-08
ADAM_WD = 0.01
ADAM_STEP = 10
PER_EXAMPLE_BATCH_AXIS = {'x': 0, 'positions': 0, 'loss_target': 0}
SHARED_INPUTS = []
_WEIGHT_DTYPES = {'norm_mix': _jnp.float32, 'norm_ffn': _jnp.float32, 'norm_final': _jnp.float32, 'mix_w_in': _jnp.float32, 'mix_w_out': _jnp.float32, 's5_A_re': _jnp.float32, 's5_A_im': _jnp.float32, 's5_log_dt': _jnp.float32, 's5_B_re': _jnp.float32, 's5_B_im': _jnp.float32, 's5_C_re': _jnp.float32, 's5_C_im': _jnp.float32, 's5_D': _jnp.float32, 's5_glu_w': _jnp.float32, 's5_glu_b': _jnp.float32, 'hgrn_gamma': _jnp.float32, 'hgrn_norm': _jnp.float32, 'att_w_qkv': _jnp.float32, 'att_w_o': _jnp.float32, 'ffn_w_in': _jnp.float32, 'ffn_conv_w': _jnp.float32, 'ffn_conv_b': _jnp.float32, 'ffn_w_out': _jnp.float32}
MOMENT_SCALE = {'norm_mix': 6.996192e-02, 'norm_ffn': 8.912160e-02, 'norm_final': 1.598155e+01, 'mix_w_in': 5.864978e-02, 'mix_w_out': 6.887799e-02, 's5_A_re': 2.803171e-03, 's5_A_im': 2.612980e-03, 's5_log_dt': 2.519027e+00, 's5_B_re': 1.553175e-03, 's5_B_im': 1.534206e-03, 's5_C_re': 3.088229e-03, 's5_C_im': 3.065340e-03, 's5_D': 5.429342e-02, 's5_glu_w': 1.494778e-02, 's5_glu_b': 2.164710e-02, 'hgrn_gamma': 7.603522e-03, 'hgrn_norm': 8.329200e-02, 'att_w_qkv': 1.543657e-02, 'att_w_o': 2.145973e-02, 'ffn_w_in': 3.733512e-02, 'ffn_conv_w': 3.726652e-02, 'ffn_conv_b': 3.679469e-02, 'ffn_w_out': 6.092053e-02}


def _to_microbatches(a, axis):
    t = _jnp.moveaxis(a, axis, 0)
    t = t.reshape((N_MICROBATCH, t.shape[0] // N_MICROBATCH) + t.shape[1:])
    return _jnp.moveaxis(t, 1, axis + 1)


def setup_inputs(seed: int = 0) -> dict:
    inp = _fwd_setup_inputs(seed)
    key = _jax.random.fold_in(_jax.random.key(seed), 7919)
    shape, _ = _output_shape()
    out = dict(inp)
    out["loss_target"] = _jax.random.normal(_jax.random.fold_in(key, 0), shape, _jnp.float32)
    for i, name in enumerate(TWIN_WEIGHTS):
        w = inp[name].astype(_jnp.float32)
        if MOMENT_SCALE is None:
            s = _jnp.sqrt(_jnp.mean(_jnp.square(w)) + 1e-30)
        else:
            s = MOMENT_SCALE[name]
        km, kv = _jax.random.split(_jax.random.fold_in(key, i + 1))
        out[name] = w
        out["m_" + name] = s * _jax.random.normal(km, w.shape, _jnp.float32)
        out["v_" + name] = (s * s) * _jax.random.uniform(kv, w.shape, _jnp.float32, 0.5, 1.5)
    if N_MICROBATCH > 1:
        for name, axis in PER_EXAMPLE_BATCH_AXIS.items():
            out[name] = _to_microbatches(out[name], axis)
    return {'x': out['x'], 'positions': out['positions'], 'norm_mix': out['norm_mix'], 'norm_ffn': out['norm_ffn'], 'norm_final': out['norm_final'], 'mix_w_in': out['mix_w_in'], 'mix_w_out': out['mix_w_out'], 's5_A_re': out['s5_A_re'], 's5_A_im': out['s5_A_im'], 's5_log_dt': out['s5_log_dt'], 's5_B_re': out['s5_B_re'], 's5_B_im': out['s5_B_im'], 's5_C_re': out['s5_C_re'], 's5_C_im': out['s5_C_im'], 's5_D': out['s5_D'], 's5_glu_w': out['s5_glu_w'], 's5_glu_b': out['s5_glu_b'], 'hgrn_gamma': out['hgrn_gamma'], 'hgrn_norm': out['hgrn_norm'], 'att_w_qkv': out['att_w_qkv'], 'att_w_o': out['att_w_o'], 'ffn_w_in': out['ffn_w_in'], 'ffn_conv_w': out['ffn_conv_w'], 'ffn_conv_b': out['ffn_conv_b'], 'ffn_w_out': out['ffn_w_out'], 'loss_target': out['loss_target'], 'm_norm_mix': out['m_norm_mix'], 'm_norm_ffn': out['m_norm_ffn'], 'm_norm_final': out['m_norm_final'], 'm_mix_w_in': out['m_mix_w_in'], 'm_mix_w_out': out['m_mix_w_out'], 'm_s5_A_re': out['m_s5_A_re'], 'm_s5_A_im': out['m_s5_A_im'], 'm_s5_log_dt': out['m_s5_log_dt'], 'm_s5_B_re': out['m_s5_B_re'], 'm_s5_B_im': out['m_s5_B_im'], 'm_s5_C_re': out['m_s5_C_re'], 'm_s5_C_im': out['m_s5_C_im'], 'm_s5_D': out['m_s5_D'], 'm_s5_glu_w': out['m_s5_glu_w'], 'm_s5_glu_b': out['m_s5_glu_b'], 'm_hgrn_gamma': out['m_hgrn_gamma'], 'm_hgrn_norm': out['m_hgrn_norm'], 'm_att_w_qkv': out['m_att_w_qkv'], 'm_att_w_o': out['m_att_w_o'], 'm_ffn_w_in': out['m_ffn_w_in'], 'm_ffn_conv_w': out['m_ffn_conv_w'], 'm_ffn_conv_b': out['m_ffn_conv_b'], 'm_ffn_w_out': out['m_ffn_w_out'], 'v_norm_mix': out['v_norm_mix'], 'v_norm_ffn': out['v_norm_ffn'], 'v_norm_final': out['v_norm_final'], 'v_mix_w_in': out['v_mix_w_in'], 'v_mix_w_out': out['v_mix_w_out'], 'v_s5_A_re': out['v_s5_A_re'], 'v_s5_A_im': out['v_s5_A_im'], 'v_s5_log_dt': out['v_s5_log_dt'], 'v_s5_B_re': out['v_s5_B_re'], 'v_s5_B_im': out['v_s5_B_im'], 'v_s5_C_re': out['v_s5_C_re'], 'v_s5_C_im': out['v_s5_C_im'], 'v_s5_D': out['v_s5_D'], 'v_s5_glu_w': out['v_s5_glu_w'], 'v_s5_glu_b': out['v_s5_glu_b'], 'v_hgrn_gamma': out['v_hgrn_gamma'], 'v_hgrn_norm': out['v_hgrn_norm'], 'v_att_w_qkv': out['v_att_w_qkv'], 'v_att_w_o': out['v_att_w_o'], 'v_ffn_w_in': out['v_ffn_w_in'], 'v_ffn_conv_w': out['v_ffn_conv_w'], 'v_ffn_conv_b': out['v_ffn_conv_b'], 'v_ffn_w_out': out['v_ffn_w_out']}


def _loss(weights, diff, rest, loss_target):
    with _jax.named_scope("forward"):
        args = {**rest, TWIN_DIFF_INPUT: diff, **{k: w.astype(_WEIGHT_DTYPES[k]) for k, w in weights.items()}}
        y = _forward(args)
    with _jax.named_scope("loss_head"):
        err = _jnp.square(y.astype(_jnp.float32) - loss_target)
        return 0.5 * _jnp.sum(_jnp.mean(err, axis=-1)) if err.ndim else 0.5 * err


def _adamw(w, g, m, v):
    m = ADAM_B1 * m + (1.0 - ADAM_B1) * g
    v = ADAM_B2 * v + (1.0 - ADAM_B2) * _jnp.square(g)
    m_hat = m / (1.0 - ADAM_B1 ** ADAM_STEP)
    v_hat = v / (1.0 - ADAM_B2 ** ADAM_STEP)
    delta = -ADAM_LR * (m_hat / (_jnp.sqrt(v_hat) + ADAM_EPS) + ADAM_WD * w)
    return delta, m, v


def reference(x, positions, norm_mix, norm_ffn, norm_final, mix_w_in, mix_w_out, s5_A_re, s5_A_im, s5_log_dt, s5_B_re, s5_B_im, s5_C_re, s5_C_im, s5_D, s5_glu_w, s5_glu_b, hgrn_gamma, hgrn_norm, att_w_qkv, att_w_o, ffn_w_in, ffn_conv_w, ffn_conv_b, ffn_w_out, loss_target, m_norm_mix, m_norm_ffn, m_norm_final, m_mix_w_in, m_mix_w_out, m_s5_A_re, m_s5_A_im, m_s5_log_dt, m_s5_B_re, m_s5_B_im, m_s5_C_re, m_s5_C_im, m_s5_D, m_s5_glu_w, m_s5_glu_b, m_hgrn_gamma, m_hgrn_norm, m_att_w_qkv, m_att_w_o, m_ffn_w_in, m_ffn_conv_w, m_ffn_conv_b, m_ffn_w_out, v_norm_mix, v_norm_ffn, v_norm_final, v_mix_w_in, v_mix_w_out, v_s5_A_re, v_s5_A_im, v_s5_log_dt, v_s5_B_re, v_s5_B_im, v_s5_C_re, v_s5_C_im, v_s5_D, v_s5_glu_w, v_s5_glu_b, v_hgrn_gamma, v_hgrn_norm, v_att_w_qkv, v_att_w_o, v_ffn_w_in, v_ffn_conv_w, v_ffn_conv_b, v_ffn_w_out):
    given = dict(x=x, positions=positions, norm_mix=norm_mix, norm_ffn=norm_ffn, norm_final=norm_final, mix_w_in=mix_w_in, mix_w_out=mix_w_out, s5_A_re=s5_A_re, s5_A_im=s5_A_im, s5_log_dt=s5_log_dt, s5_B_re=s5_B_re, s5_B_im=s5_B_im, s5_C_re=s5_C_re, s5_C_im=s5_C_im, s5_D=s5_D, s5_glu_w=s5_glu_w, s5_glu_b=s5_glu_b, hgrn_gamma=hgrn_gamma, hgrn_norm=hgrn_norm, att_w_qkv=att_w_qkv, att_w_o=att_w_o, ffn_w_in=ffn_w_in, ffn_conv_w=ffn_conv_w, ffn_conv_b=ffn_conv_b, ffn_w_out=ffn_w_out, loss_target=loss_target, m_norm_mix=m_norm_mix, m_norm_ffn=m_norm_ffn, m_norm_final=m_norm_final, m_mix_w_in=m_mix_w_in, m_mix_w_out=m_mix_w_out, m_s5_A_re=m_s5_A_re, m_s5_A_im=m_s5_A_im, m_s5_log_dt=m_s5_log_dt, m_s5_B_re=m_s5_B_re, m_s5_B_im=m_s5_B_im, m_s5_C_re=m_s5_C_re, m_s5_C_im=m_s5_C_im, m_s5_D=m_s5_D, m_s5_glu_w=m_s5_glu_w, m_s5_glu_b=m_s5_glu_b, m_hgrn_gamma=m_hgrn_gamma, m_hgrn_norm=m_hgrn_norm, m_att_w_qkv=m_att_w_qkv, m_att_w_o=m_att_w_o, m_ffn_w_in=m_ffn_w_in, m_ffn_conv_w=m_ffn_conv_w, m_ffn_conv_b=m_ffn_conv_b, m_ffn_w_out=m_ffn_w_out, v_norm_mix=v_norm_mix, v_norm_ffn=v_norm_ffn, v_norm_final=v_norm_final, v_mix_w_in=v_mix_w_in, v_mix_w_out=v_mix_w_out, v_s5_A_re=v_s5_A_re, v_s5_A_im=v_s5_A_im, v_s5_log_dt=v_s5_log_dt, v_s5_B_re=v_s5_B_re, v_s5_B_im=v_s5_B_im, v_s5_C_re=v_s5_C_re, v_s5_C_im=v_s5_C_im, v_s5_D=v_s5_D, v_s5_glu_w=v_s5_glu_w, v_s5_glu_b=v_s5_glu_b, v_hgrn_gamma=v_hgrn_gamma, v_hgrn_norm=v_hgrn_norm, v_att_w_qkv=v_att_w_qkv, v_att_w_o=v_att_w_o, v_ffn_w_in=v_ffn_w_in, v_ffn_conv_w=v_ffn_conv_w, v_ffn_conv_b=v_ffn_conv_b, v_ffn_w_out=v_ffn_w_out)
    weights = {n: given[n] for n in TWIN_WEIGHTS}
    shared = {n: given[n] for n in SHARED_INPUTS}
    per_example = {n: given[n] for n in ['x', 'positions']}
    grad_fn = _jax.value_and_grad(_loss, argnums=(0, 1))

    def one_microbatch(ex, loss_target):
        ex = dict(ex)
        diff = ex.pop(TWIN_DIFF_INPUT)
        return grad_fn(weights, diff, {**shared, **ex}, loss_target)

    if N_MICROBATCH == 1:
        loss, (grad_w, grad_x) = one_microbatch(per_example, given["loss_target"])
    else:
        def body(carry, xs):
            loss_sum, grad_sum = carry
            l_k, (gw_k, gx_k) = one_microbatch(xs[0], xs[1])
            with _jax.named_scope("update"):
                return (loss_sum + l_k, _jax.tree.map(_jnp.add, grad_sum, gw_k)), gx_k

        init = (_jnp.zeros((), _jnp.float32), _jax.tree.map(_jnp.zeros_like, weights))
        (loss, grad_w), grad_x = _jax.lax.scan(body, init, (per_example, given["loss_target"]))
    with _jax.named_scope("update"):
        delta_w, new_m, new_v = {}, {}, {}
        for n in TWIN_WEIGHTS:
            delta_w[n], new_m[n], new_v[n] = _adamw(weights[n], grad_w[n], given["m_" + n], given["v_" + n])
    return (loss, grad_x, *[grad_w[n] for n in TWIN_WEIGHTS], *[delta_w[n] for n in TWIN_WEIGHTS],
            *[new_m[n] for n in TWIN_WEIGHTS], *[new_v[n] for n in TWIN_WEIGHTS])
```

```python
import functools
import math

import jax
import jax.numpy as jnp
from jax import lax
from jax.experimental import pallas as pl
from jax.experimental.pallas import tpu as pltpu

F32 = jnp.float32
BF16 = jnp.bfloat16
MESH_ID = pl.DeviceIdType.MESH
N_DEV = 8
VMEM_LIMIT_BYTES = 56 * 1024 * 1024

NORM_EPS = 1e-6
S5_WIDTH, S5_GROUP, S5_GROUPS, S5_STATE = 512, 16, 32, 64
HG_HEADS, HG_DIM, HG_CHUNK = 4, 128, 64
ATT_E, ATT_HPG, ATT_BLOCK = 64, 8, 128
ATT_DILATIONS = (1, 4, 16)
ROT_DIM, ROPE_THETA = 16, 500000.0
D_FF = 2816
ADAM_LR, ADAM_B1, ADAM_B2, ADAM_EPS, ADAM_WD, ADAM_STEP = 0.001, 0.9, 0.999, 1e-08, 0.01, 10
NEG_BIG = -1e30


def _params(**kw):
    return pltpu.CompilerParams(vmem_limit_bytes=VMEM_LIMIT_BYTES, **kw)


def _pick(n, cands):
    for c in cands:
        if n % c == 0:
            return c
    return n


def _dot(a, b):
    return jnp.dot(a.astype(BF16), b.astype(BF16), preferred_element_type=F32)


def _dot_nt(a, b):
    return lax.dot_general(a.astype(BF16), b.astype(BF16), (((1,), (1,)), ((), ())), preferred_element_type=F32)


def _dot_tn(a, b):
    return lax.dot_general(a.astype(BF16), b.astype(BF16), (((0,), (0,)), ((), ())), preferred_element_type=F32)


def _dot_f32(a, b):
    return jnp.dot(a, b, preferred_element_type=F32, precision=lax.Precision.HIGHEST)


def _dot_f32_nt(a, b):
    return lax.dot_general(a, b, (((1,), (1,)), ((), ())), preferred_element_type=F32, precision=lax.Precision.HIGHEST)


def _dot_f32_tn(a, b):
    return lax.dot_general(a, b, (((0,), (0,)), ((), ())), preferred_element_type=F32, precision=lax.Precision.HIGHEST)


def _sigmoid(x):
    return 1.0 / (1.0 + jnp.exp(-x))


def mm(a, b, *, ta=False, tb=False, res=None, out_dtype=F32, name):
    m, k = (a.shape[1], a.shape[0]) if ta else a.shape
    n = b.shape[0] if tb else b.shape[1]
    assert (b.shape[1] if tb else b.shape[0]) == k
    tm = _pick(m, (512, 256, 128))
    tn = _pick(n, (512, 256, 128))
    tk = _pick(k, (2048, 1408, 1024, 512, 256, 128))
    nk = k // tk
    has_res = res is not None

    def body(*refs):
        if has_res:
            a_ref, b_ref, r_ref, o_ref, acc_ref = refs
        else:
            a_ref, b_ref, o_ref, acc_ref = refs
        kk = pl.program_id(2)
        dn = (((0 if ta else 1,), (1 if tb else 0,)), ((), ()))
        part = lax.dot_general(a_ref[...].astype(BF16), b_ref[...].astype(BF16), dn, preferred_element_type=F32)

        @pl.when(kk == 0)
        def _():
            acc_ref[...] = part

        @pl.when(kk > 0)
        def _():
            acc_ref[...] += part

        @pl.when(kk == nk - 1)
        def _():
            out = acc_ref[...]
            if has_res:
                out = out + r_ref[...].astype(F32)
            o_ref[...] = out.astype(o_ref.dtype)

    a_spec = pl.BlockSpec((tk, tm), lambda i, j, q: (q, i)) if ta else pl.BlockSpec((tm, tk), lambda i, j, q: (i, q))
    b_spec = pl.BlockSpec((tn, tk), lambda i, j, q: (j, q)) if tb else pl.BlockSpec((tk, tn), lambda i, j, q: (q, j))
    o_spec = pl.BlockSpec((tm, tn), lambda i, j, q: (i, j))
    in_specs = [a_spec, b_spec] + ([o_spec] if has_res else [])
    args = (a, b) + ((res,) if has_res else ())
    return pl.pallas_call(
        body, grid=(m // tm, n // tn, nk), in_specs=in_specs, out_specs=o_spec,
        out_shape=jax.ShapeDtypeStruct((m, n), out_dtype),
        scratch_shapes=[pltpu.VMEM((tm, tn), F32)],
        compiler_params=_params(dimension_semantics=("parallel", "parallel", "arbitrary")), name=name,
    )(*args)


def rms_fwd(x, g, *, name):
    L, D = x.shape
    tr = _pick(L, (256, 128))

    def body(x_ref, g_ref, o_ref):
        xv = x_ref[...]
        r = lax.rsqrt(jnp.mean(xv * xv, axis=-1, keepdims=True) + NORM_EPS)
        o_ref[...] = (xv * r * g_ref[...]).astype(o_ref.dtype)

    row = pl.BlockSpec((tr, D), lambda i: (i, 0))
    vec = pl.BlockSpec((1, D), lambda i: (0, 0))
    return pl.pallas_call(body, grid=(L // tr,), in_specs=[row, vec], out_specs=row,
                          out_shape=jax.ShapeDtypeStruct((L, D), BF16), name=name)(x, g.reshape(1, D))


def rms_bwd(x, g, dy, dres, *, name):
    L, D = x.shape
    tr = _pick(L, (256, 128))

    def body(x_ref, g_ref, dy_ref, dres_ref, dx_ref, dg_ref):
        xv = x_ref[...]
        r = lax.rsqrt(jnp.mean(xv * xv, axis=-1, keepdims=True) + NORM_EPS)
        xh = xv * r
        dyv = dy_ref[...].astype(F32)

        @pl.when(pl.program_id(0) == 0)
        def _():
            dg_ref[...] = jnp.zeros_like(dg_ref)

        dg_ref[...] += jnp.sum(dyv * xh, axis=0, keepdims=True)
        dxh = dyv * g_ref[...]
        dx_ref[...] = dres_ref[...] + r * (dxh - xh * jnp.mean(dxh * xh, axis=-1, keepdims=True))

    row = pl.BlockSpec((tr, D), lambda i: (i, 0))
    vec = pl.BlockSpec((1, D), lambda i: (0, 0))
    return pl.pallas_call(body, grid=(L // tr,), in_specs=[row, vec, row, row], out_specs=[row, vec],
                          out_shape=[jax.ShapeDtypeStruct((L, D), F32), jax.ShapeDtypeStruct((1, D), F32)],
                          compiler_params=_params(dimension_semantics=("arbitrary",)), name=name)(
        x, g.reshape(1, D), dy, dres)


def final_loss(h, g, target, *, name):
    L, D = h.shape
    tr = _pick(L, (256, 128))

    def body(x_ref, g_ref, t_ref, loss_ref, dx_ref, dg_ref):
        xv = x_ref[...]
        gv = g_ref[...]
        r = lax.rsqrt(jnp.mean(xv * xv, axis=-1, keepdims=True) + NORM_EPS)
        xh = xv * r
        err = xh * gv - t_ref[...]

        @pl.when(pl.program_id(0) == 0)
        def _():
            dg_ref[...] = jnp.zeros_like(dg_ref)
            loss_ref[...] = jnp.zeros_like(loss_ref)

        loss_ref[...] += 0.5 * jnp.sum(jnp.mean(err * err, axis=-1, keepdims=True), axis=0, keepdims=True)
        dyv = err * (1.0 / D)
        dg_ref[...] += jnp.sum(dyv * xh, axis=0, keepdims=True)
        dxh = dyv * gv
        dx_ref[...] = r * (dxh - xh * jnp.mean(dxh * xh, axis=-1, keepdims=True))

    row = pl.BlockSpec((tr, D), lambda i: (i, 0))
    vec = pl.BlockSpec((1, D), lambda i: (0, 0))
    one = pl.BlockSpec((1, 1), lambda i: (0, 0))
    return pl.pallas_call(body, grid=(L // tr,), in_specs=[row, vec, row], out_specs=[one, row, vec],
                          out_shape=[jax.ShapeDtypeStruct((1, 1), F32), jax.ShapeDtypeStruct((L, D), F32),
                                     jax.ShapeDtypeStruct((1, D), F32)],
                          compiler_params=_params(dimension_semantics=("arbitrary",)), name=name)(
        h, g.reshape(1, D), target)


def s5_scan_fwd(a_re, a_im, bu_re, bu_im, *, name):
    L, P = bu_re.shape
    W = _pick(P, (512, 256, 128))

    def body(ar_ref, ai_ref, br_ref, bi_ref, xr_ref, xi_ref):
        ar, ai = ar_ref[...], ai_ref[...]

        def step(t8, carry):
            xr, xi = carry
            base = pl.multiple_of(t8 * 8, 8)
            br = br_ref[pl.ds(base, 8), :]
            bi = bi_ref[pl.ds(base, 8), :]
            out_r, out_i = [], []
            for j in range(8):
                nr = ar * xr - ai * xi + br[j:j + 1, :]
                ni = ar * xi + ai * xr + bi[j:j + 1, :]
                xr, xi = nr, ni
                out_r.append(nr)
                out_i.append(ni)
            xr_ref[pl.ds(base, 8), :] = jnp.concatenate(out_r, axis=0)
            xi_ref[pl.ds(base, 8), :] = jnp.concatenate(out_i, axis=0)
            return xr, xi

        zero = jnp.zeros((1, W), F32)
        lax.fori_loop(0, L // 8, step, (zero, zero))

    vec = pl.BlockSpec((1, W), lambda j: (0, j))
    col = pl.BlockSpec((L, W), lambda j: (0, j))
    return pl.pallas_call(body, grid=(P // W,), in_specs=[vec, vec, col, col], out_specs=[col, col],
                          out_shape=[jax.ShapeDtypeStruct((L, P), F32)] * 2,
                          compiler_params=_params(dimension_semantics=("parallel",)), name=name)(
        a_re, a_im, bu_re, bu_im)


def s5_scan_bwd(a_re, a_im, xs_re, xs_im, dx_re, dx_im, *, name):
    L, P = xs_re.shape
    W = _pick(P, (256, 128))

    def body(ar_ref, ai_ref, xr_ref, xi_ref, dr_ref, di_ref, lr_ref, li_ref, dar_ref, dai_ref):
        ar, ai = ar_ref[...], ai_ref[...]
        nblk = L // 8

        def step(s, carry):
            lr, li = carry
            base = pl.multiple_of((nblk - 1 - s) * 8, 8)
            dr = dr_ref[pl.ds(base, 8), :]
            di = di_ref[pl.ds(base, 8), :]
            out_r, out_i = [None] * 8, [None] * 8
            for j in range(7, -1, -1):
                nr = dr[j:j + 1, :] + ar * lr + ai * li
                ni = di[j:j + 1, :] - ai * lr + ar * li
                lr, li = nr, ni
                out_r[j], out_i[j] = nr, ni
            lr_ref[pl.ds(base, 8), :] = jnp.concatenate(out_r, axis=0)
            li_ref[pl.ds(base, 8), :] = jnp.concatenate(out_i, axis=0)
            return lr, li

        zero = jnp.zeros((1, W), F32)
        lax.fori_loop(0, nblk, step, (zero, zero))
        row = lax.broadcasted_iota(jnp.int32, (L, W), 0)
        xpr = jnp.where(row >= 1, pltpu.roll(xr_ref[...], 1, 0), 0.0)
        xpi = jnp.where(row >= 1, pltpu.roll(xi_ref[...], 1, 0), 0.0)
        lr, li = lr_ref[...], li_ref[...]
        dar_ref[...] = jnp.sum(lr * xpr + li * xpi, axis=0, keepdims=True)
        dai_ref[...] = jnp.sum(li * xpr - lr * xpi, axis=0, keepdims=True)

    vec = pl.BlockSpec((1, W), lambda j: (0, j))
    col = pl.BlockSpec((L, W), lambda j: (0, j))
    return pl.pallas_call(body, grid=(P // W,), in_specs=[vec, vec, col, col, col, col],
                          out_specs=[col, col, vec, vec],
                          out_shape=[jax.ShapeDtypeStruct((L, P), F32)] * 2 + [jax.ShapeDtypeStruct((1, P), F32)] * 2,
                          compiler_params=_params(dimension_semantics=("parallel",)), name=name)(
        a_re, a_im, xs_re, xs_im, dx_re, dx_im)


def _gelu(y):
    c = math.sqrt(2.0 / math.pi)
    t = jnp.tanh(c * (y + 0.044715 * y * y * y))
    return 0.5 * y * (1.0 + t), t


def s5_out_fwd(y0, proj, dvec, glu_w, glu_b, *, name):
    L, C = y0.shape
    tr = _pick(L, (256, 128))

    def body(y_ref, u_ref, d_ref, w_ref, b_ref, o_ref):
        z, _ = _gelu(y_ref[...] + d_ref[...] * u_ref[...])
        zg = _dot(z, w_ref[...]) + b_ref[...]
        o_ref[...] = (z * _sigmoid(zg)).astype(o_ref.dtype)

    row = pl.BlockSpec((tr, C), lambda i: (i, 0))
    vec = pl.BlockSpec((1, C), lambda i: (0, 0))
    wsp = pl.BlockSpec((C, C), lambda i: (0, 0))
    return pl.pallas_call(body, grid=(L // tr,), in_specs=[row, row, vec, wsp, vec], out_specs=row,
                          out_shape=jax.ShapeDtypeStruct((L, C), BF16), name=name)(
        y0, proj, dvec, glu_w, glu_b)


def s5_out_bwd(y0, proj, dvec, glu_w, glu_b, dcat, *, name):
    L, C = y0.shape
    tr = _pick(L, (256, 128))

    def body(y_ref, u_ref, d_ref, w_ref, b_ref, do_ref, dy_ref, dud_ref, z_ref, dzg_ref, db_ref, dd_ref):
        u = u_ref[...]
        y = y_ref[...] + d_ref[...] * u
        z, t = _gelu(y)
        zg = _dot(z, w_ref[...]) + b_ref[...]
        s = _sigmoid(zg)
        do = do_ref[...]
        dzg = do * z * s * (1.0 - s)
        dz = do * s + _dot_nt(dzg, w_ref[...])
        c = math.sqrt(2.0 / math.pi)
        dgelu = 0.5 * (1.0 + t) + 0.5 * y * (1.0 - t * t) * c * (1.0 + 3.0 * 0.044715 * y * y)
        dy = dz * dgelu

        @pl.when(pl.program_id(0) == 0)
        def _():
            db_ref[...] = jnp.zeros_like(db_ref)
            dd_ref[...] = jnp.zeros_like(dd_ref)

        db_ref[...] += jnp.sum(dzg, axis=0, keepdims=True)
        dd_ref[...] += jnp.sum(dy * u, axis=0, keepdims=True)
        dy_ref[...] = dy
        dud_ref[...] = dy * d_ref[...]
        z_ref[...] = z.astype(BF16)
        dzg_ref[...] = dzg.astype(BF16)

    row = pl.BlockSpec((tr, C), lambda i: (i, 0))
    vec = pl.BlockSpec((1, C), lambda i: (0, 0))
    wsp = pl.BlockSpec((C, C), lambda i: (0, 0))
    return pl.pallas_call(body, grid=(L // tr,), in_specs=[row, row, vec, wsp, vec, row],
                          out_specs=[row, row, row, row, vec, vec],
                          out_shape=[jax.ShapeDtypeStruct((L, C), F32), jax.ShapeDtypeStruct((L, C), F32),
                                     jax.ShapeDtypeStruct((L, C), BF16), jax.ShapeDtypeStruct((L, C), BF16),
                                     jax.ShapeDtypeStruct((1, C), F32), jax.ShapeDtypeStruct((1, C), F32)],
                          compiler_params=_params(dimension_semantics=("arbitrary",)), name=name)(
        y0, proj, dvec, glu_w, glu_b, dcat)


def _hg_gates(xq, xf, lb, tri):
    C = xq.shape[0]
    sq = _sigmoid(xq)
    q = xq * sq
    sg = _sigmoid(xf)
    f = lb + (1.0 - lb) * sg
    kk = 1.0 - f
    b = _dot_f32(tri, jnp.log(f))
    bm = b[C // 2 - 1:C // 2, :]
    bl = b[C - 1:C, :]
    eb = jnp.exp(b)
    return dict(sq=sq, q=q, sg=sg, f=f, kk=kk, b=b, bm=bm, bl=bl, eb=eb, ebl=jnp.exp(bl),
                qb=q * eb, eqm=jnp.exp(b - bm), ekm=jnp.exp(bm - b), ekl=jnp.exp(bl - b))


def _tri(C, lower):
    r = lax.broadcasted_iota(jnp.int32, (C, C), 0)
    c = lax.broadcasted_iota(jnp.int32, (C, C), 1)
    return (r >= c) if lower else (c >= r)


def hgrn_fwd(proj, lb, norm_g, *, name):
    L = proj.shape[0]
    C, H, K = HG_CHUNK, HG_HEADS, HG_DIM
    HK = H * K
    nc = L // C

    def body(q_ref, f_ref, i_ref, g_ref, lb_ref, ng_ref, o_ref, sall_ref, st_ref):
        @pl.when(pl.program_id(0) == 0)
        def _():
            st_ref[...] = jnp.zeros_like(st_ref)

        mask = _tri(C, True)
        tri = mask.astype(F32)
        for h in range(H):
            sl = slice(h * K, (h + 1) * K)
            v = i_ref[:, sl]
            st = st_ref[h]
            sall_ref[h] = st
            gt = _hg_gates(q_ref[:, sl], f_ref[:, sl], lb_ref[:, sl], tri)
            qt = gt["q"] * gt["eqm"]
            kt = gt["kk"] * gt["ekm"]
            kh = gt["kk"] * gt["ekl"]
            att = jnp.where(mask, _dot_nt(qt, kt), 0.0)
            o = _dot(att, v) + _dot_nt(gt["qb"], st)
            st_ref[h] = st * gt["ebl"] + _dot_tn(v, kh)
            r = lax.rsqrt(jnp.mean(o * o, axis=-1, keepdims=True) + NORM_EPS)
            xg = g_ref[:, sl]
            o_ref[:, sl] = (o * r * ng_ref[:, sl] * (xg * _sigmoid(xg))).astype(o_ref.dtype)

    def blk(cb):
        return pl.BlockSpec((C, HK), lambda i: (i, cb))

    vec = pl.BlockSpec((1, HK), lambda i: (0, 0))
    return pl.pallas_call(
        body, grid=(nc,), in_specs=[blk(1), blk(2), blk(3), blk(4), vec, vec],
        out_specs=[pl.BlockSpec((C, HK), lambda i: (i, 0)), pl.BlockSpec((None, H, K, K), lambda i: (i, 0, 0, 0))],
        out_shape=[jax.ShapeDtypeStruct((L, HK), BF16), jax.ShapeDtypeStruct((nc, H, K, K), F32)],
        scratch_shapes=[pltpu.VMEM((H, K, K), F32)],
        compiler_params=_params(dimension_semantics=("arbitrary",)), name=name,
    )(proj, proj, proj, proj, lb, norm_g)


def hgrn_bwd(proj, lb, norm_g, sall, dcat, *, name):
    L = proj.shape[0]
    C, H, K = HG_CHUNK, HG_HEADS, HG_DIM
    HK = H * K
    nc = L // C

    def body(q_ref, f_ref, i_ref, g_ref, lb_ref, ng_ref, sall_ref, do_ref, dx_ref, dlb_ref, dng_ref, dst_ref):
        @pl.when(pl.program_id(0) == 0)
        def _():
            dst_ref[...] = jnp.zeros_like(dst_ref)
            dlb_ref[...] = jnp.zeros_like(dlb_ref)
            dng_ref[...] = jnp.zeros_like(dng_ref)

        mask = _tri(C, True)
        tri = mask.astype(F32)
        tri_t = _tri(C, False).astype(F32)
        rowi = lax.broadcasted_iota(jnp.int32, (C, K), 0)
        for h in range(H):
            sl = slice(h * K, (h + 1) * K)
            xq, xf, v, xg = q_ref[:, sl], f_ref[:, sl], i_ref[:, sl], g_ref[:, sl]
            lb_h, ng = lb_ref[:, sl], ng_ref[:, sl]
            st = sall_ref[h]
            dst = dst_ref[h]
            gt = _hg_gates(xq, xf, lb_h, tri)
            q, kk, qb = gt["q"], gt["kk"], gt["qb"]
            qt = q * gt["eqm"]
            kt = kk * gt["ekm"]
            kh = kk * gt["ekl"]
            att = jnp.where(mask, _dot_f32_nt(qt, kt), 0.0)
            o = _dot_f32(att, v) + _dot_f32_nt(qb, st)
            r = lax.rsqrt(jnp.mean(o * o, axis=-1, keepdims=True) + NORM_EPS)
            oh = o * r
            sgg = _sigmoid(xg)
            silu_g = xg * sgg
            d_ob = do_ref[:, sl]
            d_on = d_ob * silu_g
            dxg = d_ob * (oh * ng) * (sgg * (1.0 + xg * (1.0 - sgg)))
            dng_ref[:, sl] += jnp.sum(d_on * oh, axis=0, keepdims=True)
            doh = d_on * ng
            do = r * (doh - oh * jnp.mean(doh * oh, axis=-1, keepdims=True))
            d_qb = _dot_f32(do, st)
            datt = jnp.where(mask, _dot_f32_nt(do, v), 0.0)
            dv = _dot_f32_tn(att, do) + _dot_f32_nt(kh, dst)
            d_qt = _dot_f32(datt, kt)
            d_kt = _dot_f32_tn(datt, qt)
            d_kh = _dot_f32(v, dst)
            d_bl = jnp.sum(dst * st, axis=0, keepdims=True) * gt["ebl"] + jnp.sum(d_kh * kh, axis=0, keepdims=True)
            dst_ref[h] = dst * gt["ebl"] + _dot_f32_tn(do, qb)
            dq = d_qt * gt["eqm"] + d_qb * gt["eb"]
            db = d_qt * qt + d_qb * qb - d_kt * kt - d_kh * kh
            db = db + jnp.where(rowi == C - 1, d_bl, 0.0)
            dkk = d_kt * gt["ekm"] + d_kh * gt["ekl"]
            dlg = _dot_f32(tri_t, db)
            df = dlg / gt["f"] - dkk
            sg = gt["sg"]
            dxf = df * (1.0 - lb_h) * sg * (1.0 - sg)
            dlb_ref[:, sl] += jnp.sum(df * (1.0 - sg), axis=0, keepdims=True)
            sq = gt["sq"]
            dxq = dq * (sq * (1.0 + xq * (1.0 - sq)))
            dx_ref[:, h * K:(h + 1) * K] = dxq.astype(dx_ref.dtype)
            dx_ref[:, HK + h * K:HK + (h + 1) * K] = dxf.astype(dx_ref.dtype)
            dx_ref[:, 2 * HK + h * K:2 * HK + (h + 1) * K] = dv.astype(dx_ref.dtype)
            dx_ref[:, 3 * HK + h * K:3 * HK + (h + 1) * K] = dxg.astype(dx_ref.dtype)

    def blk(cb):
        return pl.BlockSpec((C, HK), lambda i: (nc - 1 - i, cb))

    vec = pl.BlockSpec((1, HK), lambda i: (0, 0))
    return pl.pallas_call(
        body, grid=(nc,),
        in_specs=[blk(1), blk(2), blk(3), blk(4), vec, vec,
                  pl.BlockSpec((None, H, K, K), lambda i: (nc - 1 - i, 0, 0, 0)), blk(1)],
        out_specs=[pl.BlockSpec((C, 4 * HK), lambda i: (nc - 1 - i, 0)), vec, vec],
        out_shape=[jax.ShapeDtypeStruct((L, 4 * HK), BF16), jax.ShapeDtypeStruct((1, HK), F32),
                   jax.ShapeDtypeStruct((1, HK), F32)],
        scratch_shapes=[pltpu.VMEM((H, K, K), F32)],
        compiler_params=_params(dimension_semantics=("arbitrary",)), name=name,
    )(proj, proj, proj, proj, lb, norm_g, sall, dcat)


def _shift_down(x, k, row):
    return jnp.where(row >= k, pltpu.roll(x, k, 0), 0.0)


def _shift_up(x, k, row):
    n = x.shape[0]
    return jnp.where(row < n - k, pltpu.roll(x, n - k, 0), 0.0)


def convgate_fwd(hu, conv_w, conv_b, *, name):
    L, C2 = hu.shape
    C = C2 // 2
    tc = _pick(C, (256, 128))
    nb = C // tc

    def body(a_ref, b_ref, wa_ref, wb_ref, ba_ref, bb_ref, o_ref):
        row = lax.broadcasted_iota(jnp.int32, (L, tc), 0)

        def conv(x, w, bias):
            return w[2:3, :] * x + w[1:2, :] * _shift_down(x, 1, row) + w[0:1, :] * _shift_down(x, 2, row) + bias

        ca = conv(a_ref[...], wa_ref[...], ba_ref[...])
        cb = conv(b_ref[...], wb_ref[...], bb_ref[...])
        o_ref[...] = (ca * _sigmoid(ca) * cb).astype(o_ref.dtype)

    def col(off, rows):
        return pl.BlockSpec((rows, tc), lambda j: (0, j + off))

    return pl.pallas_call(
        body, grid=(nb,), in_specs=[col(0, L), col(nb, L), col(0, 3), col(nb, 3), col(0, 1), col(nb, 1)],
        out_specs=col(0, L), out_shape=jax.ShapeDtypeStruct((L, C), BF16),
        compiler_params=_params(dimension_semantics=("parallel",)), name=name,
    )(hu, hu, conv_w, conv_w, conv_b, conv_b)


def convgate_bwd(hu, conv_w, conv_b, dact, *, name):
    L, C2 = hu.shape
    C = C2 // 2
    tc = _pick(C, (256, 128))
    nb = C // tc

    def body(a_ref, b_ref, wa_ref, wb_ref, ba_ref, bb_ref, d_ref, dxa_ref, dxb_ref, dwa_ref, dwb_ref, dba_ref, dbb_ref):
        row = lax.broadcasted_iota(jnp.int32, (L, tc), 0)

        def conv(x, w, bias):
            x1 = _shift_down(x, 1, row)
            x2 = _shift_down(x, 2, row)
            return w[2:3, :] * x + w[1:2, :] * x1 + w[0:1, :] * x2 + bias, x1, x2

        xa, xb = a_ref[...], b_ref[...]
        wa, wb = wa_ref[...], wb_ref[...]
        ca, xa1, xa2 = conv(xa, wa, ba_ref[...])
        cb, xb1, xb2 = conv(xb, wb, bb_ref[...])
        d = d_ref[...]
        sa = _sigmoid(ca)
        dca = d * cb * (sa * (1.0 + ca * (1.0 - sa)))
        dcb = d * (ca * sa)

        def back(dc, w, x, x1, x2, dx_ref, dw_ref, db_ref):
            dx = w[2:3, :] * dc + w[1:2, :] * _shift_up(dc, 1, row) + w[0:1, :] * _shift_up(dc, 2, row)
            dx_ref[...] = dx.astype(dx_ref.dtype)
            dw_ref[...] = jnp.concatenate([jnp.sum(dc * x2, axis=0, keepdims=True),
                                           jnp.sum(dc * x1, axis=0, keepdims=True),
                                           jnp.sum(dc * x, axis=0, keepdims=True)], axis=0)
            db_ref[...] = jnp.sum(dc, axis=0, keepdims=True)

        back(dca, wa, xa, xa1, xa2, dxa_ref, dwa_ref, dba_ref)
        back(dcb, wb, xb, xb1, xb2, dxb_ref, dwb_ref, dbb_ref)

    def col(off, rows):
        return pl.BlockSpec((rows, tc), lambda j: (0, j + off))

    outs = pl.pallas_call(
        body, grid=(nb,),
        in_specs=[col(0, L), col(nb, L), col(0, 3), col(nb, 3), col(0, 1), col(nb, 1), col(0, L)],
        out_specs=[col(0, L), col(0, L), col(0, 3), col(0, 3), col(0, 1), col(0, 1)],
        out_shape=[jax.ShapeDtypeStruct((L, C), BF16)] * 2 + [jax.ShapeDtypeStruct((3, C), F32)] * 2
        + [jax.ShapeDtypeStruct((1, C), F32)] * 2,
        compiler_params=_params(dimension_semantics=("parallel",)), name=name,
    )(hu, hu, conv_w, conv_w, conv_b, conv_b, dact)
    dxa, dxb, dwa, dwb, dba, dbb = outs
    return (jnp.concatenate([dxa, dxb], axis=1), jnp.concatenate([dwa, dwb], axis=1),
            jnp.concatenate([dba, dbb], axis=1))


def _to_branch_order(t, d):
    L, W = t.shape
    return t if d == 1 else t.reshape(L // d, d, W).transpose(1, 0, 2).reshape(L, W)


def _to_token_order(t, d):
    L, W = t.shape
    return t if d == 1 else t.reshape(d, L // d, W).transpose(1, 0, 2).reshape(L, W)


def rope_tables(positions):
    half = ROT_DIM // 2
    inv_freq = ROPE_THETA ** (-jnp.arange(half, dtype=F32) * 2.0 / ROT_DIM)
    ang = positions.astype(F32)[:, None] * inv_freq
    cos, sin = jnp.cos(ang), jnp.sin(ang)
    L = positions.shape[0]
    one = jnp.ones((L, ATT_E - ROT_DIM), F32)
    zero = jnp.zeros((L, ATT_E - ROT_DIM), F32)
    zh = jnp.zeros((L, half), F32)
    tc = jnp.concatenate([cos, cos, one], axis=1)
    ts1 = jnp.concatenate([zh, sin, zero], axis=1)
    ts2 = jnp.concatenate([-sin, zh, zero], axis=1)
    return tuple(jnp.concatenate([t, t], axis=1) for t in (tc, ts1, ts2))


def rope_fwd(qkv, tabs, *, name):
    L = qkv.shape[0]
    W = 512
    tr = _pick(L, (256, 128))
    nq = 1536 // W
    scale = ATT_E ** -0.5

    def body(x_ref, c_ref, s1_ref, s2_ref, o_ref):
        j = pl.program_id(1)
        x = x_ref[...]
        c = jnp.concatenate([c_ref[...]] * 4, axis=1)
        s1 = jnp.concatenate([s1_ref[...]] * 4, axis=1)
        s2 = jnp.concatenate([s2_ref[...]] * 4, axis=1)
        rot = x * c + pltpu.roll(x, 8, 1) * s1 + pltpu.roll(x, W - 8, 1) * s2
        mult = jnp.where(j < nq, scale, 1.0)
        o_ref[...] = jnp.where(j < 2 * nq, rot * mult, x).astype(o_ref.dtype)

    blk = pl.BlockSpec((tr, W), lambda i, j: (i, j))
    tab = pl.BlockSpec((tr, 128), lambda i, j: (i, 0))
    return pl.pallas_call(body, grid=(L // tr, 3 * nq), in_specs=[blk, tab, tab, tab], out_specs=blk,
                          out_shape=jax.ShapeDtypeStruct((L, 3 * 1536), BF16),
                          compiler_params=_params(dimension_semantics=("parallel", "parallel")), name=name)(
        qkv, *tabs)


def rope_bwd(dq, dk, dv, tabs, *, name):
    L = dq.shape[0]
    W = 512
    tr = _pick(L, (256, 128))
    nq = 1536 // W
    scale = ATT_E ** -0.5

    def body(dq_ref, dk_ref, dv_ref, c_ref, s1_ref, s2_ref, o_ref):
        j = pl.program_id(1)
        c = jnp.concatenate([c_ref[...]] * 4, axis=1)
        s1 = jnp.concatenate([s1_ref[...]] * 4, axis=1)
        s2 = jnp.concatenate([s2_ref[...]] * 4, axis=1)

        def unrot(dy):
            return dy * c + pltpu.roll(dy * s1, W - 8, 1) + pltpu.roll(dy * s2, 8, 1)

        @pl.when(j < nq)
        def _():
            o_ref[...] = (unrot(dq_ref[...]) * scale).astype(o_ref.dtype)

        @pl.when((j >= nq) & (j < 2 * nq))
        def _():
            o_ref[...] = unrot(dk_ref[...]).astype(o_ref.dtype)

        @pl.when(j >= 2 * nq)
        def _():
            o_ref[...] = dv_ref[...].astype(o_ref.dtype)

    def src(k):
        return pl.BlockSpec((tr, W), lambda i, j: (i, jnp.clip(j - k * nq, 0, nq - 1)))

    tab = pl.BlockSpec((tr, 128), lambda i, j: (i, 0))
    return pl.pallas_call(body, grid=(L // tr, 3 * nq), in_specs=[src(0), src(1), src(2), tab, tab, tab],
                          out_specs=pl.BlockSpec((tr, W), lambda i, j: (i, j)),
                          out_shape=jax.ShapeDtypeStruct((L, 3 * 1536), BF16),
                          compiler_params=_params(dimension_semantics=("parallel", "arbitrary")), name=name)(
        dq, dk, dv, *tabs)


def _att_masks(has_prev):
    qi = lax.broadcasted_iota(jnp.int32, (ATT_BLOCK, ATT_BLOCK), 0)
    kj = lax.broadcasted_iota(jnp.int32, (ATT_BLOCK, ATT_BLOCK), 1)
    return qi >= kj, (kj >= qi) & has_prev


def attn_fwd(qp, kp, vp, d, *, name):
    L, W = qp.shape
    B, E = ATT_BLOCK, ATT_E
    nblk = L // B
    nb = nblk // d

    def body(q_ref, kc_ref, kp_ref, vc_ref, vp_ref, o_ref, l_ref):
        has_prev = (pl.program_id(0) % nb) > 0
        mc, mp = _att_masks(has_prev)
        for h in range(ATT_HPG):
            sl = slice(h * E, (h + 1) * E)
            q = q_ref[:, sl]
            sc = jnp.where(mc, _dot_nt(q, kc_ref[:, sl]), NEG_BIG)
            sp = jnp.where(mp, _dot_nt(q, kp_ref[:, sl]), NEG_BIG)
            m = jnp.maximum(jnp.max(sc, axis=-1, keepdims=True), jnp.max(sp, axis=-1, keepdims=True))
            pc = jnp.exp(sc - m)
            pp = jnp.exp(sp - m)
            den = jnp.sum(pc, axis=-1, keepdims=True) + jnp.sum(pp, axis=-1, keepdims=True)
            o = (_dot(pc, vc_ref[:, sl]) + _dot(pp, vp_ref[:, sl])) / den
            o_ref[:, sl] = o
            l_ref[:, sl] = jnp.broadcast_to(m + jnp.log(den), (B, E))

    cur = pl.BlockSpec((B, W), lambda j: (j, 0))
    prev = pl.BlockSpec((B, W), lambda j: (jnp.maximum(j - 1, 0), 0))
    return pl.pallas_call(body, grid=(nblk,), in_specs=[cur, cur, prev, cur, prev], out_specs=[cur, cur],
                          out_shape=[jax.ShapeDtypeStruct((L, W), F32)] * 2,
                          compiler_params=_params(dimension_semantics=("parallel",)), name=name)(
        qp, kp, kp, vp, vp)


def attn_bwd(qp, kp, vp, lse, do, dl, d, *, name):
    L, W = qp.shape
    B, E = ATT_BLOCK, ATT_E
    nblk = L // B
    nb = nblk // d

    def body(q_ref, kc_ref, kp_ref, vc_ref, vp_ref, l_ref, do_ref, dl_ref, dq_ref, dk_ref, dv_ref, tkc, tkp, tvc, tvp):
        j = pl.program_id(0)

        @pl.when(j == 0)
        def _():
            dk_ref[...] = jnp.zeros_like(dk_ref)
            dv_ref[...] = jnp.zeros_like(dv_ref)

        has_prev = (j % nb) > 0
        mc, mp = _att_masks(has_prev)
        for h in range(ATT_HPG):
            sl = slice(h * E, (h + 1) * E)
            q = q_ref[:, sl]
            kc, kpv, vc, vpv = kc_ref[:, sl], kp_ref[:, sl], vc_ref[:, sl], vp_ref[:, sl]
            lse_h = l_ref[:, h * E:h * E + 1]
            dl_h = dl_ref[:, h * E:h * E + 1]
            doh = do_ref[:, sl]
            pc = jnp.where(mc, jnp.exp(_dot_nt(q, kc) - lse_h), 0.0)
            pp = jnp.where(mp, jnp.exp(_dot_nt(q, kpv) - lse_h), 0.0)
            dsc = pc * (_dot_nt(doh, vc) - dl_h)
            dsp = pp * (_dot_nt(doh, vpv) - dl_h)
            dq_ref[:, sl] = _dot(dsc, kc) + _dot(dsp, kpv)
            tkc[:, sl] = _dot_tn(dsc, q)
            tkp[:, sl] = _dot_tn(dsp, q)
            tvc[:, sl] = _dot_tn(pc, doh)
            tvp[:, sl] = _dot_tn(pp, doh)
        cur = pl.multiple_of(j * B, B)
        prv = pl.multiple_of(jnp.maximum(j - 1, 0) * B, B)
        dk_ref[pl.ds(cur, B), :] += tkc[...]
        dv_ref[pl.ds(cur, B), :] += tvc[...]
        dk_ref[pl.ds(prv, B), :] += tkp[...]
        dv_ref[pl.ds(prv, B), :] += tvp[...]

    cur = pl.BlockSpec((B, W), lambda j: (j, 0))
    prev = pl.BlockSpec((B, W), lambda j: (jnp.maximum(j - 1, 0), 0))
    full = pl.BlockSpec((L, W), lambda j: (0, 0))
    return pl.pallas_call(body, grid=(nblk,), in_specs=[cur, cur, prev, cur, prev, cur, cur, cur],
                          out_specs=[cur, full, full], out_shape=[jax.ShapeDtypeStruct((L, W), F32)] * 3,
                          scratch_shapes=[pltpu.VMEM((B, W), F32)] * 4,
                          compiler_params=_params(dimension_semantics=("arbitrary",)), name=name)(
        qp, kp, kp, vp, vp, lse, do, dl)


def _merge_alpha(l_refs):
    ls = [r[...] for r in l_refs]
    m = jnp.maximum(jnp.maximum(ls[0], ls[1]), ls[2])
    es = [jnp.exp(l - m) for l in ls]
    den = es[0] + es[1] + es[2]
    return [e / den for e in es]


def merge_fwd(os_, ls_, *, name):
    L, W = os_[0].shape
    tr = _pick(L, (256, 128))

    def body(o0, o1, o2, l0, l1, l2, out_ref):
        al = _merge_alpha((l0, l1, l2))
        out_ref[...] = (al[0] * o0[...] + al[1] * o1[...] + al[2] * o2[...]).astype(out_ref.dtype)

    row = pl.BlockSpec((tr, W), lambda i: (i, 0))
    return pl.pallas_call(body, grid=(L // tr,), in_specs=[row] * 6, out_specs=row,
                          out_shape=jax.ShapeDtypeStruct((L, W), BF16), name=name)(*os_, *ls_)


def merge_bwd(os_, ls_, do, *, name):
    L, W = do.shape
    tr = _pick(L, (256, 128))

    def body(o0, o1, o2, l0, l1, l2, do_ref, d0, d1, d2, e0, e1, e2):
        al = _merge_alpha((l0, l1, l2))
        dov = do_ref[...]
        r = lax.broadcasted_iota(jnp.int32, (W, W), 0) // ATT_E
        c = lax.broadcasted_iota(jnp.int32, (W, W), 1) // ATT_E
        ones_blk = (r == c).astype(F32)
        t = jnp.zeros_like(dov)
        for a, o in zip(al, (o0, o1, o2)):
            t = t + a * _dot_f32(dov * o[...], ones_blk)
        for a, d_ref, e_ref in zip(al, (d0, d1, d2), (e0, e1, e2)):
            d_ref[...] = a * dov
            e_ref[...] = a * t

    row = pl.BlockSpec((tr, W), lambda i: (i, 0))
    return pl.pallas_call(body, grid=(L // tr,), in_specs=[row] * 7, out_specs=[row] * 6,
                          out_shape=[jax.ShapeDtypeStruct((L, W), F32)] * 6, name=name)(*os_, *ls_, do)


def _me_and_peers():
    x, y, c = lax.axis_index("x"), lax.axis_index("y"), lax.axis_index("c")
    peers = []
    for k in range(1, N_DEV):
        px = 1 - x if k & 4 else x
        py = 1 - y if k & 2 else y
        pc = 1 - c if k & 1 else c
        peers.append((px, py, pc))
    return (x, y, c), peers


def _index(dev):
    return 4 * dev[0] + 2 * dev[1] + dev[2]


def all_gather(shard, *, name):
    def body(x_ref, out_ref, send_sems, recv_sems, local_sem):
        me, peers = _me_and_peers()
        mine = pltpu.make_async_copy(x_ref, out_ref.at[_index(me)], local_sem)
        mine.start()
        sends = []
        for k, peer in enumerate(peers):
            cp = pltpu.make_async_remote_copy(src_ref=x_ref, dst_ref=out_ref.at[_index(me)],
                                              send_sem=send_sems.at[k], recv_sem=recv_sems.at[k],
                                              device_id=peer, device_id_type=MESH_ID)
            cp.start()
            sends.append(cp)
        for k, peer in enumerate(peers):
            pltpu.make_async_remote_copy(src_ref=x_ref, dst_ref=out_ref.at[_index(peer)],
                                         send_sem=send_sems.at[k], recv_sem=recv_sems.at[k],
                                         device_id=peer, device_id_type=MESH_ID).wait_recv()
        for cp in sends:
            cp.wait_send()
        mine.wait()

    any_spec = pl.BlockSpec(memory_space=pl.ANY)
    return pl.pallas_call(body, in_specs=[any_spec], out_specs=any_spec,
                          out_shape=jax.ShapeDtypeStruct((N_DEV,) + shard.shape, shard.dtype),
                          scratch_shapes=[pltpu.SemaphoreType.DMA((N_DEV - 1,)), pltpu.SemaphoreType.DMA((N_DEV - 1,)),
                                          pltpu.SemaphoreType.DMA],
                          name=name)(shard)


def exchange(parts, *, name):
    def body(x_ref, out_ref, send_sems, recv_sems, local_sem):
        me, peers = _me_and_peers()
        mine = pltpu.make_async_copy(x_ref.at[_index(me)], out_ref.at[_index(me)], local_sem)
        mine.start()
        sends = []
        for k, peer in enumerate(peers):
            cp = pltpu.make_async_remote_copy(src_ref=x_ref.at[_index(peer)], dst_ref=out_ref.at[_index(me)],
                                              send_sem=send_sems.at[k], recv_sem=recv_sems.at[k],
                                              device_id=peer, device_id_type=MESH_ID)
            cp.start()
            sends.append(cp)
        for k, peer in enumerate(peers):
            pltpu.make_async_remote_copy(src_ref=x_ref.at[_index(peer)], dst_ref=out_ref.at[_index(peer)],
                                         send_sem=send_sems.at[k], recv_sem=recv_sems.at[k],
                                         device_id=peer, device_id_type=MESH_ID).wait_recv()
        for cp in sends:
            cp.wait_send()
        mine.wait()

    any_spec = pl.BlockSpec(memory_space=pl.ANY)
    return pl.pallas_call(body, in_specs=[any_spec], out_specs=any_spec,
                          out_shape=jax.ShapeDtypeStruct(parts.shape, parts.dtype),
                          scratch_shapes=[pltpu.SemaphoreType.DMA((N_DEV - 1,)), pltpu.SemaphoreType.DMA((N_DEV - 1,)),
                                          pltpu.SemaphoreType.DMA],
                          name=name)(parts)


def cast_bf16(x, *, name):
    R, C = x.shape
    tr = _pick(R, (512, 256, 128, 64))

    def body(x_ref, o_ref):
        o_ref[...] = x_ref[...].astype(BF16)

    row = pl.BlockSpec((tr, C), lambda i: (i, 0))
    return pl.pallas_call(body, grid=(R // tr,), in_specs=[row], out_specs=row,
                          out_shape=jax.ShapeDtypeStruct((R, C), BF16), name=name)(x)


def cols_from_shards(g, *, name):
    _, K, n = g.shape
    tk = _pick(K, (256, 128))

    def body(g_ref, o_ref):
        for i in range(N_DEV):
            o_ref[:, i * n:(i + 1) * n] = g_ref[i]

    return pl.pallas_call(body, grid=(K // tk,), in_specs=[pl.BlockSpec((N_DEV, tk, n), lambda i: (0, i, 0))],
                          out_specs=pl.BlockSpec((tk, N_DEV * n), lambda i: (i, 0)),
                          out_shape=jax.ShapeDtypeStruct((K, N_DEV * n), g.dtype), name=name)(g)


def shards_from_cols(w, *, name):
    K, N = w.shape
    n = N // N_DEV
    tk = _pick(K, (256, 128))

    def body(w_ref, o_ref):
        for i in range(N_DEV):
            o_ref[i] = w_ref[:, i * n:(i + 1) * n].astype(o_ref.dtype)

    return pl.pallas_call(body, grid=(K // tk,), in_specs=[pl.BlockSpec((tk, N), lambda i: (i, 0))],
                          out_specs=pl.BlockSpec((N_DEV, tk, n), lambda i: (0, i, 0)),
                          out_shape=jax.ShapeDtypeStruct((N_DEV, K, n), BF16), name=name)(w)


def _adamw(w, g, m, v):
    m = ADAM_B1 * m + (1.0 - ADAM_B1) * g
    v = ADAM_B2 * v + (1.0 - ADAM_B2) * (g * g)
    m_hat = m / (1.0 - ADAM_B1 ** ADAM_STEP)
    v_hat = v / (1.0 - ADAM_B2 ** ADAM_STEP)
    delta = -ADAM_LR * (m_hat / (jnp.sqrt(v_hat) + ADAM_EPS) + ADAM_WD * w)
    return delta, m, v


def reduce_adamw(parts, w, m, v, *, name):
    _, R, C = parts.shape
    tr = _pick(R, (256, 128, 64, 32, 16, 8))

    def body(p_ref, w_ref, m_ref, v_ref, g_ref, d_ref, nm_ref, nv_ref):
        g = p_ref[0].astype(F32)
        for i in range(1, N_DEV):
            g = g + p_ref[i].astype(F32)
        delta, nm, nv = _adamw(w_ref[...], g, m_ref[...], v_ref[...])
        g_ref[...] = g
        d_ref[...] = delta
        nm_ref[...] = nm
        nv_ref[...] = nv

    row = pl.BlockSpec((tr, C), lambda i: (i, 0))
    return pl.pallas_call(body, grid=(R // tr,),
                          in_specs=[pl.BlockSpec((N_DEV, tr, C), lambda i: (0, i, 0)), row, row, row],
                          out_specs=[row] * 4, out_shape=[jax.ShapeDtypeStruct((R, C), F32)] * 4,
                          compiler_params=_params(dimension_semantics=("parallel",)), name=name)(parts, w, m, v)


def _s5_prepare(A_re, A_im, log_dt, B_re, B_im, C_re, C_im):
    G, P, Cg = S5_GROUPS, S5_STATE, S5_GROUP
    dt = jnp.exp(log_dt)[:, None]
    mag = jnp.exp(A_re * dt)
    ab_re = mag * jnp.cos(A_im * dt)
    ab_im = mag * jnp.sin(A_im * dt)
    den = A_re * A_re + A_im * A_im
    nr, ni = ab_re - 1.0, ab_im
    c_re = (nr * A_re + ni * A_im) / den
    c_im = (ni * A_re - nr * A_im) / den
    Bb_re = c_re[..., None] * B_re - c_im[..., None] * B_im
    Bb_im = c_re[..., None] * B_im + c_im[..., None] * B_re
    eye = jnp.eye(G, dtype=F32)

    def dense_in(b):
        return jnp.einsum('gpc,gh->gchp', b, eye).reshape(G * Cg, G * P)

    def dense_out(c):
        return jnp.einsum('gcp,gh->gphc', c, eye).reshape(G * P, G * Cg)

    return (ab_re.reshape(1, G * P), ab_im.reshape(1, G * P), dense_in(Bb_re), dense_in(Bb_im),
            dense_out(C_re), dense_out(-C_im))


def _lower_bound(gamma):
    return jnp.cumsum(jax.nn.softmax(gamma, axis=0), axis=0)[0:1]


def _gather_cols(w2d, *, name):
    return cols_from_shards(all_gather(cast_bf16(w2d, name=name + "_cast"), name=name + "_ag"), name=name + "_cols")


def _gather_rows(w2d, *, name):
    g = all_gather(cast_bf16(w2d, name=name + "_cast"), name=name + "_ag")
    return g.reshape(N_DEV * w2d.shape[0], w2d.shape[1])


def _ffn_fwd(h, g_norm, w_in, conv_w, conv_b, w_out, tag):
    hn = rms_fwd(h, g_norm, name=tag + "_rms")
    hu = mm(hn, w_in, name=tag + "_in")
    act = convgate_fwd(hu, conv_w, conv_b, name=tag + "_gate")
    h_out = mm(act, w_out, res=h, name=tag + "_out")
    return h_out, (hn, hu, act)


def _ffn_bwd(h, g_norm, w_in, conv_w, conv_b, w_out, saved, dh, tag):
    hn, hu, act = saved
    dact = mm(dh, w_out, tb=True, name=tag + "_dact")
    dw_out = mm(act, dh, ta=True, name=tag + "_dwout")
    dhu, dconv_w, dconv_b = convgate_bwd(hu, conv_w, conv_b, dact, name=tag + "_dgate")
    dhn = mm(dhu, w_in, tb=True, name=tag + "_dhn")
    dw_in = mm(hn, dhu, ta=True, name=tag + "_dwin")
    dh_in, dg = rms_bwd(h, g_norm, dhn, dh, name=tag + "_drms")
    return dh_in, dg, dw_in, dconv_w, dconv_b, dw_out


def kernel(x, positions, norm_mix, norm_ffn, norm_final, mix_w_in, mix_w_out, s5_A_re, s5_A_im, s5_log_dt, s5_B_re, s5_B_im, s5_C_re, s5_C_im, s5_D, s5_glu_w, s5_glu_b, hgrn_gamma, hgrn_norm, att_w_qkv, att_w_o, ffn_w_in, ffn_conv_w, ffn_conv_b, ffn_w_out, loss_target, m_norm_mix, m_norm_ffn, m_norm_final, m_mix_w_in, m_mix_w_out, m_s5_A_re, m_s5_A_im, m_s5_log_dt, m_s5_B_re, m_s5_B_im, m_s5_C_re, m_s5_C_im, m_s5_D, m_s5_glu_w, m_s5_glu_b, m_hgrn_gamma, m_hgrn_norm, m_att_w_qkv, m_att_w_o, m_ffn_w_in, m_ffn_conv_w, m_ffn_conv_b, m_ffn_w_out, v_norm_mix, v_norm_ffn, v_norm_final, v_mix_w_in, v_mix_w_out, v_s5_A_re, v_s5_A_im, v_s5_log_dt, v_s5_B_re, v_s5_B_im, v_s5_C_re, v_s5_C_im, v_s5_D, v_s5_glu_w, v_s5_glu_b, v_hgrn_gamma, v_hgrn_norm, v_att_w_qkv, v_att_w_o, v_ffn_w_in, v_ffn_conv_w, v_ffn_conv_b, v_ffn_w_out):
    W = dict(norm_mix=norm_mix, norm_ffn=norm_ffn, norm_final=norm_final, mix_w_in=mix_w_in, mix_w_out=mix_w_out,
             s5_A_re=s5_A_re, s5_A_im=s5_A_im, s5_log_dt=s5_log_dt, s5_B_re=s5_B_re, s5_B_im=s5_B_im,
             s5_C_re=s5_C_re, s5_C_im=s5_C_im, s5_D=s5_D, s5_glu_w=s5_glu_w, s5_glu_b=s5_glu_b,
             hgrn_gamma=hgrn_gamma, hgrn_norm=hgrn_norm, att_w_qkv=att_w_qkv, att_w_o=att_w_o, ffn_w_in=ffn_w_in,
             ffn_conv_w=ffn_conv_w, ffn_conv_b=ffn_conv_b, ffn_w_out=ffn_w_out)
    M = dict(norm_mix=m_norm_mix, norm_ffn=m_norm_ffn, norm_final=m_norm_final, mix_w_in=m_mix_w_in,
             mix_w_out=m_mix_w_out, s5_A_re=m_s5_A_re, s5_A_im=m_s5_A_im, s5_log_dt=m_s5_log_dt, s5_B_re=m_s5_B_re,
             s5_B_im=m_s5_B_im, s5_C_re=m_s5_C_re, s5_C_im=m_s5_C_im, s5_D=m_s5_D, s5_glu_w=m_s5_glu_w,
             s5_glu_b=m_s5_glu_b, hgrn_gamma=m_hgrn_gamma, hgrn_norm=m_hgrn_norm, att_w_qkv=m_att_w_qkv,
             att_w_o=m_att_w_o, ffn_w_in=m_ffn_w_in, ffn_conv_w=m_ffn_conv_w, ffn_conv_b=m_ffn_conv_b,
             ffn_w_out=m_ffn_w_out)
    V = dict(norm_mix=v_norm_mix, norm_ffn=v_norm_ffn, norm_final=v_norm_final, mix_w_in=v_mix_w_in,
             mix_w_out=v_mix_w_out, s5_A_re=v_s5_A_re, s5_A_im=v_s5_A_im, s5_log_dt=v_s5_log_dt, s5_B_re=v_s5_B_re,
             s5_B_im=v_s5_B_im, s5_C_re=v_s5_C_re, s5_C_im=v_s5_C_im, s5_D=v_s5_D, s5_glu_w=v_s5_glu_w,
             s5_glu_b=v_s5_glu_b, hgrn_gamma=v_hgrn_gamma, hgrn_norm=v_hgrn_norm, att_w_qkv=v_att_w_qkv,
             att_w_o=v_att_w_o, ffn_w_in=v_ffn_w_in, ffn_conv_w=v_ffn_conv_w, ffn_conv_b=v_ffn_conv_b,
             ffn_w_out=v_ffn_w_out)
    return _step(x[0], positions[0], loss_target[0], W, M, V)


BIG = ("mix_w_in", "mix_w_out", "s5_glu_w", "att_w_qkv", "att_w_o", "ffn_w_in", "ffn_w_out")
COL_SHARDED = ("mix_w_in", "att_w_qkv", "att_w_o", "ffn_w_in")
SMALL = ("norm_mix", "norm_ffn", "norm_final", "s5_A_re", "s5_A_im", "s5_log_dt", "s5_B_re", "s5_B_im", "s5_C_re",
         "s5_C_im", "s5_D", "s5_glu_b", "hgrn_gamma", "hgrn_norm", "ffn_conv_b")
ORDER = ("norm_mix", "norm_ffn", "norm_final", "mix_w_in", "mix_w_out", "s5_A_re", "s5_A_im", "s5_log_dt", "s5_B_re",
         "s5_B_im", "s5_C_re", "s5_C_im", "s5_D", "s5_glu_w", "s5_glu_b", "hgrn_gamma", "hgrn_norm", "att_w_qkv",
         "att_w_o", "ffn_w_in", "ffn_conv_w", "ffn_conv_b", "ffn_w_out")
PACK_COLS = 1024


def _step(x, positions, target, W, M, V):
    L, D = x.shape
    w_mix_in = _gather_cols(W["mix_w_in"][0], name="w_mix_in")
    w_mix_out = _gather_rows(W["mix_w_out"][0], name="w_mix_out")
    w_glu = _gather_rows(W["s5_glu_w"][0], name="w_glu")
    w_qkv = _gather_cols(W["att_w_qkv"][0], name="w_qkv")
    w_o = _gather_cols(W["att_w_o"][0], name="w_o")
    w_ffn_in = [_gather_cols(W["ffn_w_in"][l], name=f"w_ffn_in{l}") for l in range(2)]
    w_ffn_out = [_gather_rows(W["ffn_w_out"][l], name=f"w_ffn_out{l}") for l in range(2)]
    n_cw = W["ffn_conv_w"].shape[-1]
    cw_all = all_gather(W["ffn_conv_w"].reshape(6, n_cw), name="conv_w_ag")
    conv_w = cw_all.transpose(1, 0, 2).reshape(2, 3, N_DEV * n_cw)
    conv_b = W["ffn_conv_b"].reshape(2, 1, -1)

    s5_params = (W["s5_A_re"][0], W["s5_A_im"][0], W["s5_log_dt"][0], W["s5_B_re"][0], W["s5_B_im"][0],
                 W["s5_C_re"][0], W["s5_C_im"][0])
    (a_re, a_im, wb_re, wb_im, wc_re, wc_im), s5_prep_vjp = jax.vjp(_s5_prepare, *s5_params)
    dvec = W["s5_D"].reshape(1, S5_WIDTH)
    glu_b = W["s5_glu_b"].reshape(1, S5_WIDTH)
    lb, lb_vjp = jax.vjp(_lower_bound, W["hgrn_gamma"])
    hg_norm = W["hgrn_norm"].reshape(1, -1)
    tabs = rope_tables(positions)

    hn0 = rms_fwd(x, W["norm_mix"][0], name="l0_rms")
    proj = mm(hn0, w_mix_in, name="l0_proj")
    u_bf = cast_bf16(proj[:, :S5_WIDTH], name="l0_u_cast")
    bu_re = mm(u_bf, wb_re, name="s5_bu_re")
    bu_im = mm(u_bf, wb_im, name="s5_bu_im")
    xs_re, xs_im = s5_scan_fwd(a_re, a_im, bu_re, bu_im, name="s5_scan")
    y0 = mm(xs_im, wc_im, res=mm(xs_re, wc_re, name="s5_y_re"), name="s5_y_im")
    oa = s5_out_fwd(y0, proj, dvec, w_glu, glu_b, name="s5_out")
    ob, hg_states = hgrn_fwd(proj, lb, hg_norm, name="hgrn_fwd")
    cat = jnp.concatenate([oa, ob], axis=1)
    h1 = mm(cat, w_mix_out, res=x, name="l0_mix_out")
    h2, ffn0_saved = _ffn_fwd(h1, W["norm_ffn"][0], w_ffn_in[0], conv_w[0], conv_b[0], w_ffn_out[0], "ffn0")

    hn2 = rms_fwd(h2, W["norm_mix"][1], name="l1_rms")
    qkv = mm(hn2, w_qkv, name="l1_qkv")
    qkv_r = rope_fwd(qkv, tabs, name="rope_fwd")
    att_in, att_o, att_l = [], [], []
    for g, d in enumerate(ATT_DILATIONS):
        qp = _to_branch_order(qkv_r[:, 512 * g:512 * (g + 1)], d)
        kp = _to_branch_order(qkv_r[:, 1536 + 512 * g:1536 + 512 * (g + 1)], d)
        vp = _to_branch_order(qkv_r[:, 3072 + 512 * g:3072 + 512 * (g + 1)], d)
        o_p, l_p = attn_fwd(qp, kp, vp, d, name=f"attn_fwd{g}")
        att_in.append((qp, kp, vp, l_p))
        att_o.append(_to_token_order(o_p, d))
        att_l.append(_to_token_order(l_p, d))
    o_att = merge_fwd(att_o, att_l, name="merge_fwd")
    h3 = mm(o_att, w_o, res=h2, name="l1_mix_out")
    h4, ffn1_saved = _ffn_fwd(h3, W["norm_ffn"][1], w_ffn_in[1], conv_w[1], conv_b[1], w_ffn_out[1], "ffn1")

    loss, dh4, dg_final = final_loss(h4, W["norm_final"], target, name="final_loss")
    grads = {}
    dh3, dg_ffn1, dw_in1, dcw1, dcb1, dw_out1 = _ffn_bwd(h3, W["norm_ffn"][1], w_ffn_in[1], conv_w[1], conv_b[1],
                                                         w_ffn_out[1], ffn1_saved, dh4, "ffn1")
    d_oatt = mm(dh3, w_o, tb=True, name="l1_dmix")
    grads["att_w_o"] = mm(o_att, dh3, ta=True, name="l1_dwo")
    mb = merge_bwd(att_o, att_l, d_oatt, name="merge_bwd")
    dq_t, dk_t, dv_t = [], [], []
    for g, d in enumerate(ATT_DILATIONS):
        qp, kp, vp, l_p = att_in[g]
        dq_p, dk_p, dv_p = attn_bwd(qp, kp, vp, l_p, _to_branch_order(mb[g], d), _to_branch_order(mb[3 + g], d), d,
                                    name=f"attn_bwd{g}")
        dq_t.append(_to_token_order(dq_p, d))
        dk_t.append(_to_token_order(dk_p, d))
        dv_t.append(_to_token_order(dv_p, d))
    d_qkv = rope_bwd(jnp.concatenate(dq_t, axis=1), jnp.concatenate(dk_t, axis=1), jnp.concatenate(dv_t, axis=1),
                     tabs, name="rope_bwd")
    d_hn2 = mm(d_qkv, w_qkv, tb=True, name="l1_dhn")
    grads["att_w_qkv"] = mm(hn2, d_qkv, ta=True, name="l1_dwqkv")
    dh2, dg_mix1 = rms_bwd(h2, W["norm_mix"][1], d_hn2, dh3, name="l1_drms")

    dh1, dg_ffn0, dw_in0, dcw0, dcb0, dw_out0 = _ffn_bwd(h1, W["norm_ffn"][0], w_ffn_in[0], conv_w[0], conv_b[0],
                                                         w_ffn_out[0], ffn0_saved, dh2, "ffn0")
    dcat = mm(dh1, w_mix_out, tb=True, name="l0_dcat")
    grads["mix_w_out"] = mm(cat, dh1, ta=True, name="l0_dwout")
    d_hg, dlb, dhg_norm = hgrn_bwd(proj, lb, hg_norm, hg_states, dcat, name="hgrn_bwd")
    dy, du_d, z_bf, dzg, dglu_b, dD = s5_out_bwd(y0, proj, dvec, w_glu, glu_b, dcat, name="s5_dout")
    grads["s5_glu_w"] = mm(z_bf, dzg, ta=True, name="s5_dglu")
    dxs_re = mm(dy, wc_re, tb=True, name="s5_dxs_re")
    dxs_im = mm(dy, wc_im, tb=True, name="s5_dxs_im")
    dwc_re = mm(xs_re, dy, ta=True, name="s5_dwc_re")
    dwc_im = mm(xs_im, dy, ta=True, name="s5_dwc_im")
    dbu_re, dbu_im, da_re, da_im = s5_scan_bwd(a_re, a_im, xs_re, xs_im, dxs_re, dxs_im, name="s5_dscan")
    du = mm(dbu_im, wb_im, tb=True, res=mm(dbu_re, wb_re, tb=True, res=du_d, name="s5_du_re"), out_dtype=BF16,
            name="s5_du_im")
    dwb_re = mm(u_bf, dbu_re, ta=True, name="s5_dwb_re")
    dwb_im = mm(u_bf, dbu_im, ta=True, name="s5_dwb_im")
    s5_small = s5_prep_vjp((da_re, da_im, dwb_re, dwb_im, dwc_re, dwc_im))
    d_proj = jnp.concatenate([du, d_hg], axis=1)
    d_hn0 = mm(d_proj, w_mix_in, tb=True, name="l0_dhn")
    grads["mix_w_in"] = mm(hn0, d_proj, ta=True, name="l0_dwin")
    grad_x, dg_mix0 = rms_bwd(x, W["norm_mix"][0], d_hn0, dh1, name="l0_drms")
    (d_gamma,) = lb_vjp(dlb)

    out = {}

    def finish(name, parts, shard_shape):
        recv = exchange(parts, name=name + "_xchg")
        R = math.prod(shard_shape[:-1])
        Cn = shard_shape[-1]
        res = reduce_adamw(recv.reshape(N_DEV, R, Cn), W[name].reshape(R, Cn), M[name].reshape(R, Cn),
                           V[name].reshape(R, Cn), name=name + "_adamw")
        out[name] = tuple(r.reshape(shard_shape) for r in res)

    def col_parts(g, tag):
        return shards_from_cols(g, name=tag + "_split")

    def row_parts(g, tag):
        return cast_bf16(g, name=tag + "_gcast").reshape(N_DEV, g.shape[0] // N_DEV, g.shape[1])

    finish("mix_w_in", col_parts(grads["mix_w_in"], "mix_w_in"), W["mix_w_in"].shape)
    finish("mix_w_out", row_parts(grads["mix_w_out"], "mix_w_out"), W["mix_w_out"].shape)
    finish("s5_glu_w", row_parts(grads["s5_glu_w"], "s5_glu_w"), W["s5_glu_w"].shape)
    finish("att_w_qkv", col_parts(grads["att_w_qkv"], "att_w_qkv"), W["att_w_qkv"].shape)
    finish("att_w_o", col_parts(grads["att_w_o"], "att_w_o"), W["att_w_o"].shape)
    p_in = jnp.stack([col_parts(dw_in0, "ffn_w_in0"), col_parts(dw_in1, "ffn_w_in1")], axis=1)
    finish("ffn_w_in", p_in, W["ffn_w_in"].shape)
    p_out = jnp.stack([row_parts(dw_out0, "ffn_w_out0"), row_parts(dw_out1, "ffn_w_out1")], axis=1)
    finish("ffn_w_out", p_out, W["ffn_w_out"].shape)

    dA_re, dA_im, dlog_dt, dB_re, dB_im, dC_re, dC_im = s5_small
    small_g = dict(norm_mix=jnp.concatenate([dg_mix0, dg_mix1], axis=0), norm_ffn=jnp.concatenate([dg_ffn0, dg_ffn1], axis=0),
                   norm_final=dg_final, s5_A_re=dA_re, s5_A_im=dA_im, s5_log_dt=dlog_dt, s5_B_re=dB_re, s5_B_im=dB_im,
                   s5_C_re=dC_re, s5_C_im=dC_im, s5_D=dD, s5_glu_b=dglu_b, hgrn_gamma=d_gamma, hgrn_norm=dhg_norm,
                   ffn_conv_b=jnp.concatenate([dcb0, dcb1], axis=0))
    conv_w_g = jnp.stack([dcw0, dcw1], axis=0)
    sizes = [math.prod(W[n].shape) for n in SMALL]
    n_conv = conv_w_g.size
    total = sum(sizes) + n_conv + 1
    rows = -(-total // PACK_COLS)
    rows = -(-rows // 8) * 8
    pad = rows * PACK_COLS - total

    def pack(vals, conv_part, last):
        flat = [v.reshape(-1).astype(F32) for v in vals] + [conv_part.reshape(-1), last.reshape(-1),
                                                            jnp.zeros((pad,), F32)]
        return jnp.concatenate(flat).reshape(rows, PACK_COLS)

    me = 4 * lax.axis_index("x") + 2 * lax.axis_index("y") + lax.axis_index("c")

    def conv_full(shard):
        full = jnp.zeros((2, 3, N_DEV * n_cw), F32)
        return lax.dynamic_update_slice(full, shard, (0, 0, me * n_cw))

    zero1 = jnp.zeros((1,), F32)
    g_pack = pack([small_g[n] for n in SMALL], conv_w_g, loss)
    w_pack = pack([W[n] for n in SMALL], conv_full(W["ffn_conv_w"]), zero1)
    m_pack = pack([M[n] for n in SMALL], conv_full(M["ffn_conv_w"]), zero1)
    v_pack = pack([V[n] for n in SMALL], conv_full(V["ffn_conv_w"]), zero1 + 1.0)
    parts = all_gather(g_pack, name="small_ag")
    res = reduce_adamw(parts, w_pack, m_pack, v_pack, name="small_adamw")
    flat = [r.reshape(-1) for r in res]
    off = 0
    for n, sz in zip(SMALL, sizes):
        out[n] = tuple(f[off:off + sz].reshape(W[n].shape) for f in flat)
        off += sz
    conv_res = [f[off:off + n_conv].reshape(2, 3, N_DEV * n_cw) for f in flat]
    out["ffn_conv_w"] = tuple(lax.dynamic_slice(c, (0, 0, me * n_cw), (2, 3, n_cw)) for c in conv_res)
    off += n_conv
    loss_total = flat[0][off]

    result = [loss_total, grad_x[None]]
    for k in range(4):
        result += [out[n][k] for n in ORDER]
    return tuple(result)
```

```python
import functools
import math

import jax
import jax.numpy as jnp
from jax import lax
from jax.experimental import pallas as pl
from jax.experimental.pallas import tpu as pltpu

F32 = jnp.float32
BF16 = jnp.bfloat16
MESH_ID = pl.DeviceIdType.MESH
N_DEV = 8
VMEM_LIMIT_BYTES = 56 * 1024 * 1024

NORM_EPS = 1e-6
S5_WIDTH, S5_GROUP, S5_GROUPS, S5_STATE = 512, 16, 32, 64
HG_HEADS, HG_DIM, HG_CHUNK = 4, 128, 64
ATT_E, ATT_HPG, ATT_BLOCK = 64, 8, 128
ATT_DILATIONS = (1, 4, 16)
ROT_DIM, ROPE_THETA = 16, 500000.0
D_FF = 2816
ADAM_LR, ADAM_B1, ADAM_B2, ADAM_EPS, ADAM_WD, ADAM_STEP = 0.001, 0.9, 0.999, 1e-08, 0.01, 10
NEG_BIG = -1e30


def _params(**kw):
    return pltpu.CompilerParams(vmem_limit_bytes=VMEM_LIMIT_BYTES, **kw)


def _pick(n, cands):
    for c in cands:
        if n % c == 0:
            return c
    return n


def _dot(a, b):
    return jnp.dot(a.astype(BF16), b.astype(BF16), preferred_element_type=F32)


def _dot_nt(a, b):
    return lax.dot_general(a.astype(BF16), b.astype(BF16), (((1,), (1,)), ((), ())), preferred_element_type=F32)


def _dot_tn(a, b):
    return lax.dot_general(a.astype(BF16), b.astype(BF16), (((0,), (0,)), ((), ())), preferred_element_type=F32)


def _dot_f32(a, b):
    return jnp.dot(a, b, preferred_element_type=F32, precision=lax.Precision.HIGHEST)


def _dot_f32_nt(a, b):
    return lax.dot_general(a, b, (((1,), (1,)), ((), ())), preferred_element_type=F32, precision=lax.Precision.HIGHEST)


def _dot_f32_tn(a, b):
    return lax.dot_general(a, b, (((0,), (0,)), ((), ())), preferred_element_type=F32, precision=lax.Precision.HIGHEST)


def _sigmoid(x):
    return 1.0 / (1.0 + jnp.exp(-x))


def mm(a, b, *, ta=False, tb=False, res=None, out_dtype=F32, name):
    m, k = (a.shape[1], a.shape[0]) if ta else a.shape
    n = b.shape[0] if tb else b.shape[1]
    assert (b.shape[1] if tb else b.shape[0]) == k
    tm = _pick(m, (512, 256, 128))
    tn = _pick(n, (512, 256, 128))
    tk = _pick(k, (2048, 1408, 1024, 512, 256, 128))
    nk = k // tk
    has_res = res is not None

    def body(*refs):
        if has_res:
            a_ref, b_ref, r_ref, o_ref, acc_ref = refs
        else:
            a_ref, b_ref, o_ref, acc_ref = refs
        kk = pl.program_id(2)
        dn = (((0 if ta else 1,), (1 if tb else 0,)), ((), ()))
        part = lax.dot_general(a_ref[...].astype(BF16), b_ref[...].astype(BF16), dn, preferred_element_type=F32)

        @pl.when(kk == 0)
        def _():
            acc_ref[...] = part

        @pl.when(kk > 0)
        def _():
            acc_ref[...] += part

        @pl.when(kk == nk - 1)
        def _():
            out = acc_ref[...]
            if has_res:
                out = out + r_ref[...].astype(F32)
            o_ref[...] = out.astype(o_ref.dtype)

    a_spec = pl.BlockSpec((tk, tm), lambda i, j, q: (q, i)) if ta else pl.BlockSpec((tm, tk), lambda i, j, q: (i, q))
    b_spec = pl.BlockSpec((tn, tk), lambda i, j, q: (j, q)) if tb else pl.BlockSpec((tk, tn), lambda i, j, q: (q, j))
    o_spec = pl.BlockSpec((tm, tn), lambda i, j, q: (i, j))
    in_specs = [a_spec, b_spec] + ([o_spec] if has_res else [])
    args = (a, b) + ((res,) if has_res else ())
    return pl.pallas_call(
        body, grid=(m // tm, n // tn, nk), in_specs=in_specs, out_specs=o_spec,
        out_shape=jax.ShapeDtypeStruct((m, n), out_dtype),
        scratch_shapes=[pltpu.VMEM((tm, tn), F32)],
        compiler_params=_params(dimension_semantics=("parallel", "parallel", "arbitrary")), name=name,
    )(*args)


def rms_fwd(x, g, *, dep=None, name):
    L, D = x.shape
    tr = _pick(L, (256, 128))

    def body(x_ref, g_ref, *rest):
        o_ref = rest[-1]
        xv = x_ref[...]
        r = lax.rsqrt(jnp.mean(xv * xv, axis=-1, keepdims=True) + NORM_EPS)
        o_ref[...] = (xv * r * g_ref[...]).astype(o_ref.dtype)

    row = pl.BlockSpec((tr, D), lambda i: (i, 0))
    vec = pl.BlockSpec((1, D), lambda i: (0, 0))
    deps = [] if dep is None else [dep]
    return pl.pallas_call(body, grid=(L // tr,), in_specs=[row, vec] + [pl.BlockSpec((8, 128), lambda i: (0, 0))] * len(deps),
                          out_specs=row, out_shape=jax.ShapeDtypeStruct((L, D), BF16), name=name)(
        x, g.reshape(1, D), *deps)


def rms_bwd(x, g, dy, dres, *, name):
    L, D = x.shape
    tr = _pick(L, (256, 128))

    def body(x_ref, g_ref, dy_ref, dres_ref, dx_ref, dg_ref):
        xv = x_ref[...]
        r = lax.rsqrt(jnp.mean(xv * xv, axis=-1, keepdims=True) + NORM_EPS)
        xh = xv * r
        dyv = dy_ref[...].astype(F32)

        @pl.when(pl.program_id(0) == 0)
        def _():
            dg_ref[...] = jnp.zeros_like(dg_ref)

        dg_ref[...] += jnp.sum(dyv * xh, axis=0, keepdims=True)
        dxh = dyv * g_ref[...]
        dx_ref[...] = dres_ref[...] + r * (dxh - xh * jnp.mean(dxh * xh, axis=-1, keepdims=True))

    row = pl.BlockSpec((tr, D), lambda i: (i, 0))
    vec = pl.BlockSpec((1, D), lambda i: (0, 0))
    return pl.pallas_call(body, grid=(L // tr,), in_specs=[row, vec, row, row], out_specs=[row, vec],
                          out_shape=[jax.ShapeDtypeStruct((L, D), F32), jax.ShapeDtypeStruct((1, D), F32)],
                          compiler_params=_params(dimension_semantics=("arbitrary",)), name=name)(
        x, g.reshape(1, D), dy, dres)


def final_loss(h, g, target, *, name):
    L, D = h.shape
    tr = _pick(L, (256, 128))

    def body(x_ref, g_ref, t_ref, loss_ref, dx_ref, dg_ref):
        xv = x_ref[...]
        gv = g_ref[...]
        r = lax.rsqrt(jnp.mean(xv * xv, axis=-1, keepdims=True) + NORM_EPS)
        xh = xv * r
        err = xh * gv - t_ref[...]

        @pl.when(pl.program_id(0) == 0)
        def _():
            dg_ref[...] = jnp.zeros_like(dg_ref)
            loss_ref[...] = jnp.zeros_like(loss_ref)

        loss_ref[...] += 0.5 * jnp.sum(jnp.mean(err * err, axis=-1, keepdims=True), axis=0, keepdims=True)
        dyv = err * (1.0 / D)
        dg_ref[...] += jnp.sum(dyv * xh, axis=0, keepdims=True)
        dxh = dyv * gv
        dx_ref[...] = r * (dxh - xh * jnp.mean(dxh * xh, axis=-1, keepdims=True))

    row = pl.BlockSpec((tr, D), lambda i: (i, 0))
    vec = pl.BlockSpec((1, D), lambda i: (0, 0))
    one = pl.BlockSpec((1, 1), lambda i: (0, 0))
    return pl.pallas_call(body, grid=(L // tr,), in_specs=[row, vec, row], out_specs=[one, row, vec],
                          out_shape=[jax.ShapeDtypeStruct((1, 1), F32), jax.ShapeDtypeStruct((L, D), F32),
                                     jax.ShapeDtypeStruct((1, D), F32)],
                          compiler_params=_params(dimension_semantics=("arbitrary",)), name=name)(
        h, g.reshape(1, D), target)


def s5_scan_fwd(a_re, a_im, bu_re, bu_im, *, name):
    L, P = bu_re.shape
    W = _pick(P, (512, 256, 128))

    def body(ar_ref, ai_ref, br_ref, bi_ref, xr_ref, xi_ref):
        ar, ai = ar_ref[...], ai_ref[...]

        def step(t8, carry):
            xr, xi = carry
            base = pl.multiple_of(t8 * 8, 8)
            br = br_ref[pl.ds(base, 8), :]
            bi = bi_ref[pl.ds(base, 8), :]
            out_r, out_i = [], []
            for j in range(8):
                nr = ar * xr - ai * xi + br[j:j + 1, :]
                ni = ar * xi + ai * xr + bi[j:j + 1, :]
                xr, xi = nr, ni
                out_r.append(nr)
                out_i.append(ni)
            xr_ref[pl.ds(base, 8), :] = jnp.concatenate(out_r, axis=0)
            xi_ref[pl.ds(base, 8), :] = jnp.concatenate(out_i, axis=0)
            return xr, xi

        zero = jnp.zeros((1, W), F32)
        lax.fori_loop(0, L // 8, step, (zero, zero))

    vec = pl.BlockSpec((1, W), lambda j: (0, j))
    col = pl.BlockSpec((L, W), lambda j: (0, j))
    return pl.pallas_call(body, grid=(P // W,), in_specs=[vec, vec, col, col], out_specs=[col, col],
                          out_shape=[jax.ShapeDtypeStruct((L, P), F32)] * 2,
                          compiler_params=_params(dimension_semantics=("parallel",)), name=name)(
        a_re, a_im, bu_re, bu_im)


def s5_scan_bwd(a_re, a_im, xs_re, xs_im, dx_re, dx_im, *, name):
    L, P = xs_re.shape
    W = _pick(P, (256, 128))

    def body(ar_ref, ai_ref, xr_ref, xi_ref, dr_ref, di_ref, lr_ref, li_ref, dar_ref, dai_ref):
        ar, ai = ar_ref[...], ai_ref[...]
        nblk = L // 8

        def step(s, carry):
            lr, li = carry
            base = pl.multiple_of((nblk - 1 - s) * 8, 8)
            dr = dr_ref[pl.ds(base, 8), :]
            di = di_ref[pl.ds(base, 8), :]
            out_r, out_i = [None] * 8, [None] * 8
            for j in range(7, -1, -1):
                nr = dr[j:j + 1, :] + ar * lr + ai * li
                ni = di[j:j + 1, :] - ai * lr + ar * li
                lr, li = nr, ni
                out_r[j], out_i[j] = nr, ni
            lr_ref[pl.ds(base, 8), :] = jnp.concatenate(out_r, axis=0)
            li_ref[pl.ds(base, 8), :] = jnp.concatenate(out_i, axis=0)
            return lr, li

        zero = jnp.zeros((1, W), F32)
        lax.fori_loop(0, nblk, step, (zero, zero))
        row = lax.broadcasted_iota(jnp.int32, (L, W), 0)
        xpr = jnp.where(row >= 1, pltpu.roll(xr_ref[...], 1, 0), 0.0)
        xpi = jnp.where(row >= 1, pltpu.roll(xi_ref[...], 1, 0), 0.0)
        lr, li = lr_ref[...], li_ref[...]
        dar_ref[...] = jnp.sum(lr * xpr + li * xpi, axis=0, keepdims=True)
        dai_ref[...] = jnp.sum(li * xpr - lr * xpi, axis=0, keepdims=True)

    vec = pl.BlockSpec((1, W), lambda j: (0, j))
    col = pl.BlockSpec((L, W), lambda j: (0, j))
    return pl.pallas_call(body, grid=(P // W,), in_specs=[vec, vec, col, col, col, col],
                          out_specs=[col, col, vec, vec],
                          out_shape=[jax.ShapeDtypeStruct((L, P), F32)] * 2 + [jax.ShapeDtypeStruct((1, P), F32)] * 2,
                          compiler_params=_params(dimension_semantics=("parallel",)), name=name)(
        a_re, a_im, xs_re, xs_im, dx_re, dx_im)


def _gelu(y):
    c = math.sqrt(2.0 / math.pi)
    t = jnp.tanh(c * (y + 0.044715 * y * y * y))
    return 0.5 * y * (1.0 + t), t


def s5_out_fwd(y0, proj, dvec, glu_w, glu_b, *, name):
    L, C = y0.shape
    tr = _pick(L, (256, 128))

    def body(y_ref, u_ref, d_ref, w_ref, b_ref, o_ref):
        z, _ = _gelu(y_ref[...] + d_ref[...] * u_ref[...])
        zg = _dot(z, w_ref[...]) + b_ref[...]
        o_ref[...] = (z * _sigmoid(zg)).astype(o_ref.dtype)

    row = pl.BlockSpec((tr, C), lambda i: (i, 0))
    vec = pl.BlockSpec((1, C), lambda i: (0, 0))
    wsp = pl.BlockSpec((C, C), lambda i: (0, 0))
    return pl.pallas_call(body, grid=(L // tr,), in_specs=[row, row, vec, wsp, vec], out_specs=row,
                          out_shape=jax.ShapeDtypeStruct((L, C), BF16), name=name)(
        y0, proj, dvec, glu_w, glu_b)


def s5_out_bwd(y0, proj, dvec, glu_w, glu_b, dcat, *, name):
    L, C = y0.shape
    tr = _pick(L, (256, 128))

    def body(y_ref, u_ref, d_ref, w_ref, b_ref, do_ref, dy_ref, dud_ref, z_ref, dzg_ref, db_ref, dd_ref):
        u = u_ref[...]
        y = y_ref[...] + d_ref[...] * u
        z, t = _gelu(y)
        zg = _dot(z, w_ref[...]) + b_ref[...]
        s = _sigmoid(zg)
        do = do_ref[...]
        dzg = do * z * s * (1.0 - s)
        dz = do * s + _dot_nt(dzg, w_ref[...])
        c = math.sqrt(2.0 / math.pi)
        dgelu = 0.5 * (1.0 + t) + 0.5 * y * (1.0 - t * t) * c * (1.0 + 3.0 * 0.044715 * y * y)
        dy = dz * dgelu

        @pl.when(pl.program_id(0) == 0)
        def _():
            db_ref[...] = jnp.zeros_like(db_ref)
            dd_ref[...] = jnp.zeros_like(dd_ref)

        db_ref[...] += jnp.sum(dzg, axis=0, keepdims=True)
        dd_ref[...] += jnp.sum(dy * u, axis=0, keepdims=True)
        dy_ref[...] = dy
        dud_ref[...] = dy * d_ref[...]
        z_ref[...] = z.astype(BF16)
        dzg_ref[...] = dzg.astype(BF16)

    row = pl.BlockSpec((tr, C), lambda i: (i, 0))
    vec = pl.BlockSpec((1, C), lambda i: (0, 0))
    wsp = pl.BlockSpec((C, C), lambda i: (0, 0))
    return pl.pallas_call(body, grid=(L // tr,), in_specs=[row, row, vec, wsp, vec, row],
                          out_specs=[row, row, row, row, vec, vec],
                          out_shape=[jax.ShapeDtypeStruct((L, C), F32), jax.ShapeDtypeStruct((L, C), F32),
                                     jax.ShapeDtypeStruct((L, C), BF16), jax.ShapeDtypeStruct((L, C), BF16),
                                     jax.ShapeDtypeStruct((1, C), F32), jax.ShapeDtypeStruct((1, C), F32)],
                          compiler_params=_params(dimension_semantics=("arbitrary",)), name=name)(
        y0, proj, dvec, glu_w, glu_b, dcat)


def _hg_gates(xq, xf, lb, tri):
    C = xq.shape[0]
    sq = _sigmoid(xq)
    q = xq * sq
    sg = _sigmoid(xf)
    f = lb + (1.0 - lb) * sg
    kk = 1.0 - f
    b = _dot_f32(tri, jnp.log(f))
    bm = b[C // 2 - 1:C // 2, :]
    bl = b[C - 1:C, :]
    eb = jnp.exp(b)
    return dict(sq=sq, q=q, sg=sg, f=f, kk=kk, b=b, bm=bm, bl=bl, eb=eb, ebl=jnp.exp(bl),
                qb=q * eb, eqm=jnp.exp(b - bm), ekm=jnp.exp(bm - b), ekl=jnp.exp(bl - b))


def _tri(C, lower):
    r = lax.broadcasted_iota(jnp.int32, (C, C), 0)
    c = lax.broadcasted_iota(jnp.int32, (C, C), 1)
    return (r >= c) if lower else (c >= r)


def hgrn_fwd(proj, lb, norm_g, *, name):
    L = proj.shape[0]
    C, H, K = HG_CHUNK, HG_HEADS, HG_DIM
    HK = H * K
    nc = L // C

    def body(q_ref, f_ref, i_ref, g_ref, lb_ref, ng_ref, o_ref, sall_ref, st_ref):
        @pl.when(pl.program_id(0) == 0)
        def _():
            st_ref[...] = jnp.zeros_like(st_ref)

        mask = _tri(C, True)
        tri = mask.astype(F32)
        for h in range(H):
            sl = slice(h * K, (h + 1) * K)
            v = i_ref[:, sl]
            st = st_ref[h]
            sall_ref[h] = st
            gt = _hg_gates(q_ref[:, sl], f_ref[:, sl], lb_ref[:, sl], tri)
            qt = gt["q"] * gt["eqm"]
            kt = gt["kk"] * gt["ekm"]
            kh = gt["kk"] * gt["ekl"]
            att = jnp.where(mask, _dot_nt(qt, kt), 0.0)
            o = _dot(att, v) + _dot_nt(gt["qb"], st)
            st_ref[h] = st * gt["ebl"] + _dot_tn(v, kh)
            r = lax.rsqrt(jnp.mean(o * o, axis=-1, keepdims=True) + NORM_EPS)
            xg = g_ref[:, sl]
            o_ref[:, sl] = (o * r * ng_ref[:, sl] * (xg * _sigmoid(xg))).astype(o_ref.dtype)

    def blk(cb):
        return pl.BlockSpec((C, HK), lambda i: (i, cb))

    vec = pl.BlockSpec((1, HK), lambda i: (0, 0))
    return pl.pallas_call(
        body, grid=(nc,), in_specs=[blk(1), blk(2), blk(3), blk(4), vec, vec],
        out_specs=[pl.BlockSpec((C, HK), lambda i: (i, 0)), pl.BlockSpec((None, H, K, K), lambda i: (i, 0, 0, 0))],
        out_shape=[jax.ShapeDtypeStruct((L, HK), BF16), jax.ShapeDtypeStruct((nc, H, K, K), F32)],
        scratch_shapes=[pltpu.VMEM((H, K, K), F32)],
        compiler_params=_params(dimension_semantics=("arbitrary",)), name=name,
    )(proj, proj, proj, proj, lb, norm_g)


def hgrn_bwd(proj, lb, norm_g, sall, dcat, *, name):
    L = proj.shape[0]
    C, H, K = HG_CHUNK, HG_HEADS, HG_DIM
    HK = H * K
    nc = L // C

    def body(q_ref, f_ref, i_ref, g_ref, lb_ref, ng_ref, sall_ref, do_ref, dx_ref, dlb_ref, dng_ref, dst_ref):
        @pl.when(pl.program_id(0) == 0)
        def _():
            dst_ref[...] = jnp.zeros_like(dst_ref)
            dlb_ref[...] = jnp.zeros_like(dlb_ref)
            dng_ref[...] = jnp.zeros_like(dng_ref)

        mask = _tri(C, True)
        tri = mask.astype(F32)
        tri_t = _tri(C, False).astype(F32)
        rowi = lax.broadcasted_iota(jnp.int32, (C, K), 0)
        for h in range(H):
            sl = slice(h * K, (h + 1) * K)
            xq, xf, v, xg = q_ref[:, sl], f_ref[:, sl], i_ref[:, sl], g_ref[:, sl]
            lb_h, ng = lb_ref[:, sl], ng_ref[:, sl]
            st = sall_ref[h]
            dst = dst_ref[h]
            gt = _hg_gates(xq, xf, lb_h, tri)
            q, kk, qb = gt["q"], gt["kk"], gt["qb"]
            qt = q * gt["eqm"]
            kt = kk * gt["ekm"]
            kh = kk * gt["ekl"]
            att = jnp.where(mask, _dot_f32_nt(qt, kt), 0.0)
            o = _dot_f32(att, v) + _dot_f32_nt(qb, st)
            r = lax.rsqrt(jnp.mean(o * o, axis=-1, keepdims=True) + NORM_EPS)
            oh = o * r
            sgg = _sigmoid(xg)
            silu_g = xg * sgg
            d_ob = do_ref[:, sl]
            d_on = d_ob * silu_g
            dxg = d_ob * (oh * ng) * (sgg * (1.0 + xg * (1.0 - sgg)))
            dng_ref[:, sl] += jnp.sum(d_on * oh, axis=0, keepdims=True)
            doh = d_on * ng
            do = r * (doh - oh * jnp.mean(doh * oh, axis=-1, keepdims=True))
            d_qb = _dot_f32(do, st)
            datt = jnp.where(mask, _dot_f32_nt(do, v), 0.0)
            dv = _dot_f32_tn(att, do) + _dot_f32_nt(kh, dst)
            d_qt = _dot_f32(datt, kt)
            d_kt = _dot_f32_tn(datt, qt)
            d_kh = _dot_f32(v, dst)
            d_bl = jnp.sum(dst * st, axis=0, keepdims=True) * gt["ebl"] + jnp.sum(d_kh * kh, axis=0, keepdims=True)
            dst_ref[h] = dst * gt["ebl"] + _dot_f32_tn(do, qb)
            dq = d_qt * gt["eqm"] + d_qb * gt["eb"]
            db = d_qt * qt + d_qb * qb - d_kt * kt - d_kh * kh
            db = db + jnp.where(rowi == C - 1, d_bl, 0.0)
            dkk = d_kt * gt["ekm"] + d_kh * gt["ekl"]
            dlg = _dot_f32(tri_t, db)
            df = dlg / gt["f"] - dkk
            sg = gt["sg"]
            dxf = df * (1.0 - lb_h) * sg * (1.0 - sg)
            dlb_ref[:, sl] += jnp.sum(df * (1.0 - sg), axis=0, keepdims=True)
            sq = gt["sq"]
            dxq = dq * (sq * (1.0 + xq * (1.0 - sq)))
            dx_ref[:, h * K:(h + 1) * K] = dxq.astype(dx_ref.dtype)
            dx_ref[:, HK + h * K:HK + (h + 1) * K] = dxf.astype(dx_ref.dtype)
            dx_ref[:, 2 * HK + h * K:2 * HK + (h + 1) * K] = dv.astype(dx_ref.dtype)
            dx_ref[:, 3 * HK + h * K:3 * HK + (h + 1) * K] = dxg.astype(dx_ref.dtype)

    def blk(cb):
        return pl.BlockSpec((C, HK), lambda i: (nc - 1 - i, cb))

    vec = pl.BlockSpec((1, HK), lambda i: (0, 0))
    return pl.pallas_call(
        body, grid=(nc,),
        in_specs=[blk(1), blk(2), blk(3), blk(4), vec, vec,
                  pl.BlockSpec((None, H, K, K), lambda i: (nc - 1 - i, 0, 0, 0)), blk(1)],
        out_specs=[pl.BlockSpec((C, 4 * HK), lambda i: (nc - 1 - i, 0)), vec, vec],
        out_shape=[jax.ShapeDtypeStruct((L, 4 * HK), BF16), jax.ShapeDtypeStruct((1, HK), F32),
                   jax.ShapeDtypeStruct((1, HK), F32)],
        scratch_shapes=[pltpu.VMEM((H, K, K), F32)],
        compiler_params=_params(dimension_semantics=("arbitrary",)), name=name,
    )(proj, proj, proj, proj, lb, norm_g, sall, dcat)


def _shift_down(x, k, row):
    return jnp.where(row >= k, pltpu.roll(x, k, 0), 0.0)


def _shift_up(x, k, row):
    n = x.shape[0]
    return jnp.where(row < n - k, pltpu.roll(x, n - k, 0), 0.0)


def convgate_fwd(hu, conv_w, conv_b, *, name):
    L, C2 = hu.shape
    C = C2 // 2
    tc = _pick(C, (256, 128))
    nb = C // tc

    def body(a_ref, b_ref, wa_ref, wb_ref, ba_ref, bb_ref, o_ref):
        row = lax.broadcasted_iota(jnp.int32, (L, tc), 0)

        def conv(x, w, bias):
            return w[2:3, :] * x + w[1:2, :] * _shift_down(x, 1, row) + w[0:1, :] * _shift_down(x, 2, row) + bias

        ca = conv(a_ref[...], wa_ref[...], ba_ref[...])
        cb = conv(b_ref[...], wb_ref[...], bb_ref[...])
        o_ref[...] = (ca * _sigmoid(ca) * cb).astype(o_ref.dtype)

    def col(off, rows):
        return pl.BlockSpec((rows, tc), lambda j: (0, j + off))

    return pl.pallas_call(
        body, grid=(nb,), in_specs=[col(0, L), col(nb, L), col(0, 3), col(nb, 3), col(0, 1), col(nb, 1)],
        out_specs=col(0, L), out_shape=jax.ShapeDtypeStruct((L, C), BF16),
        compiler_params=_params(dimension_semantics=("parallel",)), name=name,
    )(hu, hu, conv_w, conv_w, conv_b, conv_b)


def convgate_bwd(hu, conv_w, conv_b, dact, *, name):
    L, C2 = hu.shape
    C = C2 // 2
    tc = _pick(C, (256, 128))
    nb = C // tc

    def body(a_ref, b_ref, wa_ref, wb_ref, ba_ref, bb_ref, d_ref, dxa_ref, dxb_ref, dwa_ref, dwb_ref, dba_ref, dbb_ref):
        row = lax.broadcasted_iota(jnp.int32, (L, tc), 0)

        def conv(x, w, bias):
            x1 = _shift_down(x, 1, row)
            x2 = _shift_down(x, 2, row)
            return w[2:3, :] * x + w[1:2, :] * x1 + w[0:1, :] * x2 + bias, x1, x2

        xa, xb = a_ref[...], b_ref[...]
        wa, wb = wa_ref[...], wb_ref[...]
        ca, xa1, xa2 = conv(xa, wa, ba_ref[...])
        cb, xb1, xb2 = conv(xb, wb, bb_ref[...])
        d = d_ref[...]
        sa = _sigmoid(ca)
        dca = d * cb * (sa * (1.0 + ca * (1.0 - sa)))
        dcb = d * (ca * sa)

        def back(dc, w, x, x1, x2, dx_ref, dw_ref, db_ref):
            dx = w[2:3, :] * dc + w[1:2, :] * _shift_up(dc, 1, row) + w[0:1, :] * _shift_up(dc, 2, row)
            dx_ref[...] = dx.astype(dx_ref.dtype)
            dw_ref[...] = jnp.concatenate([jnp.sum(dc * x2, axis=0, keepdims=True),
                                           jnp.sum(dc * x1, axis=0, keepdims=True),
                                           jnp.sum(dc * x, axis=0, keepdims=True)], axis=0)
            db_ref[...] = jnp.sum(dc, axis=0, keepdims=True)

        back(dca, wa, xa, xa1, xa2, dxa_ref, dwa_ref, dba_ref)
        back(dcb, wb, xb, xb1, xb2, dxb_ref, dwb_ref, dbb_ref)

    def col(off, rows):
        return pl.BlockSpec((rows, tc), lambda j: (0, j + off))

    outs = pl.pallas_call(
        body, grid=(nb,),
        in_specs=[col(0, L), col(nb, L), col(0, 3), col(nb, 3), col(0, 1), col(nb, 1), col(0, L)],
        out_specs=[col(0, L), col(0, L), col(0, 3), col(0, 3), col(0, 1), col(0, 1)],
        out_shape=[jax.ShapeDtypeStruct((L, C), BF16)] * 2 + [jax.ShapeDtypeStruct((3, C), F32)] * 2
        + [jax.ShapeDtypeStruct((1, C), F32)] * 2,
        compiler_params=_params(dimension_semantics=("parallel",)), name=name,
    )(hu, hu, conv_w, conv_w, conv_b, conv_b, dact)
    dxa, dxb, dwa, dwb, dba, dbb = outs
    return (jnp.concatenate([dxa, dxb], axis=1), jnp.concatenate([dwa, dwb], axis=1),
            jnp.concatenate([dba, dbb], axis=1))


def _to_branch_order(t, d):
    L, W = t.shape
    return t if d == 1 else t.reshape(L // d, d, W).transpose(1, 0, 2).reshape(L, W)


def _to_token_order(t, d):
    L, W = t.shape
    return t if d == 1 else t.reshape(d, L // d, W).transpose(1, 0, 2).reshape(L, W)


def rope_tables(positions):
    half = ROT_DIM // 2
    inv_freq = ROPE_THETA ** (-jnp.arange(half, dtype=F32) * 2.0 / ROT_DIM)
    ang = positions.astype(F32)[:, None] * inv_freq
    cos, sin = jnp.cos(ang), jnp.sin(ang)
    L = positions.shape[0]
    one = jnp.ones((L, ATT_E - ROT_DIM), F32)
    zero = jnp.zeros((L, ATT_E - ROT_DIM), F32)
    zh = jnp.zeros((L, half), F32)
    tc = jnp.concatenate([cos, cos, one], axis=1)
    ts1 = jnp.concatenate([zh, sin, zero], axis=1)
    ts2 = jnp.concatenate([-sin, zh, zero], axis=1)
    return tuple(jnp.concatenate([t, t], axis=1) for t in (tc, ts1, ts2))


def rope_fwd(qkv, tabs, *, name):
    L = qkv.shape[0]
    W = 512
    tr = _pick(L, (256, 128))
    nq = 1536 // W
    scale = ATT_E ** -0.5

    def body(x_ref, c_ref, s1_ref, s2_ref, o_ref):
        j = pl.program_id(1)
        x = x_ref[...]
        c = jnp.concatenate([c_ref[...]] * 4, axis=1)
        s1 = jnp.concatenate([s1_ref[...]] * 4, axis=1)
        s2 = jnp.concatenate([s2_ref[...]] * 4, axis=1)
        rot = x * c + pltpu.roll(x, 8, 1) * s1 + pltpu.roll(x, W - 8, 1) * s2
        mult = jnp.where(j < nq, scale, 1.0)
        o_ref[...] = jnp.where(j < 2 * nq, rot * mult, x).astype(o_ref.dtype)

    blk = pl.BlockSpec((tr, W), lambda i, j: (i, j))
    tab = pl.BlockSpec((tr, 128), lambda i, j: (i, 0))
    return pl.pallas_call(body, grid=(L // tr, 3 * nq), in_specs=[blk, tab, tab, tab], out_specs=blk,
                          out_shape=jax.ShapeDtypeStruct((L, 3 * 1536), BF16),
                          compiler_params=_params(dimension_semantics=("parallel", "parallel")), name=name)(
        qkv, *tabs)


def rope_bwd(dq, dk, dv, tabs, *, name):
    L = dq.shape[0]
    W = 512
    tr = _pick(L, (256, 128))
    nq = 1536 // W
    scale = ATT_E ** -0.5

    def body(dq_ref, dk_ref, dv_ref, c_ref, s1_ref, s2_ref, o_ref):
        j = pl.program_id(1)
        c = jnp.concatenate([c_ref[...]] * 4, axis=1)
        s1 = jnp.concatenate([s1_ref[...]] * 4, axis=1)
        s2 = jnp.concatenate([s2_ref[...]] * 4, axis=1)

        def unrot(dy):
            return dy * c + pltpu.roll(dy * s1, W - 8, 1) + pltpu.roll(dy * s2, 8, 1)

        @pl.when(j < nq)
        def _():
            o_ref[...] = (unrot(dq_ref[...]) * scale).astype(o_ref.dtype)

        @pl.when((j >= nq) & (j < 2 * nq))
        def _():
            o_ref[...] = unrot(dk_ref[...]).astype(o_ref.dtype)

        @pl.when(j >= 2 * nq)
        def _():
            o_ref[...] = dv_ref[...].astype(o_ref.dtype)

    def src(k):
        return pl.BlockSpec((tr, W), lambda i, j: (i, jnp.clip(j - k * nq, 0, nq - 1)))

    tab = pl.BlockSpec((tr, 128), lambda i, j: (i, 0))
    return pl.pallas_call(body, grid=(L // tr, 3 * nq), in_specs=[src(0), src(1), src(2), tab, tab, tab],
                          out_specs=pl.BlockSpec((tr, W), lambda i, j: (i, j)),
                          out_shape=jax.ShapeDtypeStruct((L, 3 * 1536), BF16),
                          compiler_params=_params(dimension_semantics=("parallel", "arbitrary")), name=name)(
        dq, dk, dv, *tabs)


def _att_masks(has_prev):
    qi = lax.broadcasted_iota(jnp.int32, (ATT_BLOCK, ATT_BLOCK), 0)
    kj = lax.broadcasted_iota(jnp.int32, (ATT_BLOCK, ATT_BLOCK), 1)
    return qi >= kj, (kj >= qi) & has_prev


def attn_fwd(qp, kp, vp, d, *, name):
    L, W = qp.shape
    B, E = ATT_BLOCK, ATT_E
    nblk = L // B
    nb = nblk // d

    def body(q_ref, kc_ref, kp_ref, vc_ref, vp_ref, o_ref, l_ref):
        has_prev = (pl.program_id(0) % nb) > 0
        mc, mp = _att_masks(has_prev)
        for h in range(ATT_HPG):
            sl = slice(h * E, (h + 1) * E)
            q = q_ref[:, sl]
            sc = jnp.where(mc, _dot_nt(q, kc_ref[:, sl]), NEG_BIG)
            sp = jnp.where(mp, _dot_nt(q, kp_ref[:, sl]), NEG_BIG)
            m = jnp.maximum(jnp.max(sc, axis=-1, keepdims=True), jnp.max(sp, axis=-1, keepdims=True))
            pc = jnp.exp(sc - m)
            pp = jnp.exp(sp - m)
            den = jnp.sum(pc, axis=-1, keepdims=True) + jnp.sum(pp, axis=-1, keepdims=True)
            o = (_dot(pc, vc_ref[:, sl]) + _dot(pp, vp_ref[:, sl])) / den
            o_ref[:, sl] = o
            l_ref[:, sl] = jnp.broadcast_to(m + jnp.log(den), (B, E))

    cur = pl.BlockSpec((B, W), lambda j: (j, 0))
    prev = pl.BlockSpec((B, W), lambda j: (jnp.maximum(j - 1, 0), 0))
    return pl.pallas_call(body, grid=(nblk,), in_specs=[cur, cur, prev, cur, prev], out_specs=[cur, cur],
                          out_shape=[jax.ShapeDtypeStruct((L, W), F32)] * 2,
                          compiler_params=_params(dimension_semantics=("parallel",)), name=name)(
        qp, kp, kp, vp, vp)


def attn_bwd(qp, kp, vp, lse, do, dl, d, *, name):
    L, W = qp.shape
    B, E = ATT_BLOCK, ATT_E
    nblk = L // B
    nb = nblk // d

    def body(q_ref, kc_ref, kp_ref, vc_ref, vp_ref, l_ref, do_ref, dl_ref, dq_ref, dk_ref, dv_ref, tkc, tkp, tvc, tvp):
        j = pl.program_id(0)

        @pl.when(j == 0)
        def _():
            dk_ref[...] = jnp.zeros_like(dk_ref)
            dv_ref[...] = jnp.zeros_like(dv_ref)

        has_prev = (j % nb) > 0
        mc, mp = _att_masks(has_prev)
        for h in range(ATT_HPG):
            sl = slice(h * E, (h + 1) * E)
            q = q_ref[:, sl]
            kc, kpv, vc, vpv = kc_ref[:, sl], kp_ref[:, sl], vc_ref[:, sl], vp_ref[:, sl]
            lse_h = l_ref[:, h * E:h * E + 1]
            dl_h = dl_ref[:, h * E:h * E + 1]
            doh = do_ref[:, sl]
            pc = jnp.where(mc, jnp.exp(_dot_nt(q, kc) - lse_h), 0.0)
            pp = jnp.where(mp, jnp.exp(_dot_nt(q, kpv) - lse_h), 0.0)
            dsc = pc * (_dot_nt(doh, vc) - dl_h)
            dsp = pp * (_dot_nt(doh, vpv) - dl_h)
            dq_ref[:, sl] = _dot(dsc, kc) + _dot(dsp, kpv)
            tkc[:, sl] = _dot_tn(dsc, q)
            tkp[:, sl] = _dot_tn(dsp, q)
            tvc[:, sl] = _dot_tn(pc, doh)
            tvp[:, sl] = _dot_tn(pp, doh)
        cur = pl.multiple_of(j * B, B)
        prv = pl.multiple_of(jnp.maximum(j - 1, 0) * B, B)
        dk_ref[pl.ds(cur, B), :] += tkc[...]
        dv_ref[pl.ds(cur, B), :] += tvc[...]
        dk_ref[pl.ds(prv, B), :] += tkp[...]
        dv_ref[pl.ds(prv, B), :] += tvp[...]

    cur = pl.BlockSpec((B, W), lambda j: (j, 0))
    prev = pl.BlockSpec((B, W), lambda j: (jnp.maximum(j - 1, 0), 0))
    full = pl.BlockSpec((L, W), lambda j: (0, 0))
    return pl.pallas_call(body, grid=(nblk,), in_specs=[cur, cur, prev, cur, prev, cur, cur, cur],
                          out_specs=[cur, full, full], out_shape=[jax.ShapeDtypeStruct((L, W), F32)] * 3,
                          scratch_shapes=[pltpu.VMEM((B, W), F32)] * 4,
                          compiler_params=_params(dimension_semantics=("arbitrary",)), name=name)(
        qp, kp, kp, vp, vp, lse, do, dl)


def _merge_alpha(l_refs):
    ls = [r[...] for r in l_refs]
    m = jnp.maximum(jnp.maximum(ls[0], ls[1]), ls[2])
    es = [jnp.exp(l - m) for l in ls]
    den = es[0] + es[1] + es[2]
    return [e / den for e in es]


def merge_fwd(os_, ls_, *, name):
    L, W = os_[0].shape
    tr = _pick(L, (256, 128))

    def body(o0, o1, o2, l0, l1, l2, out_ref):
        al = _merge_alpha((l0, l1, l2))
        out_ref[...] = (al[0] * o0[...] + al[1] * o1[...] + al[2] * o2[...]).astype(out_ref.dtype)

    row = pl.BlockSpec((tr, W), lambda i: (i, 0))
    return pl.pallas_call(body, grid=(L // tr,), in_specs=[row] * 6, out_specs=row,
                          out_shape=jax.ShapeDtypeStruct((L, W), BF16), name=name)(*os_, *ls_)


def merge_bwd(os_, ls_, do, *, name):
    L, W = do.shape
    tr = _pick(L, (256, 128))

    def body(o0, o1, o2, l0, l1, l2, do_ref, d0, d1, d2, e0, e1, e2):
        al = _merge_alpha((l0, l1, l2))
        dov = do_ref[...]
        r = lax.broadcasted_iota(jnp.int32, (W, W), 0) // ATT_E
        c = lax.broadcasted_iota(jnp.int32, (W, W), 1) // ATT_E
        ones_blk = (r == c).astype(F32)
        t = jnp.zeros_like(dov)
        for a, o in zip(al, (o0, o1, o2)):
            t = t + a * _dot_f32(dov * o[...], ones_blk)
        for a, d_ref, e_ref in zip(al, (d0, d1, d2), (e0, e1, e2)):
            d_ref[...] = a * dov
            e_ref[...] = a * t

    row = pl.BlockSpec((tr, W), lambda i: (i, 0))
    return pl.pallas_call(body, grid=(L // tr,), in_specs=[row] * 7, out_specs=[row] * 6,
                          out_shape=[jax.ShapeDtypeStruct((L, W), F32)] * 6, name=name)(*os_, *ls_, do)


def _me_and_peers():
    x, y, c = lax.axis_index("x"), lax.axis_index("y"), lax.axis_index("c")
    peers = []
    for k in range(1, N_DEV):
        px = 1 - x if k & 4 else x
        py = 1 - y if k & 2 else y
        pc = 1 - c if k & 1 else c
        peers.append((px, py, pc))
    return (x, y, c), peers


def _index(dev):
    return 4 * dev[0] + 2 * dev[1] + dev[2]


def _hbm(a):
    return pltpu.with_memory_space_constraint(a, pltpu.HBM)


HBM_SPEC = pl.BlockSpec(memory_space=pltpu.HBM)
SEM_SPEC = pl.BlockSpec(memory_space=pltpu.SEMAPHORE)
DATAFLOW = pltpu.SideEffectType.DATAFLOW_SIDE_EFFECTING


def _remote(src_ref, land_ref, slotted, me, peer, src_is_mine, send_sem, recv_sem, k):
    sender, receiver = (me, peer) if src_is_mine else (peer, me)
    src = src_ref.at[_index(receiver)] if slotted else src_ref
    return pltpu.make_async_remote_copy(src_ref=src, dst_ref=land_ref.at[_index(sender)], send_sem=send_sem.at[k],
                                        recv_sem=recv_sem.at[k], device_id=peer, device_id_type=MESH_ID)


def copies_start(arrays, slotted, *, name):
    n = len(arrays)
    lands = [lax.empty(a.shape if slotted else (N_DEV,) + a.shape, a.dtype) for a in arrays]

    def body(*refs):
        x_refs, land_refs = refs[:n], refs[n:2 * n]
        send, recv = refs[2 * n:3 * n], refs[3 * n:4 * n]
        token = refs[-1]
        me, peers = _me_and_peers()
        for w in range(n):
            for k, peer in enumerate(peers):
                _remote(x_refs[w], land_refs[w], slotted, me, peer, True, send[w], recv[w], k).start()
        token[...] = jnp.zeros_like(token)

    sem = pltpu.SemaphoreType.DMA((N_DEV - 1,))
    out_shape = ([sem] * (2 * n) + [pltpu.HBM(a.shape, a.dtype) for a in arrays]
                 + [pltpu.HBM(l.shape, l.dtype) for l in lands] + [jax.ShapeDtypeStruct((8, 128), F32)])
    outs = pl.pallas_call(
        body, name=name, out_shape=out_shape, in_specs=[HBM_SPEC] * (2 * n),
        out_specs=[SEM_SPEC] * (2 * n) + [HBM_SPEC] * (2 * n) + [pl.BlockSpec(memory_space=pltpu.VMEM)],
        input_output_aliases={i: 2 * n + i for i in range(2 * n)},
        compiler_params=pltpu.CompilerParams(has_side_effects=DATAFLOW),
    )(*[_hbm(a) for a in arrays], *[_hbm(l) for l in lands])
    handles = [(outs[w], outs[n + w], outs[2 * n + w], outs[3 * n + w]) for w in range(n)]
    return handles, outs[-1]


def copies_wait(handle, slotted, after, *, name):
    send_sem, recv_sem, x_thru, land_thru = handle

    def body(x_ref, land_ref, send_ref, recv_ref, after_ref, x_out, land_out):
        me, peers = _me_and_peers()
        for k, peer in enumerate(peers):
            _remote(x_ref, land_ref, slotted, me, peer, True, send_ref, recv_ref, k).wait_send()
        for k, peer in enumerate(peers):
            _remote(x_ref, land_ref, slotted, me, peer, False, send_ref, recv_ref, k).wait_recv()

    return pl.pallas_call(
        body, name=name, out_shape=(pltpu.HBM(x_thru.shape, x_thru.dtype), pltpu.HBM(land_thru.shape, land_thru.dtype)),
        in_specs=(HBM_SPEC, HBM_SPEC, SEM_SPEC, SEM_SPEC, pl.BlockSpec(memory_space=pl.ANY)),
        out_specs=(HBM_SPEC, HBM_SPEC), input_output_aliases={0: 0, 1: 1},
        compiler_params=pltpu.CompilerParams(has_side_effects=DATAFLOW),
    )(x_thru, land_thru, send_sem, recv_sem, after)


def cast_bf16(x, *, name):
    R, C = x.shape
    tr = _pick(R, (512, 256, 128, 64))

    def body(x_ref, o_ref):
        o_ref[...] = x_ref[...].astype(BF16)

    row = pl.BlockSpec((tr, C), lambda i: (i, 0))
    return pl.pallas_call(body, grid=(R // tr,), in_specs=[row], out_specs=row,
                          out_shape=jax.ShapeDtypeStruct((R, C), BF16), name=name)(x)


def _my_index():
    return 4 * lax.axis_index("x") + 2 * lax.axis_index("y") + lax.axis_index("c")


def cols_from_shards(g, own, *, name):
    _, K, n = g.shape
    tk = _pick(K, (256, 128))

    def body(g_ref, own_ref, o_ref):
        me = _my_index()
        for i in range(N_DEV):
            @pl.when(me == i)
            def _():
                o_ref[:, i * n:(i + 1) * n] = own_ref[...]

            @pl.when(me != i)
            def _():
                o_ref[:, i * n:(i + 1) * n] = g_ref[i]

    return pl.pallas_call(body, grid=(K // tk,),
                          in_specs=[pl.BlockSpec((N_DEV, tk, n), lambda i: (0, i, 0)), pl.BlockSpec((tk, n), lambda i: (i, 0))],
                          out_specs=pl.BlockSpec((tk, N_DEV * n), lambda i: (i, 0)),
                          out_shape=jax.ShapeDtypeStruct((K, N_DEV * n), g.dtype), name=name)(g, own)


def rows_from_shards(g, own, *, name):
    _, k, N = g.shape

    def body(g_ref, own_ref, o_ref):
        mine = _my_index() == pl.program_id(0)

        @pl.when(mine)
        def _():
            o_ref[...] = own_ref[...]

        @pl.when(jnp.logical_not(mine))
        def _():
            o_ref[...] = g_ref[...]

    return pl.pallas_call(body, grid=(N_DEV,),
                          in_specs=[pl.BlockSpec((None, k, N), lambda i: (i, 0, 0)), pl.BlockSpec((k, N), lambda i: (0, 0))],
                          out_specs=pl.BlockSpec((k, N), lambda i: (i, 0)),
                          out_shape=jax.ShapeDtypeStruct((N_DEV * k, N), g.dtype), name=name)(g, own)


def shards_from_cols(w, *, name):
    K, N = w.shape
    n = N // N_DEV
    tk = _pick(K, (256, 128))

    def body(w_ref, o_ref):
        for i in range(N_DEV):
            o_ref[i] = w_ref[:, i * n:(i + 1) * n].astype(o_ref.dtype)

    return pl.pallas_call(body, grid=(K // tk,), in_specs=[pl.BlockSpec((tk, N), lambda i: (i, 0))],
                          out_specs=pl.BlockSpec((N_DEV, tk, n), lambda i: (0, i, 0)),
                          out_shape=jax.ShapeDtypeStruct((N_DEV, K, n), BF16), name=name)(w)


def _adamw(w, g, m, v):
    m = ADAM_B1 * m + (1.0 - ADAM_B1) * g
    v = ADAM_B2 * v + (1.0 - ADAM_B2) * (g * g)
    m_hat = m / (1.0 - ADAM_B1 ** ADAM_STEP)
    v_hat = v / (1.0 - ADAM_B2 ** ADAM_STEP)
    delta = -ADAM_LR * (m_hat / (jnp.sqrt(v_hat) + ADAM_EPS) + ADAM_WD * w)
    return delta, m, v


def reduce_adamw(recv, own, own_slotted, me, w, m, v, *, name):
    _, R, C = recv.shape
    tr = _pick(R, (256, 128, 64, 32, 16, 8))

    def body(me_ref, r_ref, own_ref, w_ref, m_ref, v_ref, g_ref, d_ref, nm_ref, nv_ref):
        mine = me_ref[0]
        g = None
        for i in range(N_DEV):
            part = jnp.where(mine == i, own_ref[...], r_ref[i]).astype(F32)
            g = part if g is None else g + part
        delta, nm, nv = _adamw(w_ref[...], g, m_ref[...], v_ref[...])
        g_ref[...] = g
        d_ref[...] = delta
        nm_ref[...] = nm
        nv_ref[...] = nv

    row = pl.BlockSpec((tr, C), lambda i, me_ref: (i, 0))
    own_spec = pl.BlockSpec((None, tr, C), lambda i, me_ref: (me_ref[0], i, 0)) if own_slotted else row
    grid_spec = pltpu.PrefetchScalarGridSpec(
        num_scalar_prefetch=1, grid=(R // tr,),
        in_specs=[pl.BlockSpec((N_DEV, tr, C), lambda i, me_ref: (0, i, 0)), own_spec, row, row, row],
        out_specs=[row] * 4)
    return pl.pallas_call(body, grid_spec=grid_spec, out_shape=[jax.ShapeDtypeStruct((R, C), F32)] * 4,
                          compiler_params=_params(dimension_semantics=("parallel",)), name=name)(
        me.reshape(1).astype(jnp.int32), recv, own, w, m, v)


def _s5_prepare(A_re, A_im, log_dt, B_re, B_im, C_re, C_im):
    G, P, Cg = S5_GROUPS, S5_STATE, S5_GROUP
    dt = jnp.exp(log_dt)[:, None]
    mag = jnp.exp(A_re * dt)
    ab_re = mag * jnp.cos(A_im * dt)
    ab_im = mag * jnp.sin(A_im * dt)
    den = A_re * A_re + A_im * A_im
    nr, ni = ab_re - 1.0, ab_im
    c_re = (nr * A_re + ni * A_im) / den
    c_im = (ni * A_re - nr * A_im) / den
    Bb_re = c_re[..., None] * B_re - c_im[..., None] * B_im
    Bb_im = c_re[..., None] * B_im + c_im[..., None] * B_re
    eye = jnp.eye(G, dtype=F32)

    def dense_in(b):
        return jnp.einsum('gpc,gh->gchp', b, eye).reshape(G * Cg, G * P)

    def dense_out(c):
        return jnp.einsum('gcp,gh->gphc', c, eye).reshape(G * P, G * Cg)

    return (ab_re.reshape(1, G * P), ab_im.reshape(1, G * P), dense_in(Bb_re), dense_in(Bb_im),
            dense_out(C_re), dense_out(-C_im))


def _lower_bound(gamma):
    return jnp.cumsum(jax.nn.softmax(gamma, axis=0), axis=0)[0:1]


def _ffn_fwd(h, g_norm, get_w_in, conv_w, conv_b, get_w_out, tag):
    hn = rms_fwd(h, g_norm, name=tag + "_rms")
    w_in = get_w_in(hn)
    hu = mm(hn, w_in, name=tag + "_in")
    act = convgate_fwd(hu, conv_w, conv_b, name=tag + "_gate")
    w_out = get_w_out(act)
    h_out = mm(act, w_out, res=h, name=tag + "_out")
    return h_out, (hn, hu, act), w_in, w_out


def _ffn_bwd(h, g_norm, w_in, conv_w, conv_b, w_out, saved, dh, tag, send_dw_in, send_dw_out):
    hn, hu, act = saved
    send_dw_out(mm(act, dh, ta=True, name=tag + "_dwout"))
    dact = mm(dh, w_out, tb=True, name=tag + "_dact")
    dhu, dconv_w, dconv_b = convgate_bwd(hu, conv_w, conv_b, dact, name=tag + "_dgate")
    send_dw_in(mm(hn, dhu, ta=True, name=tag + "_dwin"))
    dhn = mm(dhu, w_in, tb=True, name=tag + "_dhn")
    dh_in, dg = rms_bwd(h, g_norm, dhn, dh, name=tag + "_drms")
    return dh_in, dg, dconv_w, dconv_b


def kernel(x, positions, norm_mix, norm_ffn, norm_final, mix_w_in, mix_w_out, s5_A_re, s5_A_im, s5_log_dt, s5_B_re, s5_B_im, s5_C_re, s5_C_im, s5_D, s5_glu_w, s5_glu_b, hgrn_gamma, hgrn_norm, att_w_qkv, att_w_o, ffn_w_in, ffn_conv_w, ffn_conv_b, ffn_w_out, loss_target, m_norm_mix, m_norm_ffn, m_norm_final, m_mix_w_in, m_mix_w_out, m_s5_A_re, m_s5_A_im, m_s5_log_dt, m_s5_B_re, m_s5_B_im, m_s5_C_re, m_s5_C_im, m_s5_D, m_s5_glu_w, m_s5_glu_b, m_hgrn_gamma, m_hgrn_norm, m_att_w_qkv, m_att_w_o, m_ffn_w_in, m_ffn_conv_w, m_ffn_conv_b, m_ffn_w_out, v_norm_mix, v_norm_ffn, v_norm_final, v_mix_w_in, v_mix_w_out, v_s5_A_re, v_s5_A_im, v_s5_log_dt, v_s5_B_re, v_s5_B_im, v_s5_C_re, v_s5_C_im, v_s5_D, v_s5_glu_w, v_s5_glu_b, v_hgrn_gamma, v_hgrn_norm, v_att_w_qkv, v_att_w_o, v_ffn_w_in, v_ffn_conv_w, v_ffn_conv_b, v_ffn_w_out):
    W = dict(norm_mix=norm_mix, norm_ffn=norm_ffn, norm_final=norm_final, mix_w_in=mix_w_in, mix_w_out=mix_w_out,
             s5_A_re=s5_A_re, s5_A_im=s5_A_im, s5_log_dt=s5_log_dt, s5_B_re=s5_B_re, s5_B_im=s5_B_im,
             s5_C_re=s5_C_re, s5_C_im=s5_C_im, s5_D=s5_D, s5_glu_w=s5_glu_w, s5_glu_b=s5_glu_b,
             hgrn_gamma=hgrn_gamma, hgrn_norm=hgrn_norm, att_w_qkv=att_w_qkv, att_w_o=att_w_o, ffn_w_in=ffn_w_in,
             ffn_conv_w=ffn_conv_w, ffn_conv_b=ffn_conv_b, ffn_w_out=ffn_w_out)
    M = dict(norm_mix=m_norm_mix, norm_ffn=m_norm_ffn, norm_final=m_norm_final, mix_w_in=m_mix_w_in,
             mix_w_out=m_mix_w_out, s5_A_re=m_s5_A_re, s5_A_im=m_s5_A_im, s5_log_dt=m_s5_log_dt, s5_B_re=m_s5_B_re,
             s5_B_im=m_s5_B_im, s5_C_re=m_s5_C_re, s5_C_im=m_s5_C_im, s5_D=m_s5_D, s5_glu_w=m_s5_glu_w,
             s5_glu_b=m_s5_glu_b, hgrn_gamma=m_hgrn_gamma, hgrn_norm=m_hgrn_norm, att_w_qkv=m_att_w_qkv,
             att_w_o=m_att_w_o, ffn_w_in=m_ffn_w_in, ffn_conv_w=m_ffn_conv_w, ffn_conv_b=m_ffn_conv_b,
             ffn_w_out=m_ffn_w_out)
    V = dict(norm_mix=v_norm_mix, norm_ffn=v_norm_ffn, norm_final=v_norm_final, mix_w_in=v_mix_w_in,
             mix_w_out=v_mix_w_out, s5_A_re=v_s5_A_re, s5_A_im=v_s5_A_im, s5_log_dt=v_s5_log_dt, s5_B_re=v_s5_B_re,
             s5_B_im=v_s5_B_im, s5_C_re=v_s5_C_re, s5_C_im=v_s5_C_im, s5_D=v_s5_D, s5_glu_w=v_s5_glu_w,
             s5_glu_b=v_s5_glu_b, hgrn_gamma=v_hgrn_gamma, hgrn_norm=v_hgrn_norm, att_w_qkv=v_att_w_qkv,
             att_w_o=v_att_w_o, ffn_w_in=v_ffn_w_in, ffn_conv_w=v_ffn_conv_w, ffn_conv_b=v_ffn_conv_b,
             ffn_w_out=v_ffn_w_out)
    return _step(x[0], positions[0], loss_target[0], W, M, V)


BIG = ("mix_w_in", "mix_w_out", "s5_glu_w", "att_w_qkv", "att_w_o", "ffn_w_in", "ffn_w_out")
COL_SHARDED = ("mix_w_in", "att_w_qkv", "att_w_o", "ffn_w_in")
SMALL = ("norm_mix", "norm_ffn", "norm_final", "s5_A_re", "s5_A_im", "s5_log_dt", "s5_B_re", "s5_B_im", "s5_C_re",
         "s5_C_im", "s5_D", "s5_glu_b", "hgrn_gamma", "hgrn_norm", "ffn_conv_b")
ORDER = ("norm_mix", "norm_ffn", "norm_final", "mix_w_in", "mix_w_out", "s5_A_re", "s5_A_im", "s5_log_dt", "s5_B_re",
         "s5_B_im", "s5_C_re", "s5_C_im", "s5_D", "s5_glu_w", "s5_glu_b", "hgrn_gamma", "hgrn_norm", "att_w_qkv",
         "att_w_o", "ffn_w_in", "ffn_conv_w", "ffn_conv_b", "ffn_w_out")
PACK_COLS = 1024


def _step(x, positions, target, W, M, V):
    L, D = x.shape
    me = 4 * lax.axis_index("x") + 2 * lax.axis_index("y") + lax.axis_index("c")
    n_cw = W["ffn_conv_w"].shape[-1]
    shards = {
        "mix_w_in": cast_bf16(W["mix_w_in"][0], name="mix_w_in_cast"),
        "conv_w": W["ffn_conv_w"].reshape(6, n_cw),
        "s5_glu_w": cast_bf16(W["s5_glu_w"][0], name="s5_glu_w_cast"),
        "mix_w_out": cast_bf16(W["mix_w_out"][0], name="mix_w_out_cast"),
        "ffn_w_in0": cast_bf16(W["ffn_w_in"][0], name="ffn_w_in0_cast"),
        "ffn_w_out0": cast_bf16(W["ffn_w_out"][0], name="ffn_w_out0_cast"),
        "att_w_qkv": cast_bf16(W["att_w_qkv"][0], name="att_w_qkv_cast"),
        "att_w_o": cast_bf16(W["att_w_o"][0], name="att_w_o_cast"),
        "ffn_w_in1": cast_bf16(W["ffn_w_in"][1], name="ffn_w_in1_cast"),
        "ffn_w_out1": cast_bf16(W["ffn_w_out"][1], name="ffn_w_out1_cast"),
    }
    gather_handles, token = copies_start(list(shards.values()), False, name="gather_start")
    gather_handle = dict(zip(shards, gather_handles))

    def gathered(key, after, cols):
        own, land = copies_wait(gather_handle[key], False, after, name=key + "_gwait")
        return (cols_from_shards if cols else rows_from_shards)(land, own, name=key + "_asm")

    conv_b = W["ffn_conv_b"].reshape(2, 1, -1)

    s5_params = (W["s5_A_re"][0], W["s5_A_im"][0], W["s5_log_dt"][0], W["s5_B_re"][0], W["s5_B_im"][0],
                 W["s5_C_re"][0], W["s5_C_im"][0])
    (a_re, a_im, wb_re, wb_im, wc_re, wc_im), s5_prep_vjp = jax.vjp(_s5_prepare, *s5_params)
    dvec = W["s5_D"].reshape(1, S5_WIDTH)
    glu_b = W["s5_glu_b"].reshape(1, S5_WIDTH)
    lb, lb_vjp = jax.vjp(_lower_bound, W["hgrn_gamma"])
    hg_norm = W["hgrn_norm"].reshape(1, -1)
    tabs = rope_tables(positions)

    hn0 = rms_fwd(x, W["norm_mix"][0], dep=token, name="l0_rms")
    w_mix_in = gathered("mix_w_in", hn0, True)
    proj = mm(hn0, w_mix_in, name="l0_proj")
    u_bf = cast_bf16(proj[:, :S5_WIDTH], name="l0_u_cast")
    bu_re = mm(u_bf, wb_re, name="s5_bu_re")
    bu_im = mm(u_bf, wb_im, name="s5_bu_im")
    xs_re, xs_im = s5_scan_fwd(a_re, a_im, bu_re, bu_im, name="s5_scan")
    y0 = mm(xs_im, wc_im, res=mm(xs_re, wc_re, name="s5_y_re"), name="s5_y_im")
    w_glu = gathered("s5_glu_w", y0, False)
    oa = s5_out_fwd(y0, proj, dvec, w_glu, glu_b, name="s5_out")
    ob, hg_states = hgrn_fwd(proj, lb, hg_norm, name="hgrn_fwd")
    cat = jnp.concatenate([oa, ob], axis=1)
    w_mix_out = gathered("mix_w_out", cat, False)
    h1 = mm(cat, w_mix_out, res=x, name="l0_mix_out")
    cw_own, cw_land = copies_wait(gather_handle["conv_w"], False, h1, name="conv_w_gwait")
    cw_all = lax.dynamic_update_slice(cw_land, cw_own[None], (me, 0, 0))
    conv_w = cw_all.transpose(1, 0, 2).reshape(2, 3, N_DEV * n_cw)
    w_ffn_in, w_ffn_out = [None, None], [None, None]
    h2, ffn0_saved, w_ffn_in[0], w_ffn_out[0] = _ffn_fwd(
        h1, W["norm_ffn"][0], lambda a: gathered("ffn_w_in0", a, True), conv_w[0], conv_b[0],
        lambda a: gathered("ffn_w_out0", a, False), "ffn0")

    hn2 = rms_fwd(h2, W["norm_mix"][1], name="l1_rms")
    w_qkv = gathered("att_w_qkv", hn2, True)
    qkv = mm(hn2, w_qkv, name="l1_qkv")
    qkv_r = rope_fwd(qkv, tabs, name="rope_fwd")
    att_in, att_o, att_l = [], [], []
    for g, d in enumerate(ATT_DILATIONS):
        qp = _to_branch_order(qkv_r[:, 512 * g:512 * (g + 1)], d)
        kp = _to_branch_order(qkv_r[:, 1536 + 512 * g:1536 + 512 * (g + 1)], d)
        vp = _to_branch_order(qkv_r[:, 3072 + 512 * g:3072 + 512 * (g + 1)], d)
        o_p, l_p = attn_fwd(qp, kp, vp, d, name=f"attn_fwd{g}")
        att_in.append((qp, kp, vp, l_p))
        att_o.append(_to_token_order(o_p, d))
        att_l.append(_to_token_order(l_p, d))
    o_att = merge_fwd(att_o, att_l, name="merge_fwd")
    w_o = gathered("att_w_o", o_att, True)
    h3 = mm(o_att, w_o, res=h2, name="l1_mix_out")
    h4, ffn1_saved, w_ffn_in[1], w_ffn_out[1] = _ffn_fwd(
        h3, W["norm_ffn"][1], lambda a: gathered("ffn_w_in1", a, True), conv_w[1], conv_b[1],
        lambda a: gathered("ffn_w_out1", a, False), "ffn1")

    exchanges = {}

    def send_grad(key, g, cols):
        if cols:
            parts = shards_from_cols(g, name=key + "_split")
        else:
            parts = cast_bf16(g, name=key + "_gcast").reshape(N_DEV, g.shape[0] // N_DEV, g.shape[1])
        (handle,), _ = copies_start([parts], True, name=key + "_xstart")
        exchanges[key] = handle

    loss, dh4, dg_final = final_loss(h4, W["norm_final"], target, name="final_loss")
    dh3, dg_ffn1, dcw1, dcb1 = _ffn_bwd(h3, W["norm_ffn"][1], w_ffn_in[1], conv_w[1], conv_b[1], w_ffn_out[1],
                                        ffn1_saved, dh4, "ffn1", lambda g: send_grad("ffn_w_in1", g, True),
                                        lambda g: send_grad("ffn_w_out1", g, False))
    d_oatt = mm(dh3, w_o, tb=True, name="l1_dmix")
    send_grad("att_w_o", mm(o_att, dh3, ta=True, name="l1_dwo"), True)
    mb = merge_bwd(att_o, att_l, d_oatt, name="merge_bwd")
    dq_t, dk_t, dv_t = [], [], []
    for g, d in enumerate(ATT_DILATIONS):
        qp, kp, vp, l_p = att_in[g]
        dq_p, dk_p, dv_p = attn_bwd(qp, kp, vp, l_p, _to_branch_order(mb[g], d), _to_branch_order(mb[3 + g], d), d,
                                    name=f"attn_bwd{g}")
        dq_t.append(_to_token_order(dq_p, d))
        dk_t.append(_to_token_order(dk_p, d))
        dv_t.append(_to_token_order(dv_p, d))
    d_qkv = rope_bwd(jnp.concatenate(dq_t, axis=1), jnp.concatenate(dk_t, axis=1), jnp.concatenate(dv_t, axis=1),
                     tabs, name="rope_bwd")
    send_grad("att_w_qkv", mm(hn2, d_qkv, ta=True, name="l1_dwqkv"), True)
    d_hn2 = mm(d_qkv, w_qkv, tb=True, name="l1_dhn")
    dh2, dg_mix1 = rms_bwd(h2, W["norm_mix"][1], d_hn2, dh3, name="l1_drms")

    dh1, dg_ffn0, dcw0, dcb0 = _ffn_bwd(h1, W["norm_ffn"][0], w_ffn_in[0], conv_w[0], conv_b[0], w_ffn_out[0],
                                        ffn0_saved, dh2, "ffn0", lambda g: send_grad("ffn_w_in0", g, True),
                                        lambda g: send_grad("ffn_w_out0", g, False))
    send_grad("mix_w_out", mm(cat, dh1, ta=True, name="l0_dwout"), False)
    dcat = mm(dh1, w_mix_out, tb=True, name="l0_dcat")
    d_hg, dlb, dhg_norm = hgrn_bwd(proj, lb, hg_norm, hg_states, dcat, name="hgrn_bwd")
    dy, du_d, z_bf, dzg, dglu_b, dD = s5_out_bwd(y0, proj, dvec, w_glu, glu_b, dcat, name="s5_dout")
    send_grad("s5_glu_w", mm(z_bf, dzg, ta=True, name="s5_dglu"), False)
    dxs_re = mm(dy, wc_re, tb=True, name="s5_dxs_re")
    dxs_im = mm(dy, wc_im, tb=True, name="s5_dxs_im")
    dwc_re = mm(xs_re, dy, ta=True, name="s5_dwc_re")
    dwc_im = mm(xs_im, dy, ta=True, name="s5_dwc_im")
    dbu_re, dbu_im, da_re, da_im = s5_scan_bwd(a_re, a_im, xs_re, xs_im, dxs_re, dxs_im, name="s5_dscan")
    du = mm(dbu_im, wb_im, tb=True, res=mm(dbu_re, wb_re, tb=True, res=du_d, name="s5_du_re"), out_dtype=BF16,
            name="s5_du_im")
    dwb_re = mm(u_bf, dbu_re, ta=True, name="s5_dwb_re")
    dwb_im = mm(u_bf, dbu_im, ta=True, name="s5_dwb_im")
    s5_small = s5_prep_vjp((da_re, da_im, dwb_re, dwb_im, dwc_re, dwc_im))
    d_proj = jnp.concatenate([du, d_hg], axis=1)
    send_grad("mix_w_in", mm(hn0, d_proj, ta=True, name="l0_dwin"), True)
    d_hn0 = mm(d_proj, w_mix_in, tb=True, name="l0_dhn")
    grad_x, dg_mix0 = rms_bwd(x, W["norm_mix"][0], d_hn0, dh1, name="l0_drms")
    (d_gamma,) = lb_vjp(dlb)
    out = {}

    dA_re, dA_im, dlog_dt, dB_re, dB_im, dC_re, dC_im = s5_small
    small_g = dict(norm_mix=jnp.concatenate([dg_mix0, dg_mix1], axis=0), norm_ffn=jnp.concatenate([dg_ffn0, dg_ffn1], axis=0),
                   norm_final=dg_final, s5_A_re=dA_re, s5_A_im=dA_im, s5_log_dt=dlog_dt, s5_B_re=dB_re, s5_B_im=dB_im,
                   s5_C_re=dC_re, s5_C_im=dC_im, s5_D=dD, s5_glu_b=dglu_b, hgrn_gamma=d_gamma, hgrn_norm=dhg_norm,
                   ffn_conv_b=jnp.concatenate([dcb0, dcb1], axis=0))
    conv_w_g = jnp.stack([dcw0, dcw1], axis=0)
    sizes = [math.prod(W[n].shape) for n in SMALL]
    n_conv = conv_w_g.size
    total = sum(sizes) + n_conv + 1
    rows = -(-total // PACK_COLS)
    rows = -(-rows // 8) * 8
    pad = rows * PACK_COLS - total

    def pack(vals, conv_part, last):
        flat = [v.reshape(-1).astype(F32) for v in vals] + [conv_part.reshape(-1), last.reshape(-1),
                                                            jnp.zeros((pad,), F32)]
        return jnp.concatenate(flat).reshape(rows, PACK_COLS)

    def conv_full(shard):
        full = jnp.zeros((2, 3, N_DEV * n_cw), F32)
        return lax.dynamic_update_slice(full, shard, (0, 0, me * n_cw))

    zero1 = jnp.zeros((1,), F32)
    g_pack = pack([small_g[n] for n in SMALL], conv_w_g, loss)
    w_pack = pack([W[n] for n in SMALL], conv_full(W["ffn_conv_w"]), zero1)
    m_pack = pack([M[n] for n in SMALL], conv_full(M["ffn_conv_w"]), zero1)
    v_pack = pack([V[n] for n in SMALL], conv_full(V["ffn_conv_w"]), zero1 + 1.0)
    (small_handle,), _ = copies_start([g_pack], False, name="small_xstart")

    def finish(key, w, m, v):
        own, recv = copies_wait(exchanges[key], True, grad_x, name=key + "_xwait")
        _, R, Cn = recv.shape
        return reduce_adamw(recv, own, True, me, w.reshape(R, Cn), m.reshape(R, Cn), v.reshape(R, Cn),
                            name=key + "_adamw")

    for name in ("ffn_w_out", "ffn_w_in"):
        per_layer = [finish(f"{name}{l}", W[name][l], M[name][l], V[name][l]) for l in (1, 0)][::-1]
        out[name] = tuple(jnp.stack([per_layer[0][k], per_layer[1][k]], axis=0) for k in range(4))
    for name in ("att_w_o", "att_w_qkv", "mix_w_out", "s5_glu_w", "mix_w_in"):
        out[name] = tuple(r.reshape(W[name].shape) for r in finish(name, W[name], M[name], V[name]))

    small_own, small_recv = copies_wait(small_handle, False, out["ffn_w_out"][0], name="small_xwait")
    res = reduce_adamw(small_recv, small_own, False, me, w_pack, m_pack, v_pack, name="small_adamw")
    flat = [r.reshape(-1) for r in res]
    off = 0
    for n, sz in zip(SMALL, sizes):
        out[n] = tuple(f[off:off + sz].reshape(W[n].shape) for f in flat)
        off += sz
    conv_res = [f[off:off + n_conv].reshape(2, 3, N_DEV * n_cw) for f in flat]
    out["ffn_conv_w"] = tuple(lax.dynamic_slice(c, (0, 0, me * n_cw), (2, 3, n_cw)) for c in conv_res)
    off += n_conv
    loss_total = flat[0][off]

    result = [loss_total, grad_x[None]]
    for k in range(4):
        result += [out[n][k] for n in ORDER]
    return tuple(result)
```

```python
import functools
import math

import jax
import jax.numpy as jnp
from jax import lax
from jax.experimental import pallas as pl
from jax.experimental.pallas import tpu as pltpu

F32 = jnp.float32
BF16 = jnp.bfloat16
MESH_ID = pl.DeviceIdType.MESH
N_DEV = 8
VMEM_LIMIT_BYTES = 56 * 1024 * 1024

NORM_EPS = 1e-6
S5_WIDTH, S5_GROUP, S5_GROUPS, S5_STATE = 512, 16, 32, 64
HG_HEADS, HG_DIM, HG_CHUNK = 4, 128, 64
ATT_E, ATT_HPG, ATT_BLOCK = 64, 8, 128
ATT_DILATIONS = (1, 4, 16)
ROT_DIM, ROPE_THETA = 16, 500000.0
D_FF = 2816
ADAM_LR, ADAM_B1, ADAM_B2, ADAM_EPS, ADAM_WD, ADAM_STEP = 0.001, 0.9, 0.999, 1e-08, 0.01, 10
NEG_BIG = -1e30


def _params(**kw):
    return pltpu.CompilerParams(vmem_limit_bytes=VMEM_LIMIT_BYTES, **kw)


def _pick(n, cands):
    for c in cands:
        if n % c == 0:
            return c
    return n


def _dot(a, b):
    return jnp.dot(a.astype(BF16), b.astype(BF16), preferred_element_type=F32)


def _dot_nt(a, b):
    return lax.dot_general(a.astype(BF16), b.astype(BF16), (((1,), (1,)), ((), ())), preferred_element_type=F32)


def _dot_tn(a, b):
    return lax.dot_general(a.astype(BF16), b.astype(BF16), (((0,), (0,)), ((), ())), preferred_element_type=F32)


def _dot_f32(a, b):
    return jnp.dot(a, b, preferred_element_type=F32, precision=lax.Precision.HIGHEST)


def _dot_f32_nt(a, b):
    return lax.dot_general(a, b, (((1,), (1,)), ((), ())), preferred_element_type=F32, precision=lax.Precision.HIGHEST)


def _dot_f32_tn(a, b):
    return lax.dot_general(a, b, (((0,), (0,)), ((), ())), preferred_element_type=F32, precision=lax.Precision.HIGHEST)


def _sigmoid(x):
    return 1.0 / (1.0 + jnp.exp(-x))


def mm(a, b, *, ta=False, tb=False, res=None, out_dtype=F32, dep=None, name):
    m, k = (a.shape[1], a.shape[0]) if ta else a.shape
    n = b.shape[0] if tb else b.shape[1]
    assert (b.shape[1] if tb else b.shape[0]) == k
    tm = _pick(m, (512, 256, 128))
    tn = _pick(n, (512, 256, 128))
    tk = _pick(k, (2048, 1408, 1024, 512, 256, 128))
    nk = k // tk
    has_res = res is not None
    deps = [] if dep is None else [dep]

    def body(*refs):
        a_ref, b_ref = refs[:2]
        r_ref = refs[2] if has_res else None
        o_ref, acc_ref = refs[-2:]
        kk = pl.program_id(2)
        dn = (((0 if ta else 1,), (1 if tb else 0,)), ((), ()))
        part = lax.dot_general(a_ref[...].astype(BF16), b_ref[...].astype(BF16), dn, preferred_element_type=F32)

        @pl.when(kk == 0)
        def _():
            acc_ref[...] = part

        @pl.when(kk > 0)
        def _():
            acc_ref[...] += part

        @pl.when(kk == nk - 1)
        def _():
            out = acc_ref[...]
            if has_res:
                out = out + r_ref[...].astype(F32)
            o_ref[...] = out.astype(o_ref.dtype)

    a_spec = pl.BlockSpec((tk, tm), lambda i, j, q: (q, i)) if ta else pl.BlockSpec((tm, tk), lambda i, j, q: (i, q))
    b_spec = pl.BlockSpec((tn, tk), lambda i, j, q: (j, q)) if tb else pl.BlockSpec((tk, tn), lambda i, j, q: (q, j))
    o_spec = pl.BlockSpec((tm, tn), lambda i, j, q: (i, j))
    in_specs = [a_spec, b_spec] + ([o_spec] if has_res else []) + [pl.BlockSpec((8, 128), lambda i, j, q: (0, 0))] * len(deps)
    args = (a, b) + ((res,) if has_res else ()) + tuple(deps)
    return pl.pallas_call(
        body, grid=(m // tm, n // tn, nk), in_specs=in_specs, out_specs=o_spec,
        out_shape=jax.ShapeDtypeStruct((m, n), out_dtype),
        scratch_shapes=[pltpu.VMEM((tm, tn), F32)],
        compiler_params=_params(dimension_semantics=("parallel", "parallel", "arbitrary")), name=name,
    )(*args)


def rms_fwd(x, g, *, dep=None, name):
    L, D = x.shape
    tr = _pick(L, (256, 128))

    def body(x_ref, g_ref, *rest):
        o_ref = rest[-1]
        xv = x_ref[...]
        r = lax.rsqrt(jnp.mean(xv * xv, axis=-1, keepdims=True) + NORM_EPS)
        o_ref[...] = (xv * r * g_ref[...]).astype(o_ref.dtype)

    row = pl.BlockSpec((tr, D), lambda i: (i, 0))
    vec = pl.BlockSpec((1, D), lambda i: (0, 0))
    deps = [] if dep is None else [dep]
    return pl.pallas_call(body, grid=(L // tr,), in_specs=[row, vec] + [pl.BlockSpec((8, 128), lambda i: (0, 0))] * len(deps),
                          out_specs=row, out_shape=jax.ShapeDtypeStruct((L, D), BF16), name=name)(
        x, g.reshape(1, D), *deps)


def rms_bwd(x, g, dy, dres, *, name):
    L, D = x.shape
    tr = _pick(L, (256, 128))

    def body(x_ref, g_ref, dy_ref, dres_ref, dx_ref, dg_ref):
        xv = x_ref[...]
        r = lax.rsqrt(jnp.mean(xv * xv, axis=-1, keepdims=True) + NORM_EPS)
        xh = xv * r
        dyv = dy_ref[...].astype(F32)

        @pl.when(pl.program_id(0) == 0)
        def _():
            dg_ref[...] = jnp.zeros_like(dg_ref)

        dg_ref[...] += jnp.sum(dyv * xh, axis=0, keepdims=True)
        dxh = dyv * g_ref[...]
        dx_ref[...] = dres_ref[...] + r * (dxh - xh * jnp.mean(dxh * xh, axis=-1, keepdims=True))

    row = pl.BlockSpec((tr, D), lambda i: (i, 0))
    vec = pl.BlockSpec((1, D), lambda i: (0, 0))
    return pl.pallas_call(body, grid=(L // tr,), in_specs=[row, vec, row, row], out_specs=[row, vec],
                          out_shape=[jax.ShapeDtypeStruct((L, D), F32), jax.ShapeDtypeStruct((1, D), F32)],
                          compiler_params=_params(dimension_semantics=("arbitrary",)), name=name)(
        x, g.reshape(1, D), dy, dres)


def final_loss(h, g, target, *, name):
    L, D = h.shape
    tr = _pick(L, (256, 128))

    def body(x_ref, g_ref, t_ref, loss_ref, dx_ref, dg_ref):
        xv = x_ref[...]
        gv = g_ref[...]
        r = lax.rsqrt(jnp.mean(xv * xv, axis=-1, keepdims=True) + NORM_EPS)
        xh = xv * r
        err = xh * gv - t_ref[...]

        @pl.when(pl.program_id(0) == 0)
        def _():
            dg_ref[...] = jnp.zeros_like(dg_ref)
            loss_ref[...] = jnp.zeros_like(loss_ref)

        loss_ref[...] += 0.5 * jnp.sum(jnp.mean(err * err, axis=-1, keepdims=True), axis=0, keepdims=True)
        dyv = err * (1.0 / D)
        dg_ref[...] += jnp.sum(dyv * xh, axis=0, keepdims=True)
        dxh = dyv * gv
        dx_ref[...] = r * (dxh - xh * jnp.mean(dxh * xh, axis=-1, keepdims=True))

    row = pl.BlockSpec((tr, D), lambda i: (i, 0))
    vec = pl.BlockSpec((1, D), lambda i: (0, 0))
    one = pl.BlockSpec((1, 1), lambda i: (0, 0))
    return pl.pallas_call(body, grid=(L // tr,), in_specs=[row, vec, row], out_specs=[one, row, vec],
                          out_shape=[jax.ShapeDtypeStruct((1, 1), F32), jax.ShapeDtypeStruct((L, D), F32),
                                     jax.ShapeDtypeStruct((1, D), F32)],
                          compiler_params=_params(dimension_semantics=("arbitrary",)), name=name)(
        h, g.reshape(1, D), target)


def s5_scan_fwd(a_re, a_im, bu_re, bu_im, *, name):
    L, P = bu_re.shape
    W = _pick(P, (512, 256, 128))

    def body(ar_ref, ai_ref, br_ref, bi_ref, xr_ref, xi_ref):
        ar, ai = ar_ref[...], ai_ref[...]

        def step(t8, carry):
            xr, xi = carry
            base = pl.multiple_of(t8 * 8, 8)
            br = br_ref[pl.ds(base, 8), :]
            bi = bi_ref[pl.ds(base, 8), :]
            out_r, out_i = [], []
            for j in range(8):
                nr = ar * xr - ai * xi + br[j:j + 1, :]
                ni = ar * xi + ai * xr + bi[j:j + 1, :]
                xr, xi = nr, ni
                out_r.append(nr)
                out_i.append(ni)
            xr_ref[pl.ds(base, 8), :] = jnp.concatenate(out_r, axis=0)
            xi_ref[pl.ds(base, 8), :] = jnp.concatenate(out_i, axis=0)
            return xr, xi

        zero = jnp.zeros((1, W), F32)
        lax.fori_loop(0, L // 8, step, (zero, zero))

    vec = pl.BlockSpec((1, W), lambda j: (0, j))
    col = pl.BlockSpec((L, W), lambda j: (0, j))
    return pl.pallas_call(body, grid=(P // W,), in_specs=[vec, vec, col, col], out_specs=[col, col],
                          out_shape=[jax.ShapeDtypeStruct((L, P), F32)] * 2,
                          compiler_params=_params(dimension_semantics=("parallel",)), name=name)(
        a_re, a_im, bu_re, bu_im)


def s5_scan_bwd(a_re, a_im, xs_re, xs_im, dx_re, dx_im, *, name):
    L, P = xs_re.shape
    W = _pick(P, (256, 128))

    def body(ar_ref, ai_ref, xr_ref, xi_ref, dr_ref, di_ref, lr_ref, li_ref, dar_ref, dai_ref):
        ar, ai = ar_ref[...], ai_ref[...]
        nblk = L // 8

        def step(s, carry):
            lr, li = carry
            base = pl.multiple_of((nblk - 1 - s) * 8, 8)
            dr = dr_ref[pl.ds(base, 8), :]
            di = di_ref[pl.ds(base, 8), :]
            out_r, out_i = [None] * 8, [None] * 8
            for j in range(7, -1, -1):
                nr = dr[j:j + 1, :] + ar * lr + ai * li
                ni = di[j:j + 1, :] - ai * lr + ar * li
                lr, li = nr, ni
                out_r[j], out_i[j] = nr, ni
            lr_ref[pl.ds(base, 8), :] = jnp.concatenate(out_r, axis=0)
            li_ref[pl.ds(base, 8), :] = jnp.concatenate(out_i, axis=0)
            return lr, li

        zero = jnp.zeros((1, W), F32)
        lax.fori_loop(0, nblk, step, (zero, zero))
        row = lax.broadcasted_iota(jnp.int32, (L, W), 0)
        xpr = jnp.where(row >= 1, pltpu.roll(xr_ref[...], 1, 0), 0.0)
        xpi = jnp.where(row >= 1, pltpu.roll(xi_ref[...], 1, 0), 0.0)
        lr, li = lr_ref[...], li_ref[...]
        dar_ref[...] = jnp.sum(lr * xpr + li * xpi, axis=0, keepdims=True)
        dai_ref[...] = jnp.sum(li * xpr - lr * xpi, axis=0, keepdims=True)

    vec = pl.BlockSpec((1, W), lambda j: (0, j))
    col = pl.BlockSpec((L, W), lambda j: (0, j))
    return pl.pallas_call(body, grid=(P // W,), in_specs=[vec, vec, col, col, col, col],
                          out_specs=[col, col, vec, vec],
                          out_shape=[jax.ShapeDtypeStruct((L, P), F32)] * 2 + [jax.ShapeDtypeStruct((1, P), F32)] * 2,
                          compiler_params=_params(dimension_semantics=("parallel",)), name=name)(
        a_re, a_im, xs_re, xs_im, dx_re, dx_im)


def _gelu(y):
    c = math.sqrt(2.0 / math.pi)
    t = jnp.tanh(c * (y + 0.044715 * y * y * y))
    return 0.5 * y * (1.0 + t), t


def s5_out_fwd(y0, proj, dvec, glu_w, glu_b, *, name):
    L, C = y0.shape
    tr = _pick(L, (256, 128))

    def body(y_ref, u_ref, d_ref, w_ref, b_ref, o_ref):
        z, _ = _gelu(y_ref[...] + d_ref[...] * u_ref[...])
        zg = _dot(z, w_ref[...]) + b_ref[...]
        o_ref[...] = (z * _sigmoid(zg)).astype(o_ref.dtype)

    row = pl.BlockSpec((tr, C), lambda i: (i, 0))
    vec = pl.BlockSpec((1, C), lambda i: (0, 0))
    wsp = pl.BlockSpec((C, C), lambda i: (0, 0))
    return pl.pallas_call(body, grid=(L // tr,), in_specs=[row, row, vec, wsp, vec], out_specs=row,
                          out_shape=jax.ShapeDtypeStruct((L, C), BF16), name=name)(
        y0, proj, dvec, glu_w, glu_b)


def s5_out_bwd(y0, proj, dvec, glu_w, glu_b, dcat, *, name):
    L, C = y0.shape
    tr = _pick(L, (256, 128))

    def body(y_ref, u_ref, d_ref, w_ref, b_ref, do_ref, dy_ref, dud_ref, z_ref, dzg_ref, db_ref, dd_ref):
        u = u_ref[...]
        y = y_ref[...] + d_ref[...] * u
        z, t = _gelu(y)
        zg = _dot(z, w_ref[...]) + b_ref[...]
        s = _sigmoid(zg)
        do = do_ref[...]
        dzg = do * z * s * (1.0 - s)
        dz = do * s + _dot_nt(dzg, w_ref[...])
        c = math.sqrt(2.0 / math.pi)
        dgelu = 0.5 * (1.0 + t) + 0.5 * y * (1.0 - t * t) * c * (1.0 + 3.0 * 0.044715 * y * y)
        dy = dz * dgelu

        @pl.when(pl.program_id(0) == 0)
        def _():
            db_ref[...] = jnp.zeros_like(db_ref)
            dd_ref[...] = jnp.zeros_like(dd_ref)

        db_ref[...] += jnp.sum(dzg, axis=0, keepdims=True)
        dd_ref[...] += jnp.sum(dy * u, axis=0, keepdims=True)
        dy_ref[...] = dy
        dud_ref[...] = dy * d_ref[...]
        z_ref[...] = z.astype(BF16)
        dzg_ref[...] = dzg.astype(BF16)

    row = pl.BlockSpec((tr, C), lambda i: (i, 0))
    vec = pl.BlockSpec((1, C), lambda i: (0, 0))
    wsp = pl.BlockSpec((C, C), lambda i: (0, 0))
    return pl.pallas_call(body, grid=(L // tr,), in_specs=[row, row, vec, wsp, vec, row],
                          out_specs=[row, row, row, row, vec, vec],
                          out_shape=[jax.ShapeDtypeStruct((L, C), F32), jax.ShapeDtypeStruct((L, C), F32),
                                     jax.ShapeDtypeStruct((L, C), BF16), jax.ShapeDtypeStruct((L, C), BF16),
                                     jax.ShapeDtypeStruct((1, C), F32), jax.ShapeDtypeStruct((1, C), F32)],
                          compiler_params=_params(dimension_semantics=("arbitrary",)), name=name)(
        y0, proj, dvec, glu_w, glu_b, dcat)


def _hg_gates(xq, xf, lb, tri):
    C = xq.shape[0]
    sq = _sigmoid(xq)
    q = xq * sq
    sg = _sigmoid(xf)
    f = lb + (1.0 - lb) * sg
    kk = 1.0 - f
    b = _dot_f32(tri, jnp.log(f))
    bm = b[C // 2 - 1:C // 2, :]
    bl = b[C - 1:C, :]
    eb = jnp.exp(b)
    return dict(sq=sq, q=q, sg=sg, f=f, kk=kk, b=b, bm=bm, bl=bl, eb=eb, ebl=jnp.exp(bl),
                qb=q * eb, eqm=jnp.exp(b - bm), ekm=jnp.exp(bm - b), ekl=jnp.exp(bl - b))


def _tri(C, lower):
    r = lax.broadcasted_iota(jnp.int32, (C, C), 0)
    c = lax.broadcasted_iota(jnp.int32, (C, C), 1)
    return (r >= c) if lower else (c >= r)


def hgrn_fwd(proj, lb, norm_g, *, name):
    L = proj.shape[0]
    C, H, K = HG_CHUNK, HG_HEADS, HG_DIM
    HK = H * K
    nc = L // C

    def body(q_ref, f_ref, i_ref, g_ref, lb_ref, ng_ref, o_ref, sall_ref, st_ref):
        @pl.when(pl.program_id(0) == 0)
        def _():
            st_ref[...] = jnp.zeros_like(st_ref)

        mask = _tri(C, True)
        tri = mask.astype(F32)
        for h in range(H):
            sl = slice(h * K, (h + 1) * K)
            v = i_ref[:, sl]
            st = st_ref[h]
            sall_ref[h] = st
            gt = _hg_gates(q_ref[:, sl], f_ref[:, sl], lb_ref[:, sl], tri)
            qt = gt["q"] * gt["eqm"]
            kt = gt["kk"] * gt["ekm"]
            kh = gt["kk"] * gt["ekl"]
            att = jnp.where(mask, _dot_nt(qt, kt), 0.0)
            o = _dot(att, v) + _dot_nt(gt["qb"], st)
            st_ref[h] = st * gt["ebl"] + _dot_tn(v, kh)
            r = lax.rsqrt(jnp.mean(o * o, axis=-1, keepdims=True) + NORM_EPS)
            xg = g_ref[:, sl]
            o_ref[:, sl] = (o * r * ng_ref[:, sl] * (xg * _sigmoid(xg))).astype(o_ref.dtype)

    def blk(cb):
        return pl.BlockSpec((C, HK), lambda i: (i, cb))

    vec = pl.BlockSpec((1, HK), lambda i: (0, 0))
    return pl.pallas_call(
        body, grid=(nc,), in_specs=[blk(1), blk(2), blk(3), blk(4), vec, vec],
        out_specs=[pl.BlockSpec((C, HK), lambda i: (i, 0)), pl.BlockSpec((None, H, K, K), lambda i: (i, 0, 0, 0))],
        out_shape=[jax.ShapeDtypeStruct((L, HK), BF16), jax.ShapeDtypeStruct((nc, H, K, K), F32)],
        scratch_shapes=[pltpu.VMEM((H, K, K), F32)],
        compiler_params=_params(dimension_semantics=("arbitrary",)), name=name,
    )(proj, proj, proj, proj, lb, norm_g)


def hgrn_bwd(proj, lb, norm_g, sall, dcat, *, name):
    L = proj.shape[0]
    C, H, K = HG_CHUNK, HG_HEADS, HG_DIM
    HK = H * K
    nc = L // C

    def body(q_ref, f_ref, i_ref, g_ref, lb_ref, ng_ref, sall_ref, do_ref, dx_ref, dlb_ref, dng_ref, dst_ref):
        @pl.when(pl.program_id(0) == 0)
        def _():
            dst_ref[...] = jnp.zeros_like(dst_ref)
            dlb_ref[...] = jnp.zeros_like(dlb_ref)
            dng_ref[...] = jnp.zeros_like(dng_ref)

        mask = _tri(C, True)
        tri = mask.astype(F32)
        tri_t = _tri(C, False).astype(F32)
        rowi = lax.broadcasted_iota(jnp.int32, (C, K), 0)
        for h in range(H):
            sl = slice(h * K, (h + 1) * K)
            xq, xf, v, xg = q_ref[:, sl], f_ref[:, sl], i_ref[:, sl], g_ref[:, sl]
            lb_h, ng = lb_ref[:, sl], ng_ref[:, sl]
            st = sall_ref[h]
            dst = dst_ref[h]
            gt = _hg_gates(xq, xf, lb_h, tri)
            q, kk, qb = gt["q"], gt["kk"], gt["qb"]
            qt = q * gt["eqm"]
            kt = kk * gt["ekm"]
            kh = kk * gt["ekl"]
            att = jnp.where(mask, _dot_f32_nt(qt, kt), 0.0)
            o = _dot_f32(att, v) + _dot_f32_nt(qb, st)
            r = lax.rsqrt(jnp.mean(o * o, axis=-1, keepdims=True) + NORM_EPS)
            oh = o * r
            sgg = _sigmoid(xg)
            silu_g = xg * sgg
            d_ob = do_ref[:, sl]
            d_on = d_ob * silu_g
            dxg = d_ob * (oh * ng) * (sgg * (1.0 + xg * (1.0 - sgg)))
            dng_ref[:, sl] += jnp.sum(d_on * oh, axis=0, keepdims=True)
            doh = d_on * ng
            do = r * (doh - oh * jnp.mean(doh * oh, axis=-1, keepdims=True))
            d_qb = _dot_f32(do, st)
            datt = jnp.where(mask, _dot_f32_nt(do, v), 0.0)
            dv = _dot_f32_tn(att, do) + _dot_f32_nt(kh, dst)
            d_qt = _dot_f32(datt, kt)
            d_kt = _dot_f32_tn(datt, qt)
            d_kh = _dot_f32(v, dst)
            d_bl = jnp.sum(dst * st, axis=0, keepdims=True) * gt["ebl"] + jnp.sum(d_kh * kh, axis=0, keepdims=True)
            dst_ref[h] = dst * gt["ebl"] + _dot_f32_tn(do, qb)
            dq = d_qt * gt["eqm"] + d_qb * gt["eb"]
            db = d_qt * qt + d_qb * qb - d_kt * kt - d_kh * kh
            db = db + jnp.where(rowi == C - 1, d_bl, 0.0)
            dkk = d_kt * gt["ekm"] + d_kh * gt["ekl"]
            dlg = _dot_f32(tri_t, db)
            df = dlg / gt["f"] - dkk
            sg = gt["sg"]
            dxf = df * (1.0 - lb_h) * sg * (1.0 - sg)
            dlb_ref[:, sl] += jnp.sum(df * (1.0 - sg), axis=0, keepdims=True)
            sq = gt["sq"]
            dxq = dq * (sq * (1.0 + xq * (1.0 - sq)))
            dx_ref[:, h * K:(h + 1) * K] = dxq.astype(dx_ref.dtype)
            dx_ref[:, HK + h * K:HK + (h + 1) * K] = dxf.astype(dx_ref.dtype)
            dx_ref[:, 2 * HK + h * K:2 * HK + (h + 1) * K] = dv.astype(dx_ref.dtype)
            dx_ref[:, 3 * HK + h * K:3 * HK + (h + 1) * K] = dxg.astype(dx_ref.dtype)

    def blk(cb):
        return pl.BlockSpec((C, HK), lambda i: (nc - 1 - i, cb))

    vec = pl.BlockSpec((1, HK), lambda i: (0, 0))
    return pl.pallas_call(
        body, grid=(nc,),
        in_specs=[blk(1), blk(2), blk(3), blk(4), vec, vec,
                  pl.BlockSpec((None, H, K, K), lambda i: (nc - 1 - i, 0, 0, 0)), blk(1)],
        out_specs=[pl.BlockSpec((C, 4 * HK), lambda i: (nc - 1 - i, 0)), vec, vec],
        out_shape=[jax.ShapeDtypeStruct((L, 4 * HK), BF16), jax.ShapeDtypeStruct((1, HK), F32),
                   jax.ShapeDtypeStruct((1, HK), F32)],
        scratch_shapes=[pltpu.VMEM((H, K, K), F32)],
        compiler_params=_params(dimension_semantics=("arbitrary",)), name=name,
    )(proj, proj, proj, proj, lb, norm_g, sall, dcat)


def _shift_down(x, k, row):
    return jnp.where(row >= k, pltpu.roll(x, k, 0), 0.0)


def _shift_up(x, k, row):
    n = x.shape[0]
    return jnp.where(row < n - k, pltpu.roll(x, n - k, 0), 0.0)


def convgate_fwd(hu, conv_w, conv_b, *, name):
    L, C2 = hu.shape
    C = C2 // 2
    tc = _pick(C, (256, 128))
    nb = C // tc

    def body(a_ref, b_ref, wa_ref, wb_ref, ba_ref, bb_ref, o_ref):
        row = lax.broadcasted_iota(jnp.int32, (L, tc), 0)

        def conv(x, w, bias):
            return w[2:3, :] * x + w[1:2, :] * _shift_down(x, 1, row) + w[0:1, :] * _shift_down(x, 2, row) + bias

        ca = conv(a_ref[...], wa_ref[...], ba_ref[...])
        cb = conv(b_ref[...], wb_ref[...], bb_ref[...])
        o_ref[...] = (ca * _sigmoid(ca) * cb).astype(o_ref.dtype)

    def col(off, rows):
        return pl.BlockSpec((rows, tc), lambda j: (0, j + off))

    return pl.pallas_call(
        body, grid=(nb,), in_specs=[col(0, L), col(nb, L), col(0, 3), col(nb, 3), col(0, 1), col(nb, 1)],
        out_specs=col(0, L), out_shape=jax.ShapeDtypeStruct((L, C), BF16),
        compiler_params=_params(dimension_semantics=("parallel",)), name=name,
    )(hu, hu, conv_w, conv_w, conv_b, conv_b)


def convgate_bwd(hu, conv_w, conv_b, dact, *, name):
    L, C2 = hu.shape
    C = C2 // 2
    tc = _pick(C, (256, 128))
    nb = C // tc

    def body(a_ref, b_ref, wa_ref, wb_ref, ba_ref, bb_ref, d_ref, dxa_ref, dxb_ref, dwa_ref, dwb_ref, dba_ref, dbb_ref):
        row = lax.broadcasted_iota(jnp.int32, (L, tc), 0)

        def conv(x, w, bias):
            x1 = _shift_down(x, 1, row)
            x2 = _shift_down(x, 2, row)
            return w[2:3, :] * x + w[1:2, :] * x1 + w[0:1, :] * x2 + bias, x1, x2

        xa, xb = a_ref[...], b_ref[...]
        wa, wb = wa_ref[...], wb_ref[...]
        ca, xa1, xa2 = conv(xa, wa, ba_ref[...])
        cb, xb1, xb2 = conv(xb, wb, bb_ref[...])
        d = d_ref[...]
        sa = _sigmoid(ca)
        dca = d * cb * (sa * (1.0 + ca * (1.0 - sa)))
        dcb = d * (ca * sa)

        def back(dc, w, x, x1, x2, dx_ref, dw_ref, db_ref):
            dx = w[2:3, :] * dc + w[1:2, :] * _shift_up(dc, 1, row) + w[0:1, :] * _shift_up(dc, 2, row)
            dx_ref[...] = dx.astype(dx_ref.dtype)
            dw_ref[...] = jnp.concatenate([jnp.sum(dc * x2, axis=0, keepdims=True),
                                           jnp.sum(dc * x1, axis=0, keepdims=True),
                                           jnp.sum(dc * x, axis=0, keepdims=True)], axis=0)
            db_ref[...] = jnp.sum(dc, axis=0, keepdims=True)

        back(dca, wa, xa, xa1, xa2, dxa_ref, dwa_ref, dba_ref)
        back(dcb, wb, xb, xb1, xb2, dxb_ref, dwb_ref, dbb_ref)

    def col(off, rows):
        return pl.BlockSpec((rows, tc), lambda j: (0, j + off))

    outs = pl.pallas_call(
        body, grid=(nb,),
        in_specs=[col(0, L), col(nb, L), col(0, 3), col(nb, 3), col(0, 1), col(nb, 1), col(0, L)],
        out_specs=[col(0, L), col(0, L), col(0, 3), col(0, 3), col(0, 1), col(0, 1)],
        out_shape=[jax.ShapeDtypeStruct((L, C), BF16)] * 2 + [jax.ShapeDtypeStruct((3, C), F32)] * 2
        + [jax.ShapeDtypeStruct((1, C), F32)] * 2,
        compiler_params=_params(dimension_semantics=("parallel",)), name=name,
    )(hu, hu, conv_w, conv_w, conv_b, conv_b, dact)
    dxa, dxb, dwa, dwb, dba, dbb = outs
    return (jnp.concatenate([dxa, dxb], axis=1), jnp.concatenate([dwa, dwb], axis=1),
            jnp.concatenate([dba, dbb], axis=1))


def _to_branch_order(t, d):
    L, W = t.shape
    return t if d == 1 else t.reshape(L // d, d, W).transpose(1, 0, 2).reshape(L, W)


def _to_token_order(t, d):
    L, W = t.shape
    return t if d == 1 else t.reshape(d, L // d, W).transpose(1, 0, 2).reshape(L, W)


def rope_tables(positions):
    half = ROT_DIM // 2
    inv_freq = ROPE_THETA ** (-jnp.arange(half, dtype=F32) * 2.0 / ROT_DIM)
    ang = positions.astype(F32)[:, None] * inv_freq
    cos, sin = jnp.cos(ang), jnp.sin(ang)
    L = positions.shape[0]
    one = jnp.ones((L, ATT_E - ROT_DIM), F32)
    zero = jnp.zeros((L, ATT_E - ROT_DIM), F32)
    zh = jnp.zeros((L, half), F32)
    tc = jnp.concatenate([cos, cos, one], axis=1)
    ts1 = jnp.concatenate([zh, sin, zero], axis=1)
    ts2 = jnp.concatenate([-sin, zh, zero], axis=1)
    return tuple(jnp.concatenate([t, t], axis=1) for t in (tc, ts1, ts2))


def rope_fwd(qkv, tabs, *, name):
    L = qkv.shape[0]
    W = 512
    tr = _pick(L, (256, 128))
    nq = 1536 // W
    scale = ATT_E ** -0.5

    def body(x_ref, c_ref, s1_ref, s2_ref, o_ref):
        j = pl.program_id(1)
        x = x_ref[...]
        c = jnp.concatenate([c_ref[...]] * 4, axis=1)
        s1 = jnp.concatenate([s1_ref[...]] * 4, axis=1)
        s2 = jnp.concatenate([s2_ref[...]] * 4, axis=1)
        rot = x * c + pltpu.roll(x, 8, 1) * s1 + pltpu.roll(x, W - 8, 1) * s2
        mult = jnp.where(j < nq, scale, 1.0)
        o_ref[...] = jnp.where(j < 2 * nq, rot * mult, x).astype(o_ref.dtype)

    blk = pl.BlockSpec((tr, W), lambda i, j: (i, j))
    tab = pl.BlockSpec((tr, 128), lambda i, j: (i, 0))
    return pl.pallas_call(body, grid=(L // tr, 3 * nq), in_specs=[blk, tab, tab, tab], out_specs=blk,
                          out_shape=jax.ShapeDtypeStruct((L, 3 * 1536), BF16),
                          compiler_params=_params(dimension_semantics=("parallel", "parallel")), name=name)(
        qkv, *tabs)


def rope_bwd(dq, dk, dv, tabs, *, name):
    L = dq.shape[0]
    W = 512
    tr = _pick(L, (256, 128))
    nq = 1536 // W
    scale = ATT_E ** -0.5

    def body(dq_ref, dk_ref, dv_ref, c_ref, s1_ref, s2_ref, o_ref):
        j = pl.program_id(1)
        c = jnp.concatenate([c_ref[...]] * 4, axis=1)
        s1 = jnp.concatenate([s1_ref[...]] * 4, axis=1)
        s2 = jnp.concatenate([s2_ref[...]] * 4, axis=1)

        def unrot(dy):
            return dy * c + pltpu.roll(dy * s1, W - 8, 1) + pltpu.roll(dy * s2, 8, 1)

        @pl.when(j < nq)
        def _():
            o_ref[...] = (unrot(dq_ref[...]) * scale).astype(o_ref.dtype)

        @pl.when((j >= nq) & (j < 2 * nq))
        def _():
            o_ref[...] = unrot(dk_ref[...]).astype(o_ref.dtype)

        @pl.when(j >= 2 * nq)
        def _():
            o_ref[...] = dv_ref[...].astype(o_ref.dtype)

    def src(k):
        return pl.BlockSpec((tr, W), lambda i, j: (i, jnp.clip(j - k * nq, 0, nq - 1)))

    tab = pl.BlockSpec((tr, 128), lambda i, j: (i, 0))
    return pl.pallas_call(body, grid=(L // tr, 3 * nq), in_specs=[src(0), src(1), src(2), tab, tab, tab],
                          out_specs=pl.BlockSpec((tr, W), lambda i, j: (i, j)),
                          out_shape=jax.ShapeDtypeStruct((L, 3 * 1536), BF16),
                          compiler_params=_params(dimension_semantics=("parallel", "arbitrary")), name=name)(
        dq, dk, dv, *tabs)


def _att_masks(has_prev):
    qi = lax.broadcasted_iota(jnp.int32, (ATT_BLOCK, ATT_BLOCK), 0)
    kj = lax.broadcasted_iota(jnp.int32, (ATT_BLOCK, ATT_BLOCK), 1)
    return qi >= kj, (kj >= qi) & has_prev


def attn_fwd(qp, kp, vp, d, *, name):
    L, W = qp.shape
    B, E = ATT_BLOCK, ATT_E
    nblk = L // B
    nb = nblk // d

    def body(q_ref, kc_ref, kp_ref, vc_ref, vp_ref, o_ref, l_ref):
        has_prev = (pl.program_id(0) % nb) > 0
        mc, mp = _att_masks(has_prev)
        for h in range(ATT_HPG):
            sl = slice(h * E, (h + 1) * E)
            q = q_ref[:, sl]
            sc = jnp.where(mc, _dot_nt(q, kc_ref[:, sl]), NEG_BIG)
            sp = jnp.where(mp, _dot_nt(q, kp_ref[:, sl]), NEG_BIG)
            m = jnp.maximum(jnp.max(sc, axis=-1, keepdims=True), jnp.max(sp, axis=-1, keepdims=True))
            pc = jnp.exp(sc - m)
            pp = jnp.exp(sp - m)
            den = jnp.sum(pc, axis=-1, keepdims=True) + jnp.sum(pp, axis=-1, keepdims=True)
            o = (_dot(pc, vc_ref[:, sl]) + _dot(pp, vp_ref[:, sl])) / den
            o_ref[:, sl] = o
            l_ref[:, sl] = jnp.broadcast_to(m + jnp.log(den), (B, E))

    cur = pl.BlockSpec((B, W), lambda j: (j, 0))
    prev = pl.BlockSpec((B, W), lambda j: (jnp.maximum(j - 1, 0), 0))
    return pl.pallas_call(body, grid=(nblk,), in_specs=[cur, cur, prev, cur, prev], out_specs=[cur, cur],
                          out_shape=[jax.ShapeDtypeStruct((L, W), F32)] * 2,
                          compiler_params=_params(dimension_semantics=("parallel",)), name=name)(
        qp, kp, kp, vp, vp)


def attn_bwd(qp, kp, vp, lse, do, dl, d, *, name):
    L, W = qp.shape
    B, E = ATT_BLOCK, ATT_E
    nblk = L // B
    nb = nblk // d

    def body(q_ref, kc_ref, kp_ref, vc_ref, vp_ref, l_ref, do_ref, dl_ref, dq_ref, dk_ref, dv_ref, tkc, tkp, tvc, tvp):
        j = pl.program_id(0)

        @pl.when(j == 0)
        def _():
            dk_ref[...] = jnp.zeros_like(dk_ref)
            dv_ref[...] = jnp.zeros_like(dv_ref)

        has_prev = (j % nb) > 0
        mc, mp = _att_masks(has_prev)
        for h in range(ATT_HPG):
            sl = slice(h * E, (h + 1) * E)
            q = q_ref[:, sl]
            kc, kpv, vc, vpv = kc_ref[:, sl], kp_ref[:, sl], vc_ref[:, sl], vp_ref[:, sl]
            lse_h = l_ref[:, h * E:h * E + 1]
            dl_h = dl_ref[:, h * E:h * E + 1]
            doh = do_ref[:, sl]
            pc = jnp.where(mc, jnp.exp(_dot_nt(q, kc) - lse_h), 0.0)
            pp = jnp.where(mp, jnp.exp(_dot_nt(q, kpv) - lse_h), 0.0)
            dsc = pc * (_dot_nt(doh, vc) - dl_h)
            dsp = pp * (_dot_nt(doh, vpv) - dl_h)
            dq_ref[:, sl] = _dot(dsc, kc) + _dot(dsp, kpv)
            tkc[:, sl] = _dot_tn(dsc, q)
            tkp[:, sl] = _dot_tn(dsp, q)
            tvc[:, sl] = _dot_tn(pc, doh)
            tvp[:, sl] = _dot_tn(pp, doh)
        cur = pl.multiple_of(j * B, B)
        prv = pl.multiple_of(jnp.maximum(j - 1, 0) * B, B)
        dk_ref[pl.ds(cur, B), :] += tkc[...]
        dv_ref[pl.ds(cur, B), :] += tvc[...]
        dk_ref[pl.ds(prv, B), :] += tkp[...]
        dv_ref[pl.ds(prv, B), :] += tvp[...]

    cur = pl.BlockSpec((B, W), lambda j: (j, 0))
    prev = pl.BlockSpec((B, W), lambda j: (jnp.maximum(j - 1, 0), 0))
    full = pl.BlockSpec((L, W), lambda j: (0, 0))
    return pl.pallas_call(body, grid=(nblk,), in_specs=[cur, cur, prev, cur, prev, cur, cur, cur],
                          out_specs=[cur, full, full], out_shape=[jax.ShapeDtypeStruct((L, W), F32)] * 3,
                          scratch_shapes=[pltpu.VMEM((B, W), F32)] * 4,
                          compiler_params=_params(dimension_semantics=("arbitrary",)), name=name)(
        qp, kp, kp, vp, vp, lse, do, dl)


def _merge_alpha(l_refs):
    ls = [r[...] for r in l_refs]
    m = jnp.maximum(jnp.maximum(ls[0], ls[1]), ls[2])
    es = [jnp.exp(l - m) for l in ls]
    den = es[0] + es[1] + es[2]
    return [e / den for e in es]


def merge_fwd(os_, ls_, *, name):
    L, W = os_[0].shape
    tr = _pick(L, (256, 128))

    def body(o0, o1, o2, l0, l1, l2, out_ref):
        al = _merge_alpha((l0, l1, l2))
        out_ref[...] = (al[0] * o0[...] + al[1] * o1[...] + al[2] * o2[...]).astype(out_ref.dtype)

    row = pl.BlockSpec((tr, W), lambda i: (i, 0))
    return pl.pallas_call(body, grid=(L // tr,), in_specs=[row] * 6, out_specs=row,
                          out_shape=jax.ShapeDtypeStruct((L, W), BF16), name=name)(*os_, *ls_)


def merge_bwd(os_, ls_, do, *, name):
    L, W = do.shape
    tr = _pick(L, (256, 128))

    def body(o0, o1, o2, l0, l1, l2, do_ref, d0, d1, d2, e0, e1, e2):
        al = _merge_alpha((l0, l1, l2))
        dov = do_ref[...]
        r = lax.broadcasted_iota(jnp.int32, (W, W), 0) // ATT_E
        c = lax.broadcasted_iota(jnp.int32, (W, W), 1) // ATT_E
        ones_blk = (r == c).astype(F32)
        t = jnp.zeros_like(dov)
        for a, o in zip(al, (o0, o1, o2)):
            t = t + a * _dot_f32(dov * o[...], ones_blk)
        for a, d_ref, e_ref in zip(al, (d0, d1, d2), (e0, e1, e2)):
            d_ref[...] = a * dov
            e_ref[...] = a * t

    row = pl.BlockSpec((tr, W), lambda i: (i, 0))
    return pl.pallas_call(body, grid=(L // tr,), in_specs=[row] * 7, out_specs=[row] * 6,
                          out_shape=[jax.ShapeDtypeStruct((L, W), F32)] * 6, name=name)(*os_, *ls_, do)


def _me_and_peers():
    x, y, c = lax.axis_index("x"), lax.axis_index("y"), lax.axis_index("c")
    peers = []
    for k in range(1, N_DEV):
        px = 1 - x if k & 4 else x
        py = 1 - y if k & 2 else y
        pc = 1 - c if k & 1 else c
        peers.append((px, py, pc))
    return (x, y, c), peers


def _index(dev):
    return 4 * dev[0] + 2 * dev[1] + dev[2]


def _hbm(a):
    return pltpu.with_memory_space_constraint(a, pltpu.HBM)


HBM_SPEC = pl.BlockSpec(memory_space=pltpu.HBM)
SEM_SPEC = pl.BlockSpec(memory_space=pltpu.SEMAPHORE)
DATAFLOW = pltpu.SideEffectType.DATAFLOW_SIDE_EFFECTING


def _remote(src_ref, land_ref, slotted, me, peer, src_is_mine, send_sem, recv_sem, k):
    sender, receiver = (me, peer) if src_is_mine else (peer, me)
    src = src_ref.at[_index(receiver)] if slotted else src_ref
    return pltpu.make_async_remote_copy(src_ref=src, dst_ref=land_ref.at[_index(sender)], send_sem=send_sem.at[k],
                                        recv_sem=recv_sem.at[k], device_id=peer, device_id_type=MESH_ID)


def copies_start(arrays, slotted, *, name):
    n = len(arrays)
    lands = [lax.empty(a.shape if slotted else (N_DEV,) + a.shape, a.dtype) for a in arrays]

    def body(*refs):
        x_refs, land_refs = refs[:n], refs[n:2 * n]
        send, recv = refs[2 * n:3 * n], refs[3 * n:4 * n]
        token = refs[-1]
        me, peers = _me_and_peers()
        for w in range(n):
            for k, peer in enumerate(peers):
                _remote(x_refs[w], land_refs[w], slotted, me, peer, True, send[w], recv[w], k).start()
        token[...] = jnp.zeros_like(token)

    sem = pltpu.SemaphoreType.DMA((N_DEV - 1,))
    out_shape = ([sem] * (2 * n) + [pltpu.HBM(a.shape, a.dtype) for a in arrays]
                 + [pltpu.HBM(l.shape, l.dtype) for l in lands] + [jax.ShapeDtypeStruct((8, 128), F32)])
    outs = pl.pallas_call(
        body, name=name, out_shape=out_shape, in_specs=[HBM_SPEC] * (2 * n),
        out_specs=[SEM_SPEC] * (2 * n) + [HBM_SPEC] * (2 * n) + [pl.BlockSpec(memory_space=pltpu.VMEM)],
        input_output_aliases={i: 2 * n + i for i in range(2 * n)},
        compiler_params=pltpu.CompilerParams(has_side_effects=DATAFLOW),
    )(*[_hbm(a) for a in arrays], *[_hbm(l) for l in lands])
    handles = [(outs[w], outs[n + w], outs[2 * n + w], outs[3 * n + w]) for w in range(n)]
    return handles, outs[-1]


def copies_wait(handle, slotted, after, *, name):
    send_sem, recv_sem, x_thru, land_thru = handle

    def body(x_ref, land_ref, send_ref, recv_ref, after_ref, x_out, land_out):
        me, peers = _me_and_peers()
        for k, peer in enumerate(peers):
            _remote(x_ref, land_ref, slotted, me, peer, True, send_ref, recv_ref, k).wait_send()
        for k, peer in enumerate(peers):
            _remote(x_ref, land_ref, slotted, me, peer, False, send_ref, recv_ref, k).wait_recv()

    return pl.pallas_call(
        body, name=name, out_shape=(pltpu.HBM(x_thru.shape, x_thru.dtype), pltpu.HBM(land_thru.shape, land_thru.dtype)),
        in_specs=(HBM_SPEC, HBM_SPEC, SEM_SPEC, SEM_SPEC, pl.BlockSpec(memory_space=pl.ANY)),
        out_specs=(HBM_SPEC, HBM_SPEC), input_output_aliases={0: 0, 1: 1},
        compiler_params=pltpu.CompilerParams(has_side_effects=DATAFLOW),
    )(x_thru, land_thru, send_sem, recv_sem, after)


def cast_bf16(x, *, name):
    R, C = x.shape
    tr = _pick(R, (512, 256, 128, 64))

    def body(x_ref, o_ref):
        o_ref[...] = x_ref[...].astype(BF16)

    row = pl.BlockSpec((tr, C), lambda i: (i, 0))
    return pl.pallas_call(body, grid=(R // tr,), in_specs=[row], out_specs=row,
                          out_shape=jax.ShapeDtypeStruct((R, C), BF16), name=name)(x)


def _my_index():
    return 4 * lax.axis_index("x") + 2 * lax.axis_index("y") + lax.axis_index("c")


def cols_from_shards(g, own, *, name):
    _, K, n = g.shape
    tk = _pick(K, (256, 128))

    def body(g_ref, own_ref, o_ref):
        me = _my_index()
        for i in range(N_DEV):
            @pl.when(me == i)
            def _():
                o_ref[:, i * n:(i + 1) * n] = own_ref[...]

            @pl.when(me != i)
            def _():
                o_ref[:, i * n:(i + 1) * n] = g_ref[i]

    return pl.pallas_call(body, grid=(K // tk,),
                          in_specs=[pl.BlockSpec((N_DEV, tk, n), lambda i: (0, i, 0)), pl.BlockSpec((tk, n), lambda i: (i, 0))],
                          out_specs=pl.BlockSpec((tk, N_DEV * n), lambda i: (i, 0)),
                          out_shape=jax.ShapeDtypeStruct((K, N_DEV * n), g.dtype), name=name)(g, own)


def rows_from_shards(g, own, *, name):
    _, k, N = g.shape

    def body(g_ref, own_ref, o_ref):
        mine = _my_index() == pl.program_id(0)

        @pl.when(mine)
        def _():
            o_ref[...] = own_ref[...]

        @pl.when(jnp.logical_not(mine))
        def _():
            o_ref[...] = g_ref[...]

    return pl.pallas_call(body, grid=(N_DEV,),
                          in_specs=[pl.BlockSpec((None, k, N), lambda i: (i, 0, 0)), pl.BlockSpec((k, N), lambda i: (0, 0))],
                          out_specs=pl.BlockSpec((k, N), lambda i: (i, 0)),
                          out_shape=jax.ShapeDtypeStruct((N_DEV * k, N), g.dtype), name=name)(g, own)


def shards_from_cols(w, *, name):
    K, N = w.shape
    n = N // N_DEV
    tk = _pick(K, (256, 128))

    def body(w_ref, o_ref):
        for i in range(N_DEV):
            o_ref[i] = w_ref[:, i * n:(i + 1) * n].astype(o_ref.dtype)

    return pl.pallas_call(body, grid=(K // tk,), in_specs=[pl.BlockSpec((tk, N), lambda i: (i, 0))],
                          out_specs=pl.BlockSpec((N_DEV, tk, n), lambda i: (0, i, 0)),
                          out_shape=jax.ShapeDtypeStruct((N_DEV, K, n), BF16), name=name)(w)


def _adamw(w, g, m, v):
    m = ADAM_B1 * m + (1.0 - ADAM_B1) * g
    v = ADAM_B2 * v + (1.0 - ADAM_B2) * (g * g)
    m_hat = m / (1.0 - ADAM_B1 ** ADAM_STEP)
    v_hat = v / (1.0 - ADAM_B2 ** ADAM_STEP)
    delta = -ADAM_LR * (m_hat / (jnp.sqrt(v_hat) + ADAM_EPS) + ADAM_WD * w)
    return delta, m, v


def reduce_adamw(recv, own, own_slotted, me, w, m, v, *, name):
    _, R, C = recv.shape
    tr = _pick(R, (256, 128, 64, 32, 16, 8))

    def body(me_ref, r_ref, own_ref, w_ref, m_ref, v_ref, g_ref, d_ref, nm_ref, nv_ref):
        mine = me_ref[0]
        g = None
        for i in range(N_DEV):
            part = jnp.where(mine == i, own_ref[...], r_ref[i]).astype(F32)
            g = part if g is None else g + part
        delta, nm, nv = _adamw(w_ref[...], g, m_ref[...], v_ref[...])
        g_ref[...] = g
        d_ref[...] = delta
        nm_ref[...] = nm
        nv_ref[...] = nv

    row = pl.BlockSpec((tr, C), lambda i, me_ref: (i, 0))
    own_spec = pl.BlockSpec((None, tr, C), lambda i, me_ref: (me_ref[0], i, 0)) if own_slotted else row
    grid_spec = pltpu.PrefetchScalarGridSpec(
        num_scalar_prefetch=1, grid=(R // tr,),
        in_specs=[pl.BlockSpec((N_DEV, tr, C), lambda i, me_ref: (0, i, 0)), own_spec, row, row, row],
        out_specs=[row] * 4)
    return pl.pallas_call(body, grid_spec=grid_spec, out_shape=[jax.ShapeDtypeStruct((R, C), F32)] * 4,
                          compiler_params=_params(dimension_semantics=("parallel",)), name=name)(
        me.reshape(1).astype(jnp.int32), recv, own, w, m, v)


def _s5_prepare(A_re, A_im, log_dt, B_re, B_im, C_re, C_im):
    G, P, Cg = S5_GROUPS, S5_STATE, S5_GROUP
    dt = jnp.exp(log_dt)[:, None]
    mag = jnp.exp(A_re * dt)
    ab_re = mag * jnp.cos(A_im * dt)
    ab_im = mag * jnp.sin(A_im * dt)
    den = A_re * A_re + A_im * A_im
    nr, ni = ab_re - 1.0, ab_im
    c_re = (nr * A_re + ni * A_im) / den
    c_im = (ni * A_re - nr * A_im) / den
    Bb_re = c_re[..., None] * B_re - c_im[..., None] * B_im
    Bb_im = c_re[..., None] * B_im + c_im[..., None] * B_re
    eye = jnp.eye(G, dtype=F32)

    def dense_in(b):
        return jnp.einsum('gpc,gh->gchp', b, eye).reshape(G * Cg, G * P)

    def dense_out(c):
        return jnp.einsum('gcp,gh->gphc', c, eye).reshape(G * P, G * Cg)

    return (ab_re.reshape(1, G * P), ab_im.reshape(1, G * P), dense_in(Bb_re), dense_in(Bb_im),
            dense_out(C_re), dense_out(-C_im))


def _lower_bound(gamma):
    return jnp.cumsum(jax.nn.softmax(gamma, axis=0), axis=0)[0:1]


def _ffn_fwd(h, g_norm, get_w_in, conv_w, conv_b, get_w_out, tag):
    hn = rms_fwd(h, g_norm, name=tag + "_rms")
    w_in = get_w_in(hn)
    hu = mm(hn, w_in, name=tag + "_in")
    act = convgate_fwd(hu, conv_w, conv_b, name=tag + "_gate")
    w_out = get_w_out(act)
    h_out = mm(act, w_out, res=h, name=tag + "_out")
    return h_out, (hn, hu, act), w_in, w_out


def _ffn_bwd(h, g_norm, w_in, conv_w, conv_b, w_out, saved, dh, tag, send_dw_in, send_dw_out):
    hn, hu, act = saved
    sent = send_dw_out(mm(act, dh, ta=True, name=tag + "_dwout"))
    dact = mm(dh, w_out, tb=True, dep=sent, name=tag + "_dact")
    dhu, dconv_w, dconv_b = convgate_bwd(hu, conv_w, conv_b, dact, name=tag + "_dgate")
    sent = send_dw_in(mm(hn, dhu, ta=True, name=tag + "_dwin"))
    dhn = mm(dhu, w_in, tb=True, dep=sent, name=tag + "_dhn")
    dh_in, dg = rms_bwd(h, g_norm, dhn, dh, name=tag + "_drms")
    return dh_in, dg, dconv_w, dconv_b


def kernel(x, positions, norm_mix, norm_ffn, norm_final, mix_w_in, mix_w_out, s5_A_re, s5_A_im, s5_log_dt, s5_B_re, s5_B_im, s5_C_re, s5_C_im, s5_D, s5_glu_w, s5_glu_b, hgrn_gamma, hgrn_norm, att_w_qkv, att_w_o, ffn_w_in, ffn_conv_w, ffn_conv_b, ffn_w_out, loss_target, m_norm_mix, m_norm_ffn, m_norm_final, m_mix_w_in, m_mix_w_out, m_s5_A_re, m_s5_A_im, m_s5_log_dt, m_s5_B_re, m_s5_B_im, m_s5_C_re, m_s5_C_im, m_s5_D, m_s5_glu_w, m_s5_glu_b, m_hgrn_gamma, m_hgrn_norm, m_att_w_qkv, m_att_w_o, m_ffn_w_in, m_ffn_conv_w, m_ffn_conv_b, m_ffn_w_out, v_norm_mix, v_norm_ffn, v_norm_final, v_mix_w_in, v_mix_w_out, v_s5_A_re, v_s5_A_im, v_s5_log_dt, v_s5_B_re, v_s5_B_im, v_s5_C_re, v_s5_C_im, v_s5_D, v_s5_glu_w, v_s5_glu_b, v_hgrn_gamma, v_hgrn_norm, v_att_w_qkv, v_att_w_o, v_ffn_w_in, v_ffn_conv_w, v_ffn_conv_b, v_ffn_w_out):
    W = dict(norm_mix=norm_mix, norm_ffn=norm_ffn, norm_final=norm_final, mix_w_in=mix_w_in, mix_w_out=mix_w_out,
             s5_A_re=s5_A_re, s5_A_im=s5_A_im, s5_log_dt=s5_log_dt, s5_B_re=s5_B_re, s5_B_im=s5_B_im,
             s5_C_re=s5_C_re, s5_C_im=s5_C_im, s5_D=s5_D, s5_glu_w=s5_glu_w, s5_glu_b=s5_glu_b,
             hgrn_gamma=hgrn_gamma, hgrn_norm=hgrn_norm, att_w_qkv=att_w_qkv, att_w_o=att_w_o, ffn_w_in=ffn_w_in,
             ffn_conv_w=ffn_conv_w, ffn_conv_b=ffn_conv_b, ffn_w_out=ffn_w_out)
    M = dict(norm_mix=m_norm_mix, norm_ffn=m_norm_ffn, norm_final=m_norm_final, mix_w_in=m_mix_w_in,
             mix_w_out=m_mix_w_out, s5_A_re=m_s5_A_re, s5_A_im=m_s5_A_im, s5_log_dt=m_s5_log_dt, s5_B_re=m_s5_B_re,
             s5_B_im=m_s5_B_im, s5_C_re=m_s5_C_re, s5_C_im=m_s5_C_im, s5_D=m_s5_D, s5_glu_w=m_s5_glu_w,
             s5_glu_b=m_s5_glu_b, hgrn_gamma=m_hgrn_gamma, hgrn_norm=m_hgrn_norm, att_w_qkv=m_att_w_qkv,
             att_w_o=m_att_w_o, ffn_w_in=m_ffn_w_in, ffn_conv_w=m_ffn_conv_w, ffn_conv_b=m_ffn_conv_b,
             ffn_w_out=m_ffn_w_out)
    V = dict(norm_mix=v_norm_mix, norm_ffn=v_norm_ffn, norm_final=v_norm_final, mix_w_in=v_mix_w_in,
             mix_w_out=v_mix_w_out, s5_A_re=v_s5_A_re, s5_A_im=v_s5_A_im, s5_log_dt=v_s5_log_dt, s5_B_re=v_s5_B_re,
             s5_B_im=v_s5_B_im, s5_C_re=v_s5_C_re, s5_C_im=v_s5_C_im, s5_D=v_s5_D, s5_glu_w=v_s5_glu_w,
             s5_glu_b=v_s5_glu_b, hgrn_gamma=v_hgrn_gamma, hgrn_norm=v_hgrn_norm, att_w_qkv=v_att_w_qkv,
             att_w_o=v_att_w_o, ffn_w_in=v_ffn_w_in, ffn_conv_w=v_ffn_conv_w, ffn_conv_b=v_ffn_conv_b,
             ffn_w_out=v_ffn_w_out)
    return _step(x[0], positions[0], loss_target[0], W, M, V)


BIG = ("mix_w_in", "mix_w_out", "s5_glu_w", "att_w_qkv", "att_w_o", "ffn_w_in", "ffn_w_out")
COL_SHARDED = ("mix_w_in", "att_w_qkv", "att_w_o", "ffn_w_in")
SMALL = ("norm_mix", "norm_ffn", "norm_final", "s5_A_re", "s5_A_im", "s5_log_dt", "s5_B_re", "s5_B_im", "s5_C_re",
         "s5_C_im", "s5_D", "s5_glu_b", "hgrn_gamma", "hgrn_norm", "ffn_conv_b")
ORDER = ("norm_mix", "norm_ffn", "norm_final", "mix_w_in", "mix_w_out", "s5_A_re", "s5_A_im", "s5_log_dt", "s5_B_re",
         "s5_B_im", "s5_C_re", "s5_C_im", "s5_D", "s5_glu_w", "s5_glu_b", "hgrn_gamma", "hgrn_norm", "att_w_qkv",
         "att_w_o", "ffn_w_in", "ffn_conv_w", "ffn_conv_b", "ffn_w_out")
PACK_COLS = 1024


def _step(x, positions, target, W, M, V):
    L, D = x.shape
    me = 4 * lax.axis_index("x") + 2 * lax.axis_index("y") + lax.axis_index("c")
    n_cw = W["ffn_conv_w"].shape[-1]
    shards = {
        "mix_w_in": cast_bf16(W["mix_w_in"][0], name="mix_w_in_cast"),
        "conv_w": W["ffn_conv_w"].reshape(6, n_cw),
        "s5_glu_w": cast_bf16(W["s5_glu_w"][0], name="s5_glu_w_cast"),
        "mix_w_out": cast_bf16(W["mix_w_out"][0], name="mix_w_out_cast"),
        "ffn_w_in0": cast_bf16(W["ffn_w_in"][0], name="ffn_w_in0_cast"),
        "ffn_w_out0": cast_bf16(W["ffn_w_out"][0], name="ffn_w_out0_cast"),
        "att_w_qkv": cast_bf16(W["att_w_qkv"][0], name="att_w_qkv_cast"),
        "att_w_o": cast_bf16(W["att_w_o"][0], name="att_w_o_cast"),
        "ffn_w_in1": cast_bf16(W["ffn_w_in"][1], name="ffn_w_in1_cast"),
        "ffn_w_out1": cast_bf16(W["ffn_w_out"][1], name="ffn_w_out1_cast"),
    }
    gather_handles, token = copies_start(list(shards.values()), False, name="gather_start")
    gather_handle = dict(zip(shards, gather_handles))

    def gathered(key, after, cols):
        own, land = copies_wait(gather_handle[key], False, after, name=key + "_gwait")
        return (cols_from_shards if cols else rows_from_shards)(land, own, name=key + "_asm")

    conv_b = W["ffn_conv_b"].reshape(2, 1, -1)

    s5_params = (W["s5_A_re"][0], W["s5_A_im"][0], W["s5_log_dt"][0], W["s5_B_re"][0], W["s5_B_im"][0],
                 W["s5_C_re"][0], W["s5_C_im"][0])
    (a_re, a_im, wb_re, wb_im, wc_re, wc_im), s5_prep_vjp = jax.vjp(_s5_prepare, *s5_params)
    dvec = W["s5_D"].reshape(1, S5_WIDTH)
    glu_b = W["s5_glu_b"].reshape(1, S5_WIDTH)
    lb, lb_vjp = jax.vjp(_lower_bound, W["hgrn_gamma"])
    hg_norm = W["hgrn_norm"].reshape(1, -1)
    tabs = rope_tables(positions)

    hn0 = rms_fwd(x, W["norm_mix"][0], dep=token, name="l0_rms")
    w_mix_in = gathered("mix_w_in", hn0, True)
    proj = mm(hn0, w_mix_in, name="l0_proj")
    u_bf = cast_bf16(proj[:, :S5_WIDTH], name="l0_u_cast")
    bu_re = mm(u_bf, wb_re, name="s5_bu_re")
    bu_im = mm(u_bf, wb_im, name="s5_bu_im")
    xs_re, xs_im = s5_scan_fwd(a_re, a_im, bu_re, bu_im, name="s5_scan")
    y0 = mm(xs_im, wc_im, res=mm(xs_re, wc_re, name="s5_y_re"), name="s5_y_im")
    w_glu = gathered("s5_glu_w", y0, False)
    oa = s5_out_fwd(y0, proj, dvec, w_glu, glu_b, name="s5_out")
    ob, hg_states = hgrn_fwd(proj, lb, hg_norm, name="hgrn_fwd")
    cat = jnp.concatenate([oa, ob], axis=1)
    w_mix_out = gathered("mix_w_out", cat, False)
    h1 = mm(cat, w_mix_out, res=x, name="l0_mix_out")
    cw_own, cw_land = copies_wait(gather_handle["conv_w"], False, h1, name="conv_w_gwait")
    cw_all = lax.dynamic_update_slice(cw_land, cw_own[None], (me, 0, 0))
    conv_w = cw_all.transpose(1, 0, 2).reshape(2, 3, N_DEV * n_cw)
    w_ffn_in, w_ffn_out = [None, None], [None, None]
    h2, ffn0_saved, w_ffn_in[0], w_ffn_out[0] = _ffn_fwd(
        h1, W["norm_ffn"][0], lambda a: gathered("ffn_w_in0", a, True), conv_w[0], conv_b[0],
        lambda a: gathered("ffn_w_out0", a, False), "ffn0")

    hn2 = rms_fwd(h2, W["norm_mix"][1], name="l1_rms")
    w_qkv = gathered("att_w_qkv", hn2, True)
    qkv = mm(hn2, w_qkv, name="l1_qkv")
    qkv_r = rope_fwd(qkv, tabs, name="rope_fwd")
    att_in, att_o, att_l = [], [], []
    for g, d in enumerate(ATT_DILATIONS):
        qp = _to_branch_order(qkv_r[:, 512 * g:512 * (g + 1)], d)
        kp = _to_branch_order(qkv_r[:, 1536 + 512 * g:1536 + 512 * (g + 1)], d)
        vp = _to_branch_order(qkv_r[:, 3072 + 512 * g:3072 + 512 * (g + 1)], d)
        o_p, l_p = attn_fwd(qp, kp, vp, d, name=f"attn_fwd{g}")
        att_in.append((qp, kp, vp, l_p))
        att_o.append(_to_token_order(o_p, d))
        att_l.append(_to_token_order(l_p, d))
    o_att = merge_fwd(att_o, att_l, name="merge_fwd")
    w_o = gathered("att_w_o", o_att, True)
    h3 = mm(o_att, w_o, res=h2, name="l1_mix_out")
    h4, ffn1_saved, w_ffn_in[1], w_ffn_out[1] = _ffn_fwd(
        h3, W["norm_ffn"][1], lambda a: gathered("ffn_w_in1", a, True), conv_w[1], conv_b[1],
        lambda a: gathered("ffn_w_out1", a, False), "ffn1")

    exchanges = {}

    def send_grad(key, g, cols):
        if cols:
            parts = shards_from_cols(g, name=key + "_split")
        else:
            parts = cast_bf16(g, name=key + "_gcast").reshape(N_DEV, g.shape[0] // N_DEV, g.shape[1])
        (handle,), sent = copies_start([parts], True, name=key + "_xstart")
        exchanges[key] = handle
        return sent

    loss, dh4, dg_final = final_loss(h4, W["norm_final"], target, name="final_loss")
    dh3, dg_ffn1, dcw1, dcb1 = _ffn_bwd(h3, W["norm_ffn"][1], w_ffn_in[1], conv_w[1], conv_b[1], w_ffn_out[1],
                                        ffn1_saved, dh4, "ffn1", lambda g: send_grad("ffn_w_in1", g, True),
                                        lambda g: send_grad("ffn_w_out1", g, False))
    sent = send_grad("att_w_o", mm(o_att, dh3, ta=True, name="l1_dwo"), True)
    d_oatt = mm(dh3, w_o, tb=True, dep=sent, name="l1_dmix")
    mb = merge_bwd(att_o, att_l, d_oatt, name="merge_bwd")
    dq_t, dk_t, dv_t = [], [], []
    for g, d in enumerate(ATT_DILATIONS):
        qp, kp, vp, l_p = att_in[g]
        dq_p, dk_p, dv_p = attn_bwd(qp, kp, vp, l_p, _to_branch_order(mb[g], d), _to_branch_order(mb[3 + g], d), d,
                                    name=f"attn_bwd{g}")
        dq_t.append(_to_token_order(dq_p, d))
        dk_t.append(_to_token_order(dk_p, d))
        dv_t.append(_to_token_order(dv_p, d))
    d_qkv = rope_bwd(jnp.concatenate(dq_t, axis=1), jnp.concatenate(dk_t, axis=1), jnp.concatenate(dv_t, axis=1),
                     tabs, name="rope_bwd")
    sent = send_grad("att_w_qkv", mm(hn2, d_qkv, ta=True, name="l1_dwqkv"), True)
    d_hn2 = mm(d_qkv, w_qkv, tb=True, dep=sent, name="l1_dhn")
    dh2, dg_mix1 = rms_bwd(h2, W["norm_mix"][1], d_hn2, dh3, name="l1_drms")

    dh1, dg_ffn0, dcw0, dcb0 = _ffn_bwd(h1, W["norm_ffn"][0], w_ffn_in[0], conv_w[0], conv_b[0], w_ffn_out[0],
                                        ffn0_saved, dh2, "ffn0", lambda g: send_grad("ffn_w_in0", g, True),
                                        lambda g: send_grad("ffn_w_out0", g, False))
    sent = send_grad("mix_w_out", mm(cat, dh1, ta=True, name="l0_dwout"), False)
    dcat = mm(dh1, w_mix_out, tb=True, dep=sent, name="l0_dcat")
    d_hg, dlb, dhg_norm = hgrn_bwd(proj, lb, hg_norm, hg_states, dcat, name="hgrn_bwd")
    dy, du_d, z_bf, dzg, dglu_b, dD = s5_out_bwd(y0, proj, dvec, w_glu, glu_b, dcat, name="s5_dout")
    sent = send_grad("s5_glu_w", mm(z_bf, dzg, ta=True, name="s5_dglu"), False)
    dxs_re = mm(dy, wc_re, tb=True, dep=sent, name="s5_dxs_re")
    dxs_im = mm(dy, wc_im, tb=True, name="s5_dxs_im")
    dwc_re = mm(xs_re, dy, ta=True, name="s5_dwc_re")
    dwc_im = mm(xs_im, dy, ta=True, name="s5_dwc_im")
    dbu_re, dbu_im, da_re, da_im = s5_scan_bwd(a_re, a_im, xs_re, xs_im, dxs_re, dxs_im, name="s5_dscan")
    du = mm(dbu_im, wb_im, tb=True, res=mm(dbu_re, wb_re, tb=True, res=du_d, name="s5_du_re"), out_dtype=BF16,
            name="s5_du_im")
    dwb_re = mm(u_bf, dbu_re, ta=True, name="s5_dwb_re")
    dwb_im = mm(u_bf, dbu_im, ta=True, name="s5_dwb_im")
    s5_small = s5_prep_vjp((da_re, da_im, dwb_re, dwb_im, dwc_re, dwc_im))
    d_proj = jnp.concatenate([du, d_hg], axis=1)
    sent = send_grad("mix_w_in", mm(hn0, d_proj, ta=True, name="l0_dwin"), True)
    d_hn0 = mm(d_proj, w_mix_in, tb=True, dep=sent, name="l0_dhn")
    grad_x, dg_mix0 = rms_bwd(x, W["norm_mix"][0], d_hn0, dh1, name="l0_drms")
    (d_gamma,) = lb_vjp(dlb)
    out = {}

    dA_re, dA_im, dlog_dt, dB_re, dB_im, dC_re, dC_im = s5_small
    small_g = dict(norm_mix=jnp.concatenate([dg_mix0, dg_mix1], axis=0), norm_ffn=jnp.concatenate([dg_ffn0, dg_ffn1], axis=0),
                   norm_final=dg_final, s5_A_re=dA_re, s5_A_im=dA_im, s5_log_dt=dlog_dt, s5_B_re=dB_re, s5_B_im=dB_im,
                   s5_C_re=dC_re, s5_C_im=dC_im, s5_D=dD, s5_glu_b=dglu_b, hgrn_gamma=d_gamma, hgrn_norm=dhg_norm,
                   ffn_conv_b=jnp.concatenate([dcb0, dcb1], axis=0))
    conv_w_g = jnp.stack([dcw0, dcw1], axis=0)
    sizes = [math.prod(W[n].shape) for n in SMALL]
    n_conv = conv_w_g.size
    total = sum(sizes) + n_conv + 1
    rows = -(-total // PACK_COLS)
    rows = -(-rows // 8) * 8
    pad = rows * PACK_COLS - total

    def pack(vals, conv_part, last):
        flat = [v.reshape(-1).astype(F32) for v in vals] + [conv_part.reshape(-1), last.reshape(-1),
                                                            jnp.zeros((pad,), F32)]
        return jnp.concatenate(flat).reshape(rows, PACK_COLS)

    def conv_full(shard):
        full = jnp.zeros((2, 3, N_DEV * n_cw), F32)
        return lax.dynamic_update_slice(full, shard, (0, 0, me * n_cw))

    zero1 = jnp.zeros((1,), F32)
    g_pack = pack([small_g[n] for n in SMALL], conv_w_g, loss)
    w_pack = pack([W[n] for n in SMALL], conv_full(W["ffn_conv_w"]), zero1)
    m_pack = pack([M[n] for n in SMALL], conv_full(M["ffn_conv_w"]), zero1)
    v_pack = pack([V[n] for n in SMALL], conv_full(V["ffn_conv_w"]), zero1 + 1.0)
    (small_handle,), small_sent = copies_start([g_pack], False, name="small_xstart")

    def finish(key, w, m, v):
        own, recv = copies_wait(exchanges[key], True, small_sent, name=key + "_xwait")
        _, R, Cn = recv.shape
        return reduce_adamw(recv, own, True, me, w.reshape(R, Cn), m.reshape(R, Cn), v.reshape(R, Cn),
                            name=key + "_adamw")

    for name in ("ffn_w_out", "ffn_w_in"):
        per_layer = [finish(f"{name}{l}", W[name][l], M[name][l], V[name][l]) for l in (1, 0)][::-1]
        out[name] = tuple(jnp.stack([per_layer[0][k], per_layer[1][k]], axis=0) for k in range(4))
    for name in ("att_w_o", "att_w_qkv", "mix_w_out", "s5_glu_w", "mix_w_in"):
        out[name] = tuple(r.reshape(W[name].shape) for r in finish(name, W[name], M[name], V[name]))

    small_own, small_recv = copies_wait(small_handle, False, out["mix_w_in"][0], name="small_xwait")
    res = reduce_adamw(small_recv, small_own, False, me, w_pack, m_pack, v_pack, name="small_adamw")
    flat = [r.reshape(-1) for r in res]
    off = 0
    for n, sz in zip(SMALL, sizes):
        out[n] = tuple(f[off:off + sz].reshape(W[n].shape) for f in flat)
        off += sz
    conv_res = [f[off:off + n_conv].reshape(2, 3, N_DEV * n_cw) for f in flat]
    out["ffn_conv_w"] = tuple(lax.dynamic_slice(c, (0, 0, me * n_cw), (2, 3, n_cw)) for c in conv_res)
    off += n_conv
    loss_total = flat[0][off]

    result = [loss_total, grad_x[None]]
    for k in range(4):
        result += [out[n][k] for n in ORDER]
    return tuple(result)
```

```python
import functools
import math

import jax
import jax.numpy as jnp
from jax import lax
from jax.experimental import pallas as pl
from jax.experimental.pallas import tpu as pltpu

F32 = jnp.float32
BF16 = jnp.bfloat16
MESH_ID = pl.DeviceIdType.MESH
N_DEV = 8
VMEM_LIMIT_BYTES = 56 * 1024 * 1024

NORM_EPS = 1e-6
S5_WIDTH, S5_GROUP, S5_GROUPS, S5_STATE = 512, 16, 32, 64
HG_HEADS, HG_DIM, HG_CHUNK = 4, 128, 64
ATT_E, ATT_HPG, ATT_BLOCK = 64, 8, 128
ATT_DILATIONS = (1, 4, 16)
ROT_DIM, ROPE_THETA = 16, 500000.0
D_FF = 2816
ADAM_LR, ADAM_B1, ADAM_B2, ADAM_EPS, ADAM_WD, ADAM_STEP = 0.001, 0.9, 0.999, 1e-08, 0.01, 10
NEG_BIG = -1e30


def _params(**kw):
    return pltpu.CompilerParams(vmem_limit_bytes=VMEM_LIMIT_BYTES, **kw)


def _pick(n, cands):
    for c in cands:
        if n % c == 0:
            return c
    return n


def _dot(a, b):
    return jnp.dot(a.astype(BF16), b.astype(BF16), preferred_element_type=F32)


def _dot_nt(a, b):
    return lax.dot_general(a.astype(BF16), b.astype(BF16), (((1,), (1,)), ((), ())), preferred_element_type=F32)


def _dot_tn(a, b):
    return lax.dot_general(a.astype(BF16), b.astype(BF16), (((0,), (0,)), ((), ())), preferred_element_type=F32)


def _dot_f32(a, b):
    return jnp.dot(a, b, preferred_element_type=F32, precision=lax.Precision.HIGHEST)


def _dot_f32_nt(a, b):
    return lax.dot_general(a, b, (((1,), (1,)), ((), ())), preferred_element_type=F32, precision=lax.Precision.HIGHEST)


def _dot_f32_tn(a, b):
    return lax.dot_general(a, b, (((0,), (0,)), ((), ())), preferred_element_type=F32, precision=lax.Precision.HIGHEST)


def _sigmoid(x):
    return 1.0 / (1.0 + jnp.exp(-x))


V7X_HBM_BYTES_PER_S = 3.2e12
V7X_MXU_FLOPS_PER_S = 0.7e15
GRID_STEP_S = 0.35e-6
MM_VMEM_BUDGET = 40 * 1024 * 1024


def _divisors(n, cands):
    return [c for c in cands if c <= n and n % c == 0] or [n]


def _mm_tiles(m, n, k, sa, sb, so, sr):
    best = None
    for tm in _divisors(m, (2816, 2048, 1408, 1024, 512, 256, 128)):
        for tn in _divisors(n, (2816, 2048, 1408, 1024, 512, 256, 128)):
            for tk in _divisors(k, (k, 2816, 2560, 2304, 2048, 1536, 1408, 1280, 1024, 512, 256, 128)):
                nk = k // tk
                vmem = 2 * (tm * tk * sa + tk * tn * sb + tm * tn * (so + sr)) + (tm * tn * 4 if nk > 1 else 0)
                vmem += tm * tk * 2 * (sa > 2) + tk * tn * 2 * (sb > 2) + tm * tn * 4
                if vmem > MM_VMEM_BUDGET:
                    continue
                ni, nj = m // tm, n // tn
                for i_outer in (True, False):
                    if i_outer:
                        a_reads = 1 if nk == 1 else nj
                        b_reads = 1 if (nk == 1 and nj == 1) else ni
                    else:
                        b_reads = 1 if nk == 1 else ni
                        a_reads = 1 if (nk == 1 and ni == 1) else nj
                    traffic = a_reads * m * k * sa + b_reads * k * n * sb + m * n * (so + sr)
                    t = max(traffic / V7X_HBM_BYTES_PER_S, 2.0 * m * n * k / V7X_MXU_FLOPS_PER_S)
                    t += ni * nj * nk * GRID_STEP_S
                    t += (tm * tk * sa + tk * tn * sb + tm * tn * so) / V7X_HBM_BYTES_PER_S
                    if best is None or t < best[0]:
                        best = (t, tm, tn, tk, i_outer)
    assert best is not None, (m, n, k)
    return best[1:]


def mm(a, b, *, ta=False, tb=False, res=None, out_dtype=F32, dep=None, name):
    m, k = (a.shape[1], a.shape[0]) if ta else a.shape
    n = b.shape[0] if tb else b.shape[1]
    assert (b.shape[1] if tb else b.shape[0]) == k
    has_res = res is not None
    tm, tn, tk, i_outer = _mm_tiles(m, n, k, a.dtype.itemsize, b.dtype.itemsize, jnp.dtype(out_dtype).itemsize,
                                    res.dtype.itemsize if has_res else 0)
    nk = k // tk
    deps = [] if dep is None else [dep]
    dn = (((0 if ta else 1,), (1 if tb else 0,)), ((), ()))

    def body_single(*refs):
        a_ref, b_ref = refs[:2]
        o_ref = refs[-1]
        out = lax.dot_general(a_ref[...].astype(BF16), b_ref[...].astype(BF16), dn, preferred_element_type=F32)
        if has_res:
            out = out + refs[2][...].astype(F32)
        o_ref[...] = out.astype(o_ref.dtype)

    def body(*refs):
        a_ref, b_ref = refs[:2]
        r_ref = refs[2] if has_res else None
        o_ref, acc_ref = refs[-2:]
        kk = pl.program_id(2)
        part = lax.dot_general(a_ref[...].astype(BF16), b_ref[...].astype(BF16), dn, preferred_element_type=F32)

        @pl.when(kk == 0)
        def _():
            acc_ref[...] = part

        @pl.when(kk > 0)
        def _():
            acc_ref[...] += part

        @pl.when(kk == nk - 1)
        def _():
            out = acc_ref[...]
            if has_res:
                out = out + r_ref[...].astype(F32)
            o_ref[...] = out.astype(o_ref.dtype)

    def ij(f):
        return (lambda g0, g1, q: f(g0, g1, q)) if i_outer else (lambda g0, g1, q: f(g1, g0, q))

    a_spec = pl.BlockSpec((tk, tm), ij(lambda i, j, q: (q, i))) if ta else pl.BlockSpec((tm, tk), ij(lambda i, j, q: (i, q)))
    b_spec = pl.BlockSpec((tn, tk), ij(lambda i, j, q: (j, q))) if tb else pl.BlockSpec((tk, tn), ij(lambda i, j, q: (q, j)))
    o_spec = pl.BlockSpec((tm, tn), ij(lambda i, j, q: (i, j)))
    in_specs = [a_spec, b_spec] + ([o_spec] if has_res else []) + [pl.BlockSpec((8, 128), lambda g0, g1, q: (0, 0))] * len(deps)
    args = (a, b) + ((res,) if has_res else ()) + tuple(deps)
    grid = (m // tm, n // tn, nk) if i_outer else (n // tn, m // tm, nk)
    return pl.pallas_call(
        body_single if nk == 1 else body, grid=grid, in_specs=in_specs, out_specs=o_spec,
        out_shape=jax.ShapeDtypeStruct((m, n), out_dtype),
        scratch_shapes=[] if nk == 1 else [pltpu.VMEM((tm, tn), F32)],
        compiler_params=_params(dimension_semantics=("parallel", "parallel", "arbitrary")), name=name,
    )(*args)


def rms_fwd(x, g, *, dep=None, name):
    L, D = x.shape
    tr = _pick(L, (256, 128))

    def body(x_ref, g_ref, *rest):
        o_ref = rest[-1]
        xv = x_ref[...]
        r = lax.rsqrt(jnp.mean(xv * xv, axis=-1, keepdims=True) + NORM_EPS)
        o_ref[...] = (xv * r * g_ref[...]).astype(o_ref.dtype)

    row = pl.BlockSpec((tr, D), lambda i: (i, 0))
    vec = pl.BlockSpec((1, D), lambda i: (0, 0))
    deps = [] if dep is None else [dep]
    return pl.pallas_call(body, grid=(L // tr,), in_specs=[row, vec] + [pl.BlockSpec((8, 128), lambda i: (0, 0))] * len(deps),
                          out_specs=row, out_shape=jax.ShapeDtypeStruct((L, D), BF16), name=name)(
        x, g.reshape(1, D), *deps)


def rms_bwd(x, g, dy, dres, *, name):
    L, D = x.shape
    tr = _pick(L, (256, 128))

    def body(x_ref, g_ref, dy_ref, dres_ref, dx_ref, dg_ref):
        xv = x_ref[...]
        r = lax.rsqrt(jnp.mean(xv * xv, axis=-1, keepdims=True) + NORM_EPS)
        xh = xv * r
        dyv = dy_ref[...].astype(F32)

        @pl.when(pl.program_id(0) == 0)
        def _():
            dg_ref[...] = jnp.zeros_like(dg_ref)

        dg_ref[...] += jnp.sum(dyv * xh, axis=0, keepdims=True)
        dxh = dyv * g_ref[...]
        dx_ref[...] = dres_ref[...] + r * (dxh - xh * jnp.mean(dxh * xh, axis=-1, keepdims=True))

    row = pl.BlockSpec((tr, D), lambda i: (i, 0))
    vec = pl.BlockSpec((1, D), lambda i: (0, 0))
    return pl.pallas_call(body, grid=(L // tr,), in_specs=[row, vec, row, row], out_specs=[row, vec],
                          out_shape=[jax.ShapeDtypeStruct((L, D), F32), jax.ShapeDtypeStruct((1, D), F32)],
                          compiler_params=_params(dimension_semantics=("arbitrary",)), name=name)(
        x, g.reshape(1, D), dy, dres)


def final_loss(h, g, target, *, name):
    L, D = h.shape
    tr = _pick(L, (256, 128))

    def body(x_ref, g_ref, t_ref, loss_ref, dx_ref, dg_ref):
        xv = x_ref[...]
        gv = g_ref[...]
        r = lax.rsqrt(jnp.mean(xv * xv, axis=-1, keepdims=True) + NORM_EPS)
        xh = xv * r
        err = xh * gv - t_ref[...]

        @pl.when(pl.program_id(0) == 0)
        def _():
            dg_ref[...] = jnp.zeros_like(dg_ref)
            loss_ref[...] = jnp.zeros_like(loss_ref)

        loss_ref[...] += 0.5 * jnp.sum(jnp.mean(err * err, axis=-1, keepdims=True), axis=0, keepdims=True)
        dyv = err * (1.0 / D)
        dg_ref[...] += jnp.sum(dyv * xh, axis=0, keepdims=True)
        dxh = dyv * gv
        dx_ref[...] = r * (dxh - xh * jnp.mean(dxh * xh, axis=-1, keepdims=True))

    row = pl.BlockSpec((tr, D), lambda i: (i, 0))
    vec = pl.BlockSpec((1, D), lambda i: (0, 0))
    one = pl.BlockSpec((1, 1), lambda i: (0, 0))
    return pl.pallas_call(body, grid=(L // tr,), in_specs=[row, vec, row], out_specs=[one, row, vec],
                          out_shape=[jax.ShapeDtypeStruct((1, 1), F32), jax.ShapeDtypeStruct((L, D), F32),
                                     jax.ShapeDtypeStruct((1, D), F32)],
                          compiler_params=_params(dimension_semantics=("arbitrary",)), name=name)(
        h, g.reshape(1, D), target)


def _cmul(ar, ai, br, bi):
    return ar * br - ai * bi, ar * bi + ai * br


def _powers(ar, ai):
    rows = [(ar, ai)]
    for _ in range(7):
        rows.append(_cmul(rows[-1][0], rows[-1][1], ar, ai))
    table = (jnp.concatenate([r[0] for r in rows], axis=0), jnp.concatenate([r[1] for r in rows], axis=0))
    return (rows[0], rows[1], rows[3]), table


def _block_scan(br, bi, steps, shift):
    yr, yi = br, bi
    for s, (pr, pi) in zip((1, 2, 4), steps):
        sr, si = shift(yr, s), shift(yi, s)
        yr, yi = yr + pr * sr - pi * si, yi + pr * si + pi * sr
    return yr, yi


def s5_scan_fwd(a_re, a_im, bu_re, bu_im, *, name):
    L, P = bu_re.shape
    W = _pick(P, (512, 256, 128))

    def body(ar_ref, ai_ref, br_ref, bi_ref, xr_ref, xi_ref):
        steps, (tr, ti) = _powers(ar_ref[...], ai_ref[...])
        row = lax.broadcasted_iota(jnp.int32, (8, W), 0)

        def shift(y, s):
            return jnp.where(row >= s, pltpu.roll(y, s, 0), 0.0)

        def step(t8, carry):
            cr, ci = carry
            base = pl.multiple_of(t8 * 8, 8)
            yr, yi = _block_scan(br_ref[pl.ds(base, 8), :], bi_ref[pl.ds(base, 8), :], steps, shift)
            xr = yr + tr * cr - ti * ci
            xi = yi + tr * ci + ti * cr
            xr_ref[pl.ds(base, 8), :] = xr
            xi_ref[pl.ds(base, 8), :] = xi
            return jnp.broadcast_to(xr[7:8, :], (8, W)), jnp.broadcast_to(xi[7:8, :], (8, W))

        zero = jnp.zeros((8, W), F32)
        lax.fori_loop(0, L // 8, step, (zero, zero), unroll=2)

    vec = pl.BlockSpec((1, W), lambda j: (0, j))
    col = pl.BlockSpec((L, W), lambda j: (0, j))
    return pl.pallas_call(body, grid=(P // W,), in_specs=[vec, vec, col, col], out_specs=[col, col],
                          out_shape=[jax.ShapeDtypeStruct((L, P), F32)] * 2,
                          compiler_params=_params(dimension_semantics=("parallel",)), name=name)(
        a_re, a_im, bu_re, bu_im)


def s5_scan_bwd(a_re, a_im, xs_re, xs_im, dx_re, dx_im, *, name):
    L, P = xs_re.shape
    W = _pick(P, (256, 128))

    def body(ar_ref, ai_ref, xr_ref, xi_ref, dr_ref, di_ref, lr_ref, li_ref, dar_ref, dai_ref):
        ar, ai = ar_ref[...], -ai_ref[...]
        steps, (tr, ti) = _powers(ar, ai)
        tr = jnp.concatenate([tr[j:j + 1, :] for j in range(7, -1, -1)], axis=0)
        ti = jnp.concatenate([ti[j:j + 1, :] for j in range(7, -1, -1)], axis=0)
        row8 = lax.broadcasted_iota(jnp.int32, (8, W), 0)
        nblk = L // 8

        def shift(y, s):
            return jnp.where(row8 < 8 - s, pltpu.roll(y, 8 - s, 0), 0.0)

        def step(s, carry):
            cr, ci = carry
            base = pl.multiple_of((nblk - 1 - s) * 8, 8)
            yr, yi = _block_scan(dr_ref[pl.ds(base, 8), :], di_ref[pl.ds(base, 8), :], steps, shift)
            lr = yr + tr * cr - ti * ci
            li = yi + tr * ci + ti * cr
            lr_ref[pl.ds(base, 8), :] = lr
            li_ref[pl.ds(base, 8), :] = li
            return jnp.broadcast_to(lr[0:1, :], (8, W)), jnp.broadcast_to(li[0:1, :], (8, W))

        zero = jnp.zeros((8, W), F32)
        lax.fori_loop(0, nblk, step, (zero, zero), unroll=2)
        row = lax.broadcasted_iota(jnp.int32, (L, W), 0)
        xpr = jnp.where(row >= 1, pltpu.roll(xr_ref[...], 1, 0), 0.0)
        xpi = jnp.where(row >= 1, pltpu.roll(xi_ref[...], 1, 0), 0.0)
        lr, li = lr_ref[...], li_ref[...]
        dar_ref[...] = jnp.sum(lr * xpr + li * xpi, axis=0, keepdims=True)
        dai_ref[...] = jnp.sum(li * xpr - lr * xpi, axis=0, keepdims=True)

    vec = pl.BlockSpec((1, W), lambda j: (0, j))
    col = pl.BlockSpec((L, W), lambda j: (0, j))
    return pl.pallas_call(body, grid=(P // W,), in_specs=[vec, vec, col, col, col, col],
                          out_specs=[col, col, vec, vec],
                          out_shape=[jax.ShapeDtypeStruct((L, P), F32)] * 2 + [jax.ShapeDtypeStruct((1, P), F32)] * 2,
                          compiler_params=_params(dimension_semantics=("parallel",)), name=name)(
        a_re, a_im, xs_re, xs_im, dx_re, dx_im)


def _gelu(y):
    c = math.sqrt(2.0 / math.pi)
    t = jnp.tanh(c * (y + 0.044715 * y * y * y))
    return 0.5 * y * (1.0 + t), t


def s5_out_fwd(y0, proj, dvec, glu_w, glu_b, *, name):
    L, C = y0.shape
    tr = _pick(L, (256, 128))

    def body(y_ref, u_ref, d_ref, w_ref, b_ref, o_ref):
        z, _ = _gelu(y_ref[...] + d_ref[...] * u_ref[...])
        zg = _dot(z, w_ref[...]) + b_ref[...]
        o_ref[...] = (z * _sigmoid(zg)).astype(o_ref.dtype)

    row = pl.BlockSpec((tr, C), lambda i: (i, 0))
    vec = pl.BlockSpec((1, C), lambda i: (0, 0))
    wsp = pl.BlockSpec((C, C), lambda i: (0, 0))
    return pl.pallas_call(body, grid=(L // tr,), in_specs=[row, row, vec, wsp, vec], out_specs=row,
                          out_shape=jax.ShapeDtypeStruct((L, C), BF16), name=name)(
        y0, proj, dvec, glu_w, glu_b)


def s5_out_bwd(y0, proj, dvec, glu_w, glu_b, dcat, *, name):
    L, C = y0.shape
    tr = _pick(L, (256, 128))

    def body(y_ref, u_ref, d_ref, w_ref, b_ref, do_ref, dy_ref, dud_ref, z_ref, dzg_ref, db_ref, dd_ref):
        u = u_ref[...]
        y = y_ref[...] + d_ref[...] * u
        z, t = _gelu(y)
        zg = _dot(z, w_ref[...]) + b_ref[...]
        s = _sigmoid(zg)
        do = do_ref[...]
        dzg = do * z * s * (1.0 - s)
        dz = do * s + _dot_nt(dzg, w_ref[...])
        c = math.sqrt(2.0 / math.pi)
        dgelu = 0.5 * (1.0 + t) + 0.5 * y * (1.0 - t * t) * c * (1.0 + 3.0 * 0.044715 * y * y)
        dy = dz * dgelu

        @pl.when(pl.program_id(0) == 0)
        def _():
            db_ref[...] = jnp.zeros_like(db_ref)
            dd_ref[...] = jnp.zeros_like(dd_ref)

        db_ref[...] += jnp.sum(dzg, axis=0, keepdims=True)
        dd_ref[...] += jnp.sum(dy * u, axis=0, keepdims=True)
        dy_ref[...] = dy
        dud_ref[...] = dy * d_ref[...]
        z_ref[...] = z.astype(BF16)
        dzg_ref[...] = dzg.astype(BF16)

    row = pl.BlockSpec((tr, C), lambda i: (i, 0))
    vec = pl.BlockSpec((1, C), lambda i: (0, 0))
    wsp = pl.BlockSpec((C, C), lambda i: (0, 0))
    return pl.pallas_call(body, grid=(L // tr,), in_specs=[row, row, vec, wsp, vec, row],
                          out_specs=[row, row, row, row, vec, vec],
                          out_shape=[jax.ShapeDtypeStruct((L, C), F32), jax.ShapeDtypeStruct((L, C), F32),
                                     jax.ShapeDtypeStruct((L, C), BF16), jax.ShapeDtypeStruct((L, C), BF16),
                                     jax.ShapeDtypeStruct((1, C), F32), jax.ShapeDtypeStruct((1, C), F32)],
                          compiler_params=_params(dimension_semantics=("arbitrary",)), name=name)(
        y0, proj, dvec, glu_w, glu_b, dcat)


def _hg_gates(xq, xf, lb, tri):
    C = xq.shape[0]
    sq = _sigmoid(xq)
    q = xq * sq
    sg = _sigmoid(xf)
    f = lb + (1.0 - lb) * sg
    kk = 1.0 - f
    b = _dot_f32(tri, jnp.log(f))
    bm = b[C // 2 - 1:C // 2, :]
    bl = b[C - 1:C, :]
    eb = jnp.exp(b)
    return dict(sq=sq, q=q, sg=sg, f=f, kk=kk, b=b, bm=bm, bl=bl, eb=eb, ebl=jnp.exp(bl),
                qb=q * eb, eqm=jnp.exp(b - bm), ekm=jnp.exp(bm - b), ekl=jnp.exp(bl - b))


def _tri(C, lower):
    r = lax.broadcasted_iota(jnp.int32, (C, C), 0)
    c = lax.broadcasted_iota(jnp.int32, (C, C), 1)
    return (r >= c) if lower else (c >= r)


def hgrn_fwd(proj, lb, norm_g, *, name):
    L = proj.shape[0]
    C, H, K = HG_CHUNK, HG_HEADS, HG_DIM
    HK = H * K
    nc = L // C

    def body(q_ref, f_ref, i_ref, g_ref, lb_ref, ng_ref, o_ref, sall_ref, st_ref):
        @pl.when(pl.program_id(0) == 0)
        def _():
            st_ref[...] = jnp.zeros_like(st_ref)

        mask = _tri(C, True)
        tri = mask.astype(F32)
        for h in range(H):
            sl = slice(h * K, (h + 1) * K)
            v = i_ref[:, sl]
            st = st_ref[h]
            sall_ref[h] = st
            gt = _hg_gates(q_ref[:, sl], f_ref[:, sl], lb_ref[:, sl], tri)
            qt = gt["q"] * gt["eqm"]
            kt = gt["kk"] * gt["ekm"]
            kh = gt["kk"] * gt["ekl"]
            att = jnp.where(mask, _dot_nt(qt, kt), 0.0)
            o = _dot(att, v) + _dot_nt(gt["qb"], st)
            st_ref[h] = st * gt["ebl"] + _dot_tn(v, kh)
            r = lax.rsqrt(jnp.mean(o * o, axis=-1, keepdims=True) + NORM_EPS)
            xg = g_ref[:, sl]
            o_ref[:, sl] = (o * r * ng_ref[:, sl] * (xg * _sigmoid(xg))).astype(o_ref.dtype)

    def blk(cb):
        return pl.BlockSpec((C, HK), lambda i: (i, cb))

    vec = pl.BlockSpec((1, HK), lambda i: (0, 0))
    return pl.pallas_call(
        body, grid=(nc,), in_specs=[blk(1), blk(2), blk(3), blk(4), vec, vec],
        out_specs=[pl.BlockSpec((C, HK), lambda i: (i, 0)), pl.BlockSpec((None, H, K, K), lambda i: (i, 0, 0, 0))],
        out_shape=[jax.ShapeDtypeStruct((L, HK), BF16), jax.ShapeDtypeStruct((nc, H, K, K), F32)],
        scratch_shapes=[pltpu.VMEM((H, K, K), F32)],
        compiler_params=_params(dimension_semantics=("arbitrary",)), name=name,
    )(proj, proj, proj, proj, lb, norm_g)


def hgrn_bwd(proj, lb, norm_g, sall, dcat, *, name):
    L = proj.shape[0]
    C, H, K = HG_CHUNK, HG_HEADS, HG_DIM
    HK = H * K
    nc = L // C

    def body(q_ref, f_ref, i_ref, g_ref, lb_ref, ng_ref, sall_ref, do_ref, dx_ref, dlb_ref, dng_ref, dst_ref):
        @pl.when(pl.program_id(0) == 0)
        def _():
            dst_ref[...] = jnp.zeros_like(dst_ref)
            dlb_ref[...] = jnp.zeros_like(dlb_ref)
            dng_ref[...] = jnp.zeros_like(dng_ref)

        mask = _tri(C, True)
        tri = mask.astype(F32)
        tri_t = _tri(C, False).astype(F32)
        rowi = lax.broadcasted_iota(jnp.int32, (C, K), 0)
        for h in range(H):
            sl = slice(h * K, (h + 1) * K)
            xq, xf, v, xg = q_ref[:, sl], f_ref[:, sl], i_ref[:, sl], g_ref[:, sl]
            lb_h, ng = lb_ref[:, sl], ng_ref[:, sl]
            st = sall_ref[h]
            dst = dst_ref[h]
            gt = _hg_gates(xq, xf, lb_h, tri)
            q, kk, qb = gt["q"], gt["kk"], gt["qb"]
            qt = q * gt["eqm"]
            kt = kk * gt["ekm"]
            kh = kk * gt["ekl"]
            att = jnp.where(mask, _dot_nt(qt, kt), 0.0)
            o = _dot(att, v) + _dot_nt(qb, st)
            r = lax.rsqrt(jnp.mean(o * o, axis=-1, keepdims=True) + NORM_EPS)
            oh = o * r
            sgg = _sigmoid(xg)
            silu_g = xg * sgg
            d_ob = do_ref[:, sl]
            d_on = d_ob * silu_g
            dxg = d_ob * (oh * ng) * (sgg * (1.0 + xg * (1.0 - sgg)))
            dng_ref[:, sl] += jnp.sum(d_on * oh, axis=0, keepdims=True)
            doh = d_on * ng
            do = r * (doh - oh * jnp.mean(doh * oh, axis=-1, keepdims=True))
            datt = jnp.where(mask, _dot_nt(do, v), 0.0)
            dv = _dot_tn(att, do) + _dot_nt(kh, dst)
            d_qb = _dot_f32(do, st)
            d_qt = _dot_f32(datt, kt)
            d_kt = _dot_f32_tn(datt, qt)
            d_kh = _dot_f32(v, dst)
            d_bl = jnp.sum(dst * st, axis=0, keepdims=True) * gt["ebl"] + jnp.sum(d_kh * kh, axis=0, keepdims=True)
            dst_ref[h] = dst * gt["ebl"] + _dot_tn(do, qb)
            dq = d_qt * gt["eqm"] + d_qb * gt["eb"]
            db = d_qt * qt + d_qb * qb - d_kt * kt - d_kh * kh
            db = db + jnp.where(rowi == C - 1, d_bl, 0.0)
            dkk = d_kt * gt["ekm"] + d_kh * gt["ekl"]
            dlg = _dot_f32(tri_t, db)
            df = dlg / gt["f"] - dkk
            sg = gt["sg"]
            dxf = df * (1.0 - lb_h) * sg * (1.0 - sg)
            dlb_ref[:, sl] += jnp.sum(df * (1.0 - sg), axis=0, keepdims=True)
            sq = gt["sq"]
            dxq = dq * (sq * (1.0 + xq * (1.0 - sq)))
            dx_ref[:, h * K:(h + 1) * K] = dxq.astype(dx_ref.dtype)
            dx_ref[:, HK + h * K:HK + (h + 1) * K] = dxf.astype(dx_ref.dtype)
            dx_ref[:, 2 * HK + h * K:2 * HK + (h + 1) * K] = dv.astype(dx_ref.dtype)
            dx_ref[:, 3 * HK + h * K:3 * HK + (h + 1) * K] = dxg.astype(dx_ref.dtype)

    def blk(cb):
        return pl.BlockSpec((C, HK), lambda i: (nc - 1 - i, cb))

    vec = pl.BlockSpec((1, HK), lambda i: (0, 0))
    return pl.pallas_call(
        body, grid=(nc,),
        in_specs=[blk(1), blk(2), blk(3), blk(4), vec, vec,
                  pl.BlockSpec((None, H, K, K), lambda i: (nc - 1 - i, 0, 0, 0)), blk(1)],
        out_specs=[pl.BlockSpec((C, 4 * HK), lambda i: (nc - 1 - i, 0)), vec, vec],
        out_shape=[jax.ShapeDtypeStruct((L, 4 * HK), BF16), jax.ShapeDtypeStruct((1, HK), F32),
                   jax.ShapeDtypeStruct((1, HK), F32)],
        scratch_shapes=[pltpu.VMEM((H, K, K), F32)],
        compiler_params=_params(dimension_semantics=("arbitrary",)), name=name,
    )(proj, proj, proj, proj, lb, norm_g, sall, dcat)


def _shift_down(x, k, row):
    return jnp.where(row >= k, pltpu.roll(x, k, 0), 0.0)


def _shift_up(x, k, row):
    n = x.shape[0]
    return jnp.where(row < n - k, pltpu.roll(x, n - k, 0), 0.0)


def convgate_fwd(hu, conv_w, conv_b, *, name):
    L, C2 = hu.shape
    C = C2 // 2
    tc = _pick(C, (256, 128))
    nb = C // tc

    def body(a_ref, b_ref, wa_ref, wb_ref, ba_ref, bb_ref, o_ref):
        row = lax.broadcasted_iota(jnp.int32, (L, tc), 0)

        def conv(x, w, bias):
            return w[2:3, :] * x + w[1:2, :] * _shift_down(x, 1, row) + w[0:1, :] * _shift_down(x, 2, row) + bias

        ca = conv(a_ref[...], wa_ref[...], ba_ref[...])
        cb = conv(b_ref[...], wb_ref[...], bb_ref[...])
        o_ref[...] = (ca * _sigmoid(ca) * cb).astype(o_ref.dtype)

    def col(off, rows):
        return pl.BlockSpec((rows, tc), lambda j: (0, j + off))

    return pl.pallas_call(
        body, grid=(nb,), in_specs=[col(0, L), col(nb, L), col(0, 3), col(nb, 3), col(0, 1), col(nb, 1)],
        out_specs=col(0, L), out_shape=jax.ShapeDtypeStruct((L, C), BF16),
        compiler_params=_params(dimension_semantics=("parallel",)), name=name,
    )(hu, hu, conv_w, conv_w, conv_b, conv_b)


def convgate_bwd(hu, conv_w, conv_b, dact, *, name):
    L, C2 = hu.shape
    C = C2 // 2
    tc = _pick(C, (256, 128))
    nb = C // tc

    def body(a_ref, b_ref, wa_ref, wb_ref, ba_ref, bb_ref, d_ref, dxa_ref, dxb_ref, dwa_ref, dwb_ref, dba_ref, dbb_ref):
        row = lax.broadcasted_iota(jnp.int32, (L, tc), 0)

        def conv(x, w, bias):
            x1 = _shift_down(x, 1, row)
            x2 = _shift_down(x, 2, row)
            return w[2:3, :] * x + w[1:2, :] * x1 + w[0:1, :] * x2 + bias, x1, x2

        xa, xb = a_ref[...], b_ref[...]
        wa, wb = wa_ref[...], wb_ref[...]
        ca, xa1, xa2 = conv(xa, wa, ba_ref[...])
        cb, xb1, xb2 = conv(xb, wb, bb_ref[...])
        d = d_ref[...]
        sa = _sigmoid(ca)
        dca = d * cb * (sa * (1.0 + ca * (1.0 - sa)))
        dcb = d * (ca * sa)

        def back(dc, w, x, x1, x2, dx_ref, dw_ref, db_ref):
            dx = w[2:3, :] * dc + w[1:2, :] * _shift_up(dc, 1, row) + w[0:1, :] * _shift_up(dc, 2, row)
            dx_ref[...] = dx.astype(dx_ref.dtype)
            dw_ref[...] = jnp.concatenate([jnp.sum(dc * x2, axis=0, keepdims=True),
                                           jnp.sum(dc * x1, axis=0, keepdims=True),
                                           jnp.sum(dc * x, axis=0, keepdims=True)], axis=0)
            db_ref[...] = jnp.sum(dc, axis=0, keepdims=True)

        back(dca, wa, xa, xa1, xa2, dxa_ref, dwa_ref, dba_ref)
        back(dcb, wb, xb, xb1, xb2, dxb_ref, dwb_ref, dbb_ref)

    def col(off, rows):
        return pl.BlockSpec((rows, tc), lambda j: (0, j + off))

    outs = pl.pallas_call(
        body, grid=(nb,),
        in_specs=[col(0, L), col(nb, L), col(0, 3), col(nb, 3), col(0, 1), col(nb, 1), col(0, L)],
        out_specs=[col(0, L), col(0, L), col(0, 3), col(0, 3), col(0, 1), col(0, 1)],
        out_shape=[jax.ShapeDtypeStruct((L, C), BF16)] * 2 + [jax.ShapeDtypeStruct((3, C), F32)] * 2
        + [jax.ShapeDtypeStruct((1, C), F32)] * 2,
        compiler_params=_params(dimension_semantics=("parallel",)), name=name,
    )(hu, hu, conv_w, conv_w, conv_b, conv_b, dact)
    dxa, dxb, dwa, dwb, dba, dbb = outs
    return (jnp.concatenate([dxa, dxb], axis=1), jnp.concatenate([dwa, dwb], axis=1),
            jnp.concatenate([dba, dbb], axis=1))


def _to_branch_order(t, d):
    L, W = t.shape
    return t if d == 1 else t.reshape(L // d, d, W).transpose(1, 0, 2).reshape(L, W)


def _to_token_order(t, d):
    L, W = t.shape
    return t if d == 1 else t.reshape(d, L // d, W).transpose(1, 0, 2).reshape(L, W)


def rope_tables(positions):
    half = ROT_DIM // 2
    inv_freq = ROPE_THETA ** (-jnp.arange(half, dtype=F32) * 2.0 / ROT_DIM)
    ang = positions.astype(F32)[:, None] * inv_freq
    cos, sin = jnp.cos(ang), jnp.sin(ang)
    L = positions.shape[0]
    one = jnp.ones((L, ATT_E - ROT_DIM), F32)
    zero = jnp.zeros((L, ATT_E - ROT_DIM), F32)
    zh = jnp.zeros((L, half), F32)
    tc = jnp.concatenate([cos, cos, one], axis=1)
    ts1 = jnp.concatenate([zh, sin, zero], axis=1)
    ts2 = jnp.concatenate([-sin, zh, zero], axis=1)
    return tuple(jnp.concatenate([t, t], axis=1) for t in (tc, ts1, ts2))


def rope_fwd(qkv, tabs, *, name):
    L = qkv.shape[0]
    W = 512
    tr = _pick(L, (256, 128))
    nq = 1536 // W
    scale = ATT_E ** -0.5

    def body(x_ref, c_ref, s1_ref, s2_ref, o_ref):
        j = pl.program_id(1)
        x = x_ref[...]
        c = jnp.concatenate([c_ref[...]] * 4, axis=1)
        s1 = jnp.concatenate([s1_ref[...]] * 4, axis=1)
        s2 = jnp.concatenate([s2_ref[...]] * 4, axis=1)
        rot = x * c + pltpu.roll(x, 8, 1) * s1 + pltpu.roll(x, W - 8, 1) * s2
        mult = jnp.where(j < nq, scale, 1.0)
        o_ref[...] = jnp.where(j < 2 * nq, rot * mult, x).astype(o_ref.dtype)

    blk = pl.BlockSpec((tr, W), lambda i, j: (i, j))
    tab = pl.BlockSpec((tr, 128), lambda i, j: (i, 0))
    return pl.pallas_call(body, grid=(L // tr, 3 * nq), in_specs=[blk, tab, tab, tab], out_specs=blk,
                          out_shape=jax.ShapeDtypeStruct((L, 3 * 1536), BF16),
                          compiler_params=_params(dimension_semantics=("parallel", "parallel")), name=name)(
        qkv, *tabs)


def rope_bwd(dq, dk, dv, tabs, *, name):
    L = dq.shape[0]
    W = 512
    tr = _pick(L, (256, 128))
    nq = 1536 // W
    scale = ATT_E ** -0.5

    def body(dq_ref, dk_ref, dv_ref, c_ref, s1_ref, s2_ref, o_ref):
        j = pl.program_id(1)
        c = jnp.concatenate([c_ref[...]] * 4, axis=1)
        s1 = jnp.concatenate([s1_ref[...]] * 4, axis=1)
        s2 = jnp.concatenate([s2_ref[...]] * 4, axis=1)

        def unrot(dy):
            return dy * c + pltpu.roll(dy * s1, W - 8, 1) + pltpu.roll(dy * s2, 8, 1)

        @pl.when(j < nq)
        def _():
            o_ref[...] = (unrot(dq_ref[...]) * scale).astype(o_ref.dtype)

        @pl.when((j >= nq) & (j < 2 * nq))
        def _():
            o_ref[...] = unrot(dk_ref[...]).astype(o_ref.dtype)

        @pl.when(j >= 2 * nq)
        def _():
            o_ref[...] = dv_ref[...].astype(o_ref.dtype)

    def src(k):
        return pl.BlockSpec((tr, W), lambda i, j: (i, jnp.clip(j - k * nq, 0, nq - 1)))

    tab = pl.BlockSpec((tr, 128), lambda i, j: (i, 0))
    return pl.pallas_call(body, grid=(L // tr, 3 * nq), in_specs=[src(0), src(1), src(2), tab, tab, tab],
                          out_specs=pl.BlockSpec((tr, W), lambda i, j: (i, j)),
                          out_shape=jax.ShapeDtypeStruct((L, 3 * 1536), BF16),
                          compiler_params=_params(dimension_semantics=("parallel", "arbitrary")), name=name)(
        dq, dk, dv, *tabs)


def _att_masks(has_prev):
    qi = lax.broadcasted_iota(jnp.int32, (ATT_BLOCK, ATT_BLOCK), 0)
    kj = lax.broadcasted_iota(jnp.int32, (ATT_BLOCK, ATT_BLOCK), 1)
    return qi >= kj, (kj >= qi) & has_prev


def attn_fwd(qp, kp, vp, d, *, name):
    L, W = qp.shape
    B, E = ATT_BLOCK, ATT_E
    nblk = L // B
    nb = nblk // d

    def body(q_ref, kc_ref, kp_ref, vc_ref, vp_ref, o_ref, l_ref):
        has_prev = (pl.program_id(0) % nb) > 0
        mc, mp = _att_masks(has_prev)
        for h in range(ATT_HPG):
            sl = slice(h * E, (h + 1) * E)
            q = q_ref[:, sl]
            sc = jnp.where(mc, _dot_nt(q, kc_ref[:, sl]), NEG_BIG)
            sp = jnp.where(mp, _dot_nt(q, kp_ref[:, sl]), NEG_BIG)
            m = jnp.maximum(jnp.max(sc, axis=-1, keepdims=True), jnp.max(sp, axis=-1, keepdims=True))
            pc = jnp.exp(sc - m)
            pp = jnp.exp(sp - m)
            den = jnp.sum(pc, axis=-1, keepdims=True) + jnp.sum(pp, axis=-1, keepdims=True)
            o = (_dot(pc, vc_ref[:, sl]) + _dot(pp, vp_ref[:, sl])) / den
            o_ref[:, sl] = o
            l_ref[:, sl] = jnp.broadcast_to(m + jnp.log(den), (B, E))

    cur = pl.BlockSpec((B, W), lambda j: (j, 0))
    prev = pl.BlockSpec((B, W), lambda j: (jnp.maximum(j - 1, 0), 0))
    return pl.pallas_call(body, grid=(nblk,), in_specs=[cur, cur, prev, cur, prev], out_specs=[cur, cur],
                          out_shape=[jax.ShapeDtypeStruct((L, W), F32)] * 2,
                          compiler_params=_params(dimension_semantics=("parallel",)), name=name)(
        qp, kp, kp, vp, vp)


def attn_bwd(qp, kp, vp, lse, do, dl, d, *, name):
    L, W = qp.shape
    B, E = ATT_BLOCK, ATT_E
    nblk = L // B
    nb = nblk // d

    def body(q_ref, kc_ref, kp_ref, vc_ref, vp_ref, l_ref, do_ref, dl_ref, dq_ref, dk_ref, dv_ref, tkc, tkp, tvc, tvp):
        j = pl.program_id(0)

        @pl.when(j == 0)
        def _():
            dk_ref[...] = jnp.zeros_like(dk_ref)
            dv_ref[...] = jnp.zeros_like(dv_ref)

        has_prev = (j % nb) > 0
        mc, mp = _att_masks(has_prev)
        for h in range(ATT_HPG):
            sl = slice(h * E, (h + 1) * E)
            q = q_ref[:, sl]
            kc, kpv, vc, vpv = kc_ref[:, sl], kp_ref[:, sl], vc_ref[:, sl], vp_ref[:, sl]
            lse_h = l_ref[:, h * E:h * E + 1]
            dl_h = dl_ref[:, h * E:h * E + 1]
            doh = do_ref[:, sl]
            pc = jnp.where(mc, jnp.exp(_dot_nt(q, kc) - lse_h), 0.0)
            pp = jnp.where(mp, jnp.exp(_dot_nt(q, kpv) - lse_h), 0.0)
            dsc = pc * (_dot_nt(doh, vc) - dl_h)
            dsp = pp * (_dot_nt(doh, vpv) - dl_h)
            dq_ref[:, sl] = _dot(dsc, kc) + _dot(dsp, kpv)
            tkc[:, sl] = _dot_tn(dsc, q)
            tkp[:, sl] = _dot_tn(dsp, q)
            tvc[:, sl] = _dot_tn(pc, doh)
            tvp[:, sl] = _dot_tn(pp, doh)
        cur = pl.multiple_of(j * B, B)
        prv = pl.multiple_of(jnp.maximum(j - 1, 0) * B, B)
        dk_ref[pl.ds(cur, B), :] += tkc[...]
        dv_ref[pl.ds(cur, B), :] += tvc[...]
        dk_ref[pl.ds(prv, B), :] += tkp[...]
        dv_ref[pl.ds(prv, B), :] += tvp[...]

    cur = pl.BlockSpec((B, W), lambda j: (j, 0))
    prev = pl.BlockSpec((B, W), lambda j: (jnp.maximum(j - 1, 0), 0))
    full = pl.BlockSpec((L, W), lambda j: (0, 0))
    return pl.pallas_call(body, grid=(nblk,), in_specs=[cur, cur, prev, cur, prev, cur, cur, cur],
                          out_specs=[cur, full, full], out_shape=[jax.ShapeDtypeStruct((L, W), F32)] * 3,
                          scratch_shapes=[pltpu.VMEM((B, W), F32)] * 4,
                          compiler_params=_params(dimension_semantics=("arbitrary",)), name=name)(
        qp, kp, kp, vp, vp, lse, do, dl)


def _merge_alpha(l_refs):
    ls = [r[...] for r in l_refs]
    m = jnp.maximum(jnp.maximum(ls[0], ls[1]), ls[2])
    es = [jnp.exp(l - m) for l in ls]
    den = es[0] + es[1] + es[2]
    return [e / den for e in es]


def merge_fwd(os_, ls_, *, name):
    L, W = os_[0].shape
    tr = _pick(L, (256, 128))

    def body(o0, o1, o2, l0, l1, l2, out_ref):
        al = _merge_alpha((l0, l1, l2))
        out_ref[...] = (al[0] * o0[...] + al[1] * o1[...] + al[2] * o2[...]).astype(out_ref.dtype)

    row = pl.BlockSpec((tr, W), lambda i: (i, 0))
    return pl.pallas_call(body, grid=(L // tr,), in_specs=[row] * 6, out_specs=row,
                          out_shape=jax.ShapeDtypeStruct((L, W), BF16), name=name)(*os_, *ls_)


def merge_bwd(os_, ls_, do, *, name):
    L, W = do.shape
    tr = _pick(L, (256, 128))

    def body(o0, o1, o2, l0, l1, l2, do_ref, d0, d1, d2, e0, e1, e2):
        al = _merge_alpha((l0, l1, l2))
        dov = do_ref[...]
        r = lax.broadcasted_iota(jnp.int32, (W, W), 0) // ATT_E
        c = lax.broadcasted_iota(jnp.int32, (W, W), 1) // ATT_E
        ones_blk = (r == c).astype(F32)
        t = jnp.zeros_like(dov)
        for a, o in zip(al, (o0, o1, o2)):
            t = t + a * _dot_f32(dov * o[...], ones_blk)
        for a, d_ref, e_ref in zip(al, (d0, d1, d2), (e0, e1, e2)):
            d_ref[...] = a * dov
            e_ref[...] = a * t

    row = pl.BlockSpec((tr, W), lambda i: (i, 0))
    return pl.pallas_call(body, grid=(L // tr,), in_specs=[row] * 7, out_specs=[row] * 6,
                          out_shape=[jax.ShapeDtypeStruct((L, W), F32)] * 6, name=name)(*os_, *ls_, do)


def _me_and_peers():
    x, y, c = lax.axis_index("x"), lax.axis_index("y"), lax.axis_index("c")
    peers = []
    for k in range(1, N_DEV):
        px = 1 - x if k & 4 else x
        py = 1 - y if k & 2 else y
        pc = 1 - c if k & 1 else c
        peers.append((px, py, pc))
    return (x, y, c), peers


def _index(dev):
    return 4 * dev[0] + 2 * dev[1] + dev[2]


def _hbm(a):
    return pltpu.with_memory_space_constraint(a, pltpu.HBM)


HBM_SPEC = pl.BlockSpec(memory_space=pltpu.HBM)
SEM_SPEC = pl.BlockSpec(memory_space=pltpu.SEMAPHORE)
DATAFLOW = pltpu.SideEffectType.DATAFLOW_SIDE_EFFECTING


def _remote(src_ref, land_ref, slotted, me, peer, src_is_mine, send_sem, recv_sem, k):
    sender, receiver = (me, peer) if src_is_mine else (peer, me)
    src = src_ref.at[_index(receiver)] if slotted else src_ref
    return pltpu.make_async_remote_copy(src_ref=src, dst_ref=land_ref.at[_index(sender)], send_sem=send_sem.at[k],
                                        recv_sem=recv_sem.at[k], device_id=peer, device_id_type=MESH_ID)


def copies_start(arrays, slotted, *, name):
    n = len(arrays)
    lands = [lax.empty(a.shape if slotted else (N_DEV,) + a.shape, a.dtype) for a in arrays]

    def body(*refs):
        x_refs, land_refs = refs[:n], refs[n:2 * n]
        send, recv = refs[2 * n:3 * n], refs[3 * n:4 * n]
        token = refs[-1]
        me, peers = _me_and_peers()
        for w in range(n):
            for k, peer in enumerate(peers):
                _remote(x_refs[w], land_refs[w], slotted, me, peer, True, send[w], recv[w], k).start()
        token[...] = jnp.zeros_like(token)

    sem = pltpu.SemaphoreType.DMA((N_DEV - 1,))
    out_shape = ([sem] * (2 * n) + [pltpu.HBM(a.shape, a.dtype) for a in arrays]
                 + [pltpu.HBM(l.shape, l.dtype) for l in lands] + [jax.ShapeDtypeStruct((8, 128), F32)])
    outs = pl.pallas_call(
        body, name=name, out_shape=out_shape, in_specs=[HBM_SPEC] * (2 * n),
        out_specs=[SEM_SPEC] * (2 * n) + [HBM_SPEC] * (2 * n) + [pl.BlockSpec(memory_space=pltpu.VMEM)],
        input_output_aliases={i: 2 * n + i for i in range(2 * n)},
        compiler_params=pltpu.CompilerParams(has_side_effects=DATAFLOW),
    )(*[_hbm(a) for a in arrays], *[_hbm(l) for l in lands])
    handles = [(outs[w], outs[n + w], outs[2 * n + w], outs[3 * n + w]) for w in range(n)]
    return handles, outs[-1]


def copies_wait(handle, slotted, after, *, name):
    send_sem, recv_sem, x_thru, land_thru = handle

    def body(x_ref, land_ref, send_ref, recv_ref, after_ref, x_out, land_out):
        me, peers = _me_and_peers()
        for k, peer in enumerate(peers):
            _remote(x_ref, land_ref, slotted, me, peer, True, send_ref, recv_ref, k).wait_send()
        for k, peer in enumerate(peers):
            _remote(x_ref, land_ref, slotted, me, peer, False, send_ref, recv_ref, k).wait_recv()

    return pl.pallas_call(
        body, name=name, out_shape=(pltpu.HBM(x_thru.shape, x_thru.dtype), pltpu.HBM(land_thru.shape, land_thru.dtype)),
        in_specs=(HBM_SPEC, HBM_SPEC, SEM_SPEC, SEM_SPEC, pl.BlockSpec(memory_space=pl.ANY)),
        out_specs=(HBM_SPEC, HBM_SPEC), input_output_aliases={0: 0, 1: 1},
        compiler_params=pltpu.CompilerParams(has_side_effects=DATAFLOW),
    )(x_thru, land_thru, send_sem, recv_sem, after)


def cast_bf16(x, *, name):
    R, C = x.shape
    tr = _pick(R, (512, 256, 128, 64))

    def body(x_ref, o_ref):
        o_ref[...] = x_ref[...].astype(BF16)

    row = pl.BlockSpec((tr, C), lambda i: (i, 0))
    return pl.pallas_call(body, grid=(R // tr,), in_specs=[row], out_specs=row,
                          out_shape=jax.ShapeDtypeStruct((R, C), BF16), name=name)(x)


def _my_index():
    return 4 * lax.axis_index("x") + 2 * lax.axis_index("y") + lax.axis_index("c")


def cols_from_shards(g, own, *, name):
    _, K, n = g.shape
    tk = _pick(K, (256, 128))

    def body(g_ref, own_ref, o_ref):
        me = _my_index()
        for i in range(N_DEV):
            @pl.when(me == i)
            def _():
                o_ref[:, i * n:(i + 1) * n] = own_ref[...]

            @pl.when(me != i)
            def _():
                o_ref[:, i * n:(i + 1) * n] = g_ref[i]

    return pl.pallas_call(body, grid=(K // tk,),
                          in_specs=[pl.BlockSpec((N_DEV, tk, n), lambda i: (0, i, 0)), pl.BlockSpec((tk, n), lambda i: (i, 0))],
                          out_specs=pl.BlockSpec((tk, N_DEV * n), lambda i: (i, 0)),
                          out_shape=jax.ShapeDtypeStruct((K, N_DEV * n), g.dtype), name=name)(g, own)


def rows_from_shards(g, own, *, name):
    _, k, N = g.shape

    def body(g_ref, own_ref, o_ref):
        mine = _my_index() == pl.program_id(0)

        @pl.when(mine)
        def _():
            o_ref[...] = own_ref[...]

        @pl.when(jnp.logical_not(mine))
        def _():
            o_ref[...] = g_ref[...]

    return pl.pallas_call(body, grid=(N_DEV,),
                          in_specs=[pl.BlockSpec((None, k, N), lambda i: (i, 0, 0)), pl.BlockSpec((k, N), lambda i: (0, 0))],
                          out_specs=pl.BlockSpec((k, N), lambda i: (i, 0)),
                          out_shape=jax.ShapeDtypeStruct((N_DEV * k, N), g.dtype), name=name)(g, own)


def shards_from_cols(w, *, name):
    K, N = w.shape
    n = N // N_DEV
    tk = _pick(K, (256, 128))

    def body(w_ref, o_ref):
        for i in range(N_DEV):
            o_ref[i] = w_ref[:, i * n:(i + 1) * n].astype(o_ref.dtype)

    return pl.pallas_call(body, grid=(K // tk,), in_specs=[pl.BlockSpec((tk, N), lambda i: (i, 0))],
                          out_specs=pl.BlockSpec((N_DEV, tk, n), lambda i: (0, i, 0)),
                          out_shape=jax.ShapeDtypeStruct((N_DEV, K, n), BF16), name=name)(w)


def _adamw(w, g, m, v):
    m = ADAM_B1 * m + (1.0 - ADAM_B1) * g
    v = ADAM_B2 * v + (1.0 - ADAM_B2) * (g * g)
    m_hat = m / (1.0 - ADAM_B1 ** ADAM_STEP)
    v_hat = v / (1.0 - ADAM_B2 ** ADAM_STEP)
    delta = -ADAM_LR * (m_hat / (jnp.sqrt(v_hat) + ADAM_EPS) + ADAM_WD * w)
    return delta, m, v


def reduce_adamw(recv, own, own_slotted, me, w, m, v, *, name):
    _, R, C = recv.shape
    tr = _pick(R, (256, 128, 64, 32, 16, 8))

    def body(me_ref, r_ref, own_ref, w_ref, m_ref, v_ref, g_ref, d_ref, nm_ref, nv_ref):
        mine = me_ref[0]
        g = None
        for i in range(N_DEV):
            part = jnp.where(mine == i, own_ref[...], r_ref[i]).astype(F32)
            g = part if g is None else g + part
        delta, nm, nv = _adamw(w_ref[...], g, m_ref[...], v_ref[...])
        g_ref[...] = g
        d_ref[...] = delta
        nm_ref[...] = nm
        nv_ref[...] = nv

    row = pl.BlockSpec((tr, C), lambda i, me_ref: (i, 0))
    own_spec = pl.BlockSpec((None, tr, C), lambda i, me_ref: (me_ref[0], i, 0)) if own_slotted else row
    grid_spec = pltpu.PrefetchScalarGridSpec(
        num_scalar_prefetch=1, grid=(R // tr,),
        in_specs=[pl.BlockSpec((N_DEV, tr, C), lambda i, me_ref: (0, i, 0)), own_spec, row, row, row],
        out_specs=[row] * 4)
    return pl.pallas_call(body, grid_spec=grid_spec, out_shape=[jax.ShapeDtypeStruct((R, C), F32)] * 4,
                          compiler_params=_params(dimension_semantics=("parallel",)), name=name)(
        me.reshape(1).astype(jnp.int32), recv, own, w, m, v)


def _s5_prepare(A_re, A_im, log_dt, B_re, B_im, C_re, C_im):
    G, P, Cg = S5_GROUPS, S5_STATE, S5_GROUP
    dt = jnp.exp(log_dt)[:, None]
    mag = jnp.exp(A_re * dt)
    ab_re = mag * jnp.cos(A_im * dt)
    ab_im = mag * jnp.sin(A_im * dt)
    den = A_re * A_re + A_im * A_im
    nr, ni = ab_re - 1.0, ab_im
    c_re = (nr * A_re + ni * A_im) / den
    c_im = (ni * A_re - nr * A_im) / den
    Bb_re = c_re[..., None] * B_re - c_im[..., None] * B_im
    Bb_im = c_re[..., None] * B_im + c_im[..., None] * B_re
    eye = jnp.eye(G, dtype=F32)

    def dense_in(b):
        return jnp.einsum('gpc,gh->gchp', b, eye).reshape(G * Cg, G * P)

    def dense_out(c):
        return jnp.einsum('gcp,gh->gphc', c, eye).reshape(G * P, G * Cg)

    return (ab_re.reshape(1, G * P), ab_im.reshape(1, G * P), dense_in(Bb_re), dense_in(Bb_im),
            dense_out(C_re), dense_out(-C_im))


def _lower_bound(gamma):
    return jnp.cumsum(jax.nn.softmax(gamma, axis=0), axis=0)[0:1]


def _ffn_fwd(h, g_norm, get_w_in, conv_w, conv_b, get_w_out, tag):
    hn = rms_fwd(h, g_norm, name=tag + "_rms")
    w_in = get_w_in(hn)
    hu = mm(hn, w_in, name=tag + "_in")
    act = convgate_fwd(hu, conv_w, conv_b, name=tag + "_gate")
    w_out = get_w_out(act)
    h_out = mm(act, w_out, res=h, name=tag + "_out")
    return h_out, (hn, hu, act), w_in, w_out


def _ffn_bwd(h, g_norm, w_in, conv_w, conv_b, w_out, saved, dh, tag, send_dw_in, send_dw_out):
    hn, hu, act = saved
    sent = send_dw_out(mm(act, dh, ta=True, name=tag + "_dwout"))
    dact = mm(dh, w_out, tb=True, dep=sent, name=tag + "_dact")
    dhu, dconv_w, dconv_b = convgate_bwd(hu, conv_w, conv_b, dact, name=tag + "_dgate")
    sent = send_dw_in(mm(hn, dhu, ta=True, name=tag + "_dwin"))
    dhn = mm(dhu, w_in, tb=True, dep=sent, name=tag + "_dhn")
    dh_in, dg = rms_bwd(h, g_norm, dhn, dh, name=tag + "_drms")
    return dh_in, dg, dconv_w, dconv_b


def kernel(x, positions, norm_mix, norm_ffn, norm_final, mix_w_in, mix_w_out, s5_A_re, s5_A_im, s5_log_dt, s5_B_re, s5_B_im, s5_C_re, s5_C_im, s5_D, s5_glu_w, s5_glu_b, hgrn_gamma, hgrn_norm, att_w_qkv, att_w_o, ffn_w_in, ffn_conv_w, ffn_conv_b, ffn_w_out, loss_target, m_norm_mix, m_norm_ffn, m_norm_final, m_mix_w_in, m_mix_w_out, m_s5_A_re, m_s5_A_im, m_s5_log_dt, m_s5_B_re, m_s5_B_im, m_s5_C_re, m_s5_C_im, m_s5_D, m_s5_glu_w, m_s5_glu_b, m_hgrn_gamma, m_hgrn_norm, m_att_w_qkv, m_att_w_o, m_ffn_w_in, m_ffn_conv_w, m_ffn_conv_b, m_ffn_w_out, v_norm_mix, v_norm_ffn, v_norm_final, v_mix_w_in, v_mix_w_out, v_s5_A_re, v_s5_A_im, v_s5_log_dt, v_s5_B_re, v_s5_B_im, v_s5_C_re, v_s5_C_im, v_s5_D, v_s5_glu_w, v_s5_glu_b, v_hgrn_gamma, v_hgrn_norm, v_att_w_qkv, v_att_w_o, v_ffn_w_in, v_ffn_conv_w, v_ffn_conv_b, v_ffn_w_out):
    W = dict(norm_mix=norm_mix, norm_ffn=norm_ffn, norm_final=norm_final, mix_w_in=mix_w_in, mix_w_out=mix_w_out,
             s5_A_re=s5_A_re, s5_A_im=s5_A_im, s5_log_dt=s5_log_dt, s5_B_re=s5_B_re, s5_B_im=s5_B_im,
             s5_C_re=s5_C_re, s5_C_im=s5_C_im, s5_D=s5_D, s5_glu_w=s5_glu_w, s5_glu_b=s5_glu_b,
             hgrn_gamma=hgrn_gamma, hgrn_norm=hgrn_norm, att_w_qkv=att_w_qkv, att_w_o=att_w_o, ffn_w_in=ffn_w_in,
             ffn_conv_w=ffn_conv_w, ffn_conv_b=ffn_conv_b, ffn_w_out=ffn_w_out)
    M = dict(norm_mix=m_norm_mix, norm_ffn=m_norm_ffn, norm_final=m_norm_final, mix_w_in=m_mix_w_in,
             mix_w_out=m_mix_w_out, s5_A_re=m_s5_A_re, s5_A_im=m_s5_A_im, s5_log_dt=m_s5_log_dt, s5_B_re=m_s5_B_re,
             s5_B_im=m_s5_B_im, s5_C_re=m_s5_C_re, s5_C_im=m_s5_C_im, s5_D=m_s5_D, s5_glu_w=m_s5_glu_w,
             s5_glu_b=m_s5_glu_b, hgrn_gamma=m_hgrn_gamma, hgrn_norm=m_hgrn_norm, att_w_qkv=m_att_w_qkv,
             att_w_o=m_att_w_o, ffn_w_in=m_ffn_w_in, ffn_conv_w=m_ffn_conv_w, ffn_conv_b=m_ffn_conv_b,
             ffn_w_out=m_ffn_w_out)
    V = dict(norm_mix=v_norm_mix, norm_ffn=v_norm_ffn, norm_final=v_norm_final, mix_w_in=v_mix_w_in,
             mix_w_out=v_mix_w_out, s5_A_re=v_s5_A_re, s5_A_im=v_s5_A_im, s5_log_dt=v_s5_log_dt, s5_B_re=v_s5_B_re,
             s5_B_im=v_s5_B_im, s5_C_re=v_s5_C_re, s5_C_im=v_s5_C_im, s5_D=v_s5_D, s5_glu_w=v_s5_glu_w,
             s5_glu_b=v_s5_glu_b, hgrn_gamma=v_hgrn_gamma, hgrn_norm=v_hgrn_norm, att_w_qkv=v_att_w_qkv,
             att_w_o=v_att_w_o, ffn_w_in=v_ffn_w_in, ffn_conv_w=v_ffn_conv_w, ffn_conv_b=v_ffn_conv_b,
             ffn_w_out=v_ffn_w_out)
    return _step(x[0], positions[0], loss_target[0], W, M, V)


BIG = ("mix_w_in", "mix_w_out", "s5_glu_w", "att_w_qkv", "att_w_o", "ffn_w_in", "ffn_w_out")
COL_SHARDED = ("mix_w_in", "att_w_qkv", "att_w_o", "ffn_w_in")
SMALL = ("norm_mix", "norm_ffn", "norm_final", "s5_A_re", "s5_A_im", "s5_log_dt", "s5_B_re", "s5_B_im", "s5_C_re",
         "s5_C_im", "s5_D", "s5_glu_b", "hgrn_gamma", "hgrn_norm", "ffn_conv_b")
ORDER = ("norm_mix", "norm_ffn", "norm_final", "mix_w_in", "mix_w_out", "s5_A_re", "s5_A_im", "s5_log_dt", "s5_B_re",
         "s5_B_im", "s5_C_re", "s5_C_im", "s5_D", "s5_glu_w", "s5_glu_b", "hgrn_gamma", "hgrn_norm", "att_w_qkv",
         "att_w_o", "ffn_w_in", "ffn_conv_w", "ffn_conv_b", "ffn_w_out")
PACK_COLS = 1024


def _step(x, positions, target, W, M, V):
    L, D = x.shape
    me = 4 * lax.axis_index("x") + 2 * lax.axis_index("y") + lax.axis_index("c")
    n_cw = W["ffn_conv_w"].shape[-1]
    shards = {
        "mix_w_in": cast_bf16(W["mix_w_in"][0], name="mix_w_in_cast"),
        "conv_w": W["ffn_conv_w"].reshape(6, n_cw),
        "s5_glu_w": cast_bf16(W["s5_glu_w"][0], name="s5_glu_w_cast"),
        "mix_w_out": cast_bf16(W["mix_w_out"][0], name="mix_w_out_cast"),
        "ffn_w_in0": cast_bf16(W["ffn_w_in"][0], name="ffn_w_in0_cast"),
        "ffn_w_out0": cast_bf16(W["ffn_w_out"][0], name="ffn_w_out0_cast"),
        "att_w_qkv": cast_bf16(W["att_w_qkv"][0], name="att_w_qkv_cast"),
        "att_w_o": cast_bf16(W["att_w_o"][0], name="att_w_o_cast"),
        "ffn_w_in1": cast_bf16(W["ffn_w_in"][1], name="ffn_w_in1_cast"),
        "ffn_w_out1": cast_bf16(W["ffn_w_out"][1], name="ffn_w_out1_cast"),
    }
    gather_handles, token = copies_start(list(shards.values()), False, name="gather_start")
    gather_handle = dict(zip(shards, gather_handles))

    def gathered(key, after, cols):
        own, land = copies_wait(gather_handle[key], False, after, name=key + "_gwait")
        return (cols_from_shards if cols else rows_from_shards)(land, own, name=key + "_asm")

    conv_b = W["ffn_conv_b"].reshape(2, 1, -1)

    s5_params = (W["s5_A_re"][0], W["s5_A_im"][0], W["s5_log_dt"][0], W["s5_B_re"][0], W["s5_B_im"][0],
                 W["s5_C_re"][0], W["s5_C_im"][0])
    (a_re, a_im, wb_re, wb_im, wc_re, wc_im), s5_prep_vjp = jax.vjp(_s5_prepare, *s5_params)
    dvec = W["s5_D"].reshape(1, S5_WIDTH)
    glu_b = W["s5_glu_b"].reshape(1, S5_WIDTH)
    lb, lb_vjp = jax.vjp(_lower_bound, W["hgrn_gamma"])
    hg_norm = W["hgrn_norm"].reshape(1, -1)
    tabs = rope_tables(positions)

    hn0 = rms_fwd(x, W["norm_mix"][0], dep=token, name="l0_rms")
    w_mix_in = gathered("mix_w_in", hn0, True)
    proj = mm(hn0, w_mix_in, name="l0_proj")
    u_bf = cast_bf16(proj[:, :S5_WIDTH], name="l0_u_cast")
    bu_re = mm(u_bf, wb_re, name="s5_bu_re")
    bu_im = mm(u_bf, wb_im, name="s5_bu_im")
    xs_re, xs_im = s5_scan_fwd(a_re, a_im, bu_re, bu_im, name="s5_scan")
    y0 = mm(xs_im, wc_im, res=mm(xs_re, wc_re, name="s5_y_re"), name="s5_y_im")
    w_glu = gathered("s5_glu_w", y0, False)
    oa = s5_out_fwd(y0, proj, dvec, w_glu, glu_b, name="s5_out")
    ob, hg_states = hgrn_fwd(proj, lb, hg_norm, name="hgrn_fwd")
    cat = jnp.concatenate([oa, ob], axis=1)
    w_mix_out = gathered("mix_w_out", cat, False)
    h1 = mm(cat, w_mix_out, res=x, name="l0_mix_out")
    cw_own, cw_land = copies_wait(gather_handle["conv_w"], False, h1, name="conv_w_gwait")
    cw_all = lax.dynamic_update_slice(cw_land, cw_own[None], (me, 0, 0))
    conv_w = cw_all.transpose(1, 0, 2).reshape(2, 3, N_DEV * n_cw)
    w_ffn_in, w_ffn_out = [None, None], [None, None]
    h2, ffn0_saved, w_ffn_in[0], w_ffn_out[0] = _ffn_fwd(
        h1, W["norm_ffn"][0], lambda a: gathered("ffn_w_in0", a, True), conv_w[0], conv_b[0],
        lambda a: gathered("ffn_w_out0", a, False), "ffn0")

    hn2 = rms_fwd(h2, W["norm_mix"][1], name="l1_rms")
    w_qkv = gathered("att_w_qkv", hn2, True)
    qkv = mm(hn2, w_qkv, name="l1_qkv")
    qkv_r = rope_fwd(qkv, tabs, name="rope_fwd")
    att_in, att_o, att_l = [], [], []
    for g, d in enumerate(ATT_DILATIONS):
        qp = _to_branch_order(qkv_r[:, 512 * g:512 * (g + 1)], d)
        kp = _to_branch_order(qkv_r[:, 1536 + 512 * g:1536 + 512 * (g + 1)], d)
        vp = _to_branch_order(qkv_r[:, 3072 + 512 * g:3072 + 512 * (g + 1)], d)
        o_p, l_p = attn_fwd(qp, kp, vp, d, name=f"attn_fwd{g}")
        att_in.append((qp, kp, vp, l_p))
        att_o.append(_to_token_order(o_p, d))
        att_l.append(_to_token_order(l_p, d))
    o_att = merge_fwd(att_o, att_l, name="merge_fwd")
    w_o = gathered("att_w_o", o_att, True)
    h3 = mm(o_att, w_o, res=h2, name="l1_mix_out")
    h4, ffn1_saved, w_ffn_in[1], w_ffn_out[1] = _ffn_fwd(
        h3, W["norm_ffn"][1], lambda a: gathered("ffn_w_in1", a, True), conv_w[1], conv_b[1],
        lambda a: gathered("ffn_w_out1", a, False), "ffn1")

    exchanges = {}

    def send_grad(key, g, cols):
        if cols:
            parts = shards_from_cols(g, name=key + "_split")
        else:
            parts = cast_bf16(g, name=key + "_gcast").reshape(N_DEV, g.shape[0] // N_DEV, g.shape[1])
        (handle,), sent = copies_start([parts], True, name=key + "_xstart")
        exchanges[key] = handle
        return sent

    loss, dh4, dg_final = final_loss(h4, W["norm_final"], target, name="final_loss")
    dh3, dg_ffn1, dcw1, dcb1 = _ffn_bwd(h3, W["norm_ffn"][1], w_ffn_in[1], conv_w[1], conv_b[1], w_ffn_out[1],
                                        ffn1_saved, dh4, "ffn1", lambda g: send_grad("ffn_w_in1", g, True),
                                        lambda g: send_grad("ffn_w_out1", g, False))
    sent = send_grad("att_w_o", mm(o_att, dh3, ta=True, name="l1_dwo"), True)
    d_oatt = mm(dh3, w_o, tb=True, dep=sent, name="l1_dmix")
    mb = merge_bwd(att_o, att_l, d_oatt, name="merge_bwd")
    dq_t, dk_t, dv_t = [], [], []
    for g, d in enumerate(ATT_DILATIONS):
        qp, kp, vp, l_p = att_in[g]
        dq_p, dk_p, dv_p = attn_bwd(qp, kp, vp, l_p, _to_branch_order(mb[g], d), _to_branch_order(mb[3 + g], d), d,
                                    name=f"attn_bwd{g}")
        dq_t.append(_to_token_order(dq_p, d))
        dk_t.append(_to_token_order(dk_p, d))
        dv_t.append(_to_token_order(dv_p, d))
    d_qkv = rope_bwd(jnp.concatenate(dq_t, axis=1), jnp.concatenate(dk_t, axis=1), jnp.concatenate(dv_t, axis=1),
                     tabs, name="rope_bwd")
    sent = send_grad("att_w_qkv", mm(hn2, d_qkv, ta=True, name="l1_dwqkv"), True)
    d_hn2 = mm(d_qkv, w_qkv, tb=True, dep=sent, name="l1_dhn")
    dh2, dg_mix1 = rms_bwd(h2, W["norm_mix"][1], d_hn2, dh3, name="l1_drms")

    dh1, dg_ffn0, dcw0, dcb0 = _ffn_bwd(h1, W["norm_ffn"][0], w_ffn_in[0], conv_w[0], conv_b[0], w_ffn_out[0],
                                        ffn0_saved, dh2, "ffn0", lambda g: send_grad("ffn_w_in0", g, True),
                                        lambda g: send_grad("ffn_w_out0", g, False))
    sent = send_grad("mix_w_out", mm(cat, dh1, ta=True, name="l0_dwout"), False)
    dcat = mm(dh1, w_mix_out, tb=True, dep=sent, name="l0_dcat")
    d_hg, dlb, dhg_norm = hgrn_bwd(proj, lb, hg_norm, hg_states, dcat, name="hgrn_bwd")
    dy, du_d, z_bf, dzg, dglu_b, dD = s5_out_bwd(y0, proj, dvec, w_glu, glu_b, dcat, name="s5_dout")
    sent = send_grad("s5_glu_w", mm(z_bf, dzg, ta=True, name="s5_dglu"), False)
    dxs_re = mm(dy, wc_re, tb=True, dep=sent, name="s5_dxs_re")
    dxs_im = mm(dy, wc_im, tb=True, name="s5_dxs_im")
    dwc_re = mm(xs_re, dy, ta=True, name="s5_dwc_re")
    dwc_im = mm(xs_im, dy, ta=True, name="s5_dwc_im")
    dbu_re, dbu_im, da_re, da_im = s5_scan_bwd(a_re, a_im, xs_re, xs_im, dxs_re, dxs_im, name="s5_dscan")
    du = mm(dbu_im, wb_im, tb=True, res=mm(dbu_re, wb_re, tb=True, res=du_d, name="s5_du_re"), out_dtype=BF16,
            name="s5_du_im")
    dwb_re = mm(u_bf, dbu_re, ta=True, name="s5_dwb_re")
    dwb_im = mm(u_bf, dbu_im, ta=True, name="s5_dwb_im")
    s5_small = s5_prep_vjp((da_re, da_im, dwb_re, dwb_im, dwc_re, dwc_im))
    d_proj = jnp.concatenate([du, d_hg], axis=1)
    sent = send_grad("mix_w_in", mm(hn0, d_proj, ta=True, name="l0_dwin"), True)
    d_hn0 = mm(d_proj, w_mix_in, tb=True, dep=sent, name="l0_dhn")
    grad_x, dg_mix0 = rms_bwd(x, W["norm_mix"][0], d_hn0, dh1, name="l0_drms")
    (d_gamma,) = lb_vjp(dlb)
    out = {}

    dA_re, dA_im, dlog_dt, dB_re, dB_im, dC_re, dC_im = s5_small
    small_g = dict(norm_mix=jnp.concatenate([dg_mix0, dg_mix1], axis=0), norm_ffn=jnp.concatenate([dg_ffn0, dg_ffn1], axis=0),
                   norm_final=dg_final, s5_A_re=dA_re, s5_A_im=dA_im, s5_log_dt=dlog_dt, s5_B_re=dB_re, s5_B_im=dB_im,
                   s5_C_re=dC_re, s5_C_im=dC_im, s5_D=dD, s5_glu_b=dglu_b, hgrn_gamma=d_gamma, hgrn_norm=dhg_norm,
                   ffn_conv_b=jnp.concatenate([dcb0, dcb1], axis=0))
    conv_w_g = jnp.stack([dcw0, dcw1], axis=0)
    sizes = [math.prod(W[n].shape) for n in SMALL]
    n_conv = conv_w_g.size
    total = sum(sizes) + n_conv + 1
    rows = -(-total // PACK_COLS)
    rows = -(-rows // 8) * 8
    pad = rows * PACK_COLS - total

    def pack(vals, conv_part, last):
        flat = [v.reshape(-1).astype(F32) for v in vals] + [conv_part.reshape(-1), last.reshape(-1),
                                                            jnp.zeros((pad,), F32)]
        return jnp.concatenate(flat).reshape(rows, PACK_COLS)

    def conv_full(shard):
        full = jnp.zeros((2, 3, N_DEV * n_cw), F32)
        return lax.dynamic_update_slice(full, shard, (0, 0, me * n_cw))

    zero1 = jnp.zeros((1,), F32)
    g_pack = pack([small_g[n] for n in SMALL], conv_w_g, loss)
    w_pack = pack([W[n] for n in SMALL], conv_full(W["ffn_conv_w"]), zero1)
    m_pack = pack([M[n] for n in SMALL], conv_full(M["ffn_conv_w"]), zero1)
    v_pack = pack([V[n] for n in SMALL], conv_full(V["ffn_conv_w"]), zero1 + 1.0)
    (small_handle,), small_sent = copies_start([g_pack], False, name="small_xstart")

    def finish(key, w, m, v):
        own, recv = copies_wait(exchanges[key], True, small_sent, name=key + "_xwait")
        _, R, Cn = recv.shape
        return reduce_adamw(recv, own, True, me, w.reshape(R, Cn), m.reshape(R, Cn), v.reshape(R, Cn),
                            name=key + "_adamw")

    for name in ("ffn_w_out", "ffn_w_in"):
        per_layer = [finish(f"{name}{l}", W[name][l], M[name][l], V[name][l]) for l in (1, 0)][::-1]
        out[name] = tuple(jnp.stack([per_layer[0][k], per_layer[1][k]], axis=0) for k in range(4))
    for name in ("att_w_o", "att_w_qkv", "mix_w_out", "s5_glu_w", "mix_w_in"):
        out[name] = tuple(r.reshape(W[name].shape) for r in finish(name, W[name], M[name], V[name]))

    small_own, small_recv = copies_wait(small_handle, False, out["mix_w_in"][0], name="small_xwait")
    res = reduce_adamw(small_recv, small_own, False, me, w_pack, m_pack, v_pack, name="small_adamw")
    flat = [r.reshape(-1) for r in res]
    off = 0
    for n, sz in zip(SMALL, sizes):
        out[n] = tuple(f[off:off + sz].reshape(W[n].shape) for f in flat)
        off += sz
    conv_res = [f[off:off + n_conv].reshape(2, 3, N_DEV * n_cw) for f in flat]
    out["ffn_conv_w"] = tuple(lax.dynamic_slice(c, (0, 0, me * n_cw), (2, 3, n_cw)) for c in conv_res)
    off += n_conv
    loss_total = flat[0][off]

    result = [loss_total, grad_x[None]]
    for k in range(4):
        result += [out[n][k] for n in ORDER]
    return tuple(result)
```

```python
import functools
import math

import jax
import jax.numpy as jnp
from jax import lax
from jax.experimental import pallas as pl
from jax.experimental.pallas import tpu as pltpu

F32 = jnp.float32
BF16 = jnp.bfloat16
MESH_ID = pl.DeviceIdType.MESH
N_DEV = 8
VMEM_LIMIT_BYTES = 56 * 1024 * 1024

NORM_EPS = 1e-6
S5_WIDTH, S5_GROUP, S5_GROUPS, S5_STATE = 512, 16, 32, 64
HG_HEADS, HG_DIM, HG_CHUNK = 4, 128, 64
ATT_E, ATT_HPG, ATT_BLOCK = 64, 8, 128
ATT_DILATIONS = (1, 4, 16)
ROT_DIM, ROPE_THETA = 16, 500000.0
D_FF = 2816
ADAM_LR, ADAM_B1, ADAM_B2, ADAM_EPS, ADAM_WD, ADAM_STEP = 0.001, 0.9, 0.999, 1e-08, 0.01, 10
NEG_BIG = -1e30


def _params(**kw):
    return pltpu.CompilerParams(vmem_limit_bytes=VMEM_LIMIT_BYTES, **kw)


def _pick(n, cands):
    for c in cands:
        if n % c == 0:
            return c
    return n


def _dot(a, b):
    return jnp.dot(a.astype(BF16), b.astype(BF16), preferred_element_type=F32)


def _dot_nt(a, b):
    return lax.dot_general(a.astype(BF16), b.astype(BF16), (((1,), (1,)), ((), ())), preferred_element_type=F32)


def _dot_tn(a, b):
    return lax.dot_general(a.astype(BF16), b.astype(BF16), (((0,), (0,)), ((), ())), preferred_element_type=F32)


def _dot_f32(a, b):
    return jnp.dot(a, b, preferred_element_type=F32, precision=lax.Precision.HIGHEST)


def _dot_f32_nt(a, b):
    return lax.dot_general(a, b, (((1,), (1,)), ((), ())), preferred_element_type=F32, precision=lax.Precision.HIGHEST)


def _dot_f32_tn(a, b):
    return lax.dot_general(a, b, (((0,), (0,)), ((), ())), preferred_element_type=F32, precision=lax.Precision.HIGHEST)


def _sigmoid(x):
    return 1.0 / (1.0 + jnp.exp(-x))


V7X_HBM_BYTES_PER_S = 3.2e12
V7X_MXU_FLOPS_PER_S = 0.7e15
GRID_STEP_S = 0.35e-6
MM_VMEM_BUDGET = 40 * 1024 * 1024


def _divisors(n, cands):
    return [c for c in cands if c <= n and n % c == 0] or [n]


def _mm_tiles(m, n, k, sa, sb, so, sr):
    best = None
    for tm in _divisors(m, (2816, 2048, 1408, 1024, 512, 256, 128)):
        for tn in _divisors(n, (2816, 2048, 1408, 1024, 512, 256, 128)):
            for tk in _divisors(k, (k, 2816, 2560, 2304, 2048, 1536, 1408, 1280, 1024, 512, 256, 128)):
                nk = k // tk
                vmem = 2 * (tm * tk * sa + tk * tn * sb + tm * tn * (so + sr)) + (tm * tn * 4 if nk > 1 else 0)
                vmem += tm * tk * 2 * (sa > 2) + tk * tn * 2 * (sb > 2) + tm * tn * 4
                if vmem > MM_VMEM_BUDGET:
                    continue
                ni, nj = m // tm, n // tn
                for i_outer in (True, False):
                    if i_outer:
                        a_reads = 1 if nk == 1 else nj
                        b_reads = 1 if (nk == 1 and nj == 1) else ni
                    else:
                        b_reads = 1 if nk == 1 else ni
                        a_reads = 1 if (nk == 1 and ni == 1) else nj
                    traffic = a_reads * m * k * sa + b_reads * k * n * sb + m * n * (so + sr)
                    t = max(traffic / V7X_HBM_BYTES_PER_S, 2.0 * m * n * k / V7X_MXU_FLOPS_PER_S)
                    t += ni * nj * nk * GRID_STEP_S
                    t += (tm * tk * sa + tk * tn * sb + tm * tn * so) / V7X_HBM_BYTES_PER_S
                    if best is None or t < best[0]:
                        best = (t, tm, tn, tk, i_outer)
    assert best is not None, (m, n, k)
    return best[1:]


def mm(a, b, *, ta=False, tb=False, res=None, out_dtype=F32, dep=None, name):
    m, k = (a.shape[1], a.shape[0]) if ta else a.shape
    n = b.shape[0] if tb else b.shape[1]
    assert (b.shape[1] if tb else b.shape[0]) == k
    has_res = res is not None
    tm, tn, tk, i_outer = _mm_tiles(m, n, k, a.dtype.itemsize, b.dtype.itemsize, jnp.dtype(out_dtype).itemsize,
                                    res.dtype.itemsize if has_res else 0)
    nk = k // tk
    deps = [] if dep is None else [dep]
    dn = (((0 if ta else 1,), (1 if tb else 0,)), ((), ()))

    def body_single(*refs):
        a_ref, b_ref = refs[:2]
        o_ref = refs[-1]
        out = lax.dot_general(a_ref[...].astype(BF16), b_ref[...].astype(BF16), dn, preferred_element_type=F32)
        if has_res:
            out = out + refs[2][...].astype(F32)
        o_ref[...] = out.astype(o_ref.dtype)

    def body(*refs):
        a_ref, b_ref = refs[:2]
        r_ref = refs[2] if has_res else None
        o_ref, acc_ref = refs[-2:]
        kk = pl.program_id(2)
        part = lax.dot_general(a_ref[...].astype(BF16), b_ref[...].astype(BF16), dn, preferred_element_type=F32)

        @pl.when(kk == 0)
        def _():
            acc_ref[...] = part

        @pl.when(kk > 0)
        def _():
            acc_ref[...] += part

        @pl.when(kk == nk - 1)
        def _():
            out = acc_ref[...]
            if has_res:
                out = out + r_ref[...].astype(F32)
            o_ref[...] = out.astype(o_ref.dtype)

    def ij(f):
        return (lambda g0, g1, q: f(g0, g1, q)) if i_outer else (lambda g0, g1, q: f(g1, g0, q))

    a_spec = pl.BlockSpec((tk, tm), ij(lambda i, j, q: (q, i))) if ta else pl.BlockSpec((tm, tk), ij(lambda i, j, q: (i, q)))
    b_spec = pl.BlockSpec((tn, tk), ij(lambda i, j, q: (j, q))) if tb else pl.BlockSpec((tk, tn), ij(lambda i, j, q: (q, j)))
    o_spec = pl.BlockSpec((tm, tn), ij(lambda i, j, q: (i, j)))
    in_specs = [a_spec, b_spec] + ([o_spec] if has_res else []) + [pl.BlockSpec((8, 128), lambda g0, g1, q: (0, 0))] * len(deps)
    args = (a, b) + ((res,) if has_res else ()) + tuple(deps)
    grid = (m // tm, n // tn, nk) if i_outer else (n // tn, m // tm, nk)
    return pl.pallas_call(
        body_single if nk == 1 else body, grid=grid, in_specs=in_specs, out_specs=o_spec,
        out_shape=jax.ShapeDtypeStruct((m, n), out_dtype),
        scratch_shapes=[] if nk == 1 else [pltpu.VMEM((tm, tn), F32)],
        compiler_params=_params(dimension_semantics=("parallel", "parallel", "arbitrary")), name=name,
    )(*args)


def rms_fwd(x, g, *, dep=None, name):
    L, D = x.shape
    tr = _pick(L, (256, 128))

    def body(x_ref, g_ref, *rest):
        o_ref = rest[-1]
        xv = x_ref[...]
        r = lax.rsqrt(jnp.mean(xv * xv, axis=-1, keepdims=True) + NORM_EPS)
        o_ref[...] = (xv * r * g_ref[...]).astype(o_ref.dtype)

    row = pl.BlockSpec((tr, D), lambda i: (i, 0))
    vec = pl.BlockSpec((1, D), lambda i: (0, 0))
    deps = [] if dep is None else [dep]
    return pl.pallas_call(body, grid=(L // tr,), in_specs=[row, vec] + [pl.BlockSpec((8, 128), lambda i: (0, 0))] * len(deps),
                          out_specs=row, out_shape=jax.ShapeDtypeStruct((L, D), BF16), name=name)(
        x, g.reshape(1, D), *deps)


def rms_bwd(x, g, dy, dres, *, name):
    L, D = x.shape
    tr = _pick(L, (256, 128))

    def body(x_ref, g_ref, dy_ref, dres_ref, dx_ref, dg_ref):
        xv = x_ref[...]
        r = lax.rsqrt(jnp.mean(xv * xv, axis=-1, keepdims=True) + NORM_EPS)
        xh = xv * r
        dyv = dy_ref[...].astype(F32)

        @pl.when(pl.program_id(0) == 0)
        def _():
            dg_ref[...] = jnp.zeros_like(dg_ref)

        dg_ref[...] += jnp.sum(dyv * xh, axis=0, keepdims=True)
        dxh = dyv * g_ref[...]
        dx_ref[...] = dres_ref[...] + r * (dxh - xh * jnp.mean(dxh * xh, axis=-1, keepdims=True))

    row = pl.BlockSpec((tr, D), lambda i: (i, 0))
    vec = pl.BlockSpec((1, D), lambda i: (0, 0))
    return pl.pallas_call(body, grid=(L // tr,), in_specs=[row, vec, row, row], out_specs=[row, vec],
                          out_shape=[jax.ShapeDtypeStruct((L, D), F32), jax.ShapeDtypeStruct((1, D), F32)],
                          compiler_params=_params(dimension_semantics=("arbitrary",)), name=name)(
        x, g.reshape(1, D), dy, dres)


def final_loss(h, g, target, *, name):
    L, D = h.shape
    tr = _pick(L, (256, 128))

    def body(x_ref, g_ref, t_ref, loss_ref, dx_ref, dg_ref):
        xv = x_ref[...]
        gv = g_ref[...]
        r = lax.rsqrt(jnp.mean(xv * xv, axis=-1, keepdims=True) + NORM_EPS)
        xh = xv * r
        err = xh * gv - t_ref[...]

        @pl.when(pl.program_id(0) == 0)
        def _():
            dg_ref[...] = jnp.zeros_like(dg_ref)
            loss_ref[...] = jnp.zeros_like(loss_ref)

        loss_ref[...] += 0.5 * jnp.sum(jnp.mean(err * err, axis=-1, keepdims=True), axis=0, keepdims=True)
        dyv = err * (1.0 / D)
        dg_ref[...] += jnp.sum(dyv * xh, axis=0, keepdims=True)
        dxh = dyv * gv
        dx_ref[...] = r * (dxh - xh * jnp.mean(dxh * xh, axis=-1, keepdims=True))

    row = pl.BlockSpec((tr, D), lambda i: (i, 0))
    vec = pl.BlockSpec((1, D), lambda i: (0, 0))
    one = pl.BlockSpec((1, 1), lambda i: (0, 0))
    return pl.pallas_call(body, grid=(L // tr,), in_specs=[row, vec, row], out_specs=[one, row, vec],
                          out_shape=[jax.ShapeDtypeStruct((1, 1), F32), jax.ShapeDtypeStruct((L, D), F32),
                                     jax.ShapeDtypeStruct((1, D), F32)],
                          compiler_params=_params(dimension_semantics=("arbitrary",)), name=name)(
        h, g.reshape(1, D), target)


def _cmul(ar, ai, br, bi):
    return ar * br - ai * bi, ar * bi + ai * br


def _powers(ar, ai):
    rows = [(ar, ai)]
    for _ in range(7):
        rows.append(_cmul(rows[-1][0], rows[-1][1], ar, ai))
    table = (jnp.concatenate([r[0] for r in rows], axis=0), jnp.concatenate([r[1] for r in rows], axis=0))
    return (rows[0], rows[1], rows[3]), table


def _block_scan(br, bi, steps, shift):
    yr, yi = br, bi
    for s, (pr, pi) in zip((1, 2, 4), steps):
        sr, si = shift(yr, s), shift(yi, s)
        yr, yi = yr + pr * sr - pi * si, yi + pr * si + pi * sr
    return yr, yi


def s5_scan_fwd(a_re, a_im, bu_re, bu_im, *, name):
    L, P = bu_re.shape
    W = _pick(P, (512, 256, 128))

    def body(ar_ref, ai_ref, br_ref, bi_ref, xr_ref, xi_ref):
        steps, (tr, ti) = _powers(ar_ref[...], ai_ref[...])
        row = lax.broadcasted_iota(jnp.int32, (8, W), 0)

        def shift(y, s):
            return jnp.where(row >= s, pltpu.roll(y, s, 0), 0.0)

        def step(t8, carry):
            cr, ci = carry
            base = pl.multiple_of(t8 * 8, 8)
            yr, yi = _block_scan(br_ref[pl.ds(base, 8), :], bi_ref[pl.ds(base, 8), :], steps, shift)
            xr = yr + tr * cr - ti * ci
            xi = yi + tr * ci + ti * cr
            xr_ref[pl.ds(base, 8), :] = xr
            xi_ref[pl.ds(base, 8), :] = xi
            return jnp.broadcast_to(xr[7:8, :], (8, W)), jnp.broadcast_to(xi[7:8, :], (8, W))

        zero = jnp.zeros((8, W), F32)
        lax.fori_loop(0, L // 8, step, (zero, zero), unroll=2)

    vec = pl.BlockSpec((1, W), lambda j: (0, j))
    col = pl.BlockSpec((L, W), lambda j: (0, j))
    return pl.pallas_call(body, grid=(P // W,), in_specs=[vec, vec, col, col], out_specs=[col, col],
                          out_shape=[jax.ShapeDtypeStruct((L, P), F32)] * 2,
                          compiler_params=_params(dimension_semantics=("parallel",)), name=name)(
        a_re, a_im, bu_re, bu_im)


def s5_scan_bwd(a_re, a_im, xs_re, xs_im, dx_re, dx_im, *, name):
    L, P = xs_re.shape
    W = _pick(P, (256, 128))

    def body(ar_ref, ai_ref, xr_ref, xi_ref, dr_ref, di_ref, lr_ref, li_ref, dar_ref, dai_ref):
        ar, ai = ar_ref[...], -ai_ref[...]
        steps, (tr, ti) = _powers(ar, ai)
        tr = jnp.concatenate([tr[j:j + 1, :] for j in range(7, -1, -1)], axis=0)
        ti = jnp.concatenate([ti[j:j + 1, :] for j in range(7, -1, -1)], axis=0)
        row8 = lax.broadcasted_iota(jnp.int32, (8, W), 0)
        nblk = L // 8

        def shift(y, s):
            return jnp.where(row8 < 8 - s, pltpu.roll(y, 8 - s, 0), 0.0)

        def step(s, carry):
            cr, ci = carry
            base = pl.multiple_of((nblk - 1 - s) * 8, 8)
            yr, yi = _block_scan(dr_ref[pl.ds(base, 8), :], di_ref[pl.ds(base, 8), :], steps, shift)
            lr = yr + tr * cr - ti * ci
            li = yi + tr * ci + ti * cr
            lr_ref[pl.ds(base, 8), :] = lr
            li_ref[pl.ds(base, 8), :] = li
            return jnp.broadcast_to(lr[0:1, :], (8, W)), jnp.broadcast_to(li[0:1, :], (8, W))

        zero = jnp.zeros((8, W), F32)
        lax.fori_loop(0, nblk, step, (zero, zero), unroll=2)
        row = lax.broadcasted_iota(jnp.int32, (L, W), 0)
        xpr = jnp.where(row >= 1, pltpu.roll(xr_ref[...], 1, 0), 0.0)
        xpi = jnp.where(row >= 1, pltpu.roll(xi_ref[...], 1, 0), 0.0)
        lr, li = lr_ref[...], li_ref[...]
        dar_ref[...] = jnp.sum(lr * xpr + li * xpi, axis=0, keepdims=True)
        dai_ref[...] = jnp.sum(li * xpr - lr * xpi, axis=0, keepdims=True)

    vec = pl.BlockSpec((1, W), lambda j: (0, j))
    col = pl.BlockSpec((L, W), lambda j: (0, j))
    return pl.pallas_call(body, grid=(P // W,), in_specs=[vec, vec, col, col, col, col],
                          out_specs=[col, col, vec, vec],
                          out_shape=[jax.ShapeDtypeStruct((L, P), F32)] * 2 + [jax.ShapeDtypeStruct((1, P), F32)] * 2,
                          compiler_params=_params(dimension_semantics=("parallel",)), name=name)(
        a_re, a_im, xs_re, xs_im, dx_re, dx_im)


def _gelu(y):
    c = math.sqrt(2.0 / math.pi)
    t = jnp.tanh(c * (y + 0.044715 * y * y * y))
    return 0.5 * y * (1.0 + t), t


def s5_out_fwd(y0, proj, dvec, glu_w, glu_b, *, name):
    L, C = y0.shape
    tr = _pick(L, (256, 128))

    def body(y_ref, u_ref, d_ref, w_ref, b_ref, o_ref):
        z, _ = _gelu(y_ref[...] + d_ref[...] * u_ref[...])
        zg = _dot(z, w_ref[...]) + b_ref[...]
        o_ref[...] = (z * _sigmoid(zg)).astype(o_ref.dtype)

    row = pl.BlockSpec((tr, C), lambda i: (i, 0))
    vec = pl.BlockSpec((1, C), lambda i: (0, 0))
    wsp = pl.BlockSpec((C, C), lambda i: (0, 0))
    return pl.pallas_call(body, grid=(L // tr,), in_specs=[row, row, vec, wsp, vec], out_specs=row,
                          out_shape=jax.ShapeDtypeStruct((L, C), BF16), name=name)(
        y0, proj, dvec, glu_w, glu_b)


def s5_out_bwd(y0, proj, dvec, glu_w, glu_b, dcat, *, name):
    L, C = y0.shape
    tr = _pick(L, (256, 128))

    def body(y_ref, u_ref, d_ref, w_ref, b_ref, do_ref, dy_ref, dud_ref, z_ref, dzg_ref, db_ref, dd_ref):
        u = u_ref[...]
        y = y_ref[...] + d_ref[...] * u
        z, t = _gelu(y)
        zg = _dot(z, w_ref[...]) + b_ref[...]
        s = _sigmoid(zg)
        do = do_ref[...]
        dzg = do * z * s * (1.0 - s)
        dz = do * s + _dot_nt(dzg, w_ref[...])
        c = math.sqrt(2.0 / math.pi)
        dgelu = 0.5 * (1.0 + t) + 0.5 * y * (1.0 - t * t) * c * (1.0 + 3.0 * 0.044715 * y * y)
        dy = dz * dgelu

        @pl.when(pl.program_id(0) == 0)
        def _():
            db_ref[...] = jnp.zeros_like(db_ref)
            dd_ref[...] = jnp.zeros_like(dd_ref)

        db_ref[...] += jnp.sum(dzg, axis=0, keepdims=True)
        dd_ref[...] += jnp.sum(dy * u, axis=0, keepdims=True)
        dy_ref[...] = dy
        dud_ref[...] = dy * d_ref[...]
        z_ref[...] = z.astype(BF16)
        dzg_ref[...] = dzg.astype(BF16)

    row = pl.BlockSpec((tr, C), lambda i: (i, 0))
    vec = pl.BlockSpec((1, C), lambda i: (0, 0))
    wsp = pl.BlockSpec((C, C), lambda i: (0, 0))
    return pl.pallas_call(body, grid=(L // tr,), in_specs=[row, row, vec, wsp, vec, row],
                          out_specs=[row, row, row, row, vec, vec],
                          out_shape=[jax.ShapeDtypeStruct((L, C), F32), jax.ShapeDtypeStruct((L, C), F32),
                                     jax.ShapeDtypeStruct((L, C), BF16), jax.ShapeDtypeStruct((L, C), BF16),
                                     jax.ShapeDtypeStruct((1, C), F32), jax.ShapeDtypeStruct((1, C), F32)],
                          compiler_params=_params(dimension_semantics=("arbitrary",)), name=name)(
        y0, proj, dvec, glu_w, glu_b, dcat)


def _hg_gates(xq, xf, lb, tri):
    C = xq.shape[0]
    sq = _sigmoid(xq)
    q = xq * sq
    sg = _sigmoid(xf)
    f = lb + (1.0 - lb) * sg
    kk = 1.0 - f
    b = _dot_f32(tri, jnp.log(f))
    bm = b[C // 2 - 1:C // 2, :]
    bl = b[C - 1:C, :]
    eb = jnp.exp(b)
    return dict(sq=sq, q=q, sg=sg, f=f, kk=kk, b=b, bm=bm, bl=bl, eb=eb, ebl=jnp.exp(bl),
                qb=q * eb, eqm=jnp.exp(b - bm), ekm=jnp.exp(bm - b), ekl=jnp.exp(bl - b))


def _tri(C, lower):
    r = lax.broadcasted_iota(jnp.int32, (C, C), 0)
    c = lax.broadcasted_iota(jnp.int32, (C, C), 1)
    return (r >= c) if lower else (c >= r)


def hgrn_fwd(proj, lb, norm_g, *, name):
    L = proj.shape[0]
    C, H, K = HG_CHUNK, HG_HEADS, HG_DIM
    HK = H * K
    nc = L // C

    def body(q_ref, f_ref, i_ref, g_ref, lb_ref, ng_ref, o_ref, sall_ref, st_ref):
        @pl.when(pl.program_id(0) == 0)
        def _():
            st_ref[...] = jnp.zeros_like(st_ref)

        mask = _tri(C, True)
        tri = mask.astype(F32)
        for h in range(H):
            sl = slice(h * K, (h + 1) * K)
            v = i_ref[:, sl]
            st = st_ref[h]
            sall_ref[h] = st
            gt = _hg_gates(q_ref[:, sl], f_ref[:, sl], lb_ref[:, sl], tri)
            qt = gt["q"] * gt["eqm"]
            kt = gt["kk"] * gt["ekm"]
            kh = gt["kk"] * gt["ekl"]
            att = jnp.where(mask, _dot_nt(qt, kt), 0.0)
            o = _dot(att, v) + _dot_nt(gt["qb"], st)
            st_ref[h] = st * gt["ebl"] + _dot_tn(v, kh)
            r = lax.rsqrt(jnp.mean(o * o, axis=-1, keepdims=True) + NORM_EPS)
            xg = g_ref[:, sl]
            o_ref[:, sl] = (o * r * ng_ref[:, sl] * (xg * _sigmoid(xg))).astype(o_ref.dtype)

    def blk(cb):
        return pl.BlockSpec((C, HK), lambda i: (i, cb))

    vec = pl.BlockSpec((1, HK), lambda i: (0, 0))
    return pl.pallas_call(
        body, grid=(nc,), in_specs=[blk(1), blk(2), blk(3), blk(4), vec, vec],
        out_specs=[pl.BlockSpec((C, HK), lambda i: (i, 0)), pl.BlockSpec((None, H, K, K), lambda i: (i, 0, 0, 0))],
        out_shape=[jax.ShapeDtypeStruct((L, HK), BF16), jax.ShapeDtypeStruct((nc, H, K, K), F32)],
        scratch_shapes=[pltpu.VMEM((H, K, K), F32)],
        compiler_params=_params(dimension_semantics=("arbitrary",)), name=name,
    )(proj, proj, proj, proj, lb, norm_g)


def hgrn_bwd(proj, lb, norm_g, sall, dcat, *, name):
    L = proj.shape[0]
    C, H, K = HG_CHUNK, HG_HEADS, HG_DIM
    HK = H * K
    nc = L // C

    def body(q_ref, f_ref, i_ref, g_ref, lb_ref, ng_ref, sall_ref, do_ref, dx_ref, dlb_ref, dng_ref, dst_ref):
        @pl.when(pl.program_id(0) == 0)
        def _():
            dst_ref[...] = jnp.zeros_like(dst_ref)
            dlb_ref[...] = jnp.zeros_like(dlb_ref)
            dng_ref[...] = jnp.zeros_like(dng_ref)

        mask = _tri(C, True)
        tri = mask.astype(F32)
        tri_t = _tri(C, False).astype(F32)
        rowi = lax.broadcasted_iota(jnp.int32, (C, K), 0)
        for h in range(H):
            sl = slice(h * K, (h + 1) * K)
            xq, xf, v, xg = q_ref[:, sl], f_ref[:, sl], i_ref[:, sl], g_ref[:, sl]
            lb_h, ng = lb_ref[:, sl], ng_ref[:, sl]
            st = sall_ref[h]
            dst = dst_ref[h]
            gt = _hg_gates(xq, xf, lb_h, tri)
            q, kk, qb = gt["q"], gt["kk"], gt["qb"]
            qt = q * gt["eqm"]
            kt = kk * gt["ekm"]
            kh = kk * gt["ekl"]
            att = jnp.where(mask, _dot_nt(qt, kt), 0.0)
            o = _dot(att, v) + _dot_nt(qb, st)
            r = lax.rsqrt(jnp.mean(o * o, axis=-1, keepdims=True) + NORM_EPS)
            oh = o * r
            sgg = _sigmoid(xg)
            silu_g = xg * sgg
            d_ob = do_ref[:, sl]
            d_on = d_ob * silu_g
            dxg = d_ob * (oh * ng) * (sgg * (1.0 + xg * (1.0 - sgg)))
            dng_ref[:, sl] += jnp.sum(d_on * oh, axis=0, keepdims=True)
            doh = d_on * ng
            do = r * (doh - oh * jnp.mean(doh * oh, axis=-1, keepdims=True))
            datt = jnp.where(mask, _dot_nt(do, v), 0.0)
            dv = _dot_tn(att, do) + _dot_nt(kh, dst)
            d_qb = _dot_f32(do, st)
            d_qt = _dot_f32(datt, kt)
            d_kt = _dot_f32_tn(datt, qt)
            d_kh = _dot_f32(v, dst)
            d_bl = jnp.sum(dst * st, axis=0, keepdims=True) * gt["ebl"] + jnp.sum(d_kh * kh, axis=0, keepdims=True)
            dst_ref[h] = dst * gt["ebl"] + _dot_tn(do, qb)
            dq = d_qt * gt["eqm"] + d_qb * gt["eb"]
            db = d_qt * qt + d_qb * qb - d_kt * kt - d_kh * kh
            db = db + jnp.where(rowi == C - 1, d_bl, 0.0)
            dkk = d_kt * gt["ekm"] + d_kh * gt["ekl"]
            dlg = _dot_f32(tri_t, db)
            df = dlg / gt["f"] - dkk
            sg = gt["sg"]
            dxf = df * (1.0 - lb_h) * sg * (1.0 - sg)
            dlb_ref[:, sl] += jnp.sum(df * (1.0 - sg), axis=0, keepdims=True)
            sq = gt["sq"]
            dxq = dq * (sq * (1.0 + xq * (1.0 - sq)))
            dx_ref[:, h * K:(h + 1) * K] = dxq.astype(dx_ref.dtype)
            dx_ref[:, HK + h * K:HK + (h + 1) * K] = dxf.astype(dx_ref.dtype)
            dx_ref[:, 2 * HK + h * K:2 * HK + (h + 1) * K] = dv.astype(dx_ref.dtype)
            dx_ref[:, 3 * HK + h * K:3 * HK + (h + 1) * K] = dxg.astype(dx_ref.dtype)

    def blk(cb):
        return pl.BlockSpec((C, HK), lambda i: (nc - 1 - i, cb))

    vec = pl.BlockSpec((1, HK), lambda i: (0, 0))
    return pl.pallas_call(
        body, grid=(nc,),
        in_specs=[blk(1), blk(2), blk(3), blk(4), vec, vec,
                  pl.BlockSpec((None, H, K, K), lambda i: (nc - 1 - i, 0, 0, 0)), blk(1)],
        out_specs=[pl.BlockSpec((C, 4 * HK), lambda i: (nc - 1 - i, 0)), vec, vec],
        out_shape=[jax.ShapeDtypeStruct((L, 4 * HK), BF16), jax.ShapeDtypeStruct((1, HK), F32),
                   jax.ShapeDtypeStruct((1, HK), F32)],
        scratch_shapes=[pltpu.VMEM((H, K, K), F32)],
        compiler_params=_params(dimension_semantics=("arbitrary",)), name=name,
    )(proj, proj, proj, proj, lb, norm_g, sall, dcat)


def _shift_down(x, k, row):
    return jnp.where(row >= k, pltpu.roll(x, k, 0), 0.0)


def _shift_up(x, k, row):
    n = x.shape[0]
    return jnp.where(row < n - k, pltpu.roll(x, n - k, 0), 0.0)


def convgate_fwd(hu, conv_w, conv_b, *, name):
    L, C2 = hu.shape
    C = C2 // 2
    tc = _pick(C, (256, 128))
    nb = C // tc

    def body(a_ref, b_ref, wa_ref, wb_ref, ba_ref, bb_ref, o_ref):
        row = lax.broadcasted_iota(jnp.int32, (L, tc), 0)

        def conv(x, w, bias):
            return w[2:3, :] * x + w[1:2, :] * _shift_down(x, 1, row) + w[0:1, :] * _shift_down(x, 2, row) + bias

        ca = conv(a_ref[...], wa_ref[...], ba_ref[...])
        cb = conv(b_ref[...], wb_ref[...], bb_ref[...])
        o_ref[...] = (ca * _sigmoid(ca) * cb).astype(o_ref.dtype)

    def col(off, rows):
        return pl.BlockSpec((rows, tc), lambda j: (0, j + off))

    return pl.pallas_call(
        body, grid=(nb,), in_specs=[col(0, L), col(nb, L), col(0, 3), col(nb, 3), col(0, 1), col(nb, 1)],
        out_specs=col(0, L), out_shape=jax.ShapeDtypeStruct((L, C), BF16),
        compiler_params=_params(dimension_semantics=("parallel",)), name=name,
    )(hu, hu, conv_w, conv_w, conv_b, conv_b)


def convgate_bwd(hu, conv_w, conv_b, dact, *, name):
    L, C2 = hu.shape
    C = C2 // 2
    tc = _pick(C, (256, 128))
    nb = C // tc

    def body(a_ref, b_ref, wa_ref, wb_ref, ba_ref, bb_ref, d_ref, dxa_ref, dxb_ref, dwa_ref, dwb_ref, dba_ref, dbb_ref):
        row = lax.broadcasted_iota(jnp.int32, (L, tc), 0)

        def conv(x, w, bias):
            x1 = _shift_down(x, 1, row)
            x2 = _shift_down(x, 2, row)
            return w[2:3, :] * x + w[1:2, :] * x1 + w[0:1, :] * x2 + bias, x1, x2

        xa, xb = a_ref[...], b_ref[...]
        wa, wb = wa_ref[...], wb_ref[...]
        ca, xa1, xa2 = conv(xa, wa, ba_ref[...])
        cb, xb1, xb2 = conv(xb, wb, bb_ref[...])
        d = d_ref[...]
        sa = _sigmoid(ca)
        dca = d * cb * (sa * (1.0 + ca * (1.0 - sa)))
        dcb = d * (ca * sa)

        def back(dc, w, x, x1, x2, dx_ref, dw_ref, db_ref):
            dx = w[2:3, :] * dc + w[1:2, :] * _shift_up(dc, 1, row) + w[0:1, :] * _shift_up(dc, 2, row)
            dx_ref[...] = dx.astype(dx_ref.dtype)
            dw_ref[...] = jnp.concatenate([jnp.sum(dc * x2, axis=0, keepdims=True),
                                           jnp.sum(dc * x1, axis=0, keepdims=True),
                                           jnp.sum(dc * x, axis=0, keepdims=True)], axis=0)
            db_ref[...] = jnp.sum(dc, axis=0, keepdims=True)

        back(dca, wa, xa, xa1, xa2, dxa_ref, dwa_ref, dba_ref)
        back(dcb, wb, xb, xb1, xb2, dxb_ref, dwb_ref, dbb_ref)

    def col(off, rows):
        return pl.BlockSpec((rows, tc), lambda j: (0, j + off))

    outs = pl.pallas_call(
        body, grid=(nb,),
        in_specs=[col(0, L), col(nb, L), col(0, 3), col(nb, 3), col(0, 1), col(nb, 1), col(0, L)],
        out_specs=[col(0, L), col(0, L), col(0, 3), col(0, 3), col(0, 1), col(0, 1)],
        out_shape=[jax.ShapeDtypeStruct((L, C), BF16)] * 2 + [jax.ShapeDtypeStruct((3, C), F32)] * 2
        + [jax.ShapeDtypeStruct((1, C), F32)] * 2,
        compiler_params=_params(dimension_semantics=("parallel",)), name=name,
    )(hu, hu, conv_w, conv_w, conv_b, conv_b, dact)
    dxa, dxb, dwa, dwb, dba, dbb = outs
    return (jnp.concatenate([dxa, dxb], axis=1), jnp.concatenate([dwa, dwb], axis=1),
            jnp.concatenate([dba, dbb], axis=1))


def _to_branch_order(t, d):
    L, W = t.shape
    return t if d == 1 else t.reshape(L // d, d, W).transpose(1, 0, 2).reshape(L, W)


def _to_token_order(t, d):
    L, W = t.shape
    return t if d == 1 else t.reshape(d, L // d, W).transpose(1, 0, 2).reshape(L, W)


def rope_tables(positions):
    half = ROT_DIM // 2
    inv_freq = ROPE_THETA ** (-jnp.arange(half, dtype=F32) * 2.0 / ROT_DIM)
    ang = positions.astype(F32)[:, None] * inv_freq
    cos, sin = jnp.cos(ang), jnp.sin(ang)
    L = positions.shape[0]
    one = jnp.ones((L, ATT_E - ROT_DIM), F32)
    zero = jnp.zeros((L, ATT_E - ROT_DIM), F32)
    zh = jnp.zeros((L, half), F32)
    tc = jnp.concatenate([cos, cos, one], axis=1)
    ts1 = jnp.concatenate([zh, sin, zero], axis=1)
    ts2 = jnp.concatenate([-sin, zh, zero], axis=1)
    return tuple(jnp.concatenate([t, t], axis=1) for t in (tc, ts1, ts2))


def rope_fwd(qkv, tabs, *, name):
    L = qkv.shape[0]
    W = 512
    tr = _pick(L, (256, 128))
    nq = 1536 // W
    scale = ATT_E ** -0.5

    def body(x_ref, c_ref, s1_ref, s2_ref, o_ref):
        j = pl.program_id(1)
        x = x_ref[...]
        c = jnp.concatenate([c_ref[...]] * 4, axis=1)
        s1 = jnp.concatenate([s1_ref[...]] * 4, axis=1)
        s2 = jnp.concatenate([s2_ref[...]] * 4, axis=1)
        rot = x * c + pltpu.roll(x, 8, 1) * s1 + pltpu.roll(x, W - 8, 1) * s2
        mult = jnp.where(j < nq, scale, 1.0)
        o_ref[...] = jnp.where(j < 2 * nq, rot * mult, x).astype(o_ref.dtype)

    blk = pl.BlockSpec((tr, W), lambda i, j: (i, j))
    tab = pl.BlockSpec((tr, 128), lambda i, j: (i, 0))
    return pl.pallas_call(body, grid=(L // tr, 3 * nq), in_specs=[blk, tab, tab, tab], out_specs=blk,
                          out_shape=jax.ShapeDtypeStruct((L, 3 * 1536), BF16),
                          compiler_params=_params(dimension_semantics=("parallel", "parallel")), name=name)(
        qkv, *tabs)


def rope_bwd(dq, dk, dv, tabs, *, name):
    L = dq.shape[0]
    W = 512
    tr = _pick(L, (256, 128))
    nq = 1536 // W
    scale = ATT_E ** -0.5

    def body(dq_ref, dk_ref, dv_ref, c_ref, s1_ref, s2_ref, o_ref):
        j = pl.program_id(1)
        c = jnp.concatenate([c_ref[...]] * 4, axis=1)
        s1 = jnp.concatenate([s1_ref[...]] * 4, axis=1)
        s2 = jnp.concatenate([s2_ref[...]] * 4, axis=1)

        def unrot(dy):
            return dy * c + pltpu.roll(dy * s1, W - 8, 1) + pltpu.roll(dy * s2, 8, 1)

        @pl.when(j < nq)
        def _():
            o_ref[...] = (unrot(dq_ref[...]) * scale).astype(o_ref.dtype)

        @pl.when((j >= nq) & (j < 2 * nq))
        def _():
            o_ref[...] = unrot(dk_ref[...]).astype(o_ref.dtype)

        @pl.when(j >= 2 * nq)
        def _():
            o_ref[...] = dv_ref[...].astype(o_ref.dtype)

    def src(k):
        return pl.BlockSpec((tr, W), lambda i, j: (i, jnp.clip(j - k * nq, 0, nq - 1)))

    tab = pl.BlockSpec((tr, 128), lambda i, j: (i, 0))
    return pl.pallas_call(body, grid=(L // tr, 3 * nq), in_specs=[src(0), src(1), src(2), tab, tab, tab],
                          out_specs=pl.BlockSpec((tr, W), lambda i, j: (i, j)),
                          out_shape=jax.ShapeDtypeStruct((L, 3 * 1536), BF16),
                          compiler_params=_params(dimension_semantics=("parallel", "arbitrary")), name=name)(
        dq, dk, dv, *tabs)


def _att_masks(has_prev):
    qi = lax.broadcasted_iota(jnp.int32, (ATT_BLOCK, ATT_BLOCK), 0)
    kj = lax.broadcasted_iota(jnp.int32, (ATT_BLOCK, ATT_BLOCK), 1)
    return qi >= kj, (kj >= qi) & has_prev


def attn_fwd(qp, kp, vp, d, *, name):
    L, W = qp.shape
    B, E = ATT_BLOCK, ATT_E
    nblk = L // B
    nb = nblk // d

    def body(q_ref, kc_ref, kp_ref, vc_ref, vp_ref, o_ref, l_ref):
        has_prev = (pl.program_id(0) % nb) > 0
        mc, mp = _att_masks(has_prev)
        for h in range(ATT_HPG):
            sl = slice(h * E, (h + 1) * E)
            q = q_ref[:, sl]
            sc = jnp.where(mc, _dot_nt(q, kc_ref[:, sl]), NEG_BIG)
            sp = jnp.where(mp, _dot_nt(q, kp_ref[:, sl]), NEG_BIG)
            m = jnp.maximum(jnp.max(sc, axis=-1, keepdims=True), jnp.max(sp, axis=-1, keepdims=True))
            pc = jnp.exp(sc - m)
            pp = jnp.exp(sp - m)
            den = jnp.sum(pc, axis=-1, keepdims=True) + jnp.sum(pp, axis=-1, keepdims=True)
            o = (_dot(pc, vc_ref[:, sl]) + _dot(pp, vp_ref[:, sl])) / den
            o_ref[:, sl] = o
            l_ref[:, sl] = jnp.broadcast_to(m + jnp.log(den), (B, E))

    cur = pl.BlockSpec((B, W), lambda j: (j, 0))
    prev = pl.BlockSpec((B, W), lambda j: (jnp.maximum(j - 1, 0), 0))
    return pl.pallas_call(body, grid=(nblk,), in_specs=[cur, cur, prev, cur, prev], out_specs=[cur, cur],
                          out_shape=[jax.ShapeDtypeStruct((L, W), F32)] * 2,
                          compiler_params=_params(dimension_semantics=("parallel",)), name=name)(
        qp, kp, kp, vp, vp)


def attn_bwd(qp, kp, vp, lse, do, dl, d, *, name):
    L, W = qp.shape
    B, E = ATT_BLOCK, ATT_E
    nblk = L // B
    nb = nblk // d

    def body(q_ref, kc_ref, kp_ref, vc_ref, vp_ref, l_ref, do_ref, dl_ref, dq_ref, dk_ref, dv_ref, tkc, tkp, tvc, tvp):
        j = pl.program_id(0)

        @pl.when(j == 0)
        def _():
            dk_ref[...] = jnp.zeros_like(dk_ref)
            dv_ref[...] = jnp.zeros_like(dv_ref)

        has_prev = (j % nb) > 0
        mc, mp = _att_masks(has_prev)
        for h in range(ATT_HPG):
            sl = slice(h * E, (h + 1) * E)
            q = q_ref[:, sl]
            kc, kpv, vc, vpv = kc_ref[:, sl], kp_ref[:, sl], vc_ref[:, sl], vp_ref[:, sl]
            lse_h = l_ref[:, h * E:h * E + 1]
            dl_h = dl_ref[:, h * E:h * E + 1]
            doh = do_ref[:, sl]
            pc = jnp.where(mc, jnp.exp(_dot_nt(q, kc) - lse_h), 0.0)
            pp = jnp.where(mp, jnp.exp(_dot_nt(q, kpv) - lse_h), 0.0)
            dsc = pc * (_dot_nt(doh, vc) - dl_h)
            dsp = pp * (_dot_nt(doh, vpv) - dl_h)
            dq_ref[:, sl] = _dot(dsc, kc) + _dot(dsp, kpv)
            tkc[:, sl] = _dot_tn(dsc, q)
            tkp[:, sl] = _dot_tn(dsp, q)
            tvc[:, sl] = _dot_tn(pc, doh)
            tvp[:, sl] = _dot_tn(pp, doh)
        cur = pl.multiple_of(j * B, B)
        prv = pl.multiple_of(jnp.maximum(j - 1, 0) * B, B)
        dk_ref[pl.ds(cur, B), :] += tkc[...]
        dv_ref[pl.ds(cur, B), :] += tvc[...]
        dk_ref[pl.ds(prv, B), :] += tkp[...]
        dv_ref[pl.ds(prv, B), :] += tvp[...]

    cur = pl.BlockSpec((B, W), lambda j: (j, 0))
    prev = pl.BlockSpec((B, W), lambda j: (jnp.maximum(j - 1, 0), 0))
    full = pl.BlockSpec((L, W), lambda j: (0, 0))
    return pl.pallas_call(body, grid=(nblk,), in_specs=[cur, cur, prev, cur, prev, cur, cur, cur],
                          out_specs=[cur, full, full], out_shape=[jax.ShapeDtypeStruct((L, W), F32)] * 3,
                          scratch_shapes=[pltpu.VMEM((B, W), F32)] * 4,
                          compiler_params=_params(dimension_semantics=("arbitrary",)), name=name)(
        qp, kp, kp, vp, vp, lse, do, dl)


def _merge_alpha(l_refs):
    ls = [r[...] for r in l_refs]
    m = jnp.maximum(jnp.maximum(ls[0], ls[1]), ls[2])
    es = [jnp.exp(l - m) for l in ls]
    den = es[0] + es[1] + es[2]
    return [e / den for e in es]


def merge_fwd(os_, ls_, *, name):
    L, W = os_[0].shape
    tr = _pick(L, (256, 128))

    def body(o0, o1, o2, l0, l1, l2, out_ref):
        al = _merge_alpha((l0, l1, l2))
        out_ref[...] = (al[0] * o0[...] + al[1] * o1[...] + al[2] * o2[...]).astype(out_ref.dtype)

    row = pl.BlockSpec((tr, W), lambda i: (i, 0))
    return pl.pallas_call(body, grid=(L // tr,), in_specs=[row] * 6, out_specs=row,
                          out_shape=jax.ShapeDtypeStruct((L, W), BF16), name=name)(*os_, *ls_)


def merge_bwd(os_, ls_, do, *, name):
    L, W = do.shape
    tr = _pick(L, (256, 128))

    def body(o0, o1, o2, l0, l1, l2, do_ref, d0, d1, d2, e0, e1, e2):
        al = _merge_alpha((l0, l1, l2))
        dov = do_ref[...]
        r = lax.broadcasted_iota(jnp.int32, (W, W), 0) // ATT_E
        c = lax.broadcasted_iota(jnp.int32, (W, W), 1) // ATT_E
        ones_blk = (r == c).astype(F32)
        t = jnp.zeros_like(dov)
        for a, o in zip(al, (o0, o1, o2)):
            t = t + a * _dot_f32(dov * o[...], ones_blk)
        for a, d_ref, e_ref in zip(al, (d0, d1, d2), (e0, e1, e2)):
            d_ref[...] = a * dov
            e_ref[...] = a * t

    row = pl.BlockSpec((tr, W), lambda i: (i, 0))
    return pl.pallas_call(body, grid=(L // tr,), in_specs=[row] * 7, out_specs=[row] * 6,
                          out_shape=[jax.ShapeDtypeStruct((L, W), F32)] * 6, name=name)(*os_, *ls_, do)


def _me_and_peers():
    x, y, c = lax.axis_index("x"), lax.axis_index("y"), lax.axis_index("c")
    peers = []
    for k in range(1, N_DEV):
        px = 1 - x if k & 4 else x
        py = 1 - y if k & 2 else y
        pc = 1 - c if k & 1 else c
        peers.append((px, py, pc))
    return (x, y, c), peers


def _index(dev):
    return 4 * dev[0] + 2 * dev[1] + dev[2]


def _hbm(a):
    return pltpu.with_memory_space_constraint(a, pltpu.HBM)


HBM_SPEC = pl.BlockSpec(memory_space=pltpu.HBM)
SEM_SPEC = pl.BlockSpec(memory_space=pltpu.SEMAPHORE)
DATAFLOW = pltpu.SideEffectType.DATAFLOW_SIDE_EFFECTING


def _remote(src_ref, land_ref, slotted, me, peer, src_is_mine, send_sem, recv_sem, k):
    sender, receiver = (me, peer) if src_is_mine else (peer, me)
    src = src_ref.at[_index(receiver)] if slotted else src_ref
    return pltpu.make_async_remote_copy(src_ref=src, dst_ref=land_ref.at[_index(sender)], send_sem=send_sem.at[k],
                                        recv_sem=recv_sem.at[k], device_id=peer, device_id_type=MESH_ID)


def copies_start(arrays, slotted, *, name):
    n = len(arrays)
    lands = [lax.empty(a.shape if slotted else (N_DEV,) + a.shape, a.dtype) for a in arrays]

    def body(*refs):
        x_refs, land_refs = refs[:n], refs[n:2 * n]
        send, recv = refs[2 * n:3 * n], refs[3 * n:4 * n]
        token = refs[-1]
        me, peers = _me_and_peers()
        for w in range(n):
            for k, peer in enumerate(peers):
                _remote(x_refs[w], land_refs[w], slotted, me, peer, True, send[w], recv[w], k).start()
        token[...] = jnp.zeros_like(token)

    sem = pltpu.SemaphoreType.DMA((N_DEV - 1,))
    out_shape = ([sem] * (2 * n) + [pltpu.HBM(a.shape, a.dtype) for a in arrays]
                 + [pltpu.HBM(l.shape, l.dtype) for l in lands] + [jax.ShapeDtypeStruct((8, 128), F32)])
    outs = pl.pallas_call(
        body, name=name, out_shape=out_shape, in_specs=[HBM_SPEC] * (2 * n),
        out_specs=[SEM_SPEC] * (2 * n) + [HBM_SPEC] * (2 * n) + [pl.BlockSpec(memory_space=pltpu.VMEM)],
        input_output_aliases={i: 2 * n + i for i in range(2 * n)},
        compiler_params=pltpu.CompilerParams(has_side_effects=DATAFLOW),
    )(*[_hbm(a) for a in arrays], *[_hbm(l) for l in lands])
    handles = [(outs[w], outs[n + w], outs[2 * n + w], outs[3 * n + w]) for w in range(n)]
    return handles, outs[-1]


def copies_wait(handle, slotted, after, *, name):
    send_sem, recv_sem, x_thru, land_thru = handle

    def body(x_ref, land_ref, send_ref, recv_ref, after_ref, x_out, land_out):
        me, peers = _me_and_peers()
        for k, peer in enumerate(peers):
            _remote(x_ref, land_ref, slotted, me, peer, True, send_ref, recv_ref, k).wait_send()
        for k, peer in enumerate(peers):
            _remote(x_ref, land_ref, slotted, me, peer, False, send_ref, recv_ref, k).wait_recv()

    return pl.pallas_call(
        body, name=name, out_shape=(pltpu.HBM(x_thru.shape, x_thru.dtype), pltpu.HBM(land_thru.shape, land_thru.dtype)),
        in_specs=(HBM_SPEC, HBM_SPEC, SEM_SPEC, SEM_SPEC, pl.BlockSpec(memory_space=pl.ANY)),
        out_specs=(HBM_SPEC, HBM_SPEC), input_output_aliases={0: 0, 1: 1},
        compiler_params=pltpu.CompilerParams(has_side_effects=DATAFLOW),
    )(x_thru, land_thru, send_sem, recv_sem, after)


def cast_bf16(x, *, ncols=None, name):
    R = x.shape[0]
    C = ncols or x.shape[1]
    tr = _pick(R, (512, 352, 256, 128, 64))

    def body(x_ref, o_ref):
        o_ref[...] = x_ref[...].astype(BF16)

    row = pl.BlockSpec((tr, C), lambda i: (i, 0))
    return pl.pallas_call(body, grid=(R // tr,), in_specs=[row], out_specs=row,
                          out_shape=jax.ShapeDtypeStruct((R, C), BF16), name=name)(x)


def cast_bf16_layer(x3, layer, *, name):
    _, R, C = x3.shape
    tr = _pick(R, (512, 352, 256, 128, 64))

    def body(x_ref, o_ref):
        o_ref[...] = x_ref[...].astype(BF16)

    return pl.pallas_call(body, grid=(R // tr,), in_specs=[pl.BlockSpec((None, tr, C), lambda i: (layer, i, 0))],
                          out_specs=pl.BlockSpec((tr, C), lambda i: (i, 0)),
                          out_shape=jax.ShapeDtypeStruct((R, C), BF16), name=name)(x3)


def _blockdiag_call(b, build, G, r, c, name):
    def body_build(b_ref, o_ref):
        o_ref[...] = jnp.zeros_like(o_ref)
        for g in range(G):
            o_ref[g * r:(g + 1) * r, g * c:(g + 1) * c] = b_ref[g]

    def body_extract(d_ref, o_ref):
        for g in range(G):
            o_ref[g] = d_ref[g * r:(g + 1) * r, g * c:(g + 1) * c]

    out = jax.ShapeDtypeStruct((G * r, G * c) if build else (G, r, c), F32)
    return pl.pallas_call(body_build if build else body_extract, out_shape=out, name=name)(b)


def make_blockdiag(G, r, c, name):
    @jax.custom_vjp
    def blockdiag(b):
        return _blockdiag_call(b, True, G, r, c, name + "_build")

    def fwd(b):
        return blockdiag(b), None

    def bwd(_, g):
        return (_blockdiag_call(g, False, G, r, c, name + "_extract"),)

    blockdiag.defvjp(fwd, bwd)
    return blockdiag


def _my_index():
    return 4 * lax.axis_index("x") + 2 * lax.axis_index("y") + lax.axis_index("c")


def cols_from_shards(g, own, *, name):
    _, K, n = g.shape
    tk = _pick(K, (256, 128))

    def body(g_ref, own_ref, o_ref):
        me = _my_index()
        for i in range(N_DEV):
            @pl.when(me == i)
            def _():
                o_ref[:, i * n:(i + 1) * n] = own_ref[...]

            @pl.when(me != i)
            def _():
                o_ref[:, i * n:(i + 1) * n] = g_ref[i]

    return pl.pallas_call(body, grid=(K // tk,),
                          in_specs=[pl.BlockSpec((N_DEV, tk, n), lambda i: (0, i, 0)), pl.BlockSpec((tk, n), lambda i: (i, 0))],
                          out_specs=pl.BlockSpec((tk, N_DEV * n), lambda i: (i, 0)),
                          out_shape=jax.ShapeDtypeStruct((K, N_DEV * n), g.dtype), name=name)(g, own)


def rows_from_shards(g, own, *, name):
    _, k, N = g.shape

    def body(g_ref, own_ref, o_ref):
        mine = _my_index() == pl.program_id(0)

        @pl.when(mine)
        def _():
            o_ref[...] = own_ref[...]

        @pl.when(jnp.logical_not(mine))
        def _():
            o_ref[...] = g_ref[...]

    return pl.pallas_call(body, grid=(N_DEV,),
                          in_specs=[pl.BlockSpec((None, k, N), lambda i: (i, 0, 0)), pl.BlockSpec((k, N), lambda i: (0, 0))],
                          out_specs=pl.BlockSpec((k, N), lambda i: (i, 0)),
                          out_shape=jax.ShapeDtypeStruct((N_DEV * k, N), g.dtype), name=name)(g, own)


def shards_from_cols(w, *, name):
    K, N = w.shape
    n = N // N_DEV
    tk = _pick(K, (256, 128))

    def body(w_ref, o_ref):
        for i in range(N_DEV):
            o_ref[i] = w_ref[:, i * n:(i + 1) * n].astype(o_ref.dtype)

    return pl.pallas_call(body, grid=(K // tk,), in_specs=[pl.BlockSpec((tk, N), lambda i: (i, 0))],
                          out_specs=pl.BlockSpec((N_DEV, tk, n), lambda i: (0, i, 0)),
                          out_shape=jax.ShapeDtypeStruct((N_DEV, K, n), BF16), name=name)(w)


def _adamw(w, g, m, v):
    m = ADAM_B1 * m + (1.0 - ADAM_B1) * g
    v = ADAM_B2 * v + (1.0 - ADAM_B2) * (g * g)
    m_hat = m / (1.0 - ADAM_B1 ** ADAM_STEP)
    v_hat = v / (1.0 - ADAM_B2 ** ADAM_STEP)
    delta = -ADAM_LR * (m_hat / (jnp.sqrt(v_hat) + ADAM_EPS) + ADAM_WD * w)
    return delta, m, v


def reduce_adamw(recv, own, own_slotted, me, w, m, v, *, layer=0, n_layers=1, into=None, name):
    _, R, C = recv.shape
    tr = _pick(R, (352, 320, 288, 256, 128, 64, 32, 16, 8))
    off = layer * (R // tr)

    def body(me_ref, r_ref, own_ref, w_ref, m_ref, v_ref, *rest):
        g_ref, d_ref, nm_ref, nv_ref = rest[-4:]
        mine = me_ref[0]
        g = None
        for i in range(N_DEV):
            part = jnp.where(mine == i, own_ref[...], r_ref[i]).astype(F32)
            g = part if g is None else g + part
        delta, nm, nv = _adamw(w_ref[...], g, m_ref[...], v_ref[...])
        g_ref[...] = g
        d_ref[...] = delta
        nm_ref[...] = nm
        nv_ref[...] = nv

    row = pl.BlockSpec((tr, C), lambda i, me_ref: (i + off, 0))
    own_spec = (pl.BlockSpec((None, tr, C), lambda i, me_ref: (me_ref[0], i, 0)) if own_slotted
                else pl.BlockSpec((tr, C), lambda i, me_ref: (i, 0)))
    rest = [] if into is None else list(into)
    grid_spec = pltpu.PrefetchScalarGridSpec(
        num_scalar_prefetch=1, grid=(R // tr,),
        in_specs=[pl.BlockSpec((N_DEV, tr, C), lambda i, me_ref: (0, i, 0)), own_spec, row, row, row]
        + [pl.BlockSpec(memory_space=pl.ANY)] * len(rest),
        out_specs=[row] * 4)
    return pl.pallas_call(body, grid_spec=grid_spec, out_shape=[jax.ShapeDtypeStruct((n_layers * R, C), F32)] * 4,
                          input_output_aliases={6 + k: k for k in range(len(rest))},
                          compiler_params=_params(dimension_semantics=("parallel",)), name=name)(
        me.reshape(1).astype(jnp.int32), recv, own, w, m, v, *rest)


def _s5_prepare(A_re, A_im, log_dt, B_re, B_im, C_re, C_im):
    G, P, Cg = S5_GROUPS, S5_STATE, S5_GROUP
    dt = jnp.exp(log_dt)[:, None]
    mag = jnp.exp(A_re * dt)
    ab_re = mag * jnp.cos(A_im * dt)
    ab_im = mag * jnp.sin(A_im * dt)
    den = A_re * A_re + A_im * A_im
    nr, ni = ab_re - 1.0, ab_im
    c_re = (nr * A_re + ni * A_im) / den
    c_im = (ni * A_re - nr * A_im) / den
    Bb_re = c_re[..., None] * B_re - c_im[..., None] * B_im
    Bb_im = c_re[..., None] * B_im + c_im[..., None] * B_re
    def dense_in(b, name):
        return make_blockdiag(G, Cg, P, name)(b.transpose(0, 2, 1))

    def dense_out(c, name):
        return make_blockdiag(G, P, Cg, name)(c.transpose(0, 2, 1))

    return (ab_re.reshape(1, G * P), ab_im.reshape(1, G * P), dense_in(Bb_re, "s5_wb_re"), dense_in(Bb_im, "s5_wb_im"),
            dense_out(C_re, "s5_wc_re"), dense_out(-C_im, "s5_wc_im"))


def _lower_bound(gamma):
    return jnp.cumsum(jax.nn.softmax(gamma, axis=0), axis=0)[0:1]


def _ffn_fwd(h, g_norm, get_w_in, conv_w, conv_b, get_w_out, tag):
    hn = rms_fwd(h, g_norm, name=tag + "_rms")
    w_in = get_w_in(hn)
    hu = mm(hn, w_in, tb=True, name=tag + "_in")
    act = convgate_fwd(hu, conv_w, conv_b, name=tag + "_gate")
    w_out = get_w_out(act)
    h_out = mm(act, w_out, res=h, name=tag + "_out")
    return h_out, (hn, hu, act), w_in, w_out


def _ffn_bwd(h, g_norm, w_in, conv_w, conv_b, w_out, saved, dh, tag, send_dw_in, send_dw_out):
    hn, hu, act = saved
    sent = send_dw_out(mm(act, dh, ta=True, name=tag + "_dwout"))
    dact = mm(dh, w_out, tb=True, dep=sent, name=tag + "_dact")
    dhu, dconv_w, dconv_b = convgate_bwd(hu, conv_w, conv_b, dact, name=tag + "_dgate")
    sent = send_dw_in(mm(dhu, hn, ta=True, name=tag + "_dwin"))
    dhn = mm(dhu, w_in, dep=sent, name=tag + "_dhn")
    dh_in, dg = rms_bwd(h, g_norm, dhn, dh, name=tag + "_drms")
    return dh_in, dg, dconv_w, dconv_b


def kernel(x, positions, norm_mix, norm_ffn, norm_final, mix_w_in, mix_w_out, s5_A_re, s5_A_im, s5_log_dt, s5_B_re, s5_B_im, s5_C_re, s5_C_im, s5_D, s5_glu_w, s5_glu_b, hgrn_gamma, hgrn_norm, att_w_qkv, att_w_o, ffn_w_in, ffn_conv_w, ffn_conv_b, ffn_w_out, loss_target, m_norm_mix, m_norm_ffn, m_norm_final, m_mix_w_in, m_mix_w_out, m_s5_A_re, m_s5_A_im, m_s5_log_dt, m_s5_B_re, m_s5_B_im, m_s5_C_re, m_s5_C_im, m_s5_D, m_s5_glu_w, m_s5_glu_b, m_hgrn_gamma, m_hgrn_norm, m_att_w_qkv, m_att_w_o, m_ffn_w_in, m_ffn_conv_w, m_ffn_conv_b, m_ffn_w_out, v_norm_mix, v_norm_ffn, v_norm_final, v_mix_w_in, v_mix_w_out, v_s5_A_re, v_s5_A_im, v_s5_log_dt, v_s5_B_re, v_s5_B_im, v_s5_C_re, v_s5_C_im, v_s5_D, v_s5_glu_w, v_s5_glu_b, v_hgrn_gamma, v_hgrn_norm, v_att_w_qkv, v_att_w_o, v_ffn_w_in, v_ffn_conv_w, v_ffn_conv_b, v_ffn_w_out):
    W = dict(norm_mix=norm_mix, norm_ffn=norm_ffn, norm_final=norm_final, mix_w_in=mix_w_in, mix_w_out=mix_w_out,
             s5_A_re=s5_A_re, s5_A_im=s5_A_im, s5_log_dt=s5_log_dt, s5_B_re=s5_B_re, s5_B_im=s5_B_im,
             s5_C_re=s5_C_re, s5_C_im=s5_C_im, s5_D=s5_D, s5_glu_w=s5_glu_w, s5_glu_b=s5_glu_b,
             hgrn_gamma=hgrn_gamma, hgrn_norm=hgrn_norm, att_w_qkv=att_w_qkv, att_w_o=att_w_o, ffn_w_in=ffn_w_in,
             ffn_conv_w=ffn_conv_w, ffn_conv_b=ffn_conv_b, ffn_w_out=ffn_w_out)
    M = dict(norm_mix=m_norm_mix, norm_ffn=m_norm_ffn, norm_final=m_norm_final, mix_w_in=m_mix_w_in,
             mix_w_out=m_mix_w_out, s5_A_re=m_s5_A_re, s5_A_im=m_s5_A_im, s5_log_dt=m_s5_log_dt, s5_B_re=m_s5_B_re,
             s5_B_im=m_s5_B_im, s5_C_re=m_s5_C_re, s5_C_im=m_s5_C_im, s5_D=m_s5_D, s5_glu_w=m_s5_glu_w,
             s5_glu_b=m_s5_glu_b, hgrn_gamma=m_hgrn_gamma, hgrn_norm=m_hgrn_norm, att_w_qkv=m_att_w_qkv,
             att_w_o=m_att_w_o, ffn_w_in=m_ffn_w_in, ffn_conv_w=m_ffn_conv_w, ffn_conv_b=m_ffn_conv_b,
             ffn_w_out=m_ffn_w_out)
    V = dict(norm_mix=v_norm_mix, norm_ffn=v_norm_ffn, norm_final=v_norm_final, mix_w_in=v_mix_w_in,
             mix_w_out=v_mix_w_out, s5_A_re=v_s5_A_re, s5_A_im=v_s5_A_im, s5_log_dt=v_s5_log_dt, s5_B_re=v_s5_B_re,
             s5_B_im=v_s5_B_im, s5_C_re=v_s5_C_re, s5_C_im=v_s5_C_im, s5_D=v_s5_D, s5_glu_w=v_s5_glu_w,
             s5_glu_b=v_s5_glu_b, hgrn_gamma=v_hgrn_gamma, hgrn_norm=v_hgrn_norm, att_w_qkv=v_att_w_qkv,
             att_w_o=v_att_w_o, ffn_w_in=v_ffn_w_in, ffn_conv_w=v_ffn_conv_w, ffn_conv_b=v_ffn_conv_b,
             ffn_w_out=v_ffn_w_out)
    return _step(x[0], positions[0], loss_target[0], W, M, V)


TRANSPOSED = ("mix_w_in", "att_w_qkv", "ffn_w_in")
SMALL = ("norm_mix", "norm_ffn", "norm_final", "s5_A_re", "s5_A_im", "s5_log_dt", "s5_B_re", "s5_B_im", "s5_C_re",
         "s5_C_im", "s5_D", "s5_glu_b", "hgrn_gamma", "hgrn_norm", "ffn_conv_b")
ORDER = ("norm_mix", "norm_ffn", "norm_final", "mix_w_in", "mix_w_out", "s5_A_re", "s5_A_im", "s5_log_dt", "s5_B_re",
         "s5_B_im", "s5_C_re", "s5_C_im", "s5_D", "s5_glu_w", "s5_glu_b", "hgrn_gamma", "hgrn_norm", "att_w_qkv",
         "att_w_o", "ffn_w_in", "ffn_conv_w", "ffn_conv_b", "ffn_w_out")
PACK_COLS = 1024


def _step(x, positions, target, W, M, V):
    L, D = x.shape
    me = 4 * lax.axis_index("x") + 2 * lax.axis_index("y") + lax.axis_index("c")
    n_cw = W["ffn_conv_w"].shape[-1]
    T = {n: tuple(jnp.swapaxes(d[n], -1, -2) for d in (W, M, V)) for n in TRANSPOSED}
    shards = {
        "mix_w_in": cast_bf16(T["mix_w_in"][0][0], name="mix_w_in_cast"),
        "conv_w": W["ffn_conv_w"].reshape(6, n_cw),
        "s5_glu_w": cast_bf16(W["s5_glu_w"][0], name="s5_glu_w_cast"),
        "mix_w_out": cast_bf16(W["mix_w_out"][0], name="mix_w_out_cast"),
        "ffn_w_in0": cast_bf16_layer(T["ffn_w_in"][0], 0, name="ffn_w_in0_cast"),
        "ffn_w_out0": cast_bf16_layer(W["ffn_w_out"], 0, name="ffn_w_out0_cast"),
        "att_w_qkv": cast_bf16(T["att_w_qkv"][0][0], name="att_w_qkv_cast"),
        "att_w_o": cast_bf16(W["att_w_o"][0], name="att_w_o_cast"),
        "ffn_w_in1": cast_bf16_layer(T["ffn_w_in"][0], 1, name="ffn_w_in1_cast"),
        "ffn_w_out1": cast_bf16_layer(W["ffn_w_out"], 1, name="ffn_w_out1_cast"),
    }
    gather_handles, token = copies_start(list(shards.values()), False, name="gather_start")
    gather_handle = dict(zip(shards, gather_handles))

    def gathered(key, after, cols):
        own, land = copies_wait(gather_handle[key], False, after, name=key + "_gwait")
        return (cols_from_shards if cols else rows_from_shards)(land, own, name=key + "_asm")

    conv_b = W["ffn_conv_b"].reshape(2, 1, -1)

    s5_params = (W["s5_A_re"][0], W["s5_A_im"][0], W["s5_log_dt"][0], W["s5_B_re"][0], W["s5_B_im"][0],
                 W["s5_C_re"][0], W["s5_C_im"][0])
    (a_re, a_im, wb_re, wb_im, wc_re, wc_im), s5_prep_vjp = jax.vjp(_s5_prepare, *s5_params)
    dvec = W["s5_D"].reshape(1, S5_WIDTH)
    glu_b = W["s5_glu_b"].reshape(1, S5_WIDTH)
    lb, lb_vjp = jax.vjp(_lower_bound, W["hgrn_gamma"])
    hg_norm = W["hgrn_norm"].reshape(1, -1)
    tabs = rope_tables(positions)

    hn0 = rms_fwd(x, W["norm_mix"][0], dep=token, name="l0_rms")
    w_mix_in = gathered("mix_w_in", hn0, False)
    proj = mm(hn0, w_mix_in, tb=True, name="l0_proj")
    u_bf = cast_bf16(proj, ncols=S5_WIDTH, name="l0_u_cast")
    bu_re = mm(u_bf, wb_re, name="s5_bu_re")
    bu_im = mm(u_bf, wb_im, name="s5_bu_im")
    xs_re, xs_im = s5_scan_fwd(a_re, a_im, bu_re, bu_im, name="s5_scan")
    y0 = mm(xs_im, wc_im, res=mm(xs_re, wc_re, name="s5_y_re"), name="s5_y_im")
    w_glu = gathered("s5_glu_w", y0, False)
    oa = s5_out_fwd(y0, proj, dvec, w_glu, glu_b, name="s5_out")
    ob, hg_states = hgrn_fwd(proj, lb, hg_norm, name="hgrn_fwd")
    cat = jnp.concatenate([oa, ob], axis=1)
    w_mix_out = gathered("mix_w_out", cat, False)
    h1 = mm(cat, w_mix_out, res=x, name="l0_mix_out")
    cw_own, cw_land = copies_wait(gather_handle["conv_w"], False, h1, name="conv_w_gwait")
    cw_all = lax.dynamic_update_slice(cw_land, cw_own[None], (me, 0, 0))
    conv_w = cw_all.transpose(1, 0, 2).reshape(2, 3, N_DEV * n_cw)
    w_ffn_in, w_ffn_out = [None, None], [None, None]
    h2, ffn0_saved, w_ffn_in[0], w_ffn_out[0] = _ffn_fwd(
        h1, W["norm_ffn"][0], lambda a: gathered("ffn_w_in0", a, False), conv_w[0], conv_b[0],
        lambda a: gathered("ffn_w_out0", a, False), "ffn0")

    hn2 = rms_fwd(h2, W["norm_mix"][1], name="l1_rms")
    w_qkv = gathered("att_w_qkv", hn2, False)
    qkv = mm(hn2, w_qkv, tb=True, name="l1_qkv")
    qkv_r = rope_fwd(qkv, tabs, name="rope_fwd")
    att_in, att_o, att_l = [], [], []
    for g, d in enumerate(ATT_DILATIONS):
        qp = _to_branch_order(qkv_r[:, 512 * g:512 * (g + 1)], d)
        kp = _to_branch_order(qkv_r[:, 1536 + 512 * g:1536 + 512 * (g + 1)], d)
        vp = _to_branch_order(qkv_r[:, 3072 + 512 * g:3072 + 512 * (g + 1)], d)
        o_p, l_p = attn_fwd(qp, kp, vp, d, name=f"attn_fwd{g}")
        att_in.append((qp, kp, vp, l_p))
        att_o.append(_to_token_order(o_p, d))
        att_l.append(_to_token_order(l_p, d))
    o_att = merge_fwd(att_o, att_l, name="merge_fwd")
    w_o = gathered("att_w_o", o_att, True)
    h3 = mm(o_att, w_o, res=h2, name="l1_mix_out")
    h4, ffn1_saved, w_ffn_in[1], w_ffn_out[1] = _ffn_fwd(
        h3, W["norm_ffn"][1], lambda a: gathered("ffn_w_in1", a, False), conv_w[1], conv_b[1],
        lambda a: gathered("ffn_w_out1", a, False), "ffn1")

    exchanges = {}

    def send_grad(key, g, cols):
        if cols:
            parts = shards_from_cols(g, name=key + "_split")
        else:
            parts = cast_bf16(g, name=key + "_gcast").reshape(N_DEV, g.shape[0] // N_DEV, g.shape[1])
        (handle,), sent = copies_start([parts], True, name=key + "_xstart")
        exchanges[key] = handle
        return sent

    loss, dh4, dg_final = final_loss(h4, W["norm_final"], target, name="final_loss")
    dh3, dg_ffn1, dcw1, dcb1 = _ffn_bwd(h3, W["norm_ffn"][1], w_ffn_in[1], conv_w[1], conv_b[1], w_ffn_out[1],
                                        ffn1_saved, dh4, "ffn1", lambda g: send_grad("ffn_w_in1", g, False),
                                        lambda g: send_grad("ffn_w_out1", g, False))
    sent = send_grad("att_w_o", mm(o_att, dh3, ta=True, name="l1_dwo"), True)
    d_oatt = mm(dh3, w_o, tb=True, dep=sent, name="l1_dmix")
    mb = merge_bwd(att_o, att_l, d_oatt, name="merge_bwd")
    dq_t, dk_t, dv_t = [], [], []
    for g, d in enumerate(ATT_DILATIONS):
        qp, kp, vp, l_p = att_in[g]
        dq_p, dk_p, dv_p = attn_bwd(qp, kp, vp, l_p, _to_branch_order(mb[g], d), _to_branch_order(mb[3 + g], d), d,
                                    name=f"attn_bwd{g}")
        dq_t.append(_to_token_order(dq_p, d))
        dk_t.append(_to_token_order(dk_p, d))
        dv_t.append(_to_token_order(dv_p, d))
    d_qkv = rope_bwd(jnp.concatenate(dq_t, axis=1), jnp.concatenate(dk_t, axis=1), jnp.concatenate(dv_t, axis=1),
                     tabs, name="rope_bwd")
    sent = send_grad("att_w_qkv", mm(d_qkv, hn2, ta=True, name="l1_dwqkv"), False)
    d_hn2 = mm(d_qkv, w_qkv, dep=sent, name="l1_dhn")
    dh2, dg_mix1 = rms_bwd(h2, W["norm_mix"][1], d_hn2, dh3, name="l1_drms")

    dh1, dg_ffn0, dcw0, dcb0 = _ffn_bwd(h1, W["norm_ffn"][0], w_ffn_in[0], conv_w[0], conv_b[0], w_ffn_out[0],
                                        ffn0_saved, dh2, "ffn0", lambda g: send_grad("ffn_w_in0", g, False),
                                        lambda g: send_grad("ffn_w_out0", g, False))
    sent = send_grad("mix_w_out", mm(cat, dh1, ta=True, name="l0_dwout"), False)
    dcat = mm(dh1, w_mix_out, tb=True, dep=sent, name="l0_dcat")
    d_hg, dlb, dhg_norm = hgrn_bwd(proj, lb, hg_norm, hg_states, dcat, name="hgrn_bwd")
    dy, du_d, z_bf, dzg, dglu_b, dD = s5_out_bwd(y0, proj, dvec, w_glu, glu_b, dcat, name="s5_dout")
    sent = send_grad("s5_glu_w", mm(z_bf, dzg, ta=True, name="s5_dglu"), False)
    dxs_re = mm(dy, wc_re, tb=True, dep=sent, name="s5_dxs_re")
    dxs_im = mm(dy, wc_im, tb=True, name="s5_dxs_im")
    dwc_re = mm(xs_re, dy, ta=True, name="s5_dwc_re")
    dwc_im = mm(xs_im, dy, ta=True, name="s5_dwc_im")
    dbu_re, dbu_im, da_re, da_im = s5_scan_bwd(a_re, a_im, xs_re, xs_im, dxs_re, dxs_im, name="s5_dscan")
    du = mm(dbu_im, wb_im, tb=True, res=mm(dbu_re, wb_re, tb=True, res=du_d, name="s5_du_re"), out_dtype=BF16,
            name="s5_du_im")
    dwb_re = mm(u_bf, dbu_re, ta=True, name="s5_dwb_re")
    dwb_im = mm(u_bf, dbu_im, ta=True, name="s5_dwb_im")
    s5_small = s5_prep_vjp((da_re, da_im, dwb_re, dwb_im, dwc_re, dwc_im))
    d_proj = jnp.concatenate([du, d_hg], axis=1)
    sent = send_grad("mix_w_in", mm(d_proj, hn0, ta=True, name="l0_dwin"), False)
    d_hn0 = mm(d_proj, w_mix_in, dep=sent, name="l0_dhn")
    grad_x, dg_mix0 = rms_bwd(x, W["norm_mix"][0], d_hn0, dh1, name="l0_drms")
    (d_gamma,) = lb_vjp(dlb)
    out = {}

    dA_re, dA_im, dlog_dt, dB_re, dB_im, dC_re, dC_im = s5_small
    small_g = dict(norm_mix=jnp.concatenate([dg_mix0, dg_mix1], axis=0), norm_ffn=jnp.concatenate([dg_ffn0, dg_ffn1], axis=0),
                   norm_final=dg_final, s5_A_re=dA_re, s5_A_im=dA_im, s5_log_dt=dlog_dt, s5_B_re=dB_re, s5_B_im=dB_im,
                   s5_C_re=dC_re, s5_C_im=dC_im, s5_D=dD, s5_glu_b=dglu_b, hgrn_gamma=d_gamma, hgrn_norm=dhg_norm,
                   ffn_conv_b=jnp.concatenate([dcb0, dcb1], axis=0))
    conv_w_g = jnp.stack([dcw0, dcw1], axis=0)
    sizes = [math.prod(W[n].shape) for n in SMALL]
    n_conv = conv_w_g.size
    total = sum(sizes) + n_conv + 1
    rows = -(-total // PACK_COLS)
    rows = -(-rows // 8) * 8
    pad = rows * PACK_COLS - total

    def pack(vals, conv_part, last):
        flat = [v.reshape(-1).astype(F32) for v in vals] + [conv_part.reshape(-1), last.reshape(-1),
                                                            jnp.zeros((pad,), F32)]
        return jnp.concatenate(flat).reshape(rows, PACK_COLS)

    def conv_full(shard):
        col_owner = lax.broadcasted_iota(jnp.int32, (2, 3, N_DEV * n_cw), 2) // n_cw
        return jnp.where(col_owner == me, jnp.tile(shard, (1, 1, N_DEV)), 0.0)

    zero1 = jnp.zeros((1,), F32)
    g_pack = pack([small_g[n] for n in SMALL], conv_w_g, loss)
    w_pack = pack([W[n] for n in SMALL], conv_full(W["ffn_conv_w"]), zero1)
    m_pack = pack([M[n] for n in SMALL], conv_full(M["ffn_conv_w"]), zero1)
    v_pack = pack([V[n] for n in SMALL], conv_full(V["ffn_conv_w"]), zero1 + 1.0)
    (small_handle,), small_sent = copies_start([g_pack], False, name="small_xstart")

    def finish(name, n_layers):
        w3, m3, v3 = T[name] if name in TRANSPOSED else (W[name], M[name], V[name])
        res = None
        for layer in reversed(range(n_layers)):
            key = name if n_layers == 1 else f"{name}{layer}"
            own, recv = copies_wait(exchanges[key], True, small_sent, name=key + "_xwait")
            _, R, Cn = recv.shape
            res = reduce_adamw(recv, own, True, me, w3.reshape(n_layers * R, Cn), m3.reshape(n_layers * R, Cn),
                               v3.reshape(n_layers * R, Cn), layer=layer, n_layers=n_layers, into=res,
                               name=key + "_adamw")
        res = [r.reshape(w3.shape) for r in res]
        return tuple(jnp.swapaxes(r, -1, -2) for r in res) if name in TRANSPOSED else tuple(res)

    for name in ("ffn_w_out", "ffn_w_in"):
        out[name] = finish(name, 2)
    for name in ("att_w_o", "att_w_qkv", "mix_w_out", "s5_glu_w", "mix_w_in"):
        out[name] = finish(name, 1)

    small_own, small_recv = copies_wait(small_handle, False, out["s5_glu_w"][0], name="small_xwait")
    res = reduce_adamw(small_recv, small_own, False, me, w_pack, m_pack, v_pack, name="small_adamw")
    flat = [r.reshape(-1) for r in res]
    off = 0
    for n, sz in zip(SMALL, sizes):
        out[n] = tuple(f[off:off + sz].reshape(W[n].shape) for f in flat)
        off += sz
    conv_res = [f[off:off + n_conv].reshape(2, 3, N_DEV * n_cw) for f in flat]
    out["ffn_conv_w"] = tuple(lax.dynamic_slice(c, (0, 0, me * n_cw), (2, 3, n_cw)) for c in conv_res)
    off += n_conv
    loss_total = flat[0][off]

    result = [loss_total, grad_x[None]]
    for k in range(4):
        result += [out[n][k] for n in ORDER]
    return tuple(result)
```

```python
import functools
import math

import jax
import jax.numpy as jnp
from jax import lax
from jax.experimental import pallas as pl
from jax.experimental.pallas import tpu as pltpu

F32 = jnp.float32
BF16 = jnp.bfloat16
MESH_ID = pl.DeviceIdType.MESH
N_DEV = 8
VMEM_LIMIT_BYTES = 56 * 1024 * 1024

NORM_EPS = 1e-6
S5_WIDTH, S5_GROUP, S5_GROUPS, S5_STATE = 512, 16, 32, 64
HG_HEADS, HG_DIM, HG_CHUNK = 4, 128, 64
ATT_E, ATT_HPG, ATT_BLOCK = 64, 8, 128
ATT_DILATIONS = (1, 4, 16)
ROT_DIM, ROPE_THETA = 16, 500000.0
D_FF = 2816
ADAM_LR, ADAM_B1, ADAM_B2, ADAM_EPS, ADAM_WD, ADAM_STEP = 0.001, 0.9, 0.999, 1e-08, 0.01, 10
NEG_BIG = -1e30


def _params(**kw):
    return pltpu.CompilerParams(vmem_limit_bytes=VMEM_LIMIT_BYTES, **kw)


def _pick(n, cands):
    for c in cands:
        if n % c == 0:
            return c
    return n


def _dot(a, b):
    return jnp.dot(a.astype(BF16), b.astype(BF16), preferred_element_type=F32)


def _dot_nt(a, b):
    return lax.dot_general(a.astype(BF16), b.astype(BF16), (((1,), (1,)), ((), ())), preferred_element_type=F32)


def _dot_tn(a, b):
    return lax.dot_general(a.astype(BF16), b.astype(BF16), (((0,), (0,)), ((), ())), preferred_element_type=F32)


def _dot_f32(a, b):
    return jnp.dot(a, b, preferred_element_type=F32, precision=lax.Precision.HIGHEST)


def _dot_f32_nt(a, b):
    return lax.dot_general(a, b, (((1,), (1,)), ((), ())), preferred_element_type=F32, precision=lax.Precision.HIGHEST)


def _dot_f32_tn(a, b):
    return lax.dot_general(a, b, (((0,), (0,)), ((), ())), preferred_element_type=F32, precision=lax.Precision.HIGHEST)


def _sigmoid(x):
    return 1.0 / (1.0 + jnp.exp(-x))


V7X_HBM_BYTES_PER_S = 3.2e12
V7X_MXU_FLOPS_PER_S = 0.7e15
GRID_STEP_S = 0.35e-6
MM_VMEM_BUDGET = 40 * 1024 * 1024


def _divisors(n, cands):
    return [c for c in cands if c <= n and n % c == 0] or [n]


def _mm_tiles(m, n, k, sa, sb, so, sr):
    best = None
    for tm in _divisors(m, (2816, 2048, 1408, 1024, 512, 256, 128)):
        for tn in _divisors(n, (2816, 2048, 1408, 1024, 512, 256, 128)):
            for tk in _divisors(k, (k, 2816, 2560, 2304, 2048, 1536, 1408, 1280, 1024, 512, 256, 128)):
                nk = k // tk
                vmem = 2 * (tm * tk * sa + tk * tn * sb + tm * tn * (so + sr)) + (tm * tn * 4 if nk > 1 else 0)
                vmem += tm * tk * 2 * (sa > 2) + tk * tn * 2 * (sb > 2) + tm * tn * 4
                if vmem > MM_VMEM_BUDGET:
                    continue
                ni, nj = m // tm, n // tn
                for i_outer in (True, False):
                    if i_outer:
                        a_reads = 1 if nk == 1 else nj
                        b_reads = 1 if (nk == 1 and nj == 1) else ni
                    else:
                        b_reads = 1 if nk == 1 else ni
                        a_reads = 1 if (nk == 1 and ni == 1) else nj
                    traffic = a_reads * m * k * sa + b_reads * k * n * sb + m * n * (so + sr)
                    t = max(traffic / V7X_HBM_BYTES_PER_S, 2.0 * m * n * k / V7X_MXU_FLOPS_PER_S)
                    t += ni * nj * nk * GRID_STEP_S
                    t += (tm * tk * sa + tk * tn * sb + tm * tn * so) / V7X_HBM_BYTES_PER_S
                    if best is None or t < best[0]:
                        best = (t, tm, tn, tk, i_outer)
    assert best is not None, (m, n, k)
    return best[1:]


def mm(a, b, *, ta=False, tb=False, res=None, out_dtype=F32, dep=None, name):
    m, k = (a.shape[1], a.shape[0]) if ta else a.shape
    n = b.shape[0] if tb else b.shape[1]
    assert (b.shape[1] if tb else b.shape[0]) == k
    has_res = res is not None
    tm, tn, tk, i_outer = _mm_tiles(m, n, k, a.dtype.itemsize, b.dtype.itemsize, jnp.dtype(out_dtype).itemsize,
                                    res.dtype.itemsize if has_res else 0)
    nk = k // tk
    deps = [] if dep is None else [dep]
    dn = (((0 if ta else 1,), (1 if tb else 0,)), ((), ()))

    def body_single(*refs):
        a_ref, b_ref = refs[:2]
        o_ref = refs[-1]
        out = lax.dot_general(a_ref[...].astype(BF16), b_ref[...].astype(BF16), dn, preferred_element_type=F32)
        if has_res:
            out = out + refs[2][...].astype(F32)
        o_ref[...] = out.astype(o_ref.dtype)

    def body(*refs):
        a_ref, b_ref = refs[:2]
        r_ref = refs[2] if has_res else None
        o_ref, acc_ref = refs[-2:]
        kk = pl.program_id(2)
        part = lax.dot_general(a_ref[...].astype(BF16), b_ref[...].astype(BF16), dn, preferred_element_type=F32)

        @pl.when(kk == 0)
        def _():
            acc_ref[...] = part

        @pl.when(kk > 0)
        def _():
            acc_ref[...] += part

        @pl.when(kk == nk - 1)
        def _():
            out = acc_ref[...]
            if has_res:
                out = out + r_ref[...].astype(F32)
            o_ref[...] = out.astype(o_ref.dtype)

    def ij(f):
        return (lambda g0, g1, q: f(g0, g1, q)) if i_outer else (lambda g0, g1, q: f(g1, g0, q))

    a_spec = pl.BlockSpec((tk, tm), ij(lambda i, j, q: (q, i))) if ta else pl.BlockSpec((tm, tk), ij(lambda i, j, q: (i, q)))
    b_spec = pl.BlockSpec((tn, tk), ij(lambda i, j, q: (j, q))) if tb else pl.BlockSpec((tk, tn), ij(lambda i, j, q: (q, j)))
    o_spec = pl.BlockSpec((tm, tn), ij(lambda i, j, q: (i, j)))
    in_specs = [a_spec, b_spec] + ([o_spec] if has_res else []) + [pl.BlockSpec((8, 128), lambda g0, g1, q: (0, 0))] * len(deps)
    args = (a, b) + ((res,) if has_res else ()) + tuple(deps)
    grid = (m // tm, n // tn, nk) if i_outer else (n // tn, m // tm, nk)
    return pl.pallas_call(
        body_single if nk == 1 else body, grid=grid, in_specs=in_specs, out_specs=o_spec,
        out_shape=jax.ShapeDtypeStruct((m, n), out_dtype),
        scratch_shapes=[] if nk == 1 else [pltpu.VMEM((tm, tn), F32)],
        compiler_params=_params(dimension_semantics=("parallel", "parallel", "arbitrary")), name=name,
    )(*args)


def rms_fwd(x, g, *, dep=None, name):
    L, D = x.shape
    tr = _pick(L, (256, 128))

    def body(x_ref, g_ref, *rest):
        o_ref = rest[-1]
        xv = x_ref[...]
        r = lax.rsqrt(jnp.mean(xv * xv, axis=-1, keepdims=True) + NORM_EPS)
        o_ref[...] = (xv * r * g_ref[...]).astype(o_ref.dtype)

    row = pl.BlockSpec((tr, D), lambda i: (i, 0))
    vec = pl.BlockSpec((1, D), lambda i: (0, 0))
    deps = [] if dep is None else [dep]
    return pl.pallas_call(body, grid=(L // tr,), in_specs=[row, vec] + [pl.BlockSpec((8, 128), lambda i: (0, 0))] * len(deps),
                          out_specs=row, out_shape=jax.ShapeDtypeStruct((L, D), BF16), name=name)(
        x, g.reshape(1, D), *deps)


def rms_bwd(x, g, dy, dres, *, name):
    L, D = x.shape
    tr = _pick(L, (256, 128))

    def body(x_ref, g_ref, dy_ref, dres_ref, dx_ref, dg_ref):
        xv = x_ref[...]
        r = lax.rsqrt(jnp.mean(xv * xv, axis=-1, keepdims=True) + NORM_EPS)
        xh = xv * r
        dyv = dy_ref[...].astype(F32)

        @pl.when(pl.program_id(0) == 0)
        def _():
            dg_ref[...] = jnp.zeros_like(dg_ref)

        dg_ref[...] += jnp.sum(dyv * xh, axis=0, keepdims=True)
        dxh = dyv * g_ref[...]
        dx_ref[...] = dres_ref[...] + r * (dxh - xh * jnp.mean(dxh * xh, axis=-1, keepdims=True))

    row = pl.BlockSpec((tr, D), lambda i: (i, 0))
    vec = pl.BlockSpec((1, D), lambda i: (0, 0))
    return pl.pallas_call(body, grid=(L // tr,), in_specs=[row, vec, row, row], out_specs=[row, vec],
                          out_shape=[jax.ShapeDtypeStruct((L, D), F32), jax.ShapeDtypeStruct((1, D), F32)],
                          compiler_params=_params(dimension_semantics=("arbitrary",)), name=name)(
        x, g.reshape(1, D), dy, dres)


def final_loss(h, g, target, *, name):
    L, D = h.shape
    tr = _pick(L, (256, 128))

    def body(x_ref, g_ref, t_ref, loss_ref, dx_ref, dg_ref):
        xv = x_ref[...]
        gv = g_ref[...]
        r = lax.rsqrt(jnp.mean(xv * xv, axis=-1, keepdims=True) + NORM_EPS)
        xh = xv * r
        err = xh * gv - t_ref[...]

        @pl.when(pl.program_id(0) == 0)
        def _():
            dg_ref[...] = jnp.zeros_like(dg_ref)
            loss_ref[...] = jnp.zeros_like(loss_ref)

        loss_ref[...] += 0.5 * jnp.sum(jnp.mean(err * err, axis=-1, keepdims=True), axis=0, keepdims=True)
        dyv = err * (1.0 / D)
        dg_ref[...] += jnp.sum(dyv * xh, axis=0, keepdims=True)
        dxh = dyv * gv
        dx_ref[...] = r * (dxh - xh * jnp.mean(dxh * xh, axis=-1, keepdims=True))

    row = pl.BlockSpec((tr, D), lambda i: (i, 0))
    vec = pl.BlockSpec((1, D), lambda i: (0, 0))
    one = pl.BlockSpec((1, 1), lambda i: (0, 0))
    return pl.pallas_call(body, grid=(L // tr,), in_specs=[row, vec, row], out_specs=[one, row, vec],
                          out_shape=[jax.ShapeDtypeStruct((1, 1), F32), jax.ShapeDtypeStruct((L, D), F32),
                                     jax.ShapeDtypeStruct((1, D), F32)],
                          compiler_params=_params(dimension_semantics=("arbitrary",)), name=name)(
        h, g.reshape(1, D), target)


def _cmul(ar, ai, br, bi):
    return ar * br - ai * bi, ar * bi + ai * br


def _powers(ar, ai):
    rows = [(ar, ai)]
    for _ in range(7):
        rows.append(_cmul(rows[-1][0], rows[-1][1], ar, ai))
    table = (jnp.concatenate([r[0] for r in rows], axis=0), jnp.concatenate([r[1] for r in rows], axis=0))
    return (rows[0], rows[1], rows[3]), table


def _block_scan(br, bi, steps, shift):
    yr, yi = br, bi
    for s, (pr, pi) in zip((1, 2, 4), steps):
        sr, si = shift(yr, s), shift(yi, s)
        yr, yi = yr + pr * sr - pi * si, yi + pr * si + pi * sr
    return yr, yi


def s5_scan_fwd(a_re, a_im, bu_re, bu_im, *, name):
    L, P = bu_re.shape
    W = _pick(P, (512, 256, 128))

    def body(ar_ref, ai_ref, br_ref, bi_ref, xr_ref, xi_ref):
        steps, (tr, ti) = _powers(ar_ref[...], ai_ref[...])
        row = lax.broadcasted_iota(jnp.int32, (8, W), 0)

        def shift(y, s):
            return jnp.where(row >= s, pltpu.roll(y, s, 0), 0.0)

        def step(t8, carry):
            cr, ci = carry
            base = pl.multiple_of(t8 * 8, 8)
            yr, yi = _block_scan(br_ref[pl.ds(base, 8), :], bi_ref[pl.ds(base, 8), :], steps, shift)
            xr = yr + tr * cr - ti * ci
            xi = yi + tr * ci + ti * cr
            xr_ref[pl.ds(base, 8), :] = xr
            xi_ref[pl.ds(base, 8), :] = xi
            return jnp.broadcast_to(xr[7:8, :], (8, W)), jnp.broadcast_to(xi[7:8, :], (8, W))

        zero = jnp.zeros((8, W), F32)
        lax.fori_loop(0, L // 8, step, (zero, zero), unroll=2)

    vec = pl.BlockSpec((1, W), lambda j: (0, j))
    col = pl.BlockSpec((L, W), lambda j: (0, j))
    return pl.pallas_call(body, grid=(P // W,), in_specs=[vec, vec, col, col], out_specs=[col, col],
                          out_shape=[jax.ShapeDtypeStruct((L, P), F32)] * 2,
                          compiler_params=_params(dimension_semantics=("parallel",)), name=name)(
        a_re, a_im, bu_re, bu_im)


def s5_scan_bwd(a_re, a_im, xs_re, xs_im, dx_re, dx_im, *, name):
    L, P = xs_re.shape
    W = _pick(P, (256, 128))

    def body(ar_ref, ai_ref, xr_ref, xi_ref, dr_ref, di_ref, lr_ref, li_ref, dar_ref, dai_ref):
        ar, ai = ar_ref[...], -ai_ref[...]
        steps, (tr, ti) = _powers(ar, ai)
        tr = jnp.concatenate([tr[j:j + 1, :] for j in range(7, -1, -1)], axis=0)
        ti = jnp.concatenate([ti[j:j + 1, :] for j in range(7, -1, -1)], axis=0)
        row8 = lax.broadcasted_iota(jnp.int32, (8, W), 0)
        nblk = L // 8

        def shift(y, s):
            return jnp.where(row8 < 8 - s, pltpu.roll(y, 8 - s, 0), 0.0)

        def step(s, carry):
            cr, ci = carry
            base = pl.multiple_of((nblk - 1 - s) * 8, 8)
            yr, yi = _block_scan(dr_ref[pl.ds(base, 8), :], di_ref[pl.ds(base, 8), :], steps, shift)
            lr = yr + tr * cr - ti * ci
            li = yi + tr * ci + ti * cr
            lr_ref[pl.ds(base, 8), :] = lr
            li_ref[pl.ds(base, 8), :] = li
            return jnp.broadcast_to(lr[0:1, :], (8, W)), jnp.broadcast_to(li[0:1, :], (8, W))

        zero = jnp.zeros((8, W), F32)
        lax.fori_loop(0, nblk, step, (zero, zero), unroll=2)
        row = lax.broadcasted_iota(jnp.int32, (L, W), 0)
        xpr = jnp.where(row >= 1, pltpu.roll(xr_ref[...], 1, 0), 0.0)
        xpi = jnp.where(row >= 1, pltpu.roll(xi_ref[...], 1, 0), 0.0)
        lr, li = lr_ref[...], li_ref[...]
        dar_ref[...] = jnp.sum(lr * xpr + li * xpi, axis=0, keepdims=True)
        dai_ref[...] = jnp.sum(li * xpr - lr * xpi, axis=0, keepdims=True)

    vec = pl.BlockSpec((1, W), lambda j: (0, j))
    col = pl.BlockSpec((L, W), lambda j: (0, j))
    return pl.pallas_call(body, grid=(P // W,), in_specs=[vec, vec, col, col, col, col],
                          out_specs=[col, col, vec, vec],
                          out_shape=[jax.ShapeDtypeStruct((L, P), F32)] * 2 + [jax.ShapeDtypeStruct((1, P), F32)] * 2,
                          compiler_params=_params(dimension_semantics=("parallel",)), name=name)(
        a_re, a_im, xs_re, xs_im, dx_re, dx_im)


def _gelu(y):
    c = math.sqrt(2.0 / math.pi)
    t = jnp.tanh(c * (y + 0.044715 * y * y * y))
    return 0.5 * y * (1.0 + t), t


def s5_out_fwd(y0, proj, dvec, glu_w, glu_b, *, name):
    L, C = y0.shape
    tr = _pick(L, (256, 128))

    def body(y_ref, u_ref, d_ref, w_ref, b_ref, o_ref):
        z, _ = _gelu(y_ref[...] + d_ref[...] * u_ref[...])
        zg = _dot(z, w_ref[...]) + b_ref[...]
        o_ref[...] = (z * _sigmoid(zg)).astype(o_ref.dtype)

    row = pl.BlockSpec((tr, C), lambda i: (i, 0))
    vec = pl.BlockSpec((1, C), lambda i: (0, 0))
    wsp = pl.BlockSpec((C, C), lambda i: (0, 0))
    return pl.pallas_call(body, grid=(L // tr,), in_specs=[row, row, vec, wsp, vec], out_specs=row,
                          out_shape=jax.ShapeDtypeStruct((L, C), BF16), name=name)(
        y0, proj, dvec, glu_w, glu_b)


def s5_out_bwd(y0, proj, dvec, glu_w, glu_b, dcat, *, name):
    L, C = y0.shape
    tr = _pick(L, (256, 128))

    def body(y_ref, u_ref, d_ref, w_ref, b_ref, do_ref, dy_ref, dud_ref, z_ref, dzg_ref, db_ref, dd_ref):
        u = u_ref[...]
        y = y_ref[...] + d_ref[...] * u
        z, t = _gelu(y)
        zg = _dot(z, w_ref[...]) + b_ref[...]
        s = _sigmoid(zg)
        do = do_ref[...]
        dzg = do * z * s * (1.0 - s)
        dz = do * s + _dot_nt(dzg, w_ref[...])
        c = math.sqrt(2.0 / math.pi)
        dgelu = 0.5 * (1.0 + t) + 0.5 * y * (1.0 - t * t) * c * (1.0 + 3.0 * 0.044715 * y * y)
        dy = dz * dgelu

        @pl.when(pl.program_id(0) == 0)
        def _():
            db_ref[...] = jnp.zeros_like(db_ref)
            dd_ref[...] = jnp.zeros_like(dd_ref)

        db_ref[...] += jnp.sum(dzg, axis=0, keepdims=True)
        dd_ref[...] += jnp.sum(dy * u, axis=0, keepdims=True)
        dy_ref[...] = dy
        dud_ref[...] = dy * d_ref[...]
        z_ref[...] = z.astype(BF16)
        dzg_ref[...] = dzg.astype(BF16)

    row = pl.BlockSpec((tr, C), lambda i: (i, 0))
    vec = pl.BlockSpec((1, C), lambda i: (0, 0))
    wsp = pl.BlockSpec((C, C), lambda i: (0, 0))
    return pl.pallas_call(body, grid=(L // tr,), in_specs=[row, row, vec, wsp, vec, row],
                          out_specs=[row, row, row, row, vec, vec],
                          out_shape=[jax.ShapeDtypeStruct((L, C), F32), jax.ShapeDtypeStruct((L, C), F32),
                                     jax.ShapeDtypeStruct((L, C), BF16), jax.ShapeDtypeStruct((L, C), BF16),
                                     jax.ShapeDtypeStruct((1, C), F32), jax.ShapeDtypeStruct((1, C), F32)],
                          compiler_params=_params(dimension_semantics=("arbitrary",)), name=name)(
        y0, proj, dvec, glu_w, glu_b, dcat)


def _hg_gates(xq, xf, lb, tri):
    C = xq.shape[0]
    sq = _sigmoid(xq)
    q = xq * sq
    sg = _sigmoid(xf)
    f = lb + (1.0 - lb) * sg
    kk = 1.0 - f
    b = _dot_f32(tri, jnp.log(f))
    bm = b[C // 2 - 1:C // 2, :]
    bl = b[C - 1:C, :]
    eb = jnp.exp(b)
    return dict(sq=sq, q=q, sg=sg, f=f, kk=kk, b=b, bm=bm, bl=bl, eb=eb, ebl=jnp.exp(bl),
                qb=q * eb, eqm=jnp.exp(b - bm), ekm=jnp.exp(bm - b), ekl=jnp.exp(bl - b))


def _tri(C, lower):
    r = lax.broadcasted_iota(jnp.int32, (C, C), 0)
    c = lax.broadcasted_iota(jnp.int32, (C, C), 1)
    return (r >= c) if lower else (c >= r)


def hgrn_fwd(proj, lb, norm_g, *, name):
    L = proj.shape[0]
    C, H, K = HG_CHUNK, HG_HEADS, HG_DIM
    HK = H * K
    nc = L // C

    def body(q_ref, f_ref, i_ref, g_ref, lb_ref, ng_ref, o_ref, sall_ref, st_ref):
        @pl.when(pl.program_id(0) == 0)
        def _():
            st_ref[...] = jnp.zeros_like(st_ref)

        mask = _tri(C, True)
        tri = mask.astype(F32)
        for h in range(H):
            sl = slice(h * K, (h + 1) * K)
            v = i_ref[:, sl]
            st = st_ref[h]
            sall_ref[h] = st
            gt = _hg_gates(q_ref[:, sl], f_ref[:, sl], lb_ref[:, sl], tri)
            qt = gt["q"] * gt["eqm"]
            kt = gt["kk"] * gt["ekm"]
            kh = gt["kk"] * gt["ekl"]
            att = jnp.where(mask, _dot_nt(qt, kt), 0.0)
            o = _dot(att, v) + _dot_nt(gt["qb"], st)
            st_ref[h] = st * gt["ebl"] + _dot_tn(v, kh)
            r = lax.rsqrt(jnp.mean(o * o, axis=-1, keepdims=True) + NORM_EPS)
            xg = g_ref[:, sl]
            o_ref[:, sl] = (o * r * ng_ref[:, sl] * (xg * _sigmoid(xg))).astype(o_ref.dtype)

    def blk(cb):
        return pl.BlockSpec((C, HK), lambda i: (i, cb))

    vec = pl.BlockSpec((1, HK), lambda i: (0, 0))
    return pl.pallas_call(
        body, grid=(nc,), in_specs=[blk(1), blk(2), blk(3), blk(4), vec, vec],
        out_specs=[pl.BlockSpec((C, HK), lambda i: (i, 0)), pl.BlockSpec((None, H, K, K), lambda i: (i, 0, 0, 0))],
        out_shape=[jax.ShapeDtypeStruct((L, HK), BF16), jax.ShapeDtypeStruct((nc, H, K, K), F32)],
        scratch_shapes=[pltpu.VMEM((H, K, K), F32)],
        compiler_params=_params(dimension_semantics=("arbitrary",)), name=name,
    )(proj, proj, proj, proj, lb, norm_g)


def hgrn_bwd(proj, lb, norm_g, sall, dcat, *, name):
    L = proj.shape[0]
    C, H, K = HG_CHUNK, HG_HEADS, HG_DIM
    HK = H * K
    nc = L // C

    def body(q_ref, f_ref, i_ref, g_ref, lb_ref, ng_ref, sall_ref, do_ref, dx_ref, dlb_ref, dng_ref, dst_ref):
        @pl.when(pl.program_id(0) == 0)
        def _():
            dst_ref[...] = jnp.zeros_like(dst_ref)
            dlb_ref[...] = jnp.zeros_like(dlb_ref)
            dng_ref[...] = jnp.zeros_like(dng_ref)

        mask = _tri(C, True)
        tri = mask.astype(F32)
        tri_t = _tri(C, False).astype(F32)
        rowi = lax.broadcasted_iota(jnp.int32, (C, K), 0)
        for h in range(H):
            sl = slice(h * K, (h + 1) * K)
            xq, xf, v, xg = q_ref[:, sl], f_ref[:, sl], i_ref[:, sl], g_ref[:, sl]
            lb_h, ng = lb_ref[:, sl], ng_ref[:, sl]
            st = sall_ref[h]
            dst = dst_ref[h]
            gt = _hg_gates(xq, xf, lb_h, tri)
            q, kk, qb = gt["q"], gt["kk"], gt["qb"]
            qt = q * gt["eqm"]
            kt = kk * gt["ekm"]
            kh = kk * gt["ekl"]
            att = jnp.where(mask, _dot_nt(qt, kt), 0.0)
            o = _dot(att, v) + _dot_nt(qb, st)
            r = lax.rsqrt(jnp.mean(o * o, axis=-1, keepdims=True) + NORM_EPS)
            oh = o * r
            sgg = _sigmoid(xg)
            silu_g = xg * sgg
            d_ob = do_ref[:, sl]
            d_on = d_ob * silu_g
            dxg = d_ob * (oh * ng) * (sgg * (1.0 + xg * (1.0 - sgg)))
            dng_ref[:, sl] += jnp.sum(d_on * oh, axis=0, keepdims=True)
            doh = d_on * ng
            do = r * (doh - oh * jnp.mean(doh * oh, axis=-1, keepdims=True))
            datt = jnp.where(mask, _dot_nt(do, v), 0.0)
            dv = _dot_tn(att, do) + _dot_nt(kh, dst)
            d_qb = _dot_f32(do, st)
            d_qt = _dot_f32(datt, kt)
            d_kt = _dot_f32_tn(datt, qt)
            d_kh = _dot_f32(v, dst)
            d_bl = jnp.sum(dst * st, axis=0, keepdims=True) * gt["ebl"] + jnp.sum(d_kh * kh, axis=0, keepdims=True)
            dst_ref[h] = dst * gt["ebl"] + _dot_tn(do, qb)
            dq = d_qt * gt["eqm"] + d_qb * gt["eb"]
            db = d_qt * qt + d_qb * qb - d_kt * kt - d_kh * kh
            db = db + jnp.where(rowi == C - 1, d_bl, 0.0)
            dkk = d_kt * gt["ekm"] + d_kh * gt["ekl"]
            dlg = _dot_f32(tri_t, db)
            df = dlg / gt["f"] - dkk
            sg = gt["sg"]
            dxf = df * (1.0 - lb_h) * sg * (1.0 - sg)
            dlb_ref[:, sl] += jnp.sum(df * (1.0 - sg), axis=0, keepdims=True)
            sq = gt["sq"]
            dxq = dq * (sq * (1.0 + xq * (1.0 - sq)))
            dx_ref[:, h * K:(h + 1) * K] = dxq.astype(dx_ref.dtype)
            dx_ref[:, HK + h * K:HK + (h + 1) * K] = dxf.astype(dx_ref.dtype)
            dx_ref[:, 2 * HK + h * K:2 * HK + (h + 1) * K] = dv.astype(dx_ref.dtype)
            dx_ref[:, 3 * HK + h * K:3 * HK + (h + 1) * K] = dxg.astype(dx_ref.dtype)

    def blk(cb):
        return pl.BlockSpec((C, HK), lambda i: (nc - 1 - i, cb))

    vec = pl.BlockSpec((1, HK), lambda i: (0, 0))
    return pl.pallas_call(
        body, grid=(nc,),
        in_specs=[blk(1), blk(2), blk(3), blk(4), vec, vec,
                  pl.BlockSpec((None, H, K, K), lambda i: (nc - 1 - i, 0, 0, 0)), blk(1)],
        out_specs=[pl.BlockSpec((C, 4 * HK), lambda i: (nc - 1 - i, 0)), vec, vec],
        out_shape=[jax.ShapeDtypeStruct((L, 4 * HK), BF16), jax.ShapeDtypeStruct((1, HK), F32),
                   jax.ShapeDtypeStruct((1, HK), F32)],
        scratch_shapes=[pltpu.VMEM((H, K, K), F32)],
        compiler_params=_params(dimension_semantics=("arbitrary",)), name=name,
    )(proj, proj, proj, proj, lb, norm_g, sall, dcat)


def _shift_down(x, k, row):
    return jnp.where(row >= k, pltpu.roll(x, k, 0), 0.0)


def _shift_up(x, k, row):
    n = x.shape[0]
    return jnp.where(row < n - k, pltpu.roll(x, n - k, 0), 0.0)


def convgate_fwd(hu, conv_w, conv_b, *, name):
    L, C2 = hu.shape
    C = C2 // 2
    tc = _pick(C, (256, 128))
    nb = C // tc

    def body(a_ref, b_ref, wa_ref, wb_ref, ba_ref, bb_ref, o_ref):
        row = lax.broadcasted_iota(jnp.int32, (L, tc), 0)

        def conv(x, w, bias):
            return w[2:3, :] * x + w[1:2, :] * _shift_down(x, 1, row) + w[0:1, :] * _shift_down(x, 2, row) + bias

        ca = conv(a_ref[...], wa_ref[...], ba_ref[...])
        cb = conv(b_ref[...], wb_ref[...], bb_ref[...])
        o_ref[...] = (ca * _sigmoid(ca) * cb).astype(o_ref.dtype)

    def col(off, rows):
        return pl.BlockSpec((rows, tc), lambda j: (0, j + off))

    return pl.pallas_call(
        body, grid=(nb,), in_specs=[col(0, L), col(nb, L), col(0, 3), col(nb, 3), col(0, 1), col(nb, 1)],
        out_specs=col(0, L), out_shape=jax.ShapeDtypeStruct((L, C), BF16),
        compiler_params=_params(dimension_semantics=("parallel",)), name=name,
    )(hu, hu, conv_w, conv_w, conv_b, conv_b)


def convgate_bwd(hu, conv_w, conv_b, dact, *, name):
    L, C2 = hu.shape
    C = C2 // 2
    tc = _pick(C, (256, 128))
    nb = C // tc

    def body(a_ref, b_ref, wa_ref, wb_ref, ba_ref, bb_ref, d_ref, dxa_ref, dxb_ref, dwa_ref, dwb_ref, dba_ref, dbb_ref):
        row = lax.broadcasted_iota(jnp.int32, (L, tc), 0)

        def conv(x, w, bias):
            x1 = _shift_down(x, 1, row)
            x2 = _shift_down(x, 2, row)
            return w[2:3, :] * x + w[1:2, :] * x1 + w[0:1, :] * x2 + bias, x1, x2

        xa, xb = a_ref[...], b_ref[...]
        wa, wb = wa_ref[...], wb_ref[...]
        ca, xa1, xa2 = conv(xa, wa, ba_ref[...])
        cb, xb1, xb2 = conv(xb, wb, bb_ref[...])
        d = d_ref[...]
        sa = _sigmoid(ca)
        dca = d * cb * (sa * (1.0 + ca * (1.0 - sa)))
        dcb = d * (ca * sa)

        def back(dc, w, x, x1, x2, dx_ref, dw_ref, db_ref):
            dx = w[2:3, :] * dc + w[1:2, :] * _shift_up(dc, 1, row) + w[0:1, :] * _shift_up(dc, 2, row)
            dx_ref[...] = dx.astype(dx_ref.dtype)
            dw_ref[...] = jnp.concatenate([jnp.sum(dc * x2, axis=0, keepdims=True),
                                           jnp.sum(dc * x1, axis=0, keepdims=True),
                                           jnp.sum(dc * x, axis=0, keepdims=True)], axis=0)
            db_ref[...] = jnp.sum(dc, axis=0, keepdims=True)

        back(dca, wa, xa, xa1, xa2, dxa_ref, dwa_ref, dba_ref)
        back(dcb, wb, xb, xb1, xb2, dxb_ref, dwb_ref, dbb_ref)

    def col(off, rows):
        return pl.BlockSpec((rows, tc), lambda j: (0, j + off))

    outs = pl.pallas_call(
        body, grid=(nb,),
        in_specs=[col(0, L), col(nb, L), col(0, 3), col(nb, 3), col(0, 1), col(nb, 1), col(0, L)],
        out_specs=[col(0, L), col(0, L), col(0, 3), col(0, 3), col(0, 1), col(0, 1)],
        out_shape=[jax.ShapeDtypeStruct((L, C), BF16)] * 2 + [jax.ShapeDtypeStruct((3, C), F32)] * 2
        + [jax.ShapeDtypeStruct((1, C), F32)] * 2,
        compiler_params=_params(dimension_semantics=("parallel",)), name=name,
    )(hu, hu, conv_w, conv_w, conv_b, conv_b, dact)
    dxa, dxb, dwa, dwb, dba, dbb = outs
    return (jnp.concatenate([dxa, dxb], axis=1), jnp.concatenate([dwa, dwb], axis=1),
            jnp.concatenate([dba, dbb], axis=1))


def _to_branch_order(t, d):
    L, W = t.shape
    return t if d == 1 else t.reshape(L // d, d, W).transpose(1, 0, 2).reshape(L, W)


def _to_token_order(t, d):
    L, W = t.shape
    return t if d == 1 else t.reshape(d, L // d, W).transpose(1, 0, 2).reshape(L, W)


def rope_tables(positions):
    half = ROT_DIM // 2
    inv_freq = ROPE_THETA ** (-jnp.arange(half, dtype=F32) * 2.0 / ROT_DIM)
    ang = positions.astype(F32)[:, None] * inv_freq
    cos, sin = jnp.cos(ang), jnp.sin(ang)
    L = positions.shape[0]
    one = jnp.ones((L, ATT_E - ROT_DIM), F32)
    zero = jnp.zeros((L, ATT_E - ROT_DIM), F32)
    zh = jnp.zeros((L, half), F32)
    tc = jnp.concatenate([cos, cos, one], axis=1)
    ts1 = jnp.concatenate([zh, sin, zero], axis=1)
    ts2 = jnp.concatenate([-sin, zh, zero], axis=1)
    return tuple(jnp.concatenate([t, t], axis=1) for t in (tc, ts1, ts2))


def rope_fwd(qkv, tabs, *, name):
    L = qkv.shape[0]
    W = 512
    tr = _pick(L, (256, 128))
    nq = 1536 // W
    scale = ATT_E ** -0.5

    def body(x_ref, c_ref, s1_ref, s2_ref, *o_refs):
        c = jnp.concatenate([c_ref[...]] * 4, axis=1)
        s1 = jnp.concatenate([s1_ref[...]] * 4, axis=1)
        s2 = jnp.concatenate([s2_ref[...]] * 4, axis=1)
        for j, o_ref in enumerate(o_refs):
            x = x_ref[:, j * W:(j + 1) * W]
            if j < 2 * nq:
                x = x * c + pltpu.roll(x, 8, 1) * s1 + pltpu.roll(x, W - 8, 1) * s2
            if j < nq:
                x = x * scale
            o_ref[...] = x.astype(o_ref.dtype)

    slab = pl.BlockSpec((tr, W), lambda i: (i, 0))
    tab = pl.BlockSpec((tr, 128), lambda i: (i, 0))
    return pl.pallas_call(body, grid=(L // tr,), in_specs=[pl.BlockSpec((tr, 3 * nq * W), lambda i: (i, 0)), tab, tab, tab],
                          out_specs=[slab] * (3 * nq), out_shape=[jax.ShapeDtypeStruct((L, W), BF16)] * (3 * nq),
                          compiler_params=_params(dimension_semantics=("parallel",)), name=name)(qkv, *tabs)


def rope_bwd(slabs, tabs, *, name):
    L, W = slabs[0].shape
    tr = _pick(L, (256, 128))
    nq = len(slabs) // 3
    scale = ATT_E ** -0.5

    def body(*refs):
        d_refs, (c_ref, s1_ref, s2_ref, o_ref) = refs[:3 * nq], refs[3 * nq:]
        c = jnp.concatenate([c_ref[...]] * 4, axis=1)
        s1 = jnp.concatenate([s1_ref[...]] * 4, axis=1)
        s2 = jnp.concatenate([s2_ref[...]] * 4, axis=1)
        for j, d_ref in enumerate(d_refs):
            dy = d_ref[...]
            if j < 2 * nq:
                dy = dy * c + pltpu.roll(dy * s1, W - 8, 1) + pltpu.roll(dy * s2, 8, 1)
            if j < nq:
                dy = dy * scale
            o_ref[:, j * W:(j + 1) * W] = dy.astype(o_ref.dtype)

    slab = pl.BlockSpec((tr, W), lambda i: (i, 0))
    tab = pl.BlockSpec((tr, 128), lambda i: (i, 0))
    return pl.pallas_call(body, grid=(L // tr,), in_specs=[slab] * (3 * nq) + [tab, tab, tab],
                          out_specs=pl.BlockSpec((tr, 3 * nq * W), lambda i: (i, 0)),
                          out_shape=jax.ShapeDtypeStruct((L, 3 * nq * W), BF16),
                          compiler_params=_params(dimension_semantics=("parallel",)), name=name)(*slabs, *tabs)


def _att_masks(has_prev):
    qi = lax.broadcasted_iota(jnp.int32, (ATT_BLOCK, ATT_BLOCK), 0)
    kj = lax.broadcasted_iota(jnp.int32, (ATT_BLOCK, ATT_BLOCK), 1)
    return qi >= kj, (kj >= qi) & has_prev


def attn_fwd(qp, kp, vp, d, *, name):
    L, W = qp.shape
    B, E = ATT_BLOCK, ATT_E
    nblk = L // B
    nb = nblk // d

    def body(q_ref, kc_ref, kp_ref, vc_ref, vp_ref, o_ref, l_ref):
        has_prev = (pl.program_id(0) % nb) > 0
        mc, mp = _att_masks(has_prev)
        for h in range(ATT_HPG):
            sl = slice(h * E, (h + 1) * E)
            q = q_ref[:, sl]
            sc = jnp.where(mc, _dot_nt(q, kc_ref[:, sl]), NEG_BIG)
            sp = jnp.where(mp, _dot_nt(q, kp_ref[:, sl]), NEG_BIG)
            m = jnp.maximum(jnp.max(sc, axis=-1, keepdims=True), jnp.max(sp, axis=-1, keepdims=True))
            pc = jnp.exp(sc - m)
            pp = jnp.exp(sp - m)
            den = jnp.sum(pc, axis=-1, keepdims=True) + jnp.sum(pp, axis=-1, keepdims=True)
            o = (_dot(pc, vc_ref[:, sl]) + _dot(pp, vp_ref[:, sl])) / den
            o_ref[:, sl] = o
            l_ref[:, sl] = jnp.broadcast_to(m + jnp.log(den), (B, E))

    cur = pl.BlockSpec((B, W), lambda j: (j, 0))
    prev = pl.BlockSpec((B, W), lambda j: (jnp.maximum(j - 1, 0), 0))
    return pl.pallas_call(body, grid=(nblk,), in_specs=[cur, cur, prev, cur, prev], out_specs=[cur, cur],
                          out_shape=[jax.ShapeDtypeStruct((L, W), F32)] * 2,
                          compiler_params=_params(dimension_semantics=("parallel",)), name=name)(
        qp, kp, kp, vp, vp)


def attn_bwd(qp, kp, vp, lse, do, dl, d, *, name):
    L, W = qp.shape
    B, E = ATT_BLOCK, ATT_E
    nblk = L // B
    nb = nblk // d

    def body(q_ref, kc_ref, kp_ref, vc_ref, vp_ref, l_ref, do_ref, dl_ref, dq_ref, dk_ref, dv_ref, tkc, tkp, tvc, tvp):
        j = pl.program_id(0)

        @pl.when(j == 0)
        def _():
            dk_ref[...] = jnp.zeros_like(dk_ref)
            dv_ref[...] = jnp.zeros_like(dv_ref)

        has_prev = (j % nb) > 0
        mc, mp = _att_masks(has_prev)
        for h in range(ATT_HPG):
            sl = slice(h * E, (h + 1) * E)
            q = q_ref[:, sl]
            kc, kpv, vc, vpv = kc_ref[:, sl], kp_ref[:, sl], vc_ref[:, sl], vp_ref[:, sl]
            lse_h = l_ref[:, h * E:h * E + 1]
            dl_h = dl_ref[:, h * E:h * E + 1]
            doh = do_ref[:, sl]
            pc = jnp.where(mc, jnp.exp(_dot_nt(q, kc) - lse_h), 0.0)
            pp = jnp.where(mp, jnp.exp(_dot_nt(q, kpv) - lse_h), 0.0)
            dsc = pc * (_dot_nt(doh, vc) - dl_h)
            dsp = pp * (_dot_nt(doh, vpv) - dl_h)
            dq_ref[:, sl] = _dot(dsc, kc) + _dot(dsp, kpv)
            tkc[:, sl] = _dot_tn(dsc, q)
            tkp[:, sl] = _dot_tn(dsp, q)
            tvc[:, sl] = _dot_tn(pc, doh)
            tvp[:, sl] = _dot_tn(pp, doh)
        cur = pl.multiple_of(j * B, B)
        prv = pl.multiple_of(jnp.maximum(j - 1, 0) * B, B)
        dk_ref[pl.ds(cur, B), :] += tkc[...]
        dv_ref[pl.ds(cur, B), :] += tvc[...]
        dk_ref[pl.ds(prv, B), :] += tkp[...]
        dv_ref[pl.ds(prv, B), :] += tvp[...]

    cur = pl.BlockSpec((B, W), lambda j: (j, 0))
    prev = pl.BlockSpec((B, W), lambda j: (jnp.maximum(j - 1, 0), 0))
    full = pl.BlockSpec((L, W), lambda j: (0, 0))
    return pl.pallas_call(body, grid=(nblk,), in_specs=[cur, cur, prev, cur, prev, cur, cur, cur],
                          out_specs=[cur, full, full], out_shape=[jax.ShapeDtypeStruct((L, W), F32)] * 3,
                          scratch_shapes=[pltpu.VMEM((B, W), F32)] * 4,
                          compiler_params=_params(dimension_semantics=("arbitrary",)), name=name)(
        qp, kp, kp, vp, vp, lse, do, dl)


def _merge_alpha(l_refs):
    ls = [r[...] for r in l_refs]
    m = jnp.maximum(jnp.maximum(ls[0], ls[1]), ls[2])
    es = [jnp.exp(l - m) for l in ls]
    den = es[0] + es[1] + es[2]
    return [e / den for e in es]


def merge_fwd(os_, ls_, *, name):
    L, W = os_[0].shape
    tr = _pick(L, (256, 128))

    def body(o0, o1, o2, l0, l1, l2, out_ref):
        al = _merge_alpha((l0, l1, l2))
        out_ref[...] = (al[0] * o0[...] + al[1] * o1[...] + al[2] * o2[...]).astype(out_ref.dtype)

    row = pl.BlockSpec((tr, W), lambda i: (i, 0))
    return pl.pallas_call(body, grid=(L // tr,), in_specs=[row] * 6, out_specs=row,
                          out_shape=jax.ShapeDtypeStruct((L, W), BF16), name=name)(*os_, *ls_)


def merge_bwd(os_, ls_, do, *, name):
    L, W = do.shape
    tr = _pick(L, (256, 128))

    def body(o0, o1, o2, l0, l1, l2, do_ref, d0, d1, d2, e0, e1, e2):
        al = _merge_alpha((l0, l1, l2))
        dov = do_ref[...]
        r = lax.broadcasted_iota(jnp.int32, (W, W), 0) // ATT_E
        c = lax.broadcasted_iota(jnp.int32, (W, W), 1) // ATT_E
        ones_blk = (r == c).astype(F32)
        t = jnp.zeros_like(dov)
        for a, o in zip(al, (o0, o1, o2)):
            t = t + a * _dot_f32(dov * o[...], ones_blk)
        for a, d_ref, e_ref in zip(al, (d0, d1, d2), (e0, e1, e2)):
            d_ref[...] = a * dov
            e_ref[...] = a * t

    row = pl.BlockSpec((tr, W), lambda i: (i, 0))
    return pl.pallas_call(body, grid=(L // tr,), in_specs=[row] * 7, out_specs=[row] * 6,
                          out_shape=[jax.ShapeDtypeStruct((L, W), F32)] * 6, name=name)(*os_, *ls_, do)


def _me_and_peers():
    x, y, c = lax.axis_index("x"), lax.axis_index("y"), lax.axis_index("c")
    peers = []
    for k in range(1, N_DEV):
        px = 1 - x if k & 4 else x
        py = 1 - y if k & 2 else y
        pc = 1 - c if k & 1 else c
        peers.append((px, py, pc))
    return (x, y, c), peers


def _index(dev):
    return 4 * dev[0] + 2 * dev[1] + dev[2]


def _hbm(a):
    return pltpu.with_memory_space_constraint(a, pltpu.HBM)


HBM_SPEC = pl.BlockSpec(memory_space=pltpu.HBM)
SEM_SPEC = pl.BlockSpec(memory_space=pltpu.SEMAPHORE)
DATAFLOW = pltpu.SideEffectType.DATAFLOW_SIDE_EFFECTING


def _remote(src_ref, land_ref, slotted, me, peer, src_is_mine, send_sem, recv_sem, k):
    sender, receiver = (me, peer) if src_is_mine else (peer, me)
    src = src_ref.at[_index(receiver)] if slotted else src_ref
    return pltpu.make_async_remote_copy(src_ref=src, dst_ref=land_ref.at[_index(sender)], send_sem=send_sem.at[k],
                                        recv_sem=recv_sem.at[k], device_id=peer, device_id_type=MESH_ID)


def copies_start(arrays, slotted, *, name):
    n = len(arrays)
    lands = [lax.empty(a.shape if slotted else (N_DEV,) + a.shape, a.dtype) for a in arrays]

    def body(*refs):
        x_refs, land_refs = refs[:n], refs[n:2 * n]
        send, recv = refs[2 * n:3 * n], refs[3 * n:4 * n]
        token = refs[-1]
        me, peers = _me_and_peers()
        for w in range(n):
            for k, peer in enumerate(peers):
                _remote(x_refs[w], land_refs[w], slotted, me, peer, True, send[w], recv[w], k).start()
        token[...] = jnp.zeros_like(token)

    sem = pltpu.SemaphoreType.DMA((N_DEV - 1,))
    out_shape = ([sem] * (2 * n) + [pltpu.HBM(a.shape, a.dtype) for a in arrays]
                 + [pltpu.HBM(l.shape, l.dtype) for l in lands] + [jax.ShapeDtypeStruct((8, 128), F32)])
    outs = pl.pallas_call(
        body, name=name, out_shape=out_shape, in_specs=[HBM_SPEC] * (2 * n),
        out_specs=[SEM_SPEC] * (2 * n) + [HBM_SPEC] * (2 * n) + [pl.BlockSpec(memory_space=pltpu.VMEM)],
        input_output_aliases={i: 2 * n + i for i in range(2 * n)},
        compiler_params=pltpu.CompilerParams(has_side_effects=DATAFLOW),
    )(*[_hbm(a) for a in arrays], *[_hbm(l) for l in lands])
    handles = [(outs[w], outs[n + w], outs[2 * n + w], outs[3 * n + w]) for w in range(n)]
    return handles, outs[-1]


def copies_wait(handle, slotted, after, *, name):
    send_sem, recv_sem, x_thru, land_thru = handle

    def body(x_ref, land_ref, send_ref, recv_ref, after_ref, x_out, land_out, *local_sem):
        me, peers = _me_and_peers()
        if not slotted:
            mine = pltpu.make_async_copy(x_ref, land_ref.at[_index(me)], local_sem[0])
            mine.start()
        for k, peer in enumerate(peers):
            _remote(x_ref, land_ref, slotted, me, peer, True, send_ref, recv_ref, k).wait_send()
        for k, peer in enumerate(peers):
            _remote(x_ref, land_ref, slotted, me, peer, False, send_ref, recv_ref, k).wait_recv()
        if not slotted:
            mine.wait()

    return pl.pallas_call(
        body, name=name, out_shape=(pltpu.HBM(x_thru.shape, x_thru.dtype), pltpu.HBM(land_thru.shape, land_thru.dtype)),
        in_specs=(HBM_SPEC, HBM_SPEC, SEM_SPEC, SEM_SPEC, pl.BlockSpec(memory_space=pl.ANY)),
        out_specs=(HBM_SPEC, HBM_SPEC), input_output_aliases={0: 0, 1: 1},
        scratch_shapes=[] if slotted else [pltpu.SemaphoreType.DMA],
        compiler_params=pltpu.CompilerParams(has_side_effects=DATAFLOW),
    )(x_thru, land_thru, send_sem, recv_sem, after)


def cast_bf16(x, *, ncols=None, name):
    R = x.shape[0]
    C = ncols or x.shape[1]
    tr = _pick(R, (512, 352, 256, 128, 64))

    def body(x_ref, o_ref):
        o_ref[...] = x_ref[...].astype(BF16)

    row = pl.BlockSpec((tr, C), lambda i: (i, 0))
    return pl.pallas_call(body, grid=(R // tr,), in_specs=[row], out_specs=row,
                          out_shape=jax.ShapeDtypeStruct((R, C), BF16), name=name)(x)


def cast_bf16_layer(x3, layer, *, name):
    _, R, C = x3.shape
    tr = _pick(R, (512, 352, 256, 128, 64))

    def body(x_ref, o_ref):
        o_ref[...] = x_ref[...].astype(BF16)

    return pl.pallas_call(body, grid=(R // tr,), in_specs=[pl.BlockSpec((None, tr, C), lambda i: (layer, i, 0))],
                          out_specs=pl.BlockSpec((tr, C), lambda i: (i, 0)),
                          out_shape=jax.ShapeDtypeStruct((R, C), BF16), name=name)(x3)


def _blockdiag_call(b, build, G, r, c, name):
    def body_build(b_ref, o_ref):
        o_ref[...] = jnp.zeros_like(o_ref)
        for g in range(G):
            o_ref[g * r:(g + 1) * r, g * c:(g + 1) * c] = b_ref[g]

    def body_extract(d_ref, o_ref):
        for g in range(G):
            o_ref[g] = d_ref[g * r:(g + 1) * r, g * c:(g + 1) * c]

    out = jax.ShapeDtypeStruct((G * r, G * c) if build else (G, r, c), F32)
    return pl.pallas_call(body_build if build else body_extract, out_shape=out, name=name)(b)


def make_blockdiag(G, r, c, name):
    @jax.custom_vjp
    def blockdiag(b):
        return _blockdiag_call(b, True, G, r, c, name + "_build")

    def fwd(b):
        return blockdiag(b), None

    def bwd(_, g):
        return (_blockdiag_call(g, False, G, r, c, name + "_extract"),)

    blockdiag.defvjp(fwd, bwd)
    return blockdiag


def _my_index():
    return 4 * lax.axis_index("x") + 2 * lax.axis_index("y") + lax.axis_index("c")


def cols_from_shards(g, *, name):
    _, K, n = g.shape
    tk = _pick(K, (256, 128))

    def body(g_ref, o_ref):
        for i in range(N_DEV):
            o_ref[:, i * n:(i + 1) * n] = g_ref[i]

    return pl.pallas_call(body, grid=(K // tk,), in_specs=[pl.BlockSpec((N_DEV, tk, n), lambda i: (0, i, 0))],
                          out_specs=pl.BlockSpec((tk, N_DEV * n), lambda i: (i, 0)),
                          out_shape=jax.ShapeDtypeStruct((K, N_DEV * n), g.dtype), name=name)(g)


def shards_from_cols(w, *, name):
    K, N = w.shape
    n = N // N_DEV
    tk = _pick(K, (256, 128))

    def body(w_ref, o_ref):
        for i in range(N_DEV):
            o_ref[i] = w_ref[:, i * n:(i + 1) * n].astype(o_ref.dtype)

    return pl.pallas_call(body, grid=(K // tk,), in_specs=[pl.BlockSpec((tk, N), lambda i: (i, 0))],
                          out_specs=pl.BlockSpec((N_DEV, tk, n), lambda i: (0, i, 0)),
                          out_shape=jax.ShapeDtypeStruct((N_DEV, K, n), BF16), name=name)(w)


def _adamw(w, g, m, v):
    m = ADAM_B1 * m + (1.0 - ADAM_B1) * g
    v = ADAM_B2 * v + (1.0 - ADAM_B2) * (g * g)
    m_hat = m / (1.0 - ADAM_B1 ** ADAM_STEP)
    v_hat = v / (1.0 - ADAM_B2 ** ADAM_STEP)
    delta = -ADAM_LR * (m_hat / (jnp.sqrt(v_hat) + ADAM_EPS) + ADAM_WD * w)
    return delta, m, v


def reduce_adamw(recv, own, own_slotted, me, w, m, v, *, layer=0, n_layers=1, into=None, name):
    _, R, C = recv.shape
    tr = _pick(R, (352, 320, 288, 256, 128, 64, 32, 16, 8))
    off = layer * (R // tr)

    def body(me_ref, r_ref, own_ref, w_ref, m_ref, v_ref, *rest):
        g_ref, d_ref, nm_ref, nv_ref = rest[-4:]
        mine = me_ref[0]
        g = None
        for i in range(N_DEV):
            part = jnp.where(mine == i, own_ref[...], r_ref[i]).astype(F32)
            g = part if g is None else g + part
        delta, nm, nv = _adamw(w_ref[...], g, m_ref[...], v_ref[...])
        g_ref[...] = g
        d_ref[...] = delta
        nm_ref[...] = nm
        nv_ref[...] = nv

    row = pl.BlockSpec((tr, C), lambda i, me_ref: (i + off, 0))
    own_spec = (pl.BlockSpec((None, tr, C), lambda i, me_ref: (me_ref[0], i, 0)) if own_slotted
                else pl.BlockSpec((tr, C), lambda i, me_ref: (i, 0)))
    rest = [] if into is None else list(into)
    grid_spec = pltpu.PrefetchScalarGridSpec(
        num_scalar_prefetch=1, grid=(R // tr,),
        in_specs=[pl.BlockSpec((N_DEV, tr, C), lambda i, me_ref: (0, i, 0)), own_spec, row, row, row]
        + [pl.BlockSpec(memory_space=pl.ANY)] * len(rest),
        out_specs=[row] * 4)
    return pl.pallas_call(body, grid_spec=grid_spec, out_shape=[jax.ShapeDtypeStruct((n_layers * R, C), F32)] * 4,
                          input_output_aliases={6 + k: k for k in range(len(rest))},
                          compiler_params=_params(dimension_semantics=("parallel",)), name=name)(
        me.reshape(1).astype(jnp.int32), recv, own, w, m, v, *rest)


def _s5_prepare(A_re, A_im, log_dt, B_re, B_im, C_re, C_im):
    G, P, Cg = S5_GROUPS, S5_STATE, S5_GROUP
    dt = jnp.exp(log_dt)[:, None]
    mag = jnp.exp(A_re * dt)
    ab_re = mag * jnp.cos(A_im * dt)
    ab_im = mag * jnp.sin(A_im * dt)
    den = A_re * A_re + A_im * A_im
    nr, ni = ab_re - 1.0, ab_im
    c_re = (nr * A_re + ni * A_im) / den
    c_im = (ni * A_re - nr * A_im) / den
    Bb_re = c_re[..., None] * B_re - c_im[..., None] * B_im
    Bb_im = c_re[..., None] * B_im + c_im[..., None] * B_re
    def dense_in(b, name):
        return make_blockdiag(G, Cg, P, name)(b.transpose(0, 2, 1))

    def dense_out(c, name):
        return make_blockdiag(G, P, Cg, name)(c.transpose(0, 2, 1))

    return (ab_re.reshape(1, G * P), ab_im.reshape(1, G * P), dense_in(Bb_re, "s5_wb_re"), dense_in(Bb_im, "s5_wb_im"),
            dense_out(C_re, "s5_wc_re"), dense_out(-C_im, "s5_wc_im"))


def _lower_bound(gamma):
    return jnp.cumsum(jax.nn.softmax(gamma, axis=0), axis=0)[0:1]


def _ffn_fwd(h, g_norm, get_w_in, conv_w, conv_b, get_w_out, tag):
    hn = rms_fwd(h, g_norm, name=tag + "_rms")
    w_in = get_w_in(hn)
    hu = mm(hn, w_in, tb=True, name=tag + "_in")
    act = convgate_fwd(hu, conv_w, conv_b, name=tag + "_gate")
    w_out = get_w_out(act)
    h_out = mm(act, w_out, res=h, name=tag + "_out")
    return h_out, (hn, hu, act), w_in, w_out


def _ffn_bwd(h, g_norm, w_in, conv_w, conv_b, w_out, saved, dh, tag, send_dw_in, send_dw_out):
    hn, hu, act = saved
    sent = send_dw_out(mm(act, dh, ta=True, out_dtype=BF16, name=tag + "_dwout"))
    dact = mm(dh, w_out, tb=True, dep=sent, name=tag + "_dact")
    dhu, dconv_w, dconv_b = convgate_bwd(hu, conv_w, conv_b, dact, name=tag + "_dgate")
    sent = send_dw_in(mm(dhu, hn, ta=True, out_dtype=BF16, name=tag + "_dwin"))
    dhn = mm(dhu, w_in, dep=sent, name=tag + "_dhn")
    dh_in, dg = rms_bwd(h, g_norm, dhn, dh, name=tag + "_drms")
    return dh_in, dg, dconv_w, dconv_b


def kernel(x, positions, norm_mix, norm_ffn, norm_final, mix_w_in, mix_w_out, s5_A_re, s5_A_im, s5_log_dt, s5_B_re, s5_B_im, s5_C_re, s5_C_im, s5_D, s5_glu_w, s5_glu_b, hgrn_gamma, hgrn_norm, att_w_qkv, att_w_o, ffn_w_in, ffn_conv_w, ffn_conv_b, ffn_w_out, loss_target, m_norm_mix, m_norm_ffn, m_norm_final, m_mix_w_in, m_mix_w_out, m_s5_A_re, m_s5_A_im, m_s5_log_dt, m_s5_B_re, m_s5_B_im, m_s5_C_re, m_s5_C_im, m_s5_D, m_s5_glu_w, m_s5_glu_b, m_hgrn_gamma, m_hgrn_norm, m_att_w_qkv, m_att_w_o, m_ffn_w_in, m_ffn_conv_w, m_ffn_conv_b, m_ffn_w_out, v_norm_mix, v_norm_ffn, v_norm_final, v_mix_w_in, v_mix_w_out, v_s5_A_re, v_s5_A_im, v_s5_log_dt, v_s5_B_re, v_s5_B_im, v_s5_C_re, v_s5_C_im, v_s5_D, v_s5_glu_w, v_s5_glu_b, v_hgrn_gamma, v_hgrn_norm, v_att_w_qkv, v_att_w_o, v_ffn_w_in, v_ffn_conv_w, v_ffn_conv_b, v_ffn_w_out):
    W = dict(norm_mix=norm_mix, norm_ffn=norm_ffn, norm_final=norm_final, mix_w_in=mix_w_in, mix_w_out=mix_w_out,
             s5_A_re=s5_A_re, s5_A_im=s5_A_im, s5_log_dt=s5_log_dt, s5_B_re=s5_B_re, s5_B_im=s5_B_im,
             s5_C_re=s5_C_re, s5_C_im=s5_C_im, s5_D=s5_D, s5_glu_w=s5_glu_w, s5_glu_b=s5_glu_b,
             hgrn_gamma=hgrn_gamma, hgrn_norm=hgrn_norm, att_w_qkv=att_w_qkv, att_w_o=att_w_o, ffn_w_in=ffn_w_in,
             ffn_conv_w=ffn_conv_w, ffn_conv_b=ffn_conv_b, ffn_w_out=ffn_w_out)
    M = dict(norm_mix=m_norm_mix, norm_ffn=m_norm_ffn, norm_final=m_norm_final, mix_w_in=m_mix_w_in,
             mix_w_out=m_mix_w_out, s5_A_re=m_s5_A_re, s5_A_im=m_s5_A_im, s5_log_dt=m_s5_log_dt, s5_B_re=m_s5_B_re,
             s5_B_im=m_s5_B_im, s5_C_re=m_s5_C_re, s5_C_im=m_s5_C_im, s5_D=m_s5_D, s5_glu_w=m_s5_glu_w,
             s5_glu_b=m_s5_glu_b, hgrn_gamma=m_hgrn_gamma, hgrn_norm=m_hgrn_norm, att_w_qkv=m_att_w_qkv,
             att_w_o=m_att_w_o, ffn_w_in=m_ffn_w_in, ffn_conv_w=m_ffn_conv_w, ffn_conv_b=m_ffn_conv_b,
             ffn_w_out=m_ffn_w_out)
    V = dict(norm_mix=v_norm_mix, norm_ffn=v_norm_ffn, norm_final=v_norm_final, mix_w_in=v_mix_w_in,
             mix_w_out=v_mix_w_out, s5_A_re=v_s5_A_re, s5_A_im=v_s5_A_im, s5_log_dt=v_s5_log_dt, s5_B_re=v_s5_B_re,
             s5_B_im=v_s5_B_im, s5_C_re=v_s5_C_re, s5_C_im=v_s5_C_im, s5_D=v_s5_D, s5_glu_w=v_s5_glu_w,
             s5_glu_b=v_s5_glu_b, hgrn_gamma=v_hgrn_gamma, hgrn_norm=v_hgrn_norm, att_w_qkv=v_att_w_qkv,
             att_w_o=v_att_w_o, ffn_w_in=v_ffn_w_in, ffn_conv_w=v_ffn_conv_w, ffn_conv_b=v_ffn_conv_b,
             ffn_w_out=v_ffn_w_out)
    return _step(x[0], positions[0], loss_target[0], W, M, V)


TRANSPOSED = ("mix_w_in", "att_w_qkv", "ffn_w_in")
SMALL = ("norm_mix", "norm_ffn", "norm_final", "s5_A_re", "s5_A_im", "s5_log_dt", "s5_B_re", "s5_B_im", "s5_C_re",
         "s5_C_im", "s5_D", "s5_glu_b", "hgrn_gamma", "hgrn_norm", "ffn_conv_b")
ORDER = ("norm_mix", "norm_ffn", "norm_final", "mix_w_in", "mix_w_out", "s5_A_re", "s5_A_im", "s5_log_dt", "s5_B_re",
         "s5_B_im", "s5_C_re", "s5_C_im", "s5_D", "s5_glu_w", "s5_glu_b", "hgrn_gamma", "hgrn_norm", "att_w_qkv",
         "att_w_o", "ffn_w_in", "ffn_conv_w", "ffn_conv_b", "ffn_w_out")
PACK_COLS = 1024


def _step(x, positions, target, W, M, V):
    L, D = x.shape
    me = 4 * lax.axis_index("x") + 2 * lax.axis_index("y") + lax.axis_index("c")
    n_cw = W["ffn_conv_w"].shape[-1]
    T = {n: tuple(jnp.swapaxes(d[n], -1, -2) for d in (W, M, V)) for n in TRANSPOSED}
    shards = {
        "mix_w_in": cast_bf16(T["mix_w_in"][0][0], name="mix_w_in_cast"),
        "conv_w": W["ffn_conv_w"].reshape(6, n_cw),
        "s5_glu_w": cast_bf16(W["s5_glu_w"][0], name="s5_glu_w_cast"),
        "mix_w_out": cast_bf16(W["mix_w_out"][0], name="mix_w_out_cast"),
        "ffn_w_in0": cast_bf16_layer(T["ffn_w_in"][0], 0, name="ffn_w_in0_cast"),
        "ffn_w_out0": cast_bf16_layer(W["ffn_w_out"], 0, name="ffn_w_out0_cast"),
        "att_w_qkv": cast_bf16(T["att_w_qkv"][0][0], name="att_w_qkv_cast"),
        "att_w_o": cast_bf16(W["att_w_o"][0], name="att_w_o_cast"),
        "ffn_w_in1": cast_bf16_layer(T["ffn_w_in"][0], 1, name="ffn_w_in1_cast"),
        "ffn_w_out1": cast_bf16_layer(W["ffn_w_out"], 1, name="ffn_w_out1_cast"),
    }
    gather_handles, token = copies_start(list(shards.values()), False, name="gather_start")
    gather_handle = dict(zip(shards, gather_handles))

    def gathered(key, after, cols):
        _, land = copies_wait(gather_handle[key], False, after, name=key + "_gwait")
        return cols_from_shards(land, name=key + "_asm") if cols else land.reshape(-1, land.shape[-1])

    conv_b = W["ffn_conv_b"].reshape(2, 1, -1)

    s5_params = (W["s5_A_re"][0], W["s5_A_im"][0], W["s5_log_dt"][0], W["s5_B_re"][0], W["s5_B_im"][0],
                 W["s5_C_re"][0], W["s5_C_im"][0])
    (a_re, a_im, wb_re, wb_im, wc_re, wc_im), s5_prep_vjp = jax.vjp(_s5_prepare, *s5_params)
    dvec = W["s5_D"].reshape(1, S5_WIDTH)
    glu_b = W["s5_glu_b"].reshape(1, S5_WIDTH)
    lb, lb_vjp = jax.vjp(_lower_bound, W["hgrn_gamma"])
    hg_norm = W["hgrn_norm"].reshape(1, -1)
    tabs = rope_tables(positions)

    hn0 = rms_fwd(x, W["norm_mix"][0], dep=token, name="l0_rms")
    w_mix_in = gathered("mix_w_in", hn0, False)
    proj = mm(hn0, w_mix_in, tb=True, name="l0_proj")
    u_bf = cast_bf16(proj, ncols=S5_WIDTH, name="l0_u_cast")
    bu_re = mm(u_bf, wb_re, name="s5_bu_re")
    bu_im = mm(u_bf, wb_im, name="s5_bu_im")
    xs_re, xs_im = s5_scan_fwd(a_re, a_im, bu_re, bu_im, name="s5_scan")
    y0 = mm(xs_im, wc_im, res=mm(xs_re, wc_re, name="s5_y_re"), name="s5_y_im")
    w_glu = gathered("s5_glu_w", y0, False)
    oa = s5_out_fwd(y0, proj, dvec, w_glu, glu_b, name="s5_out")
    ob, hg_states = hgrn_fwd(proj, lb, hg_norm, name="hgrn_fwd")
    cat = jnp.concatenate([oa, ob], axis=1)
    w_mix_out = gathered("mix_w_out", cat, False)
    h1 = mm(cat, w_mix_out, res=x, name="l0_mix_out")
    _, cw_all = copies_wait(gather_handle["conv_w"], False, h1, name="conv_w_gwait")
    conv_w = cw_all.transpose(1, 0, 2).reshape(2, 3, N_DEV * n_cw)
    w_ffn_in, w_ffn_out = [None, None], [None, None]
    h2, ffn0_saved, w_ffn_in[0], w_ffn_out[0] = _ffn_fwd(
        h1, W["norm_ffn"][0], lambda a: gathered("ffn_w_in0", a, False), conv_w[0], conv_b[0],
        lambda a: gathered("ffn_w_out0", a, False), "ffn0")

    hn2 = rms_fwd(h2, W["norm_mix"][1], name="l1_rms")
    w_qkv = gathered("att_w_qkv", hn2, False)
    qkv = mm(hn2, w_qkv, tb=True, name="l1_qkv")
    qkv_r = rope_fwd(qkv, tabs, name="rope_fwd")
    att_in, att_o, att_l = [], [], []
    for g, d in enumerate(ATT_DILATIONS):
        qp = _to_branch_order(qkv_r[g], d)
        kp = _to_branch_order(qkv_r[3 + g], d)
        vp = _to_branch_order(qkv_r[6 + g], d)
        o_p, l_p = attn_fwd(qp, kp, vp, d, name=f"attn_fwd{g}")
        att_in.append((qp, kp, vp, l_p))
        att_o.append(_to_token_order(o_p, d))
        att_l.append(_to_token_order(l_p, d))
    o_att = merge_fwd(att_o, att_l, name="merge_fwd")
    w_o = gathered("att_w_o", o_att, True)
    h3 = mm(o_att, w_o, res=h2, name="l1_mix_out")
    h4, ffn1_saved, w_ffn_in[1], w_ffn_out[1] = _ffn_fwd(
        h3, W["norm_ffn"][1], lambda a: gathered("ffn_w_in1", a, False), conv_w[1], conv_b[1],
        lambda a: gathered("ffn_w_out1", a, False), "ffn1")

    exchanges = {}

    def send_grad(key, g, cols):
        if cols:
            parts = shards_from_cols(g, name=key + "_split")
        else:
            parts = g.reshape(N_DEV, g.shape[0] // N_DEV, g.shape[1])
        (handle,), sent = copies_start([parts], True, name=key + "_xstart")
        exchanges[key] = handle
        return sent

    loss, dh4, dg_final = final_loss(h4, W["norm_final"], target, name="final_loss")
    dh3, dg_ffn1, dcw1, dcb1 = _ffn_bwd(h3, W["norm_ffn"][1], w_ffn_in[1], conv_w[1], conv_b[1], w_ffn_out[1],
                                        ffn1_saved, dh4, "ffn1", lambda g: send_grad("ffn_w_in1", g, False),
                                        lambda g: send_grad("ffn_w_out1", g, False))
    sent = send_grad("att_w_o", mm(o_att, dh3, ta=True, name="l1_dwo"), True)
    d_oatt = mm(dh3, w_o, tb=True, dep=sent, name="l1_dmix")
    mb = merge_bwd(att_o, att_l, d_oatt, name="merge_bwd")
    dq_t, dk_t, dv_t = [], [], []
    for g, d in enumerate(ATT_DILATIONS):
        qp, kp, vp, l_p = att_in[g]
        dq_p, dk_p, dv_p = attn_bwd(qp, kp, vp, l_p, _to_branch_order(mb[g], d), _to_branch_order(mb[3 + g], d), d,
                                    name=f"attn_bwd{g}")
        dq_t.append(_to_token_order(dq_p, d))
        dk_t.append(_to_token_order(dk_p, d))
        dv_t.append(_to_token_order(dv_p, d))
    d_qkv = rope_bwd(dq_t + dk_t + dv_t, tabs, name="rope_bwd")
    sent = send_grad("att_w_qkv", mm(d_qkv, hn2, ta=True, out_dtype=BF16, name="l1_dwqkv"), False)
    d_hn2 = mm(d_qkv, w_qkv, dep=sent, name="l1_dhn")
    dh2, dg_mix1 = rms_bwd(h2, W["norm_mix"][1], d_hn2, dh3, name="l1_drms")

    dh1, dg_ffn0, dcw0, dcb0 = _ffn_bwd(h1, W["norm_ffn"][0], w_ffn_in[0], conv_w[0], conv_b[0], w_ffn_out[0],
                                        ffn0_saved, dh2, "ffn0", lambda g: send_grad("ffn_w_in0", g, False),
                                        lambda g: send_grad("ffn_w_out0", g, False))
    sent = send_grad("mix_w_out", mm(cat, dh1, ta=True, out_dtype=BF16, name="l0_dwout"), False)
    dcat = mm(dh1, w_mix_out, tb=True, dep=sent, name="l0_dcat")
    d_hg, dlb, dhg_norm = hgrn_bwd(proj, lb, hg_norm, hg_states, dcat, name="hgrn_bwd")
    dy, du_d, z_bf, dzg, dglu_b, dD = s5_out_bwd(y0, proj, dvec, w_glu, glu_b, dcat, name="s5_dout")
    sent = send_grad("s5_glu_w", mm(z_bf, dzg, ta=True, out_dtype=BF16, name="s5_dglu"), False)
    dxs_re = mm(dy, wc_re, tb=True, dep=sent, name="s5_dxs_re")
    dxs_im = mm(dy, wc_im, tb=True, name="s5_dxs_im")
    dwc_re = mm(xs_re, dy, ta=True, name="s5_dwc_re")
    dwc_im = mm(xs_im, dy, ta=True, name="s5_dwc_im")
    dbu_re, dbu_im, da_re, da_im = s5_scan_bwd(a_re, a_im, xs_re, xs_im, dxs_re, dxs_im, name="s5_dscan")
    du = mm(dbu_im, wb_im, tb=True, res=mm(dbu_re, wb_re, tb=True, res=du_d, name="s5_du_re"), out_dtype=BF16,
            name="s5_du_im")
    dwb_re = mm(u_bf, dbu_re, ta=True, name="s5_dwb_re")
    dwb_im = mm(u_bf, dbu_im, ta=True, name="s5_dwb_im")
    s5_small = s5_prep_vjp((da_re, da_im, dwb_re, dwb_im, dwc_re, dwc_im))
    d_proj = jnp.concatenate([du, d_hg], axis=1)
    sent = send_grad("mix_w_in", mm(d_proj, hn0, ta=True, out_dtype=BF16, name="l0_dwin"), False)
    d_hn0 = mm(d_proj, w_mix_in, dep=sent, name="l0_dhn")
    grad_x, dg_mix0 = rms_bwd(x, W["norm_mix"][0], d_hn0, dh1, name="l0_drms")
    (d_gamma,) = lb_vjp(dlb)
    out = {}

    dA_re, dA_im, dlog_dt, dB_re, dB_im, dC_re, dC_im = s5_small
    small_g = dict(norm_mix=jnp.concatenate([dg_mix0, dg_mix1], axis=0), norm_ffn=jnp.concatenate([dg_ffn0, dg_ffn1], axis=0),
                   norm_final=dg_final, s5_A_re=dA_re, s5_A_im=dA_im, s5_log_dt=dlog_dt, s5_B_re=dB_re, s5_B_im=dB_im,
                   s5_C_re=dC_re, s5_C_im=dC_im, s5_D=dD, s5_glu_b=dglu_b, hgrn_gamma=d_gamma, hgrn_norm=dhg_norm,
                   ffn_conv_b=jnp.concatenate([dcb0, dcb1], axis=0))
    conv_w_g = jnp.stack([dcw0, dcw1], axis=0)
    sizes = [math.prod(W[n].shape) for n in SMALL]
    n_conv = conv_w_g.size
    total = sum(sizes) + n_conv + 1
    rows = -(-total // PACK_COLS)
    rows = -(-rows // 8) * 8
    pad = rows * PACK_COLS - total

    def pack(vals, conv_part, last):
        flat = [v.reshape(-1).astype(F32) for v in vals] + [conv_part.reshape(-1), last.reshape(-1),
                                                            jnp.zeros((pad,), F32)]
        return jnp.concatenate(flat).reshape(rows, PACK_COLS)

    def conv_full(shard):
        col_owner = lax.broadcasted_iota(jnp.int32, (2, 3, N_DEV * n_cw), 2) // n_cw
        return jnp.where(col_owner == me, jnp.tile(shard, (1, 1, N_DEV)), 0.0)

    zero1 = jnp.zeros((1,), F32)
    g_pack = pack([small_g[n] for n in SMALL], conv_w_g, loss)
    w_pack = pack([W[n] for n in SMALL], conv_full(W["ffn_conv_w"]), zero1)
    m_pack = pack([M[n] for n in SMALL], conv_full(M["ffn_conv_w"]), zero1)
    v_pack = pack([V[n] for n in SMALL], conv_full(V["ffn_conv_w"]), zero1 + 1.0)
    (small_handle,), small_sent = copies_start([g_pack], False, name="small_xstart")

    def finish(name, n_layers):
        w3, m3, v3 = T[name] if name in TRANSPOSED else (W[name], M[name], V[name])
        res = None
        for layer in reversed(range(n_layers)):
            key = name if n_layers == 1 else f"{name}{layer}"
            own, recv = copies_wait(exchanges[key], True, small_sent, name=key + "_xwait")
            _, R, Cn = recv.shape
            res = reduce_adamw(recv, own, True, me, w3.reshape(n_layers * R, Cn), m3.reshape(n_layers * R, Cn),
                               v3.reshape(n_layers * R, Cn), layer=layer, n_layers=n_layers, into=res,
                               name=key + "_adamw")
        res = [r.reshape(w3.shape) for r in res]
        return tuple(jnp.swapaxes(r, -1, -2) for r in res) if name in TRANSPOSED else tuple(res)

    for name in ("ffn_w_out", "ffn_w_in"):
        out[name] = finish(name, 2)
    for name in ("att_w_o", "att_w_qkv", "mix_w_out", "s5_glu_w", "mix_w_in"):
        out[name] = finish(name, 1)

    small_own, small_recv = copies_wait(small_handle, False, out["s5_glu_w"][0], name="small_xwait")
    res = reduce_adamw(small_recv, small_own, False, me, w_pack, m_pack, v_pack, name="small_adamw")
    flat = [r.reshape(-1) for r in res]
    off = 0
    for n, sz in zip(SMALL, sizes):
        out[n] = tuple(f[off:off + sz].reshape(W[n].shape) for f in flat)
        off += sz
    conv_res = [f[off:off + n_conv].reshape(2, 3, N_DEV * n_cw) for f in flat]
    out["ffn_conv_w"] = tuple(lax.dynamic_slice(c, (0, 0, me * n_cw), (2, 3, n_cw)) for c in conv_res)
    off += n_conv
    loss_total = flat[0][off]

    result = [loss_total, grad_x[None]]
    for k in range(4):
        result += [out[n][k] for n in ORDER]
    return tuple(result)
```

```python
import functools
import math

import jax
import jax.numpy as jnp
from jax import lax
from jax.experimental import pallas as pl
from jax.experimental.pallas import tpu as pltpu

F32 = jnp.float32
BF16 = jnp.bfloat16
MESH_ID = pl.DeviceIdType.MESH
N_DEV = 8
VMEM_LIMIT_BYTES = 56 * 1024 * 1024

NORM_EPS = 1e-6
S5_WIDTH, S5_GROUP, S5_GROUPS, S5_STATE = 512, 16, 32, 64
HG_HEADS, HG_DIM, HG_CHUNK = 4, 128, 64
ATT_E, ATT_HPG, ATT_BLOCK = 64, 8, 128
ATT_DILATIONS = (1, 4, 16)
ROT_DIM, ROPE_THETA = 16, 500000.0
D_FF = 2816
ADAM_LR, ADAM_B1, ADAM_B2, ADAM_EPS, ADAM_WD, ADAM_STEP = 0.001, 0.9, 0.999, 1e-08, 0.01, 10
NEG_BIG = -1e30


def _params(**kw):
    return pltpu.CompilerParams(vmem_limit_bytes=VMEM_LIMIT_BYTES, **kw)


def _pick(n, cands):
    for c in cands:
        if n % c == 0:
            return c
    return n


def _dot(a, b):
    return jnp.dot(a.astype(BF16), b.astype(BF16), preferred_element_type=F32)


def _dot_nt(a, b):
    return lax.dot_general(a.astype(BF16), b.astype(BF16), (((1,), (1,)), ((), ())), preferred_element_type=F32)


def _dot_tn(a, b):
    return lax.dot_general(a.astype(BF16), b.astype(BF16), (((0,), (0,)), ((), ())), preferred_element_type=F32)


def _dot_f32(a, b):
    return jnp.dot(a, b, preferred_element_type=F32, precision=lax.Precision.HIGHEST)


def _dot_f32_nt(a, b):
    return lax.dot_general(a, b, (((1,), (1,)), ((), ())), preferred_element_type=F32, precision=lax.Precision.HIGHEST)


def _dot_f32_tn(a, b):
    return lax.dot_general(a, b, (((0,), (0,)), ((), ())), preferred_element_type=F32, precision=lax.Precision.HIGHEST)


def _sigmoid(x):
    return 1.0 / (1.0 + jnp.exp(-x))


V7X_HBM_BYTES_PER_S = 3.2e12
V7X_MXU_FLOPS_PER_S = 0.7e15
GRID_STEP_S = 0.35e-6
MM_VMEM_BUDGET = 40 * 1024 * 1024


def _divisors(n, cands):
    return [c for c in cands if c <= n and n % c == 0] or [n]


def _mm_tiles(m, n, k, sa, sb, so, sr):
    best = None
    for tm in _divisors(m, (2816, 2048, 1408, 1024, 512, 256, 128)):
        for tn in _divisors(n, (2816, 2048, 1408, 1024, 512, 256, 128)):
            for tk in _divisors(k, (k, 2816, 2560, 2304, 2048, 1536, 1408, 1280, 1024, 512, 256, 128)):
                nk = k // tk
                vmem = 2 * (tm * tk * sa + tk * tn * sb + tm * tn * (so + sr)) + (tm * tn * 4 if nk > 1 else 0)
                vmem += tm * tk * 2 * (sa > 2) + tk * tn * 2 * (sb > 2) + tm * tn * 4
                if vmem > MM_VMEM_BUDGET:
                    continue
                ni, nj = m // tm, n // tn
                for i_outer in (True, False):
                    if i_outer:
                        a_reads = 1 if nk == 1 else nj
                        b_reads = 1 if (nk == 1 and nj == 1) else ni
                    else:
                        b_reads = 1 if nk == 1 else ni
                        a_reads = 1 if (nk == 1 and ni == 1) else nj
                    traffic = a_reads * m * k * sa + b_reads * k * n * sb + m * n * (so + sr)
                    t = max(traffic / V7X_HBM_BYTES_PER_S, 2.0 * m * n * k / V7X_MXU_FLOPS_PER_S)
                    t += ni * nj * nk * GRID_STEP_S
                    t += (tm * tk * sa + tk * tn * sb + tm * tn * so) / V7X_HBM_BYTES_PER_S
                    if best is None or t < best[0]:
                        best = (t, tm, tn, tk, i_outer)
    assert best is not None, (m, n, k)
    return best[1:]


def mm(a, b, *, ta=False, tb=False, res=None, out_dtype=F32, dep=None, name):
    m, k = (a.shape[1], a.shape[0]) if ta else a.shape
    n = b.shape[0] if tb else b.shape[1]
    assert (b.shape[1] if tb else b.shape[0]) == k
    has_res = res is not None
    tm, tn, tk, i_outer = _mm_tiles(m, n, k, a.dtype.itemsize, b.dtype.itemsize, jnp.dtype(out_dtype).itemsize,
                                    res.dtype.itemsize if has_res else 0)
    nk = k // tk
    deps = [] if dep is None else [dep]
    dn = (((0 if ta else 1,), (1 if tb else 0,)), ((), ()))

    def body_single(*refs):
        a_ref, b_ref = refs[:2]
        o_ref = refs[-1]
        out = lax.dot_general(a_ref[...].astype(BF16), b_ref[...].astype(BF16), dn, preferred_element_type=F32)
        if has_res:
            out = out + refs[2][...].astype(F32)
        o_ref[...] = out.astype(o_ref.dtype)

    def body(*refs):
        a_ref, b_ref = refs[:2]
        r_ref = refs[2] if has_res else None
        o_ref, acc_ref = refs[-2:]
        kk = pl.program_id(2)
        part = lax.dot_general(a_ref[...].astype(BF16), b_ref[...].astype(BF16), dn, preferred_element_type=F32)

        @pl.when(kk == 0)
        def _():
            acc_ref[...] = part

        @pl.when(kk > 0)
        def _():
            acc_ref[...] += part

        @pl.when(kk == nk - 1)
        def _():
            out = acc_ref[...]
            if has_res:
                out = out + r_ref[...].astype(F32)
            o_ref[...] = out.astype(o_ref.dtype)

    def ij(f):
        return (lambda g0, g1, q: f(g0, g1, q)) if i_outer else (lambda g0, g1, q: f(g1, g0, q))

    a_spec = pl.BlockSpec((tk, tm), ij(lambda i, j, q: (q, i))) if ta else pl.BlockSpec((tm, tk), ij(lambda i, j, q: (i, q)))
    b_spec = pl.BlockSpec((tn, tk), ij(lambda i, j, q: (j, q))) if tb else pl.BlockSpec((tk, tn), ij(lambda i, j, q: (q, j)))
    o_spec = pl.BlockSpec((tm, tn), ij(lambda i, j, q: (i, j)))
    in_specs = [a_spec, b_spec] + ([o_spec] if has_res else []) + [pl.BlockSpec((8, 128), lambda g0, g1, q: (0, 0))] * len(deps)
    args = (a, b) + ((res,) if has_res else ()) + tuple(deps)
    grid = (m // tm, n // tn, nk) if i_outer else (n // tn, m // tm, nk)
    return pl.pallas_call(
        body_single if nk == 1 else body, grid=grid, in_specs=in_specs, out_specs=o_spec,
        out_shape=jax.ShapeDtypeStruct((m, n), out_dtype),
        scratch_shapes=[] if nk == 1 else [pltpu.VMEM((tm, tn), F32)],
        compiler_params=_params(dimension_semantics=("parallel", "parallel", "arbitrary")), name=name,
    )(*args)


def rms_fwd(x, g, *, dep=None, name):
    L, D = x.shape
    tr = _pick(L, (256, 128))

    def body(x_ref, g_ref, *rest):
        o_ref = rest[-1]
        xv = x_ref[...]
        r = lax.rsqrt(jnp.mean(xv * xv, axis=-1, keepdims=True) + NORM_EPS)
        o_ref[...] = (xv * r * g_ref[...]).astype(o_ref.dtype)

    row = pl.BlockSpec((tr, D), lambda i: (i, 0))
    vec = pl.BlockSpec((1, D), lambda i: (0, 0))
    deps = [] if dep is None else [dep]
    return pl.pallas_call(body, grid=(L // tr,), in_specs=[row, vec] + [pl.BlockSpec((8, 128), lambda i: (0, 0))] * len(deps),
                          out_specs=row, out_shape=jax.ShapeDtypeStruct((L, D), BF16), name=name)(
        x, g.reshape(1, D), *deps)


def rms_bwd(x, g, dy, dres, *, name):
    L, D = x.shape
    tr = _pick(L, (256, 128))

    def body(x_ref, g_ref, dy_ref, dres_ref, dx_ref, dg_ref):
        xv = x_ref[...]
        r = lax.rsqrt(jnp.mean(xv * xv, axis=-1, keepdims=True) + NORM_EPS)
        xh = xv * r
        dyv = dy_ref[...].astype(F32)

        @pl.when(pl.program_id(0) == 0)
        def _():
            dg_ref[...] = jnp.zeros_like(dg_ref)

        dg_ref[...] += jnp.sum(dyv * xh, axis=0, keepdims=True)
        dxh = dyv * g_ref[...]
        dx_ref[...] = dres_ref[...] + r * (dxh - xh * jnp.mean(dxh * xh, axis=-1, keepdims=True))

    row = pl.BlockSpec((tr, D), lambda i: (i, 0))
    vec = pl.BlockSpec((1, D), lambda i: (0, 0))
    return pl.pallas_call(body, grid=(L // tr,), in_specs=[row, vec, row, row], out_specs=[row, vec],
                          out_shape=[jax.ShapeDtypeStruct((L, D), F32), jax.ShapeDtypeStruct((1, D), F32)],
                          compiler_params=_params(dimension_semantics=("arbitrary",)), name=name)(
        x, g.reshape(1, D), dy, dres)


def final_loss(h, g, target, *, name):
    L, D = h.shape
    tr = _pick(L, (256, 128))

    def body(x_ref, g_ref, t_ref, loss_ref, dx_ref, dg_ref):
        xv = x_ref[...]
        gv = g_ref[...]
        r = lax.rsqrt(jnp.mean(xv * xv, axis=-1, keepdims=True) + NORM_EPS)
        xh = xv * r
        err = xh * gv - t_ref[...]

        @pl.when(pl.program_id(0) == 0)
        def _():
            dg_ref[...] = jnp.zeros_like(dg_ref)
            loss_ref[...] = jnp.zeros_like(loss_ref)

        loss_ref[...] += 0.5 * jnp.sum(jnp.mean(err * err, axis=-1, keepdims=True), axis=0, keepdims=True)
        dyv = err * (1.0 / D)
        dg_ref[...] += jnp.sum(dyv * xh, axis=0, keepdims=True)
        dxh = dyv * gv
        dx_ref[...] = r * (dxh - xh * jnp.mean(dxh * xh, axis=-1, keepdims=True))

    row = pl.BlockSpec((tr, D), lambda i: (i, 0))
    vec = pl.BlockSpec((1, D), lambda i: (0, 0))
    one = pl.BlockSpec((1, 1), lambda i: (0, 0))
    return pl.pallas_call(body, grid=(L // tr,), in_specs=[row, vec, row], out_specs=[one, row, vec],
                          out_shape=[jax.ShapeDtypeStruct((1, 1), F32), jax.ShapeDtypeStruct((L, D), F32),
                                     jax.ShapeDtypeStruct((1, D), F32)],
                          compiler_params=_params(dimension_semantics=("arbitrary",)), name=name)(
        h, g.reshape(1, D), target)


def _cmul(ar, ai, br, bi):
    return ar * br - ai * bi, ar * bi + ai * br


def _powers(ar, ai):
    rows = [(ar, ai)]
    for _ in range(7):
        rows.append(_cmul(rows[-1][0], rows[-1][1], ar, ai))
    table = (jnp.concatenate([r[0] for r in rows], axis=0), jnp.concatenate([r[1] for r in rows], axis=0))
    return (rows[0], rows[1], rows[3]), table


def _block_scan(br, bi, steps, shift):
    yr, yi = br, bi
    for s, (pr, pi) in zip((1, 2, 4), steps):
        sr, si = shift(yr, s), shift(yi, s)
        yr, yi = yr + pr * sr - pi * si, yi + pr * si + pi * sr
    return yr, yi


def s5_scan_fwd(a_re, a_im, bu_re, bu_im, *, name):
    L, P = bu_re.shape
    W = _pick(P, (512, 256, 128))

    def body(ar_ref, ai_ref, br_ref, bi_ref, xr_ref, xi_ref):
        steps, (tr, ti) = _powers(ar_ref[...], ai_ref[...])
        row = lax.broadcasted_iota(jnp.int32, (8, W), 0)

        def shift(y, s):
            return jnp.where(row >= s, pltpu.roll(y, s, 0), 0.0)

        def step(t8, carry):
            cr, ci = carry
            base = pl.multiple_of(t8 * 8, 8)
            yr, yi = _block_scan(br_ref[pl.ds(base, 8), :], bi_ref[pl.ds(base, 8), :], steps, shift)
            xr = yr + tr * cr - ti * ci
            xi = yi + tr * ci + ti * cr
            xr_ref[pl.ds(base, 8), :] = xr
            xi_ref[pl.ds(base, 8), :] = xi
            return jnp.broadcast_to(xr[7:8, :], (8, W)), jnp.broadcast_to(xi[7:8, :], (8, W))

        zero = jnp.zeros((8, W), F32)
        lax.fori_loop(0, L // 8, step, (zero, zero), unroll=2)

    vec = pl.BlockSpec((1, W), lambda j: (0, j))
    col = pl.BlockSpec((L, W), lambda j: (0, j))
    return pl.pallas_call(body, grid=(P // W,), in_specs=[vec, vec, col, col], out_specs=[col, col],
                          out_shape=[jax.ShapeDtypeStruct((L, P), F32)] * 2,
                          compiler_params=_params(dimension_semantics=("parallel",)), name=name)(
        a_re, a_im, bu_re, bu_im)


def s5_scan_bwd(a_re, a_im, xs_re, xs_im, dx_re, dx_im, *, name):
    L, P = xs_re.shape
    W = _pick(P, (256, 128))

    def body(ar_ref, ai_ref, xr_ref, xi_ref, dr_ref, di_ref, lr_ref, li_ref, dar_ref, dai_ref):
        ar, ai = ar_ref[...], -ai_ref[...]
        steps, (tr, ti) = _powers(ar, ai)
        tr = jnp.concatenate([tr[j:j + 1, :] for j in range(7, -1, -1)], axis=0)
        ti = jnp.concatenate([ti[j:j + 1, :] for j in range(7, -1, -1)], axis=0)
        row8 = lax.broadcasted_iota(jnp.int32, (8, W), 0)
        nblk = L // 8

        def shift(y, s):
            return jnp.where(row8 < 8 - s, pltpu.roll(y, 8 - s, 0), 0.0)

        def step(s, carry):
            cr, ci = carry
            base = pl.multiple_of((nblk - 1 - s) * 8, 8)
            yr, yi = _block_scan(dr_ref[pl.ds(base, 8), :], di_ref[pl.ds(base, 8), :], steps, shift)
            lr = yr + tr * cr - ti * ci
            li = yi + tr * ci + ti * cr
            lr_ref[pl.ds(base, 8), :] = lr
            li_ref[pl.ds(base, 8), :] = li
            return jnp.broadcast_to(lr[0:1, :], (8, W)), jnp.broadcast_to(li[0:1, :], (8, W))

        zero = jnp.zeros((8, W), F32)
        lax.fori_loop(0, nblk, step, (zero, zero), unroll=2)
        row = lax.broadcasted_iota(jnp.int32, (L, W), 0)
        xpr = jnp.where(row >= 1, pltpu.roll(xr_ref[...], 1, 0), 0.0)
        xpi = jnp.where(row >= 1, pltpu.roll(xi_ref[...], 1, 0), 0.0)
        lr, li = lr_ref[...], li_ref[...]
        dar_ref[...] = jnp.sum(lr * xpr + li * xpi, axis=0, keepdims=True)
        dai_ref[...] = jnp.sum(li * xpr - lr * xpi, axis=0, keepdims=True)

    vec = pl.BlockSpec((1, W), lambda j: (0, j))
    col = pl.BlockSpec((L, W), lambda j: (0, j))
    return pl.pallas_call(body, grid=(P // W,), in_specs=[vec, vec, col, col, col, col],
                          out_specs=[col, col, vec, vec],
                          out_shape=[jax.ShapeDtypeStruct((L, P), F32)] * 2 + [jax.ShapeDtypeStruct((1, P), F32)] * 2,
                          compiler_params=_params(dimension_semantics=("parallel",)), name=name)(
        a_re, a_im, xs_re, xs_im, dx_re, dx_im)


def _gelu(y):
    c = math.sqrt(2.0 / math.pi)
    t = jnp.tanh(c * (y + 0.044715 * y * y * y))
    return 0.5 * y * (1.0 + t), t


def s5_out_fwd(y0, proj, dvec, glu_w, glu_b, *, name):
    L, C = y0.shape
    tr = _pick(L, (256, 128))

    def body(y_ref, u_ref, d_ref, w_ref, b_ref, o_ref):
        z, _ = _gelu(y_ref[...] + d_ref[...] * u_ref[...])
        zg = _dot(z, w_ref[...]) + b_ref[...]
        o_ref[...] = (z * _sigmoid(zg)).astype(o_ref.dtype)

    row = pl.BlockSpec((tr, C), lambda i: (i, 0))
    vec = pl.BlockSpec((1, C), lambda i: (0, 0))
    wsp = pl.BlockSpec((C, C), lambda i: (0, 0))
    return pl.pallas_call(body, grid=(L // tr,), in_specs=[row, row, vec, wsp, vec], out_specs=row,
                          out_shape=jax.ShapeDtypeStruct((L, C), BF16), name=name)(
        y0, proj, dvec, glu_w, glu_b)


def s5_out_bwd(y0, proj, dvec, glu_w, glu_b, dcat, *, name):
    L, C = y0.shape
    tr = _pick(L, (256, 128))

    def body(y_ref, u_ref, d_ref, w_ref, b_ref, do_ref, dy_ref, dud_ref, z_ref, dzg_ref, db_ref, dd_ref):
        u = u_ref[...]
        y = y_ref[...] + d_ref[...] * u
        z, t = _gelu(y)
        zg = _dot(z, w_ref[...]) + b_ref[...]
        s = _sigmoid(zg)
        do = do_ref[...]
        dzg = do * z * s * (1.0 - s)
        dz = do * s + _dot_nt(dzg, w_ref[...])
        c = math.sqrt(2.0 / math.pi)
        dgelu = 0.5 * (1.0 + t) + 0.5 * y * (1.0 - t * t) * c * (1.0 + 3.0 * 0.044715 * y * y)
        dy = dz * dgelu

        @pl.when(pl.program_id(0) == 0)
        def _():
            db_ref[...] = jnp.zeros_like(db_ref)
            dd_ref[...] = jnp.zeros_like(dd_ref)

        db_ref[...] += jnp.sum(dzg, axis=0, keepdims=True)
        dd_ref[...] += jnp.sum(dy * u, axis=0, keepdims=True)
        dy_ref[...] = dy
        dud_ref[...] = dy * d_ref[...]
        z_ref[...] = z.astype(BF16)
        dzg_ref[...] = dzg.astype(BF16)

    row = pl.BlockSpec((tr, C), lambda i: (i, 0))
    vec = pl.BlockSpec((1, C), lambda i: (0, 0))
    wsp = pl.BlockSpec((C, C), lambda i: (0, 0))
    return pl.pallas_call(body, grid=(L // tr,), in_specs=[row, row, vec, wsp, vec, row],
                          out_specs=[row, row, row, row, vec, vec],
                          out_shape=[jax.ShapeDtypeStruct((L, C), F32), jax.ShapeDtypeStruct((L, C), F32),
                                     jax.ShapeDtypeStruct((L, C), BF16), jax.ShapeDtypeStruct((L, C), BF16),
                                     jax.ShapeDtypeStruct((1, C), F32), jax.ShapeDtypeStruct((1, C), F32)],
                          compiler_params=_params(dimension_semantics=("arbitrary",)), name=name)(
        y0, proj, dvec, glu_w, glu_b, dcat)


def _hg_gates(xq, xf, lb, tri):
    C = xq.shape[0]
    sq = _sigmoid(xq)
    q = xq * sq
    sg = _sigmoid(xf)
    f = lb + (1.0 - lb) * sg
    kk = 1.0 - f
    b = _dot_f32(tri, jnp.log(f))
    bm = b[C // 2 - 1:C // 2, :]
    bl = b[C - 1:C, :]
    eb = jnp.exp(b)
    return dict(sq=sq, q=q, sg=sg, f=f, kk=kk, b=b, bm=bm, bl=bl, eb=eb, ebl=jnp.exp(bl),
                qb=q * eb, eqm=jnp.exp(b - bm), ekm=jnp.exp(bm - b), ekl=jnp.exp(bl - b))


def _tri(C, lower):
    r = lax.broadcasted_iota(jnp.int32, (C, C), 0)
    c = lax.broadcasted_iota(jnp.int32, (C, C), 1)
    return (r >= c) if lower else (c >= r)


def hgrn_fwd(proj, lb, norm_g, *, name):
    L = proj.shape[0]
    C, H, K = HG_CHUNK, HG_HEADS, HG_DIM
    HK = H * K
    nc = L // C

    def body(q_ref, f_ref, i_ref, g_ref, lb_ref, ng_ref, o_ref, sall_ref, st_ref):
        @pl.when(pl.program_id(0) == 0)
        def _():
            st_ref[...] = jnp.zeros_like(st_ref)

        mask = _tri(C, True)
        tri = mask.astype(F32)
        for h in range(H):
            sl = slice(h * K, (h + 1) * K)
            v = i_ref[:, sl]
            st = st_ref[h]
            sall_ref[h] = st
            gt = _hg_gates(q_ref[:, sl], f_ref[:, sl], lb_ref[:, sl], tri)
            qt = gt["q"] * gt["eqm"]
            kt = gt["kk"] * gt["ekm"]
            kh = gt["kk"] * gt["ekl"]
            att = jnp.where(mask, _dot_nt(qt, kt), 0.0)
            o = _dot(att, v) + _dot_nt(gt["qb"], st)
            st_ref[h] = st * gt["ebl"] + _dot_tn(v, kh)
            r = lax.rsqrt(jnp.mean(o * o, axis=-1, keepdims=True) + NORM_EPS)
            xg = g_ref[:, sl]
            o_ref[:, sl] = (o * r * ng_ref[:, sl] * (xg * _sigmoid(xg))).astype(o_ref.dtype)

    def blk(cb):
        return pl.BlockSpec((C, HK), lambda i: (i, cb))

    vec = pl.BlockSpec((1, HK), lambda i: (0, 0))
    return pl.pallas_call(
        body, grid=(nc,), in_specs=[blk(1), blk(2), blk(3), blk(4), vec, vec],
        out_specs=[pl.BlockSpec((C, HK), lambda i: (i, 0)), pl.BlockSpec((None, H, K, K), lambda i: (i, 0, 0, 0))],
        out_shape=[jax.ShapeDtypeStruct((L, HK), BF16), jax.ShapeDtypeStruct((nc, H, K, K), F32)],
        scratch_shapes=[pltpu.VMEM((H, K, K), F32)],
        compiler_params=_params(dimension_semantics=("arbitrary",)), name=name,
    )(proj, proj, proj, proj, lb, norm_g)


def hgrn_bwd(proj, lb, norm_g, sall, dcat, *, name):
    L = proj.shape[0]
    C, H, K = HG_CHUNK, HG_HEADS, HG_DIM
    HK = H * K
    nc = L // C

    def body(q_ref, f_ref, i_ref, g_ref, lb_ref, ng_ref, sall_ref, do_ref, dx_ref, dlb_ref, dng_ref, dst_ref):
        @pl.when(pl.program_id(0) == 0)
        def _():
            dst_ref[...] = jnp.zeros_like(dst_ref)
            dlb_ref[...] = jnp.zeros_like(dlb_ref)
            dng_ref[...] = jnp.zeros_like(dng_ref)

        mask = _tri(C, True)
        tri = mask.astype(F32)
        tri_t = _tri(C, False).astype(F32)
        rowi = lax.broadcasted_iota(jnp.int32, (C, K), 0)
        for h in range(H):
            sl = slice(h * K, (h + 1) * K)
            xq, xf, v, xg = q_ref[:, sl], f_ref[:, sl], i_ref[:, sl], g_ref[:, sl]
            lb_h, ng = lb_ref[:, sl], ng_ref[:, sl]
            st = sall_ref[h]
            dst = dst_ref[h]
            gt = _hg_gates(xq, xf, lb_h, tri)
            q, kk, qb = gt["q"], gt["kk"], gt["qb"]
            qt = q * gt["eqm"]
            kt = kk * gt["ekm"]
            kh = kk * gt["ekl"]
            att = jnp.where(mask, _dot_nt(qt, kt), 0.0)
            o = _dot(att, v) + _dot_nt(qb, st)
            r = lax.rsqrt(jnp.mean(o * o, axis=-1, keepdims=True) + NORM_EPS)
            oh = o * r
            sgg = _sigmoid(xg)
            silu_g = xg * sgg
            d_ob = do_ref[:, sl]
            d_on = d_ob * silu_g
            dxg = d_ob * (oh * ng) * (sgg * (1.0 + xg * (1.0 - sgg)))
            dng_ref[:, sl] += jnp.sum(d_on * oh, axis=0, keepdims=True)
            doh = d_on * ng
            do = r * (doh - oh * jnp.mean(doh * oh, axis=-1, keepdims=True))
            datt = jnp.where(mask, _dot_nt(do, v), 0.0)
            dv = _dot_tn(att, do) + _dot_nt(kh, dst)
            d_qb = _dot_f32(do, st)
            d_qt = _dot_f32(datt, kt)
            d_kt = _dot_f32_tn(datt, qt)
            d_kh = _dot_f32(v, dst)
            d_bl = jnp.sum(dst * st, axis=0, keepdims=True) * gt["ebl"] + jnp.sum(d_kh * kh, axis=0, keepdims=True)
            dst_ref[h] = dst * gt["ebl"] + _dot_tn(do, qb)
            dq = d_qt * gt["eqm"] + d_qb * gt["eb"]
            db = d_qt * qt + d_qb * qb - d_kt * kt - d_kh * kh
            db = db + jnp.where(rowi == C - 1, d_bl, 0.0)
            dkk = d_kt * gt["ekm"] + d_kh * gt["ekl"]
            dlg = _dot_f32(tri_t, db)
            df = dlg / gt["f"] - dkk
            sg = gt["sg"]
            dxf = df * (1.0 - lb_h) * sg * (1.0 - sg)
            dlb_ref[:, sl] += jnp.sum(df * (1.0 - sg), axis=0, keepdims=True)
            sq = gt["sq"]
            dxq = dq * (sq * (1.0 + xq * (1.0 - sq)))
            dx_ref[:, h * K:(h + 1) * K] = dxq.astype(dx_ref.dtype)
            dx_ref[:, HK + h * K:HK + (h + 1) * K] = dxf.astype(dx_ref.dtype)
            dx_ref[:, 2 * HK + h * K:2 * HK + (h + 1) * K] = dv.astype(dx_ref.dtype)
            dx_ref[:, 3 * HK + h * K:3 * HK + (h + 1) * K] = dxg.astype(dx_ref.dtype)

    def blk(cb):
        return pl.BlockSpec((C, HK), lambda i: (nc - 1 - i, cb))

    vec = pl.BlockSpec((1, HK), lambda i: (0, 0))
    return pl.pallas_call(
        body, grid=(nc,),
        in_specs=[blk(1), blk(2), blk(3), blk(4), vec, vec,
                  pl.BlockSpec((None, H, K, K), lambda i: (nc - 1 - i, 0, 0, 0)), blk(1)],
        out_specs=[pl.BlockSpec((C, 4 * HK), lambda i: (nc - 1 - i, 0)), vec, vec],
        out_shape=[jax.ShapeDtypeStruct((L, 4 * HK), BF16), jax.ShapeDtypeStruct((1, HK), F32),
                   jax.ShapeDtypeStruct((1, HK), F32)],
        scratch_shapes=[pltpu.VMEM((H, K, K), F32)],
        compiler_params=_params(dimension_semantics=("arbitrary",)), name=name,
    )(proj, proj, proj, proj, lb, norm_g, sall, dcat)


def _shift_down(x, k, row):
    return jnp.where(row >= k, pltpu.roll(x, k, 0), 0.0)


def _shift_up(x, k, row):
    n = x.shape[0]
    return jnp.where(row < n - k, pltpu.roll(x, n - k, 0), 0.0)


def convgate_fwd(hu, conv_w, conv_b, *, name):
    L, C2 = hu.shape
    C = C2 // 2
    tc = _pick(C, (256, 128))
    nb = C // tc

    def body(a_ref, b_ref, wa_ref, wb_ref, ba_ref, bb_ref, o_ref):
        row = lax.broadcasted_iota(jnp.int32, (L, tc), 0)

        def conv(x, w, bias):
            return w[2:3, :] * x + w[1:2, :] * _shift_down(x, 1, row) + w[0:1, :] * _shift_down(x, 2, row) + bias

        ca = conv(a_ref[...], wa_ref[...], ba_ref[...])
        cb = conv(b_ref[...], wb_ref[...], bb_ref[...])
        o_ref[...] = (ca * _sigmoid(ca) * cb).astype(o_ref.dtype)

    def col(off, rows):
        return pl.BlockSpec((rows, tc), lambda j: (0, j + off))

    return pl.pallas_call(
        body, grid=(nb,), in_specs=[col(0, L), col(nb, L), col(0, 3), col(nb, 3), col(0, 1), col(nb, 1)],
        out_specs=col(0, L), out_shape=jax.ShapeDtypeStruct((L, C), BF16),
        compiler_params=_params(dimension_semantics=("parallel",)), name=name,
    )(hu, hu, conv_w, conv_w, conv_b, conv_b)


def convgate_bwd(hu, conv_w, conv_b, dact, *, name):
    L, C2 = hu.shape
    C = C2 // 2
    tc = _pick(C, (256, 128))
    nb = C // tc

    def body(a_ref, b_ref, wa_ref, wb_ref, ba_ref, bb_ref, d_ref, dxa_ref, dxb_ref, dwa_ref, dwb_ref, dba_ref, dbb_ref):
        row = lax.broadcasted_iota(jnp.int32, (L, tc), 0)

        def conv(x, w, bias):
            x1 = _shift_down(x, 1, row)
            x2 = _shift_down(x, 2, row)
            return w[2:3, :] * x + w[1:2, :] * x1 + w[0:1, :] * x2 + bias, x1, x2

        xa, xb = a_ref[...], b_ref[...]
        wa, wb = wa_ref[...], wb_ref[...]
        ca, xa1, xa2 = conv(xa, wa, ba_ref[...])
        cb, xb1, xb2 = conv(xb, wb, bb_ref[...])
        d = d_ref[...]
        sa = _sigmoid(ca)
        dca = d * cb * (sa * (1.0 + ca * (1.0 - sa)))
        dcb = d * (ca * sa)

        def back(dc, w, x, x1, x2, dx_ref, dw_ref, db_ref):
            dx = w[2:3, :] * dc + w[1:2, :] * _shift_up(dc, 1, row) + w[0:1, :] * _shift_up(dc, 2, row)
            dx_ref[...] = dx.astype(dx_ref.dtype)
            dw_ref[...] = jnp.concatenate([jnp.sum(dc * x2, axis=0, keepdims=True),
                                           jnp.sum(dc * x1, axis=0, keepdims=True),
                                           jnp.sum(dc * x, axis=0, keepdims=True)], axis=0)
            db_ref[...] = jnp.sum(dc, axis=0, keepdims=True)

        back(dca, wa, xa, xa1, xa2, dxa_ref, dwa_ref, dba_ref)
        back(dcb, wb, xb, xb1, xb2, dxb_ref, dwb_ref, dbb_ref)

    def col(off, rows):
        return pl.BlockSpec((rows, tc), lambda j: (0, j + off))

    outs = pl.pallas_call(
        body, grid=(nb,),
        in_specs=[col(0, L), col(nb, L), col(0, 3), col(nb, 3), col(0, 1), col(nb, 1), col(0, L)],
        out_specs=[col(0, L), col(0, L), col(0, 3), col(0, 3), col(0, 1), col(0, 1)],
        out_shape=[jax.ShapeDtypeStruct((L, C), BF16)] * 2 + [jax.ShapeDtypeStruct((3, C), F32)] * 2
        + [jax.ShapeDtypeStruct((1, C), F32)] * 2,
        compiler_params=_params(dimension_semantics=("parallel",)), name=name,
    )(hu, hu, conv_w, conv_w, conv_b, conv_b, dact)
    dxa, dxb, dwa, dwb, dba, dbb = outs
    return (jnp.concatenate([dxa, dxb], axis=1), jnp.concatenate([dwa, dwb], axis=1),
            jnp.concatenate([dba, dbb], axis=1))


def _to_branch_order(t, d):
    L, W = t.shape
    return t if d == 1 else t.reshape(L // d, d, W).transpose(1, 0, 2).reshape(L, W)


def _to_token_order(t, d):
    L, W = t.shape
    return t if d == 1 else t.reshape(d, L // d, W).transpose(1, 0, 2).reshape(L, W)


def rope_tables(positions):
    half = ROT_DIM // 2
    inv_freq = ROPE_THETA ** (-jnp.arange(half, dtype=F32) * 2.0 / ROT_DIM)
    ang = positions.astype(F32)[:, None] * inv_freq
    cos, sin = jnp.cos(ang), jnp.sin(ang)
    L = positions.shape[0]
    one = jnp.ones((L, ATT_E - ROT_DIM), F32)
    zero = jnp.zeros((L, ATT_E - ROT_DIM), F32)
    zh = jnp.zeros((L, half), F32)
    tc = jnp.concatenate([cos, cos, one], axis=1)
    ts1 = jnp.concatenate([zh, sin, zero], axis=1)
    ts2 = jnp.concatenate([-sin, zh, zero], axis=1)
    return tuple(jnp.concatenate([t, t], axis=1) for t in (tc, ts1, ts2))


def rope_fwd(qkv, tabs, *, name):
    L = qkv.shape[0]
    W = 512
    tr = _pick(L, (256, 128))
    nq = 1536 // W
    scale = ATT_E ** -0.5

    def body(x_ref, c_ref, s1_ref, s2_ref, *o_refs):
        c = jnp.concatenate([c_ref[...]] * 4, axis=1)
        s1 = jnp.concatenate([s1_ref[...]] * 4, axis=1)
        s2 = jnp.concatenate([s2_ref[...]] * 4, axis=1)
        for j, o_ref in enumerate(o_refs):
            x = x_ref[:, j * W:(j + 1) * W]
            if j < 2 * nq:
                x = x * c + pltpu.roll(x, 8, 1) * s1 + pltpu.roll(x, W - 8, 1) * s2
            if j < nq:
                x = x * scale
            o_ref[...] = x.astype(o_ref.dtype)

    slab = pl.BlockSpec((tr, W), lambda i: (i, 0))
    tab = pl.BlockSpec((tr, 128), lambda i: (i, 0))
    return pl.pallas_call(body, grid=(L // tr,), in_specs=[pl.BlockSpec((tr, 3 * nq * W), lambda i: (i, 0)), tab, tab, tab],
                          out_specs=[slab] * (3 * nq), out_shape=[jax.ShapeDtypeStruct((L, W), BF16)] * (3 * nq),
                          compiler_params=_params(dimension_semantics=("parallel",)), name=name)(qkv, *tabs)


def rope_bwd(slabs, tabs, *, name):
    L, W = slabs[0].shape
    tr = _pick(L, (256, 128))
    nq = len(slabs) // 3
    scale = ATT_E ** -0.5

    def body(*refs):
        d_refs, (c_ref, s1_ref, s2_ref, o_ref) = refs[:3 * nq], refs[3 * nq:]
        c = jnp.concatenate([c_ref[...]] * 4, axis=1)
        s1 = jnp.concatenate([s1_ref[...]] * 4, axis=1)
        s2 = jnp.concatenate([s2_ref[...]] * 4, axis=1)
        for j, d_ref in enumerate(d_refs):
            dy = d_ref[...]
            if j < 2 * nq:
                dy = dy * c + pltpu.roll(dy * s1, W - 8, 1) + pltpu.roll(dy * s2, 8, 1)
            if j < nq:
                dy = dy * scale
            o_ref[:, j * W:(j + 1) * W] = dy.astype(o_ref.dtype)

    slab = pl.BlockSpec((tr, W), lambda i: (i, 0))
    tab = pl.BlockSpec((tr, 128), lambda i: (i, 0))
    return pl.pallas_call(body, grid=(L // tr,), in_specs=[slab] * (3 * nq) + [tab, tab, tab],
                          out_specs=pl.BlockSpec((tr, 3 * nq * W), lambda i: (i, 0)),
                          out_shape=jax.ShapeDtypeStruct((L, 3 * nq * W), BF16),
                          compiler_params=_params(dimension_semantics=("parallel",)), name=name)(*slabs, *tabs)


def _att_masks(has_prev):
    qi = lax.broadcasted_iota(jnp.int32, (ATT_BLOCK, ATT_BLOCK), 0)
    kj = lax.broadcasted_iota(jnp.int32, (ATT_BLOCK, ATT_BLOCK), 1)
    return qi >= kj, (kj >= qi) & has_prev


def attn_fwd(qp, kp, vp, d, *, name):
    L, W = qp.shape
    B, E = ATT_BLOCK, ATT_E
    nblk = L // B
    nb = nblk // d

    def body(q_ref, kc_ref, kp_ref, vc_ref, vp_ref, o_ref, l_ref):
        has_prev = (pl.program_id(0) % nb) > 0
        mc, mp = _att_masks(has_prev)
        for h in range(ATT_HPG):
            sl = slice(h * E, (h + 1) * E)
            q = q_ref[:, sl]
            sc = jnp.where(mc, _dot_nt(q, kc_ref[:, sl]), NEG_BIG)
            sp = jnp.where(mp, _dot_nt(q, kp_ref[:, sl]), NEG_BIG)
            m = jnp.maximum(jnp.max(sc, axis=-1, keepdims=True), jnp.max(sp, axis=-1, keepdims=True))
            pc = jnp.exp(sc - m)
            pp = jnp.exp(sp - m)
            den = jnp.sum(pc, axis=-1, keepdims=True) + jnp.sum(pp, axis=-1, keepdims=True)
            o = (_dot(pc, vc_ref[:, sl]) + _dot(pp, vp_ref[:, sl])) / den
            o_ref[:, sl] = o
            l_ref[:, sl] = jnp.broadcast_to(m + jnp.log(den), (B, E))

    cur = pl.BlockSpec((B, W), lambda j: (j, 0))
    prev = pl.BlockSpec((B, W), lambda j: (jnp.maximum(j - 1, 0), 0))
    return pl.pallas_call(body, grid=(nblk,), in_specs=[cur, cur, prev, cur, prev], out_specs=[cur, cur],
                          out_shape=[jax.ShapeDtypeStruct((L, W), F32)] * 2,
                          compiler_params=_params(dimension_semantics=("parallel",)), name=name)(
        qp, kp, kp, vp, vp)


def attn_bwd(qp, kp, vp, lse, do, dl, d, *, name):
    L, W = qp.shape
    B, E = ATT_BLOCK, ATT_E
    nblk = L // B
    nb = nblk // d

    def body(q_ref, kc_ref, kp_ref, vc_ref, vp_ref, l_ref, do_ref, dl_ref, dq_ref, dk_ref, dv_ref, tkc, tkp, tvc, tvp):
        j = pl.program_id(0)

        @pl.when(j == 0)
        def _():
            dk_ref[...] = jnp.zeros_like(dk_ref)
            dv_ref[...] = jnp.zeros_like(dv_ref)

        has_prev = (j % nb) > 0
        mc, mp = _att_masks(has_prev)
        for h in range(ATT_HPG):
            sl = slice(h * E, (h + 1) * E)
            q = q_ref[:, sl]
            kc, kpv, vc, vpv = kc_ref[:, sl], kp_ref[:, sl], vc_ref[:, sl], vp_ref[:, sl]
            lse_h = l_ref[:, h * E:h * E + 1]
            dl_h = dl_ref[:, h * E:h * E + 1]
            doh = do_ref[:, sl]
            pc = jnp.where(mc, jnp.exp(_dot_nt(q, kc) - lse_h), 0.0)
            pp = jnp.where(mp, jnp.exp(_dot_nt(q, kpv) - lse_h), 0.0)
            dsc = pc * (_dot_nt(doh, vc) - dl_h)
            dsp = pp * (_dot_nt(doh, vpv) - dl_h)
            dq_ref[:, sl] = _dot(dsc, kc) + _dot(dsp, kpv)
            tkc[:, sl] = _dot_tn(dsc, q)
            tkp[:, sl] = _dot_tn(dsp, q)
            tvc[:, sl] = _dot_tn(pc, doh)
            tvp[:, sl] = _dot_tn(pp, doh)
        cur = pl.multiple_of(j * B, B)
        prv = pl.multiple_of(jnp.maximum(j - 1, 0) * B, B)
        dk_ref[pl.ds(cur, B), :] += tkc[...]
        dv_ref[pl.ds(cur, B), :] += tvc[...]
        dk_ref[pl.ds(prv, B), :] += tkp[...]
        dv_ref[pl.ds(prv, B), :] += tvp[...]

    cur = pl.BlockSpec((B, W), lambda j: (j, 0))
    prev = pl.BlockSpec((B, W), lambda j: (jnp.maximum(j - 1, 0), 0))
    full = pl.BlockSpec((L, W), lambda j: (0, 0))
    return pl.pallas_call(body, grid=(nblk,), in_specs=[cur, cur, prev, cur, prev, cur, cur, cur],
                          out_specs=[cur, full, full], out_shape=[jax.ShapeDtypeStruct((L, W), F32)] * 3,
                          scratch_shapes=[pltpu.VMEM((B, W), F32)] * 4,
                          compiler_params=_params(dimension_semantics=("arbitrary",)), name=name)(
        qp, kp, kp, vp, vp, lse, do, dl)


def _merge_alpha(l_refs):
    ls = [r[...] for r in l_refs]
    m = jnp.maximum(jnp.maximum(ls[0], ls[1]), ls[2])
    es = [jnp.exp(l - m) for l in ls]
    den = es[0] + es[1] + es[2]
    return [e / den for e in es]


def merge_fwd(os_, ls_, *, name):
    L, W = os_[0].shape
    tr = _pick(L, (256, 128))

    def body(o0, o1, o2, l0, l1, l2, out_ref):
        al = _merge_alpha((l0, l1, l2))
        out_ref[...] = (al[0] * o0[...] + al[1] * o1[...] + al[2] * o2[...]).astype(out_ref.dtype)

    row = pl.BlockSpec((tr, W), lambda i: (i, 0))
    return pl.pallas_call(body, grid=(L // tr,), in_specs=[row] * 6, out_specs=row,
                          out_shape=jax.ShapeDtypeStruct((L, W), BF16), name=name)(*os_, *ls_)


def merge_bwd(os_, ls_, do, *, name):
    L, W = do.shape
    tr = _pick(L, (256, 128))

    def body(o0, o1, o2, l0, l1, l2, do_ref, d0, d1, d2, e0, e1, e2):
        al = _merge_alpha((l0, l1, l2))
        dov = do_ref[...]
        r = lax.broadcasted_iota(jnp.int32, (W, W), 0) // ATT_E
        c = lax.broadcasted_iota(jnp.int32, (W, W), 1) // ATT_E
        ones_blk = (r == c).astype(F32)
        t = jnp.zeros_like(dov)
        for a, o in zip(al, (o0, o1, o2)):
            t = t + a * _dot_f32(dov * o[...], ones_blk)
        for a, d_ref, e_ref in zip(al, (d0, d1, d2), (e0, e1, e2)):
            d_ref[...] = a * dov
            e_ref[...] = a * t

    row = pl.BlockSpec((tr, W), lambda i: (i, 0))
    return pl.pallas_call(body, grid=(L // tr,), in_specs=[row] * 7, out_specs=[row] * 6,
                          out_shape=[jax.ShapeDtypeStruct((L, W), F32)] * 6, name=name)(*os_, *ls_, do)


def _me_and_peers():
    x, y, c = lax.axis_index("x"), lax.axis_index("y"), lax.axis_index("c")
    peers = []
    for k in range(1, N_DEV):
        px = 1 - x if k & 4 else x
        py = 1 - y if k & 2 else y
        pc = 1 - c if k & 1 else c
        peers.append((px, py, pc))
    return (x, y, c), peers


def _index(dev):
    return 4 * dev[0] + 2 * dev[1] + dev[2]


def _hbm(a):
    return pltpu.with_memory_space_constraint(a, pltpu.HBM)


HBM_SPEC = pl.BlockSpec(memory_space=pltpu.HBM)
SEM_SPEC = pl.BlockSpec(memory_space=pltpu.SEMAPHORE)
DATAFLOW = pltpu.SideEffectType.DATAFLOW_SIDE_EFFECTING


def _remote(src_ref, land_ref, slotted, me, peer, src_is_mine, send_sem, recv_sem, k):
    sender, receiver = (me, peer) if src_is_mine else (peer, me)
    src = src_ref.at[_index(receiver)] if slotted else src_ref
    return pltpu.make_async_remote_copy(src_ref=src, dst_ref=land_ref.at[_index(sender)], send_sem=send_sem.at[k],
                                        recv_sem=recv_sem.at[k], device_id=peer, device_id_type=MESH_ID)


def copies_start(arrays, slotted, *, name):
    n = len(arrays)
    lands = [lax.empty(a.shape if slotted else (N_DEV,) + a.shape, a.dtype) for a in arrays]

    def body(*refs):
        x_refs, land_refs = refs[:n], refs[n:2 * n]
        send, recv = refs[2 * n:3 * n], refs[3 * n:4 * n]
        token = refs[-1]
        me, peers = _me_and_peers()
        for w in range(n):
            for k, peer in enumerate(peers):
                _remote(x_refs[w], land_refs[w], slotted, me, peer, True, send[w], recv[w], k).start()
            if not slotted:
                pltpu.make_async_copy(x_refs[w], land_refs[w].at[_index(me)], recv[w].at[N_DEV - 1]).start()
        token[...] = jnp.zeros_like(token)

    sem = pltpu.SemaphoreType.DMA((N_DEV,))
    out_shape = ([sem] * (2 * n) + [pltpu.HBM(a.shape, a.dtype) for a in arrays]
                 + [pltpu.HBM(l.shape, l.dtype) for l in lands] + [jax.ShapeDtypeStruct((8, 128), F32)])
    outs = pl.pallas_call(
        body, name=name, out_shape=out_shape, in_specs=[HBM_SPEC] * (2 * n),
        out_specs=[SEM_SPEC] * (2 * n) + [HBM_SPEC] * (2 * n) + [pl.BlockSpec(memory_space=pltpu.VMEM)],
        input_output_aliases={i: 2 * n + i for i in range(2 * n)},
        compiler_params=pltpu.CompilerParams(has_side_effects=DATAFLOW),
    )(*[_hbm(a) for a in arrays], *[_hbm(l) for l in lands])
    handles = [(outs[w], outs[n + w], outs[2 * n + w], outs[3 * n + w]) for w in range(n)]
    return handles, outs[-1]


def copies_wait(handle, slotted, after, *, name):
    send_sem, recv_sem, x_thru, land_thru = handle

    def body(x_ref, land_ref, send_ref, recv_ref, after_ref, x_out, land_out):
        me, peers = _me_and_peers()
        for k, peer in enumerate(peers):
            _remote(x_ref, land_ref, slotted, me, peer, True, send_ref, recv_ref, k).wait_send()
        for k, peer in enumerate(peers):
            _remote(x_ref, land_ref, slotted, me, peer, False, send_ref, recv_ref, k).wait_recv()
        if not slotted:
            pltpu.make_async_copy(x_ref, land_ref.at[_index(me)], recv_ref.at[N_DEV - 1]).wait()

    return pl.pallas_call(
        body, name=name, out_shape=(pltpu.HBM(x_thru.shape, x_thru.dtype), pltpu.HBM(land_thru.shape, land_thru.dtype)),
        in_specs=(HBM_SPEC, HBM_SPEC, SEM_SPEC, SEM_SPEC, pl.BlockSpec(memory_space=pl.ANY)),
        out_specs=(HBM_SPEC, HBM_SPEC), input_output_aliases={0: 0, 1: 1},
        compiler_params=pltpu.CompilerParams(has_side_effects=DATAFLOW),
    )(x_thru, land_thru, send_sem, recv_sem, after)


def cast_bf16(x, *, ncols=None, name):
    R = x.shape[0]
    C = ncols or x.shape[1]
    tr = _pick(R, (512, 352, 256, 128, 64))

    def body(x_ref, o_ref):
        o_ref[...] = x_ref[...].astype(BF16)

    row = pl.BlockSpec((tr, C), lambda i: (i, 0))
    return pl.pallas_call(body, grid=(R // tr,), in_specs=[row], out_specs=row,
                          out_shape=jax.ShapeDtypeStruct((R, C), BF16), name=name)(x)


def cast_bf16_layer(x3, layer, *, name):
    _, R, C = x3.shape
    tr = _pick(R, (512, 352, 256, 128, 64))

    def body(x_ref, o_ref):
        o_ref[...] = x_ref[...].astype(BF16)

    return pl.pallas_call(body, grid=(R // tr,), in_specs=[pl.BlockSpec((None, tr, C), lambda i: (layer, i, 0))],
                          out_specs=pl.BlockSpec((tr, C), lambda i: (i, 0)),
                          out_shape=jax.ShapeDtypeStruct((R, C), BF16), name=name)(x3)


def _blockdiag_call(b, build, G, r, c, name):
    def body_build(b_ref, o_ref):
        o_ref[...] = jnp.zeros_like(o_ref)
        for g in range(G):
            o_ref[g * r:(g + 1) * r, g * c:(g + 1) * c] = b_ref[g]

    def body_extract(d_ref, o_ref):
        for g in range(G):
            o_ref[g] = d_ref[g * r:(g + 1) * r, g * c:(g + 1) * c]

    out = jax.ShapeDtypeStruct((G * r, G * c) if build else (G, r, c), F32)
    return pl.pallas_call(body_build if build else body_extract, out_shape=out, name=name)(b)


def make_blockdiag(G, r, c, name):
    @jax.custom_vjp
    def blockdiag(b):
        return _blockdiag_call(b, True, G, r, c, name + "_build")

    def fwd(b):
        return blockdiag(b), None

    def bwd(_, g):
        return (_blockdiag_call(g, False, G, r, c, name + "_extract"),)

    blockdiag.defvjp(fwd, bwd)
    return blockdiag


def _my_index():
    return 4 * lax.axis_index("x") + 2 * lax.axis_index("y") + lax.axis_index("c")


def cols_from_shards(g, *, name):
    _, K, n = g.shape
    tk = _pick(K, (256, 128))

    def body(g_ref, o_ref):
        for i in range(N_DEV):
            o_ref[:, i * n:(i + 1) * n] = g_ref[i]

    return pl.pallas_call(body, grid=(K // tk,), in_specs=[pl.BlockSpec((N_DEV, tk, n), lambda i: (0, i, 0))],
                          out_specs=pl.BlockSpec((tk, N_DEV * n), lambda i: (i, 0)),
                          out_shape=jax.ShapeDtypeStruct((K, N_DEV * n), g.dtype), name=name)(g)


def shards_from_cols(w, *, name):
    K, N = w.shape
    n = N // N_DEV
    tk = _pick(K, (256, 128))

    def body(w_ref, o_ref):
        for i in range(N_DEV):
            o_ref[i] = w_ref[:, i * n:(i + 1) * n].astype(o_ref.dtype)

    return pl.pallas_call(body, grid=(K // tk,), in_specs=[pl.BlockSpec((tk, N), lambda i: (i, 0))],
                          out_specs=pl.BlockSpec((N_DEV, tk, n), lambda i: (0, i, 0)),
                          out_shape=jax.ShapeDtypeStruct((N_DEV, K, n), BF16), name=name)(w)


def _adamw(w, g, m, v):
    m = ADAM_B1 * m + (1.0 - ADAM_B1) * g
    v = ADAM_B2 * v + (1.0 - ADAM_B2) * (g * g)
    m_hat = m / (1.0 - ADAM_B1 ** ADAM_STEP)
    v_hat = v / (1.0 - ADAM_B2 ** ADAM_STEP)
    delta = -ADAM_LR * (m_hat / (jnp.sqrt(v_hat) + ADAM_EPS) + ADAM_WD * w)
    return delta, m, v


def reduce_adamw(recv, own, own_slotted, me, w, m, v, *, layer=0, n_layers=1, into=None, name):
    _, R, C = recv.shape
    tr = _pick(R, (352, 320, 288, 256, 128, 64, 32, 16, 8))
    off = layer * (R // tr)

    def body(me_ref, r_ref, own_ref, w_ref, m_ref, v_ref, *rest):
        g_ref, d_ref, nm_ref, nv_ref = rest[-4:]
        mine = me_ref[0]
        g = None
        for i in range(N_DEV):
            part = jnp.where(mine == i, own_ref[...], r_ref[i]).astype(F32)
            g = part if g is None else g + part
        delta, nm, nv = _adamw(w_ref[...], g, m_ref[...], v_ref[...])
        g_ref[...] = g
        d_ref[...] = delta
        nm_ref[...] = nm
        nv_ref[...] = nv

    row = pl.BlockSpec((tr, C), lambda i, me_ref: (i + off, 0))
    own_spec = (pl.BlockSpec((None, tr, C), lambda i, me_ref: (me_ref[0], i, 0)) if own_slotted
                else pl.BlockSpec((tr, C), lambda i, me_ref: (i, 0)))
    rest = [] if into is None else list(into)
    grid_spec = pltpu.PrefetchScalarGridSpec(
        num_scalar_prefetch=1, grid=(R // tr,),
        in_specs=[pl.BlockSpec((N_DEV, tr, C), lambda i, me_ref: (0, i, 0)), own_spec, row, row, row]
        + [pl.BlockSpec(memory_space=pl.ANY)] * len(rest),
        out_specs=[row] * 4)
    return pl.pallas_call(body, grid_spec=grid_spec, out_shape=[jax.ShapeDtypeStruct((n_layers * R, C), F32)] * 4,
                          input_output_aliases={6 + k: k for k in range(len(rest))},
                          compiler_params=_params(dimension_semantics=("parallel",)), name=name)(
        me.reshape(1).astype(jnp.int32), recv, own, w, m, v, *rest)


def _s5_prepare(A_re, A_im, log_dt, B_re, B_im, C_re, C_im):
    G, P, Cg = S5_GROUPS, S5_STATE, S5_GROUP
    dt = jnp.exp(log_dt)[:, None]
    mag = jnp.exp(A_re * dt)
    ab_re = mag * jnp.cos(A_im * dt)
    ab_im = mag * jnp.sin(A_im * dt)
    den = A_re * A_re + A_im * A_im
    nr, ni = ab_re - 1.0, ab_im
    c_re = (nr * A_re + ni * A_im) / den
    c_im = (ni * A_re - nr * A_im) / den
    Bb_re = c_re[..., None] * B_re - c_im[..., None] * B_im
    Bb_im = c_re[..., None] * B_im + c_im[..., None] * B_re
    def dense_in(b, name):
        return make_blockdiag(G, Cg, P, name)(b.transpose(0, 2, 1))

    def dense_out(c, name):
        return make_blockdiag(G, P, Cg, name)(c.transpose(0, 2, 1))

    return (ab_re.reshape(1, G * P), ab_im.reshape(1, G * P), dense_in(Bb_re, "s5_wb_re"), dense_in(Bb_im, "s5_wb_im"),
            dense_out(C_re, "s5_wc_re"), dense_out(-C_im, "s5_wc_im"))


def _lower_bound(gamma):
    return jnp.cumsum(jax.nn.softmax(gamma, axis=0), axis=0)[0:1]


def _ffn_fwd(h, g_norm, get_w_in, conv_w, conv_b, get_w_out, tag):
    hn = rms_fwd(h, g_norm, name=tag + "_rms")
    w_in = get_w_in(hn)
    hu = mm(hn, w_in, tb=True, name=tag + "_in")
    act = convgate_fwd(hu, conv_w, conv_b, name=tag + "_gate")
    w_out = get_w_out(act)
    h_out = mm(act, w_out, res=h, name=tag + "_out")
    return h_out, (hn, hu, act), w_in, w_out


def _ffn_bwd(h, g_norm, w_in, conv_w, conv_b, w_out, saved, dh, tag, send_dw_in, send_dw_out):
    hn, hu, act = saved
    sent = send_dw_out(mm(act, dh, ta=True, out_dtype=BF16, name=tag + "_dwout"))
    dact = mm(dh, w_out, tb=True, dep=sent, name=tag + "_dact")
    dhu, dconv_w, dconv_b = convgate_bwd(hu, conv_w, conv_b, dact, name=tag + "_dgate")
    sent = send_dw_in(mm(dhu, hn, ta=True, out_dtype=BF16, name=tag + "_dwin"))
    dhn = mm(dhu, w_in, dep=sent, name=tag + "_dhn")
    dh_in, dg = rms_bwd(h, g_norm, dhn, dh, name=tag + "_drms")
    return dh_in, dg, dconv_w, dconv_b


def kernel(x, positions, norm_mix, norm_ffn, norm_final, mix_w_in, mix_w_out, s5_A_re, s5_A_im, s5_log_dt, s5_B_re, s5_B_im, s5_C_re, s5_C_im, s5_D, s5_glu_w, s5_glu_b, hgrn_gamma, hgrn_norm, att_w_qkv, att_w_o, ffn_w_in, ffn_conv_w, ffn_conv_b, ffn_w_out, loss_target, m_norm_mix, m_norm_ffn, m_norm_final, m_mix_w_in, m_mix_w_out, m_s5_A_re, m_s5_A_im, m_s5_log_dt, m_s5_B_re, m_s5_B_im, m_s5_C_re, m_s5_C_im, m_s5_D, m_s5_glu_w, m_s5_glu_b, m_hgrn_gamma, m_hgrn_norm, m_att_w_qkv, m_att_w_o, m_ffn_w_in, m_ffn_conv_w, m_ffn_conv_b, m_ffn_w_out, v_norm_mix, v_norm_ffn, v_norm_final, v_mix_w_in, v_mix_w_out, v_s5_A_re, v_s5_A_im, v_s5_log_dt, v_s5_B_re, v_s5_B_im, v_s5_C_re, v_s5_C_im, v_s5_D, v_s5_glu_w, v_s5_glu_b, v_hgrn_gamma, v_hgrn_norm, v_att_w_qkv, v_att_w_o, v_ffn_w_in, v_ffn_conv_w, v_ffn_conv_b, v_ffn_w_out):
    W = dict(norm_mix=norm_mix, norm_ffn=norm_ffn, norm_final=norm_final, mix_w_in=mix_w_in, mix_w_out=mix_w_out,
             s5_A_re=s5_A_re, s5_A_im=s5_A_im, s5_log_dt=s5_log_dt, s5_B_re=s5_B_re, s5_B_im=s5_B_im,
             s5_C_re=s5_C_re, s5_C_im=s5_C_im, s5_D=s5_D, s5_glu_w=s5_glu_w, s5_glu_b=s5_glu_b,
             hgrn_gamma=hgrn_gamma, hgrn_norm=hgrn_norm, att_w_qkv=att_w_qkv, att_w_o=att_w_o, ffn_w_in=ffn_w_in,
             ffn_conv_w=ffn_conv_w, ffn_conv_b=ffn_conv_b, ffn_w_out=ffn_w_out)
    M = dict(norm_mix=m_norm_mix, norm_ffn=m_norm_ffn, norm_final=m_norm_final, mix_w_in=m_mix_w_in,
             mix_w_out=m_mix_w_out, s5_A_re=m_s5_A_re, s5_A_im=m_s5_A_im, s5_log_dt=m_s5_log_dt, s5_B_re=m_s5_B_re,
             s5_B_im=m_s5_B_im, s5_C_re=m_s5_C_re, s5_C_im=m_s5_C_im, s5_D=m_s5_D, s5_glu_w=m_s5_glu_w,
             s5_glu_b=m_s5_glu_b, hgrn_gamma=m_hgrn_gamma, hgrn_norm=m_hgrn_norm, att_w_qkv=m_att_w_qkv,
             att_w_o=m_att_w_o, ffn_w_in=m_ffn_w_in, ffn_conv_w=m_ffn_conv_w, ffn_conv_b=m_ffn_conv_b,
             ffn_w_out=m_ffn_w_out)
    V = dict(norm_mix=v_norm_mix, norm_ffn=v_norm_ffn, norm_final=v_norm_final, mix_w_in=v_mix_w_in,
             mix_w_out=v_mix_w_out, s5_A_re=v_s5_A_re, s5_A_im=v_s5_A_im, s5_log_dt=v_s5_log_dt, s5_B_re=v_s5_B_re,
             s5_B_im=v_s5_B_im, s5_C_re=v_s5_C_re, s5_C_im=v_s5_C_im, s5_D=v_s5_D, s5_glu_w=v_s5_glu_w,
             s5_glu_b=v_s5_glu_b, hgrn_gamma=v_hgrn_gamma, hgrn_norm=v_hgrn_norm, att_w_qkv=v_att_w_qkv,
             att_w_o=v_att_w_o, ffn_w_in=v_ffn_w_in, ffn_conv_w=v_ffn_conv_w, ffn_conv_b=v_ffn_conv_b,
             ffn_w_out=v_ffn_w_out)
    return _step(x[0], positions[0], loss_target[0], W, M, V)


TRANSPOSED = ("mix_w_in", "att_w_qkv", "ffn_w_in")
SMALL = ("norm_mix", "norm_ffn", "norm_final", "s5_A_re", "s5_A_im", "s5_log_dt", "s5_B_re", "s5_B_im", "s5_C_re",
         "s5_C_im", "s5_D", "s5_glu_b", "hgrn_gamma", "hgrn_norm", "ffn_conv_b")
ORDER = ("norm_mix", "norm_ffn", "norm_final", "mix_w_in", "mix_w_out", "s5_A_re", "s5_A_im", "s5_log_dt", "s5_B_re",
         "s5_B_im", "s5_C_re", "s5_C_im", "s5_D", "s5_glu_w", "s5_glu_b", "hgrn_gamma", "hgrn_norm", "att_w_qkv",
         "att_w_o", "ffn_w_in", "ffn_conv_w", "ffn_conv_b", "ffn_w_out")
PACK_COLS = 1024


def _step(x, positions, target, W, M, V):
    L, D = x.shape
    me = 4 * lax.axis_index("x") + 2 * lax.axis_index("y") + lax.axis_index("c")
    n_cw = W["ffn_conv_w"].shape[-1]
    T = {n: tuple(jnp.swapaxes(d[n], -1, -2) for d in (W, M, V)) for n in TRANSPOSED}
    shards = {
        "mix_w_in": cast_bf16(T["mix_w_in"][0][0], name="mix_w_in_cast"),
        "conv_w": W["ffn_conv_w"].reshape(6, n_cw),
        "s5_glu_w": cast_bf16(W["s5_glu_w"][0], name="s5_glu_w_cast"),
        "mix_w_out": cast_bf16(W["mix_w_out"][0], name="mix_w_out_cast"),
        "ffn_w_in0": cast_bf16_layer(T["ffn_w_in"][0], 0, name="ffn_w_in0_cast"),
        "ffn_w_out0": cast_bf16_layer(W["ffn_w_out"], 0, name="ffn_w_out0_cast"),
        "att_w_qkv": cast_bf16(T["att_w_qkv"][0][0], name="att_w_qkv_cast"),
        "att_w_o": cast_bf16(W["att_w_o"][0], name="att_w_o_cast"),
        "ffn_w_in1": cast_bf16_layer(T["ffn_w_in"][0], 1, name="ffn_w_in1_cast"),
        "ffn_w_out1": cast_bf16_layer(W["ffn_w_out"], 1, name="ffn_w_out1_cast"),
    }
    gather_handles, token = copies_start(list(shards.values()), False, name="gather_start")
    gather_handle = dict(zip(shards, gather_handles))

    def gathered(key, after, cols):
        _, land = copies_wait(gather_handle[key], False, after, name=key + "_gwait")
        return cols_from_shards(land, name=key + "_asm") if cols else land.reshape(-1, land.shape[-1])

    conv_b = W["ffn_conv_b"].reshape(2, 1, -1)

    s5_params = (W["s5_A_re"][0], W["s5_A_im"][0], W["s5_log_dt"][0], W["s5_B_re"][0], W["s5_B_im"][0],
                 W["s5_C_re"][0], W["s5_C_im"][0])
    (a_re, a_im, wb_re, wb_im, wc_re, wc_im), s5_prep_vjp = jax.vjp(_s5_prepare, *s5_params)
    dvec = W["s5_D"].reshape(1, S5_WIDTH)
    glu_b = W["s5_glu_b"].reshape(1, S5_WIDTH)
    lb, lb_vjp = jax.vjp(_lower_bound, W["hgrn_gamma"])
    hg_norm = W["hgrn_norm"].reshape(1, -1)
    tabs = rope_tables(positions)

    hn0 = rms_fwd(x, W["norm_mix"][0], dep=token, name="l0_rms")
    w_mix_in = gathered("mix_w_in", hn0, False)
    proj = mm(hn0, w_mix_in, tb=True, name="l0_proj")
    u_bf = cast_bf16(proj, ncols=S5_WIDTH, name="l0_u_cast")
    bu_re = mm(u_bf, wb_re, name="s5_bu_re")
    bu_im = mm(u_bf, wb_im, name="s5_bu_im")
    xs_re, xs_im = s5_scan_fwd(a_re, a_im, bu_re, bu_im, name="s5_scan")
    y0 = mm(xs_im, wc_im, res=mm(xs_re, wc_re, name="s5_y_re"), name="s5_y_im")
    w_glu = gathered("s5_glu_w", y0, False)
    oa = s5_out_fwd(y0, proj, dvec, w_glu, glu_b, name="s5_out")
    ob, hg_states = hgrn_fwd(proj, lb, hg_norm, name="hgrn_fwd")
    cat = jnp.concatenate([oa, ob], axis=1)
    w_mix_out = gathered("mix_w_out", cat, False)
    h1 = mm(cat, w_mix_out, res=x, name="l0_mix_out")
    _, cw_all = copies_wait(gather_handle["conv_w"], False, h1, name="conv_w_gwait")
    conv_w = cw_all.transpose(1, 0, 2).reshape(2, 3, N_DEV * n_cw)
    w_ffn_in, w_ffn_out = [None, None], [None, None]
    h2, ffn0_saved, w_ffn_in[0], w_ffn_out[0] = _ffn_fwd(
        h1, W["norm_ffn"][0], lambda a: gathered("ffn_w_in0", a, False), conv_w[0], conv_b[0],
        lambda a: gathered("ffn_w_out0", a, False), "ffn0")

    hn2 = rms_fwd(h2, W["norm_mix"][1], name="l1_rms")
    w_qkv = gathered("att_w_qkv", hn2, False)
    qkv = mm(hn2, w_qkv, tb=True, name="l1_qkv")
    qkv_r = rope_fwd(qkv, tabs, name="rope_fwd")
    att_in, att_o, att_l = [], [], []
    for g, d in enumerate(ATT_DILATIONS):
        qp = _to_branch_order(qkv_r[g], d)
        kp = _to_branch_order(qkv_r[3 + g], d)
        vp = _to_branch_order(qkv_r[6 + g], d)
        o_p, l_p = attn_fwd(qp, kp, vp, d, name=f"attn_fwd{g}")
        att_in.append((qp, kp, vp, l_p))
        att_o.append(_to_token_order(o_p, d))
        att_l.append(_to_token_order(l_p, d))
    o_att = merge_fwd(att_o, att_l, name="merge_fwd")
    w_o = gathered("att_w_o", o_att, True)
    h3 = mm(o_att, w_o, res=h2, name="l1_mix_out")
    h4, ffn1_saved, w_ffn_in[1], w_ffn_out[1] = _ffn_fwd(
        h3, W["norm_ffn"][1], lambda a: gathered("ffn_w_in1", a, False), conv_w[1], conv_b[1],
        lambda a: gathered("ffn_w_out1", a, False), "ffn1")

    exchanges = {}

    def send_grad(key, g, cols):
        if cols:
            parts = shards_from_cols(g, name=key + "_split")
        else:
            parts = g.reshape(N_DEV, g.shape[0] // N_DEV, g.shape[1])
        (handle,), sent = copies_start([parts], True, name=key + "_xstart")
        exchanges[key] = handle
        return sent

    loss, dh4, dg_final = final_loss(h4, W["norm_final"], target, name="final_loss")
    dh3, dg_ffn1, dcw1, dcb1 = _ffn_bwd(h3, W["norm_ffn"][1], w_ffn_in[1], conv_w[1], conv_b[1], w_ffn_out[1],
                                        ffn1_saved, dh4, "ffn1", lambda g: send_grad("ffn_w_in1", g, False),
                                        lambda g: send_grad("ffn_w_out1", g, False))
    sent = send_grad("att_w_o", mm(o_att, dh3, ta=True, name="l1_dwo"), True)
    d_oatt = mm(dh3, w_o, tb=True, dep=sent, name="l1_dmix")
    mb = merge_bwd(att_o, att_l, d_oatt, name="merge_bwd")
    dq_t, dk_t, dv_t = [], [], []
    for g, d in enumerate(ATT_DILATIONS):
        qp, kp, vp, l_p = att_in[g]
        dq_p, dk_p, dv_p = attn_bwd(qp, kp, vp, l_p, _to_branch_order(mb[g], d), _to_branch_order(mb[3 + g], d), d,
                                    name=f"attn_bwd{g}")
        dq_t.append(_to_token_order(dq_p, d))
        dk_t.append(_to_token_order(dk_p, d))
        dv_t.append(_to_token_order(dv_p, d))
    d_qkv = rope_bwd(dq_t + dk_t + dv_t, tabs, name="rope_bwd")
    sent = send_grad("att_w_qkv", mm(d_qkv, hn2, ta=True, out_dtype=BF16, name="l1_dwqkv"), False)
    d_hn2 = mm(d_qkv, w_qkv, dep=sent, name="l1_dhn")
    dh2, dg_mix1 = rms_bwd(h2, W["norm_mix"][1], d_hn2, dh3, name="l1_drms")

    dh1, dg_ffn0, dcw0, dcb0 = _ffn_bwd(h1, W["norm_ffn"][0], w_ffn_in[0], conv_w[0], conv_b[0], w_ffn_out[0],
                                        ffn0_saved, dh2, "ffn0", lambda g: send_grad("ffn_w_in0", g, False),
                                        lambda g: send_grad("ffn_w_out0", g, False))
    sent = send_grad("mix_w_out", mm(cat, dh1, ta=True, out_dtype=BF16, name="l0_dwout"), False)
    dcat = mm(dh1, w_mix_out, tb=True, dep=sent, name="l0_dcat")
    d_hg, dlb, dhg_norm = hgrn_bwd(proj, lb, hg_norm, hg_states, dcat, name="hgrn_bwd")
    dy, du_d, z_bf, dzg, dglu_b, dD = s5_out_bwd(y0, proj, dvec, w_glu, glu_b, dcat, name="s5_dout")
    sent = send_grad("s5_glu_w", mm(z_bf, dzg, ta=True, out_dtype=BF16, name="s5_dglu"), False)
    dxs_re = mm(dy, wc_re, tb=True, dep=sent, name="s5_dxs_re")
    dxs_im = mm(dy, wc_im, tb=True, name="s5_dxs_im")
    dwc_re = mm(xs_re, dy, ta=True, name="s5_dwc_re")
    dwc_im = mm(xs_im, dy, ta=True, name="s5_dwc_im")
    dbu_re, dbu_im, da_re, da_im = s5_scan_bwd(a_re, a_im, xs_re, xs_im, dxs_re, dxs_im, name="s5_dscan")
    du = mm(dbu_im, wb_im, tb=True, res=mm(dbu_re, wb_re, tb=True, res=du_d, name="s5_du_re"), out_dtype=BF16,
            name="s5_du_im")
    dwb_re = mm(u_bf, dbu_re, ta=True, name="s5_dwb_re")
    dwb_im = mm(u_bf, dbu_im, ta=True, name="s5_dwb_im")
    s5_small = s5_prep_vjp((da_re, da_im, dwb_re, dwb_im, dwc_re, dwc_im))
    d_proj = jnp.concatenate([du, d_hg], axis=1)
    sent = send_grad("mix_w_in", mm(d_proj, hn0, ta=True, out_dtype=BF16, name="l0_dwin"), False)
    d_hn0 = mm(d_proj, w_mix_in, dep=sent, name="l0_dhn")
    grad_x, dg_mix0 = rms_bwd(x, W["norm_mix"][0], d_hn0, dh1, name="l0_drms")
    (d_gamma,) = lb_vjp(dlb)
    out = {}

    dA_re, dA_im, dlog_dt, dB_re, dB_im, dC_re, dC_im = s5_small
    small_g = dict(norm_mix=jnp.concatenate([dg_mix0, dg_mix1], axis=0), norm_ffn=jnp.concatenate([dg_ffn0, dg_ffn1], axis=0),
                   norm_final=dg_final, s5_A_re=dA_re, s5_A_im=dA_im, s5_log_dt=dlog_dt, s5_B_re=dB_re, s5_B_im=dB_im,
                   s5_C_re=dC_re, s5_C_im=dC_im, s5_D=dD, s5_glu_b=dglu_b, hgrn_gamma=d_gamma, hgrn_norm=dhg_norm,
                   ffn_conv_b=jnp.concatenate([dcb0, dcb1], axis=0))
    conv_w_g = jnp.stack([dcw0, dcw1], axis=0)
    sizes = [math.prod(W[n].shape) for n in SMALL]
    n_conv = conv_w_g.size
    total = sum(sizes) + n_conv + 1
    rows = -(-total // PACK_COLS)
    rows = -(-rows // 8) * 8
    pad = rows * PACK_COLS - total

    def pack(vals, conv_part, last):
        flat = [v.reshape(-1).astype(F32) for v in vals] + [conv_part.reshape(-1), last.reshape(-1),
                                                            jnp.zeros((pad,), F32)]
        return jnp.concatenate(flat).reshape(rows, PACK_COLS)

    def conv_full(shard):
        col_owner = lax.broadcasted_iota(jnp.int32, (2, 3, N_DEV * n_cw), 2) // n_cw
        return jnp.where(col_owner == me, jnp.tile(shard, (1, 1, N_DEV)), 0.0)

    zero1 = jnp.zeros((1,), F32)
    g_pack = pack([small_g[n] for n in SMALL], conv_w_g, loss)
    w_pack = pack([W[n] for n in SMALL], conv_full(W["ffn_conv_w"]), zero1)
    m_pack = pack([M[n] for n in SMALL], conv_full(M["ffn_conv_w"]), zero1)
    v_pack = pack([V[n] for n in SMALL], conv_full(V["ffn_conv_w"]), zero1 + 1.0)
    (small_handle,), small_sent = copies_start([g_pack], False, name="small_xstart")

    def finish(name, n_layers):
        w3, m3, v3 = T[name] if name in TRANSPOSED else (W[name], M[name], V[name])
        res = None
        for layer in reversed(range(n_layers)):
            key = name if n_layers == 1 else f"{name}{layer}"
            own, recv = copies_wait(exchanges[key], True, small_sent, name=key + "_xwait")
            _, R, Cn = recv.shape
            res = reduce_adamw(recv, own, True, me, w3.reshape(n_layers * R, Cn), m3.reshape(n_layers * R, Cn),
                               v3.reshape(n_layers * R, Cn), layer=layer, n_layers=n_layers, into=res,
                               name=key + "_adamw")
        res = [r.reshape(w3.shape) for r in res]
        return tuple(jnp.swapaxes(r, -1, -2) for r in res) if name in TRANSPOSED else tuple(res)

    for name in ("ffn_w_out", "ffn_w_in"):
        out[name] = finish(name, 2)
    for name in ("att_w_o", "att_w_qkv", "mix_w_out", "s5_glu_w", "mix_w_in"):
        out[name] = finish(name, 1)

    small_own, small_recv = copies_wait(small_handle, False, out["s5_glu_w"][0], name="small_xwait")
    res = reduce_adamw(small_recv, small_own, False, me, w_pack, m_pack, v_pack, name="small_adamw")
    flat = [r.reshape(-1) for r in res]
    off = 0
    for n, sz in zip(SMALL, sizes):
        out[n] = tuple(f[off:off + sz].reshape(W[n].shape) for f in flat)
        off += sz
    conv_res = [f[off:off + n_conv].reshape(2, 3, N_DEV * n_cw) for f in flat]
    out["ffn_conv_w"] = tuple(lax.dynamic_slice(c, (0, 0, me * n_cw), (2, 3, n_cw)) for c in conv_res)
    off += n_conv
    loss_total = flat[0][off]

    result = [loss_total, grad_x[None]]
    for k in range(4):
        result += [out[n][k] for n in ORDER]
    return tuple(result)
```

```python
import functools
import math

import jax
import jax.numpy as jnp
from jax import lax
from jax.experimental import pallas as pl
from jax.experimental.pallas import tpu as pltpu

F32 = jnp.float32
BF16 = jnp.bfloat16
MESH_ID = pl.DeviceIdType.MESH
N_DEV = 8
VMEM_LIMIT_BYTES = 56 * 1024 * 1024

NORM_EPS = 1e-6
S5_WIDTH, S5_GROUP, S5_GROUPS, S5_STATE = 512, 16, 32, 64
HG_HEADS, HG_DIM, HG_CHUNK = 4, 128, 64
ATT_E, ATT_HPG, ATT_BLOCK = 64, 8, 128
ATT_DILATIONS = (1, 4, 16)
ROT_DIM, ROPE_THETA = 16, 500000.0
D_FF = 2816
ADAM_LR, ADAM_B1, ADAM_B2, ADAM_EPS, ADAM_WD, ADAM_STEP = 0.001, 0.9, 0.999, 1e-08, 0.01, 10
NEG_BIG = -1e30


def _params(**kw):
    return pltpu.CompilerParams(vmem_limit_bytes=VMEM_LIMIT_BYTES, **kw)


def _pick(n, cands):
    for c in cands:
        if n % c == 0:
            return c
    return n


def _dot(a, b):
    return jnp.dot(a.astype(BF16), b.astype(BF16), preferred_element_type=F32)


def _dot_nt(a, b):
    return lax.dot_general(a.astype(BF16), b.astype(BF16), (((1,), (1,)), ((), ())), preferred_element_type=F32)


def _dot_tn(a, b):
    return lax.dot_general(a.astype(BF16), b.astype(BF16), (((0,), (0,)), ((), ())), preferred_element_type=F32)


def _dot_f32(a, b):
    return jnp.dot(a, b, preferred_element_type=F32, precision=lax.Precision.HIGHEST)


def _dot_f32_nt(a, b):
    return lax.dot_general(a, b, (((1,), (1,)), ((), ())), preferred_element_type=F32, precision=lax.Precision.HIGHEST)


def _dot_f32_tn(a, b):
    return lax.dot_general(a, b, (((0,), (0,)), ((), ())), preferred_element_type=F32, precision=lax.Precision.HIGHEST)


def _sigmoid(x):
    return 1.0 / (1.0 + jnp.exp(-x))


V7X_HBM_BYTES_PER_S = 3.2e12
V7X_MXU_FLOPS_PER_S = 0.7e15
GRID_STEP_S = 0.35e-6
MM_VMEM_BUDGET = 40 * 1024 * 1024


def _divisors(n, cands):
    return [c for c in cands if c <= n and n % c == 0] or [n]


def _mm_tiles(m, n, k, sa, sb, so, sr):
    best = None
    for tm in _divisors(m, (2816, 2048, 1408, 1024, 512, 256, 128)):
        for tn in _divisors(n, (2816, 2048, 1408, 1024, 512, 256, 128)):
            for tk in _divisors(k, (k, 2816, 2560, 2304, 2048, 1536, 1408, 1280, 1024, 512, 256, 128)):
                nk = k // tk
                vmem = 2 * (tm * tk * sa + tk * tn * sb + tm * tn * (so + sr)) + (tm * tn * 4 if nk > 1 else 0)
                vmem += tm * tk * 2 * (sa > 2) + tk * tn * 2 * (sb > 2) + tm * tn * 4
                if vmem > MM_VMEM_BUDGET:
                    continue
                ni, nj = m // tm, n // tn
                for i_outer in (True, False):
                    if i_outer:
                        a_reads = 1 if nk == 1 else nj
                        b_reads = 1 if (nk == 1 and nj == 1) else ni
                    else:
                        b_reads = 1 if nk == 1 else ni
                        a_reads = 1 if (nk == 1 and ni == 1) else nj
                    traffic = a_reads * m * k * sa + b_reads * k * n * sb + m * n * (so + sr)
                    t = max(traffic / V7X_HBM_BYTES_PER_S, 2.0 * m * n * k / V7X_MXU_FLOPS_PER_S)
                    t += ni * nj * nk * GRID_STEP_S
                    t += (tm * tk * sa + tk * tn * sb + tm * tn * so) / V7X_HBM_BYTES_PER_S
                    if best is None or t < best[0]:
                        best = (t, tm, tn, tk, i_outer)
    assert best is not None, (m, n, k)
    return best[1:]


def mm(a, b, *, ta=False, tb=False, res=None, out_dtype=F32, dep=None, name):
    m, k = (a.shape[1], a.shape[0]) if ta else a.shape
    n = b.shape[0] if tb else b.shape[1]
    assert (b.shape[1] if tb else b.shape[0]) == k
    has_res = res is not None
    tm, tn, tk, i_outer = _mm_tiles(m, n, k, a.dtype.itemsize, b.dtype.itemsize, jnp.dtype(out_dtype).itemsize,
                                    res.dtype.itemsize if has_res else 0)
    nk = k // tk
    deps = [] if dep is None else [dep]
    dn = (((0 if ta else 1,), (1 if tb else 0,)), ((), ()))

    def body_single(*refs):
        a_ref, b_ref = refs[:2]
        o_ref = refs[-1]
        out = lax.dot_general(a_ref[...].astype(BF16), b_ref[...].astype(BF16), dn, preferred_element_type=F32)
        if has_res:
            out = out + refs[2][...].astype(F32)
        o_ref[...] = out.astype(o_ref.dtype)

    def body(*refs):
        a_ref, b_ref = refs[:2]
        r_ref = refs[2] if has_res else None
        o_ref, acc_ref = refs[-2:]
        kk = pl.program_id(2)
        part = lax.dot_general(a_ref[...].astype(BF16), b_ref[...].astype(BF16), dn, preferred_element_type=F32)

        @pl.when(kk == 0)
        def _():
            acc_ref[...] = part

        @pl.when(kk > 0)
        def _():
            acc_ref[...] += part

        @pl.when(kk == nk - 1)
        def _():
            out = acc_ref[...]
            if has_res:
                out = out + r_ref[...].astype(F32)
            o_ref[...] = out.astype(o_ref.dtype)

    def ij(f):
        return (lambda g0, g1, q: f(g0, g1, q)) if i_outer else (lambda g0, g1, q: f(g1, g0, q))

    a_spec = pl.BlockSpec((tk, tm), ij(lambda i, j, q: (q, i))) if ta else pl.BlockSpec((tm, tk), ij(lambda i, j, q: (i, q)))
    b_spec = pl.BlockSpec((tn, tk), ij(lambda i, j, q: (j, q))) if tb else pl.BlockSpec((tk, tn), ij(lambda i, j, q: (q, j)))
    o_spec = pl.BlockSpec((tm, tn), ij(lambda i, j, q: (i, j)))
    in_specs = [a_spec, b_spec] + ([o_spec] if has_res else []) + [pl.BlockSpec((8, 128), lambda g0, g1, q: (0, 0))] * len(deps)
    args = (a, b) + ((res,) if has_res else ()) + tuple(deps)
    grid = (m // tm, n // tn, nk) if i_outer else (n // tn, m // tm, nk)
    return pl.pallas_call(
        body_single if nk == 1 else body, grid=grid, in_specs=in_specs, out_specs=o_spec,
        out_shape=jax.ShapeDtypeStruct((m, n), out_dtype),
        scratch_shapes=[] if nk == 1 else [pltpu.VMEM((tm, tn), F32)],
        compiler_params=_params(dimension_semantics=("parallel", "parallel", "arbitrary")), name=name,
    )(*args)


def rms_fwd(x, g, *, dep=None, name):
    L, D = x.shape
    tr = _pick(L, (256, 128))

    def body(x_ref, g_ref, *rest):
        o_ref = rest[-1]
        xv = x_ref[...]
        r = lax.rsqrt(jnp.mean(xv * xv, axis=-1, keepdims=True) + NORM_EPS)
        o_ref[...] = (xv * r * g_ref[...]).astype(o_ref.dtype)

    row = pl.BlockSpec((tr, D), lambda i: (i, 0))
    vec = pl.BlockSpec((1, D), lambda i: (0, 0))
    deps = [] if dep is None else [dep]
    return pl.pallas_call(body, grid=(L // tr,), in_specs=[row, vec] + [pl.BlockSpec((8, 128), lambda i: (0, 0))] * len(deps),
                          out_specs=row, out_shape=jax.ShapeDtypeStruct((L, D), BF16), name=name)(
        x, g.reshape(1, D), *deps)


def rms_bwd(x, g, dy, dres, *, name):
    L, D = x.shape
    tr = _pick(L, (256, 128))

    def body(x_ref, g_ref, dy_ref, dres_ref, dx_ref, dg_ref):
        xv = x_ref[...]
        r = lax.rsqrt(jnp.mean(xv * xv, axis=-1, keepdims=True) + NORM_EPS)
        xh = xv * r
        dyv = dy_ref[...].astype(F32)

        @pl.when(pl.program_id(0) == 0)
        def _():
            dg_ref[...] = jnp.zeros_like(dg_ref)

        dg_ref[...] += jnp.sum(dyv * xh, axis=0, keepdims=True)
        dxh = dyv * g_ref[...]
        dx_ref[...] = dres_ref[...] + r * (dxh - xh * jnp.mean(dxh * xh, axis=-1, keepdims=True))

    row = pl.BlockSpec((tr, D), lambda i: (i, 0))
    vec = pl.BlockSpec((1, D), lambda i: (0, 0))
    return pl.pallas_call(body, grid=(L // tr,), in_specs=[row, vec, row, row], out_specs=[row, vec],
                          out_shape=[jax.ShapeDtypeStruct((L, D), F32), jax.ShapeDtypeStruct((1, D), F32)],
                          compiler_params=_params(dimension_semantics=("arbitrary",)), name=name)(
        x, g.reshape(1, D), dy, dres)


def final_loss(h, g, target, *, name):
    L, D = h.shape
    tr = _pick(L, (256, 128))

    def body(x_ref, g_ref, t_ref, loss_ref, dx_ref, dg_ref):
        xv = x_ref[...]
        gv = g_ref[...]
        r = lax.rsqrt(jnp.mean(xv * xv, axis=-1, keepdims=True) + NORM_EPS)
        xh = xv * r
        err = xh * gv - t_ref[...]

        @pl.when(pl.program_id(0) == 0)
        def _():
            dg_ref[...] = jnp.zeros_like(dg_ref)
            loss_ref[...] = jnp.zeros_like(loss_ref)

        loss_ref[...] += 0.5 * jnp.sum(jnp.mean(err * err, axis=-1, keepdims=True), axis=0, keepdims=True)
        dyv = err * (1.0 / D)
        dg_ref[...] += jnp.sum(dyv * xh, axis=0, keepdims=True)
        dxh = dyv * gv
        dx_ref[...] = r * (dxh - xh * jnp.mean(dxh * xh, axis=-1, keepdims=True))

    row = pl.BlockSpec((tr, D), lambda i: (i, 0))
    vec = pl.BlockSpec((1, D), lambda i: (0, 0))
    one = pl.BlockSpec((1, 1), lambda i: (0, 0))
    return pl.pallas_call(body, grid=(L // tr,), in_specs=[row, vec, row], out_specs=[one, row, vec],
                          out_shape=[jax.ShapeDtypeStruct((1, 1), F32), jax.ShapeDtypeStruct((L, D), F32),
                                     jax.ShapeDtypeStruct((1, D), F32)],
                          compiler_params=_params(dimension_semantics=("arbitrary",)), name=name)(
        h, g.reshape(1, D), target)


def _cmul(ar, ai, br, bi):
    return ar * br - ai * bi, ar * bi + ai * br


def _powers(ar, ai):
    rows = [(ar, ai)]
    for _ in range(7):
        rows.append(_cmul(rows[-1][0], rows[-1][1], ar, ai))
    table = (jnp.concatenate([r[0] for r in rows], axis=0), jnp.concatenate([r[1] for r in rows], axis=0))
    return (rows[0], rows[1], rows[3]), table


def _block_scan(br, bi, steps, shift):
    yr, yi = br, bi
    for s, (pr, pi) in zip((1, 2, 4), steps):
        sr, si = shift(yr, s), shift(yi, s)
        yr, yi = yr + pr * sr - pi * si, yi + pr * si + pi * sr
    return yr, yi


def s5_scan_fwd(a_re, a_im, bu_re, bu_im, *, name):
    L, P = bu_re.shape
    W = _pick(P, (512, 256, 128))

    def body(ar_ref, ai_ref, br_ref, bi_ref, xr_ref, xi_ref):
        steps, (tr, ti) = _powers(ar_ref[...], ai_ref[...])
        row = lax.broadcasted_iota(jnp.int32, (8, W), 0)

        def shift(y, s):
            return jnp.where(row >= s, pltpu.roll(y, s, 0), 0.0)

        def step(t8, carry):
            cr, ci = carry
            base = pl.multiple_of(t8 * 8, 8)
            yr, yi = _block_scan(br_ref[pl.ds(base, 8), :], bi_ref[pl.ds(base, 8), :], steps, shift)
            xr = yr + tr * cr - ti * ci
            xi = yi + tr * ci + ti * cr
            xr_ref[pl.ds(base, 8), :] = xr
            xi_ref[pl.ds(base, 8), :] = xi
            return jnp.broadcast_to(xr[7:8, :], (8, W)), jnp.broadcast_to(xi[7:8, :], (8, W))

        zero = jnp.zeros((8, W), F32)
        lax.fori_loop(0, L // 8, step, (zero, zero), unroll=2)

    vec = pl.BlockSpec((1, W), lambda j: (0, j))
    col = pl.BlockSpec((L, W), lambda j: (0, j))
    return pl.pallas_call(body, grid=(P // W,), in_specs=[vec, vec, col, col], out_specs=[col, col],
                          out_shape=[jax.ShapeDtypeStruct((L, P), F32)] * 2,
                          compiler_params=_params(dimension_semantics=("parallel",)), name=name)(
        a_re, a_im, bu_re, bu_im)


def s5_scan_bwd(a_re, a_im, xs_re, xs_im, dx_re, dx_im, *, name):
    L, P = xs_re.shape
    W = _pick(P, (256, 128))

    def body(ar_ref, ai_ref, xr_ref, xi_ref, dr_ref, di_ref, lr_ref, li_ref, dar_ref, dai_ref):
        ar, ai = ar_ref[...], -ai_ref[...]
        steps, (tr, ti) = _powers(ar, ai)
        tr = jnp.concatenate([tr[j:j + 1, :] for j in range(7, -1, -1)], axis=0)
        ti = jnp.concatenate([ti[j:j + 1, :] for j in range(7, -1, -1)], axis=0)
        row8 = lax.broadcasted_iota(jnp.int32, (8, W), 0)
        nblk = L // 8

        def shift(y, s):
            return jnp.where(row8 < 8 - s, pltpu.roll(y, 8 - s, 0), 0.0)

        def step(s, carry):
            cr, ci = carry
            base = pl.multiple_of((nblk - 1 - s) * 8, 8)
            yr, yi = _block_scan(dr_ref[pl.ds(base, 8), :], di_ref[pl.ds(base, 8), :], steps, shift)
            lr = yr + tr * cr - ti * ci
            li = yi + tr * ci + ti * cr
            lr_ref[pl.ds(base, 8), :] = lr
            li_ref[pl.ds(base, 8), :] = li
            return jnp.broadcast_to(lr[0:1, :], (8, W)), jnp.broadcast_to(li[0:1, :], (8, W))

        zero = jnp.zeros((8, W), F32)
        lax.fori_loop(0, nblk, step, (zero, zero), unroll=2)
        row = lax.broadcasted_iota(jnp.int32, (L, W), 0)
        xpr = jnp.where(row >= 1, pltpu.roll(xr_ref[...], 1, 0), 0.0)
        xpi = jnp.where(row >= 1, pltpu.roll(xi_ref[...], 1, 0), 0.0)
        lr, li = lr_ref[...], li_ref[...]
        dar_ref[...] = jnp.sum(lr * xpr + li * xpi, axis=0, keepdims=True)
        dai_ref[...] = jnp.sum(li * xpr - lr * xpi, axis=0, keepdims=True)

    vec = pl.BlockSpec((1, W), lambda j: (0, j))
    col = pl.BlockSpec((L, W), lambda j: (0, j))
    return pl.pallas_call(body, grid=(P // W,), in_specs=[vec, vec, col, col, col, col],
                          out_specs=[col, col, vec, vec],
                          out_shape=[jax.ShapeDtypeStruct((L, P), F32)] * 2 + [jax.ShapeDtypeStruct((1, P), F32)] * 2,
                          compiler_params=_params(dimension_semantics=("parallel",)), name=name)(
        a_re, a_im, xs_re, xs_im, dx_re, dx_im)


def _gelu(y):
    c = math.sqrt(2.0 / math.pi)
    t = jnp.tanh(c * (y + 0.044715 * y * y * y))
    return 0.5 * y * (1.0 + t), t


def s5_out_fwd(y0, proj, dvec, glu_w, glu_b, *, name):
    L, C = y0.shape
    tr = _pick(L, (256, 128))

    def body(y_ref, u_ref, d_ref, w_ref, b_ref, o_ref):
        z, _ = _gelu(y_ref[...] + d_ref[...] * u_ref[...])
        zg = _dot(z, w_ref[...]) + b_ref[...]
        o_ref[...] = (z * _sigmoid(zg)).astype(o_ref.dtype)

    row = pl.BlockSpec((tr, C), lambda i: (i, 0))
    vec = pl.BlockSpec((1, C), lambda i: (0, 0))
    wsp = pl.BlockSpec((C, C), lambda i: (0, 0))
    return pl.pallas_call(body, grid=(L // tr,), in_specs=[row, row, vec, wsp, vec], out_specs=row,
                          out_shape=jax.ShapeDtypeStruct((L, C), BF16), name=name)(
        y0, proj, dvec, glu_w, glu_b)


def s5_out_bwd(y0, proj, dvec, glu_w, glu_b, dcat, *, name):
    L, C = y0.shape
    tr = _pick(L, (256, 128))

    def body(y_ref, u_ref, d_ref, w_ref, b_ref, do_ref, dy_ref, dud_ref, z_ref, dzg_ref, db_ref, dd_ref):
        u = u_ref[...]
        y = y_ref[...] + d_ref[...] * u
        z, t = _gelu(y)
        zg = _dot(z, w_ref[...]) + b_ref[...]
        s = _sigmoid(zg)
        do = do_ref[...]
        dzg = do * z * s * (1.0 - s)
        dz = do * s + _dot_nt(dzg, w_ref[...])
        c = math.sqrt(2.0 / math.pi)
        dgelu = 0.5 * (1.0 + t) + 0.5 * y * (1.0 - t * t) * c * (1.0 + 3.0 * 0.044715 * y * y)
        dy = dz * dgelu

        @pl.when(pl.program_id(0) == 0)
        def _():
            db_ref[...] = jnp.zeros_like(db_ref)
            dd_ref[...] = jnp.zeros_like(dd_ref)

        db_ref[...] += jnp.sum(dzg, axis=0, keepdims=True)
        dd_ref[...] += jnp.sum(dy * u, axis=0, keepdims=True)
        dy_ref[...] = dy
        dud_ref[...] = dy * d_ref[...]
        z_ref[...] = z.astype(BF16)
        dzg_ref[...] = dzg.astype(BF16)

    row = pl.BlockSpec((tr, C), lambda i: (i, 0))
    vec = pl.BlockSpec((1, C), lambda i: (0, 0))
    wsp = pl.BlockSpec((C, C), lambda i: (0, 0))
    return pl.pallas_call(body, grid=(L // tr,), in_specs=[row, row, vec, wsp, vec, row],
                          out_specs=[row, row, row, row, vec, vec],
                          out_shape=[jax.ShapeDtypeStruct((L, C), F32), jax.ShapeDtypeStruct((L, C), F32),
                                     jax.ShapeDtypeStruct((L, C), BF16), jax.ShapeDtypeStruct((L, C), BF16),
                                     jax.ShapeDtypeStruct((1, C), F32), jax.ShapeDtypeStruct((1, C), F32)],
                          compiler_params=_params(dimension_semantics=("arbitrary",)), name=name)(
        y0, proj, dvec, glu_w, glu_b, dcat)


def _hg_gates(xq, xf, lb, tri):
    C = xq.shape[0]
    sq = _sigmoid(xq)
    q = xq * sq
    sg = _sigmoid(xf)
    f = lb + (1.0 - lb) * sg
    kk = 1.0 - f
    b = _dot_f32(tri, jnp.log(f))
    bm = b[C // 2 - 1:C // 2, :]
    bl = b[C - 1:C, :]
    eb = jnp.exp(b)
    return dict(sq=sq, q=q, sg=sg, f=f, kk=kk, b=b, bm=bm, bl=bl, eb=eb, ebl=jnp.exp(bl),
                qb=q * eb, eqm=jnp.exp(b - bm), ekm=jnp.exp(bm - b), ekl=jnp.exp(bl - b))


def _tri(C, lower):
    r = lax.broadcasted_iota(jnp.int32, (C, C), 0)
    c = lax.broadcasted_iota(jnp.int32, (C, C), 1)
    return (r >= c) if lower else (c >= r)


def hgrn_fwd(proj, lb, norm_g, *, name):
    L = proj.shape[0]
    C, H, K = HG_CHUNK, HG_HEADS, HG_DIM
    HK = H * K
    nc = L // C

    def body(q_ref, f_ref, i_ref, g_ref, lb_ref, ng_ref, o_ref, sall_ref, st_ref):
        @pl.when(pl.program_id(0) == 0)
        def _():
            st_ref[...] = jnp.zeros_like(st_ref)

        mask = _tri(C, True)
        tri = mask.astype(F32)
        for h in range(H):
            sl = slice(h * K, (h + 1) * K)
            v = i_ref[:, sl]
            st = st_ref[h]
            sall_ref[h] = st
            gt = _hg_gates(q_ref[:, sl], f_ref[:, sl], lb_ref[:, sl], tri)
            qt = gt["q"] * gt["eqm"]
            kt = gt["kk"] * gt["ekm"]
            kh = gt["kk"] * gt["ekl"]
            att = jnp.where(mask, _dot_nt(qt, kt), 0.0)
            o = _dot(att, v) + _dot_nt(gt["qb"], st)
            st_ref[h] = st * gt["ebl"] + _dot_tn(v, kh)
            r = lax.rsqrt(jnp.mean(o * o, axis=-1, keepdims=True) + NORM_EPS)
            xg = g_ref[:, sl]
            o_ref[:, sl] = (o * r * ng_ref[:, sl] * (xg * _sigmoid(xg))).astype(o_ref.dtype)

    def blk(cb):
        return pl.BlockSpec((C, HK), lambda i: (i, cb))

    vec = pl.BlockSpec((1, HK), lambda i: (0, 0))
    return pl.pallas_call(
        body, grid=(nc,), in_specs=[blk(1), blk(2), blk(3), blk(4), vec, vec],
        out_specs=[pl.BlockSpec((C, HK), lambda i: (i, 0)), pl.BlockSpec((None, H, K, K), lambda i: (i, 0, 0, 0))],
        out_shape=[jax.ShapeDtypeStruct((L, HK), BF16), jax.ShapeDtypeStruct((nc, H, K, K), F32)],
        scratch_shapes=[pltpu.VMEM((H, K, K), F32)],
        compiler_params=_params(dimension_semantics=("arbitrary",)), name=name,
    )(proj, proj, proj, proj, lb, norm_g)


def hgrn_bwd(proj, lb, norm_g, sall, dcat, *, name):
    L = proj.shape[0]
    C, H, K = HG_CHUNK, HG_HEADS, HG_DIM
    HK = H * K
    nc = L // C

    def body(q_ref, f_ref, i_ref, g_ref, lb_ref, ng_ref, sall_ref, do_ref, dx_ref, dlb_ref, dng_ref, dst_ref):
        @pl.when(pl.program_id(0) == 0)
        def _():
            dst_ref[...] = jnp.zeros_like(dst_ref)
            dlb_ref[...] = jnp.zeros_like(dlb_ref)
            dng_ref[...] = jnp.zeros_like(dng_ref)

        mask = _tri(C, True)
        tri = mask.astype(F32)
        tri_t = _tri(C, False).astype(F32)
        rowi = lax.broadcasted_iota(jnp.int32, (C, K), 0)
        for h in range(H):
            sl = slice(h * K, (h + 1) * K)
            xq, xf, v, xg = q_ref[:, sl], f_ref[:, sl], i_ref[:, sl], g_ref[:, sl]
            lb_h, ng = lb_ref[:, sl], ng_ref[:, sl]
            st = sall_ref[h]
            dst = dst_ref[h]
            gt = _hg_gates(xq, xf, lb_h, tri)
            q, kk, qb = gt["q"], gt["kk"], gt["qb"]
            qt = q * gt["eqm"]
            kt = kk * gt["ekm"]
            kh = kk * gt["ekl"]
            att = jnp.where(mask, _dot_nt(qt, kt), 0.0)
            o = _dot(att, v) + _dot_nt(qb, st)
            r = lax.rsqrt(jnp.mean(o * o, axis=-1, keepdims=True) + NORM_EPS)
            oh = o * r
            sgg = _sigmoid(xg)
            silu_g = xg * sgg
            d_ob = do_ref[:, sl]
            d_on = d_ob * silu_g
            dxg = d_ob * (oh * ng) * (sgg * (1.0 + xg * (1.0 - sgg)))
            dng_ref[:, sl] += jnp.sum(d_on * oh, axis=0, keepdims=True)
            doh = d_on * ng
            do = r * (doh - oh * jnp.mean(doh * oh, axis=-1, keepdims=True))
            datt = jnp.where(mask, _dot_nt(do, v), 0.0)
            dv = _dot_tn(att, do) + _dot_nt(kh, dst)
            d_qb = _dot_f32(do, st)
            d_qt = _dot_f32(datt, kt)
            d_kt = _dot_f32_tn(datt, qt)
            d_kh = _dot_f32(v, dst)
            d_bl = jnp.sum(dst * st, axis=0, keepdims=True) * gt["ebl"] + jnp.sum(d_kh * kh, axis=0, keepdims=True)
            dst_ref[h] = dst * gt["ebl"] + _dot_tn(do, qb)
            dq = d_qt * gt["eqm"] + d_qb * gt["eb"]
            db = d_qt * qt + d_qb * qb - d_kt * kt - d_kh * kh
            db = db + jnp.where(rowi == C - 1, d_bl, 0.0)
            dkk = d_kt * gt["ekm"] + d_kh * gt["ekl"]
            dlg = _dot_f32(tri_t, db)
            df = dlg / gt["f"] - dkk
            sg = gt["sg"]
            dxf = df * (1.0 - lb_h) * sg * (1.0 - sg)
            dlb_ref[:, sl] += jnp.sum(df * (1.0 - sg), axis=0, keepdims=True)
            sq = gt["sq"]
            dxq = dq * (sq * (1.0 + xq * (1.0 - sq)))
            dx_ref[:, h * K:(h + 1) * K] = dxq.astype(dx_ref.dtype)
            dx_ref[:, HK + h * K:HK + (h + 1) * K] = dxf.astype(dx_ref.dtype)
            dx_ref[:, 2 * HK + h * K:2 * HK + (h + 1) * K] = dv.astype(dx_ref.dtype)
            dx_ref[:, 3 * HK + h * K:3 * HK + (h + 1) * K] = dxg.astype(dx_ref.dtype)

    def blk(cb):
        return pl.BlockSpec((C, HK), lambda i: (nc - 1 - i, cb))

    vec = pl.BlockSpec((1, HK), lambda i: (0, 0))
    return pl.pallas_call(
        body, grid=(nc,),
        in_specs=[blk(1), blk(2), blk(3), blk(4), vec, vec,
                  pl.BlockSpec((None, H, K, K), lambda i: (nc - 1 - i, 0, 0, 0)), blk(1)],
        out_specs=[pl.BlockSpec((C, 4 * HK), lambda i: (nc - 1 - i, 0)), vec, vec],
        out_shape=[jax.ShapeDtypeStruct((L, 4 * HK), BF16), jax.ShapeDtypeStruct((1, HK), F32),
                   jax.ShapeDtypeStruct((1, HK), F32)],
        scratch_shapes=[pltpu.VMEM((H, K, K), F32)],
        compiler_params=_params(dimension_semantics=("arbitrary",)), name=name,
    )(proj, proj, proj, proj, lb, norm_g, sall, dcat)


def _shift_down(x, k, row):
    return jnp.where(row >= k, pltpu.roll(x, k, 0), 0.0)


def _shift_up(x, k, row):
    n = x.shape[0]
    return jnp.where(row < n - k, pltpu.roll(x, n - k, 0), 0.0)


def convgate_fwd(hu, conv_w, conv_b, *, name):
    L, C2 = hu.shape
    C = C2 // 2
    tc = _pick(C, (256, 128))
    nb = C // tc

    def body(a_ref, b_ref, wa_ref, wb_ref, ba_ref, bb_ref, o_ref):
        row = lax.broadcasted_iota(jnp.int32, (L, tc), 0)

        def conv(x, w, bias):
            return w[2:3, :] * x + w[1:2, :] * _shift_down(x, 1, row) + w[0:1, :] * _shift_down(x, 2, row) + bias

        ca = conv(a_ref[...], wa_ref[...], ba_ref[...])
        cb = conv(b_ref[...], wb_ref[...], bb_ref[...])
        o_ref[...] = (ca * _sigmoid(ca) * cb).astype(o_ref.dtype)

    def col(off, rows):
        return pl.BlockSpec((rows, tc), lambda j: (0, j + off))

    return pl.pallas_call(
        body, grid=(nb,), in_specs=[col(0, L), col(nb, L), col(0, 3), col(nb, 3), col(0, 1), col(nb, 1)],
        out_specs=col(0, L), out_shape=jax.ShapeDtypeStruct((L, C), BF16),
        compiler_params=_params(dimension_semantics=("parallel",)), name=name,
    )(hu, hu, conv_w, conv_w, conv_b, conv_b)


def convgate_bwd(hu, conv_w, conv_b, dact, *, name):
    L, C2 = hu.shape
    C = C2 // 2
    tc = _pick(C, (256, 128))
    nb = C // tc

    def body(a_ref, b_ref, wa_ref, wb_ref, ba_ref, bb_ref, d_ref, dxa_ref, dxb_ref, dwa_ref, dwb_ref, dba_ref, dbb_ref):
        row = lax.broadcasted_iota(jnp.int32, (L, tc), 0)

        def conv(x, w, bias):
            x1 = _shift_down(x, 1, row)
            x2 = _shift_down(x, 2, row)
            return w[2:3, :] * x + w[1:2, :] * x1 + w[0:1, :] * x2 + bias, x1, x2

        xa, xb = a_ref[...], b_ref[...]
        wa, wb = wa_ref[...], wb_ref[...]
        ca, xa1, xa2 = conv(xa, wa, ba_ref[...])
        cb, xb1, xb2 = conv(xb, wb, bb_ref[...])
        d = d_ref[...]
        sa = _sigmoid(ca)
        dca = d * cb * (sa * (1.0 + ca * (1.0 - sa)))
        dcb = d * (ca * sa)

        def back(dc, w, x, x1, x2, dx_ref, dw_ref, db_ref):
            dx = w[2:3, :] * dc + w[1:2, :] * _shift_up(dc, 1, row) + w[0:1, :] * _shift_up(dc, 2, row)
            dx_ref[...] = dx.astype(dx_ref.dtype)
            dw_ref[...] = jnp.concatenate([jnp.sum(dc * x2, axis=0, keepdims=True),
                                           jnp.sum(dc * x1, axis=0, keepdims=True),
                                           jnp.sum(dc * x, axis=0, keepdims=True)], axis=0)
            db_ref[...] = jnp.sum(dc, axis=0, keepdims=True)

        back(dca, wa, xa, xa1, xa2, dxa_ref, dwa_ref, dba_ref)
        back(dcb, wb, xb, xb1, xb2, dxb_ref, dwb_ref, dbb_ref)

    def col(off, rows):
        return pl.BlockSpec((rows, tc), lambda j: (0, j + off))

    outs = pl.pallas_call(
        body, grid=(nb,),
        in_specs=[col(0, L), col(nb, L), col(0, 3), col(nb, 3), col(0, 1), col(nb, 1), col(0, L)],
        out_specs=[col(0, L), col(0, L), col(0, 3), col(0, 3), col(0, 1), col(0, 1)],
        out_shape=[jax.ShapeDtypeStruct((L, C), BF16)] * 2 + [jax.ShapeDtypeStruct((3, C), F32)] * 2
        + [jax.ShapeDtypeStruct((1, C), F32)] * 2,
        compiler_params=_params(dimension_semantics=("parallel",)), name=name,
    )(hu, hu, conv_w, conv_w, conv_b, conv_b, dact)
    dxa, dxb, dwa, dwb, dba, dbb = outs
    return (jnp.concatenate([dxa, dxb], axis=1), jnp.concatenate([dwa, dwb], axis=1),
            jnp.concatenate([dba, dbb], axis=1))


def rope_tables(positions):
    half = ROT_DIM // 2
    inv_freq = ROPE_THETA ** (-jnp.arange(half, dtype=F32) * 2.0 / ROT_DIM)
    ang = positions.astype(F32)[:, None] * inv_freq
    cos, sin = jnp.cos(ang), jnp.sin(ang)
    L = positions.shape[0]
    one = jnp.ones((L, ATT_E - ROT_DIM), F32)
    zero = jnp.zeros((L, ATT_E - ROT_DIM), F32)
    zh = jnp.zeros((L, half), F32)
    tc = jnp.concatenate([cos, cos, one], axis=1)
    ts1 = jnp.concatenate([zh, sin, zero], axis=1)
    ts2 = jnp.concatenate([-sin, zh, zero], axis=1)
    return tuple(jnp.concatenate([t, t], axis=1) for t in (tc, ts1, ts2))


def rope_fwd(qkv, tabs, *, name):
    L = qkv.shape[0]
    W = 512
    tr = _pick(L, (256, 128))
    nq = 1536 // W
    scale = ATT_E ** -0.5

    def body(x_ref, c_ref, s1_ref, s2_ref, *o_refs):
        c = jnp.concatenate([c_ref[...]] * 4, axis=1)
        s1 = jnp.concatenate([s1_ref[...]] * 4, axis=1)
        s2 = jnp.concatenate([s2_ref[...]] * 4, axis=1)
        for j, o_ref in enumerate(o_refs):
            x = x_ref[:, j * W:(j + 1) * W]
            if j < 2 * nq:
                x = x * c + pltpu.roll(x, 8, 1) * s1 + pltpu.roll(x, W - 8, 1) * s2
            if j < nq:
                x = x * scale
            o_ref[...] = x.astype(o_ref.dtype)

    slab = pl.BlockSpec((tr, W), lambda i: (i, 0))
    tab = pl.BlockSpec((tr, 128), lambda i: (i, 0))
    return pl.pallas_call(body, grid=(L // tr,), in_specs=[pl.BlockSpec((tr, 3 * nq * W), lambda i: (i, 0)), tab, tab, tab],
                          out_specs=[slab] * (3 * nq), out_shape=[jax.ShapeDtypeStruct((L, W), F32)] * (3 * nq),
                          compiler_params=_params(dimension_semantics=("parallel",)), name=name)(qkv, *tabs)


def rope_bwd(slabs, tabs, *, name):
    L, W = slabs[0].shape
    tr = _pick(L, (256, 128))
    nq = len(slabs) // 3
    scale = ATT_E ** -0.5

    def body(*refs):
        d_refs, (c_ref, s1_ref, s2_ref, o_ref) = refs[:3 * nq], refs[3 * nq:]
        c = jnp.concatenate([c_ref[...]] * 4, axis=1)
        s1 = jnp.concatenate([s1_ref[...]] * 4, axis=1)
        s2 = jnp.concatenate([s2_ref[...]] * 4, axis=1)
        for j, d_ref in enumerate(d_refs):
            dy = d_ref[...]
            if j < 2 * nq:
                dy = dy * c + pltpu.roll(dy * s1, W - 8, 1) + pltpu.roll(dy * s2, 8, 1)
            if j < nq:
                dy = dy * scale
            o_ref[:, j * W:(j + 1) * W] = dy.astype(o_ref.dtype)

    slab = pl.BlockSpec((tr, W), lambda i: (i, 0))
    tab = pl.BlockSpec((tr, 128), lambda i: (i, 0))
    return pl.pallas_call(body, grid=(L // tr,), in_specs=[slab] * (3 * nq) + [tab, tab, tab],
                          out_specs=pl.BlockSpec((tr, 3 * nq * W), lambda i: (i, 0)),
                          out_shape=jax.ShapeDtypeStruct((L, 3 * nq * W), BF16),
                          compiler_params=_params(dimension_semantics=("parallel",)), name=name)(*slabs, *tabs)


def _att_masks(has_prev):
    qi = lax.broadcasted_iota(jnp.int32, (ATT_BLOCK, ATT_BLOCK), 0)
    kj = lax.broadcasted_iota(jnp.int32, (ATT_BLOCK, ATT_BLOCK), 1)
    return qi >= kj, (kj >= qi) & has_prev


ATT_COLS = 128


def _att_rows(j, d, nb):
    B = ATT_BLOCK
    r, n = j // nb, j % nb
    start = r + d * B * n
    has_prev = n > 0
    pstart = jnp.where(has_prev, start - d * B, start)
    if d == 1:
        return pl.ds(pl.multiple_of(start, B), B), pl.ds(pl.multiple_of(pstart, B), B), has_prev
    return pl.ds(start, B, stride=d), pl.ds(pstart, B, stride=d), has_prev


def attn_fwd(q, k, v, d, *, name):
    L, W = q.shape
    B, E = ATT_BLOCK, ATT_E
    nblk = L // B
    nb = nblk // d

    def body(q_ref, k_ref, v_ref, o_ref, l_ref):
        def step(j, carry):
            cur, prv, has_prev = _att_rows(j, d, nb)
            mc, mp = _att_masks(has_prev)
            qb, kc, kp, vc, vp = q_ref[cur, :], k_ref[cur, :], k_ref[prv, :], v_ref[cur, :], v_ref[prv, :]
            outs, lses = [], []
            for h in range(ATT_COLS // E):
                sl = slice(h * E, (h + 1) * E)
                sc = jnp.where(mc, _dot_nt(qb[:, sl], kc[:, sl]), NEG_BIG)
                sp = jnp.where(mp, _dot_nt(qb[:, sl], kp[:, sl]), NEG_BIG)
                m = jnp.maximum(jnp.max(sc, axis=-1, keepdims=True), jnp.max(sp, axis=-1, keepdims=True))
                pc = jnp.exp(sc - m)
                pp = jnp.exp(sp - m)
                den = jnp.sum(pc, axis=-1, keepdims=True) + jnp.sum(pp, axis=-1, keepdims=True)
                outs.append((_dot(pc, vc[:, sl]) + _dot(pp, vp[:, sl])) / den)
                lses.append(jnp.broadcast_to(m + jnp.log(den), (B, E)))
            o_ref[cur, :] = jnp.concatenate(outs, axis=1)
            l_ref[cur, :] = jnp.concatenate(lses, axis=1)
            return carry

        lax.fori_loop(0, nblk, step, 0, unroll=2)

    col = pl.BlockSpec((L, ATT_COLS), lambda c: (0, c))
    return pl.pallas_call(body, grid=(W // ATT_COLS,), in_specs=[col] * 3, out_specs=[col] * 2,
                          out_shape=[jax.ShapeDtypeStruct((L, W), F32)] * 2,
                          compiler_params=_params(dimension_semantics=("parallel",)), name=name)(q, k, v)


def attn_bwd(q, k, v, lse, do, dl, d, *, name):
    L, W = q.shape
    B, E = ATT_BLOCK, ATT_E
    nblk = L // B
    nb = nblk // d

    def body(q_ref, k_ref, v_ref, l_ref, do_ref, dl_ref, dq_ref, dk_ref, dv_ref):
        dk_ref[...] = jnp.zeros_like(dk_ref)
        dv_ref[...] = jnp.zeros_like(dv_ref)

        def step(j, carry):
            cur, prv, has_prev = _att_rows(j, d, nb)
            mc, mp = _att_masks(has_prev)
            qb, kc, kp, vc, vp = q_ref[cur, :], k_ref[cur, :], k_ref[prv, :], v_ref[cur, :], v_ref[prv, :]
            lb, dob, dlb = l_ref[cur, :], do_ref[cur, :], dl_ref[cur, :]
            dqs, dkc, dkp, dvc, dvp = [], [], [], [], []
            for h in range(ATT_COLS // E):
                sl = slice(h * E, (h + 1) * E)
                qh, doh = qb[:, sl], dob[:, sl]
                lse_h, dl_h = lb[:, h * E:h * E + 1], dlb[:, h * E:h * E + 1]
                pc = jnp.where(mc, jnp.exp(_dot_nt(qh, kc[:, sl]) - lse_h), 0.0)
                pp = jnp.where(mp, jnp.exp(_dot_nt(qh, kp[:, sl]) - lse_h), 0.0)
                dsc = pc * (_dot_nt(doh, vc[:, sl]) - dl_h)
                dsp = pp * (_dot_nt(doh, vp[:, sl]) - dl_h)
                dqs.append(_dot(dsc, kc[:, sl]) + _dot(dsp, kp[:, sl]))
                dkc.append(_dot_tn(dsc, qh))
                dkp.append(_dot_tn(dsp, qh))
                dvc.append(_dot_tn(pc, doh))
                dvp.append(_dot_tn(pp, doh))
            dq_ref[cur, :] = jnp.concatenate(dqs, axis=1)
            dk_ref[cur, :] = dk_ref[cur, :] + jnp.concatenate(dkc, axis=1)
            dv_ref[cur, :] = dv_ref[cur, :] + jnp.concatenate(dvc, axis=1)
            dk_ref[prv, :] = dk_ref[prv, :] + jnp.concatenate(dkp, axis=1)
            dv_ref[prv, :] = dv_ref[prv, :] + jnp.concatenate(dvp, axis=1)
            return carry

        lax.fori_loop(0, nblk, step, 0)

    col = pl.BlockSpec((L, ATT_COLS), lambda c: (0, c))
    return pl.pallas_call(body, grid=(W // ATT_COLS,), in_specs=[col] * 6, out_specs=[col] * 3,
                          out_shape=[jax.ShapeDtypeStruct((L, W), F32)] * 3,
                          compiler_params=_params(dimension_semantics=("parallel",)), name=name)(q, k, v, lse, do, dl)


def _merge_alpha(l_refs):
    ls = [r[...] for r in l_refs]
    m = jnp.maximum(jnp.maximum(ls[0], ls[1]), ls[2])
    es = [jnp.exp(l - m) for l in ls]
    den = es[0] + es[1] + es[2]
    return [e / den for e in es]


def merge_fwd(os_, ls_, *, name):
    L, W = os_[0].shape
    tr = _pick(L, (256, 128))

    def body(o0, o1, o2, l0, l1, l2, out_ref):
        al = _merge_alpha((l0, l1, l2))
        out_ref[...] = (al[0] * o0[...] + al[1] * o1[...] + al[2] * o2[...]).astype(out_ref.dtype)

    row = pl.BlockSpec((tr, W), lambda i: (i, 0))
    return pl.pallas_call(body, grid=(L // tr,), in_specs=[row] * 6, out_specs=row,
                          out_shape=jax.ShapeDtypeStruct((L, W), BF16), name=name)(*os_, *ls_)


def merge_bwd(os_, ls_, do, *, name):
    L, W = do.shape
    tr = _pick(L, (256, 128))

    def body(o0, o1, o2, l0, l1, l2, do_ref, d0, d1, d2, e0, e1, e2):
        al = _merge_alpha((l0, l1, l2))
        dov = do_ref[...]
        r = lax.broadcasted_iota(jnp.int32, (W, W), 0) // ATT_E
        c = lax.broadcasted_iota(jnp.int32, (W, W), 1) // ATT_E
        ones_blk = (r == c).astype(F32)
        t = jnp.zeros_like(dov)
        for a, o in zip(al, (o0, o1, o2)):
            t = t + a * _dot_f32(dov * o[...], ones_blk)
        for a, d_ref, e_ref in zip(al, (d0, d1, d2), (e0, e1, e2)):
            d_ref[...] = a * dov
            e_ref[...] = a * t

    row = pl.BlockSpec((tr, W), lambda i: (i, 0))
    return pl.pallas_call(body, grid=(L // tr,), in_specs=[row] * 7, out_specs=[row] * 6,
                          out_shape=[jax.ShapeDtypeStruct((L, W), F32)] * 6, name=name)(*os_, *ls_, do)


def _me_and_peers():
    x, y, c = lax.axis_index("x"), lax.axis_index("y"), lax.axis_index("c")
    peers = []
    for k in range(1, N_DEV):
        px = 1 - x if k & 4 else x
        py = 1 - y if k & 2 else y
        pc = 1 - c if k & 1 else c
        peers.append((px, py, pc))
    return (x, y, c), peers


def _index(dev):
    return 4 * dev[0] + 2 * dev[1] + dev[2]


def _hbm(a):
    return pltpu.with_memory_space_constraint(a, pltpu.HBM)


HBM_SPEC = pl.BlockSpec(memory_space=pltpu.HBM)
SEM_SPEC = pl.BlockSpec(memory_space=pltpu.SEMAPHORE)
DATAFLOW = pltpu.SideEffectType.DATAFLOW_SIDE_EFFECTING


def _remote(src_ref, land_ref, slotted, me, peer, src_is_mine, send_sem, recv_sem, k):
    sender, receiver = (me, peer) if src_is_mine else (peer, me)
    src = src_ref.at[_index(receiver)] if slotted else src_ref
    return pltpu.make_async_remote_copy(src_ref=src, dst_ref=land_ref.at[_index(sender)], send_sem=send_sem.at[k],
                                        recv_sem=recv_sem.at[k], device_id=peer, device_id_type=MESH_ID)


def copies_start(arrays, slotted, *, name):
    n = len(arrays)
    lands = [lax.empty(a.shape if slotted else (N_DEV,) + a.shape, a.dtype) for a in arrays]

    def body(*refs):
        x_refs, land_refs = refs[:n], refs[n:2 * n]
        send, recv = refs[2 * n:3 * n], refs[3 * n:4 * n]
        token = refs[-1]
        me, peers = _me_and_peers()
        for w in range(n):
            for k, peer in enumerate(peers):
                _remote(x_refs[w], land_refs[w], slotted, me, peer, True, send[w], recv[w], k).start()
            if not slotted:
                pltpu.make_async_copy(x_refs[w], land_refs[w].at[_index(me)], recv[w].at[N_DEV - 1]).start()
        token[...] = jnp.zeros_like(token)

    sem = pltpu.SemaphoreType.DMA((N_DEV,))
    out_shape = ([sem] * (2 * n) + [pltpu.HBM(a.shape, a.dtype) for a in arrays]
                 + [pltpu.HBM(l.shape, l.dtype) for l in lands] + [jax.ShapeDtypeStruct((8, 128), F32)])
    outs = pl.pallas_call(
        body, name=name, out_shape=out_shape, in_specs=[HBM_SPEC] * (2 * n),
        out_specs=[SEM_SPEC] * (2 * n) + [HBM_SPEC] * (2 * n) + [pl.BlockSpec(memory_space=pltpu.VMEM)],
        input_output_aliases={i: 2 * n + i for i in range(2 * n)},
        compiler_params=pltpu.CompilerParams(has_side_effects=DATAFLOW),
    )(*[_hbm(a) for a in arrays], *[_hbm(l) for l in lands])
    handles = [(outs[w], outs[n + w], outs[2 * n + w], outs[3 * n + w]) for w in range(n)]
    return handles, outs[-1]


def copies_wait(handle, slotted, after, *, name):
    send_sem, recv_sem, x_thru, land_thru = handle

    def body(x_ref, land_ref, send_ref, recv_ref, after_ref, x_out, land_out):
        me, peers = _me_and_peers()
        for k, peer in enumerate(peers):
            _remote(x_ref, land_ref, slotted, me, peer, True, send_ref, recv_ref, k).wait_send()
        for k, peer in enumerate(peers):
            _remote(x_ref, land_ref, slotted, me, peer, False, send_ref, recv_ref, k).wait_recv()
        if not slotted:
            pltpu.make_async_copy(x_ref, land_ref.at[_index(me)], recv_ref.at[N_DEV - 1]).wait()

    return pl.pallas_call(
        body, name=name, out_shape=(pltpu.HBM(x_thru.shape, x_thru.dtype), pltpu.HBM(land_thru.shape, land_thru.dtype)),
        in_specs=(HBM_SPEC, HBM_SPEC, SEM_SPEC, SEM_SPEC, pl.BlockSpec(memory_space=pl.ANY)),
        out_specs=(HBM_SPEC, HBM_SPEC), input_output_aliases={0: 0, 1: 1},
        compiler_params=pltpu.CompilerParams(has_side_effects=DATAFLOW),
    )(x_thru, land_thru, send_sem, recv_sem, after)


def cast_bf16(x, *, ncols=None, name):
    R = x.shape[0]
    C = ncols or x.shape[1]
    tr = _pick(R, (512, 352, 256, 128, 64))

    def body(x_ref, o_ref):
        o_ref[...] = x_ref[...].astype(BF16)

    row = pl.BlockSpec((tr, C), lambda i: (i, 0))
    return pl.pallas_call(body, grid=(R // tr,), in_specs=[row], out_specs=row,
                          out_shape=jax.ShapeDtypeStruct((R, C), BF16), name=name)(x)


def cast_bf16_layer(x3, layer, *, name):
    _, R, C = x3.shape
    tr = _pick(R, (512, 352, 256, 128, 64))

    def body(x_ref, o_ref):
        o_ref[...] = x_ref[...].astype(BF16)

    return pl.pallas_call(body, grid=(R // tr,), in_specs=[pl.BlockSpec((None, tr, C), lambda i: (layer, i, 0))],
                          out_specs=pl.BlockSpec((tr, C), lambda i: (i, 0)),
                          out_shape=jax.ShapeDtypeStruct((R, C), BF16), name=name)(x3)


def _blockdiag_call(b, build, G, r, c, name):
    def body_build(b_ref, o_ref):
        o_ref[...] = jnp.zeros_like(o_ref)
        for g in range(G):
            o_ref[g * r:(g + 1) * r, g * c:(g + 1) * c] = b_ref[g]

    def body_extract(d_ref, o_ref):
        for g in range(G):
            o_ref[g] = d_ref[g * r:(g + 1) * r, g * c:(g + 1) * c]

    out = jax.ShapeDtypeStruct((G * r, G * c) if build else (G, r, c), F32)
    return pl.pallas_call(body_build if build else body_extract, out_shape=out, name=name)(b)


def make_blockdiag(G, r, c, name):
    @jax.custom_vjp
    def blockdiag(b):
        return _blockdiag_call(b, True, G, r, c, name + "_build")

    def fwd(b):
        return blockdiag(b), None

    def bwd(_, g):
        return (_blockdiag_call(g, False, G, r, c, name + "_extract"),)

    blockdiag.defvjp(fwd, bwd)
    return blockdiag


def _my_index():
    return 4 * lax.axis_index("x") + 2 * lax.axis_index("y") + lax.axis_index("c")


def cols_from_shards(g, *, name):
    _, K, n = g.shape
    tk = _pick(K, (256, 128))

    def body(g_ref, o_ref):
        for i in range(N_DEV):
            o_ref[:, i * n:(i + 1) * n] = g_ref[i]

    return pl.pallas_call(body, grid=(K // tk,), in_specs=[pl.BlockSpec((N_DEV, tk, n), lambda i: (0, i, 0))],
                          out_specs=pl.BlockSpec((tk, N_DEV * n), lambda i: (i, 0)),
                          out_shape=jax.ShapeDtypeStruct((K, N_DEV * n), g.dtype), name=name)(g)


def shards_from_cols(w, *, name):
    K, N = w.shape
    n = N // N_DEV
    tk = _pick(K, (256, 128))

    def body(w_ref, o_ref):
        for i in range(N_DEV):
            o_ref[i] = w_ref[:, i * n:(i + 1) * n].astype(o_ref.dtype)

    return pl.pallas_call(body, grid=(K // tk,), in_specs=[pl.BlockSpec((tk, N), lambda i: (i, 0))],
                          out_specs=pl.BlockSpec((N_DEV, tk, n), lambda i: (0, i, 0)),
                          out_shape=jax.ShapeDtypeStruct((N_DEV, K, n), BF16), name=name)(w)


def _adamw(w, g, m, v):
    m = ADAM_B1 * m + (1.0 - ADAM_B1) * g
    v = ADAM_B2 * v + (1.0 - ADAM_B2) * (g * g)
    m_hat = m / (1.0 - ADAM_B1 ** ADAM_STEP)
    v_hat = v / (1.0 - ADAM_B2 ** ADAM_STEP)
    delta = -ADAM_LR * (m_hat / (jnp.sqrt(v_hat) + ADAM_EPS) + ADAM_WD * w)
    return delta, m, v


def reduce_adamw(recv, own, own_slotted, me, w, m, v, *, layer=0, n_layers=1, into=None, name):
    _, R, C = recv.shape
    tr = _pick(R, (352, 320, 288, 256, 128, 64, 32, 16, 8))
    off = layer * (R // tr)

    def body(me_ref, r_ref, own_ref, w_ref, m_ref, v_ref, *rest):
        g_ref, d_ref, nm_ref, nv_ref = rest[-4:]
        mine = me_ref[0]
        g = None
        for i in range(N_DEV):
            part = jnp.where(mine == i, own_ref[...], r_ref[i]).astype(F32)
            g = part if g is None else g + part
        delta, nm, nv = _adamw(w_ref[...], g, m_ref[...], v_ref[...])
        g_ref[...] = g
        d_ref[...] = delta
        nm_ref[...] = nm
        nv_ref[...] = nv

    row = pl.BlockSpec((tr, C), lambda i, me_ref: (i + off, 0))
    own_spec = (pl.BlockSpec((None, tr, C), lambda i, me_ref: (me_ref[0], i, 0)) if own_slotted
                else pl.BlockSpec((tr, C), lambda i, me_ref: (i, 0)))
    rest = [] if into is None else list(into)
    grid_spec = pltpu.PrefetchScalarGridSpec(
        num_scalar_prefetch=1, grid=(R // tr,),
        in_specs=[pl.BlockSpec((N_DEV, tr, C), lambda i, me_ref: (0, i, 0)), own_spec, row, row, row]
        + [pl.BlockSpec(memory_space=pl.ANY)] * len(rest),
        out_specs=[row] * 4)
    return pl.pallas_call(body, grid_spec=grid_spec, out_shape=[jax.ShapeDtypeStruct((n_layers * R, C), F32)] * 4,
                          input_output_aliases={6 + k: k for k in range(len(rest))},
                          compiler_params=_params(dimension_semantics=("parallel",)), name=name)(
        me.reshape(1).astype(jnp.int32), recv, own, w, m, v, *rest)


def _s5_prepare(A_re, A_im, log_dt, B_re, B_im, C_re, C_im):
    G, P, Cg = S5_GROUPS, S5_STATE, S5_GROUP
    dt = jnp.exp(log_dt)[:, None]
    mag = jnp.exp(A_re * dt)
    ab_re = mag * jnp.cos(A_im * dt)
    ab_im = mag * jnp.sin(A_im * dt)
    den = A_re * A_re + A_im * A_im
    nr, ni = ab_re - 1.0, ab_im
    c_re = (nr * A_re + ni * A_im) / den
    c_im = (ni * A_re - nr * A_im) / den
    Bb_re = c_re[..., None] * B_re - c_im[..., None] * B_im
    Bb_im = c_re[..., None] * B_im + c_im[..., None] * B_re
    def dense_in(b, name):
        return make_blockdiag(G, Cg, P, name)(b.transpose(0, 2, 1))

    def dense_out(c, name):
        return make_blockdiag(G, P, Cg, name)(c.transpose(0, 2, 1))

    return (ab_re.reshape(1, G * P), ab_im.reshape(1, G * P), dense_in(Bb_re, "s5_wb_re"), dense_in(Bb_im, "s5_wb_im"),
            dense_out(C_re, "s5_wc_re"), dense_out(-C_im, "s5_wc_im"))


def _lower_bound(gamma):
    return jnp.cumsum(jax.nn.softmax(gamma, axis=0), axis=0)[0:1]


def _ffn_fwd(h, g_norm, get_w_in, conv_w, conv_b, get_w_out, tag):
    hn = rms_fwd(h, g_norm, name=tag + "_rms")
    w_in = get_w_in(hn)
    hu = mm(hn, w_in, tb=True, name=tag + "_in")
    act = convgate_fwd(hu, conv_w, conv_b, name=tag + "_gate")
    w_out = get_w_out(act)
    h_out = mm(act, w_out, res=h, name=tag + "_out")
    return h_out, (hn, hu, act), w_in, w_out


def _ffn_bwd(h, g_norm, w_in, conv_w, conv_b, w_out, saved, dh, tag, send_dw_in, send_dw_out):
    hn, hu, act = saved
    sent = send_dw_out(mm(act, dh, ta=True, out_dtype=BF16, name=tag + "_dwout"))
    dact = mm(dh, w_out, tb=True, dep=sent, name=tag + "_dact")
    dhu, dconv_w, dconv_b = convgate_bwd(hu, conv_w, conv_b, dact, name=tag + "_dgate")
    sent = send_dw_in(mm(dhu, hn, ta=True, out_dtype=BF16, name=tag + "_dwin"))
    dhn = mm(dhu, w_in, dep=sent, name=tag + "_dhn")
    dh_in, dg = rms_bwd(h, g_norm, dhn, dh, name=tag + "_drms")
    return dh_in, dg, dconv_w, dconv_b


def kernel(x, positions, norm_mix, norm_ffn, norm_final, mix_w_in, mix_w_out, s5_A_re, s5_A_im, s5_log_dt, s5_B_re, s5_B_im, s5_C_re, s5_C_im, s5_D, s5_glu_w, s5_glu_b, hgrn_gamma, hgrn_norm, att_w_qkv, att_w_o, ffn_w_in, ffn_conv_w, ffn_conv_b, ffn_w_out, loss_target, m_norm_mix, m_norm_ffn, m_norm_final, m_mix_w_in, m_mix_w_out, m_s5_A_re, m_s5_A_im, m_s5_log_dt, m_s5_B_re, m_s5_B_im, m_s5_C_re, m_s5_C_im, m_s5_D, m_s5_glu_w, m_s5_glu_b, m_hgrn_gamma, m_hgrn_norm, m_att_w_qkv, m_att_w_o, m_ffn_w_in, m_ffn_conv_w, m_ffn_conv_b, m_ffn_w_out, v_norm_mix, v_norm_ffn, v_norm_final, v_mix_w_in, v_mix_w_out, v_s5_A_re, v_s5_A_im, v_s5_log_dt, v_s5_B_re, v_s5_B_im, v_s5_C_re, v_s5_C_im, v_s5_D, v_s5_glu_w, v_s5_glu_b, v_hgrn_gamma, v_hgrn_norm, v_att_w_qkv, v_att_w_o, v_ffn_w_in, v_ffn_conv_w, v_ffn_conv_b, v_ffn_w_out):
    W = dict(norm_mix=norm_mix, norm_ffn=norm_ffn, norm_final=norm_final, mix_w_in=mix_w_in, mix_w_out=mix_w_out,
             s5_A_re=s5_A_re, s5_A_im=s5_A_im, s5_log_dt=s5_log_dt, s5_B_re=s5_B_re, s5_B_im=s5_B_im,
             s5_C_re=s5_C_re, s5_C_im=s5_C_im, s5_D=s5_D, s5_glu_w=s5_glu_w, s5_glu_b=s5_glu_b,
             hgrn_gamma=hgrn_gamma, hgrn_norm=hgrn_norm, att_w_qkv=att_w_qkv, att_w_o=att_w_o, ffn_w_in=ffn_w_in,
             ffn_conv_w=ffn_conv_w, ffn_conv_b=ffn_conv_b, ffn_w_out=ffn_w_out)
    M = dict(norm_mix=m_norm_mix, norm_ffn=m_norm_ffn, norm_final=m_norm_final, mix_w_in=m_mix_w_in,
             mix_w_out=m_mix_w_out, s5_A_re=m_s5_A_re, s5_A_im=m_s5_A_im, s5_log_dt=m_s5_log_dt, s5_B_re=m_s5_B_re,
             s5_B_im=m_s5_B_im, s5_C_re=m_s5_C_re, s5_C_im=m_s5_C_im, s5_D=m_s5_D, s5_glu_w=m_s5_glu_w,
             s5_glu_b=m_s5_glu_b, hgrn_gamma=m_hgrn_gamma, hgrn_norm=m_hgrn_norm, att_w_qkv=m_att_w_qkv,
             att_w_o=m_att_w_o, ffn_w_in=m_ffn_w_in, ffn_conv_w=m_ffn_conv_w, ffn_conv_b=m_ffn_conv_b,
             ffn_w_out=m_ffn_w_out)
    V = dict(norm_mix=v_norm_mix, norm_ffn=v_norm_ffn, norm_final=v_norm_final, mix_w_in=v_mix_w_in,
             mix_w_out=v_mix_w_out, s5_A_re=v_s5_A_re, s5_A_im=v_s5_A_im, s5_log_dt=v_s5_log_dt, s5_B_re=v_s5_B_re,
             s5_B_im=v_s5_B_im, s5_C_re=v_s5_C_re, s5_C_im=v_s5_C_im, s5_D=v_s5_D, s5_glu_w=v_s5_glu_w,
             s5_glu_b=v_s5_glu_b, hgrn_gamma=v_hgrn_gamma, hgrn_norm=v_hgrn_norm, att_w_qkv=v_att_w_qkv,
             att_w_o=v_att_w_o, ffn_w_in=v_ffn_w_in, ffn_conv_w=v_ffn_conv_w, ffn_conv_b=v_ffn_conv_b,
             ffn_w_out=v_ffn_w_out)
    return _step(x[0], positions[0], loss_target[0], W, M, V)


TRANSPOSED = ("mix_w_in", "att_w_qkv", "ffn_w_in")
SMALL = ("norm_mix", "norm_ffn", "norm_final", "s5_A_re", "s5_A_im", "s5_log_dt", "s5_B_re", "s5_B_im", "s5_C_re",
         "s5_C_im", "s5_D", "s5_glu_b", "hgrn_gamma", "hgrn_norm", "ffn_conv_b")
ORDER = ("norm_mix", "norm_ffn", "norm_final", "mix_w_in", "mix_w_out", "s5_A_re", "s5_A_im", "s5_log_dt", "s5_B_re",
         "s5_B_im", "s5_C_re", "s5_C_im", "s5_D", "s5_glu_w", "s5_glu_b", "hgrn_gamma", "hgrn_norm", "att_w_qkv",
         "att_w_o", "ffn_w_in", "ffn_conv_w", "ffn_conv_b", "ffn_w_out")
PACK_COLS = 1024


def _step(x, positions, target, W, M, V):
    L, D = x.shape
    me = 4 * lax.axis_index("x") + 2 * lax.axis_index("y") + lax.axis_index("c")
    n_cw = W["ffn_conv_w"].shape[-1]
    T = {n: tuple(jnp.swapaxes(d[n], -1, -2) for d in (W, M, V)) for n in TRANSPOSED}
    shards = {
        "mix_w_in": cast_bf16(T["mix_w_in"][0][0], name="mix_w_in_cast"),
        "conv_w": W["ffn_conv_w"].reshape(6, n_cw),
        "s5_glu_w": cast_bf16(W["s5_glu_w"][0], name="s5_glu_w_cast"),
        "mix_w_out": cast_bf16(W["mix_w_out"][0], name="mix_w_out_cast"),
        "ffn_w_in0": cast_bf16_layer(T["ffn_w_in"][0], 0, name="ffn_w_in0_cast"),
        "ffn_w_out0": cast_bf16_layer(W["ffn_w_out"], 0, name="ffn_w_out0_cast"),
        "att_w_qkv": cast_bf16(T["att_w_qkv"][0][0], name="att_w_qkv_cast"),
        "att_w_o": cast_bf16(W["att_w_o"][0], name="att_w_o_cast"),
        "ffn_w_in1": cast_bf16_layer(T["ffn_w_in"][0], 1, name="ffn_w_in1_cast"),
        "ffn_w_out1": cast_bf16_layer(W["ffn_w_out"], 1, name="ffn_w_out1_cast"),
    }
    gather_handles, token = copies_start(list(shards.values()), False, name="gather_start")
    gather_handle = dict(zip(shards, gather_handles))

    def gathered(key, after, cols):
        _, land = copies_wait(gather_handle[key], False, after, name=key + "_gwait")
        return cols_from_shards(land, name=key + "_asm") if cols else land.reshape(-1, land.shape[-1])

    conv_b = W["ffn_conv_b"].reshape(2, 1, -1)

    s5_params = (W["s5_A_re"][0], W["s5_A_im"][0], W["s5_log_dt"][0], W["s5_B_re"][0], W["s5_B_im"][0],
                 W["s5_C_re"][0], W["s5_C_im"][0])
    (a_re, a_im, wb_re, wb_im, wc_re, wc_im), s5_prep_vjp = jax.vjp(_s5_prepare, *s5_params)
    dvec = W["s5_D"].reshape(1, S5_WIDTH)
    glu_b = W["s5_glu_b"].reshape(1, S5_WIDTH)
    lb, lb_vjp = jax.vjp(_lower_bound, W["hgrn_gamma"])
    hg_norm = W["hgrn_norm"].reshape(1, -1)
    tabs = rope_tables(positions)

    hn0 = rms_fwd(x, W["norm_mix"][0], dep=token, name="l0_rms")
    w_mix_in = gathered("mix_w_in", hn0, False)
    proj = mm(hn0, w_mix_in, tb=True, name="l0_proj")
    u_bf = cast_bf16(proj, ncols=S5_WIDTH, name="l0_u_cast")
    bu_re = mm(u_bf, wb_re, name="s5_bu_re")
    bu_im = mm(u_bf, wb_im, name="s5_bu_im")
    xs_re, xs_im = s5_scan_fwd(a_re, a_im, bu_re, bu_im, name="s5_scan")
    y0 = mm(xs_im, wc_im, res=mm(xs_re, wc_re, name="s5_y_re"), name="s5_y_im")
    w_glu = gathered("s5_glu_w", y0, False)
    oa = s5_out_fwd(y0, proj, dvec, w_glu, glu_b, name="s5_out")
    ob, hg_states = hgrn_fwd(proj, lb, hg_norm, name="hgrn_fwd")
    cat = jnp.concatenate([oa, ob], axis=1)
    w_mix_out = gathered("mix_w_out", cat, False)
    h1 = mm(cat, w_mix_out, res=x, name="l0_mix_out")
    _, cw_all = copies_wait(gather_handle["conv_w"], False, h1, name="conv_w_gwait")
    conv_w = cw_all.transpose(1, 0, 2).reshape(2, 3, N_DEV * n_cw)
    w_ffn_in, w_ffn_out = [None, None], [None, None]
    h2, ffn0_saved, w_ffn_in[0], w_ffn_out[0] = _ffn_fwd(
        h1, W["norm_ffn"][0], lambda a: gathered("ffn_w_in0", a, False), conv_w[0], conv_b[0],
        lambda a: gathered("ffn_w_out0", a, False), "ffn0")

    hn2 = rms_fwd(h2, W["norm_mix"][1], name="l1_rms")
    w_qkv = gathered("att_w_qkv", hn2, False)
    qkv = mm(hn2, w_qkv, tb=True, name="l1_qkv")
    qkv_r = rope_fwd(qkv, tabs, name="rope_fwd")
    att_o, att_l = [], []
    for g, d in enumerate(ATT_DILATIONS):
        o_g, l_g = attn_fwd(qkv_r[g], qkv_r[3 + g], qkv_r[6 + g], d, name=f"attn_fwd{g}")
        att_o.append(o_g)
        att_l.append(l_g)
    o_att = merge_fwd(att_o, att_l, name="merge_fwd")
    w_o = gathered("att_w_o", o_att, True)
    h3 = mm(o_att, w_o, res=h2, name="l1_mix_out")
    h4, ffn1_saved, w_ffn_in[1], w_ffn_out[1] = _ffn_fwd(
        h3, W["norm_ffn"][1], lambda a: gathered("ffn_w_in1", a, False), conv_w[1], conv_b[1],
        lambda a: gathered("ffn_w_out1", a, False), "ffn1")

    exchanges = {}

    def send_grad(key, g, cols):
        if cols:
            parts = shards_from_cols(g, name=key + "_split")
        else:
            parts = g.reshape(N_DEV, g.shape[0] // N_DEV, g.shape[1])
        (handle,), sent = copies_start([parts], True, name=key + "_xstart")
        exchanges[key] = handle
        return sent

    loss, dh4, dg_final = final_loss(h4, W["norm_final"], target, name="final_loss")
    dh3, dg_ffn1, dcw1, dcb1 = _ffn_bwd(h3, W["norm_ffn"][1], w_ffn_in[1], conv_w[1], conv_b[1], w_ffn_out[1],
                                        ffn1_saved, dh4, "ffn1", lambda g: send_grad("ffn_w_in1", g, False),
                                        lambda g: send_grad("ffn_w_out1", g, False))
    sent = send_grad("att_w_o", mm(o_att, dh3, ta=True, name="l1_dwo"), True)
    d_oatt = mm(dh3, w_o, tb=True, dep=sent, name="l1_dmix")
    mb = merge_bwd(att_o, att_l, d_oatt, name="merge_bwd")
    d_slabs = [attn_bwd(qkv_r[g], qkv_r[3 + g], qkv_r[6 + g], att_l[g], mb[g], mb[3 + g], d, name=f"attn_bwd{g}")
               for g, d in enumerate(ATT_DILATIONS)]
    d_qkv = rope_bwd([s[0] for s in d_slabs] + [s[1] for s in d_slabs] + [s[2] for s in d_slabs], tabs,
                     name="rope_bwd")
    sent = send_grad("att_w_qkv", mm(d_qkv, hn2, ta=True, out_dtype=BF16, name="l1_dwqkv"), False)
    d_hn2 = mm(d_qkv, w_qkv, dep=sent, name="l1_dhn")
    dh2, dg_mix1 = rms_bwd(h2, W["norm_mix"][1], d_hn2, dh3, name="l1_drms")

    dh1, dg_ffn0, dcw0, dcb0 = _ffn_bwd(h1, W["norm_ffn"][0], w_ffn_in[0], conv_w[0], conv_b[0], w_ffn_out[0],
                                        ffn0_saved, dh2, "ffn0", lambda g: send_grad("ffn_w_in0", g, False),
                                        lambda g: send_grad("ffn_w_out0", g, False))
    sent = send_grad("mix_w_out", mm(cat, dh1, ta=True, out_dtype=BF16, name="l0_dwout"), False)
    dcat = mm(dh1, w_mix_out, tb=True, dep=sent, name="l0_dcat")
    d_hg, dlb, dhg_norm = hgrn_bwd(proj, lb, hg_norm, hg_states, dcat, name="hgrn_bwd")
    dy, du_d, z_bf, dzg, dglu_b, dD = s5_out_bwd(y0, proj, dvec, w_glu, glu_b, dcat, name="s5_dout")
    sent = send_grad("s5_glu_w", mm(z_bf, dzg, ta=True, out_dtype=BF16, name="s5_dglu"), False)
    dxs_re = mm(dy, wc_re, tb=True, dep=sent, name="s5_dxs_re")
    dxs_im = mm(dy, wc_im, tb=True, name="s5_dxs_im")
    dwc_re = mm(xs_re, dy, ta=True, name="s5_dwc_re")
    dwc_im = mm(xs_im, dy, ta=True, name="s5_dwc_im")
    dbu_re, dbu_im, da_re, da_im = s5_scan_bwd(a_re, a_im, xs_re, xs_im, dxs_re, dxs_im, name="s5_dscan")
    du = mm(dbu_im, wb_im, tb=True, res=mm(dbu_re, wb_re, tb=True, res=du_d, name="s5_du_re"), out_dtype=BF16,
            name="s5_du_im")
    dwb_re = mm(u_bf, dbu_re, ta=True, name="s5_dwb_re")
    dwb_im = mm(u_bf, dbu_im, ta=True, name="s5_dwb_im")
    s5_small = s5_prep_vjp((da_re, da_im, dwb_re, dwb_im, dwc_re, dwc_im))
    d_proj = jnp.concatenate([du, d_hg], axis=1)
    sent = send_grad("mix_w_in", mm(d_proj, hn0, ta=True, out_dtype=BF16, name="l0_dwin"), False)
    d_hn0 = mm(d_proj, w_mix_in, dep=sent, name="l0_dhn")
    grad_x, dg_mix0 = rms_bwd(x, W["norm_mix"][0], d_hn0, dh1, name="l0_drms")
    (d_gamma,) = lb_vjp(dlb)
    out = {}

    dA_re, dA_im, dlog_dt, dB_re, dB_im, dC_re, dC_im = s5_small
    small_g = dict(norm_mix=jnp.concatenate([dg_mix0, dg_mix1], axis=0), norm_ffn=jnp.concatenate([dg_ffn0, dg_ffn1], axis=0),
                   norm_final=dg_final, s5_A_re=dA_re, s5_A_im=dA_im, s5_log_dt=dlog_dt, s5_B_re=dB_re, s5_B_im=dB_im,
                   s5_C_re=dC_re, s5_C_im=dC_im, s5_D=dD, s5_glu_b=dglu_b, hgrn_gamma=d_gamma, hgrn_norm=dhg_norm,
                   ffn_conv_b=jnp.concatenate([dcb0, dcb1], axis=0))
    conv_w_g = jnp.stack([dcw0, dcw1], axis=0)
    sizes = [math.prod(W[n].shape) for n in SMALL]
    n_conv = conv_w_g.size
    total = sum(sizes) + n_conv + 1
    rows = -(-total // PACK_COLS)
    rows = -(-rows // 8) * 8
    pad = rows * PACK_COLS - total

    def pack(vals, conv_part, last):
        flat = [v.reshape(-1).astype(F32) for v in vals] + [conv_part.reshape(-1), last.reshape(-1),
                                                            jnp.zeros((pad,), F32)]
        return jnp.concatenate(flat).reshape(rows, PACK_COLS)

    def conv_full(shard):
        col_owner = lax.broadcasted_iota(jnp.int32, (2, 3, N_DEV * n_cw), 2) // n_cw
        return jnp.where(col_owner == me, jnp.tile(shard, (1, 1, N_DEV)), 0.0)

    zero1 = jnp.zeros((1,), F32)
    g_pack = pack([small_g[n] for n in SMALL], conv_w_g, loss)
    w_pack = pack([W[n] for n in SMALL], conv_full(W["ffn_conv_w"]), zero1)
    m_pack = pack([M[n] for n in SMALL], conv_full(M["ffn_conv_w"]), zero1)
    v_pack = pack([V[n] for n in SMALL], conv_full(V["ffn_conv_w"]), zero1 + 1.0)
    (small_handle,), small_sent = copies_start([g_pack], False, name="small_xstart")

    def finish(name, n_layers):
        w3, m3, v3 = T[name] if name in TRANSPOSED else (W[name], M[name], V[name])
        res = None
        for layer in reversed(range(n_layers)):
            key = name if n_layers == 1 else f"{name}{layer}"
            own, recv = copies_wait(exchanges[key], True, small_sent, name=key + "_xwait")
            _, R, Cn = recv.shape
            res = reduce_adamw(recv, own, True, me, w3.reshape(n_layers * R, Cn), m3.reshape(n_layers * R, Cn),
                               v3.reshape(n_layers * R, Cn), layer=layer, n_layers=n_layers, into=res,
                               name=key + "_adamw")
        res = [r.reshape(w3.shape) for r in res]
        return tuple(jnp.swapaxes(r, -1, -2) for r in res) if name in TRANSPOSED else tuple(res)

    for name in ("ffn_w_out", "ffn_w_in"):
        out[name] = finish(name, 2)
    for name in ("att_w_o", "att_w_qkv", "mix_w_out", "s5_glu_w", "mix_w_in"):
        out[name] = finish(name, 1)

    small_own, small_recv = copies_wait(small_handle, False, out["s5_glu_w"][0], name="small_xwait")
    res = reduce_adamw(small_recv, small_own, False, me, w_pack, m_pack, v_pack, name="small_adamw")
    flat = [r.reshape(-1) for r in res]
    off = 0
    for n, sz in zip(SMALL, sizes):
        out[n] = tuple(f[off:off + sz].reshape(W[n].shape) for f in flat)
        off += sz
    conv_res = [f[off:off + n_conv].reshape(2, 3, N_DEV * n_cw) for f in flat]
    out["ffn_conv_w"] = tuple(lax.dynamic_slice(c, (0, 0, me * n_cw), (2, 3, n_cw)) for c in conv_res)
    off += n_conv
    loss_total = flat[0][off]

    result = [loss_total, grad_x[None]]
    for k in range(4):
        result += [out[n][k] for n in ORDER]
    return tuple(result)
```

```python
import functools
import math

import jax
import jax.numpy as jnp
from jax import lax
from jax.experimental import pallas as pl
from jax.experimental.pallas import tpu as pltpu

F32 = jnp.float32
BF16 = jnp.bfloat16
MESH_ID = pl.DeviceIdType.MESH
N_DEV = 8
VMEM_LIMIT_BYTES = 56 * 1024 * 1024

NORM_EPS = 1e-6
S5_WIDTH, S5_GROUP, S5_GROUPS, S5_STATE = 512, 16, 32, 64
HG_HEADS, HG_DIM, HG_CHUNK = 4, 128, 64
ATT_E, ATT_HPG, ATT_BLOCK = 64, 8, 128
ATT_DILATIONS = (1, 4, 16)
ROT_DIM, ROPE_THETA = 16, 500000.0
D_FF = 2816
ADAM_LR, ADAM_B1, ADAM_B2, ADAM_EPS, ADAM_WD, ADAM_STEP = 0.001, 0.9, 0.999, 1e-08, 0.01, 10
NEG_BIG = -1e30


def _params(**kw):
    return pltpu.CompilerParams(vmem_limit_bytes=VMEM_LIMIT_BYTES, **kw)


def _pick(n, cands):
    for c in cands:
        if n % c == 0:
            return c
    return n


def _dot(a, b):
    return jnp.dot(a.astype(BF16), b.astype(BF16), preferred_element_type=F32)


def _dot_nt(a, b):
    return lax.dot_general(a.astype(BF16), b.astype(BF16), (((1,), (1,)), ((), ())), preferred_element_type=F32)


def _dot_tn(a, b):
    return lax.dot_general(a.astype(BF16), b.astype(BF16), (((0,), (0,)), ((), ())), preferred_element_type=F32)


def _split2(x):
    hi = x.astype(BF16)
    return hi, (x - hi.astype(F32)).astype(BF16)


def _dot_x3(a, b, contract=((1,), (0,))):
    dn = (contract, ((), ()))
    a1, a2 = _split2(a)
    b1, b2 = _split2(b)
    return (lax.dot_general(a1, b1, dn, preferred_element_type=F32) + lax.dot_general(a1, b2, dn, preferred_element_type=F32)
            + lax.dot_general(a2, b1, dn, preferred_element_type=F32))


def _sigmoid(x):
    return 1.0 / (1.0 + jnp.exp(-x))


V7X_HBM_BYTES_PER_S = 3.2e12
V7X_MXU_FLOPS_PER_S = 0.7e15
GRID_STEP_S = 0.35e-6
MM_VMEM_BUDGET = 40 * 1024 * 1024


def _divisors(n, cands):
    return [c for c in cands if c <= n and n % c == 0] or [n]


def _mm_tiles(m, n, k, sa, sb, so, sr):
    best = None
    for tm in _divisors(m, (2816, 2048, 1408, 1024, 512, 256, 128)):
        for tn in _divisors(n, (2816, 2048, 1408, 1024, 512, 256, 128)):
            for tk in _divisors(k, (k, 2816, 2560, 2304, 2048, 1536, 1408, 1280, 1024, 512, 256, 128)):
                nk = k // tk
                vmem = 2 * (tm * tk * sa + tk * tn * sb + tm * tn * (so + sr)) + (tm * tn * 4 if nk > 1 else 0)
                vmem += tm * tk * 2 * (sa > 2) + tk * tn * 2 * (sb > 2) + tm * tn * 4
                if vmem > MM_VMEM_BUDGET:
                    continue
                ni, nj = m // tm, n // tn
                for i_outer in (True, False):
                    if i_outer:
                        a_reads = 1 if nk == 1 else nj
                        b_reads = 1 if (nk == 1 and nj == 1) else ni
                    else:
                        b_reads = 1 if nk == 1 else ni
                        a_reads = 1 if (nk == 1 and ni == 1) else nj
                    traffic = a_reads * m * k * sa + b_reads * k * n * sb + m * n * (so + sr)
                    t = max(traffic / V7X_HBM_BYTES_PER_S, 2.0 * m * n * k / V7X_MXU_FLOPS_PER_S)
                    t += ni * nj * nk * GRID_STEP_S
                    t += (tm * tk * sa + tk * tn * sb + tm * tn * so) / V7X_HBM_BYTES_PER_S
                    if best is None or t < best[0]:
                        best = (t, tm, tn, tk, i_outer)
    assert best is not None, (m, n, k)
    return best[1:]


def mm(a, b, *, ta=False, tb=False, res=None, out_dtype=F32, dep=None, name):
    m, k = (a.shape[1], a.shape[0]) if ta else a.shape
    n = b.shape[0] if tb else b.shape[1]
    assert (b.shape[1] if tb else b.shape[0]) == k
    has_res = res is not None
    tm, tn, tk, i_outer = _mm_tiles(m, n, k, a.dtype.itemsize, b.dtype.itemsize, jnp.dtype(out_dtype).itemsize,
                                    res.dtype.itemsize if has_res else 0)
    nk = k // tk
    deps = [] if dep is None else [dep]
    dn = (((0 if ta else 1,), (1 if tb else 0,)), ((), ()))

    def body_single(*refs):
        a_ref, b_ref = refs[:2]
        o_ref = refs[-1]
        out = lax.dot_general(a_ref[...].astype(BF16), b_ref[...].astype(BF16), dn, preferred_element_type=F32)
        if has_res:
            out = out + refs[2][...].astype(F32)
        o_ref[...] = out.astype(o_ref.dtype)

    def body(*refs):
        a_ref, b_ref = refs[:2]
        r_ref = refs[2] if has_res else None
        o_ref, acc_ref = refs[-2:]
        kk = pl.program_id(2)
        part = lax.dot_general(a_ref[...].astype(BF16), b_ref[...].astype(BF16), dn, preferred_element_type=F32)

        @pl.when(kk == 0)
        def _():
            acc_ref[...] = part

        @pl.when(kk > 0)
        def _():
            acc_ref[...] += part

        @pl.when(kk == nk - 1)
        def _():
            out = acc_ref[...]
            if has_res:
                out = out + r_ref[...].astype(F32)
            o_ref[...] = out.astype(o_ref.dtype)

    def ij(f):
        return (lambda g0, g1, q: f(g0, g1, q)) if i_outer else (lambda g0, g1, q: f(g1, g0, q))

    a_spec = pl.BlockSpec((tk, tm), ij(lambda i, j, q: (q, i))) if ta else pl.BlockSpec((tm, tk), ij(lambda i, j, q: (i, q)))
    b_spec = pl.BlockSpec((tn, tk), ij(lambda i, j, q: (j, q))) if tb else pl.BlockSpec((tk, tn), ij(lambda i, j, q: (q, j)))
    o_spec = pl.BlockSpec((tm, tn), ij(lambda i, j, q: (i, j)))
    in_specs = [a_spec, b_spec] + ([o_spec] if has_res else []) + [pl.BlockSpec((8, 128), lambda g0, g1, q: (0, 0))] * len(deps)
    args = (a, b) + ((res,) if has_res else ()) + tuple(deps)
    grid = (m // tm, n // tn, nk) if i_outer else (n // tn, m // tm, nk)
    return pl.pallas_call(
        body_single if nk == 1 else body, grid=grid, in_specs=in_specs, out_specs=o_spec,
        out_shape=jax.ShapeDtypeStruct((m, n), out_dtype),
        scratch_shapes=[] if nk == 1 else [pltpu.VMEM((tm, tn), F32)],
        compiler_params=_params(dimension_semantics=("parallel", "parallel", "arbitrary")), name=name,
    )(*args)


def mm_parts(a, b, *, ta=False, tb=False, res=None, out_dtype=F32, dep=None, name):
    if ta:
        P = BD_PARTS
        k = a.shape[0]
        m, n = a.shape[1] // P, b.shape[1] // P
    else:
        P = b.shape[0]
        m, k = a.shape[0], a.shape[1] // P
        n = b.shape[1] if tb else b.shape[2]
    has_res = res is not None
    tm, tn, tk, _ = _mm_tiles(m, n, k, a.dtype.itemsize, b.dtype.itemsize, jnp.dtype(out_dtype).itemsize,
                              res.dtype.itemsize if has_res else 0)
    ni, nj, nk = m // tm, n // tn, k // tk
    deps = [] if dep is None else [dep]
    dn = (((0 if ta else 1,), (1 if tb else 0,)), ((), ()))

    def body(*refs):
        a_ref, b_ref = refs[:2]
        o_ref, acc_ref = refs[-2:]
        q = pl.program_id(3)
        part = lax.dot_general(a_ref[...].astype(BF16), b_ref[...].astype(BF16), dn, preferred_element_type=F32)

        @pl.when(q == 0)
        def _():
            acc_ref[...] = part

        @pl.when(q > 0)
        def _():
            acc_ref[...] += part

        @pl.when(q == nk - 1)
        def _():
            out = acc_ref[...]
            if has_res:
                out = out + refs[2][...].astype(F32)
            o_ref[...] = out.astype(o_ref.dtype)

    if ta:
        a_spec = pl.BlockSpec((tk, tm), lambda p, i, j, q: (q, p * ni + i))
        b_spec = pl.BlockSpec((tk, tn), lambda p, i, j, q: (q, p * nj + j))
        o_spec = pl.BlockSpec((None, tm, tn), lambda p, i, j, q: (p, i, j))
        out_shape = jax.ShapeDtypeStruct((P, m, n), out_dtype)
    else:
        a_spec = pl.BlockSpec((tm, tk), lambda p, i, j, q: (i, p * nk + q))
        b_spec = (pl.BlockSpec((None, tn, tk), lambda p, i, j, q: (p, j, q)) if tb
                  else pl.BlockSpec((None, tk, tn), lambda p, i, j, q: (p, q, j)))
        o_spec = pl.BlockSpec((tm, tn), lambda p, i, j, q: (i, p * nj + j))
        out_shape = jax.ShapeDtypeStruct((m, P * n), out_dtype)
    in_specs = [a_spec, b_spec] + ([o_spec] if has_res else []) + [pl.BlockSpec((8, 128), lambda p, i, j, q: (0, 0))] * len(deps)
    args = (a, b) + ((res,) if has_res else ()) + tuple(deps)
    return pl.pallas_call(
        body, grid=(P, ni, nj, nk), in_specs=in_specs, out_specs=o_spec, out_shape=out_shape,
        scratch_shapes=[pltpu.VMEM((tm, tn), F32)],
        compiler_params=_params(dimension_semantics=("parallel", "parallel", "parallel", "arbitrary")), name=name,
    )(*args)


def rms_fwd(x, g, *, dep=None, name):
    L, D = x.shape
    tr = _pick(L, (256, 128))

    def body(x_ref, g_ref, *rest):
        o_ref = rest[-1]
        xv = x_ref[...]
        r = lax.rsqrt(jnp.mean(xv * xv, axis=-1, keepdims=True) + NORM_EPS)
        o_ref[...] = (xv * r * g_ref[...]).astype(o_ref.dtype)

    row = pl.BlockSpec((tr, D), lambda i: (i, 0))
    vec = pl.BlockSpec((1, D), lambda i: (0, 0))
    deps = [] if dep is None else [dep]
    return pl.pallas_call(body, grid=(L // tr,), in_specs=[row, vec] + [pl.BlockSpec((8, 128), lambda i: (0, 0))] * len(deps),
                          out_specs=row, out_shape=jax.ShapeDtypeStruct((L, D), BF16), name=name)(
        x, g.reshape(1, D), *deps)


def rms_bwd(x, g, dy, dres, *, name):
    L, D = x.shape
    tr = _pick(L, (256, 128))

    def body(x_ref, g_ref, dy_ref, dres_ref, dx_ref, dg_ref):
        xv = x_ref[...]
        r = lax.rsqrt(jnp.mean(xv * xv, axis=-1, keepdims=True) + NORM_EPS)
        xh = xv * r
        dyv = dy_ref[...].astype(F32)

        @pl.when(pl.program_id(0) == 0)
        def _():
            dg_ref[...] = jnp.zeros_like(dg_ref)

        dg_ref[...] += jnp.sum(dyv * xh, axis=0, keepdims=True)
        dxh = dyv * g_ref[...]
        dx_ref[...] = dres_ref[...] + r * (dxh - xh * jnp.mean(dxh * xh, axis=-1, keepdims=True))

    row = pl.BlockSpec((tr, D), lambda i: (i, 0))
    vec = pl.BlockSpec((1, D), lambda i: (0, 0))
    return pl.pallas_call(body, grid=(L // tr,), in_specs=[row, vec, row, row], out_specs=[row, vec],
                          out_shape=[jax.ShapeDtypeStruct((L, D), F32), jax.ShapeDtypeStruct((1, D), F32)],
                          compiler_params=_params(dimension_semantics=("arbitrary",)), name=name)(
        x, g.reshape(1, D), dy, dres)


def final_loss(h, g, target, *, name):
    L, D = h.shape
    tr = _pick(L, (256, 128))

    def body(x_ref, g_ref, t_ref, loss_ref, dx_ref, dg_ref):
        xv = x_ref[...]
        gv = g_ref[...]
        r = lax.rsqrt(jnp.mean(xv * xv, axis=-1, keepdims=True) + NORM_EPS)
        xh = xv * r
        err = xh * gv - t_ref[...]

        @pl.when(pl.program_id(0) == 0)
        def _():
            dg_ref[...] = jnp.zeros_like(dg_ref)
            loss_ref[...] = jnp.zeros_like(loss_ref)

        loss_ref[...] += 0.5 * jnp.sum(jnp.mean(err * err, axis=-1, keepdims=True), axis=0, keepdims=True)
        dyv = err * (1.0 / D)
        dg_ref[...] += jnp.sum(dyv * xh, axis=0, keepdims=True)
        dxh = dyv * gv
        dx_ref[...] = r * (dxh - xh * jnp.mean(dxh * xh, axis=-1, keepdims=True))

    row = pl.BlockSpec((tr, D), lambda i: (i, 0))
    vec = pl.BlockSpec((1, D), lambda i: (0, 0))
    one = pl.BlockSpec((1, 1), lambda i: (0, 0))
    return pl.pallas_call(body, grid=(L // tr,), in_specs=[row, vec, row], out_specs=[one, row, vec],
                          out_shape=[jax.ShapeDtypeStruct((1, 1), F32), jax.ShapeDtypeStruct((L, D), F32),
                                     jax.ShapeDtypeStruct((1, D), F32)],
                          compiler_params=_params(dimension_semantics=("arbitrary",)), name=name)(
        h, g.reshape(1, D), target)


def _cmul(ar, ai, br, bi):
    return ar * br - ai * bi, ar * bi + ai * br


def _powers(ar, ai):
    rows = [(ar, ai)]
    for _ in range(7):
        rows.append(_cmul(rows[-1][0], rows[-1][1], ar, ai))
    table = (jnp.concatenate([r[0] for r in rows], axis=0), jnp.concatenate([r[1] for r in rows], axis=0))
    return (rows[0], rows[1], rows[3]), table


def _block_scan(br, bi, steps, shift):
    yr, yi = br, bi
    for s, (pr, pi) in zip((1, 2, 4), steps):
        sr, si = shift(yr, s), shift(yi, s)
        yr, yi = yr + pr * sr - pi * si, yi + pr * si + pi * sr
    return yr, yi


def s5_scan_fwd(a_re, a_im, bu_re, bu_im, *, name):
    L, P = bu_re.shape
    W = _pick(P, (512, 256, 128))

    def body(ar_ref, ai_ref, br_ref, bi_ref, xr_ref, xi_ref):
        steps, (tr, ti) = _powers(ar_ref[...], ai_ref[...])
        row = lax.broadcasted_iota(jnp.int32, (8, W), 0)

        def shift(y, s):
            return jnp.where(row >= s, pltpu.roll(y, s, 0), 0.0)

        def step(t8, carry):
            cr, ci = carry
            base = pl.multiple_of(t8 * 8, 8)
            yr, yi = _block_scan(br_ref[pl.ds(base, 8), :], bi_ref[pl.ds(base, 8), :], steps, shift)
            xr = yr + tr * cr - ti * ci
            xi = yi + tr * ci + ti * cr
            xr_ref[pl.ds(base, 8), :] = xr
            xi_ref[pl.ds(base, 8), :] = xi
            return jnp.broadcast_to(xr[7:8, :], (8, W)), jnp.broadcast_to(xi[7:8, :], (8, W))

        zero = jnp.zeros((8, W), F32)
        lax.fori_loop(0, L // 8, step, (zero, zero), unroll=2)

    vec = pl.BlockSpec((1, W), lambda j: (0, j))
    col = pl.BlockSpec((L, W), lambda j: (0, j))
    return pl.pallas_call(body, grid=(P // W,), in_specs=[vec, vec, col, col], out_specs=[col, col],
                          out_shape=[jax.ShapeDtypeStruct((L, P), F32)] * 2,
                          compiler_params=_params(dimension_semantics=("parallel",)), name=name)(
        a_re, a_im, bu_re, bu_im)


def s5_scan_bwd(a_re, a_im, xs_re, xs_im, dx_re, dx_im, *, name):
    L, P = xs_re.shape
    W = _pick(P, (256, 128))

    def body(ar_ref, ai_ref, xr_ref, xi_ref, dr_ref, di_ref, lr_ref, li_ref, dar_ref, dai_ref):
        ar, ai = ar_ref[...], -ai_ref[...]
        steps, (tr, ti) = _powers(ar, ai)
        tr = jnp.concatenate([tr[j:j + 1, :] for j in range(7, -1, -1)], axis=0)
        ti = jnp.concatenate([ti[j:j + 1, :] for j in range(7, -1, -1)], axis=0)
        row8 = lax.broadcasted_iota(jnp.int32, (8, W), 0)
        nblk = L // 8

        def shift(y, s):
            return jnp.where(row8 < 8 - s, pltpu.roll(y, 8 - s, 0), 0.0)

        def step(s, carry):
            cr, ci = carry
            base = pl.multiple_of((nblk - 1 - s) * 8, 8)
            yr, yi = _block_scan(dr_ref[pl.ds(base, 8), :], di_ref[pl.ds(base, 8), :], steps, shift)
            lr = yr + tr * cr - ti * ci
            li = yi + tr * ci + ti * cr
            lr_ref[pl.ds(base, 8), :] = lr
            li_ref[pl.ds(base, 8), :] = li
            return jnp.broadcast_to(lr[0:1, :], (8, W)), jnp.broadcast_to(li[0:1, :], (8, W))

        zero = jnp.zeros((8, W), F32)
        lax.fori_loop(0, nblk, step, (zero, zero), unroll=2)
        row = lax.broadcasted_iota(jnp.int32, (L, W), 0)
        xpr = jnp.where(row >= 1, pltpu.roll(xr_ref[...], 1, 0), 0.0)
        xpi = jnp.where(row >= 1, pltpu.roll(xi_ref[...], 1, 0), 0.0)
        lr, li = lr_ref[...], li_ref[...]
        dar_ref[...] = jnp.sum(lr * xpr + li * xpi, axis=0, keepdims=True)
        dai_ref[...] = jnp.sum(li * xpr - lr * xpi, axis=0, keepdims=True)

    vec = pl.BlockSpec((1, W), lambda j: (0, j))
    col = pl.BlockSpec((L, W), lambda j: (0, j))
    return pl.pallas_call(body, grid=(P // W,), in_specs=[vec, vec, col, col, col, col],
                          out_specs=[col, col, vec, vec],
                          out_shape=[jax.ShapeDtypeStruct((L, P), F32)] * 2 + [jax.ShapeDtypeStruct((1, P), F32)] * 2,
                          compiler_params=_params(dimension_semantics=("parallel",)), name=name)(
        a_re, a_im, xs_re, xs_im, dx_re, dx_im)


def _gelu(y):
    c = math.sqrt(2.0 / math.pi)
    t = jnp.tanh(c * (y + 0.044715 * y * y * y))
    return 0.5 * y * (1.0 + t), t


def s5_out_fwd(y0, proj, dvec, glu_w, glu_b, *, name):
    L, C = y0.shape
    tr = _pick(L, (256, 128))

    def body(y_ref, u_ref, d_ref, w_ref, b_ref, o_ref):
        z, _ = _gelu(y_ref[...] + d_ref[...] * u_ref[...])
        zg = _dot(z, w_ref[...]) + b_ref[...]
        o_ref[...] = (z * _sigmoid(zg)).astype(o_ref.dtype)

    row = pl.BlockSpec((tr, C), lambda i: (i, 0))
    vec = pl.BlockSpec((1, C), lambda i: (0, 0))
    wsp = pl.BlockSpec((C, C), lambda i: (0, 0))
    return pl.pallas_call(body, grid=(L // tr,), in_specs=[row, row, vec, wsp, vec], out_specs=row,
                          out_shape=jax.ShapeDtypeStruct((L, C), BF16), name=name)(
        y0, proj, dvec, glu_w, glu_b)


def s5_out_bwd(y0, proj, dvec, glu_w, glu_b, dcat, *, name):
    L, C = y0.shape
    tr = _pick(L, (256, 128))

    def body(y_ref, u_ref, d_ref, w_ref, b_ref, do_ref, dy_ref, dud_ref, z_ref, dzg_ref, db_ref, dd_ref):
        u = u_ref[...]
        y = y_ref[...] + d_ref[...] * u
        z, t = _gelu(y)
        zg = _dot(z, w_ref[...]) + b_ref[...]
        s = _sigmoid(zg)
        do = do_ref[...]
        dzg = do * z * s * (1.0 - s)
        dz = do * s + _dot_nt(dzg, w_ref[...])
        c = math.sqrt(2.0 / math.pi)
        dgelu = 0.5 * (1.0 + t) + 0.5 * y * (1.0 - t * t) * c * (1.0 + 3.0 * 0.044715 * y * y)
        dy = dz * dgelu

        @pl.when(pl.program_id(0) == 0)
        def _():
            db_ref[...] = jnp.zeros_like(db_ref)
            dd_ref[...] = jnp.zeros_like(dd_ref)

        db_ref[...] += jnp.sum(dzg, axis=0, keepdims=True)
        dd_ref[...] += jnp.sum(dy * u, axis=0, keepdims=True)
        dy_ref[...] = dy
        dud_ref[...] = dy * d_ref[...]
        z_ref[...] = z.astype(BF16)
        dzg_ref[...] = dzg.astype(BF16)

    row = pl.BlockSpec((tr, C), lambda i: (i, 0))
    vec = pl.BlockSpec((1, C), lambda i: (0, 0))
    wsp = pl.BlockSpec((C, C), lambda i: (0, 0))
    return pl.pallas_call(body, grid=(L // tr,), in_specs=[row, row, vec, wsp, vec, row],
                          out_specs=[row, row, row, row, vec, vec],
                          out_shape=[jax.ShapeDtypeStruct((L, C), F32), jax.ShapeDtypeStruct((L, C), F32),
                                     jax.ShapeDtypeStruct((L, C), BF16), jax.ShapeDtypeStruct((L, C), BF16),
                                     jax.ShapeDtypeStruct((1, C), F32), jax.ShapeDtypeStruct((1, C), F32)],
                          compiler_params=_params(dimension_semantics=("arbitrary",)), name=name)(
        y0, proj, dvec, glu_w, glu_b, dcat)


def _dot_tri(tri, x, tri_left=True):
    t = tri.astype(BF16)
    x1 = x.astype(BF16)
    r1 = x - x1.astype(F32)
    x2 = r1.astype(BF16)
    x3 = (r1 - x2.astype(F32)).astype(BF16)
    dot = (lambda p: jnp.dot(t, p, preferred_element_type=F32)) if tri_left else (
        lambda p: jnp.dot(p, t, preferred_element_type=F32))
    return dot(x1) + dot(x2) + dot(x3)


def _hg_gates(xq, xf, lb, tri):
    C = xq.shape[0]
    sq = _sigmoid(xq)
    q = xq * sq
    sg = _sigmoid(xf)
    f = lb + (1.0 - lb) * sg
    kk = 1.0 - f
    b = _dot_tri(tri, jnp.log(f))
    bm = b[C // 2 - 1:C // 2, :]
    bl = b[C - 1:C, :]
    eb = jnp.exp(b)
    eqm, ekm, ekl = jnp.exp(b - bm), jnp.exp(bm - b), jnp.exp(bl - b)
    return dict(sq=sq, q=q, sg=sg, f=f, kk=kk, eb=eb, ebl=jnp.exp(bl), eqm=eqm, ekm=ekm, ekl=ekl,
                qb=q * eb, qt=q * eqm, kt=kk * ekm, kh=kk * ekl)


def _tri(C, lower):
    r = lax.broadcasted_iota(jnp.int32, (C, C), 0)
    c = lax.broadcasted_iota(jnp.int32, (C, C), 1)
    return (r >= c) if lower else (c >= r)


def hgrn_fwd(proj, lb, norm_g, *, name):
    L = proj.shape[0]
    C, H, K = HG_CHUNK, HG_HEADS, HG_DIM
    HK = H * K
    nc = L // C

    def body(q_ref, f_ref, i_ref, g_ref, lb_ref, ng_ref, o_ref, sall_ref, st_ref):
        @pl.when(pl.program_id(0) == 0)
        def _():
            st_ref[...] = jnp.zeros_like(st_ref)

        mask = _tri(C, True)
        gt = _hg_gates(q_ref[...], f_ref[...], lb_ref[...], mask.astype(F32))
        v_all = i_ref[...]
        outs = []
        for h in range(H):
            sl = slice(h * K, (h + 1) * K)
            v = v_all[:, sl]
            st = st_ref[h]
            sall_ref[h] = st
            att = jnp.where(mask, _dot_nt(gt["qt"][:, sl], gt["kt"][:, sl]), 0.0)
            o = _dot(att, v) + _dot_nt(gt["qb"][:, sl], st)
            st_ref[h] = st * gt["ebl"][:, sl] + _dot_tn(v, gt["kh"][:, sl])
            outs.append(o * lax.rsqrt(jnp.mean(o * o, axis=-1, keepdims=True) + NORM_EPS))
        xg = g_ref[...]
        o_ref[...] = (jnp.concatenate(outs, axis=1) * ng_ref[...] * (xg * _sigmoid(xg))).astype(o_ref.dtype)

    def blk(cb):
        return pl.BlockSpec((C, HK), lambda i: (i, cb))

    vec = pl.BlockSpec((1, HK), lambda i: (0, 0))
    return pl.pallas_call(
        body, grid=(nc,), in_specs=[blk(1), blk(2), blk(3), blk(4), vec, vec],
        out_specs=[pl.BlockSpec((C, HK), lambda i: (i, 0)), pl.BlockSpec((None, H, K, K), lambda i: (i, 0, 0, 0))],
        out_shape=[jax.ShapeDtypeStruct((L, HK), BF16), jax.ShapeDtypeStruct((nc, H, K, K), F32)],
        scratch_shapes=[pltpu.VMEM((H, K, K), F32)],
        compiler_params=_params(dimension_semantics=("arbitrary",)), name=name,
    )(proj, proj, proj, proj, lb, norm_g)


def hgrn_bwd(proj, lb, norm_g, sall, dcat, *, name):
    L = proj.shape[0]
    C, H, K = HG_CHUNK, HG_HEADS, HG_DIM
    HK = H * K
    nc = L // C

    def body(q_ref, f_ref, i_ref, g_ref, lb_ref, ng_ref, sall_ref, do_ref, dx_ref, dlb_ref, dng_ref, dst_ref):
        @pl.when(pl.program_id(0) == 0)
        def _():
            dst_ref[...] = jnp.zeros_like(dst_ref)
            dlb_ref[...] = jnp.zeros_like(dlb_ref)
            dng_ref[...] = jnp.zeros_like(dng_ref)

        mask = _tri(C, True)
        xq, xg, v_all = q_ref[...], g_ref[...], i_ref[...]
        lb_all, ng = lb_ref[...], ng_ref[...]
        gt = _hg_gates(xq, f_ref[...], lb_all, mask.astype(F32))
        sgg = _sigmoid(xg)
        d_ob = do_ref[...]
        d_on = d_ob * (xg * sgg)
        doh = d_on * ng
        ohs, d_qts, d_qbs, d_kts, d_khs, dvs, d_bls = [], [], [], [], [], [], []
        for h in range(H):
            sl = slice(h * K, (h + 1) * K)
            v, st, dst = v_all[:, sl], sall_ref[h], dst_ref[h]
            qt, kt, kh, qb = gt["qt"][:, sl], gt["kt"][:, sl], gt["kh"][:, sl], gt["qb"][:, sl]
            att = jnp.where(mask, _dot_nt(qt, kt), 0.0)
            o = _dot(att, v) + _dot_nt(qb, st)
            r = lax.rsqrt(jnp.mean(o * o, axis=-1, keepdims=True) + NORM_EPS)
            oh = o * r
            do = r * (doh[:, sl] - oh * jnp.mean(doh[:, sl] * oh, axis=-1, keepdims=True))
            datt = jnp.where(mask, _dot_nt(do, v), 0.0)
            dvs.append(_dot_tn(att, do) + _dot_nt(kh, dst))
            d_qbs.append(_dot_x3(do, st))
            d_qts.append(_dot_x3(datt, kt))
            d_kts.append(_dot_x3(datt, qt, ((0,), (0,))))
            d_kh = _dot_x3(v, dst)
            d_khs.append(d_kh)
            d_bls.append(jnp.sum(dst * st, axis=0, keepdims=True) * gt["ebl"][:, sl]
                         + jnp.sum(d_kh * kh, axis=0, keepdims=True))
            dst_ref[h] = dst * gt["ebl"][:, sl] + _dot_tn(do, qb)
            ohs.append(oh)
        oh, d_qt, d_qb, d_kt, d_kh, dv, d_bl = (jnp.concatenate(p, axis=1) for p in
                                                (ohs, d_qts, d_qbs, d_kts, d_khs, dvs, d_bls))
        dxg = d_ob * (oh * ng) * (sgg * (1.0 + xg * (1.0 - sgg)))
        dng_ref[...] += jnp.sum(d_on * oh, axis=0, keepdims=True)
        dq = d_qt * gt["eqm"] + d_qb * gt["eb"]
        db = d_qt * gt["qt"] + d_qb * gt["qb"] - d_kt * gt["kt"] - d_kh * gt["kh"]
        rowi = lax.broadcasted_iota(jnp.int32, (C, HK), 0)
        db = db + jnp.where(rowi == C - 1, d_bl, 0.0)
        dkk = d_kt * gt["ekm"] + d_kh * gt["ekl"]
        dlg = _dot_tri(_tri(C, False).astype(F32), db)
        df = dlg / gt["f"] - dkk
        sg, sq = gt["sg"], gt["sq"]
        dlb_ref[...] += jnp.sum(df * (1.0 - sg), axis=0, keepdims=True)
        dx_ref[:, 0:HK] = (dq * (sq * (1.0 + xq * (1.0 - sq)))).astype(dx_ref.dtype)
        dx_ref[:, HK:2 * HK] = (df * (1.0 - lb_all) * sg * (1.0 - sg)).astype(dx_ref.dtype)
        dx_ref[:, 2 * HK:3 * HK] = dv.astype(dx_ref.dtype)
        dx_ref[:, 3 * HK:4 * HK] = dxg.astype(dx_ref.dtype)

    def blk(cb):
        return pl.BlockSpec((C, HK), lambda i: (nc - 1 - i, cb))

    vec = pl.BlockSpec((1, HK), lambda i: (0, 0))
    return pl.pallas_call(
        body, grid=(nc,),
        in_specs=[blk(1), blk(2), blk(3), blk(4), vec, vec,
                  pl.BlockSpec((None, H, K, K), lambda i: (nc - 1 - i, 0, 0, 0)), blk(1)],
        out_specs=[pl.BlockSpec((C, 4 * HK), lambda i: (nc - 1 - i, 0)), vec, vec],
        out_shape=[jax.ShapeDtypeStruct((L, 4 * HK), BF16), jax.ShapeDtypeStruct((1, HK), F32),
                   jax.ShapeDtypeStruct((1, HK), F32)],
        scratch_shapes=[pltpu.VMEM((H, K, K), F32)],
        compiler_params=_params(dimension_semantics=("arbitrary",)), name=name,
    )(proj, proj, proj, proj, lb, norm_g, sall, dcat)


def _shift_down(x, k, row):
    return jnp.where(row >= k, pltpu.roll(x, k, 0), 0.0)


def _shift_up(x, k, row):
    n = x.shape[0]
    return jnp.where(row < n - k, pltpu.roll(x, n - k, 0), 0.0)


def convgate_fwd(hu, conv_w, conv_b, *, name):
    L, C2 = hu.shape
    C = C2 // 2
    tc = _pick(C, (256, 128))
    nb = C // tc

    def body(a_ref, b_ref, wa_ref, wb_ref, ba_ref, bb_ref, o_ref):
        row = lax.broadcasted_iota(jnp.int32, (L, tc), 0)

        def conv(x, w, bias):
            return w[2:3, :] * x + w[1:2, :] * _shift_down(x, 1, row) + w[0:1, :] * _shift_down(x, 2, row) + bias

        ca = conv(a_ref[...], wa_ref[...], ba_ref[...])
        cb = conv(b_ref[...], wb_ref[...], bb_ref[...])
        o_ref[...] = (ca * _sigmoid(ca) * cb).astype(o_ref.dtype)

    def col(off, rows):
        return pl.BlockSpec((rows, tc), lambda j: (0, j + off))

    return pl.pallas_call(
        body, grid=(nb,), in_specs=[col(0, L), col(nb, L), col(0, 3), col(nb, 3), col(0, 1), col(nb, 1)],
        out_specs=col(0, L), out_shape=jax.ShapeDtypeStruct((L, C), BF16),
        compiler_params=_params(dimension_semantics=("parallel",)), name=name,
    )(hu, hu, conv_w, conv_w, conv_b, conv_b)


def convgate_bwd(hu, conv_w, conv_b, dact, *, name):
    L, C2 = hu.shape
    C = C2 // 2
    tc = _pick(C, (256, 128))
    nb = C // tc

    def body(a_ref, b_ref, wa_ref, wb_ref, ba_ref, bb_ref, d_ref, dxa_ref, dxb_ref, dwa_ref, dwb_ref, dba_ref, dbb_ref):
        row = lax.broadcasted_iota(jnp.int32, (L, tc), 0)

        def conv(x, w, bias):
            x1 = _shift_down(x, 1, row)
            x2 = _shift_down(x, 2, row)
            return w[2:3, :] * x + w[1:2, :] * x1 + w[0:1, :] * x2 + bias, x1, x2

        xa, xb = a_ref[...], b_ref[...]
        wa, wb = wa_ref[...], wb_ref[...]
        ca, xa1, xa2 = conv(xa, wa, ba_ref[...])
        cb, xb1, xb2 = conv(xb, wb, bb_ref[...])
        d = d_ref[...]
        sa = _sigmoid(ca)
        dca = d * cb * (sa * (1.0 + ca * (1.0 - sa)))
        dcb = d * (ca * sa)

        def back(dc, w, x, x1, x2, dx_ref, dw_ref, db_ref):
            dx = w[2:3, :] * dc + w[1:2, :] * _shift_up(dc, 1, row) + w[0:1, :] * _shift_up(dc, 2, row)
            dx_ref[...] = dx.astype(dx_ref.dtype)
            dw_ref[...] = jnp.concatenate([jnp.sum(dc * x2, axis=0, keepdims=True),
                                           jnp.sum(dc * x1, axis=0, keepdims=True),
                                           jnp.sum(dc * x, axis=0, keepdims=True)], axis=0)
            db_ref[...] = jnp.sum(dc, axis=0, keepdims=True)

        back(dca, wa, xa, xa1, xa2, dxa_ref, dwa_ref, dba_ref)
        back(dcb, wb, xb, xb1, xb2, dxb_ref, dwb_ref, dbb_ref)

    def col(off, rows):
        return pl.BlockSpec((rows, tc), lambda j: (0, j + off))

    outs = pl.pallas_call(
        body, grid=(nb,),
        in_specs=[col(0, L), col(nb, L), col(0, 3), col(nb, 3), col(0, 1), col(nb, 1), col(0, L)],
        out_specs=[col(0, L), col(0, L), col(0, 3), col(0, 3), col(0, 1), col(0, 1)],
        out_shape=[jax.ShapeDtypeStruct((L, C), BF16)] * 2 + [jax.ShapeDtypeStruct((3, C), F32)] * 2
        + [jax.ShapeDtypeStruct((1, C), F32)] * 2,
        compiler_params=_params(dimension_semantics=("parallel",)), name=name,
    )(hu, hu, conv_w, conv_w, conv_b, conv_b, dact)
    dxa, dxb, dwa, dwb, dba, dbb = outs
    return (jnp.concatenate([dxa, dxb], axis=1), jnp.concatenate([dwa, dwb], axis=1),
            jnp.concatenate([dba, dbb], axis=1))


def rope_tables(positions):
    half = ROT_DIM // 2
    inv_freq = ROPE_THETA ** (-jnp.arange(half, dtype=F32) * 2.0 / ROT_DIM)
    ang = positions.astype(F32)[:, None] * inv_freq
    cos, sin = jnp.cos(ang), jnp.sin(ang)
    L = positions.shape[0]
    one = jnp.ones((L, ATT_E - ROT_DIM), F32)
    zero = jnp.zeros((L, ATT_E - ROT_DIM), F32)
    zh = jnp.zeros((L, half), F32)
    tc = jnp.concatenate([cos, cos, one], axis=1)
    ts1 = jnp.concatenate([zh, sin, zero], axis=1)
    ts2 = jnp.concatenate([-sin, zh, zero], axis=1)
    return tuple(jnp.concatenate([t, t], axis=1) for t in (tc, ts1, ts2))


def rope_fwd(qkv, tabs, *, name):
    L = qkv.shape[0]
    W = 512
    tr = _pick(L, (256, 128))
    nq = 1536 // W
    scale = ATT_E ** -0.5

    def body(x_ref, c_ref, s1_ref, s2_ref, *o_refs):
        c = jnp.concatenate([c_ref[...]] * 4, axis=1)
        s1 = jnp.concatenate([s1_ref[...]] * 4, axis=1)
        s2 = jnp.concatenate([s2_ref[...]] * 4, axis=1)
        for j, o_ref in enumerate(o_refs):
            x = x_ref[:, j * W:(j + 1) * W]
            if j < 2 * nq:
                x = x * c + pltpu.roll(x, 8, 1) * s1 + pltpu.roll(x, W - 8, 1) * s2
            if j < nq:
                x = x * scale
            o_ref[...] = x.astype(o_ref.dtype)

    slab = pl.BlockSpec((tr, W), lambda i: (i, 0))
    tab = pl.BlockSpec((tr, 128), lambda i: (i, 0))
    return pl.pallas_call(body, grid=(L // tr,), in_specs=[pl.BlockSpec((tr, 3 * nq * W), lambda i: (i, 0)), tab, tab, tab],
                          out_specs=[slab] * (3 * nq), out_shape=[jax.ShapeDtypeStruct((L, W), F32)] * (3 * nq),
                          compiler_params=_params(dimension_semantics=("parallel",)), name=name)(qkv, *tabs)


def rope_bwd(slabs, tabs, *, name):
    L, W = slabs[0].shape
    tr = _pick(L, (256, 128))
    nq = len(slabs) // 3
    scale = ATT_E ** -0.5

    def body(*refs):
        d_refs, (c_ref, s1_ref, s2_ref, o_ref) = refs[:3 * nq], refs[3 * nq:]
        c = jnp.concatenate([c_ref[...]] * 4, axis=1)
        s1 = jnp.concatenate([s1_ref[...]] * 4, axis=1)
        s2 = jnp.concatenate([s2_ref[...]] * 4, axis=1)
        for j, d_ref in enumerate(d_refs):
            dy = d_ref[...]
            if j < 2 * nq:
                dy = dy * c + pltpu.roll(dy * s1, W - 8, 1) + pltpu.roll(dy * s2, 8, 1)
            if j < nq:
                dy = dy * scale
            o_ref[:, j * W:(j + 1) * W] = dy.astype(o_ref.dtype)

    slab = pl.BlockSpec((tr, W), lambda i: (i, 0))
    tab = pl.BlockSpec((tr, 128), lambda i: (i, 0))
    return pl.pallas_call(body, grid=(L // tr,), in_specs=[slab] * (3 * nq) + [tab, tab, tab],
                          out_specs=pl.BlockSpec((tr, 3 * nq * W), lambda i: (i, 0)),
                          out_shape=jax.ShapeDtypeStruct((L, 3 * nq * W), BF16),
                          compiler_params=_params(dimension_semantics=("parallel",)), name=name)(*slabs, *tabs)


def _att_masks(has_prev):
    qi = lax.broadcasted_iota(jnp.int32, (ATT_BLOCK, ATT_BLOCK), 0)
    kj = lax.broadcasted_iota(jnp.int32, (ATT_BLOCK, ATT_BLOCK), 1)
    return qi >= kj, (kj >= qi) & has_prev


ATT_COLS = 128


def _att_rows(j, d, nb):
    B = ATT_BLOCK
    r, n = j // nb, j % nb
    start = r + d * B * n
    has_prev = n > 0
    pstart = jnp.where(has_prev, start - d * B, start)
    if d == 1:
        return pl.ds(pl.multiple_of(start, B), B), pl.ds(pl.multiple_of(pstart, B), B), has_prev
    return pl.ds(start, B, stride=d), pl.ds(pstart, B, stride=d), has_prev


def attn_fwd(q, k, v, d, *, name):
    L, W = q.shape
    B, E = ATT_BLOCK, ATT_E
    nblk = L // B
    nb = nblk // d

    def body(q_ref, k_ref, v_ref, o_ref, l_ref):
        def step(j, carry):
            cur, prv, has_prev = _att_rows(j, d, nb)
            mc, mp = _att_masks(has_prev)
            qb, kc, kp, vc, vp = q_ref[cur, :], k_ref[cur, :], k_ref[prv, :], v_ref[cur, :], v_ref[prv, :]
            outs, lses = [], []
            for h in range(ATT_COLS // E):
                sl = slice(h * E, (h + 1) * E)
                sc = jnp.where(mc, _dot_nt(qb[:, sl], kc[:, sl]), NEG_BIG)
                sp = jnp.where(mp, _dot_nt(qb[:, sl], kp[:, sl]), NEG_BIG)
                m = jnp.maximum(jnp.max(sc, axis=-1, keepdims=True), jnp.max(sp, axis=-1, keepdims=True))
                pc = jnp.exp(sc - m)
                pp = jnp.exp(sp - m)
                den = jnp.sum(pc, axis=-1, keepdims=True) + jnp.sum(pp, axis=-1, keepdims=True)
                outs.append((_dot(pc, vc[:, sl]) + _dot(pp, vp[:, sl])) / den)
                lses.append(jnp.broadcast_to(m + jnp.log(den), (B, E)))
            o_ref[cur, :] = jnp.concatenate(outs, axis=1)
            l_ref[cur, :] = jnp.concatenate(lses, axis=1)
            return carry

        lax.fori_loop(0, nblk, step, 0, unroll=2)

    col = pl.BlockSpec((L, ATT_COLS), lambda c: (0, c))
    return pl.pallas_call(body, grid=(W // ATT_COLS,), in_specs=[col] * 3, out_specs=[col] * 2,
                          out_shape=[jax.ShapeDtypeStruct((L, W), F32)] * 2,
                          compiler_params=_params(dimension_semantics=("parallel",)), name=name)(q, k, v)


def attn_bwd(q, k, v, lse, do, dl, d, *, name):
    L, W = q.shape
    B, E = ATT_BLOCK, ATT_E
    nblk = L // B
    nb = nblk // d

    def body(q_ref, k_ref, v_ref, l_ref, do_ref, dl_ref, dq_ref, dk_ref, dv_ref):
        dk_ref[...] = jnp.zeros_like(dk_ref)
        dv_ref[...] = jnp.zeros_like(dv_ref)

        def step(j, carry):
            cur, prv, has_prev = _att_rows(j, d, nb)
            mc, mp = _att_masks(has_prev)
            qb, kc, kp, vc, vp = q_ref[cur, :], k_ref[cur, :], k_ref[prv, :], v_ref[cur, :], v_ref[prv, :]
            lb, dob, dlb = l_ref[cur, :], do_ref[cur, :], dl_ref[cur, :]
            dqs, dkc, dkp, dvc, dvp = [], [], [], [], []
            for h in range(ATT_COLS // E):
                sl = slice(h * E, (h + 1) * E)
                qh, doh = qb[:, sl], dob[:, sl]
                lse_h, dl_h = lb[:, h * E:h * E + 1], dlb[:, h * E:h * E + 1]
                pc = jnp.where(mc, jnp.exp(_dot_nt(qh, kc[:, sl]) - lse_h), 0.0)
                pp = jnp.where(mp, jnp.exp(_dot_nt(qh, kp[:, sl]) - lse_h), 0.0)
                dsc = pc * (_dot_nt(doh, vc[:, sl]) - dl_h)
                dsp = pp * (_dot_nt(doh, vp[:, sl]) - dl_h)
                dqs.append(_dot(dsc, kc[:, sl]) + _dot(dsp, kp[:, sl]))
                dkc.append(_dot_tn(dsc, qh))
                dkp.append(_dot_tn(dsp, qh))
                dvc.append(_dot_tn(pc, doh))
                dvp.append(_dot_tn(pp, doh))
            dq_ref[cur, :] = jnp.concatenate(dqs, axis=1)
            dk_ref[cur, :] = dk_ref[cur, :] + jnp.concatenate(dkc, axis=1)
            dv_ref[cur, :] = dv_ref[cur, :] + jnp.concatenate(dvc, axis=1)
            dk_ref[prv, :] = dk_ref[prv, :] + jnp.concatenate(dkp, axis=1)
            dv_ref[prv, :] = dv_ref[prv, :] + jnp.concatenate(dvp, axis=1)
            return carry

        lax.fori_loop(0, nblk, step, 0)

    col = pl.BlockSpec((L, ATT_COLS), lambda c: (0, c))
    return pl.pallas_call(body, grid=(W // ATT_COLS,), in_specs=[col] * 6, out_specs=[col] * 3,
                          out_shape=[jax.ShapeDtypeStruct((L, W), F32)] * 3,
                          compiler_params=_params(dimension_semantics=("parallel",)), name=name)(q, k, v, lse, do, dl)


def _merge_alpha(l_refs):
    ls = [r[...] for r in l_refs]
    m = jnp.maximum(jnp.maximum(ls[0], ls[1]), ls[2])
    es = [jnp.exp(l - m) for l in ls]
    den = es[0] + es[1] + es[2]
    return [e / den for e in es]


def merge_fwd(os_, ls_, *, name):
    L, W = os_[0].shape
    tr = _pick(L, (256, 128))

    def body(o0, o1, o2, l0, l1, l2, out_ref):
        al = _merge_alpha((l0, l1, l2))
        out_ref[...] = (al[0] * o0[...] + al[1] * o1[...] + al[2] * o2[...]).astype(out_ref.dtype)

    row = pl.BlockSpec((tr, W), lambda i: (i, 0))
    return pl.pallas_call(body, grid=(L // tr,), in_specs=[row] * 6, out_specs=row,
                          out_shape=jax.ShapeDtypeStruct((L, W), BF16), name=name)(*os_, *ls_)


def merge_bwd(os_, ls_, do, *, name):
    L, W = do.shape
    tr = _pick(L, (256, 128))

    def body(o0, o1, o2, l0, l1, l2, do_ref, d0, d1, d2, e0, e1, e2):
        al = _merge_alpha((l0, l1, l2))
        dov = do_ref[...]
        r = lax.broadcasted_iota(jnp.int32, (W, W), 0) // ATT_E
        c = lax.broadcasted_iota(jnp.int32, (W, W), 1) // ATT_E
        ones_blk = (r == c).astype(F32)
        t = jnp.zeros_like(dov)
        for a, o in zip(al, (o0, o1, o2)):
            t = t + a * _dot_tri(ones_blk, dov * o[...], tri_left=False)
        for a, d_ref, e_ref in zip(al, (d0, d1, d2), (e0, e1, e2)):
            d_ref[...] = a * dov
            e_ref[...] = a * t

    row = pl.BlockSpec((tr, W), lambda i: (i, 0))
    return pl.pallas_call(body, grid=(L // tr,), in_specs=[row] * 7, out_specs=[row] * 6,
                          out_shape=[jax.ShapeDtypeStruct((L, W), F32)] * 6, name=name)(*os_, *ls_, do)


def _me_and_peers():
    x, y, c = lax.axis_index("x"), lax.axis_index("y"), lax.axis_index("c")
    peers = []
    for k in range(1, N_DEV):
        px = 1 - x if k & 4 else x
        py = 1 - y if k & 2 else y
        pc = 1 - c if k & 1 else c
        peers.append((px, py, pc))
    return (x, y, c), peers


def _index(dev):
    return 4 * dev[0] + 2 * dev[1] + dev[2]


def _hbm(a):
    return pltpu.with_memory_space_constraint(a, pltpu.HBM)


HBM_SPEC = pl.BlockSpec(memory_space=pltpu.HBM)
SEM_SPEC = pl.BlockSpec(memory_space=pltpu.SEMAPHORE)
DATAFLOW = pltpu.SideEffectType.DATAFLOW_SIDE_EFFECTING


def _remote(src_ref, land_ref, slotted, me, peer, src_is_mine, send_sem, recv_sem, k):
    sender, receiver = (me, peer) if src_is_mine else (peer, me)
    src = src_ref.at[_index(receiver)] if slotted else src_ref
    return pltpu.make_async_remote_copy(src_ref=src, dst_ref=land_ref.at[_index(sender)], send_sem=send_sem.at[k],
                                        recv_sem=recv_sem.at[k], device_id=peer, device_id_type=MESH_ID)


def copies_start(arrays, slotted, *, name):
    n = len(arrays)
    lands = [lax.empty(a.shape if slotted else (N_DEV,) + a.shape, a.dtype) for a in arrays]

    def body(*refs):
        x_refs, land_refs = refs[:n], refs[n:2 * n]
        send, recv = refs[2 * n:3 * n], refs[3 * n:4 * n]
        token = refs[-1]
        me, peers = _me_and_peers()
        for w in range(n):
            for k, peer in enumerate(peers):
                _remote(x_refs[w], land_refs[w], slotted, me, peer, True, send[w], recv[w], k).start()
            if not slotted:
                pltpu.make_async_copy(x_refs[w], land_refs[w].at[_index(me)], recv[w].at[N_DEV - 1]).start()
        token[...] = jnp.zeros_like(token)

    sem = pltpu.SemaphoreType.DMA((N_DEV,))
    out_shape = ([sem] * (2 * n) + [pltpu.HBM(a.shape, a.dtype) for a in arrays]
                 + [pltpu.HBM(l.shape, l.dtype) for l in lands] + [jax.ShapeDtypeStruct((8, 128), F32)])
    outs = pl.pallas_call(
        body, name=name, out_shape=out_shape, in_specs=[HBM_SPEC] * (2 * n),
        out_specs=[SEM_SPEC] * (2 * n) + [HBM_SPEC] * (2 * n) + [pl.BlockSpec(memory_space=pltpu.VMEM)],
        input_output_aliases={i: 2 * n + i for i in range(2 * n)},
        compiler_params=pltpu.CompilerParams(has_side_effects=DATAFLOW),
    )(*[_hbm(a) for a in arrays], *[_hbm(l) for l in lands])
    handles = [(outs[w], outs[n + w], outs[2 * n + w], outs[3 * n + w]) for w in range(n)]
    return handles, outs[-1]


def copies_wait(handle, slotted, after, *, name):
    send_sem, recv_sem, x_thru, land_thru = handle

    def body(x_ref, land_ref, send_ref, recv_ref, after_ref, x_out, land_out):
        me, peers = _me_and_peers()
        for k, peer in enumerate(peers):
            _remote(x_ref, land_ref, slotted, me, peer, True, send_ref, recv_ref, k).wait_send()
        for k, peer in enumerate(peers):
            _remote(x_ref, land_ref, slotted, me, peer, False, send_ref, recv_ref, k).wait_recv()
        if not slotted:
            pltpu.make_async_copy(x_ref, land_ref.at[_index(me)], recv_ref.at[N_DEV - 1]).wait()

    return pl.pallas_call(
        body, name=name, out_shape=(pltpu.HBM(x_thru.shape, x_thru.dtype), pltpu.HBM(land_thru.shape, land_thru.dtype)),
        in_specs=(HBM_SPEC, HBM_SPEC, SEM_SPEC, SEM_SPEC, pl.BlockSpec(memory_space=pl.ANY)),
        out_specs=(HBM_SPEC, HBM_SPEC), input_output_aliases={0: 0, 1: 1},
        compiler_params=pltpu.CompilerParams(has_side_effects=DATAFLOW),
    )(x_thru, land_thru, send_sem, recv_sem, after)


def cast_bf16(x, *, ncols=None, name):
    R = x.shape[0]
    C = ncols or x.shape[1]
    tr = _pick(R, (512, 352, 256, 128, 64))

    def body(x_ref, o_ref):
        o_ref[...] = x_ref[...].astype(BF16)

    row = pl.BlockSpec((tr, C), lambda i: (i, 0))
    return pl.pallas_call(body, grid=(R // tr,), in_specs=[row], out_specs=row,
                          out_shape=jax.ShapeDtypeStruct((R, C), BF16), name=name)(x)


def cast_bf16_layer(x3, layer, *, name):
    _, R, C = x3.shape
    tr = _pick(R, (512, 352, 256, 128, 64))

    def body(x_ref, o_ref):
        o_ref[...] = x_ref[...].astype(BF16)

    return pl.pallas_call(body, grid=(R // tr,), in_specs=[pl.BlockSpec((None, tr, C), lambda i: (layer, i, 0))],
                          out_specs=pl.BlockSpec((tr, C), lambda i: (i, 0)),
                          out_shape=jax.ShapeDtypeStruct((R, C), BF16), name=name)(x3)


BD_PARTS = 2


def _blockdiag_call(b, build, G, r, c, name):
    gp = G // BD_PARTS

    def body_build(b_ref, o_ref):
        o_ref[...] = jnp.zeros_like(o_ref)
        for g in range(G):
            o_ref[g // gp, (g % gp) * r:(g % gp + 1) * r, (g % gp) * c:(g % gp + 1) * c] = b_ref[g]

    def body_extract(d_ref, o_ref):
        for g in range(G):
            o_ref[g] = d_ref[g // gp, (g % gp) * r:(g % gp + 1) * r, (g % gp) * c:(g % gp + 1) * c]

    out = jax.ShapeDtypeStruct((BD_PARTS, gp * r, gp * c) if build else (G, r, c), F32)
    return pl.pallas_call(body_build if build else body_extract, out_shape=out, name=name)(b)


def make_blockdiag(G, r, c, name):
    @jax.custom_vjp
    def blockdiag(b):
        return _blockdiag_call(b, True, G, r, c, name + "_build")

    def fwd(b):
        return blockdiag(b), None

    def bwd(_, g):
        return (_blockdiag_call(g, False, G, r, c, name + "_extract"),)

    blockdiag.defvjp(fwd, bwd)
    return blockdiag


def _my_index():
    return 4 * lax.axis_index("x") + 2 * lax.axis_index("y") + lax.axis_index("c")


def cols_from_shards(g, *, name):
    _, K, n = g.shape
    tk = _pick(K, (256, 128))

    def body(g_ref, o_ref):
        for i in range(N_DEV):
            o_ref[:, i * n:(i + 1) * n] = g_ref[i]

    return pl.pallas_call(body, grid=(K // tk,), in_specs=[pl.BlockSpec((N_DEV, tk, n), lambda i: (0, i, 0))],
                          out_specs=pl.BlockSpec((tk, N_DEV * n), lambda i: (i, 0)),
                          out_shape=jax.ShapeDtypeStruct((K, N_DEV * n), g.dtype), name=name)(g)


def shards_from_cols(w, *, name):
    K, N = w.shape
    n = N // N_DEV
    tk = _pick(K, (256, 128))

    def body(w_ref, o_ref):
        for i in range(N_DEV):
            o_ref[i] = w_ref[:, i * n:(i + 1) * n].astype(o_ref.dtype)

    return pl.pallas_call(body, grid=(K // tk,), in_specs=[pl.BlockSpec((tk, N), lambda i: (i, 0))],
                          out_specs=pl.BlockSpec((N_DEV, tk, n), lambda i: (0, i, 0)),
                          out_shape=jax.ShapeDtypeStruct((N_DEV, K, n), BF16), name=name)(w)


def _adamw(w, g, m, v):
    m = ADAM_B1 * m + (1.0 - ADAM_B1) * g
    v = ADAM_B2 * v + (1.0 - ADAM_B2) * (g * g)
    m_hat = m / (1.0 - ADAM_B1 ** ADAM_STEP)
    v_hat = v / (1.0 - ADAM_B2 ** ADAM_STEP)
    delta = -ADAM_LR * (m_hat / (jnp.sqrt(v_hat) + ADAM_EPS) + ADAM_WD * w)
    return delta, m, v


def reduce_adamw(recv, own, own_slotted, me, w, m, v, *, layer=0, n_layers=1, into=None, name):
    _, R, C = recv.shape
    tr = _pick(R, (352, 320, 288, 256, 128, 64, 32, 16, 8))
    off = layer * (R // tr)

    def body(me_ref, r_ref, own_ref, w_ref, m_ref, v_ref, *rest):
        g_ref, d_ref, nm_ref, nv_ref = rest[-4:]
        mine = me_ref[0]
        g = None
        for i in range(N_DEV):
            part = jnp.where(mine == i, own_ref[...], r_ref[i]).astype(F32)
            g = part if g is None else g + part
        delta, nm, nv = _adamw(w_ref[...], g, m_ref[...], v_ref[...])
        g_ref[...] = g
        d_ref[...] = delta
        nm_ref[...] = nm
        nv_ref[...] = nv

    row = pl.BlockSpec((tr, C), lambda i, me_ref: (i + off, 0))
    own_spec = (pl.BlockSpec((None, tr, C), lambda i, me_ref: (me_ref[0], i, 0)) if own_slotted
                else pl.BlockSpec((tr, C), lambda i, me_ref: (i, 0)))
    rest = [] if into is None else list(into)
    grid_spec = pltpu.PrefetchScalarGridSpec(
        num_scalar_prefetch=1, grid=(R // tr,),
        in_specs=[pl.BlockSpec((N_DEV, tr, C), lambda i, me_ref: (0, i, 0)), own_spec, row, row, row]
        + [pl.BlockSpec(memory_space=pl.ANY)] * len(rest),
        out_specs=[row] * 4)
    return pl.pallas_call(body, grid_spec=grid_spec, out_shape=[jax.ShapeDtypeStruct((n_layers * R, C), F32)] * 4,
                          input_output_aliases={6 + k: k for k in range(len(rest))},
                          compiler_params=_params(dimension_semantics=("parallel",)), name=name)(
        me.reshape(1).astype(jnp.int32), recv, own, w, m, v, *rest)


def _s5_prepare(A_re, A_im, log_dt, B_re, B_im, C_re, C_im):
    G, P, Cg = S5_GROUPS, S5_STATE, S5_GROUP
    dt = jnp.exp(log_dt)[:, None]
    mag = jnp.exp(A_re * dt)
    ab_re = mag * jnp.cos(A_im * dt)
    ab_im = mag * jnp.sin(A_im * dt)
    den = A_re * A_re + A_im * A_im
    nr, ni = ab_re - 1.0, ab_im
    c_re = (nr * A_re + ni * A_im) / den
    c_im = (ni * A_re - nr * A_im) / den
    Bb_re = c_re[..., None] * B_re - c_im[..., None] * B_im
    Bb_im = c_re[..., None] * B_im + c_im[..., None] * B_re
    def dense_in(b, name):
        return make_blockdiag(G, Cg, P, name)(b.transpose(0, 2, 1))

    def dense_out(c, name):
        return make_blockdiag(G, P, Cg, name)(c.transpose(0, 2, 1))

    return (ab_re.reshape(1, G * P), ab_im.reshape(1, G * P), dense_in(Bb_re, "s5_wb_re"), dense_in(Bb_im, "s5_wb_im"),
            dense_out(C_re, "s5_wc_re"), dense_out(-C_im, "s5_wc_im"))


def _lower_bound(gamma):
    return jnp.cumsum(jax.nn.softmax(gamma, axis=0), axis=0)[0:1]


def _ffn_fwd(h, g_norm, get_w_in, conv_w, conv_b, get_w_out, tag):
    hn = rms_fwd(h, g_norm, name=tag + "_rms")
    w_in = get_w_in(hn)
    hu = mm(hn, w_in, tb=True, name=tag + "_in")
    act = convgate_fwd(hu, conv_w, conv_b, name=tag + "_gate")
    w_out = get_w_out(act)
    h_out = mm(act, w_out, res=h, name=tag + "_out")
    return h_out, (hn, hu, act), w_in, w_out


def _ffn_bwd(h, g_norm, w_in, conv_w, conv_b, w_out, saved, dh, tag, send_dw_in, send_dw_out):
    hn, hu, act = saved
    sent = send_dw_out(mm(act, dh, ta=True, out_dtype=BF16, name=tag + "_dwout"))
    dact = mm(dh, w_out, tb=True, dep=sent, name=tag + "_dact")
    dhu, dconv_w, dconv_b = convgate_bwd(hu, conv_w, conv_b, dact, name=tag + "_dgate")
    sent = send_dw_in(mm(dhu, hn, ta=True, out_dtype=BF16, name=tag + "_dwin"))
    dhn = mm(dhu, w_in, dep=sent, name=tag + "_dhn")
    dh_in, dg = rms_bwd(h, g_norm, dhn, dh, name=tag + "_drms")
    return dh_in, dg, dconv_w, dconv_b


def kernel(x, positions, norm_mix, norm_ffn, norm_final, mix_w_in, mix_w_out, s5_A_re, s5_A_im, s5_log_dt, s5_B_re, s5_B_im, s5_C_re, s5_C_im, s5_D, s5_glu_w, s5_glu_b, hgrn_gamma, hgrn_norm, att_w_qkv, att_w_o, ffn_w_in, ffn_conv_w, ffn_conv_b, ffn_w_out, loss_target, m_norm_mix, m_norm_ffn, m_norm_final, m_mix_w_in, m_mix_w_out, m_s5_A_re, m_s5_A_im, m_s5_log_dt, m_s5_B_re, m_s5_B_im, m_s5_C_re, m_s5_C_im, m_s5_D, m_s5_glu_w, m_s5_glu_b, m_hgrn_gamma, m_hgrn_norm, m_att_w_qkv, m_att_w_o, m_ffn_w_in, m_ffn_conv_w, m_ffn_conv_b, m_ffn_w_out, v_norm_mix, v_norm_ffn, v_norm_final, v_mix_w_in, v_mix_w_out, v_s5_A_re, v_s5_A_im, v_s5_log_dt, v_s5_B_re, v_s5_B_im, v_s5_C_re, v_s5_C_im, v_s5_D, v_s5_glu_w, v_s5_glu_b, v_hgrn_gamma, v_hgrn_norm, v_att_w_qkv, v_att_w_o, v_ffn_w_in, v_ffn_conv_w, v_ffn_conv_b, v_ffn_w_out):
    W = dict(norm_mix=norm_mix, norm_ffn=norm_ffn, norm_final=norm_final, mix_w_in=mix_w_in, mix_w_out=mix_w_out,
             s5_A_re=s5_A_re, s5_A_im=s5_A_im, s5_log_dt=s5_log_dt, s5_B_re=s5_B_re, s5_B_im=s5_B_im,
             s5_C_re=s5_C_re, s5_C_im=s5_C_im, s5_D=s5_D, s5_glu_w=s5_glu_w, s5_glu_b=s5_glu_b,
             hgrn_gamma=hgrn_gamma, hgrn_norm=hgrn_norm, att_w_qkv=att_w_qkv, att_w_o=att_w_o, ffn_w_in=ffn_w_in,
             ffn_conv_w=ffn_conv_w, ffn_conv_b=ffn_conv_b, ffn_w_out=ffn_w_out)
    M = dict(norm_mix=m_norm_mix, norm_ffn=m_norm_ffn, norm_final=m_norm_final, mix_w_in=m_mix_w_in,
             mix_w_out=m_mix_w_out, s5_A_re=m_s5_A_re, s5_A_im=m_s5_A_im, s5_log_dt=m_s5_log_dt, s5_B_re=m_s5_B_re,
             s5_B_im=m_s5_B_im, s5_C_re=m_s5_C_re, s5_C_im=m_s5_C_im, s5_D=m_s5_D, s5_glu_w=m_s5_glu_w,
             s5_glu_b=m_s5_glu_b, hgrn_gamma=m_hgrn_gamma, hgrn_norm=m_hgrn_norm, att_w_qkv=m_att_w_qkv,
             att_w_o=m_att_w_o, ffn_w_in=m_ffn_w_in, ffn_conv_w=m_ffn_conv_w, ffn_conv_b=m_ffn_conv_b,
             ffn_w_out=m_ffn_w_out)
    V = dict(norm_mix=v_norm_mix, norm_ffn=v_norm_ffn, norm_final=v_norm_final, mix_w_in=v_mix_w_in,
             mix_w_out=v_mix_w_out, s5_A_re=v_s5_A_re, s5_A_im=v_s5_A_im, s5_log_dt=v_s5_log_dt, s5_B_re=v_s5_B_re,
             s5_B_im=v_s5_B_im, s5_C_re=v_s5_C_re, s5_C_im=v_s5_C_im, s5_D=v_s5_D, s5_glu_w=v_s5_glu_w,
             s5_glu_b=v_s5_glu_b, hgrn_gamma=v_hgrn_gamma, hgrn_norm=v_hgrn_norm, att_w_qkv=v_att_w_qkv,
             att_w_o=v_att_w_o, ffn_w_in=v_ffn_w_in, ffn_conv_w=v_ffn_conv_w, ffn_conv_b=v_ffn_conv_b,
             ffn_w_out=v_ffn_w_out)
    return _step(x[0], positions[0], loss_target[0], W, M, V)


TRANSPOSED = ("mix_w_in", "att_w_qkv", "ffn_w_in")
SMALL = ("norm_mix", "norm_ffn", "norm_final", "s5_A_re", "s5_A_im", "s5_log_dt", "s5_B_re", "s5_B_im", "s5_C_re",
         "s5_C_im", "s5_D", "s5_glu_b", "hgrn_gamma", "hgrn_norm", "ffn_conv_b")
ORDER = ("norm_mix", "norm_ffn", "norm_final", "mix_w_in", "mix_w_out", "s5_A_re", "s5_A_im", "s5_log_dt", "s5_B_re",
         "s5_B_im", "s5_C_re", "s5_C_im", "s5_D", "s5_glu_w", "s5_glu_b", "hgrn_gamma", "hgrn_norm", "att_w_qkv",
         "att_w_o", "ffn_w_in", "ffn_conv_w", "ffn_conv_b", "ffn_w_out")
PACK_COLS = 1024


def _step(x, positions, target, W, M, V):
    L, D = x.shape
    me = 4 * lax.axis_index("x") + 2 * lax.axis_index("y") + lax.axis_index("c")
    n_cw = W["ffn_conv_w"].shape[-1]
    T = {n: tuple(jnp.swapaxes(d[n], -1, -2) for d in (W, M, V)) for n in TRANSPOSED}
    shards = {
        "mix_w_in": cast_bf16(T["mix_w_in"][0][0], name="mix_w_in_cast"),
        "conv_w": W["ffn_conv_w"].reshape(6, n_cw),
        "s5_glu_w": cast_bf16(W["s5_glu_w"][0], name="s5_glu_w_cast"),
        "mix_w_out": cast_bf16(W["mix_w_out"][0], name="mix_w_out_cast"),
        "ffn_w_in0": cast_bf16_layer(T["ffn_w_in"][0], 0, name="ffn_w_in0_cast"),
        "ffn_w_out0": cast_bf16_layer(W["ffn_w_out"], 0, name="ffn_w_out0_cast"),
        "att_w_qkv": cast_bf16(T["att_w_qkv"][0][0], name="att_w_qkv_cast"),
        "att_w_o": cast_bf16(W["att_w_o"][0], name="att_w_o_cast"),
        "ffn_w_in1": cast_bf16_layer(T["ffn_w_in"][0], 1, name="ffn_w_in1_cast"),
        "ffn_w_out1": cast_bf16_layer(W["ffn_w_out"], 1, name="ffn_w_out1_cast"),
    }
    gather_handles, token = copies_start(list(shards.values()), False, name="gather_start")
    gather_handle = dict(zip(shards, gather_handles))

    def gathered(key, after, cols):
        _, land = copies_wait(gather_handle[key], False, after, name=key + "_gwait")
        return cols_from_shards(land, name=key + "_asm") if cols else land.reshape(-1, land.shape[-1])

    conv_b = W["ffn_conv_b"].reshape(2, 1, -1)

    s5_params = (W["s5_A_re"][0], W["s5_A_im"][0], W["s5_log_dt"][0], W["s5_B_re"][0], W["s5_B_im"][0],
                 W["s5_C_re"][0], W["s5_C_im"][0])
    (a_re, a_im, wb_re, wb_im, wc_re, wc_im), s5_prep_vjp = jax.vjp(_s5_prepare, *s5_params)
    dvec = W["s5_D"].reshape(1, S5_WIDTH)
    glu_b = W["s5_glu_b"].reshape(1, S5_WIDTH)
    lb, lb_vjp = jax.vjp(_lower_bound, W["hgrn_gamma"])
    hg_norm = W["hgrn_norm"].reshape(1, -1)
    tabs = rope_tables(positions)

    hn0 = rms_fwd(x, W["norm_mix"][0], dep=token, name="l0_rms")
    w_mix_in = gathered("mix_w_in", hn0, False)
    proj = mm(hn0, w_mix_in, tb=True, name="l0_proj")
    u_bf = cast_bf16(proj, ncols=S5_WIDTH, name="l0_u_cast")
    bu_re = mm_parts(u_bf, wb_re, name="s5_bu_re")
    bu_im = mm_parts(u_bf, wb_im, name="s5_bu_im")
    xs_re, xs_im = s5_scan_fwd(a_re, a_im, bu_re, bu_im, name="s5_scan")
    y0 = mm_parts(xs_im, wc_im, res=mm_parts(xs_re, wc_re, name="s5_y_re"), name="s5_y_im")
    w_glu = gathered("s5_glu_w", y0, False)
    oa = s5_out_fwd(y0, proj, dvec, w_glu, glu_b, name="s5_out")
    ob, hg_states = hgrn_fwd(proj, lb, hg_norm, name="hgrn_fwd")
    cat = jnp.concatenate([oa, ob], axis=1)
    w_mix_out = gathered("mix_w_out", cat, False)
    h1 = mm(cat, w_mix_out, res=x, name="l0_mix_out")
    _, cw_all = copies_wait(gather_handle["conv_w"], False, h1, name="conv_w_gwait")
    conv_w = cw_all.transpose(1, 0, 2).reshape(2, 3, N_DEV * n_cw)
    w_ffn_in, w_ffn_out = [None, None], [None, None]
    h2, ffn0_saved, w_ffn_in[0], w_ffn_out[0] = _ffn_fwd(
        h1, W["norm_ffn"][0], lambda a: gathered("ffn_w_in0", a, False), conv_w[0], conv_b[0],
        lambda a: gathered("ffn_w_out0", a, False), "ffn0")

    hn2 = rms_fwd(h2, W["norm_mix"][1], name="l1_rms")
    w_qkv = gathered("att_w_qkv", hn2, False)
    qkv = mm(hn2, w_qkv, tb=True, name="l1_qkv")
    qkv_r = rope_fwd(qkv, tabs, name="rope_fwd")
    att_o, att_l = [], []
    for g, d in enumerate(ATT_DILATIONS):
        o_g, l_g = attn_fwd(qkv_r[g], qkv_r[3 + g], qkv_r[6 + g], d, name=f"attn_fwd{g}")
        att_o.append(o_g)
        att_l.append(l_g)
    o_att = merge_fwd(att_o, att_l, name="merge_fwd")
    w_o = gathered("att_w_o", o_att, True)
    h3 = mm(o_att, w_o, res=h2, name="l1_mix_out")
    h4, ffn1_saved, w_ffn_in[1], w_ffn_out[1] = _ffn_fwd(
        h3, W["norm_ffn"][1], lambda a: gathered("ffn_w_in1", a, False), conv_w[1], conv_b[1],
        lambda a: gathered("ffn_w_out1", a, False), "ffn1")

    exchanges = {}

    def send_grad(key, g, cols):
        if cols:
            parts = shards_from_cols(g, name=key + "_split")
        else:
            parts = g.reshape(N_DEV, g.shape[0] // N_DEV, g.shape[1])
        (handle,), sent = copies_start([parts], True, name=key + "_xstart")
        exchanges[key] = handle
        return sent

    loss, dh4, dg_final = final_loss(h4, W["norm_final"], target, name="final_loss")
    dh3, dg_ffn1, dcw1, dcb1 = _ffn_bwd(h3, W["norm_ffn"][1], w_ffn_in[1], conv_w[1], conv_b[1], w_ffn_out[1],
                                        ffn1_saved, dh4, "ffn1", lambda g: send_grad("ffn_w_in1", g, False),
                                        lambda g: send_grad("ffn_w_out1", g, False))
    sent = send_grad("att_w_o", mm(o_att, dh3, ta=True, name="l1_dwo"), True)
    d_oatt = mm(dh3, w_o, tb=True, dep=sent, name="l1_dmix")
    mb = merge_bwd(att_o, att_l, d_oatt, name="merge_bwd")
    d_slabs = [attn_bwd(qkv_r[g], qkv_r[3 + g], qkv_r[6 + g], att_l[g], mb[g], mb[3 + g], d, name=f"attn_bwd{g}")
               for g, d in enumerate(ATT_DILATIONS)]
    d_qkv = rope_bwd([s[0] for s in d_slabs] + [s[1] for s in d_slabs] + [s[2] for s in d_slabs], tabs,
                     name="rope_bwd")
    sent = send_grad("att_w_qkv", mm(d_qkv, hn2, ta=True, out_dtype=BF16, name="l1_dwqkv"), False)
    d_hn2 = mm(d_qkv, w_qkv, dep=sent, name="l1_dhn")
    dh2, dg_mix1 = rms_bwd(h2, W["norm_mix"][1], d_hn2, dh3, name="l1_drms")

    dh1, dg_ffn0, dcw0, dcb0 = _ffn_bwd(h1, W["norm_ffn"][0], w_ffn_in[0], conv_w[0], conv_b[0], w_ffn_out[0],
                                        ffn0_saved, dh2, "ffn0", lambda g: send_grad("ffn_w_in0", g, False),
                                        lambda g: send_grad("ffn_w_out0", g, False))
    sent = send_grad("mix_w_out", mm(cat, dh1, ta=True, out_dtype=BF16, name="l0_dwout"), False)
    dcat = mm(dh1, w_mix_out, tb=True, dep=sent, name="l0_dcat")
    d_hg, dlb, dhg_norm = hgrn_bwd(proj, lb, hg_norm, hg_states, dcat, name="hgrn_bwd")
    dy, du_d, z_bf, dzg, dglu_b, dD = s5_out_bwd(y0, proj, dvec, w_glu, glu_b, dcat, name="s5_dout")
    sent = send_grad("s5_glu_w", mm(z_bf, dzg, ta=True, out_dtype=BF16, name="s5_dglu"), False)
    dxs_re = mm_parts(dy, wc_re, tb=True, dep=sent, name="s5_dxs_re")
    dxs_im = mm_parts(dy, wc_im, tb=True, name="s5_dxs_im")
    dwc_re = mm_parts(xs_re, dy, ta=True, name="s5_dwc_re")
    dwc_im = mm_parts(xs_im, dy, ta=True, name="s5_dwc_im")
    dbu_re, dbu_im, da_re, da_im = s5_scan_bwd(a_re, a_im, xs_re, xs_im, dxs_re, dxs_im, name="s5_dscan")
    du = mm_parts(dbu_im, wb_im, tb=True, res=mm_parts(dbu_re, wb_re, tb=True, res=du_d, name="s5_du_re"),
                  out_dtype=BF16, name="s5_du_im")
    dwb_re = mm_parts(u_bf, dbu_re, ta=True, name="s5_dwb_re")
    dwb_im = mm_parts(u_bf, dbu_im, ta=True, name="s5_dwb_im")
    s5_small = s5_prep_vjp((da_re, da_im, dwb_re, dwb_im, dwc_re, dwc_im))
    d_proj = jnp.concatenate([du, d_hg], axis=1)
    sent = send_grad("mix_w_in", mm(d_proj, hn0, ta=True, out_dtype=BF16, name="l0_dwin"), False)
    d_hn0 = mm(d_proj, w_mix_in, dep=sent, name="l0_dhn")
    grad_x, dg_mix0 = rms_bwd(x, W["norm_mix"][0], d_hn0, dh1, name="l0_drms")
    (d_gamma,) = lb_vjp(dlb)
    out = {}

    dA_re, dA_im, dlog_dt, dB_re, dB_im, dC_re, dC_im = s5_small
    small_g = dict(norm_mix=jnp.concatenate([dg_mix0, dg_mix1], axis=0), norm_ffn=jnp.concatenate([dg_ffn0, dg_ffn1], axis=0),
                   norm_final=dg_final, s5_A_re=dA_re, s5_A_im=dA_im, s5_log_dt=dlog_dt, s5_B_re=dB_re, s5_B_im=dB_im,
                   s5_C_re=dC_re, s5_C_im=dC_im, s5_D=dD, s5_glu_b=dglu_b, hgrn_gamma=d_gamma, hgrn_norm=dhg_norm,
                   ffn_conv_b=jnp.concatenate([dcb0, dcb1], axis=0))
    conv_w_g = jnp.stack([dcw0, dcw1], axis=0)
    sizes = [math.prod(W[n].shape) for n in SMALL]
    n_conv = conv_w_g.size
    total = sum(sizes) + n_conv + 1
    rows = -(-total // PACK_COLS)
    rows = -(-rows // 8) * 8
    pad = rows * PACK_COLS - total

    def pack(vals, conv_part, last):
        flat = [v.reshape(-1).astype(F32) for v in vals] + [conv_part.reshape(-1), last.reshape(-1),
                                                            jnp.zeros((pad,), F32)]
        return jnp.concatenate(flat).reshape(rows, PACK_COLS)

    def conv_full(shard):
        col_owner = lax.broadcasted_iota(jnp.int32, (2, 3, N_DEV * n_cw), 2) // n_cw
        return jnp.where(col_owner == me, jnp.tile(shard, (1, 1, N_DEV)), 0.0)

    zero1 = jnp.zeros((1,), F32)
    g_pack = pack([small_g[n] for n in SMALL], conv_w_g, loss)
    w_pack = pack([W[n] for n in SMALL], conv_full(W["ffn_conv_w"]), zero1)
    m_pack = pack([M[n] for n in SMALL], conv_full(M["ffn_conv_w"]), zero1)
    v_pack = pack([V[n] for n in SMALL], conv_full(V["ffn_conv_w"]), zero1 + 1.0)
    (small_handle,), small_sent = copies_start([g_pack], False, name="small_xstart")

    def finish(name, n_layers):
        w3, m3, v3 = T[name] if name in TRANSPOSED else (W[name], M[name], V[name])
        res = None
        for layer in reversed(range(n_layers)):
            key = name if n_layers == 1 else f"{name}{layer}"
            own, recv = copies_wait(exchanges[key], True, small_sent, name=key + "_xwait")
            _, R, Cn = recv.shape
            res = reduce_adamw(recv, own, True, me, w3.reshape(n_layers * R, Cn), m3.reshape(n_layers * R, Cn),
                               v3.reshape(n_layers * R, Cn), layer=layer, n_layers=n_layers, into=res,
                               name=key + "_adamw")
        res = [r.reshape(w3.shape) for r in res]
        return tuple(jnp.swapaxes(r, -1, -2) for r in res) if name in TRANSPOSED else tuple(res)

    for name in ("ffn_w_out", "ffn_w_in"):
        out[name] = finish(name, 2)
    for name in ("att_w_o", "att_w_qkv", "mix_w_out", "s5_glu_w", "mix_w_in"):
        out[name] = finish(name, 1)

    small_own, small_recv = copies_wait(small_handle, False, out["s5_glu_w"][0], name="small_xwait")
    res = reduce_adamw(small_recv, small_own, False, me, w_pack, m_pack, v_pack, name="small_adamw")
    flat = [r.reshape(-1) for r in res]
    off = 0
    for n, sz in zip(SMALL, sizes):
        out[n] = tuple(f[off:off + sz].reshape(W[n].shape) for f in flat)
        off += sz
    conv_res = [f[off:off + n_conv].reshape(2, 3, N_DEV * n_cw) for f in flat]
    out["ffn_conv_w"] = tuple(lax.dynamic_slice(c, (0, 0, me * n_cw), (2, 3, n_cw)) for c in conv_res)
    off += n_conv
    loss_total = flat[0][off]

    result = [loss_total, grad_x[None]]
    for k in range(4):
        result += [out[n][k] for n in ORDER]
    return tuple(result)
```

```python
import functools
import math

import jax
import jax.numpy as jnp
from jax import lax
from jax.experimental import pallas as pl
from jax.experimental.pallas import tpu as pltpu

F32 = jnp.float32
BF16 = jnp.bfloat16
MESH_ID = pl.DeviceIdType.MESH
N_DEV = 8
VMEM_LIMIT_BYTES = 56 * 1024 * 1024

NORM_EPS = 1e-6
S5_WIDTH, S5_GROUP, S5_GROUPS, S5_STATE = 512, 16, 32, 64
HG_HEADS, HG_DIM, HG_CHUNK = 4, 128, 64
ATT_E, ATT_HPG, ATT_BLOCK = 64, 8, 128
ATT_DILATIONS = (1, 4, 16)
ROT_DIM, ROPE_THETA = 16, 500000.0
D_FF = 2816
ADAM_LR, ADAM_B1, ADAM_B2, ADAM_EPS, ADAM_WD, ADAM_STEP = 0.001, 0.9, 0.999, 1e-08, 0.01, 10
NEG_BIG = -1e30


def _params(**kw):
    return pltpu.CompilerParams(vmem_limit_bytes=VMEM_LIMIT_BYTES, **kw)


def _pick(n, cands):
    for c in cands:
        if n % c == 0:
            return c
    return n


def _dot(a, b):
    return jnp.dot(a.astype(BF16), b.astype(BF16), preferred_element_type=F32)


def _dot_nt(a, b):
    return lax.dot_general(a.astype(BF16), b.astype(BF16), (((1,), (1,)), ((), ())), preferred_element_type=F32)


def _dot_tn(a, b):
    return lax.dot_general(a.astype(BF16), b.astype(BF16), (((0,), (0,)), ((), ())), preferred_element_type=F32)


def _split2(x):
    hi = x.astype(BF16)
    return hi, (x - hi.astype(F32)).astype(BF16)


def _dot_x3(a, b, contract=((1,), (0,))):
    dn = (contract, ((), ()))
    a1, a2 = _split2(a)
    b1, b2 = _split2(b)
    return (lax.dot_general(a1, b1, dn, preferred_element_type=F32) + lax.dot_general(a1, b2, dn, preferred_element_type=F32)
            + lax.dot_general(a2, b1, dn, preferred_element_type=F32))


def _sigmoid(x):
    return 1.0 / (1.0 + jnp.exp(-x))


V7X_HBM_BYTES_PER_S = 3.2e12
V7X_MXU_FLOPS_PER_S = 0.7e15
GRID_STEP_S = 0.35e-6
MM_VMEM_BUDGET = 40 * 1024 * 1024


def _divisors(n, cands):
    return [c for c in cands if c <= n and n % c == 0] or [n]


def _mm_tiles(m, n, k, sa, sb, so, sr):
    best = None
    for tm in _divisors(m, (2816, 2048, 1408, 1024, 512, 256, 128)):
        for tn in _divisors(n, (2816, 2048, 1408, 1024, 512, 256, 128)):
            for tk in _divisors(k, (k, 2816, 2560, 2304, 2048, 1536, 1408, 1280, 1024, 512, 256, 128)):
                nk = k // tk
                vmem = 2 * (tm * tk * sa + tk * tn * sb + tm * tn * (so + sr)) + (tm * tn * 4 if nk > 1 else 0)
                vmem += tm * tk * 2 * (sa > 2) + tk * tn * 2 * (sb > 2) + tm * tn * 4
                if vmem > MM_VMEM_BUDGET:
                    continue
                ni, nj = m // tm, n // tn
                for i_outer in (True, False):
                    if i_outer:
                        a_reads = 1 if nk == 1 else nj
                        b_reads = 1 if (nk == 1 and nj == 1) else ni
                    else:
                        b_reads = 1 if nk == 1 else ni
                        a_reads = 1 if (nk == 1 and ni == 1) else nj
                    traffic = a_reads * m * k * sa + b_reads * k * n * sb + m * n * (so + sr)
                    t = max(traffic / V7X_HBM_BYTES_PER_S, 2.0 * m * n * k / V7X_MXU_FLOPS_PER_S)
                    t += ni * nj * nk * GRID_STEP_S
                    t += (tm * tk * sa + tk * tn * sb + tm * tn * so) / V7X_HBM_BYTES_PER_S
                    if best is None or t < best[0]:
                        best = (t, tm, tn, tk, i_outer)
    assert best is not None, (m, n, k)
    return best[1:]


def mm(a, b, *, ta=False, tb=False, res=None, out_dtype=F32, dep=None, name):
    m, k = (a.shape[1], a.shape[0]) if ta else a.shape
    n = b.shape[0] if tb else b.shape[1]
    assert (b.shape[1] if tb else b.shape[0]) == k
    has_res = res is not None
    tm, tn, tk, i_outer = _mm_tiles(m, n, k, a.dtype.itemsize, b.dtype.itemsize, jnp.dtype(out_dtype).itemsize,
                                    res.dtype.itemsize if has_res else 0)
    nk = k // tk
    deps = [] if dep is None else [dep]
    dn = (((0 if ta else 1,), (1 if tb else 0,)), ((), ()))

    def body_single(*refs):
        a_ref, b_ref = refs[:2]
        o_ref = refs[-1]
        out = lax.dot_general(a_ref[...].astype(BF16), b_ref[...].astype(BF16), dn, preferred_element_type=F32)
        if has_res:
            out = out + refs[2][...].astype(F32)
        o_ref[...] = out.astype(o_ref.dtype)

    def body(*refs):
        a_ref, b_ref = refs[:2]
        r_ref = refs[2] if has_res else None
        o_ref, acc_ref = refs[-2:]
        kk = pl.program_id(2)
        part = lax.dot_general(a_ref[...].astype(BF16), b_ref[...].astype(BF16), dn, preferred_element_type=F32)

        @pl.when(kk == 0)
        def _():
            acc_ref[...] = part

        @pl.when(kk > 0)
        def _():
            acc_ref[...] += part

        @pl.when(kk == nk - 1)
        def _():
            out = acc_ref[...]
            if has_res:
                out = out + r_ref[...].astype(F32)
            o_ref[...] = out.astype(o_ref.dtype)

    def ij(f):
        return (lambda g0, g1, q: f(g0, g1, q)) if i_outer else (lambda g0, g1, q: f(g1, g0, q))

    a_spec = pl.BlockSpec((tk, tm), ij(lambda i, j, q: (q, i))) if ta else pl.BlockSpec((tm, tk), ij(lambda i, j, q: (i, q)))
    b_spec = pl.BlockSpec((tn, tk), ij(lambda i, j, q: (j, q))) if tb else pl.BlockSpec((tk, tn), ij(lambda i, j, q: (q, j)))
    o_spec = pl.BlockSpec((tm, tn), ij(lambda i, j, q: (i, j)))
    in_specs = [a_spec, b_spec] + ([o_spec] if has_res else []) + [pl.BlockSpec((8, 128), lambda g0, g1, q: (0, 0))] * len(deps)
    args = (a, b) + ((res,) if has_res else ()) + tuple(deps)
    grid = (m // tm, n // tn, nk) if i_outer else (n // tn, m // tm, nk)
    return pl.pallas_call(
        body_single if nk == 1 else body, grid=grid, in_specs=in_specs, out_specs=o_spec,
        out_shape=jax.ShapeDtypeStruct((m, n), out_dtype),
        scratch_shapes=[] if nk == 1 else [pltpu.VMEM((tm, tn), F32)],
        compiler_params=_params(dimension_semantics=("parallel", "parallel", "arbitrary")), name=name,
    )(*args)


def rms_fwd(x, g, *, dep=None, name):
    L, D = x.shape
    tr = _pick(L, (256, 128))

    def body(x_ref, g_ref, *rest):
        o_ref = rest[-1]
        xv = x_ref[...]
        r = lax.rsqrt(jnp.mean(xv * xv, axis=-1, keepdims=True) + NORM_EPS)
        o_ref[...] = (xv * r * g_ref[...]).astype(o_ref.dtype)

    row = pl.BlockSpec((tr, D), lambda i: (i, 0))
    vec = pl.BlockSpec((1, D), lambda i: (0, 0))
    deps = [] if dep is None else [dep]
    return pl.pallas_call(body, grid=(L // tr,), in_specs=[row, vec] + [pl.BlockSpec((8, 128), lambda i: (0, 0))] * len(deps),
                          out_specs=row, out_shape=jax.ShapeDtypeStruct((L, D), BF16), name=name)(
        x, g.reshape(1, D), *deps)


def rms_bwd(x, g, dy, dres, *, name):
    L, D = x.shape
    tr = _pick(L, (256, 128))

    def body(x_ref, g_ref, dy_ref, dres_ref, dx_ref, dg_ref):
        xv = x_ref[...]
        r = lax.rsqrt(jnp.mean(xv * xv, axis=-1, keepdims=True) + NORM_EPS)
        xh = xv * r
        dyv = dy_ref[...].astype(F32)

        @pl.when(pl.program_id(0) == 0)
        def _():
            dg_ref[...] = jnp.zeros_like(dg_ref)

        dg_ref[...] += jnp.sum(dyv * xh, axis=0, keepdims=True)
        dxh = dyv * g_ref[...]
        dx_ref[...] = dres_ref[...] + r * (dxh - xh * jnp.mean(dxh * xh, axis=-1, keepdims=True))

    row = pl.BlockSpec((tr, D), lambda i: (i, 0))
    vec = pl.BlockSpec((1, D), lambda i: (0, 0))
    return pl.pallas_call(body, grid=(L // tr,), in_specs=[row, vec, row, row], out_specs=[row, vec],
                          out_shape=[jax.ShapeDtypeStruct((L, D), F32), jax.ShapeDtypeStruct((1, D), F32)],
                          compiler_params=_params(dimension_semantics=("arbitrary",)), name=name)(
        x, g.reshape(1, D), dy, dres)


def final_loss(h, g, target, *, name):
    L, D = h.shape
    tr = _pick(L, (256, 128))

    def body(x_ref, g_ref, t_ref, loss_ref, dx_ref, dg_ref):
        xv = x_ref[...]
        gv = g_ref[...]
        r = lax.rsqrt(jnp.mean(xv * xv, axis=-1, keepdims=True) + NORM_EPS)
        xh = xv * r
        err = xh * gv - t_ref[...]

        @pl.when(pl.program_id(0) == 0)
        def _():
            dg_ref[...] = jnp.zeros_like(dg_ref)
            loss_ref[...] = jnp.zeros_like(loss_ref)

        loss_ref[...] += 0.5 * jnp.sum(jnp.mean(err * err, axis=-1, keepdims=True), axis=0, keepdims=True)
        dyv = err * (1.0 / D)
        dg_ref[...] += jnp.sum(dyv * xh, axis=0, keepdims=True)
        dxh = dyv * gv
        dx_ref[...] = r * (dxh - xh * jnp.mean(dxh * xh, axis=-1, keepdims=True))

    row = pl.BlockSpec((tr, D), lambda i: (i, 0))
    vec = pl.BlockSpec((1, D), lambda i: (0, 0))
    one = pl.BlockSpec((1, 1), lambda i: (0, 0))
    return pl.pallas_call(body, grid=(L // tr,), in_specs=[row, vec, row], out_specs=[one, row, vec],
                          out_shape=[jax.ShapeDtypeStruct((1, 1), F32), jax.ShapeDtypeStruct((L, D), F32),
                                     jax.ShapeDtypeStruct((1, D), F32)],
                          compiler_params=_params(dimension_semantics=("arbitrary",)), name=name)(
        h, g.reshape(1, D), target)


def _cmul(ar, ai, br, bi):
    return ar * br - ai * bi, ar * bi + ai * br


def _powers(ar, ai):
    rows = [(ar, ai)]
    for _ in range(7):
        rows.append(_cmul(rows[-1][0], rows[-1][1], ar, ai))
    table = (jnp.concatenate([r[0] for r in rows], axis=0), jnp.concatenate([r[1] for r in rows], axis=0))
    return (rows[0], rows[1], rows[3]), table


def _block_scan(br, bi, steps, shift):
    yr, yi = br, bi
    for s, (pr, pi) in zip((1, 2, 4), steps):
        sr, si = shift(yr, s), shift(yi, s)
        yr, yi = yr + pr * sr - pi * si, yi + pr * si + pi * sr
    return yr, yi


def s5_core_fwd(proj, a_re, a_im, wb_re, wb_im, wc_re, wc_im, *, name):
    L = proj.shape[0]
    parts, cu, W = wb_re.shape

    def body(u_ref, ar_ref, ai_ref, wbr_ref, wbi_ref, wcr_ref, wci_ref, y_ref, xr_ref, xi_ref, br_ref, bi_ref):
        u = u_ref[...]
        br_ref[...] = _dot(u, wbr_ref[...])
        bi_ref[...] = _dot(u, wbi_ref[...])
        steps, (tr, ti) = _powers(ar_ref[...], ai_ref[...])
        row = lax.broadcasted_iota(jnp.int32, (8, W), 0)

        def shift(y, s):
            return jnp.where(row >= s, pltpu.roll(y, s, 0), 0.0)

        def step(t8, carry):
            cr, ci = carry
            base = pl.multiple_of(t8 * 8, 8)
            yr, yi = _block_scan(br_ref[pl.ds(base, 8), :], bi_ref[pl.ds(base, 8), :], steps, shift)
            xr = yr + tr * cr - ti * ci
            xi = yi + tr * ci + ti * cr
            xr_ref[pl.ds(base, 8), :] = xr
            xi_ref[pl.ds(base, 8), :] = xi
            return jnp.broadcast_to(xr[7:8, :], (8, W)), jnp.broadcast_to(xi[7:8, :], (8, W))

        zero = jnp.zeros((8, W), F32)
        lax.fori_loop(0, L // 8, step, (zero, zero), unroll=2)
        y_ref[...] = _dot(xr_ref[...], wcr_ref[...]) + _dot(xi_ref[...], wci_ref[...])

    ucol = pl.BlockSpec((L, cu), lambda t: (0, t))
    vec = pl.BlockSpec((1, W), lambda t: (0, t))
    col = pl.BlockSpec((L, W), lambda t: (0, t))
    wb = pl.BlockSpec((None, cu, W), lambda t: (t, 0, 0))
    wc = pl.BlockSpec((None, W, cu), lambda t: (t, 0, 0))
    return pl.pallas_call(body, grid=(parts,), in_specs=[ucol, vec, vec, wb, wb, wc, wc], out_specs=[ucol, col, col],
                          out_shape=[jax.ShapeDtypeStruct((L, parts * cu), F32)]
                          + [jax.ShapeDtypeStruct((L, parts * W), F32)] * 2,
                          scratch_shapes=[pltpu.VMEM((L, W), F32)] * 2,
                          compiler_params=_params(dimension_semantics=("parallel",)), name=name)(
        proj, a_re, a_im, wb_re, wb_im, wc_re, wc_im)


def s5_core_bwd(dy, du_d, proj, xs_re, xs_im, a_re, a_im, wb_re, wb_im, wc_re, wc_im, *, name):
    L = proj.shape[0]
    parts, cu, W = wb_re.shape

    def body(dy_ref, dud_ref, u_ref, xr_ref, xi_ref, ar_ref, ai_ref, wbr_ref, wbi_ref, wcr_ref, wci_ref,
             du_ref, dwbr_ref, dwbi_ref, dwcr_ref, dwci_ref, dar_ref, dai_ref, lr_ref, li_ref):
        dy = dy_ref[...]
        lr_ref[...] = _dot_nt(dy, wcr_ref[...])
        li_ref[...] = _dot_nt(dy, wci_ref[...])
        dwcr_ref[...] = _dot_tn(xr_ref[...], dy)
        dwci_ref[...] = _dot_tn(xi_ref[...], dy)
        ar, ai = ar_ref[...], -ai_ref[...]
        steps, (tr, ti) = _powers(ar, ai)
        tr = jnp.concatenate([tr[j:j + 1, :] for j in range(7, -1, -1)], axis=0)
        ti = jnp.concatenate([ti[j:j + 1, :] for j in range(7, -1, -1)], axis=0)
        row8 = lax.broadcasted_iota(jnp.int32, (8, W), 0)
        nblk = L // 8

        def shift(y, s):
            return jnp.where(row8 < 8 - s, pltpu.roll(y, 8 - s, 0), 0.0)

        def step(s, carry):
            cr, ci = carry
            base = pl.multiple_of((nblk - 1 - s) * 8, 8)
            yr, yi = _block_scan(lr_ref[pl.ds(base, 8), :], li_ref[pl.ds(base, 8), :], steps, shift)
            lr = yr + tr * cr - ti * ci
            li = yi + tr * ci + ti * cr
            lr_ref[pl.ds(base, 8), :] = lr
            li_ref[pl.ds(base, 8), :] = li
            return jnp.broadcast_to(lr[0:1, :], (8, W)), jnp.broadcast_to(li[0:1, :], (8, W))

        zero = jnp.zeros((8, W), F32)
        lax.fori_loop(0, nblk, step, (zero, zero), unroll=2)
        row = lax.broadcasted_iota(jnp.int32, (L, W), 0)
        xpr = jnp.where(row >= 1, pltpu.roll(xr_ref[...], 1, 0), 0.0)
        xpi = jnp.where(row >= 1, pltpu.roll(xi_ref[...], 1, 0), 0.0)
        lr, li = lr_ref[...], li_ref[...]
        dar_ref[...] = jnp.sum(lr * xpr + li * xpi, axis=0, keepdims=True)
        dai_ref[...] = jnp.sum(li * xpr - lr * xpi, axis=0, keepdims=True)
        u = u_ref[...]
        dwbr_ref[...] = _dot_tn(u, lr)
        dwbi_ref[...] = _dot_tn(u, li)
        du_ref[...] = (dud_ref[...] + _dot_nt(lr, wbr_ref[...]) + _dot_nt(li, wbi_ref[...])).astype(du_ref.dtype)

    ucol = pl.BlockSpec((L, cu), lambda t: (0, t))
    vec = pl.BlockSpec((1, W), lambda t: (0, t))
    col = pl.BlockSpec((L, W), lambda t: (0, t))
    wb = pl.BlockSpec((None, cu, W), lambda t: (t, 0, 0))
    wc = pl.BlockSpec((None, W, cu), lambda t: (t, 0, 0))
    return pl.pallas_call(
        body, grid=(parts,), in_specs=[ucol, ucol, ucol, col, col, vec, vec, wb, wb, wc, wc],
        out_specs=[ucol, wb, wb, wc, wc, vec, vec],
        out_shape=[jax.ShapeDtypeStruct((L, parts * cu), BF16)] + [jax.ShapeDtypeStruct((parts, cu, W), F32)] * 2
        + [jax.ShapeDtypeStruct((parts, W, cu), F32)] * 2 + [jax.ShapeDtypeStruct((1, parts * W), F32)] * 2,
        scratch_shapes=[pltpu.VMEM((L, W), F32)] * 2,
        compiler_params=_params(dimension_semantics=("parallel",)), name=name,
    )(dy, du_d, proj, xs_re, xs_im, a_re, a_im, wb_re, wb_im, wc_re, wc_im)


def _gelu(y):
    c = math.sqrt(2.0 / math.pi)
    t = jnp.tanh(c * (y + 0.044715 * y * y * y))
    return 0.5 * y * (1.0 + t), t


def s5_out_fwd(y0, proj, dvec, glu_w, glu_b, *, name):
    L, C = y0.shape
    tr = _pick(L, (256, 128))

    def body(y_ref, u_ref, d_ref, w_ref, b_ref, o_ref):
        z, _ = _gelu(y_ref[...] + d_ref[...] * u_ref[...])
        zg = _dot(z, w_ref[...]) + b_ref[...]
        o_ref[...] = (z * _sigmoid(zg)).astype(o_ref.dtype)

    row = pl.BlockSpec((tr, C), lambda i: (i, 0))
    vec = pl.BlockSpec((1, C), lambda i: (0, 0))
    wsp = pl.BlockSpec((C, C), lambda i: (0, 0))
    return pl.pallas_call(body, grid=(L // tr,), in_specs=[row, row, vec, wsp, vec], out_specs=row,
                          out_shape=jax.ShapeDtypeStruct((L, C), BF16), name=name)(
        y0, proj, dvec, glu_w, glu_b)


def s5_out_bwd(y0, proj, dvec, glu_w, glu_b, dcat, *, name):
    L, C = y0.shape
    tr = _pick(L, (256, 128))

    def body(y_ref, u_ref, d_ref, w_ref, b_ref, do_ref, dy_ref, dud_ref, z_ref, dzg_ref, db_ref, dd_ref):
        u = u_ref[...]
        y = y_ref[...] + d_ref[...] * u
        z, t = _gelu(y)
        zg = _dot(z, w_ref[...]) + b_ref[...]
        s = _sigmoid(zg)
        do = do_ref[...]
        dzg = do * z * s * (1.0 - s)
        dz = do * s + _dot_nt(dzg, w_ref[...])
        c = math.sqrt(2.0 / math.pi)
        dgelu = 0.5 * (1.0 + t) + 0.5 * y * (1.0 - t * t) * c * (1.0 + 3.0 * 0.044715 * y * y)
        dy = dz * dgelu

        @pl.when(pl.program_id(0) == 0)
        def _():
            db_ref[...] = jnp.zeros_like(db_ref)
            dd_ref[...] = jnp.zeros_like(dd_ref)

        db_ref[...] += jnp.sum(dzg, axis=0, keepdims=True)
        dd_ref[...] += jnp.sum(dy * u, axis=0, keepdims=True)
        dy_ref[...] = dy
        dud_ref[...] = dy * d_ref[...]
        z_ref[...] = z.astype(BF16)
        dzg_ref[...] = dzg.astype(BF16)

    row = pl.BlockSpec((tr, C), lambda i: (i, 0))
    vec = pl.BlockSpec((1, C), lambda i: (0, 0))
    wsp = pl.BlockSpec((C, C), lambda i: (0, 0))
    return pl.pallas_call(body, grid=(L // tr,), in_specs=[row, row, vec, wsp, vec, row],
                          out_specs=[row, row, row, row, vec, vec],
                          out_shape=[jax.ShapeDtypeStruct((L, C), F32), jax.ShapeDtypeStruct((L, C), F32),
                                     jax.ShapeDtypeStruct((L, C), BF16), jax.ShapeDtypeStruct((L, C), BF16),
                                     jax.ShapeDtypeStruct((1, C), F32), jax.ShapeDtypeStruct((1, C), F32)],
                          compiler_params=_params(dimension_semantics=("arbitrary",)), name=name)(
        y0, proj, dvec, glu_w, glu_b, dcat)


def _dot_tri(tri, x, tri_left=True):
    t = tri.astype(BF16)
    x1 = x.astype(BF16)
    r1 = x - x1.astype(F32)
    x2 = r1.astype(BF16)
    x3 = (r1 - x2.astype(F32)).astype(BF16)
    dot = (lambda p: jnp.dot(t, p, preferred_element_type=F32)) if tri_left else (
        lambda p: jnp.dot(p, t, preferred_element_type=F32))
    return dot(x1) + dot(x2) + dot(x3)


def _hg_gates(xq, xf, lb, tri):
    C = xq.shape[0]
    sq = _sigmoid(xq)
    q = xq * sq
    sg = _sigmoid(xf)
    f = lb + (1.0 - lb) * sg
    kk = 1.0 - f
    b = _dot_tri(tri, jnp.log(f))
    bm = b[C // 2 - 1:C // 2, :]
    bl = b[C - 1:C, :]
    eb = jnp.exp(b)
    eqm, ekm, ekl = jnp.exp(b - bm), jnp.exp(bm - b), jnp.exp(bl - b)
    return dict(sq=sq, q=q, sg=sg, f=f, kk=kk, eb=eb, ebl=jnp.exp(bl), eqm=eqm, ekm=ekm, ekl=ekl,
                qb=q * eb, qt=q * eqm, kt=kk * ekm, kh=kk * ekl)


def _tri(C, lower):
    r = lax.broadcasted_iota(jnp.int32, (C, C), 0)
    c = lax.broadcasted_iota(jnp.int32, (C, C), 1)
    return (r >= c) if lower else (c >= r)


def hgrn_fwd(proj, lb, norm_g, *, name):
    L = proj.shape[0]
    C, H, K = HG_CHUNK, HG_HEADS, HG_DIM
    HK = H * K
    nc = L // C

    def body(q_ref, f_ref, i_ref, g_ref, lb_ref, ng_ref, o_ref, sall_ref, st_ref):
        @pl.when(pl.program_id(0) == 0)
        def _():
            st_ref[...] = jnp.zeros_like(st_ref)

        mask = _tri(C, True)
        gt = _hg_gates(q_ref[...], f_ref[...], lb_ref[...], mask.astype(F32))
        v_all = i_ref[...]
        outs = []
        for h in range(H):
            sl = slice(h * K, (h + 1) * K)
            v = v_all[:, sl]
            st = st_ref[h]
            sall_ref[h] = st
            att = jnp.where(mask, _dot_nt(gt["qt"][:, sl], gt["kt"][:, sl]), 0.0)
            o = _dot(att, v) + _dot_nt(gt["qb"][:, sl], st)
            st_ref[h] = st * gt["ebl"][:, sl] + _dot_tn(v, gt["kh"][:, sl])
            outs.append(o * lax.rsqrt(jnp.mean(o * o, axis=-1, keepdims=True) + NORM_EPS))
        xg = g_ref[...]
        o_ref[...] = (jnp.concatenate(outs, axis=1) * ng_ref[...] * (xg * _sigmoid(xg))).astype(o_ref.dtype)

    def blk(cb):
        return pl.BlockSpec((C, HK), lambda i: (i, cb))

    vec = pl.BlockSpec((1, HK), lambda i: (0, 0))
    return pl.pallas_call(
        body, grid=(nc,), in_specs=[blk(1), blk(2), blk(3), blk(4), vec, vec],
        out_specs=[pl.BlockSpec((C, HK), lambda i: (i, 0)), pl.BlockSpec((None, H, K, K), lambda i: (i, 0, 0, 0))],
        out_shape=[jax.ShapeDtypeStruct((L, HK), BF16), jax.ShapeDtypeStruct((nc, H, K, K), F32)],
        scratch_shapes=[pltpu.VMEM((H, K, K), F32)],
        compiler_params=_params(dimension_semantics=("arbitrary",)), name=name,
    )(proj, proj, proj, proj, lb, norm_g)


def hgrn_bwd(proj, lb, norm_g, sall, dcat, *, name):
    L = proj.shape[0]
    C, H, K = HG_CHUNK, HG_HEADS, HG_DIM
    HK = H * K
    nc = L // C

    def body(q_ref, f_ref, i_ref, g_ref, lb_ref, ng_ref, sall_ref, do_ref, dx_ref, dlb_ref, dng_ref, dst_ref):
        @pl.when(pl.program_id(0) == 0)
        def _():
            dst_ref[...] = jnp.zeros_like(dst_ref)
            dlb_ref[...] = jnp.zeros_like(dlb_ref)
            dng_ref[...] = jnp.zeros_like(dng_ref)

        mask = _tri(C, True)
        xq, xg, v_all = q_ref[...], g_ref[...], i_ref[...]
        lb_all, ng = lb_ref[...], ng_ref[...]
        gt = _hg_gates(xq, f_ref[...], lb_all, mask.astype(F32))
        sgg = _sigmoid(xg)
        d_ob = do_ref[...]
        d_on = d_ob * (xg * sgg)
        doh = d_on * ng
        ohs, d_qts, d_qbs, d_kts, d_khs, dvs, d_bls = [], [], [], [], [], [], []
        for h in range(H):
            sl = slice(h * K, (h + 1) * K)
            v, st, dst = v_all[:, sl], sall_ref[h], dst_ref[h]
            qt, kt, kh, qb = gt["qt"][:, sl], gt["kt"][:, sl], gt["kh"][:, sl], gt["qb"][:, sl]
            att = jnp.where(mask, _dot_nt(qt, kt), 0.0)
            o = _dot(att, v) + _dot_nt(qb, st)
            r = lax.rsqrt(jnp.mean(o * o, axis=-1, keepdims=True) + NORM_EPS)
            oh = o * r
            do = r * (doh[:, sl] - oh * jnp.mean(doh[:, sl] * oh, axis=-1, keepdims=True))
            datt = jnp.where(mask, _dot_nt(do, v), 0.0)
            dvs.append(_dot_tn(att, do) + _dot_nt(kh, dst))
            d_qbs.append(_dot_x3(do, st))
            d_qts.append(_dot_x3(datt, kt))
            d_kts.append(_dot_x3(datt, qt, ((0,), (0,))))
            d_kh = _dot_x3(v, dst)
            d_khs.append(d_kh)
            d_bls.append(jnp.sum(dst * st, axis=0, keepdims=True) * gt["ebl"][:, sl]
                         + jnp.sum(d_kh * kh, axis=0, keepdims=True))
            dst_ref[h] = dst * gt["ebl"][:, sl] + _dot_tn(do, qb)
            ohs.append(oh)
        oh, d_qt, d_qb, d_kt, d_kh, dv, d_bl = (jnp.concatenate(p, axis=1) for p in
                                                (ohs, d_qts, d_qbs, d_kts, d_khs, dvs, d_bls))
        dxg = d_ob * (oh * ng) * (sgg * (1.0 + xg * (1.0 - sgg)))
        dng_ref[...] += jnp.sum(d_on * oh, axis=0, keepdims=True)
        dq = d_qt * gt["eqm"] + d_qb * gt["eb"]
        db = d_qt * gt["qt"] + d_qb * gt["qb"] - d_kt * gt["kt"] - d_kh * gt["kh"]
        rowi = lax.broadcasted_iota(jnp.int32, (C, HK), 0)
        db = db + jnp.where(rowi == C - 1, d_bl, 0.0)
        dkk = d_kt * gt["ekm"] + d_kh * gt["ekl"]
        dlg = _dot_tri(_tri(C, False).astype(F32), db)
        df = dlg / gt["f"] - dkk
        sg, sq = gt["sg"], gt["sq"]
        dlb_ref[...] += jnp.sum(df * (1.0 - sg), axis=0, keepdims=True)
        dx_ref[:, 0:HK] = (dq * (sq * (1.0 + xq * (1.0 - sq)))).astype(dx_ref.dtype)
        dx_ref[:, HK:2 * HK] = (df * (1.0 - lb_all) * sg * (1.0 - sg)).astype(dx_ref.dtype)
        dx_ref[:, 2 * HK:3 * HK] = dv.astype(dx_ref.dtype)
        dx_ref[:, 3 * HK:4 * HK] = dxg.astype(dx_ref.dtype)

    def blk(cb):
        return pl.BlockSpec((C, HK), lambda i: (nc - 1 - i, cb))

    vec = pl.BlockSpec((1, HK), lambda i: (0, 0))
    return pl.pallas_call(
        body, grid=(nc,),
        in_specs=[blk(1), blk(2), blk(3), blk(4), vec, vec,
                  pl.BlockSpec((None, H, K, K), lambda i: (nc - 1 - i, 0, 0, 0)), blk(1)],
        out_specs=[pl.BlockSpec((C, 4 * HK), lambda i: (nc - 1 - i, 0)), vec, vec],
        out_shape=[jax.ShapeDtypeStruct((L, 4 * HK), BF16), jax.ShapeDtypeStruct((1, HK), F32),
                   jax.ShapeDtypeStruct((1, HK), F32)],
        scratch_shapes=[pltpu.VMEM((H, K, K), F32)],
        compiler_params=_params(dimension_semantics=("arbitrary",)), name=name,
    )(proj, proj, proj, proj, lb, norm_g, sall, dcat)


def _shift_down(x, k, row):
    return jnp.where(row >= k, pltpu.roll(x, k, 0), 0.0)


def _shift_up(x, k, row):
    n = x.shape[0]
    return jnp.where(row < n - k, pltpu.roll(x, n - k, 0), 0.0)


def convgate_fwd(hu, conv_w, conv_b, *, name):
    L, C2 = hu.shape
    C = C2 // 2
    tc = _pick(C, (256, 128))
    nb = C // tc

    def body(a_ref, b_ref, wa_ref, wb_ref, ba_ref, bb_ref, o_ref):
        row = lax.broadcasted_iota(jnp.int32, (L, tc), 0)

        def conv(x, w, bias):
            return w[2:3, :] * x + w[1:2, :] * _shift_down(x, 1, row) + w[0:1, :] * _shift_down(x, 2, row) + bias

        ca = conv(a_ref[...], wa_ref[...], ba_ref[...])
        cb = conv(b_ref[...], wb_ref[...], bb_ref[...])
        o_ref[...] = (ca * _sigmoid(ca) * cb).astype(o_ref.dtype)

    def col(off, rows):
        return pl.BlockSpec((rows, tc), lambda j: (0, j + off))

    return pl.pallas_call(
        body, grid=(nb,), in_specs=[col(0, L), col(nb, L), col(0, 3), col(nb, 3), col(0, 1), col(nb, 1)],
        out_specs=col(0, L), out_shape=jax.ShapeDtypeStruct((L, C), BF16),
        compiler_params=_params(dimension_semantics=("parallel",)), name=name,
    )(hu, hu, conv_w, conv_w, conv_b, conv_b)


def convgate_bwd(hu, conv_w, conv_b, dact, *, name):
    L, C2 = hu.shape
    C = C2 // 2
    tc = _pick(C, (256, 128))
    nb = C // tc

    def body(a_ref, b_ref, wa_ref, wb_ref, ba_ref, bb_ref, d_ref, dxa_ref, dxb_ref, dwa_ref, dwb_ref, dba_ref, dbb_ref):
        row = lax.broadcasted_iota(jnp.int32, (L, tc), 0)

        def conv(x, w, bias):
            x1 = _shift_down(x, 1, row)
            x2 = _shift_down(x, 2, row)
            return w[2:3, :] * x + w[1:2, :] * x1 + w[0:1, :] * x2 + bias, x1, x2

        xa, xb = a_ref[...], b_ref[...]
        wa, wb = wa_ref[...], wb_ref[...]
        ca, xa1, xa2 = conv(xa, wa, ba_ref[...])
        cb, xb1, xb2 = conv(xb, wb, bb_ref[...])
        d = d_ref[...]
        sa = _sigmoid(ca)
        dca = d * cb * (sa * (1.0 + ca * (1.0 - sa)))
        dcb = d * (ca * sa)

        def back(dc, w, x, x1, x2, dx_ref, dw_ref, db_ref):
            dx = w[2:3, :] * dc + w[1:2, :] * _shift_up(dc, 1, row) + w[0:1, :] * _shift_up(dc, 2, row)
            dx_ref[...] = dx.astype(dx_ref.dtype)
            dw_ref[...] = jnp.concatenate([jnp.sum(dc * x2, axis=0, keepdims=True),
                                           jnp.sum(dc * x1, axis=0, keepdims=True),
                                           jnp.sum(dc * x, axis=0, keepdims=True)], axis=0)
            db_ref[...] = jnp.sum(dc, axis=0, keepdims=True)

        back(dca, wa, xa, xa1, xa2, dxa_ref, dwa_ref, dba_ref)
        back(dcb, wb, xb, xb1, xb2, dxb_ref, dwb_ref, dbb_ref)

    def col(off, rows):
        return pl.BlockSpec((rows, tc), lambda j: (0, j + off))

    outs = pl.pallas_call(
        body, grid=(nb,),
        in_specs=[col(0, L), col(nb, L), col(0, 3), col(nb, 3), col(0, 1), col(nb, 1), col(0, L)],
        out_specs=[col(0, L), col(0, L), col(0, 3), col(0, 3), col(0, 1), col(0, 1)],
        out_shape=[jax.ShapeDtypeStruct((L, C), BF16)] * 2 + [jax.ShapeDtypeStruct((3, C), F32)] * 2
        + [jax.ShapeDtypeStruct((1, C), F32)] * 2,
        compiler_params=_params(dimension_semantics=("parallel",)), name=name,
    )(hu, hu, conv_w, conv_w, conv_b, conv_b, dact)
    dxa, dxb, dwa, dwb, dba, dbb = outs
    return (jnp.concatenate([dxa, dxb], axis=1), jnp.concatenate([dwa, dwb], axis=1),
            jnp.concatenate([dba, dbb], axis=1))


def rope_tables(positions):
    half = ROT_DIM // 2
    inv_freq = ROPE_THETA ** (-jnp.arange(half, dtype=F32) * 2.0 / ROT_DIM)
    ang = positions.astype(F32)[:, None] * inv_freq
    cos, sin = jnp.cos(ang), jnp.sin(ang)
    L = positions.shape[0]
    one = jnp.ones((L, ATT_E - ROT_DIM), F32)
    zero = jnp.zeros((L, ATT_E - ROT_DIM), F32)
    zh = jnp.zeros((L, half), F32)
    tc = jnp.concatenate([cos, cos, one], axis=1)
    ts1 = jnp.concatenate([zh, sin, zero], axis=1)
    ts2 = jnp.concatenate([-sin, zh, zero], axis=1)
    return tuple(jnp.concatenate([t, t], axis=1) for t in (tc, ts1, ts2))


def rope_fwd(qkv, tabs, *, name):
    L = qkv.shape[0]
    W = 512
    tr = _pick(L, (256, 128))
    nq = 1536 // W
    scale = ATT_E ** -0.5

    def body(x_ref, c_ref, s1_ref, s2_ref, *o_refs):
        c = jnp.concatenate([c_ref[...]] * 4, axis=1)
        s1 = jnp.concatenate([s1_ref[...]] * 4, axis=1)
        s2 = jnp.concatenate([s2_ref[...]] * 4, axis=1)
        for j, o_ref in enumerate(o_refs):
            x = x_ref[:, j * W:(j + 1) * W]
            if j < 2 * nq:
                x = x * c + pltpu.roll(x, 8, 1) * s1 + pltpu.roll(x, W - 8, 1) * s2
            if j < nq:
                x = x * scale
            o_ref[...] = x.astype(o_ref.dtype)

    slab = pl.BlockSpec((tr, W), lambda i: (i, 0))
    tab = pl.BlockSpec((tr, 128), lambda i: (i, 0))
    return pl.pallas_call(body, grid=(L // tr,), in_specs=[pl.BlockSpec((tr, 3 * nq * W), lambda i: (i, 0)), tab, tab, tab],
                          out_specs=[slab] * (3 * nq), out_shape=[jax.ShapeDtypeStruct((L, W), F32)] * (3 * nq),
                          compiler_params=_params(dimension_semantics=("parallel",)), name=name)(qkv, *tabs)


def rope_bwd(slabs, tabs, *, name):
    L, W = slabs[0].shape
    tr = _pick(L, (256, 128))
    nq = len(slabs) // 3
    scale = ATT_E ** -0.5

    def body(*refs):
        d_refs, (c_ref, s1_ref, s2_ref, o_ref) = refs[:3 * nq], refs[3 * nq:]
        c = jnp.concatenate([c_ref[...]] * 4, axis=1)
        s1 = jnp.concatenate([s1_ref[...]] * 4, axis=1)
        s2 = jnp.concatenate([s2_ref[...]] * 4, axis=1)
        for j, d_ref in enumerate(d_refs):
            dy = d_ref[...]
            if j < 2 * nq:
                dy = dy * c + pltpu.roll(dy * s1, W - 8, 1) + pltpu.roll(dy * s2, 8, 1)
            if j < nq:
                dy = dy * scale
            o_ref[:, j * W:(j + 1) * W] = dy.astype(o_ref.dtype)

    slab = pl.BlockSpec((tr, W), lambda i: (i, 0))
    tab = pl.BlockSpec((tr, 128), lambda i: (i, 0))
    return pl.pallas_call(body, grid=(L // tr,), in_specs=[slab] * (3 * nq) + [tab, tab, tab],
                          out_specs=pl.BlockSpec((tr, 3 * nq * W), lambda i: (i, 0)),
                          out_shape=jax.ShapeDtypeStruct((L, 3 * nq * W), BF16),
                          compiler_params=_params(dimension_semantics=("parallel",)), name=name)(*slabs, *tabs)


def _att_masks(has_prev):
    qi = lax.broadcasted_iota(jnp.int32, (ATT_BLOCK, ATT_BLOCK), 0)
    kj = lax.broadcasted_iota(jnp.int32, (ATT_BLOCK, ATT_BLOCK), 1)
    return qi >= kj, (kj >= qi) & has_prev


ATT_COLS = 128


def _att_rows(j, d, nb):
    B = ATT_BLOCK
    r, n = j // nb, j % nb
    start = r + d * B * n
    has_prev = n > 0
    pstart = jnp.where(has_prev, start - d * B, start)
    if d == 1:
        return pl.ds(pl.multiple_of(start, B), B), pl.ds(pl.multiple_of(pstart, B), B), has_prev
    return pl.ds(start, B, stride=d), pl.ds(pstart, B, stride=d), has_prev


def attn_fwd(q, k, v, d, *, name):
    L, W = q.shape
    B, E = ATT_BLOCK, ATT_E
    nblk = L // B
    nb = nblk // d

    def body(q_ref, k_ref, v_ref, o_ref, l_ref):
        def step(j, carry):
            cur, prv, has_prev = _att_rows(j, d, nb)
            mc, mp = _att_masks(has_prev)
            qb, kc, kp, vc, vp = q_ref[cur, :], k_ref[cur, :], k_ref[prv, :], v_ref[cur, :], v_ref[prv, :]
            outs, lses = [], []
            for h in range(ATT_COLS // E):
                sl = slice(h * E, (h + 1) * E)
                sc = jnp.where(mc, _dot_nt(qb[:, sl], kc[:, sl]), NEG_BIG)
                sp = jnp.where(mp, _dot_nt(qb[:, sl], kp[:, sl]), NEG_BIG)
                m = jnp.maximum(jnp.max(sc, axis=-1, keepdims=True), jnp.max(sp, axis=-1, keepdims=True))
                pc = jnp.exp(sc - m)
                pp = jnp.exp(sp - m)
                den = jnp.sum(pc, axis=-1, keepdims=True) + jnp.sum(pp, axis=-1, keepdims=True)
                outs.append((_dot(pc, vc[:, sl]) + _dot(pp, vp[:, sl])) / den)
                lses.append(jnp.broadcast_to(m + jnp.log(den), (B, E)))
            o_ref[cur, :] = jnp.concatenate(outs, axis=1)
            l_ref[cur, :] = jnp.concatenate(lses, axis=1)
            return carry

        lax.fori_loop(0, nblk, step, 0, unroll=2)

    col = pl.BlockSpec((L, ATT_COLS), lambda c: (0, c))
    return pl.pallas_call(body, grid=(W // ATT_COLS,), in_specs=[col] * 3, out_specs=[col] * 2,
                          out_shape=[jax.ShapeDtypeStruct((L, W), F32)] * 2,
                          compiler_params=_params(dimension_semantics=("parallel",)), name=name)(q, k, v)


def attn_bwd(q, k, v, lse, do, dl, d, *, name):
    L, W = q.shape
    B, E = ATT_BLOCK, ATT_E
    nblk = L // B
    nb = nblk // d

    def body(q_ref, k_ref, v_ref, l_ref, do_ref, dl_ref, dq_ref, dk_ref, dv_ref):
        dk_ref[...] = jnp.zeros_like(dk_ref)
        dv_ref[...] = jnp.zeros_like(dv_ref)

        def step(j, carry):
            cur, prv, has_prev = _att_rows(j, d, nb)
            mc, mp = _att_masks(has_prev)
            qb, kc, kp, vc, vp = q_ref[cur, :], k_ref[cur, :], k_ref[prv, :], v_ref[cur, :], v_ref[prv, :]
            lb, dob, dlb = l_ref[cur, :], do_ref[cur, :], dl_ref[cur, :]
            dqs, dkc, dkp, dvc, dvp = [], [], [], [], []
            for h in range(ATT_COLS // E):
                sl = slice(h * E, (h + 1) * E)
                qh, doh = qb[:, sl], dob[:, sl]
                lse_h, dl_h = lb[:, h * E:h * E + 1], dlb[:, h * E:h * E + 1]
                pc = jnp.where(mc, jnp.exp(_dot_nt(qh, kc[:, sl]) - lse_h), 0.0)
                pp = jnp.where(mp, jnp.exp(_dot_nt(qh, kp[:, sl]) - lse_h), 0.0)
                dsc = pc * (_dot_nt(doh, vc[:, sl]) - dl_h)
                dsp = pp * (_dot_nt(doh, vp[:, sl]) - dl_h)
                dqs.append(_dot(dsc, kc[:, sl]) + _dot(dsp, kp[:, sl]))
                dkc.append(_dot_tn(dsc, qh))
                dkp.append(_dot_tn(dsp, qh))
                dvc.append(_dot_tn(pc, doh))
                dvp.append(_dot_tn(pp, doh))
            dq_ref[cur, :] = jnp.concatenate(dqs, axis=1)
            dk_ref[cur, :] = dk_ref[cur, :] + jnp.concatenate(dkc, axis=1)
            dv_ref[cur, :] = dv_ref[cur, :] + jnp.concatenate(dvc, axis=1)
            dk_ref[prv, :] = dk_ref[prv, :] + jnp.concatenate(dkp, axis=1)
            dv_ref[prv, :] = dv_ref[prv, :] + jnp.concatenate(dvp, axis=1)
            return carry

        lax.fori_loop(0, nblk, step, 0)

    col = pl.BlockSpec((L, ATT_COLS), lambda c: (0, c))
    return pl.pallas_call(body, grid=(W // ATT_COLS,), in_specs=[col] * 6, out_specs=[col] * 3,
                          out_shape=[jax.ShapeDtypeStruct((L, W), F32)] * 3,
                          compiler_params=_params(dimension_semantics=("parallel",)), name=name)(q, k, v, lse, do, dl)


def _merge_alpha(l_refs):
    ls = [r[...] for r in l_refs]
    m = jnp.maximum(jnp.maximum(ls[0], ls[1]), ls[2])
    es = [jnp.exp(l - m) for l in ls]
    den = es[0] + es[1] + es[2]
    return [e / den for e in es]


def merge_fwd(os_, ls_, *, name):
    L, W = os_[0].shape
    tr = _pick(L, (256, 128))

    def body(o0, o1, o2, l0, l1, l2, out_ref):
        al = _merge_alpha((l0, l1, l2))
        out_ref[...] = (al[0] * o0[...] + al[1] * o1[...] + al[2] * o2[...]).astype(out_ref.dtype)

    row = pl.BlockSpec((tr, W), lambda i: (i, 0))
    return pl.pallas_call(body, grid=(L // tr,), in_specs=[row] * 6, out_specs=row,
                          out_shape=jax.ShapeDtypeStruct((L, W), BF16), name=name)(*os_, *ls_)


def merge_bwd(os_, ls_, do, *, name):
    L, W = do.shape
    tr = _pick(L, (256, 128))

    def body(o0, o1, o2, l0, l1, l2, do_ref, d0, d1, d2, e0, e1, e2):
        al = _merge_alpha((l0, l1, l2))
        dov = do_ref[...]
        r = lax.broadcasted_iota(jnp.int32, (W, W), 0) // ATT_E
        c = lax.broadcasted_iota(jnp.int32, (W, W), 1) // ATT_E
        ones_blk = (r == c).astype(F32)
        t = jnp.zeros_like(dov)
        for a, o in zip(al, (o0, o1, o2)):
            t = t + a * _dot_tri(ones_blk, dov * o[...], tri_left=False)
        for a, d_ref, e_ref in zip(al, (d0, d1, d2), (e0, e1, e2)):
            d_ref[...] = a * dov
            e_ref[...] = a * t

    row = pl.BlockSpec((tr, W), lambda i: (i, 0))
    return pl.pallas_call(body, grid=(L // tr,), in_specs=[row] * 7, out_specs=[row] * 6,
                          out_shape=[jax.ShapeDtypeStruct((L, W), F32)] * 6, name=name)(*os_, *ls_, do)


def _me_and_peers():
    x, y, c = lax.axis_index("x"), lax.axis_index("y"), lax.axis_index("c")
    peers = []
    for k in range(1, N_DEV):
        px = 1 - x if k & 4 else x
        py = 1 - y if k & 2 else y
        pc = 1 - c if k & 1 else c
        peers.append((px, py, pc))
    return (x, y, c), peers


def _index(dev):
    return 4 * dev[0] + 2 * dev[1] + dev[2]


def _hbm(a):
    return pltpu.with_memory_space_constraint(a, pltpu.HBM)


HBM_SPEC = pl.BlockSpec(memory_space=pltpu.HBM)
SEM_SPEC = pl.BlockSpec(memory_space=pltpu.SEMAPHORE)
DATAFLOW = pltpu.SideEffectType.DATAFLOW_SIDE_EFFECTING


def _remote(src_ref, land_ref, slotted, me, peer, src_is_mine, send_sem, recv_sem, k):
    sender, receiver = (me, peer) if src_is_mine else (peer, me)
    src = src_ref.at[_index(receiver)] if slotted else src_ref
    return pltpu.make_async_remote_copy(src_ref=src, dst_ref=land_ref.at[_index(sender)], send_sem=send_sem.at[k],
                                        recv_sem=recv_sem.at[k], device_id=peer, device_id_type=MESH_ID)


def copies_start(arrays, slotted, *, name):
    n = len(arrays)
    lands = [lax.empty(a.shape if slotted else (N_DEV,) + a.shape, a.dtype) for a in arrays]

    def body(*refs):
        x_refs, land_refs = refs[:n], refs[n:2 * n]
        send, recv = refs[2 * n:3 * n], refs[3 * n:4 * n]
        token = refs[-1]
        me, peers = _me_and_peers()
        for w in range(n):
            for k, peer in enumerate(peers):
                _remote(x_refs[w], land_refs[w], slotted, me, peer, True, send[w], recv[w], k).start()
            if not slotted:
                pltpu.make_async_copy(x_refs[w], land_refs[w].at[_index(me)], recv[w].at[N_DEV - 1]).start()
        token[...] = jnp.zeros_like(token)

    sem = pltpu.SemaphoreType.DMA((N_DEV,))
    out_shape = ([sem] * (2 * n) + [pltpu.HBM(a.shape, a.dtype) for a in arrays]
                 + [pltpu.HBM(l.shape, l.dtype) for l in lands] + [jax.ShapeDtypeStruct((8, 128), F32)])
    outs = pl.pallas_call(
        body, name=name, out_shape=out_shape, in_specs=[HBM_SPEC] * (2 * n),
        out_specs=[SEM_SPEC] * (2 * n) + [HBM_SPEC] * (2 * n) + [pl.BlockSpec(memory_space=pltpu.VMEM)],
        input_output_aliases={i: 2 * n + i for i in range(2 * n)},
        compiler_params=pltpu.CompilerParams(has_side_effects=DATAFLOW),
    )(*[_hbm(a) for a in arrays], *[_hbm(l) for l in lands])
    handles = [(outs[w], outs[n + w], outs[2 * n + w], outs[3 * n + w]) for w in range(n)]
    return handles, outs[-1]


def copies_wait(handle, slotted, after, *, name):
    send_sem, recv_sem, x_thru, land_thru = handle

    def body(x_ref, land_ref, send_ref, recv_ref, after_ref, x_out, land_out):
        me, peers = _me_and_peers()
        for k, peer in enumerate(peers):
            _remote(x_ref, land_ref, slotted, me, peer, True, send_ref, recv_ref, k).wait_send()
        for k, peer in enumerate(peers):
            _remote(x_ref, land_ref, slotted, me, peer, False, send_ref, recv_ref, k).wait_recv()
        if not slotted:
            pltpu.make_async_copy(x_ref, land_ref.at[_index(me)], recv_ref.at[N_DEV - 1]).wait()

    return pl.pallas_call(
        body, name=name, out_shape=(pltpu.HBM(x_thru.shape, x_thru.dtype), pltpu.HBM(land_thru.shape, land_thru.dtype)),
        in_specs=(HBM_SPEC, HBM_SPEC, SEM_SPEC, SEM_SPEC, pl.BlockSpec(memory_space=pl.ANY)),
        out_specs=(HBM_SPEC, HBM_SPEC), input_output_aliases={0: 0, 1: 1},
        compiler_params=pltpu.CompilerParams(has_side_effects=DATAFLOW),
    )(x_thru, land_thru, send_sem, recv_sem, after)


def cast_bf16(x, *, ncols=None, name):
    R = x.shape[0]
    C = ncols or x.shape[1]
    tr = _pick(R, (512, 352, 256, 128, 64))

    def body(x_ref, o_ref):
        o_ref[...] = x_ref[...].astype(BF16)

    row = pl.BlockSpec((tr, C), lambda i: (i, 0))
    return pl.pallas_call(body, grid=(R // tr,), in_specs=[row], out_specs=row,
                          out_shape=jax.ShapeDtypeStruct((R, C), BF16), name=name)(x)


def cast_bf16_layer(x3, layer, *, name):
    _, R, C = x3.shape
    tr = _pick(R, (512, 352, 256, 128, 64))

    def body(x_ref, o_ref):
        o_ref[...] = x_ref[...].astype(BF16)

    return pl.pallas_call(body, grid=(R // tr,), in_specs=[pl.BlockSpec((None, tr, C), lambda i: (layer, i, 0))],
                          out_specs=pl.BlockSpec((tr, C), lambda i: (i, 0)),
                          out_shape=jax.ShapeDtypeStruct((R, C), BF16), name=name)(x3)


BD_PARTS = 4


def _blockdiag_call(b, build, G, r, c, name):
    gp = G // BD_PARTS

    def body_build(b_ref, o_ref):
        o_ref[...] = jnp.zeros_like(o_ref)
        for g in range(G):
            o_ref[g // gp, (g % gp) * r:(g % gp + 1) * r, (g % gp) * c:(g % gp + 1) * c] = b_ref[g]

    def body_extract(d_ref, o_ref):
        for g in range(G):
            o_ref[g] = d_ref[g // gp, (g % gp) * r:(g % gp + 1) * r, (g % gp) * c:(g % gp + 1) * c]

    out = jax.ShapeDtypeStruct((BD_PARTS, gp * r, gp * c) if build else (G, r, c), F32)
    return pl.pallas_call(body_build if build else body_extract, out_shape=out, name=name)(b)


def make_blockdiag(G, r, c, name):
    @jax.custom_vjp
    def blockdiag(b):
        return _blockdiag_call(b, True, G, r, c, name + "_build")

    def fwd(b):
        return blockdiag(b), None

    def bwd(_, g):
        return (_blockdiag_call(g, False, G, r, c, name + "_extract"),)

    blockdiag.defvjp(fwd, bwd)
    return blockdiag


def _my_index():
    return 4 * lax.axis_index("x") + 2 * lax.axis_index("y") + lax.axis_index("c")


def cols_from_shards(g, *, name):
    _, K, n = g.shape
    tk = _pick(K, (256, 128))

    def body(g_ref, o_ref):
        for i in range(N_DEV):
            o_ref[:, i * n:(i + 1) * n] = g_ref[i]

    return pl.pallas_call(body, grid=(K // tk,), in_specs=[pl.BlockSpec((N_DEV, tk, n), lambda i: (0, i, 0))],
                          out_specs=pl.BlockSpec((tk, N_DEV * n), lambda i: (i, 0)),
                          out_shape=jax.ShapeDtypeStruct((K, N_DEV * n), g.dtype), name=name)(g)


def shards_from_cols(w, *, name):
    K, N = w.shape
    n = N // N_DEV
    tk = _pick(K, (256, 128))

    def body(w_ref, o_ref):
        for i in range(N_DEV):
            o_ref[i] = w_ref[:, i * n:(i + 1) * n].astype(o_ref.dtype)

    return pl.pallas_call(body, grid=(K // tk,), in_specs=[pl.BlockSpec((tk, N), lambda i: (i, 0))],
                          out_specs=pl.BlockSpec((N_DEV, tk, n), lambda i: (0, i, 0)),
                          out_shape=jax.ShapeDtypeStruct((N_DEV, K, n), BF16), name=name)(w)


def _adamw(w, g, m, v):
    m = ADAM_B1 * m + (1.0 - ADAM_B1) * g
    v = ADAM_B2 * v + (1.0 - ADAM_B2) * (g * g)
    m_hat = m / (1.0 - ADAM_B1 ** ADAM_STEP)
    v_hat = v / (1.0 - ADAM_B2 ** ADAM_STEP)
    delta = -ADAM_LR * (m_hat / (jnp.sqrt(v_hat) + ADAM_EPS) + ADAM_WD * w)
    return delta, m, v


def reduce_adamw(recv, own, own_slotted, me, w, m, v, *, layer=0, n_layers=1, into=None, name):
    _, R, C = recv.shape
    tr = _pick(R, (352, 320, 288, 256, 128, 64, 32, 16, 8))
    off = layer * (R // tr)

    def body(me_ref, r_ref, own_ref, w_ref, m_ref, v_ref, *rest):
        g_ref, d_ref, nm_ref, nv_ref = rest[-4:]
        mine = me_ref[0]
        g = None
        for i in range(N_DEV):
            part = jnp.where(mine == i, own_ref[...], r_ref[i]).astype(F32)
            g = part if g is None else g + part
        delta, nm, nv = _adamw(w_ref[...], g, m_ref[...], v_ref[...])
        g_ref[...] = g
        d_ref[...] = delta
        nm_ref[...] = nm
        nv_ref[...] = nv

    row = pl.BlockSpec((tr, C), lambda i, me_ref: (i + off, 0))
    own_spec = (pl.BlockSpec((None, tr, C), lambda i, me_ref: (me_ref[0], i, 0)) if own_slotted
                else pl.BlockSpec((tr, C), lambda i, me_ref: (i, 0)))
    rest = [] if into is None else list(into)
    grid_spec = pltpu.PrefetchScalarGridSpec(
        num_scalar_prefetch=1, grid=(R // tr,),
        in_specs=[pl.BlockSpec((N_DEV, tr, C), lambda i, me_ref: (0, i, 0)), own_spec, row, row, row]
        + [pl.BlockSpec(memory_space=pl.ANY)] * len(rest),
        out_specs=[row] * 4)
    return pl.pallas_call(body, grid_spec=grid_spec, out_shape=[jax.ShapeDtypeStruct((n_layers * R, C), F32)] * 4,
                          input_output_aliases={6 + k: k for k in range(len(rest))},
                          compiler_params=_params(dimension_semantics=("parallel",)), name=name)(
        me.reshape(1).astype(jnp.int32), recv, own, w, m, v, *rest)


def _s5_prepare(A_re, A_im, log_dt, B_re, B_im, C_re, C_im):
    G, P, Cg = S5_GROUPS, S5_STATE, S5_GROUP
    dt = jnp.exp(log_dt)[:, None]
    mag = jnp.exp(A_re * dt)
    ab_re = mag * jnp.cos(A_im * dt)
    ab_im = mag * jnp.sin(A_im * dt)
    den = A_re * A_re + A_im * A_im
    nr, ni = ab_re - 1.0, ab_im
    c_re = (nr * A_re + ni * A_im) / den
    c_im = (ni * A_re - nr * A_im) / den
    Bb_re = c_re[..., None] * B_re - c_im[..., None] * B_im
    Bb_im = c_re[..., None] * B_im + c_im[..., None] * B_re
    def dense_in(b, name):
        return make_blockdiag(G, Cg, P, name)(b.transpose(0, 2, 1))

    def dense_out(c, name):
        return make_blockdiag(G, P, Cg, name)(c.transpose(0, 2, 1))

    return (ab_re.reshape(1, G * P), ab_im.reshape(1, G * P), dense_in(Bb_re, "s5_wb_re"), dense_in(Bb_im, "s5_wb_im"),
            dense_out(C_re, "s5_wc_re"), dense_out(-C_im, "s5_wc_im"))


def _lower_bound(gamma):
    return jnp.cumsum(jax.nn.softmax(gamma, axis=0), axis=0)[0:1]


def _ffn_fwd(h, g_norm, get_w_in, conv_w, conv_b, get_w_out, tag):
    hn = rms_fwd(h, g_norm, name=tag + "_rms")
    w_in = get_w_in(hn)
    hu = mm(hn, w_in, tb=True, name=tag + "_in")
    act = convgate_fwd(hu, conv_w, conv_b, name=tag + "_gate")
    w_out = get_w_out(act)
    h_out = mm(act, w_out, res=h, name=tag + "_out")
    return h_out, (hn, hu, act), w_in, w_out


def _ffn_bwd(h, g_norm, w_in, conv_w, conv_b, w_out, saved, dh, tag, send_dw_in, send_dw_out):
    hn, hu, act = saved
    sent = send_dw_out(mm(act, dh, ta=True, out_dtype=BF16, name=tag + "_dwout"))
    dact = mm(dh, w_out, tb=True, dep=sent, name=tag + "_dact")
    dhu, dconv_w, dconv_b = convgate_bwd(hu, conv_w, conv_b, dact, name=tag + "_dgate")
    sent = send_dw_in(mm(dhu, hn, ta=True, out_dtype=BF16, name=tag + "_dwin"))
    dhn = mm(dhu, w_in, dep=sent, name=tag + "_dhn")
    dh_in, dg = rms_bwd(h, g_norm, dhn, dh, name=tag + "_drms")
    return dh_in, dg, dconv_w, dconv_b


def kernel(x, positions, norm_mix, norm_ffn, norm_final, mix_w_in, mix_w_out, s5_A_re, s5_A_im, s5_log_dt, s5_B_re, s5_B_im, s5_C_re, s5_C_im, s5_D, s5_glu_w, s5_glu_b, hgrn_gamma, hgrn_norm, att_w_qkv, att_w_o, ffn_w_in, ffn_conv_w, ffn_conv_b, ffn_w_out, loss_target, m_norm_mix, m_norm_ffn, m_norm_final, m_mix_w_in, m_mix_w_out, m_s5_A_re, m_s5_A_im, m_s5_log_dt, m_s5_B_re, m_s5_B_im, m_s5_C_re, m_s5_C_im, m_s5_D, m_s5_glu_w, m_s5_glu_b, m_hgrn_gamma, m_hgrn_norm, m_att_w_qkv, m_att_w_o, m_ffn_w_in, m_ffn_conv_w, m_ffn_conv_b, m_ffn_w_out, v_norm_mix, v_norm_ffn, v_norm_final, v_mix_w_in, v_mix_w_out, v_s5_A_re, v_s5_A_im, v_s5_log_dt, v_s5_B_re, v_s5_B_im, v_s5_C_re, v_s5_C_im, v_s5_D, v_s5_glu_w, v_s5_glu_b, v_hgrn_gamma, v_hgrn_norm, v_att_w_qkv, v_att_w_o, v_ffn_w_in, v_ffn_conv_w, v_ffn_conv_b, v_ffn_w_out):
    W = dict(norm_mix=norm_mix, norm_ffn=norm_ffn, norm_final=norm_final, mix_w_in=mix_w_in, mix_w_out=mix_w_out,
             s5_A_re=s5_A_re, s5_A_im=s5_A_im, s5_log_dt=s5_log_dt, s5_B_re=s5_B_re, s5_B_im=s5_B_im,
             s5_C_re=s5_C_re, s5_C_im=s5_C_im, s5_D=s5_D, s5_glu_w=s5_glu_w, s5_glu_b=s5_glu_b,
             hgrn_gamma=hgrn_gamma, hgrn_norm=hgrn_norm, att_w_qkv=att_w_qkv, att_w_o=att_w_o, ffn_w_in=ffn_w_in,
             ffn_conv_w=ffn_conv_w, ffn_conv_b=ffn_conv_b, ffn_w_out=ffn_w_out)
    M = dict(norm_mix=m_norm_mix, norm_ffn=m_norm_ffn, norm_final=m_norm_final, mix_w_in=m_mix_w_in,
             mix_w_out=m_mix_w_out, s5_A_re=m_s5_A_re, s5_A_im=m_s5_A_im, s5_log_dt=m_s5_log_dt, s5_B_re=m_s5_B_re,
             s5_B_im=m_s5_B_im, s5_C_re=m_s5_C_re, s5_C_im=m_s5_C_im, s5_D=m_s5_D, s5_glu_w=m_s5_glu_w,
             s5_glu_b=m_s5_glu_b, hgrn_gamma=m_hgrn_gamma, hgrn_norm=m_hgrn_norm, att_w_qkv=m_att_w_qkv,
             att_w_o=m_att_w_o, ffn_w_in=m_ffn_w_in, ffn_conv_w=m_ffn_conv_w, ffn_conv_b=m_ffn_conv_b,
             ffn_w_out=m_ffn_w_out)
    V = dict(norm_mix=v_norm_mix, norm_ffn=v_norm_ffn, norm_final=v_norm_final, mix_w_in=v_mix_w_in,
             mix_w_out=v_mix_w_out, s5_A_re=v_s5_A_re, s5_A_im=v_s5_A_im, s5_log_dt=v_s5_log_dt, s5_B_re=v_s5_B_re,
             s5_B_im=v_s5_B_im, s5_C_re=v_s5_C_re, s5_C_im=v_s5_C_im, s5_D=v_s5_D, s5_glu_w=v_s5_glu_w,
             s5_glu_b=v_s5_glu_b, hgrn_gamma=v_hgrn_gamma, hgrn_norm=v_hgrn_norm, att_w_qkv=v_att_w_qkv,
             att_w_o=v_att_w_o, ffn_w_in=v_ffn_w_in, ffn_conv_w=v_ffn_conv_w, ffn_conv_b=v_ffn_conv_b,
             ffn_w_out=v_ffn_w_out)
    return _step(x[0], positions[0], loss_target[0], W, M, V)


TRANSPOSED = ("mix_w_in", "att_w_qkv", "ffn_w_in")
SMALL = ("norm_mix", "norm_ffn", "norm_final", "s5_A_re", "s5_A_im", "s5_log_dt", "s5_B_re", "s5_B_im", "s5_C_re",
         "s5_C_im", "s5_D", "s5_glu_b", "hgrn_gamma", "hgrn_norm", "ffn_conv_b")
ORDER = ("norm_mix", "norm_ffn", "norm_final", "mix_w_in", "mix_w_out", "s5_A_re", "s5_A_im", "s5_log_dt", "s5_B_re",
         "s5_B_im", "s5_C_re", "s5_C_im", "s5_D", "s5_glu_w", "s5_glu_b", "hgrn_gamma", "hgrn_norm", "att_w_qkv",
         "att_w_o", "ffn_w_in", "ffn_conv_w", "ffn_conv_b", "ffn_w_out")
PACK_COLS = 1024


def _step(x, positions, target, W, M, V):
    L, D = x.shape
    me = 4 * lax.axis_index("x") + 2 * lax.axis_index("y") + lax.axis_index("c")
    n_cw = W["ffn_conv_w"].shape[-1]
    T = {n: tuple(jnp.swapaxes(d[n], -1, -2) for d in (W, M, V)) for n in TRANSPOSED}
    shards = {
        "mix_w_in": cast_bf16(T["mix_w_in"][0][0], name="mix_w_in_cast"),
        "conv_w": W["ffn_conv_w"].reshape(6, n_cw),
        "s5_glu_w": cast_bf16(W["s5_glu_w"][0], name="s5_glu_w_cast"),
        "mix_w_out": cast_bf16(W["mix_w_out"][0], name="mix_w_out_cast"),
        "ffn_w_in0": cast_bf16_layer(T["ffn_w_in"][0], 0, name="ffn_w_in0_cast"),
        "ffn_w_out0": cast_bf16_layer(W["ffn_w_out"], 0, name="ffn_w_out0_cast"),
        "att_w_qkv": cast_bf16(T["att_w_qkv"][0][0], name="att_w_qkv_cast"),
        "att_w_o": cast_bf16(W["att_w_o"][0], name="att_w_o_cast"),
        "ffn_w_in1": cast_bf16_layer(T["ffn_w_in"][0], 1, name="ffn_w_in1_cast"),
        "ffn_w_out1": cast_bf16_layer(W["ffn_w_out"], 1, name="ffn_w_out1_cast"),
    }
    gather_handles, token = copies_start(list(shards.values()), False, name="gather_start")
    gather_handle = dict(zip(shards, gather_handles))

    def gathered(key, after, cols):
        _, land = copies_wait(gather_handle[key], False, after, name=key + "_gwait")
        return cols_from_shards(land, name=key + "_asm") if cols else land.reshape(-1, land.shape[-1])

    conv_b = W["ffn_conv_b"].reshape(2, 1, -1)

    s5_params = (W["s5_A_re"][0], W["s5_A_im"][0], W["s5_log_dt"][0], W["s5_B_re"][0], W["s5_B_im"][0],
                 W["s5_C_re"][0], W["s5_C_im"][0])
    (a_re, a_im, wb_re, wb_im, wc_re, wc_im), s5_prep_vjp = jax.vjp(_s5_prepare, *s5_params)
    dvec = W["s5_D"].reshape(1, S5_WIDTH)
    glu_b = W["s5_glu_b"].reshape(1, S5_WIDTH)
    lb, lb_vjp = jax.vjp(_lower_bound, W["hgrn_gamma"])
    hg_norm = W["hgrn_norm"].reshape(1, -1)
    tabs = rope_tables(positions)

    hn0 = rms_fwd(x, W["norm_mix"][0], dep=token, name="l0_rms")
    w_mix_in = gathered("mix_w_in", hn0, False)
    proj = mm(hn0, w_mix_in, tb=True, name="l0_proj")
    y0, xs_re, xs_im = s5_core_fwd(proj, a_re, a_im, wb_re, wb_im, wc_re, wc_im, name="s5_core")
    w_glu = gathered("s5_glu_w", y0, False)
    oa = s5_out_fwd(y0, proj, dvec, w_glu, glu_b, name="s5_out")
    ob, hg_states = hgrn_fwd(proj, lb, hg_norm, name="hgrn_fwd")
    cat = jnp.concatenate([oa, ob], axis=1)
    w_mix_out = gathered("mix_w_out", cat, False)
    h1 = mm(cat, w_mix_out, res=x, name="l0_mix_out")
    _, cw_all = copies_wait(gather_handle["conv_w"], False, h1, name="conv_w_gwait")
    conv_w = cw_all.transpose(1, 0, 2).reshape(2, 3, N_DEV * n_cw)
    w_ffn_in, w_ffn_out = [None, None], [None, None]
    h2, ffn0_saved, w_ffn_in[0], w_ffn_out[0] = _ffn_fwd(
        h1, W["norm_ffn"][0], lambda a: gathered("ffn_w_in0", a, False), conv_w[0], conv_b[0],
        lambda a: gathered("ffn_w_out0", a, False), "ffn0")

    hn2 = rms_fwd(h2, W["norm_mix"][1], name="l1_rms")
    w_qkv = gathered("att_w_qkv", hn2, False)
    qkv = mm(hn2, w_qkv, tb=True, name="l1_qkv")
    qkv_r = rope_fwd(qkv, tabs, name="rope_fwd")
    att_o, att_l = [], []
    for g, d in enumerate(ATT_DILATIONS):
        o_g, l_g = attn_fwd(qkv_r[g], qkv_r[3 + g], qkv_r[6 + g], d, name=f"attn_fwd{g}")
        att_o.append(o_g)
        att_l.append(l_g)
    o_att = merge_fwd(att_o, att_l, name="merge_fwd")
    w_o = gathered("att_w_o", o_att, True)
    h3 = mm(o_att, w_o, res=h2, name="l1_mix_out")
    h4, ffn1_saved, w_ffn_in[1], w_ffn_out[1] = _ffn_fwd(
        h3, W["norm_ffn"][1], lambda a: gathered("ffn_w_in1", a, False), conv_w[1], conv_b[1],
        lambda a: gathered("ffn_w_out1", a, False), "ffn1")

    exchanges = {}

    def send_grad(key, g, cols):
        if cols:
            parts = shards_from_cols(g, name=key + "_split")
        else:
            parts = g.reshape(N_DEV, g.shape[0] // N_DEV, g.shape[1])
        (handle,), sent = copies_start([parts], True, name=key + "_xstart")
        exchanges[key] = handle
        return sent

    loss, dh4, dg_final = final_loss(h4, W["norm_final"], target, name="final_loss")
    dh3, dg_ffn1, dcw1, dcb1 = _ffn_bwd(h3, W["norm_ffn"][1], w_ffn_in[1], conv_w[1], conv_b[1], w_ffn_out[1],
                                        ffn1_saved, dh4, "ffn1", lambda g: send_grad("ffn_w_in1", g, False),
                                        lambda g: send_grad("ffn_w_out1", g, False))
    sent = send_grad("att_w_o", mm(o_att, dh3, ta=True, name="l1_dwo"), True)
    d_oatt = mm(dh3, w_o, tb=True, dep=sent, name="l1_dmix")
    mb = merge_bwd(att_o, att_l, d_oatt, name="merge_bwd")
    d_slabs = [attn_bwd(qkv_r[g], qkv_r[3 + g], qkv_r[6 + g], att_l[g], mb[g], mb[3 + g], d, name=f"attn_bwd{g}")
               for g, d in enumerate(ATT_DILATIONS)]
    d_qkv = rope_bwd([s[0] for s in d_slabs] + [s[1] for s in d_slabs] + [s[2] for s in d_slabs], tabs,
                     name="rope_bwd")
    sent = send_grad("att_w_qkv", mm(d_qkv, hn2, ta=True, out_dtype=BF16, name="l1_dwqkv"), False)
    d_hn2 = mm(d_qkv, w_qkv, dep=sent, name="l1_dhn")
    dh2, dg_mix1 = rms_bwd(h2, W["norm_mix"][1], d_hn2, dh3, name="l1_drms")

    dh1, dg_ffn0, dcw0, dcb0 = _ffn_bwd(h1, W["norm_ffn"][0], w_ffn_in[0], conv_w[0], conv_b[0], w_ffn_out[0],
                                        ffn0_saved, dh2, "ffn0", lambda g: send_grad("ffn_w_in0", g, False),
                                        lambda g: send_grad("ffn_w_out0", g, False))
    sent = send_grad("mix_w_out", mm(cat, dh1, ta=True, out_dtype=BF16, name="l0_dwout"), False)
    dcat = mm(dh1, w_mix_out, tb=True, dep=sent, name="l0_dcat")
    d_hg, dlb, dhg_norm = hgrn_bwd(proj, lb, hg_norm, hg_states, dcat, name="hgrn_bwd")
    dy, du_d, z_bf, dzg, dglu_b, dD = s5_out_bwd(y0, proj, dvec, w_glu, glu_b, dcat, name="s5_dout")
    sent_glu = send_grad("s5_glu_w", mm(z_bf, dzg, ta=True, out_dtype=BF16, name="s5_dglu"), False)
    du, dwb_re, dwb_im, dwc_re, dwc_im, da_re, da_im = s5_core_bwd(
        dy, du_d, proj, xs_re, xs_im, a_re, a_im, wb_re, wb_im, wc_re, wc_im, name="s5_dcore")
    s5_small = s5_prep_vjp((da_re, da_im, dwb_re, dwb_im, dwc_re, dwc_im))
    d_proj = jnp.concatenate([du, d_hg], axis=1)
    sent = send_grad("mix_w_in", mm(d_proj, hn0, ta=True, out_dtype=BF16, dep=sent_glu, name="l0_dwin"), False)
    d_hn0 = mm(d_proj, w_mix_in, dep=sent, name="l0_dhn")
    grad_x, dg_mix0 = rms_bwd(x, W["norm_mix"][0], d_hn0, dh1, name="l0_drms")
    (d_gamma,) = lb_vjp(dlb)
    out = {}

    dA_re, dA_im, dlog_dt, dB_re, dB_im, dC_re, dC_im = s5_small
    small_g = dict(norm_mix=jnp.concatenate([dg_mix0, dg_mix1], axis=0), norm_ffn=jnp.concatenate([dg_ffn0, dg_ffn1], axis=0),
                   norm_final=dg_final, s5_A_re=dA_re, s5_A_im=dA_im, s5_log_dt=dlog_dt, s5_B_re=dB_re, s5_B_im=dB_im,
                   s5_C_re=dC_re, s5_C_im=dC_im, s5_D=dD, s5_glu_b=dglu_b, hgrn_gamma=d_gamma, hgrn_norm=dhg_norm,
                   ffn_conv_b=jnp.concatenate([dcb0, dcb1], axis=0))
    conv_w_g = jnp.stack([dcw0, dcw1], axis=0)
    sizes = [math.prod(W[n].shape) for n in SMALL]
    n_conv = conv_w_g.size
    total = sum(sizes) + n_conv + 1
    rows = -(-total // PACK_COLS)
    rows = -(-rows // 8) * 8
    pad = rows * PACK_COLS - total

    def pack(vals, conv_part, last):
        flat = [v.reshape(-1).astype(F32) for v in vals] + [conv_part.reshape(-1), last.reshape(-1),
                                                            jnp.zeros((pad,), F32)]
        return jnp.concatenate(flat).reshape(rows, PACK_COLS)

    def conv_full(shard):
        col_owner = lax.broadcasted_iota(jnp.int32, (2, 3, N_DEV * n_cw), 2) // n_cw
        return jnp.where(col_owner == me, jnp.tile(shard, (1, 1, N_DEV)), 0.0)

    zero1 = jnp.zeros((1,), F32)
    g_pack = pack([small_g[n] for n in SMALL], conv_w_g, loss)
    w_pack = pack([W[n] for n in SMALL], conv_full(W["ffn_conv_w"]), zero1)
    m_pack = pack([M[n] for n in SMALL], conv_full(M["ffn_conv_w"]), zero1)
    v_pack = pack([V[n] for n in SMALL], conv_full(V["ffn_conv_w"]), zero1 + 1.0)
    (small_handle,), small_sent = copies_start([g_pack], False, name="small_xstart")

    def finish(name, n_layers):
        w3, m3, v3 = T[name] if name in TRANSPOSED else (W[name], M[name], V[name])
        res = None
        for layer in reversed(range(n_layers)):
            key = name if n_layers == 1 else f"{name}{layer}"
            own, recv = copies_wait(exchanges[key], True, small_sent, name=key + "_xwait")
            _, R, Cn = recv.shape
            res = reduce_adamw(recv, own, True, me, w3.reshape(n_layers * R, Cn), m3.reshape(n_layers * R, Cn),
                               v3.reshape(n_layers * R, Cn), layer=layer, n_layers=n_layers, into=res,
                               name=key + "_adamw")
        res = [r.reshape(w3.shape) for r in res]
        return tuple(jnp.swapaxes(r, -1, -2) for r in res) if name in TRANSPOSED else tuple(res)

    for name in ("ffn_w_out", "ffn_w_in"):
        out[name] = finish(name, 2)
    for name in ("att_w_o", "att_w_qkv", "mix_w_out", "s5_glu_w", "mix_w_in"):
        out[name] = finish(name, 1)

    small_own, small_recv = copies_wait(small_handle, False, out["s5_glu_w"][0], name="small_xwait")
    res = reduce_adamw(small_recv, small_own, False, me, w_pack, m_pack, v_pack, name="small_adamw")
    flat = [r.reshape(-1) for r in res]
    off = 0
    for n, sz in zip(SMALL, sizes):
        out[n] = tuple(f[off:off + sz].reshape(W[n].shape) for f in flat)
        off += sz
    conv_res = [f[off:off + n_conv].reshape(2, 3, N_DEV * n_cw) for f in flat]
    out["ffn_conv_w"] = tuple(lax.dynamic_slice(c, (0, 0, me * n_cw), (2, 3, n_cw)) for c in conv_res)
    off += n_conv
    loss_total = flat[0][off]

    result = [loss_total, grad_x[None]]
    for k in range(4):
        result += [out[n][k] for n in ORDER]
    return tuple(result)
```

```python
import functools
import math

import jax
import jax.numpy as jnp
from jax import lax
from jax.experimental import pallas as pl
from jax.experimental.pallas import tpu as pltpu

F32 = jnp.float32
BF16 = jnp.bfloat16
MESH_ID = pl.DeviceIdType.MESH
N_DEV = 8
VMEM_LIMIT_BYTES = 56 * 1024 * 1024

NORM_EPS = 1e-6
S5_WIDTH, S5_GROUP, S5_GROUPS, S5_STATE = 512, 16, 32, 64
HG_HEADS, HG_DIM, HG_CHUNK = 4, 128, 64
HG_STEP_CHUNKS = 2
ATT_E, ATT_HPG, ATT_BLOCK = 64, 8, 128
ATT_DILATIONS = (1, 4, 16)
ROT_DIM, ROPE_THETA = 16, 500000.0
D_FF = 2816
ADAM_LR, ADAM_B1, ADAM_B2, ADAM_EPS, ADAM_WD, ADAM_STEP = 0.001, 0.9, 0.999, 1e-08, 0.01, 10
NEG_BIG = -1e30


def _params(**kw):
    return pltpu.CompilerParams(vmem_limit_bytes=VMEM_LIMIT_BYTES, **kw)


def _pick(n, cands):
    for c in cands:
        if n % c == 0:
            return c
    return n


def _dot(a, b):
    return jnp.dot(a.astype(BF16), b.astype(BF16), preferred_element_type=F32)


def _dot_nt(a, b):
    return lax.dot_general(a.astype(BF16), b.astype(BF16), (((1,), (1,)), ((), ())), preferred_element_type=F32)


def _dot_tn(a, b):
    return lax.dot_general(a.astype(BF16), b.astype(BF16), (((0,), (0,)), ((), ())), preferred_element_type=F32)


def _split2(x):
    hi = x.astype(BF16)
    return hi, (x - hi.astype(F32)).astype(BF16)


def _dot_x3(a, b, contract=((1,), (0,))):
    dn = (contract, ((), ()))
    a1, a2 = _split2(a)
    b1, b2 = _split2(b)
    return (lax.dot_general(a1, b1, dn, preferred_element_type=F32) + lax.dot_general(a1, b2, dn, preferred_element_type=F32)
            + lax.dot_general(a2, b1, dn, preferred_element_type=F32))


def _sigmoid(x):
    return 1.0 / (1.0 + jnp.exp(-x))


V7X_HBM_BYTES_PER_S = 3.2e12
V7X_MXU_FLOPS_PER_S = 0.7e15
GRID_STEP_S = 0.35e-6
MM_VMEM_BUDGET = 40 * 1024 * 1024


def _divisors(n, cands):
    return [c for c in cands if c <= n and n % c == 0] or [n]


def _mm_tiles(m, n, k, sa, sb, so, sr):
    best = None
    for tm in _divisors(m, (2816, 2048, 1408, 1024, 512, 256, 128)):
        for tn in _divisors(n, (2816, 2048, 1408, 1024, 512, 256, 128)):
            for tk in _divisors(k, (k, 2816, 2560, 2304, 2048, 1536, 1408, 1280, 1024, 512, 256, 128)):
                nk = k // tk
                vmem = 2 * (tm * tk * sa + tk * tn * sb + tm * tn * (so + sr)) + (tm * tn * 4 if nk > 1 else 0)
                vmem += tm * tk * 2 * (sa > 2) + tk * tn * 2 * (sb > 2) + tm * tn * 4
                if vmem > MM_VMEM_BUDGET:
                    continue
                ni, nj = m // tm, n // tn
                for i_outer in (True, False):
                    if i_outer:
                        a_reads = 1 if nk == 1 else nj
                        b_reads = 1 if (nk == 1 and nj == 1) else ni
                    else:
                        b_reads = 1 if nk == 1 else ni
                        a_reads = 1 if (nk == 1 and ni == 1) else nj
                    traffic = a_reads * m * k * sa + b_reads * k * n * sb + m * n * (so + sr)
                    t = max(traffic / V7X_HBM_BYTES_PER_S, 2.0 * m * n * k / V7X_MXU_FLOPS_PER_S)
                    t += ni * nj * nk * GRID_STEP_S
                    t += (tm * tk * sa + tk * tn * sb + tm * tn * so) / V7X_HBM_BYTES_PER_S
                    if best is None or t < best[0]:
                        best = (t, tm, tn, tk, i_outer)
    assert best is not None, (m, n, k)
    return best[1:]


def mm(a, b, *, ta=False, tb=False, res=None, out_dtype=F32, dep=None, name):
    m, k = (a.shape[1], a.shape[0]) if ta else a.shape
    n = b.shape[0] if tb else b.shape[1]
    assert (b.shape[1] if tb else b.shape[0]) == k
    has_res = res is not None
    tm, tn, tk, i_outer = _mm_tiles(m, n, k, a.dtype.itemsize, b.dtype.itemsize, jnp.dtype(out_dtype).itemsize,
                                    res.dtype.itemsize if has_res else 0)
    nk = k // tk
    deps = [] if dep is None else [dep]
    dn = (((0 if ta else 1,), (1 if tb else 0,)), ((), ()))

    def body_single(*refs):
        a_ref, b_ref = refs[:2]
        o_ref = refs[-1]
        out = lax.dot_general(a_ref[...].astype(BF16), b_ref[...].astype(BF16), dn, preferred_element_type=F32)
        if has_res:
            out = out + refs[2][...].astype(F32)
        o_ref[...] = out.astype(o_ref.dtype)

    def body(*refs):
        a_ref, b_ref = refs[:2]
        r_ref = refs[2] if has_res else None
        o_ref, acc_ref = refs[-2:]
        kk = pl.program_id(2)
        part = lax.dot_general(a_ref[...].astype(BF16), b_ref[...].astype(BF16), dn, preferred_element_type=F32)

        @pl.when(kk == 0)
        def _():
            acc_ref[...] = part

        @pl.when(kk > 0)
        def _():
            acc_ref[...] += part

        @pl.when(kk == nk - 1)
        def _():
            out = acc_ref[...]
            if has_res:
                out = out + r_ref[...].astype(F32)
            o_ref[...] = out.astype(o_ref.dtype)

    def ij(f):
        return (lambda g0, g1, q: f(g0, g1, q)) if i_outer else (lambda g0, g1, q: f(g1, g0, q))

    a_spec = pl.BlockSpec((tk, tm), ij(lambda i, j, q: (q, i))) if ta else pl.BlockSpec((tm, tk), ij(lambda i, j, q: (i, q)))
    b_spec = pl.BlockSpec((tn, tk), ij(lambda i, j, q: (j, q))) if tb else pl.BlockSpec((tk, tn), ij(lambda i, j, q: (q, j)))
    o_spec = pl.BlockSpec((tm, tn), ij(lambda i, j, q: (i, j)))
    in_specs = [a_spec, b_spec] + ([o_spec] if has_res else []) + [pl.BlockSpec((8, 128), lambda g0, g1, q: (0, 0))] * len(deps)
    args = (a, b) + ((res,) if has_res else ()) + tuple(deps)
    grid = (m // tm, n // tn, nk) if i_outer else (n // tn, m // tm, nk)
    return pl.pallas_call(
        body_single if nk == 1 else body, grid=grid, in_specs=in_specs, out_specs=o_spec,
        out_shape=jax.ShapeDtypeStruct((m, n), out_dtype),
        scratch_shapes=[] if nk == 1 else [pltpu.VMEM((tm, tn), F32)],
        compiler_params=_params(dimension_semantics=("parallel", "parallel", "arbitrary")), name=name,
    )(*args)


def rms_fwd(x, g, *, dep=None, name):
    L, D = x.shape
    tr = _pick(L, (256, 128))

    def body(x_ref, g_ref, *rest):
        o_ref = rest[-1]
        xv = x_ref[...]
        r = lax.rsqrt(jnp.mean(xv * xv, axis=-1, keepdims=True) + NORM_EPS)
        o_ref[...] = (xv * r * g_ref[...]).astype(o_ref.dtype)

    row = pl.BlockSpec((tr, D), lambda i: (i, 0))
    vec = pl.BlockSpec((1, D), lambda i: (0, 0))
    deps = [] if dep is None else [dep]
    return pl.pallas_call(body, grid=(L // tr,), in_specs=[row, vec] + [pl.BlockSpec((8, 128), lambda i: (0, 0))] * len(deps),
                          out_specs=row, out_shape=jax.ShapeDtypeStruct((L, D), BF16), name=name)(
        x, g.reshape(1, D), *deps)


def rms_bwd(x, g, dy, dres, *, name):
    L, D = x.shape
    tr = _pick(L, (256, 128))

    def body(x_ref, g_ref, dy_ref, dres_ref, dx_ref, dg_ref):
        xv = x_ref[...]
        r = lax.rsqrt(jnp.mean(xv * xv, axis=-1, keepdims=True) + NORM_EPS)
        xh = xv * r
        dyv = dy_ref[...].astype(F32)

        @pl.when(pl.program_id(0) == 0)
        def _():
            dg_ref[...] = jnp.zeros_like(dg_ref)

        dg_ref[...] += jnp.sum(dyv * xh, axis=0, keepdims=True)
        dxh = dyv * g_ref[...]
        dx_ref[...] = dres_ref[...] + r * (dxh - xh * jnp.mean(dxh * xh, axis=-1, keepdims=True))

    row = pl.BlockSpec((tr, D), lambda i: (i, 0))
    vec = pl.BlockSpec((1, D), lambda i: (0, 0))
    return pl.pallas_call(body, grid=(L // tr,), in_specs=[row, vec, row, row], out_specs=[row, vec],
                          out_shape=[jax.ShapeDtypeStruct((L, D), F32), jax.ShapeDtypeStruct((1, D), F32)],
                          compiler_params=_params(dimension_semantics=("arbitrary",)), name=name)(
        x, g.reshape(1, D), dy, dres)


def final_loss(h, g, target, *, name):
    L, D = h.shape
    tr = _pick(L, (256, 128))

    def body(x_ref, g_ref, t_ref, loss_ref, dx_ref, dg_ref):
        xv = x_ref[...]
        gv = g_ref[...]
        r = lax.rsqrt(jnp.mean(xv * xv, axis=-1, keepdims=True) + NORM_EPS)
        xh = xv * r
        err = xh * gv - t_ref[...]

        @pl.when(pl.program_id(0) == 0)
        def _():
            dg_ref[...] = jnp.zeros_like(dg_ref)
            loss_ref[...] = jnp.zeros_like(loss_ref)

        loss_ref[...] += 0.5 * jnp.sum(jnp.mean(err * err, axis=-1, keepdims=True), axis=0, keepdims=True)
        dyv = err * (1.0 / D)
        dg_ref[...] += jnp.sum(dyv * xh, axis=0, keepdims=True)
        dxh = dyv * gv
        dx_ref[...] = r * (dxh - xh * jnp.mean(dxh * xh, axis=-1, keepdims=True))

    row = pl.BlockSpec((tr, D), lambda i: (i, 0))
    vec = pl.BlockSpec((1, D), lambda i: (0, 0))
    one = pl.BlockSpec((1, 1), lambda i: (0, 0))
    return pl.pallas_call(body, grid=(L // tr,), in_specs=[row, vec, row], out_specs=[one, row, vec],
                          out_shape=[jax.ShapeDtypeStruct((1, 1), F32), jax.ShapeDtypeStruct((L, D), F32),
                                     jax.ShapeDtypeStruct((1, D), F32)],
                          compiler_params=_params(dimension_semantics=("arbitrary",)), name=name)(
        h, g.reshape(1, D), target)


def _cmul(ar, ai, br, bi):
    return ar * br - ai * bi, ar * bi + ai * br


def _powers(ar, ai):
    rows = [(ar, ai)]
    for _ in range(7):
        rows.append(_cmul(rows[-1][0], rows[-1][1], ar, ai))
    table = (jnp.concatenate([r[0] for r in rows], axis=0), jnp.concatenate([r[1] for r in rows], axis=0))
    return (rows[0], rows[1], rows[3]), table


def _block_scan(br, bi, steps, shift):
    yr, yi = br, bi
    for s, (pr, pi) in zip((1, 2, 4), steps):
        sr, si = shift(yr, s), shift(yi, s)
        yr, yi = yr + pr * sr - pi * si, yi + pr * si + pi * sr
    return yr, yi


def s5_core_fwd(proj, a_re, a_im, wb_re, wb_im, wc_re, wc_im, *, name):
    L = proj.shape[0]
    parts, cu, W = wb_re.shape

    def body(u_ref, ar_ref, ai_ref, wbr_ref, wbi_ref, wcr_ref, wci_ref, y_ref, xr_ref, xi_ref, br_ref, bi_ref):
        u = u_ref[...]
        br_ref[...] = _dot(u, wbr_ref[...])
        bi_ref[...] = _dot(u, wbi_ref[...])
        steps, (tr, ti) = _powers(ar_ref[...], ai_ref[...])
        row = lax.broadcasted_iota(jnp.int32, (8, W), 0)

        def shift(y, s):
            return jnp.where(row >= s, pltpu.roll(y, s, 0), 0.0)

        def step(t8, carry):
            cr, ci = carry
            base = pl.multiple_of(t8 * 8, 8)
            yr, yi = _block_scan(br_ref[pl.ds(base, 8), :], bi_ref[pl.ds(base, 8), :], steps, shift)
            xr = yr + tr * cr - ti * ci
            xi = yi + tr * ci + ti * cr
            xr_ref[pl.ds(base, 8), :] = xr
            xi_ref[pl.ds(base, 8), :] = xi
            return jnp.broadcast_to(xr[7:8, :], (8, W)), jnp.broadcast_to(xi[7:8, :], (8, W))

        zero = jnp.zeros((8, W), F32)
        lax.fori_loop(0, L // 8, step, (zero, zero), unroll=2)
        y_ref[...] = _dot(xr_ref[...], wcr_ref[...]) + _dot(xi_ref[...], wci_ref[...])

    ucol = pl.BlockSpec((L, cu), lambda t: (0, t))
    vec = pl.BlockSpec((1, W), lambda t: (0, t))
    col = pl.BlockSpec((L, W), lambda t: (0, t))
    wb = pl.BlockSpec((None, cu, W), lambda t: (t, 0, 0))
    wc = pl.BlockSpec((None, W, cu), lambda t: (t, 0, 0))
    return pl.pallas_call(body, grid=(parts,), in_specs=[ucol, vec, vec, wb, wb, wc, wc], out_specs=[ucol, col, col],
                          out_shape=[jax.ShapeDtypeStruct((L, parts * cu), F32)]
                          + [jax.ShapeDtypeStruct((L, parts * W), F32)] * 2,
                          scratch_shapes=[pltpu.VMEM((L, W), F32)] * 2,
                          compiler_params=_params(dimension_semantics=("parallel",)), name=name)(
        proj, a_re, a_im, wb_re, wb_im, wc_re, wc_im)


def s5_core_bwd(dy, du_d, proj, xs_re, xs_im, a_re, a_im, wb_re, wb_im, wc_re, wc_im, *, name):
    L = proj.shape[0]
    parts, cu, W = wb_re.shape

    def body(dy_ref, dud_ref, u_ref, xr_ref, xi_ref, ar_ref, ai_ref, wbr_ref, wbi_ref, wcr_ref, wci_ref,
             du_ref, dwbr_ref, dwbi_ref, dwcr_ref, dwci_ref, dar_ref, dai_ref, lr_ref, li_ref):
        dy = dy_ref[...]
        lr_ref[...] = _dot_nt(dy, wcr_ref[...])
        li_ref[...] = _dot_nt(dy, wci_ref[...])
        dwcr_ref[...] = _dot_tn(xr_ref[...], dy)
        dwci_ref[...] = _dot_tn(xi_ref[...], dy)
        ar, ai = ar_ref[...], -ai_ref[...]
        steps, (tr, ti) = _powers(ar, ai)
        tr = jnp.concatenate([tr[j:j + 1, :] for j in range(7, -1, -1)], axis=0)
        ti = jnp.concatenate([ti[j:j + 1, :] for j in range(7, -1, -1)], axis=0)
        row8 = lax.broadcasted_iota(jnp.int32, (8, W), 0)
        nblk = L // 8

        def shift(y, s):
            return jnp.where(row8 < 8 - s, pltpu.roll(y, 8 - s, 0), 0.0)

        def step(s, carry):
            cr, ci = carry
            base = pl.multiple_of((nblk - 1 - s) * 8, 8)
            yr, yi = _block_scan(lr_ref[pl.ds(base, 8), :], li_ref[pl.ds(base, 8), :], steps, shift)
            lr = yr + tr * cr - ti * ci
            li = yi + tr * ci + ti * cr
            lr_ref[pl.ds(base, 8), :] = lr
            li_ref[pl.ds(base, 8), :] = li
            return jnp.broadcast_to(lr[0:1, :], (8, W)), jnp.broadcast_to(li[0:1, :], (8, W))

        zero = jnp.zeros((8, W), F32)
        lax.fori_loop(0, nblk, step, (zero, zero), unroll=2)
        row = lax.broadcasted_iota(jnp.int32, (L, W), 0)
        xpr = jnp.where(row >= 1, pltpu.roll(xr_ref[...], 1, 0), 0.0)
        xpi = jnp.where(row >= 1, pltpu.roll(xi_ref[...], 1, 0), 0.0)
        lr, li = lr_ref[...], li_ref[...]
        dar_ref[...] = jnp.sum(lr * xpr + li * xpi, axis=0, keepdims=True)
        dai_ref[...] = jnp.sum(li * xpr - lr * xpi, axis=0, keepdims=True)
        u = u_ref[...]
        dwbr_ref[...] = _dot_tn(u, lr)
        dwbi_ref[...] = _dot_tn(u, li)
        du_ref[...] = (dud_ref[...] + _dot_nt(lr, wbr_ref[...]) + _dot_nt(li, wbi_ref[...])).astype(du_ref.dtype)

    ucol = pl.BlockSpec((L, cu), lambda t: (0, t))
    vec = pl.BlockSpec((1, W), lambda t: (0, t))
    col = pl.BlockSpec((L, W), lambda t: (0, t))
    wb = pl.BlockSpec((None, cu, W), lambda t: (t, 0, 0))
    wc = pl.BlockSpec((None, W, cu), lambda t: (t, 0, 0))
    return pl.pallas_call(
        body, grid=(parts,), in_specs=[ucol, ucol, ucol, col, col, vec, vec, wb, wb, wc, wc],
        out_specs=[ucol, wb, wb, wc, wc, vec, vec],
        out_shape=[jax.ShapeDtypeStruct((L, parts * cu), BF16)] + [jax.ShapeDtypeStruct((parts, cu, W), F32)] * 2
        + [jax.ShapeDtypeStruct((parts, W, cu), F32)] * 2 + [jax.ShapeDtypeStruct((1, parts * W), F32)] * 2,
        scratch_shapes=[pltpu.VMEM((L, W), F32)] * 2,
        compiler_params=_params(dimension_semantics=("parallel",)), name=name,
    )(dy, du_d, proj, xs_re, xs_im, a_re, a_im, wb_re, wb_im, wc_re, wc_im)


def _gelu(y):
    c = math.sqrt(2.0 / math.pi)
    t = jnp.tanh(c * (y + 0.044715 * y * y * y))
    return 0.5 * y * (1.0 + t), t


def s5_out_fwd(y0, proj, dvec, glu_w, glu_b, *, name):
    L, C = y0.shape
    tr = _pick(L, (256, 128))

    def body(y_ref, u_ref, d_ref, w_ref, b_ref, o_ref):
        z, _ = _gelu(y_ref[...] + d_ref[...] * u_ref[...])
        zg = _dot(z, w_ref[...]) + b_ref[...]
        o_ref[...] = (z * _sigmoid(zg)).astype(o_ref.dtype)

    row = pl.BlockSpec((tr, C), lambda i: (i, 0))
    vec = pl.BlockSpec((1, C), lambda i: (0, 0))
    wsp = pl.BlockSpec((C, C), lambda i: (0, 0))
    return pl.pallas_call(body, grid=(L // tr,), in_specs=[row, row, vec, wsp, vec], out_specs=row,
                          out_shape=jax.ShapeDtypeStruct((L, C), BF16), name=name)(
        y0, proj, dvec, glu_w, glu_b)


def s5_out_bwd(y0, proj, dvec, glu_w, glu_b, dcat, *, name):
    L, C = y0.shape
    tr = _pick(L, (256, 128))

    def body(y_ref, u_ref, d_ref, w_ref, b_ref, do_ref, dy_ref, dud_ref, z_ref, dzg_ref, db_ref, dd_ref):
        u = u_ref[...]
        y = y_ref[...] + d_ref[...] * u
        z, t = _gelu(y)
        zg = _dot(z, w_ref[...]) + b_ref[...]
        s = _sigmoid(zg)
        do = do_ref[...]
        dzg = do * z * s * (1.0 - s)
        dz = do * s + _dot_nt(dzg, w_ref[...])
        c = math.sqrt(2.0 / math.pi)
        dgelu = 0.5 * (1.0 + t) + 0.5 * y * (1.0 - t * t) * c * (1.0 + 3.0 * 0.044715 * y * y)
        dy = dz * dgelu

        @pl.when(pl.program_id(0) == 0)
        def _():
            db_ref[...] = jnp.zeros_like(db_ref)
            dd_ref[...] = jnp.zeros_like(dd_ref)

        db_ref[...] += jnp.sum(dzg, axis=0, keepdims=True)
        dd_ref[...] += jnp.sum(dy * u, axis=0, keepdims=True)
        dy_ref[...] = dy
        dud_ref[...] = dy * d_ref[...]
        z_ref[...] = z.astype(BF16)
        dzg_ref[...] = dzg.astype(BF16)

    row = pl.BlockSpec((tr, C), lambda i: (i, 0))
    vec = pl.BlockSpec((1, C), lambda i: (0, 0))
    wsp = pl.BlockSpec((C, C), lambda i: (0, 0))
    return pl.pallas_call(body, grid=(L // tr,), in_specs=[row, row, vec, wsp, vec, row],
                          out_specs=[row, row, row, row, vec, vec],
                          out_shape=[jax.ShapeDtypeStruct((L, C), F32), jax.ShapeDtypeStruct((L, C), F32),
                                     jax.ShapeDtypeStruct((L, C), BF16), jax.ShapeDtypeStruct((L, C), BF16),
                                     jax.ShapeDtypeStruct((1, C), F32), jax.ShapeDtypeStruct((1, C), F32)],
                          compiler_params=_params(dimension_semantics=("arbitrary",)), name=name)(
        y0, proj, dvec, glu_w, glu_b, dcat)


def _dot_tri(tri, x, tri_left=True):
    t = tri.astype(BF16)
    x1 = x.astype(BF16)
    r1 = x - x1.astype(F32)
    x2 = r1.astype(BF16)
    x3 = (r1 - x2.astype(F32)).astype(BF16)
    dot = (lambda p: jnp.dot(t, p, preferred_element_type=F32)) if tri_left else (
        lambda p: jnp.dot(p, t, preferred_element_type=F32))
    return dot(x1) + dot(x2) + dot(x3)


def _hg_gates(xq, xf, lb, tri):
    C = xq.shape[0]
    sq = _sigmoid(xq)
    q = xq * sq
    sg = _sigmoid(xf)
    f = lb + (1.0 - lb) * sg
    kk = 1.0 - f
    b = _dot_tri(tri, jnp.log(f))
    bm = b[C // 2 - 1:C // 2, :]
    bl = b[C - 1:C, :]
    eb = jnp.exp(b)
    eqm, ekm, ekl = jnp.exp(b - bm), jnp.exp(bm - b), jnp.exp(bl - b)
    return dict(sq=sq, q=q, sg=sg, f=f, kk=kk, eb=eb, ebl=jnp.exp(bl), eqm=eqm, ekm=ekm, ekl=ekl,
                qb=q * eb, qt=q * eqm, kt=kk * ekm, kh=kk * ekl)


def _tri(C, lower):
    r = lax.broadcasted_iota(jnp.int32, (C, C), 0)
    c = lax.broadcasted_iota(jnp.int32, (C, C), 1)
    return (r >= c) if lower else (c >= r)


def hgrn_fwd(proj, lb, norm_g, *, name):
    L = proj.shape[0]
    C, H, K = HG_CHUNK, HG_HEADS, HG_DIM
    HK = H * K
    nc = L // C

    def body(q_ref, f_ref, i_ref, g_ref, lb_ref, ng_ref, o_ref, sall_ref, st_ref):
        @pl.when(pl.program_id(0) == 0)
        def _():
            st_ref[...] = jnp.zeros_like(st_ref)

        mask = _tri(C, True)
        sts = [st_ref[h] for h in range(H)]
        for s in range(S):
            rs = slice(s * C, (s + 1) * C)
            gt = _hg_gates(q_ref[rs, :], f_ref[rs, :], lb_ref[...], mask.astype(F32))
            v_all = i_ref[rs, :]
            outs = []
            for h in range(H):
                sl = slice(h * K, (h + 1) * K)
                v, st = v_all[:, sl], sts[h]
                sall_ref[s, h] = st
                att = jnp.where(mask, _dot_nt(gt["qt"][:, sl], gt["kt"][:, sl]), 0.0)
                o = _dot(att, v) + _dot_nt(gt["qb"][:, sl], st)
                sts[h] = st * gt["ebl"][:, sl] + _dot_tn(v, gt["kh"][:, sl])
                outs.append(o * lax.rsqrt(jnp.mean(o * o, axis=-1, keepdims=True) + NORM_EPS))
            xg = g_ref[rs, :]
            o_ref[rs, :] = (jnp.concatenate(outs, axis=1) * ng_ref[...] * (xg * _sigmoid(xg))).astype(o_ref.dtype)
        for h in range(H):
            st_ref[h] = sts[h]

    S = HG_STEP_CHUNKS

    def blk(cb):
        return pl.BlockSpec((S * C, HK), lambda i: (i, cb))

    vec = pl.BlockSpec((1, HK), lambda i: (0, 0))
    return pl.pallas_call(
        body, grid=(nc // S,), in_specs=[blk(1), blk(2), blk(3), blk(4), vec, vec],
        out_specs=[pl.BlockSpec((S * C, HK), lambda i: (i, 0)), pl.BlockSpec((S, H, K, K), lambda i: (i, 0, 0, 0))],
        out_shape=[jax.ShapeDtypeStruct((L, HK), BF16), jax.ShapeDtypeStruct((nc, H, K, K), F32)],
        scratch_shapes=[pltpu.VMEM((H, K, K), F32)],
        compiler_params=_params(dimension_semantics=("arbitrary",)), name=name,
    )(proj, proj, proj, proj, lb, norm_g)


def hgrn_bwd(proj, lb, norm_g, sall, dcat, *, name):
    L = proj.shape[0]
    C, H, K = HG_CHUNK, HG_HEADS, HG_DIM
    HK = H * K
    nc = L // C

    def body(q_ref, f_ref, i_ref, g_ref, lb_ref, ng_ref, sall_ref, do_ref, dx_ref, dlb_ref, dng_ref, dst_ref):
        @pl.when(pl.program_id(0) == 0)
        def _():
            dst_ref[...] = jnp.zeros_like(dst_ref)
            dlb_ref[...] = jnp.zeros_like(dlb_ref)
            dng_ref[...] = jnp.zeros_like(dng_ref)

        mask = _tri(C, True)
        lb_all, ng = lb_ref[...], ng_ref[...]
        dsts = [dst_ref[h] for h in range(H)]
        for s in reversed(range(S)):
            rs = slice(s * C, (s + 1) * C)
            dsts = chunk_bwd(rs, s, dsts, mask, lb_all, ng, q_ref, f_ref, i_ref, g_ref, sall_ref, do_ref,
                             dx_ref, dlb_ref, dng_ref)
        for h in range(H):
            dst_ref[h] = dsts[h]

    def chunk_bwd(rs, s, dsts, mask, lb_all, ng, q_ref, f_ref, i_ref, g_ref, sall_ref, do_ref, dx_ref, dlb_ref, dng_ref):
        xq, xg, v_all = q_ref[rs, :], g_ref[rs, :], i_ref[rs, :]
        gt = _hg_gates(xq, f_ref[rs, :], lb_all, mask.astype(F32))
        sgg = _sigmoid(xg)
        d_ob = do_ref[rs, :]
        d_on = d_ob * (xg * sgg)
        doh = d_on * ng
        ohs, d_qts, d_qbs, d_kts, d_khs, dvs, d_bls, new_dsts = [], [], [], [], [], [], [], []
        for h in range(H):
            sl = slice(h * K, (h + 1) * K)
            v, st, dst = v_all[:, sl], sall_ref[s, h], dsts[h]
            qt, kt, kh, qb = gt["qt"][:, sl], gt["kt"][:, sl], gt["kh"][:, sl], gt["qb"][:, sl]
            att = jnp.where(mask, _dot_nt(qt, kt), 0.0)
            o = _dot(att, v) + _dot_nt(qb, st)
            r = lax.rsqrt(jnp.mean(o * o, axis=-1, keepdims=True) + NORM_EPS)
            oh = o * r
            do = r * (doh[:, sl] - oh * jnp.mean(doh[:, sl] * oh, axis=-1, keepdims=True))
            datt = jnp.where(mask, _dot_nt(do, v), 0.0)
            dvs.append(_dot_tn(att, do) + _dot_nt(kh, dst))
            d_qbs.append(_dot_x3(do, st))
            d_qts.append(_dot_x3(datt, kt))
            d_kts.append(_dot_x3(datt, qt, ((0,), (0,))))
            d_kh = _dot_x3(v, dst)
            d_khs.append(d_kh)
            d_bls.append(jnp.sum(dst * st, axis=0, keepdims=True) * gt["ebl"][:, sl]
                         + jnp.sum(d_kh * kh, axis=0, keepdims=True))
            new_dsts.append(dst * gt["ebl"][:, sl] + _dot_tn(do, qb))
            ohs.append(oh)
        oh, d_qt, d_qb, d_kt, d_kh, dv, d_bl = (jnp.concatenate(p, axis=1) for p in
                                                (ohs, d_qts, d_qbs, d_kts, d_khs, dvs, d_bls))
        dxg = d_ob * (oh * ng) * (sgg * (1.0 + xg * (1.0 - sgg)))
        dng_ref[...] += jnp.sum(d_on * oh, axis=0, keepdims=True)
        dq = d_qt * gt["eqm"] + d_qb * gt["eb"]
        db = d_qt * gt["qt"] + d_qb * gt["qb"] - d_kt * gt["kt"] - d_kh * gt["kh"]
        rowi = lax.broadcasted_iota(jnp.int32, (C, HK), 0)
        db = db + jnp.where(rowi == C - 1, d_bl, 0.0)
        dkk = d_kt * gt["ekm"] + d_kh * gt["ekl"]
        dlg = _dot_tri(_tri(C, False).astype(F32), db)
        df = dlg / gt["f"] - dkk
        sg, sq = gt["sg"], gt["sq"]
        dlb_ref[...] += jnp.sum(df * (1.0 - sg), axis=0, keepdims=True)
        dx_ref[rs, 0:HK] = (dq * (sq * (1.0 + xq * (1.0 - sq)))).astype(dx_ref.dtype)
        dx_ref[rs, HK:2 * HK] = (df * (1.0 - lb_all) * sg * (1.0 - sg)).astype(dx_ref.dtype)
        dx_ref[rs, 2 * HK:3 * HK] = dv.astype(dx_ref.dtype)
        dx_ref[rs, 3 * HK:4 * HK] = dxg.astype(dx_ref.dtype)
        return new_dsts

    S = HG_STEP_CHUNKS
    ns = nc // S

    def blk(cb):
        return pl.BlockSpec((S * C, HK), lambda i: (ns - 1 - i, cb))

    vec = pl.BlockSpec((1, HK), lambda i: (0, 0))
    return pl.pallas_call(
        body, grid=(ns,),
        in_specs=[blk(1), blk(2), blk(3), blk(4), vec, vec,
                  pl.BlockSpec((S, H, K, K), lambda i: (ns - 1 - i, 0, 0, 0)), blk(1)],
        out_specs=[pl.BlockSpec((S * C, 4 * HK), lambda i: (ns - 1 - i, 0)), vec, vec],
        out_shape=[jax.ShapeDtypeStruct((L, 4 * HK), BF16), jax.ShapeDtypeStruct((1, HK), F32),
                   jax.ShapeDtypeStruct((1, HK), F32)],
        scratch_shapes=[pltpu.VMEM((H, K, K), F32)],
        compiler_params=_params(dimension_semantics=("arbitrary",)), name=name,
    )(proj, proj, proj, proj, lb, norm_g, sall, dcat)


def _shift_down(x, k, row):
    return jnp.where(row >= k, pltpu.roll(x, k, 0), 0.0)


def _shift_up(x, k, row):
    n = x.shape[0]
    return jnp.where(row < n - k, pltpu.roll(x, n - k, 0), 0.0)


def convgate_fwd(hu, conv_w, conv_b, *, name):
    L, C2 = hu.shape
    C = C2 // 2
    tc = _pick(C, (256, 128))
    nb = C // tc

    def body(a_ref, b_ref, wa_ref, wb_ref, ba_ref, bb_ref, o_ref):
        row = lax.broadcasted_iota(jnp.int32, (L, tc), 0)

        def conv(x, w, bias):
            return w[2:3, :] * x + w[1:2, :] * _shift_down(x, 1, row) + w[0:1, :] * _shift_down(x, 2, row) + bias

        ca = conv(a_ref[...], wa_ref[...], ba_ref[...])
        cb = conv(b_ref[...], wb_ref[...], bb_ref[...])
        o_ref[...] = (ca * _sigmoid(ca) * cb).astype(o_ref.dtype)

    def col(off, rows):
        return pl.BlockSpec((rows, tc), lambda j: (0, j + off))

    return pl.pallas_call(
        body, grid=(nb,), in_specs=[col(0, L), col(nb, L), col(0, 3), col(nb, 3), col(0, 1), col(nb, 1)],
        out_specs=col(0, L), out_shape=jax.ShapeDtypeStruct((L, C), BF16),
        compiler_params=_params(dimension_semantics=("parallel",)), name=name,
    )(hu, hu, conv_w, conv_w, conv_b, conv_b)


def convgate_bwd(hu, conv_w, conv_b, dact, *, name):
    L, C2 = hu.shape
    C = C2 // 2
    tc = _pick(C, (256, 128))
    nb = C // tc

    def body(a_ref, b_ref, wa_ref, wb_ref, ba_ref, bb_ref, d_ref, dxa_ref, dxb_ref, dwa_ref, dwb_ref, dba_ref, dbb_ref):
        row = lax.broadcasted_iota(jnp.int32, (L, tc), 0)

        def conv(x, w, bias):
            x1 = _shift_down(x, 1, row)
            x2 = _shift_down(x, 2, row)
            return w[2:3, :] * x + w[1:2, :] * x1 + w[0:1, :] * x2 + bias, x1, x2

        xa, xb = a_ref[...], b_ref[...]
        wa, wb = wa_ref[...], wb_ref[...]
        ca, xa1, xa2 = conv(xa, wa, ba_ref[...])
        cb, xb1, xb2 = conv(xb, wb, bb_ref[...])
        d = d_ref[...]
        sa = _sigmoid(ca)
        dca = d * cb * (sa * (1.0 + ca * (1.0 - sa)))
        dcb = d * (ca * sa)

        def back(dc, w, x, x1, x2, dx_ref, dw_ref, db_ref):
            dx = w[2:3, :] * dc + w[1:2, :] * _shift_up(dc, 1, row) + w[0:1, :] * _shift_up(dc, 2, row)
            dx_ref[...] = dx.astype(dx_ref.dtype)
            dw_ref[...] = jnp.concatenate([jnp.sum(dc * x2, axis=0, keepdims=True),
                                           jnp.sum(dc * x1, axis=0, keepdims=True),
                                           jnp.sum(dc * x, axis=0, keepdims=True)], axis=0)
            db_ref[...] = jnp.sum(dc, axis=0, keepdims=True)

        back(dca, wa, xa, xa1, xa2, dxa_ref, dwa_ref, dba_ref)
        back(dcb, wb, xb, xb1, xb2, dxb_ref, dwb_ref, dbb_ref)

    def col(off, rows):
        return pl.BlockSpec((rows, tc), lambda j: (0, j + off))

    outs = pl.pallas_call(
        body, grid=(nb,),
        in_specs=[col(0, L), col(nb, L), col(0, 3), col(nb, 3), col(0, 1), col(nb, 1), col(0, L)],
        out_specs=[col(0, L), col(0, L), col(0, 3), col(0, 3), col(0, 1), col(0, 1)],
        out_shape=[jax.ShapeDtypeStruct((L, C), BF16)] * 2 + [jax.ShapeDtypeStruct((3, C), F32)] * 2
        + [jax.ShapeDtypeStruct((1, C), F32)] * 2,
        compiler_params=_params(dimension_semantics=("parallel",)), name=name,
    )(hu, hu, conv_w, conv_w, conv_b, conv_b, dact)
    dxa, dxb, dwa, dwb, dba, dbb = outs
    return (jnp.concatenate([dxa, dxb], axis=1), jnp.concatenate([dwa, dwb], axis=1),
            jnp.concatenate([dba, dbb], axis=1))


def rope_tables(positions):
    half = ROT_DIM // 2
    inv_freq = ROPE_THETA ** (-jnp.arange(half, dtype=F32) * 2.0 / ROT_DIM)
    ang = positions.astype(F32)[:, None] * inv_freq
    cos, sin = jnp.cos(ang), jnp.sin(ang)
    L = positions.shape[0]
    one = jnp.ones((L, ATT_E - ROT_DIM), F32)
    zero = jnp.zeros((L, ATT_E - ROT_DIM), F32)
    zh = jnp.zeros((L, half), F32)
    tc = jnp.concatenate([cos, cos, one], axis=1)
    ts1 = jnp.concatenate([zh, sin, zero], axis=1)
    ts2 = jnp.concatenate([-sin, zh, zero], axis=1)
    return tuple(jnp.concatenate([t, t], axis=1) for t in (tc, ts1, ts2))


def rope_fwd(qkv, tabs, *, name):
    L = qkv.shape[0]
    W = 512
    tr = _pick(L, (256, 128))
    nq = 1536 // W
    scale = ATT_E ** -0.5

    def body(x_ref, c_ref, s1_ref, s2_ref, *o_refs):
        c = jnp.concatenate([c_ref[...]] * 4, axis=1)
        s1 = jnp.concatenate([s1_ref[...]] * 4, axis=1)
        s2 = jnp.concatenate([s2_ref[...]] * 4, axis=1)
        for j, o_ref in enumerate(o_refs):
            x = x_ref[:, j * W:(j + 1) * W]
            if j < 2 * nq:
                x = x * c + pltpu.roll(x, 8, 1) * s1 + pltpu.roll(x, W - 8, 1) * s2
            if j < nq:
                x = x * scale
            o_ref[...] = x.astype(o_ref.dtype)

    slab = pl.BlockSpec((tr, W), lambda i: (i, 0))
    tab = pl.BlockSpec((tr, 128), lambda i: (i, 0))
    return pl.pallas_call(body, grid=(L // tr,), in_specs=[pl.BlockSpec((tr, 3 * nq * W), lambda i: (i, 0)), tab, tab, tab],
                          out_specs=[slab] * (3 * nq), out_shape=[jax.ShapeDtypeStruct((L, W), F32)] * (3 * nq),
                          compiler_params=_params(dimension_semantics=("parallel",)), name=name)(qkv, *tabs)


def rope_bwd(slabs, tabs, *, name):
    L, W = slabs[0].shape
    tr = _pick(L, (256, 128))
    nq = len(slabs) // 3
    scale = ATT_E ** -0.5

    def body(*refs):
        d_refs, (c_ref, s1_ref, s2_ref, o_ref) = refs[:3 * nq], refs[3 * nq:]
        c = jnp.concatenate([c_ref[...]] * 4, axis=1)
        s1 = jnp.concatenate([s1_ref[...]] * 4, axis=1)
        s2 = jnp.concatenate([s2_ref[...]] * 4, axis=1)
        for j, d_ref in enumerate(d_refs):
            dy = d_ref[...]
            if j < 2 * nq:
                dy = dy * c + pltpu.roll(dy * s1, W - 8, 1) + pltpu.roll(dy * s2, 8, 1)
            if j < nq:
                dy = dy * scale
            o_ref[:, j * W:(j + 1) * W] = dy.astype(o_ref.dtype)

    slab = pl.BlockSpec((tr, W), lambda i: (i, 0))
    tab = pl.BlockSpec((tr, 128), lambda i: (i, 0))
    return pl.pallas_call(body, grid=(L // tr,), in_specs=[slab] * (3 * nq) + [tab, tab, tab],
                          out_specs=pl.BlockSpec((tr, 3 * nq * W), lambda i: (i, 0)),
                          out_shape=jax.ShapeDtypeStruct((L, 3 * nq * W), BF16),
                          compiler_params=_params(dimension_semantics=("parallel",)), name=name)(*slabs, *tabs)


def _att_masks(has_prev):
    qi = lax.broadcasted_iota(jnp.int32, (ATT_BLOCK, ATT_BLOCK), 0)
    kj = lax.broadcasted_iota(jnp.int32, (ATT_BLOCK, ATT_BLOCK), 1)
    return qi >= kj, (kj >= qi) & has_prev


ATT_COLS = 128


def _att_rows(j, d, nb):
    B = ATT_BLOCK
    r, n = j // nb, j % nb
    start = r + d * B * n
    has_prev = n > 0
    pstart = jnp.where(has_prev, start - d * B, start)
    if d == 1:
        return pl.ds(pl.multiple_of(start, B), B), pl.ds(pl.multiple_of(pstart, B), B), has_prev
    return pl.ds(start, B, stride=d), pl.ds(pstart, B, stride=d), has_prev


def attn_fwd(q, k, v, d, *, name):
    L, W = q.shape
    B, E = ATT_BLOCK, ATT_E
    nblk = L // B
    nb = nblk // d

    def body(q_ref, k_ref, v_ref, o_ref, l_ref):
        def step(j, carry):
            cur, prv, has_prev = _att_rows(j, d, nb)
            mc, mp = _att_masks(has_prev)
            qb, kc, kp, vc, vp = q_ref[cur, :], k_ref[cur, :], k_ref[prv, :], v_ref[cur, :], v_ref[prv, :]
            outs, lses = [], []
            for h in range(ATT_COLS // E):
                sl = slice(h * E, (h + 1) * E)
                sc = jnp.where(mc, _dot_nt(qb[:, sl], kc[:, sl]), NEG_BIG)
                sp = jnp.where(mp, _dot_nt(qb[:, sl], kp[:, sl]), NEG_BIG)
                m = jnp.maximum(jnp.max(sc, axis=-1, keepdims=True), jnp.max(sp, axis=-1, keepdims=True))
                pc = jnp.exp(sc - m)
                pp = jnp.exp(sp - m)
                den = jnp.sum(pc, axis=-1, keepdims=True) + jnp.sum(pp, axis=-1, keepdims=True)
                outs.append((_dot(pc, vc[:, sl]) + _dot(pp, vp[:, sl])) / den)
                lses.append(jnp.broadcast_to(m + jnp.log(den), (B, E)))
            o_ref[cur, :] = jnp.concatenate(outs, axis=1)
            l_ref[cur, :] = jnp.concatenate(lses, axis=1)
            return carry

        lax.fori_loop(0, nblk, step, 0, unroll=4)

    col = pl.BlockSpec((L, ATT_COLS), lambda c: (0, c))
    return pl.pallas_call(body, grid=(W // ATT_COLS,), in_specs=[col] * 3, out_specs=[col] * 2,
                          out_shape=[jax.ShapeDtypeStruct((L, W), F32)] * 2,
                          compiler_params=_params(dimension_semantics=("parallel",)), name=name)(q, k, v)


def attn_bwd(q, k, v, lse, do, dl, d, *, name):
    L, W = q.shape
    B, E = ATT_BLOCK, ATT_E
    nblk = L // B
    nb = nblk // d

    def body(q_ref, k_ref, v_ref, l_ref, do_ref, dl_ref, dq_ref, dk_ref, dv_ref):
        dk_ref[...] = jnp.zeros_like(dk_ref)
        dv_ref[...] = jnp.zeros_like(dv_ref)

        def step(j, carry):
            cur, prv, has_prev = _att_rows(j, d, nb)
            mc, mp = _att_masks(has_prev)
            qb, kc, kp, vc, vp = q_ref[cur, :], k_ref[cur, :], k_ref[prv, :], v_ref[cur, :], v_ref[prv, :]
            lb, dob, dlb = l_ref[cur, :], do_ref[cur, :], dl_ref[cur, :]
            dqs, dkc, dkp, dvc, dvp = [], [], [], [], []
            for h in range(ATT_COLS // E):
                sl = slice(h * E, (h + 1) * E)
                qh, doh = qb[:, sl], dob[:, sl]
                lse_h, dl_h = lb[:, h * E:h * E + 1], dlb[:, h * E:h * E + 1]
                pc = jnp.where(mc, jnp.exp(_dot_nt(qh, kc[:, sl]) - lse_h), 0.0)
                pp = jnp.where(mp, jnp.exp(_dot_nt(qh, kp[:, sl]) - lse_h), 0.0)
                dsc = pc * (_dot_nt(doh, vc[:, sl]) - dl_h)
                dsp = pp * (_dot_nt(doh, vp[:, sl]) - dl_h)
                dqs.append(_dot(dsc, kc[:, sl]) + _dot(dsp, kp[:, sl]))
                dkc.append(_dot_tn(dsc, qh))
                dkp.append(_dot_tn(dsp, qh))
                dvc.append(_dot_tn(pc, doh))
                dvp.append(_dot_tn(pp, doh))
            dq_ref[cur, :] = jnp.concatenate(dqs, axis=1)
            dk_ref[cur, :] = dk_ref[cur, :] + jnp.concatenate(dkc, axis=1)
            dv_ref[cur, :] = dv_ref[cur, :] + jnp.concatenate(dvc, axis=1)
            dk_ref[prv, :] = dk_ref[prv, :] + jnp.concatenate(dkp, axis=1)
            dv_ref[prv, :] = dv_ref[prv, :] + jnp.concatenate(dvp, axis=1)
            return carry

        lax.fori_loop(0, nblk, step, 0, unroll=4)

    col = pl.BlockSpec((L, ATT_COLS), lambda c: (0, c))
    return pl.pallas_call(body, grid=(W // ATT_COLS,), in_specs=[col] * 6, out_specs=[col] * 3,
                          out_shape=[jax.ShapeDtypeStruct((L, W), F32)] * 3,
                          compiler_params=_params(dimension_semantics=("parallel",)), name=name)(q, k, v, lse, do, dl)


def _merge_alpha(l_refs):
    ls = [r[...] for r in l_refs]
    m = jnp.maximum(jnp.maximum(ls[0], ls[1]), ls[2])
    es = [jnp.exp(l - m) for l in ls]
    den = es[0] + es[1] + es[2]
    return [e / den for e in es]


def merge_fwd(os_, ls_, *, name):
    L, W = os_[0].shape
    tr = _pick(L, (256, 128))

    def body(o0, o1, o2, l0, l1, l2, out_ref):
        al = _merge_alpha((l0, l1, l2))
        out_ref[...] = (al[0] * o0[...] + al[1] * o1[...] + al[2] * o2[...]).astype(out_ref.dtype)

    row = pl.BlockSpec((tr, W), lambda i: (i, 0))
    return pl.pallas_call(body, grid=(L // tr,), in_specs=[row] * 6, out_specs=row,
                          out_shape=jax.ShapeDtypeStruct((L, W), BF16), name=name)(*os_, *ls_)


def merge_bwd(os_, ls_, do, *, name):
    L, W = do.shape
    tr = _pick(L, (256, 128))

    def body(o0, o1, o2, l0, l1, l2, do_ref, d0, d1, d2, e0, e1, e2):
        al = _merge_alpha((l0, l1, l2))
        dov = do_ref[...]
        r = lax.broadcasted_iota(jnp.int32, (W, W), 0) // ATT_E
        c = lax.broadcasted_iota(jnp.int32, (W, W), 1) // ATT_E
        ones_blk = (r == c).astype(F32)
        t = jnp.zeros_like(dov)
        for a, o in zip(al, (o0, o1, o2)):
            t = t + a * _dot_tri(ones_blk, dov * o[...], tri_left=False)
        for a, d_ref, e_ref in zip(al, (d0, d1, d2), (e0, e1, e2)):
            d_ref[...] = a * dov
            e_ref[...] = a * t

    row = pl.BlockSpec((tr, W), lambda i: (i, 0))
    return pl.pallas_call(body, grid=(L // tr,), in_specs=[row] * 7, out_specs=[row] * 6,
                          out_shape=[jax.ShapeDtypeStruct((L, W), F32)] * 6, name=name)(*os_, *ls_, do)


def _me_and_peers():
    x, y, c = lax.axis_index("x"), lax.axis_index("y"), lax.axis_index("c")
    peers = []
    for k in range(1, N_DEV):
        px = 1 - x if k & 4 else x
        py = 1 - y if k & 2 else y
        pc = 1 - c if k & 1 else c
        peers.append((px, py, pc))
    return (x, y, c), peers


def _index(dev):
    return 4 * dev[0] + 2 * dev[1] + dev[2]


def _hbm(a):
    return pltpu.with_memory_space_constraint(a, pltpu.HBM)


HBM_SPEC = pl.BlockSpec(memory_space=pltpu.HBM)
SEM_SPEC = pl.BlockSpec(memory_space=pltpu.SEMAPHORE)
DATAFLOW = pltpu.SideEffectType.DATAFLOW_SIDE_EFFECTING


def _remote(src_ref, land_ref, slotted, me, peer, src_is_mine, send_sem, recv_sem, k):
    sender, receiver = (me, peer) if src_is_mine else (peer, me)
    src = src_ref.at[_index(receiver)] if slotted else src_ref
    return pltpu.make_async_remote_copy(src_ref=src, dst_ref=land_ref.at[_index(sender)], send_sem=send_sem.at[k],
                                        recv_sem=recv_sem.at[k], device_id=peer, device_id_type=MESH_ID)


def copies_start(arrays, slotted, *, name):
    n = len(arrays)
    lands = [lax.empty(a.shape if slotted else (N_DEV,) + a.shape, a.dtype) for a in arrays]

    def body(*refs):
        x_refs, land_refs = refs[:n], refs[n:2 * n]
        send, recv = refs[2 * n:3 * n], refs[3 * n:4 * n]
        token = refs[-1]
        me, peers = _me_and_peers()
        for w in range(n):
            for k, peer in enumerate(peers):
                _remote(x_refs[w], land_refs[w], slotted, me, peer, True, send[w], recv[w], k).start()
            if not slotted:
                pltpu.make_async_copy(x_refs[w], land_refs[w].at[_index(me)], recv[w].at[N_DEV - 1]).start()
        token[...] = jnp.zeros_like(token)

    sem = pltpu.SemaphoreType.DMA((N_DEV,))
    out_shape = ([sem] * (2 * n) + [pltpu.HBM(a.shape, a.dtype) for a in arrays]
                 + [pltpu.HBM(l.shape, l.dtype) for l in lands] + [jax.ShapeDtypeStruct((8, 128), F32)])
    outs = pl.pallas_call(
        body, name=name, out_shape=out_shape, in_specs=[HBM_SPEC] * (2 * n),
        out_specs=[SEM_SPEC] * (2 * n) + [HBM_SPEC] * (2 * n) + [pl.BlockSpec(memory_space=pltpu.VMEM)],
        input_output_aliases={i: 2 * n + i for i in range(2 * n)},
        compiler_params=pltpu.CompilerParams(has_side_effects=DATAFLOW),
    )(*[_hbm(a) for a in arrays], *[_hbm(l) for l in lands])
    handles = [(outs[w], outs[n + w], outs[2 * n + w], outs[3 * n + w]) for w in range(n)]
    return handles, outs[-1]


def copies_wait(handle, slotted, after, *, name):
    send_sem, recv_sem, x_thru, land_thru = handle

    def body(x_ref, land_ref, send_ref, recv_ref, after_ref, x_out, land_out):
        me, peers = _me_and_peers()
        for k, peer in enumerate(peers):
            _remote(x_ref, land_ref, slotted, me, peer, True, send_ref, recv_ref, k).wait_send()
        for k, peer in enumerate(peers):
            _remote(x_ref, land_ref, slotted, me, peer, False, send_ref, recv_ref, k).wait_recv()
        if not slotted:
            pltpu.make_async_copy(x_ref, land_ref.at[_index(me)], recv_ref.at[N_DEV - 1]).wait()

    return pl.pallas_call(
        body, name=name, out_shape=(pltpu.HBM(x_thru.shape, x_thru.dtype), pltpu.HBM(land_thru.shape, land_thru.dtype)),
        in_specs=(HBM_SPEC, HBM_SPEC, SEM_SPEC, SEM_SPEC, pl.BlockSpec(memory_space=pl.ANY)),
        out_specs=(HBM_SPEC, HBM_SPEC), input_output_aliases={0: 0, 1: 1},
        compiler_params=pltpu.CompilerParams(has_side_effects=DATAFLOW),
    )(x_thru, land_thru, send_sem, recv_sem, after)


def cast_bf16(x, *, ncols=None, name):
    R = x.shape[0]
    C = ncols or x.shape[1]
    tr = _pick(R, (512, 352, 256, 128, 64))

    def body(x_ref, o_ref):
        o_ref[...] = x_ref[...].astype(BF16)

    row = pl.BlockSpec((tr, C), lambda i: (i, 0))
    return pl.pallas_call(body, grid=(R // tr,), in_specs=[row], out_specs=row,
                          out_shape=jax.ShapeDtypeStruct((R, C), BF16), name=name)(x)


def cast_bf16_layer(x3, layer, *, name):
    _, R, C = x3.shape
    tr = _pick(R, (512, 352, 256, 128, 64))

    def body(x_ref, o_ref):
        o_ref[...] = x_ref[...].astype(BF16)

    return pl.pallas_call(body, grid=(R // tr,), in_specs=[pl.BlockSpec((None, tr, C), lambda i: (layer, i, 0))],
                          out_specs=pl.BlockSpec((tr, C), lambda i: (i, 0)),
                          out_shape=jax.ShapeDtypeStruct((R, C), BF16), name=name)(x3)


BD_PARTS = 4


def _blockdiag_call(b, build, G, r, c, name):
    gp = G // BD_PARTS

    def body_build(b_ref, o_ref):
        o_ref[...] = jnp.zeros_like(o_ref)
        for g in range(G):
            o_ref[g // gp, (g % gp) * r:(g % gp + 1) * r, (g % gp) * c:(g % gp + 1) * c] = b_ref[g]

    def body_extract(d_ref, o_ref):
        for g in range(G):
            o_ref[g] = d_ref[g // gp, (g % gp) * r:(g % gp + 1) * r, (g % gp) * c:(g % gp + 1) * c]

    out = jax.ShapeDtypeStruct((BD_PARTS, gp * r, gp * c) if build else (G, r, c), F32)
    return pl.pallas_call(body_build if build else body_extract, out_shape=out, name=name)(b)


def make_blockdiag(G, r, c, name):
    @jax.custom_vjp
    def blockdiag(b):
        return _blockdiag_call(b, True, G, r, c, name + "_build")

    def fwd(b):
        return blockdiag(b), None

    def bwd(_, g):
        return (_blockdiag_call(g, False, G, r, c, name + "_extract"),)

    blockdiag.defvjp(fwd, bwd)
    return blockdiag


def _my_index():
    return 4 * lax.axis_index("x") + 2 * lax.axis_index("y") + lax.axis_index("c")


def cols_from_shards(g, *, name):
    _, K, n = g.shape
    tk = _pick(K, (256, 128))

    def body(g_ref, o_ref):
        for i in range(N_DEV):
            o_ref[:, i * n:(i + 1) * n] = g_ref[i]

    return pl.pallas_call(body, grid=(K // tk,), in_specs=[pl.BlockSpec((N_DEV, tk, n), lambda i: (0, i, 0))],
                          out_specs=pl.BlockSpec((tk, N_DEV * n), lambda i: (i, 0)),
                          out_shape=jax.ShapeDtypeStruct((K, N_DEV * n), g.dtype), name=name)(g)


def shards_from_cols(w, *, name):
    K, N = w.shape
    n = N // N_DEV
    tk = _pick(K, (256, 128))

    def body(w_ref, o_ref):
        for i in range(N_DEV):
            o_ref[i] = w_ref[:, i * n:(i + 1) * n].astype(o_ref.dtype)

    return pl.pallas_call(body, grid=(K // tk,), in_specs=[pl.BlockSpec((tk, N), lambda i: (i, 0))],
                          out_specs=pl.BlockSpec((N_DEV, tk, n), lambda i: (0, i, 0)),
                          out_shape=jax.ShapeDtypeStruct((N_DEV, K, n), BF16), name=name)(w)


def _adamw(w, g, m, v):
    m = ADAM_B1 * m + (1.0 - ADAM_B1) * g
    v = ADAM_B2 * v + (1.0 - ADAM_B2) * (g * g)
    m_hat = m / (1.0 - ADAM_B1 ** ADAM_STEP)
    v_hat = v / (1.0 - ADAM_B2 ** ADAM_STEP)
    delta = -ADAM_LR * (m_hat / (jnp.sqrt(v_hat) + ADAM_EPS) + ADAM_WD * w)
    return delta, m, v


def reduce_adamw(recv, own, own_slotted, me, w, m, v, *, layer=0, n_layers=1, into=None, name):
    _, R, C = recv.shape
    tr = _pick(R, (352, 320, 288, 256, 128, 64, 32, 16, 8))
    off = layer * (R // tr)

    def body(me_ref, r_ref, own_ref, w_ref, m_ref, v_ref, *rest):
        g_ref, d_ref, nm_ref, nv_ref = rest[-4:]
        mine = me_ref[0]
        g = None
        for i in range(N_DEV):
            part = jnp.where(mine == i, own_ref[...], r_ref[i]).astype(F32)
            g = part if g is None else g + part
        delta, nm, nv = _adamw(w_ref[...], g, m_ref[...], v_ref[...])
        g_ref[...] = g
        d_ref[...] = delta
        nm_ref[...] = nm
        nv_ref[...] = nv

    row = pl.BlockSpec((tr, C), lambda i, me_ref: (i + off, 0))
    own_spec = (pl.BlockSpec((None, tr, C), lambda i, me_ref: (me_ref[0], i, 0)) if own_slotted
                else pl.BlockSpec((tr, C), lambda i, me_ref: (i, 0)))
    rest = [] if into is None else list(into)
    grid_spec = pltpu.PrefetchScalarGridSpec(
        num_scalar_prefetch=1, grid=(R // tr,),
        in_specs=[pl.BlockSpec((N_DEV, tr, C), lambda i, me_ref: (0, i, 0)), own_spec, row, row, row]
        + [pl.BlockSpec(memory_space=pl.ANY)] * len(rest),
        out_specs=[row] * 4)
    return pl.pallas_call(body, grid_spec=grid_spec, out_shape=[jax.ShapeDtypeStruct((n_layers * R, C), F32)] * 4,
                          input_output_aliases={6 + k: k for k in range(len(rest))},
                          compiler_params=_params(dimension_semantics=("parallel",)), name=name)(
        me.reshape(1).astype(jnp.int32), recv, own, w, m, v, *rest)


def _s5_prepare(A_re, A_im, log_dt, B_re, B_im, C_re, C_im):
    G, P, Cg = S5_GROUPS, S5_STATE, S5_GROUP
    dt = jnp.exp(log_dt)[:, None]
    mag = jnp.exp(A_re * dt)
    ab_re = mag * jnp.cos(A_im * dt)
    ab_im = mag * jnp.sin(A_im * dt)
    den = A_re * A_re + A_im * A_im
    nr, ni = ab_re - 1.0, ab_im
    c_re = (nr * A_re + ni * A_im) / den
    c_im = (ni * A_re - nr * A_im) / den
    Bb_re = c_re[..., None] * B_re - c_im[..., None] * B_im
    Bb_im = c_re[..., None] * B_im + c_im[..., None] * B_re
    def dense_in(b, name):
        return make_blockdiag(G, Cg, P, name)(b.transpose(0, 2, 1))

    def dense_out(c, name):
        return make_blockdiag(G, P, Cg, name)(c.transpose(0, 2, 1))

    return (ab_re.reshape(1, G * P), ab_im.reshape(1, G * P), dense_in(Bb_re, "s5_wb_re"), dense_in(Bb_im, "s5_wb_im"),
            dense_out(C_re, "s5_wc_re"), dense_out(-C_im, "s5_wc_im"))


def _lower_bound(gamma):
    return jnp.cumsum(jax.nn.softmax(gamma, axis=0), axis=0)[0:1]


def _ffn_fwd(h, g_norm, get_w_in, conv_w, conv_b, get_w_out, tag):
    hn = rms_fwd(h, g_norm, name=tag + "_rms")
    w_in = get_w_in(hn)
    hu = mm(hn, w_in, tb=True, name=tag + "_in")
    act = convgate_fwd(hu, conv_w, conv_b, name=tag + "_gate")
    w_out = get_w_out(act)
    h_out = mm(act, w_out, res=h, name=tag + "_out")
    return h_out, (hn, hu, act), w_in, w_out


def _ffn_bwd(h, g_norm, w_in, conv_w, conv_b, w_out, saved, dh, tag, send_dw_in, send_dw_out):
    hn, hu, act = saved
    sent = send_dw_out(mm(act, dh, ta=True, out_dtype=BF16, name=tag + "_dwout"))
    dact = mm(dh, w_out, tb=True, dep=sent, name=tag + "_dact")
    dhu, dconv_w, dconv_b = convgate_bwd(hu, conv_w, conv_b, dact, name=tag + "_dgate")
    sent = send_dw_in(mm(dhu, hn, ta=True, out_dtype=BF16, name=tag + "_dwin"))
    dhn = mm(dhu, w_in, dep=sent, name=tag + "_dhn")
    dh_in, dg = rms_bwd(h, g_norm, dhn, dh, name=tag + "_drms")
    return dh_in, dg, dconv_w, dconv_b


def kernel(x, positions, norm_mix, norm_ffn, norm_final, mix_w_in, mix_w_out, s5_A_re, s5_A_im, s5_log_dt, s5_B_re, s5_B_im, s5_C_re, s5_C_im, s5_D, s5_glu_w, s5_glu_b, hgrn_gamma, hgrn_norm, att_w_qkv, att_w_o, ffn_w_in, ffn_conv_w, ffn_conv_b, ffn_w_out, loss_target, m_norm_mix, m_norm_ffn, m_norm_final, m_mix_w_in, m_mix_w_out, m_s5_A_re, m_s5_A_im, m_s5_log_dt, m_s5_B_re, m_s5_B_im, m_s5_C_re, m_s5_C_im, m_s5_D, m_s5_glu_w, m_s5_glu_b, m_hgrn_gamma, m_hgrn_norm, m_att_w_qkv, m_att_w_o, m_ffn_w_in, m_ffn_conv_w, m_ffn_conv_b, m_ffn_w_out, v_norm_mix, v_norm_ffn, v_norm_final, v_mix_w_in, v_mix_w_out, v_s5_A_re, v_s5_A_im, v_s5_log_dt, v_s5_B_re, v_s5_B_im, v_s5_C_re, v_s5_C_im, v_s5_D, v_s5_glu_w, v_s5_glu_b, v_hgrn_gamma, v_hgrn_norm, v_att_w_qkv, v_att_w_o, v_ffn_w_in, v_ffn_conv_w, v_ffn_conv_b, v_ffn_w_out):
    W = dict(norm_mix=norm_mix, norm_ffn=norm_ffn, norm_final=norm_final, mix_w_in=mix_w_in, mix_w_out=mix_w_out,
             s5_A_re=s5_A_re, s5_A_im=s5_A_im, s5_log_dt=s5_log_dt, s5_B_re=s5_B_re, s5_B_im=s5_B_im,
             s5_C_re=s5_C_re, s5_C_im=s5_C_im, s5_D=s5_D, s5_glu_w=s5_glu_w, s5_glu_b=s5_glu_b,
             hgrn_gamma=hgrn_gamma, hgrn_norm=hgrn_norm, att_w_qkv=att_w_qkv, att_w_o=att_w_o, ffn_w_in=ffn_w_in,
             ffn_conv_w=ffn_conv_w, ffn_conv_b=ffn_conv_b, ffn_w_out=ffn_w_out)
    M = dict(norm_mix=m_norm_mix, norm_ffn=m_norm_ffn, norm_final=m_norm_final, mix_w_in=m_mix_w_in,
             mix_w_out=m_mix_w_out, s5_A_re=m_s5_A_re, s5_A_im=m_s5_A_im, s5_log_dt=m_s5_log_dt, s5_B_re=m_s5_B_re,
             s5_B_im=m_s5_B_im, s5_C_re=m_s5_C_re, s5_C_im=m_s5_C_im, s5_D=m_s5_D, s5_glu_w=m_s5_glu_w,
             s5_glu_b=m_s5_glu_b, hgrn_gamma=m_hgrn_gamma, hgrn_norm=m_hgrn_norm, att_w_qkv=m_att_w_qkv,
             att_w_o=m_att_w_o, ffn_w_in=m_ffn_w_in, ffn_conv_w=m_ffn_conv_w, ffn_conv_b=m_ffn_conv_b,
             ffn_w_out=m_ffn_w_out)
    V = dict(norm_mix=v_norm_mix, norm_ffn=v_norm_ffn, norm_final=v_norm_final, mix_w_in=v_mix_w_in,
             mix_w_out=v_mix_w_out, s5_A_re=v_s5_A_re, s5_A_im=v_s5_A_im, s5_log_dt=v_s5_log_dt, s5_B_re=v_s5_B_re,
             s5_B_im=v_s5_B_im, s5_C_re=v_s5_C_re, s5_C_im=v_s5_C_im, s5_D=v_s5_D, s5_glu_w=v_s5_glu_w,
             s5_glu_b=v_s5_glu_b, hgrn_gamma=v_hgrn_gamma, hgrn_norm=v_hgrn_norm, att_w_qkv=v_att_w_qkv,
             att_w_o=v_att_w_o, ffn_w_in=v_ffn_w_in, ffn_conv_w=v_ffn_conv_w, ffn_conv_b=v_ffn_conv_b,
             ffn_w_out=v_ffn_w_out)
    return _step(x[0], positions[0], loss_target[0], W, M, V)


TRANSPOSED = ("mix_w_in", "att_w_qkv", "ffn_w_in")
SMALL = ("norm_mix", "norm_ffn", "norm_final", "s5_A_re", "s5_A_im", "s5_log_dt", "s5_B_re", "s5_B_im", "s5_C_re",
         "s5_C_im", "s5_D", "s5_glu_b", "hgrn_gamma", "hgrn_norm", "ffn_conv_b")
ORDER = ("norm_mix", "norm_ffn", "norm_final", "mix_w_in", "mix_w_out", "s5_A_re", "s5_A_im", "s5_log_dt", "s5_B_re",
         "s5_B_im", "s5_C_re", "s5_C_im", "s5_D", "s5_glu_w", "s5_glu_b", "hgrn_gamma", "hgrn_norm", "att_w_qkv",
         "att_w_o", "ffn_w_in", "ffn_conv_w", "ffn_conv_b", "ffn_w_out")
PACK_COLS = 1024


def _step(x, positions, target, W, M, V):
    L, D = x.shape
    me = 4 * lax.axis_index("x") + 2 * lax.axis_index("y") + lax.axis_index("c")
    n_cw = W["ffn_conv_w"].shape[-1]
    T = {n: tuple(jnp.swapaxes(d[n], -1, -2) for d in (W, M, V)) for n in TRANSPOSED}
    shards = {
        "mix_w_in": cast_bf16(T["mix_w_in"][0][0], name="mix_w_in_cast"),
        "conv_w": W["ffn_conv_w"].reshape(6, n_cw),
        "s5_glu_w": cast_bf16(W["s5_glu_w"][0], name="s5_glu_w_cast"),
        "mix_w_out": cast_bf16(W["mix_w_out"][0], name="mix_w_out_cast"),
        "ffn_w_in0": cast_bf16_layer(T["ffn_w_in"][0], 0, name="ffn_w_in0_cast"),
        "ffn_w_out0": cast_bf16_layer(W["ffn_w_out"], 0, name="ffn_w_out0_cast"),
        "att_w_qkv": cast_bf16(T["att_w_qkv"][0][0], name="att_w_qkv_cast"),
        "att_w_o": cast_bf16(W["att_w_o"][0], name="att_w_o_cast"),
        "ffn_w_in1": cast_bf16_layer(T["ffn_w_in"][0], 1, name="ffn_w_in1_cast"),
        "ffn_w_out1": cast_bf16_layer(W["ffn_w_out"], 1, name="ffn_w_out1_cast"),
    }
    gather_handles, token = copies_start(list(shards.values()), False, name="gather_start")
    gather_handle = dict(zip(shards, gather_handles))

    def gathered(key, after, cols):
        _, land = copies_wait(gather_handle[key], False, after, name=key + "_gwait")
        return cols_from_shards(land, name=key + "_asm") if cols else land.reshape(-1, land.shape[-1])

    conv_b = W["ffn_conv_b"].reshape(2, 1, -1)

    s5_params = (W["s5_A_re"][0], W["s5_A_im"][0], W["s5_log_dt"][0], W["s5_B_re"][0], W["s5_B_im"][0],
                 W["s5_C_re"][0], W["s5_C_im"][0])
    (a_re, a_im, wb_re, wb_im, wc_re, wc_im), s5_prep_vjp = jax.vjp(_s5_prepare, *s5_params)
    dvec = W["s5_D"].reshape(1, S5_WIDTH)
    glu_b = W["s5_glu_b"].reshape(1, S5_WIDTH)
    lb, lb_vjp = jax.vjp(_lower_bound, W["hgrn_gamma"])
    hg_norm = W["hgrn_norm"].reshape(1, -1)
    tabs = rope_tables(positions)

    hn0 = rms_fwd(x, W["norm_mix"][0], dep=token, name="l0_rms")
    w_mix_in = gathered("mix_w_in", hn0, False)
    proj = mm(hn0, w_mix_in, tb=True, name="l0_proj")
    y0, xs_re, xs_im = s5_core_fwd(proj, a_re, a_im, wb_re, wb_im, wc_re, wc_im, name="s5_core")
    w_glu = gathered("s5_glu_w", y0, False)
    oa = s5_out_fwd(y0, proj, dvec, w_glu, glu_b, name="s5_out")
    ob, hg_states = hgrn_fwd(proj, lb, hg_norm, name="hgrn_fwd")
    cat = jnp.concatenate([oa, ob], axis=1)
    w_mix_out = gathered("mix_w_out", cat, False)
    h1 = mm(cat, w_mix_out, res=x, name="l0_mix_out")
    _, cw_all = copies_wait(gather_handle["conv_w"], False, h1, name="conv_w_gwait")
    conv_w = cw_all.transpose(1, 0, 2).reshape(2, 3, N_DEV * n_cw)
    w_ffn_in, w_ffn_out = [None, None], [None, None]
    h2, ffn0_saved, w_ffn_in[0], w_ffn_out[0] = _ffn_fwd(
        h1, W["norm_ffn"][0], lambda a: gathered("ffn_w_in0", a, False), conv_w[0], conv_b[0],
        lambda a: gathered("ffn_w_out0", a, False), "ffn0")

    hn2 = rms_fwd(h2, W["norm_mix"][1], name="l1_rms")
    w_qkv = gathered("att_w_qkv", hn2, False)
    qkv = mm(hn2, w_qkv, tb=True, name="l1_qkv")
    qkv_r = rope_fwd(qkv, tabs, name="rope_fwd")
    att_o, att_l = [], []
    for g, d in enumerate(ATT_DILATIONS):
        o_g, l_g = attn_fwd(qkv_r[g], qkv_r[3 + g], qkv_r[6 + g], d, name=f"attn_fwd{g}")
        att_o.append(o_g)
        att_l.append(l_g)
    o_att = merge_fwd(att_o, att_l, name="merge_fwd")
    w_o = gathered("att_w_o", o_att, True)
    h3 = mm(o_att, w_o, res=h2, name="l1_mix_out")
    h4, ffn1_saved, w_ffn_in[1], w_ffn_out[1] = _ffn_fwd(
        h3, W["norm_ffn"][1], lambda a: gathered("ffn_w_in1", a, False), conv_w[1], conv_b[1],
        lambda a: gathered("ffn_w_out1", a, False), "ffn1")

    exchanges = {}

    def send_grad(key, g, cols):
        if cols:
            parts = shards_from_cols(g, name=key + "_split")
        else:
            parts = g.reshape(N_DEV, g.shape[0] // N_DEV, g.shape[1])
        (handle,), sent = copies_start([parts], True, name=key + "_xstart")
        exchanges[key] = handle
        return sent

    loss, dh4, dg_final = final_loss(h4, W["norm_final"], target, name="final_loss")
    dh3, dg_ffn1, dcw1, dcb1 = _ffn_bwd(h3, W["norm_ffn"][1], w_ffn_in[1], conv_w[1], conv_b[1], w_ffn_out[1],
                                        ffn1_saved, dh4, "ffn1", lambda g: send_grad("ffn_w_in1", g, False),
                                        lambda g: send_grad("ffn_w_out1", g, False))
    sent = send_grad("att_w_o", mm(o_att, dh3, ta=True, name="l1_dwo"), True)
    d_oatt = mm(dh3, w_o, tb=True, dep=sent, name="l1_dmix")
    mb = merge_bwd(att_o, att_l, d_oatt, name="merge_bwd")
    d_slabs = [attn_bwd(qkv_r[g], qkv_r[3 + g], qkv_r[6 + g], att_l[g], mb[g], mb[3 + g], d, name=f"attn_bwd{g}")
               for g, d in enumerate(ATT_DILATIONS)]
    d_qkv = rope_bwd([s[0] for s in d_slabs] + [s[1] for s in d_slabs] + [s[2] for s in d_slabs], tabs,
                     name="rope_bwd")
    sent = send_grad("att_w_qkv", mm(d_qkv, hn2, ta=True, out_dtype=BF16, name="l1_dwqkv"), False)
    d_hn2 = mm(d_qkv, w_qkv, dep=sent, name="l1_dhn")
    dh2, dg_mix1 = rms_bwd(h2, W["norm_mix"][1], d_hn2, dh3, name="l1_drms")

    dh1, dg_ffn0, dcw0, dcb0 = _ffn_bwd(h1, W["norm_ffn"][0], w_ffn_in[0], conv_w[0], conv_b[0], w_ffn_out[0],
                                        ffn0_saved, dh2, "ffn0", lambda g: send_grad("ffn_w_in0", g, False),
                                        lambda g: send_grad("ffn_w_out0", g, False))
    sent = send_grad("mix_w_out", mm(cat, dh1, ta=True, out_dtype=BF16, name="l0_dwout"), False)
    dcat = mm(dh1, w_mix_out, tb=True, dep=sent, name="l0_dcat")
    d_hg, dlb, dhg_norm = hgrn_bwd(proj, lb, hg_norm, hg_states, dcat, name="hgrn_bwd")
    dy, du_d, z_bf, dzg, dglu_b, dD = s5_out_bwd(y0, proj, dvec, w_glu, glu_b, dcat, name="s5_dout")
    sent_glu = send_grad("s5_glu_w", mm(z_bf, dzg, ta=True, out_dtype=BF16, name="s5_dglu"), False)
    du, dwb_re, dwb_im, dwc_re, dwc_im, da_re, da_im = s5_core_bwd(
        dy, du_d, proj, xs_re, xs_im, a_re, a_im, wb_re, wb_im, wc_re, wc_im, name="s5_dcore")
    s5_small = s5_prep_vjp((da_re, da_im, dwb_re, dwb_im, dwc_re, dwc_im))
    d_proj = jnp.concatenate([du, d_hg], axis=1)
    sent = send_grad("mix_w_in", mm(d_proj, hn0, ta=True, out_dtype=BF16, dep=sent_glu, name="l0_dwin"), False)
    d_hn0 = mm(d_proj, w_mix_in, dep=sent, name="l0_dhn")
    grad_x, dg_mix0 = rms_bwd(x, W["norm_mix"][0], d_hn0, dh1, name="l0_drms")
    (d_gamma,) = lb_vjp(dlb)
    out = {}

    dA_re, dA_im, dlog_dt, dB_re, dB_im, dC_re, dC_im = s5_small
    small_g = dict(norm_mix=jnp.concatenate([dg_mix0, dg_mix1], axis=0), norm_ffn=jnp.concatenate([dg_ffn0, dg_ffn1], axis=0),
                   norm_final=dg_final, s5_A_re=dA_re, s5_A_im=dA_im, s5_log_dt=dlog_dt, s5_B_re=dB_re, s5_B_im=dB_im,
                   s5_C_re=dC_re, s5_C_im=dC_im, s5_D=dD, s5_glu_b=dglu_b, hgrn_gamma=d_gamma, hgrn_norm=dhg_norm,
                   ffn_conv_b=jnp.concatenate([dcb0, dcb1], axis=0))
    conv_w_g = jnp.stack([dcw0, dcw1], axis=0)
    sizes = [math.prod(W[n].shape) for n in SMALL]
    n_conv = conv_w_g.size
    total = sum(sizes) + n_conv + 1
    rows = -(-total // PACK_COLS)
    rows = -(-rows // 8) * 8
    pad = rows * PACK_COLS - total

    def pack(vals, conv_part, last):
        flat = [v.reshape(-1).astype(F32) for v in vals] + [conv_part.reshape(-1), last.reshape(-1),
                                                            jnp.zeros((pad,), F32)]
        return jnp.concatenate(flat).reshape(rows, PACK_COLS)

    def conv_full(shard):
        col_owner = lax.broadcasted_iota(jnp.int32, (2, 3, N_DEV * n_cw), 2) // n_cw
        return jnp.where(col_owner == me, jnp.tile(shard, (1, 1, N_DEV)), 0.0)

    zero1 = jnp.zeros((1,), F32)
    g_pack = pack([small_g[n] for n in SMALL], conv_w_g, loss)
    w_pack = pack([W[n] for n in SMALL], conv_full(W["ffn_conv_w"]), zero1)
    m_pack = pack([M[n] for n in SMALL], conv_full(M["ffn_conv_w"]), zero1)
    v_pack = pack([V[n] for n in SMALL], conv_full(V["ffn_conv_w"]), zero1 + 1.0)
    (small_handle,), small_sent = copies_start([g_pack], False, name="small_xstart")

    def finish(name, n_layers):
        w3, m3, v3 = T[name] if name in TRANSPOSED else (W[name], M[name], V[name])
        res = None
        for layer in reversed(range(n_layers)):
            key = name if n_layers == 1 else f"{name}{layer}"
            own, recv = copies_wait(exchanges[key], True, small_sent, name=key + "_xwait")
            _, R, Cn = recv.shape
            res = reduce_adamw(recv, own, True, me, w3.reshape(n_layers * R, Cn), m3.reshape(n_layers * R, Cn),
                               v3.reshape(n_layers * R, Cn), layer=layer, n_layers=n_layers, into=res,
                               name=key + "_adamw")
        res = [r.reshape(w3.shape) for r in res]
        return tuple(jnp.swapaxes(r, -1, -2) for r in res) if name in TRANSPOSED else tuple(res)

    for name in ("ffn_w_out", "ffn_w_in"):
        out[name] = finish(name, 2)
    for name in ("att_w_o", "att_w_qkv", "mix_w_out", "s5_glu_w", "mix_w_in"):
        out[name] = finish(name, 1)

    small_own, small_recv = copies_wait(small_handle, False, out["s5_glu_w"][0], name="small_xwait")
    res = reduce_adamw(small_recv, small_own, False, me, w_pack, m_pack, v_pack, name="small_adamw")
    flat = [r.reshape(-1) for r in res]
    off = 0
    for n, sz in zip(SMALL, sizes):
        out[n] = tuple(f[off:off + sz].reshape(W[n].shape) for f in flat)
        off += sz
    conv_res = [f[off:off + n_conv].reshape(2, 3, N_DEV * n_cw) for f in flat]
    out["ffn_conv_w"] = tuple(lax.dynamic_slice(c, (0, 0, me * n_cw), (2, 3, n_cw)) for c in conv_res)
    off += n_conv
    loss_total = flat[0][off]

    result = [loss_total, grad_x[None]]
    for k in range(4):
        result += [out[n][k] for n in ORDER]
    return tuple(result)
```

```python
import functools
import math

import jax
import jax.numpy as jnp
from jax import lax
from jax.experimental import pallas as pl
from jax.experimental.pallas import tpu as pltpu

F32 = jnp.float32
BF16 = jnp.bfloat16
MESH_ID = pl.DeviceIdType.MESH
N_DEV = 8
VMEM_LIMIT_BYTES = 56 * 1024 * 1024

NORM_EPS = 1e-6
S5_WIDTH, S5_GROUP, S5_GROUPS, S5_STATE = 512, 16, 32, 64
HG_HEADS, HG_DIM, HG_CHUNK = 4, 128, 64
HG_STEP_CHUNKS = 2
ATT_E, ATT_HPG, ATT_BLOCK = 64, 8, 128
ATT_DILATIONS = (1, 4, 16)
ROT_DIM, ROPE_THETA = 16, 500000.0
D_FF = 2816
ADAM_LR, ADAM_B1, ADAM_B2, ADAM_EPS, ADAM_WD, ADAM_STEP = 0.001, 0.9, 0.999, 1e-08, 0.01, 10
NEG_BIG = -1e30


def _params(**kw):
    return pltpu.CompilerParams(vmem_limit_bytes=VMEM_LIMIT_BYTES, **kw)


def _pick(n, cands):
    for c in cands:
        if n % c == 0:
            return c
    return n


def _dot(a, b):
    return jnp.dot(a.astype(BF16), b.astype(BF16), preferred_element_type=F32)


def _dot_nt(a, b):
    return lax.dot_general(a.astype(BF16), b.astype(BF16), (((1,), (1,)), ((), ())), preferred_element_type=F32)


def _dot_tn(a, b):
    return lax.dot_general(a.astype(BF16), b.astype(BF16), (((0,), (0,)), ((), ())), preferred_element_type=F32)


def _split2(x):
    hi = x.astype(BF16)
    return hi, (x - hi.astype(F32)).astype(BF16)


def _dot_x3(a, b, contract=((1,), (0,))):
    dn = (contract, ((), ()))
    a1, a2 = _split2(a)
    b1, b2 = _split2(b)
    return (lax.dot_general(a1, b1, dn, preferred_element_type=F32) + lax.dot_general(a1, b2, dn, preferred_element_type=F32)
            + lax.dot_general(a2, b1, dn, preferred_element_type=F32))


def _sigmoid(x):
    return 1.0 / (1.0 + jnp.exp(-x))


V7X_HBM_BYTES_PER_S = 3.2e12
V7X_MXU_FLOPS_PER_S = 0.7e15
GRID_STEP_S = 0.35e-6
MM_VMEM_BUDGET = 40 * 1024 * 1024


def _divisors(n, cands):
    return [c for c in cands if c <= n and n % c == 0] or [n]


def _mm_tiles(m, n, k, sa, sb, so, sr):
    best = None
    for tm in _divisors(m, (2816, 2048, 1408, 1024, 512, 256, 128)):
        for tn in _divisors(n, (2816, 2048, 1408, 1024, 512, 256, 128)):
            for tk in _divisors(k, (k, 2816, 2560, 2304, 2048, 1536, 1408, 1280, 1024, 512, 256, 128)):
                nk = k // tk
                vmem = 2 * (tm * tk * sa + tk * tn * sb + tm * tn * (so + sr)) + (tm * tn * 4 if nk > 1 else 0)
                vmem += tm * tk * 2 * (sa > 2) + tk * tn * 2 * (sb > 2) + tm * tn * 4
                if vmem > MM_VMEM_BUDGET:
                    continue
                ni, nj = m // tm, n // tn
                for i_outer in (True, False):
                    if i_outer:
                        a_reads = 1 if nk == 1 else nj
                        b_reads = 1 if (nk == 1 and nj == 1) else ni
                    else:
                        b_reads = 1 if nk == 1 else ni
                        a_reads = 1 if (nk == 1 and ni == 1) else nj
                    traffic = a_reads * m * k * sa + b_reads * k * n * sb + m * n * (so + sr)
                    t = max(traffic / V7X_HBM_BYTES_PER_S, 2.0 * m * n * k / V7X_MXU_FLOPS_PER_S)
                    t += ni * nj * nk * GRID_STEP_S
                    t += (tm * tk * sa + tk * tn * sb + tm * tn * so) / V7X_HBM_BYTES_PER_S
                    if best is None or t < best[0]:
                        best = (t, tm, tn, tk, i_outer)
    assert best is not None, (m, n, k)
    return best[1:]


def mm(a, b, *, ta=False, tb=False, res=None, out_dtype=F32, dep=None, name):
    m, k = (a.shape[1], a.shape[0]) if ta else a.shape
    n = b.shape[0] if tb else b.shape[1]
    assert (b.shape[1] if tb else b.shape[0]) == k
    has_res = res is not None
    tm, tn, tk, i_outer = _mm_tiles(m, n, k, a.dtype.itemsize, b.dtype.itemsize, jnp.dtype(out_dtype).itemsize,
                                    res.dtype.itemsize if has_res else 0)
    nk = k // tk
    deps = [] if dep is None else [dep]
    dn = (((0 if ta else 1,), (1 if tb else 0,)), ((), ()))

    def body_single(*refs):
        a_ref, b_ref = refs[:2]
        o_ref = refs[-1]
        out = lax.dot_general(a_ref[...].astype(BF16), b_ref[...].astype(BF16), dn, preferred_element_type=F32)
        if has_res:
            out = out + refs[2][...].astype(F32)
        o_ref[...] = out.astype(o_ref.dtype)

    def body(*refs):
        a_ref, b_ref = refs[:2]
        r_ref = refs[2] if has_res else None
        o_ref, acc_ref = refs[-2:]
        kk = pl.program_id(2)
        part = lax.dot_general(a_ref[...].astype(BF16), b_ref[...].astype(BF16), dn, preferred_element_type=F32)

        @pl.when(kk == 0)
        def _():
            acc_ref[...] = part

        @pl.when(kk > 0)
        def _():
            acc_ref[...] += part

        @pl.when(kk == nk - 1)
        def _():
            out = acc_ref[...]
            if has_res:
                out = out + r_ref[...].astype(F32)
            o_ref[...] = out.astype(o_ref.dtype)

    def ij(f):
        return (lambda g0, g1, q: f(g0, g1, q)) if i_outer else (lambda g0, g1, q: f(g1, g0, q))

    a_spec = pl.BlockSpec((tk, tm), ij(lambda i, j, q: (q, i))) if ta else pl.BlockSpec((tm, tk), ij(lambda i, j, q: (i, q)))
    b_spec = pl.BlockSpec((tn, tk), ij(lambda i, j, q: (j, q))) if tb else pl.BlockSpec((tk, tn), ij(lambda i, j, q: (q, j)))
    o_spec = pl.BlockSpec((tm, tn), ij(lambda i, j, q: (i, j)))
    in_specs = [a_spec, b_spec] + ([o_spec] if has_res else []) + [pl.BlockSpec((8, 128), lambda g0, g1, q: (0, 0))] * len(deps)
    args = (a, b) + ((res,) if has_res else ()) + tuple(deps)
    grid = (m // tm, n // tn, nk) if i_outer else (n // tn, m // tm, nk)
    return pl.pallas_call(
        body_single if nk == 1 else body, grid=grid, in_specs=in_specs, out_specs=o_spec,
        out_shape=jax.ShapeDtypeStruct((m, n), out_dtype),
        scratch_shapes=[] if nk == 1 else [pltpu.VMEM((tm, tn), F32)],
        compiler_params=_params(dimension_semantics=("parallel", "parallel", "arbitrary")), name=name,
    )(*args)


def mm_drms(dy_in, w, x, g, dres, *, dep=None, name):
    m, k = dy_in.shape
    D = w.shape[1]
    tm = _pick(m, (512, 256, 128))
    tk = max(_divisors(k, (1536, 1408, 1280, 1024, 512, 256, 128)))
    nk = k // tk
    deps = [] if dep is None else [dep]

    def body(a_ref, b_ref, x_ref, g_ref, dres_ref, *rest):
        dx_ref, dg_ref, acc_ref = rest[-3:]
        i, q = pl.program_id(0), pl.program_id(1)
        part = jnp.dot(a_ref[...], b_ref[...], preferred_element_type=F32)

        @pl.when(q == 0)
        def _():
            acc_ref[...] = part

        @pl.when(q > 0)
        def _():
            acc_ref[...] += part

        @pl.when((i == 0) & (q == 0))
        def _():
            dg_ref[...] = jnp.zeros_like(dg_ref)

        @pl.when(q == nk - 1)
        def _():
            dyv = acc_ref[...]
            xv = x_ref[...]
            r = lax.rsqrt(jnp.mean(xv * xv, axis=-1, keepdims=True) + NORM_EPS)
            xh = xv * r
            dg_ref[...] += jnp.sum(dyv * xh, axis=0, keepdims=True)
            dxh = dyv * g_ref[...]
            dx_ref[...] = dres_ref[...] + r * (dxh - xh * jnp.mean(dxh * xh, axis=-1, keepdims=True))

    row = pl.BlockSpec((tm, D), lambda i, q: (i, 0))
    vec = pl.BlockSpec((1, D), lambda i, q: (0, 0))
    in_specs = [pl.BlockSpec((tm, tk), lambda i, q: (i, q)), pl.BlockSpec((tk, D), lambda i, q: (q, 0)), row, vec, row]
    in_specs += [pl.BlockSpec((8, 128), lambda i, q: (0, 0))] * len(deps)
    return pl.pallas_call(
        body, grid=(m // tm, nk), in_specs=in_specs, out_specs=[row, vec],
        out_shape=[jax.ShapeDtypeStruct((m, D), F32), jax.ShapeDtypeStruct((1, D), F32)],
        scratch_shapes=[pltpu.VMEM((tm, D), F32)],
        compiler_params=_params(dimension_semantics=("arbitrary", "arbitrary")), name=name,
    )(dy_in, w, x, g.reshape(1, D), dres, *deps)


def rms_fwd(x, g, *, dep=None, name):
    L, D = x.shape
    tr = _pick(L, (256, 128))

    def body(x_ref, g_ref, *rest):
        o_ref = rest[-1]
        xv = x_ref[...]
        r = lax.rsqrt(jnp.mean(xv * xv, axis=-1, keepdims=True) + NORM_EPS)
        o_ref[...] = (xv * r * g_ref[...]).astype(o_ref.dtype)

    row = pl.BlockSpec((tr, D), lambda i: (i, 0))
    vec = pl.BlockSpec((1, D), lambda i: (0, 0))
    deps = [] if dep is None else [dep]
    return pl.pallas_call(body, grid=(L // tr,), in_specs=[row, vec] + [pl.BlockSpec((8, 128), lambda i: (0, 0))] * len(deps),
                          out_specs=row, out_shape=jax.ShapeDtypeStruct((L, D), BF16), name=name)(
        x, g.reshape(1, D), *deps)


def final_loss(h, g, target, *, name):
    L, D = h.shape
    tr = _pick(L, (256, 128))

    def body(x_ref, g_ref, t_ref, loss_ref, dx_ref, dg_ref):
        xv = x_ref[...]
        gv = g_ref[...]
        r = lax.rsqrt(jnp.mean(xv * xv, axis=-1, keepdims=True) + NORM_EPS)
        xh = xv * r
        err = xh * gv - t_ref[...]

        @pl.when(pl.program_id(0) == 0)
        def _():
            dg_ref[...] = jnp.zeros_like(dg_ref)
            loss_ref[...] = jnp.zeros_like(loss_ref)

        loss_ref[...] += 0.5 * jnp.sum(jnp.mean(err * err, axis=-1, keepdims=True), axis=0, keepdims=True)
        dyv = err * (1.0 / D)
        dg_ref[...] += jnp.sum(dyv * xh, axis=0, keepdims=True)
        dxh = dyv * gv
        dx_ref[...] = r * (dxh - xh * jnp.mean(dxh * xh, axis=-1, keepdims=True))

    row = pl.BlockSpec((tr, D), lambda i: (i, 0))
    vec = pl.BlockSpec((1, D), lambda i: (0, 0))
    one = pl.BlockSpec((1, 1), lambda i: (0, 0))
    return pl.pallas_call(body, grid=(L // tr,), in_specs=[row, vec, row], out_specs=[one, row, vec],
                          out_shape=[jax.ShapeDtypeStruct((1, 1), F32), jax.ShapeDtypeStruct((L, D), F32),
                                     jax.ShapeDtypeStruct((1, D), F32)],
                          compiler_params=_params(dimension_semantics=("arbitrary",)), name=name)(
        h, g.reshape(1, D), target)


def _cmul(ar, ai, br, bi):
    return ar * br - ai * bi, ar * bi + ai * br


def _powers(ar, ai):
    rows = [(ar, ai)]
    for _ in range(7):
        rows.append(_cmul(rows[-1][0], rows[-1][1], ar, ai))
    table = (jnp.concatenate([r[0] for r in rows], axis=0), jnp.concatenate([r[1] for r in rows], axis=0))
    return (rows[0], rows[1], rows[3]), table


def _block_scan(br, bi, steps, shift):
    yr, yi = br, bi
    for s, (pr, pi) in zip((1, 2, 4), steps):
        sr, si = shift(yr, s), shift(yi, s)
        yr, yi = yr + pr * sr - pi * si, yi + pr * si + pi * sr
    return yr, yi


def s5_core_fwd(proj, a_re, a_im, wb_re, wb_im, wc_re, wc_im, *, name):
    L = proj.shape[0]
    parts, cu, W = wb_re.shape

    def body(u_ref, ar_ref, ai_ref, wbr_ref, wbi_ref, wcr_ref, wci_ref, y_ref, xr_ref, xi_ref, br_ref, bi_ref):
        u = u_ref[...]
        br_ref[...] = _dot(u, wbr_ref[...])
        bi_ref[...] = _dot(u, wbi_ref[...])
        steps, (tr, ti) = _powers(ar_ref[...], ai_ref[...])
        row = lax.broadcasted_iota(jnp.int32, (8, W), 0)

        def shift(y, s):
            return jnp.where(row >= s, pltpu.roll(y, s, 0), 0.0)

        def step(t8, carry):
            cr, ci = carry
            base = pl.multiple_of(t8 * 8, 8)
            yr, yi = _block_scan(br_ref[pl.ds(base, 8), :], bi_ref[pl.ds(base, 8), :], steps, shift)
            xr = yr + tr * cr - ti * ci
            xi = yi + tr * ci + ti * cr
            xr_ref[pl.ds(base, 8), :] = xr
            xi_ref[pl.ds(base, 8), :] = xi
            return jnp.broadcast_to(xr[7:8, :], (8, W)), jnp.broadcast_to(xi[7:8, :], (8, W))

        zero = jnp.zeros((8, W), F32)
        lax.fori_loop(0, L // 8, step, (zero, zero), unroll=2)
        y_ref[...] = _dot(xr_ref[...], wcr_ref[...]) + _dot(xi_ref[...], wci_ref[...])

    ucol = pl.BlockSpec((L, cu), lambda t: (0, t))
    vec = pl.BlockSpec((1, W), lambda t: (0, t))
    col = pl.BlockSpec((L, W), lambda t: (0, t))
    wb = pl.BlockSpec((None, cu, W), lambda t: (t, 0, 0))
    wc = pl.BlockSpec((None, W, cu), lambda t: (t, 0, 0))
    return pl.pallas_call(body, grid=(parts,), in_specs=[ucol, vec, vec, wb, wb, wc, wc], out_specs=[ucol, col, col],
                          out_shape=[jax.ShapeDtypeStruct((L, parts * cu), F32)]
                          + [jax.ShapeDtypeStruct((L, parts * W), F32)] * 2,
                          scratch_shapes=[pltpu.VMEM((L, W), F32)] * 2,
                          compiler_params=_params(dimension_semantics=("parallel",)), name=name)(
        proj, a_re, a_im, wb_re, wb_im, wc_re, wc_im)


def s5_core_bwd(dy, du_d, proj, xs_re, xs_im, a_re, a_im, wb_re, wb_im, wc_re, wc_im, *, name):
    L = proj.shape[0]
    parts, cu, W = wb_re.shape

    def body(dy_ref, dud_ref, u_ref, xr_ref, xi_ref, ar_ref, ai_ref, wbr_ref, wbi_ref, wcr_ref, wci_ref,
             du_ref, dwbr_ref, dwbi_ref, dwcr_ref, dwci_ref, dar_ref, dai_ref, lr_ref, li_ref):
        dy = dy_ref[...]
        lr_ref[...] = _dot_nt(dy, wcr_ref[...])
        li_ref[...] = _dot_nt(dy, wci_ref[...])
        dwcr_ref[...] = _dot_tn(xr_ref[...], dy)
        dwci_ref[...] = _dot_tn(xi_ref[...], dy)
        ar, ai = ar_ref[...], -ai_ref[...]
        steps, (tr, ti) = _powers(ar, ai)
        tr = jnp.concatenate([tr[j:j + 1, :] for j in range(7, -1, -1)], axis=0)
        ti = jnp.concatenate([ti[j:j + 1, :] for j in range(7, -1, -1)], axis=0)
        row8 = lax.broadcasted_iota(jnp.int32, (8, W), 0)
        nblk = L // 8

        def shift(y, s):
            return jnp.where(row8 < 8 - s, pltpu.roll(y, 8 - s, 0), 0.0)

        def step(s, carry):
            cr, ci = carry
            base = pl.multiple_of((nblk - 1 - s) * 8, 8)
            yr, yi = _block_scan(lr_ref[pl.ds(base, 8), :], li_ref[pl.ds(base, 8), :], steps, shift)
            lr = yr + tr * cr - ti * ci
            li = yi + tr * ci + ti * cr
            lr_ref[pl.ds(base, 8), :] = lr
            li_ref[pl.ds(base, 8), :] = li
            return jnp.broadcast_to(lr[0:1, :], (8, W)), jnp.broadcast_to(li[0:1, :], (8, W))

        zero = jnp.zeros((8, W), F32)
        lax.fori_loop(0, nblk, step, (zero, zero), unroll=2)
        row = lax.broadcasted_iota(jnp.int32, (L, W), 0)
        xpr = jnp.where(row >= 1, pltpu.roll(xr_ref[...], 1, 0), 0.0)
        xpi = jnp.where(row >= 1, pltpu.roll(xi_ref[...], 1, 0), 0.0)
        lr, li = lr_ref[...], li_ref[...]
        dar_ref[...] = jnp.sum(lr * xpr + li * xpi, axis=0, keepdims=True)
        dai_ref[...] = jnp.sum(li * xpr - lr * xpi, axis=0, keepdims=True)
        u = u_ref[...]
        dwbr_ref[...] = _dot_tn(u, lr)
        dwbi_ref[...] = _dot_tn(u, li)
        du_ref[...] = (dud_ref[...] + _dot_nt(lr, wbr_ref[...]) + _dot_nt(li, wbi_ref[...])).astype(du_ref.dtype)

    ucol = pl.BlockSpec((L, cu), lambda t: (0, t))
    vec = pl.BlockSpec((1, W), lambda t: (0, t))
    col = pl.BlockSpec((L, W), lambda t: (0, t))
    wb = pl.BlockSpec((None, cu, W), lambda t: (t, 0, 0))
    wc = pl.BlockSpec((None, W, cu), lambda t: (t, 0, 0))
    return pl.pallas_call(
        body, grid=(parts,), in_specs=[ucol, ucol, ucol, col, col, vec, vec, wb, wb, wc, wc],
        out_specs=[ucol, wb, wb, wc, wc, vec, vec],
        out_shape=[jax.ShapeDtypeStruct((L, parts * cu), BF16)] + [jax.ShapeDtypeStruct((parts, cu, W), F32)] * 2
        + [jax.ShapeDtypeStruct((parts, W, cu), F32)] * 2 + [jax.ShapeDtypeStruct((1, parts * W), F32)] * 2,
        scratch_shapes=[pltpu.VMEM((L, W), F32)] * 2,
        compiler_params=_params(dimension_semantics=("parallel",)), name=name,
    )(dy, du_d, proj, xs_re, xs_im, a_re, a_im, wb_re, wb_im, wc_re, wc_im)


def _gelu(y):
    c = math.sqrt(2.0 / math.pi)
    t = jnp.tanh(c * (y + 0.044715 * y * y * y))
    return 0.5 * y * (1.0 + t), t


def s5_out_fwd(y0, proj, dvec, glu_w, glu_b, *, name):
    L, C = y0.shape
    tr = _pick(L, (256, 128))

    def body(y_ref, u_ref, d_ref, w_ref, b_ref, o_ref):
        z, _ = _gelu(y_ref[...] + d_ref[...] * u_ref[...])
        zg = _dot(z, w_ref[...]) + b_ref[...]
        o_ref[...] = (z * _sigmoid(zg)).astype(o_ref.dtype)

    row = pl.BlockSpec((tr, C), lambda i: (i, 0))
    vec = pl.BlockSpec((1, C), lambda i: (0, 0))
    wsp = pl.BlockSpec((C, C), lambda i: (0, 0))
    return pl.pallas_call(body, grid=(L // tr,), in_specs=[row, row, vec, wsp, vec], out_specs=row,
                          out_shape=jax.ShapeDtypeStruct((L, C), BF16), name=name)(
        y0, proj, dvec, glu_w, glu_b)


def s5_out_bwd(y0, proj, dvec, glu_w, glu_b, dcat, *, name):
    L, C = y0.shape
    tr = _pick(L, (256, 128))

    def body(y_ref, u_ref, d_ref, w_ref, b_ref, do_ref, dy_ref, dud_ref, z_ref, dzg_ref, db_ref, dd_ref):
        u = u_ref[...]
        y = y_ref[...] + d_ref[...] * u
        z, t = _gelu(y)
        zg = _dot(z, w_ref[...]) + b_ref[...]
        s = _sigmoid(zg)
        do = do_ref[...]
        dzg = do * z * s * (1.0 - s)
        dz = do * s + _dot_nt(dzg, w_ref[...])
        c = math.sqrt(2.0 / math.pi)
        dgelu = 0.5 * (1.0 + t) + 0.5 * y * (1.0 - t * t) * c * (1.0 + 3.0 * 0.044715 * y * y)
        dy = dz * dgelu

        @pl.when(pl.program_id(0) == 0)
        def _():
            db_ref[...] = jnp.zeros_like(db_ref)
            dd_ref[...] = jnp.zeros_like(dd_ref)

        db_ref[...] += jnp.sum(dzg, axis=0, keepdims=True)
        dd_ref[...] += jnp.sum(dy * u, axis=0, keepdims=True)
        dy_ref[...] = dy
        dud_ref[...] = dy * d_ref[...]
        z_ref[...] = z.astype(BF16)
        dzg_ref[...] = dzg.astype(BF16)

    row = pl.BlockSpec((tr, C), lambda i: (i, 0))
    vec = pl.BlockSpec((1, C), lambda i: (0, 0))
    wsp = pl.BlockSpec((C, C), lambda i: (0, 0))
    return pl.pallas_call(body, grid=(L // tr,), in_specs=[row, row, vec, wsp, vec, row],
                          out_specs=[row, row, row, row, vec, vec],
                          out_shape=[jax.ShapeDtypeStruct((L, C), F32), jax.ShapeDtypeStruct((L, C), F32),
                                     jax.ShapeDtypeStruct((L, C), BF16), jax.ShapeDtypeStruct((L, C), BF16),
                                     jax.ShapeDtypeStruct((1, C), F32), jax.ShapeDtypeStruct((1, C), F32)],
                          compiler_params=_params(dimension_semantics=("arbitrary",)), name=name)(
        y0, proj, dvec, glu_w, glu_b, dcat)


def _dot_tri(tri, x, tri_left=True):
    t = tri.astype(BF16)
    x1 = x.astype(BF16)
    r1 = x - x1.astype(F32)
    x2 = r1.astype(BF16)
    x3 = (r1 - x2.astype(F32)).astype(BF16)
    dot = (lambda p: jnp.dot(t, p, preferred_element_type=F32)) if tri_left else (
        lambda p: jnp.dot(p, t, preferred_element_type=F32))
    return dot(x1) + dot(x2) + dot(x3)


def _hg_gates(xq, xf, lb, tri):
    C = xq.shape[0]
    sq = _sigmoid(xq)
    q = xq * sq
    sg = _sigmoid(xf)
    f = lb + (1.0 - lb) * sg
    kk = 1.0 - f
    b = _dot_tri(tri, jnp.log(f))
    bm = b[C // 2 - 1:C // 2, :]
    bl = b[C - 1:C, :]
    eb = jnp.exp(b)
    eqm, ekm, ekl = jnp.exp(b - bm), jnp.exp(bm - b), jnp.exp(bl - b)
    return dict(sq=sq, q=q, sg=sg, f=f, kk=kk, eb=eb, ebl=jnp.exp(bl), eqm=eqm, ekm=ekm, ekl=ekl,
                qb=q * eb, qt=q * eqm, kt=kk * ekm, kh=kk * ekl)


def _tri(C, lower):
    r = lax.broadcasted_iota(jnp.int32, (C, C), 0)
    c = lax.broadcasted_iota(jnp.int32, (C, C), 1)
    return (r >= c) if lower else (c >= r)


def hgrn_fwd(proj, lb, norm_g, *, name):
    L = proj.shape[0]
    C, H, K = HG_CHUNK, HG_HEADS, HG_DIM
    HK = H * K
    nc = L // C

    def body(q_ref, f_ref, i_ref, g_ref, lb_ref, ng_ref, o_ref, sall_ref, st_ref):
        @pl.when(pl.program_id(0) == 0)
        def _():
            st_ref[...] = jnp.zeros_like(st_ref)

        mask = _tri(C, True)
        sts = [st_ref[h] for h in range(H)]
        for s in range(S):
            rs = slice(s * C, (s + 1) * C)
            gt = _hg_gates(q_ref[rs, :], f_ref[rs, :], lb_ref[...], mask.astype(F32))
            v_all = i_ref[rs, :]
            outs = []
            for h in range(H):
                sl = slice(h * K, (h + 1) * K)
                v, st = v_all[:, sl], sts[h]
                sall_ref[s, h] = st
                att = jnp.where(mask, _dot_nt(gt["qt"][:, sl], gt["kt"][:, sl]), 0.0)
                o = _dot(att, v) + _dot_nt(gt["qb"][:, sl], st)
                sts[h] = st * gt["ebl"][:, sl] + _dot_tn(v, gt["kh"][:, sl])
                outs.append(o * lax.rsqrt(jnp.mean(o * o, axis=-1, keepdims=True) + NORM_EPS))
            xg = g_ref[rs, :]
            o_ref[rs, :] = (jnp.concatenate(outs, axis=1) * ng_ref[...] * (xg * _sigmoid(xg))).astype(o_ref.dtype)
        for h in range(H):
            st_ref[h] = sts[h]

    S = HG_STEP_CHUNKS

    def blk(cb):
        return pl.BlockSpec((S * C, HK), lambda i: (i, cb))

    vec = pl.BlockSpec((1, HK), lambda i: (0, 0))
    return pl.pallas_call(
        body, grid=(nc // S,), in_specs=[blk(1), blk(2), blk(3), blk(4), vec, vec],
        out_specs=[pl.BlockSpec((S * C, HK), lambda i: (i, 0)), pl.BlockSpec((S, H, K, K), lambda i: (i, 0, 0, 0))],
        out_shape=[jax.ShapeDtypeStruct((L, HK), BF16), jax.ShapeDtypeStruct((nc, H, K, K), F32)],
        scratch_shapes=[pltpu.VMEM((H, K, K), F32)],
        compiler_params=_params(dimension_semantics=("arbitrary",)), name=name,
    )(proj, proj, proj, proj, lb, norm_g)


def hgrn_bwd(proj, lb, norm_g, sall, dcat, *, name):
    L = proj.shape[0]
    C, H, K = HG_CHUNK, HG_HEADS, HG_DIM
    HK = H * K
    nc = L // C

    def body(q_ref, f_ref, i_ref, g_ref, lb_ref, ng_ref, sall_ref, do_ref, dx_ref, dlb_ref, dng_ref, dst_ref):
        @pl.when(pl.program_id(0) == 0)
        def _():
            dst_ref[...] = jnp.zeros_like(dst_ref)
            dlb_ref[...] = jnp.zeros_like(dlb_ref)
            dng_ref[...] = jnp.zeros_like(dng_ref)

        mask = _tri(C, True)
        lb_all, ng = lb_ref[...], ng_ref[...]
        dsts = [dst_ref[h] for h in range(H)]
        for s in reversed(range(S)):
            rs = slice(s * C, (s + 1) * C)
            dsts = chunk_bwd(rs, s, dsts, mask, lb_all, ng, q_ref, f_ref, i_ref, g_ref, sall_ref, do_ref,
                             dx_ref, dlb_ref, dng_ref)
        for h in range(H):
            dst_ref[h] = dsts[h]

    def chunk_bwd(rs, s, dsts, mask, lb_all, ng, q_ref, f_ref, i_ref, g_ref, sall_ref, do_ref, dx_ref, dlb_ref, dng_ref):
        xq, xg, v_all = q_ref[rs, :], g_ref[rs, :], i_ref[rs, :]
        gt = _hg_gates(xq, f_ref[rs, :], lb_all, mask.astype(F32))
        sgg = _sigmoid(xg)
        d_ob = do_ref[rs, :]
        d_on = d_ob * (xg * sgg)
        doh = d_on * ng
        ohs, d_qts, d_qbs, d_kts, d_khs, dvs, d_bls, new_dsts = [], [], [], [], [], [], [], []
        for h in range(H):
            sl = slice(h * K, (h + 1) * K)
            v, st, dst = v_all[:, sl], sall_ref[s, h], dsts[h]
            qt, kt, kh, qb = gt["qt"][:, sl], gt["kt"][:, sl], gt["kh"][:, sl], gt["qb"][:, sl]
            att = jnp.where(mask, _dot_nt(qt, kt), 0.0)
            o = _dot(att, v) + _dot_nt(qb, st)
            r = lax.rsqrt(jnp.mean(o * o, axis=-1, keepdims=True) + NORM_EPS)
            oh = o * r
            do = r * (doh[:, sl] - oh * jnp.mean(doh[:, sl] * oh, axis=-1, keepdims=True))
            datt = jnp.where(mask, _dot_nt(do, v), 0.0)
            dvs.append(_dot_tn(att, do) + _dot_nt(kh, dst))
            d_qbs.append(_dot_x3(do, st))
            d_qts.append(_dot_x3(datt, kt))
            d_kts.append(_dot_x3(datt, qt, ((0,), (0,))))
            d_kh = _dot_x3(v, dst)
            d_khs.append(d_kh)
            d_bls.append(jnp.sum(dst * st, axis=0, keepdims=True) * gt["ebl"][:, sl]
                         + jnp.sum(d_kh * kh, axis=0, keepdims=True))
            new_dsts.append(dst * gt["ebl"][:, sl] + _dot_tn(do, qb))
            ohs.append(oh)
        oh, d_qt, d_qb, d_kt, d_kh, dv, d_bl = (jnp.concatenate(p, axis=1) for p in
                                                (ohs, d_qts, d_qbs, d_kts, d_khs, dvs, d_bls))
        dxg = d_ob * (oh * ng) * (sgg * (1.0 + xg * (1.0 - sgg)))
        dng_ref[...] += jnp.sum(d_on * oh, axis=0, keepdims=True)
        dq = d_qt * gt["eqm"] + d_qb * gt["eb"]
        db = d_qt * gt["qt"] + d_qb * gt["qb"] - d_kt * gt["kt"] - d_kh * gt["kh"]
        rowi = lax.broadcasted_iota(jnp.int32, (C, HK), 0)
        db = db + jnp.where(rowi == C - 1, d_bl, 0.0)
        dkk = d_kt * gt["ekm"] + d_kh * gt["ekl"]
        dlg = _dot_tri(_tri(C, False).astype(F32), db)
        df = dlg / gt["f"] - dkk
        sg, sq = gt["sg"], gt["sq"]
        dlb_ref[...] += jnp.sum(df * (1.0 - sg), axis=0, keepdims=True)
        dx_ref[rs, 0:HK] = (dq * (sq * (1.0 + xq * (1.0 - sq)))).astype(dx_ref.dtype)
        dx_ref[rs, HK:2 * HK] = (df * (1.0 - lb_all) * sg * (1.0 - sg)).astype(dx_ref.dtype)
        dx_ref[rs, 2 * HK:3 * HK] = dv.astype(dx_ref.dtype)
        dx_ref[rs, 3 * HK:4 * HK] = dxg.astype(dx_ref.dtype)
        return new_dsts

    S = HG_STEP_CHUNKS
    ns = nc // S

    def blk(cb):
        return pl.BlockSpec((S * C, HK), lambda i: (ns - 1 - i, cb))

    vec = pl.BlockSpec((1, HK), lambda i: (0, 0))
    return pl.pallas_call(
        body, grid=(ns,),
        in_specs=[blk(1), blk(2), blk(3), blk(4), vec, vec,
                  pl.BlockSpec((S, H, K, K), lambda i: (ns - 1 - i, 0, 0, 0)), blk(1)],
        out_specs=[pl.BlockSpec((S * C, 4 * HK), lambda i: (ns - 1 - i, 0)), vec, vec],
        out_shape=[jax.ShapeDtypeStruct((L, 4 * HK), BF16), jax.ShapeDtypeStruct((1, HK), F32),
                   jax.ShapeDtypeStruct((1, HK), F32)],
        scratch_shapes=[pltpu.VMEM((H, K, K), F32)],
        compiler_params=_params(dimension_semantics=("arbitrary",)), name=name,
    )(proj, proj, proj, proj, lb, norm_g, sall, dcat)


def _shift_down(x, k, row):
    return jnp.where(row >= k, pltpu.roll(x, k, 0), 0.0)


def _shift_up(x, k, row):
    n = x.shape[0]
    return jnp.where(row < n - k, pltpu.roll(x, n - k, 0), 0.0)


def convgate_fwd(hu, conv_w, conv_b, *, name):
    L, C2 = hu.shape
    C = C2 // 2
    tc = _pick(C, (256, 128))
    nb = C // tc

    def body(a_ref, b_ref, wa_ref, wb_ref, ba_ref, bb_ref, o_ref):
        row = lax.broadcasted_iota(jnp.int32, (L, tc), 0)

        def conv(x, w, bias):
            return w[2:3, :] * x + w[1:2, :] * _shift_down(x, 1, row) + w[0:1, :] * _shift_down(x, 2, row) + bias

        ca = conv(a_ref[...], wa_ref[...], ba_ref[...])
        cb = conv(b_ref[...], wb_ref[...], bb_ref[...])
        o_ref[...] = (ca * _sigmoid(ca) * cb).astype(o_ref.dtype)

    def col(off, rows):
        return pl.BlockSpec((rows, tc), lambda j: (0, j + off))

    return pl.pallas_call(
        body, grid=(nb,), in_specs=[col(0, L), col(nb, L), col(0, 3), col(nb, 3), col(0, 1), col(nb, 1)],
        out_specs=col(0, L), out_shape=jax.ShapeDtypeStruct((L, C), BF16),
        compiler_params=_params(dimension_semantics=("parallel",)), name=name,
    )(hu, hu, conv_w, conv_w, conv_b, conv_b)


def convgate_bwd(hu, conv_w, conv_b, dact, *, name):
    L, C2 = hu.shape
    C = C2 // 2
    tc = _pick(C, (256, 128))
    nb = C // tc

    def body(a_ref, b_ref, wa_ref, wb_ref, ba_ref, bb_ref, d_ref, dxa_ref, dxb_ref, dwa_ref, dwb_ref, dba_ref, dbb_ref):
        row = lax.broadcasted_iota(jnp.int32, (L, tc), 0)

        def conv(x, w, bias):
            x1 = _shift_down(x, 1, row)
            x2 = _shift_down(x, 2, row)
            return w[2:3, :] * x + w[1:2, :] * x1 + w[0:1, :] * x2 + bias, x1, x2

        xa, xb = a_ref[...], b_ref[...]
        wa, wb = wa_ref[...], wb_ref[...]
        ca, xa1, xa2 = conv(xa, wa, ba_ref[...])
        cb, xb1, xb2 = conv(xb, wb, bb_ref[...])
        d = d_ref[...]
        sa = _sigmoid(ca)
        dca = d * cb * (sa * (1.0 + ca * (1.0 - sa)))
        dcb = d * (ca * sa)

        def back(dc, w, x, x1, x2, dx_ref, dw_ref, db_ref):
            dx = w[2:3, :] * dc + w[1:2, :] * _shift_up(dc, 1, row) + w[0:1, :] * _shift_up(dc, 2, row)
            dx_ref[...] = dx.astype(dx_ref.dtype)
            dw_ref[...] = jnp.concatenate([jnp.sum(dc * x2, axis=0, keepdims=True),
                                           jnp.sum(dc * x1, axis=0, keepdims=True),
                                           jnp.sum(dc * x, axis=0, keepdims=True)], axis=0)
            db_ref[...] = jnp.sum(dc, axis=0, keepdims=True)

        back(dca, wa, xa, xa1, xa2, dxa_ref, dwa_ref, dba_ref)
        back(dcb, wb, xb, xb1, xb2, dxb_ref, dwb_ref, dbb_ref)

    def col(off, rows):
        return pl.BlockSpec((rows, tc), lambda j: (0, j + off))

    outs = pl.pallas_call(
        body, grid=(nb,),
        in_specs=[col(0, L), col(nb, L), col(0, 3), col(nb, 3), col(0, 1), col(nb, 1), col(0, L)],
        out_specs=[col(0, L), col(0, L), col(0, 3), col(0, 3), col(0, 1), col(0, 1)],
        out_shape=[jax.ShapeDtypeStruct((L, C), BF16)] * 2 + [jax.ShapeDtypeStruct((3, C), F32)] * 2
        + [jax.ShapeDtypeStruct((1, C), F32)] * 2,
        compiler_params=_params(dimension_semantics=("parallel",)), name=name,
    )(hu, hu, conv_w, conv_w, conv_b, conv_b, dact)
    dxa, dxb, dwa, dwb, dba, dbb = outs
    return (jnp.concatenate([dxa, dxb], axis=1), jnp.concatenate([dwa, dwb], axis=1),
            jnp.concatenate([dba, dbb], axis=1))


def rope_tables(positions):
    half = ROT_DIM // 2
    inv_freq = ROPE_THETA ** (-jnp.arange(half, dtype=F32) * 2.0 / ROT_DIM)
    ang = positions.astype(F32)[:, None] * inv_freq
    cos, sin = jnp.cos(ang), jnp.sin(ang)
    L = positions.shape[0]
    one = jnp.ones((L, ATT_E - ROT_DIM), F32)
    zero = jnp.zeros((L, ATT_E - ROT_DIM), F32)
    zh = jnp.zeros((L, half), F32)
    tc = jnp.concatenate([cos, cos, one], axis=1)
    ts1 = jnp.concatenate([zh, sin, zero], axis=1)
    ts2 = jnp.concatenate([-sin, zh, zero], axis=1)
    return tuple(jnp.concatenate([t, t], axis=1) for t in (tc, ts1, ts2))


def qkv_rope(hn, w_t, tabs, *, name):
    L, D = hn.shape
    N = w_t.shape[0]
    W = 512
    tm = _pick(L, (1024, 512, 256, 128))
    nq = N // (3 * W)
    scale = ATT_E ** -0.5

    def body(a_ref, b_ref, c_ref, s1_ref, s2_ref, o_ref):
        j = pl.program_id(1)
        x = _dot_nt(a_ref[...], b_ref[...])
        c = jnp.concatenate([c_ref[...]] * 4, axis=1)
        s1 = jnp.concatenate([s1_ref[...]] * 4, axis=1)
        s2 = jnp.concatenate([s2_ref[...]] * 4, axis=1)
        rot = x * c + pltpu.roll(x, 8, 1) * s1 + pltpu.roll(x, W - 8, 1) * s2
        mult = jnp.where(j < nq, scale, 1.0)
        o_ref[...] = jnp.where(j < 2 * nq, rot * mult, x)

    tab = pl.BlockSpec((tm, 128), lambda i, j: (i, 0))
    return pl.pallas_call(body, grid=(L // tm, N // W),
                          in_specs=[pl.BlockSpec((tm, D), lambda i, j: (i, 0)), pl.BlockSpec((W, D), lambda i, j: (j, 0)),
                                    tab, tab, tab],
                          out_specs=pl.BlockSpec((tm, W), lambda i, j: (i, j)),
                          out_shape=jax.ShapeDtypeStruct((L, N), F32),
                          compiler_params=_params(dimension_semantics=("parallel", "parallel")), name=name)(
        hn, w_t, *tabs)


def rope_bwd(slabs, tabs, *, name):
    L, W = slabs[0].shape
    tr = _pick(L, (256, 128))
    nq = len(slabs) // 3
    scale = ATT_E ** -0.5

    def body(*refs):
        d_refs, (c_ref, s1_ref, s2_ref, o_ref) = refs[:3 * nq], refs[3 * nq:]
        c = jnp.concatenate([c_ref[...]] * 4, axis=1)
        s1 = jnp.concatenate([s1_ref[...]] * 4, axis=1)
        s2 = jnp.concatenate([s2_ref[...]] * 4, axis=1)
        for j, d_ref in enumerate(d_refs):
            dy = d_ref[...]
            if j < 2 * nq:
                dy = dy * c + pltpu.roll(dy * s1, W - 8, 1) + pltpu.roll(dy * s2, 8, 1)
            if j < nq:
                dy = dy * scale
            o_ref[:, j * W:(j + 1) * W] = dy.astype(o_ref.dtype)

    slab = pl.BlockSpec((tr, W), lambda i: (i, 0))
    tab = pl.BlockSpec((tr, 128), lambda i: (i, 0))
    return pl.pallas_call(body, grid=(L // tr,), in_specs=[slab] * (3 * nq) + [tab, tab, tab],
                          out_specs=pl.BlockSpec((tr, 3 * nq * W), lambda i: (i, 0)),
                          out_shape=jax.ShapeDtypeStruct((L, 3 * nq * W), BF16),
                          compiler_params=_params(dimension_semantics=("parallel",)), name=name)(*slabs, *tabs)


def _att_masks(has_prev):
    qi = lax.broadcasted_iota(jnp.int32, (ATT_BLOCK, ATT_BLOCK), 0)
    kj = lax.broadcasted_iota(jnp.int32, (ATT_BLOCK, ATT_BLOCK), 1)
    return qi >= kj, (kj >= qi) & has_prev


ATT_COLS = 128


def _att_rows(j, d, nb):
    B = ATT_BLOCK
    r, n = j // nb, j % nb
    start = r + d * B * n
    has_prev = n > 0
    pstart = jnp.where(has_prev, start - d * B, start)
    if d == 1:
        return pl.ds(pl.multiple_of(start, B), B), pl.ds(pl.multiple_of(pstart, B), B), has_prev
    return pl.ds(start, B, stride=d), pl.ds(pstart, B, stride=d), has_prev


def _qkv_specs(L, g):
    per = ATT_HPG * ATT_E // ATT_COLS
    third = len(ATT_DILATIONS) * per
    return [pl.BlockSpec((L, ATT_COLS), lambda c, base=base: (0, base + c))
            for base in (g * per, third + g * per, 2 * third + g * per)]


def attn_fwd(qkv, g, d, *, name):
    L, W = qkv.shape[0], ATT_HPG * ATT_E
    B, E = ATT_BLOCK, ATT_E
    nblk = L // B
    nb = nblk // d

    def body(q_ref, k_ref, v_ref, o_ref, l_ref):
        def step(j, carry):
            cur, prv, has_prev = _att_rows(j, d, nb)
            mc, mp = _att_masks(has_prev)
            qb, kc, kp, vc, vp = q_ref[cur, :], k_ref[cur, :], k_ref[prv, :], v_ref[cur, :], v_ref[prv, :]
            outs, lses = [], []
            for h in range(ATT_COLS // E):
                sl = slice(h * E, (h + 1) * E)
                sc = jnp.where(mc, _dot_nt(qb[:, sl], kc[:, sl]), NEG_BIG)
                sp = jnp.where(mp, _dot_nt(qb[:, sl], kp[:, sl]), NEG_BIG)
                m = jnp.maximum(jnp.max(sc, axis=-1, keepdims=True), jnp.max(sp, axis=-1, keepdims=True))
                pc = jnp.exp(sc - m)
                pp = jnp.exp(sp - m)
                den = jnp.sum(pc, axis=-1, keepdims=True) + jnp.sum(pp, axis=-1, keepdims=True)
                outs.append((_dot(pc, vc[:, sl]) + _dot(pp, vp[:, sl])) / den)
                lses.append(jnp.broadcast_to(m + jnp.log(den), (B, E)))
            o_ref[cur, :] = jnp.concatenate(outs, axis=1)
            l_ref[cur, :] = jnp.concatenate(lses, axis=1)
            return carry

        lax.fori_loop(0, nblk, step, 0, unroll=4)

    col = pl.BlockSpec((L, ATT_COLS), lambda c: (0, c))
    return pl.pallas_call(body, grid=(W // ATT_COLS,), in_specs=_qkv_specs(L, g), out_specs=[col] * 2,
                          out_shape=[jax.ShapeDtypeStruct((L, W), F32)] * 2,
                          compiler_params=_params(dimension_semantics=("parallel",)), name=name)(qkv, qkv, qkv)


def attn_bwd(qkv, g, lse, do, dl, d, *, name):
    L, W = qkv.shape[0], ATT_HPG * ATT_E
    B, E = ATT_BLOCK, ATT_E
    nblk = L // B
    nb = nblk // d

    def body(q_ref, k_ref, v_ref, l_ref, do_ref, dl_ref, dq_ref, dk_ref, dv_ref):
        dk_ref[...] = jnp.zeros_like(dk_ref)
        dv_ref[...] = jnp.zeros_like(dv_ref)

        def step(j, carry):
            cur, prv, has_prev = _att_rows(j, d, nb)
            mc, mp = _att_masks(has_prev)
            qb, kc, kp, vc, vp = q_ref[cur, :], k_ref[cur, :], k_ref[prv, :], v_ref[cur, :], v_ref[prv, :]
            lb, dob, dlb = l_ref[cur, :], do_ref[cur, :], dl_ref[cur, :]
            dqs, dkc, dkp, dvc, dvp = [], [], [], [], []
            for h in range(ATT_COLS // E):
                sl = slice(h * E, (h + 1) * E)
                qh, doh = qb[:, sl], dob[:, sl]
                lse_h, dl_h = lb[:, h * E:h * E + 1], dlb[:, h * E:h * E + 1]
                pc = jnp.where(mc, jnp.exp(_dot_nt(qh, kc[:, sl]) - lse_h), 0.0)
                pp = jnp.where(mp, jnp.exp(_dot_nt(qh, kp[:, sl]) - lse_h), 0.0)
                dsc = pc * (_dot_nt(doh, vc[:, sl]) - dl_h)
                dsp = pp * (_dot_nt(doh, vp[:, sl]) - dl_h)
                dqs.append(_dot(dsc, kc[:, sl]) + _dot(dsp, kp[:, sl]))
                dkc.append(_dot_tn(dsc, qh))
                dkp.append(_dot_tn(dsp, qh))
                dvc.append(_dot_tn(pc, doh))
                dvp.append(_dot_tn(pp, doh))
            dq_ref[cur, :] = jnp.concatenate(dqs, axis=1)
            dk_ref[cur, :] = dk_ref[cur, :] + jnp.concatenate(dkc, axis=1)
            dv_ref[cur, :] = dv_ref[cur, :] + jnp.concatenate(dvc, axis=1)
            dk_ref[prv, :] = dk_ref[prv, :] + jnp.concatenate(dkp, axis=1)
            dv_ref[prv, :] = dv_ref[prv, :] + jnp.concatenate(dvp, axis=1)
            return carry

        lax.fori_loop(0, nblk, step, 0, unroll=4)

    col = pl.BlockSpec((L, ATT_COLS), lambda c: (0, c))
    return pl.pallas_call(body, grid=(W // ATT_COLS,), in_specs=_qkv_specs(L, g) + [col] * 3, out_specs=[col] * 3,
                          out_shape=[jax.ShapeDtypeStruct((L, W), F32)] * 3,
                          compiler_params=_params(dimension_semantics=("parallel",)), name=name)(
        qkv, qkv, qkv, lse, do, dl)


def _merge_alpha(l_refs):
    ls = [r[...] for r in l_refs]
    m = jnp.maximum(jnp.maximum(ls[0], ls[1]), ls[2])
    es = [jnp.exp(l - m) for l in ls]
    den = es[0] + es[1] + es[2]
    return [e / den for e in es]


def merge_fwd(os_, ls_, *, name):
    L, W = os_[0].shape
    tr = _pick(L, (256, 128))

    def body(o0, o1, o2, l0, l1, l2, out_ref):
        al = _merge_alpha((l0, l1, l2))
        out_ref[...] = (al[0] * o0[...] + al[1] * o1[...] + al[2] * o2[...]).astype(out_ref.dtype)

    row = pl.BlockSpec((tr, W), lambda i: (i, 0))
    return pl.pallas_call(body, grid=(L // tr,), in_specs=[row] * 6, out_specs=row,
                          out_shape=jax.ShapeDtypeStruct((L, W), BF16), name=name)(*os_, *ls_)


def merge_bwd(os_, ls_, do, *, name):
    L, W = do.shape
    tr = _pick(L, (256, 128))

    def body(o0, o1, o2, l0, l1, l2, do_ref, d0, d1, d2, e0, e1, e2):
        al = _merge_alpha((l0, l1, l2))
        dov = do_ref[...]
        r = lax.broadcasted_iota(jnp.int32, (W, W), 0) // ATT_E
        c = lax.broadcasted_iota(jnp.int32, (W, W), 1) // ATT_E
        ones_blk = (r == c).astype(F32)
        t = jnp.zeros_like(dov)
        for a, o in zip(al, (o0, o1, o2)):
            t = t + a * _dot_tri(ones_blk, dov * o[...], tri_left=False)
        for a, d_ref, e_ref in zip(al, (d0, d1, d2), (e0, e1, e2)):
            d_ref[...] = a * dov
            e_ref[...] = a * t

    row = pl.BlockSpec((tr, W), lambda i: (i, 0))
    return pl.pallas_call(body, grid=(L // tr,), in_specs=[row] * 7, out_specs=[row] * 6,
                          out_shape=[jax.ShapeDtypeStruct((L, W), F32)] * 6, name=name)(*os_, *ls_, do)


def _me_and_peers():
    x, y, c = lax.axis_index("x"), lax.axis_index("y"), lax.axis_index("c")
    peers = []
    for k in range(1, N_DEV):
        px = 1 - x if k & 4 else x
        py = 1 - y if k & 2 else y
        pc = 1 - c if k & 1 else c
        peers.append((px, py, pc))
    return (x, y, c), peers


def _index(dev):
    return 4 * dev[0] + 2 * dev[1] + dev[2]


def _hbm(a):
    return pltpu.with_memory_space_constraint(a, pltpu.HBM)


HBM_SPEC = pl.BlockSpec(memory_space=pltpu.HBM)
SEM_SPEC = pl.BlockSpec(memory_space=pltpu.SEMAPHORE)
DATAFLOW = pltpu.SideEffectType.DATAFLOW_SIDE_EFFECTING


def _remote(src_ref, land_ref, slotted, me, peer, src_is_mine, send_sem, recv_sem, k):
    sender, receiver = (me, peer) if src_is_mine else (peer, me)
    src = src_ref.at[_index(receiver)] if slotted else src_ref
    return pltpu.make_async_remote_copy(src_ref=src, dst_ref=land_ref.at[_index(sender)], send_sem=send_sem.at[k],
                                        recv_sem=recv_sem.at[k], device_id=peer, device_id_type=MESH_ID)


def copies_start(arrays, slotted, *, name):
    n = len(arrays)
    lands = [lax.empty(a.shape if slotted else (N_DEV,) + a.shape, a.dtype) for a in arrays]

    def body(*refs):
        x_refs, land_refs = refs[:n], refs[n:2 * n]
        send, recv = refs[2 * n:3 * n], refs[3 * n:4 * n]
        token = refs[-1]
        me, peers = _me_and_peers()
        for w in range(n):
            for k, peer in enumerate(peers):
                _remote(x_refs[w], land_refs[w], slotted, me, peer, True, send[w], recv[w], k).start()
            if not slotted:
                pltpu.make_async_copy(x_refs[w], land_refs[w].at[_index(me)], recv[w].at[N_DEV - 1]).start()
        token[...] = jnp.zeros_like(token)

    sem = pltpu.SemaphoreType.DMA((N_DEV,))
    out_shape = ([sem] * (2 * n) + [pltpu.HBM(a.shape, a.dtype) for a in arrays]
                 + [pltpu.HBM(l.shape, l.dtype) for l in lands] + [jax.ShapeDtypeStruct((8, 128), F32)])
    outs = pl.pallas_call(
        body, name=name, out_shape=out_shape, in_specs=[HBM_SPEC] * (2 * n),
        out_specs=[SEM_SPEC] * (2 * n) + [HBM_SPEC] * (2 * n) + [pl.BlockSpec(memory_space=pltpu.VMEM)],
        input_output_aliases={i: 2 * n + i for i in range(2 * n)},
        compiler_params=pltpu.CompilerParams(has_side_effects=DATAFLOW),
    )(*[_hbm(a) for a in arrays], *[_hbm(l) for l in lands])
    handles = [(outs[w], outs[n + w], outs[2 * n + w], outs[3 * n + w]) for w in range(n)]
    return handles, outs[-1]


def copies_wait(handle, slotted, after, *, name):
    send_sem, recv_sem, x_thru, land_thru = handle

    def body(x_ref, land_ref, send_ref, recv_ref, after_ref, x_out, land_out):
        me, peers = _me_and_peers()
        for k, peer in enumerate(peers):
            _remote(x_ref, land_ref, slotted, me, peer, True, send_ref, recv_ref, k).wait_send()
        for k, peer in enumerate(peers):
            _remote(x_ref, land_ref, slotted, me, peer, False, send_ref, recv_ref, k).wait_recv()
        if not slotted:
            pltpu.make_async_copy(x_ref, land_ref.at[_index(me)], recv_ref.at[N_DEV - 1]).wait()

    return pl.pallas_call(
        body, name=name, out_shape=(pltpu.HBM(x_thru.shape, x_thru.dtype), pltpu.HBM(land_thru.shape, land_thru.dtype)),
        in_specs=(HBM_SPEC, HBM_SPEC, SEM_SPEC, SEM_SPEC, pl.BlockSpec(memory_space=pl.ANY)),
        out_specs=(HBM_SPEC, HBM_SPEC), input_output_aliases={0: 0, 1: 1},
        compiler_params=pltpu.CompilerParams(has_side_effects=DATAFLOW),
    )(x_thru, land_thru, send_sem, recv_sem, after)


def cast_bf16(x, *, ncols=None, name):
    R = x.shape[0]
    C = ncols or x.shape[1]
    tr = _pick(R, (512, 352, 256, 128, 64))

    def body(x_ref, o_ref):
        o_ref[...] = x_ref[...].astype(BF16)

    row = pl.BlockSpec((tr, C), lambda i: (i, 0))
    return pl.pallas_call(body, grid=(R // tr,), in_specs=[row], out_specs=row,
                          out_shape=jax.ShapeDtypeStruct((R, C), BF16), name=name)(x)


def cast_bf16_layer(x3, layer, *, name):
    _, R, C = x3.shape
    tr = _pick(R, (512, 352, 256, 128, 64))

    def body(x_ref, o_ref):
        o_ref[...] = x_ref[...].astype(BF16)

    return pl.pallas_call(body, grid=(R // tr,), in_specs=[pl.BlockSpec((None, tr, C), lambda i: (layer, i, 0))],
                          out_specs=pl.BlockSpec((tr, C), lambda i: (i, 0)),
                          out_shape=jax.ShapeDtypeStruct((R, C), BF16), name=name)(x3)


BD_PARTS = 4


def _blockdiag_call(b, build, G, r, c, name):
    gp = G // BD_PARTS

    def body_build(b_ref, o_ref):
        o_ref[...] = jnp.zeros_like(o_ref)
        for g in range(G):
            o_ref[g // gp, (g % gp) * r:(g % gp + 1) * r, (g % gp) * c:(g % gp + 1) * c] = b_ref[g]

    def body_extract(d_ref, o_ref):
        for g in range(G):
            o_ref[g] = d_ref[g // gp, (g % gp) * r:(g % gp + 1) * r, (g % gp) * c:(g % gp + 1) * c]

    out = jax.ShapeDtypeStruct((BD_PARTS, gp * r, gp * c) if build else (G, r, c), F32)
    return pl.pallas_call(body_build if build else body_extract, out_shape=out, name=name)(b)


def make_blockdiag(G, r, c, name):
    @jax.custom_vjp
    def blockdiag(b):
        return _blockdiag_call(b, True, G, r, c, name + "_build")

    def fwd(b):
        return blockdiag(b), None

    def bwd(_, g):
        return (_blockdiag_call(g, False, G, r, c, name + "_extract"),)

    blockdiag.defvjp(fwd, bwd)
    return blockdiag


def _my_index():
    return 4 * lax.axis_index("x") + 2 * lax.axis_index("y") + lax.axis_index("c")


def cols_from_shards(g, *, name):
    _, K, n = g.shape
    tk = _pick(K, (256, 128))

    def body(g_ref, o_ref):
        for i in range(N_DEV):
            o_ref[:, i * n:(i + 1) * n] = g_ref[i]

    return pl.pallas_call(body, grid=(K // tk,), in_specs=[pl.BlockSpec((N_DEV, tk, n), lambda i: (0, i, 0))],
                          out_specs=pl.BlockSpec((tk, N_DEV * n), lambda i: (i, 0)),
                          out_shape=jax.ShapeDtypeStruct((K, N_DEV * n), g.dtype), name=name)(g)


def shards_from_cols(w, *, name):
    K, N = w.shape
    n = N // N_DEV
    tk = _pick(K, (256, 128))

    def body(w_ref, o_ref):
        for i in range(N_DEV):
            o_ref[i] = w_ref[:, i * n:(i + 1) * n].astype(o_ref.dtype)

    return pl.pallas_call(body, grid=(K // tk,), in_specs=[pl.BlockSpec((tk, N), lambda i: (i, 0))],
                          out_specs=pl.BlockSpec((N_DEV, tk, n), lambda i: (0, i, 0)),
                          out_shape=jax.ShapeDtypeStruct((N_DEV, K, n), BF16), name=name)(w)


def _adamw(w, g, m, v):
    m = ADAM_B1 * m + (1.0 - ADAM_B1) * g
    v = ADAM_B2 * v + (1.0 - ADAM_B2) * (g * g)
    m_hat = m / (1.0 - ADAM_B1 ** ADAM_STEP)
    v_hat = v / (1.0 - ADAM_B2 ** ADAM_STEP)
    delta = -ADAM_LR * (m_hat / (jnp.sqrt(v_hat) + ADAM_EPS) + ADAM_WD * w)
    return delta, m, v


def reduce_adamw(recv, own, own_slotted, me, w, m, v, *, layer=0, n_layers=1, into=None, name):
    _, R, C = recv.shape
    tr = _pick(R, (352, 320, 288, 256, 128, 64, 32, 16, 8))
    off = layer * (R // tr)

    def body(me_ref, r_ref, own_ref, w_ref, m_ref, v_ref, *rest):
        g_ref, d_ref, nm_ref, nv_ref = rest[-4:]
        mine = me_ref[0]
        g = None
        for i in range(N_DEV):
            part = jnp.where(mine == i, own_ref[...], r_ref[i]).astype(F32)
            g = part if g is None else g + part
        delta, nm, nv = _adamw(w_ref[...], g, m_ref[...], v_ref[...])
        g_ref[...] = g
        d_ref[...] = delta
        nm_ref[...] = nm
        nv_ref[...] = nv

    row = pl.BlockSpec((tr, C), lambda i, me_ref: (i + off, 0))
    own_spec = (pl.BlockSpec((None, tr, C), lambda i, me_ref: (me_ref[0], i, 0)) if own_slotted
                else pl.BlockSpec((tr, C), lambda i, me_ref: (i, 0)))
    rest = [] if into is None else list(into)
    grid_spec = pltpu.PrefetchScalarGridSpec(
        num_scalar_prefetch=1, grid=(R // tr,),
        in_specs=[pl.BlockSpec((N_DEV, tr, C), lambda i, me_ref: (0, i, 0)), own_spec, row, row, row]
        + [pl.BlockSpec(memory_space=pl.ANY)] * len(rest),
        out_specs=[row] * 4)
    return pl.pallas_call(body, grid_spec=grid_spec, out_shape=[jax.ShapeDtypeStruct((n_layers * R, C), F32)] * 4,
                          input_output_aliases={6 + k: k for k in range(len(rest))},
                          compiler_params=_params(dimension_semantics=("parallel",)), name=name)(
        me.reshape(1).astype(jnp.int32), recv, own, w, m, v, *rest)


def _s5_prepare(A_re, A_im, log_dt, B_re, B_im, C_re, C_im):
    G, P, Cg = S5_GROUPS, S5_STATE, S5_GROUP
    dt = jnp.exp(log_dt)[:, None]
    mag = jnp.exp(A_re * dt)
    ab_re = mag * jnp.cos(A_im * dt)
    ab_im = mag * jnp.sin(A_im * dt)
    den = A_re * A_re + A_im * A_im
    nr, ni = ab_re - 1.0, ab_im
    c_re = (nr * A_re + ni * A_im) / den
    c_im = (ni * A_re - nr * A_im) / den
    Bb_re = c_re[..., None] * B_re - c_im[..., None] * B_im
    Bb_im = c_re[..., None] * B_im + c_im[..., None] * B_re
    def dense_in(b, name):
        return make_blockdiag(G, Cg, P, name)(b.transpose(0, 2, 1))

    def dense_out(c, name):
        return make_blockdiag(G, P, Cg, name)(c.transpose(0, 2, 1))

    return (ab_re.reshape(1, G * P), ab_im.reshape(1, G * P), dense_in(Bb_re, "s5_wb_re"), dense_in(Bb_im, "s5_wb_im"),
            dense_out(C_re, "s5_wc_re"), dense_out(-C_im, "s5_wc_im"))


def _lower_bound(gamma):
    return jnp.cumsum(jax.nn.softmax(gamma, axis=0), axis=0)[0:1]


def _ffn_fwd(h, g_norm, get_w_in, conv_w, conv_b, get_w_out, tag):
    hn = rms_fwd(h, g_norm, name=tag + "_rms")
    w_in = get_w_in(hn)
    hu = mm(hn, w_in, tb=True, name=tag + "_in")
    act = convgate_fwd(hu, conv_w, conv_b, name=tag + "_gate")
    w_out = get_w_out(act)
    h_out = mm(act, w_out, res=h, name=tag + "_out")
    return h_out, (hn, hu, act), w_in, w_out


def _ffn_bwd(h, g_norm, w_in, conv_w, conv_b, w_out, saved, dh, tag, send_dw_in, send_dw_out):
    hn, hu, act = saved
    sent = send_dw_out(mm(act, dh, ta=True, out_dtype=BF16, name=tag + "_dwout"))
    dact = mm(dh, w_out, tb=True, dep=sent, name=tag + "_dact")
    dhu, dconv_w, dconv_b = convgate_bwd(hu, conv_w, conv_b, dact, name=tag + "_dgate")
    sent = send_dw_in(mm(dhu, hn, ta=True, out_dtype=BF16, name=tag + "_dwin"))
    dh_in, dg = mm_drms(dhu, w_in, h, g_norm, dh, dep=sent, name=tag + "_dhn")
    return dh_in, dg, dconv_w, dconv_b


def kernel(x, positions, norm_mix, norm_ffn, norm_final, mix_w_in, mix_w_out, s5_A_re, s5_A_im, s5_log_dt, s5_B_re, s5_B_im, s5_C_re, s5_C_im, s5_D, s5_glu_w, s5_glu_b, hgrn_gamma, hgrn_norm, att_w_qkv, att_w_o, ffn_w_in, ffn_conv_w, ffn_conv_b, ffn_w_out, loss_target, m_norm_mix, m_norm_ffn, m_norm_final, m_mix_w_in, m_mix_w_out, m_s5_A_re, m_s5_A_im, m_s5_log_dt, m_s5_B_re, m_s5_B_im, m_s5_C_re, m_s5_C_im, m_s5_D, m_s5_glu_w, m_s5_glu_b, m_hgrn_gamma, m_hgrn_norm, m_att_w_qkv, m_att_w_o, m_ffn_w_in, m_ffn_conv_w, m_ffn_conv_b, m_ffn_w_out, v_norm_mix, v_norm_ffn, v_norm_final, v_mix_w_in, v_mix_w_out, v_s5_A_re, v_s5_A_im, v_s5_log_dt, v_s5_B_re, v_s5_B_im, v_s5_C_re, v_s5_C_im, v_s5_D, v_s5_glu_w, v_s5_glu_b, v_hgrn_gamma, v_hgrn_norm, v_att_w_qkv, v_att_w_o, v_ffn_w_in, v_ffn_conv_w, v_ffn_conv_b, v_ffn_w_out):
    W = dict(norm_mix=norm_mix, norm_ffn=norm_ffn, norm_final=norm_final, mix_w_in=mix_w_in, mix_w_out=mix_w_out,
             s5_A_re=s5_A_re, s5_A_im=s5_A_im, s5_log_dt=s5_log_dt, s5_B_re=s5_B_re, s5_B_im=s5_B_im,
             s5_C_re=s5_C_re, s5_C_im=s5_C_im, s5_D=s5_D, s5_glu_w=s5_glu_w, s5_glu_b=s5_glu_b,
             hgrn_gamma=hgrn_gamma, hgrn_norm=hgrn_norm, att_w_qkv=att_w_qkv, att_w_o=att_w_o, ffn_w_in=ffn_w_in,
             ffn_conv_w=ffn_conv_w, ffn_conv_b=ffn_conv_b, ffn_w_out=ffn_w_out)
    M = dict(norm_mix=m_norm_mix, norm_ffn=m_norm_ffn, norm_final=m_norm_final, mix_w_in=m_mix_w_in,
             mix_w_out=m_mix_w_out, s5_A_re=m_s5_A_re, s5_A_im=m_s5_A_im, s5_log_dt=m_s5_log_dt, s5_B_re=m_s5_B_re,
             s5_B_im=m_s5_B_im, s5_C_re=m_s5_C_re, s5_C_im=m_s5_C_im, s5_D=m_s5_D, s5_glu_w=m_s5_glu_w,
             s5_glu_b=m_s5_glu_b, hgrn_gamma=m_hgrn_gamma, hgrn_norm=m_hgrn_norm, att_w_qkv=m_att_w_qkv,
             att_w_o=m_att_w_o, ffn_w_in=m_ffn_w_in, ffn_conv_w=m_ffn_conv_w, ffn_conv_b=m_ffn_conv_b,
             ffn_w_out=m_ffn_w_out)
    V = dict(norm_mix=v_norm_mix, norm_ffn=v_norm_ffn, norm_final=v_norm_final, mix_w_in=v_mix_w_in,
             mix_w_out=v_mix_w_out, s5_A_re=v_s5_A_re, s5_A_im=v_s5_A_im, s5_log_dt=v_s5_log_dt, s5_B_re=v_s5_B_re,
             s5_B_im=v_s5_B_im, s5_C_re=v_s5_C_re, s5_C_im=v_s5_C_im, s5_D=v_s5_D, s5_glu_w=v_s5_glu_w,
             s5_glu_b=v_s5_glu_b, hgrn_gamma=v_hgrn_gamma, hgrn_norm=v_hgrn_norm, att_w_qkv=v_att_w_qkv,
             att_w_o=v_att_w_o, ffn_w_in=v_ffn_w_in, ffn_conv_w=v_ffn_conv_w, ffn_conv_b=v_ffn_conv_b,
             ffn_w_out=v_ffn_w_out)
    return _step(x[0], positions[0], loss_target[0], W, M, V)


TRANSPOSED = ("mix_w_in", "att_w_qkv", "ffn_w_in")
SMALL = ("norm_mix", "norm_ffn", "norm_final", "s5_A_re", "s5_A_im", "s5_log_dt", "s5_B_re", "s5_B_im", "s5_C_re",
         "s5_C_im", "s5_D", "s5_glu_b", "hgrn_gamma", "hgrn_norm", "ffn_conv_b")
ORDER = ("norm_mix", "norm_ffn", "norm_final", "mix_w_in", "mix_w_out", "s5_A_re", "s5_A_im", "s5_log_dt", "s5_B_re",
         "s5_B_im", "s5_C_re", "s5_C_im", "s5_D", "s5_glu_w", "s5_glu_b", "hgrn_gamma", "hgrn_norm", "att_w_qkv",
         "att_w_o", "ffn_w_in", "ffn_conv_w", "ffn_conv_b", "ffn_w_out")
PACK_COLS = 1024


def _step(x, positions, target, W, M, V):
    L, D = x.shape
    me = 4 * lax.axis_index("x") + 2 * lax.axis_index("y") + lax.axis_index("c")
    n_cw = W["ffn_conv_w"].shape[-1]
    T = {n: tuple(jnp.swapaxes(d[n], -1, -2) for d in (W, M, V)) for n in TRANSPOSED}
    shards = {
        "mix_w_in": cast_bf16(T["mix_w_in"][0][0], name="mix_w_in_cast"),
        "conv_w": W["ffn_conv_w"].reshape(6, n_cw),
        "s5_glu_w": cast_bf16(W["s5_glu_w"][0], name="s5_glu_w_cast"),
        "mix_w_out": cast_bf16(W["mix_w_out"][0], name="mix_w_out_cast"),
        "ffn_w_in0": cast_bf16_layer(T["ffn_w_in"][0], 0, name="ffn_w_in0_cast"),
        "ffn_w_out0": cast_bf16_layer(W["ffn_w_out"], 0, name="ffn_w_out0_cast"),
        "att_w_qkv": cast_bf16(T["att_w_qkv"][0][0], name="att_w_qkv_cast"),
        "att_w_o": cast_bf16(W["att_w_o"][0], name="att_w_o_cast"),
        "ffn_w_in1": cast_bf16_layer(T["ffn_w_in"][0], 1, name="ffn_w_in1_cast"),
        "ffn_w_out1": cast_bf16_layer(W["ffn_w_out"], 1, name="ffn_w_out1_cast"),
    }
    gather_handles, token = copies_start(list(shards.values()), False, name="gather_start")
    gather_handle = dict(zip(shards, gather_handles))

    def gathered(key, after, cols):
        _, land = copies_wait(gather_handle[key], False, after, name=key + "_gwait")
        return cols_from_shards(land, name=key + "_asm") if cols else land.reshape(-1, land.shape[-1])

    conv_b = W["ffn_conv_b"].reshape(2, 1, -1)

    s5_params = (W["s5_A_re"][0], W["s5_A_im"][0], W["s5_log_dt"][0], W["s5_B_re"][0], W["s5_B_im"][0],
                 W["s5_C_re"][0], W["s5_C_im"][0])
    (a_re, a_im, wb_re, wb_im, wc_re, wc_im), s5_prep_vjp = jax.vjp(_s5_prepare, *s5_params)
    dvec = W["s5_D"].reshape(1, S5_WIDTH)
    glu_b = W["s5_glu_b"].reshape(1, S5_WIDTH)
    lb, lb_vjp = jax.vjp(_lower_bound, W["hgrn_gamma"])
    hg_norm = W["hgrn_norm"].reshape(1, -1)
    tabs = rope_tables(positions)

    hn0 = rms_fwd(x, W["norm_mix"][0], dep=token, name="l0_rms")
    w_mix_in = gathered("mix_w_in", hn0, False)
    proj = mm(hn0, w_mix_in, tb=True, name="l0_proj")
    y0, xs_re, xs_im = s5_core_fwd(proj, a_re, a_im, wb_re, wb_im, wc_re, wc_im, name="s5_core")
    w_glu = gathered("s5_glu_w", y0, False)
    oa = s5_out_fwd(y0, proj, dvec, w_glu, glu_b, name="s5_out")
    ob, hg_states = hgrn_fwd(proj, lb, hg_norm, name="hgrn_fwd")
    cat = jnp.concatenate([oa, ob], axis=1)
    w_mix_out = gathered("mix_w_out", cat, False)
    h1 = mm(cat, w_mix_out, res=x, name="l0_mix_out")
    _, cw_all = copies_wait(gather_handle["conv_w"], False, h1, name="conv_w_gwait")
    conv_w = cw_all.transpose(1, 0, 2).reshape(2, 3, N_DEV * n_cw)
    w_ffn_in, w_ffn_out = [None, None], [None, None]
    h2, ffn0_saved, w_ffn_in[0], w_ffn_out[0] = _ffn_fwd(
        h1, W["norm_ffn"][0], lambda a: gathered("ffn_w_in0", a, False), conv_w[0], conv_b[0],
        lambda a: gathered("ffn_w_out0", a, False), "ffn0")

    hn2 = rms_fwd(h2, W["norm_mix"][1], name="l1_rms")
    w_qkv = gathered("att_w_qkv", hn2, False)
    qkv_r = qkv_rope(hn2, w_qkv, tabs, name="l1_qkv")
    att_o, att_l = [], []
    for g, d in enumerate(ATT_DILATIONS):
        o_g, l_g = attn_fwd(qkv_r, g, d, name=f"attn_fwd{g}")
        att_o.append(o_g)
        att_l.append(l_g)
    o_att = merge_fwd(att_o, att_l, name="merge_fwd")
    w_o = gathered("att_w_o", o_att, True)
    h3 = mm(o_att, w_o, res=h2, name="l1_mix_out")
    h4, ffn1_saved, w_ffn_in[1], w_ffn_out[1] = _ffn_fwd(
        h3, W["norm_ffn"][1], lambda a: gathered("ffn_w_in1", a, False), conv_w[1], conv_b[1],
        lambda a: gathered("ffn_w_out1", a, False), "ffn1")

    exchanges = {}

    def send_grad(key, g, cols):
        if cols:
            parts = shards_from_cols(g, name=key + "_split")
        else:
            parts = g.reshape(N_DEV, g.shape[0] // N_DEV, g.shape[1])
        (handle,), sent = copies_start([parts], True, name=key + "_xstart")
        exchanges[key] = handle
        return sent

    loss, dh4, dg_final = final_loss(h4, W["norm_final"], target, name="final_loss")
    dh3, dg_ffn1, dcw1, dcb1 = _ffn_bwd(h3, W["norm_ffn"][1], w_ffn_in[1], conv_w[1], conv_b[1], w_ffn_out[1],
                                        ffn1_saved, dh4, "ffn1", lambda g: send_grad("ffn_w_in1", g, False),
                                        lambda g: send_grad("ffn_w_out1", g, False))
    sent = send_grad("att_w_o", mm(o_att, dh3, ta=True, name="l1_dwo"), True)
    d_oatt = mm(dh3, w_o, tb=True, dep=sent, name="l1_dmix")
    mb = merge_bwd(att_o, att_l, d_oatt, name="merge_bwd")
    d_slabs = [attn_bwd(qkv_r, g, att_l[g], mb[g], mb[3 + g], d, name=f"attn_bwd{g}")
               for g, d in enumerate(ATT_DILATIONS)]
    d_qkv = rope_bwd([s[0] for s in d_slabs] + [s[1] for s in d_slabs] + [s[2] for s in d_slabs], tabs,
                     name="rope_bwd")
    sent = send_grad("att_w_qkv", mm(d_qkv, hn2, ta=True, out_dtype=BF16, name="l1_dwqkv"), False)
    dh2, dg_mix1 = mm_drms(d_qkv, w_qkv, h2, W["norm_mix"][1], dh3, dep=sent, name="l1_dhn")

    dh1, dg_ffn0, dcw0, dcb0 = _ffn_bwd(h1, W["norm_ffn"][0], w_ffn_in[0], conv_w[0], conv_b[0], w_ffn_out[0],
                                        ffn0_saved, dh2, "ffn0", lambda g: send_grad("ffn_w_in0", g, False),
                                        lambda g: send_grad("ffn_w_out0", g, False))
    sent = send_grad("mix_w_out", mm(cat, dh1, ta=True, out_dtype=BF16, name="l0_dwout"), False)
    dcat = mm(dh1, w_mix_out, tb=True, dep=sent, name="l0_dcat")
    d_hg, dlb, dhg_norm = hgrn_bwd(proj, lb, hg_norm, hg_states, dcat, name="hgrn_bwd")
    dy, du_d, z_bf, dzg, dglu_b, dD = s5_out_bwd(y0, proj, dvec, w_glu, glu_b, dcat, name="s5_dout")
    sent_glu = send_grad("s5_glu_w", mm(z_bf, dzg, ta=True, out_dtype=BF16, name="s5_dglu"), False)
    du, dwb_re, dwb_im, dwc_re, dwc_im, da_re, da_im = s5_core_bwd(
        dy, du_d, proj, xs_re, xs_im, a_re, a_im, wb_re, wb_im, wc_re, wc_im, name="s5_dcore")
    s5_small = s5_prep_vjp((da_re, da_im, dwb_re, dwb_im, dwc_re, dwc_im))
    d_proj = jnp.concatenate([du, d_hg], axis=1)
    sent = send_grad("mix_w_in", mm(d_proj, hn0, ta=True, out_dtype=BF16, dep=sent_glu, name="l0_dwin"), False)
    grad_x, dg_mix0 = mm_drms(d_proj, w_mix_in, x, W["norm_mix"][0], dh1, dep=sent, name="l0_dhn")
    (d_gamma,) = lb_vjp(dlb)
    out = {}

    dA_re, dA_im, dlog_dt, dB_re, dB_im, dC_re, dC_im = s5_small
    small_g = dict(norm_mix=jnp.concatenate([dg_mix0, dg_mix1], axis=0), norm_ffn=jnp.concatenate([dg_ffn0, dg_ffn1], axis=0),
                   norm_final=dg_final, s5_A_re=dA_re, s5_A_im=dA_im, s5_log_dt=dlog_dt, s5_B_re=dB_re, s5_B_im=dB_im,
                   s5_C_re=dC_re, s5_C_im=dC_im, s5_D=dD, s5_glu_b=dglu_b, hgrn_gamma=d_gamma, hgrn_norm=dhg_norm,
                   ffn_conv_b=jnp.concatenate([dcb0, dcb1], axis=0))
    conv_w_g = jnp.stack([dcw0, dcw1], axis=0)
    sizes = [math.prod(W[n].shape) for n in SMALL]
    n_conv = conv_w_g.size
    total = sum(sizes) + n_conv + 1
    rows = -(-total // PACK_COLS)
    rows = -(-rows // 8) * 8
    pad = rows * PACK_COLS - total

    def pack(vals, conv_part, last):
        flat = [v.reshape(-1).astype(F32) for v in vals] + [conv_part.reshape(-1), last.reshape(-1),
                                                            jnp.zeros((pad,), F32)]
        return jnp.concatenate(flat).reshape(rows, PACK_COLS)

    def conv_full(shard):
        col_owner = lax.broadcasted_iota(jnp.int32, (2, 3, N_DEV * n_cw), 2) // n_cw
        return jnp.where(col_owner == me, jnp.tile(shard, (1, 1, N_DEV)), 0.0)

    zero1 = jnp.zeros((1,), F32)
    g_pack = pack([small_g[n] for n in SMALL], conv_w_g, loss)
    w_pack = pack([W[n] for n in SMALL], conv_full(W["ffn_conv_w"]), zero1)
    m_pack = pack([M[n] for n in SMALL], conv_full(M["ffn_conv_w"]), zero1)
    v_pack = pack([V[n] for n in SMALL], conv_full(V["ffn_conv_w"]), zero1 + 1.0)
    (small_handle,), small_sent = copies_start([g_pack], False, name="small_xstart")

    def finish(name, n_layers):
        w3, m3, v3 = T[name] if name in TRANSPOSED else (W[name], M[name], V[name])
        res = None
        for layer in reversed(range(n_layers)):
            key = name if n_layers == 1 else f"{name}{layer}"
            own, recv = copies_wait(exchanges[key], True, small_sent, name=key + "_xwait")
            _, R, Cn = recv.shape
            res = reduce_adamw(recv, own, True, me, w3.reshape(n_layers * R, Cn), m3.reshape(n_layers * R, Cn),
                               v3.reshape(n_layers * R, Cn), layer=layer, n_layers=n_layers, into=res,
                               name=key + "_adamw")
        res = [r.reshape(w3.shape) for r in res]
        return tuple(jnp.swapaxes(r, -1, -2) for r in res) if name in TRANSPOSED else tuple(res)

    for name in ("ffn_w_out", "ffn_w_in"):
        out[name] = finish(name, 2)
    for name in ("att_w_o", "att_w_qkv", "mix_w_out", "s5_glu_w", "mix_w_in"):
        out[name] = finish(name, 1)

    small_own, small_recv = copies_wait(small_handle, False, out["s5_glu_w"][0], name="small_xwait")
    res = reduce_adamw(small_recv, small_own, False, me, w_pack, m_pack, v_pack, name="small_adamw")
    flat = [r.reshape(-1) for r in res]
    off = 0
    for n, sz in zip(SMALL, sizes):
        out[n] = tuple(f[off:off + sz].reshape(W[n].shape) for f in flat)
        off += sz
    conv_res = [f[off:off + n_conv].reshape(2, 3, N_DEV * n_cw) for f in flat]
    out["ffn_conv_w"] = tuple(lax.dynamic_slice(c, (0, 0, me * n_cw), (2, 3, n_cw)) for c in conv_res)
    off += n_conv
    loss_total = flat[0][off]

    result = [loss_total, grad_x[None]]
    for k in range(4):
        result += [out[n][k] for n in ORDER]
    return tuple(result)
```

```python
import functools
import math

import jax
import jax.numpy as jnp
from jax import lax
from jax.experimental import pallas as pl
from jax.experimental.pallas import tpu as pltpu

F32 = jnp.float32
BF16 = jnp.bfloat16
MESH_ID = pl.DeviceIdType.MESH
N_DEV = 8
VMEM_LIMIT_BYTES = 56 * 1024 * 1024

NORM_EPS = 1e-6
S5_WIDTH, S5_GROUP, S5_GROUPS, S5_STATE = 512, 16, 32, 64
HG_HEADS, HG_DIM, HG_CHUNK = 4, 128, 64
HG_STEP_CHUNKS = 4
ATT_E, ATT_HPG, ATT_BLOCK = 64, 8, 128
ATT_DILATIONS = (1, 4, 16)
ROT_DIM, ROPE_THETA = 16, 500000.0
D_FF = 2816
ADAM_LR, ADAM_B1, ADAM_B2, ADAM_EPS, ADAM_WD, ADAM_STEP = 0.001, 0.9, 0.999, 1e-08, 0.01, 10
NEG_BIG = -1e30


def _params(**kw):
    return pltpu.CompilerParams(vmem_limit_bytes=VMEM_LIMIT_BYTES, **kw)


def _pick(n, cands):
    for c in cands:
        if n % c == 0:
            return c
    return n


def _dot(a, b):
    return jnp.dot(a.astype(BF16), b.astype(BF16), preferred_element_type=F32)


def _dot_nt(a, b):
    return lax.dot_general(a.astype(BF16), b.astype(BF16), (((1,), (1,)), ((), ())), preferred_element_type=F32)


def _dot_tn(a, b):
    return lax.dot_general(a.astype(BF16), b.astype(BF16), (((0,), (0,)), ((), ())), preferred_element_type=F32)


def _split2(x):
    hi = x.astype(BF16)
    return hi, (x - hi.astype(F32)).astype(BF16)


def _dot_x3(a, b, contract=((1,), (0,))):
    dn = (contract, ((), ()))
    a1, a2 = _split2(a)
    b1, b2 = _split2(b)
    return (lax.dot_general(a1, b1, dn, preferred_element_type=F32) + lax.dot_general(a1, b2, dn, preferred_element_type=F32)
            + lax.dot_general(a2, b1, dn, preferred_element_type=F32))


def _sigmoid(x):
    return 1.0 / (1.0 + jnp.exp(-x))


V7X_HBM_BYTES_PER_S = 3.2e12
V7X_MXU_FLOPS_PER_S = 0.7e15
GRID_STEP_S = 0.35e-6
MM_VMEM_BUDGET = 40 * 1024 * 1024


def _divisors(n, cands):
    return [c for c in cands if c <= n and n % c == 0] or [n]


def _mm_tiles(m, n, k, sa, sb, so, sr):
    best = None
    for tm in _divisors(m, (2816, 2048, 1408, 1024, 512, 256, 128)):
        for tn in _divisors(n, (2816, 2048, 1408, 1024, 512, 256, 128)):
            for tk in _divisors(k, (k, 2816, 2560, 2304, 2048, 1536, 1408, 1280, 1024, 512, 256, 128)):
                nk = k // tk
                vmem = 2 * (tm * tk * sa + tk * tn * sb + tm * tn * (so + sr)) + (tm * tn * 4 if nk > 1 else 0)
                vmem += tm * tk * 2 * (sa > 2) + tk * tn * 2 * (sb > 2) + tm * tn * 4
                if vmem > MM_VMEM_BUDGET:
                    continue
                ni, nj = m // tm, n // tn
                for i_outer in (True, False):
                    if i_outer:
                        a_reads = 1 if nk == 1 else nj
                        b_reads = 1 if (nk == 1 and nj == 1) else ni
                    else:
                        b_reads = 1 if nk == 1 else ni
                        a_reads = 1 if (nk == 1 and ni == 1) else nj
                    traffic = a_reads * m * k * sa + b_reads * k * n * sb + m * n * (so + sr)
                    t = max(traffic / V7X_HBM_BYTES_PER_S, 2.0 * m * n * k / V7X_MXU_FLOPS_PER_S)
                    t += ni * nj * nk * GRID_STEP_S
                    t += (tm * tk * sa + tk * tn * sb + tm * tn * so) / V7X_HBM_BYTES_PER_S
                    if best is None or t < best[0]:
                        best = (t, tm, tn, tk, i_outer)
    assert best is not None, (m, n, k)
    return best[1:]


def mm(a, b, *, ta=False, tb=False, res=None, out_dtype=F32, dep=None, name):
    m, k = (a.shape[1], a.shape[0]) if ta else a.shape
    n = b.shape[0] if tb else b.shape[1]
    assert (b.shape[1] if tb else b.shape[0]) == k
    has_res = res is not None
    tm, tn, tk, i_outer = _mm_tiles(m, n, k, a.dtype.itemsize, b.dtype.itemsize, jnp.dtype(out_dtype).itemsize,
                                    res.dtype.itemsize if has_res else 0)
    nk = k // tk
    deps = [] if dep is None else [dep]
    dn = (((0 if ta else 1,), (1 if tb else 0,)), ((), ()))

    def body_single(*refs):
        a_ref, b_ref = refs[:2]
        o_ref = refs[-1]
        out = lax.dot_general(a_ref[...].astype(BF16), b_ref[...].astype(BF16), dn, preferred_element_type=F32)
        if has_res:
            out = out + refs[2][...].astype(F32)
        o_ref[...] = out.astype(o_ref.dtype)

    def body(*refs):
        a_ref, b_ref = refs[:2]
        r_ref = refs[2] if has_res else None
        o_ref, acc_ref = refs[-2:]
        kk = pl.program_id(2)
        part = lax.dot_general(a_ref[...].astype(BF16), b_ref[...].astype(BF16), dn, preferred_element_type=F32)

        @pl.when(kk == 0)
        def _():
            acc_ref[...] = part

        @pl.when(kk > 0)
        def _():
            acc_ref[...] += part

        @pl.when(kk == nk - 1)
        def _():
            out = acc_ref[...]
            if has_res:
                out = out + r_ref[...].astype(F32)
            o_ref[...] = out.astype(o_ref.dtype)

    def ij(f):
        return (lambda g0, g1, q: f(g0, g1, q)) if i_outer else (lambda g0, g1, q: f(g1, g0, q))

    a_spec = pl.BlockSpec((tk, tm), ij(lambda i, j, q: (q, i))) if ta else pl.BlockSpec((tm, tk), ij(lambda i, j, q: (i, q)))
    b_spec = pl.BlockSpec((tn, tk), ij(lambda i, j, q: (j, q))) if tb else pl.BlockSpec((tk, tn), ij(lambda i, j, q: (q, j)))
    o_spec = pl.BlockSpec((tm, tn), ij(lambda i, j, q: (i, j)))
    in_specs = [a_spec, b_spec] + ([o_spec] if has_res else []) + [pl.BlockSpec((8, 128), lambda g0, g1, q: (0, 0))] * len(deps)
    args = (a, b) + ((res,) if has_res else ()) + tuple(deps)
    grid = (m // tm, n // tn, nk) if i_outer else (n // tn, m // tm, nk)
    return pl.pallas_call(
        body_single if nk == 1 else body, grid=grid, in_specs=in_specs, out_specs=o_spec,
        out_shape=jax.ShapeDtypeStruct((m, n), out_dtype),
        scratch_shapes=[] if nk == 1 else [pltpu.VMEM((tm, tn), F32)],
        compiler_params=_params(dimension_semantics=("parallel", "parallel", "arbitrary")), name=name,
    )(*args)


def mm_drms(dy_in, w, x, g, dres, *, dep=None, name):
    m, k = dy_in.shape
    D = w.shape[1]
    tm = _pick(m, (512, 256, 128))
    tk = max(_divisors(k, (1536, 1408, 1280, 1024, 512, 256, 128)))
    nk = k // tk
    deps = [] if dep is None else [dep]

    def body(a_ref, b_ref, x_ref, g_ref, dres_ref, *rest):
        dx_ref, dg_ref, acc_ref = rest[-3:]
        i, q = pl.program_id(0), pl.program_id(1)
        part = jnp.dot(a_ref[...], b_ref[...], preferred_element_type=F32)

        @pl.when(q == 0)
        def _():
            acc_ref[...] = part

        @pl.when(q > 0)
        def _():
            acc_ref[...] += part

        @pl.when((i == 0) & (q == 0))
        def _():
            dg_ref[...] = jnp.zeros_like(dg_ref)

        @pl.when(q == nk - 1)
        def _():
            dyv = acc_ref[...]
            xv = x_ref[...]
            r = lax.rsqrt(jnp.mean(xv * xv, axis=-1, keepdims=True) + NORM_EPS)
            xh = xv * r
            dg_ref[...] += jnp.sum(dyv * xh, axis=0, keepdims=True)
            dxh = dyv * g_ref[...]
            dx_ref[...] = dres_ref[...] + r * (dxh - xh * jnp.mean(dxh * xh, axis=-1, keepdims=True))

    row = pl.BlockSpec((tm, D), lambda i, q: (i, 0))
    vec = pl.BlockSpec((1, D), lambda i, q: (0, 0))
    in_specs = [pl.BlockSpec((tm, tk), lambda i, q: (i, q)), pl.BlockSpec((tk, D), lambda i, q: (q, 0)), row, vec, row]
    in_specs += [pl.BlockSpec((8, 128), lambda i, q: (0, 0))] * len(deps)
    return pl.pallas_call(
        body, grid=(m // tm, nk), in_specs=in_specs, out_specs=[row, vec],
        out_shape=[jax.ShapeDtypeStruct((m, D), F32), jax.ShapeDtypeStruct((1, D), F32)],
        scratch_shapes=[pltpu.VMEM((tm, D), F32)],
        compiler_params=_params(dimension_semantics=("arbitrary", "arbitrary")), name=name,
    )(dy_in, w, x, g.reshape(1, D), dres, *deps)


def rms_fwd(x, g, *, dep=None, name):
    L, D = x.shape
    tr = _pick(L, (256, 128))

    def body(x_ref, g_ref, *rest):
        o_ref = rest[-1]
        xv = x_ref[...]
        r = lax.rsqrt(jnp.mean(xv * xv, axis=-1, keepdims=True) + NORM_EPS)
        o_ref[...] = (xv * r * g_ref[...]).astype(o_ref.dtype)

    row = pl.BlockSpec((tr, D), lambda i: (i, 0))
    vec = pl.BlockSpec((1, D), lambda i: (0, 0))
    deps = [] if dep is None else [dep]
    return pl.pallas_call(body, grid=(L // tr,), in_specs=[row, vec] + [pl.BlockSpec((8, 128), lambda i: (0, 0))] * len(deps),
                          out_specs=row, out_shape=jax.ShapeDtypeStruct((L, D), BF16), name=name)(
        x, g.reshape(1, D), *deps)


def final_loss(h, g, target, *, name):
    L, D = h.shape
    tr = _pick(L, (256, 128))

    def body(x_ref, g_ref, t_ref, loss_ref, dx_ref, dg_ref):
        xv = x_ref[...]
        gv = g_ref[...]
        r = lax.rsqrt(jnp.mean(xv * xv, axis=-1, keepdims=True) + NORM_EPS)
        xh = xv * r
        err = xh * gv - t_ref[...]

        @pl.when(pl.program_id(0) == 0)
        def _():
            dg_ref[...] = jnp.zeros_like(dg_ref)
            loss_ref[...] = jnp.zeros_like(loss_ref)

        loss_ref[...] += 0.5 * jnp.sum(jnp.mean(err * err, axis=-1, keepdims=True), axis=0, keepdims=True)
        dyv = err * (1.0 / D)
        dg_ref[...] += jnp.sum(dyv * xh, axis=0, keepdims=True)
        dxh = dyv * gv
        dx_ref[...] = r * (dxh - xh * jnp.mean(dxh * xh, axis=-1, keepdims=True))

    row = pl.BlockSpec((tr, D), lambda i: (i, 0))
    vec = pl.BlockSpec((1, D), lambda i: (0, 0))
    one = pl.BlockSpec((1, 1), lambda i: (0, 0))
    return pl.pallas_call(body, grid=(L // tr,), in_specs=[row, vec, row], out_specs=[one, row, vec],
                          out_shape=[jax.ShapeDtypeStruct((1, 1), F32), jax.ShapeDtypeStruct((L, D), F32),
                                     jax.ShapeDtypeStruct((1, D), F32)],
                          compiler_params=_params(dimension_semantics=("arbitrary",)), name=name)(
        h, g.reshape(1, D), target)


def _cmul(ar, ai, br, bi):
    return ar * br - ai * bi, ar * bi + ai * br


def _powers(ar, ai):
    rows = [(ar, ai)]
    for _ in range(7):
        rows.append(_cmul(rows[-1][0], rows[-1][1], ar, ai))
    table = (jnp.concatenate([r[0] for r in rows], axis=0), jnp.concatenate([r[1] for r in rows], axis=0))
    return (rows[0], rows[1], rows[3]), table


def _block_scan(br, bi, steps, shift):
    yr, yi = br, bi
    for s, (pr, pi) in zip((1, 2, 4), steps):
        sr, si = shift(yr, s), shift(yi, s)
        yr, yi = yr + pr * sr - pi * si, yi + pr * si + pi * sr
    return yr, yi


def s5_core_fwd(proj, a_re, a_im, wb_re, wb_im, wc_re, wc_im, *, name):
    L = proj.shape[0]
    parts, cu, W = wb_re.shape

    def body(u_ref, ar_ref, ai_ref, wbr_ref, wbi_ref, wcr_ref, wci_ref, y_ref, xr_ref, xi_ref, br_ref, bi_ref):
        u = u_ref[...]
        br_ref[...] = _dot(u, wbr_ref[...])
        bi_ref[...] = _dot(u, wbi_ref[...])
        steps, (tr, ti) = _powers(ar_ref[...], ai_ref[...])
        row = lax.broadcasted_iota(jnp.int32, (8, W), 0)

        def shift(y, s):
            return jnp.where(row >= s, pltpu.roll(y, s, 0), 0.0)

        def step(t8, carry):
            cr, ci = carry
            base = pl.multiple_of(t8 * 8, 8)
            yr, yi = _block_scan(br_ref[pl.ds(base, 8), :], bi_ref[pl.ds(base, 8), :], steps, shift)
            xr = yr + tr * cr - ti * ci
            xi = yi + tr * ci + ti * cr
            xr_ref[pl.ds(base, 8), :] = xr
            xi_ref[pl.ds(base, 8), :] = xi
            return jnp.broadcast_to(xr[7:8, :], (8, W)), jnp.broadcast_to(xi[7:8, :], (8, W))

        zero = jnp.zeros((8, W), F32)
        lax.fori_loop(0, L // 8, step, (zero, zero), unroll=2)
        y_ref[...] = _dot(xr_ref[...], wcr_ref[...]) + _dot(xi_ref[...], wci_ref[...])

    ucol = pl.BlockSpec((L, cu), lambda t: (0, t))
    vec = pl.BlockSpec((1, W), lambda t: (0, t))
    col = pl.BlockSpec((L, W), lambda t: (0, t))
    wb = pl.BlockSpec((None, cu, W), lambda t: (t, 0, 0))
    wc = pl.BlockSpec((None, W, cu), lambda t: (t, 0, 0))
    return pl.pallas_call(body, grid=(parts,), in_specs=[ucol, vec, vec, wb, wb, wc, wc], out_specs=[ucol, col, col],
                          out_shape=[jax.ShapeDtypeStruct((L, parts * cu), F32)]
                          + [jax.ShapeDtypeStruct((L, parts * W), F32)] * 2,
                          scratch_shapes=[pltpu.VMEM((L, W), F32)] * 2,
                          compiler_params=_params(dimension_semantics=("parallel",)), name=name)(
        proj, a_re, a_im, wb_re, wb_im, wc_re, wc_im)


def s5_core_bwd(dy, du_d, proj, xs_re, xs_im, a_re, a_im, wb_re, wb_im, wc_re, wc_im, *, name):
    L = proj.shape[0]
    parts, cu, W = wb_re.shape

    def body(dy_ref, dud_ref, u_ref, xr_ref, xi_ref, ar_ref, ai_ref, wbr_ref, wbi_ref, wcr_ref, wci_ref,
             du_ref, dwbr_ref, dwbi_ref, dwcr_ref, dwci_ref, dar_ref, dai_ref, lr_ref, li_ref):
        dy = dy_ref[...]
        lr_ref[...] = _dot_nt(dy, wcr_ref[...])
        li_ref[...] = _dot_nt(dy, wci_ref[...])
        dwcr_ref[...] = _dot_tn(xr_ref[...], dy)
        dwci_ref[...] = _dot_tn(xi_ref[...], dy)
        ar, ai = ar_ref[...], -ai_ref[...]
        steps, (tr, ti) = _powers(ar, ai)
        tr = jnp.concatenate([tr[j:j + 1, :] for j in range(7, -1, -1)], axis=0)
        ti = jnp.concatenate([ti[j:j + 1, :] for j in range(7, -1, -1)], axis=0)
        row8 = lax.broadcasted_iota(jnp.int32, (8, W), 0)
        nblk = L // 8

        def shift(y, s):
            return jnp.where(row8 < 8 - s, pltpu.roll(y, 8 - s, 0), 0.0)

        def step(s, carry):
            cr, ci = carry
            base = pl.multiple_of((nblk - 1 - s) * 8, 8)
            yr, yi = _block_scan(lr_ref[pl.ds(base, 8), :], li_ref[pl.ds(base, 8), :], steps, shift)
            lr = yr + tr * cr - ti * ci
            li = yi + tr * ci + ti * cr
            lr_ref[pl.ds(base, 8), :] = lr
            li_ref[pl.ds(base, 8), :] = li
            return jnp.broadcast_to(lr[0:1, :], (8, W)), jnp.broadcast_to(li[0:1, :], (8, W))

        zero = jnp.zeros((8, W), F32)
        lax.fori_loop(0, nblk, step, (zero, zero), unroll=2)
        row = lax.broadcasted_iota(jnp.int32, (L, W), 0)
        xpr = jnp.where(row >= 1, pltpu.roll(xr_ref[...], 1, 0), 0.0)
        xpi = jnp.where(row >= 1, pltpu.roll(xi_ref[...], 1, 0), 0.0)
        lr, li = lr_ref[...], li_ref[...]
        dar_ref[...] = jnp.sum(lr * xpr + li * xpi, axis=0, keepdims=True)
        dai_ref[...] = jnp.sum(li * xpr - lr * xpi, axis=0, keepdims=True)
        u = u_ref[...]
        dwbr_ref[...] = _dot_tn(u, lr)
        dwbi_ref[...] = _dot_tn(u, li)
        du_ref[...] = (dud_ref[...] + _dot_nt(lr, wbr_ref[...]) + _dot_nt(li, wbi_ref[...])).astype(du_ref.dtype)

    ucol = pl.BlockSpec((L, cu), lambda t: (0, t))
    vec = pl.BlockSpec((1, W), lambda t: (0, t))
    col = pl.BlockSpec((L, W), lambda t: (0, t))
    wb = pl.BlockSpec((None, cu, W), lambda t: (t, 0, 0))
    wc = pl.BlockSpec((None, W, cu), lambda t: (t, 0, 0))
    return pl.pallas_call(
        body, grid=(parts,), in_specs=[ucol, ucol, ucol, col, col, vec, vec, wb, wb, wc, wc],
        out_specs=[ucol, wb, wb, wc, wc, vec, vec],
        out_shape=[jax.ShapeDtypeStruct((L, parts * cu), BF16)] + [jax.ShapeDtypeStruct((parts, cu, W), F32)] * 2
        + [jax.ShapeDtypeStruct((parts, W, cu), F32)] * 2 + [jax.ShapeDtypeStruct((1, parts * W), F32)] * 2,
        scratch_shapes=[pltpu.VMEM((L, W), F32)] * 2,
        compiler_params=_params(dimension_semantics=("parallel",)), name=name,
    )(dy, du_d, proj, xs_re, xs_im, a_re, a_im, wb_re, wb_im, wc_re, wc_im)


def _gelu(y):
    c = math.sqrt(2.0 / math.pi)
    t = jnp.tanh(c * (y + 0.044715 * y * y * y))
    return 0.5 * y * (1.0 + t), t


def s5_out_fwd(y0, proj, dvec, glu_w, glu_b, *, name):
    L, C = y0.shape
    tr = _pick(L, (256, 128))

    def body(y_ref, u_ref, d_ref, w_ref, b_ref, o_ref):
        z, _ = _gelu(y_ref[...] + d_ref[...] * u_ref[...])
        zg = _dot(z, w_ref[...]) + b_ref[...]
        o_ref[...] = (z * _sigmoid(zg)).astype(o_ref.dtype)

    row = pl.BlockSpec((tr, C), lambda i: (i, 0))
    vec = pl.BlockSpec((1, C), lambda i: (0, 0))
    wsp = pl.BlockSpec((C, C), lambda i: (0, 0))
    return pl.pallas_call(body, grid=(L // tr,), in_specs=[row, row, vec, wsp, vec], out_specs=row,
                          out_shape=jax.ShapeDtypeStruct((L, C), BF16), name=name)(
        y0, proj, dvec, glu_w, glu_b)


def s5_out_bwd(y0, proj, dvec, glu_w, glu_b, dcat, *, name):
    L, C = y0.shape
    tr = _pick(L, (256, 128))

    def body(y_ref, u_ref, d_ref, w_ref, b_ref, do_ref, dy_ref, dud_ref, z_ref, dzg_ref, db_ref, dd_ref):
        u = u_ref[...]
        y = y_ref[...] + d_ref[...] * u
        z, t = _gelu(y)
        zg = _dot(z, w_ref[...]) + b_ref[...]
        s = _sigmoid(zg)
        do = do_ref[...]
        dzg = do * z * s * (1.0 - s)
        dz = do * s + _dot_nt(dzg, w_ref[...])
        c = math.sqrt(2.0 / math.pi)
        dgelu = 0.5 * (1.0 + t) + 0.5 * y * (1.0 - t * t) * c * (1.0 + 3.0 * 0.044715 * y * y)
        dy = dz * dgelu

        @pl.when(pl.program_id(0) == 0)
        def _():
            db_ref[...] = jnp.zeros_like(db_ref)
            dd_ref[...] = jnp.zeros_like(dd_ref)

        db_ref[...] += jnp.sum(dzg, axis=0, keepdims=True)
        dd_ref[...] += jnp.sum(dy * u, axis=0, keepdims=True)
        dy_ref[...] = dy
        dud_ref[...] = dy * d_ref[...]
        z_ref[...] = z.astype(BF16)
        dzg_ref[...] = dzg.astype(BF16)

    row = pl.BlockSpec((tr, C), lambda i: (i, 0))
    vec = pl.BlockSpec((1, C), lambda i: (0, 0))
    wsp = pl.BlockSpec((C, C), lambda i: (0, 0))
    return pl.pallas_call(body, grid=(L // tr,), in_specs=[row, row, vec, wsp, vec, row],
                          out_specs=[row, row, row, row, vec, vec],
                          out_shape=[jax.ShapeDtypeStruct((L, C), F32), jax.ShapeDtypeStruct((L, C), F32),
                                     jax.ShapeDtypeStruct((L, C), BF16), jax.ShapeDtypeStruct((L, C), BF16),
                                     jax.ShapeDtypeStruct((1, C), F32), jax.ShapeDtypeStruct((1, C), F32)],
                          compiler_params=_params(dimension_semantics=("arbitrary",)), name=name)(
        y0, proj, dvec, glu_w, glu_b, dcat)


def _dot_tri(tri, x, tri_left=True):
    t = tri.astype(BF16)
    x1 = x.astype(BF16)
    r1 = x - x1.astype(F32)
    x2 = r1.astype(BF16)
    x3 = (r1 - x2.astype(F32)).astype(BF16)
    dot = (lambda p: jnp.dot(t, p, preferred_element_type=F32)) if tri_left else (
        lambda p: jnp.dot(p, t, preferred_element_type=F32))
    return dot(x1) + dot(x2) + dot(x3)


def _hg_gates(xq, xf, lb, tri):
    C = xq.shape[0]
    sq = _sigmoid(xq)
    q = xq * sq
    sg = _sigmoid(xf)
    f = lb + (1.0 - lb) * sg
    kk = 1.0 - f
    b = _dot_tri(tri, jnp.log(f))
    bm = b[C // 2 - 1:C // 2, :]
    bl = b[C - 1:C, :]
    eb = jnp.exp(b)
    eqm, ekm, ekl = jnp.exp(b - bm), jnp.exp(bm - b), jnp.exp(bl - b)
    return dict(sq=sq, q=q, sg=sg, f=f, kk=kk, eb=eb, ebl=jnp.exp(bl), eqm=eqm, ekm=ekm, ekl=ekl,
                qb=q * eb, qt=q * eqm, kt=kk * ekm, kh=kk * ekl)


def _tri(C, lower):
    r = lax.broadcasted_iota(jnp.int32, (C, C), 0)
    c = lax.broadcasted_iota(jnp.int32, (C, C), 1)
    return (r >= c) if lower else (c >= r)


def hgrn_fwd(proj, lb, norm_g, *, name):
    L = proj.shape[0]
    C, H, K = HG_CHUNK, HG_HEADS, HG_DIM
    HK = H * K
    nc = L // C

    def body(q_ref, f_ref, i_ref, g_ref, lb_ref, ng_ref, o_ref, sall_ref, st_ref):
        @pl.when(pl.program_id(0) == 0)
        def _():
            st_ref[...] = jnp.zeros_like(st_ref)

        mask = _tri(C, True)
        sts = [st_ref[h] for h in range(H)]
        for s in range(S):
            rs = slice(s * C, (s + 1) * C)
            gt = _hg_gates(q_ref[rs, :], f_ref[rs, :], lb_ref[...], mask.astype(F32))
            v_all = i_ref[rs, :]
            outs = []
            for h in range(H):
                sl = slice(h * K, (h + 1) * K)
                v, st = v_all[:, sl], sts[h]
                sall_ref[s, h] = st
                att = jnp.where(mask, _dot_nt(gt["qt"][:, sl], gt["kt"][:, sl]), 0.0)
                o = _dot(att, v) + _dot_nt(gt["qb"][:, sl], st)
                sts[h] = st * gt["ebl"][:, sl] + _dot_tn(v, gt["kh"][:, sl])
                outs.append(o * lax.rsqrt(jnp.mean(o * o, axis=-1, keepdims=True) + NORM_EPS))
            xg = g_ref[rs, :]
            o_ref[rs, :] = (jnp.concatenate(outs, axis=1) * ng_ref[...] * (xg * _sigmoid(xg))).astype(o_ref.dtype)
        for h in range(H):
            st_ref[h] = sts[h]

    S = HG_STEP_CHUNKS

    def blk(cb):
        return pl.BlockSpec((S * C, HK), lambda i: (i, cb))

    vec = pl.BlockSpec((1, HK), lambda i: (0, 0))
    return pl.pallas_call(
        body, grid=(nc // S,), in_specs=[blk(1), blk(2), blk(3), blk(4), vec, vec],
        out_specs=[pl.BlockSpec((S * C, HK), lambda i: (i, 0)), pl.BlockSpec((S, H, K, K), lambda i: (i, 0, 0, 0))],
        out_shape=[jax.ShapeDtypeStruct((L, HK), BF16), jax.ShapeDtypeStruct((nc, H, K, K), F32)],
        scratch_shapes=[pltpu.VMEM((H, K, K), F32)],
        compiler_params=_params(dimension_semantics=("arbitrary",)), name=name,
    )(proj, proj, proj, proj, lb, norm_g)


def hgrn_bwd(proj, lb, norm_g, sall, dcat, *, name):
    L = proj.shape[0]
    C, H, K = HG_CHUNK, HG_HEADS, HG_DIM
    HK = H * K
    nc = L // C

    def body(q_ref, f_ref, i_ref, g_ref, lb_ref, ng_ref, sall_ref, do_ref, dx_ref, dlb_ref, dng_ref, dst_ref):
        @pl.when(pl.program_id(0) == 0)
        def _():
            dst_ref[...] = jnp.zeros_like(dst_ref)
            dlb_ref[...] = jnp.zeros_like(dlb_ref)
            dng_ref[...] = jnp.zeros_like(dng_ref)

        mask = _tri(C, True)
        lb_all, ng = lb_ref[...], ng_ref[...]
        dsts = [dst_ref[h] for h in range(H)]
        for s in reversed(range(S)):
            rs = slice(s * C, (s + 1) * C)
            dsts = chunk_bwd(rs, s, dsts, mask, lb_all, ng, q_ref, f_ref, i_ref, g_ref, sall_ref, do_ref,
                             dx_ref, dlb_ref, dng_ref)
        for h in range(H):
            dst_ref[h] = dsts[h]

    def chunk_bwd(rs, s, dsts, mask, lb_all, ng, q_ref, f_ref, i_ref, g_ref, sall_ref, do_ref, dx_ref, dlb_ref, dng_ref):
        xq, xg, v_all = q_ref[rs, :], g_ref[rs, :], i_ref[rs, :]
        gt = _hg_gates(xq, f_ref[rs, :], lb_all, mask.astype(F32))
        sgg = _sigmoid(xg)
        d_ob = do_ref[rs, :]
        d_on = d_ob * (xg * sgg)
        doh = d_on * ng
        ohs, d_qts, d_qbs, d_kts, d_khs, dvs, d_bls, new_dsts = [], [], [], [], [], [], [], []
        for h in range(H):
            sl = slice(h * K, (h + 1) * K)
            v, st, dst = v_all[:, sl], sall_ref[s, h], dsts[h]
            qt, kt, kh, qb = gt["qt"][:, sl], gt["kt"][:, sl], gt["kh"][:, sl], gt["qb"][:, sl]
            att = jnp.where(mask, _dot_nt(qt, kt), 0.0)
            o = _dot(att, v) + _dot_nt(qb, st)
            r = lax.rsqrt(jnp.mean(o * o, axis=-1, keepdims=True) + NORM_EPS)
            oh = o * r
            do = r * (doh[:, sl] - oh * jnp.mean(doh[:, sl] * oh, axis=-1, keepdims=True))
            datt = jnp.where(mask, _dot_nt(do, v), 0.0)
            dvs.append(_dot_tn(att, do) + _dot_nt(kh, dst))
            d_qbs.append(_dot_x3(do, st))
            d_qts.append(_dot_x3(datt, kt))
            d_kts.append(_dot_x3(datt, qt, ((0,), (0,))))
            d_kh = _dot_x3(v, dst)
            d_khs.append(d_kh)
            d_bls.append(jnp.sum(dst * st, axis=0, keepdims=True) * gt["ebl"][:, sl]
                         + jnp.sum(d_kh * kh, axis=0, keepdims=True))
            new_dsts.append(dst * gt["ebl"][:, sl] + _dot_tn(do, qb))
            ohs.append(oh)
        oh, d_qt, d_qb, d_kt, d_kh, dv, d_bl = (jnp.concatenate(p, axis=1) for p in
                                                (ohs, d_qts, d_qbs, d_kts, d_khs, dvs, d_bls))
        dxg = d_ob * (oh * ng) * (sgg * (1.0 + xg * (1.0 - sgg)))
        dng_ref[...] += jnp.sum(d_on * oh, axis=0, keepdims=True)
        dq = d_qt * gt["eqm"] + d_qb * gt["eb"]
        db = d_qt * gt["qt"] + d_qb * gt["qb"] - d_kt * gt["kt"] - d_kh * gt["kh"]
        rowi = lax.broadcasted_iota(jnp.int32, (C, HK), 0)
        db = db + jnp.where(rowi == C - 1, d_bl, 0.0)
        dkk = d_kt * gt["ekm"] + d_kh * gt["ekl"]
        dlg = _dot_tri(_tri(C, False).astype(F32), db)
        df = dlg / gt["f"] - dkk
        sg, sq = gt["sg"], gt["sq"]
        dlb_ref[...] += jnp.sum(df * (1.0 - sg), axis=0, keepdims=True)
        dx_ref[rs, 0:HK] = (dq * (sq * (1.0 + xq * (1.0 - sq)))).astype(dx_ref.dtype)
        dx_ref[rs, HK:2 * HK] = (df * (1.0 - lb_all) * sg * (1.0 - sg)).astype(dx_ref.dtype)
        dx_ref[rs, 2 * HK:3 * HK] = dv.astype(dx_ref.dtype)
        dx_ref[rs, 3 * HK:4 * HK] = dxg.astype(dx_ref.dtype)
        return new_dsts

    S = HG_STEP_CHUNKS
    ns = nc // S

    def blk(cb):
        return pl.BlockSpec((S * C, HK), lambda i: (ns - 1 - i, cb))

    vec = pl.BlockSpec((1, HK), lambda i: (0, 0))
    return pl.pallas_call(
        body, grid=(ns,),
        in_specs=[blk(1), blk(2), blk(3), blk(4), vec, vec,
                  pl.BlockSpec((S, H, K, K), lambda i: (ns - 1 - i, 0, 0, 0)), blk(1)],
        out_specs=[pl.BlockSpec((S * C, 4 * HK), lambda i: (ns - 1 - i, 0)), vec, vec],
        out_shape=[jax.ShapeDtypeStruct((L, 4 * HK), BF16), jax.ShapeDtypeStruct((1, HK), F32),
                   jax.ShapeDtypeStruct((1, HK), F32)],
        scratch_shapes=[pltpu.VMEM((H, K, K), F32)],
        compiler_params=_params(dimension_semantics=("arbitrary",)), name=name,
    )(proj, proj, proj, proj, lb, norm_g, sall, dcat)


def _shift_down(x, k, row):
    return jnp.where(row >= k, pltpu.roll(x, k, 0), 0.0)


def _shift_up(x, k, row):
    n = x.shape[0]
    return jnp.where(row < n - k, pltpu.roll(x, n - k, 0), 0.0)


def convgate_fwd(hu, conv_w, conv_b, *, name):
    L, C2 = hu.shape
    C = C2 // 2
    tc = _pick(C, (256, 128))
    nb = C // tc

    def body(a_ref, b_ref, wa_ref, wb_ref, ba_ref, bb_ref, o_ref):
        row = lax.broadcasted_iota(jnp.int32, (L, tc), 0)

        def conv(x, w, bias):
            return w[2:3, :] * x + w[1:2, :] * _shift_down(x, 1, row) + w[0:1, :] * _shift_down(x, 2, row) + bias

        ca = conv(a_ref[...], wa_ref[...], ba_ref[...])
        cb = conv(b_ref[...], wb_ref[...], bb_ref[...])
        o_ref[...] = (ca * _sigmoid(ca) * cb).astype(o_ref.dtype)

    def col(off, rows):
        return pl.BlockSpec((rows, tc), lambda j: (0, j + off))

    return pl.pallas_call(
        body, grid=(nb,), in_specs=[col(0, L), col(nb, L), col(0, 3), col(nb, 3), col(0, 1), col(nb, 1)],
        out_specs=col(0, L), out_shape=jax.ShapeDtypeStruct((L, C), BF16),
        compiler_params=_params(dimension_semantics=("parallel",)), name=name,
    )(hu, hu, conv_w, conv_w, conv_b, conv_b)


def convgate_bwd(hu, conv_w, conv_b, dact, *, name):
    L, C2 = hu.shape
    C = C2 // 2
    tc = _pick(C, (256, 128))
    nb = C // tc

    def body(a_ref, b_ref, wa_ref, wb_ref, ba_ref, bb_ref, d_ref, dxa_ref, dxb_ref, dwa_ref, dwb_ref, dba_ref, dbb_ref):
        row = lax.broadcasted_iota(jnp.int32, (L, tc), 0)

        def conv(x, w, bias):
            x1 = _shift_down(x, 1, row)
            x2 = _shift_down(x, 2, row)
            return w[2:3, :] * x + w[1:2, :] * x1 + w[0:1, :] * x2 + bias, x1, x2

        xa, xb = a_ref[...], b_ref[...]
        wa, wb = wa_ref[...], wb_ref[...]
        ca, xa1, xa2 = conv(xa, wa, ba_ref[...])
        cb, xb1, xb2 = conv(xb, wb, bb_ref[...])
        d = d_ref[...]
        sa = _sigmoid(ca)
        dca = d * cb * (sa * (1.0 + ca * (1.0 - sa)))
        dcb = d * (ca * sa)

        def back(dc, w, x, x1, x2, dx_ref, dw_ref, db_ref):
            dx = w[2:3, :] * dc + w[1:2, :] * _shift_up(dc, 1, row) + w[0:1, :] * _shift_up(dc, 2, row)
            dx_ref[...] = dx.astype(dx_ref.dtype)
            dw_ref[...] = jnp.concatenate([jnp.sum(dc * x2, axis=0, keepdims=True),
                                           jnp.sum(dc * x1, axis=0, keepdims=True),
                                           jnp.sum(dc * x, axis=0, keepdims=True)], axis=0)
            db_ref[...] = jnp.sum(dc, axis=0, keepdims=True)

        back(dca, wa, xa, xa1, xa2, dxa_ref, dwa_ref, dba_ref)
        back(dcb, wb, xb, xb1, xb2, dxb_ref, dwb_ref, dbb_ref)

    def col(off, rows):
        return pl.BlockSpec((rows, tc), lambda j: (0, j + off))

    outs = pl.pallas_call(
        body, grid=(nb,),
        in_specs=[col(0, L), col(nb, L), col(0, 3), col(nb, 3), col(0, 1), col(nb, 1), col(0, L)],
        out_specs=[col(0, L), col(0, L), col(0, 3), col(0, 3), col(0, 1), col(0, 1)],
        out_shape=[jax.ShapeDtypeStruct((L, C), BF16)] * 2 + [jax.ShapeDtypeStruct((3, C), F32)] * 2
        + [jax.ShapeDtypeStruct((1, C), F32)] * 2,
        compiler_params=_params(dimension_semantics=("parallel",)), name=name,
    )(hu, hu, conv_w, conv_w, conv_b, conv_b, dact)
    dxa, dxb, dwa, dwb, dba, dbb = outs
    return (jnp.concatenate([dxa, dxb], axis=1), jnp.concatenate([dwa, dwb], axis=1),
            jnp.concatenate([dba, dbb], axis=1))


def rope_tables(positions):
    half = ROT_DIM // 2
    inv_freq = ROPE_THETA ** (-jnp.arange(half, dtype=F32) * 2.0 / ROT_DIM)
    ang = positions.astype(F32)[:, None] * inv_freq
    cos, sin = jnp.cos(ang), jnp.sin(ang)
    L = positions.shape[0]
    one = jnp.ones((L, ATT_E - ROT_DIM), F32)
    zero = jnp.zeros((L, ATT_E - ROT_DIM), F32)
    zh = jnp.zeros((L, half), F32)
    tc = jnp.concatenate([cos, cos, one], axis=1)
    ts1 = jnp.concatenate([zh, sin, zero], axis=1)
    ts2 = jnp.concatenate([-sin, zh, zero], axis=1)
    return tuple(jnp.concatenate([t, t], axis=1) for t in (tc, ts1, ts2))


def qkv_rope(hn, w_t, tabs, *, name):
    L, D = hn.shape
    N = w_t.shape[0]
    W = 512
    tm = _pick(L, (1024, 512, 256, 128))
    nq = N // (3 * W)
    scale = ATT_E ** -0.5

    def body(a_ref, b_ref, c_ref, s1_ref, s2_ref, o_ref):
        j = pl.program_id(1)
        x = _dot_nt(a_ref[...], b_ref[...])
        c = jnp.concatenate([c_ref[...]] * 4, axis=1)
        s1 = jnp.concatenate([s1_ref[...]] * 4, axis=1)
        s2 = jnp.concatenate([s2_ref[...]] * 4, axis=1)
        rot = x * c + pltpu.roll(x, 8, 1) * s1 + pltpu.roll(x, W - 8, 1) * s2
        mult = jnp.where(j < nq, scale, 1.0)
        o_ref[...] = jnp.where(j < 2 * nq, rot * mult, x)

    tab = pl.BlockSpec((tm, 128), lambda i, j: (i, 0))
    return pl.pallas_call(body, grid=(L // tm, N // W),
                          in_specs=[pl.BlockSpec((tm, D), lambda i, j: (i, 0)), pl.BlockSpec((W, D), lambda i, j: (j, 0)),
                                    tab, tab, tab],
                          out_specs=pl.BlockSpec((tm, W), lambda i, j: (i, j)),
                          out_shape=jax.ShapeDtypeStruct((L, N), F32),
                          compiler_params=_params(dimension_semantics=("parallel", "parallel")), name=name)(
        hn, w_t, *tabs)


def rope_bwd(slabs, tabs, *, name):
    L, W = slabs[0].shape
    tr = _pick(L, (256, 128))
    nq = len(slabs) // 3
    scale = ATT_E ** -0.5

    def body(*refs):
        d_refs, (c_ref, s1_ref, s2_ref, o_ref) = refs[:3 * nq], refs[3 * nq:]
        c = jnp.concatenate([c_ref[...]] * 4, axis=1)
        s1 = jnp.concatenate([s1_ref[...]] * 4, axis=1)
        s2 = jnp.concatenate([s2_ref[...]] * 4, axis=1)
        for j, d_ref in enumerate(d_refs):
            dy = d_ref[...]
            if j < 2 * nq:
                dy = dy * c + pltpu.roll(dy * s1, W - 8, 1) + pltpu.roll(dy * s2, 8, 1)
            if j < nq:
                dy = dy * scale
            o_ref[:, j * W:(j + 1) * W] = dy.astype(o_ref.dtype)

    slab = pl.BlockSpec((tr, W), lambda i: (i, 0))
    tab = pl.BlockSpec((tr, 128), lambda i: (i, 0))
    return pl.pallas_call(body, grid=(L // tr,), in_specs=[slab] * (3 * nq) + [tab, tab, tab],
                          out_specs=pl.BlockSpec((tr, 3 * nq * W), lambda i: (i, 0)),
                          out_shape=jax.ShapeDtypeStruct((L, 3 * nq * W), BF16),
                          compiler_params=_params(dimension_semantics=("parallel",)), name=name)(*slabs, *tabs)


def _att_masks(has_prev):
    qi = lax.broadcasted_iota(jnp.int32, (ATT_BLOCK, ATT_BLOCK), 0)
    kj = lax.broadcasted_iota(jnp.int32, (ATT_BLOCK, ATT_BLOCK), 1)
    return qi >= kj, (kj >= qi) & has_prev


ATT_COLS = 128


def _att_rows(j, d, nb):
    B = ATT_BLOCK
    r, n = j // nb, j % nb
    start = r + d * B * n
    has_prev = n > 0
    pstart = jnp.where(has_prev, start - d * B, start)
    if d == 1:
        return pl.ds(pl.multiple_of(start, B), B), pl.ds(pl.multiple_of(pstart, B), B), has_prev
    return pl.ds(start, B, stride=d), pl.ds(pstart, B, stride=d), has_prev


def _qkv_specs(L, g):
    per = ATT_HPG * ATT_E // ATT_COLS
    third = len(ATT_DILATIONS) * per
    return [pl.BlockSpec((L, ATT_COLS), lambda c, base=base: (0, base + c))
            for base in (g * per, third + g * per, 2 * third + g * per)]


def attn_fwd(qkv, g, d, *, name):
    L, W = qkv.shape[0], ATT_HPG * ATT_E
    B, E = ATT_BLOCK, ATT_E
    nblk = L // B
    nb = nblk // d

    def body(q_ref, k_ref, v_ref, o_ref, l_ref):
        def step(j, carry):
            cur, prv, has_prev = _att_rows(j, d, nb)
            mc, mp = _att_masks(has_prev)
            qb, kc, kp, vc, vp = q_ref[cur, :], k_ref[cur, :], k_ref[prv, :], v_ref[cur, :], v_ref[prv, :]
            outs, lses = [], []
            for h in range(ATT_COLS // E):
                sl = slice(h * E, (h + 1) * E)
                sc = jnp.where(mc, _dot_nt(qb[:, sl], kc[:, sl]), NEG_BIG)
                sp = jnp.where(mp, _dot_nt(qb[:, sl], kp[:, sl]), NEG_BIG)
                m = jnp.maximum(jnp.max(sc, axis=-1, keepdims=True), jnp.max(sp, axis=-1, keepdims=True))
                pc = jnp.exp(sc - m)
                pp = jnp.exp(sp - m)
                den = jnp.sum(pc, axis=-1, keepdims=True) + jnp.sum(pp, axis=-1, keepdims=True)
                outs.append((_dot(pc, vc[:, sl]) + _dot(pp, vp[:, sl])) / den)
                lses.append(jnp.broadcast_to(m + jnp.log(den), (B, E)))
            o_ref[cur, :] = jnp.concatenate(outs, axis=1)
            l_ref[cur, :] = jnp.concatenate(lses, axis=1)
            return carry

        lax.fori_loop(0, nblk, step, 0, unroll=4)

    col = pl.BlockSpec((L, ATT_COLS), lambda c: (0, c))
    return pl.pallas_call(body, grid=(W // ATT_COLS,), in_specs=_qkv_specs(L, g), out_specs=[col] * 2,
                          out_shape=[jax.ShapeDtypeStruct((L, W), F32)] * 2,
                          compiler_params=_params(dimension_semantics=("parallel",)), name=name)(qkv, qkv, qkv)


def attn_bwd(qkv, g, lse, do, dl, d, *, name):
    L, W = qkv.shape[0], ATT_HPG * ATT_E
    B, E = ATT_BLOCK, ATT_E
    nblk = L // B
    nb = nblk // d

    def body(q_ref, k_ref, v_ref, l_ref, do_ref, dl_ref, dq_ref, dk_ref, dv_ref):
        dk_ref[...] = jnp.zeros_like(dk_ref)
        dv_ref[...] = jnp.zeros_like(dv_ref)

        def step(j, carry):
            cur, prv, has_prev = _att_rows(j, d, nb)
            mc, mp = _att_masks(has_prev)
            qb, kc, kp, vc, vp = q_ref[cur, :], k_ref[cur, :], k_ref[prv, :], v_ref[cur, :], v_ref[prv, :]
            lb, dob, dlb = l_ref[cur, :], do_ref[cur, :], dl_ref[cur, :]
            dqs, dkc, dkp, dvc, dvp = [], [], [], [], []
            for h in range(ATT_COLS // E):
                sl = slice(h * E, (h + 1) * E)
                qh, doh = qb[:, sl], dob[:, sl]
                lse_h, dl_h = lb[:, h * E:h * E + 1], dlb[:, h * E:h * E + 1]
                pc = jnp.where(mc, jnp.exp(_dot_nt(qh, kc[:, sl]) - lse_h), 0.0)
                pp = jnp.where(mp, jnp.exp(_dot_nt(qh, kp[:, sl]) - lse_h), 0.0)
                dsc = pc * (_dot_nt(doh, vc[:, sl]) - dl_h)
                dsp = pp * (_dot_nt(doh, vp[:, sl]) - dl_h)
                dqs.append(_dot(dsc, kc[:, sl]) + _dot(dsp, kp[:, sl]))
                dkc.append(_dot_tn(dsc, qh))
                dkp.append(_dot_tn(dsp, qh))
                dvc.append(_dot_tn(pc, doh))
                dvp.append(_dot_tn(pp, doh))
            dq_ref[cur, :] = jnp.concatenate(dqs, axis=1)
            dk_ref[cur, :] = dk_ref[cur, :] + jnp.concatenate(dkc, axis=1)
            dv_ref[cur, :] = dv_ref[cur, :] + jnp.concatenate(dvc, axis=1)
            dk_ref[prv, :] = dk_ref[prv, :] + jnp.concatenate(dkp, axis=1)
            dv_ref[prv, :] = dv_ref[prv, :] + jnp.concatenate(dvp, axis=1)
            return carry

        lax.fori_loop(0, nblk, step, 0, unroll=4)

    col = pl.BlockSpec((L, ATT_COLS), lambda c: (0, c))
    return pl.pallas_call(body, grid=(W // ATT_COLS,), in_specs=_qkv_specs(L, g) + [col] * 3, out_specs=[col] * 3,
                          out_shape=[jax.ShapeDtypeStruct((L, W), F32)] * 3,
                          compiler_params=_params(dimension_semantics=("parallel",)), name=name)(
        qkv, qkv, qkv, lse, do, dl)


def _merge_alpha(l_refs):
    ls = [r[...] for r in l_refs]
    m = jnp.maximum(jnp.maximum(ls[0], ls[1]), ls[2])
    es = [jnp.exp(l - m) for l in ls]
    den = es[0] + es[1] + es[2]
    return [e / den for e in es]


def merge_fwd(os_, ls_, *, name):
    L, W = os_[0].shape
    tr = _pick(L, (256, 128))

    def body(o0, o1, o2, l0, l1, l2, out_ref):
        al = _merge_alpha((l0, l1, l2))
        out_ref[...] = (al[0] * o0[...] + al[1] * o1[...] + al[2] * o2[...]).astype(out_ref.dtype)

    row = pl.BlockSpec((tr, W), lambda i: (i, 0))
    return pl.pallas_call(body, grid=(L // tr,), in_specs=[row] * 6, out_specs=row,
                          out_shape=jax.ShapeDtypeStruct((L, W), BF16), name=name)(*os_, *ls_)


def merge_bwd(os_, ls_, do, *, name):
    L, W = do.shape
    tr = _pick(L, (256, 128))

    def body(o0, o1, o2, l0, l1, l2, do_ref, d0, d1, d2, e0, e1, e2):
        al = _merge_alpha((l0, l1, l2))
        dov = do_ref[...]
        r = lax.broadcasted_iota(jnp.int32, (W, W), 0) // ATT_E
        c = lax.broadcasted_iota(jnp.int32, (W, W), 1) // ATT_E
        ones_blk = (r == c).astype(F32)
        t = jnp.zeros_like(dov)
        for a, o in zip(al, (o0, o1, o2)):
            t = t + a * _dot_tri(ones_blk, dov * o[...], tri_left=False)
        for a, d_ref, e_ref in zip(al, (d0, d1, d2), (e0, e1, e2)):
            d_ref[...] = a * dov
            e_ref[...] = a * t

    row = pl.BlockSpec((tr, W), lambda i: (i, 0))
    return pl.pallas_call(body, grid=(L // tr,), in_specs=[row] * 7, out_specs=[row] * 6,
                          out_shape=[jax.ShapeDtypeStruct((L, W), F32)] * 6, name=name)(*os_, *ls_, do)


def _me_and_peers():
    x, y, c = lax.axis_index("x"), lax.axis_index("y"), lax.axis_index("c")
    peers = []
    for k in range(1, N_DEV):
        px = 1 - x if k & 4 else x
        py = 1 - y if k & 2 else y
        pc = 1 - c if k & 1 else c
        peers.append((px, py, pc))
    return (x, y, c), peers


def _index(dev):
    return 4 * dev[0] + 2 * dev[1] + dev[2]


def _hbm(a):
    return pltpu.with_memory_space_constraint(a, pltpu.HBM)


HBM_SPEC = pl.BlockSpec(memory_space=pltpu.HBM)
SEM_SPEC = pl.BlockSpec(memory_space=pltpu.SEMAPHORE)
DATAFLOW = pltpu.SideEffectType.DATAFLOW_SIDE_EFFECTING


def _remote(src_ref, land_ref, slotted, me, peer, src_is_mine, send_sem, recv_sem, k):
    sender, receiver = (me, peer) if src_is_mine else (peer, me)
    src = src_ref.at[_index(receiver)] if slotted else src_ref
    return pltpu.make_async_remote_copy(src_ref=src, dst_ref=land_ref.at[_index(sender)], send_sem=send_sem.at[k],
                                        recv_sem=recv_sem.at[k], device_id=peer, device_id_type=MESH_ID)


SIBLING = 0
SAME_CORE = (1, 3, 5)
OTHER_CORE = (2, 4, 6)


def copies_start(arrays, mode, *, name):
    n = len(arrays)
    slotted = mode == "exchange"
    lands = [lax.empty(a.shape if slotted else (N_DEV,) + a.shape, a.dtype) for a in arrays]
    targets = (SIBLING,) + SAME_CORE if mode == "gather2" else tuple(range(N_DEV - 1))

    def body(*refs):
        x_refs, land_refs = refs[:n], refs[n:2 * n]
        send, recv = refs[2 * n:3 * n], refs[3 * n:4 * n]
        token = refs[-1]
        me, peers = _me_and_peers()
        for w in range(n):
            for k in targets:
                _remote(x_refs[w], land_refs[w], slotted, me, peers[k], True, send[w], recv[w], k).start()
            if not slotted:
                pltpu.make_async_copy(x_refs[w], land_refs[w].at[_index(me)], recv[w].at[N_DEV - 1]).start()
        token[...] = jnp.zeros_like(token)

    sem = pltpu.SemaphoreType.DMA((N_DEV,))
    out_shape = ([sem] * (2 * n) + [pltpu.HBM(a.shape, a.dtype) for a in arrays]
                 + [pltpu.HBM(l.shape, l.dtype) for l in lands] + [jax.ShapeDtypeStruct((8, 128), F32)])
    outs = pl.pallas_call(
        body, name=name, out_shape=out_shape, in_specs=[HBM_SPEC] * (2 * n),
        out_specs=[SEM_SPEC] * (2 * n) + [HBM_SPEC] * (2 * n) + [pl.BlockSpec(memory_space=pltpu.VMEM)],
        input_output_aliases={i: 2 * n + i for i in range(2 * n)},
        compiler_params=pltpu.CompilerParams(has_side_effects=DATAFLOW),
    )(*[_hbm(a) for a in arrays], *[_hbm(l) for l in lands])
    handles = [(outs[w], outs[n + w], outs[2 * n + w], outs[3 * n + w]) for w in range(n)]
    return handles, outs[-1]


def _forward(land_ref, me, peers, j, fsend, frecv, mine):
    block = _index(peers[SAME_CORE[j]] if mine else peers[OTHER_CORE[j]])
    return pltpu.make_async_remote_copy(src_ref=land_ref.at[block], dst_ref=land_ref.at[block], send_sem=fsend.at[j],
                                        recv_sem=frecv.at[j], device_id=peers[SIBLING], device_id_type=MESH_ID)


def copies_forward(handles, after, *, name):
    n = len(handles)

    def body(*refs):
        land_refs, recv = refs[:n], refs[n:2 * n]
        fsend, frecv = refs[2 * n + 1:3 * n + 1], refs[3 * n + 1:4 * n + 1]
        token = refs[-1]
        me, peers = _me_and_peers()
        for w in range(n):
            for j, k in enumerate(SAME_CORE):
                block = land_refs[w].at[_index(peers[k])]
                pltpu.make_async_remote_copy(src_ref=block, dst_ref=block, send_sem=recv[w].at[N_DEV - 1],
                                             recv_sem=recv[w].at[k], device_id=peers[k], device_id_type=MESH_ID).wait_recv()
                _forward(land_refs[w], me, peers, j, fsend[w], frecv[w], True).start()
        token[...] = jnp.zeros_like(token)

    sem = pltpu.SemaphoreType.DMA((len(SAME_CORE),))
    lands = [h[3] for h in handles]
    outs = pl.pallas_call(
        body, name=name,
        out_shape=[sem] * (2 * n) + [pltpu.HBM(l.shape, l.dtype) for l in lands] + [jax.ShapeDtypeStruct((8, 128), F32)],
        in_specs=[HBM_SPEC] * n + [SEM_SPEC] * n + [pl.BlockSpec(memory_space=pl.ANY)],
        out_specs=[SEM_SPEC] * (2 * n) + [HBM_SPEC] * n + [pl.BlockSpec(memory_space=pltpu.VMEM)],
        input_output_aliases={w: 2 * n + w for w in range(n)},
        compiler_params=pltpu.CompilerParams(has_side_effects=DATAFLOW),
    )(*lands, *[h[1] for h in handles], after)
    new = [(h[0], h[1], h[2], outs[2 * n + w], outs[w], outs[n + w]) for w, h in enumerate(handles)]
    return new, outs[-1]


def copies_wait(handle, mode, after, *, name):
    slotted = mode == "exchange"
    two_level = mode == "gather2"
    send_sem, recv_sem, x_thru, land_thru = handle[:4]
    targets = (SIBLING,) + SAME_CORE if two_level else tuple(range(N_DEV - 1))
    arrivals = (SIBLING,) if two_level else targets

    def body(x_ref, land_ref, send_ref, recv_ref, *rest):
        me, peers = _me_and_peers()
        for k in targets:
            _remote(x_ref, land_ref, slotted, me, peers[k], True, send_ref, recv_ref, k).wait_send()
        for k in arrivals:
            _remote(x_ref, land_ref, slotted, me, peers[k], False, send_ref, recv_ref, k).wait_recv()
        if not slotted:
            pltpu.make_async_copy(x_ref, land_ref.at[_index(me)], recv_ref.at[N_DEV - 1]).wait()
        if two_level:
            fsend, frecv = rest[0], rest[1]
            for j in range(len(SAME_CORE)):
                _forward(land_ref, me, peers, j, fsend, frecv, True).wait_send()
                _forward(land_ref, me, peers, j, fsend, frecv, False).wait_recv()

    extra = list(handle[4:])
    return pl.pallas_call(
        body, name=name, out_shape=(pltpu.HBM(x_thru.shape, x_thru.dtype), pltpu.HBM(land_thru.shape, land_thru.dtype)),
        in_specs=[HBM_SPEC, HBM_SPEC, SEM_SPEC, SEM_SPEC] + [SEM_SPEC] * len(extra) + [pl.BlockSpec(memory_space=pl.ANY)],
        out_specs=(HBM_SPEC, HBM_SPEC), input_output_aliases={0: 0, 1: 1},
        compiler_params=pltpu.CompilerParams(has_side_effects=DATAFLOW),
    )(x_thru, land_thru, send_sem, recv_sem, *extra, after)


def cast_bf16(x, *, ncols=None, name):
    R = x.shape[0]
    C = ncols or x.shape[1]
    tr = _pick(R, (512, 352, 256, 128, 64))

    def body(x_ref, o_ref):
        o_ref[...] = x_ref[...].astype(BF16)

    row = pl.BlockSpec((tr, C), lambda i: (i, 0))
    return pl.pallas_call(body, grid=(R // tr,), in_specs=[row], out_specs=row,
                          out_shape=jax.ShapeDtypeStruct((R, C), BF16), name=name)(x)


def cast_bf16_layer(x3, layer, *, name):
    _, R, C = x3.shape
    tr = _pick(R, (512, 352, 256, 128, 64))

    def body(x_ref, o_ref):
        o_ref[...] = x_ref[...].astype(BF16)

    return pl.pallas_call(body, grid=(R // tr,), in_specs=[pl.BlockSpec((None, tr, C), lambda i: (layer, i, 0))],
                          out_specs=pl.BlockSpec((tr, C), lambda i: (i, 0)),
                          out_shape=jax.ShapeDtypeStruct((R, C), BF16), name=name)(x3)


BD_PARTS = 4


def _blockdiag_call(b, build, G, r, c, name):
    gp = G // BD_PARTS

    def body_build(b_ref, o_ref):
        o_ref[...] = jnp.zeros_like(o_ref)
        for g in range(G):
            o_ref[g // gp, (g % gp) * r:(g % gp + 1) * r, (g % gp) * c:(g % gp + 1) * c] = b_ref[g]

    def body_extract(d_ref, o_ref):
        for g in range(G):
            o_ref[g] = d_ref[g // gp, (g % gp) * r:(g % gp + 1) * r, (g % gp) * c:(g % gp + 1) * c]

    out = jax.ShapeDtypeStruct((BD_PARTS, gp * r, gp * c) if build else (G, r, c), F32)
    return pl.pallas_call(body_build if build else body_extract, out_shape=out, name=name)(b)


def make_blockdiag(G, r, c, name):
    @jax.custom_vjp
    def blockdiag(b):
        return _blockdiag_call(b, True, G, r, c, name + "_build")

    def fwd(b):
        return blockdiag(b), None

    def bwd(_, g):
        return (_blockdiag_call(g, False, G, r, c, name + "_extract"),)

    blockdiag.defvjp(fwd, bwd)
    return blockdiag


def _my_index():
    return 4 * lax.axis_index("x") + 2 * lax.axis_index("y") + lax.axis_index("c")


def cols_from_shards(g, *, name):
    _, K, n = g.shape
    tk = _pick(K, (256, 128))

    def body(g_ref, o_ref):
        for i in range(N_DEV):
            o_ref[:, i * n:(i + 1) * n] = g_ref[i]

    return pl.pallas_call(body, grid=(K // tk,), in_specs=[pl.BlockSpec((N_DEV, tk, n), lambda i: (0, i, 0))],
                          out_specs=pl.BlockSpec((tk, N_DEV * n), lambda i: (i, 0)),
                          out_shape=jax.ShapeDtypeStruct((K, N_DEV * n), g.dtype), name=name)(g)


def shards_from_cols(w, *, name):
    K, N = w.shape
    n = N // N_DEV
    tk = _pick(K, (256, 128))

    def body(w_ref, o_ref):
        for i in range(N_DEV):
            o_ref[i] = w_ref[:, i * n:(i + 1) * n].astype(o_ref.dtype)

    return pl.pallas_call(body, grid=(K // tk,), in_specs=[pl.BlockSpec((tk, N), lambda i: (i, 0))],
                          out_specs=pl.BlockSpec((N_DEV, tk, n), lambda i: (0, i, 0)),
                          out_shape=jax.ShapeDtypeStruct((N_DEV, K, n), BF16), name=name)(w)


def _adamw(w, g, m, v):
    m = ADAM_B1 * m + (1.0 - ADAM_B1) * g
    v = ADAM_B2 * v + (1.0 - ADAM_B2) * (g * g)
    m_hat = m / (1.0 - ADAM_B1 ** ADAM_STEP)
    v_hat = v / (1.0 - ADAM_B2 ** ADAM_STEP)
    delta = -ADAM_LR * (m_hat / (jnp.sqrt(v_hat) + ADAM_EPS) + ADAM_WD * w)
    return delta, m, v


def reduce_adamw(recv, own, own_slotted, me, w, m, v, *, layer=0, n_layers=1, into=None, name):
    _, R, C = recv.shape
    tr = _pick(R, (352, 320, 288, 256, 128, 64, 32, 16, 8))
    off = layer * (R // tr)

    def body(me_ref, r_ref, own_ref, w_ref, m_ref, v_ref, *rest):
        g_ref, d_ref, nm_ref, nv_ref = rest[-4:]
        mine = me_ref[0]
        g = None
        for i in range(N_DEV):
            part = jnp.where(mine == i, own_ref[...], r_ref[i]).astype(F32)
            g = part if g is None else g + part
        delta, nm, nv = _adamw(w_ref[...], g, m_ref[...], v_ref[...])
        g_ref[...] = g
        d_ref[...] = delta
        nm_ref[...] = nm
        nv_ref[...] = nv

    row = pl.BlockSpec((tr, C), lambda i, me_ref: (i + off, 0))
    own_spec = (pl.BlockSpec((None, tr, C), lambda i, me_ref: (me_ref[0], i, 0)) if own_slotted
                else pl.BlockSpec((tr, C), lambda i, me_ref: (i, 0)))
    rest = [] if into is None else list(into)
    grid_spec = pltpu.PrefetchScalarGridSpec(
        num_scalar_prefetch=1, grid=(R // tr,),
        in_specs=[pl.BlockSpec((N_DEV, tr, C), lambda i, me_ref: (0, i, 0)), own_spec, row, row, row]
        + [pl.BlockSpec(memory_space=pl.ANY)] * len(rest),
        out_specs=[row] * 4)
    return pl.pallas_call(body, grid_spec=grid_spec, out_shape=[jax.ShapeDtypeStruct((n_layers * R, C), F32)] * 4,
                          input_output_aliases={6 + k: k for k in range(len(rest))},
                          compiler_params=_params(dimension_semantics=("parallel",)), name=name)(
        me.reshape(1).astype(jnp.int32), recv, own, w, m, v, *rest)


def _s5_prepare(A_re, A_im, log_dt, B_re, B_im, C_re, C_im):
    G, P, Cg = S5_GROUPS, S5_STATE, S5_GROUP
    dt = jnp.exp(log_dt)[:, None]
    mag = jnp.exp(A_re * dt)
    ab_re = mag * jnp.cos(A_im * dt)
    ab_im = mag * jnp.sin(A_im * dt)
    den = A_re * A_re + A_im * A_im
    nr, ni = ab_re - 1.0, ab_im
    c_re = (nr * A_re + ni * A_im) / den
    c_im = (ni * A_re - nr * A_im) / den
    Bb_re = c_re[..., None] * B_re - c_im[..., None] * B_im
    Bb_im = c_re[..., None] * B_im + c_im[..., None] * B_re
    def dense_in(b, name):
        return make_blockdiag(G, Cg, P, name)(b.transpose(0, 2, 1))

    def dense_out(c, name):
        return make_blockdiag(G, P, Cg, name)(c.transpose(0, 2, 1))

    return (ab_re.reshape(1, G * P), ab_im.reshape(1, G * P), dense_in(Bb_re, "s5_wb_re"), dense_in(Bb_im, "s5_wb_im"),
            dense_out(C_re, "s5_wc_re"), dense_out(-C_im, "s5_wc_im"))


def _lower_bound(gamma):
    return jnp.cumsum(jax.nn.softmax(gamma, axis=0), axis=0)[0:1]


def _ffn_fwd(h, g_norm, get_w_in, conv_w, conv_b, get_w_out, tag):
    hn = rms_fwd(h, g_norm, name=tag + "_rms")
    w_in = get_w_in(hn)
    hu = mm(hn, w_in, tb=True, name=tag + "_in")
    act = convgate_fwd(hu, conv_w, conv_b, name=tag + "_gate")
    w_out = get_w_out(act)
    h_out = mm(act, w_out, res=h, name=tag + "_out")
    return h_out, (hn, hu, act), w_in, w_out


def _ffn_bwd(h, g_norm, w_in, conv_w, conv_b, w_out, saved, dh, tag, send_dw_in, send_dw_out):
    hn, hu, act = saved
    sent = send_dw_out(mm(act, dh, ta=True, out_dtype=BF16, name=tag + "_dwout"))
    dact = mm(dh, w_out, tb=True, dep=sent, name=tag + "_dact")
    dhu, dconv_w, dconv_b = convgate_bwd(hu, conv_w, conv_b, dact, name=tag + "_dgate")
    sent = send_dw_in(mm(dhu, hn, ta=True, out_dtype=BF16, name=tag + "_dwin"))
    dh_in, dg = mm_drms(dhu, w_in, h, g_norm, dh, dep=sent, name=tag + "_dhn")
    return dh_in, dg, dconv_w, dconv_b


def kernel(x, positions, norm_mix, norm_ffn, norm_final, mix_w_in, mix_w_out, s5_A_re, s5_A_im, s5_log_dt, s5_B_re, s5_B_im, s5_C_re, s5_C_im, s5_D, s5_glu_w, s5_glu_b, hgrn_gamma, hgrn_norm, att_w_qkv, att_w_o, ffn_w_in, ffn_conv_w, ffn_conv_b, ffn_w_out, loss_target, m_norm_mix, m_norm_ffn, m_norm_final, m_mix_w_in, m_mix_w_out, m_s5_A_re, m_s5_A_im, m_s5_log_dt, m_s5_B_re, m_s5_B_im, m_s5_C_re, m_s5_C_im, m_s5_D, m_s5_glu_w, m_s5_glu_b, m_hgrn_gamma, m_hgrn_norm, m_att_w_qkv, m_att_w_o, m_ffn_w_in, m_ffn_conv_w, m_ffn_conv_b, m_ffn_w_out, v_norm_mix, v_norm_ffn, v_norm_final, v_mix_w_in, v_mix_w_out, v_s5_A_re, v_s5_A_im, v_s5_log_dt, v_s5_B_re, v_s5_B_im, v_s5_C_re, v_s5_C_im, v_s5_D, v_s5_glu_w, v_s5_glu_b, v_hgrn_gamma, v_hgrn_norm, v_att_w_qkv, v_att_w_o, v_ffn_w_in, v_ffn_conv_w, v_ffn_conv_b, v_ffn_w_out):
    W = dict(norm_mix=norm_mix, norm_ffn=norm_ffn, norm_final=norm_final, mix_w_in=mix_w_in, mix_w_out=mix_w_out,
             s5_A_re=s5_A_re, s5_A_im=s5_A_im, s5_log_dt=s5_log_dt, s5_B_re=s5_B_re, s5_B_im=s5_B_im,
             s5_C_re=s5_C_re, s5_C_im=s5_C_im, s5_D=s5_D, s5_glu_w=s5_glu_w, s5_glu_b=s5_glu_b,
             hgrn_gamma=hgrn_gamma, hgrn_norm=hgrn_norm, att_w_qkv=att_w_qkv, att_w_o=att_w_o, ffn_w_in=ffn_w_in,
             ffn_conv_w=ffn_conv_w, ffn_conv_b=ffn_conv_b, ffn_w_out=ffn_w_out)
    M = dict(norm_mix=m_norm_mix, norm_ffn=m_norm_ffn, norm_final=m_norm_final, mix_w_in=m_mix_w_in,
             mix_w_out=m_mix_w_out, s5_A_re=m_s5_A_re, s5_A_im=m_s5_A_im, s5_log_dt=m_s5_log_dt, s5_B_re=m_s5_B_re,
             s5_B_im=m_s5_B_im, s5_C_re=m_s5_C_re, s5_C_im=m_s5_C_im, s5_D=m_s5_D, s5_glu_w=m_s5_glu_w,
             s5_glu_b=m_s5_glu_b, hgrn_gamma=m_hgrn_gamma, hgrn_norm=m_hgrn_norm, att_w_qkv=m_att_w_qkv,
             att_w_o=m_att_w_o, ffn_w_in=m_ffn_w_in, ffn_conv_w=m_ffn_conv_w, ffn_conv_b=m_ffn_conv_b,
             ffn_w_out=m_ffn_w_out)
    V = dict(norm_mix=v_norm_mix, norm_ffn=v_norm_ffn, norm_final=v_norm_final, mix_w_in=v_mix_w_in,
             mix_w_out=v_mix_w_out, s5_A_re=v_s5_A_re, s5_A_im=v_s5_A_im, s5_log_dt=v_s5_log_dt, s5_B_re=v_s5_B_re,
             s5_B_im=v_s5_B_im, s5_C_re=v_s5_C_re, s5_C_im=v_s5_C_im, s5_D=v_s5_D, s5_glu_w=v_s5_glu_w,
             s5_glu_b=v_s5_glu_b, hgrn_gamma=v_hgrn_gamma, hgrn_norm=v_hgrn_norm, att_w_qkv=v_att_w_qkv,
             att_w_o=v_att_w_o, ffn_w_in=v_ffn_w_in, ffn_conv_w=v_ffn_conv_w, ffn_conv_b=v_ffn_conv_b,
             ffn_w_out=v_ffn_w_out)
    return _step(x[0], positions[0], loss_target[0], W, M, V)


TRANSPOSED = ("mix_w_in", "att_w_qkv", "ffn_w_in")
SMALL = ("norm_mix", "norm_ffn", "norm_final", "s5_A_re", "s5_A_im", "s5_log_dt", "s5_B_re", "s5_B_im", "s5_C_re",
         "s5_C_im", "s5_D", "s5_glu_b", "hgrn_gamma", "hgrn_norm", "ffn_conv_b")
ORDER = ("norm_mix", "norm_ffn", "norm_final", "mix_w_in", "mix_w_out", "s5_A_re", "s5_A_im", "s5_log_dt", "s5_B_re",
         "s5_B_im", "s5_C_re", "s5_C_im", "s5_D", "s5_glu_w", "s5_glu_b", "hgrn_gamma", "hgrn_norm", "att_w_qkv",
         "att_w_o", "ffn_w_in", "ffn_conv_w", "ffn_conv_b", "ffn_w_out")
PACK_COLS = 1024


def _step(x, positions, target, W, M, V):
    L, D = x.shape
    me = 4 * lax.axis_index("x") + 2 * lax.axis_index("y") + lax.axis_index("c")
    n_cw = W["ffn_conv_w"].shape[-1]
    T = {n: tuple(jnp.swapaxes(d[n], -1, -2) for d in (W, M, V)) for n in TRANSPOSED}
    shards = {
        "mix_w_in": cast_bf16(T["mix_w_in"][0][0], name="mix_w_in_cast"),
        "conv_w": W["ffn_conv_w"].reshape(6, n_cw),
        "s5_glu_w": cast_bf16(W["s5_glu_w"][0], name="s5_glu_w_cast"),
        "mix_w_out": cast_bf16(W["mix_w_out"][0], name="mix_w_out_cast"),
        "ffn_w_in0": cast_bf16_layer(T["ffn_w_in"][0], 0, name="ffn_w_in0_cast"),
        "ffn_w_out0": cast_bf16_layer(W["ffn_w_out"], 0, name="ffn_w_out0_cast"),
        "att_w_qkv": cast_bf16(T["att_w_qkv"][0][0], name="att_w_qkv_cast"),
        "att_w_o": cast_bf16(W["att_w_o"][0], name="att_w_o_cast"),
        "ffn_w_in1": cast_bf16_layer(T["ffn_w_in"][0], 1, name="ffn_w_in1_cast"),
        "ffn_w_out1": cast_bf16_layer(W["ffn_w_out"], 1, name="ffn_w_out1_cast"),
    }
    gather_handles, token = copies_start(list(shards.values()), "gather2", name="gather_start")
    gather_handle = dict(zip(shards, gather_handles))

    def forward(keys, after, name):
        new, sent = copies_forward([gather_handle[k] for k in keys], after, name=name)
        gather_handle.update(zip(keys, new))
        return sent

    def gathered(key, after, cols):
        _, land = copies_wait(gather_handle[key], "gather2", after, name=key + "_gwait")
        return cols_from_shards(land, name=key + "_asm") if cols else land.reshape(-1, land.shape[-1])

    conv_b = W["ffn_conv_b"].reshape(2, 1, -1)

    s5_params = (W["s5_A_re"][0], W["s5_A_im"][0], W["s5_log_dt"][0], W["s5_B_re"][0], W["s5_B_im"][0],
                 W["s5_C_re"][0], W["s5_C_im"][0])
    (a_re, a_im, wb_re, wb_im, wc_re, wc_im), s5_prep_vjp = jax.vjp(_s5_prepare, *s5_params)
    dvec = W["s5_D"].reshape(1, S5_WIDTH)
    glu_b = W["s5_glu_b"].reshape(1, S5_WIDTH)
    lb, lb_vjp = jax.vjp(_lower_bound, W["hgrn_gamma"])
    hg_norm = W["hgrn_norm"].reshape(1, -1)
    tabs = rope_tables(positions)

    hn0 = rms_fwd(x, W["norm_mix"][0], dep=token, name="l0_rms")
    forward(["mix_w_in", "conv_w", "s5_glu_w"], hn0, "forward_a")
    w_mix_in = gathered("mix_w_in", hn0, False)
    proj = mm(hn0, w_mix_in, tb=True, name="l0_proj")
    y0, xs_re, xs_im = s5_core_fwd(proj, a_re, a_im, wb_re, wb_im, wc_re, wc_im, name="s5_core")
    w_glu = gathered("s5_glu_w", y0, False)
    oa = s5_out_fwd(y0, proj, dvec, w_glu, glu_b, name="s5_out")
    ob, hg_states = hgrn_fwd(proj, lb, hg_norm, name="hgrn_fwd")
    forward(["mix_w_out", "ffn_w_in0", "ffn_w_out0", "att_w_qkv"], ob, "forward_b")
    cat = jnp.concatenate([oa, ob], axis=1)
    w_mix_out = gathered("mix_w_out", cat, False)
    h1 = mm(cat, w_mix_out, res=x, name="l0_mix_out")
    _, cw_all = copies_wait(gather_handle["conv_w"], "gather2", h1, name="conv_w_gwait")
    conv_w = cw_all.transpose(1, 0, 2).reshape(2, 3, N_DEV * n_cw)
    w_ffn_in, w_ffn_out = [None, None], [None, None]
    h2, ffn0_saved, w_ffn_in[0], w_ffn_out[0] = _ffn_fwd(
        h1, W["norm_ffn"][0], lambda a: gathered("ffn_w_in0", a, False), conv_w[0], conv_b[0],
        lambda a: gathered("ffn_w_out0", a, False), "ffn0")

    forward(["att_w_o", "ffn_w_in1", "ffn_w_out1"], h2, "forward_c")
    hn2 = rms_fwd(h2, W["norm_mix"][1], name="l1_rms")
    w_qkv = gathered("att_w_qkv", hn2, False)
    qkv_r = qkv_rope(hn2, w_qkv, tabs, name="l1_qkv")
    att_o, att_l = [], []
    for g, d in enumerate(ATT_DILATIONS):
        o_g, l_g = attn_fwd(qkv_r, g, d, name=f"attn_fwd{g}")
        att_o.append(o_g)
        att_l.append(l_g)
    o_att = merge_fwd(att_o, att_l, name="merge_fwd")
    w_o = gathered("att_w_o", o_att, True)
    h3 = mm(o_att, w_o, res=h2, name="l1_mix_out")
    h4, ffn1_saved, w_ffn_in[1], w_ffn_out[1] = _ffn_fwd(
        h3, W["norm_ffn"][1], lambda a: gathered("ffn_w_in1", a, False), conv_w[1], conv_b[1],
        lambda a: gathered("ffn_w_out1", a, False), "ffn1")

    exchanges = {}

    def send_grad(key, g, cols):
        if cols:
            parts = shards_from_cols(g, name=key + "_split")
        else:
            parts = g.reshape(N_DEV, g.shape[0] // N_DEV, g.shape[1])
        (handle,), sent = copies_start([parts], "exchange", name=key + "_xstart")
        exchanges[key] = handle
        return sent

    loss, dh4, dg_final = final_loss(h4, W["norm_final"], target, name="final_loss")
    dh3, dg_ffn1, dcw1, dcb1 = _ffn_bwd(h3, W["norm_ffn"][1], w_ffn_in[1], conv_w[1], conv_b[1], w_ffn_out[1],
                                        ffn1_saved, dh4, "ffn1", lambda g: send_grad("ffn_w_in1", g, False),
                                        lambda g: send_grad("ffn_w_out1", g, False))
    sent = send_grad("att_w_o", mm(o_att, dh3, ta=True, name="l1_dwo"), True)
    d_oatt = mm(dh3, w_o, tb=True, dep=sent, name="l1_dmix")
    mb = merge_bwd(att_o, att_l, d_oatt, name="merge_bwd")
    d_slabs = [attn_bwd(qkv_r, g, att_l[g], mb[g], mb[3 + g], d, name=f"attn_bwd{g}")
               for g, d in enumerate(ATT_DILATIONS)]
    d_qkv = rope_bwd([s[0] for s in d_slabs] + [s[1] for s in d_slabs] + [s[2] for s in d_slabs], tabs,
                     name="rope_bwd")
    sent = send_grad("att_w_qkv", mm(d_qkv, hn2, ta=True, out_dtype=BF16, name="l1_dwqkv"), False)
    dh2, dg_mix1 = mm_drms(d_qkv, w_qkv, h2, W["norm_mix"][1], dh3, dep=sent, name="l1_dhn")

    dh1, dg_ffn0, dcw0, dcb0 = _ffn_bwd(h1, W["norm_ffn"][0], w_ffn_in[0], conv_w[0], conv_b[0], w_ffn_out[0],
                                        ffn0_saved, dh2, "ffn0", lambda g: send_grad("ffn_w_in0", g, False),
                                        lambda g: send_grad("ffn_w_out0", g, False))
    sent = send_grad("mix_w_out", mm(cat, dh1, ta=True, out_dtype=BF16, name="l0_dwout"), False)
    dcat = mm(dh1, w_mix_out, tb=True, dep=sent, name="l0_dcat")
    d_hg, dlb, dhg_norm = hgrn_bwd(proj, lb, hg_norm, hg_states, dcat, name="hgrn_bwd")
    dy, du_d, z_bf, dzg, dglu_b, dD = s5_out_bwd(y0, proj, dvec, w_glu, glu_b, dcat, name="s5_dout")
    sent_glu = send_grad("s5_glu_w", mm(z_bf, dzg, ta=True, out_dtype=BF16, name="s5_dglu"), False)
    du, dwb_re, dwb_im, dwc_re, dwc_im, da_re, da_im = s5_core_bwd(
        dy, du_d, proj, xs_re, xs_im, a_re, a_im, wb_re, wb_im, wc_re, wc_im, name="s5_dcore")
    s5_small = s5_prep_vjp((da_re, da_im, dwb_re, dwb_im, dwc_re, dwc_im))
    d_proj = jnp.concatenate([du, d_hg], axis=1)
    sent = send_grad("mix_w_in", mm(d_proj, hn0, ta=True, out_dtype=BF16, dep=sent_glu, name="l0_dwin"), False)
    grad_x, dg_mix0 = mm_drms(d_proj, w_mix_in, x, W["norm_mix"][0], dh1, dep=sent, name="l0_dhn")
    (d_gamma,) = lb_vjp(dlb)
    out = {}

    dA_re, dA_im, dlog_dt, dB_re, dB_im, dC_re, dC_im = s5_small
    small_g = dict(norm_mix=jnp.concatenate([dg_mix0, dg_mix1], axis=0), norm_ffn=jnp.concatenate([dg_ffn0, dg_ffn1], axis=0),
                   norm_final=dg_final, s5_A_re=dA_re, s5_A_im=dA_im, s5_log_dt=dlog_dt, s5_B_re=dB_re, s5_B_im=dB_im,
                   s5_C_re=dC_re, s5_C_im=dC_im, s5_D=dD, s5_glu_b=dglu_b, hgrn_gamma=d_gamma, hgrn_norm=dhg_norm,
                   ffn_conv_b=jnp.concatenate([dcb0, dcb1], axis=0))
    conv_w_g = jnp.stack([dcw0, dcw1], axis=0)
    sizes = [math.prod(W[n].shape) for n in SMALL]
    n_conv = conv_w_g.size
    total = sum(sizes) + n_conv + 1
    rows = -(-total // PACK_COLS)
    rows = -(-rows // 8) * 8
    pad = rows * PACK_COLS - total

    def pack(vals, conv_part, last):
        flat = [v.reshape(-1).astype(F32) for v in vals] + [conv_part.reshape(-1), last.reshape(-1),
                                                            jnp.zeros((pad,), F32)]
        return jnp.concatenate(flat).reshape(rows, PACK_COLS)

    def conv_full(shard):
        col_owner = lax.broadcasted_iota(jnp.int32, (2, 3, N_DEV * n_cw), 2) // n_cw
        return jnp.where(col_owner == me, jnp.tile(shard, (1, 1, N_DEV)), 0.0)

    zero1 = jnp.zeros((1,), F32)
    g_pack = pack([small_g[n] for n in SMALL], conv_w_g, loss)
    w_pack = pack([W[n] for n in SMALL], conv_full(W["ffn_conv_w"]), zero1)
    m_pack = pack([M[n] for n in SMALL], conv_full(M["ffn_conv_w"]), zero1)
    v_pack = pack([V[n] for n in SMALL], conv_full(V["ffn_conv_w"]), zero1 + 1.0)
    (small_handle,), small_sent = copies_start([g_pack], "gather", name="small_xstart")

    def finish(name, n_layers):
        w3, m3, v3 = T[name] if name in TRANSPOSED else (W[name], M[name], V[name])
        res = None
        for layer in reversed(range(n_layers)):
            key = name if n_layers == 1 else f"{name}{layer}"
            own, recv = copies_wait(exchanges[key], "exchange", small_sent, name=key + "_xwait")
            _, R, Cn = recv.shape
            res = reduce_adamw(recv, own, True, me, w3.reshape(n_layers * R, Cn), m3.reshape(n_layers * R, Cn),
                               v3.reshape(n_layers * R, Cn), layer=layer, n_layers=n_layers, into=res,
                               name=key + "_adamw")
        res = [r.reshape(w3.shape) for r in res]
        return tuple(jnp.swapaxes(r, -1, -2) for r in res) if name in TRANSPOSED else tuple(res)

    for name in ("ffn_w_out", "ffn_w_in"):
        out[name] = finish(name, 2)
    for name in ("att_w_o", "att_w_qkv", "mix_w_out", "s5_glu_w", "mix_w_in"):
        out[name] = finish(name, 1)

    small_own, small_recv = copies_wait(small_handle, "gather", out["s5_glu_w"][0], name="small_xwait")
    res = reduce_adamw(small_recv, small_own, False, me, w_pack, m_pack, v_pack, name="small_adamw")
    flat = [r.reshape(-1) for r in res]
    off = 0
    for n, sz in zip(SMALL, sizes):
        out[n] = tuple(f[off:off + sz].reshape(W[n].shape) for f in flat)
        off += sz
    conv_res = [f[off:off + n_conv].reshape(2, 3, N_DEV * n_cw) for f in flat]
    out["ffn_conv_w"] = tuple(lax.dynamic_slice(c, (0, 0, me * n_cw), (2, 3, n_cw)) for c in conv_res)
    off += n_conv
    loss_total = flat[0][off]

    result = [loss_total, grad_x[None]]
    for k in range(4):
        result += [out[n][k] for n in ORDER]
    return tuple(result)
```

```python
import functools
import math

import jax
import jax.numpy as jnp
from jax import lax
from jax.experimental import pallas as pl
from jax.experimental.pallas import tpu as pltpu

F32 = jnp.float32
BF16 = jnp.bfloat16
MESH_ID = pl.DeviceIdType.MESH
N_DEV = 8
VMEM_LIMIT_BYTES = 56 * 1024 * 1024

NORM_EPS = 1e-6
S5_WIDTH, S5_GROUP, S5_GROUPS, S5_STATE = 512, 16, 32, 64
HG_HEADS, HG_DIM, HG_CHUNK = 4, 128, 64
HG_STEP_CHUNKS = 4
ATT_E, ATT_HPG, ATT_BLOCK = 64, 8, 128
ATT_DILATIONS = (1, 4, 16)
ROT_DIM, ROPE_THETA = 16, 500000.0
D_FF = 2816
ADAM_LR, ADAM_B1, ADAM_B2, ADAM_EPS, ADAM_WD, ADAM_STEP = 0.001, 0.9, 0.999, 1e-08, 0.01, 10
NEG_BIG = -1e30


def _params(**kw):
    return pltpu.CompilerParams(vmem_limit_bytes=VMEM_LIMIT_BYTES, **kw)


def _pick(n, cands):
    for c in cands:
        if n % c == 0:
            return c
    return n


def _dot(a, b):
    return jnp.dot(a.astype(BF16), b.astype(BF16), preferred_element_type=F32)


def _dot_nt(a, b):
    return lax.dot_general(a.astype(BF16), b.astype(BF16), (((1,), (1,)), ((), ())), preferred_element_type=F32)


def _dot_tn(a, b):
    return lax.dot_general(a.astype(BF16), b.astype(BF16), (((0,), (0,)), ((), ())), preferred_element_type=F32)


def _split2(x):
    hi = x.astype(BF16)
    return hi, (x - hi.astype(F32)).astype(BF16)


def _dot_x3(a, b, contract=((1,), (0,))):
    dn = (contract, ((), ()))
    a1, a2 = _split2(a)
    b1, b2 = _split2(b)
    return (lax.dot_general(a1, b1, dn, preferred_element_type=F32) + lax.dot_general(a1, b2, dn, preferred_element_type=F32)
            + lax.dot_general(a2, b1, dn, preferred_element_type=F32))


def _sigmoid(x):
    return 1.0 / (1.0 + jnp.exp(-x))


V7X_HBM_BYTES_PER_S = 3.2e12
V7X_MXU_FLOPS_PER_S = 0.7e15
GRID_STEP_S = 0.35e-6
MM_VMEM_BUDGET = 40 * 1024 * 1024


def _divisors(n, cands):
    return [c for c in cands if c <= n and n % c == 0] or [n]


def _mm_tiles(m, n, k, sa, sb, so, sr):
    best = None
    for tm in _divisors(m, (2816, 2048, 1408, 1024, 512, 256, 128)):
        for tn in _divisors(n, (2816, 2048, 1408, 1024, 512, 256, 128)):
            for tk in _divisors(k, (k, 2816, 2560, 2304, 2048, 1536, 1408, 1280, 1024, 512, 256, 128)):
                nk = k // tk
                vmem = 2 * (tm * tk * sa + tk * tn * sb + tm * tn * (so + sr)) + (tm * tn * 4 if nk > 1 else 0)
                vmem += tm * tk * 2 * (sa > 2) + tk * tn * 2 * (sb > 2) + tm * tn * 4
                if vmem > MM_VMEM_BUDGET:
                    continue
                ni, nj = m // tm, n // tn
                for i_outer in (True, False):
                    if i_outer:
                        a_reads = 1 if nk == 1 else nj
                        b_reads = 1 if (nk == 1 and nj == 1) else ni
                    else:
                        b_reads = 1 if nk == 1 else ni
                        a_reads = 1 if (nk == 1 and ni == 1) else nj
                    traffic = a_reads * m * k * sa + b_reads * k * n * sb + m * n * (so + sr)
                    t = max(traffic / V7X_HBM_BYTES_PER_S, 2.0 * m * n * k / V7X_MXU_FLOPS_PER_S)
                    t += ni * nj * nk * GRID_STEP_S
                    t += (tm * tk * sa + tk * tn * sb + tm * tn * so) / V7X_HBM_BYTES_PER_S
                    if best is None or t < best[0]:
                        best = (t, tm, tn, tk, i_outer)
    assert best is not None, (m, n, k)
    return best[1:]


def mm(a, b, *, ta=False, tb=False, res=None, out_dtype=F32, dep=None, name):
    m, k = (a.shape[1], a.shape[0]) if ta else a.shape
    n = b.shape[0] if tb else b.shape[1]
    assert (b.shape[1] if tb else b.shape[0]) == k
    has_res = res is not None
    tm, tn, tk, i_outer = _mm_tiles(m, n, k, a.dtype.itemsize, b.dtype.itemsize, jnp.dtype(out_dtype).itemsize,
                                    res.dtype.itemsize if has_res else 0)
    nk = k // tk
    deps = [] if dep is None else [dep]
    dn = (((0 if ta else 1,), (1 if tb else 0,)), ((), ()))

    def body_single(*refs):
        a_ref, b_ref = refs[:2]
        o_ref = refs[-1]
        out = lax.dot_general(a_ref[...].astype(BF16), b_ref[...].astype(BF16), dn, preferred_element_type=F32)
        if has_res:
            out = out + refs[2][...].astype(F32)
        o_ref[...] = out.astype(o_ref.dtype)

    def body(*refs):
        a_ref, b_ref = refs[:2]
        r_ref = refs[2] if has_res else None
        o_ref, acc_ref = refs[-2:]
        kk = pl.program_id(2)
        part = lax.dot_general(a_ref[...].astype(BF16), b_ref[...].astype(BF16), dn, preferred_element_type=F32)

        @pl.when(kk == 0)
        def _():
            acc_ref[...] = part

        @pl.when(kk > 0)
        def _():
            acc_ref[...] += part

        @pl.when(kk == nk - 1)
        def _():
            out = acc_ref[...]
            if has_res:
                out = out + r_ref[...].astype(F32)
            o_ref[...] = out.astype(o_ref.dtype)

    def ij(f):
        return (lambda g0, g1, q: f(g0, g1, q)) if i_outer else (lambda g0, g1, q: f(g1, g0, q))

    a_spec = pl.BlockSpec((tk, tm), ij(lambda i, j, q: (q, i))) if ta else pl.BlockSpec((tm, tk), ij(lambda i, j, q: (i, q)))
    b_spec = pl.BlockSpec((tn, tk), ij(lambda i, j, q: (j, q))) if tb else pl.BlockSpec((tk, tn), ij(lambda i, j, q: (q, j)))
    o_spec = pl.BlockSpec((tm, tn), ij(lambda i, j, q: (i, j)))
    in_specs = [a_spec, b_spec] + ([o_spec] if has_res else []) + [pl.BlockSpec((8, 128), lambda g0, g1, q: (0, 0))] * len(deps)
    args = (a, b) + ((res,) if has_res else ()) + tuple(deps)
    grid = (m // tm, n // tn, nk) if i_outer else (n // tn, m // tm, nk)
    return pl.pallas_call(
        body_single if nk == 1 else body, grid=grid, in_specs=in_specs, out_specs=o_spec,
        out_shape=jax.ShapeDtypeStruct((m, n), out_dtype),
        scratch_shapes=[] if nk == 1 else [pltpu.VMEM((tm, tn), F32)],
        compiler_params=_params(dimension_semantics=("parallel", "parallel", "arbitrary")), name=name,
    )(*args)


def mm_drms(dy_in, w, x, g, dres, *, dep=None, name):
    m, k = dy_in.shape
    D = w.shape[1]
    tm = _pick(m, (512, 256, 128))
    tk = max(_divisors(k, (1536, 1408, 1280, 1024, 512, 256, 128)))
    nk = k // tk
    deps = [] if dep is None else [dep]

    def body(a_ref, b_ref, x_ref, g_ref, dres_ref, *rest):
        dx_ref, dg_ref, acc_ref = rest[-3:]
        i, q = pl.program_id(0), pl.program_id(1)
        part = jnp.dot(a_ref[...], b_ref[...], preferred_element_type=F32)

        @pl.when(q == 0)
        def _():
            acc_ref[...] = part

        @pl.when(q > 0)
        def _():
            acc_ref[...] += part

        @pl.when((i == 0) & (q == 0))
        def _():
            dg_ref[...] = jnp.zeros_like(dg_ref)

        @pl.when(q == nk - 1)
        def _():
            dyv = acc_ref[...]
            xv = x_ref[...]
            r = lax.rsqrt(jnp.mean(xv * xv, axis=-1, keepdims=True) + NORM_EPS)
            xh = xv * r
            dg_ref[...] += jnp.sum(dyv * xh, axis=0, keepdims=True)
            dxh = dyv * g_ref[...]
            dx_ref[...] = dres_ref[...] + r * (dxh - xh * jnp.mean(dxh * xh, axis=-1, keepdims=True))

    row = pl.BlockSpec((tm, D), lambda i, q: (i, 0))
    vec = pl.BlockSpec((1, D), lambda i, q: (0, 0))
    in_specs = [pl.BlockSpec((tm, tk), lambda i, q: (i, q)), pl.BlockSpec((tk, D), lambda i, q: (q, 0)), row, vec, row]
    in_specs += [pl.BlockSpec((8, 128), lambda i, q: (0, 0))] * len(deps)
    return pl.pallas_call(
        body, grid=(m // tm, nk), in_specs=in_specs, out_specs=[row, vec],
        out_shape=[jax.ShapeDtypeStruct((m, D), F32), jax.ShapeDtypeStruct((1, D), F32)],
        scratch_shapes=[pltpu.VMEM((tm, D), F32)],
        compiler_params=_params(dimension_semantics=("arbitrary", "arbitrary")), name=name,
    )(dy_in, w, x, g.reshape(1, D), dres, *deps)


def rms_fwd(x, g, *, dep=None, name):
    L, D = x.shape
    tr = _pick(L, (256, 128))

    def body(x_ref, g_ref, *rest):
        o_ref = rest[-1]
        xv = x_ref[...]
        r = lax.rsqrt(jnp.mean(xv * xv, axis=-1, keepdims=True) + NORM_EPS)
        o_ref[...] = (xv * r * g_ref[...]).astype(o_ref.dtype)

    row = pl.BlockSpec((tr, D), lambda i: (i, 0))
    vec = pl.BlockSpec((1, D), lambda i: (0, 0))
    deps = [] if dep is None else [dep]
    return pl.pallas_call(body, grid=(L // tr,), in_specs=[row, vec] + [pl.BlockSpec((8, 128), lambda i: (0, 0))] * len(deps),
                          out_specs=row, out_shape=jax.ShapeDtypeStruct((L, D), BF16), name=name)(
        x, g.reshape(1, D), *deps)


def final_loss(h, g, target, *, name):
    L, D = h.shape
    tr = _pick(L, (256, 128))

    def body(x_ref, g_ref, t_ref, loss_ref, dx_ref, dg_ref):
        xv = x_ref[...]
        gv = g_ref[...]
        r = lax.rsqrt(jnp.mean(xv * xv, axis=-1, keepdims=True) + NORM_EPS)
        xh = xv * r
        err = xh * gv - t_ref[...]

        @pl.when(pl.program_id(0) == 0)
        def _():
            dg_ref[...] = jnp.zeros_like(dg_ref)
            loss_ref[...] = jnp.zeros_like(loss_ref)

        loss_ref[...] += 0.5 * jnp.sum(jnp.mean(err * err, axis=-1, keepdims=True), axis=0, keepdims=True)
        dyv = err * (1.0 / D)
        dg_ref[...] += jnp.sum(dyv * xh, axis=0, keepdims=True)
        dxh = dyv * gv
        dx_ref[...] = r * (dxh - xh * jnp.mean(dxh * xh, axis=-1, keepdims=True))

    row = pl.BlockSpec((tr, D), lambda i: (i, 0))
    vec = pl.BlockSpec((1, D), lambda i: (0, 0))
    one = pl.BlockSpec((1, 1), lambda i: (0, 0))
    return pl.pallas_call(body, grid=(L // tr,), in_specs=[row, vec, row], out_specs=[one, row, vec],
                          out_shape=[jax.ShapeDtypeStruct((1, 1), F32), jax.ShapeDtypeStruct((L, D), F32),
                                     jax.ShapeDtypeStruct((1, D), F32)],
                          compiler_params=_params(dimension_semantics=("arbitrary",)), name=name)(
        h, g.reshape(1, D), target)


def _cmul(ar, ai, br, bi):
    return ar * br - ai * bi, ar * bi + ai * br


def _powers(ar, ai):
    rows = [(ar, ai)]
    for _ in range(7):
        rows.append(_cmul(rows[-1][0], rows[-1][1], ar, ai))
    table = (jnp.concatenate([r[0] for r in rows], axis=0), jnp.concatenate([r[1] for r in rows], axis=0))
    return (rows[0], rows[1], rows[3]), table


def _block_scan(br, bi, steps, shift):
    yr, yi = br, bi
    for s, (pr, pi) in zip((1, 2, 4), steps):
        sr, si = shift(yr, s), shift(yi, s)
        yr, yi = yr + pr * sr - pi * si, yi + pr * si + pi * sr
    return yr, yi


def s5_core_fwd(proj, a_re, a_im, wb_re, wb_im, wc_re, wc_im, *, name):
    L = proj.shape[0]
    parts, cu, W = wb_re.shape

    def body(u_ref, ar_ref, ai_ref, wbr_ref, wbi_ref, wcr_ref, wci_ref, y_ref, xr_ref, xi_ref, br_ref, bi_ref):
        u = u_ref[...]
        br_ref[...] = _dot(u, wbr_ref[...])
        bi_ref[...] = _dot(u, wbi_ref[...])
        steps, (tr, ti) = _powers(ar_ref[...], ai_ref[...])
        row = lax.broadcasted_iota(jnp.int32, (8, W), 0)

        def shift(y, s):
            return jnp.where(row >= s, pltpu.roll(y, s, 0), 0.0)

        def step(t8, carry):
            cr, ci = carry
            base = pl.multiple_of(t8 * 8, 8)
            yr, yi = _block_scan(br_ref[pl.ds(base, 8), :], bi_ref[pl.ds(base, 8), :], steps, shift)
            xr = yr + tr * cr - ti * ci
            xi = yi + tr * ci + ti * cr
            xr_ref[pl.ds(base, 8), :] = xr
            xi_ref[pl.ds(base, 8), :] = xi
            return jnp.broadcast_to(xr[7:8, :], (8, W)), jnp.broadcast_to(xi[7:8, :], (8, W))

        zero = jnp.zeros((8, W), F32)
        lax.fori_loop(0, L // 8, step, (zero, zero), unroll=2)
        y_ref[...] = _dot(xr_ref[...], wcr_ref[...]) + _dot(xi_ref[...], wci_ref[...])

    ucol = pl.BlockSpec((L, cu), lambda t: (0, t))
    vec = pl.BlockSpec((1, W), lambda t: (0, t))
    col = pl.BlockSpec((L, W), lambda t: (0, t))
    wb = pl.BlockSpec((None, cu, W), lambda t: (t, 0, 0))
    wc = pl.BlockSpec((None, W, cu), lambda t: (t, 0, 0))
    return pl.pallas_call(body, grid=(parts,), in_specs=[ucol, vec, vec, wb, wb, wc, wc], out_specs=[ucol, col, col],
                          out_shape=[jax.ShapeDtypeStruct((L, parts * cu), F32)]
                          + [jax.ShapeDtypeStruct((L, parts * W), F32)] * 2,
                          scratch_shapes=[pltpu.VMEM((L, W), F32)] * 2,
                          compiler_params=_params(dimension_semantics=("parallel",)), name=name)(
        proj, a_re, a_im, wb_re, wb_im, wc_re, wc_im)


def s5_core_bwd(dy, du_d, proj, xs_re, xs_im, a_re, a_im, wb_re, wb_im, wc_re, wc_im, *, name):
    L = proj.shape[0]
    parts, cu, W = wb_re.shape

    def body(dy_ref, dud_ref, u_ref, xr_ref, xi_ref, ar_ref, ai_ref, wbr_ref, wbi_ref, wcr_ref, wci_ref,
             du_ref, dwbr_ref, dwbi_ref, dwcr_ref, dwci_ref, dar_ref, dai_ref, lr_ref, li_ref):
        dy = dy_ref[...]
        lr_ref[...] = _dot_nt(dy, wcr_ref[...])
        li_ref[...] = _dot_nt(dy, wci_ref[...])
        dwcr_ref[...] = _dot_tn(xr_ref[...], dy)
        dwci_ref[...] = _dot_tn(xi_ref[...], dy)
        ar, ai = ar_ref[...], -ai_ref[...]
        steps, (tr, ti) = _powers(ar, ai)
        tr = jnp.concatenate([tr[j:j + 1, :] for j in range(7, -1, -1)], axis=0)
        ti = jnp.concatenate([ti[j:j + 1, :] for j in range(7, -1, -1)], axis=0)
        row8 = lax.broadcasted_iota(jnp.int32, (8, W), 0)
        nblk = L // 8

        def shift(y, s):
            return jnp.where(row8 < 8 - s, pltpu.roll(y, 8 - s, 0), 0.0)

        def step(s, carry):
            cr, ci = carry
            base = pl.multiple_of((nblk - 1 - s) * 8, 8)
            yr, yi = _block_scan(lr_ref[pl.ds(base, 8), :], li_ref[pl.ds(base, 8), :], steps, shift)
            lr = yr + tr * cr - ti * ci
            li = yi + tr * ci + ti * cr
            lr_ref[pl.ds(base, 8), :] = lr
            li_ref[pl.ds(base, 8), :] = li
            return jnp.broadcast_to(lr[0:1, :], (8, W)), jnp.broadcast_to(li[0:1, :], (8, W))

        zero = jnp.zeros((8, W), F32)
        lax.fori_loop(0, nblk, step, (zero, zero), unroll=2)
        row = lax.broadcasted_iota(jnp.int32, (L, W), 0)
        xpr = jnp.where(row >= 1, pltpu.roll(xr_ref[...], 1, 0), 0.0)
        xpi = jnp.where(row >= 1, pltpu.roll(xi_ref[...], 1, 0), 0.0)
        lr, li = lr_ref[...], li_ref[...]
        dar_ref[...] = jnp.sum(lr * xpr + li * xpi, axis=0, keepdims=True)
        dai_ref[...] = jnp.sum(li * xpr - lr * xpi, axis=0, keepdims=True)
        u = u_ref[...]
        dwbr_ref[...] = _dot_tn(u, lr)
        dwbi_ref[...] = _dot_tn(u, li)
        du_ref[...] = (dud_ref[...] + _dot_nt(lr, wbr_ref[...]) + _dot_nt(li, wbi_ref[...])).astype(du_ref.dtype)

    ucol = pl.BlockSpec((L, cu), lambda t: (0, t))
    vec = pl.BlockSpec((1, W), lambda t: (0, t))
    col = pl.BlockSpec((L, W), lambda t: (0, t))
    wb = pl.BlockSpec((None, cu, W), lambda t: (t, 0, 0))
    wc = pl.BlockSpec((None, W, cu), lambda t: (t, 0, 0))
    return pl.pallas_call(
        body, grid=(parts,), in_specs=[ucol, ucol, ucol, col, col, vec, vec, wb, wb, wc, wc],
        out_specs=[ucol, wb, wb, wc, wc, vec, vec],
        out_shape=[jax.ShapeDtypeStruct((L, parts * cu), BF16)] + [jax.ShapeDtypeStruct((parts, cu, W), F32)] * 2
        + [jax.ShapeDtypeStruct((parts, W, cu), F32)] * 2 + [jax.ShapeDtypeStruct((1, parts * W), F32)] * 2,
        scratch_shapes=[pltpu.VMEM((L, W), F32)] * 2,
        compiler_params=_params(dimension_semantics=("parallel",)), name=name,
    )(dy, du_d, proj, xs_re, xs_im, a_re, a_im, wb_re, wb_im, wc_re, wc_im)


def _gelu(y):
    c = math.sqrt(2.0 / math.pi)
    t = jnp.tanh(c * (y + 0.044715 * y * y * y))
    return 0.5 * y * (1.0 + t), t


def s5_out_fwd(y0, proj, dvec, glu_w, glu_b, *, name):
    L, C = y0.shape
    tr = _pick(L, (256, 128))

    def body(y_ref, u_ref, d_ref, w_ref, b_ref, o_ref):
        z, _ = _gelu(y_ref[...] + d_ref[...] * u_ref[...])
        zg = _dot(z, w_ref[...]) + b_ref[...]
        o_ref[...] = (z * _sigmoid(zg)).astype(o_ref.dtype)

    row = pl.BlockSpec((tr, C), lambda i: (i, 0))
    vec = pl.BlockSpec((1, C), lambda i: (0, 0))
    wsp = pl.BlockSpec((C, C), lambda i: (0, 0))
    return pl.pallas_call(body, grid=(L // tr,), in_specs=[row, row, vec, wsp, vec], out_specs=row,
                          out_shape=jax.ShapeDtypeStruct((L, C), BF16), name=name)(
        y0, proj, dvec, glu_w, glu_b)


def s5_out_bwd(y0, proj, dvec, glu_w, glu_b, dcat, *, name):
    L, C = y0.shape
    tr = _pick(L, (256, 128))

    def body(y_ref, u_ref, d_ref, w_ref, b_ref, do_ref, dy_ref, dud_ref, z_ref, dzg_ref, db_ref, dd_ref):
        u = u_ref[...]
        y = y_ref[...] + d_ref[...] * u
        z, t = _gelu(y)
        zg = _dot(z, w_ref[...]) + b_ref[...]
        s = _sigmoid(zg)
        do = do_ref[...]
        dzg = do * z * s * (1.0 - s)
        dz = do * s + _dot_nt(dzg, w_ref[...])
        c = math.sqrt(2.0 / math.pi)
        dgelu = 0.5 * (1.0 + t) + 0.5 * y * (1.0 - t * t) * c * (1.0 + 3.0 * 0.044715 * y * y)
        dy = dz * dgelu

        @pl.when(pl.program_id(0) == 0)
        def _():
            db_ref[...] = jnp.zeros_like(db_ref)
            dd_ref[...] = jnp.zeros_like(dd_ref)

        db_ref[...] += jnp.sum(dzg, axis=0, keepdims=True)
        dd_ref[...] += jnp.sum(dy * u, axis=0, keepdims=True)
        dy_ref[...] = dy
        dud_ref[...] = dy * d_ref[...]
        z_ref[...] = z.astype(BF16)
        dzg_ref[...] = dzg.astype(BF16)

    row = pl.BlockSpec((tr, C), lambda i: (i, 0))
    vec = pl.BlockSpec((1, C), lambda i: (0, 0))
    wsp = pl.BlockSpec((C, C), lambda i: (0, 0))
    return pl.pallas_call(body, grid=(L // tr,), in_specs=[row, row, vec, wsp, vec, row],
                          out_specs=[row, row, row, row, vec, vec],
                          out_shape=[jax.ShapeDtypeStruct((L, C), F32), jax.ShapeDtypeStruct((L, C), F32),
                                     jax.ShapeDtypeStruct((L, C), BF16), jax.ShapeDtypeStruct((L, C), BF16),
                                     jax.ShapeDtypeStruct((1, C), F32), jax.ShapeDtypeStruct((1, C), F32)],
                          compiler_params=_params(dimension_semantics=("arbitrary",)), name=name)(
        y0, proj, dvec, glu_w, glu_b, dcat)


def _dot_tri(tri, x, tri_left=True):
    t = tri.astype(BF16)
    x1 = x.astype(BF16)
    r1 = x - x1.astype(F32)
    x2 = r1.astype(BF16)
    x3 = (r1 - x2.astype(F32)).astype(BF16)
    dot = (lambda p: jnp.dot(t, p, preferred_element_type=F32)) if tri_left else (
        lambda p: jnp.dot(p, t, preferred_element_type=F32))
    return dot(x1) + dot(x2) + dot(x3)


def _hg_gates(xq, xf, lb, tri):
    C = xq.shape[0]
    sq = _sigmoid(xq)
    q = xq * sq
    sg = _sigmoid(xf)
    f = lb + (1.0 - lb) * sg
    kk = 1.0 - f
    b = _dot_tri(tri, jnp.log(f))
    bm = b[C // 2 - 1:C // 2, :]
    bl = b[C - 1:C, :]
    eb = jnp.exp(b)
    eqm, ekm, ekl = jnp.exp(b - bm), jnp.exp(bm - b), jnp.exp(bl - b)
    return dict(sq=sq, q=q, sg=sg, f=f, kk=kk, eb=eb, ebl=jnp.exp(bl), eqm=eqm, ekm=ekm, ekl=ekl,
                qb=q * eb, qt=q * eqm, kt=kk * ekm, kh=kk * ekl)


def _tri(C, lower):
    r = lax.broadcasted_iota(jnp.int32, (C, C), 0)
    c = lax.broadcasted_iota(jnp.int32, (C, C), 1)
    return (r >= c) if lower else (c >= r)


def hgrn_fwd(proj, lb, norm_g, *, name):
    L = proj.shape[0]
    C, H, K = HG_CHUNK, HG_HEADS, HG_DIM
    HK = H * K
    nc = L // C

    def body(q_ref, f_ref, i_ref, g_ref, lb_ref, ng_ref, o_ref, sall_ref, st_ref):
        @pl.when(pl.program_id(0) == 0)
        def _():
            st_ref[...] = jnp.zeros_like(st_ref)

        mask = _tri(C, True)
        sts = [st_ref[h] for h in range(H)]
        for s in range(S):
            rs = slice(s * C, (s + 1) * C)
            gt = _hg_gates(q_ref[rs, :], f_ref[rs, :], lb_ref[...], mask.astype(F32))
            v_all = i_ref[rs, :]
            outs = []
            for h in range(H):
                sl = slice(h * K, (h + 1) * K)
                v, st = v_all[:, sl], sts[h]
                sall_ref[s, h] = st
                att = jnp.where(mask, _dot_nt(gt["qt"][:, sl], gt["kt"][:, sl]), 0.0)
                o = _dot(att, v) + _dot_nt(gt["qb"][:, sl], st)
                sts[h] = st * gt["ebl"][:, sl] + _dot_tn(v, gt["kh"][:, sl])
                outs.append(o * lax.rsqrt(jnp.mean(o * o, axis=-1, keepdims=True) + NORM_EPS))
            xg = g_ref[rs, :]
            o_ref[rs, :] = (jnp.concatenate(outs, axis=1) * ng_ref[...] * (xg * _sigmoid(xg))).astype(o_ref.dtype)
        for h in range(H):
            st_ref[h] = sts[h]

    S = HG_STEP_CHUNKS

    def blk(cb):
        return pl.BlockSpec((S * C, HK), lambda i: (i, cb))

    vec = pl.BlockSpec((1, HK), lambda i: (0, 0))
    return pl.pallas_call(
        body, grid=(nc // S,), in_specs=[blk(1), blk(2), blk(3), blk(4), vec, vec],
        out_specs=[pl.BlockSpec((S * C, HK), lambda i: (i, 0)), pl.BlockSpec((S, H, K, K), lambda i: (i, 0, 0, 0))],
        out_shape=[jax.ShapeDtypeStruct((L, HK), BF16), jax.ShapeDtypeStruct((nc, H, K, K), F32)],
        scratch_shapes=[pltpu.VMEM((H, K, K), F32)],
        compiler_params=_params(dimension_semantics=("arbitrary",)), name=name,
    )(proj, proj, proj, proj, lb, norm_g)


def hgrn_bwd(proj, lb, norm_g, sall, dcat, *, name):
    L = proj.shape[0]
    C, H, K = HG_CHUNK, HG_HEADS, HG_DIM
    HK = H * K
    nc = L // C

    def body(q_ref, f_ref, i_ref, g_ref, lb_ref, ng_ref, sall_ref, do_ref, dx_ref, dlb_ref, dng_ref, dst_ref):
        @pl.when(pl.program_id(0) == 0)
        def _():
            dst_ref[...] = jnp.zeros_like(dst_ref)
            dlb_ref[...] = jnp.zeros_like(dlb_ref)
            dng_ref[...] = jnp.zeros_like(dng_ref)

        mask = _tri(C, True)
        lb_all, ng = lb_ref[...], ng_ref[...]
        dsts = [dst_ref[h] for h in range(H)]
        for s in reversed(range(S)):
            rs = slice(s * C, (s + 1) * C)
            dsts = chunk_bwd(rs, s, dsts, mask, lb_all, ng, q_ref, f_ref, i_ref, g_ref, sall_ref, do_ref,
                             dx_ref, dlb_ref, dng_ref)
        for h in range(H):
            dst_ref[h] = dsts[h]

    def chunk_bwd(rs, s, dsts, mask, lb_all, ng, q_ref, f_ref, i_ref, g_ref, sall_ref, do_ref, dx_ref, dlb_ref, dng_ref):
        xq, xg, v_all = q_ref[rs, :], g_ref[rs, :], i_ref[rs, :]
        gt = _hg_gates(xq, f_ref[rs, :], lb_all, mask.astype(F32))
        sgg = _sigmoid(xg)
        d_ob = do_ref[rs, :]
        d_on = d_ob * (xg * sgg)
        doh = d_on * ng
        ohs, d_qts, d_qbs, d_kts, d_khs, dvs, d_bls, new_dsts = [], [], [], [], [], [], [], []
        for h in range(H):
            sl = slice(h * K, (h + 1) * K)
            v, st, dst = v_all[:, sl], sall_ref[s, h], dsts[h]
            qt, kt, kh, qb = gt["qt"][:, sl], gt["kt"][:, sl], gt["kh"][:, sl], gt["qb"][:, sl]
            att = jnp.where(mask, _dot_nt(qt, kt), 0.0)
            o = _dot(att, v) + _dot_nt(qb, st)
            r = lax.rsqrt(jnp.mean(o * o, axis=-1, keepdims=True) + NORM_EPS)
            oh = o * r
            do = r * (doh[:, sl] - oh * jnp.mean(doh[:, sl] * oh, axis=-1, keepdims=True))
            datt = jnp.where(mask, _dot_nt(do, v), 0.0)
            dvs.append(_dot_tn(att, do) + _dot_nt(kh, dst))
            d_qbs.append(_dot_x3(do, st))
            d_qts.append(_dot_x3(datt, kt))
            d_kts.append(_dot_x3(datt, qt, ((0,), (0,))))
            d_kh = _dot_x3(v, dst)
            d_khs.append(d_kh)
            d_bls.append(jnp.sum(dst * st, axis=0, keepdims=True) * gt["ebl"][:, sl]
                         + jnp.sum(d_kh * kh, axis=0, keepdims=True))
            new_dsts.append(dst * gt["ebl"][:, sl] + _dot_tn(do, qb))
            ohs.append(oh)
        oh, d_qt, d_qb, d_kt, d_kh, dv, d_bl = (jnp.concatenate(p, axis=1) for p in
                                                (ohs, d_qts, d_qbs, d_kts, d_khs, dvs, d_bls))
        dxg = d_ob * (oh * ng) * (sgg * (1.0 + xg * (1.0 - sgg)))
        dng_ref[...] += jnp.sum(d_on * oh, axis=0, keepdims=True)
        dq = d_qt * gt["eqm"] + d_qb * gt["eb"]
        db = d_qt * gt["qt"] + d_qb * gt["qb"] - d_kt * gt["kt"] - d_kh * gt["kh"]
        rowi = lax.broadcasted_iota(jnp.int32, (C, HK), 0)
        db = db + jnp.where(rowi == C - 1, d_bl, 0.0)
        dkk = d_kt * gt["ekm"] + d_kh * gt["ekl"]
        dlg = _dot_tri(_tri(C, False).astype(F32), db)
        df = dlg / gt["f"] - dkk
        sg, sq = gt["sg"], gt["sq"]
        dlb_ref[...] += jnp.sum(df * (1.0 - sg), axis=0, keepdims=True)
        dx_ref[rs, 0:HK] = (dq * (sq * (1.0 + xq * (1.0 - sq)))).astype(dx_ref.dtype)
        dx_ref[rs, HK:2 * HK] = (df * (1.0 - lb_all) * sg * (1.0 - sg)).astype(dx_ref.dtype)
        dx_ref[rs, 2 * HK:3 * HK] = dv.astype(dx_ref.dtype)
        dx_ref[rs, 3 * HK:4 * HK] = dxg.astype(dx_ref.dtype)
        return new_dsts

    S = HG_STEP_CHUNKS
    ns = nc // S

    def blk(cb):
        return pl.BlockSpec((S * C, HK), lambda i: (ns - 1 - i, cb))

    vec = pl.BlockSpec((1, HK), lambda i: (0, 0))
    return pl.pallas_call(
        body, grid=(ns,),
        in_specs=[blk(1), blk(2), blk(3), blk(4), vec, vec,
                  pl.BlockSpec((S, H, K, K), lambda i: (ns - 1 - i, 0, 0, 0)), blk(1)],
        out_specs=[pl.BlockSpec((S * C, 4 * HK), lambda i: (ns - 1 - i, 0)), vec, vec],
        out_shape=[jax.ShapeDtypeStruct((L, 4 * HK), BF16), jax.ShapeDtypeStruct((1, HK), F32),
                   jax.ShapeDtypeStruct((1, HK), F32)],
        scratch_shapes=[pltpu.VMEM((H, K, K), F32)],
        compiler_params=_params(dimension_semantics=("arbitrary",)), name=name,
    )(proj, proj, proj, proj, lb, norm_g, sall, dcat)


def _shift_down(x, k, row):
    return jnp.where(row >= k, pltpu.roll(x, k, 0), 0.0)


def _shift_up(x, k, row):
    n = x.shape[0]
    return jnp.where(row < n - k, pltpu.roll(x, n - k, 0), 0.0)


def convgate_fwd(hu, conv_w, conv_b, *, name):
    L, C2 = hu.shape
    C = C2 // 2
    tc = _pick(C, (256, 128))
    nb = C // tc

    def body(a_ref, b_ref, wa_ref, wb_ref, ba_ref, bb_ref, o_ref):
        row = lax.broadcasted_iota(jnp.int32, (L, tc), 0)

        def conv(x, w, bias):
            return w[2:3, :] * x + w[1:2, :] * _shift_down(x, 1, row) + w[0:1, :] * _shift_down(x, 2, row) + bias

        ca = conv(a_ref[...], wa_ref[...], ba_ref[...])
        cb = conv(b_ref[...], wb_ref[...], bb_ref[...])
        o_ref[...] = (ca * _sigmoid(ca) * cb).astype(o_ref.dtype)

    def col(off, rows):
        return pl.BlockSpec((rows, tc), lambda j: (0, j + off))

    return pl.pallas_call(
        body, grid=(nb,), in_specs=[col(0, L), col(nb, L), col(0, 3), col(nb, 3), col(0, 1), col(nb, 1)],
        out_specs=col(0, L), out_shape=jax.ShapeDtypeStruct((L, C), BF16),
        compiler_params=_params(dimension_semantics=("parallel",)), name=name,
    )(hu, hu, conv_w, conv_w, conv_b, conv_b)


def convgate_bwd(hu, conv_w, conv_b, dact, *, name):
    L, C2 = hu.shape
    C = C2 // 2
    tc = _pick(C, (256, 128))
    nb = C // tc

    def body(a_ref, b_ref, wa_ref, wb_ref, ba_ref, bb_ref, d_ref, dxa_ref, dxb_ref, dwa_ref, dwb_ref, dba_ref, dbb_ref):
        row = lax.broadcasted_iota(jnp.int32, (L, tc), 0)

        def conv(x, w, bias):
            x1 = _shift_down(x, 1, row)
            x2 = _shift_down(x, 2, row)
            return w[2:3, :] * x + w[1:2, :] * x1 + w[0:1, :] * x2 + bias, x1, x2

        xa, xb = a_ref[...], b_ref[...]
        wa, wb = wa_ref[...], wb_ref[...]
        ca, xa1, xa2 = conv(xa, wa, ba_ref[...])
        cb, xb1, xb2 = conv(xb, wb, bb_ref[...])
        d = d_ref[...]
        sa = _sigmoid(ca)
        dca = d * cb * (sa * (1.0 + ca * (1.0 - sa)))
        dcb = d * (ca * sa)

        def back(dc, w, x, x1, x2, dx_ref, dw_ref, db_ref):
            dx = w[2:3, :] * dc + w[1:2, :] * _shift_up(dc, 1, row) + w[0:1, :] * _shift_up(dc, 2, row)
            dx_ref[...] = dx.astype(dx_ref.dtype)
            dw_ref[...] = jnp.concatenate([jnp.sum(dc * x2, axis=0, keepdims=True),
                                           jnp.sum(dc * x1, axis=0, keepdims=True),
                                           jnp.sum(dc * x, axis=0, keepdims=True)], axis=0)
            db_ref[...] = jnp.sum(dc, axis=0, keepdims=True)

        back(dca, wa, xa, xa1, xa2, dxa_ref, dwa_ref, dba_ref)
        back(dcb, wb, xb, xb1, xb2, dxb_ref, dwb_ref, dbb_ref)

    def col(off, rows):
        return pl.BlockSpec((rows, tc), lambda j: (0, j + off))

    outs = pl.pallas_call(
        body, grid=(nb,),
        in_specs=[col(0, L), col(nb, L), col(0, 3), col(nb, 3), col(0, 1), col(nb, 1), col(0, L)],
        out_specs=[col(0, L), col(0, L), col(0, 3), col(0, 3), col(0, 1), col(0, 1)],
        out_shape=[jax.ShapeDtypeStruct((L, C), BF16)] * 2 + [jax.ShapeDtypeStruct((3, C), F32)] * 2
        + [jax.ShapeDtypeStruct((1, C), F32)] * 2,
        compiler_params=_params(dimension_semantics=("parallel",)), name=name,
    )(hu, hu, conv_w, conv_w, conv_b, conv_b, dact)
    dxa, dxb, dwa, dwb, dba, dbb = outs
    return (jnp.concatenate([dxa, dxb], axis=1), jnp.concatenate([dwa, dwb], axis=1),
            jnp.concatenate([dba, dbb], axis=1))


def rope_tables(positions):
    half = ROT_DIM // 2
    inv_freq = ROPE_THETA ** (-jnp.arange(half, dtype=F32) * 2.0 / ROT_DIM)
    ang = positions.astype(F32)[:, None] * inv_freq
    cos, sin = jnp.cos(ang), jnp.sin(ang)
    L = positions.shape[0]
    one = jnp.ones((L, ATT_E - ROT_DIM), F32)
    zero = jnp.zeros((L, ATT_E - ROT_DIM), F32)
    zh = jnp.zeros((L, half), F32)
    tc = jnp.concatenate([cos, cos, one], axis=1)
    ts1 = jnp.concatenate([zh, sin, zero], axis=1)
    ts2 = jnp.concatenate([-sin, zh, zero], axis=1)
    return tuple(jnp.concatenate([t, t], axis=1) for t in (tc, ts1, ts2))


def qkv_rope(hn, w_t, tabs, *, name):
    L, D = hn.shape
    N = w_t.shape[0]
    W = 512
    tm = _pick(L, (1024, 512, 256, 128))
    nq = N // (3 * W)
    scale = ATT_E ** -0.5

    def body(a_ref, b_ref, c_ref, s1_ref, s2_ref, o_ref):
        j = pl.program_id(1)
        x = _dot_nt(a_ref[...], b_ref[...])
        c = jnp.concatenate([c_ref[...]] * 4, axis=1)
        s1 = jnp.concatenate([s1_ref[...]] * 4, axis=1)
        s2 = jnp.concatenate([s2_ref[...]] * 4, axis=1)
        rot = x * c + pltpu.roll(x, 8, 1) * s1 + pltpu.roll(x, W - 8, 1) * s2
        mult = jnp.where(j < nq, scale, 1.0)
        o_ref[...] = jnp.where(j < 2 * nq, rot * mult, x)

    tab = pl.BlockSpec((tm, 128), lambda i, j: (i, 0))
    return pl.pallas_call(body, grid=(L // tm, N // W),
                          in_specs=[pl.BlockSpec((tm, D), lambda i, j: (i, 0)), pl.BlockSpec((W, D), lambda i, j: (j, 0)),
                                    tab, tab, tab],
                          out_specs=pl.BlockSpec((tm, W), lambda i, j: (i, j)),
                          out_shape=jax.ShapeDtypeStruct((L, N), F32),
                          compiler_params=_params(dimension_semantics=("parallel", "parallel")), name=name)(
        hn, w_t, *tabs)


def rope_bwd(slabs, tabs, *, name):
    L, W = slabs[0].shape
    tr = _pick(L, (256, 128))
    nq = len(slabs) // 3
    scale = ATT_E ** -0.5

    def body(*refs):
        d_refs, (c_ref, s1_ref, s2_ref, o_ref) = refs[:3 * nq], refs[3 * nq:]
        c = jnp.concatenate([c_ref[...]] * 4, axis=1)
        s1 = jnp.concatenate([s1_ref[...]] * 4, axis=1)
        s2 = jnp.concatenate([s2_ref[...]] * 4, axis=1)
        for j, d_ref in enumerate(d_refs):
            dy = d_ref[...]
            if j < 2 * nq:
                dy = dy * c + pltpu.roll(dy * s1, W - 8, 1) + pltpu.roll(dy * s2, 8, 1)
            if j < nq:
                dy = dy * scale
            o_ref[:, j * W:(j + 1) * W] = dy.astype(o_ref.dtype)

    slab = pl.BlockSpec((tr, W), lambda i: (i, 0))
    tab = pl.BlockSpec((tr, 128), lambda i: (i, 0))
    return pl.pallas_call(body, grid=(L // tr,), in_specs=[slab] * (3 * nq) + [tab, tab, tab],
                          out_specs=pl.BlockSpec((tr, 3 * nq * W), lambda i: (i, 0)),
                          out_shape=jax.ShapeDtypeStruct((L, 3 * nq * W), BF16),
                          compiler_params=_params(dimension_semantics=("parallel",)), name=name)(*slabs, *tabs)


def _att_masks(has_prev):
    qi = lax.broadcasted_iota(jnp.int32, (ATT_BLOCK, ATT_BLOCK), 0)
    kj = lax.broadcasted_iota(jnp.int32, (ATT_BLOCK, ATT_BLOCK), 1)
    return qi >= kj, (kj >= qi) & has_prev


ATT_COLS = 128


def _att_rows(j, d, nb):
    B = ATT_BLOCK
    r, n = j // nb, j % nb
    start = r + d * B * n
    has_prev = n > 0
    pstart = jnp.where(has_prev, start - d * B, start)
    if d == 1:
        return pl.ds(pl.multiple_of(start, B), B), pl.ds(pl.multiple_of(pstart, B), B), has_prev
    return pl.ds(start, B, stride=d), pl.ds(pstart, B, stride=d), has_prev


def _qkv_specs(L, g):
    per = ATT_HPG * ATT_E // ATT_COLS
    third = len(ATT_DILATIONS) * per
    return [pl.BlockSpec((L, ATT_COLS), lambda c, base=base: (0, base + c))
            for base in (g * per, third + g * per, 2 * third + g * per)]


def attn_fwd(qkv, g, d, *, name):
    L, W = qkv.shape[0], ATT_HPG * ATT_E
    B, E = ATT_BLOCK, ATT_E
    nblk = L // B
    nb = nblk // d

    def body(q_ref, k_ref, v_ref, o_ref, l_ref):
        def step(j, carry):
            cur, prv, has_prev = _att_rows(j, d, nb)
            mc, mp = _att_masks(has_prev)
            qb, kc, kp, vc, vp = q_ref[cur, :], k_ref[cur, :], k_ref[prv, :], v_ref[cur, :], v_ref[prv, :]
            outs, lses = [], []
            for h in range(ATT_COLS // E):
                sl = slice(h * E, (h + 1) * E)
                sc = jnp.where(mc, _dot_nt(qb[:, sl], kc[:, sl]), NEG_BIG)
                sp = jnp.where(mp, _dot_nt(qb[:, sl], kp[:, sl]), NEG_BIG)
                m = jnp.maximum(jnp.max(sc, axis=-1, keepdims=True), jnp.max(sp, axis=-1, keepdims=True))
                pc = jnp.exp(sc - m)
                pp = jnp.exp(sp - m)
                den = jnp.sum(pc, axis=-1, keepdims=True) + jnp.sum(pp, axis=-1, keepdims=True)
                outs.append((_dot(pc, vc[:, sl]) + _dot(pp, vp[:, sl])) / den)
                lses.append(jnp.broadcast_to(m + jnp.log(den), (B, E)))
            o_ref[cur, :] = jnp.concatenate(outs, axis=1)
            l_ref[cur, :] = jnp.concatenate(lses, axis=1)
            return carry

        lax.fori_loop(0, nblk, step, 0, unroll=4)

    col = pl.BlockSpec((L, ATT_COLS), lambda c: (0, c))
    return pl.pallas_call(body, grid=(W // ATT_COLS,), in_specs=_qkv_specs(L, g), out_specs=[col] * 2,
                          out_shape=[jax.ShapeDtypeStruct((L, W), F32)] * 2,
                          compiler_params=_params(dimension_semantics=("parallel",)), name=name)(qkv, qkv, qkv)


def attn_bwd(qkv, g, lse, do, dl, d, *, name):
    L, W = qkv.shape[0], ATT_HPG * ATT_E
    B, E = ATT_BLOCK, ATT_E
    nblk = L // B
    nb = nblk // d

    def body(q_ref, k_ref, v_ref, l_ref, do_ref, dl_ref, dq_ref, dk_ref, dv_ref):
        dk_ref[...] = jnp.zeros_like(dk_ref)
        dv_ref[...] = jnp.zeros_like(dv_ref)

        def step(j, carry):
            cur, prv, has_prev = _att_rows(j, d, nb)
            mc, mp = _att_masks(has_prev)
            qb, kc, kp, vc, vp = q_ref[cur, :], k_ref[cur, :], k_ref[prv, :], v_ref[cur, :], v_ref[prv, :]
            lb, dob, dlb = l_ref[cur, :], do_ref[cur, :], dl_ref[cur, :]
            dqs, dkc, dkp, dvc, dvp = [], [], [], [], []
            for h in range(ATT_COLS // E):
                sl = slice(h * E, (h + 1) * E)
                qh, doh = qb[:, sl], dob[:, sl]
                lse_h, dl_h = lb[:, h * E:h * E + 1], dlb[:, h * E:h * E + 1]
                pc = jnp.where(mc, jnp.exp(_dot_nt(qh, kc[:, sl]) - lse_h), 0.0)
                pp = jnp.where(mp, jnp.exp(_dot_nt(qh, kp[:, sl]) - lse_h), 0.0)
                dsc = pc * (_dot_nt(doh, vc[:, sl]) - dl_h)
                dsp = pp * (_dot_nt(doh, vp[:, sl]) - dl_h)
                dqs.append(_dot(dsc, kc[:, sl]) + _dot(dsp, kp[:, sl]))
                dkc.append(_dot_tn(dsc, qh))
                dkp.append(_dot_tn(dsp, qh))
                dvc.append(_dot_tn(pc, doh))
                dvp.append(_dot_tn(pp, doh))
            dq_ref[cur, :] = jnp.concatenate(dqs, axis=1)
            dk_ref[cur, :] = dk_ref[cur, :] + jnp.concatenate(dkc, axis=1)
            dv_ref[cur, :] = dv_ref[cur, :] + jnp.concatenate(dvc, axis=1)
            dk_ref[prv, :] = dk_ref[prv, :] + jnp.concatenate(dkp, axis=1)
            dv_ref[prv, :] = dv_ref[prv, :] + jnp.concatenate(dvp, axis=1)
            return carry

        lax.fori_loop(0, nblk, step, 0, unroll=4)

    col = pl.BlockSpec((L, ATT_COLS), lambda c: (0, c))
    return pl.pallas_call(body, grid=(W // ATT_COLS,), in_specs=_qkv_specs(L, g) + [col] * 3, out_specs=[col] * 3,
                          out_shape=[jax.ShapeDtypeStruct((L, W), F32)] * 3,
                          compiler_params=_params(dimension_semantics=("parallel",)), name=name)(
        qkv, qkv, qkv, lse, do, dl)


def _merge_alpha(l_refs):
    ls = [r[...] for r in l_refs]
    m = jnp.maximum(jnp.maximum(ls[0], ls[1]), ls[2])
    es = [jnp.exp(l - m) for l in ls]
    den = es[0] + es[1] + es[2]
    return [e / den for e in es]


def merge_fwd(os_, ls_, *, name):
    L, W = os_[0].shape
    tr = _pick(L, (256, 128))

    def body(o0, o1, o2, l0, l1, l2, out_ref):
        al = _merge_alpha((l0, l1, l2))
        out_ref[...] = (al[0] * o0[...] + al[1] * o1[...] + al[2] * o2[...]).astype(out_ref.dtype)

    row = pl.BlockSpec((tr, W), lambda i: (i, 0))
    return pl.pallas_call(body, grid=(L // tr,), in_specs=[row] * 6, out_specs=row,
                          out_shape=jax.ShapeDtypeStruct((L, W), BF16), name=name)(*os_, *ls_)


def merge_bwd(os_, ls_, do, *, name):
    L, W = do.shape
    tr = _pick(L, (256, 128))

    def body(o0, o1, o2, l0, l1, l2, do_ref, d0, d1, d2, e0, e1, e2):
        al = _merge_alpha((l0, l1, l2))
        dov = do_ref[...]
        r = lax.broadcasted_iota(jnp.int32, (W, W), 0) // ATT_E
        c = lax.broadcasted_iota(jnp.int32, (W, W), 1) // ATT_E
        ones_blk = (r == c).astype(F32)
        t = jnp.zeros_like(dov)
        for a, o in zip(al, (o0, o1, o2)):
            t = t + a * _dot_tri(ones_blk, dov * o[...], tri_left=False)
        for a, d_ref, e_ref in zip(al, (d0, d1, d2), (e0, e1, e2)):
            d_ref[...] = a * dov
            e_ref[...] = a * t

    row = pl.BlockSpec((tr, W), lambda i: (i, 0))
    return pl.pallas_call(body, grid=(L // tr,), in_specs=[row] * 7, out_specs=[row] * 6,
                          out_shape=[jax.ShapeDtypeStruct((L, W), F32)] * 6, name=name)(*os_, *ls_, do)


def _me_and_peers():
    x, y, c = lax.axis_index("x"), lax.axis_index("y"), lax.axis_index("c")
    peers = []
    for k in range(1, N_DEV):
        px = 1 - x if k & 4 else x
        py = 1 - y if k & 2 else y
        pc = 1 - c if k & 1 else c
        peers.append((px, py, pc))
    return (x, y, c), peers


def _index(dev):
    return 4 * dev[0] + 2 * dev[1] + dev[2]


def _hbm(a):
    return pltpu.with_memory_space_constraint(a, pltpu.HBM)


HBM_SPEC = pl.BlockSpec(memory_space=pltpu.HBM)
SEM_SPEC = pl.BlockSpec(memory_space=pltpu.SEMAPHORE)
DATAFLOW = pltpu.SideEffectType.DATAFLOW_SIDE_EFFECTING


def _remote(src_ref, land_ref, slotted, me, peer, src_is_mine, send_sem, recv_sem, k):
    sender, receiver = (me, peer) if src_is_mine else (peer, me)
    src = src_ref.at[_index(receiver)] if slotted else src_ref
    return pltpu.make_async_remote_copy(src_ref=src, dst_ref=land_ref.at[_index(sender)], send_sem=send_sem.at[k],
                                        recv_sem=recv_sem.at[k], device_id=peer, device_id_type=MESH_ID)


SIBLING = 0
SAME_CORE = (1, 3, 5)
OTHER_CORE = (2, 4, 6)


def copies_start(arrays, mode, *, name):
    n = len(arrays)
    slotted = mode == "exchange"
    lands = [lax.empty(a.shape if slotted else (N_DEV,) + a.shape, a.dtype) for a in arrays]
    targets = (SIBLING,) + SAME_CORE if mode == "gather2" else tuple(range(N_DEV - 1))

    def body(*refs):
        x_refs, land_refs = refs[:n], refs[n:2 * n]
        send, recv = refs[2 * n:3 * n], refs[3 * n:4 * n]
        token = refs[-1]
        me, peers = _me_and_peers()
        for w in range(n):
            for k in targets:
                _remote(x_refs[w], land_refs[w], slotted, me, peers[k], True, send[w], recv[w], k).start()
            if not slotted:
                pltpu.make_async_copy(x_refs[w], land_refs[w].at[_index(me)], recv[w].at[N_DEV - 1]).start()
        token[...] = jnp.zeros_like(token)

    sem = pltpu.SemaphoreType.DMA((N_DEV,))
    out_shape = ([sem] * (2 * n) + [pltpu.HBM(a.shape, a.dtype) for a in arrays]
                 + [pltpu.HBM(l.shape, l.dtype) for l in lands] + [jax.ShapeDtypeStruct((8, 128), F32)])
    outs = pl.pallas_call(
        body, name=name, out_shape=out_shape, in_specs=[HBM_SPEC] * (2 * n),
        out_specs=[SEM_SPEC] * (2 * n) + [HBM_SPEC] * (2 * n) + [pl.BlockSpec(memory_space=pltpu.VMEM)],
        input_output_aliases={i: 2 * n + i for i in range(2 * n)},
        compiler_params=pltpu.CompilerParams(has_side_effects=DATAFLOW),
    )(*[_hbm(a) for a in arrays], *[_hbm(l) for l in lands])
    handles = [(outs[w], outs[n + w], outs[2 * n + w], outs[3 * n + w]) for w in range(n)]
    return handles, outs[-1]


def _forward(land_ref, me, peers, j, fsend, frecv, mine):
    block = _index(peers[SAME_CORE[j]] if mine else peers[OTHER_CORE[j]])
    return pltpu.make_async_remote_copy(src_ref=land_ref.at[block], dst_ref=land_ref.at[block], send_sem=fsend.at[j],
                                        recv_sem=frecv.at[j], device_id=peers[SIBLING], device_id_type=MESH_ID)


def copies_forward(handles, after, *, name):
    n = len(handles)

    def body(*refs):
        land_refs, recv = refs[:n], refs[n:2 * n]
        fsend, frecv = refs[2 * n + 1:3 * n + 1], refs[3 * n + 1:4 * n + 1]
        token = refs[-1]
        me, peers = _me_and_peers()
        for w in range(n):
            for j, k in enumerate(SAME_CORE):
                block = land_refs[w].at[_index(peers[k])]
                pltpu.make_async_remote_copy(src_ref=block, dst_ref=block, send_sem=recv[w].at[N_DEV - 1],
                                             recv_sem=recv[w].at[k], device_id=peers[k], device_id_type=MESH_ID).wait_recv()
                _forward(land_refs[w], me, peers, j, fsend[w], frecv[w], True).start()
        token[...] = jnp.zeros_like(token)

    sem = pltpu.SemaphoreType.DMA((len(SAME_CORE),))
    lands = [h[3] for h in handles]
    outs = pl.pallas_call(
        body, name=name,
        out_shape=[sem] * (2 * n) + [pltpu.HBM(l.shape, l.dtype) for l in lands] + [jax.ShapeDtypeStruct((8, 128), F32)],
        in_specs=[HBM_SPEC] * n + [SEM_SPEC] * n + [pl.BlockSpec(memory_space=pl.ANY)],
        out_specs=[SEM_SPEC] * (2 * n) + [HBM_SPEC] * n + [pl.BlockSpec(memory_space=pltpu.VMEM)],
        input_output_aliases={w: 2 * n + w for w in range(n)},
        compiler_params=pltpu.CompilerParams(has_side_effects=DATAFLOW),
    )(*lands, *[h[1] for h in handles], after)
    new = [(h[0], h[1], h[2], outs[2 * n + w], outs[w], outs[n + w]) for w, h in enumerate(handles)]
    return new, outs[-1]


def copies_wait(handle, mode, after, *, name):
    slotted = mode == "exchange"
    two_level = mode == "gather2"
    send_sem, recv_sem, x_thru, land_thru = handle[:4]
    targets = (SIBLING,) + SAME_CORE if two_level else tuple(range(N_DEV - 1))
    arrivals = (SIBLING,) if two_level else targets

    def body(x_ref, land_ref, send_ref, recv_ref, *rest):
        me, peers = _me_and_peers()
        for k in targets:
            _remote(x_ref, land_ref, slotted, me, peers[k], True, send_ref, recv_ref, k).wait_send()
        for k in arrivals:
            _remote(x_ref, land_ref, slotted, me, peers[k], False, send_ref, recv_ref, k).wait_recv()
        if not slotted:
            pltpu.make_async_copy(x_ref, land_ref.at[_index(me)], recv_ref.at[N_DEV - 1]).wait()
        if two_level:
            fsend, frecv = rest[0], rest[1]
            for j in range(len(SAME_CORE)):
                _forward(land_ref, me, peers, j, fsend, frecv, True).wait_send()
                _forward(land_ref, me, peers, j, fsend, frecv, False).wait_recv()

    extra = list(handle[4:])
    return pl.pallas_call(
        body, name=name, out_shape=(pltpu.HBM(x_thru.shape, x_thru.dtype), pltpu.HBM(land_thru.shape, land_thru.dtype)),
        in_specs=[HBM_SPEC, HBM_SPEC, SEM_SPEC, SEM_SPEC] + [SEM_SPEC] * len(extra) + [pl.BlockSpec(memory_space=pl.ANY)],
        out_specs=(HBM_SPEC, HBM_SPEC), input_output_aliases={0: 0, 1: 1},
        compiler_params=pltpu.CompilerParams(has_side_effects=DATAFLOW),
    )(x_thru, land_thru, send_sem, recv_sem, *extra, after)


def cast_bf16(x, *, ncols=None, name):
    R = x.shape[0]
    C = ncols or x.shape[1]
    tr = _pick(R, (512, 352, 256, 128, 64))

    def body(x_ref, o_ref):
        o_ref[...] = x_ref[...].astype(BF16)

    row = pl.BlockSpec((tr, C), lambda i: (i, 0))
    return pl.pallas_call(body, grid=(R // tr,), in_specs=[row], out_specs=row,
                          out_shape=jax.ShapeDtypeStruct((R, C), BF16), name=name)(x)


def cast_bf16_layer(x3, layer, *, name):
    _, R, C = x3.shape
    tr = _pick(R, (512, 352, 256, 128, 64))

    def body(x_ref, o_ref):
        o_ref[...] = x_ref[...].astype(BF16)

    return pl.pallas_call(body, grid=(R // tr,), in_specs=[pl.BlockSpec((None, tr, C), lambda i: (layer, i, 0))],
                          out_specs=pl.BlockSpec((tr, C), lambda i: (i, 0)),
                          out_shape=jax.ShapeDtypeStruct((R, C), BF16), name=name)(x3)


BD_PARTS = 4


def _blockdiag_call(b, build, G, r, c, name):
    gp = G // BD_PARTS

    def body_build(b_ref, o_ref):
        o_ref[...] = jnp.zeros_like(o_ref)
        for g in range(G):
            o_ref[g // gp, (g % gp) * r:(g % gp + 1) * r, (g % gp) * c:(g % gp + 1) * c] = b_ref[g]

    def body_extract(d_ref, o_ref):
        for g in range(G):
            o_ref[g] = d_ref[g // gp, (g % gp) * r:(g % gp + 1) * r, (g % gp) * c:(g % gp + 1) * c]

    out = jax.ShapeDtypeStruct((BD_PARTS, gp * r, gp * c) if build else (G, r, c), F32)
    return pl.pallas_call(body_build if build else body_extract, out_shape=out, name=name)(b)


def make_blockdiag(G, r, c, name):
    @jax.custom_vjp
    def blockdiag(b):
        return _blockdiag_call(b, True, G, r, c, name + "_build")

    def fwd(b):
        return blockdiag(b), None

    def bwd(_, g):
        return (_blockdiag_call(g, False, G, r, c, name + "_extract"),)

    blockdiag.defvjp(fwd, bwd)
    return blockdiag


def _my_index():
    return 4 * lax.axis_index("x") + 2 * lax.axis_index("y") + lax.axis_index("c")


def cols_from_shards(g, *, name):
    _, K, n = g.shape
    tk = _pick(K, (256, 128))

    def body(g_ref, o_ref):
        for i in range(N_DEV):
            o_ref[:, i * n:(i + 1) * n] = g_ref[i]

    return pl.pallas_call(body, grid=(K // tk,), in_specs=[pl.BlockSpec((N_DEV, tk, n), lambda i: (0, i, 0))],
                          out_specs=pl.BlockSpec((tk, N_DEV * n), lambda i: (i, 0)),
                          out_shape=jax.ShapeDtypeStruct((K, N_DEV * n), g.dtype), name=name)(g)


def shards_from_cols(w, *, name):
    K, N = w.shape
    n = N // N_DEV
    tk = _pick(K, (256, 128))

    def body(w_ref, o_ref):
        for i in range(N_DEV):
            o_ref[i] = w_ref[:, i * n:(i + 1) * n].astype(o_ref.dtype)

    return pl.pallas_call(body, grid=(K // tk,), in_specs=[pl.BlockSpec((tk, N), lambda i: (i, 0))],
                          out_specs=pl.BlockSpec((N_DEV, tk, n), lambda i: (0, i, 0)),
                          out_shape=jax.ShapeDtypeStruct((N_DEV, K, n), BF16), name=name)(w)


def _adamw(w, g, m, v):
    m = ADAM_B1 * m + (1.0 - ADAM_B1) * g
    v = ADAM_B2 * v + (1.0 - ADAM_B2) * (g * g)
    m_hat = m / (1.0 - ADAM_B1 ** ADAM_STEP)
    v_hat = v / (1.0 - ADAM_B2 ** ADAM_STEP)
    delta = -ADAM_LR * (m_hat / (jnp.sqrt(v_hat) + ADAM_EPS) + ADAM_WD * w)
    return delta, m, v


def reduce_adamw(recv, own, own_slotted, me, w, m, v, *, layer=0, n_layers=1, into=None, name):
    _, R, C = recv.shape
    tr = _pick(R, (352, 320, 288, 256, 128, 64, 32, 16, 8))
    off = layer * (R // tr)

    def body(me_ref, r_ref, own_ref, w_ref, m_ref, v_ref, *rest):
        g_ref, d_ref, nm_ref, nv_ref = rest[-4:]
        mine = me_ref[0]
        g = None
        for i in range(N_DEV):
            part = jnp.where(mine == i, own_ref[...], r_ref[i]).astype(F32)
            g = part if g is None else g + part
        delta, nm, nv = _adamw(w_ref[...], g, m_ref[...], v_ref[...])
        g_ref[...] = g
        d_ref[...] = delta
        nm_ref[...] = nm
        nv_ref[...] = nv

    row = pl.BlockSpec((tr, C), lambda i, me_ref: (i + off, 0))
    own_spec = (pl.BlockSpec((None, tr, C), lambda i, me_ref: (me_ref[0], i, 0)) if own_slotted
                else pl.BlockSpec((tr, C), lambda i, me_ref: (i, 0)))
    rest = [] if into is None else list(into)
    grid_spec = pltpu.PrefetchScalarGridSpec(
        num_scalar_prefetch=1, grid=(R // tr,),
        in_specs=[pl.BlockSpec((N_DEV, tr, C), lambda i, me_ref: (0, i, 0)), own_spec, row, row, row]
        + [pl.BlockSpec(memory_space=pl.ANY)] * len(rest),
        out_specs=[row] * 4)
    return pl.pallas_call(body, grid_spec=grid_spec, out_shape=[jax.ShapeDtypeStruct((n_layers * R, C), F32)] * 4,
                          input_output_aliases={6 + k: k for k in range(len(rest))},
                          compiler_params=_params(dimension_semantics=("parallel",)), name=name)(
        me.reshape(1).astype(jnp.int32), recv, own, w, m, v, *rest)


def _s5_prepare(A_re, A_im, log_dt, B_re, B_im, C_re, C_im):
    G, P, Cg = S5_GROUPS, S5_STATE, S5_GROUP
    dt = jnp.exp(log_dt)[:, None]
    mag = jnp.exp(A_re * dt)
    ab_re = mag * jnp.cos(A_im * dt)
    ab_im = mag * jnp.sin(A_im * dt)
    den = A_re * A_re + A_im * A_im
    nr, ni = ab_re - 1.0, ab_im
    c_re = (nr * A_re + ni * A_im) / den
    c_im = (ni * A_re - nr * A_im) / den
    Bb_re = c_re[..., None] * B_re - c_im[..., None] * B_im
    Bb_im = c_re[..., None] * B_im + c_im[..., None] * B_re
    def dense_in(b, name):
        return make_blockdiag(G, Cg, P, name)(b.transpose(0, 2, 1))

    def dense_out(c, name):
        return make_blockdiag(G, P, Cg, name)(c.transpose(0, 2, 1))

    return (ab_re.reshape(1, G * P), ab_im.reshape(1, G * P), dense_in(Bb_re, "s5_wb_re"), dense_in(Bb_im, "s5_wb_im"),
            dense_out(C_re, "s5_wc_re"), dense_out(-C_im, "s5_wc_im"))


def _lower_bound(gamma):
    return jnp.cumsum(jax.nn.softmax(gamma, axis=0), axis=0)[0:1]


def _ffn_fwd(h, g_norm, get_w_in, conv_w, conv_b, get_w_out, tag):
    hn = rms_fwd(h, g_norm, name=tag + "_rms")
    w_in = get_w_in(hn)
    hu = mm(hn, w_in, tb=True, name=tag + "_in")
    act = convgate_fwd(hu, conv_w, conv_b, name=tag + "_gate")
    w_out = get_w_out(act)
    h_out = mm(act, w_out, res=h, name=tag + "_out")
    return h_out, (hn, hu, act), w_in, w_out


def _ffn_bwd(h, g_norm, w_in, conv_w, conv_b, w_out, saved, dh, tag, send_dw_in, send_dw_out):
    hn, hu, act = saved
    sent = send_dw_out(mm(act, dh, ta=True, out_dtype=BF16, name=tag + "_dwout"))
    dact = mm(dh, w_out, tb=True, dep=sent, name=tag + "_dact")
    dhu, dconv_w, dconv_b = convgate_bwd(hu, conv_w, conv_b, dact, name=tag + "_dgate")
    sent = send_dw_in(mm(dhu, hn, ta=True, out_dtype=BF16, name=tag + "_dwin"))
    dh_in, dg = mm_drms(dhu, w_in, h, g_norm, dh, dep=sent, name=tag + "_dhn")
    return dh_in, dg, dconv_w, dconv_b


def kernel(x, positions, norm_mix, norm_ffn, norm_final, mix_w_in, mix_w_out, s5_A_re, s5_A_im, s5_log_dt, s5_B_re, s5_B_im, s5_C_re, s5_C_im, s5_D, s5_glu_w, s5_glu_b, hgrn_gamma, hgrn_norm, att_w_qkv, att_w_o, ffn_w_in, ffn_conv_w, ffn_conv_b, ffn_w_out, loss_target, m_norm_mix, m_norm_ffn, m_norm_final, m_mix_w_in, m_mix_w_out, m_s5_A_re, m_s5_A_im, m_s5_log_dt, m_s5_B_re, m_s5_B_im, m_s5_C_re, m_s5_C_im, m_s5_D, m_s5_glu_w, m_s5_glu_b, m_hgrn_gamma, m_hgrn_norm, m_att_w_qkv, m_att_w_o, m_ffn_w_in, m_ffn_conv_w, m_ffn_conv_b, m_ffn_w_out, v_norm_mix, v_norm_ffn, v_norm_final, v_mix_w_in, v_mix_w_out, v_s5_A_re, v_s5_A_im, v_s5_log_dt, v_s5_B_re, v_s5_B_im, v_s5_C_re, v_s5_C_im, v_s5_D, v_s5_glu_w, v_s5_glu_b, v_hgrn_gamma, v_hgrn_norm, v_att_w_qkv, v_att_w_o, v_ffn_w_in, v_ffn_conv_w, v_ffn_conv_b, v_ffn_w_out):
    W = dict(norm_mix=norm_mix, norm_ffn=norm_ffn, norm_final=norm_final, mix_w_in=mix_w_in, mix_w_out=mix_w_out,
             s5_A_re=s5_A_re, s5_A_im=s5_A_im, s5_log_dt=s5_log_dt, s5_B_re=s5_B_re, s5_B_im=s5_B_im,
             s5_C_re=s5_C_re, s5_C_im=s5_C_im, s5_D=s5_D, s5_glu_w=s5_glu_w, s5_glu_b=s5_glu_b,
             hgrn_gamma=hgrn_gamma, hgrn_norm=hgrn_norm, att_w_qkv=att_w_qkv, att_w_o=att_w_o, ffn_w_in=ffn_w_in,
             ffn_conv_w=ffn_conv_w, ffn_conv_b=ffn_conv_b, ffn_w_out=ffn_w_out)
    M = dict(norm_mix=m_norm_mix, norm_ffn=m_norm_ffn, norm_final=m_norm_final, mix_w_in=m_mix_w_in,
             mix_w_out=m_mix_w_out, s5_A_re=m_s5_A_re, s5_A_im=m_s5_A_im, s5_log_dt=m_s5_log_dt, s5_B_re=m_s5_B_re,
             s5_B_im=m_s5_B_im, s5_C_re=m_s5_C_re, s5_C_im=m_s5_C_im, s5_D=m_s5_D, s5_glu_w=m_s5_glu_w,
             s5_glu_b=m_s5_glu_b, hgrn_gamma=m_hgrn_gamma, hgrn_norm=m_hgrn_norm, att_w_qkv=m_att_w_qkv,
             att_w_o=m_att_w_o, ffn_w_in=m_ffn_w_in, ffn_conv_w=m_ffn_conv_w, ffn_conv_b=m_ffn_conv_b,
             ffn_w_out=m_ffn_w_out)
    V = dict(norm_mix=v_norm_mix, norm_ffn=v_norm_ffn, norm_final=v_norm_final, mix_w_in=v_mix_w_in,
             mix_w_out=v_mix_w_out, s5_A_re=v_s5_A_re, s5_A_im=v_s5_A_im, s5_log_dt=v_s5_log_dt, s5_B_re=v_s5_B_re,
             s5_B_im=v_s5_B_im, s5_C_re=v_s5_C_re, s5_C_im=v_s5_C_im, s5_D=v_s5_D, s5_glu_w=v_s5_glu_w,
             s5_glu_b=v_s5_glu_b, hgrn_gamma=v_hgrn_gamma, hgrn_norm=v_hgrn_norm, att_w_qkv=v_att_w_qkv,
             att_w_o=v_att_w_o, ffn_w_in=v_ffn_w_in, ffn_conv_w=v_ffn_conv_w, ffn_conv_b=v_ffn_conv_b,
             ffn_w_out=v_ffn_w_out)
    return _step(x[0], positions[0], loss_target[0], W, M, V)


TRANSPOSED = ("mix_w_in", "att_w_qkv", "ffn_w_in")
SMALL = ("norm_mix", "norm_ffn", "norm_final", "s5_A_re", "s5_A_im", "s5_log_dt", "s5_B_re", "s5_B_im", "s5_C_re",
         "s5_C_im", "s5_D", "s5_glu_b", "hgrn_gamma", "hgrn_norm", "ffn_conv_b")
ORDER = ("norm_mix", "norm_ffn", "norm_final", "mix_w_in", "mix_w_out", "s5_A_re", "s5_A_im", "s5_log_dt", "s5_B_re",
         "s5_B_im", "s5_C_re", "s5_C_im", "s5_D", "s5_glu_w", "s5_glu_b", "hgrn_gamma", "hgrn_norm", "att_w_qkv",
         "att_w_o", "ffn_w_in", "ffn_conv_w", "ffn_conv_b", "ffn_w_out")
PACK_COLS = 1024


def _step(x, positions, target, W, M, V):
    L, D = x.shape
    me = 4 * lax.axis_index("x") + 2 * lax.axis_index("y") + lax.axis_index("c")
    n_cw = W["ffn_conv_w"].shape[-1]
    T = {n: tuple(jnp.swapaxes(d[n], -1, -2) for d in (W, M, V)) for n in TRANSPOSED}
    shards = {
        "mix_w_in": cast_bf16(T["mix_w_in"][0][0], name="mix_w_in_cast"),
        "conv_w": W["ffn_conv_w"].reshape(6, n_cw),
        "s5_glu_w": cast_bf16(W["s5_glu_w"][0], name="s5_glu_w_cast"),
        "mix_w_out": cast_bf16(W["mix_w_out"][0], name="mix_w_out_cast"),
        "ffn_w_in0": cast_bf16_layer(T["ffn_w_in"][0], 0, name="ffn_w_in0_cast"),
        "ffn_w_out0": cast_bf16_layer(W["ffn_w_out"], 0, name="ffn_w_out0_cast"),
        "att_w_qkv": cast_bf16(T["att_w_qkv"][0][0], name="att_w_qkv_cast"),
        "att_w_o": cast_bf16(W["att_w_o"][0], name="att_w_o_cast"),
        "ffn_w_in1": cast_bf16_layer(T["ffn_w_in"][0], 1, name="ffn_w_in1_cast"),
        "ffn_w_out1": cast_bf16_layer(W["ffn_w_out"], 1, name="ffn_w_out1_cast"),
    }
    gather_handles, token = copies_start(list(shards.values()), "gather2", name="gather_start")
    gather_handle = dict(zip(shards, gather_handles))

    def forward(keys, after, name):
        new, sent = copies_forward([gather_handle[k] for k in keys], after, name=name)
        gather_handle.update(zip(keys, new))
        return sent

    def gathered(key, after, cols):
        _, land = copies_wait(gather_handle[key], "gather2", after, name=key + "_gwait")
        return cols_from_shards(land, name=key + "_asm") if cols else land.reshape(-1, land.shape[-1])

    conv_b = W["ffn_conv_b"].reshape(2, 1, -1)

    s5_params = (W["s5_A_re"][0], W["s5_A_im"][0], W["s5_log_dt"][0], W["s5_B_re"][0], W["s5_B_im"][0],
                 W["s5_C_re"][0], W["s5_C_im"][0])
    (a_re, a_im, wb_re, wb_im, wc_re, wc_im), s5_prep_vjp = jax.vjp(_s5_prepare, *s5_params)
    dvec = W["s5_D"].reshape(1, S5_WIDTH)
    glu_b = W["s5_glu_b"].reshape(1, S5_WIDTH)
    lb, lb_vjp = jax.vjp(_lower_bound, W["hgrn_gamma"])
    hg_norm = W["hgrn_norm"].reshape(1, -1)
    tabs = rope_tables(positions)

    hn0 = rms_fwd(x, W["norm_mix"][0], dep=token, name="l0_rms")
    forward(["mix_w_in", "conv_w", "s5_glu_w"], hn0, "forward_a")
    w_mix_in = gathered("mix_w_in", hn0, False)
    proj = mm(hn0, w_mix_in, tb=True, name="l0_proj")
    y0, xs_re, xs_im = s5_core_fwd(proj, a_re, a_im, wb_re, wb_im, wc_re, wc_im, name="s5_core")
    w_glu = gathered("s5_glu_w", y0, False)
    oa = s5_out_fwd(y0, proj, dvec, w_glu, glu_b, name="s5_out")
    ob, hg_states = hgrn_fwd(proj, lb, hg_norm, name="hgrn_fwd")
    forward(["mix_w_out", "ffn_w_in0"], ob, "forward_b")
    cat = jnp.concatenate([oa, ob], axis=1)
    w_mix_out = gathered("mix_w_out", cat, False)
    h1 = mm(cat, w_mix_out, res=x, name="l0_mix_out")
    _, cw_all = copies_wait(gather_handle["conv_w"], "gather2", h1, name="conv_w_gwait")
    conv_w = cw_all.transpose(1, 0, 2).reshape(2, 3, N_DEV * n_cw)
    w_ffn_in, w_ffn_out = [None, None], [None, None]
    h2, ffn0_saved, w_ffn_in[0], w_ffn_out[0] = _ffn_fwd(
        h1, W["norm_ffn"][0], lambda a: gathered("ffn_w_in0", a, False), conv_w[0], conv_b[0],
        lambda a: (forward(["ffn_w_out0"], a, "forward_c"), gathered("ffn_w_out0", a, False))[1], "ffn0")

    forward(["att_w_qkv", "att_w_o"], h2, "forward_d")
    hn2 = rms_fwd(h2, W["norm_mix"][1], name="l1_rms")
    w_qkv = gathered("att_w_qkv", hn2, False)
    qkv_r = qkv_rope(hn2, w_qkv, tabs, name="l1_qkv")
    att_o, att_l = [], []
    for g, d in enumerate(ATT_DILATIONS):
        o_g, l_g = attn_fwd(qkv_r, g, d, name=f"attn_fwd{g}")
        att_o.append(o_g)
        att_l.append(l_g)
    o_att = merge_fwd(att_o, att_l, name="merge_fwd")
    forward(["ffn_w_in1", "ffn_w_out1"], o_att, "forward_e")
    w_o = gathered("att_w_o", o_att, True)
    h3 = mm(o_att, w_o, res=h2, name="l1_mix_out")
    h4, ffn1_saved, w_ffn_in[1], w_ffn_out[1] = _ffn_fwd(
        h3, W["norm_ffn"][1], lambda a: gathered("ffn_w_in1", a, False), conv_w[1], conv_b[1],
        lambda a: gathered("ffn_w_out1", a, False), "ffn1")

    exchanges = {}

    def send_grad(key, g, cols):
        if cols:
            parts = shards_from_cols(g, name=key + "_split")
        else:
            parts = g.reshape(N_DEV, g.shape[0] // N_DEV, g.shape[1])
        (handle,), sent = copies_start([parts], "exchange", name=key + "_xstart")
        exchanges[key] = handle
        return sent

    loss, dh4, dg_final = final_loss(h4, W["norm_final"], target, name="final_loss")
    dh3, dg_ffn1, dcw1, dcb1 = _ffn_bwd(h3, W["norm_ffn"][1], w_ffn_in[1], conv_w[1], conv_b[1], w_ffn_out[1],
                                        ffn1_saved, dh4, "ffn1", lambda g: send_grad("ffn_w_in1", g, False),
                                        lambda g: send_grad("ffn_w_out1", g, False))
    sent = send_grad("att_w_o", mm(o_att, dh3, ta=True, name="l1_dwo"), True)
    d_oatt = mm(dh3, w_o, tb=True, dep=sent, name="l1_dmix")
    mb = merge_bwd(att_o, att_l, d_oatt, name="merge_bwd")
    d_slabs = [attn_bwd(qkv_r, g, att_l[g], mb[g], mb[3 + g], d, name=f"attn_bwd{g}")
               for g, d in enumerate(ATT_DILATIONS)]
    d_qkv = rope_bwd([s[0] for s in d_slabs] + [s[1] for s in d_slabs] + [s[2] for s in d_slabs], tabs,
                     name="rope_bwd")
    sent = send_grad("att_w_qkv", mm(d_qkv, hn2, ta=True, out_dtype=BF16, name="l1_dwqkv"), False)
    dh2, dg_mix1 = mm_drms(d_qkv, w_qkv, h2, W["norm_mix"][1], dh3, dep=sent, name="l1_dhn")

    dh1, dg_ffn0, dcw0, dcb0 = _ffn_bwd(h1, W["norm_ffn"][0], w_ffn_in[0], conv_w[0], conv_b[0], w_ffn_out[0],
                                        ffn0_saved, dh2, "ffn0", lambda g: send_grad("ffn_w_in0", g, False),
                                        lambda g: send_grad("ffn_w_out0", g, False))
    sent = send_grad("mix_w_out", mm(cat, dh1, ta=True, out_dtype=BF16, name="l0_dwout"), False)
    dcat = mm(dh1, w_mix_out, tb=True, dep=sent, name="l0_dcat")
    d_hg, dlb, dhg_norm = hgrn_bwd(proj, lb, hg_norm, hg_states, dcat, name="hgrn_bwd")
    dy, du_d, z_bf, dzg, dglu_b, dD = s5_out_bwd(y0, proj, dvec, w_glu, glu_b, dcat, name="s5_dout")
    sent_glu = send_grad("s5_glu_w", mm(z_bf, dzg, ta=True, out_dtype=BF16, name="s5_dglu"), False)
    du, dwb_re, dwb_im, dwc_re, dwc_im, da_re, da_im = s5_core_bwd(
        dy, du_d, proj, xs_re, xs_im, a_re, a_im, wb_re, wb_im, wc_re, wc_im, name="s5_dcore")
    s5_small = s5_prep_vjp((da_re, da_im, dwb_re, dwb_im, dwc_re, dwc_im))
    d_proj = jnp.concatenate([du, d_hg], axis=1)
    sent = send_grad("mix_w_in", mm(d_proj, hn0, ta=True, out_dtype=BF16, dep=sent_glu, name="l0_dwin"), False)
    grad_x, dg_mix0 = mm_drms(d_proj, w_mix_in, x, W["norm_mix"][0], dh1, dep=sent, name="l0_dhn")
    (d_gamma,) = lb_vjp(dlb)
    out = {}

    dA_re, dA_im, dlog_dt, dB_re, dB_im, dC_re, dC_im = s5_small
    small_g = dict(norm_mix=jnp.concatenate([dg_mix0, dg_mix1], axis=0), norm_ffn=jnp.concatenate([dg_ffn0, dg_ffn1], axis=0),
                   norm_final=dg_final, s5_A_re=dA_re, s5_A_im=dA_im, s5_log_dt=dlog_dt, s5_B_re=dB_re, s5_B_im=dB_im,
                   s5_C_re=dC_re, s5_C_im=dC_im, s5_D=dD, s5_glu_b=dglu_b, hgrn_gamma=d_gamma, hgrn_norm=dhg_norm,
                   ffn_conv_b=jnp.concatenate([dcb0, dcb1], axis=0))
    conv_w_g = jnp.stack([dcw0, dcw1], axis=0)
    sizes = [math.prod(W[n].shape) for n in SMALL]
    n_conv = conv_w_g.size
    total = sum(sizes) + n_conv + 1
    rows = -(-total // PACK_COLS)
    rows = -(-rows // 8) * 8
    pad = rows * PACK_COLS - total

    def pack(vals, conv_part, last):
        flat = [v.reshape(-1).astype(F32) for v in vals] + [conv_part.reshape(-1), last.reshape(-1),
                                                            jnp.zeros((pad,), F32)]
        return jnp.concatenate(flat).reshape(rows, PACK_COLS)

    def conv_full(shard):
        col_owner = lax.broadcasted_iota(jnp.int32, (2, 3, N_DEV * n_cw), 2) // n_cw
        return jnp.where(col_owner == me, jnp.tile(shard, (1, 1, N_DEV)), 0.0)

    zero1 = jnp.zeros((1,), F32)
    g_pack = pack([small_g[n] for n in SMALL], conv_w_g, loss)
    w_pack = pack([W[n] for n in SMALL], conv_full(W["ffn_conv_w"]), zero1)
    m_pack = pack([M[n] for n in SMALL], conv_full(M["ffn_conv_w"]), zero1)
    v_pack = pack([V[n] for n in SMALL], conv_full(V["ffn_conv_w"]), zero1 + 1.0)
    (small_handle,), small_sent = copies_start([g_pack], "gather", name="small_xstart")

    def finish(name, n_layers):
        w3, m3, v3 = T[name] if name in TRANSPOSED else (W[name], M[name], V[name])
        res = None
        for layer in reversed(range(n_layers)):
            key = name if n_layers == 1 else f"{name}{layer}"
            own, recv = copies_wait(exchanges[key], "exchange", small_sent, name=key + "_xwait")
            _, R, Cn = recv.shape
            res = reduce_adamw(recv, own, True, me, w3.reshape(n_layers * R, Cn), m3.reshape(n_layers * R, Cn),
                               v3.reshape(n_layers * R, Cn), layer=layer, n_layers=n_layers, into=res,
                               name=key + "_adamw")
        res = [r.reshape(w3.shape) for r in res]
        return tuple(jnp.swapaxes(r, -1, -2) for r in res) if name in TRANSPOSED else tuple(res)

    for name in ("ffn_w_out", "ffn_w_in"):
        out[name] = finish(name, 2)
    for name in ("att_w_o", "att_w_qkv", "mix_w_out", "s5_glu_w", "mix_w_in"):
        out[name] = finish(name, 1)

    small_own, small_recv = copies_wait(small_handle, "gather", out["s5_glu_w"][0], name="small_xwait")
    res = reduce_adamw(small_recv, small_own, False, me, w_pack, m_pack, v_pack, name="small_adamw")
    flat = [r.reshape(-1) for r in res]
    off = 0
    for n, sz in zip(SMALL, sizes):
        out[n] = tuple(f[off:off + sz].reshape(W[n].shape) for f in flat)
        off += sz
    conv_res = [f[off:off + n_conv].reshape(2, 3, N_DEV * n_cw) for f in flat]
    out["ffn_conv_w"] = tuple(lax.dynamic_slice(c, (0, 0, me * n_cw), (2, 3, n_cw)) for c in conv_res)
    off += n_conv
    loss_total = flat[0][off]

    result = [loss_total, grad_x[None]]
    for k in range(4):
        result += [out[n][k] for n in ORDER]
    return tuple(result)
```

```python
import functools
import math

import jax
import jax.numpy as jnp
from jax import lax
from jax.experimental import pallas as pl
from jax.experimental.pallas import tpu as pltpu

F32 = jnp.float32
BF16 = jnp.bfloat16
MESH_ID = pl.DeviceIdType.MESH
N_DEV = 8
VMEM_LIMIT_BYTES = 56 * 1024 * 1024

NORM_EPS = 1e-6
S5_WIDTH, S5_GROUP, S5_GROUPS, S5_STATE = 512, 16, 32, 64
HG_HEADS, HG_DIM, HG_CHUNK = 4, 128, 64
HG_STEP_CHUNKS = 4
ATT_E, ATT_HPG, ATT_BLOCK = 64, 8, 128
ATT_DILATIONS = (1, 4, 16)
ROT_DIM, ROPE_THETA = 16, 500000.0
D_FF = 2816
ADAM_LR, ADAM_B1, ADAM_B2, ADAM_EPS, ADAM_WD, ADAM_STEP = 0.001, 0.9, 0.999, 1e-08, 0.01, 10
NEG_BIG = -1e30


def _params(**kw):
    return pltpu.CompilerParams(vmem_limit_bytes=VMEM_LIMIT_BYTES, **kw)


def _pick(n, cands):
    for c in cands:
        if n % c == 0:
            return c
    return n


def _dot(a, b):
    return jnp.dot(a.astype(BF16), b.astype(BF16), preferred_element_type=F32)


def _dot_nt(a, b):
    return lax.dot_general(a.astype(BF16), b.astype(BF16), (((1,), (1,)), ((), ())), preferred_element_type=F32)


def _dot_tn(a, b):
    return lax.dot_general(a.astype(BF16), b.astype(BF16), (((0,), (0,)), ((), ())), preferred_element_type=F32)


def _split2(x):
    hi = x.astype(BF16)
    return hi, (x - hi.astype(F32)).astype(BF16)


def _dot_x3(a, b, contract=((1,), (0,))):
    dn = (contract, ((), ()))
    a1, a2 = _split2(a)
    b1, b2 = _split2(b)
    return (lax.dot_general(a1, b1, dn, preferred_element_type=F32) + lax.dot_general(a1, b2, dn, preferred_element_type=F32)
            + lax.dot_general(a2, b1, dn, preferred_element_type=F32))


def _sigmoid(x):
    return 1.0 / (1.0 + jnp.exp(-x))


V7X_HBM_BYTES_PER_S = 3.2e12
V7X_MXU_FLOPS_PER_S = 0.7e15
GRID_STEP_S = 0.35e-6
MM_VMEM_BUDGET = 40 * 1024 * 1024


def _divisors(n, cands):
    return [c for c in cands if c <= n and n % c == 0] or [n]


def _mm_tiles(m, n, k, sa, sb, so, sr):
    best = None
    for tm in _divisors(m, (2816, 2048, 1408, 1024, 512, 256, 128)):
        for tn in _divisors(n, (2816, 2048, 1408, 1024, 512, 256, 128)):
            for tk in _divisors(k, (k, 2816, 2560, 2304, 2048, 1536, 1408, 1280, 1024, 512, 256, 128)):
                nk = k // tk
                vmem = 2 * (tm * tk * sa + tk * tn * sb + tm * tn * (so + sr)) + (tm * tn * 4 if nk > 1 else 0)
                vmem += tm * tk * 2 * (sa > 2) + tk * tn * 2 * (sb > 2) + tm * tn * 4
                if vmem > MM_VMEM_BUDGET:
                    continue
                ni, nj = m // tm, n // tn
                for i_outer in (True, False):
                    if i_outer:
                        a_reads = 1 if nk == 1 else nj
                        b_reads = 1 if (nk == 1 and nj == 1) else ni
                    else:
                        b_reads = 1 if nk == 1 else ni
                        a_reads = 1 if (nk == 1 and ni == 1) else nj
                    traffic = a_reads * m * k * sa + b_reads * k * n * sb + m * n * (so + sr)
                    t = max(traffic / V7X_HBM_BYTES_PER_S, 2.0 * m * n * k / V7X_MXU_FLOPS_PER_S)
                    t += ni * nj * nk * GRID_STEP_S
                    t += (tm * tk * sa + tk * tn * sb + tm * tn * so) / V7X_HBM_BYTES_PER_S
                    if best is None or t < best[0]:
                        best = (t, tm, tn, tk, i_outer)
    assert best is not None, (m, n, k)
    return best[1:]


def mm(a, b, *, ta=False, tb=False, res=None, out_dtype=F32, dep=None, name):
    m, k = (a.shape[1], a.shape[0]) if ta else a.shape
    n = b.shape[0] if tb else b.shape[1]
    assert (b.shape[1] if tb else b.shape[0]) == k
    has_res = res is not None
    tm, tn, tk, i_outer = _mm_tiles(m, n, k, a.dtype.itemsize, b.dtype.itemsize, jnp.dtype(out_dtype).itemsize,
                                    res.dtype.itemsize if has_res else 0)
    nk = k // tk
    deps = [] if dep is None else [dep]
    dn = (((0 if ta else 1,), (1 if tb else 0,)), ((), ()))

    def body_single(*refs):
        a_ref, b_ref = refs[:2]
        o_ref = refs[-1]
        out = lax.dot_general(a_ref[...].astype(BF16), b_ref[...].astype(BF16), dn, preferred_element_type=F32)
        if has_res:
            out = out + refs[2][...].astype(F32)
        o_ref[...] = out.astype(o_ref.dtype)

    def body(*refs):
        a_ref, b_ref = refs[:2]
        r_ref = refs[2] if has_res else None
        o_ref, acc_ref = refs[-2:]
        kk = pl.program_id(2)
        part = lax.dot_general(a_ref[...].astype(BF16), b_ref[...].astype(BF16), dn, preferred_element_type=F32)

        @pl.when(kk == 0)
        def _():
            acc_ref[...] = part

        @pl.when(kk > 0)
        def _():
            acc_ref[...] += part

        @pl.when(kk == nk - 1)
        def _():
            out = acc_ref[...]
            if has_res:
                out = out + r_ref[...].astype(F32)
            o_ref[...] = out.astype(o_ref.dtype)

    def ij(f):
        return (lambda g0, g1, q: f(g0, g1, q)) if i_outer else (lambda g0, g1, q: f(g1, g0, q))

    a_spec = pl.BlockSpec((tk, tm), ij(lambda i, j, q: (q, i))) if ta else pl.BlockSpec((tm, tk), ij(lambda i, j, q: (i, q)))
    b_spec = pl.BlockSpec((tn, tk), ij(lambda i, j, q: (j, q))) if tb else pl.BlockSpec((tk, tn), ij(lambda i, j, q: (q, j)))
    o_spec = pl.BlockSpec((tm, tn), ij(lambda i, j, q: (i, j)))
    in_specs = [a_spec, b_spec] + ([o_spec] if has_res else []) + [pl.BlockSpec((8, 128), lambda g0, g1, q: (0, 0))] * len(deps)
    args = (a, b) + ((res,) if has_res else ()) + tuple(deps)
    grid = (m // tm, n // tn, nk) if i_outer else (n // tn, m // tm, nk)
    return pl.pallas_call(
        body_single if nk == 1 else body, grid=grid, in_specs=in_specs, out_specs=o_spec,
        out_shape=jax.ShapeDtypeStruct((m, n), out_dtype),
        scratch_shapes=[] if nk == 1 else [pltpu.VMEM((tm, tn), F32)],
        compiler_params=_params(dimension_semantics=("parallel", "parallel", "arbitrary")), name=name,
    )(*args)


def mm_drms(dy_in, w, x, g, dres, *, dep=None, name):
    m, k = dy_in.shape
    D = w.shape[1]
    tm = _pick(m, (512, 256, 128))
    tk = max(_divisors(k, (1536, 1408, 1280, 1024, 512, 256, 128)))
    nk = k // tk
    deps = [] if dep is None else [dep]

    def body(a_ref, b_ref, x_ref, g_ref, dres_ref, *rest):
        dx_ref, dg_ref, acc_ref = rest[-3:]
        i, q = pl.program_id(0), pl.program_id(1)
        part = jnp.dot(a_ref[...], b_ref[...], preferred_element_type=F32)

        @pl.when(q == 0)
        def _():
            acc_ref[...] = part

        @pl.when(q > 0)
        def _():
            acc_ref[...] += part

        @pl.when((i == 0) & (q == 0))
        def _():
            dg_ref[...] = jnp.zeros_like(dg_ref)

        @pl.when(q == nk - 1)
        def _():
            dyv = acc_ref[...]
            xv = x_ref[...]
            r = lax.rsqrt(jnp.mean(xv * xv, axis=-1, keepdims=True) + NORM_EPS)
            xh = xv * r
            dg_ref[...] += jnp.sum(dyv * xh, axis=0, keepdims=True)
            dxh = dyv * g_ref[...]
            dx_ref[...] = dres_ref[...] + r * (dxh - xh * jnp.mean(dxh * xh, axis=-1, keepdims=True))

    row = pl.BlockSpec((tm, D), lambda i, q: (i, 0))
    vec = pl.BlockSpec((1, D), lambda i, q: (0, 0))
    in_specs = [pl.BlockSpec((tm, tk), lambda i, q: (i, q)), pl.BlockSpec((tk, D), lambda i, q: (q, 0)), row, vec, row]
    in_specs += [pl.BlockSpec((8, 128), lambda i, q: (0, 0))] * len(deps)
    return pl.pallas_call(
        body, grid=(m // tm, nk), in_specs=in_specs, out_specs=[row, vec],
        out_shape=[jax.ShapeDtypeStruct((m, D), F32), jax.ShapeDtypeStruct((1, D), F32)],
        scratch_shapes=[pltpu.VMEM((tm, D), F32)],
        compiler_params=_params(dimension_semantics=("arbitrary", "arbitrary")), name=name,
    )(dy_in, w, x, g.reshape(1, D), dres, *deps)


def rms_fwd(x, g, *, dep=None, name):
    L, D = x.shape
    tr = _pick(L, (256, 128))

    def body(x_ref, g_ref, *rest):
        o_ref = rest[-1]
        xv = x_ref[...]
        r = lax.rsqrt(jnp.mean(xv * xv, axis=-1, keepdims=True) + NORM_EPS)
        o_ref[...] = (xv * r * g_ref[...]).astype(o_ref.dtype)

    row = pl.BlockSpec((tr, D), lambda i: (i, 0))
    vec = pl.BlockSpec((1, D), lambda i: (0, 0))
    deps = [] if dep is None else [dep]
    return pl.pallas_call(body, grid=(L // tr,), in_specs=[row, vec] + [pl.BlockSpec((8, 128), lambda i: (0, 0))] * len(deps),
                          out_specs=row, out_shape=jax.ShapeDtypeStruct((L, D), BF16), name=name)(
        x, g.reshape(1, D), *deps)


def final_loss(h, g, target, *, name):
    L, D = h.shape
    tr = _pick(L, (256, 128))

    def body(x_ref, g_ref, t_ref, loss_ref, dx_ref, dg_ref):
        xv = x_ref[...]
        gv = g_ref[...]
        r = lax.rsqrt(jnp.mean(xv * xv, axis=-1, keepdims=True) + NORM_EPS)
        xh = xv * r
        err = xh * gv - t_ref[...]

        @pl.when(pl.program_id(0) == 0)
        def _():
            dg_ref[...] = jnp.zeros_like(dg_ref)
            loss_ref[...] = jnp.zeros_like(loss_ref)

        loss_ref[...] += 0.5 * jnp.sum(jnp.mean(err * err, axis=-1, keepdims=True), axis=0, keepdims=True)
        dyv = err * (1.0 / D)
        dg_ref[...] += jnp.sum(dyv * xh, axis=0, keepdims=True)
        dxh = dyv * gv
        dx_ref[...] = r * (dxh - xh * jnp.mean(dxh * xh, axis=-1, keepdims=True))

    row = pl.BlockSpec((tr, D), lambda i: (i, 0))
    vec = pl.BlockSpec((1, D), lambda i: (0, 0))
    one = pl.BlockSpec((1, 1), lambda i: (0, 0))
    return pl.pallas_call(body, grid=(L // tr,), in_specs=[row, vec, row], out_specs=[one, row, vec],
                          out_shape=[jax.ShapeDtypeStruct((1, 1), F32), jax.ShapeDtypeStruct((L, D), F32),
                                     jax.ShapeDtypeStruct((1, D), F32)],
                          compiler_params=_params(dimension_semantics=("arbitrary",)), name=name)(
        h, g.reshape(1, D), target)


def _cmul(ar, ai, br, bi):
    return ar * br - ai * bi, ar * bi + ai * br


def _powers(ar, ai):
    rows = [(ar, ai)]
    for _ in range(7):
        rows.append(_cmul(rows[-1][0], rows[-1][1], ar, ai))
    table = (jnp.concatenate([r[0] for r in rows], axis=0), jnp.concatenate([r[1] for r in rows], axis=0))
    return (rows[0], rows[1], rows[3]), table


def _block_scan(br, bi, steps, shift):
    yr, yi = br, bi
    for s, (pr, pi) in zip((1, 2, 4), steps):
        sr, si = shift(yr, s), shift(yi, s)
        yr, yi = yr + pr * sr - pi * si, yi + pr * si + pi * sr
    return yr, yi


def s5_core_fwd(proj, a_re, a_im, wb_re, wb_im, wc_re, wc_im, *, name):
    L = proj.shape[0]
    parts, cu, W = wb_re.shape

    def body(u_ref, ar_ref, ai_ref, wbr_ref, wbi_ref, wcr_ref, wci_ref, y_ref, xr_ref, xi_ref, br_ref, bi_ref):
        u = u_ref[...]
        br_ref[...] = _dot(u, wbr_ref[...])
        bi_ref[...] = _dot(u, wbi_ref[...])
        steps, (tr, ti) = _powers(ar_ref[...], ai_ref[...])
        row = lax.broadcasted_iota(jnp.int32, (8, W), 0)

        def shift(y, s):
            return jnp.where(row >= s, pltpu.roll(y, s, 0), 0.0)

        def step(t8, carry):
            cr, ci = carry
            base = pl.multiple_of(t8 * 8, 8)
            yr, yi = _block_scan(br_ref[pl.ds(base, 8), :], bi_ref[pl.ds(base, 8), :], steps, shift)
            xr = yr + tr * cr - ti * ci
            xi = yi + tr * ci + ti * cr
            xr_ref[pl.ds(base, 8), :] = xr
            xi_ref[pl.ds(base, 8), :] = xi
            return jnp.broadcast_to(xr[7:8, :], (8, W)), jnp.broadcast_to(xi[7:8, :], (8, W))

        zero = jnp.zeros((8, W), F32)
        lax.fori_loop(0, L // 8, step, (zero, zero), unroll=2)
        y_ref[...] = _dot(xr_ref[...], wcr_ref[...]) + _dot(xi_ref[...], wci_ref[...])

    ucol = pl.BlockSpec((L, cu), lambda t: (0, t))
    vec = pl.BlockSpec((1, W), lambda t: (0, t))
    col = pl.BlockSpec((L, W), lambda t: (0, t))
    wb = pl.BlockSpec((None, cu, W), lambda t: (t, 0, 0))
    wc = pl.BlockSpec((None, W, cu), lambda t: (t, 0, 0))
    return pl.pallas_call(body, grid=(parts,), in_specs=[ucol, vec, vec, wb, wb, wc, wc], out_specs=[ucol, col, col],
                          out_shape=[jax.ShapeDtypeStruct((L, parts * cu), F32)]
                          + [jax.ShapeDtypeStruct((L, parts * W), F32)] * 2,
                          scratch_shapes=[pltpu.VMEM((L, W), F32)] * 2,
                          compiler_params=_params(dimension_semantics=("parallel",)), name=name)(
        proj, a_re, a_im, wb_re, wb_im, wc_re, wc_im)


def s5_core_bwd(dy, du_d, proj, xs_re, xs_im, a_re, a_im, wb_re, wb_im, wc_re, wc_im, *, name):
    L = proj.shape[0]
    parts, cu, W = wb_re.shape

    def body(dy_ref, dud_ref, u_ref, xr_ref, xi_ref, ar_ref, ai_ref, wbr_ref, wbi_ref, wcr_ref, wci_ref,
             du_ref, dwbr_ref, dwbi_ref, dwcr_ref, dwci_ref, dar_ref, dai_ref, lr_ref, li_ref):
        dy = dy_ref[...]
        lr_ref[...] = _dot_nt(dy, wcr_ref[...])
        li_ref[...] = _dot_nt(dy, wci_ref[...])
        dwcr_ref[...] = _dot_tn(xr_ref[...], dy)
        dwci_ref[...] = _dot_tn(xi_ref[...], dy)
        ar, ai = ar_ref[...], -ai_ref[...]
        steps, (tr, ti) = _powers(ar, ai)
        tr = jnp.concatenate([tr[j:j + 1, :] for j in range(7, -1, -1)], axis=0)
        ti = jnp.concatenate([ti[j:j + 1, :] for j in range(7, -1, -1)], axis=0)
        row8 = lax.broadcasted_iota(jnp.int32, (8, W), 0)
        nblk = L // 8

        def shift(y, s):
            return jnp.where(row8 < 8 - s, pltpu.roll(y, 8 - s, 0), 0.0)

        def step(s, carry):
            cr, ci = carry
            base = pl.multiple_of((nblk - 1 - s) * 8, 8)
            yr, yi = _block_scan(lr_ref[pl.ds(base, 8), :], li_ref[pl.ds(base, 8), :], steps, shift)
            lr = yr + tr * cr - ti * ci
            li = yi + tr * ci + ti * cr
            lr_ref[pl.ds(base, 8), :] = lr
            li_ref[pl.ds(base, 8), :] = li
            return jnp.broadcast_to(lr[0:1, :], (8, W)), jnp.broadcast_to(li[0:1, :], (8, W))

        zero = jnp.zeros((8, W), F32)
        lax.fori_loop(0, nblk, step, (zero, zero), unroll=2)
        row = lax.broadcasted_iota(jnp.int32, (L, W), 0)
        xpr = jnp.where(row >= 1, pltpu.roll(xr_ref[...], 1, 0), 0.0)
        xpi = jnp.where(row >= 1, pltpu.roll(xi_ref[...], 1, 0), 0.0)
        lr, li = lr_ref[...], li_ref[...]
        dar_ref[...] = jnp.sum(lr * xpr + li * xpi, axis=0, keepdims=True)
        dai_ref[...] = jnp.sum(li * xpr - lr * xpi, axis=0, keepdims=True)
        u = u_ref[...]
        dwbr_ref[...] = _dot_tn(u, lr)
        dwbi_ref[...] = _dot_tn(u, li)
        du_ref[...] = (dud_ref[...] + _dot_nt(lr, wbr_ref[...]) + _dot_nt(li, wbi_ref[...])).astype(du_ref.dtype)

    ucol = pl.BlockSpec((L, cu), lambda t: (0, t))
    vec = pl.BlockSpec((1, W), lambda t: (0, t))
    col = pl.BlockSpec((L, W), lambda t: (0, t))
    wb = pl.BlockSpec((None, cu, W), lambda t: (t, 0, 0))
    wc = pl.BlockSpec((None, W, cu), lambda t: (t, 0, 0))
    return pl.pallas_call(
        body, grid=(parts,), in_specs=[ucol, ucol, ucol, col, col, vec, vec, wb, wb, wc, wc],
        out_specs=[ucol, wb, wb, wc, wc, vec, vec],
        out_shape=[jax.ShapeDtypeStruct((L, parts * cu), BF16)] + [jax.ShapeDtypeStruct((parts, cu, W), F32)] * 2
        + [jax.ShapeDtypeStruct((parts, W, cu), F32)] * 2 + [jax.ShapeDtypeStruct((1, parts * W), F32)] * 2,
        scratch_shapes=[pltpu.VMEM((L, W), F32)] * 2,
        compiler_params=_params(dimension_semantics=("parallel",)), name=name,
    )(dy, du_d, proj, xs_re, xs_im, a_re, a_im, wb_re, wb_im, wc_re, wc_im)


def _gelu(y):
    c = math.sqrt(2.0 / math.pi)
    t = jnp.tanh(c * (y + 0.044715 * y * y * y))
    return 0.5 * y * (1.0 + t), t


def s5_out_fwd(y0, proj, dvec, glu_w, glu_b, *, name):
    L, C = y0.shape
    tr = _pick(L, (256, 128))

    def body(y_ref, u_ref, d_ref, w_ref, b_ref, o_ref):
        z, _ = _gelu(y_ref[...] + d_ref[...] * u_ref[...])
        zg = _dot(z, w_ref[...]) + b_ref[...]
        o_ref[...] = (z * _sigmoid(zg)).astype(o_ref.dtype)

    row = pl.BlockSpec((tr, C), lambda i: (i, 0))
    vec = pl.BlockSpec((1, C), lambda i: (0, 0))
    wsp = pl.BlockSpec((C, C), lambda i: (0, 0))
    return pl.pallas_call(body, grid=(L // tr,), in_specs=[row, row, vec, wsp, vec], out_specs=row,
                          out_shape=jax.ShapeDtypeStruct((L, C), BF16), name=name)(
        y0, proj, dvec, glu_w, glu_b)


def s5_out_bwd(y0, proj, dvec, glu_w, glu_b, dcat, *, name):
    L, C = y0.shape
    tr = _pick(L, (256, 128))

    def body(y_ref, u_ref, d_ref, w_ref, b_ref, do_ref, dy_ref, dud_ref, z_ref, dzg_ref, db_ref, dd_ref):
        u = u_ref[...]
        y = y_ref[...] + d_ref[...] * u
        z, t = _gelu(y)
        zg = _dot(z, w_ref[...]) + b_ref[...]
        s = _sigmoid(zg)
        do = do_ref[...]
        dzg = do * z * s * (1.0 - s)
        dz = do * s + _dot_nt(dzg, w_ref[...])
        c = math.sqrt(2.0 / math.pi)
        dgelu = 0.5 * (1.0 + t) + 0.5 * y * (1.0 - t * t) * c * (1.0 + 3.0 * 0.044715 * y * y)
        dy = dz * dgelu

        @pl.when(pl.program_id(0) == 0)
        def _():
            db_ref[...] = jnp.zeros_like(db_ref)
            dd_ref[...] = jnp.zeros_like(dd_ref)

        db_ref[...] += jnp.sum(dzg, axis=0, keepdims=True)
        dd_ref[...] += jnp.sum(dy * u, axis=0, keepdims=True)
        dy_ref[...] = dy
        dud_ref[...] = dy * d_ref[...]
        z_ref[...] = z.astype(BF16)
        dzg_ref[...] = dzg.astype(BF16)

    row = pl.BlockSpec((tr, C), lambda i: (i, 0))
    vec = pl.BlockSpec((1, C), lambda i: (0, 0))
    wsp = pl.BlockSpec((C, C), lambda i: (0, 0))
    return pl.pallas_call(body, grid=(L // tr,), in_specs=[row, row, vec, wsp, vec, row],
                          out_specs=[row, row, row, row, vec, vec],
                          out_shape=[jax.ShapeDtypeStruct((L, C), F32), jax.ShapeDtypeStruct((L, C), F32),
                                     jax.ShapeDtypeStruct((L, C), BF16), jax.ShapeDtypeStruct((L, C), BF16),
                                     jax.ShapeDtypeStruct((1, C), F32), jax.ShapeDtypeStruct((1, C), F32)],
                          compiler_params=_params(dimension_semantics=("arbitrary",)), name=name)(
        y0, proj, dvec, glu_w, glu_b, dcat)


def _dot_tri(tri, x, tri_left=True):
    t = tri.astype(BF16)
    x1 = x.astype(BF16)
    r1 = x - x1.astype(F32)
    x2 = r1.astype(BF16)
    x3 = (r1 - x2.astype(F32)).astype(BF16)
    dot = (lambda p: jnp.dot(t, p, preferred_element_type=F32)) if tri_left else (
        lambda p: jnp.dot(p, t, preferred_element_type=F32))
    return dot(x1) + dot(x2) + dot(x3)


def _hg_gates(xq, xf, lb, tri):
    C = xq.shape[0]
    sq = _sigmoid(xq)
    q = xq * sq
    sg = _sigmoid(xf)
    f = lb + (1.0 - lb) * sg
    kk = 1.0 - f
    b = _dot_tri(tri, jnp.log(f))
    bm = b[C // 2 - 1:C // 2, :]
    bl = b[C - 1:C, :]
    eb = jnp.exp(b)
    eqm, ekm, ekl = jnp.exp(b - bm), jnp.exp(bm - b), jnp.exp(bl - b)
    return dict(sq=sq, q=q, sg=sg, f=f, kk=kk, eb=eb, ebl=jnp.exp(bl), eqm=eqm, ekm=ekm, ekl=ekl,
                qb=q * eb, qt=q * eqm, kt=kk * ekm, kh=kk * ekl)


def _tri(C, lower):
    r = lax.broadcasted_iota(jnp.int32, (C, C), 0)
    c = lax.broadcasted_iota(jnp.int32, (C, C), 1)
    return (r >= c) if lower else (c >= r)


def hgrn_fwd(proj, lb, norm_g, *, name):
    L = proj.shape[0]
    C, H, K = HG_CHUNK, HG_HEADS, HG_DIM
    HK = H * K
    nc = L // C

    def body(q_ref, f_ref, i_ref, g_ref, lb_ref, ng_ref, o_ref, sall_ref, st_ref):
        @pl.when(pl.program_id(0) == 0)
        def _():
            st_ref[...] = jnp.zeros_like(st_ref)

        mask = _tri(C, True)
        sts = [st_ref[h] for h in range(H)]
        for s in range(S):
            rs = slice(s * C, (s + 1) * C)
            gt = _hg_gates(q_ref[rs, :], f_ref[rs, :], lb_ref[...], mask.astype(F32))
            v_all = i_ref[rs, :]
            outs = []
            for h in range(H):
                sl = slice(h * K, (h + 1) * K)
                v, st = v_all[:, sl], sts[h]
                sall_ref[s, h] = st
                att = jnp.where(mask, _dot_nt(gt["qt"][:, sl], gt["kt"][:, sl]), 0.0)
                o = _dot(att, v) + _dot_nt(gt["qb"][:, sl], st)
                sts[h] = st * gt["ebl"][:, sl] + _dot_tn(v, gt["kh"][:, sl])
                outs.append(o * lax.rsqrt(jnp.mean(o * o, axis=-1, keepdims=True) + NORM_EPS))
            xg = g_ref[rs, :]
            o_ref[rs, :] = (jnp.concatenate(outs, axis=1) * ng_ref[...] * (xg * _sigmoid(xg))).astype(o_ref.dtype)
        for h in range(H):
            st_ref[h] = sts[h]

    S = HG_STEP_CHUNKS

    def blk(cb):
        return pl.BlockSpec((S * C, HK), lambda i: (i, cb))

    vec = pl.BlockSpec((1, HK), lambda i: (0, 0))
    return pl.pallas_call(
        body, grid=(nc // S,), in_specs=[blk(1), blk(2), blk(3), blk(4), vec, vec],
        out_specs=[pl.BlockSpec((S * C, HK), lambda i: (i, 0)), pl.BlockSpec((S, H, K, K), lambda i: (i, 0, 0, 0))],
        out_shape=[jax.ShapeDtypeStruct((L, HK), BF16), jax.ShapeDtypeStruct((nc, H, K, K), F32)],
        scratch_shapes=[pltpu.VMEM((H, K, K), F32)],
        compiler_params=_params(dimension_semantics=("arbitrary",)), name=name,
    )(proj, proj, proj, proj, lb, norm_g)


def hgrn_bwd(proj, lb, norm_g, sall, dcat, *, name):
    L = proj.shape[0]
    C, H, K = HG_CHUNK, HG_HEADS, HG_DIM
    HK = H * K
    nc = L // C

    def body(q_ref, f_ref, i_ref, g_ref, lb_ref, ng_ref, sall_ref, do_ref, dx_ref, dlb_ref, dng_ref, dst_ref):
        @pl.when(pl.program_id(0) == 0)
        def _():
            dst_ref[...] = jnp.zeros_like(dst_ref)
            dlb_ref[...] = jnp.zeros_like(dlb_ref)
            dng_ref[...] = jnp.zeros_like(dng_ref)

        mask = _tri(C, True)
        lb_all, ng = lb_ref[...], ng_ref[...]
        dsts = [dst_ref[h] for h in range(H)]
        for s in reversed(range(S)):
            rs = slice(s * C, (s + 1) * C)
            dsts = chunk_bwd(rs, s, dsts, mask, lb_all, ng, q_ref, f_ref, i_ref, g_ref, sall_ref, do_ref,
                             dx_ref, dlb_ref, dng_ref)
        for h in range(H):
            dst_ref[h] = dsts[h]

    def chunk_bwd(rs, s, dsts, mask, lb_all, ng, q_ref, f_ref, i_ref, g_ref, sall_ref, do_ref, dx_ref, dlb_ref, dng_ref):
        xq, xg, v_all = q_ref[rs, :], g_ref[rs, :], i_ref[rs, :]
        gt = _hg_gates(xq, f_ref[rs, :], lb_all, mask.astype(F32))
        sgg = _sigmoid(xg)
        d_ob = do_ref[rs, :]
        d_on = d_ob * (xg * sgg)
        doh = d_on * ng
        ohs, d_qts, d_qbs, d_kts, d_khs, dvs, d_bls, new_dsts = [], [], [], [], [], [], [], []
        for h in range(H):
            sl = slice(h * K, (h + 1) * K)
            v, st, dst = v_all[:, sl], sall_ref[s, h], dsts[h]
            qt, kt, kh, qb = gt["qt"][:, sl], gt["kt"][:, sl], gt["kh"][:, sl], gt["qb"][:, sl]
            att = jnp.where(mask, _dot_nt(qt, kt), 0.0)
            o = _dot(att, v) + _dot_nt(qb, st)
            r = lax.rsqrt(jnp.mean(o * o, axis=-1, keepdims=True) + NORM_EPS)
            oh = o * r
            do = r * (doh[:, sl] - oh * jnp.mean(doh[:, sl] * oh, axis=-1, keepdims=True))
            datt = jnp.where(mask, _dot_nt(do, v), 0.0)
            dvs.append(_dot_tn(att, do) + _dot_nt(kh, dst))
            d_qbs.append(_dot_x3(do, st))
            d_qts.append(_dot_x3(datt, kt))
            d_kts.append(_dot_x3(datt, qt, ((0,), (0,))))
            d_kh = _dot_x3(v, dst)
            d_khs.append(d_kh)
            d_bls.append(jnp.sum(dst * st, axis=0, keepdims=True) * gt["ebl"][:, sl]
                         + jnp.sum(d_kh * kh, axis=0, keepdims=True))
            new_dsts.append(dst * gt["ebl"][:, sl] + _dot_tn(do, qb))
            ohs.append(oh)
        oh, d_qt, d_qb, d_kt, d_kh, dv, d_bl = (jnp.concatenate(p, axis=1) for p in
                                                (ohs, d_qts, d_qbs, d_kts, d_khs, dvs, d_bls))
        dxg = d_ob * (oh * ng) * (sgg * (1.0 + xg * (1.0 - sgg)))
        dng_ref[...] += jnp.sum(d_on * oh, axis=0, keepdims=True)
        dq = d_qt * gt["eqm"] + d_qb * gt["eb"]
        db = d_qt * gt["qt"] + d_qb * gt["qb"] - d_kt * gt["kt"] - d_kh * gt["kh"]
        rowi = lax.broadcasted_iota(jnp.int32, (C, HK), 0)
        db = db + jnp.where(rowi == C - 1, d_bl, 0.0)
        dkk = d_kt * gt["ekm"] + d_kh * gt["ekl"]
        dlg = _dot_tri(_tri(C, False).astype(F32), db)
        df = dlg / gt["f"] - dkk
        sg, sq = gt["sg"], gt["sq"]
        dlb_ref[...] += jnp.sum(df * (1.0 - sg), axis=0, keepdims=True)
        dx_ref[rs, 0:HK] = (dq * (sq * (1.0 + xq * (1.0 - sq)))).astype(dx_ref.dtype)
        dx_ref[rs, HK:2 * HK] = (df * (1.0 - lb_all) * sg * (1.0 - sg)).astype(dx_ref.dtype)
        dx_ref[rs, 2 * HK:3 * HK] = dv.astype(dx_ref.dtype)
        dx_ref[rs, 3 * HK:4 * HK] = dxg.astype(dx_ref.dtype)
        return new_dsts

    S = HG_STEP_CHUNKS
    ns = nc // S

    def blk(cb):
        return pl.BlockSpec((S * C, HK), lambda i: (ns - 1 - i, cb))

    vec = pl.BlockSpec((1, HK), lambda i: (0, 0))
    return pl.pallas_call(
        body, grid=(ns,),
        in_specs=[blk(1), blk(2), blk(3), blk(4), vec, vec,
                  pl.BlockSpec((S, H, K, K), lambda i: (ns - 1 - i, 0, 0, 0)), blk(1)],
        out_specs=[pl.BlockSpec((S * C, 4 * HK), lambda i: (ns - 1 - i, 0)), vec, vec],
        out_shape=[jax.ShapeDtypeStruct((L, 4 * HK), BF16), jax.ShapeDtypeStruct((1, HK), F32),
                   jax.ShapeDtypeStruct((1, HK), F32)],
        scratch_shapes=[pltpu.VMEM((H, K, K), F32)],
        compiler_params=_params(dimension_semantics=("arbitrary",)), name=name,
    )(proj, proj, proj, proj, lb, norm_g, sall, dcat)


def _shift_down(x, k, row):
    return jnp.where(row >= k, pltpu.roll(x, k, 0), 0.0)


def _shift_up(x, k, row):
    n = x.shape[0]
    return jnp.where(row < n - k, pltpu.roll(x, n - k, 0), 0.0)


def convgate_fwd(hu, conv_w, conv_b, *, name):
    L, C2 = hu.shape
    C = C2 // 2
    tc = _pick(C, (256, 128))
    nb = C // tc

    def body(a_ref, b_ref, wa_ref, wb_ref, ba_ref, bb_ref, o_ref):
        row = lax.broadcasted_iota(jnp.int32, (L, tc), 0)

        def conv(x, w, bias):
            return w[2:3, :] * x + w[1:2, :] * _shift_down(x, 1, row) + w[0:1, :] * _shift_down(x, 2, row) + bias

        ca = conv(a_ref[...], wa_ref[...], ba_ref[...])
        cb = conv(b_ref[...], wb_ref[...], bb_ref[...])
        o_ref[...] = (ca * _sigmoid(ca) * cb).astype(o_ref.dtype)

    def col(off, rows):
        return pl.BlockSpec((rows, tc), lambda j: (0, j + off))

    return pl.pallas_call(
        body, grid=(nb,), in_specs=[col(0, L), col(nb, L), col(0, 3), col(nb, 3), col(0, 1), col(nb, 1)],
        out_specs=col(0, L), out_shape=jax.ShapeDtypeStruct((L, C), BF16),
        compiler_params=_params(dimension_semantics=("parallel",)), name=name,
    )(hu, hu, conv_w, conv_w, conv_b, conv_b)


def convgate_bwd(hu, conv_w, conv_b, dact, *, name):
    L, C2 = hu.shape
    C = C2 // 2
    tc = _pick(C, (256, 128))
    nb = C // tc

    def body(a_ref, b_ref, wa_ref, wb_ref, ba_ref, bb_ref, d_ref, dxa_ref, dxb_ref, dwa_ref, dwb_ref, dba_ref, dbb_ref):
        row = lax.broadcasted_iota(jnp.int32, (L, tc), 0)

        def conv(x, w, bias):
            x1 = _shift_down(x, 1, row)
            x2 = _shift_down(x, 2, row)
            return w[2:3, :] * x + w[1:2, :] * x1 + w[0:1, :] * x2 + bias, x1, x2

        xa, xb = a_ref[...], b_ref[...]
        wa, wb = wa_ref[...], wb_ref[...]
        ca, xa1, xa2 = conv(xa, wa, ba_ref[...])
        cb, xb1, xb2 = conv(xb, wb, bb_ref[...])
        d = d_ref[...]
        sa = _sigmoid(ca)
        dca = d * cb * (sa * (1.0 + ca * (1.0 - sa)))
        dcb = d * (ca * sa)

        def back(dc, w, x, x1, x2, dx_ref, dw_ref, db_ref):
            dx = w[2:3, :] * dc + w[1:2, :] * _shift_up(dc, 1, row) + w[0:1, :] * _shift_up(dc, 2, row)
            dx_ref[...] = dx.astype(dx_ref.dtype)
            dw_ref[...] = jnp.concatenate([jnp.sum(dc * x2, axis=0, keepdims=True),
                                           jnp.sum(dc * x1, axis=0, keepdims=True),
                                           jnp.sum(dc * x, axis=0, keepdims=True)], axis=0)
            db_ref[...] = jnp.sum(dc, axis=0, keepdims=True)

        back(dca, wa, xa, xa1, xa2, dxa_ref, dwa_ref, dba_ref)
        back(dcb, wb, xb, xb1, xb2, dxb_ref, dwb_ref, dbb_ref)

    def col(off, rows):
        return pl.BlockSpec((rows, tc), lambda j: (0, j + off))

    outs = pl.pallas_call(
        body, grid=(nb,),
        in_specs=[col(0, L), col(nb, L), col(0, 3), col(nb, 3), col(0, 1), col(nb, 1), col(0, L)],
        out_specs=[col(0, L), col(0, L), col(0, 3), col(0, 3), col(0, 1), col(0, 1)],
        out_shape=[jax.ShapeDtypeStruct((L, C), BF16)] * 2 + [jax.ShapeDtypeStruct((3, C), F32)] * 2
        + [jax.ShapeDtypeStruct((1, C), F32)] * 2,
        compiler_params=_params(dimension_semantics=("parallel",)), name=name,
    )(hu, hu, conv_w, conv_w, conv_b, conv_b, dact)
    dxa, dxb, dwa, dwb, dba, dbb = outs
    return (jnp.concatenate([dxa, dxb], axis=1), jnp.concatenate([dwa, dwb], axis=1),
            jnp.concatenate([dba, dbb], axis=1))


def rope_tables(positions):
    half = ROT_DIM // 2
    inv_freq = ROPE_THETA ** (-jnp.arange(half, dtype=F32) * 2.0 / ROT_DIM)
    ang = positions.astype(F32)[:, None] * inv_freq
    cos, sin = jnp.cos(ang), jnp.sin(ang)
    L = positions.shape[0]
    one = jnp.ones((L, ATT_E - ROT_DIM), F32)
    zero = jnp.zeros((L, ATT_E - ROT_DIM), F32)
    zh = jnp.zeros((L, half), F32)
    tc = jnp.concatenate([cos, cos, one], axis=1)
    ts1 = jnp.concatenate([zh, sin, zero], axis=1)
    ts2 = jnp.concatenate([-sin, zh, zero], axis=1)
    return tuple(jnp.concatenate([t, t], axis=1) for t in (tc, ts1, ts2))


def qkv_rope(hn, w_t, tabs, *, name):
    L, D = hn.shape
    N = w_t.shape[0]
    W = 512
    tm = _pick(L, (1024, 512, 256, 128))
    nq = N // (3 * W)
    scale = ATT_E ** -0.5

    def body(a_ref, b_ref, c_ref, s1_ref, s2_ref, o_ref):
        j = pl.program_id(1)
        x = _dot_nt(a_ref[...], b_ref[...])
        c = jnp.concatenate([c_ref[...]] * 4, axis=1)
        s1 = jnp.concatenate([s1_ref[...]] * 4, axis=1)
        s2 = jnp.concatenate([s2_ref[...]] * 4, axis=1)
        rot = x * c + pltpu.roll(x, 8, 1) * s1 + pltpu.roll(x, W - 8, 1) * s2
        mult = jnp.where(j < nq, scale, 1.0)
        o_ref[...] = jnp.where(j < 2 * nq, rot * mult, x)

    tab = pl.BlockSpec((tm, 128), lambda i, j: (i, 0))
    return pl.pallas_call(body, grid=(L // tm, N // W),
                          in_specs=[pl.BlockSpec((tm, D), lambda i, j: (i, 0)), pl.BlockSpec((W, D), lambda i, j: (j, 0)),
                                    tab, tab, tab],
                          out_specs=pl.BlockSpec((tm, W), lambda i, j: (i, j)),
                          out_shape=jax.ShapeDtypeStruct((L, N), F32),
                          compiler_params=_params(dimension_semantics=("parallel", "parallel")), name=name)(
        hn, w_t, *tabs)


def rope_bwd(slabs, tabs, *, name):
    L, W = slabs[0].shape
    tr = _pick(L, (256, 128))
    nq = len(slabs) // 3
    scale = ATT_E ** -0.5

    def body(*refs):
        d_refs, (c_ref, s1_ref, s2_ref, o_ref) = refs[:3 * nq], refs[3 * nq:]
        c = jnp.concatenate([c_ref[...]] * 4, axis=1)
        s1 = jnp.concatenate([s1_ref[...]] * 4, axis=1)
        s2 = jnp.concatenate([s2_ref[...]] * 4, axis=1)
        for j, d_ref in enumerate(d_refs):
            dy = d_ref[...]
            if j < 2 * nq:
                dy = dy * c + pltpu.roll(dy * s1, W - 8, 1) + pltpu.roll(dy * s2, 8, 1)
            if j < nq:
                dy = dy * scale
            o_ref[:, j * W:(j + 1) * W] = dy.astype(o_ref.dtype)

    slab = pl.BlockSpec((tr, W), lambda i: (i, 0))
    tab = pl.BlockSpec((tr, 128), lambda i: (i, 0))
    return pl.pallas_call(body, grid=(L // tr,), in_specs=[slab] * (3 * nq) + [tab, tab, tab],
                          out_specs=pl.BlockSpec((tr, 3 * nq * W), lambda i: (i, 0)),
                          out_shape=jax.ShapeDtypeStruct((L, 3 * nq * W), BF16),
                          compiler_params=_params(dimension_semantics=("parallel",)), name=name)(*slabs, *tabs)


def _att_masks(has_prev):
    qi = lax.broadcasted_iota(jnp.int32, (ATT_BLOCK, ATT_BLOCK), 0)
    kj = lax.broadcasted_iota(jnp.int32, (ATT_BLOCK, ATT_BLOCK), 1)
    return qi >= kj, (kj >= qi) & has_prev


ATT_COLS = 128


def _att_rows(j, d, nb):
    B = ATT_BLOCK
    r, n = j // nb, j % nb
    start = r + d * B * n
    has_prev = n > 0
    pstart = jnp.where(has_prev, start - d * B, start)
    if d == 1:
        return pl.ds(pl.multiple_of(start, B), B), pl.ds(pl.multiple_of(pstart, B), B), has_prev
    return pl.ds(start, B, stride=d), pl.ds(pstart, B, stride=d), has_prev


def _qkv_specs(L, g):
    per = ATT_HPG * ATT_E // ATT_COLS
    third = len(ATT_DILATIONS) * per
    return [pl.BlockSpec((L, ATT_COLS), lambda c, base=base: (0, base + c))
            for base in (g * per, third + g * per, 2 * third + g * per)]


def attn_fwd(qkv, g, d, *, name):
    L, W = qkv.shape[0], ATT_HPG * ATT_E
    B, E = ATT_BLOCK, ATT_E
    nblk = L // B
    nb = nblk // d

    def body(q_ref, k_ref, v_ref, o_ref, l_ref):
        def step(j, carry):
            cur, prv, has_prev = _att_rows(j, d, nb)
            mc, mp = _att_masks(has_prev)
            qb, kc, kp, vc, vp = q_ref[cur, :], k_ref[cur, :], k_ref[prv, :], v_ref[cur, :], v_ref[prv, :]
            outs, lses = [], []
            for h in range(ATT_COLS // E):
                sl = slice(h * E, (h + 1) * E)
                sc = jnp.where(mc, _dot_nt(qb[:, sl], kc[:, sl]), NEG_BIG)
                sp = jnp.where(mp, _dot_nt(qb[:, sl], kp[:, sl]), NEG_BIG)
                m = jnp.maximum(jnp.max(sc, axis=-1, keepdims=True), jnp.max(sp, axis=-1, keepdims=True))
                pc = jnp.exp(sc - m)
                pp = jnp.exp(sp - m)
                den = jnp.sum(pc, axis=-1, keepdims=True) + jnp.sum(pp, axis=-1, keepdims=True)
                outs.append((_dot(pc, vc[:, sl]) + _dot(pp, vp[:, sl])) / den)
                lses.append(jnp.broadcast_to(m + jnp.log(den), (B, E)))
            o_ref[cur, :] = jnp.concatenate(outs, axis=1)
            l_ref[cur, :] = jnp.concatenate(lses, axis=1)
            return carry

        lax.fori_loop(0, nblk, step, 0, unroll=4)

    col = pl.BlockSpec((L, ATT_COLS), lambda c: (0, c))
    return pl.pallas_call(body, grid=(W // ATT_COLS,), in_specs=_qkv_specs(L, g), out_specs=[col] * 2,
                          out_shape=[jax.ShapeDtypeStruct((L, W), F32)] * 2,
                          compiler_params=_params(dimension_semantics=("parallel",)), name=name)(qkv, qkv, qkv)


def attn_bwd(qkv, g, lse, do, dl, d, *, name):
    L, W = qkv.shape[0], ATT_HPG * ATT_E
    B, E = ATT_BLOCK, ATT_E
    nblk = L // B
    nb = nblk // d

    def body(q_ref, k_ref, v_ref, l_ref, do_ref, dl_ref, dq_ref, dk_ref, dv_ref):
        dk_ref[...] = jnp.zeros_like(dk_ref)
        dv_ref[...] = jnp.zeros_like(dv_ref)

        def step(j, carry):
            cur, prv, has_prev = _att_rows(j, d, nb)
            mc, mp = _att_masks(has_prev)
            qb, kc, kp, vc, vp = q_ref[cur, :], k_ref[cur, :], k_ref[prv, :], v_ref[cur, :], v_ref[prv, :]
            lb, dob, dlb = l_ref[cur, :], do_ref[cur, :], dl_ref[cur, :]
            dqs, dkc, dkp, dvc, dvp = [], [], [], [], []
            for h in range(ATT_COLS // E):
                sl = slice(h * E, (h + 1) * E)
                qh, doh = qb[:, sl], dob[:, sl]
                lse_h, dl_h = lb[:, h * E:h * E + 1], dlb[:, h * E:h * E + 1]
                pc = jnp.where(mc, jnp.exp(_dot_nt(qh, kc[:, sl]) - lse_h), 0.0)
                pp = jnp.where(mp, jnp.exp(_dot_nt(qh, kp[:, sl]) - lse_h), 0.0)
                dsc = pc * (_dot_nt(doh, vc[:, sl]) - dl_h)
                dsp = pp * (_dot_nt(doh, vp[:, sl]) - dl_h)
                dqs.append(_dot(dsc, kc[:, sl]) + _dot(dsp, kp[:, sl]))
                dkc.append(_dot_tn(dsc, qh))
                dkp.append(_dot_tn(dsp, qh))
                dvc.append(_dot_tn(pc, doh))
                dvp.append(_dot_tn(pp, doh))
            dq_ref[cur, :] = jnp.concatenate(dqs, axis=1)
            dk_ref[cur, :] = dk_ref[cur, :] + jnp.concatenate(dkc, axis=1)
            dv_ref[cur, :] = dv_ref[cur, :] + jnp.concatenate(dvc, axis=1)
            dk_ref[prv, :] = dk_ref[prv, :] + jnp.concatenate(dkp, axis=1)
            dv_ref[prv, :] = dv_ref[prv, :] + jnp.concatenate(dvp, axis=1)
            return carry

        lax.fori_loop(0, nblk, step, 0, unroll=4)

    col = pl.BlockSpec((L, ATT_COLS), lambda c: (0, c))
    return pl.pallas_call(body, grid=(W // ATT_COLS,), in_specs=_qkv_specs(L, g) + [col] * 3, out_specs=[col] * 3,
                          out_shape=[jax.ShapeDtypeStruct((L, W), F32)] * 3,
                          compiler_params=_params(dimension_semantics=("parallel",)), name=name)(
        qkv, qkv, qkv, lse, do, dl)


def _merge_alpha(l_refs):
    ls = [r[...] for r in l_refs]
    m = jnp.maximum(jnp.maximum(ls[0], ls[1]), ls[2])
    es = [jnp.exp(l - m) for l in ls]
    den = es[0] + es[1] + es[2]
    return [e / den for e in es]


def merge_fwd(os_, ls_, *, name):
    L, W = os_[0].shape
    tr = _pick(L, (256, 128))

    def body(o0, o1, o2, l0, l1, l2, out_ref):
        al = _merge_alpha((l0, l1, l2))
        out_ref[...] = (al[0] * o0[...] + al[1] * o1[...] + al[2] * o2[...]).astype(out_ref.dtype)

    row = pl.BlockSpec((tr, W), lambda i: (i, 0))
    return pl.pallas_call(body, grid=(L // tr,), in_specs=[row] * 6, out_specs=row,
                          out_shape=jax.ShapeDtypeStruct((L, W), BF16), name=name)(*os_, *ls_)


def merge_bwd(os_, ls_, do, *, name):
    L, W = do.shape
    tr = _pick(L, (256, 128))

    def body(o0, o1, o2, l0, l1, l2, do_ref, d0, d1, d2, e0, e1, e2):
        al = _merge_alpha((l0, l1, l2))
        dov = do_ref[...]
        r = lax.broadcasted_iota(jnp.int32, (W, W), 0) // ATT_E
        c = lax.broadcasted_iota(jnp.int32, (W, W), 1) // ATT_E
        ones_blk = (r == c).astype(F32)
        t = jnp.zeros_like(dov)
        for a, o in zip(al, (o0, o1, o2)):
            t = t + a * _dot_tri(ones_blk, dov * o[...], tri_left=False)
        for a, d_ref, e_ref in zip(al, (d0, d1, d2), (e0, e1, e2)):
            d_ref[...] = a * dov
            e_ref[...] = a * t

    row = pl.BlockSpec((tr, W), lambda i: (i, 0))
    return pl.pallas_call(body, grid=(L // tr,), in_specs=[row] * 7, out_specs=[row] * 6,
                          out_shape=[jax.ShapeDtypeStruct((L, W), F32)] * 6, name=name)(*os_, *ls_, do)


def _me_and_peers():
    x, y, c = lax.axis_index("x"), lax.axis_index("y"), lax.axis_index("c")
    peers = []
    for k in range(1, N_DEV):
        px = 1 - x if k & 4 else x
        py = 1 - y if k & 2 else y
        pc = 1 - c if k & 1 else c
        peers.append((px, py, pc))
    return (x, y, c), peers


def _index(dev):
    return 4 * dev[0] + 2 * dev[1] + dev[2]


def _hbm(a):
    return pltpu.with_memory_space_constraint(a, pltpu.HBM)


HBM_SPEC = pl.BlockSpec(memory_space=pltpu.HBM)
SEM_SPEC = pl.BlockSpec(memory_space=pltpu.SEMAPHORE)
DATAFLOW = pltpu.SideEffectType.DATAFLOW_SIDE_EFFECTING


def _remote(src_ref, land_ref, slotted, me, peer, src_is_mine, send_sem, recv_sem, k):
    sender, receiver = (me, peer) if src_is_mine else (peer, me)
    src = src_ref.at[_index(receiver)] if slotted else src_ref
    return pltpu.make_async_remote_copy(src_ref=src, dst_ref=land_ref.at[_index(sender)], send_sem=send_sem.at[k],
                                        recv_sem=recv_sem.at[k], device_id=peer, device_id_type=MESH_ID)


SIBLING = 0
SAME_CORE = (1, 3, 5)
OTHER_CORE = (2, 4, 6)


def copies_start(arrays, mode, *, name):
    n = len(arrays)
    slotted = mode == "exchange"
    lands = [lax.empty(a.shape if slotted else (N_DEV,) + a.shape, a.dtype) for a in arrays]
    targets = (SIBLING,) + SAME_CORE if mode == "gather2" else tuple(range(N_DEV - 1))

    def body(*refs):
        x_refs, land_refs = refs[:n], refs[n:2 * n]
        send, recv = refs[2 * n:3 * n], refs[3 * n:4 * n]
        token = refs[-1]
        me, peers = _me_and_peers()
        for w in range(n):
            for k in targets:
                _remote(x_refs[w], land_refs[w], slotted, me, peers[k], True, send[w], recv[w], k).start()
            if not slotted:
                pltpu.make_async_copy(x_refs[w], land_refs[w].at[_index(me)], recv[w].at[N_DEV - 1]).start()
        token[...] = jnp.zeros_like(token)

    sem = pltpu.SemaphoreType.DMA((N_DEV,))
    out_shape = ([sem] * (2 * n) + [pltpu.HBM(a.shape, a.dtype) for a in arrays]
                 + [pltpu.HBM(l.shape, l.dtype) for l in lands] + [jax.ShapeDtypeStruct((8, 128), F32)])
    outs = pl.pallas_call(
        body, name=name, out_shape=out_shape, in_specs=[HBM_SPEC] * (2 * n),
        out_specs=[SEM_SPEC] * (2 * n) + [HBM_SPEC] * (2 * n) + [pl.BlockSpec(memory_space=pltpu.VMEM)],
        input_output_aliases={i: 2 * n + i for i in range(2 * n)},
        compiler_params=pltpu.CompilerParams(has_side_effects=DATAFLOW),
    )(*[_hbm(a) for a in arrays], *[_hbm(l) for l in lands])
    handles = [(outs[w], outs[n + w], outs[2 * n + w], outs[3 * n + w]) for w in range(n)]
    return handles, outs[-1]


def _forward(land_ref, me, peers, j, fsend, frecv, mine):
    block = _index(peers[SAME_CORE[j]] if mine else peers[OTHER_CORE[j]])
    return pltpu.make_async_remote_copy(src_ref=land_ref.at[block], dst_ref=land_ref.at[block], send_sem=fsend.at[j],
                                        recv_sem=frecv.at[j], device_id=peers[SIBLING], device_id_type=MESH_ID)


def copies_forward(handles, after, *, name):
    n = len(handles)

    def body(*refs):
        land_refs, recv = refs[:n], refs[n:2 * n]
        fsend, frecv = refs[2 * n + 1:3 * n + 1], refs[3 * n + 1:4 * n + 1]
        token = refs[-1]
        me, peers = _me_and_peers()
        for w in range(n):
            for j, k in enumerate(SAME_CORE):
                block = land_refs[w].at[_index(peers[k])]
                pltpu.make_async_remote_copy(src_ref=block, dst_ref=block, send_sem=recv[w].at[N_DEV - 1],
                                             recv_sem=recv[w].at[k], device_id=peers[k], device_id_type=MESH_ID).wait_recv()
                _forward(land_refs[w], me, peers, j, fsend[w], frecv[w], True).start()
        token[...] = jnp.zeros_like(token)

    sem = pltpu.SemaphoreType.DMA((len(SAME_CORE),))
    lands = [h[3] for h in handles]
    outs = pl.pallas_call(
        body, name=name,
        out_shape=[sem] * (2 * n) + [pltpu.HBM(l.shape, l.dtype) for l in lands] + [jax.ShapeDtypeStruct((8, 128), F32)],
        in_specs=[HBM_SPEC] * n + [SEM_SPEC] * n + [pl.BlockSpec(memory_space=pl.ANY)],
        out_specs=[SEM_SPEC] * (2 * n) + [HBM_SPEC] * n + [pl.BlockSpec(memory_space=pltpu.VMEM)],
        input_output_aliases={w: 2 * n + w for w in range(n)},
        compiler_params=pltpu.CompilerParams(has_side_effects=DATAFLOW),
    )(*lands, *[h[1] for h in handles], after)
    new = [(h[0], h[1], h[2], outs[2 * n + w], outs[w], outs[n + w]) for w, h in enumerate(handles)]
    return new, outs[-1]


def copies_wait(handle, mode, after, *, name):
    slotted = mode == "exchange"
    two_level = mode == "gather2"
    send_sem, recv_sem, x_thru, land_thru = handle[:4]
    targets = (SIBLING,) + SAME_CORE if two_level else tuple(range(N_DEV - 1))
    arrivals = (SIBLING,) if two_level else targets

    def body(x_ref, land_ref, send_ref, recv_ref, *rest):
        me, peers = _me_and_peers()
        for k in targets:
            _remote(x_ref, land_ref, slotted, me, peers[k], True, send_ref, recv_ref, k).wait_send()
        for k in arrivals:
            _remote(x_ref, land_ref, slotted, me, peers[k], False, send_ref, recv_ref, k).wait_recv()
        if not slotted:
            pltpu.make_async_copy(x_ref, land_ref.at[_index(me)], recv_ref.at[N_DEV - 1]).wait()
        if two_level:
            fsend, frecv = rest[0], rest[1]
            for j in range(len(SAME_CORE)):
                _forward(land_ref, me, peers, j, fsend, frecv, True).wait_send()
                _forward(land_ref, me, peers, j, fsend, frecv, False).wait_recv()

    extra = list(handle[4:])
    return pl.pallas_call(
        body, name=name, out_shape=(pltpu.HBM(x_thru.shape, x_thru.dtype), pltpu.HBM(land_thru.shape, land_thru.dtype)),
        in_specs=[HBM_SPEC, HBM_SPEC, SEM_SPEC, SEM_SPEC] + [SEM_SPEC] * len(extra) + [pl.BlockSpec(memory_space=pl.ANY)],
        out_specs=(HBM_SPEC, HBM_SPEC), input_output_aliases={0: 0, 1: 1},
        compiler_params=pltpu.CompilerParams(has_side_effects=DATAFLOW),
    )(x_thru, land_thru, send_sem, recv_sem, *extra, after)


def cast_bf16(x, *, ncols=None, name):
    R = x.shape[0]
    C = ncols or x.shape[1]
    tr = _pick(R, (512, 352, 256, 128, 64))

    def body(x_ref, o_ref):
        o_ref[...] = x_ref[...].astype(BF16)

    row = pl.BlockSpec((tr, C), lambda i: (i, 0))
    return pl.pallas_call(body, grid=(R // tr,), in_specs=[row], out_specs=row,
                          out_shape=jax.ShapeDtypeStruct((R, C), BF16), name=name)(x)


def cast_bf16_layer(x3, layer, *, name):
    _, R, C = x3.shape
    tr = _pick(R, (512, 352, 256, 128, 64))

    def body(x_ref, o_ref):
        o_ref[...] = x_ref[...].astype(BF16)

    return pl.pallas_call(body, grid=(R // tr,), in_specs=[pl.BlockSpec((None, tr, C), lambda i: (layer, i, 0))],
                          out_specs=pl.BlockSpec((tr, C), lambda i: (i, 0)),
                          out_shape=jax.ShapeDtypeStruct((R, C), BF16), name=name)(x3)


BD_PARTS = 4


def _blockdiag_call(b, build, G, r, c, name):
    gp = G // BD_PARTS

    def body_build(b_ref, o_ref):
        o_ref[...] = jnp.zeros_like(o_ref)
        for g in range(G):
            o_ref[g // gp, (g % gp) * r:(g % gp + 1) * r, (g % gp) * c:(g % gp + 1) * c] = b_ref[g]

    def body_extract(d_ref, o_ref):
        for g in range(G):
            o_ref[g] = d_ref[g // gp, (g % gp) * r:(g % gp + 1) * r, (g % gp) * c:(g % gp + 1) * c]

    out = jax.ShapeDtypeStruct((BD_PARTS, gp * r, gp * c) if build else (G, r, c), F32)
    return pl.pallas_call(body_build if build else body_extract, out_shape=out, name=name)(b)


def make_blockdiag(G, r, c, name):
    @jax.custom_vjp
    def blockdiag(b):
        return _blockdiag_call(b, True, G, r, c, name + "_build")

    def fwd(b):
        return blockdiag(b), None

    def bwd(_, g):
        return (_blockdiag_call(g, False, G, r, c, name + "_extract"),)

    blockdiag.defvjp(fwd, bwd)
    return blockdiag


def _my_index():
    return 4 * lax.axis_index("x") + 2 * lax.axis_index("y") + lax.axis_index("c")


def cols_from_shards(g, *, name):
    _, K, n = g.shape
    tk = _pick(K, (256, 128))

    def body(g_ref, o_ref):
        for i in range(N_DEV):
            o_ref[:, i * n:(i + 1) * n] = g_ref[i]

    return pl.pallas_call(body, grid=(K // tk,), in_specs=[pl.BlockSpec((N_DEV, tk, n), lambda i: (0, i, 0))],
                          out_specs=pl.BlockSpec((tk, N_DEV * n), lambda i: (i, 0)),
                          out_shape=jax.ShapeDtypeStruct((K, N_DEV * n), g.dtype), name=name)(g)


def shards_from_cols(w, *, name):
    K, N = w.shape
    n = N // N_DEV
    tk = _pick(K, (256, 128))

    def body(w_ref, o_ref):
        for i in range(N_DEV):
            o_ref[i] = w_ref[:, i * n:(i + 1) * n].astype(o_ref.dtype)

    return pl.pallas_call(body, grid=(K // tk,), in_specs=[pl.BlockSpec((tk, N), lambda i: (i, 0))],
                          out_specs=pl.BlockSpec((N_DEV, tk, n), lambda i: (0, i, 0)),
                          out_shape=jax.ShapeDtypeStruct((N_DEV, K, n), BF16), name=name)(w)


def _adamw(w, g, m, v):
    m = ADAM_B1 * m + (1.0 - ADAM_B1) * g
    v = ADAM_B2 * v + (1.0 - ADAM_B2) * (g * g)
    m_hat = m / (1.0 - ADAM_B1 ** ADAM_STEP)
    v_hat = v / (1.0 - ADAM_B2 ** ADAM_STEP)
    delta = -ADAM_LR * (m_hat / (jnp.sqrt(v_hat) + ADAM_EPS) + ADAM_WD * w)
    return delta, m, v


def reduce_adamw(recv, own, own_slotted, me, w, m, v, *, layer=0, n_layers=1, into=None, name):
    _, R, C = recv.shape
    tr = _pick(R, (352, 320, 288, 256, 128, 64, 32, 16, 8))
    off = layer * (R // tr)

    def body(me_ref, r_ref, own_ref, w_ref, m_ref, v_ref, *rest):
        g_ref, d_ref, nm_ref, nv_ref = rest[-4:]
        mine = me_ref[0]
        g = None
        for i in range(N_DEV):
            part = jnp.where(mine == i, own_ref[...], r_ref[i]).astype(F32)
            g = part if g is None else g + part
        delta, nm, nv = _adamw(w_ref[...], g, m_ref[...], v_ref[...])
        g_ref[...] = g
        d_ref[...] = delta
        nm_ref[...] = nm
        nv_ref[...] = nv

    row = pl.BlockSpec((tr, C), lambda i, me_ref: (i + off, 0))
    own_spec = (pl.BlockSpec((None, tr, C), lambda i, me_ref: (me_ref[0], i, 0)) if own_slotted
                else pl.BlockSpec((tr, C), lambda i, me_ref: (i, 0)))
    rest = [] if into is None else list(into)
    grid_spec = pltpu.PrefetchScalarGridSpec(
        num_scalar_prefetch=1, grid=(R // tr,),
        in_specs=[pl.BlockSpec((N_DEV, tr, C), lambda i, me_ref: (0, i, 0)), own_spec, row, row, row]
        + [pl.BlockSpec(memory_space=pl.ANY)] * len(rest),
        out_specs=[row] * 4)
    return pl.pallas_call(body, grid_spec=grid_spec, out_shape=[jax.ShapeDtypeStruct((n_layers * R, C), F32)] * 4,
                          input_output_aliases={6 + k: k for k in range(len(rest))},
                          compiler_params=_params(dimension_semantics=("parallel",)), name=name)(
        me.reshape(1).astype(jnp.int32), recv, own, w, m, v, *rest)


def _s5_prepare(A_re, A_im, log_dt, B_re, B_im, C_re, C_im):
    G, P, Cg = S5_GROUPS, S5_STATE, S5_GROUP
    dt = jnp.exp(log_dt)[:, None]
    mag = jnp.exp(A_re * dt)
    ab_re = mag * jnp.cos(A_im * dt)
    ab_im = mag * jnp.sin(A_im * dt)
    den = A_re * A_re + A_im * A_im
    nr, ni = ab_re - 1.0, ab_im
    c_re = (nr * A_re + ni * A_im) / den
    c_im = (ni * A_re - nr * A_im) / den
    Bb_re = c_re[..., None] * B_re - c_im[..., None] * B_im
    Bb_im = c_re[..., None] * B_im + c_im[..., None] * B_re
    def dense_in(b, name):
        return make_blockdiag(G, Cg, P, name)(b.transpose(0, 2, 1))

    def dense_out(c, name):
        return make_blockdiag(G, P, Cg, name)(c.transpose(0, 2, 1))

    return (ab_re.reshape(1, G * P), ab_im.reshape(1, G * P), dense_in(Bb_re, "s5_wb_re"), dense_in(Bb_im, "s5_wb_im"),
            dense_out(C_re, "s5_wc_re"), dense_out(-C_im, "s5_wc_im"))


def _lower_bound(gamma):
    return jnp.cumsum(jax.nn.softmax(gamma, axis=0), axis=0)[0:1]


def _ffn_fwd(h, g_norm, get_w_in, conv_w, conv_b, get_w_out, tag):
    hn = rms_fwd(h, g_norm, name=tag + "_rms")
    w_in = get_w_in(hn)
    hu = mm(hn, w_in, tb=True, name=tag + "_in")
    act = convgate_fwd(hu, conv_w, conv_b, name=tag + "_gate")
    w_out = get_w_out(act)
    h_out = mm(act, w_out, res=h, name=tag + "_out")
    return h_out, (hn, hu, act), w_in, w_out


def _ffn_bwd(h, g_norm, w_in, conv_w, conv_b, w_out, saved, dh, tag, send_dw_in, send_dw_out):
    hn, hu, act = saved
    sent = send_dw_out(mm(act, dh, ta=True, out_dtype=BF16, name=tag + "_dwout"))
    dact = mm(dh, w_out, tb=True, dep=sent, name=tag + "_dact")
    dhu, dconv_w, dconv_b = convgate_bwd(hu, conv_w, conv_b, dact, name=tag + "_dgate")
    sent = send_dw_in(mm(dhu, hn, ta=True, out_dtype=BF16, name=tag + "_dwin"))
    dh_in, dg = mm_drms(dhu, w_in, h, g_norm, dh, dep=sent, name=tag + "_dhn")
    return dh_in, dg, dconv_w, dconv_b


def kernel(x, positions, norm_mix, norm_ffn, norm_final, mix_w_in, mix_w_out, s5_A_re, s5_A_im, s5_log_dt, s5_B_re, s5_B_im, s5_C_re, s5_C_im, s5_D, s5_glu_w, s5_glu_b, hgrn_gamma, hgrn_norm, att_w_qkv, att_w_o, ffn_w_in, ffn_conv_w, ffn_conv_b, ffn_w_out, loss_target, m_norm_mix, m_norm_ffn, m_norm_final, m_mix_w_in, m_mix_w_out, m_s5_A_re, m_s5_A_im, m_s5_log_dt, m_s5_B_re, m_s5_B_im, m_s5_C_re, m_s5_C_im, m_s5_D, m_s5_glu_w, m_s5_glu_b, m_hgrn_gamma, m_hgrn_norm, m_att_w_qkv, m_att_w_o, m_ffn_w_in, m_ffn_conv_w, m_ffn_conv_b, m_ffn_w_out, v_norm_mix, v_norm_ffn, v_norm_final, v_mix_w_in, v_mix_w_out, v_s5_A_re, v_s5_A_im, v_s5_log_dt, v_s5_B_re, v_s5_B_im, v_s5_C_re, v_s5_C_im, v_s5_D, v_s5_glu_w, v_s5_glu_b, v_hgrn_gamma, v_hgrn_norm, v_att_w_qkv, v_att_w_o, v_ffn_w_in, v_ffn_conv_w, v_ffn_conv_b, v_ffn_w_out):
    W = dict(norm_mix=norm_mix, norm_ffn=norm_ffn, norm_final=norm_final, mix_w_in=mix_w_in, mix_w_out=mix_w_out,
             s5_A_re=s5_A_re, s5_A_im=s5_A_im, s5_log_dt=s5_log_dt, s5_B_re=s5_B_re, s5_B_im=s5_B_im,
             s5_C_re=s5_C_re, s5_C_im=s5_C_im, s5_D=s5_D, s5_glu_w=s5_glu_w, s5_glu_b=s5_glu_b,
             hgrn_gamma=hgrn_gamma, hgrn_norm=hgrn_norm, att_w_qkv=att_w_qkv, att_w_o=att_w_o, ffn_w_in=ffn_w_in,
             ffn_conv_w=ffn_conv_w, ffn_conv_b=ffn_conv_b, ffn_w_out=ffn_w_out)
    M = dict(norm_mix=m_norm_mix, norm_ffn=m_norm_ffn, norm_final=m_norm_final, mix_w_in=m_mix_w_in,
             mix_w_out=m_mix_w_out, s5_A_re=m_s5_A_re, s5_A_im=m_s5_A_im, s5_log_dt=m_s5_log_dt, s5_B_re=m_s5_B_re,
             s5_B_im=m_s5_B_im, s5_C_re=m_s5_C_re, s5_C_im=m_s5_C_im, s5_D=m_s5_D, s5_glu_w=m_s5_glu_w,
             s5_glu_b=m_s5_glu_b, hgrn_gamma=m_hgrn_gamma, hgrn_norm=m_hgrn_norm, att_w_qkv=m_att_w_qkv,
             att_w_o=m_att_w_o, ffn_w_in=m_ffn_w_in, ffn_conv_w=m_ffn_conv_w, ffn_conv_b=m_ffn_conv_b,
             ffn_w_out=m_ffn_w_out)
    V = dict(norm_mix=v_norm_mix, norm_ffn=v_norm_ffn, norm_final=v_norm_final, mix_w_in=v_mix_w_in,
             mix_w_out=v_mix_w_out, s5_A_re=v_s5_A_re, s5_A_im=v_s5_A_im, s5_log_dt=v_s5_log_dt, s5_B_re=v_s5_B_re,
             s5_B_im=v_s5_B_im, s5_C_re=v_s5_C_re, s5_C_im=v_s5_C_im, s5_D=v_s5_D, s5_glu_w=v_s5_glu_w,
             s5_glu_b=v_s5_glu_b, hgrn_gamma=v_hgrn_gamma, hgrn_norm=v_hgrn_norm, att_w_qkv=v_att_w_qkv,
             att_w_o=v_att_w_o, ffn_w_in=v_ffn_w_in, ffn_conv_w=v_ffn_conv_w, ffn_conv_b=v_ffn_conv_b,
             ffn_w_out=v_ffn_w_out)
    return _step(x[0], positions[0], loss_target[0], W, M, V)


TRANSPOSED = ("mix_w_in", "att_w_qkv", "ffn_w_in")
SMALL = ("norm_mix", "norm_ffn", "norm_final", "s5_A_re", "s5_A_im", "s5_log_dt", "s5_B_re", "s5_B_im", "s5_C_re",
         "s5_C_im", "s5_D", "s5_glu_b", "hgrn_gamma", "hgrn_norm", "ffn_conv_b")
ORDER = ("norm_mix", "norm_ffn", "norm_final", "mix_w_in", "mix_w_out", "s5_A_re", "s5_A_im", "s5_log_dt", "s5_B_re",
         "s5_B_im", "s5_C_re", "s5_C_im", "s5_D", "s5_glu_w", "s5_glu_b", "hgrn_gamma", "hgrn_norm", "att_w_qkv",
         "att_w_o", "ffn_w_in", "ffn_conv_w", "ffn_conv_b", "ffn_w_out")
PACK_COLS = 1024


def _step(x, positions, target, W, M, V):
    L, D = x.shape
    me = 4 * lax.axis_index("x") + 2 * lax.axis_index("y") + lax.axis_index("c")
    n_cw = W["ffn_conv_w"].shape[-1]
    T = {n: tuple(jnp.swapaxes(d[n], -1, -2) for d in (W, M, V)) for n in TRANSPOSED}
    shards = {
        "mix_w_in": cast_bf16(T["mix_w_in"][0][0], name="mix_w_in_cast"),
        "conv_w": W["ffn_conv_w"].reshape(6, n_cw),
        "s5_glu_w": cast_bf16(W["s5_glu_w"][0], name="s5_glu_w_cast"),
        "mix_w_out": cast_bf16(W["mix_w_out"][0], name="mix_w_out_cast"),
        "ffn_w_in0": cast_bf16_layer(T["ffn_w_in"][0], 0, name="ffn_w_in0_cast"),
        "ffn_w_out0": cast_bf16_layer(W["ffn_w_out"], 0, name="ffn_w_out0_cast"),
        "att_w_qkv": cast_bf16(T["att_w_qkv"][0][0], name="att_w_qkv_cast"),
        "att_w_o": cast_bf16(W["att_w_o"][0], name="att_w_o_cast"),
        "ffn_w_in1": cast_bf16_layer(T["ffn_w_in"][0], 1, name="ffn_w_in1_cast"),
        "ffn_w_out1": cast_bf16_layer(W["ffn_w_out"], 1, name="ffn_w_out1_cast"),
    }
    gather_handles, token = copies_start(list(shards.values()), "gather2", name="gather_start")
    gather_handle = dict(zip(shards, gather_handles))

    def forward(keys, after, name):
        new, sent = copies_forward([gather_handle[k] for k in keys], after, name=name)
        gather_handle.update(zip(keys, new))
        return sent

    def gathered(key, after, cols):
        _, land = copies_wait(gather_handle[key], "gather2", after, name=key + "_gwait")
        return cols_from_shards(land, name=key + "_asm") if cols else land.reshape(-1, land.shape[-1])

    conv_b = W["ffn_conv_b"].reshape(2, 1, -1)

    s5_params = (W["s5_A_re"][0], W["s5_A_im"][0], W["s5_log_dt"][0], W["s5_B_re"][0], W["s5_B_im"][0],
                 W["s5_C_re"][0], W["s5_C_im"][0])
    (a_re, a_im, wb_re, wb_im, wc_re, wc_im), s5_prep_vjp = jax.vjp(_s5_prepare, *s5_params)
    dvec = W["s5_D"].reshape(1, S5_WIDTH)
    glu_b = W["s5_glu_b"].reshape(1, S5_WIDTH)
    lb, lb_vjp = jax.vjp(_lower_bound, W["hgrn_gamma"])
    hg_norm = W["hgrn_norm"].reshape(1, -1)
    tabs = rope_tables(positions)

    hn0 = rms_fwd(x, W["norm_mix"][0], dep=token, name="l0_rms")
    forward(["mix_w_in", "conv_w", "s5_glu_w"], hn0, "forward_a")
    w_mix_in = gathered("mix_w_in", hn0, False)
    proj = mm(hn0, w_mix_in, tb=True, name="l0_proj")
    y0, xs_re, xs_im = s5_core_fwd(proj, a_re, a_im, wb_re, wb_im, wc_re, wc_im, name="s5_core")
    w_glu = gathered("s5_glu_w", y0, False)
    oa = s5_out_fwd(y0, proj, dvec, w_glu, glu_b, name="s5_out")
    ob, hg_states = hgrn_fwd(proj, lb, hg_norm, name="hgrn_fwd")
    forward(["mix_w_out"], ob, "forward_b")
    cat = jnp.concatenate([oa, ob], axis=1)
    w_mix_out = gathered("mix_w_out", cat, False)
    h1 = mm(cat, w_mix_out, res=x, name="l0_mix_out")
    _, cw_all = copies_wait(gather_handle["conv_w"], "gather2", h1, name="conv_w_gwait")
    conv_w = cw_all.transpose(1, 0, 2).reshape(2, 3, N_DEV * n_cw)
    w_ffn_in, w_ffn_out = [None, None], [None, None]
    h2, ffn0_saved, w_ffn_in[0], w_ffn_out[0] = _ffn_fwd(
        h1, W["norm_ffn"][0],
        lambda a: (forward(["ffn_w_in0"], a, "forward_b2"), gathered("ffn_w_in0", a, False))[1], conv_w[0], conv_b[0],
        lambda a: (forward(["ffn_w_out0"], a, "forward_c"), gathered("ffn_w_out0", a, False))[1], "ffn0")

    forward(["att_w_qkv", "att_w_o"], h2, "forward_d")
    hn2 = rms_fwd(h2, W["norm_mix"][1], name="l1_rms")
    w_qkv = gathered("att_w_qkv", hn2, False)
    qkv_r = qkv_rope(hn2, w_qkv, tabs, name="l1_qkv")
    att_o, att_l = [], []
    for g, d in enumerate(ATT_DILATIONS):
        o_g, l_g = attn_fwd(qkv_r, g, d, name=f"attn_fwd{g}")
        att_o.append(o_g)
        att_l.append(l_g)
    o_att = merge_fwd(att_o, att_l, name="merge_fwd")
    forward(["ffn_w_in1", "ffn_w_out1"], o_att, "forward_e")
    w_o = gathered("att_w_o", o_att, True)
    h3 = mm(o_att, w_o, res=h2, name="l1_mix_out")
    h4, ffn1_saved, w_ffn_in[1], w_ffn_out[1] = _ffn_fwd(
        h3, W["norm_ffn"][1], lambda a: gathered("ffn_w_in1", a, False), conv_w[1], conv_b[1],
        lambda a: gathered("ffn_w_out1", a, False), "ffn1")

    exchanges = {}

    pending = []

    def send_grad(key, g, cols, flush=True):
        if cols:
            parts = shards_from_cols(g, name=key + "_split")
        else:
            parts = g.reshape(N_DEV, g.shape[0] // N_DEV, g.shape[1])
        pending.append((key, parts))
        if not flush:
            return None
        handles, sent = copies_start([p for _, p in pending], "exchange", name=key + "_xstart")
        exchanges.update(zip([k for k, _ in pending], handles))
        pending.clear()
        return sent

    loss, dh4, dg_final = final_loss(h4, W["norm_final"], target, name="final_loss")
    dh3, dg_ffn1, dcw1, dcb1 = _ffn_bwd(h3, W["norm_ffn"][1], w_ffn_in[1], conv_w[1], conv_b[1], w_ffn_out[1],
                                        ffn1_saved, dh4, "ffn1", lambda g: send_grad("ffn_w_in1", g, False),
                                        lambda g: send_grad("ffn_w_out1", g, False, flush=False))
    sent = send_grad("att_w_o", mm(o_att, dh3, ta=True, name="l1_dwo"), True, flush=False)
    d_oatt = mm(dh3, w_o, tb=True, dep=sent, name="l1_dmix")
    mb = merge_bwd(att_o, att_l, d_oatt, name="merge_bwd")
    d_slabs = [attn_bwd(qkv_r, g, att_l[g], mb[g], mb[3 + g], d, name=f"attn_bwd{g}")
               for g, d in enumerate(ATT_DILATIONS)]
    d_qkv = rope_bwd([s[0] for s in d_slabs] + [s[1] for s in d_slabs] + [s[2] for s in d_slabs], tabs,
                     name="rope_bwd")
    sent = send_grad("att_w_qkv", mm(d_qkv, hn2, ta=True, out_dtype=BF16, name="l1_dwqkv"), False)
    dh2, dg_mix1 = mm_drms(d_qkv, w_qkv, h2, W["norm_mix"][1], dh3, dep=sent, name="l1_dhn")

    dh1, dg_ffn0, dcw0, dcb0 = _ffn_bwd(h1, W["norm_ffn"][0], w_ffn_in[0], conv_w[0], conv_b[0], w_ffn_out[0],
                                        ffn0_saved, dh2, "ffn0", lambda g: send_grad("ffn_w_in0", g, False),
                                        lambda g: send_grad("ffn_w_out0", g, False, flush=False))
    sent = send_grad("mix_w_out", mm(cat, dh1, ta=True, out_dtype=BF16, name="l0_dwout"), False, flush=False)
    dcat = mm(dh1, w_mix_out, tb=True, dep=sent, name="l0_dcat")
    d_hg, dlb, dhg_norm = hgrn_bwd(proj, lb, hg_norm, hg_states, dcat, name="hgrn_bwd")
    dy, du_d, z_bf, dzg, dglu_b, dD = s5_out_bwd(y0, proj, dvec, w_glu, glu_b, dcat, name="s5_dout")
    sent_glu = send_grad("s5_glu_w", mm(z_bf, dzg, ta=True, out_dtype=BF16, name="s5_dglu"), False, flush=False)
    du, dwb_re, dwb_im, dwc_re, dwc_im, da_re, da_im = s5_core_bwd(
        dy, du_d, proj, xs_re, xs_im, a_re, a_im, wb_re, wb_im, wc_re, wc_im, name="s5_dcore")
    s5_small = s5_prep_vjp((da_re, da_im, dwb_re, dwb_im, dwc_re, dwc_im))
    d_proj = jnp.concatenate([du, d_hg], axis=1)
    sent = send_grad("mix_w_in", mm(d_proj, hn0, ta=True, out_dtype=BF16, dep=sent_glu, name="l0_dwin"), False)
    grad_x, dg_mix0 = mm_drms(d_proj, w_mix_in, x, W["norm_mix"][0], dh1, dep=sent, name="l0_dhn")
    (d_gamma,) = lb_vjp(dlb)
    out = {}

    dA_re, dA_im, dlog_dt, dB_re, dB_im, dC_re, dC_im = s5_small
    small_g = dict(norm_mix=jnp.concatenate([dg_mix0, dg_mix1], axis=0), norm_ffn=jnp.concatenate([dg_ffn0, dg_ffn1], axis=0),
                   norm_final=dg_final, s5_A_re=dA_re, s5_A_im=dA_im, s5_log_dt=dlog_dt, s5_B_re=dB_re, s5_B_im=dB_im,
                   s5_C_re=dC_re, s5_C_im=dC_im, s5_D=dD, s5_glu_b=dglu_b, hgrn_gamma=d_gamma, hgrn_norm=dhg_norm,
                   ffn_conv_b=jnp.concatenate([dcb0, dcb1], axis=0))
    conv_w_g = jnp.stack([dcw0, dcw1], axis=0)
    sizes = [math.prod(W[n].shape) for n in SMALL]
    n_conv = conv_w_g.size
    total = sum(sizes) + n_conv + 1
    rows = -(-total // PACK_COLS)
    rows = -(-rows // 8) * 8
    pad = rows * PACK_COLS - total

    def pack(vals, conv_part, last):
        flat = [v.reshape(-1).astype(F32) for v in vals] + [conv_part.reshape(-1), last.reshape(-1),
                                                            jnp.zeros((pad,), F32)]
        return jnp.concatenate(flat).reshape(rows, PACK_COLS)

    def conv_full(shard):
        col_owner = lax.broadcasted_iota(jnp.int32, (2, 3, N_DEV * n_cw), 2) // n_cw
        return jnp.where(col_owner == me, jnp.tile(shard, (1, 1, N_DEV)), 0.0)

    zero1 = jnp.zeros((1,), F32)
    g_pack = pack([small_g[n] for n in SMALL], conv_w_g, loss)
    w_pack = pack([W[n] for n in SMALL], conv_full(W["ffn_conv_w"]), zero1)
    m_pack = pack([M[n] for n in SMALL], conv_full(M["ffn_conv_w"]), zero1)
    v_pack = pack([V[n] for n in SMALL], conv_full(V["ffn_conv_w"]), zero1 + 1.0)
    (small_handle,), small_sent = copies_start([g_pack], "gather", name="small_xstart")

    def finish(name, n_layers):
        w3, m3, v3 = T[name] if name in TRANSPOSED else (W[name], M[name], V[name])
        res = None
        for layer in reversed(range(n_layers)):
            key = name if n_layers == 1 else f"{name}{layer}"
            own, recv = copies_wait(exchanges[key], "exchange", small_sent, name=key + "_xwait")
            _, R, Cn = recv.shape
            res = reduce_adamw(recv, own, True, me, w3.reshape(n_layers * R, Cn), m3.reshape(n_layers * R, Cn),
                               v3.reshape(n_layers * R, Cn), layer=layer, n_layers=n_layers, into=res,
                               name=key + "_adamw")
        res = [r.reshape(w3.shape) for r in res]
        return tuple(jnp.swapaxes(r, -1, -2) for r in res) if name in TRANSPOSED else tuple(res)

    for name in ("ffn_w_out", "ffn_w_in"):
        out[name] = finish(name, 2)
    for name in ("att_w_o", "att_w_qkv", "mix_w_out", "s5_glu_w", "mix_w_in"):
        out[name] = finish(name, 1)

    small_own, small_recv = copies_wait(small_handle, "gather", out["s5_glu_w"][0], name="small_xwait")
    res = reduce_adamw(small_recv, small_own, False, me, w_pack, m_pack, v_pack, name="small_adamw")
    flat = [r.reshape(-1) for r in res]
    off = 0
    for n, sz in zip(SMALL, sizes):
        out[n] = tuple(f[off:off + sz].reshape(W[n].shape) for f in flat)
        off += sz
    conv_res = [f[off:off + n_conv].reshape(2, 3, N_DEV * n_cw) for f in flat]
    out["ffn_conv_w"] = tuple(lax.dynamic_slice(c, (0, 0, me * n_cw), (2, 3, n_cw)) for c in conv_res)
    off += n_conv
    loss_total = flat[0][off]

    result = [loss_total, grad_x[None]]
    for k in range(4):
        result += [out[n][k] for n in ORDER]
    return tuple(result)
```

```python
import functools
import math

import jax
import jax.numpy as jnp
from jax import lax
from jax.experimental import pallas as pl
from jax.experimental.pallas import tpu as pltpu

F32 = jnp.float32
BF16 = jnp.bfloat16
MESH_ID = pl.DeviceIdType.MESH
N_DEV = 8
VMEM_LIMIT_BYTES = 56 * 1024 * 1024

NORM_EPS = 1e-6
S5_WIDTH, S5_GROUP, S5_GROUPS, S5_STATE = 512, 16, 32, 64
HG_HEADS, HG_DIM, HG_CHUNK = 4, 128, 64
HG_STEP_CHUNKS = 4
ATT_E, ATT_HPG, ATT_BLOCK = 64, 8, 128
ATT_DILATIONS = (1, 4, 16)
ROT_DIM, ROPE_THETA = 16, 500000.0
D_FF = 2816
ADAM_LR, ADAM_B1, ADAM_B2, ADAM_EPS, ADAM_WD, ADAM_STEP = 0.001, 0.9, 0.999, 1e-08, 0.01, 10
NEG_BIG = -1e30


def _params(**kw):
    return pltpu.CompilerParams(vmem_limit_bytes=VMEM_LIMIT_BYTES, **kw)


def _pick(n, cands):
    for c in cands:
        if n % c == 0:
            return c
    return n


def _dot(a, b):
    return jnp.dot(a.astype(BF16), b.astype(BF16), preferred_element_type=F32)


def _dot_nt(a, b):
    return lax.dot_general(a.astype(BF16), b.astype(BF16), (((1,), (1,)), ((), ())), preferred_element_type=F32)


def _dot_tn(a, b):
    return lax.dot_general(a.astype(BF16), b.astype(BF16), (((0,), (0,)), ((), ())), preferred_element_type=F32)


def _split2(x):
    hi = x.astype(BF16)
    return hi, (x - hi.astype(F32)).astype(BF16)


def _dot_x3(a, b, contract=((1,), (0,))):
    dn = (contract, ((), ()))
    a1, a2 = _split2(a)
    b1, b2 = _split2(b)
    return (lax.dot_general(a1, b1, dn, preferred_element_type=F32) + lax.dot_general(a1, b2, dn, preferred_element_type=F32)
            + lax.dot_general(a2, b1, dn, preferred_element_type=F32))


def _sigmoid(x):
    return 1.0 / (1.0 + jnp.exp(-x))


V7X_HBM_BYTES_PER_S = 3.2e12
V7X_MXU_FLOPS_PER_S = 0.7e15
GRID_STEP_S = 0.35e-6
MM_VMEM_BUDGET = 40 * 1024 * 1024


def _divisors(n, cands):
    return [c for c in cands if c <= n and n % c == 0] or [n]


def _mm_tiles(m, n, k, sa, sb, so, sr):
    best = None
    for tm in _divisors(m, (2816, 2048, 1408, 1024, 512, 256, 128)):
        for tn in _divisors(n, (2816, 2048, 1408, 1024, 512, 256, 128)):
            for tk in _divisors(k, (k, 2816, 2560, 2304, 2048, 1536, 1408, 1280, 1024, 512, 256, 128)):
                nk = k // tk
                vmem = 2 * (tm * tk * sa + tk * tn * sb + tm * tn * (so + sr)) + (tm * tn * 4 if nk > 1 else 0)
                vmem += tm * tk * 2 * (sa > 2) + tk * tn * 2 * (sb > 2) + tm * tn * 4
                if vmem > MM_VMEM_BUDGET:
                    continue
                ni, nj = m // tm, n // tn
                for i_outer in (True, False):
                    if i_outer:
                        a_reads = 1 if nk == 1 else nj
                        b_reads = 1 if (nk == 1 and nj == 1) else ni
                    else:
                        b_reads = 1 if nk == 1 else ni
                        a_reads = 1 if (nk == 1 and ni == 1) else nj
                    traffic = a_reads * m * k * sa + b_reads * k * n * sb + m * n * (so + sr)
                    t = max(traffic / V7X_HBM_BYTES_PER_S, 2.0 * m * n * k / V7X_MXU_FLOPS_PER_S)
                    t += ni * nj * nk * GRID_STEP_S
                    t += (tm * tk * sa + tk * tn * sb + tm * tn * so) / V7X_HBM_BYTES_PER_S
                    if best is None or t < best[0]:
                        best = (t, tm, tn, tk, i_outer)
    assert best is not None, (m, n, k)
    return best[1:]


def mm(a, b, *, ta=False, tb=False, res=None, out_dtype=F32, dep=None, name):
    m, k = (a.shape[1], a.shape[0]) if ta else a.shape
    n = b.shape[0] if tb else b.shape[1]
    assert (b.shape[1] if tb else b.shape[0]) == k
    has_res = res is not None
    tm, tn, tk, i_outer = _mm_tiles(m, n, k, a.dtype.itemsize, b.dtype.itemsize, jnp.dtype(out_dtype).itemsize,
                                    res.dtype.itemsize if has_res else 0)
    nk = k // tk
    deps = [] if dep is None else [dep]
    dn = (((0 if ta else 1,), (1 if tb else 0,)), ((), ()))

    def body_single(*refs):
        a_ref, b_ref = refs[:2]
        o_ref = refs[-1]
        out = lax.dot_general(a_ref[...].astype(BF16), b_ref[...].astype(BF16), dn, preferred_element_type=F32)
        if has_res:
            out = out + refs[2][...].astype(F32)
        o_ref[...] = out.astype(o_ref.dtype)

    def body(*refs):
        a_ref, b_ref = refs[:2]
        r_ref = refs[2] if has_res else None
        o_ref, acc_ref = refs[-2:]
        kk = pl.program_id(2)
        part = lax.dot_general(a_ref[...].astype(BF16), b_ref[...].astype(BF16), dn, preferred_element_type=F32)

        @pl.when(kk == 0)
        def _():
            acc_ref[...] = part

        @pl.when(kk > 0)
        def _():
            acc_ref[...] += part

        @pl.when(kk == nk - 1)
        def _():
            out = acc_ref[...]
            if has_res:
                out = out + r_ref[...].astype(F32)
            o_ref[...] = out.astype(o_ref.dtype)

    def ij(f):
        return (lambda g0, g1, q: f(g0, g1, q)) if i_outer else (lambda g0, g1, q: f(g1, g0, q))

    a_spec = pl.BlockSpec((tk, tm), ij(lambda i, j, q: (q, i))) if ta else pl.BlockSpec((tm, tk), ij(lambda i, j, q: (i, q)))
    b_spec = pl.BlockSpec((tn, tk), ij(lambda i, j, q: (j, q))) if tb else pl.BlockSpec((tk, tn), ij(lambda i, j, q: (q, j)))
    o_spec = pl.BlockSpec((tm, tn), ij(lambda i, j, q: (i, j)))
    in_specs = [a_spec, b_spec] + ([o_spec] if has_res else []) + [pl.BlockSpec((8, 128), lambda g0, g1, q: (0, 0))] * len(deps)
    args = (a, b) + ((res,) if has_res else ()) + tuple(deps)
    grid = (m // tm, n // tn, nk) if i_outer else (n // tn, m // tm, nk)
    return pl.pallas_call(
        body_single if nk == 1 else body, grid=grid, in_specs=in_specs, out_specs=o_spec,
        out_shape=jax.ShapeDtypeStruct((m, n), out_dtype),
        scratch_shapes=[] if nk == 1 else [pltpu.VMEM((tm, tn), F32)],
        compiler_params=_params(dimension_semantics=("parallel", "parallel", "arbitrary")), name=name,
    )(*args)


def mm_drms(dy_in, w, x, g, dres, *, dep=None, name):
    m, k = dy_in.shape
    D = w.shape[1]
    tm = _pick(m, (512, 256, 128))
    tk = max(_divisors(k, (1536, 1408, 1280, 1024, 512, 256, 128)))
    nk = k // tk
    deps = [] if dep is None else [dep]

    def body(a_ref, b_ref, x_ref, g_ref, dres_ref, *rest):
        dx_ref, dg_ref, acc_ref = rest[-3:]
        i, q = pl.program_id(0), pl.program_id(1)
        part = jnp.dot(a_ref[...], b_ref[...], preferred_element_type=F32)

        @pl.when(q == 0)
        def _():
            acc_ref[...] = part

        @pl.when(q > 0)
        def _():
            acc_ref[...] += part

        @pl.when((i == 0) & (q == 0))
        def _():
            dg_ref[...] = jnp.zeros_like(dg_ref)

        @pl.when(q == nk - 1)
        def _():
            dyv = acc_ref[...]
            xv = x_ref[...]
            r = lax.rsqrt(jnp.mean(xv * xv, axis=-1, keepdims=True) + NORM_EPS)
            xh = xv * r
            dg_ref[...] += jnp.sum(dyv * xh, axis=0, keepdims=True)
            dxh = dyv * g_ref[...]
            dx_ref[...] = dres_ref[...] + r * (dxh - xh * jnp.mean(dxh * xh, axis=-1, keepdims=True))

    row = pl.BlockSpec((tm, D), lambda i, q: (i, 0))
    vec = pl.BlockSpec((1, D), lambda i, q: (0, 0))
    in_specs = [pl.BlockSpec((tm, tk), lambda i, q: (i, q)), pl.BlockSpec((tk, D), lambda i, q: (q, 0)), row, vec, row]
    in_specs += [pl.BlockSpec((8, 128), lambda i, q: (0, 0))] * len(deps)
    return pl.pallas_call(
        body, grid=(m // tm, nk), in_specs=in_specs, out_specs=[row, vec],
        out_shape=[jax.ShapeDtypeStruct((m, D), F32), jax.ShapeDtypeStruct((1, D), F32)],
        scratch_shapes=[pltpu.VMEM((tm, D), F32)],
        compiler_params=_params(dimension_semantics=("arbitrary", "arbitrary")), name=name,
    )(dy_in, w, x, g.reshape(1, D), dres, *deps)


def final_loss(h, g, target, *, name):
    L, D = h.shape
    tr = _pick(L, (256, 128))

    def body(x_ref, g_ref, t_ref, loss_ref, dx_ref, dg_ref):
        xv = x_ref[...]
        gv = g_ref[...]
        r = lax.rsqrt(jnp.mean(xv * xv, axis=-1, keepdims=True) + NORM_EPS)
        xh = xv * r
        err = xh * gv - t_ref[...]

        @pl.when(pl.program_id(0) == 0)
        def _():
            dg_ref[...] = jnp.zeros_like(dg_ref)
            loss_ref[...] = jnp.zeros_like(loss_ref)

        loss_ref[...] += 0.5 * jnp.sum(jnp.mean(err * err, axis=-1, keepdims=True), axis=0, keepdims=True)
        dyv = err * (1.0 / D)
        dg_ref[...] += jnp.sum(dyv * xh, axis=0, keepdims=True)
        dxh = dyv * gv
        dx_ref[...] = r * (dxh - xh * jnp.mean(dxh * xh, axis=-1, keepdims=True))

    row = pl.BlockSpec((tr, D), lambda i: (i, 0))
    vec = pl.BlockSpec((1, D), lambda i: (0, 0))
    one = pl.BlockSpec((1, 1), lambda i: (0, 0))
    return pl.pallas_call(body, grid=(L // tr,), in_specs=[row, vec, row], out_specs=[one, row, vec],
                          out_shape=[jax.ShapeDtypeStruct((1, 1), F32), jax.ShapeDtypeStruct((L, D), F32),
                                     jax.ShapeDtypeStruct((1, D), F32)],
                          compiler_params=_params(dimension_semantics=("arbitrary",)), name=name)(
        h, g.reshape(1, D), target)


def _cmul(ar, ai, br, bi):
    return ar * br - ai * bi, ar * bi + ai * br


def _powers(ar, ai):
    rows = [(ar, ai)]
    for _ in range(7):
        rows.append(_cmul(rows[-1][0], rows[-1][1], ar, ai))
    table = (jnp.concatenate([r[0] for r in rows], axis=0), jnp.concatenate([r[1] for r in rows], axis=0))
    return (rows[0], rows[1], rows[3]), table


def _block_scan(br, bi, steps, shift):
    yr, yi = br, bi
    for s, (pr, pi) in zip((1, 2, 4), steps):
        sr, si = shift(yr, s), shift(yi, s)
        yr, yi = yr + pr * sr - pi * si, yi + pr * si + pi * sr
    return yr, yi


def s5_core_fwd(proj, a_re, a_im, wb_re, wb_im, wc_re, wc_im, *, name):
    L = proj.shape[0]
    parts, cu, W = wb_re.shape

    def body(u_ref, ar_ref, ai_ref, wbr_ref, wbi_ref, wcr_ref, wci_ref, y_ref, xr_ref, xi_ref, br_ref, bi_ref):
        u = u_ref[...]
        br_ref[...] = _dot(u, wbr_ref[...])
        bi_ref[...] = _dot(u, wbi_ref[...])
        steps, (tr, ti) = _powers(ar_ref[...], ai_ref[...])
        row = lax.broadcasted_iota(jnp.int32, (8, W), 0)

        def shift(y, s):
            return jnp.where(row >= s, pltpu.roll(y, s, 0), 0.0)

        def step(t8, carry):
            cr, ci = carry
            base = pl.multiple_of(t8 * 8, 8)
            yr, yi = _block_scan(br_ref[pl.ds(base, 8), :], bi_ref[pl.ds(base, 8), :], steps, shift)
            xr = yr + tr * cr - ti * ci
            xi = yi + tr * ci + ti * cr
            xr_ref[pl.ds(base, 8), :] = xr
            xi_ref[pl.ds(base, 8), :] = xi
            return jnp.broadcast_to(xr[7:8, :], (8, W)), jnp.broadcast_to(xi[7:8, :], (8, W))

        zero = jnp.zeros((8, W), F32)
        lax.fori_loop(0, L // 8, step, (zero, zero), unroll=2)
        y_ref[...] = _dot(xr_ref[...], wcr_ref[...]) + _dot(xi_ref[...], wci_ref[...])

    ucol = pl.BlockSpec((L, cu), lambda t: (0, t))
    vec = pl.BlockSpec((1, W), lambda t: (0, t))
    col = pl.BlockSpec((L, W), lambda t: (0, t))
    wb = pl.BlockSpec((None, cu, W), lambda t: (t, 0, 0))
    wc = pl.BlockSpec((None, W, cu), lambda t: (t, 0, 0))
    return pl.pallas_call(body, grid=(parts,), in_specs=[ucol, vec, vec, wb, wb, wc, wc], out_specs=[ucol, col, col],
                          out_shape=[jax.ShapeDtypeStruct((L, parts * cu), F32)]
                          + [jax.ShapeDtypeStruct((L, parts * W), F32)] * 2,
                          scratch_shapes=[pltpu.VMEM((L, W), F32)] * 2,
                          compiler_params=_params(dimension_semantics=("parallel",)), name=name)(
        proj, a_re, a_im, wb_re, wb_im, wc_re, wc_im)


def s5_core_bwd(dy, du_d, proj, xs_re, xs_im, a_re, a_im, wb_re, wb_im, wc_re, wc_im, *, name):
    L = proj.shape[0]
    parts, cu, W = wb_re.shape

    def body(dy_ref, dud_ref, u_ref, xr_ref, xi_ref, ar_ref, ai_ref, wbr_ref, wbi_ref, wcr_ref, wci_ref,
             du_ref, dwbr_ref, dwbi_ref, dwcr_ref, dwci_ref, dar_ref, dai_ref, lr_ref, li_ref):
        dy = dy_ref[...]
        lr_ref[...] = _dot_nt(dy, wcr_ref[...])
        li_ref[...] = _dot_nt(dy, wci_ref[...])
        dwcr_ref[...] = _dot_tn(xr_ref[...], dy)
        dwci_ref[...] = _dot_tn(xi_ref[...], dy)
        ar, ai = ar_ref[...], -ai_ref[...]
        steps, (tr, ti) = _powers(ar, ai)
        tr = jnp.concatenate([tr[j:j + 1, :] for j in range(7, -1, -1)], axis=0)
        ti = jnp.concatenate([ti[j:j + 1, :] for j in range(7, -1, -1)], axis=0)
        row8 = lax.broadcasted_iota(jnp.int32, (8, W), 0)
        nblk = L // 8

        def shift(y, s):
            return jnp.where(row8 < 8 - s, pltpu.roll(y, 8 - s, 0), 0.0)

        def step(s, carry):
            cr, ci = carry
            base = pl.multiple_of((nblk - 1 - s) * 8, 8)
            yr, yi = _block_scan(lr_ref[pl.ds(base, 8), :], li_ref[pl.ds(base, 8), :], steps, shift)
            lr = yr + tr * cr - ti * ci
            li = yi + tr * ci + ti * cr
            lr_ref[pl.ds(base, 8), :] = lr
            li_ref[pl.ds(base, 8), :] = li
            return jnp.broadcast_to(lr[0:1, :], (8, W)), jnp.broadcast_to(li[0:1, :], (8, W))

        zero = jnp.zeros((8, W), F32)
        lax.fori_loop(0, nblk, step, (zero, zero), unroll=2)
        row = lax.broadcasted_iota(jnp.int32, (L, W), 0)
        xpr = jnp.where(row >= 1, pltpu.roll(xr_ref[...], 1, 0), 0.0)
        xpi = jnp.where(row >= 1, pltpu.roll(xi_ref[...], 1, 0), 0.0)
        lr, li = lr_ref[...], li_ref[...]
        dar_ref[...] = jnp.sum(lr * xpr + li * xpi, axis=0, keepdims=True)
        dai_ref[...] = jnp.sum(li * xpr - lr * xpi, axis=0, keepdims=True)
        u = u_ref[...]
        dwbr_ref[...] = _dot_tn(u, lr)
        dwbi_ref[...] = _dot_tn(u, li)
        du_ref[...] = (dud_ref[...] + _dot_nt(lr, wbr_ref[...]) + _dot_nt(li, wbi_ref[...])).astype(du_ref.dtype)

    ucol = pl.BlockSpec((L, cu), lambda t: (0, t))
    vec = pl.BlockSpec((1, W), lambda t: (0, t))
    col = pl.BlockSpec((L, W), lambda t: (0, t))
    wb = pl.BlockSpec((None, cu, W), lambda t: (t, 0, 0))
    wc = pl.BlockSpec((None, W, cu), lambda t: (t, 0, 0))
    return pl.pallas_call(
        body, grid=(parts,), in_specs=[ucol, ucol, ucol, col, col, vec, vec, wb, wb, wc, wc],
        out_specs=[ucol, wb, wb, wc, wc, vec, vec],
        out_shape=[jax.ShapeDtypeStruct((L, parts * cu), BF16)] + [jax.ShapeDtypeStruct((parts, cu, W), F32)] * 2
        + [jax.ShapeDtypeStruct((parts, W, cu), F32)] * 2 + [jax.ShapeDtypeStruct((1, parts * W), F32)] * 2,
        scratch_shapes=[pltpu.VMEM((L, W), F32)] * 2,
        compiler_params=_params(dimension_semantics=("parallel",)), name=name,
    )(dy, du_d, proj, xs_re, xs_im, a_re, a_im, wb_re, wb_im, wc_re, wc_im)


def _gelu(y):
    c = math.sqrt(2.0 / math.pi)
    t = jnp.tanh(c * (y + 0.044715 * y * y * y))
    return 0.5 * y * (1.0 + t), t


def s5_out_fwd(y0, proj, dvec, glu_w, glu_b, *, name):
    L, C = y0.shape
    tr = _pick(L, (256, 128))

    def body(y_ref, u_ref, d_ref, w_ref, b_ref, o_ref):
        z, _ = _gelu(y_ref[...] + d_ref[...] * u_ref[...])
        zg = _dot(z, w_ref[...]) + b_ref[...]
        o_ref[...] = (z * _sigmoid(zg)).astype(o_ref.dtype)

    row = pl.BlockSpec((tr, C), lambda i: (i, 0))
    vec = pl.BlockSpec((1, C), lambda i: (0, 0))
    wsp = pl.BlockSpec((C, C), lambda i: (0, 0))
    return pl.pallas_call(body, grid=(L // tr,), in_specs=[row, row, vec, wsp, vec], out_specs=row,
                          out_shape=jax.ShapeDtypeStruct((L, C), BF16), name=name)(
        y0, proj, dvec, glu_w, glu_b)


def s5_out_bwd(y0, proj, dvec, glu_w, glu_b, dcat, *, name):
    L, C = y0.shape
    tr = _pick(L, (256, 128))

    def body(y_ref, u_ref, d_ref, w_ref, b_ref, do_ref, dy_ref, dud_ref, z_ref, dzg_ref, db_ref, dd_ref):
        u = u_ref[...]
        y = y_ref[...] + d_ref[...] * u
        z, t = _gelu(y)
        zg = _dot(z, w_ref[...]) + b_ref[...]
        s = _sigmoid(zg)
        do = do_ref[...]
        dzg = do * z * s * (1.0 - s)
        dz = do * s + _dot_nt(dzg, w_ref[...])
        c = math.sqrt(2.0 / math.pi)
        dgelu = 0.5 * (1.0 + t) + 0.5 * y * (1.0 - t * t) * c * (1.0 + 3.0 * 0.044715 * y * y)
        dy = dz * dgelu

        @pl.when(pl.program_id(0) == 0)
        def _():
            db_ref[...] = jnp.zeros_like(db_ref)
            dd_ref[...] = jnp.zeros_like(dd_ref)

        db_ref[...] += jnp.sum(dzg, axis=0, keepdims=True)
        dd_ref[...] += jnp.sum(dy * u, axis=0, keepdims=True)
        dy_ref[...] = dy
        dud_ref[...] = dy * d_ref[...]
        z_ref[...] = z.astype(BF16)
        dzg_ref[...] = dzg.astype(BF16)

    row = pl.BlockSpec((tr, C), lambda i: (i, 0))
    vec = pl.BlockSpec((1, C), lambda i: (0, 0))
    wsp = pl.BlockSpec((C, C), lambda i: (0, 0))
    return pl.pallas_call(body, grid=(L // tr,), in_specs=[row, row, vec, wsp, vec, row],
                          out_specs=[row, row, row, row, vec, vec],
                          out_shape=[jax.ShapeDtypeStruct((L, C), F32), jax.ShapeDtypeStruct((L, C), F32),
                                     jax.ShapeDtypeStruct((L, C), BF16), jax.ShapeDtypeStruct((L, C), BF16),
                                     jax.ShapeDtypeStruct((1, C), F32), jax.ShapeDtypeStruct((1, C), F32)],
                          compiler_params=_params(dimension_semantics=("arbitrary",)), name=name)(
        y0, proj, dvec, glu_w, glu_b, dcat)


def _dot_tri(tri, x, tri_left=True):
    t = tri.astype(BF16)
    x1 = x.astype(BF16)
    r1 = x - x1.astype(F32)
    x2 = r1.astype(BF16)
    x3 = (r1 - x2.astype(F32)).astype(BF16)
    dot = (lambda p: jnp.dot(t, p, preferred_element_type=F32)) if tri_left else (
        lambda p: jnp.dot(p, t, preferred_element_type=F32))
    return dot(x1) + dot(x2) + dot(x3)


def _hg_gates(xq, xf, lb, tri):
    C = xq.shape[0]
    sq = _sigmoid(xq)
    q = xq * sq
    sg = _sigmoid(xf)
    f = lb + (1.0 - lb) * sg
    kk = 1.0 - f
    b = _dot_tri(tri, jnp.log(f))
    bm = b[C // 2 - 1:C // 2, :]
    bl = b[C - 1:C, :]
    eb = jnp.exp(b)
    eqm, ekm, ekl = jnp.exp(b - bm), jnp.exp(bm - b), jnp.exp(bl - b)
    return dict(sq=sq, q=q, sg=sg, f=f, kk=kk, eb=eb, ebl=jnp.exp(bl), eqm=eqm, ekm=ekm, ekl=ekl,
                qb=q * eb, qt=q * eqm, kt=kk * ekm, kh=kk * ekl)


def _tri(C, lower):
    r = lax.broadcasted_iota(jnp.int32, (C, C), 0)
    c = lax.broadcasted_iota(jnp.int32, (C, C), 1)
    return (r >= c) if lower else (c >= r)


def hgrn_fwd(proj, lb, norm_g, *, name):
    L = proj.shape[0]
    C, H, K = HG_CHUNK, HG_HEADS, HG_DIM
    HK = H * K
    nc = L // C

    def body(q_ref, f_ref, i_ref, g_ref, lb_ref, ng_ref, o_ref, sall_ref, st_ref):
        @pl.when(pl.program_id(0) == 0)
        def _():
            st_ref[...] = jnp.zeros_like(st_ref)

        mask = _tri(C, True)
        sts = [st_ref[h] for h in range(H)]
        for s in range(S):
            rs = slice(s * C, (s + 1) * C)
            gt = _hg_gates(q_ref[rs, :], f_ref[rs, :], lb_ref[...], mask.astype(F32))
            v_all = i_ref[rs, :]
            outs = []
            for h in range(H):
                sl = slice(h * K, (h + 1) * K)
                v, st = v_all[:, sl], sts[h]
                sall_ref[s, h] = st
                att = jnp.where(mask, _dot_nt(gt["qt"][:, sl], gt["kt"][:, sl]), 0.0)
                o = _dot(att, v) + _dot_nt(gt["qb"][:, sl], st)
                sts[h] = st * gt["ebl"][:, sl] + _dot_tn(v, gt["kh"][:, sl])
                outs.append(o * lax.rsqrt(jnp.mean(o * o, axis=-1, keepdims=True) + NORM_EPS))
            xg = g_ref[rs, :]
            o_ref[rs, :] = (jnp.concatenate(outs, axis=1) * ng_ref[...] * (xg * _sigmoid(xg))).astype(o_ref.dtype)
        for h in range(H):
            st_ref[h] = sts[h]

    S = HG_STEP_CHUNKS

    def blk(cb):
        return pl.BlockSpec((S * C, HK), lambda i: (i, cb))

    vec = pl.BlockSpec((1, HK), lambda i: (0, 0))
    return pl.pallas_call(
        body, grid=(nc // S,), in_specs=[blk(1), blk(2), blk(3), blk(4), vec, vec],
        out_specs=[pl.BlockSpec((S * C, HK), lambda i: (i, 0)), pl.BlockSpec((S, H, K, K), lambda i: (i, 0, 0, 0))],
        out_shape=[jax.ShapeDtypeStruct((L, HK), BF16), jax.ShapeDtypeStruct((nc, H, K, K), F32)],
        scratch_shapes=[pltpu.VMEM((H, K, K), F32)],
        compiler_params=_params(dimension_semantics=("arbitrary",)), name=name,
    )(proj, proj, proj, proj, lb, norm_g)


def hgrn_bwd(proj, lb, norm_g, sall, dcat, *, name):
    L = proj.shape[0]
    C, H, K = HG_CHUNK, HG_HEADS, HG_DIM
    HK = H * K
    nc = L // C

    def body(q_ref, f_ref, i_ref, g_ref, lb_ref, ng_ref, sall_ref, do_ref, dx_ref, dlb_ref, dng_ref, dst_ref):
        @pl.when(pl.program_id(0) == 0)
        def _():
            dst_ref[...] = jnp.zeros_like(dst_ref)
            dlb_ref[...] = jnp.zeros_like(dlb_ref)
            dng_ref[...] = jnp.zeros_like(dng_ref)

        mask = _tri(C, True)
        lb_all, ng = lb_ref[...], ng_ref[...]
        dsts = [dst_ref[h] for h in range(H)]
        for s in reversed(range(S)):
            rs = slice(s * C, (s + 1) * C)
            dsts = chunk_bwd(rs, s, dsts, mask, lb_all, ng, q_ref, f_ref, i_ref, g_ref, sall_ref, do_ref,
                             dx_ref, dlb_ref, dng_ref)
        for h in range(H):
            dst_ref[h] = dsts[h]

    def chunk_bwd(rs, s, dsts, mask, lb_all, ng, q_ref, f_ref, i_ref, g_ref, sall_ref, do_ref, dx_ref, dlb_ref, dng_ref):
        xq, xg, v_all = q_ref[rs, :], g_ref[rs, :], i_ref[rs, :]
        gt = _hg_gates(xq, f_ref[rs, :], lb_all, mask.astype(F32))
        sgg = _sigmoid(xg)
        d_ob = do_ref[rs, :]
        d_on = d_ob * (xg * sgg)
        doh = d_on * ng
        ohs, d_qts, d_qbs, d_kts, d_khs, dvs, d_bls, new_dsts = [], [], [], [], [], [], [], []
        for h in range(H):
            sl = slice(h * K, (h + 1) * K)
            v, st, dst = v_all[:, sl], sall_ref[s, h], dsts[h]
            qt, kt, kh, qb = gt["qt"][:, sl], gt["kt"][:, sl], gt["kh"][:, sl], gt["qb"][:, sl]
            att = jnp.where(mask, _dot_nt(qt, kt), 0.0)
            o = _dot(att, v) + _dot_nt(qb, st)
            r = lax.rsqrt(jnp.mean(o * o, axis=-1, keepdims=True) + NORM_EPS)
            oh = o * r
            do = r * (doh[:, sl] - oh * jnp.mean(doh[:, sl] * oh, axis=-1, keepdims=True))
            datt = jnp.where(mask, _dot_nt(do, v), 0.0)
            dvs.append(_dot_tn(att, do) + _dot_nt(kh, dst))
            d_qbs.append(_dot_x3(do, st))
            d_qts.append(_dot_x3(datt, kt))
            d_kts.append(_dot_x3(datt, qt, ((0,), (0,))))
            d_kh = _dot_x3(v, dst)
            d_khs.append(d_kh)
            d_bls.append(jnp.sum(dst * st, axis=0, keepdims=True) * gt["ebl"][:, sl]
                         + jnp.sum(d_kh * kh, axis=0, keepdims=True))
            new_dsts.append(dst * gt["ebl"][:, sl] + _dot_tn(do, qb))
            ohs.append(oh)
        oh, d_qt, d_qb, d_kt, d_kh, dv, d_bl = (jnp.concatenate(p, axis=1) for p in
                                                (ohs, d_qts, d_qbs, d_kts, d_khs, dvs, d_bls))
        dxg = d_ob * (oh * ng) * (sgg * (1.0 + xg * (1.0 - sgg)))
        dng_ref[...] += jnp.sum(d_on * oh, axis=0, keepdims=True)
        dq = d_qt * gt["eqm"] + d_qb * gt["eb"]
        db = d_qt * gt["qt"] + d_qb * gt["qb"] - d_kt * gt["kt"] - d_kh * gt["kh"]
        rowi = lax.broadcasted_iota(jnp.int32, (C, HK), 0)
        db = db + jnp.where(rowi == C - 1, d_bl, 0.0)
        dkk = d_kt * gt["ekm"] + d_kh * gt["ekl"]
        dlg = _dot_tri(_tri(C, False).astype(F32), db)
        df = dlg / gt["f"] - dkk
        sg, sq = gt["sg"], gt["sq"]
        dlb_ref[...] += jnp.sum(df * (1.0 - sg), axis=0, keepdims=True)
        dx_ref[rs, 0:HK] = (dq * (sq * (1.0 + xq * (1.0 - sq)))).astype(dx_ref.dtype)
        dx_ref[rs, HK:2 * HK] = (df * (1.0 - lb_all) * sg * (1.0 - sg)).astype(dx_ref.dtype)
        dx_ref[rs, 2 * HK:3 * HK] = dv.astype(dx_ref.dtype)
        dx_ref[rs, 3 * HK:4 * HK] = dxg.astype(dx_ref.dtype)
        return new_dsts

    S = HG_STEP_CHUNKS
    ns = nc // S

    def blk(cb):
        return pl.BlockSpec((S * C, HK), lambda i: (ns - 1 - i, cb))

    vec = pl.BlockSpec((1, HK), lambda i: (0, 0))
    return pl.pallas_call(
        body, grid=(ns,),
        in_specs=[blk(1), blk(2), blk(3), blk(4), vec, vec,
                  pl.BlockSpec((S, H, K, K), lambda i: (ns - 1 - i, 0, 0, 0)), blk(1)],
        out_specs=[pl.BlockSpec((S * C, 4 * HK), lambda i: (ns - 1 - i, 0)), vec, vec],
        out_shape=[jax.ShapeDtypeStruct((L, 4 * HK), BF16), jax.ShapeDtypeStruct((1, HK), F32),
                   jax.ShapeDtypeStruct((1, HK), F32)],
        scratch_shapes=[pltpu.VMEM((H, K, K), F32)],
        compiler_params=_params(dimension_semantics=("arbitrary",)), name=name,
    )(proj, proj, proj, proj, lb, norm_g, sall, dcat)


def _shift_down(x, k, row):
    return jnp.where(row >= k, pltpu.roll(x, k, 0), 0.0)


def _shift_up(x, k, row):
    n = x.shape[0]
    return jnp.where(row < n - k, pltpu.roll(x, n - k, 0), 0.0)


def convgate_fwd(hu, conv_w, conv_b, *, name):
    L, C2 = hu.shape
    C = C2 // 2
    tc = _pick(C, (256, 128))
    nb = C // tc

    def body(a_ref, b_ref, wa_ref, wb_ref, ba_ref, bb_ref, o_ref):
        row = lax.broadcasted_iota(jnp.int32, (L, tc), 0)

        def conv(x, w, bias):
            return w[2:3, :] * x + w[1:2, :] * _shift_down(x, 1, row) + w[0:1, :] * _shift_down(x, 2, row) + bias

        ca = conv(a_ref[...], wa_ref[...], ba_ref[...])
        cb = conv(b_ref[...], wb_ref[...], bb_ref[...])
        o_ref[...] = (ca * _sigmoid(ca) * cb).astype(o_ref.dtype)

    def col(off, rows):
        return pl.BlockSpec((rows, tc), lambda j: (0, j + off))

    return pl.pallas_call(
        body, grid=(nb,), in_specs=[col(0, L), col(nb, L), col(0, 3), col(nb, 3), col(0, 1), col(nb, 1)],
        out_specs=col(0, L), out_shape=jax.ShapeDtypeStruct((L, C), BF16),
        compiler_params=_params(dimension_semantics=("parallel",)), name=name,
    )(hu, hu, conv_w, conv_w, conv_b, conv_b)


def convgate_bwd(hu, conv_w, conv_b, dact, *, name):
    L, C2 = hu.shape
    C = C2 // 2
    tc = _pick(C, (256, 128))
    nb = C // tc

    def body(a_ref, b_ref, wa_ref, wb_ref, ba_ref, bb_ref, d_ref, dxa_ref, dxb_ref, dwa_ref, dwb_ref, dba_ref, dbb_ref):
        row = lax.broadcasted_iota(jnp.int32, (L, tc), 0)

        def conv(x, w, bias):
            x1 = _shift_down(x, 1, row)
            x2 = _shift_down(x, 2, row)
            return w[2:3, :] * x + w[1:2, :] * x1 + w[0:1, :] * x2 + bias, x1, x2

        xa, xb = a_ref[...], b_ref[...]
        wa, wb = wa_ref[...], wb_ref[...]
        ca, xa1, xa2 = conv(xa, wa, ba_ref[...])
        cb, xb1, xb2 = conv(xb, wb, bb_ref[...])
        d = d_ref[...]
        sa = _sigmoid(ca)
        dca = d * cb * (sa * (1.0 + ca * (1.0 - sa)))
        dcb = d * (ca * sa)

        def back(dc, w, x, x1, x2, dx_ref, dw_ref, db_ref):
            dx = w[2:3, :] * dc + w[1:2, :] * _shift_up(dc, 1, row) + w[0:1, :] * _shift_up(dc, 2, row)
            dx_ref[...] = dx.astype(dx_ref.dtype)
            dw_ref[...] = jnp.concatenate([jnp.sum(dc * x2, axis=0, keepdims=True),
                                           jnp.sum(dc * x1, axis=0, keepdims=True),
                                           jnp.sum(dc * x, axis=0, keepdims=True)], axis=0)
            db_ref[...] = jnp.sum(dc, axis=0, keepdims=True)

        back(dca, wa, xa, xa1, xa2, dxa_ref, dwa_ref, dba_ref)
        back(dcb, wb, xb, xb1, xb2, dxb_ref, dwb_ref, dbb_ref)

    def col(off, rows):
        return pl.BlockSpec((rows, tc), lambda j: (0, j + off))

    outs = pl.pallas_call(
        body, grid=(nb,),
        in_specs=[col(0, L), col(nb, L), col(0, 3), col(nb, 3), col(0, 1), col(nb, 1), col(0, L)],
        out_specs=[col(0, L), col(0, L), col(0, 3), col(0, 3), col(0, 1), col(0, 1)],
        out_shape=[jax.ShapeDtypeStruct((L, C), BF16)] * 2 + [jax.ShapeDtypeStruct((3, C), F32)] * 2
        + [jax.ShapeDtypeStruct((1, C), F32)] * 2,
        compiler_params=_params(dimension_semantics=("parallel",)), name=name,
    )(hu, hu, conv_w, conv_w, conv_b, conv_b, dact)
    dxa, dxb, dwa, dwb, dba, dbb = outs
    return (jnp.concatenate([dxa, dxb], axis=1), jnp.concatenate([dwa, dwb], axis=1),
            jnp.concatenate([dba, dbb], axis=1))


def rope_tables(positions):
    half = ROT_DIM // 2
    inv_freq = ROPE_THETA ** (-jnp.arange(half, dtype=F32) * 2.0 / ROT_DIM)
    ang = positions.astype(F32)[:, None] * inv_freq
    cos, sin = jnp.cos(ang), jnp.sin(ang)
    L = positions.shape[0]
    one = jnp.ones((L, ATT_E - ROT_DIM), F32)
    zero = jnp.zeros((L, ATT_E - ROT_DIM), F32)
    zh = jnp.zeros((L, half), F32)
    tc = jnp.concatenate([cos, cos, one], axis=1)
    ts1 = jnp.concatenate([zh, sin, zero], axis=1)
    ts2 = jnp.concatenate([-sin, zh, zero], axis=1)
    return tuple(jnp.concatenate([t, t], axis=1) for t in (tc, ts1, ts2))


def norm_mm(x, g, w_t, *, tabs=None, name):
    L, D = x.shape
    N = w_t.shape[0]
    W = 512
    tm = _pick(L, (1024, 512, 256, 128))
    nq = N // (3 * W)
    scale = ATT_E ** -0.5
    rope = tabs is not None

    def body(x_ref, g_ref, b_ref, *rest):
        hn_ref, o_ref, hn_scr = rest[-3:]
        j = pl.program_id(1)

        @pl.when(j == 0)
        def _():
            xv = x_ref[...]
            r = lax.rsqrt(jnp.mean(xv * xv, axis=-1, keepdims=True) + NORM_EPS)
            hn = (xv * r * g_ref[...]).astype(BF16)
            hn_scr[...] = hn
            hn_ref[...] = hn

        out = _dot_nt(hn_scr[...], b_ref[...])
        if rope:
            c_ref, s1_ref, s2_ref = rest[:3]
            c = jnp.concatenate([c_ref[...]] * 4, axis=1)
            s1 = jnp.concatenate([s1_ref[...]] * 4, axis=1)
            s2 = jnp.concatenate([s2_ref[...]] * 4, axis=1)
            rot = out * c + pltpu.roll(out, 8, 1) * s1 + pltpu.roll(out, W - 8, 1) * s2
            out = jnp.where(j < 2 * nq, rot * jnp.where(j < nq, scale, 1.0), out)
        o_ref[...] = out

    row = pl.BlockSpec((tm, D), lambda i, j: (i, 0))
    tab = pl.BlockSpec((tm, 128), lambda i, j: (i, 0))
    return pl.pallas_call(body, grid=(L // tm, N // W),
                          in_specs=[row, pl.BlockSpec((1, D), lambda i, j: (0, 0)), pl.BlockSpec((W, D), lambda i, j: (j, 0))]
                          + ([tab, tab, tab] if rope else []),
                          out_specs=[row, pl.BlockSpec((tm, W), lambda i, j: (i, j))],
                          out_shape=[jax.ShapeDtypeStruct((L, D), BF16), jax.ShapeDtypeStruct((L, N), F32)],
                          scratch_shapes=[pltpu.VMEM((tm, D), BF16)],
                          compiler_params=_params(dimension_semantics=("parallel", "arbitrary")), name=name)(
        x, g.reshape(1, D), w_t, *(tabs or ()))


def rope_bwd(slabs, tabs, *, name):
    L, W = slabs[0].shape
    tr = _pick(L, (256, 128))
    nq = len(slabs) // 3
    scale = ATT_E ** -0.5

    def body(*refs):
        d_refs, (c_ref, s1_ref, s2_ref, o_ref) = refs[:3 * nq], refs[3 * nq:]
        c = jnp.concatenate([c_ref[...]] * 4, axis=1)
        s1 = jnp.concatenate([s1_ref[...]] * 4, axis=1)
        s2 = jnp.concatenate([s2_ref[...]] * 4, axis=1)
        for j, d_ref in enumerate(d_refs):
            dy = d_ref[...]
            if j < 2 * nq:
                dy = dy * c + pltpu.roll(dy * s1, W - 8, 1) + pltpu.roll(dy * s2, 8, 1)
            if j < nq:
                dy = dy * scale
            o_ref[:, j * W:(j + 1) * W] = dy.astype(o_ref.dtype)

    slab = pl.BlockSpec((tr, W), lambda i: (i, 0))
    tab = pl.BlockSpec((tr, 128), lambda i: (i, 0))
    return pl.pallas_call(body, grid=(L // tr,), in_specs=[slab] * (3 * nq) + [tab, tab, tab],
                          out_specs=pl.BlockSpec((tr, 3 * nq * W), lambda i: (i, 0)),
                          out_shape=jax.ShapeDtypeStruct((L, 3 * nq * W), BF16),
                          compiler_params=_params(dimension_semantics=("parallel",)), name=name)(*slabs, *tabs)


def _att_masks(has_prev):
    qi = lax.broadcasted_iota(jnp.int32, (ATT_BLOCK, ATT_BLOCK), 0)
    kj = lax.broadcasted_iota(jnp.int32, (ATT_BLOCK, ATT_BLOCK), 1)
    return qi >= kj, (kj >= qi) & has_prev


ATT_COLS = 128


def _att_rows(j, d, nb):
    B = ATT_BLOCK
    r, n = j // nb, j % nb
    start = r + d * B * n
    has_prev = n > 0
    pstart = jnp.where(has_prev, start - d * B, start)
    if d == 1:
        return pl.ds(pl.multiple_of(start, B), B), pl.ds(pl.multiple_of(pstart, B), B), has_prev
    return pl.ds(start, B, stride=d), pl.ds(pstart, B, stride=d), has_prev


def _qkv_specs(L, g):
    per = ATT_HPG * ATT_E // ATT_COLS
    third = len(ATT_DILATIONS) * per
    return [pl.BlockSpec((L, ATT_COLS), lambda c, base=base: (0, base + c))
            for base in (g * per, third + g * per, 2 * third + g * per)]


def attn_fwd(qkv, g, d, *, name):
    L, W = qkv.shape[0], ATT_HPG * ATT_E
    B, E = ATT_BLOCK, ATT_E
    nblk = L // B
    nb = nblk // d

    def body(q_ref, k_ref, v_ref, o_ref, l_ref):
        def step(j, carry):
            cur, prv, has_prev = _att_rows(j, d, nb)
            mc, mp = _att_masks(has_prev)
            qb, kc, kp, vc, vp = q_ref[cur, :], k_ref[cur, :], k_ref[prv, :], v_ref[cur, :], v_ref[prv, :]
            outs, lses = [], []
            for h in range(ATT_COLS // E):
                sl = slice(h * E, (h + 1) * E)
                sc = jnp.where(mc, _dot_nt(qb[:, sl], kc[:, sl]), NEG_BIG)
                sp = jnp.where(mp, _dot_nt(qb[:, sl], kp[:, sl]), NEG_BIG)
                m = jnp.maximum(jnp.max(sc, axis=-1, keepdims=True), jnp.max(sp, axis=-1, keepdims=True))
                pc = jnp.exp(sc - m)
                pp = jnp.exp(sp - m)
                den = jnp.sum(pc, axis=-1, keepdims=True) + jnp.sum(pp, axis=-1, keepdims=True)
                outs.append((_dot(pc, vc[:, sl]) + _dot(pp, vp[:, sl])) / den)
                lses.append(jnp.broadcast_to(m + jnp.log(den), (B, E)))
            o_ref[cur, :] = jnp.concatenate(outs, axis=1)
            l_ref[cur, :] = jnp.concatenate(lses, axis=1)
            return carry

        lax.fori_loop(0, nblk, step, 0, unroll=4)

    col = pl.BlockSpec((L, ATT_COLS), lambda c: (0, c))
    return pl.pallas_call(body, grid=(W // ATT_COLS,), in_specs=_qkv_specs(L, g), out_specs=[col] * 2,
                          out_shape=[jax.ShapeDtypeStruct((L, W), F32)] * 2,
                          compiler_params=_params(dimension_semantics=("parallel",)), name=name)(qkv, qkv, qkv)


def attn_bwd(qkv, g, lse, do, dl, d, *, name):
    L, W = qkv.shape[0], ATT_HPG * ATT_E
    B, E = ATT_BLOCK, ATT_E
    nblk = L // B
    nb = nblk // d

    def body(q_ref, k_ref, v_ref, l_ref, do_ref, dl_ref, dq_ref, dk_ref, dv_ref):
        dk_ref[...] = jnp.zeros_like(dk_ref)
        dv_ref[...] = jnp.zeros_like(dv_ref)

        def step(j, carry):
            cur, prv, has_prev = _att_rows(j, d, nb)
            mc, mp = _att_masks(has_prev)
            qb, kc, kp, vc, vp = q_ref[cur, :], k_ref[cur, :], k_ref[prv, :], v_ref[cur, :], v_ref[prv, :]
            lb, dob, dlb = l_ref[cur, :], do_ref[cur, :], dl_ref[cur, :]
            dqs, dkc, dkp, dvc, dvp = [], [], [], [], []
            for h in range(ATT_COLS // E):
                sl = slice(h * E, (h + 1) * E)
                qh, doh = qb[:, sl], dob[:, sl]
                lse_h, dl_h = lb[:, h * E:h * E + 1], dlb[:, h * E:h * E + 1]
                pc = jnp.where(mc, jnp.exp(_dot_nt(qh, kc[:, sl]) - lse_h), 0.0)
                pp = jnp.where(mp, jnp.exp(_dot_nt(qh, kp[:, sl]) - lse_h), 0.0)
                dsc = pc * (_dot_nt(doh, vc[:, sl]) - dl_h)
                dsp = pp * (_dot_nt(doh, vp[:, sl]) - dl_h)
                dqs.append(_dot(dsc, kc[:, sl]) + _dot(dsp, kp[:, sl]))
                dkc.append(_dot_tn(dsc, qh))
                dkp.append(_dot_tn(dsp, qh))
                dvc.append(_dot_tn(pc, doh))
                dvp.append(_dot_tn(pp, doh))
            dq_ref[cur, :] = jnp.concatenate(dqs, axis=1)
            dk_ref[cur, :] = dk_ref[cur, :] + jnp.concatenate(dkc, axis=1)
            dv_ref[cur, :] = dv_ref[cur, :] + jnp.concatenate(dvc, axis=1)
            dk_ref[prv, :] = dk_ref[prv, :] + jnp.concatenate(dkp, axis=1)
            dv_ref[prv, :] = dv_ref[prv, :] + jnp.concatenate(dvp, axis=1)
            return carry

        lax.fori_loop(0, nblk, step, 0, unroll=4)

    col = pl.BlockSpec((L, ATT_COLS), lambda c: (0, c))
    return pl.pallas_call(body, grid=(W // ATT_COLS,), in_specs=_qkv_specs(L, g) + [col] * 3, out_specs=[col] * 3,
                          out_shape=[jax.ShapeDtypeStruct((L, W), F32)] * 3,
                          compiler_params=_params(dimension_semantics=("parallel",)), name=name)(
        qkv, qkv, qkv, lse, do, dl)


def _merge_alpha(l_refs):
    ls = [r[...] for r in l_refs]
    m = jnp.maximum(jnp.maximum(ls[0], ls[1]), ls[2])
    es = [jnp.exp(l - m) for l in ls]
    den = es[0] + es[1] + es[2]
    return [e / den for e in es]


def merge_fwd(os_, ls_, *, name):
    L, W = os_[0].shape
    tr = _pick(L, (256, 128))

    def body(o0, o1, o2, l0, l1, l2, out_ref):
        al = _merge_alpha((l0, l1, l2))
        out_ref[...] = (al[0] * o0[...] + al[1] * o1[...] + al[2] * o2[...]).astype(out_ref.dtype)

    row = pl.BlockSpec((tr, W), lambda i: (i, 0))
    return pl.pallas_call(body, grid=(L // tr,), in_specs=[row] * 6, out_specs=row,
                          out_shape=jax.ShapeDtypeStruct((L, W), BF16), name=name)(*os_, *ls_)


def merge_bwd(os_, ls_, do, *, name):
    L, W = do.shape
    tr = _pick(L, (256, 128))

    def body(o0, o1, o2, l0, l1, l2, do_ref, d0, d1, d2, e0, e1, e2):
        al = _merge_alpha((l0, l1, l2))
        dov = do_ref[...]
        r = lax.broadcasted_iota(jnp.int32, (W, W), 0) // ATT_E
        c = lax.broadcasted_iota(jnp.int32, (W, W), 1) // ATT_E
        ones_blk = (r == c).astype(F32)
        t = jnp.zeros_like(dov)
        for a, o in zip(al, (o0, o1, o2)):
            t = t + a * _dot_tri(ones_blk, dov * o[...], tri_left=False)
        for a, d_ref, e_ref in zip(al, (d0, d1, d2), (e0, e1, e2)):
            d_ref[...] = a * dov
            e_ref[...] = a * t

    row = pl.BlockSpec((tr, W), lambda i: (i, 0))
    return pl.pallas_call(body, grid=(L // tr,), in_specs=[row] * 7, out_specs=[row] * 6,
                          out_shape=[jax.ShapeDtypeStruct((L, W), F32)] * 6, name=name)(*os_, *ls_, do)


def _me_and_peers():
    x, y, c = lax.axis_index("x"), lax.axis_index("y"), lax.axis_index("c")
    peers = []
    for k in range(1, N_DEV):
        px = 1 - x if k & 4 else x
        py = 1 - y if k & 2 else y
        pc = 1 - c if k & 1 else c
        peers.append((px, py, pc))
    return (x, y, c), peers


def _index(dev):
    return 4 * dev[0] + 2 * dev[1] + dev[2]


def _hbm(a):
    return pltpu.with_memory_space_constraint(a, pltpu.HBM)


HBM_SPEC = pl.BlockSpec(memory_space=pltpu.HBM)
SEM_SPEC = pl.BlockSpec(memory_space=pltpu.SEMAPHORE)
DATAFLOW = pltpu.SideEffectType.DATAFLOW_SIDE_EFFECTING


def _remote(src_ref, land_ref, slotted, me, peer, src_is_mine, send_sem, recv_sem, k):
    sender, receiver = (me, peer) if src_is_mine else (peer, me)
    src = src_ref.at[_index(receiver)] if slotted else src_ref
    return pltpu.make_async_remote_copy(src_ref=src, dst_ref=land_ref.at[_index(sender)], send_sem=send_sem.at[k],
                                        recv_sem=recv_sem.at[k], device_id=peer, device_id_type=MESH_ID)


SIBLING = 0
SAME_CORE = (1, 3, 5)
OTHER_CORE = (2, 4, 6)


def copies_start(arrays, mode, *, name):
    n = len(arrays)
    slotted = mode == "exchange"
    lands = [lax.empty(a.shape if slotted else (N_DEV,) + a.shape, a.dtype) for a in arrays]
    targets = (SIBLING,) + SAME_CORE if mode == "gather2" else tuple(range(N_DEV - 1))

    def body(*refs):
        x_refs, land_refs = refs[:n], refs[n:2 * n]
        send, recv = refs[2 * n:3 * n], refs[3 * n:4 * n]
        token = refs[-1]
        me, peers = _me_and_peers()
        for w in range(n):
            for k in targets:
                _remote(x_refs[w], land_refs[w], slotted, me, peers[k], True, send[w], recv[w], k).start()
            if not slotted:
                pltpu.make_async_copy(x_refs[w], land_refs[w].at[_index(me)], recv[w].at[N_DEV - 1]).start()
        token[...] = jnp.zeros_like(token)

    sem = pltpu.SemaphoreType.DMA((N_DEV,))
    out_shape = ([sem] * (2 * n) + [pltpu.HBM(a.shape, a.dtype) for a in arrays]
                 + [pltpu.HBM(l.shape, l.dtype) for l in lands] + [jax.ShapeDtypeStruct((8, 128), F32)])
    outs = pl.pallas_call(
        body, name=name, out_shape=out_shape, in_specs=[HBM_SPEC] * (2 * n),
        out_specs=[SEM_SPEC] * (2 * n) + [HBM_SPEC] * (2 * n) + [pl.BlockSpec(memory_space=pltpu.VMEM)],
        input_output_aliases={i: 2 * n + i for i in range(2 * n)},
        compiler_params=pltpu.CompilerParams(has_side_effects=DATAFLOW),
    )(*[_hbm(a) for a in arrays], *[_hbm(l) for l in lands])
    handles = [(outs[w], outs[n + w], outs[2 * n + w], outs[3 * n + w]) for w in range(n)]
    return handles, outs[-1]


def _forward(land_ref, me, peers, j, fsend, frecv, mine):
    block = _index(peers[SAME_CORE[j]] if mine else peers[OTHER_CORE[j]])
    return pltpu.make_async_remote_copy(src_ref=land_ref.at[block], dst_ref=land_ref.at[block], send_sem=fsend.at[j],
                                        recv_sem=frecv.at[j], device_id=peers[SIBLING], device_id_type=MESH_ID)


def copies_forward(handles, after, *, name):
    n = len(handles)

    def body(*refs):
        land_refs, recv = refs[:n], refs[n:2 * n]
        fsend, frecv = refs[2 * n + 1:3 * n + 1], refs[3 * n + 1:4 * n + 1]
        token = refs[-1]
        me, peers = _me_and_peers()
        for w in range(n):
            for j, k in enumerate(SAME_CORE):
                block = land_refs[w].at[_index(peers[k])]
                pltpu.make_async_remote_copy(src_ref=block, dst_ref=block, send_sem=recv[w].at[N_DEV - 1],
                                             recv_sem=recv[w].at[k], device_id=peers[k], device_id_type=MESH_ID).wait_recv()
                _forward(land_refs[w], me, peers, j, fsend[w], frecv[w], True).start()
        token[...] = jnp.zeros_like(token)

    sem = pltpu.SemaphoreType.DMA((len(SAME_CORE),))
    lands = [h[3] for h in handles]
    outs = pl.pallas_call(
        body, name=name,
        out_shape=[sem] * (2 * n) + [pltpu.HBM(l.shape, l.dtype) for l in lands] + [jax.ShapeDtypeStruct((8, 128), F32)],
        in_specs=[HBM_SPEC] * n + [SEM_SPEC] * n + [pl.BlockSpec(memory_space=pl.ANY)],
        out_specs=[SEM_SPEC] * (2 * n) + [HBM_SPEC] * n + [pl.BlockSpec(memory_space=pltpu.VMEM)],
        input_output_aliases={w: 2 * n + w for w in range(n)},
        compiler_params=pltpu.CompilerParams(has_side_effects=DATAFLOW),
    )(*lands, *[h[1] for h in handles], after)
    new = [(h[0], h[1], h[2], outs[2 * n + w], outs[w], outs[n + w]) for w, h in enumerate(handles)]
    return new, outs[-1]


def copies_wait(handle, mode, after, *, name):
    slotted = mode == "exchange"
    two_level = mode == "gather2"
    send_sem, recv_sem, x_thru, land_thru = handle[:4]
    targets = (SIBLING,) + SAME_CORE if two_level else tuple(range(N_DEV - 1))
    arrivals = (SIBLING,) if two_level else targets

    def body(x_ref, land_ref, send_ref, recv_ref, *rest):
        me, peers = _me_and_peers()
        for k in targets:
            _remote(x_ref, land_ref, slotted, me, peers[k], True, send_ref, recv_ref, k).wait_send()
        for k in arrivals:
            _remote(x_ref, land_ref, slotted, me, peers[k], False, send_ref, recv_ref, k).wait_recv()
        if not slotted:
            pltpu.make_async_copy(x_ref, land_ref.at[_index(me)], recv_ref.at[N_DEV - 1]).wait()
        if two_level:
            fsend, frecv = rest[0], rest[1]
            for j in range(len(SAME_CORE)):
                _forward(land_ref, me, peers, j, fsend, frecv, True).wait_send()
                _forward(land_ref, me, peers, j, fsend, frecv, False).wait_recv()

    extra = list(handle[4:])
    return pl.pallas_call(
        body, name=name, out_shape=(pltpu.HBM(x_thru.shape, x_thru.dtype), pltpu.HBM(land_thru.shape, land_thru.dtype)),
        in_specs=[HBM_SPEC, HBM_SPEC, SEM_SPEC, SEM_SPEC] + [SEM_SPEC] * len(extra) + [pl.BlockSpec(memory_space=pl.ANY)],
        out_specs=(HBM_SPEC, HBM_SPEC), input_output_aliases={0: 0, 1: 1},
        compiler_params=pltpu.CompilerParams(has_side_effects=DATAFLOW),
    )(x_thru, land_thru, send_sem, recv_sem, *extra, after)


def cast_bf16(x, *, ncols=None, name):
    R = x.shape[0]
    C = ncols or x.shape[1]
    tr = _pick(R, (512, 352, 256, 128, 64))

    def body(x_ref, o_ref):
        o_ref[...] = x_ref[...].astype(BF16)

    row = pl.BlockSpec((tr, C), lambda i: (i, 0))
    return pl.pallas_call(body, grid=(R // tr,), in_specs=[row], out_specs=row,
                          out_shape=jax.ShapeDtypeStruct((R, C), BF16), name=name)(x)


def cast_bf16_layer(x3, layer, *, name):
    _, R, C = x3.shape
    tr = _pick(R, (512, 352, 256, 128, 64))

    def body(x_ref, o_ref):
        o_ref[...] = x_ref[...].astype(BF16)

    return pl.pallas_call(body, grid=(R // tr,), in_specs=[pl.BlockSpec((None, tr, C), lambda i: (layer, i, 0))],
                          out_specs=pl.BlockSpec((tr, C), lambda i: (i, 0)),
                          out_shape=jax.ShapeDtypeStruct((R, C), BF16), name=name)(x3)


BD_PARTS = 4


def _blockdiag_call(b, build, G, r, c, name):
    gp = G // BD_PARTS

    def body_build(b_ref, o_ref):
        o_ref[...] = jnp.zeros_like(o_ref)
        for g in range(G):
            o_ref[g // gp, (g % gp) * r:(g % gp + 1) * r, (g % gp) * c:(g % gp + 1) * c] = b_ref[g]

    def body_extract(d_ref, o_ref):
        for g in range(G):
            o_ref[g] = d_ref[g // gp, (g % gp) * r:(g % gp + 1) * r, (g % gp) * c:(g % gp + 1) * c]

    out = jax.ShapeDtypeStruct((BD_PARTS, gp * r, gp * c) if build else (G, r, c), F32)
    return pl.pallas_call(body_build if build else body_extract, out_shape=out, name=name)(b)


def make_blockdiag(G, r, c, name):
    @jax.custom_vjp
    def blockdiag(b):
        return _blockdiag_call(b, True, G, r, c, name + "_build")

    def fwd(b):
        return blockdiag(b), None

    def bwd(_, g):
        return (_blockdiag_call(g, False, G, r, c, name + "_extract"),)

    blockdiag.defvjp(fwd, bwd)
    return blockdiag


def _my_index():
    return 4 * lax.axis_index("x") + 2 * lax.axis_index("y") + lax.axis_index("c")


def cols_from_shards(g, *, name):
    _, K, n = g.shape
    tk = _pick(K, (256, 128))

    def body(g_ref, o_ref):
        for i in range(N_DEV):
            o_ref[:, i * n:(i + 1) * n] = g_ref[i]

    return pl.pallas_call(body, grid=(K // tk,), in_specs=[pl.BlockSpec((N_DEV, tk, n), lambda i: (0, i, 0))],
                          out_specs=pl.BlockSpec((tk, N_DEV * n), lambda i: (i, 0)),
                          out_shape=jax.ShapeDtypeStruct((K, N_DEV * n), g.dtype), name=name)(g)


def shards_from_cols(w, *, name):
    K, N = w.shape
    n = N // N_DEV
    tk = _pick(K, (256, 128))

    def body(w_ref, o_ref):
        for i in range(N_DEV):
            o_ref[i] = w_ref[:, i * n:(i + 1) * n].astype(o_ref.dtype)

    return pl.pallas_call(body, grid=(K // tk,), in_specs=[pl.BlockSpec((tk, N), lambda i: (i, 0))],
                          out_specs=pl.BlockSpec((N_DEV, tk, n), lambda i: (0, i, 0)),
                          out_shape=jax.ShapeDtypeStruct((N_DEV, K, n), BF16), name=name)(w)


def _adamw(w, g, m, v):
    m = ADAM_B1 * m + (1.0 - ADAM_B1) * g
    v = ADAM_B2 * v + (1.0 - ADAM_B2) * (g * g)
    m_hat = m / (1.0 - ADAM_B1 ** ADAM_STEP)
    v_hat = v / (1.0 - ADAM_B2 ** ADAM_STEP)
    delta = -ADAM_LR * (m_hat / (jnp.sqrt(v_hat) + ADAM_EPS) + ADAM_WD * w)
    return delta, m, v


def reduce_adamw(recv, own, own_slotted, me, w, m, v, *, layer=0, n_layers=1, into=None, name):
    _, R, C = recv.shape
    tr = _pick(R, (352, 320, 288, 256, 128, 64, 32, 16, 8))
    off = layer * (R // tr)

    def body(me_ref, r_ref, own_ref, w_ref, m_ref, v_ref, *rest):
        g_ref, d_ref, nm_ref, nv_ref = rest[-4:]
        mine = me_ref[0]
        g = None
        for i in range(N_DEV):
            part = jnp.where(mine == i, own_ref[...], r_ref[i]).astype(F32)
            g = part if g is None else g + part
        delta, nm, nv = _adamw(w_ref[...], g, m_ref[...], v_ref[...])
        g_ref[...] = g
        d_ref[...] = delta
        nm_ref[...] = nm
        nv_ref[...] = nv

    row = pl.BlockSpec((tr, C), lambda i, me_ref: (i + off, 0))
    own_spec = (pl.BlockSpec((None, tr, C), lambda i, me_ref: (me_ref[0], i, 0)) if own_slotted
                else pl.BlockSpec((tr, C), lambda i, me_ref: (i, 0)))
    rest = [] if into is None else list(into)
    grid_spec = pltpu.PrefetchScalarGridSpec(
        num_scalar_prefetch=1, grid=(R // tr,),
        in_specs=[pl.BlockSpec((N_DEV, tr, C), lambda i, me_ref: (0, i, 0)), own_spec, row, row, row]
        + [pl.BlockSpec(memory_space=pl.ANY)] * len(rest),
        out_specs=[row] * 4)
    return pl.pallas_call(body, grid_spec=grid_spec, out_shape=[jax.ShapeDtypeStruct((n_layers * R, C), F32)] * 4,
                          input_output_aliases={6 + k: k for k in range(len(rest))},
                          compiler_params=_params(dimension_semantics=("parallel",)), name=name)(
        me.reshape(1).astype(jnp.int32), recv, own, w, m, v, *rest)


def _s5_prepare(A_re, A_im, log_dt, B_re, B_im, C_re, C_im):
    G, P, Cg = S5_GROUPS, S5_STATE, S5_GROUP
    dt = jnp.exp(log_dt)[:, None]
    mag = jnp.exp(A_re * dt)
    ab_re = mag * jnp.cos(A_im * dt)
    ab_im = mag * jnp.sin(A_im * dt)
    den = A_re * A_re + A_im * A_im
    nr, ni = ab_re - 1.0, ab_im
    c_re = (nr * A_re + ni * A_im) / den
    c_im = (ni * A_re - nr * A_im) / den
    Bb_re = c_re[..., None] * B_re - c_im[..., None] * B_im
    Bb_im = c_re[..., None] * B_im + c_im[..., None] * B_re
    def dense_in(b, name):
        return make_blockdiag(G, Cg, P, name)(b.transpose(0, 2, 1))

    def dense_out(c, name):
        return make_blockdiag(G, P, Cg, name)(c.transpose(0, 2, 1))

    return (ab_re.reshape(1, G * P), ab_im.reshape(1, G * P), dense_in(Bb_re, "s5_wb_re"), dense_in(Bb_im, "s5_wb_im"),
            dense_out(C_re, "s5_wc_re"), dense_out(-C_im, "s5_wc_im"))


def _lower_bound(gamma):
    return jnp.cumsum(jax.nn.softmax(gamma, axis=0), axis=0)[0:1]


def _ffn_fwd(h, g_norm, get_w_in, conv_w, conv_b, get_w_out, tag):
    w_in = get_w_in(h)
    hn, hu = norm_mm(h, g_norm, w_in, name=tag + "_in")
    act = convgate_fwd(hu, conv_w, conv_b, name=tag + "_gate")
    w_out = get_w_out(act)
    h_out = mm(act, w_out, res=h, name=tag + "_out")
    return h_out, (hn, hu, act), w_in, w_out


def _ffn_bwd(h, g_norm, w_in, conv_w, conv_b, w_out, saved, dh, tag, send_dw_in, send_dw_out):
    hn, hu, act = saved
    sent = send_dw_out(mm(act, dh, ta=True, out_dtype=BF16, name=tag + "_dwout"))
    dact = mm(dh, w_out, tb=True, dep=sent, name=tag + "_dact")
    dhu, dconv_w, dconv_b = convgate_bwd(hu, conv_w, conv_b, dact, name=tag + "_dgate")
    sent = send_dw_in(mm(dhu, hn, ta=True, out_dtype=BF16, name=tag + "_dwin"))
    dh_in, dg = mm_drms(dhu, w_in, h, g_norm, dh, dep=sent, name=tag + "_dhn")
    return dh_in, dg, dconv_w, dconv_b


def kernel(x, positions, norm_mix, norm_ffn, norm_final, mix_w_in, mix_w_out, s5_A_re, s5_A_im, s5_log_dt, s5_B_re, s5_B_im, s5_C_re, s5_C_im, s5_D, s5_glu_w, s5_glu_b, hgrn_gamma, hgrn_norm, att_w_qkv, att_w_o, ffn_w_in, ffn_conv_w, ffn_conv_b, ffn_w_out, loss_target, m_norm_mix, m_norm_ffn, m_norm_final, m_mix_w_in, m_mix_w_out, m_s5_A_re, m_s5_A_im, m_s5_log_dt, m_s5_B_re, m_s5_B_im, m_s5_C_re, m_s5_C_im, m_s5_D, m_s5_glu_w, m_s5_glu_b, m_hgrn_gamma, m_hgrn_norm, m_att_w_qkv, m_att_w_o, m_ffn_w_in, m_ffn_conv_w, m_ffn_conv_b, m_ffn_w_out, v_norm_mix, v_norm_ffn, v_norm_final, v_mix_w_in, v_mix_w_out, v_s5_A_re, v_s5_A_im, v_s5_log_dt, v_s5_B_re, v_s5_B_im, v_s5_C_re, v_s5_C_im, v_s5_D, v_s5_glu_w, v_s5_glu_b, v_hgrn_gamma, v_hgrn_norm, v_att_w_qkv, v_att_w_o, v_ffn_w_in, v_ffn_conv_w, v_ffn_conv_b, v_ffn_w_out):
    W = dict(norm_mix=norm_mix, norm_ffn=norm_ffn, norm_final=norm_final, mix_w_in=mix_w_in, mix_w_out=mix_w_out,
             s5_A_re=s5_A_re, s5_A_im=s5_A_im, s5_log_dt=s5_log_dt, s5_B_re=s5_B_re, s5_B_im=s5_B_im,
             s5_C_re=s5_C_re, s5_C_im=s5_C_im, s5_D=s5_D, s5_glu_w=s5_glu_w, s5_glu_b=s5_glu_b,
             hgrn_gamma=hgrn_gamma, hgrn_norm=hgrn_norm, att_w_qkv=att_w_qkv, att_w_o=att_w_o, ffn_w_in=ffn_w_in,
             ffn_conv_w=ffn_conv_w, ffn_conv_b=ffn_conv_b, ffn_w_out=ffn_w_out)
    M = dict(norm_mix=m_norm_mix, norm_ffn=m_norm_ffn, norm_final=m_norm_final, mix_w_in=m_mix_w_in,
             mix_w_out=m_mix_w_out, s5_A_re=m_s5_A_re, s5_A_im=m_s5_A_im, s5_log_dt=m_s5_log_dt, s5_B_re=m_s5_B_re,
             s5_B_im=m_s5_B_im, s5_C_re=m_s5_C_re, s5_C_im=m_s5_C_im, s5_D=m_s5_D, s5_glu_w=m_s5_glu_w,
             s5_glu_b=m_s5_glu_b, hgrn_gamma=m_hgrn_gamma, hgrn_norm=m_hgrn_norm, att_w_qkv=m_att_w_qkv,
             att_w_o=m_att_w_o, ffn_w_in=m_ffn_w_in, ffn_conv_w=m_ffn_conv_w, ffn_conv_b=m_ffn_conv_b,
             ffn_w_out=m_ffn_w_out)
    V = dict(norm_mix=v_norm_mix, norm_ffn=v_norm_ffn, norm_final=v_norm_final, mix_w_in=v_mix_w_in,
             mix_w_out=v_mix_w_out, s5_A_re=v_s5_A_re, s5_A_im=v_s5_A_im, s5_log_dt=v_s5_log_dt, s5_B_re=v_s5_B_re,
             s5_B_im=v_s5_B_im, s5_C_re=v_s5_C_re, s5_C_im=v_s5_C_im, s5_D=v_s5_D, s5_glu_w=v_s5_glu_w,
             s5_glu_b=v_s5_glu_b, hgrn_gamma=v_hgrn_gamma, hgrn_norm=v_hgrn_norm, att_w_qkv=v_att_w_qkv,
             att_w_o=v_att_w_o, ffn_w_in=v_ffn_w_in, ffn_conv_w=v_ffn_conv_w, ffn_conv_b=v_ffn_conv_b,
             ffn_w_out=v_ffn_w_out)
    return _step(x[0], positions[0], loss_target[0], W, M, V)


TRANSPOSED = ("mix_w_in", "att_w_qkv", "ffn_w_in")
SMALL = ("norm_mix", "norm_ffn", "norm_final", "s5_A_re", "s5_A_im", "s5_log_dt", "s5_B_re", "s5_B_im", "s5_C_re",
         "s5_C_im", "s5_D", "s5_glu_b", "hgrn_gamma", "hgrn_norm", "ffn_conv_b")
ORDER = ("norm_mix", "norm_ffn", "norm_final", "mix_w_in", "mix_w_out", "s5_A_re", "s5_A_im", "s5_log_dt", "s5_B_re",
         "s5_B_im", "s5_C_re", "s5_C_im", "s5_D", "s5_glu_w", "s5_glu_b", "hgrn_gamma", "hgrn_norm", "att_w_qkv",
         "att_w_o", "ffn_w_in", "ffn_conv_w", "ffn_conv_b", "ffn_w_out")
PACK_COLS = 1024


def _step(x, positions, target, W, M, V):
    L, D = x.shape
    me = 4 * lax.axis_index("x") + 2 * lax.axis_index("y") + lax.axis_index("c")
    n_cw = W["ffn_conv_w"].shape[-1]
    T = {n: tuple(jnp.swapaxes(d[n], -1, -2) for d in (W, M, V)) for n in TRANSPOSED}
    shards = {
        "mix_w_in": cast_bf16(T["mix_w_in"][0][0], name="mix_w_in_cast"),
        "conv_w": W["ffn_conv_w"].reshape(6, n_cw),
        "s5_glu_w": cast_bf16(W["s5_glu_w"][0], name="s5_glu_w_cast"),
        "mix_w_out": cast_bf16(W["mix_w_out"][0], name="mix_w_out_cast"),
        "ffn_w_in0": cast_bf16_layer(T["ffn_w_in"][0], 0, name="ffn_w_in0_cast"),
        "ffn_w_out0": cast_bf16_layer(W["ffn_w_out"], 0, name="ffn_w_out0_cast"),
        "att_w_qkv": cast_bf16(T["att_w_qkv"][0][0], name="att_w_qkv_cast"),
        "att_w_o": cast_bf16(W["att_w_o"][0], name="att_w_o_cast"),
        "ffn_w_in1": cast_bf16_layer(T["ffn_w_in"][0], 1, name="ffn_w_in1_cast"),
        "ffn_w_out1": cast_bf16_layer(W["ffn_w_out"], 1, name="ffn_w_out1_cast"),
    }
    gather_handles, token = copies_start(list(shards.values()), "gather2", name="gather_start")
    gather_handle = dict(zip(shards, gather_handles))

    def forward(keys, after, name):
        new, sent = copies_forward([gather_handle[k] for k in keys], after, name=name)
        gather_handle.update(zip(keys, new))
        return sent

    def gathered(key, after, cols):
        _, land = copies_wait(gather_handle[key], "gather2", after, name=key + "_gwait")
        return cols_from_shards(land, name=key + "_asm") if cols else land.reshape(-1, land.shape[-1])

    conv_b = W["ffn_conv_b"].reshape(2, 1, -1)

    s5_params = (W["s5_A_re"][0], W["s5_A_im"][0], W["s5_log_dt"][0], W["s5_B_re"][0], W["s5_B_im"][0],
                 W["s5_C_re"][0], W["s5_C_im"][0])
    (a_re, a_im, wb_re, wb_im, wc_re, wc_im), s5_prep_vjp = jax.vjp(_s5_prepare, *s5_params)
    dvec = W["s5_D"].reshape(1, S5_WIDTH)
    glu_b = W["s5_glu_b"].reshape(1, S5_WIDTH)
    lb, lb_vjp = jax.vjp(_lower_bound, W["hgrn_gamma"])
    hg_norm = W["hgrn_norm"].reshape(1, -1)
    tabs = rope_tables(positions)

    sent = forward(["mix_w_in", "conv_w", "s5_glu_w"], token, "forward_a")
    w_mix_in = gathered("mix_w_in", sent, False)
    hn0, proj = norm_mm(x, W["norm_mix"][0], w_mix_in, name="l0_proj")
    y0, xs_re, xs_im = s5_core_fwd(proj, a_re, a_im, wb_re, wb_im, wc_re, wc_im, name="s5_core")
    w_glu = gathered("s5_glu_w", y0, False)
    oa = s5_out_fwd(y0, proj, dvec, w_glu, glu_b, name="s5_out")
    ob, hg_states = hgrn_fwd(proj, lb, hg_norm, name="hgrn_fwd")
    forward(["mix_w_out"], ob, "forward_b")
    cat = jnp.concatenate([oa, ob], axis=1)
    w_mix_out = gathered("mix_w_out", cat, False)
    h1 = mm(cat, w_mix_out, res=x, name="l0_mix_out")
    _, cw_all = copies_wait(gather_handle["conv_w"], "gather2", h1, name="conv_w_gwait")
    conv_w = cw_all.transpose(1, 0, 2).reshape(2, 3, N_DEV * n_cw)
    w_ffn_in, w_ffn_out = [None, None], [None, None]
    h2, ffn0_saved, w_ffn_in[0], w_ffn_out[0] = _ffn_fwd(
        h1, W["norm_ffn"][0],
        lambda a: (forward(["ffn_w_in0"], a, "forward_b2"), gathered("ffn_w_in0", a, False))[1], conv_w[0], conv_b[0],
        lambda a: (forward(["ffn_w_out0"], a, "forward_c"), gathered("ffn_w_out0", a, False))[1], "ffn0")

    forward(["att_w_qkv", "att_w_o"], h2, "forward_d")
    w_qkv = gathered("att_w_qkv", h2, False)
    hn2, qkv_r = norm_mm(h2, W["norm_mix"][1], w_qkv, tabs=tabs, name="l1_qkv")
    att_o, att_l = [], []
    for g, d in enumerate(ATT_DILATIONS):
        o_g, l_g = attn_fwd(qkv_r, g, d, name=f"attn_fwd{g}")
        att_o.append(o_g)
        att_l.append(l_g)
    o_att = merge_fwd(att_o, att_l, name="merge_fwd")
    forward(["ffn_w_in1", "ffn_w_out1"], o_att, "forward_e")
    w_o = gathered("att_w_o", o_att, True)
    h3 = mm(o_att, w_o, res=h2, name="l1_mix_out")
    h4, ffn1_saved, w_ffn_in[1], w_ffn_out[1] = _ffn_fwd(
        h3, W["norm_ffn"][1], lambda a: gathered("ffn_w_in1", a, False), conv_w[1], conv_b[1],
        lambda a: gathered("ffn_w_out1", a, False), "ffn1")

    exchanges = {}

    pending = []

    def send_grad(key, g, cols, flush=True):
        if cols:
            parts = shards_from_cols(g, name=key + "_split")
        else:
            parts = g.reshape(N_DEV, g.shape[0] // N_DEV, g.shape[1])
        pending.append((key, parts))
        if not flush:
            return None
        handles, sent = copies_start([p for _, p in pending], "exchange", name=key + "_xstart")
        exchanges.update(zip([k for k, _ in pending], handles))
        pending.clear()
        return sent

    loss, dh4, dg_final = final_loss(h4, W["norm_final"], target, name="final_loss")
    dh3, dg_ffn1, dcw1, dcb1 = _ffn_bwd(h3, W["norm_ffn"][1], w_ffn_in[1], conv_w[1], conv_b[1], w_ffn_out[1],
                                        ffn1_saved, dh4, "ffn1", lambda g: send_grad("ffn_w_in1", g, False),
                                        lambda g: send_grad("ffn_w_out1", g, False, flush=False))
    sent = send_grad("att_w_o", mm(o_att, dh3, ta=True, name="l1_dwo"), True, flush=False)
    d_oatt = mm(dh3, w_o, tb=True, dep=sent, name="l1_dmix")
    mb = merge_bwd(att_o, att_l, d_oatt, name="merge_bwd")
    d_slabs = [attn_bwd(qkv_r, g, att_l[g], mb[g], mb[3 + g], d, name=f"attn_bwd{g}")
               for g, d in enumerate(ATT_DILATIONS)]
    d_qkv = rope_bwd([s[0] for s in d_slabs] + [s[1] for s in d_slabs] + [s[2] for s in d_slabs], tabs,
                     name="rope_bwd")
    sent = send_grad("att_w_qkv", mm(d_qkv, hn2, ta=True, out_dtype=BF16, name="l1_dwqkv"), False)
    dh2, dg_mix1 = mm_drms(d_qkv, w_qkv, h2, W["norm_mix"][1], dh3, dep=sent, name="l1_dhn")

    dh1, dg_ffn0, dcw0, dcb0 = _ffn_bwd(h1, W["norm_ffn"][0], w_ffn_in[0], conv_w[0], conv_b[0], w_ffn_out[0],
                                        ffn0_saved, dh2, "ffn0", lambda g: send_grad("ffn_w_in0", g, False),
                                        lambda g: send_grad("ffn_w_out0", g, False, flush=False))
    sent = send_grad("mix_w_out", mm(cat, dh1, ta=True, out_dtype=BF16, name="l0_dwout"), False)
    dcat = mm(dh1, w_mix_out, tb=True, dep=sent, name="l0_dcat")
    d_hg, dlb, dhg_norm = hgrn_bwd(proj, lb, hg_norm, hg_states, dcat, name="hgrn_bwd")
    dy, du_d, z_bf, dzg, dglu_b, dD = s5_out_bwd(y0, proj, dvec, w_glu, glu_b, dcat, name="s5_dout")
    sent_glu = send_grad("s5_glu_w", mm(z_bf, dzg, ta=True, out_dtype=BF16, name="s5_dglu"), False, flush=False)
    du, dwb_re, dwb_im, dwc_re, dwc_im, da_re, da_im = s5_core_bwd(
        dy, du_d, proj, xs_re, xs_im, a_re, a_im, wb_re, wb_im, wc_re, wc_im, name="s5_dcore")
    s5_small = s5_prep_vjp((da_re, da_im, dwb_re, dwb_im, dwc_re, dwc_im))
    d_proj = jnp.concatenate([du, d_hg], axis=1)
    sent = send_grad("mix_w_in", mm(d_proj, hn0, ta=True, out_dtype=BF16, dep=sent_glu, name="l0_dwin"), False)
    grad_x, dg_mix0 = mm_drms(d_proj, w_mix_in, x, W["norm_mix"][0], dh1, dep=sent, name="l0_dhn")
    (d_gamma,) = lb_vjp(dlb)
    out = {}

    dA_re, dA_im, dlog_dt, dB_re, dB_im, dC_re, dC_im = s5_small
    small_g = dict(norm_mix=jnp.concatenate([dg_mix0, dg_mix1], axis=0), norm_ffn=jnp.concatenate([dg_ffn0, dg_ffn1], axis=0),
                   norm_final=dg_final, s5_A_re=dA_re, s5_A_im=dA_im, s5_log_dt=dlog_dt, s5_B_re=dB_re, s5_B_im=dB_im,
                   s5_C_re=dC_re, s5_C_im=dC_im, s5_D=dD, s5_glu_b=dglu_b, hgrn_gamma=d_gamma, hgrn_norm=dhg_norm,
                   ffn_conv_b=jnp.concatenate([dcb0, dcb1], axis=0))
    conv_w_g = jnp.stack([dcw0, dcw1], axis=0)
    sizes = [math.prod(W[n].shape) for n in SMALL]
    n_conv = conv_w_g.size
    total = sum(sizes) + n_conv + 1
    rows = -(-total // PACK_COLS)
    rows = -(-rows // 8) * 8
    pad = rows * PACK_COLS - total

    def pack(vals, conv_part, last):
        flat = [v.reshape(-1).astype(F32) for v in vals] + [conv_part.reshape(-1), last.reshape(-1),
                                                            jnp.zeros((pad,), F32)]
        return jnp.concatenate(flat).reshape(rows, PACK_COLS)

    def conv_full(shard):
        col_owner = lax.broadcasted_iota(jnp.int32, (2, 3, N_DEV * n_cw), 2) // n_cw
        return jnp.where(col_owner == me, jnp.tile(shard, (1, 1, N_DEV)), 0.0)

    zero1 = jnp.zeros((1,), F32)
    g_pack = pack([small_g[n] for n in SMALL], conv_w_g, loss)
    w_pack = pack([W[n] for n in SMALL], conv_full(W["ffn_conv_w"]), zero1)
    m_pack = pack([M[n] for n in SMALL], conv_full(M["ffn_conv_w"]), zero1)
    v_pack = pack([V[n] for n in SMALL], conv_full(V["ffn_conv_w"]), zero1 + 1.0)
    (small_handle,), small_sent = copies_start([g_pack], "gather", name="small_xstart")

    def finish(name, n_layers):
        w3, m3, v3 = T[name] if name in TRANSPOSED else (W[name], M[name], V[name])
        res = None
        for layer in reversed(range(n_layers)):
            key = name if n_layers == 1 else f"{name}{layer}"
            own, recv = copies_wait(exchanges[key], "exchange", small_sent, name=key + "_xwait")
            _, R, Cn = recv.shape
            res = reduce_adamw(recv, own, True, me, w3.reshape(n_layers * R, Cn), m3.reshape(n_layers * R, Cn),
                               v3.reshape(n_layers * R, Cn), layer=layer, n_layers=n_layers, into=res,
                               name=key + "_adamw")
        res = [r.reshape(w3.shape) for r in res]
        return tuple(jnp.swapaxes(r, -1, -2) for r in res) if name in TRANSPOSED else tuple(res)

    for name in ("ffn_w_out", "ffn_w_in"):
        out[name] = finish(name, 2)
    for name in ("att_w_o", "att_w_qkv", "mix_w_out", "s5_glu_w", "mix_w_in"):
        out[name] = finish(name, 1)

    small_own, small_recv = copies_wait(small_handle, "gather", out["s5_glu_w"][0], name="small_xwait")
    res = reduce_adamw(small_recv, small_own, False, me, w_pack, m_pack, v_pack, name="small_adamw")
    flat = [r.reshape(-1) for r in res]
    off = 0
    for n, sz in zip(SMALL, sizes):
        out[n] = tuple(f[off:off + sz].reshape(W[n].shape) for f in flat)
        off += sz
    conv_res = [f[off:off + n_conv].reshape(2, 3, N_DEV * n_cw) for f in flat]
    out["ffn_conv_w"] = tuple(lax.dynamic_slice(c, (0, 0, me * n_cw), (2, 3, n_cw)) for c in conv_res)
    off += n_conv
    loss_total = flat[0][off]

    result = [loss_total, grad_x[None]]
    for k in range(4):
        result += [out[n][k] for n in ORDER]
    return tuple(result)
```

```python
import functools
import math

import jax
import jax.numpy as jnp
from jax import lax
from jax.experimental import pallas as pl
from jax.experimental.pallas import tpu as pltpu

F32 = jnp.float32
BF16 = jnp.bfloat16
MESH_ID = pl.DeviceIdType.MESH
N_DEV = 8
VMEM_LIMIT_BYTES = 56 * 1024 * 1024

NORM_EPS = 1e-6
S5_WIDTH, S5_GROUP, S5_GROUPS, S5_STATE = 512, 16, 32, 64
HG_HEADS, HG_DIM, HG_CHUNK = 4, 128, 64
HG_STEP_CHUNKS = 4
ATT_E, ATT_HPG, ATT_BLOCK = 64, 8, 128
ATT_DILATIONS = (1, 4, 16)
ROT_DIM, ROPE_THETA = 16, 500000.0
D_FF = 2816
ADAM_LR, ADAM_B1, ADAM_B2, ADAM_EPS, ADAM_WD, ADAM_STEP = 0.001, 0.9, 0.999, 1e-08, 0.01, 10
NEG_BIG = -1e30


def _params(**kw):
    return pltpu.CompilerParams(vmem_limit_bytes=VMEM_LIMIT_BYTES, **kw)


def _pick(n, cands):
    for c in cands:
        if n % c == 0:
            return c
    return n


def _dot(a, b):
    return jnp.dot(a.astype(BF16), b.astype(BF16), preferred_element_type=F32)


def _dot_nt(a, b):
    return lax.dot_general(a.astype(BF16), b.astype(BF16), (((1,), (1,)), ((), ())), preferred_element_type=F32)


def _dot_tn(a, b):
    return lax.dot_general(a.astype(BF16), b.astype(BF16), (((0,), (0,)), ((), ())), preferred_element_type=F32)


def _split2(x):
    hi = x.astype(BF16)
    return hi, (x - hi.astype(F32)).astype(BF16)


def _dot_x3(a, b, contract=((1,), (0,))):
    dn = (contract, ((), ()))
    a1, a2 = _split2(a)
    b1, b2 = _split2(b)
    return (lax.dot_general(a1, b1, dn, preferred_element_type=F32) + lax.dot_general(a1, b2, dn, preferred_element_type=F32)
            + lax.dot_general(a2, b1, dn, preferred_element_type=F32))


def _sigmoid(x):
    return 1.0 / (1.0 + jnp.exp(-x))


V7X_HBM_BYTES_PER_S = 3.2e12
V7X_MXU_FLOPS_PER_S = 0.7e15
GRID_STEP_S = 0.35e-6
MM_VMEM_BUDGET = 40 * 1024 * 1024


def _divisors(n, cands):
    return [c for c in cands if c <= n and n % c == 0] or [n]


def _mm_tiles(m, n, k, sa, sb, so, sr):
    best = None
    for tm in _divisors(m, (2816, 2048, 1408, 1024, 512, 256, 128)):
        for tn in _divisors(n, (2816, 2048, 1408, 1024, 512, 256, 128)):
            for tk in _divisors(k, (k, 2816, 2560, 2304, 2048, 1536, 1408, 1280, 1024, 512, 256, 128)):
                nk = k // tk
                vmem = 2 * (tm * tk * sa + tk * tn * sb + tm * tn * (so + sr)) + (tm * tn * 4 if nk > 1 else 0)
                vmem += tm * tk * 2 * (sa > 2) + tk * tn * 2 * (sb > 2) + tm * tn * 4
                if vmem > MM_VMEM_BUDGET:
                    continue
                ni, nj = m // tm, n // tn
                for i_outer in (True, False):
                    if i_outer:
                        a_reads = 1 if nk == 1 else nj
                        b_reads = 1 if (nk == 1 and nj == 1) else ni
                    else:
                        b_reads = 1 if nk == 1 else ni
                        a_reads = 1 if (nk == 1 and ni == 1) else nj
                    traffic = a_reads * m * k * sa + b_reads * k * n * sb + m * n * (so + sr)
                    t = max(traffic / V7X_HBM_BYTES_PER_S, 2.0 * m * n * k / V7X_MXU_FLOPS_PER_S)
                    t += ni * nj * nk * GRID_STEP_S
                    t += (tm * tk * sa + tk * tn * sb + tm * tn * so) / V7X_HBM_BYTES_PER_S
                    if best is None or t < best[0]:
                        best = (t, tm, tn, tk, i_outer)
    assert best is not None, (m, n, k)
    return best[1:]


def mm(a, b, *, ta=False, tb=False, res=None, out_dtype=F32, dep=None, name):
    m, k = (a.shape[1], a.shape[0]) if ta else a.shape
    n = b.shape[0] if tb else b.shape[1]
    assert (b.shape[1] if tb else b.shape[0]) == k
    has_res = res is not None
    tm, tn, tk, i_outer = _mm_tiles(m, n, k, a.dtype.itemsize, b.dtype.itemsize, jnp.dtype(out_dtype).itemsize,
                                    res.dtype.itemsize if has_res else 0)
    nk = k // tk
    deps = [] if dep is None else [dep]
    dn = (((0 if ta else 1,), (1 if tb else 0,)), ((), ()))

    def body_single(*refs):
        a_ref, b_ref = refs[:2]
        o_ref = refs[-1]
        out = lax.dot_general(a_ref[...].astype(BF16), b_ref[...].astype(BF16), dn, preferred_element_type=F32)
        if has_res:
            out = out + refs[2][...].astype(F32)
        o_ref[...] = out.astype(o_ref.dtype)

    def body(*refs):
        a_ref, b_ref = refs[:2]
        r_ref = refs[2] if has_res else None
        o_ref, acc_ref = refs[-2:]
        kk = pl.program_id(2)
        part = lax.dot_general(a_ref[...].astype(BF16), b_ref[...].astype(BF16), dn, preferred_element_type=F32)

        @pl.when(kk == 0)
        def _():
            acc_ref[...] = part

        @pl.when(kk > 0)
        def _():
            acc_ref[...] += part

        @pl.when(kk == nk - 1)
        def _():
            out = acc_ref[...]
            if has_res:
                out = out + r_ref[...].astype(F32)
            o_ref[...] = out.astype(o_ref.dtype)

    def ij(f):
        return (lambda g0, g1, q: f(g0, g1, q)) if i_outer else (lambda g0, g1, q: f(g1, g0, q))

    a_spec = pl.BlockSpec((tk, tm), ij(lambda i, j, q: (q, i))) if ta else pl.BlockSpec((tm, tk), ij(lambda i, j, q: (i, q)))
    b_spec = pl.BlockSpec((tn, tk), ij(lambda i, j, q: (j, q))) if tb else pl.BlockSpec((tk, tn), ij(lambda i, j, q: (q, j)))
    o_spec = pl.BlockSpec((tm, tn), ij(lambda i, j, q: (i, j)))
    in_specs = [a_spec, b_spec] + ([o_spec] if has_res else []) + [pl.BlockSpec((8, 128), lambda g0, g1, q: (0, 0))] * len(deps)
    args = (a, b) + ((res,) if has_res else ()) + tuple(deps)
    grid = (m // tm, n // tn, nk) if i_outer else (n // tn, m // tm, nk)
    return pl.pallas_call(
        body_single if nk == 1 else body, grid=grid, in_specs=in_specs, out_specs=o_spec,
        out_shape=jax.ShapeDtypeStruct((m, n), out_dtype),
        scratch_shapes=[] if nk == 1 else [pltpu.VMEM((tm, tn), F32)],
        compiler_params=_params(dimension_semantics=("parallel", "parallel", "arbitrary")), name=name,
    )(*args)


def mm_drms(dy_in, w, x, g, dres, *, dep=None, name):
    m, k = dy_in.shape
    D = w.shape[1]
    tm = _pick(m, (512, 256, 128))
    tk = max(_divisors(k, (1536, 1408, 1280, 1024, 512, 256, 128)))
    nk = k // tk
    deps = [] if dep is None else [dep]

    def body(a_ref, b_ref, x_ref, g_ref, dres_ref, *rest):
        dx_ref, dg_ref, acc_ref = rest[-3:]
        i, q = pl.program_id(0), pl.program_id(1)
        part = jnp.dot(a_ref[...], b_ref[...], preferred_element_type=F32)

        @pl.when(q == 0)
        def _():
            acc_ref[...] = part

        @pl.when(q > 0)
        def _():
            acc_ref[...] += part

        @pl.when((i == 0) & (q == 0))
        def _():
            dg_ref[...] = jnp.zeros_like(dg_ref)

        @pl.when(q == nk - 1)
        def _():
            dyv = acc_ref[...]
            xv = x_ref[...]
            r = lax.rsqrt(jnp.mean(xv * xv, axis=-1, keepdims=True) + NORM_EPS)
            xh = xv * r
            dg_ref[...] += jnp.sum(dyv * xh, axis=0, keepdims=True)
            dxh = dyv * g_ref[...]
            dx_ref[...] = dres_ref[...] + r * (dxh - xh * jnp.mean(dxh * xh, axis=-1, keepdims=True))

    row = pl.BlockSpec((tm, D), lambda i, q: (i, 0))
    vec = pl.BlockSpec((1, D), lambda i, q: (0, 0))
    in_specs = [pl.BlockSpec((tm, tk), lambda i, q: (i, q)), pl.BlockSpec((tk, D), lambda i, q: (q, 0)), row, vec, row]
    in_specs += [pl.BlockSpec((8, 128), lambda i, q: (0, 0))] * len(deps)
    return pl.pallas_call(
        body, grid=(m // tm, nk), in_specs=in_specs, out_specs=[row, vec],
        out_shape=[jax.ShapeDtypeStruct((m, D), F32), jax.ShapeDtypeStruct((1, D), F32)],
        scratch_shapes=[pltpu.VMEM((tm, D), F32)],
        compiler_params=_params(dimension_semantics=("arbitrary", "arbitrary")), name=name,
    )(dy_in, w, x, g.reshape(1, D), dres, *deps)


def final_loss(h, g, target, *, name):
    L, D = h.shape
    tr = _pick(L, (256, 128))

    def body(x_ref, g_ref, t_ref, loss_ref, dx_ref, dg_ref):
        xv = x_ref[...]
        gv = g_ref[...]
        r = lax.rsqrt(jnp.mean(xv * xv, axis=-1, keepdims=True) + NORM_EPS)
        xh = xv * r
        err = xh * gv - t_ref[...]

        @pl.when(pl.program_id(0) == 0)
        def _():
            dg_ref[...] = jnp.zeros_like(dg_ref)
            loss_ref[...] = jnp.zeros_like(loss_ref)

        loss_ref[...] += 0.5 * jnp.sum(jnp.mean(err * err, axis=-1, keepdims=True), axis=0, keepdims=True)
        dyv = err * (1.0 / D)
        dg_ref[...] += jnp.sum(dyv * xh, axis=0, keepdims=True)
        dxh = dyv * gv
        dx_ref[...] = r * (dxh - xh * jnp.mean(dxh * xh, axis=-1, keepdims=True))

    row = pl.BlockSpec((tr, D), lambda i: (i, 0))
    vec = pl.BlockSpec((1, D), lambda i: (0, 0))
    one = pl.BlockSpec((1, 1), lambda i: (0, 0))
    return pl.pallas_call(body, grid=(L // tr,), in_specs=[row, vec, row], out_specs=[one, row, vec],
                          out_shape=[jax.ShapeDtypeStruct((1, 1), F32), jax.ShapeDtypeStruct((L, D), F32),
                                     jax.ShapeDtypeStruct((1, D), F32)],
                          compiler_params=_params(dimension_semantics=("arbitrary",)), name=name)(
        h, g.reshape(1, D), target)


def _cmul(ar, ai, br, bi):
    return ar * br - ai * bi, ar * bi + ai * br


def _powers(ar, ai):
    rows = [(ar, ai)]
    for _ in range(7):
        rows.append(_cmul(rows[-1][0], rows[-1][1], ar, ai))
    table = (jnp.concatenate([r[0] for r in rows], axis=0), jnp.concatenate([r[1] for r in rows], axis=0))
    return (rows[0], rows[1], rows[3]), table


def _block_scan(br, bi, steps, shift):
    yr, yi = br, bi
    for s, (pr, pi) in zip((1, 2, 4), steps):
        sr, si = shift(yr, s), shift(yi, s)
        yr, yi = yr + pr * sr - pi * si, yi + pr * si + pi * sr
    return yr, yi


def s5_core_fwd(proj, a_re, a_im, wb_re, wb_im, wc_re, wc_im, *, name):
    L = proj.shape[0]
    parts, cu, W = wb_re.shape

    def body(u_ref, ar_ref, ai_ref, wbr_ref, wbi_ref, wcr_ref, wci_ref, y_ref, xr_ref, xi_ref, br_ref, bi_ref):
        u = u_ref[...]
        br_ref[...] = _dot(u, wbr_ref[...])
        bi_ref[...] = _dot(u, wbi_ref[...])
        steps, (tr, ti) = _powers(ar_ref[...], ai_ref[...])
        row = lax.broadcasted_iota(jnp.int32, (8, W), 0)

        def shift(y, s):
            return jnp.where(row >= s, pltpu.roll(y, s, 0), 0.0)

        def step(t8, carry):
            cr, ci = carry
            base = pl.multiple_of(t8 * 8, 8)
            yr, yi = _block_scan(br_ref[pl.ds(base, 8), :], bi_ref[pl.ds(base, 8), :], steps, shift)
            xr = yr + tr * cr - ti * ci
            xi = yi + tr * ci + ti * cr
            xr_ref[pl.ds(base, 8), :] = xr
            xi_ref[pl.ds(base, 8), :] = xi
            return jnp.broadcast_to(xr[7:8, :], (8, W)), jnp.broadcast_to(xi[7:8, :], (8, W))

        zero = jnp.zeros((8, W), F32)
        lax.fori_loop(0, L // 8, step, (zero, zero), unroll=2)
        y_ref[...] = _dot(xr_ref[...], wcr_ref[...]) + _dot(xi_ref[...], wci_ref[...])

    ucol = pl.BlockSpec((L, cu), lambda t: (0, t))
    vec = pl.BlockSpec((1, W), lambda t: (0, t))
    col = pl.BlockSpec((L, W), lambda t: (0, t))
    wb = pl.BlockSpec((None, cu, W), lambda t: (t, 0, 0))
    wc = pl.BlockSpec((None, W, cu), lambda t: (t, 0, 0))
    return pl.pallas_call(body, grid=(parts,), in_specs=[ucol, vec, vec, wb, wb, wc, wc], out_specs=[ucol, col, col],
                          out_shape=[jax.ShapeDtypeStruct((L, parts * cu), F32)]
                          + [jax.ShapeDtypeStruct((L, parts * W), F32)] * 2,
                          scratch_shapes=[pltpu.VMEM((L, W), F32)] * 2,
                          compiler_params=_params(dimension_semantics=("parallel",)), name=name)(
        proj, a_re, a_im, wb_re, wb_im, wc_re, wc_im)


def s5_core_bwd(dy, du_d, proj, xs_re, xs_im, a_re, a_im, wb_re, wb_im, wc_re, wc_im, *, name):
    L = proj.shape[0]
    parts, cu, W = wb_re.shape

    def body(dy_ref, dud_ref, u_ref, xr_ref, xi_ref, ar_ref, ai_ref, wbr_ref, wbi_ref, wcr_ref, wci_ref,
             du_ref, dwbr_ref, dwbi_ref, dwcr_ref, dwci_ref, dar_ref, dai_ref, lr_ref, li_ref):
        dy = dy_ref[...]
        lr_ref[...] = _dot_nt(dy, wcr_ref[...])
        li_ref[...] = _dot_nt(dy, wci_ref[...])
        dwcr_ref[...] = _dot_tn(xr_ref[...], dy)
        dwci_ref[...] = _dot_tn(xi_ref[...], dy)
        ar, ai = ar_ref[...], -ai_ref[...]
        steps, (tr, ti) = _powers(ar, ai)
        tr = jnp.concatenate([tr[j:j + 1, :] for j in range(7, -1, -1)], axis=0)
        ti = jnp.concatenate([ti[j:j + 1, :] for j in range(7, -1, -1)], axis=0)
        row8 = lax.broadcasted_iota(jnp.int32, (8, W), 0)
        nblk = L // 8

        def shift(y, s):
            return jnp.where(row8 < 8 - s, pltpu.roll(y, 8 - s, 0), 0.0)

        def step(s, carry):
            cr, ci = carry
            base = pl.multiple_of((nblk - 1 - s) * 8, 8)
            yr, yi = _block_scan(lr_ref[pl.ds(base, 8), :], li_ref[pl.ds(base, 8), :], steps, shift)
            lr = yr + tr * cr - ti * ci
            li = yi + tr * ci + ti * cr
            lr_ref[pl.ds(base, 8), :] = lr
            li_ref[pl.ds(base, 8), :] = li
            return jnp.broadcast_to(lr[0:1, :], (8, W)), jnp.broadcast_to(li[0:1, :], (8, W))

        zero = jnp.zeros((8, W), F32)
        lax.fori_loop(0, nblk, step, (zero, zero), unroll=2)
        row = lax.broadcasted_iota(jnp.int32, (L, W), 0)
        xpr = jnp.where(row >= 1, pltpu.roll(xr_ref[...], 1, 0), 0.0)
        xpi = jnp.where(row >= 1, pltpu.roll(xi_ref[...], 1, 0), 0.0)
        lr, li = lr_ref[...], li_ref[...]
        dar_ref[...] = jnp.sum(lr * xpr + li * xpi, axis=0, keepdims=True)
        dai_ref[...] = jnp.sum(li * xpr - lr * xpi, axis=0, keepdims=True)
        u = u_ref[...]
        dwbr_ref[...] = _dot_tn(u, lr)
        dwbi_ref[...] = _dot_tn(u, li)
        du_ref[...] = (dud_ref[...] + _dot_nt(lr, wbr_ref[...]) + _dot_nt(li, wbi_ref[...])).astype(du_ref.dtype)

    ucol = pl.BlockSpec((L, cu), lambda t: (0, t))
    vec = pl.BlockSpec((1, W), lambda t: (0, t))
    col = pl.BlockSpec((L, W), lambda t: (0, t))
    wb = pl.BlockSpec((None, cu, W), lambda t: (t, 0, 0))
    wc = pl.BlockSpec((None, W, cu), lambda t: (t, 0, 0))
    return pl.pallas_call(
        body, grid=(parts,), in_specs=[ucol, ucol, ucol, col, col, vec, vec, wb, wb, wc, wc],
        out_specs=[ucol, wb, wb, wc, wc, vec, vec],
        out_shape=[jax.ShapeDtypeStruct((L, parts * cu), BF16)] + [jax.ShapeDtypeStruct((parts, cu, W), F32)] * 2
        + [jax.ShapeDtypeStruct((parts, W, cu), F32)] * 2 + [jax.ShapeDtypeStruct((1, parts * W), F32)] * 2,
        scratch_shapes=[pltpu.VMEM((L, W), F32)] * 2,
        compiler_params=_params(dimension_semantics=("parallel",)), name=name,
    )(dy, du_d, proj, xs_re, xs_im, a_re, a_im, wb_re, wb_im, wc_re, wc_im)


def _gelu(y):
    c = math.sqrt(2.0 / math.pi)
    t = jnp.tanh(c * (y + 0.044715 * y * y * y))
    return 0.5 * y * (1.0 + t), t


def s5_out_fwd(y0, proj, dvec, glu_w, glu_b, *, name):
    L, C = y0.shape
    tr = _pick(L, (256, 128))

    def body(y_ref, u_ref, d_ref, w_ref, b_ref, o_ref):
        z, _ = _gelu(y_ref[...] + d_ref[...] * u_ref[...])
        zg = _dot(z, w_ref[...]) + b_ref[...]
        o_ref[...] = (z * _sigmoid(zg)).astype(o_ref.dtype)

    row = pl.BlockSpec((tr, C), lambda i: (i, 0))
    vec = pl.BlockSpec((1, C), lambda i: (0, 0))
    wsp = pl.BlockSpec((C, C), lambda i: (0, 0))
    return pl.pallas_call(body, grid=(L // tr,), in_specs=[row, row, vec, wsp, vec], out_specs=row,
                          out_shape=jax.ShapeDtypeStruct((L, C), BF16), name=name)(
        y0, proj, dvec, glu_w, glu_b)


def s5_out_bwd(y0, proj, dvec, glu_w, glu_b, dcat, *, name):
    L, C = y0.shape
    tr = _pick(L, (256, 128))

    def body(y_ref, u_ref, d_ref, w_ref, b_ref, do_ref, dy_ref, dud_ref, z_ref, dzg_ref, db_ref, dd_ref):
        u = u_ref[...]
        y = y_ref[...] + d_ref[...] * u
        z, t = _gelu(y)
        zg = _dot(z, w_ref[...]) + b_ref[...]
        s = _sigmoid(zg)
        do = do_ref[...]
        dzg = do * z * s * (1.0 - s)
        dz = do * s + _dot_nt(dzg, w_ref[...])
        c = math.sqrt(2.0 / math.pi)
        dgelu = 0.5 * (1.0 + t) + 0.5 * y * (1.0 - t * t) * c * (1.0 + 3.0 * 0.044715 * y * y)
        dy = dz * dgelu

        @pl.when(pl.program_id(0) == 0)
        def _():
            db_ref[...] = jnp.zeros_like(db_ref)
            dd_ref[...] = jnp.zeros_like(dd_ref)

        db_ref[...] += jnp.sum(dzg, axis=0, keepdims=True)
        dd_ref[...] += jnp.sum(dy * u, axis=0, keepdims=True)
        dy_ref[...] = dy
        dud_ref[...] = dy * d_ref[...]
        z_ref[...] = z.astype(BF16)
        dzg_ref[...] = dzg.astype(BF16)

    row = pl.BlockSpec((tr, C), lambda i: (i, 0))
    vec = pl.BlockSpec((1, C), lambda i: (0, 0))
    wsp = pl.BlockSpec((C, C), lambda i: (0, 0))
    return pl.pallas_call(body, grid=(L // tr,), in_specs=[row, row, vec, wsp, vec, row],
                          out_specs=[row, row, row, row, vec, vec],
                          out_shape=[jax.ShapeDtypeStruct((L, C), F32), jax.ShapeDtypeStruct((L, C), F32),
                                     jax.ShapeDtypeStruct((L, C), BF16), jax.ShapeDtypeStruct((L, C), BF16),
                                     jax.ShapeDtypeStruct((1, C), F32), jax.ShapeDtypeStruct((1, C), F32)],
                          compiler_params=_params(dimension_semantics=("arbitrary",)), name=name)(
        y0, proj, dvec, glu_w, glu_b, dcat)


def _dot_tri(tri, x, tri_left=True):
    t = tri.astype(BF16)
    x1 = x.astype(BF16)
    r1 = x - x1.astype(F32)
    x2 = r1.astype(BF16)
    x3 = (r1 - x2.astype(F32)).astype(BF16)
    dot = (lambda p: jnp.dot(t, p, preferred_element_type=F32)) if tri_left else (
        lambda p: jnp.dot(p, t, preferred_element_type=F32))
    return dot(x1) + dot(x2) + dot(x3)


def _hg_gates(xq, xf, lb, tri):
    C = xq.shape[0]
    sq = _sigmoid(xq)
    q = xq * sq
    sg = _sigmoid(xf)
    f = lb + (1.0 - lb) * sg
    kk = 1.0 - f
    b = _dot_tri(tri, jnp.log(f))
    bm = b[C // 2 - 1:C // 2, :]
    bl = b[C - 1:C, :]
    eb = jnp.exp(b)
    eqm, ekm, ekl = jnp.exp(b - bm), jnp.exp(bm - b), jnp.exp(bl - b)
    return dict(sq=sq, q=q, sg=sg, f=f, kk=kk, eb=eb, ebl=jnp.exp(bl), eqm=eqm, ekm=ekm, ekl=ekl,
                qb=q * eb, qt=q * eqm, kt=kk * ekm, kh=kk * ekl)


def _tri(C, lower):
    r = lax.broadcasted_iota(jnp.int32, (C, C), 0)
    c = lax.broadcasted_iota(jnp.int32, (C, C), 1)
    return (r >= c) if lower else (c >= r)


def hgrn_fwd(proj, lb, norm_g, *, name):
    L = proj.shape[0]
    C, H, K = HG_CHUNK, HG_HEADS, HG_DIM
    HK = H * K
    nc = L // C

    def body(q_ref, f_ref, i_ref, g_ref, lb_ref, ng_ref, o_ref, sall_ref, st_ref):
        @pl.when(pl.program_id(0) == 0)
        def _():
            st_ref[...] = jnp.zeros_like(st_ref)

        mask = _tri(C, True)
        sts = [st_ref[h] for h in range(H)]
        for s in range(S):
            rs = slice(s * C, (s + 1) * C)
            gt = _hg_gates(q_ref[rs, :], f_ref[rs, :], lb_ref[...], mask.astype(F32))
            v_all = i_ref[rs, :]
            outs = []
            for h in range(H):
                sl = slice(h * K, (h + 1) * K)
                v, st = v_all[:, sl], sts[h]
                sall_ref[s, h] = st
                att = jnp.where(mask, _dot_nt(gt["qt"][:, sl], gt["kt"][:, sl]), 0.0)
                o = _dot(att, v) + _dot_nt(gt["qb"][:, sl], st)
                sts[h] = st * gt["ebl"][:, sl] + _dot_tn(v, gt["kh"][:, sl])
                outs.append(o * lax.rsqrt(jnp.mean(o * o, axis=-1, keepdims=True) + NORM_EPS))
            xg = g_ref[rs, :]
            o_ref[rs, :] = (jnp.concatenate(outs, axis=1) * ng_ref[...] * (xg * _sigmoid(xg))).astype(o_ref.dtype)
        for h in range(H):
            st_ref[h] = sts[h]

    S = HG_STEP_CHUNKS

    def blk(cb):
        return pl.BlockSpec((S * C, HK), lambda i: (i, cb))

    vec = pl.BlockSpec((1, HK), lambda i: (0, 0))
    return pl.pallas_call(
        body, grid=(nc // S,), in_specs=[blk(1), blk(2), blk(3), blk(4), vec, vec],
        out_specs=[pl.BlockSpec((S * C, HK), lambda i: (i, 0)), pl.BlockSpec((S, H, K, K), lambda i: (i, 0, 0, 0))],
        out_shape=[jax.ShapeDtypeStruct((L, HK), BF16), jax.ShapeDtypeStruct((nc, H, K, K), F32)],
        scratch_shapes=[pltpu.VMEM((H, K, K), F32)],
        compiler_params=_params(dimension_semantics=("arbitrary",)), name=name,
    )(proj, proj, proj, proj, lb, norm_g)


def hgrn_bwd(proj, lb, norm_g, sall, dcat, *, name):
    L = proj.shape[0]
    C, H, K = HG_CHUNK, HG_HEADS, HG_DIM
    HK = H * K
    nc = L // C

    def body(q_ref, f_ref, i_ref, g_ref, lb_ref, ng_ref, sall_ref, do_ref, dx_ref, dlb_ref, dng_ref, dst_ref):
        @pl.when(pl.program_id(0) == 0)
        def _():
            dst_ref[...] = jnp.zeros_like(dst_ref)
            dlb_ref[...] = jnp.zeros_like(dlb_ref)
            dng_ref[...] = jnp.zeros_like(dng_ref)

        mask = _tri(C, True)
        lb_all, ng = lb_ref[...], ng_ref[...]
        dsts = [dst_ref[h] for h in range(H)]
        for s in reversed(range(S)):
            rs = slice(s * C, (s + 1) * C)
            dsts = chunk_bwd(rs, s, dsts, mask, lb_all, ng, q_ref, f_ref, i_ref, g_ref, sall_ref, do_ref,
                             dx_ref, dlb_ref, dng_ref)
        for h in range(H):
            dst_ref[h] = dsts[h]

    def chunk_bwd(rs, s, dsts, mask, lb_all, ng, q_ref, f_ref, i_ref, g_ref, sall_ref, do_ref, dx_ref, dlb_ref, dng_ref):
        xq, xg, v_all = q_ref[rs, :], g_ref[rs, :], i_ref[rs, :]
        gt = _hg_gates(xq, f_ref[rs, :], lb_all, mask.astype(F32))
        sgg = _sigmoid(xg)
        d_ob = do_ref[rs, :]
        d_on = d_ob * (xg * sgg)
        doh = d_on * ng
        ohs, d_qts, d_qbs, d_kts, d_khs, dvs, d_bls, new_dsts = [], [], [], [], [], [], [], []
        for h in range(H):
            sl = slice(h * K, (h + 1) * K)
            v, st, dst = v_all[:, sl], sall_ref[s, h], dsts[h]
            qt, kt, kh, qb = gt["qt"][:, sl], gt["kt"][:, sl], gt["kh"][:, sl], gt["qb"][:, sl]
            att = jnp.where(mask, _dot_nt(qt, kt), 0.0)
            o = _dot(att, v) + _dot_nt(qb, st)
            r = lax.rsqrt(jnp.mean(o * o, axis=-1, keepdims=True) + NORM_EPS)
            oh = o * r
            do = r * (doh[:, sl] - oh * jnp.mean(doh[:, sl] * oh, axis=-1, keepdims=True))
            datt = jnp.where(mask, _dot_nt(do, v), 0.0)
            dvs.append(_dot_tn(att, do) + _dot_nt(kh, dst))
            d_qbs.append(_dot_x3(do, st))
            d_qts.append(_dot_x3(datt, kt))
            d_kts.append(_dot_x3(datt, qt, ((0,), (0,))))
            d_kh = _dot_x3(v, dst)
            d_khs.append(d_kh)
            d_bls.append(jnp.sum(dst * st, axis=0, keepdims=True) * gt["ebl"][:, sl]
                         + jnp.sum(d_kh * kh, axis=0, keepdims=True))
            new_dsts.append(dst * gt["ebl"][:, sl] + _dot_tn(do, qb))
            ohs.append(oh)
        oh, d_qt, d_qb, d_kt, d_kh, dv, d_bl = (jnp.concatenate(p, axis=1) for p in
                                                (ohs, d_qts, d_qbs, d_kts, d_khs, dvs, d_bls))
        dxg = d_ob * (oh * ng) * (sgg * (1.0 + xg * (1.0 - sgg)))
        dng_ref[...] += jnp.sum(d_on * oh, axis=0, keepdims=True)
        dq = d_qt * gt["eqm"] + d_qb * gt["eb"]
        db = d_qt * gt["qt"] + d_qb * gt["qb"] - d_kt * gt["kt"] - d_kh * gt["kh"]
        rowi = lax.broadcasted_iota(jnp.int32, (C, HK), 0)
        db = db + jnp.where(rowi == C - 1, d_bl, 0.0)
        dkk = d_kt * gt["ekm"] + d_kh * gt["ekl"]
        dlg = _dot_tri(_tri(C, False).astype(F32), db)
        df = dlg / gt["f"] - dkk
        sg, sq = gt["sg"], gt["sq"]
        dlb_ref[...] += jnp.sum(df * (1.0 - sg), axis=0, keepdims=True)
        dx_ref[rs, 0:HK] = (dq * (sq * (1.0 + xq * (1.0 - sq)))).astype(dx_ref.dtype)
        dx_ref[rs, HK:2 * HK] = (df * (1.0 - lb_all) * sg * (1.0 - sg)).astype(dx_ref.dtype)
        dx_ref[rs, 2 * HK:3 * HK] = dv.astype(dx_ref.dtype)
        dx_ref[rs, 3 * HK:4 * HK] = dxg.astype(dx_ref.dtype)
        return new_dsts

    S = HG_STEP_CHUNKS
    ns = nc // S

    def blk(cb):
        return pl.BlockSpec((S * C, HK), lambda i: (ns - 1 - i, cb))

    vec = pl.BlockSpec((1, HK), lambda i: (0, 0))
    return pl.pallas_call(
        body, grid=(ns,),
        in_specs=[blk(1), blk(2), blk(3), blk(4), vec, vec,
                  pl.BlockSpec((S, H, K, K), lambda i: (ns - 1 - i, 0, 0, 0)), blk(1)],
        out_specs=[pl.BlockSpec((S * C, 4 * HK), lambda i: (ns - 1 - i, 0)), vec, vec],
        out_shape=[jax.ShapeDtypeStruct((L, 4 * HK), BF16), jax.ShapeDtypeStruct((1, HK), F32),
                   jax.ShapeDtypeStruct((1, HK), F32)],
        scratch_shapes=[pltpu.VMEM((H, K, K), F32)],
        compiler_params=_params(dimension_semantics=("arbitrary",)), name=name,
    )(proj, proj, proj, proj, lb, norm_g, sall, dcat)


def _shift_down(x, k, row):
    return jnp.where(row >= k, pltpu.roll(x, k, 0), 0.0)


def _shift_up(x, k, row):
    n = x.shape[0]
    return jnp.where(row < n - k, pltpu.roll(x, n - k, 0), 0.0)


def convgate_fwd(hu, conv_w, conv_b, *, name):
    L, C2 = hu.shape
    C = C2 // 2
    tc = _pick(C, (256, 128))
    nb = C // tc

    def body(a_ref, b_ref, wa_ref, wb_ref, ba_ref, bb_ref, o_ref):
        row = lax.broadcasted_iota(jnp.int32, (L, tc), 0)

        def conv(x, w, bias):
            return w[2:3, :] * x + w[1:2, :] * _shift_down(x, 1, row) + w[0:1, :] * _shift_down(x, 2, row) + bias

        ca = conv(a_ref[...], wa_ref[...], ba_ref[...])
        cb = conv(b_ref[...], wb_ref[...], bb_ref[...])
        o_ref[...] = (ca * _sigmoid(ca) * cb).astype(o_ref.dtype)

    def col(off, rows):
        return pl.BlockSpec((rows, tc), lambda j: (0, j + off))

    return pl.pallas_call(
        body, grid=(nb,), in_specs=[col(0, L), col(nb, L), col(0, 3), col(nb, 3), col(0, 1), col(nb, 1)],
        out_specs=col(0, L), out_shape=jax.ShapeDtypeStruct((L, C), BF16),
        compiler_params=_params(dimension_semantics=("parallel",)), name=name,
    )(hu, hu, conv_w, conv_w, conv_b, conv_b)


def convgate_bwd(hu, conv_w, conv_b, dact, *, name):
    L, C2 = hu.shape
    C = C2 // 2
    tc = _pick(C, (256, 128))
    nb = C // tc

    def body(a_ref, b_ref, wa_ref, wb_ref, ba_ref, bb_ref, d_ref, dxa_ref, dxb_ref, dwa_ref, dwb_ref, dba_ref, dbb_ref):
        row = lax.broadcasted_iota(jnp.int32, (L, tc), 0)

        def conv(x, w, bias):
            x1 = _shift_down(x, 1, row)
            x2 = _shift_down(x, 2, row)
            return w[2:3, :] * x + w[1:2, :] * x1 + w[0:1, :] * x2 + bias, x1, x2

        xa, xb = a_ref[...], b_ref[...]
        wa, wb = wa_ref[...], wb_ref[...]
        ca, xa1, xa2 = conv(xa, wa, ba_ref[...])
        cb, xb1, xb2 = conv(xb, wb, bb_ref[...])
        d = d_ref[...]
        sa = _sigmoid(ca)
        dca = d * cb * (sa * (1.0 + ca * (1.0 - sa)))
        dcb = d * (ca * sa)

        def back(dc, w, x, x1, x2, dx_ref, dw_ref, db_ref):
            dx = w[2:3, :] * dc + w[1:2, :] * _shift_up(dc, 1, row) + w[0:1, :] * _shift_up(dc, 2, row)
            dx_ref[...] = dx.astype(dx_ref.dtype)
            dw_ref[...] = jnp.concatenate([jnp.sum(dc * x2, axis=0, keepdims=True),
                                           jnp.sum(dc * x1, axis=0, keepdims=True),
                                           jnp.sum(dc * x, axis=0, keepdims=True)], axis=0)
            db_ref[...] = jnp.sum(dc, axis=0, keepdims=True)

        back(dca, wa, xa, xa1, xa2, dxa_ref, dwa_ref, dba_ref)
        back(dcb, wb, xb, xb1, xb2, dxb_ref, dwb_ref, dbb_ref)

    def col(off, rows):
        return pl.BlockSpec((rows, tc), lambda j: (0, j + off))

    outs = pl.pallas_call(
        body, grid=(nb,),
        in_specs=[col(0, L), col(nb, L), col(0, 3), col(nb, 3), col(0, 1), col(nb, 1), col(0, L)],
        out_specs=[col(0, L), col(0, L), col(0, 3), col(0, 3), col(0, 1), col(0, 1)],
        out_shape=[jax.ShapeDtypeStruct((L, C), BF16)] * 2 + [jax.ShapeDtypeStruct((3, C), F32)] * 2
        + [jax.ShapeDtypeStruct((1, C), F32)] * 2,
        compiler_params=_params(dimension_semantics=("parallel",)), name=name,
    )(hu, hu, conv_w, conv_w, conv_b, conv_b, dact)
    dxa, dxb, dwa, dwb, dba, dbb = outs
    return (jnp.concatenate([dxa, dxb], axis=1), jnp.concatenate([dwa, dwb], axis=1),
            jnp.concatenate([dba, dbb], axis=1))


def rope_tables(positions):
    half = ROT_DIM // 2
    inv_freq = ROPE_THETA ** (-jnp.arange(half, dtype=F32) * 2.0 / ROT_DIM)
    ang = positions.astype(F32)[:, None] * inv_freq
    cos, sin = jnp.cos(ang), jnp.sin(ang)
    L = positions.shape[0]
    one = jnp.ones((L, ATT_E - ROT_DIM), F32)
    zero = jnp.zeros((L, ATT_E - ROT_DIM), F32)
    zh = jnp.zeros((L, half), F32)
    tc = jnp.concatenate([cos, cos, one], axis=1)
    ts1 = jnp.concatenate([zh, sin, zero], axis=1)
    ts2 = jnp.concatenate([-sin, zh, zero], axis=1)
    return tuple(jnp.concatenate([t, t], axis=1) for t in (tc, ts1, ts2))


def norm_mm(x, g, w_t, *, tabs=None, name):
    L, D = x.shape
    N = w_t.shape[0]
    W = 512
    tm = _pick(L, (1024, 512, 256, 128))
    nq = N // (3 * W)
    scale = ATT_E ** -0.5
    rope = tabs is not None

    def body(x_ref, g_ref, b_ref, *rest):
        hn_ref, o_ref, hn_scr = rest[-3:]
        j = pl.program_id(1)

        @pl.when(j == 0)
        def _():
            xv = x_ref[...]
            r = lax.rsqrt(jnp.mean(xv * xv, axis=-1, keepdims=True) + NORM_EPS)
            hn = (xv * r * g_ref[...]).astype(BF16)
            hn_scr[...] = hn
            hn_ref[...] = hn

        out = _dot_nt(hn_scr[...], b_ref[...])
        if rope:
            c_ref, s1_ref, s2_ref = rest[:3]
            c = jnp.concatenate([c_ref[...]] * 4, axis=1)
            s1 = jnp.concatenate([s1_ref[...]] * 4, axis=1)
            s2 = jnp.concatenate([s2_ref[...]] * 4, axis=1)
            rot = out * c + pltpu.roll(out, 8, 1) * s1 + pltpu.roll(out, W - 8, 1) * s2
            out = jnp.where(j < 2 * nq, rot * jnp.where(j < nq, scale, 1.0), out)
        o_ref[...] = out

    row = pl.BlockSpec((tm, D), lambda i, j: (i, 0))
    tab = pl.BlockSpec((tm, 128), lambda i, j: (i, 0))
    return pl.pallas_call(body, grid=(L // tm, N // W),
                          in_specs=[row, pl.BlockSpec((1, D), lambda i, j: (0, 0)), pl.BlockSpec((W, D), lambda i, j: (j, 0))]
                          + ([tab, tab, tab] if rope else []),
                          out_specs=[row, pl.BlockSpec((tm, W), lambda i, j: (i, j))],
                          out_shape=[jax.ShapeDtypeStruct((L, D), BF16), jax.ShapeDtypeStruct((L, N), F32)],
                          scratch_shapes=[pltpu.VMEM((tm, D), BF16)],
                          compiler_params=_params(dimension_semantics=("parallel", "arbitrary")), name=name)(
        x, g.reshape(1, D), w_t, *(tabs or ()))


def rope_bwd(slabs, tabs, *, name):
    L, W = slabs[0].shape
    tr = _pick(L, (256, 128))
    nq = len(slabs) // 3
    scale = ATT_E ** -0.5

    def body(*refs):
        d_refs, (c_ref, s1_ref, s2_ref, o_ref) = refs[:3 * nq], refs[3 * nq:]
        c = jnp.concatenate([c_ref[...]] * 4, axis=1)
        s1 = jnp.concatenate([s1_ref[...]] * 4, axis=1)
        s2 = jnp.concatenate([s2_ref[...]] * 4, axis=1)
        for j, d_ref in enumerate(d_refs):
            dy = d_ref[...]
            if j < 2 * nq:
                dy = dy * c + pltpu.roll(dy * s1, W - 8, 1) + pltpu.roll(dy * s2, 8, 1)
            if j < nq:
                dy = dy * scale
            o_ref[:, j * W:(j + 1) * W] = dy.astype(o_ref.dtype)

    slab = pl.BlockSpec((tr, W), lambda i: (i, 0))
    tab = pl.BlockSpec((tr, 128), lambda i: (i, 0))
    return pl.pallas_call(body, grid=(L // tr,), in_specs=[slab] * (3 * nq) + [tab, tab, tab],
                          out_specs=pl.BlockSpec((tr, 3 * nq * W), lambda i: (i, 0)),
                          out_shape=jax.ShapeDtypeStruct((L, 3 * nq * W), BF16),
                          compiler_params=_params(dimension_semantics=("parallel",)), name=name)(*slabs, *tabs)


def _att_masks(has_prev):
    qi = lax.broadcasted_iota(jnp.int32, (ATT_BLOCK, ATT_BLOCK), 0)
    kj = lax.broadcasted_iota(jnp.int32, (ATT_BLOCK, ATT_BLOCK), 1)
    return qi >= kj, (kj >= qi) & has_prev


ATT_COLS = 128


def _att_rows(j, d, nb):
    B = ATT_BLOCK
    r, n = j // nb, j % nb
    start = r + d * B * n
    has_prev = n > 0
    pstart = jnp.where(has_prev, start - d * B, start)
    if d == 1:
        return pl.ds(pl.multiple_of(start, B), B), pl.ds(pl.multiple_of(pstart, B), B), has_prev
    return pl.ds(start, B, stride=d), pl.ds(pstart, B, stride=d), has_prev


def _qkv_specs(L, g):
    per = ATT_HPG * ATT_E // ATT_COLS
    third = len(ATT_DILATIONS) * per
    return [pl.BlockSpec((L, ATT_COLS), lambda c, base=base: (0, base + c))
            for base in (g * per, third + g * per, 2 * third + g * per)]


def attn_fwd(qkv, g, d, *, name):
    L, W = qkv.shape[0], ATT_HPG * ATT_E
    B, E = ATT_BLOCK, ATT_E
    nblk = L // B
    nb = nblk // d

    def body(q_ref, k_ref, v_ref, o_ref, l_ref):
        def step(j, carry):
            cur, prv, has_prev = _att_rows(j, d, nb)
            mc, mp = _att_masks(has_prev)
            qb, kc, kp, vc, vp = q_ref[cur, :], k_ref[cur, :], k_ref[prv, :], v_ref[cur, :], v_ref[prv, :]
            outs, lses = [], []
            for h in range(ATT_COLS // E):
                sl = slice(h * E, (h + 1) * E)
                sc = jnp.where(mc, _dot_nt(qb[:, sl], kc[:, sl]), NEG_BIG)
                sp = jnp.where(mp, _dot_nt(qb[:, sl], kp[:, sl]), NEG_BIG)
                m = jnp.maximum(jnp.max(sc, axis=-1, keepdims=True), jnp.max(sp, axis=-1, keepdims=True))
                pc = jnp.exp(sc - m)
                pp = jnp.exp(sp - m)
                den = jnp.sum(pc, axis=-1, keepdims=True) + jnp.sum(pp, axis=-1, keepdims=True)
                outs.append((_dot(pc, vc[:, sl]) + _dot(pp, vp[:, sl])) / den)
                lses.append(jnp.broadcast_to(m + jnp.log(den), (B, E)))
            o_ref[cur, :] = jnp.concatenate(outs, axis=1)
            l_ref[cur, :] = jnp.concatenate(lses, axis=1)
            return carry

        lax.fori_loop(0, nblk, step, 0, unroll=4)

    col = pl.BlockSpec((L, ATT_COLS), lambda c: (0, c))
    return pl.pallas_call(body, grid=(W // ATT_COLS,), in_specs=_qkv_specs(L, g), out_specs=[col] * 2,
                          out_shape=[jax.ShapeDtypeStruct((L, W), F32)] * 2,
                          compiler_params=_params(dimension_semantics=("parallel",)), name=name)(qkv, qkv, qkv)


def attn_bwd(qkv, g, lse, do, dl, d, *, name):
    L, W = qkv.shape[0], ATT_HPG * ATT_E
    B, E = ATT_BLOCK, ATT_E
    nblk = L // B
    nb = nblk // d

    def body(q_ref, k_ref, v_ref, l_ref, do_ref, dl_ref, dq_ref, dk_ref, dv_ref):
        dk_ref[...] = jnp.zeros_like(dk_ref)
        dv_ref[...] = jnp.zeros_like(dv_ref)

        def step(j, carry):
            cur, prv, has_prev = _att_rows(j, d, nb)
            mc, mp = _att_masks(has_prev)
            qb, kc, kp, vc, vp = q_ref[cur, :], k_ref[cur, :], k_ref[prv, :], v_ref[cur, :], v_ref[prv, :]
            lb, dob, dlb = l_ref[cur, :], do_ref[cur, :], dl_ref[cur, :]
            dqs, dkc, dkp, dvc, dvp = [], [], [], [], []
            for h in range(ATT_COLS // E):
                sl = slice(h * E, (h + 1) * E)
                qh, doh = qb[:, sl], dob[:, sl]
                lse_h, dl_h = lb[:, h * E:h * E + 1], dlb[:, h * E:h * E + 1]
                pc = jnp.where(mc, jnp.exp(_dot_nt(qh, kc[:, sl]) - lse_h), 0.0)
                pp = jnp.where(mp, jnp.exp(_dot_nt(qh, kp[:, sl]) - lse_h), 0.0)
                dsc = pc * (_dot_nt(doh, vc[:, sl]) - dl_h)
                dsp = pp * (_dot_nt(doh, vp[:, sl]) - dl_h)
                dqs.append(_dot(dsc, kc[:, sl]) + _dot(dsp, kp[:, sl]))
                dkc.append(_dot_tn(dsc, qh))
                dkp.append(_dot_tn(dsp, qh))
                dvc.append(_dot_tn(pc, doh))
                dvp.append(_dot_tn(pp, doh))
            dq_ref[cur, :] = jnp.concatenate(dqs, axis=1)
            dk_ref[cur, :] = dk_ref[cur, :] + jnp.concatenate(dkc, axis=1)
            dv_ref[cur, :] = dv_ref[cur, :] + jnp.concatenate(dvc, axis=1)
            dk_ref[prv, :] = dk_ref[prv, :] + jnp.concatenate(dkp, axis=1)
            dv_ref[prv, :] = dv_ref[prv, :] + jnp.concatenate(dvp, axis=1)
            return carry

        lax.fori_loop(0, nblk, step, 0, unroll=4)

    col = pl.BlockSpec((L, ATT_COLS), lambda c: (0, c))
    return pl.pallas_call(body, grid=(W // ATT_COLS,), in_specs=_qkv_specs(L, g) + [col] * 3, out_specs=[col] * 3,
                          out_shape=[jax.ShapeDtypeStruct((L, W), F32)] * 3,
                          compiler_params=_params(dimension_semantics=("parallel",)), name=name)(
        qkv, qkv, qkv, lse, do, dl)


def _merge_alpha(l_refs):
    ls = [r[...] for r in l_refs]
    m = jnp.maximum(jnp.maximum(ls[0], ls[1]), ls[2])
    es = [jnp.exp(l - m) for l in ls]
    den = es[0] + es[1] + es[2]
    return [e / den for e in es]


def merge_fwd(os_, ls_, *, name):
    L, W = os_[0].shape
    tr = _pick(L, (256, 128))

    def body(o0, o1, o2, l0, l1, l2, out_ref):
        al = _merge_alpha((l0, l1, l2))
        out_ref[...] = (al[0] * o0[...] + al[1] * o1[...] + al[2] * o2[...]).astype(out_ref.dtype)

    row = pl.BlockSpec((tr, W), lambda i: (i, 0))
    return pl.pallas_call(body, grid=(L // tr,), in_specs=[row] * 6, out_specs=row,
                          out_shape=jax.ShapeDtypeStruct((L, W), BF16), name=name)(*os_, *ls_)


def merge_bwd(os_, ls_, do, *, name):
    L, W = do.shape
    tr = _pick(L, (256, 128))

    def body(o0, o1, o2, l0, l1, l2, do_ref, d0, d1, d2, e0, e1, e2):
        al = _merge_alpha((l0, l1, l2))
        dov = do_ref[...]
        r = lax.broadcasted_iota(jnp.int32, (W, W), 0) // ATT_E
        c = lax.broadcasted_iota(jnp.int32, (W, W), 1) // ATT_E
        ones_blk = (r == c).astype(F32)
        t = jnp.zeros_like(dov)
        for a, o in zip(al, (o0, o1, o2)):
            t = t + a * _dot_tri(ones_blk, dov * o[...], tri_left=False)
        for a, d_ref, e_ref in zip(al, (d0, d1, d2), (e0, e1, e2)):
            d_ref[...] = a * dov
            e_ref[...] = a * t

    row = pl.BlockSpec((tr, W), lambda i: (i, 0))
    return pl.pallas_call(body, grid=(L // tr,), in_specs=[row] * 7, out_specs=[row] * 6,
                          out_shape=[jax.ShapeDtypeStruct((L, W), F32)] * 6, name=name)(*os_, *ls_, do)


def _me_and_peers():
    x, y, c = lax.axis_index("x"), lax.axis_index("y"), lax.axis_index("c")
    peers = []
    for k in range(1, N_DEV):
        px = 1 - x if k & 4 else x
        py = 1 - y if k & 2 else y
        pc = 1 - c if k & 1 else c
        peers.append((px, py, pc))
    return (x, y, c), peers


def _index(dev):
    return 4 * dev[0] + 2 * dev[1] + dev[2]


def _hbm(a):
    return pltpu.with_memory_space_constraint(a, pltpu.HBM)


HBM_SPEC = pl.BlockSpec(memory_space=pltpu.HBM)
SEM_SPEC = pl.BlockSpec(memory_space=pltpu.SEMAPHORE)
DATAFLOW = pltpu.SideEffectType.DATAFLOW_SIDE_EFFECTING


def _remote(src_ref, land_ref, slotted, me, peer, src_is_mine, send_sem, recv_sem, k):
    sender, receiver = (me, peer) if src_is_mine else (peer, me)
    src = src_ref.at[_index(receiver)] if slotted else src_ref
    return pltpu.make_async_remote_copy(src_ref=src, dst_ref=land_ref.at[_index(sender)], send_sem=send_sem.at[k],
                                        recv_sem=recv_sem.at[k], device_id=peer, device_id_type=MESH_ID)


SIBLING = 0
SAME_CORE = (1, 3, 5)
OTHER_CORE = (2, 4, 6)


def copies_start(arrays, mode, *, name):
    n = len(arrays)
    slotted = mode == "exchange"
    lands = [lax.empty(a.shape if slotted else (N_DEV,) + a.shape, a.dtype) for a in arrays]
    targets = (SIBLING,) + SAME_CORE if mode == "gather2" else tuple(range(N_DEV - 1))

    def body(*refs):
        x_refs, land_refs = refs[:n], refs[n:2 * n]
        send, recv = refs[2 * n:3 * n], refs[3 * n:4 * n]
        token = refs[-1]
        me, peers = _me_and_peers()
        for w in range(n):
            for k in targets:
                _remote(x_refs[w], land_refs[w], slotted, me, peers[k], True, send[w], recv[w], k).start()
            if not slotted:
                pltpu.make_async_copy(x_refs[w], land_refs[w].at[_index(me)], recv[w].at[N_DEV - 1]).start()
        token[...] = jnp.zeros_like(token)

    sem = pltpu.SemaphoreType.DMA((N_DEV,))
    out_shape = ([sem] * (2 * n) + [pltpu.HBM(a.shape, a.dtype) for a in arrays]
                 + [pltpu.HBM(l.shape, l.dtype) for l in lands] + [jax.ShapeDtypeStruct((8, 128), F32)])
    outs = pl.pallas_call(
        body, name=name, out_shape=out_shape, in_specs=[HBM_SPEC] * (2 * n),
        out_specs=[SEM_SPEC] * (2 * n) + [HBM_SPEC] * (2 * n) + [pl.BlockSpec(memory_space=pltpu.VMEM)],
        input_output_aliases={i: 2 * n + i for i in range(2 * n)},
        compiler_params=pltpu.CompilerParams(has_side_effects=DATAFLOW),
    )(*[_hbm(a) for a in arrays], *[_hbm(l) for l in lands])
    handles = [(outs[w], outs[n + w], outs[2 * n + w], outs[3 * n + w]) for w in range(n)]
    return handles, outs[-1]


def _forward(land_ref, me, peers, j, fsend, frecv, mine):
    block = _index(peers[SAME_CORE[j]] if mine else peers[OTHER_CORE[j]])
    return pltpu.make_async_remote_copy(src_ref=land_ref.at[block], dst_ref=land_ref.at[block], send_sem=fsend.at[j],
                                        recv_sem=frecv.at[j], device_id=peers[SIBLING], device_id_type=MESH_ID)


def copies_forward(handles, after, *, name):
    n = len(handles)

    def body(*refs):
        land_refs, recv = refs[:n], refs[n:2 * n]
        fsend, frecv = refs[2 * n + 1:3 * n + 1], refs[3 * n + 1:4 * n + 1]
        token = refs[-1]
        me, peers = _me_and_peers()
        for w in range(n):
            for j, k in enumerate(SAME_CORE):
                block = land_refs[w].at[_index(peers[k])]
                pltpu.make_async_remote_copy(src_ref=block, dst_ref=block, send_sem=recv[w].at[N_DEV - 1],
                                             recv_sem=recv[w].at[k], device_id=peers[k], device_id_type=MESH_ID).wait_recv()
                _forward(land_refs[w], me, peers, j, fsend[w], frecv[w], True).start()
        token[...] = jnp.zeros_like(token)

    sem = pltpu.SemaphoreType.DMA((len(SAME_CORE),))
    lands = [h[3] for h in handles]
    outs = pl.pallas_call(
        body, name=name,
        out_shape=[sem] * (2 * n) + [pltpu.HBM(l.shape, l.dtype) for l in lands] + [jax.ShapeDtypeStruct((8, 128), F32)],
        in_specs=[HBM_SPEC] * n + [SEM_SPEC] * n + [pl.BlockSpec(memory_space=pl.ANY)],
        out_specs=[SEM_SPEC] * (2 * n) + [HBM_SPEC] * n + [pl.BlockSpec(memory_space=pltpu.VMEM)],
        input_output_aliases={w: 2 * n + w for w in range(n)},
        compiler_params=pltpu.CompilerParams(has_side_effects=DATAFLOW),
    )(*lands, *[h[1] for h in handles], after)
    new = [(h[0], h[1], h[2], outs[2 * n + w], outs[w], outs[n + w]) for w, h in enumerate(handles)]
    return new, outs[-1]


def copies_wait(handle, mode, after, *, name):
    slotted = mode == "exchange"
    two_level = mode == "gather2"
    send_sem, recv_sem, x_thru, land_thru = handle[:4]
    targets = (SIBLING,) + SAME_CORE if two_level else tuple(range(N_DEV - 1))
    arrivals = (SIBLING,) if two_level else targets

    def body(x_ref, land_ref, send_ref, recv_ref, *rest):
        me, peers = _me_and_peers()
        for k in targets:
            _remote(x_ref, land_ref, slotted, me, peers[k], True, send_ref, recv_ref, k).wait_send()
        for k in arrivals:
            _remote(x_ref, land_ref, slotted, me, peers[k], False, send_ref, recv_ref, k).wait_recv()
        if not slotted:
            pltpu.make_async_copy(x_ref, land_ref.at[_index(me)], recv_ref.at[N_DEV - 1]).wait()
        if two_level:
            fsend, frecv = rest[0], rest[1]
            for j in range(len(SAME_CORE)):
                _forward(land_ref, me, peers, j, fsend, frecv, True).wait_send()
                _forward(land_ref, me, peers, j, fsend, frecv, False).wait_recv()

    extra = list(handle[4:])
    return pl.pallas_call(
        body, name=name, out_shape=(pltpu.HBM(x_thru.shape, x_thru.dtype), pltpu.HBM(land_thru.shape, land_thru.dtype)),
        in_specs=[HBM_SPEC, HBM_SPEC, SEM_SPEC, SEM_SPEC] + [SEM_SPEC] * len(extra) + [pl.BlockSpec(memory_space=pl.ANY)],
        out_specs=(HBM_SPEC, HBM_SPEC), input_output_aliases={0: 0, 1: 1},
        compiler_params=pltpu.CompilerParams(has_side_effects=DATAFLOW),
    )(x_thru, land_thru, send_sem, recv_sem, *extra, after)


def cast_bf16(x, *, dep=None, name):
    R, C = x.shape
    tr = _pick(R, (512, 352, 256, 128, 64))
    deps = [] if dep is None else [dep]

    def body(x_ref, *rest):
        rest[-1][...] = x_ref[...].astype(BF16)

    row = pl.BlockSpec((tr, C), lambda i: (i, 0))
    return pl.pallas_call(body, grid=(R // tr,), in_specs=[row] + [pl.BlockSpec((8, 128), lambda i: (0, 0))] * len(deps),
                          out_specs=row, out_shape=jax.ShapeDtypeStruct((R, C), BF16), name=name)(x, *deps)


def cast_bf16_layer(x3, layer, *, name):
    _, R, C = x3.shape
    tr = _pick(R, (512, 352, 256, 128, 64))

    def body(x_ref, o_ref):
        o_ref[...] = x_ref[...].astype(BF16)

    return pl.pallas_call(body, grid=(R // tr,), in_specs=[pl.BlockSpec((None, tr, C), lambda i: (layer, i, 0))],
                          out_specs=pl.BlockSpec((tr, C), lambda i: (i, 0)),
                          out_shape=jax.ShapeDtypeStruct((R, C), BF16), name=name)(x3)


BD_PARTS = 4


def _blockdiag_call(b, build, G, r, c, name):
    gp = G // BD_PARTS

    def body_build(b_ref, o_ref):
        o_ref[...] = jnp.zeros_like(o_ref)
        for g in range(G):
            o_ref[g // gp, (g % gp) * r:(g % gp + 1) * r, (g % gp) * c:(g % gp + 1) * c] = b_ref[g]

    def body_extract(d_ref, o_ref):
        for g in range(G):
            o_ref[g] = d_ref[g // gp, (g % gp) * r:(g % gp + 1) * r, (g % gp) * c:(g % gp + 1) * c]

    out = jax.ShapeDtypeStruct((BD_PARTS, gp * r, gp * c) if build else (G, r, c), F32)
    return pl.pallas_call(body_build if build else body_extract, out_shape=out, name=name)(b)


def make_blockdiag(G, r, c, name):
    @jax.custom_vjp
    def blockdiag(b):
        return _blockdiag_call(b, True, G, r, c, name + "_build")

    def fwd(b):
        return blockdiag(b), None

    def bwd(_, g):
        return (_blockdiag_call(g, False, G, r, c, name + "_extract"),)

    blockdiag.defvjp(fwd, bwd)
    return blockdiag


def _my_index():
    return 4 * lax.axis_index("x") + 2 * lax.axis_index("y") + lax.axis_index("c")


def cols_from_shards(g, *, name):
    _, K, n = g.shape
    tk = _pick(K, (256, 128))

    def body(g_ref, o_ref):
        for i in range(N_DEV):
            o_ref[:, i * n:(i + 1) * n] = g_ref[i]

    return pl.pallas_call(body, grid=(K // tk,), in_specs=[pl.BlockSpec((N_DEV, tk, n), lambda i: (0, i, 0))],
                          out_specs=pl.BlockSpec((tk, N_DEV * n), lambda i: (i, 0)),
                          out_shape=jax.ShapeDtypeStruct((K, N_DEV * n), g.dtype), name=name)(g)


def shards_from_cols(w, *, name):
    K, N = w.shape
    n = N // N_DEV
    tk = _pick(K, (256, 128))

    def body(w_ref, o_ref):
        for i in range(N_DEV):
            o_ref[i] = w_ref[:, i * n:(i + 1) * n].astype(o_ref.dtype)

    return pl.pallas_call(body, grid=(K // tk,), in_specs=[pl.BlockSpec((tk, N), lambda i: (i, 0))],
                          out_specs=pl.BlockSpec((N_DEV, tk, n), lambda i: (0, i, 0)),
                          out_shape=jax.ShapeDtypeStruct((N_DEV, K, n), BF16), name=name)(w)


def _adamw(w, g, m, v):
    m = ADAM_B1 * m + (1.0 - ADAM_B1) * g
    v = ADAM_B2 * v + (1.0 - ADAM_B2) * (g * g)
    m_hat = m / (1.0 - ADAM_B1 ** ADAM_STEP)
    v_hat = v / (1.0 - ADAM_B2 ** ADAM_STEP)
    delta = -ADAM_LR * (m_hat / (jnp.sqrt(v_hat) + ADAM_EPS) + ADAM_WD * w)
    return delta, m, v


def reduce_adamw(recv, own, own_slotted, me, w, m, v, *, layer=0, n_layers=1, into=None, name):
    _, R, C = recv.shape
    tr = _pick(R, (352, 320, 288, 256, 128, 64, 32, 16, 8))
    off = layer * (R // tr)

    def body(me_ref, r_ref, own_ref, w_ref, m_ref, v_ref, *rest):
        g_ref, d_ref, nm_ref, nv_ref = rest[-4:]
        mine = me_ref[0]
        g = None
        for i in range(N_DEV):
            part = jnp.where(mine == i, own_ref[...], r_ref[i]).astype(F32)
            g = part if g is None else g + part
        delta, nm, nv = _adamw(w_ref[...], g, m_ref[...], v_ref[...])
        g_ref[...] = g
        d_ref[...] = delta
        nm_ref[...] = nm
        nv_ref[...] = nv

    row = pl.BlockSpec((tr, C), lambda i, me_ref: (i + off, 0))
    own_spec = (pl.BlockSpec((None, tr, C), lambda i, me_ref: (me_ref[0], i, 0)) if own_slotted
                else pl.BlockSpec((tr, C), lambda i, me_ref: (i, 0)))
    rest = [] if into is None else list(into)
    grid_spec = pltpu.PrefetchScalarGridSpec(
        num_scalar_prefetch=1, grid=(R // tr,),
        in_specs=[pl.BlockSpec((N_DEV, tr, C), lambda i, me_ref: (0, i, 0)), own_spec, row, row, row]
        + [pl.BlockSpec(memory_space=pl.ANY)] * len(rest),
        out_specs=[row] * 4)
    return pl.pallas_call(body, grid_spec=grid_spec, out_shape=[jax.ShapeDtypeStruct((n_layers * R, C), F32)] * 4,
                          input_output_aliases={6 + k: k for k in range(len(rest))},
                          compiler_params=_params(dimension_semantics=("parallel",)), name=name)(
        me.reshape(1).astype(jnp.int32), recv, own, w, m, v, *rest)


def _s5_prepare(A_re, A_im, log_dt, B_re, B_im, C_re, C_im):
    G, P, Cg = S5_GROUPS, S5_STATE, S5_GROUP
    dt = jnp.exp(log_dt)[:, None]
    mag = jnp.exp(A_re * dt)
    ab_re = mag * jnp.cos(A_im * dt)
    ab_im = mag * jnp.sin(A_im * dt)
    den = A_re * A_re + A_im * A_im
    nr, ni = ab_re - 1.0, ab_im
    c_re = (nr * A_re + ni * A_im) / den
    c_im = (ni * A_re - nr * A_im) / den
    Bb_re = c_re[..., None] * B_re - c_im[..., None] * B_im
    Bb_im = c_re[..., None] * B_im + c_im[..., None] * B_re
    def dense_in(b, name):
        return make_blockdiag(G, Cg, P, name)(b.transpose(0, 2, 1))

    def dense_out(c, name):
        return make_blockdiag(G, P, Cg, name)(c.transpose(0, 2, 1))

    return (ab_re.reshape(1, G * P), ab_im.reshape(1, G * P), dense_in(Bb_re, "s5_wb_re"), dense_in(Bb_im, "s5_wb_im"),
            dense_out(C_re, "s5_wc_re"), dense_out(-C_im, "s5_wc_im"))


def _lower_bound(gamma):
    return jnp.cumsum(jax.nn.softmax(gamma, axis=0), axis=0)[0:1]


def _ffn_fwd(h, g_norm, get_w_in, conv_w, conv_b, get_w_out, tag):
    w_in = get_w_in(h)
    hn, hu = norm_mm(h, g_norm, w_in, name=tag + "_in")
    act = convgate_fwd(hu, conv_w, conv_b, name=tag + "_gate")
    w_out = get_w_out(act)
    h_out = mm(act, w_out, res=h, name=tag + "_out")
    return h_out, (hn, hu, act), w_in, w_out


def _ffn_bwd(h, g_norm, w_in, conv_w, conv_b, w_out, saved, dh, tag, send_dw_in, send_dw_out):
    hn, hu, act = saved
    sent = send_dw_out(mm(act, dh, ta=True, out_dtype=BF16, name=tag + "_dwout"))
    dact = mm(dh, w_out, tb=True, dep=sent, name=tag + "_dact")
    dhu, dconv_w, dconv_b = convgate_bwd(hu, conv_w, conv_b, dact, name=tag + "_dgate")
    sent = send_dw_in(mm(dhu, hn, ta=True, out_dtype=BF16, name=tag + "_dwin"))
    dh_in, dg = mm_drms(dhu, w_in, h, g_norm, dh, dep=sent, name=tag + "_dhn")
    return dh_in, dg, dconv_w, dconv_b


def kernel(x, positions, norm_mix, norm_ffn, norm_final, mix_w_in, mix_w_out, s5_A_re, s5_A_im, s5_log_dt, s5_B_re, s5_B_im, s5_C_re, s5_C_im, s5_D, s5_glu_w, s5_glu_b, hgrn_gamma, hgrn_norm, att_w_qkv, att_w_o, ffn_w_in, ffn_conv_w, ffn_conv_b, ffn_w_out, loss_target, m_norm_mix, m_norm_ffn, m_norm_final, m_mix_w_in, m_mix_w_out, m_s5_A_re, m_s5_A_im, m_s5_log_dt, m_s5_B_re, m_s5_B_im, m_s5_C_re, m_s5_C_im, m_s5_D, m_s5_glu_w, m_s5_glu_b, m_hgrn_gamma, m_hgrn_norm, m_att_w_qkv, m_att_w_o, m_ffn_w_in, m_ffn_conv_w, m_ffn_conv_b, m_ffn_w_out, v_norm_mix, v_norm_ffn, v_norm_final, v_mix_w_in, v_mix_w_out, v_s5_A_re, v_s5_A_im, v_s5_log_dt, v_s5_B_re, v_s5_B_im, v_s5_C_re, v_s5_C_im, v_s5_D, v_s5_glu_w, v_s5_glu_b, v_hgrn_gamma, v_hgrn_norm, v_att_w_qkv, v_att_w_o, v_ffn_w_in, v_ffn_conv_w, v_ffn_conv_b, v_ffn_w_out):
    W = dict(norm_mix=norm_mix, norm_ffn=norm_ffn, norm_final=norm_final, mix_w_in=mix_w_in, mix_w_out=mix_w_out,
             s5_A_re=s5_A_re, s5_A_im=s5_A_im, s5_log_dt=s5_log_dt, s5_B_re=s5_B_re, s5_B_im=s5_B_im,
             s5_C_re=s5_C_re, s5_C_im=s5_C_im, s5_D=s5_D, s5_glu_w=s5_glu_w, s5_glu_b=s5_glu_b,
             hgrn_gamma=hgrn_gamma, hgrn_norm=hgrn_norm, att_w_qkv=att_w_qkv, att_w_o=att_w_o, ffn_w_in=ffn_w_in,
             ffn_conv_w=ffn_conv_w, ffn_conv_b=ffn_conv_b, ffn_w_out=ffn_w_out)
    M = dict(norm_mix=m_norm_mix, norm_ffn=m_norm_ffn, norm_final=m_norm_final, mix_w_in=m_mix_w_in,
             mix_w_out=m_mix_w_out, s5_A_re=m_s5_A_re, s5_A_im=m_s5_A_im, s5_log_dt=m_s5_log_dt, s5_B_re=m_s5_B_re,
             s5_B_im=m_s5_B_im, s5_C_re=m_s5_C_re, s5_C_im=m_s5_C_im, s5_D=m_s5_D, s5_glu_w=m_s5_glu_w,
             s5_glu_b=m_s5_glu_b, hgrn_gamma=m_hgrn_gamma, hgrn_norm=m_hgrn_norm, att_w_qkv=m_att_w_qkv,
             att_w_o=m_att_w_o, ffn_w_in=m_ffn_w_in, ffn_conv_w=m_ffn_conv_w, ffn_conv_b=m_ffn_conv_b,
             ffn_w_out=m_ffn_w_out)
    V = dict(norm_mix=v_norm_mix, norm_ffn=v_norm_ffn, norm_final=v_norm_final, mix_w_in=v_mix_w_in,
             mix_w_out=v_mix_w_out, s5_A_re=v_s5_A_re, s5_A_im=v_s5_A_im, s5_log_dt=v_s5_log_dt, s5_B_re=v_s5_B_re,
             s5_B_im=v_s5_B_im, s5_C_re=v_s5_C_re, s5_C_im=v_s5_C_im, s5_D=v_s5_D, s5_glu_w=v_s5_glu_w,
             s5_glu_b=v_s5_glu_b, hgrn_gamma=v_hgrn_gamma, hgrn_norm=v_hgrn_norm, att_w_qkv=v_att_w_qkv,
             att_w_o=v_att_w_o, ffn_w_in=v_ffn_w_in, ffn_conv_w=v_ffn_conv_w, ffn_conv_b=v_ffn_conv_b,
             ffn_w_out=v_ffn_w_out)
    return _step(x[0], positions[0], loss_target[0], W, M, V)


TRANSPOSED = ("mix_w_in", "att_w_qkv", "ffn_w_in")
SMALL = ("norm_mix", "norm_ffn", "norm_final", "s5_A_re", "s5_A_im", "s5_log_dt", "s5_B_re", "s5_B_im", "s5_C_re",
         "s5_C_im", "s5_D", "s5_glu_b", "hgrn_gamma", "hgrn_norm", "ffn_conv_b")
ORDER = ("norm_mix", "norm_ffn", "norm_final", "mix_w_in", "mix_w_out", "s5_A_re", "s5_A_im", "s5_log_dt", "s5_B_re",
         "s5_B_im", "s5_C_re", "s5_C_im", "s5_D", "s5_glu_w", "s5_glu_b", "hgrn_gamma", "hgrn_norm", "att_w_qkv",
         "att_w_o", "ffn_w_in", "ffn_conv_w", "ffn_conv_b", "ffn_w_out")
PACK_COLS = 1024


def _step(x, positions, target, W, M, V):
    L, D = x.shape
    me = 4 * lax.axis_index("x") + 2 * lax.axis_index("y") + lax.axis_index("c")
    n_cw = W["ffn_conv_w"].shape[-1]
    T = {n: tuple(jnp.swapaxes(d[n], -1, -2) for d in (W, M, V)) for n in TRANSPOSED}
    first = {
        "mix_w_in": cast_bf16(T["mix_w_in"][0][0], name="mix_w_in_cast"),
        "conv_w": W["ffn_conv_w"].reshape(6, n_cw),
        "s5_glu_w": cast_bf16(W["s5_glu_w"][0], name="s5_glu_w_cast"),
    }
    first_handles, token = copies_start(list(first.values()), "gather2", name="gather_start_first")
    shards = {
        "mix_w_out": cast_bf16(W["mix_w_out"][0], dep=token, name="mix_w_out_cast"),
        "ffn_w_in0": cast_bf16_layer(T["ffn_w_in"][0], 0, name="ffn_w_in0_cast"),
        "ffn_w_out0": cast_bf16_layer(W["ffn_w_out"], 0, name="ffn_w_out0_cast"),
        "att_w_qkv": cast_bf16(T["att_w_qkv"][0][0], name="att_w_qkv_cast"),
        "att_w_o": cast_bf16(W["att_w_o"][0], name="att_w_o_cast"),
        "ffn_w_in1": cast_bf16_layer(T["ffn_w_in"][0], 1, name="ffn_w_in1_cast"),
        "ffn_w_out1": cast_bf16_layer(W["ffn_w_out"], 1, name="ffn_w_out1_cast"),
    }
    gather_handles, token = copies_start(list(shards.values()), "gather2", name="gather_start")
    gather_handle = dict(zip(list(first) + list(shards), first_handles + gather_handles))

    def forward(keys, after, name):
        new, sent = copies_forward([gather_handle[k] for k in keys], after, name=name)
        gather_handle.update(zip(keys, new))
        return sent

    def gathered(key, after, cols):
        _, land = copies_wait(gather_handle[key], "gather2", after, name=key + "_gwait")
        return cols_from_shards(land, name=key + "_asm") if cols else land.reshape(-1, land.shape[-1])

    conv_b = W["ffn_conv_b"].reshape(2, 1, -1)

    s5_params = (W["s5_A_re"][0], W["s5_A_im"][0], W["s5_log_dt"][0], W["s5_B_re"][0], W["s5_B_im"][0],
                 W["s5_C_re"][0], W["s5_C_im"][0])
    (a_re, a_im, wb_re, wb_im, wc_re, wc_im), s5_prep_vjp = jax.vjp(_s5_prepare, *s5_params)
    dvec = W["s5_D"].reshape(1, S5_WIDTH)
    glu_b = W["s5_glu_b"].reshape(1, S5_WIDTH)
    lb, lb_vjp = jax.vjp(_lower_bound, W["hgrn_gamma"])
    hg_norm = W["hgrn_norm"].reshape(1, -1)
    tabs = rope_tables(positions)

    sent = forward(["mix_w_in", "conv_w", "s5_glu_w"], token, "forward_a")
    w_mix_in = gathered("mix_w_in", sent, False)
    hn0, proj = norm_mm(x, W["norm_mix"][0], w_mix_in, name="l0_proj")
    y0, xs_re, xs_im = s5_core_fwd(proj, a_re, a_im, wb_re, wb_im, wc_re, wc_im, name="s5_core")
    w_glu = gathered("s5_glu_w", y0, False)
    oa = s5_out_fwd(y0, proj, dvec, w_glu, glu_b, name="s5_out")
    ob, hg_states = hgrn_fwd(proj, lb, hg_norm, name="hgrn_fwd")
    forward(["mix_w_out"], ob, "forward_b")
    cat = jnp.concatenate([oa, ob], axis=1)
    w_mix_out = gathered("mix_w_out", cat, False)
    h1 = mm(cat, w_mix_out, res=x, name="l0_mix_out")
    _, cw_all = copies_wait(gather_handle["conv_w"], "gather2", h1, name="conv_w_gwait")
    conv_w = cw_all.transpose(1, 0, 2).reshape(2, 3, N_DEV * n_cw)
    w_ffn_in, w_ffn_out = [None, None], [None, None]
    h2, ffn0_saved, w_ffn_in[0], w_ffn_out[0] = _ffn_fwd(
        h1, W["norm_ffn"][0],
        lambda a: (forward(["ffn_w_in0"], a, "forward_b2"), gathered("ffn_w_in0", a, False))[1], conv_w[0], conv_b[0],
        lambda a: (forward(["ffn_w_out0"], a, "forward_c"), gathered("ffn_w_out0", a, False))[1], "ffn0")

    forward(["att_w_qkv", "att_w_o"], h2, "forward_d")
    w_qkv = gathered("att_w_qkv", h2, False)
    hn2, qkv_r = norm_mm(h2, W["norm_mix"][1], w_qkv, tabs=tabs, name="l1_qkv")
    att_o, att_l = [], []
    for g, d in enumerate(ATT_DILATIONS):
        o_g, l_g = attn_fwd(qkv_r, g, d, name=f"attn_fwd{g}")
        att_o.append(o_g)
        att_l.append(l_g)
    o_att = merge_fwd(att_o, att_l, name="merge_fwd")
    forward(["ffn_w_in1", "ffn_w_out1"], o_att, "forward_e")
    w_o = gathered("att_w_o", o_att, True)
    h3 = mm(o_att, w_o, res=h2, name="l1_mix_out")
    h4, ffn1_saved, w_ffn_in[1], w_ffn_out[1] = _ffn_fwd(
        h3, W["norm_ffn"][1], lambda a: gathered("ffn_w_in1", a, False), conv_w[1], conv_b[1],
        lambda a: gathered("ffn_w_out1", a, False), "ffn1")

    exchanges = {}

    pending = []

    def send_grad(key, g, cols, flush=True):
        if cols:
            parts = shards_from_cols(g, name=key + "_split")
        else:
            parts = g.reshape(N_DEV, g.shape[0] // N_DEV, g.shape[1])
        pending.append((key, parts))
        if not flush:
            return None
        handles, sent = copies_start([p for _, p in pending], "exchange", name=key + "_xstart")
        exchanges.update(zip([k for k, _ in pending], handles))
        pending.clear()
        return sent

    loss, dh4, dg_final = final_loss(h4, W["norm_final"], target, name="final_loss")
    dh3, dg_ffn1, dcw1, dcb1 = _ffn_bwd(h3, W["norm_ffn"][1], w_ffn_in[1], conv_w[1], conv_b[1], w_ffn_out[1],
                                        ffn1_saved, dh4, "ffn1", lambda g: send_grad("ffn_w_in1", g, False),
                                        lambda g: send_grad("ffn_w_out1", g, False, flush=False))
    sent = send_grad("att_w_o", mm(o_att, dh3, ta=True, name="l1_dwo"), True, flush=False)
    d_oatt = mm(dh3, w_o, tb=True, dep=sent, name="l1_dmix")
    mb = merge_bwd(att_o, att_l, d_oatt, name="merge_bwd")
    d_slabs = [attn_bwd(qkv_r, g, att_l[g], mb[g], mb[3 + g], d, name=f"attn_bwd{g}")
               for g, d in enumerate(ATT_DILATIONS)]
    d_qkv = rope_bwd([s[0] for s in d_slabs] + [s[1] for s in d_slabs] + [s[2] for s in d_slabs], tabs,
                     name="rope_bwd")
    sent = send_grad("att_w_qkv", mm(d_qkv, hn2, ta=True, out_dtype=BF16, name="l1_dwqkv"), False)
    dh2, dg_mix1 = mm_drms(d_qkv, w_qkv, h2, W["norm_mix"][1], dh3, dep=sent, name="l1_dhn")

    dh1, dg_ffn0, dcw0, dcb0 = _ffn_bwd(h1, W["norm_ffn"][0], w_ffn_in[0], conv_w[0], conv_b[0], w_ffn_out[0],
                                        ffn0_saved, dh2, "ffn0", lambda g: send_grad("ffn_w_in0", g, False),
                                        lambda g: send_grad("ffn_w_out0", g, False, flush=False))
    sent = send_grad("mix_w_out", mm(cat, dh1, ta=True, out_dtype=BF16, name="l0_dwout"), False)
    dcat = mm(dh1, w_mix_out, tb=True, dep=sent, name="l0_dcat")
    d_hg, dlb, dhg_norm = hgrn_bwd(proj, lb, hg_norm, hg_states, dcat, name="hgrn_bwd")
    dy, du_d, z_bf, dzg, dglu_b, dD = s5_out_bwd(y0, proj, dvec, w_glu, glu_b, dcat, name="s5_dout")
    sent_glu = send_grad("s5_glu_w", mm(z_bf, dzg, ta=True, out_dtype=BF16, name="s5_dglu"), False, flush=False)
    du, dwb_re, dwb_im, dwc_re, dwc_im, da_re, da_im = s5_core_bwd(
        dy, du_d, proj, xs_re, xs_im, a_re, a_im, wb_re, wb_im, wc_re, wc_im, name="s5_dcore")
    s5_small = s5_prep_vjp((da_re, da_im, dwb_re, dwb_im, dwc_re, dwc_im))
    d_proj = jnp.concatenate([du, d_hg], axis=1)
    sent = send_grad("mix_w_in", mm(d_proj, hn0, ta=True, out_dtype=BF16, dep=sent_glu, name="l0_dwin"), False)
    grad_x, dg_mix0 = mm_drms(d_proj, w_mix_in, x, W["norm_mix"][0], dh1, dep=sent, name="l0_dhn")
    (d_gamma,) = lb_vjp(dlb)
    out = {}

    dA_re, dA_im, dlog_dt, dB_re, dB_im, dC_re, dC_im = s5_small
    small_g = dict(norm_mix=jnp.concatenate([dg_mix0, dg_mix1], axis=0), norm_ffn=jnp.concatenate([dg_ffn0, dg_ffn1], axis=0),
                   norm_final=dg_final, s5_A_re=dA_re, s5_A_im=dA_im, s5_log_dt=dlog_dt, s5_B_re=dB_re, s5_B_im=dB_im,
                   s5_C_re=dC_re, s5_C_im=dC_im, s5_D=dD, s5_glu_b=dglu_b, hgrn_gamma=d_gamma, hgrn_norm=dhg_norm,
                   ffn_conv_b=jnp.concatenate([dcb0, dcb1], axis=0))
    conv_w_g = jnp.stack([dcw0, dcw1], axis=0)
    sizes = [math.prod(W[n].shape) for n in SMALL]
    n_conv = conv_w_g.size
    total = sum(sizes) + n_conv + 1
    rows = -(-total // PACK_COLS)
    rows = -(-rows // 8) * 8
    pad = rows * PACK_COLS - total

    def pack(vals, conv_part, last):
        flat = [v.reshape(-1).astype(F32) for v in vals] + [conv_part.reshape(-1), last.reshape(-1),
                                                            jnp.zeros((pad,), F32)]
        return jnp.concatenate(flat).reshape(rows, PACK_COLS)

    def conv_full(shard):
        col_owner = lax.broadcasted_iota(jnp.int32, (2, 3, N_DEV * n_cw), 2) // n_cw
        return jnp.where(col_owner == me, jnp.tile(shard, (1, 1, N_DEV)), 0.0)

    zero1 = jnp.zeros((1,), F32)
    g_pack = pack([small_g[n] for n in SMALL], conv_w_g, loss)
    w_pack = pack([W[n] for n in SMALL], conv_full(W["ffn_conv_w"]), zero1)
    m_pack = pack([M[n] for n in SMALL], conv_full(M["ffn_conv_w"]), zero1)
    v_pack = pack([V[n] for n in SMALL], conv_full(V["ffn_conv_w"]), zero1 + 1.0)
    (small_handle,), small_sent = copies_start([g_pack], "gather", name="small_xstart")

    def finish(name, n_layers):
        w3, m3, v3 = T[name] if name in TRANSPOSED else (W[name], M[name], V[name])
        res = None
        for layer in reversed(range(n_layers)):
            key = name if n_layers == 1 else f"{name}{layer}"
            own, recv = copies_wait(exchanges[key], "exchange", small_sent, name=key + "_xwait")
            _, R, Cn = recv.shape
            res = reduce_adamw(recv, own, True, me, w3.reshape(n_layers * R, Cn), m3.reshape(n_layers * R, Cn),
                               v3.reshape(n_layers * R, Cn), layer=layer, n_layers=n_layers, into=res,
                               name=key + "_adamw")
        res = [r.reshape(w3.shape) for r in res]
        return tuple(jnp.swapaxes(r, -1, -2) for r in res) if name in TRANSPOSED else tuple(res)

    for name in ("ffn_w_out", "ffn_w_in"):
        out[name] = finish(name, 2)
    for name in ("att_w_o", "att_w_qkv", "mix_w_out", "s5_glu_w", "mix_w_in"):
        out[name] = finish(name, 1)

    small_own, small_recv = copies_wait(small_handle, "gather", out["s5_glu_w"][0], name="small_xwait")
    res = reduce_adamw(small_recv, small_own, False, me, w_pack, m_pack, v_pack, name="small_adamw")
    flat = [r.reshape(-1) for r in res]
    off = 0
    for n, sz in zip(SMALL, sizes):
        out[n] = tuple(f[off:off + sz].reshape(W[n].shape) for f in flat)
        off += sz
    conv_res = [f[off:off + n_conv].reshape(2, 3, N_DEV * n_cw) for f in flat]
    out["ffn_conv_w"] = tuple(lax.dynamic_slice(c, (0, 0, me * n_cw), (2, 3, n_cw)) for c in conv_res)
    off += n_conv
    loss_total = flat[0][off]

    result = [loss_total, grad_x[None]]
    for k in range(4):
        result += [out[n][k] for n in ORDER]
    return tuple(result)
```

```python
import functools
import math

import jax
import jax.numpy as jnp
from jax import lax
from jax.experimental import pallas as pl
from jax.experimental.pallas import tpu as pltpu

F32 = jnp.float32
BF16 = jnp.bfloat16
MESH_ID = pl.DeviceIdType.MESH
N_DEV = 8
VMEM_LIMIT_BYTES = 56 * 1024 * 1024

NORM_EPS = 1e-6
S5_WIDTH, S5_GROUP, S5_GROUPS, S5_STATE = 512, 16, 32, 64
HG_HEADS, HG_DIM, HG_CHUNK = 4, 128, 64
HG_STEP_CHUNKS = 4
ATT_E, ATT_HPG, ATT_BLOCK = 64, 8, 128
ATT_DILATIONS = (1, 4, 16)
ROT_DIM, ROPE_THETA = 16, 500000.0
D_FF = 2816
ADAM_LR, ADAM_B1, ADAM_B2, ADAM_EPS, ADAM_WD, ADAM_STEP = 0.001, 0.9, 0.999, 1e-08, 0.01, 10
NEG_BIG = -1e30


def _params(**kw):
    return pltpu.CompilerParams(vmem_limit_bytes=VMEM_LIMIT_BYTES, **kw)


def _pick(n, cands):
    for c in cands:
        if n % c == 0:
            return c
    return n


def _dot(a, b):
    return jnp.dot(a.astype(BF16), b.astype(BF16), preferred_element_type=F32)


def _dot_nt(a, b):
    return lax.dot_general(a.astype(BF16), b.astype(BF16), (((1,), (1,)), ((), ())), preferred_element_type=F32)


def _dot_tn(a, b):
    return lax.dot_general(a.astype(BF16), b.astype(BF16), (((0,), (0,)), ((), ())), preferred_element_type=F32)


def _split2(x):
    hi = x.astype(BF16)
    return hi, (x - hi.astype(F32)).astype(BF16)


def _dot_x3(a, b, contract=((1,), (0,))):
    dn = (contract, ((), ()))
    a1, a2 = _split2(a)
    b1, b2 = _split2(b)
    return (lax.dot_general(a1, b1, dn, preferred_element_type=F32) + lax.dot_general(a1, b2, dn, preferred_element_type=F32)
            + lax.dot_general(a2, b1, dn, preferred_element_type=F32))


def _sigmoid(x):
    return 1.0 / (1.0 + jnp.exp(-x))


V7X_HBM_BYTES_PER_S = 3.2e12
V7X_MXU_FLOPS_PER_S = 0.7e15
GRID_STEP_S = 0.35e-6
MM_VMEM_BUDGET = 40 * 1024 * 1024


def _divisors(n, cands):
    return [c for c in cands if c <= n and n % c == 0] or [n]


def _mm_tiles(m, n, k, sa, sb, so, sr):
    best = None
    for tm in _divisors(m, (2816, 2048, 1408, 1024, 512, 256, 128)):
        for tn in _divisors(n, (2816, 2048, 1408, 1024, 512, 256, 128)):
            for tk in _divisors(k, (k, 2816, 2560, 2304, 2048, 1536, 1408, 1280, 1024, 512, 256, 128)):
                nk = k // tk
                vmem = 2 * (tm * tk * sa + tk * tn * sb + tm * tn * (so + sr)) + (tm * tn * 4 if nk > 1 else 0)
                vmem += tm * tk * 2 * (sa > 2) + tk * tn * 2 * (sb > 2) + tm * tn * 4
                if vmem > MM_VMEM_BUDGET:
                    continue
                ni, nj = m // tm, n // tn
                for i_outer in (True, False):
                    if i_outer:
                        a_reads = 1 if nk == 1 else nj
                        b_reads = 1 if (nk == 1 and nj == 1) else ni
                    else:
                        b_reads = 1 if nk == 1 else ni
                        a_reads = 1 if (nk == 1 and ni == 1) else nj
                    traffic = a_reads * m * k * sa + b_reads * k * n * sb + m * n * (so + sr)
                    t = max(traffic / V7X_HBM_BYTES_PER_S, 2.0 * m * n * k / V7X_MXU_FLOPS_PER_S)
                    t += ni * nj * nk * GRID_STEP_S
                    t += (tm * tk * sa + tk * tn * sb + tm * tn * so) / V7X_HBM_BYTES_PER_S
                    if best is None or t < best[0]:
                        best = (t, tm, tn, tk, i_outer)
    assert best is not None, (m, n, k)
    return best[1:]


def mm(a, b, *, ta=False, tb=False, res=None, out_dtype=F32, dep=None, name):
    m, k = (a.shape[1], a.shape[0]) if ta else a.shape
    n = b.shape[0] if tb else b.shape[1]
    assert (b.shape[1] if tb else b.shape[0]) == k
    has_res = res is not None
    tm, tn, tk, i_outer = _mm_tiles(m, n, k, a.dtype.itemsize, b.dtype.itemsize, jnp.dtype(out_dtype).itemsize,
                                    res.dtype.itemsize if has_res else 0)
    nk = k // tk
    deps = [] if dep is None else [dep]
    dn = (((0 if ta else 1,), (1 if tb else 0,)), ((), ()))

    def body_single(*refs):
        a_ref, b_ref = refs[:2]
        o_ref = refs[-1]
        out = lax.dot_general(a_ref[...].astype(BF16), b_ref[...].astype(BF16), dn, preferred_element_type=F32)
        if has_res:
            out = out + refs[2][...].astype(F32)
        o_ref[...] = out.astype(o_ref.dtype)

    def body(*refs):
        a_ref, b_ref = refs[:2]
        r_ref = refs[2] if has_res else None
        o_ref, acc_ref = refs[-2:]
        kk = pl.program_id(2)
        part = lax.dot_general(a_ref[...].astype(BF16), b_ref[...].astype(BF16), dn, preferred_element_type=F32)

        @pl.when(kk == 0)
        def _():
            acc_ref[...] = part

        @pl.when(kk > 0)
        def _():
            acc_ref[...] += part

        @pl.when(kk == nk - 1)
        def _():
            out = acc_ref[...]
            if has_res:
                out = out + r_ref[...].astype(F32)
            o_ref[...] = out.astype(o_ref.dtype)

    def ij(f):
        return (lambda g0, g1, q: f(g0, g1, q)) if i_outer else (lambda g0, g1, q: f(g1, g0, q))

    a_spec = pl.BlockSpec((tk, tm), ij(lambda i, j, q: (q, i))) if ta else pl.BlockSpec((tm, tk), ij(lambda i, j, q: (i, q)))
    b_spec = pl.BlockSpec((tn, tk), ij(lambda i, j, q: (j, q))) if tb else pl.BlockSpec((tk, tn), ij(lambda i, j, q: (q, j)))
    o_spec = pl.BlockSpec((tm, tn), ij(lambda i, j, q: (i, j)))
    in_specs = [a_spec, b_spec] + ([o_spec] if has_res else []) + [pl.BlockSpec((8, 128), lambda g0, g1, q: (0, 0))] * len(deps)
    args = (a, b) + ((res,) if has_res else ()) + tuple(deps)
    grid = (m // tm, n // tn, nk) if i_outer else (n // tn, m // tm, nk)
    return pl.pallas_call(
        body_single if nk == 1 else body, grid=grid, in_specs=in_specs, out_specs=o_spec,
        out_shape=jax.ShapeDtypeStruct((m, n), out_dtype),
        scratch_shapes=[] if nk == 1 else [pltpu.VMEM((tm, tn), F32)],
        compiler_params=_params(dimension_semantics=("parallel", "parallel", "arbitrary")), name=name,
    )(*args)


def mm_drms(dy_in, w, x, g, dres, *, dep=None, name):
    m, k = dy_in.shape
    D = w.shape[1]
    tm = _pick(m, (512, 256, 128))
    tk = max(_divisors(k, (1536, 1408, 1280, 1024, 512, 256, 128)))
    nk = k // tk
    deps = [] if dep is None else [dep]

    def body(a_ref, b_ref, x_ref, g_ref, dres_ref, *rest):
        dx_ref, dg_ref, acc_ref = rest[-3:]
        i, q = pl.program_id(0), pl.program_id(1)
        part = jnp.dot(a_ref[...], b_ref[...], preferred_element_type=F32)

        @pl.when(q == 0)
        def _():
            acc_ref[...] = part

        @pl.when(q > 0)
        def _():
            acc_ref[...] += part

        @pl.when((i == 0) & (q == 0))
        def _():
            dg_ref[...] = jnp.zeros_like(dg_ref)

        @pl.when(q == nk - 1)
        def _():
            dyv = acc_ref[...]
            xv = x_ref[...]
            r = lax.rsqrt(jnp.mean(xv * xv, axis=-1, keepdims=True) + NORM_EPS)
            xh = xv * r
            dg_ref[...] += jnp.sum(dyv * xh, axis=0, keepdims=True)
            dxh = dyv * g_ref[...]
            dx_ref[...] = dres_ref[...] + r * (dxh - xh * jnp.mean(dxh * xh, axis=-1, keepdims=True))

    row = pl.BlockSpec((tm, D), lambda i, q: (i, 0))
    vec = pl.BlockSpec((1, D), lambda i, q: (0, 0))
    in_specs = [pl.BlockSpec((tm, tk), lambda i, q: (i, q)), pl.BlockSpec((tk, D), lambda i, q: (q, 0)), row, vec, row]
    in_specs += [pl.BlockSpec((8, 128), lambda i, q: (0, 0))] * len(deps)
    return pl.pallas_call(
        body, grid=(m // tm, nk), in_specs=in_specs, out_specs=[row, vec],
        out_shape=[jax.ShapeDtypeStruct((m, D), F32), jax.ShapeDtypeStruct((1, D), F32)],
        scratch_shapes=[pltpu.VMEM((tm, D), F32)],
        compiler_params=_params(dimension_semantics=("arbitrary", "arbitrary")), name=name,
    )(dy_in, w, x, g.reshape(1, D), dres, *deps)


def final_loss(h, g, target, *, name):
    L, D = h.shape
    tr = _pick(L, (256, 128))

    def body(x_ref, g_ref, t_ref, loss_ref, dx_ref, dg_ref):
        xv = x_ref[...]
        gv = g_ref[...]
        r = lax.rsqrt(jnp.mean(xv * xv, axis=-1, keepdims=True) + NORM_EPS)
        xh = xv * r
        err = xh * gv - t_ref[...]

        @pl.when(pl.program_id(0) == 0)
        def _():
            dg_ref[...] = jnp.zeros_like(dg_ref)
            loss_ref[...] = jnp.zeros_like(loss_ref)

        loss_ref[...] += 0.5 * jnp.sum(jnp.mean(err * err, axis=-1, keepdims=True), axis=0, keepdims=True)
        dyv = err * (1.0 / D)
        dg_ref[...] += jnp.sum(dyv * xh, axis=0, keepdims=True)
        dxh = dyv * gv
        dx_ref[...] = r * (dxh - xh * jnp.mean(dxh * xh, axis=-1, keepdims=True))

    row = pl.BlockSpec((tr, D), lambda i: (i, 0))
    vec = pl.BlockSpec((1, D), lambda i: (0, 0))
    one = pl.BlockSpec((1, 1), lambda i: (0, 0))
    return pl.pallas_call(body, grid=(L // tr,), in_specs=[row, vec, row], out_specs=[one, row, vec],
                          out_shape=[jax.ShapeDtypeStruct((1, 1), F32), jax.ShapeDtypeStruct((L, D), F32),
                                     jax.ShapeDtypeStruct((1, D), F32)],
                          compiler_params=_params(dimension_semantics=("arbitrary",)), name=name)(
        h, g.reshape(1, D), target)


def _cmul(ar, ai, br, bi):
    return ar * br - ai * bi, ar * bi + ai * br


def _powers(ar, ai):
    rows = [(ar, ai)]
    for _ in range(7):
        rows.append(_cmul(rows[-1][0], rows[-1][1], ar, ai))
    table = (jnp.concatenate([r[0] for r in rows], axis=0), jnp.concatenate([r[1] for r in rows], axis=0))
    return (rows[0], rows[1], rows[3]), table


def _block_scan(br, bi, steps, shift):
    yr, yi = br, bi
    for s, (pr, pi) in zip((1, 2, 4), steps):
        sr, si = shift(yr, s), shift(yi, s)
        yr, yi = yr + pr * sr - pi * si, yi + pr * si + pi * sr
    return yr, yi


def s5_core_fwd(proj, a_re, a_im, wb_re, wb_im, wc_re, wc_im, *, name):
    L = proj.shape[0]
    parts, cu, W = wb_re.shape

    def body(u_ref, ar_ref, ai_ref, wbr_ref, wbi_ref, wcr_ref, wci_ref, y_ref, xr_ref, xi_ref, br_ref, bi_ref):
        u = u_ref[...]
        br_ref[...] = _dot(u, wbr_ref[...])
        bi_ref[...] = _dot(u, wbi_ref[...])
        steps, (tr, ti) = _powers(ar_ref[...], ai_ref[...])
        row = lax.broadcasted_iota(jnp.int32, (8, W), 0)

        def shift(y, s):
            return jnp.where(row >= s, pltpu.roll(y, s, 0), 0.0)

        def step(t8, carry):
            cr, ci = carry
            base = pl.multiple_of(t8 * 8, 8)
            yr, yi = _block_scan(br_ref[pl.ds(base, 8), :], bi_ref[pl.ds(base, 8), :], steps, shift)
            xr = yr + tr * cr - ti * ci
            xi = yi + tr * ci + ti * cr
            xr_ref[pl.ds(base, 8), :] = xr
            xi_ref[pl.ds(base, 8), :] = xi
            return jnp.broadcast_to(xr[7:8, :], (8, W)), jnp.broadcast_to(xi[7:8, :], (8, W))

        zero = jnp.zeros((8, W), F32)
        lax.fori_loop(0, L // 8, step, (zero, zero), unroll=2)
        y_ref[...] = _dot(xr_ref[...], wcr_ref[...]) + _dot(xi_ref[...], wci_ref[...])

    ucol = pl.BlockSpec((L, cu), lambda t: (0, t))
    vec = pl.BlockSpec((1, W), lambda t: (0, t))
    col = pl.BlockSpec((L, W), lambda t: (0, t))
    wb = pl.BlockSpec((None, cu, W), lambda t: (t, 0, 0))
    wc = pl.BlockSpec((None, W, cu), lambda t: (t, 0, 0))
    return pl.pallas_call(body, grid=(parts,), in_specs=[ucol, vec, vec, wb, wb, wc, wc], out_specs=[ucol, col, col],
                          out_shape=[jax.ShapeDtypeStruct((L, parts * cu), F32)]
                          + [jax.ShapeDtypeStruct((L, parts * W), F32)] * 2,
                          scratch_shapes=[pltpu.VMEM((L, W), F32)] * 2,
                          compiler_params=_params(dimension_semantics=("parallel",)), name=name)(
        proj, a_re, a_im, wb_re, wb_im, wc_re, wc_im)


def s5_core_bwd(dy, du_d, proj, xs_re, xs_im, a_re, a_im, wb_re, wb_im, wc_re, wc_im, *, name):
    L = proj.shape[0]
    parts, cu, W = wb_re.shape

    def body(dy_ref, dud_ref, u_ref, xr_ref, xi_ref, ar_ref, ai_ref, wbr_ref, wbi_ref, wcr_ref, wci_ref,
             du_ref, dwbr_ref, dwbi_ref, dwcr_ref, dwci_ref, dar_ref, dai_ref, lr_ref, li_ref):
        dy = dy_ref[...]
        lr_ref[...] = _dot_nt(dy, wcr_ref[...])
        li_ref[...] = _dot_nt(dy, wci_ref[...])
        dwcr_ref[...] = _dot_tn(xr_ref[...], dy)
        dwci_ref[...] = _dot_tn(xi_ref[...], dy)
        ar, ai = ar_ref[...], -ai_ref[...]
        steps, (tr, ti) = _powers(ar, ai)
        tr = jnp.concatenate([tr[j:j + 1, :] for j in range(7, -1, -1)], axis=0)
        ti = jnp.concatenate([ti[j:j + 1, :] for j in range(7, -1, -1)], axis=0)
        row8 = lax.broadcasted_iota(jnp.int32, (8, W), 0)
        nblk = L // 8

        def shift(y, s):
            return jnp.where(row8 < 8 - s, pltpu.roll(y, 8 - s, 0), 0.0)

        def step(s, carry):
            cr, ci = carry
            base = pl.multiple_of((nblk - 1 - s) * 8, 8)
            yr, yi = _block_scan(lr_ref[pl.ds(base, 8), :], li_ref[pl.ds(base, 8), :], steps, shift)
            lr = yr + tr * cr - ti * ci
            li = yi + tr * ci + ti * cr
            lr_ref[pl.ds(base, 8), :] = lr
            li_ref[pl.ds(base, 8), :] = li
            return jnp.broadcast_to(lr[0:1, :], (8, W)), jnp.broadcast_to(li[0:1, :], (8, W))

        zero = jnp.zeros((8, W), F32)
        lax.fori_loop(0, nblk, step, (zero, zero), unroll=2)
        row = lax.broadcasted_iota(jnp.int32, (L, W), 0)
        xpr = jnp.where(row >= 1, pltpu.roll(xr_ref[...], 1, 0), 0.0)
        xpi = jnp.where(row >= 1, pltpu.roll(xi_ref[...], 1, 0), 0.0)
        lr, li = lr_ref[...], li_ref[...]
        dar_ref[...] = jnp.sum(lr * xpr + li * xpi, axis=0, keepdims=True)
        dai_ref[...] = jnp.sum(li * xpr - lr * xpi, axis=0, keepdims=True)
        u = u_ref[...]
        dwbr_ref[...] = _dot_tn(u, lr)
        dwbi_ref[...] = _dot_tn(u, li)
        du_ref[...] = (dud_ref[...] + _dot_nt(lr, wbr_ref[...]) + _dot_nt(li, wbi_ref[...])).astype(du_ref.dtype)

    ucol = pl.BlockSpec((L, cu), lambda t: (0, t))
    vec = pl.BlockSpec((1, W), lambda t: (0, t))
    col = pl.BlockSpec((L, W), lambda t: (0, t))
    wb = pl.BlockSpec((None, cu, W), lambda t: (t, 0, 0))
    wc = pl.BlockSpec((None, W, cu), lambda t: (t, 0, 0))
    return pl.pallas_call(
        body, grid=(parts,), in_specs=[ucol, ucol, ucol, col, col, vec, vec, wb, wb, wc, wc],
        out_specs=[ucol, wb, wb, wc, wc, vec, vec],
        out_shape=[jax.ShapeDtypeStruct((L, parts * cu), BF16)] + [jax.ShapeDtypeStruct((parts, cu, W), F32)] * 2
        + [jax.ShapeDtypeStruct((parts, W, cu), F32)] * 2 + [jax.ShapeDtypeStruct((1, parts * W), F32)] * 2,
        scratch_shapes=[pltpu.VMEM((L, W), F32)] * 2,
        compiler_params=_params(dimension_semantics=("parallel",)), name=name,
    )(dy, du_d, proj, xs_re, xs_im, a_re, a_im, wb_re, wb_im, wc_re, wc_im)


def _gelu(y):
    c = math.sqrt(2.0 / math.pi)
    t = jnp.tanh(c * (y + 0.044715 * y * y * y))
    return 0.5 * y * (1.0 + t), t


def s5_out_fwd(y0, proj, dvec, glu_w, glu_b, *, name):
    L, C = y0.shape
    tr = _pick(L, (256, 128))

    def body(y_ref, u_ref, d_ref, w_ref, b_ref, o_ref):
        z, _ = _gelu(y_ref[...] + d_ref[...] * u_ref[...])
        zg = _dot(z, w_ref[...]) + b_ref[...]
        o_ref[...] = (z * _sigmoid(zg)).astype(o_ref.dtype)

    row = pl.BlockSpec((tr, C), lambda i: (i, 0))
    vec = pl.BlockSpec((1, C), lambda i: (0, 0))
    wsp = pl.BlockSpec((C, C), lambda i: (0, 0))
    return pl.pallas_call(body, grid=(L // tr,), in_specs=[row, row, vec, wsp, vec], out_specs=row,
                          out_shape=jax.ShapeDtypeStruct((L, 2 * C), BF16), name=name)(
        y0, proj, dvec, glu_w, glu_b)


def s5_out_bwd(y0, proj, dvec, glu_w, glu_b, dcat, *, name):
    L, C = y0.shape
    tr = _pick(L, (256, 128))

    def body(y_ref, u_ref, d_ref, w_ref, b_ref, do_ref, dy_ref, dud_ref, z_ref, dzg_ref, db_ref, dd_ref):
        u = u_ref[...]
        y = y_ref[...] + d_ref[...] * u
        z, t = _gelu(y)
        zg = _dot(z, w_ref[...]) + b_ref[...]
        s = _sigmoid(zg)
        do = do_ref[...]
        dzg = do * z * s * (1.0 - s)
        dz = do * s + _dot_nt(dzg, w_ref[...])
        c = math.sqrt(2.0 / math.pi)
        dgelu = 0.5 * (1.0 + t) + 0.5 * y * (1.0 - t * t) * c * (1.0 + 3.0 * 0.044715 * y * y)
        dy = dz * dgelu

        @pl.when(pl.program_id(0) == 0)
        def _():
            db_ref[...] = jnp.zeros_like(db_ref)
            dd_ref[...] = jnp.zeros_like(dd_ref)

        db_ref[...] += jnp.sum(dzg, axis=0, keepdims=True)
        dd_ref[...] += jnp.sum(dy * u, axis=0, keepdims=True)
        dy_ref[...] = dy
        dud_ref[...] = dy * d_ref[...]
        z_ref[...] = z.astype(BF16)
        dzg_ref[...] = dzg.astype(BF16)

    row = pl.BlockSpec((tr, C), lambda i: (i, 0))
    vec = pl.BlockSpec((1, C), lambda i: (0, 0))
    wsp = pl.BlockSpec((C, C), lambda i: (0, 0))
    return pl.pallas_call(body, grid=(L // tr,), in_specs=[row, row, vec, wsp, vec, row],
                          out_specs=[row, row, row, row, vec, vec],
                          out_shape=[jax.ShapeDtypeStruct((L, C), F32), jax.ShapeDtypeStruct((L, C), F32),
                                     jax.ShapeDtypeStruct((L, C), BF16), jax.ShapeDtypeStruct((L, C), BF16),
                                     jax.ShapeDtypeStruct((1, C), F32), jax.ShapeDtypeStruct((1, C), F32)],
                          compiler_params=_params(dimension_semantics=("arbitrary",)), name=name)(
        y0, proj, dvec, glu_w, glu_b, dcat)


def _dot_tri(tri, x, tri_left=True):
    t = tri.astype(BF16)
    x1 = x.astype(BF16)
    r1 = x - x1.astype(F32)
    x2 = r1.astype(BF16)
    x3 = (r1 - x2.astype(F32)).astype(BF16)
    dot = (lambda p: jnp.dot(t, p, preferred_element_type=F32)) if tri_left else (
        lambda p: jnp.dot(p, t, preferred_element_type=F32))
    return dot(x1) + dot(x2) + dot(x3)


def _hg_gates(xq, xf, lb, tri):
    C = xq.shape[0]
    sq = _sigmoid(xq)
    q = xq * sq
    sg = _sigmoid(xf)
    f = lb + (1.0 - lb) * sg
    kk = 1.0 - f
    b = _dot_tri(tri, jnp.log(f))
    bm = b[C // 2 - 1:C // 2, :]
    bl = b[C - 1:C, :]
    eb = jnp.exp(b)
    eqm, ekm, ekl = jnp.exp(b - bm), jnp.exp(bm - b), jnp.exp(bl - b)
    return dict(sq=sq, q=q, sg=sg, f=f, kk=kk, eb=eb, ebl=jnp.exp(bl), eqm=eqm, ekm=ekm, ekl=ekl,
                qb=q * eb, qt=q * eqm, kt=kk * ekm, kh=kk * ekl)


def _tri(C, lower):
    r = lax.broadcasted_iota(jnp.int32, (C, C), 0)
    c = lax.broadcasted_iota(jnp.int32, (C, C), 1)
    return (r >= c) if lower else (c >= r)


def hgrn_fwd(proj, lb, norm_g, cat, *, name):
    L = proj.shape[0]
    C, H, K = HG_CHUNK, HG_HEADS, HG_DIM
    HK = H * K
    nc = L // C

    def body(q_ref, f_ref, i_ref, g_ref, lb_ref, ng_ref, cat_ref, o_ref, sall_ref, st_ref):
        @pl.when(pl.program_id(0) == 0)
        def _():
            st_ref[...] = jnp.zeros_like(st_ref)

        mask = _tri(C, True)
        sts = [st_ref[h] for h in range(H)]
        for s in range(S):
            rs = slice(s * C, (s + 1) * C)
            gt = _hg_gates(q_ref[rs, :], f_ref[rs, :], lb_ref[...], mask.astype(F32))
            v_all = i_ref[rs, :]
            outs = []
            for h in range(H):
                sl = slice(h * K, (h + 1) * K)
                v, st = v_all[:, sl], sts[h]
                sall_ref[s, h] = st
                att = jnp.where(mask, _dot_nt(gt["qt"][:, sl], gt["kt"][:, sl]), 0.0)
                o = _dot(att, v) + _dot_nt(gt["qb"][:, sl], st)
                sts[h] = st * gt["ebl"][:, sl] + _dot_tn(v, gt["kh"][:, sl])
                outs.append(o * lax.rsqrt(jnp.mean(o * o, axis=-1, keepdims=True) + NORM_EPS))
            xg = g_ref[rs, :]
            o_ref[rs, :] = (jnp.concatenate(outs, axis=1) * ng_ref[...] * (xg * _sigmoid(xg))).astype(o_ref.dtype)
        for h in range(H):
            st_ref[h] = sts[h]

    S = HG_STEP_CHUNKS

    def blk(cb):
        return pl.BlockSpec((S * C, HK), lambda i: (i, cb))

    vec = pl.BlockSpec((1, HK), lambda i: (0, 0))
    return pl.pallas_call(
        body, grid=(nc // S,), in_specs=[blk(1), blk(2), blk(3), blk(4), vec, vec, pl.BlockSpec(memory_space=pl.ANY)],
        out_specs=[pl.BlockSpec((S * C, HK), lambda i: (i, 1)), pl.BlockSpec((S, H, K, K), lambda i: (i, 0, 0, 0))],
        out_shape=[jax.ShapeDtypeStruct((L, 2 * HK), BF16), jax.ShapeDtypeStruct((nc, H, K, K), F32)],
        input_output_aliases={6: 0},
        scratch_shapes=[pltpu.VMEM((H, K, K), F32)],
        compiler_params=_params(dimension_semantics=("arbitrary",)), name=name,
    )(proj, proj, proj, proj, lb, norm_g, cat)


def hgrn_bwd(proj, lb, norm_g, sall, dcat, du, *, name):
    L = proj.shape[0]
    C, H, K = HG_CHUNK, HG_HEADS, HG_DIM
    HK = H * K
    nc = L // C

    def body(q_ref, f_ref, i_ref, g_ref, lb_ref, ng_ref, sall_ref, do_ref, du_ref, dx_ref, dlb_ref, dng_ref, dst_ref):
        @pl.when(pl.program_id(0) == 0)
        def _():
            dst_ref[...] = jnp.zeros_like(dst_ref)
            dlb_ref[...] = jnp.zeros_like(dlb_ref)
            dng_ref[...] = jnp.zeros_like(dng_ref)

        mask = _tri(C, True)
        lb_all, ng = lb_ref[...], ng_ref[...]
        dx_ref[:, 0:HK] = du_ref[...]
        dsts = [dst_ref[h] for h in range(H)]
        for s in reversed(range(S)):
            rs = slice(s * C, (s + 1) * C)
            dsts = chunk_bwd(rs, s, dsts, mask, lb_all, ng, q_ref, f_ref, i_ref, g_ref, sall_ref, do_ref,
                             dx_ref, dlb_ref, dng_ref)
        for h in range(H):
            dst_ref[h] = dsts[h]

    def chunk_bwd(rs, s, dsts, mask, lb_all, ng, q_ref, f_ref, i_ref, g_ref, sall_ref, do_ref, dx_ref, dlb_ref, dng_ref):
        xq, xg, v_all = q_ref[rs, :], g_ref[rs, :], i_ref[rs, :]
        gt = _hg_gates(xq, f_ref[rs, :], lb_all, mask.astype(F32))
        sgg = _sigmoid(xg)
        d_ob = do_ref[rs, :]
        d_on = d_ob * (xg * sgg)
        doh = d_on * ng
        ohs, d_qts, d_qbs, d_kts, d_khs, dvs, d_bls, new_dsts = [], [], [], [], [], [], [], []
        for h in range(H):
            sl = slice(h * K, (h + 1) * K)
            v, st, dst = v_all[:, sl], sall_ref[s, h], dsts[h]
            qt, kt, kh, qb = gt["qt"][:, sl], gt["kt"][:, sl], gt["kh"][:, sl], gt["qb"][:, sl]
            att = jnp.where(mask, _dot_nt(qt, kt), 0.0)
            o = _dot(att, v) + _dot_nt(qb, st)
            r = lax.rsqrt(jnp.mean(o * o, axis=-1, keepdims=True) + NORM_EPS)
            oh = o * r
            do = r * (doh[:, sl] - oh * jnp.mean(doh[:, sl] * oh, axis=-1, keepdims=True))
            datt = jnp.where(mask, _dot_nt(do, v), 0.0)
            dvs.append(_dot_tn(att, do) + _dot_nt(kh, dst))
            d_qbs.append(_dot_x3(do, st))
            d_qts.append(_dot_x3(datt, kt))
            d_kts.append(_dot_x3(datt, qt, ((0,), (0,))))
            d_kh = _dot_x3(v, dst)
            d_khs.append(d_kh)
            d_bls.append(jnp.sum(dst * st, axis=0, keepdims=True) * gt["ebl"][:, sl]
                         + jnp.sum(d_kh * kh, axis=0, keepdims=True))
            new_dsts.append(dst * gt["ebl"][:, sl] + _dot_tn(do, qb))
            ohs.append(oh)
        oh, d_qt, d_qb, d_kt, d_kh, dv, d_bl = (jnp.concatenate(p, axis=1) for p in
                                                (ohs, d_qts, d_qbs, d_kts, d_khs, dvs, d_bls))
        dxg = d_ob * (oh * ng) * (sgg * (1.0 + xg * (1.0 - sgg)))
        dng_ref[...] += jnp.sum(d_on * oh, axis=0, keepdims=True)
        dq = d_qt * gt["eqm"] + d_qb * gt["eb"]
        db = d_qt * gt["qt"] + d_qb * gt["qb"] - d_kt * gt["kt"] - d_kh * gt["kh"]
        rowi = lax.broadcasted_iota(jnp.int32, (C, HK), 0)
        db = db + jnp.where(rowi == C - 1, d_bl, 0.0)
        dkk = d_kt * gt["ekm"] + d_kh * gt["ekl"]
        dlg = _dot_tri(_tri(C, False).astype(F32), db)
        df = dlg / gt["f"] - dkk
        sg, sq = gt["sg"], gt["sq"]
        dlb_ref[...] += jnp.sum(df * (1.0 - sg), axis=0, keepdims=True)
        dx_ref[rs, HK:2 * HK] = (dq * (sq * (1.0 + xq * (1.0 - sq)))).astype(dx_ref.dtype)
        dx_ref[rs, 2 * HK:3 * HK] = (df * (1.0 - lb_all) * sg * (1.0 - sg)).astype(dx_ref.dtype)
        dx_ref[rs, 3 * HK:4 * HK] = dv.astype(dx_ref.dtype)
        dx_ref[rs, 4 * HK:5 * HK] = dxg.astype(dx_ref.dtype)
        return new_dsts

    S = HG_STEP_CHUNKS
    ns = nc // S

    def blk(cb):
        return pl.BlockSpec((S * C, HK), lambda i: (ns - 1 - i, cb))

    vec = pl.BlockSpec((1, HK), lambda i: (0, 0))
    return pl.pallas_call(
        body, grid=(ns,),
        in_specs=[blk(1), blk(2), blk(3), blk(4), vec, vec,
                  pl.BlockSpec((S, H, K, K), lambda i: (ns - 1 - i, 0, 0, 0)), blk(1), blk(0)],
        out_specs=[pl.BlockSpec((S * C, 5 * HK), lambda i: (ns - 1 - i, 0)), vec, vec],
        out_shape=[jax.ShapeDtypeStruct((L, 5 * HK), BF16), jax.ShapeDtypeStruct((1, HK), F32),
                   jax.ShapeDtypeStruct((1, HK), F32)],
        scratch_shapes=[pltpu.VMEM((H, K, K), F32)],
        compiler_params=_params(dimension_semantics=("arbitrary",)), name=name,
    )(proj, proj, proj, proj, lb, norm_g, sall, dcat, du)


def _shift_down(x, k, row):
    return jnp.where(row >= k, pltpu.roll(x, k, 0), 0.0)


def _shift_up(x, k, row):
    n = x.shape[0]
    return jnp.where(row < n - k, pltpu.roll(x, n - k, 0), 0.0)


def convgate_fwd(hu, conv_w, conv_b, *, name):
    L, C2 = hu.shape
    C = C2 // 2
    tc = _pick(C, (256, 128))
    nb = C // tc

    def body(a_ref, b_ref, wa_ref, wb_ref, ba_ref, bb_ref, o_ref):
        row = lax.broadcasted_iota(jnp.int32, (L, tc), 0)

        def conv(x, w, bias):
            return w[2:3, :] * x + w[1:2, :] * _shift_down(x, 1, row) + w[0:1, :] * _shift_down(x, 2, row) + bias

        ca = conv(a_ref[...], wa_ref[...], ba_ref[...])
        cb = conv(b_ref[...], wb_ref[...], bb_ref[...])
        o_ref[...] = (ca * _sigmoid(ca) * cb).astype(o_ref.dtype)

    def col(off, rows):
        return pl.BlockSpec((rows, tc), lambda j: (0, j + off))

    return pl.pallas_call(
        body, grid=(nb,), in_specs=[col(0, L), col(nb, L), col(0, 3), col(nb, 3), col(0, 1), col(nb, 1)],
        out_specs=col(0, L), out_shape=jax.ShapeDtypeStruct((L, C), BF16),
        compiler_params=_params(dimension_semantics=("parallel",)), name=name,
    )(hu, hu, conv_w, conv_w, conv_b, conv_b)


def convgate_bwd(hu, conv_w, conv_b, dact, *, name):
    L, C2 = hu.shape
    C = C2 // 2
    tc = _pick(C, (256, 128))
    nb = C // tc

    def body(a_ref, b_ref, wa_ref, wb_ref, ba_ref, bb_ref, d_ref, dxa_ref, dxb_ref, dwa_ref, dwb_ref, dba_ref, dbb_ref):
        row = lax.broadcasted_iota(jnp.int32, (L, tc), 0)

        def conv(x, w, bias):
            x1 = _shift_down(x, 1, row)
            x2 = _shift_down(x, 2, row)
            return w[2:3, :] * x + w[1:2, :] * x1 + w[0:1, :] * x2 + bias, x1, x2

        xa, xb = a_ref[...], b_ref[...]
        wa, wb = wa_ref[...], wb_ref[...]
        ca, xa1, xa2 = conv(xa, wa, ba_ref[...])
        cb, xb1, xb2 = conv(xb, wb, bb_ref[...])
        d = d_ref[...]
        sa = _sigmoid(ca)
        dca = d * cb * (sa * (1.0 + ca * (1.0 - sa)))
        dcb = d * (ca * sa)

        def back(dc, w, x, x1, x2, dx_ref, dw_ref, db_ref):
            dx = w[2:3, :] * dc + w[1:2, :] * _shift_up(dc, 1, row) + w[0:1, :] * _shift_up(dc, 2, row)
            dx_ref[...] = dx.astype(dx_ref.dtype)
            dw_ref[...] = jnp.concatenate([jnp.sum(dc * x2, axis=0, keepdims=True),
                                           jnp.sum(dc * x1, axis=0, keepdims=True),
                                           jnp.sum(dc * x, axis=0, keepdims=True)], axis=0)
            db_ref[...] = jnp.sum(dc, axis=0, keepdims=True)

        back(dca, wa, xa, xa1, xa2, dxa_ref, dwa_ref, dba_ref)
        back(dcb, wb, xb, xb1, xb2, dxb_ref, dwb_ref, dbb_ref)

    def col(off, rows):
        return pl.BlockSpec((rows, tc), lambda j: (0, j + off))

    outs = pl.pallas_call(
        body, grid=(nb,),
        in_specs=[col(0, L), col(nb, L), col(0, 3), col(nb, 3), col(0, 1), col(nb, 1), col(0, L)],
        out_specs=[col(0, L), col(0, L), col(0, 3), col(0, 3), col(0, 1), col(0, 1)],
        out_shape=[jax.ShapeDtypeStruct((L, C), BF16)] * 2 + [jax.ShapeDtypeStruct((3, C), F32)] * 2
        + [jax.ShapeDtypeStruct((1, C), F32)] * 2,
        compiler_params=_params(dimension_semantics=("parallel",)), name=name,
    )(hu, hu, conv_w, conv_w, conv_b, conv_b, dact)
    dxa, dxb, dwa, dwb, dba, dbb = outs
    return (jnp.concatenate([dxa, dxb], axis=1), jnp.concatenate([dwa, dwb], axis=1),
            jnp.concatenate([dba, dbb], axis=1))


def rope_tables(positions):
    half = ROT_DIM // 2
    inv_freq = ROPE_THETA ** (-jnp.arange(half, dtype=F32) * 2.0 / ROT_DIM)
    ang = positions.astype(F32)[:, None] * inv_freq
    cos, sin = jnp.cos(ang), jnp.sin(ang)
    L = positions.shape[0]
    one = jnp.ones((L, ATT_E - ROT_DIM), F32)
    zero = jnp.zeros((L, ATT_E - ROT_DIM), F32)
    zh = jnp.zeros((L, half), F32)
    tc = jnp.concatenate([cos, cos, one], axis=1)
    ts1 = jnp.concatenate([zh, sin, zero], axis=1)
    ts2 = jnp.concatenate([-sin, zh, zero], axis=1)
    return tuple(jnp.concatenate([t, t], axis=1) for t in (tc, ts1, ts2))


def norm_mm(x, g, w_t, *, tabs=None, name):
    L, D = x.shape
    N = w_t.shape[0]
    W = 512
    tm = _pick(L, (1024, 512, 256, 128))
    nq = N // (3 * W)
    scale = ATT_E ** -0.5
    rope = tabs is not None

    def body(x_ref, g_ref, b_ref, *rest):
        hn_ref, o_ref, hn_scr = rest[-3:]
        j = pl.program_id(1)

        @pl.when(j == 0)
        def _():
            xv = x_ref[...]
            r = lax.rsqrt(jnp.mean(xv * xv, axis=-1, keepdims=True) + NORM_EPS)
            hn = (xv * r * g_ref[...]).astype(BF16)
            hn_scr[...] = hn
            hn_ref[...] = hn

        out = _dot_nt(hn_scr[...], b_ref[...])
        if rope:
            c_ref, s1_ref, s2_ref = rest[:3]
            c = jnp.concatenate([c_ref[...]] * 4, axis=1)
            s1 = jnp.concatenate([s1_ref[...]] * 4, axis=1)
            s2 = jnp.concatenate([s2_ref[...]] * 4, axis=1)
            rot = out * c + pltpu.roll(out, 8, 1) * s1 + pltpu.roll(out, W - 8, 1) * s2
            out = jnp.where(j < 2 * nq, rot * jnp.where(j < nq, scale, 1.0), out)
        o_ref[...] = out

    row = pl.BlockSpec((tm, D), lambda i, j: (i, 0))
    tab = pl.BlockSpec((tm, 128), lambda i, j: (i, 0))
    return pl.pallas_call(body, grid=(L // tm, N // W),
                          in_specs=[row, pl.BlockSpec((1, D), lambda i, j: (0, 0)), pl.BlockSpec((W, D), lambda i, j: (j, 0))]
                          + ([tab, tab, tab] if rope else []),
                          out_specs=[row, pl.BlockSpec((tm, W), lambda i, j: (i, j))],
                          out_shape=[jax.ShapeDtypeStruct((L, D), BF16), jax.ShapeDtypeStruct((L, N), F32)],
                          scratch_shapes=[pltpu.VMEM((tm, D), BF16)],
                          compiler_params=_params(dimension_semantics=("parallel", "arbitrary")), name=name)(
        x, g.reshape(1, D), w_t, *(tabs or ()))


def rope_bwd(slabs, tabs, *, name):
    L, W = slabs[0].shape
    tr = _pick(L, (256, 128))
    nq = len(slabs) // 3
    scale = ATT_E ** -0.5

    def body(*refs):
        d_refs, (c_ref, s1_ref, s2_ref, o_ref) = refs[:3 * nq], refs[3 * nq:]
        c = jnp.concatenate([c_ref[...]] * 4, axis=1)
        s1 = jnp.concatenate([s1_ref[...]] * 4, axis=1)
        s2 = jnp.concatenate([s2_ref[...]] * 4, axis=1)
        for j, d_ref in enumerate(d_refs):
            dy = d_ref[...]
            if j < 2 * nq:
                dy = dy * c + pltpu.roll(dy * s1, W - 8, 1) + pltpu.roll(dy * s2, 8, 1)
            if j < nq:
                dy = dy * scale
            o_ref[:, j * W:(j + 1) * W] = dy.astype(o_ref.dtype)

    slab = pl.BlockSpec((tr, W), lambda i: (i, 0))
    tab = pl.BlockSpec((tr, 128), lambda i: (i, 0))
    return pl.pallas_call(body, grid=(L // tr,), in_specs=[slab] * (3 * nq) + [tab, tab, tab],
                          out_specs=pl.BlockSpec((tr, 3 * nq * W), lambda i: (i, 0)),
                          out_shape=jax.ShapeDtypeStruct((L, 3 * nq * W), BF16),
                          compiler_params=_params(dimension_semantics=("parallel",)), name=name)(*slabs, *tabs)


def _att_masks(has_prev):
    qi = lax.broadcasted_iota(jnp.int32, (ATT_BLOCK, ATT_BLOCK), 0)
    kj = lax.broadcasted_iota(jnp.int32, (ATT_BLOCK, ATT_BLOCK), 1)
    return qi >= kj, (kj >= qi) & has_prev


ATT_COLS = 128


def _att_rows(j, d, nb):
    B = ATT_BLOCK
    r, n = j // nb, j % nb
    start = r + d * B * n
    has_prev = n > 0
    pstart = jnp.where(has_prev, start - d * B, start)
    if d == 1:
        return pl.ds(pl.multiple_of(start, B), B), pl.ds(pl.multiple_of(pstart, B), B), has_prev
    return pl.ds(start, B, stride=d), pl.ds(pstart, B, stride=d), has_prev


def _qkv_specs(L, g):
    per = ATT_HPG * ATT_E // ATT_COLS
    third = len(ATT_DILATIONS) * per
    return [pl.BlockSpec((L, ATT_COLS), lambda c, base=base: (0, base + c))
            for base in (g * per, third + g * per, 2 * third + g * per)]


def attn_fwd(qkv, g, d, *, name):
    L, W = qkv.shape[0], ATT_HPG * ATT_E
    B, E = ATT_BLOCK, ATT_E
    nblk = L // B
    nb = nblk // d

    def body(q_ref, k_ref, v_ref, o_ref, l_ref):
        def step(j, carry):
            cur, prv, has_prev = _att_rows(j, d, nb)
            mc, mp = _att_masks(has_prev)
            qb, kc, kp, vc, vp = q_ref[cur, :], k_ref[cur, :], k_ref[prv, :], v_ref[cur, :], v_ref[prv, :]
            outs, lses = [], []
            for h in range(ATT_COLS // E):
                sl = slice(h * E, (h + 1) * E)
                sc = jnp.where(mc, _dot_nt(qb[:, sl], kc[:, sl]), NEG_BIG)
                sp = jnp.where(mp, _dot_nt(qb[:, sl], kp[:, sl]), NEG_BIG)
                m = jnp.maximum(jnp.max(sc, axis=-1, keepdims=True), jnp.max(sp, axis=-1, keepdims=True))
                pc = jnp.exp(sc - m)
                pp = jnp.exp(sp - m)
                den = jnp.sum(pc, axis=-1, keepdims=True) + jnp.sum(pp, axis=-1, keepdims=True)
                outs.append((_dot(pc, vc[:, sl]) + _dot(pp, vp[:, sl])) / den)
                lses.append(jnp.broadcast_to(m + jnp.log(den), (B, E)))
            o_ref[cur, :] = jnp.concatenate(outs, axis=1)
            l_ref[cur, :] = jnp.concatenate(lses, axis=1)
            return carry

        lax.fori_loop(0, nblk, step, 0, unroll=4)

    col = pl.BlockSpec((L, ATT_COLS), lambda c: (0, c))
    return pl.pallas_call(body, grid=(W // ATT_COLS,), in_specs=_qkv_specs(L, g), out_specs=[col] * 2,
                          out_shape=[jax.ShapeDtypeStruct((L, W), F32)] * 2,
                          compiler_params=_params(dimension_semantics=("parallel",)), name=name)(qkv, qkv, qkv)


def attn_bwd(qkv, g, lse, do, dl, d, *, name):
    L, W = qkv.shape[0], ATT_HPG * ATT_E
    B, E = ATT_BLOCK, ATT_E
    nblk = L // B
    nb = nblk // d

    def body(q_ref, k_ref, v_ref, l_ref, do_ref, dl_ref, dq_ref, dk_ref, dv_ref):
        dk_ref[...] = jnp.zeros_like(dk_ref)
        dv_ref[...] = jnp.zeros_like(dv_ref)

        def step(j, carry):
            cur, prv, has_prev = _att_rows(j, d, nb)
            mc, mp = _att_masks(has_prev)
            qb, kc, kp, vc, vp = q_ref[cur, :], k_ref[cur, :], k_ref[prv, :], v_ref[cur, :], v_ref[prv, :]
            lb, dob, dlb = l_ref[cur, :], do_ref[cur, :], dl_ref[cur, :]
            dqs, dkc, dkp, dvc, dvp = [], [], [], [], []
            for h in range(ATT_COLS // E):
                sl = slice(h * E, (h + 1) * E)
                qh, doh = qb[:, sl], dob[:, sl]
                lse_h, dl_h = lb[:, h * E:h * E + 1], dlb[:, h * E:h * E + 1]
                pc = jnp.where(mc, jnp.exp(_dot_nt(qh, kc[:, sl]) - lse_h), 0.0)
                pp = jnp.where(mp, jnp.exp(_dot_nt(qh, kp[:, sl]) - lse_h), 0.0)
                dsc = pc * (_dot_nt(doh, vc[:, sl]) - dl_h)
                dsp = pp * (_dot_nt(doh, vp[:, sl]) - dl_h)
                dqs.append(_dot(dsc, kc[:, sl]) + _dot(dsp, kp[:, sl]))
                dkc.append(_dot_tn(dsc, qh))
                dkp.append(_dot_tn(dsp, qh))
                dvc.append(_dot_tn(pc, doh))
                dvp.append(_dot_tn(pp, doh))
            dq_ref[cur, :] = jnp.concatenate(dqs, axis=1)
            dk_ref[cur, :] = dk_ref[cur, :] + jnp.concatenate(dkc, axis=1)
            dv_ref[cur, :] = dv_ref[cur, :] + jnp.concatenate(dvc, axis=1)
            dk_ref[prv, :] = dk_ref[prv, :] + jnp.concatenate(dkp, axis=1)
            dv_ref[prv, :] = dv_ref[prv, :] + jnp.concatenate(dvp, axis=1)
            return carry

        lax.fori_loop(0, nblk, step, 0, unroll=4)

    col = pl.BlockSpec((L, ATT_COLS), lambda c: (0, c))
    return pl.pallas_call(body, grid=(W // ATT_COLS,), in_specs=_qkv_specs(L, g) + [col] * 3, out_specs=[col] * 3,
                          out_shape=[jax.ShapeDtypeStruct((L, W), F32)] * 3,
                          compiler_params=_params(dimension_semantics=("parallel",)), name=name)(
        qkv, qkv, qkv, lse, do, dl)


def _merge_alpha(l_refs):
    ls = [r[...] for r in l_refs]
    m = jnp.maximum(jnp.maximum(ls[0], ls[1]), ls[2])
    es = [jnp.exp(l - m) for l in ls]
    den = es[0] + es[1] + es[2]
    return [e / den for e in es]


def merge_fwd(os_, ls_, *, name):
    L, W = os_[0].shape
    tr = _pick(L, (256, 128))

    def body(o0, o1, o2, l0, l1, l2, out_ref):
        al = _merge_alpha((l0, l1, l2))
        out_ref[...] = (al[0] * o0[...] + al[1] * o1[...] + al[2] * o2[...]).astype(out_ref.dtype)

    row = pl.BlockSpec((tr, W), lambda i: (i, 0))
    return pl.pallas_call(body, grid=(L // tr,), in_specs=[row] * 6, out_specs=row,
                          out_shape=jax.ShapeDtypeStruct((L, W), BF16), name=name)(*os_, *ls_)


def merge_bwd(os_, ls_, do, *, name):
    L, W = do.shape
    tr = _pick(L, (256, 128))

    def body(o0, o1, o2, l0, l1, l2, do_ref, d0, d1, d2, e0, e1, e2):
        al = _merge_alpha((l0, l1, l2))
        dov = do_ref[...]
        r = lax.broadcasted_iota(jnp.int32, (W, W), 0) // ATT_E
        c = lax.broadcasted_iota(jnp.int32, (W, W), 1) // ATT_E
        ones_blk = (r == c).astype(F32)
        t = jnp.zeros_like(dov)
        for a, o in zip(al, (o0, o1, o2)):
            t = t + a * _dot_tri(ones_blk, dov * o[...], tri_left=False)
        for a, d_ref, e_ref in zip(al, (d0, d1, d2), (e0, e1, e2)):
            d_ref[...] = a * dov
            e_ref[...] = a * t

    row = pl.BlockSpec((tr, W), lambda i: (i, 0))
    return pl.pallas_call(body, grid=(L // tr,), in_specs=[row] * 7, out_specs=[row] * 6,
                          out_shape=[jax.ShapeDtypeStruct((L, W), F32)] * 6, name=name)(*os_, *ls_, do)


def _me_and_peers():
    x, y, c = lax.axis_index("x"), lax.axis_index("y"), lax.axis_index("c")
    peers = []
    for k in range(1, N_DEV):
        px = 1 - x if k & 4 else x
        py = 1 - y if k & 2 else y
        pc = 1 - c if k & 1 else c
        peers.append((px, py, pc))
    return (x, y, c), peers


def _index(dev):
    return 4 * dev[0] + 2 * dev[1] + dev[2]


def _hbm(a):
    return pltpu.with_memory_space_constraint(a, pltpu.HBM)


HBM_SPEC = pl.BlockSpec(memory_space=pltpu.HBM)
SEM_SPEC = pl.BlockSpec(memory_space=pltpu.SEMAPHORE)
DATAFLOW = pltpu.SideEffectType.DATAFLOW_SIDE_EFFECTING


def _remote(src_ref, land_ref, slotted, me, peer, src_is_mine, send_sem, recv_sem, k):
    sender, receiver = (me, peer) if src_is_mine else (peer, me)
    src = src_ref.at[_index(receiver)] if slotted else src_ref
    return pltpu.make_async_remote_copy(src_ref=src, dst_ref=land_ref.at[_index(sender)], send_sem=send_sem.at[k],
                                        recv_sem=recv_sem.at[k], device_id=peer, device_id_type=MESH_ID)


SIBLING = 0
SAME_CORE = (1, 3, 5)
OTHER_CORE = (2, 4, 6)


def copies_start(arrays, mode, *, name):
    n = len(arrays)
    slotted = mode == "exchange"
    lands = [lax.empty(a.shape if slotted else (N_DEV,) + a.shape, a.dtype) for a in arrays]
    targets = (SIBLING,) + SAME_CORE if mode == "gather2" else tuple(range(N_DEV - 1))

    def body(*refs):
        x_refs, land_refs = refs[:n], refs[n:2 * n]
        send, recv = refs[2 * n:3 * n], refs[3 * n:4 * n]
        token = refs[-1]
        me, peers = _me_and_peers()
        for w in range(n):
            for k in targets:
                _remote(x_refs[w], land_refs[w], slotted, me, peers[k], True, send[w], recv[w], k).start()
            if not slotted:
                pltpu.make_async_copy(x_refs[w], land_refs[w].at[_index(me)], recv[w].at[N_DEV - 1]).start()
        token[...] = jnp.zeros_like(token)

    sem = pltpu.SemaphoreType.DMA((N_DEV,))
    out_shape = ([sem] * (2 * n) + [pltpu.HBM(a.shape, a.dtype) for a in arrays]
                 + [pltpu.HBM(l.shape, l.dtype) for l in lands] + [jax.ShapeDtypeStruct((8, 128), F32)])
    outs = pl.pallas_call(
        body, name=name, out_shape=out_shape, in_specs=[HBM_SPEC] * (2 * n),
        out_specs=[SEM_SPEC] * (2 * n) + [HBM_SPEC] * (2 * n) + [pl.BlockSpec(memory_space=pltpu.VMEM)],
        input_output_aliases={i: 2 * n + i for i in range(2 * n)},
        compiler_params=pltpu.CompilerParams(has_side_effects=DATAFLOW),
    )(*[_hbm(a) for a in arrays], *[_hbm(l) for l in lands])
    handles = [(outs[w], outs[n + w], outs[2 * n + w], outs[3 * n + w]) for w in range(n)]
    return handles, outs[-1]


def _forward(land_ref, me, peers, j, fsend, frecv, mine):
    block = _index(peers[SAME_CORE[j]] if mine else peers[OTHER_CORE[j]])
    return pltpu.make_async_remote_copy(src_ref=land_ref.at[block], dst_ref=land_ref.at[block], send_sem=fsend.at[j],
                                        recv_sem=frecv.at[j], device_id=peers[SIBLING], device_id_type=MESH_ID)


def copies_forward(handles, after, *, name):
    n = len(handles)

    def body(*refs):
        land_refs, recv = refs[:n], refs[n:2 * n]
        fsend, frecv = refs[2 * n + 1:3 * n + 1], refs[3 * n + 1:4 * n + 1]
        token = refs[-1]
        me, peers = _me_and_peers()
        for w in range(n):
            for j, k in enumerate(SAME_CORE):
                block = land_refs[w].at[_index(peers[k])]
                pltpu.make_async_remote_copy(src_ref=block, dst_ref=block, send_sem=recv[w].at[N_DEV - 1],
                                             recv_sem=recv[w].at[k], device_id=peers[k], device_id_type=MESH_ID).wait_recv()
                _forward(land_refs[w], me, peers, j, fsend[w], frecv[w], True).start()
        token[...] = jnp.zeros_like(token)

    sem = pltpu.SemaphoreType.DMA((len(SAME_CORE),))
    lands = [h[3] for h in handles]
    outs = pl.pallas_call(
        body, name=name,
        out_shape=[sem] * (2 * n) + [pltpu.HBM(l.shape, l.dtype) for l in lands] + [jax.ShapeDtypeStruct((8, 128), F32)],
        in_specs=[HBM_SPEC] * n + [SEM_SPEC] * n + [pl.BlockSpec(memory_space=pl.ANY)],
        out_specs=[SEM_SPEC] * (2 * n) + [HBM_SPEC] * n + [pl.BlockSpec(memory_space=pltpu.VMEM)],
        input_output_aliases={w: 2 * n + w for w in range(n)},
        compiler_params=pltpu.CompilerParams(has_side_effects=DATAFLOW),
    )(*lands, *[h[1] for h in handles], after)
    new = [(h[0], h[1], h[2], outs[2 * n + w], outs[w], outs[n + w]) for w, h in enumerate(handles)]
    return new, outs[-1]


def copies_wait(handle, mode, after, *, name):
    slotted = mode == "exchange"
    two_level = mode == "gather2"
    send_sem, recv_sem, x_thru, land_thru = handle[:4]
    targets = (SIBLING,) + SAME_CORE if two_level else tuple(range(N_DEV - 1))
    arrivals = (SIBLING,) if two_level else targets

    def body(x_ref, land_ref, send_ref, recv_ref, *rest):
        me, peers = _me_and_peers()
        for k in targets:
            _remote(x_ref, land_ref, slotted, me, peers[k], True, send_ref, recv_ref, k).wait_send()
        for k in arrivals:
            _remote(x_ref, land_ref, slotted, me, peers[k], False, send_ref, recv_ref, k).wait_recv()
        if not slotted:
            pltpu.make_async_copy(x_ref, land_ref.at[_index(me)], recv_ref.at[N_DEV - 1]).wait()
        if two_level:
            fsend, frecv = rest[0], rest[1]
            for j in range(len(SAME_CORE)):
                _forward(land_ref, me, peers, j, fsend, frecv, True).wait_send()
                _forward(land_ref, me, peers, j, fsend, frecv, False).wait_recv()

    extra = list(handle[4:])
    return pl.pallas_call(
        body, name=name, out_shape=(pltpu.HBM(x_thru.shape, x_thru.dtype), pltpu.HBM(land_thru.shape, land_thru.dtype)),
        in_specs=[HBM_SPEC, HBM_SPEC, SEM_SPEC, SEM_SPEC] + [SEM_SPEC] * len(extra) + [pl.BlockSpec(memory_space=pl.ANY)],
        out_specs=(HBM_SPEC, HBM_SPEC), input_output_aliases={0: 0, 1: 1},
        compiler_params=pltpu.CompilerParams(has_side_effects=DATAFLOW),
    )(x_thru, land_thru, send_sem, recv_sem, *extra, after)


def cast_bf16(x, *, dep=None, name):
    R, C = x.shape
    tr = _pick(R, (512, 352, 256, 128, 64))
    deps = [] if dep is None else [dep]

    def body(x_ref, *rest):
        rest[-1][...] = x_ref[...].astype(BF16)

    row = pl.BlockSpec((tr, C), lambda i: (i, 0))
    return pl.pallas_call(body, grid=(R // tr,), in_specs=[row] + [pl.BlockSpec((8, 128), lambda i: (0, 0))] * len(deps),
                          out_specs=row, out_shape=jax.ShapeDtypeStruct((R, C), BF16), name=name)(x, *deps)


def cast_bf16_layer(x3, layer, *, name):
    _, R, C = x3.shape
    tr = _pick(R, (512, 352, 256, 128, 64))

    def body(x_ref, o_ref):
        o_ref[...] = x_ref[...].astype(BF16)

    return pl.pallas_call(body, grid=(R // tr,), in_specs=[pl.BlockSpec((None, tr, C), lambda i: (layer, i, 0))],
                          out_specs=pl.BlockSpec((tr, C), lambda i: (i, 0)),
                          out_shape=jax.ShapeDtypeStruct((R, C), BF16), name=name)(x3)


BD_PARTS = 4


def _blockdiag_call(b, build, G, r, c, name):
    gp = G // BD_PARTS

    def body_build(b_ref, o_ref):
        o_ref[...] = jnp.zeros_like(o_ref)
        for g in range(G):
            o_ref[g // gp, (g % gp) * r:(g % gp + 1) * r, (g % gp) * c:(g % gp + 1) * c] = b_ref[g]

    def body_extract(d_ref, o_ref):
        for g in range(G):
            o_ref[g] = d_ref[g // gp, (g % gp) * r:(g % gp + 1) * r, (g % gp) * c:(g % gp + 1) * c]

    out = jax.ShapeDtypeStruct((BD_PARTS, gp * r, gp * c) if build else (G, r, c), F32)
    return pl.pallas_call(body_build if build else body_extract, out_shape=out, name=name)(b)


def make_blockdiag(G, r, c, name):
    @jax.custom_vjp
    def blockdiag(b):
        return _blockdiag_call(b, True, G, r, c, name + "_build")

    def fwd(b):
        return blockdiag(b), None

    def bwd(_, g):
        return (_blockdiag_call(g, False, G, r, c, name + "_extract"),)

    blockdiag.defvjp(fwd, bwd)
    return blockdiag


def _my_index():
    return 4 * lax.axis_index("x") + 2 * lax.axis_index("y") + lax.axis_index("c")


def cols_from_shards(g, *, name):
    _, K, n = g.shape
    tk = _pick(K, (256, 128))

    def body(g_ref, o_ref):
        for i in range(N_DEV):
            o_ref[:, i * n:(i + 1) * n] = g_ref[i]

    return pl.pallas_call(body, grid=(K // tk,), in_specs=[pl.BlockSpec((N_DEV, tk, n), lambda i: (0, i, 0))],
                          out_specs=pl.BlockSpec((tk, N_DEV * n), lambda i: (i, 0)),
                          out_shape=jax.ShapeDtypeStruct((K, N_DEV * n), g.dtype), name=name)(g)


def shards_from_cols(w, *, name):
    K, N = w.shape
    n = N // N_DEV
    tk = _pick(K, (256, 128))

    def body(w_ref, o_ref):
        for i in range(N_DEV):
            o_ref[i] = w_ref[:, i * n:(i + 1) * n].astype(o_ref.dtype)

    return pl.pallas_call(body, grid=(K // tk,), in_specs=[pl.BlockSpec((tk, N), lambda i: (i, 0))],
                          out_specs=pl.BlockSpec((N_DEV, tk, n), lambda i: (0, i, 0)),
                          out_shape=jax.ShapeDtypeStruct((N_DEV, K, n), BF16), name=name)(w)


def _adamw(w, g, m, v):
    m = ADAM_B1 * m + (1.0 - ADAM_B1) * g
    v = ADAM_B2 * v + (1.0 - ADAM_B2) * (g * g)
    m_hat = m / (1.0 - ADAM_B1 ** ADAM_STEP)
    v_hat = v / (1.0 - ADAM_B2 ** ADAM_STEP)
    delta = -ADAM_LR * (m_hat / (jnp.sqrt(v_hat) + ADAM_EPS) + ADAM_WD * w)
    return delta, m, v


def reduce_adamw(recv, own, own_slotted, me, w, m, v, *, layer=0, n_layers=1, into=None, name):
    _, R, C = recv.shape
    tr = _pick(R, (352, 320, 288, 256, 128, 64, 32, 16, 8))
    off = layer * (R // tr)

    def body(me_ref, r_ref, own_ref, w_ref, m_ref, v_ref, *rest):
        g_ref, d_ref, nm_ref, nv_ref = rest[-4:]
        mine = me_ref[0]
        g = None
        for i in range(N_DEV):
            part = jnp.where(mine == i, own_ref[...], r_ref[i]).astype(F32)
            g = part if g is None else g + part
        delta, nm, nv = _adamw(w_ref[...], g, m_ref[...], v_ref[...])
        g_ref[...] = g
        d_ref[...] = delta
        nm_ref[...] = nm
        nv_ref[...] = nv

    row = pl.BlockSpec((tr, C), lambda i, me_ref: (i + off, 0))
    own_spec = (pl.BlockSpec((None, tr, C), lambda i, me_ref: (me_ref[0], i, 0)) if own_slotted
                else pl.BlockSpec((tr, C), lambda i, me_ref: (i, 0)))
    rest = [] if into is None else list(into)
    grid_spec = pltpu.PrefetchScalarGridSpec(
        num_scalar_prefetch=1, grid=(R // tr,),
        in_specs=[pl.BlockSpec((N_DEV, tr, C), lambda i, me_ref: (0, i, 0)), own_spec, row, row, row]
        + [pl.BlockSpec(memory_space=pl.ANY)] * len(rest),
        out_specs=[row] * 4)
    return pl.pallas_call(body, grid_spec=grid_spec, out_shape=[jax.ShapeDtypeStruct((n_layers * R, C), F32)] * 4,
                          input_output_aliases={6 + k: k for k in range(len(rest))},
                          compiler_params=_params(dimension_semantics=("parallel",)), name=name)(
        me.reshape(1).astype(jnp.int32), recv, own, w, m, v, *rest)


def _s5_prepare(A_re, A_im, log_dt, B_re, B_im, C_re, C_im):
    G, P, Cg = S5_GROUPS, S5_STATE, S5_GROUP
    dt = jnp.exp(log_dt)[:, None]
    mag = jnp.exp(A_re * dt)
    ab_re = mag * jnp.cos(A_im * dt)
    ab_im = mag * jnp.sin(A_im * dt)
    den = A_re * A_re + A_im * A_im
    nr, ni = ab_re - 1.0, ab_im
    c_re = (nr * A_re + ni * A_im) / den
    c_im = (ni * A_re - nr * A_im) / den
    Bb_re = c_re[..., None] * B_re - c_im[..., None] * B_im
    Bb_im = c_re[..., None] * B_im + c_im[..., None] * B_re
    def dense_in(b, name):
        return make_blockdiag(G, Cg, P, name)(b.transpose(0, 2, 1))

    def dense_out(c, name):
        return make_blockdiag(G, P, Cg, name)(c.transpose(0, 2, 1))

    return (ab_re.reshape(1, G * P), ab_im.reshape(1, G * P), dense_in(Bb_re, "s5_wb_re"), dense_in(Bb_im, "s5_wb_im"),
            dense_out(C_re, "s5_wc_re"), dense_out(-C_im, "s5_wc_im"))


def _lower_bound(gamma):
    return jnp.cumsum(jax.nn.softmax(gamma, axis=0), axis=0)[0:1]


def _ffn_fwd(h, g_norm, get_w_in, conv_w, conv_b, get_w_out, tag):
    w_in = get_w_in(h)
    hn, hu = norm_mm(h, g_norm, w_in, name=tag + "_in")
    act = convgate_fwd(hu, conv_w, conv_b, name=tag + "_gate")
    w_out = get_w_out(act)
    h_out = mm(act, w_out, res=h, name=tag + "_out")
    return h_out, (hn, hu, act), w_in, w_out


def _ffn_bwd(h, g_norm, w_in, conv_w, conv_b, w_out, saved, dh, tag, send_dw_in, send_dw_out):
    hn, hu, act = saved
    sent = send_dw_out(mm(act, dh, ta=True, out_dtype=BF16, name=tag + "_dwout"))
    dact = mm(dh, w_out, tb=True, dep=sent, name=tag + "_dact")
    dhu, dconv_w, dconv_b = convgate_bwd(hu, conv_w, conv_b, dact, name=tag + "_dgate")
    sent = send_dw_in(mm(dhu, hn, ta=True, out_dtype=BF16, name=tag + "_dwin"))
    dh_in, dg = mm_drms(dhu, w_in, h, g_norm, dh, dep=sent, name=tag + "_dhn")
    return dh_in, dg, dconv_w, dconv_b


def kernel(x, positions, norm_mix, norm_ffn, norm_final, mix_w_in, mix_w_out, s5_A_re, s5_A_im, s5_log_dt, s5_B_re, s5_B_im, s5_C_re, s5_C_im, s5_D, s5_glu_w, s5_glu_b, hgrn_gamma, hgrn_norm, att_w_qkv, att_w_o, ffn_w_in, ffn_conv_w, ffn_conv_b, ffn_w_out, loss_target, m_norm_mix, m_norm_ffn, m_norm_final, m_mix_w_in, m_mix_w_out, m_s5_A_re, m_s5_A_im, m_s5_log_dt, m_s5_B_re, m_s5_B_im, m_s5_C_re, m_s5_C_im, m_s5_D, m_s5_glu_w, m_s5_glu_b, m_hgrn_gamma, m_hgrn_norm, m_att_w_qkv, m_att_w_o, m_ffn_w_in, m_ffn_conv_w, m_ffn_conv_b, m_ffn_w_out, v_norm_mix, v_norm_ffn, v_norm_final, v_mix_w_in, v_mix_w_out, v_s5_A_re, v_s5_A_im, v_s5_log_dt, v_s5_B_re, v_s5_B_im, v_s5_C_re, v_s5_C_im, v_s5_D, v_s5_glu_w, v_s5_glu_b, v_hgrn_gamma, v_hgrn_norm, v_att_w_qkv, v_att_w_o, v_ffn_w_in, v_ffn_conv_w, v_ffn_conv_b, v_ffn_w_out):
    W = dict(norm_mix=norm_mix, norm_ffn=norm_ffn, norm_final=norm_final, mix_w_in=mix_w_in, mix_w_out=mix_w_out,
             s5_A_re=s5_A_re, s5_A_im=s5_A_im, s5_log_dt=s5_log_dt, s5_B_re=s5_B_re, s5_B_im=s5_B_im,
             s5_C_re=s5_C_re, s5_C_im=s5_C_im, s5_D=s5_D, s5_glu_w=s5_glu_w, s5_glu_b=s5_glu_b,
             hgrn_gamma=hgrn_gamma, hgrn_norm=hgrn_norm, att_w_qkv=att_w_qkv, att_w_o=att_w_o, ffn_w_in=ffn_w_in,
             ffn_conv_w=ffn_conv_w, ffn_conv_b=ffn_conv_b, ffn_w_out=ffn_w_out)
    M = dict(norm_mix=m_norm_mix, norm_ffn=m_norm_ffn, norm_final=m_norm_final, mix_w_in=m_mix_w_in,
             mix_w_out=m_mix_w_out, s5_A_re=m_s5_A_re, s5_A_im=m_s5_A_im, s5_log_dt=m_s5_log_dt, s5_B_re=m_s5_B_re,
             s5_B_im=m_s5_B_im, s5_C_re=m_s5_C_re, s5_C_im=m_s5_C_im, s5_D=m_s5_D, s5_glu_w=m_s5_glu_w,
             s5_glu_b=m_s5_glu_b, hgrn_gamma=m_hgrn_gamma, hgrn_norm=m_hgrn_norm, att_w_qkv=m_att_w_qkv,
             att_w_o=m_att_w_o, ffn_w_in=m_ffn_w_in, ffn_conv_w=m_ffn_conv_w, ffn_conv_b=m_ffn_conv_b,
             ffn_w_out=m_ffn_w_out)
    V = dict(norm_mix=v_norm_mix, norm_ffn=v_norm_ffn, norm_final=v_norm_final, mix_w_in=v_mix_w_in,
             mix_w_out=v_mix_w_out, s5_A_re=v_s5_A_re, s5_A_im=v_s5_A_im, s5_log_dt=v_s5_log_dt, s5_B_re=v_s5_B_re,
             s5_B_im=v_s5_B_im, s5_C_re=v_s5_C_re, s5_C_im=v_s5_C_im, s5_D=v_s5_D, s5_glu_w=v_s5_glu_w,
             s5_glu_b=v_s5_glu_b, hgrn_gamma=v_hgrn_gamma, hgrn_norm=v_hgrn_norm, att_w_qkv=v_att_w_qkv,
             att_w_o=v_att_w_o, ffn_w_in=v_ffn_w_in, ffn_conv_w=v_ffn_conv_w, ffn_conv_b=v_ffn_conv_b,
             ffn_w_out=v_ffn_w_out)
    return _step(x[0], positions[0], loss_target[0], W, M, V)


TRANSPOSED = ("mix_w_in", "att_w_qkv", "ffn_w_in")
SMALL = ("norm_mix", "norm_ffn", "norm_final", "s5_A_re", "s5_A_im", "s5_log_dt", "s5_B_re", "s5_B_im", "s5_C_re",
         "s5_C_im", "s5_D", "s5_glu_b", "hgrn_gamma", "hgrn_norm", "ffn_conv_b")
ORDER = ("norm_mix", "norm_ffn", "norm_final", "mix_w_in", "mix_w_out", "s5_A_re", "s5_A_im", "s5_log_dt", "s5_B_re",
         "s5_B_im", "s5_C_re", "s5_C_im", "s5_D", "s5_glu_w", "s5_glu_b", "hgrn_gamma", "hgrn_norm", "att_w_qkv",
         "att_w_o", "ffn_w_in", "ffn_conv_w", "ffn_conv_b", "ffn_w_out")
PACK_COLS = 1024


def _step(x, positions, target, W, M, V):
    L, D = x.shape
    me = 4 * lax.axis_index("x") + 2 * lax.axis_index("y") + lax.axis_index("c")
    n_cw = W["ffn_conv_w"].shape[-1]
    T = {n: tuple(jnp.swapaxes(d[n], -1, -2) for d in (W, M, V)) for n in TRANSPOSED}
    first = {
        "mix_w_in": cast_bf16(T["mix_w_in"][0][0], name="mix_w_in_cast"),
        "conv_w": W["ffn_conv_w"].reshape(6, n_cw),
        "s5_glu_w": cast_bf16(W["s5_glu_w"][0], name="s5_glu_w_cast"),
    }
    first_handles, token = copies_start(list(first.values()), "gather2", name="gather_start_first")
    shards = {
        "mix_w_out": cast_bf16(W["mix_w_out"][0], dep=token, name="mix_w_out_cast"),
        "ffn_w_in0": cast_bf16_layer(T["ffn_w_in"][0], 0, name="ffn_w_in0_cast"),
        "ffn_w_out0": cast_bf16_layer(W["ffn_w_out"], 0, name="ffn_w_out0_cast"),
        "att_w_qkv": cast_bf16(T["att_w_qkv"][0][0], name="att_w_qkv_cast"),
        "att_w_o": cast_bf16(W["att_w_o"][0], name="att_w_o_cast"),
        "ffn_w_in1": cast_bf16_layer(T["ffn_w_in"][0], 1, name="ffn_w_in1_cast"),
        "ffn_w_out1": cast_bf16_layer(W["ffn_w_out"], 1, name="ffn_w_out1_cast"),
    }
    gather_handles, token = copies_start(list(shards.values()), "gather2", name="gather_start")
    gather_handle = dict(zip(list(first) + list(shards), first_handles + gather_handles))

    def forward(keys, after, name):
        new, sent = copies_forward([gather_handle[k] for k in keys], after, name=name)
        gather_handle.update(zip(keys, new))
        return sent

    def gathered(key, after, cols):
        _, land = copies_wait(gather_handle[key], "gather2", after, name=key + "_gwait")
        return cols_from_shards(land, name=key + "_asm") if cols else land.reshape(-1, land.shape[-1])

    conv_b = W["ffn_conv_b"].reshape(2, 1, -1)

    s5_params = (W["s5_A_re"][0], W["s5_A_im"][0], W["s5_log_dt"][0], W["s5_B_re"][0], W["s5_B_im"][0],
                 W["s5_C_re"][0], W["s5_C_im"][0])
    (a_re, a_im, wb_re, wb_im, wc_re, wc_im), s5_prep_vjp = jax.vjp(_s5_prepare, *s5_params)
    dvec = W["s5_D"].reshape(1, S5_WIDTH)
    glu_b = W["s5_glu_b"].reshape(1, S5_WIDTH)
    lb, lb_vjp = jax.vjp(_lower_bound, W["hgrn_gamma"])
    hg_norm = W["hgrn_norm"].reshape(1, -1)
    tabs = rope_tables(positions)

    sent = forward(["mix_w_in", "conv_w", "s5_glu_w"], token, "forward_a")
    w_mix_in = gathered("mix_w_in", sent, False)
    hn0, proj = norm_mm(x, W["norm_mix"][0], w_mix_in, name="l0_proj")
    y0, xs_re, xs_im = s5_core_fwd(proj, a_re, a_im, wb_re, wb_im, wc_re, wc_im, name="s5_core")
    w_glu = gathered("s5_glu_w", y0, False)
    cat = s5_out_fwd(y0, proj, dvec, w_glu, glu_b, name="s5_out")
    cat, hg_states = hgrn_fwd(proj, lb, hg_norm, cat, name="hgrn_fwd")
    forward(["mix_w_out"], cat, "forward_b")
    w_mix_out = gathered("mix_w_out", cat, False)
    h1 = mm(cat, w_mix_out, res=x, name="l0_mix_out")
    _, cw_all = copies_wait(gather_handle["conv_w"], "gather2", h1, name="conv_w_gwait")
    conv_w = cw_all.transpose(1, 0, 2).reshape(2, 3, N_DEV * n_cw)
    w_ffn_in, w_ffn_out = [None, None], [None, None]
    h2, ffn0_saved, w_ffn_in[0], w_ffn_out[0] = _ffn_fwd(
        h1, W["norm_ffn"][0],
        lambda a: (forward(["ffn_w_in0"], a, "forward_b2"), gathered("ffn_w_in0", a, False))[1], conv_w[0], conv_b[0],
        lambda a: (forward(["ffn_w_out0"], a, "forward_c"), gathered("ffn_w_out0", a, False))[1], "ffn0")

    forward(["att_w_qkv", "att_w_o"], h2, "forward_d")
    w_qkv = gathered("att_w_qkv", h2, False)
    hn2, qkv_r = norm_mm(h2, W["norm_mix"][1], w_qkv, tabs=tabs, name="l1_qkv")
    att_o, att_l = [], []
    for g, d in enumerate(ATT_DILATIONS):
        o_g, l_g = attn_fwd(qkv_r, g, d, name=f"attn_fwd{g}")
        att_o.append(o_g)
        att_l.append(l_g)
    o_att = merge_fwd(att_o, att_l, name="merge_fwd")
    forward(["ffn_w_in1", "ffn_w_out1"], o_att, "forward_e")
    w_o = gathered("att_w_o", o_att, True)
    h3 = mm(o_att, w_o, res=h2, name="l1_mix_out")
    h4, ffn1_saved, w_ffn_in[1], w_ffn_out[1] = _ffn_fwd(
        h3, W["norm_ffn"][1], lambda a: gathered("ffn_w_in1", a, False), conv_w[1], conv_b[1],
        lambda a: gathered("ffn_w_out1", a, False), "ffn1")

    exchanges = {}

    pending = []

    def send_grad(key, g, cols, flush=True):
        if cols:
            parts = shards_from_cols(g, name=key + "_split")
        else:
            parts = g.reshape(N_DEV, g.shape[0] // N_DEV, g.shape[1])
        pending.append((key, parts))
        if not flush:
            return None
        handles, sent = copies_start([p for _, p in pending], "exchange", name=key + "_xstart")
        exchanges.update(zip([k for k, _ in pending], handles))
        pending.clear()
        return sent

    loss, dh4, dg_final = final_loss(h4, W["norm_final"], target, name="final_loss")
    dh3, dg_ffn1, dcw1, dcb1 = _ffn_bwd(h3, W["norm_ffn"][1], w_ffn_in[1], conv_w[1], conv_b[1], w_ffn_out[1],
                                        ffn1_saved, dh4, "ffn1", lambda g: send_grad("ffn_w_in1", g, False),
                                        lambda g: send_grad("ffn_w_out1", g, False, flush=False))
    sent = send_grad("att_w_o", mm(o_att, dh3, ta=True, name="l1_dwo"), True, flush=False)
    d_oatt = mm(dh3, w_o, tb=True, dep=sent, name="l1_dmix")
    mb = merge_bwd(att_o, att_l, d_oatt, name="merge_bwd")
    d_slabs = [attn_bwd(qkv_r, g, att_l[g], mb[g], mb[3 + g], d, name=f"attn_bwd{g}")
               for g, d in enumerate(ATT_DILATIONS)]
    d_qkv = rope_bwd([s[0] for s in d_slabs] + [s[1] for s in d_slabs] + [s[2] for s in d_slabs], tabs,
                     name="rope_bwd")
    sent = send_grad("att_w_qkv", mm(d_qkv, hn2, ta=True, out_dtype=BF16, name="l1_dwqkv"), False)
    dh2, dg_mix1 = mm_drms(d_qkv, w_qkv, h2, W["norm_mix"][1], dh3, dep=sent, name="l1_dhn")

    dh1, dg_ffn0, dcw0, dcb0 = _ffn_bwd(h1, W["norm_ffn"][0], w_ffn_in[0], conv_w[0], conv_b[0], w_ffn_out[0],
                                        ffn0_saved, dh2, "ffn0", lambda g: send_grad("ffn_w_in0", g, False),
                                        lambda g: send_grad("ffn_w_out0", g, False, flush=False))
    sent = send_grad("mix_w_out", mm(cat, dh1, ta=True, out_dtype=BF16, name="l0_dwout"), False)
    dcat = mm(dh1, w_mix_out, tb=True, dep=sent, name="l0_dcat")
    dy, du_d, z_bf, dzg, dglu_b, dD = s5_out_bwd(y0, proj, dvec, w_glu, glu_b, dcat, name="s5_dout")
    sent_glu = send_grad("s5_glu_w", mm(z_bf, dzg, ta=True, out_dtype=BF16, name="s5_dglu"), False, flush=False)
    du, dwb_re, dwb_im, dwc_re, dwc_im, da_re, da_im = s5_core_bwd(
        dy, du_d, proj, xs_re, xs_im, a_re, a_im, wb_re, wb_im, wc_re, wc_im, name="s5_dcore")
    s5_small = s5_prep_vjp((da_re, da_im, dwb_re, dwb_im, dwc_re, dwc_im))
    d_proj, dlb, dhg_norm = hgrn_bwd(proj, lb, hg_norm, hg_states, dcat, du, name="hgrn_bwd")
    sent = send_grad("mix_w_in", mm(d_proj, hn0, ta=True, out_dtype=BF16, dep=sent_glu, name="l0_dwin"), False)
    grad_x, dg_mix0 = mm_drms(d_proj, w_mix_in, x, W["norm_mix"][0], dh1, dep=sent, name="l0_dhn")
    (d_gamma,) = lb_vjp(dlb)
    out = {}

    dA_re, dA_im, dlog_dt, dB_re, dB_im, dC_re, dC_im = s5_small
    small_g = dict(norm_mix=jnp.concatenate([dg_mix0, dg_mix1], axis=0), norm_ffn=jnp.concatenate([dg_ffn0, dg_ffn1], axis=0),
                   norm_final=dg_final, s5_A_re=dA_re, s5_A_im=dA_im, s5_log_dt=dlog_dt, s5_B_re=dB_re, s5_B_im=dB_im,
                   s5_C_re=dC_re, s5_C_im=dC_im, s5_D=dD, s5_glu_b=dglu_b, hgrn_gamma=d_gamma, hgrn_norm=dhg_norm,
                   ffn_conv_b=jnp.concatenate([dcb0, dcb1], axis=0))
    conv_w_g = jnp.stack([dcw0, dcw1], axis=0)
    sizes = [math.prod(W[n].shape) for n in SMALL]
    n_conv = conv_w_g.size
    total = sum(sizes) + n_conv + 1
    rows = -(-total // PACK_COLS)
    rows = -(-rows // 8) * 8
    pad = rows * PACK_COLS - total

    def pack(vals, conv_part, last):
        flat = [v.reshape(-1).astype(F32) for v in vals] + [conv_part.reshape(-1), last.reshape(-1),
                                                            jnp.zeros((pad,), F32)]
        return jnp.concatenate(flat).reshape(rows, PACK_COLS)

    def conv_full(shard):
        col_owner = lax.broadcasted_iota(jnp.int32, (2, 3, N_DEV * n_cw), 2) // n_cw
        return jnp.where(col_owner == me, jnp.tile(shard, (1, 1, N_DEV)), 0.0)

    zero1 = jnp.zeros((1,), F32)
    g_pack = pack([small_g[n] for n in SMALL], conv_w_g, loss)
    w_pack = pack([W[n] for n in SMALL], conv_full(W["ffn_conv_w"]), zero1)
    m_pack = pack([M[n] for n in SMALL], conv_full(M["ffn_conv_w"]), zero1)
    v_pack = pack([V[n] for n in SMALL], conv_full(V["ffn_conv_w"]), zero1 + 1.0)
    (small_handle,), small_sent = copies_start([g_pack], "gather", name="small_xstart")

    def finish(name, n_layers):
        w3, m3, v3 = T[name] if name in TRANSPOSED else (W[name], M[name], V[name])
        res = None
        for layer in reversed(range(n_layers)):
            key = name if n_layers == 1 else f"{name}{layer}"
            own, recv = copies_wait(exchanges[key], "exchange", small_sent, name=key + "_xwait")
            _, R, Cn = recv.shape
            res = reduce_adamw(recv, own, True, me, w3.reshape(n_layers * R, Cn), m3.reshape(n_layers * R, Cn),
                               v3.reshape(n_layers * R, Cn), layer=layer, n_layers=n_layers, into=res,
                               name=key + "_adamw")
        res = [r.reshape(w3.shape) for r in res]
        return tuple(jnp.swapaxes(r, -1, -2) for r in res) if name in TRANSPOSED else tuple(res)

    for name in ("ffn_w_out", "ffn_w_in"):
        out[name] = finish(name, 2)
    for name in ("att_w_o", "att_w_qkv", "mix_w_out", "s5_glu_w", "mix_w_in"):
        out[name] = finish(name, 1)

    small_own, small_recv = copies_wait(small_handle, "gather", out["s5_glu_w"][0], name="small_xwait")
    res = reduce_adamw(small_recv, small_own, False, me, w_pack, m_pack, v_pack, name="small_adamw")
    flat = [r.reshape(-1) for r in res]
    off = 0
    for n, sz in zip(SMALL, sizes):
        out[n] = tuple(f[off:off + sz].reshape(W[n].shape) for f in flat)
        off += sz
    conv_res = [f[off:off + n_conv].reshape(2, 3, N_DEV * n_cw) for f in flat]
    out["ffn_conv_w"] = tuple(lax.dynamic_slice(c, (0, 0, me * n_cw), (2, 3, n_cw)) for c in conv_res)
    off += n_conv
    loss_total = flat[0][off]

    result = [loss_total, grad_x[None]]
    for k in range(4):
        result += [out[n][k] for n in ORDER]
    return tuple(result)
```

```python
import math

import jax
import jax.numpy as jnp
from jax import lax
from jax.experimental import pallas as pl
from jax.experimental.pallas import tpu as pltpu

F32 = jnp.float32
BF16 = jnp.bfloat16
MESH_ID = pl.DeviceIdType.MESH
N_DEV = 8
VMEM_LIMIT_BYTES = 56 * 1024 * 1024

NORM_EPS = 1e-6
S5_WIDTH, S5_GROUP, S5_GROUPS, S5_STATE = 512, 16, 32, 64
HG_HEADS, HG_DIM, HG_CHUNK = 4, 128, 64
HG_STEP_CHUNKS = 4
ATT_E, ATT_HPG, ATT_BLOCK = 64, 8, 128
ATT_DILATIONS = (1, 4, 16)
ROT_DIM, ROPE_THETA = 16, 500000.0
D_FF = 2816
ADAM_LR, ADAM_B1, ADAM_B2, ADAM_EPS, ADAM_WD, ADAM_STEP = 0.001, 0.9, 0.999, 1e-08, 0.01, 10
NEG_BIG = -1e30


def _params(**kw):
    return pltpu.CompilerParams(vmem_limit_bytes=VMEM_LIMIT_BYTES, **kw)


def _pick(n, cands):
    for c in cands:
        if n % c == 0:
            return c
    return n


def _dot(a, b):
    return jnp.dot(a.astype(BF16), b.astype(BF16), preferred_element_type=F32)


def _dot_nt(a, b):
    return lax.dot_general(a.astype(BF16), b.astype(BF16), (((1,), (1,)), ((), ())), preferred_element_type=F32)


def _dot_tn(a, b):
    return lax.dot_general(a.astype(BF16), b.astype(BF16), (((0,), (0,)), ((), ())), preferred_element_type=F32)


def _split2(x):
    hi = x.astype(BF16)
    return hi, (x - hi.astype(F32)).astype(BF16)


def _dot_x3(a, b, contract=((1,), (0,))):
    dn = (contract, ((), ()))
    a1, a2 = _split2(a)
    b1, b2 = _split2(b)
    return (lax.dot_general(a1, b1, dn, preferred_element_type=F32) + lax.dot_general(a1, b2, dn, preferred_element_type=F32)
            + lax.dot_general(a2, b1, dn, preferred_element_type=F32))


def _sigmoid(x):
    return 1.0 / (1.0 + jnp.exp(-x))


V7X_HBM_BYTES_PER_S = 3.2e12
V7X_MXU_FLOPS_PER_S = 0.7e15
GRID_STEP_S = 0.35e-6
MM_VMEM_BUDGET = 40 * 1024 * 1024


def _divisors(n, cands):
    return [c for c in cands if c <= n and n % c == 0] or [n]


def _mm_tiles(m, n, k, sa, sb, so, sr):
    best = None
    for tm in _divisors(m, (2816, 2048, 1408, 1024, 512, 256, 128)):
        for tn in _divisors(n, (2816, 2048, 1408, 1024, 512, 256, 128)):
            for tk in _divisors(k, (k, 2816, 2560, 2304, 2048, 1536, 1408, 1280, 1024, 512, 256, 128)):
                nk = k // tk
                vmem = 2 * (tm * tk * sa + tk * tn * sb + tm * tn * (so + sr)) + (tm * tn * 4 if nk > 1 else 0)
                vmem += tm * tk * 2 * (sa > 2) + tk * tn * 2 * (sb > 2) + tm * tn * 4
                if vmem > MM_VMEM_BUDGET:
                    continue
                ni, nj = m // tm, n // tn
                for i_outer in (True, False):
                    if i_outer:
                        a_reads = 1 if nk == 1 else nj
                        b_reads = 1 if (nk == 1 and nj == 1) else ni
                    else:
                        b_reads = 1 if nk == 1 else ni
                        a_reads = 1 if (nk == 1 and ni == 1) else nj
                    traffic = a_reads * m * k * sa + b_reads * k * n * sb + m * n * (so + sr)
                    t = max(traffic / V7X_HBM_BYTES_PER_S, 2.0 * m * n * k / V7X_MXU_FLOPS_PER_S)
                    t += ni * nj * nk * GRID_STEP_S
                    t += (tm * tk * sa + tk * tn * sb + tm * tn * so) / V7X_HBM_BYTES_PER_S
                    if best is None or t < best[0]:
                        best = (t, tm, tn, tk, i_outer)
    assert best is not None, (m, n, k)
    return best[1:]


def mm(a, b, *, ta=False, tb=False, res=None, out_dtype=F32, dep=None, out_rows=None, out_off=0, into=None, name):
    m, k = (a.shape[1], a.shape[0]) if ta else a.shape
    n = b.shape[0] if tb else b.shape[1]
    assert (b.shape[1] if tb else b.shape[0]) == k
    has_res = res is not None
    tm, tn, tk, i_outer = _mm_tiles(m, n, k, a.dtype.itemsize, b.dtype.itemsize, jnp.dtype(out_dtype).itemsize,
                                    res.dtype.itemsize if has_res else 0)
    nk = k // tk
    deps = [] if dep is None else [dep]
    dn = (((0 if ta else 1,), (1 if tb else 0,)), ((), ()))

    def body_single(*refs):
        a_ref, b_ref = refs[:2]
        o_ref = refs[-1]
        out = lax.dot_general(a_ref[...].astype(BF16), b_ref[...].astype(BF16), dn, preferred_element_type=F32)
        if has_res:
            out = out + refs[2][...].astype(F32)
        o_ref[...] = out.astype(o_ref.dtype)

    def body(*refs):
        a_ref, b_ref = refs[:2]
        r_ref = refs[2] if has_res else None
        o_ref, acc_ref = refs[-2:]
        kk = pl.program_id(2)
        part = lax.dot_general(a_ref[...].astype(BF16), b_ref[...].astype(BF16), dn, preferred_element_type=F32)

        @pl.when(kk == 0)
        def _():
            acc_ref[...] = part

        @pl.when(kk > 0)
        def _():
            acc_ref[...] += part

        @pl.when(kk == nk - 1)
        def _():
            out = acc_ref[...]
            if has_res:
                out = out + r_ref[...].astype(F32)
            o_ref[...] = out.astype(o_ref.dtype)

    def ij(f):
        return (lambda g0, g1, q: f(g0, g1, q)) if i_outer else (lambda g0, g1, q: f(g1, g0, q))

    a_spec = pl.BlockSpec((tk, tm), ij(lambda i, j, q: (q, i))) if ta else pl.BlockSpec((tm, tk), ij(lambda i, j, q: (i, q)))
    b_spec = pl.BlockSpec((tn, tk), ij(lambda i, j, q: (j, q))) if tb else pl.BlockSpec((tk, tn), ij(lambda i, j, q: (q, j)))
    assert out_off % tm == 0
    off = out_off // tm
    r_spec = pl.BlockSpec((tm, tn), ij(lambda i, j, q: (i, j)))
    o_spec = pl.BlockSpec((tm, tn), ij(lambda i, j, q: (i + off, j)))
    rest = [] if into is None else [into]
    in_specs = ([a_spec, b_spec] + ([r_spec] if has_res else []) + [pl.BlockSpec((8, 128), lambda g0, g1, q: (0, 0))] * len(deps)
                + [pl.BlockSpec(memory_space=pl.ANY)] * len(rest))
    args = (a, b) + ((res,) if has_res else ()) + tuple(deps) + tuple(rest)
    grid = (m // tm, n // tn, nk) if i_outer else (n // tn, m // tm, nk)
    return pl.pallas_call(
        body_single if nk == 1 else body, grid=grid, in_specs=in_specs, out_specs=o_spec,
        out_shape=jax.ShapeDtypeStruct((out_rows or m, n), out_dtype),
        input_output_aliases={len(args) - 1: 0} if rest else {},
        scratch_shapes=[] if nk == 1 else [pltpu.VMEM((tm, tn), F32)],
        compiler_params=_params(dimension_semantics=("parallel", "parallel", "arbitrary")), name=name,
    )(*args)


def mm_drms(dy_in, w, x, g, dres, *, dep=None, name):
    halves = dy_in if isinstance(dy_in, (tuple, list)) else (dy_in,)
    m, kh = halves[0].shape
    k = kh * len(halves)
    D = w.shape[1]
    tm = _pick(m, (512, 256, 128))
    tk = max(_divisors(kh, (1536, 1408, 1280, 1024, 512, 256, 128)))
    nk, nh = k // tk, kh // tk
    deps = [] if dep is None else [dep]

    def body(*refs):
        a_refs, (b_ref, x_ref, g_ref, dres_ref) = refs[:len(halves)], refs[len(halves):len(halves) + 4]
        dx_ref, dg_ref, acc_ref = refs[-3:]
        i, q = pl.program_id(0), pl.program_id(1)
        a = a_refs[0][...] if len(halves) == 1 else jnp.where(q < nh, a_refs[0][...], a_refs[1][...])
        part = jnp.dot(a, b_ref[...], preferred_element_type=F32)

        @pl.when(q == 0)
        def _():
            acc_ref[...] = part

        @pl.when(q > 0)
        def _():
            acc_ref[...] += part

        @pl.when((i == 0) & (q == 0))
        def _():
            dg_ref[...] = jnp.zeros_like(dg_ref)

        @pl.when(q == nk - 1)
        def _():
            dyv = acc_ref[...]
            xv = x_ref[...]
            r = lax.rsqrt(jnp.mean(xv * xv, axis=-1, keepdims=True) + NORM_EPS)
            xh = xv * r
            dg_ref[...] += jnp.sum(dyv * xh, axis=0, keepdims=True)
            dxh = dyv * g_ref[...]
            dx_ref[...] = dres_ref[...] + r * (dxh - xh * jnp.mean(dxh * xh, axis=-1, keepdims=True))

    row = pl.BlockSpec((tm, D), lambda i, q: (i, 0))
    vec = pl.BlockSpec((1, D), lambda i, q: (0, 0))
    a_specs = [pl.BlockSpec((tm, tk), lambda i, q, h=h: (i, jnp.clip(q - h * nh, 0, nh - 1))) for h in range(len(halves))]
    in_specs = a_specs + [pl.BlockSpec((tk, D), lambda i, q: (q, 0)), row, vec, row]
    in_specs += [pl.BlockSpec((8, 128), lambda i, q: (0, 0))] * len(deps)
    return pl.pallas_call(
        body, grid=(m // tm, nk), in_specs=in_specs, out_specs=[row, vec],
        out_shape=[jax.ShapeDtypeStruct((m, D), F32), jax.ShapeDtypeStruct((1, D), F32)],
        scratch_shapes=[pltpu.VMEM((tm, D), F32)],
        compiler_params=_params(dimension_semantics=("arbitrary", "arbitrary")), name=name,
    )(*halves, w, x, g.reshape(1, D), dres, *deps)


def final_loss(h, g, target, *, name):
    L, D = h.shape
    tr = _pick(L, (256, 128))

    def body(x_ref, g_ref, t_ref, loss_ref, dx_ref, dg_ref):
        xv = x_ref[...]
        gv = g_ref[...]
        r = lax.rsqrt(jnp.mean(xv * xv, axis=-1, keepdims=True) + NORM_EPS)
        xh = xv * r
        err = xh * gv - t_ref[...]

        @pl.when(pl.program_id(0) == 0)
        def _():
            dg_ref[...] = jnp.zeros_like(dg_ref)
            loss_ref[...] = jnp.zeros_like(loss_ref)

        loss_ref[...] += 0.5 * jnp.sum(jnp.mean(err * err, axis=-1, keepdims=True), axis=0, keepdims=True)
        dyv = err * (1.0 / D)
        dg_ref[...] += jnp.sum(dyv * xh, axis=0, keepdims=True)
        dxh = dyv * gv
        dx_ref[...] = r * (dxh - xh * jnp.mean(dxh * xh, axis=-1, keepdims=True))

    row = pl.BlockSpec((tr, D), lambda i: (i, 0))
    vec = pl.BlockSpec((1, D), lambda i: (0, 0))
    one = pl.BlockSpec((1, 1), lambda i: (0, 0))
    return pl.pallas_call(body, grid=(L // tr,), in_specs=[row, vec, row], out_specs=[one, row, vec],
                          out_shape=[jax.ShapeDtypeStruct((1, 1), F32), jax.ShapeDtypeStruct((L, D), F32),
                                     jax.ShapeDtypeStruct((1, D), F32)],
                          compiler_params=_params(dimension_semantics=("arbitrary",)), name=name)(
        h, g.reshape(1, D), target)


def _cmul(ar, ai, br, bi):
    return ar * br - ai * bi, ar * bi + ai * br


def _powers(ar, ai):
    rows = [(ar, ai)]
    for _ in range(7):
        rows.append(_cmul(rows[-1][0], rows[-1][1], ar, ai))
    table = (jnp.concatenate([r[0] for r in rows], axis=0), jnp.concatenate([r[1] for r in rows], axis=0))
    return (rows[0], rows[1], rows[3]), table


def _block_scan(br, bi, steps, shift):
    yr, yi = br, bi
    for s, (pr, pi) in zip((1, 2, 4), steps):
        sr, si = shift(yr, s), shift(yi, s)
        yr, yi = yr + pr * sr - pi * si, yi + pr * si + pi * sr
    return yr, yi


def s5_core_fwd(proj, a_re, a_im, wb_re, wb_im, wc_re, wc_im, *, name):
    L = proj.shape[0]
    parts, cu, W = wb_re.shape

    def body(u_ref, ar_ref, ai_ref, wbr_ref, wbi_ref, wcr_ref, wci_ref, y_ref, xr_ref, xi_ref, br_ref, bi_ref):
        u = u_ref[...]
        br_ref[...] = _dot(u, wbr_ref[...])
        bi_ref[...] = _dot(u, wbi_ref[...])
        steps, (tr, ti) = _powers(ar_ref[...], ai_ref[...])
        row = lax.broadcasted_iota(jnp.int32, (8, W), 0)

        def shift(y, s):
            return jnp.where(row >= s, pltpu.roll(y, s, 0), 0.0)

        def step(t8, carry):
            cr, ci = carry
            base = pl.multiple_of(t8 * 8, 8)
            yr, yi = _block_scan(br_ref[pl.ds(base, 8), :], bi_ref[pl.ds(base, 8), :], steps, shift)
            xr = yr + tr * cr - ti * ci
            xi = yi + tr * ci + ti * cr
            xr_ref[pl.ds(base, 8), :] = xr
            xi_ref[pl.ds(base, 8), :] = xi
            return jnp.broadcast_to(xr[7:8, :], (8, W)), jnp.broadcast_to(xi[7:8, :], (8, W))

        zero = jnp.zeros((8, W), F32)
        lax.fori_loop(0, L // 8, step, (zero, zero), unroll=2)
        y_ref[...] = _dot(xr_ref[...], wcr_ref[...]) + _dot(xi_ref[...], wci_ref[...])

    ucol = pl.BlockSpec((L, cu), lambda t: (0, t))
    vec = pl.BlockSpec((1, W), lambda t: (0, t))
    col = pl.BlockSpec((L, W), lambda t: (0, t))
    wb = pl.BlockSpec((None, cu, W), lambda t: (t, 0, 0))
    wc = pl.BlockSpec((None, W, cu), lambda t: (t, 0, 0))
    return pl.pallas_call(body, grid=(parts,), in_specs=[ucol, vec, vec, wb, wb, wc, wc], out_specs=[ucol, col, col],
                          out_shape=[jax.ShapeDtypeStruct((L, parts * cu), F32)]
                          + [jax.ShapeDtypeStruct((L, parts * W), F32)] * 2,
                          scratch_shapes=[pltpu.VMEM((L, W), F32)] * 2,
                          compiler_params=_params(dimension_semantics=("parallel",)), name=name)(
        proj, a_re, a_im, wb_re, wb_im, wc_re, wc_im)


def s5_core_bwd(dy, du_d, proj, xs_re, xs_im, a_re, a_im, wb_re, wb_im, wc_re, wc_im, *, name):
    L = proj.shape[0]
    parts, cu, W = wb_re.shape

    def body(dy_ref, dud_ref, u_ref, xr_ref, xi_ref, ar_ref, ai_ref, wbr_ref, wbi_ref, wcr_ref, wci_ref,
             du_ref, dwbr_ref, dwbi_ref, dwcr_ref, dwci_ref, dar_ref, dai_ref, lr_ref, li_ref):
        dy = dy_ref[...]
        lr_ref[...] = _dot_nt(dy, wcr_ref[...])
        li_ref[...] = _dot_nt(dy, wci_ref[...])
        dwcr_ref[...] = _dot_tn(xr_ref[...], dy)
        dwci_ref[...] = _dot_tn(xi_ref[...], dy)
        ar, ai = ar_ref[...], -ai_ref[...]
        steps, (tr, ti) = _powers(ar, ai)
        tr = jnp.concatenate([tr[j:j + 1, :] for j in range(7, -1, -1)], axis=0)
        ti = jnp.concatenate([ti[j:j + 1, :] for j in range(7, -1, -1)], axis=0)
        row8 = lax.broadcasted_iota(jnp.int32, (8, W), 0)
        nblk = L // 8

        def shift(y, s):
            return jnp.where(row8 < 8 - s, pltpu.roll(y, 8 - s, 0), 0.0)

        def step(s, carry):
            cr, ci = carry
            base = pl.multiple_of((nblk - 1 - s) * 8, 8)
            yr, yi = _block_scan(lr_ref[pl.ds(base, 8), :], li_ref[pl.ds(base, 8), :], steps, shift)
            lr = yr + tr * cr - ti * ci
            li = yi + tr * ci + ti * cr
            lr_ref[pl.ds(base, 8), :] = lr
            li_ref[pl.ds(base, 8), :] = li
            return jnp.broadcast_to(lr[0:1, :], (8, W)), jnp.broadcast_to(li[0:1, :], (8, W))

        zero = jnp.zeros((8, W), F32)
        lax.fori_loop(0, nblk, step, (zero, zero), unroll=2)
        row = lax.broadcasted_iota(jnp.int32, (L, W), 0)
        xpr = jnp.where(row >= 1, pltpu.roll(xr_ref[...], 1, 0), 0.0)
        xpi = jnp.where(row >= 1, pltpu.roll(xi_ref[...], 1, 0), 0.0)
        lr, li = lr_ref[...], li_ref[...]
        dar_ref[...] = jnp.sum(lr * xpr + li * xpi, axis=0, keepdims=True)
        dai_ref[...] = jnp.sum(li * xpr - lr * xpi, axis=0, keepdims=True)
        u = u_ref[...]
        dwbr_ref[...] = _dot_tn(u, lr)
        dwbi_ref[...] = _dot_tn(u, li)
        du_ref[...] = (dud_ref[...] + _dot_nt(lr, wbr_ref[...]) + _dot_nt(li, wbi_ref[...])).astype(du_ref.dtype)

    ucol = pl.BlockSpec((L, cu), lambda t: (0, t))
    vec = pl.BlockSpec((1, W), lambda t: (0, t))
    col = pl.BlockSpec((L, W), lambda t: (0, t))
    wb = pl.BlockSpec((None, cu, W), lambda t: (t, 0, 0))
    wc = pl.BlockSpec((None, W, cu), lambda t: (t, 0, 0))
    return pl.pallas_call(
        body, grid=(parts,), in_specs=[ucol, ucol, ucol, col, col, vec, vec, wb, wb, wc, wc],
        out_specs=[ucol, wb, wb, wc, wc, vec, vec],
        out_shape=[jax.ShapeDtypeStruct((L, parts * cu), BF16)] + [jax.ShapeDtypeStruct((parts, cu, W), F32)] * 2
        + [jax.ShapeDtypeStruct((parts, W, cu), F32)] * 2 + [jax.ShapeDtypeStruct((1, parts * W), F32)] * 2,
        scratch_shapes=[pltpu.VMEM((L, W), F32)] * 2,
        compiler_params=_params(dimension_semantics=("parallel",)), name=name,
    )(dy, du_d, proj, xs_re, xs_im, a_re, a_im, wb_re, wb_im, wc_re, wc_im)


def _gelu(y):
    c = math.sqrt(2.0 / math.pi)
    t = jnp.tanh(c * (y + 0.044715 * y * y * y))
    return 0.5 * y * (1.0 + t), t


def s5_out_fwd(y0, proj, dvec, glu_w, glu_b, *, name):
    L, C = y0.shape
    tr = _pick(L, (256, 128))

    def body(y_ref, u_ref, d_ref, w_ref, b_ref, o_ref):
        z, _ = _gelu(y_ref[...] + d_ref[...] * u_ref[...])
        zg = _dot(z, w_ref[...]) + b_ref[...]
        o_ref[...] = (z * _sigmoid(zg)).astype(o_ref.dtype)

    row = pl.BlockSpec((tr, C), lambda i: (i, 0))
    vec = pl.BlockSpec((1, C), lambda i: (0, 0))
    wsp = pl.BlockSpec((C, C), lambda i: (0, 0))
    return pl.pallas_call(body, grid=(L // tr,), in_specs=[row, row, vec, wsp, vec], out_specs=row,
                          out_shape=jax.ShapeDtypeStruct((L, 2 * C), BF16), name=name)(
        y0, proj, dvec, glu_w, glu_b)


def s5_out_bwd(y0, proj, dvec, glu_w, glu_b, dcat, *, name):
    L, C = y0.shape
    tr = _pick(L, (256, 128))

    def body(y_ref, u_ref, d_ref, w_ref, b_ref, do_ref, dy_ref, dud_ref, z_ref, dzg_ref, db_ref, dd_ref):
        u = u_ref[...]
        y = y_ref[...] + d_ref[...] * u
        z, t = _gelu(y)
        zg = _dot(z, w_ref[...]) + b_ref[...]
        s = _sigmoid(zg)
        do = do_ref[...]
        dzg = do * z * s * (1.0 - s)
        dz = do * s + _dot_nt(dzg, w_ref[...])
        c = math.sqrt(2.0 / math.pi)
        dgelu = 0.5 * (1.0 + t) + 0.5 * y * (1.0 - t * t) * c * (1.0 + 3.0 * 0.044715 * y * y)
        dy = dz * dgelu

        @pl.when(pl.program_id(0) == 0)
        def _():
            db_ref[...] = jnp.zeros_like(db_ref)
            dd_ref[...] = jnp.zeros_like(dd_ref)

        db_ref[...] += jnp.sum(dzg, axis=0, keepdims=True)
        dd_ref[...] += jnp.sum(dy * u, axis=0, keepdims=True)
        dy_ref[...] = dy
        dud_ref[...] = dy * d_ref[...]
        z_ref[...] = z.astype(BF16)
        dzg_ref[...] = dzg.astype(BF16)

    row = pl.BlockSpec((tr, C), lambda i: (i, 0))
    vec = pl.BlockSpec((1, C), lambda i: (0, 0))
    wsp = pl.BlockSpec((C, C), lambda i: (0, 0))
    return pl.pallas_call(body, grid=(L // tr,), in_specs=[row, row, vec, wsp, vec, row],
                          out_specs=[row, row, row, row, vec, vec],
                          out_shape=[jax.ShapeDtypeStruct((L, C), F32), jax.ShapeDtypeStruct((L, C), F32),
                                     jax.ShapeDtypeStruct((L, C), BF16), jax.ShapeDtypeStruct((L, C), BF16),
                                     jax.ShapeDtypeStruct((1, C), F32), jax.ShapeDtypeStruct((1, C), F32)],
                          compiler_params=_params(dimension_semantics=("arbitrary",)), name=name)(
        y0, proj, dvec, glu_w, glu_b, dcat)


def _dot_tri(tri, x, tri_left=True):
    t = tri.astype(BF16)
    x1 = x.astype(BF16)
    r1 = x - x1.astype(F32)
    x2 = r1.astype(BF16)
    x3 = (r1 - x2.astype(F32)).astype(BF16)
    dot = (lambda p: jnp.dot(t, p, preferred_element_type=F32)) if tri_left else (
        lambda p: jnp.dot(p, t, preferred_element_type=F32))
    return dot(x1) + dot(x2) + dot(x3)


def _hg_gates(xq, xf, lb, tri):
    C = xq.shape[0]
    sq = _sigmoid(xq)
    q = xq * sq
    sg = _sigmoid(xf)
    f = lb + (1.0 - lb) * sg
    kk = 1.0 - f
    b = _dot_tri(tri, jnp.log(f))
    bm = b[C // 2 - 1:C // 2, :]
    bl = b[C - 1:C, :]
    eb = jnp.exp(b)
    eqm, ekm, ekl = jnp.exp(b - bm), jnp.exp(bm - b), jnp.exp(bl - b)
    return dict(sq=sq, q=q, sg=sg, f=f, kk=kk, eb=eb, ebl=jnp.exp(bl), eqm=eqm, ekm=ekm, ekl=ekl,
                qb=q * eb, qt=q * eqm, kt=kk * ekm, kh=kk * ekl)


def _tri(C, lower):
    r = lax.broadcasted_iota(jnp.int32, (C, C), 0)
    c = lax.broadcasted_iota(jnp.int32, (C, C), 1)
    return (r >= c) if lower else (c >= r)


def hgrn_fwd(proj, lb, norm_g, cat, *, name):
    L = proj.shape[0]
    C, H, K = HG_CHUNK, HG_HEADS, HG_DIM
    HK = H * K
    nc = L // C

    def body(q_ref, f_ref, i_ref, g_ref, lb_ref, ng_ref, cat_ref, o_ref, sall_ref, st_ref):
        @pl.when(pl.program_id(0) == 0)
        def _():
            st_ref[...] = jnp.zeros_like(st_ref)

        mask = _tri(C, True)
        sts = [st_ref[h] for h in range(H)]
        for s in range(S):
            rs = slice(s * C, (s + 1) * C)
            gt = _hg_gates(q_ref[rs, :], f_ref[rs, :], lb_ref[...], mask.astype(F32))
            v_all = i_ref[rs, :]
            outs = []
            for h in range(H):
                sl = slice(h * K, (h + 1) * K)
                v, st = v_all[:, sl], sts[h]
                sall_ref[s, h] = st
                att = jnp.where(mask, _dot_nt(gt["qt"][:, sl], gt["kt"][:, sl]), 0.0)
                o = _dot(att, v) + _dot_nt(gt["qb"][:, sl], st)
                sts[h] = st * gt["ebl"][:, sl] + _dot_tn(v, gt["kh"][:, sl])
                outs.append(o * lax.rsqrt(jnp.mean(o * o, axis=-1, keepdims=True) + NORM_EPS))
            xg = g_ref[rs, :]
            o_ref[rs, :] = (jnp.concatenate(outs, axis=1) * ng_ref[...] * (xg * _sigmoid(xg))).astype(o_ref.dtype)
        for h in range(H):
            st_ref[h] = sts[h]

    S = HG_STEP_CHUNKS

    def blk(cb):
        return pl.BlockSpec((S * C, HK), lambda i: (i, cb))

    vec = pl.BlockSpec((1, HK), lambda i: (0, 0))
    return pl.pallas_call(
        body, grid=(nc // S,), in_specs=[blk(1), blk(2), blk(3), blk(4), vec, vec, pl.BlockSpec(memory_space=pl.ANY)],
        out_specs=[pl.BlockSpec((S * C, HK), lambda i: (i, 1)), pl.BlockSpec((S, H, K, K), lambda i: (i, 0, 0, 0))],
        out_shape=[jax.ShapeDtypeStruct((L, 2 * HK), BF16), jax.ShapeDtypeStruct((nc, H, K, K), F32)],
        input_output_aliases={6: 0},
        scratch_shapes=[pltpu.VMEM((H, K, K), F32)],
        compiler_params=_params(dimension_semantics=("arbitrary",)), name=name,
    )(proj, proj, proj, proj, lb, norm_g, cat)


def hgrn_bwd(proj, lb, norm_g, sall, dcat, du, *, name):
    L = proj.shape[0]
    C, H, K = HG_CHUNK, HG_HEADS, HG_DIM
    HK = H * K
    nc = L // C

    def body(q_ref, f_ref, i_ref, g_ref, lb_ref, ng_ref, sall_ref, do_ref, du_ref, dx_ref, dlb_ref, dng_ref, dst_ref):
        @pl.when(pl.program_id(0) == 0)
        def _():
            dst_ref[...] = jnp.zeros_like(dst_ref)
            dlb_ref[...] = jnp.zeros_like(dlb_ref)
            dng_ref[...] = jnp.zeros_like(dng_ref)

        mask = _tri(C, True)
        lb_all, ng = lb_ref[...], ng_ref[...]
        dx_ref[:, 0:HK] = du_ref[...]
        dsts = [dst_ref[h] for h in range(H)]
        for s in reversed(range(S)):
            rs = slice(s * C, (s + 1) * C)
            dsts = chunk_bwd(rs, s, dsts, mask, lb_all, ng, q_ref, f_ref, i_ref, g_ref, sall_ref, do_ref,
                             dx_ref, dlb_ref, dng_ref)
        for h in range(H):
            dst_ref[h] = dsts[h]

    def chunk_bwd(rs, s, dsts, mask, lb_all, ng, q_ref, f_ref, i_ref, g_ref, sall_ref, do_ref, dx_ref, dlb_ref, dng_ref):
        xq, xg, v_all = q_ref[rs, :], g_ref[rs, :], i_ref[rs, :]
        gt = _hg_gates(xq, f_ref[rs, :], lb_all, mask.astype(F32))
        sgg = _sigmoid(xg)
        d_ob = do_ref[rs, :]
        d_on = d_ob * (xg * sgg)
        doh = d_on * ng
        ohs, d_qts, d_qbs, d_kts, d_khs, dvs, d_bls, new_dsts = [], [], [], [], [], [], [], []
        for h in range(H):
            sl = slice(h * K, (h + 1) * K)
            v, st, dst = v_all[:, sl], sall_ref[s, h], dsts[h]
            qt, kt, kh, qb = gt["qt"][:, sl], gt["kt"][:, sl], gt["kh"][:, sl], gt["qb"][:, sl]
            att = jnp.where(mask, _dot_nt(qt, kt), 0.0)
            o = _dot(att, v) + _dot_nt(qb, st)
            r = lax.rsqrt(jnp.mean(o * o, axis=-1, keepdims=True) + NORM_EPS)
            oh = o * r
            do = r * (doh[:, sl] - oh * jnp.mean(doh[:, sl] * oh, axis=-1, keepdims=True))
            datt = jnp.where(mask, _dot_nt(do, v), 0.0)
            dvs.append(_dot_tn(att, do) + _dot_nt(kh, dst))
            d_qbs.append(_dot_x3(do, st))
            d_qts.append(_dot_x3(datt, kt))
            d_kts.append(_dot_x3(datt, qt, ((0,), (0,))))
            d_kh = _dot_x3(v, dst)
            d_khs.append(d_kh)
            d_bls.append(jnp.sum(dst * st, axis=0, keepdims=True) * gt["ebl"][:, sl]
                         + jnp.sum(d_kh * kh, axis=0, keepdims=True))
            new_dsts.append(dst * gt["ebl"][:, sl] + _dot_tn(do, qb))
            ohs.append(oh)
        oh, d_qt, d_qb, d_kt, d_kh, dv, d_bl = (jnp.concatenate(p, axis=1) for p in
                                                (ohs, d_qts, d_qbs, d_kts, d_khs, dvs, d_bls))
        dxg = d_ob * (oh * ng) * (sgg * (1.0 + xg * (1.0 - sgg)))
        dng_ref[...] += jnp.sum(d_on * oh, axis=0, keepdims=True)
        dq = d_qt * gt["eqm"] + d_qb * gt["eb"]
        db = d_qt * gt["qt"] + d_qb * gt["qb"] - d_kt * gt["kt"] - d_kh * gt["kh"]
        rowi = lax.broadcasted_iota(jnp.int32, (C, HK), 0)
        db = db + jnp.where(rowi == C - 1, d_bl, 0.0)
        dkk = d_kt * gt["ekm"] + d_kh * gt["ekl"]
        dlg = _dot_tri(_tri(C, False).astype(F32), db)
        df = dlg / gt["f"] - dkk
        sg, sq = gt["sg"], gt["sq"]
        dlb_ref[...] += jnp.sum(df * (1.0 - sg), axis=0, keepdims=True)
        dx_ref[rs, HK:2 * HK] = (dq * (sq * (1.0 + xq * (1.0 - sq)))).astype(dx_ref.dtype)
        dx_ref[rs, 2 * HK:3 * HK] = (df * (1.0 - lb_all) * sg * (1.0 - sg)).astype(dx_ref.dtype)
        dx_ref[rs, 3 * HK:4 * HK] = dv.astype(dx_ref.dtype)
        dx_ref[rs, 4 * HK:5 * HK] = dxg.astype(dx_ref.dtype)
        return new_dsts

    S = HG_STEP_CHUNKS
    ns = nc // S

    def blk(cb):
        return pl.BlockSpec((S * C, HK), lambda i: (ns - 1 - i, cb))

    vec = pl.BlockSpec((1, HK), lambda i: (0, 0))
    return pl.pallas_call(
        body, grid=(ns,),
        in_specs=[blk(1), blk(2), blk(3), blk(4), vec, vec,
                  pl.BlockSpec((S, H, K, K), lambda i: (ns - 1 - i, 0, 0, 0)), blk(1), blk(0)],
        out_specs=[pl.BlockSpec((S * C, 5 * HK), lambda i: (ns - 1 - i, 0)), vec, vec],
        out_shape=[jax.ShapeDtypeStruct((L, 5 * HK), BF16), jax.ShapeDtypeStruct((1, HK), F32),
                   jax.ShapeDtypeStruct((1, HK), F32)],
        scratch_shapes=[pltpu.VMEM((H, K, K), F32)],
        compiler_params=_params(dimension_semantics=("arbitrary",)), name=name,
    )(proj, proj, proj, proj, lb, norm_g, sall, dcat, du)


def _shift_down(x, k, row):
    return jnp.where(row >= k, pltpu.roll(x, k, 0), 0.0)


def _shift_up(x, k, row):
    n = x.shape[0]
    return jnp.where(row < n - k, pltpu.roll(x, n - k, 0), 0.0)


def convgate_fwd(hu, conv_w, conv_b, *, name):
    L, C2 = hu.shape
    C = C2 // 2
    tc = _pick(C, (256, 128))
    nb = C // tc

    def body(a_ref, b_ref, wa_ref, wb_ref, ba_ref, bb_ref, o_ref):
        row = lax.broadcasted_iota(jnp.int32, (L, tc), 0)

        def conv(x, w, bias):
            return w[2:3, :] * x + w[1:2, :] * _shift_down(x, 1, row) + w[0:1, :] * _shift_down(x, 2, row) + bias

        ca = conv(a_ref[...], wa_ref[...], ba_ref[...])
        cb = conv(b_ref[...], wb_ref[...], bb_ref[...])
        o_ref[...] = (ca * _sigmoid(ca) * cb).astype(o_ref.dtype)

    def col(off, rows):
        return pl.BlockSpec((rows, tc), lambda j: (0, j + off))

    return pl.pallas_call(
        body, grid=(nb,), in_specs=[col(0, L), col(nb, L), col(0, 3), col(nb, 3), col(0, 1), col(nb, 1)],
        out_specs=col(0, L), out_shape=jax.ShapeDtypeStruct((L, C), BF16),
        compiler_params=_params(dimension_semantics=("parallel",)), name=name,
    )(hu, hu, conv_w, conv_w, conv_b, conv_b)


def convgate_bwd(hu, conv_w, conv_b, dact, *, name):
    L, C2 = hu.shape
    C = C2 // 2
    tc = _pick(C, (256, 128))
    nb = C // tc

    def body(a_ref, b_ref, wa_ref, wb_ref, ba_ref, bb_ref, d_ref, dxa_ref, dxb_ref, dwa_ref, dwb_ref, dba_ref, dbb_ref):
        row = lax.broadcasted_iota(jnp.int32, (L, tc), 0)

        def conv(x, w, bias):
            x1 = _shift_down(x, 1, row)
            x2 = _shift_down(x, 2, row)
            return w[2:3, :] * x + w[1:2, :] * x1 + w[0:1, :] * x2 + bias, x1, x2

        xa, xb = a_ref[...], b_ref[...]
        wa, wb = wa_ref[...], wb_ref[...]
        ca, xa1, xa2 = conv(xa, wa, ba_ref[...])
        cb, xb1, xb2 = conv(xb, wb, bb_ref[...])
        d = d_ref[...]
        sa = _sigmoid(ca)
        dca = d * cb * (sa * (1.0 + ca * (1.0 - sa)))
        dcb = d * (ca * sa)

        def back(dc, w, x, x1, x2, dx_ref, dw_ref, db_ref):
            dx = w[2:3, :] * dc + w[1:2, :] * _shift_up(dc, 1, row) + w[0:1, :] * _shift_up(dc, 2, row)
            dx_ref[...] = dx.astype(dx_ref.dtype)
            dw_ref[...] = jnp.concatenate([jnp.sum(dc * x2, axis=0, keepdims=True),
                                           jnp.sum(dc * x1, axis=0, keepdims=True),
                                           jnp.sum(dc * x, axis=0, keepdims=True)], axis=0)
            db_ref[...] = jnp.sum(dc, axis=0, keepdims=True)

        back(dca, wa, xa, xa1, xa2, dxa_ref, dwa_ref, dba_ref)
        back(dcb, wb, xb, xb1, xb2, dxb_ref, dwb_ref, dbb_ref)

    def col(off, rows):
        return pl.BlockSpec((rows, tc), lambda j: (0, j + off))

    outs = pl.pallas_call(
        body, grid=(nb,),
        in_specs=[col(0, L), col(nb, L), col(0, 3), col(nb, 3), col(0, 1), col(nb, 1), col(0, L)],
        out_specs=[col(0, L), col(0, L), col(0, 3), col(0, 3), col(0, 1), col(0, 1)],
        out_shape=[jax.ShapeDtypeStruct((L, C), BF16)] * 2 + [jax.ShapeDtypeStruct((3, C), F32)] * 2
        + [jax.ShapeDtypeStruct((1, C), F32)] * 2,
        compiler_params=_params(dimension_semantics=("parallel",)), name=name,
    )(hu, hu, conv_w, conv_w, conv_b, conv_b, dact)
    dxa, dxb, dwa, dwb, dba, dbb = outs
    return (dxa, dxb), jnp.concatenate([dwa, dwb], axis=1), jnp.concatenate([dba, dbb], axis=1)


def rope_tables(positions):
    half = ROT_DIM // 2
    inv_freq = ROPE_THETA ** (-jnp.arange(half, dtype=F32) * 2.0 / ROT_DIM)
    ang = positions.astype(F32)[:, None] * inv_freq
    cos, sin = jnp.cos(ang), jnp.sin(ang)
    L = positions.shape[0]
    one = jnp.ones((L, ATT_E - ROT_DIM), F32)
    zero = jnp.zeros((L, ATT_E - ROT_DIM), F32)
    zh = jnp.zeros((L, half), F32)
    tc = jnp.concatenate([cos, cos, one], axis=1)
    ts1 = jnp.concatenate([zh, sin, zero], axis=1)
    ts2 = jnp.concatenate([-sin, zh, zero], axis=1)
    return tuple(jnp.concatenate([t, t], axis=1) for t in (tc, ts1, ts2))


def norm_mm(x, g, w_t, *, tabs=None, name):
    L, D = x.shape
    N = w_t.shape[0]
    W = 512
    tm = _pick(L, (1024, 512, 256, 128))
    nq = N // (3 * W)
    scale = ATT_E ** -0.5
    rope = tabs is not None

    def body(x_ref, g_ref, b_ref, *rest):
        hn_ref, o_ref, hn_scr = rest[-3:]
        j = pl.program_id(1)

        @pl.when(j == 0)
        def _():
            xv = x_ref[...]
            r = lax.rsqrt(jnp.mean(xv * xv, axis=-1, keepdims=True) + NORM_EPS)
            hn = (xv * r * g_ref[...]).astype(BF16)
            hn_scr[...] = hn
            hn_ref[...] = hn

        out = _dot_nt(hn_scr[...], b_ref[...])
        if rope:
            c_ref, s1_ref, s2_ref = rest[:3]
            c = jnp.concatenate([c_ref[...]] * 4, axis=1)
            s1 = jnp.concatenate([s1_ref[...]] * 4, axis=1)
            s2 = jnp.concatenate([s2_ref[...]] * 4, axis=1)
            rot = out * c + pltpu.roll(out, 8, 1) * s1 + pltpu.roll(out, W - 8, 1) * s2
            out = jnp.where(j < 2 * nq, rot * jnp.where(j < nq, scale, 1.0), out)
        o_ref[...] = out

    row = pl.BlockSpec((tm, D), lambda i, j: (i, 0))
    tab = pl.BlockSpec((tm, 128), lambda i, j: (i, 0))
    return pl.pallas_call(body, grid=(L // tm, N // W),
                          in_specs=[row, pl.BlockSpec((1, D), lambda i, j: (0, 0)), pl.BlockSpec((W, D), lambda i, j: (j, 0))]
                          + ([tab, tab, tab] if rope else []),
                          out_specs=[row, pl.BlockSpec((tm, W), lambda i, j: (i, j))],
                          out_shape=[jax.ShapeDtypeStruct((L, D), BF16), jax.ShapeDtypeStruct((L, N), F32)],
                          scratch_shapes=[pltpu.VMEM((tm, D), BF16)],
                          compiler_params=_params(dimension_semantics=("parallel", "arbitrary")), name=name)(
        x, g.reshape(1, D), w_t, *(tabs or ()))


def rope_bwd(slabs, tabs, *, name):
    L, W = slabs[0].shape
    tr = _pick(L, (256, 128))
    nq = len(slabs) // 3
    scale = ATT_E ** -0.5

    def body(*refs):
        d_refs, (c_ref, s1_ref, s2_ref, o_ref) = refs[:3 * nq], refs[3 * nq:]
        c = jnp.concatenate([c_ref[...]] * 4, axis=1)
        s1 = jnp.concatenate([s1_ref[...]] * 4, axis=1)
        s2 = jnp.concatenate([s2_ref[...]] * 4, axis=1)
        for j, d_ref in enumerate(d_refs):
            dy = d_ref[...]
            if j < 2 * nq:
                dy = dy * c + pltpu.roll(dy * s1, W - 8, 1) + pltpu.roll(dy * s2, 8, 1)
            if j < nq:
                dy = dy * scale
            o_ref[:, j * W:(j + 1) * W] = dy.astype(o_ref.dtype)

    slab = pl.BlockSpec((tr, W), lambda i: (i, 0))
    tab = pl.BlockSpec((tr, 128), lambda i: (i, 0))
    return pl.pallas_call(body, grid=(L // tr,), in_specs=[slab] * (3 * nq) + [tab, tab, tab],
                          out_specs=pl.BlockSpec((tr, 3 * nq * W), lambda i: (i, 0)),
                          out_shape=jax.ShapeDtypeStruct((L, 3 * nq * W), BF16),
                          compiler_params=_params(dimension_semantics=("parallel",)), name=name)(*slabs, *tabs)


def _att_masks(has_prev):
    qi = lax.broadcasted_iota(jnp.int32, (ATT_BLOCK, ATT_BLOCK), 0)
    kj = lax.broadcasted_iota(jnp.int32, (ATT_BLOCK, ATT_BLOCK), 1)
    return qi >= kj, (kj >= qi) & has_prev


ATT_COLS = 128


def _att_rows(j, d, nb):
    B = ATT_BLOCK
    r, n = j // nb, j % nb
    start = r + d * B * n
    has_prev = n > 0
    pstart = jnp.where(has_prev, start - d * B, start)
    if d == 1:
        return pl.ds(pl.multiple_of(start, B), B), pl.ds(pl.multiple_of(pstart, B), B), has_prev
    return pl.ds(start, B, stride=d), pl.ds(pstart, B, stride=d), has_prev


def _qkv_specs(L, g):
    per = ATT_HPG * ATT_E // ATT_COLS
    third = len(ATT_DILATIONS) * per
    return [pl.BlockSpec((L, ATT_COLS), lambda c, base=base: (0, base + c))
            for base in (g * per, third + g * per, 2 * third + g * per)]


def attn_fwd(qkv, g, d, *, name):
    L, W = qkv.shape[0], ATT_HPG * ATT_E
    B, E = ATT_BLOCK, ATT_E
    nblk = L // B
    nb = nblk // d

    def body(q_ref, k_ref, v_ref, o_ref, l_ref):
        def step(j, carry):
            cur, prv, has_prev = _att_rows(j, d, nb)
            mc, mp = _att_masks(has_prev)
            qb, kc, kp, vc, vp = q_ref[cur, :], k_ref[cur, :], k_ref[prv, :], v_ref[cur, :], v_ref[prv, :]
            outs, lses = [], []
            for h in range(ATT_COLS // E):
                sl = slice(h * E, (h + 1) * E)
                sc = jnp.where(mc, _dot_nt(qb[:, sl], kc[:, sl]), NEG_BIG)
                sp = jnp.where(mp, _dot_nt(qb[:, sl], kp[:, sl]), NEG_BIG)
                m = jnp.maximum(jnp.max(sc, axis=-1, keepdims=True), jnp.max(sp, axis=-1, keepdims=True))
                pc = jnp.exp(sc - m)
                pp = jnp.exp(sp - m)
                den = jnp.sum(pc, axis=-1, keepdims=True) + jnp.sum(pp, axis=-1, keepdims=True)
                outs.append((_dot(pc, vc[:, sl]) + _dot(pp, vp[:, sl])) / den)
                lses.append(jnp.broadcast_to(m + jnp.log(den), (B, E)))
            o_ref[cur, :] = jnp.concatenate(outs, axis=1)
            l_ref[cur, :] = jnp.concatenate(lses, axis=1)
            return carry

        lax.fori_loop(0, nblk, step, 0, unroll=4)

    col = pl.BlockSpec((L, ATT_COLS), lambda c: (0, c))
    return pl.pallas_call(body, grid=(W // ATT_COLS,), in_specs=_qkv_specs(L, g), out_specs=[col] * 2,
                          out_shape=[jax.ShapeDtypeStruct((L, W), F32)] * 2,
                          compiler_params=_params(dimension_semantics=("parallel",)), name=name)(qkv, qkv, qkv)


def attn_bwd(qkv, g, lse, do, dl, d, *, name):
    L, W = qkv.shape[0], ATT_HPG * ATT_E
    B, E = ATT_BLOCK, ATT_E
    nblk = L // B
    nb = nblk // d

    def body(q_ref, k_ref, v_ref, l_ref, do_ref, dl_ref, dq_ref, dk_ref, dv_ref):
        dk_ref[...] = jnp.zeros_like(dk_ref)
        dv_ref[...] = jnp.zeros_like(dv_ref)

        def step(j, carry):
            cur, prv, has_prev = _att_rows(j, d, nb)
            mc, mp = _att_masks(has_prev)
            qb, kc, kp, vc, vp = q_ref[cur, :], k_ref[cur, :], k_ref[prv, :], v_ref[cur, :], v_ref[prv, :]
            lb, dob, dlb = l_ref[cur, :], do_ref[cur, :], dl_ref[cur, :]
            dqs, dkc, dkp, dvc, dvp = [], [], [], [], []
            for h in range(ATT_COLS // E):
                sl = slice(h * E, (h + 1) * E)
                qh, doh = qb[:, sl], dob[:, sl]
                lse_h, dl_h = lb[:, h * E:h * E + 1], dlb[:, h * E:h * E + 1]
                pc = jnp.where(mc, jnp.exp(_dot_nt(qh, kc[:, sl]) - lse_h), 0.0)
                pp = jnp.where(mp, jnp.exp(_dot_nt(qh, kp[:, sl]) - lse_h), 0.0)
                dsc = pc * (_dot_nt(doh, vc[:, sl]) - dl_h)
                dsp = pp * (_dot_nt(doh, vp[:, sl]) - dl_h)
                dqs.append(_dot(dsc, kc[:, sl]) + _dot(dsp, kp[:, sl]))
                dkc.append(_dot_tn(dsc, qh))
                dkp.append(_dot_tn(dsp, qh))
                dvc.append(_dot_tn(pc, doh))
                dvp.append(_dot_tn(pp, doh))
            dq_ref[cur, :] = jnp.concatenate(dqs, axis=1)
            dk_ref[cur, :] = dk_ref[cur, :] + jnp.concatenate(dkc, axis=1)
            dv_ref[cur, :] = dv_ref[cur, :] + jnp.concatenate(dvc, axis=1)
            dk_ref[prv, :] = dk_ref[prv, :] + jnp.concatenate(dkp, axis=1)
            dv_ref[prv, :] = dv_ref[prv, :] + jnp.concatenate(dvp, axis=1)
            return carry

        lax.fori_loop(0, nblk, step, 0, unroll=4)

    col = pl.BlockSpec((L, ATT_COLS), lambda c: (0, c))
    return pl.pallas_call(body, grid=(W // ATT_COLS,), in_specs=_qkv_specs(L, g) + [col] * 3, out_specs=[col] * 3,
                          out_shape=[jax.ShapeDtypeStruct((L, W), F32)] * 3,
                          compiler_params=_params(dimension_semantics=("parallel",)), name=name)(
        qkv, qkv, qkv, lse, do, dl)


def _merge_alpha(l_refs):
    ls = [r[...] for r in l_refs]
    m = jnp.maximum(jnp.maximum(ls[0], ls[1]), ls[2])
    es = [jnp.exp(l - m) for l in ls]
    den = es[0] + es[1] + es[2]
    return [e / den for e in es]


def merge_fwd(os_, ls_, *, name):
    L, W = os_[0].shape
    tr = _pick(L, (256, 128))

    def body(o0, o1, o2, l0, l1, l2, out_ref):
        al = _merge_alpha((l0, l1, l2))
        out_ref[...] = (al[0] * o0[...] + al[1] * o1[...] + al[2] * o2[...]).astype(out_ref.dtype)

    row = pl.BlockSpec((tr, W), lambda i: (i, 0))
    return pl.pallas_call(body, grid=(L // tr,), in_specs=[row] * 6, out_specs=row,
                          out_shape=jax.ShapeDtypeStruct((L, W), BF16), name=name)(*os_, *ls_)


def merge_bwd(os_, ls_, do, *, name):
    L, W = do.shape
    tr = _pick(L, (256, 128))

    def body(o0, o1, o2, l0, l1, l2, do_ref, d0, d1, d2, e0, e1, e2):
        al = _merge_alpha((l0, l1, l2))
        dov = do_ref[...]
        r = lax.broadcasted_iota(jnp.int32, (W, W), 0) // ATT_E
        c = lax.broadcasted_iota(jnp.int32, (W, W), 1) // ATT_E
        ones_blk = (r == c).astype(F32)
        t = jnp.zeros_like(dov)
        for a, o in zip(al, (o0, o1, o2)):
            t = t + a * _dot_tri(ones_blk, dov * o[...], tri_left=False)
        for a, d_ref, e_ref in zip(al, (d0, d1, d2), (e0, e1, e2)):
            d_ref[...] = a * dov
            e_ref[...] = a * t

    row = pl.BlockSpec((tr, W), lambda i: (i, 0))
    return pl.pallas_call(body, grid=(L // tr,), in_specs=[row] * 7, out_specs=[row] * 6,
                          out_shape=[jax.ShapeDtypeStruct((L, W), F32)] * 6, name=name)(*os_, *ls_, do)


def _me_and_peers():
    x, y, c = lax.axis_index("x"), lax.axis_index("y"), lax.axis_index("c")
    peers = []
    for k in range(1, N_DEV):
        px = 1 - x if k & 4 else x
        py = 1 - y if k & 2 else y
        pc = 1 - c if k & 1 else c
        peers.append((px, py, pc))
    return (x, y, c), peers


def _index(dev):
    return 4 * dev[0] + 2 * dev[1] + dev[2]


def _hbm(a):
    return pltpu.with_memory_space_constraint(a, pltpu.HBM)


HBM_SPEC = pl.BlockSpec(memory_space=pltpu.HBM)
SEM_SPEC = pl.BlockSpec(memory_space=pltpu.SEMAPHORE)
DATAFLOW = pltpu.SideEffectType.DATAFLOW_SIDE_EFFECTING


def _remote(src_ref, land_ref, slotted, me, peer, src_is_mine, send_sem, recv_sem, k):
    sender, receiver = (me, peer) if src_is_mine else (peer, me)
    src = src_ref.at[_index(receiver)] if slotted else src_ref
    return pltpu.make_async_remote_copy(src_ref=src, dst_ref=land_ref.at[_index(sender)], send_sem=send_sem.at[k],
                                        recv_sem=recv_sem.at[k], device_id=peer, device_id_type=MESH_ID)


SIBLING = 0
SAME_CORE = (1, 3, 5)
OTHER_CORE = (2, 4, 6)


def copies_start(arrays, mode, *, name):
    n = len(arrays)
    slotted = mode == "exchange"
    lands = [lax.empty(a.shape if slotted else (N_DEV,) + a.shape, a.dtype) for a in arrays]
    targets = (SIBLING,) + SAME_CORE if mode == "gather2" else tuple(range(N_DEV - 1))

    def body(*refs):
        x_refs, land_refs = refs[:n], refs[n:2 * n]
        send, recv = refs[2 * n:3 * n], refs[3 * n:4 * n]
        token = refs[-1]
        me, peers = _me_and_peers()
        for w in range(n):
            for k in targets:
                _remote(x_refs[w], land_refs[w], slotted, me, peers[k], True, send[w], recv[w], k).start()
            if not slotted:
                pltpu.make_async_copy(x_refs[w], land_refs[w].at[_index(me)], recv[w].at[N_DEV - 1]).start()
        token[...] = jnp.zeros_like(token)

    sem = pltpu.SemaphoreType.DMA((N_DEV,))
    out_shape = ([sem] * (2 * n) + [pltpu.HBM(a.shape, a.dtype) for a in arrays]
                 + [pltpu.HBM(l.shape, l.dtype) for l in lands] + [jax.ShapeDtypeStruct((8, 128), F32)])
    outs = pl.pallas_call(
        body, name=name, out_shape=out_shape, in_specs=[HBM_SPEC] * (2 * n),
        out_specs=[SEM_SPEC] * (2 * n) + [HBM_SPEC] * (2 * n) + [pl.BlockSpec(memory_space=pltpu.VMEM)],
        input_output_aliases={i: 2 * n + i for i in range(2 * n)},
        compiler_params=pltpu.CompilerParams(has_side_effects=DATAFLOW),
    )(*[_hbm(a) for a in arrays], *[_hbm(l) for l in lands])
    handles = [(outs[w], outs[n + w], outs[2 * n + w], outs[3 * n + w]) for w in range(n)]
    return handles, outs[-1]


def _forward(land_ref, me, peers, j, fsend, frecv, mine):
    block = _index(peers[SAME_CORE[j]] if mine else peers[OTHER_CORE[j]])
    return pltpu.make_async_remote_copy(src_ref=land_ref.at[block], dst_ref=land_ref.at[block], send_sem=fsend.at[j],
                                        recv_sem=frecv.at[j], device_id=peers[SIBLING], device_id_type=MESH_ID)


def copies_forward(handles, after, *, name):
    n = len(handles)

    def body(*refs):
        land_refs, recv = refs[:n], refs[n:2 * n]
        fsend, frecv = refs[2 * n + 1:3 * n + 1], refs[3 * n + 1:4 * n + 1]
        token = refs[-1]
        me, peers = _me_and_peers()
        for w in range(n):
            for j, k in enumerate(SAME_CORE):
                block = land_refs[w].at[_index(peers[k])]
                pltpu.make_async_remote_copy(src_ref=block, dst_ref=block, send_sem=recv[w].at[N_DEV - 1],
                                             recv_sem=recv[w].at[k], device_id=peers[k], device_id_type=MESH_ID).wait_recv()
                _forward(land_refs[w], me, peers, j, fsend[w], frecv[w], True).start()
        token[...] = jnp.zeros_like(token)

    sem = pltpu.SemaphoreType.DMA((len(SAME_CORE),))
    lands = [h[3] for h in handles]
    outs = pl.pallas_call(
        body, name=name,
        out_shape=[sem] * (2 * n) + [pltpu.HBM(l.shape, l.dtype) for l in lands] + [jax.ShapeDtypeStruct((8, 128), F32)],
        in_specs=[HBM_SPEC] * n + [SEM_SPEC] * n + [pl.BlockSpec(memory_space=pl.ANY)],
        out_specs=[SEM_SPEC] * (2 * n) + [HBM_SPEC] * n + [pl.BlockSpec(memory_space=pltpu.VMEM)],
        input_output_aliases={w: 2 * n + w for w in range(n)},
        compiler_params=pltpu.CompilerParams(has_side_effects=DATAFLOW),
    )(*lands, *[h[1] for h in handles], after)
    new = [(h[0], h[1], h[2], outs[2 * n + w], outs[w], outs[n + w]) for w, h in enumerate(handles)]
    return new, outs[-1]


def copies_wait(handle, mode, after, *, name):
    slotted = mode == "exchange"
    two_level = mode == "gather2"
    send_sem, recv_sem, x_thru, land_thru = handle[:4]
    targets = (SIBLING,) + SAME_CORE if two_level else tuple(range(N_DEV - 1))
    arrivals = (SIBLING,) if two_level else targets

    def body(x_ref, land_ref, send_ref, recv_ref, *rest):
        me, peers = _me_and_peers()
        for k in targets:
            _remote(x_ref, land_ref, slotted, me, peers[k], True, send_ref, recv_ref, k).wait_send()
        for k in arrivals:
            _remote(x_ref, land_ref, slotted, me, peers[k], False, send_ref, recv_ref, k).wait_recv()
        if not slotted:
            pltpu.make_async_copy(x_ref, land_ref.at[_index(me)], recv_ref.at[N_DEV - 1]).wait()
        if two_level:
            fsend, frecv = rest[0], rest[1]
            for j in range(len(SAME_CORE)):
                _forward(land_ref, me, peers, j, fsend, frecv, True).wait_send()
                _forward(land_ref, me, peers, j, fsend, frecv, False).wait_recv()

    extra = list(handle[4:])
    return pl.pallas_call(
        body, name=name, out_shape=(pltpu.HBM(x_thru.shape, x_thru.dtype), pltpu.HBM(land_thru.shape, land_thru.dtype)),
        in_specs=[HBM_SPEC, HBM_SPEC, SEM_SPEC, SEM_SPEC] + [SEM_SPEC] * len(extra) + [pl.BlockSpec(memory_space=pl.ANY)],
        out_specs=(HBM_SPEC, HBM_SPEC), input_output_aliases={0: 0, 1: 1},
        compiler_params=pltpu.CompilerParams(has_side_effects=DATAFLOW),
    )(x_thru, land_thru, send_sem, recv_sem, *extra, after)


def cast_bf16(x, *, dep=None, name):
    R, C = x.shape
    tr = _pick(R, (512, 352, 256, 128, 64))
    deps = [] if dep is None else [dep]

    def body(x_ref, *rest):
        rest[-1][...] = x_ref[...].astype(BF16)

    row = pl.BlockSpec((tr, C), lambda i: (i, 0))
    return pl.pallas_call(body, grid=(R // tr,), in_specs=[row] + [pl.BlockSpec((8, 128), lambda i: (0, 0))] * len(deps),
                          out_specs=row, out_shape=jax.ShapeDtypeStruct((R, C), BF16), name=name)(x, *deps)


def cast_bf16_layer(x3, layer, *, name):
    _, R, C = x3.shape
    tr = _pick(R, (512, 352, 256, 128, 64))

    def body(x_ref, o_ref):
        o_ref[...] = x_ref[...].astype(BF16)

    return pl.pallas_call(body, grid=(R // tr,), in_specs=[pl.BlockSpec((None, tr, C), lambda i: (layer, i, 0))],
                          out_specs=pl.BlockSpec((tr, C), lambda i: (i, 0)),
                          out_shape=jax.ShapeDtypeStruct((R, C), BF16), name=name)(x3)


BD_PARTS = 4


def _blockdiag_call(b, build, G, r, c, name):
    gp = G // BD_PARTS

    def body_build(b_ref, o_ref):
        o_ref[...] = jnp.zeros_like(o_ref)
        for g in range(G):
            o_ref[g // gp, (g % gp) * r:(g % gp + 1) * r, (g % gp) * c:(g % gp + 1) * c] = b_ref[g]

    def body_extract(d_ref, o_ref):
        for g in range(G):
            o_ref[g] = d_ref[g // gp, (g % gp) * r:(g % gp + 1) * r, (g % gp) * c:(g % gp + 1) * c]

    out = jax.ShapeDtypeStruct((BD_PARTS, gp * r, gp * c) if build else (G, r, c), F32)
    return pl.pallas_call(body_build if build else body_extract, out_shape=out, name=name)(b)


def make_blockdiag(G, r, c, name):
    @jax.custom_vjp
    def blockdiag(b):
        return _blockdiag_call(b, True, G, r, c, name + "_build")

    def fwd(b):
        return blockdiag(b), None

    def bwd(_, g):
        return (_blockdiag_call(g, False, G, r, c, name + "_extract"),)

    blockdiag.defvjp(fwd, bwd)
    return blockdiag


def cols_from_shards(g, *, name):
    _, K, n = g.shape
    tk = _pick(K, (256, 128))

    def body(g_ref, o_ref):
        for i in range(N_DEV):
            o_ref[:, i * n:(i + 1) * n] = g_ref[i]

    return pl.pallas_call(body, grid=(K // tk,), in_specs=[pl.BlockSpec((N_DEV, tk, n), lambda i: (0, i, 0))],
                          out_specs=pl.BlockSpec((tk, N_DEV * n), lambda i: (i, 0)),
                          out_shape=jax.ShapeDtypeStruct((K, N_DEV * n), g.dtype), name=name)(g)


def shards_from_cols(w, *, name):
    K, N = w.shape
    n = N // N_DEV
    tk = _pick(K, (256, 128))

    def body(w_ref, o_ref):
        for i in range(N_DEV):
            o_ref[i] = w_ref[:, i * n:(i + 1) * n].astype(o_ref.dtype)

    return pl.pallas_call(body, grid=(K // tk,), in_specs=[pl.BlockSpec((tk, N), lambda i: (i, 0))],
                          out_specs=pl.BlockSpec((N_DEV, tk, n), lambda i: (0, i, 0)),
                          out_shape=jax.ShapeDtypeStruct((N_DEV, K, n), BF16), name=name)(w)


def _adamw(w, g, m, v):
    m = ADAM_B1 * m + (1.0 - ADAM_B1) * g
    v = ADAM_B2 * v + (1.0 - ADAM_B2) * (g * g)
    m_hat = m / (1.0 - ADAM_B1 ** ADAM_STEP)
    v_hat = v / (1.0 - ADAM_B2 ** ADAM_STEP)
    delta = -ADAM_LR * (m_hat / (jnp.sqrt(v_hat) + ADAM_EPS) + ADAM_WD * w)
    return delta, m, v


def reduce_adamw(recv, own, own_slotted, me, w, m, v, *, layer=0, n_layers=1, into=None, name):
    _, R, C = recv.shape
    tr = _pick(R, (352, 320, 288, 256, 128, 64, 32, 16, 8))
    off = layer * (R // tr)

    def body(me_ref, r_ref, own_ref, w_ref, m_ref, v_ref, *rest):
        g_ref, d_ref, nm_ref, nv_ref = rest[-4:]
        mine = me_ref[0]
        g = None
        for i in range(N_DEV):
            part = jnp.where(mine == i, own_ref[...], r_ref[i]).astype(F32)
            g = part if g is None else g + part
        delta, nm, nv = _adamw(w_ref[...], g, m_ref[...], v_ref[...])
        g_ref[...] = g
        d_ref[...] = delta
        nm_ref[...] = nm
        nv_ref[...] = nv

    row = pl.BlockSpec((tr, C), lambda i, me_ref: (i + off, 0))
    own_spec = (pl.BlockSpec((None, tr, C), lambda i, me_ref: (me_ref[0], i, 0)) if own_slotted
                else pl.BlockSpec((tr, C), lambda i, me_ref: (i, 0)))
    rest = [] if into is None else list(into)
    grid_spec = pltpu.PrefetchScalarGridSpec(
        num_scalar_prefetch=1, grid=(R // tr,),
        in_specs=[pl.BlockSpec((N_DEV, tr, C), lambda i, me_ref: (0, i, 0)), own_spec, row, row, row]
        + [pl.BlockSpec(memory_space=pl.ANY)] * len(rest),
        out_specs=[row] * 4)
    return pl.pallas_call(body, grid_spec=grid_spec, out_shape=[jax.ShapeDtypeStruct((n_layers * R, C), F32)] * 4,
                          input_output_aliases={6 + k: k for k in range(len(rest))},
                          compiler_params=_params(dimension_semantics=("parallel",)), name=name)(
        me.reshape(1).astype(jnp.int32), recv, own, w, m, v, *rest)


def _s5_prepare(A_re, A_im, log_dt, B_re, B_im, C_re, C_im):
    G, P, Cg = S5_GROUPS, S5_STATE, S5_GROUP
    dt = jnp.exp(log_dt)[:, None]
    mag = jnp.exp(A_re * dt)
    ab_re = mag * jnp.cos(A_im * dt)
    ab_im = mag * jnp.sin(A_im * dt)
    den = A_re * A_re + A_im * A_im
    nr, ni = ab_re - 1.0, ab_im
    c_re = (nr * A_re + ni * A_im) / den
    c_im = (ni * A_re - nr * A_im) / den
    Bb_re = c_re[..., None] * B_re - c_im[..., None] * B_im
    Bb_im = c_re[..., None] * B_im + c_im[..., None] * B_re
    def dense_in(b, name):
        return make_blockdiag(G, Cg, P, name)(b.transpose(0, 2, 1))

    def dense_out(c, name):
        return make_blockdiag(G, P, Cg, name)(c.transpose(0, 2, 1))

    return (ab_re.reshape(1, G * P), ab_im.reshape(1, G * P), dense_in(Bb_re, "s5_wb_re"), dense_in(Bb_im, "s5_wb_im"),
            dense_out(C_re, "s5_wc_re"), dense_out(-C_im, "s5_wc_im"))


def _lower_bound(gamma):
    return jnp.cumsum(jax.nn.softmax(gamma, axis=0), axis=0)[0:1]


def _ffn_fwd(h, g_norm, get_w_in, conv_w, conv_b, get_w_out, tag):
    w_in = get_w_in(h)
    hn, hu = norm_mm(h, g_norm, w_in, name=tag + "_in")
    act = convgate_fwd(hu, conv_w, conv_b, name=tag + "_gate")
    w_out = get_w_out(act)
    h_out = mm(act, w_out, res=h, name=tag + "_out")
    return h_out, (hn, hu, act), w_in, w_out


def _ffn_bwd(h, g_norm, w_in, conv_w, conv_b, w_out, saved, dh, tag, send_dw_in, send_dw_out):
    hn, hu, act = saved
    sent = send_dw_out(mm(act, dh, ta=True, out_dtype=BF16, name=tag + "_dwout"))
    dact = mm(dh, w_out, tb=True, dep=sent, name=tag + "_dact")
    (dhu_a, dhu_b), dconv_w, dconv_b = convgate_bwd(hu, conv_w, conv_b, dact, name=tag + "_dgate")
    rows = 2 * dhu_a.shape[1]
    dw_in = mm(dhu_a, hn, ta=True, out_dtype=BF16, out_rows=rows, name=tag + "_dwin_a")
    dw_in = mm(dhu_b, hn, ta=True, out_dtype=BF16, out_rows=rows, out_off=rows // 2, into=dw_in, name=tag + "_dwin_b")
    sent = send_dw_in(dw_in)
    dh_in, dg = mm_drms((dhu_a, dhu_b), w_in, h, g_norm, dh, dep=sent, name=tag + "_dhn")
    return dh_in, dg, dconv_w, dconv_b


def kernel(x, positions, norm_mix, norm_ffn, norm_final, mix_w_in, mix_w_out, s5_A_re, s5_A_im, s5_log_dt, s5_B_re, s5_B_im, s5_C_re, s5_C_im, s5_D, s5_glu_w, s5_glu_b, hgrn_gamma, hgrn_norm, att_w_qkv, att_w_o, ffn_w_in, ffn_conv_w, ffn_conv_b, ffn_w_out, loss_target, m_norm_mix, m_norm_ffn, m_norm_final, m_mix_w_in, m_mix_w_out, m_s5_A_re, m_s5_A_im, m_s5_log_dt, m_s5_B_re, m_s5_B_im, m_s5_C_re, m_s5_C_im, m_s5_D, m_s5_glu_w, m_s5_glu_b, m_hgrn_gamma, m_hgrn_norm, m_att_w_qkv, m_att_w_o, m_ffn_w_in, m_ffn_conv_w, m_ffn_conv_b, m_ffn_w_out, v_norm_mix, v_norm_ffn, v_norm_final, v_mix_w_in, v_mix_w_out, v_s5_A_re, v_s5_A_im, v_s5_log_dt, v_s5_B_re, v_s5_B_im, v_s5_C_re, v_s5_C_im, v_s5_D, v_s5_glu_w, v_s5_glu_b, v_hgrn_gamma, v_hgrn_norm, v_att_w_qkv, v_att_w_o, v_ffn_w_in, v_ffn_conv_w, v_ffn_conv_b, v_ffn_w_out):
    W = dict(norm_mix=norm_mix, norm_ffn=norm_ffn, norm_final=norm_final, mix_w_in=mix_w_in, mix_w_out=mix_w_out,
             s5_A_re=s5_A_re, s5_A_im=s5_A_im, s5_log_dt=s5_log_dt, s5_B_re=s5_B_re, s5_B_im=s5_B_im,
             s5_C_re=s5_C_re, s5_C_im=s5_C_im, s5_D=s5_D, s5_glu_w=s5_glu_w, s5_glu_b=s5_glu_b,
             hgrn_gamma=hgrn_gamma, hgrn_norm=hgrn_norm, att_w_qkv=att_w_qkv, att_w_o=att_w_o, ffn_w_in=ffn_w_in,
             ffn_conv_w=ffn_conv_w, ffn_conv_b=ffn_conv_b, ffn_w_out=ffn_w_out)
    M = dict(norm_mix=m_norm_mix, norm_ffn=m_norm_ffn, norm_final=m_norm_final, mix_w_in=m_mix_w_in,
             mix_w_out=m_mix_w_out, s5_A_re=m_s5_A_re, s5_A_im=m_s5_A_im, s5_log_dt=m_s5_log_dt, s5_B_re=m_s5_B_re,
             s5_B_im=m_s5_B_im, s5_C_re=m_s5_C_re, s5_C_im=m_s5_C_im, s5_D=m_s5_D, s5_glu_w=m_s5_glu_w,
             s5_glu_b=m_s5_glu_b, hgrn_gamma=m_hgrn_gamma, hgrn_norm=m_hgrn_norm, att_w_qkv=m_att_w_qkv,
             att_w_o=m_att_w_o, ffn_w_in=m_ffn_w_in, ffn_conv_w=m_ffn_conv_w, ffn_conv_b=m_ffn_conv_b,
             ffn_w_out=m_ffn_w_out)
    V = dict(norm_mix=v_norm_mix, norm_ffn=v_norm_ffn, norm_final=v_norm_final, mix_w_in=v_mix_w_in,
             mix_w_out=v_mix_w_out, s5_A_re=v_s5_A_re, s5_A_im=v_s5_A_im, s5_log_dt=v_s5_log_dt, s5_B_re=v_s5_B_re,
             s5_B_im=v_s5_B_im, s5_C_re=v_s5_C_re, s5_C_im=v_s5_C_im, s5_D=v_s5_D, s5_glu_w=v_s5_glu_w,
             s5_glu_b=v_s5_glu_b, hgrn_gamma=v_hgrn_gamma, hgrn_norm=v_hgrn_norm, att_w_qkv=v_att_w_qkv,
             att_w_o=v_att_w_o, ffn_w_in=v_ffn_w_in, ffn_conv_w=v_ffn_conv_w, ffn_conv_b=v_ffn_conv_b,
             ffn_w_out=v_ffn_w_out)
    return _step(x[0], positions[0], loss_target[0], W, M, V)


TRANSPOSED = ("mix_w_in", "att_w_qkv", "ffn_w_in")
SMALL = ("norm_mix", "norm_ffn", "norm_final", "s5_A_re", "s5_A_im", "s5_log_dt", "s5_B_re", "s5_B_im", "s5_C_re",
         "s5_C_im", "s5_D", "s5_glu_b", "hgrn_gamma", "hgrn_norm", "ffn_conv_b")
ORDER = ("norm_mix", "norm_ffn", "norm_final", "mix_w_in", "mix_w_out", "s5_A_re", "s5_A_im", "s5_log_dt", "s5_B_re",
         "s5_B_im", "s5_C_re", "s5_C_im", "s5_D", "s5_glu_w", "s5_glu_b", "hgrn_gamma", "hgrn_norm", "att_w_qkv",
         "att_w_o", "ffn_w_in", "ffn_conv_w", "ffn_conv_b", "ffn_w_out")
PACK_COLS = 1024


def _step(x, positions, target, W, M, V):
    L, D = x.shape
    me = 4 * lax.axis_index("x") + 2 * lax.axis_index("y") + lax.axis_index("c")
    n_cw = W["ffn_conv_w"].shape[-1]
    T = {n: tuple(jnp.swapaxes(d[n], -1, -2) for d in (W, M, V)) for n in TRANSPOSED}
    first = {
        "mix_w_in": cast_bf16(T["mix_w_in"][0][0], name="mix_w_in_cast"),
        "conv_w": W["ffn_conv_w"].reshape(6, n_cw),
        "s5_glu_w": cast_bf16(W["s5_glu_w"][0], name="s5_glu_w_cast"),
    }
    first_handles, token = copies_start(list(first.values()), "gather2", name="gather_start_first")
    shards = {
        "mix_w_out": cast_bf16(W["mix_w_out"][0], dep=token, name="mix_w_out_cast"),
        "ffn_w_in0": cast_bf16_layer(T["ffn_w_in"][0], 0, name="ffn_w_in0_cast"),
        "ffn_w_out0": cast_bf16_layer(W["ffn_w_out"], 0, name="ffn_w_out0_cast"),
        "att_w_qkv": cast_bf16(T["att_w_qkv"][0][0], name="att_w_qkv_cast"),
        "att_w_o": cast_bf16(W["att_w_o"][0], name="att_w_o_cast"),
        "ffn_w_in1": cast_bf16_layer(T["ffn_w_in"][0], 1, name="ffn_w_in1_cast"),
        "ffn_w_out1": cast_bf16_layer(W["ffn_w_out"], 1, name="ffn_w_out1_cast"),
    }
    gather_handles, token = copies_start(list(shards.values()), "gather2", name="gather_start")
    gather_handle = dict(zip(list(first) + list(shards), first_handles + gather_handles))

    def forward(keys, after, name):
        new, sent = copies_forward([gather_handle[k] for k in keys], after, name=name)
        gather_handle.update(zip(keys, new))
        return sent

    def gathered(key, after, cols):
        _, land = copies_wait(gather_handle[key], "gather2", after, name=key + "_gwait")
        return cols_from_shards(land, name=key + "_asm") if cols else land.reshape(-1, land.shape[-1])

    conv_b = W["ffn_conv_b"].reshape(2, 1, -1)

    s5_params = (W["s5_A_re"][0], W["s5_A_im"][0], W["s5_log_dt"][0], W["s5_B_re"][0], W["s5_B_im"][0],
                 W["s5_C_re"][0], W["s5_C_im"][0])
    (a_re, a_im, wb_re, wb_im, wc_re, wc_im), s5_prep_vjp = jax.vjp(_s5_prepare, *s5_params)
    dvec = W["s5_D"].reshape(1, S5_WIDTH)
    glu_b = W["s5_glu_b"].reshape(1, S5_WIDTH)
    lb, lb_vjp = jax.vjp(_lower_bound, W["hgrn_gamma"])
    hg_norm = W["hgrn_norm"].reshape(1, -1)
    tabs = rope_tables(positions)

    sent = forward(["mix_w_in", "conv_w", "s5_glu_w"], token, "forward_a")
    w_mix_in = gathered("mix_w_in", sent, False)
    hn0, proj = norm_mm(x, W["norm_mix"][0], w_mix_in, name="l0_proj")
    y0, xs_re, xs_im = s5_core_fwd(proj, a_re, a_im, wb_re, wb_im, wc_re, wc_im, name="s5_core")
    w_glu = gathered("s5_glu_w", y0, False)
    cat = s5_out_fwd(y0, proj, dvec, w_glu, glu_b, name="s5_out")
    cat, hg_states = hgrn_fwd(proj, lb, hg_norm, cat, name="hgrn_fwd")
    forward(["mix_w_out"], cat, "forward_b")
    w_mix_out = gathered("mix_w_out", cat, False)
    h1 = mm(cat, w_mix_out, res=x, name="l0_mix_out")
    _, cw_all = copies_wait(gather_handle["conv_w"], "gather2", h1, name="conv_w_gwait")
    conv_w = cw_all.transpose(1, 0, 2).reshape(2, 3, N_DEV * n_cw)
    w_ffn_in, w_ffn_out = [None, None], [None, None]
    h2, ffn0_saved, w_ffn_in[0], w_ffn_out[0] = _ffn_fwd(
        h1, W["norm_ffn"][0],
        lambda a: (forward(["ffn_w_in0"], a, "forward_b2"), gathered("ffn_w_in0", a, False))[1], conv_w[0], conv_b[0],
        lambda a: (forward(["ffn_w_out0"], a, "forward_c"), gathered("ffn_w_out0", a, False))[1], "ffn0")

    forward(["att_w_qkv", "att_w_o"], h2, "forward_d")
    w_qkv = gathered("att_w_qkv", h2, False)
    hn2, qkv_r = norm_mm(h2, W["norm_mix"][1], w_qkv, tabs=tabs, name="l1_qkv")
    att_o, att_l = [], []
    for g, d in enumerate(ATT_DILATIONS):
        o_g, l_g = attn_fwd(qkv_r, g, d, name=f"attn_fwd{g}")
        att_o.append(o_g)
        att_l.append(l_g)
    o_att = merge_fwd(att_o, att_l, name="merge_fwd")
    forward(["ffn_w_in1", "ffn_w_out1"], o_att, "forward_e")
    w_o = gathered("att_w_o", o_att, True)
    h3 = mm(o_att, w_o, res=h2, name="l1_mix_out")
    h4, ffn1_saved, w_ffn_in[1], w_ffn_out[1] = _ffn_fwd(
        h3, W["norm_ffn"][1], lambda a: gathered("ffn_w_in1", a, False), conv_w[1], conv_b[1],
        lambda a: gathered("ffn_w_out1", a, False), "ffn1")

    exchanges = {}

    pending = []

    def send_grad(key, g, cols, flush=True):
        if cols:
            parts = shards_from_cols(g, name=key + "_split")
        else:
            parts = g.reshape(N_DEV, g.shape[0] // N_DEV, g.shape[1])
        pending.append((key, parts))
        if not flush:
            return None
        handles, sent = copies_start([p for _, p in pending], "exchange", name=key + "_xstart")
        exchanges.update(zip([k for k, _ in pending], handles))
        pending.clear()
        return sent

    loss, dh4, dg_final = final_loss(h4, W["norm_final"], target, name="final_loss")
    dh3, dg_ffn1, dcw1, dcb1 = _ffn_bwd(h3, W["norm_ffn"][1], w_ffn_in[1], conv_w[1], conv_b[1], w_ffn_out[1],
                                        ffn1_saved, dh4, "ffn1", lambda g: send_grad("ffn_w_in1", g, False),
                                        lambda g: send_grad("ffn_w_out1", g, False, flush=False))
    sent = send_grad("att_w_o", mm(o_att, dh3, ta=True, name="l1_dwo"), True, flush=False)
    d_oatt = mm(dh3, w_o, tb=True, dep=sent, name="l1_dmix")
    mb = merge_bwd(att_o, att_l, d_oatt, name="merge_bwd")
    d_slabs = [attn_bwd(qkv_r, g, att_l[g], mb[g], mb[3 + g], d, name=f"attn_bwd{g}")
               for g, d in enumerate(ATT_DILATIONS)]
    d_qkv = rope_bwd([s[0] for s in d_slabs] + [s[1] for s in d_slabs] + [s[2] for s in d_slabs], tabs,
                     name="rope_bwd")
    sent = send_grad("att_w_qkv", mm(d_qkv, hn2, ta=True, out_dtype=BF16, name="l1_dwqkv"), False)
    dh2, dg_mix1 = mm_drms(d_qkv, w_qkv, h2, W["norm_mix"][1], dh3, dep=sent, name="l1_dhn")

    dh1, dg_ffn0, dcw0, dcb0 = _ffn_bwd(h1, W["norm_ffn"][0], w_ffn_in[0], conv_w[0], conv_b[0], w_ffn_out[0],
                                        ffn0_saved, dh2, "ffn0", lambda g: send_grad("ffn_w_in0", g, False),
                                        lambda g: send_grad("ffn_w_out0", g, False, flush=False))
    sent = send_grad("mix_w_out", mm(cat, dh1, ta=True, out_dtype=BF16, name="l0_dwout"), False)
    dcat = mm(dh1, w_mix_out, tb=True, dep=sent, name="l0_dcat")
    dy, du_d, z_bf, dzg, dglu_b, dD = s5_out_bwd(y0, proj, dvec, w_glu, glu_b, dcat, name="s5_dout")
    sent_glu = send_grad("s5_glu_w", mm(z_bf, dzg, ta=True, out_dtype=BF16, name="s5_dglu"), False, flush=False)
    du, dwb_re, dwb_im, dwc_re, dwc_im, da_re, da_im = s5_core_bwd(
        dy, du_d, proj, xs_re, xs_im, a_re, a_im, wb_re, wb_im, wc_re, wc_im, name="s5_dcore")
    s5_small = s5_prep_vjp((da_re, da_im, dwb_re, dwb_im, dwc_re, dwc_im))
    d_proj, dlb, dhg_norm = hgrn_bwd(proj, lb, hg_norm, hg_states, dcat, du, name="hgrn_bwd")
    sent = send_grad("mix_w_in", mm(d_proj, hn0, ta=True, out_dtype=BF16, dep=sent_glu, name="l0_dwin"), False)
    grad_x, dg_mix0 = mm_drms(d_proj, w_mix_in, x, W["norm_mix"][0], dh1, dep=sent, name="l0_dhn")
    (d_gamma,) = lb_vjp(dlb)
    out = {}

    dA_re, dA_im, dlog_dt, dB_re, dB_im, dC_re, dC_im = s5_small
    small_g = dict(norm_mix=jnp.concatenate([dg_mix0, dg_mix1], axis=0), norm_ffn=jnp.concatenate([dg_ffn0, dg_ffn1], axis=0),
                   norm_final=dg_final, s5_A_re=dA_re, s5_A_im=dA_im, s5_log_dt=dlog_dt, s5_B_re=dB_re, s5_B_im=dB_im,
                   s5_C_re=dC_re, s5_C_im=dC_im, s5_D=dD, s5_glu_b=dglu_b, hgrn_gamma=d_gamma, hgrn_norm=dhg_norm,
                   ffn_conv_b=jnp.concatenate([dcb0, dcb1], axis=0))
    conv_w_g = jnp.stack([dcw0, dcw1], axis=0)
    sizes = [math.prod(W[n].shape) for n in SMALL]
    n_conv = conv_w_g.size
    total = sum(sizes) + n_conv + 1
    rows = -(-total // PACK_COLS)
    rows = -(-rows // 8) * 8
    pad = rows * PACK_COLS - total

    def pack(vals, conv_part, last):
        flat = [v.reshape(-1).astype(F32) for v in vals] + [conv_part.reshape(-1), last.reshape(-1),
                                                            jnp.zeros((pad,), F32)]
        return jnp.concatenate(flat).reshape(rows, PACK_COLS)

    def conv_full(shard):
        col_owner = lax.broadcasted_iota(jnp.int32, (2, 3, N_DEV * n_cw), 2) // n_cw
        return jnp.where(col_owner == me, jnp.tile(shard, (1, 1, N_DEV)), 0.0)

    zero1 = jnp.zeros((1,), F32)
    g_pack = pack([small_g[n] for n in SMALL], conv_w_g, loss)
    w_pack = pack([W[n] for n in SMALL], conv_full(W["ffn_conv_w"]), zero1)
    m_pack = pack([M[n] for n in SMALL], conv_full(M["ffn_conv_w"]), zero1)
    v_pack = pack([V[n] for n in SMALL], conv_full(V["ffn_conv_w"]), zero1 + 1.0)
    (small_handle,), small_sent = copies_start([g_pack], "gather", name="small_xstart")

    def finish(name, n_layers):
        w3, m3, v3 = T[name] if name in TRANSPOSED else (W[name], M[name], V[name])
        res = None
        for layer in reversed(range(n_layers)):
            key = name if n_layers == 1 else f"{name}{layer}"
            own, recv = copies_wait(exchanges[key], "exchange", small_sent, name=key + "_xwait")
            _, R, Cn = recv.shape
            res = reduce_adamw(recv, own, True, me, w3.reshape(n_layers * R, Cn), m3.reshape(n_layers * R, Cn),
                               v3.reshape(n_layers * R, Cn), layer=layer, n_layers=n_layers, into=res,
                               name=key + "_adamw")
        res = [r.reshape(w3.shape) for r in res]
        return tuple(jnp.swapaxes(r, -1, -2) for r in res) if name in TRANSPOSED else tuple(res)

    for name in ("ffn_w_out", "ffn_w_in"):
        out[name] = finish(name, 2)
    for name in ("att_w_o", "att_w_qkv", "mix_w_out", "s5_glu_w", "mix_w_in"):
        out[name] = finish(name, 1)

    small_own, small_recv = copies_wait(small_handle, "gather", out["s5_glu_w"][0], name="small_xwait")
    res = reduce_adamw(small_recv, small_own, False, me, w_pack, m_pack, v_pack, name="small_adamw")
    flat = [r.reshape(-1) for r in res]
    off = 0
    for n, sz in zip(SMALL, sizes):
        out[n] = tuple(f[off:off + sz].reshape(W[n].shape) for f in flat)
        off += sz
    conv_res = [f[off:off + n_conv].reshape(2, 3, N_DEV * n_cw) for f in flat]
    out["ffn_conv_w"] = tuple(lax.dynamic_slice(c, (0, 0, me * n_cw), (2, 3, n_cw)) for c in conv_res)
    off += n_conv
    loss_total = flat[0][off]

    result = [loss_total, grad_x[None]]
    for k in range(4):
        result += [out[n][k] for n in ORDER]
    return tuple(result)
```

```python
import math

import jax
import jax.numpy as jnp
from jax import lax
from jax.experimental import pallas as pl
from jax.experimental.pallas import tpu as pltpu

F32 = jnp.float32
BF16 = jnp.bfloat16
MESH_ID = pl.DeviceIdType.MESH
N_DEV = 8
VMEM_LIMIT_BYTES = 56 * 1024 * 1024

NORM_EPS = 1e-6
S5_WIDTH, S5_GROUP, S5_GROUPS, S5_STATE = 512, 16, 32, 64
HG_HEADS, HG_DIM, HG_CHUNK = 4, 128, 64
HG_STEP_CHUNKS = 4
ATT_E, ATT_HPG, ATT_BLOCK = 64, 8, 128
ATT_DILATIONS = (1, 4, 16)
ROT_DIM, ROPE_THETA = 16, 500000.0
D_FF = 2816
ADAM_LR, ADAM_B1, ADAM_B2, ADAM_EPS, ADAM_WD, ADAM_STEP = 0.001, 0.9, 0.999, 1e-08, 0.01, 10
NEG_BIG = -1e30


def _params(**kw):
    return pltpu.CompilerParams(vmem_limit_bytes=VMEM_LIMIT_BYTES, **kw)


def _pick(n, cands):
    for c in cands:
        if n % c == 0:
            return c
    return n


def _dot(a, b):
    return jnp.dot(a.astype(BF16), b.astype(BF16), preferred_element_type=F32)


def _dot_nt(a, b):
    return lax.dot_general(a.astype(BF16), b.astype(BF16), (((1,), (1,)), ((), ())), preferred_element_type=F32)


def _dot_tn(a, b):
    return lax.dot_general(a.astype(BF16), b.astype(BF16), (((0,), (0,)), ((), ())), preferred_element_type=F32)


def _split2(x):
    hi = x.astype(BF16)
    return hi, (x - hi.astype(F32)).astype(BF16)


def _dot_x3(a, b, contract=((1,), (0,))):
    dn = (contract, ((), ()))
    a1, a2 = _split2(a)
    b1, b2 = _split2(b)
    return (lax.dot_general(a1, b1, dn, preferred_element_type=F32) + lax.dot_general(a1, b2, dn, preferred_element_type=F32)
            + lax.dot_general(a2, b1, dn, preferred_element_type=F32))


def _sigmoid(x):
    return 1.0 / (1.0 + jnp.exp(-x))


V7X_HBM_BYTES_PER_S = 3.2e12
V7X_MXU_FLOPS_PER_S = 0.7e15
GRID_STEP_S = 0.35e-6
MM_VMEM_BUDGET = 40 * 1024 * 1024


def _divisors(n, cands):
    return [c for c in cands if c <= n and n % c == 0] or [n]


def _mm_tiles(m, n, k, sa, sb, so, sr):
    best = None
    for tm in _divisors(m, (2816, 2048, 1408, 1024, 512, 256, 128)):
        for tn in _divisors(n, (2816, 2048, 1408, 1024, 512, 256, 128)):
            for tk in _divisors(k, (k, 2816, 2560, 2304, 2048, 1536, 1408, 1280, 1024, 512, 256, 128)):
                nk = k // tk
                vmem = 2 * (tm * tk * sa + tk * tn * sb + tm * tn * (so + sr)) + (tm * tn * 4 if nk > 1 else 0)
                vmem += tm * tk * 2 * (sa > 2) + tk * tn * 2 * (sb > 2) + tm * tn * 4
                if vmem > MM_VMEM_BUDGET:
                    continue
                ni, nj = m // tm, n // tn
                for i_outer in (True, False):
                    if i_outer:
                        a_reads = 1 if nk == 1 else nj
                        b_reads = 1 if (nk == 1 and nj == 1) else ni
                    else:
                        b_reads = 1 if nk == 1 else ni
                        a_reads = 1 if (nk == 1 and ni == 1) else nj
                    traffic = a_reads * m * k * sa + b_reads * k * n * sb + m * n * (so + sr)
                    t = max(traffic / V7X_HBM_BYTES_PER_S, 2.0 * m * n * k / V7X_MXU_FLOPS_PER_S)
                    t += ni * nj * nk * GRID_STEP_S
                    t += (tm * tk * sa + tk * tn * sb + tm * tn * so) / V7X_HBM_BYTES_PER_S
                    if best is None or t < best[0]:
                        best = (t, tm, tn, tk, i_outer)
    assert best is not None, (m, n, k)
    return best[1:]


def mm(a, b, *, ta=False, tb=False, res=None, out_dtype=F32, dep=None, out_rows=None, out_off=0, into=None, name):
    m, k = (a.shape[1], a.shape[0]) if ta else a.shape
    n = b.shape[0] if tb else b.shape[1]
    assert (b.shape[1] if tb else b.shape[0]) == k
    has_res = res is not None
    tm, tn, tk, i_outer = _mm_tiles(m, n, k, a.dtype.itemsize, b.dtype.itemsize, jnp.dtype(out_dtype).itemsize,
                                    res.dtype.itemsize if has_res else 0)
    nk = k // tk
    deps = [] if dep is None else [dep]
    dn = (((0 if ta else 1,), (1 if tb else 0,)), ((), ()))

    def body_single(*refs):
        a_ref, b_ref = refs[:2]
        o_ref = refs[-1]
        out = lax.dot_general(a_ref[...].astype(BF16), b_ref[...].astype(BF16), dn, preferred_element_type=F32)
        if has_res:
            out = out + refs[2][...].astype(F32)
        o_ref[...] = out.astype(o_ref.dtype)

    def body(*refs):
        a_ref, b_ref = refs[:2]
        r_ref = refs[2] if has_res else None
        o_ref, acc_ref = refs[-2:]
        kk = pl.program_id(2)
        part = lax.dot_general(a_ref[...].astype(BF16), b_ref[...].astype(BF16), dn, preferred_element_type=F32)

        @pl.when(kk == 0)
        def _():
            acc_ref[...] = part

        @pl.when(kk > 0)
        def _():
            acc_ref[...] += part

        @pl.when(kk == nk - 1)
        def _():
            out = acc_ref[...]
            if has_res:
                out = out + r_ref[...].astype(F32)
            o_ref[...] = out.astype(o_ref.dtype)

    def ij(f):
        return (lambda g0, g1, q: f(g0, g1, q)) if i_outer else (lambda g0, g1, q: f(g1, g0, q))

    a_spec = pl.BlockSpec((tk, tm), ij(lambda i, j, q: (q, i))) if ta else pl.BlockSpec((tm, tk), ij(lambda i, j, q: (i, q)))
    b_spec = pl.BlockSpec((tn, tk), ij(lambda i, j, q: (j, q))) if tb else pl.BlockSpec((tk, tn), ij(lambda i, j, q: (q, j)))
    assert out_off % tm == 0
    off = out_off // tm
    r_spec = pl.BlockSpec((tm, tn), ij(lambda i, j, q: (i, j)))
    o_spec = pl.BlockSpec((tm, tn), ij(lambda i, j, q: (i + off, j)))
    rest = [] if into is None else [into]
    in_specs = ([a_spec, b_spec] + ([r_spec] if has_res else []) + [pl.BlockSpec((8, 128), lambda g0, g1, q: (0, 0))] * len(deps)
                + [pl.BlockSpec(memory_space=pl.ANY)] * len(rest))
    args = (a, b) + ((res,) if has_res else ()) + tuple(deps) + tuple(rest)
    grid = (m // tm, n // tn, nk) if i_outer else (n // tn, m // tm, nk)
    return pl.pallas_call(
        body_single if nk == 1 else body, grid=grid, in_specs=in_specs, out_specs=o_spec,
        out_shape=jax.ShapeDtypeStruct((out_rows or m, n), out_dtype),
        input_output_aliases={len(args) - 1: 0} if rest else {},
        scratch_shapes=[] if nk == 1 else [pltpu.VMEM((tm, tn), F32)],
        compiler_params=_params(dimension_semantics=("parallel", "parallel", "arbitrary")), name=name,
    )(*args)


def mm_drms(dy_in, w, x, g, dres, *, dep=None, name):
    halves = dy_in if isinstance(dy_in, (tuple, list)) else (dy_in,)
    m, kh = halves[0].shape
    k = kh * len(halves)
    D = w.shape[1]
    tm = _pick(m, (512, 256, 128))
    tk = max(_divisors(kh, (1536, 1408, 1280, 1024, 512, 256, 128)))
    nk, nh = k // tk, kh // tk
    deps = [] if dep is None else [dep]

    def body(*refs):
        a_refs, (b_ref, x_ref, g_ref, dres_ref) = refs[:len(halves)], refs[len(halves):len(halves) + 4]
        dx_ref, dg_ref, acc_ref = refs[-3:]
        i, q = pl.program_id(0), pl.program_id(1)
        a = a_refs[0][...] if len(halves) == 1 else jnp.where(q < nh, a_refs[0][...], a_refs[1][...])
        part = jnp.dot(a, b_ref[...], preferred_element_type=F32)

        @pl.when(q == 0)
        def _():
            acc_ref[...] = part

        @pl.when(q > 0)
        def _():
            acc_ref[...] += part

        @pl.when((i == 0) & (q == 0))
        def _():
            dg_ref[...] = jnp.zeros_like(dg_ref)

        @pl.when(q == nk - 1)
        def _():
            dyv = acc_ref[...]
            xv = x_ref[...]
            r = lax.rsqrt(jnp.mean(xv * xv, axis=-1, keepdims=True) + NORM_EPS)
            xh = xv * r
            dg_ref[...] += jnp.sum(dyv * xh, axis=0, keepdims=True)
            dxh = dyv * g_ref[...]
            dx_ref[...] = dres_ref[...] + r * (dxh - xh * jnp.mean(dxh * xh, axis=-1, keepdims=True))

    row = pl.BlockSpec((tm, D), lambda i, q: (i, 0))
    vec = pl.BlockSpec((1, D), lambda i, q: (0, 0))
    a_specs = [pl.BlockSpec((tm, tk), lambda i, q, h=h: (i, jnp.clip(q - h * nh, 0, nh - 1))) for h in range(len(halves))]
    in_specs = a_specs + [pl.BlockSpec((tk, D), lambda i, q: (q, 0)), row, vec, row]
    in_specs += [pl.BlockSpec((8, 128), lambda i, q: (0, 0))] * len(deps)
    return pl.pallas_call(
        body, grid=(m // tm, nk), in_specs=in_specs, out_specs=[row, vec],
        out_shape=[jax.ShapeDtypeStruct((m, D), F32), jax.ShapeDtypeStruct((1, D), F32)],
        scratch_shapes=[pltpu.VMEM((tm, D), F32)],
        compiler_params=_params(dimension_semantics=("arbitrary", "arbitrary")), name=name,
    )(*halves, w, x, g.reshape(1, D), dres, *deps)


def mm_final_loss(act, w, h_res, g, target, *, name):
    L, K = act.shape
    D = w.shape[1]
    tm = _pick(L, (512, 256, 128))
    tk = max(_divisors(K, (1536, 1408, 1280, 1024, 512, 256, 128)))
    nk = K // tk

    def body(a_ref, b_ref, r_ref, g_ref, t_ref, loss_ref, dx_ref, dg_ref, acc_ref):
        i, q = pl.program_id(0), pl.program_id(1)
        part = jnp.dot(a_ref[...], b_ref[...], preferred_element_type=F32)

        @pl.when(q == 0)
        def _():
            acc_ref[...] = part

        @pl.when(q > 0)
        def _():
            acc_ref[...] += part

        @pl.when((i == 0) & (q == 0))
        def _():
            dg_ref[...] = jnp.zeros_like(dg_ref)
            loss_ref[...] = jnp.zeros_like(loss_ref)

        @pl.when(q == nk - 1)
        def _():
            xv = acc_ref[...] + r_ref[...]
            gv = g_ref[...]
            r = lax.rsqrt(jnp.mean(xv * xv, axis=-1, keepdims=True) + NORM_EPS)
            xh = xv * r
            err = xh * gv - t_ref[...]
            loss_ref[...] += 0.5 * jnp.sum(jnp.mean(err * err, axis=-1, keepdims=True), axis=0, keepdims=True)
            dyv = err * (1.0 / D)
            dg_ref[...] += jnp.sum(dyv * xh, axis=0, keepdims=True)
            dxh = dyv * gv
            dx_ref[...] = r * (dxh - xh * jnp.mean(dxh * xh, axis=-1, keepdims=True))

    row = pl.BlockSpec((tm, D), lambda i, q: (i, 0))
    vec = pl.BlockSpec((1, D), lambda i, q: (0, 0))
    one = pl.BlockSpec((1, 1), lambda i, q: (0, 0))
    return pl.pallas_call(
        body, grid=(L // tm, nk),
        in_specs=[pl.BlockSpec((tm, tk), lambda i, q: (i, q)), pl.BlockSpec((tk, D), lambda i, q: (q, 0)), row, vec, row],
        out_specs=[one, row, vec],
        out_shape=[jax.ShapeDtypeStruct((1, 1), F32), jax.ShapeDtypeStruct((L, D), F32), jax.ShapeDtypeStruct((1, D), F32)],
        scratch_shapes=[pltpu.VMEM((tm, D), F32)],
        compiler_params=_params(dimension_semantics=("arbitrary", "arbitrary")), name=name,
    )(act, w, h_res, g.reshape(1, D), target)


def _cmul(ar, ai, br, bi):
    return ar * br - ai * bi, ar * bi + ai * br


def _powers(ar, ai):
    rows = [(ar, ai)]
    for _ in range(7):
        rows.append(_cmul(rows[-1][0], rows[-1][1], ar, ai))
    table = (jnp.concatenate([r[0] for r in rows], axis=0), jnp.concatenate([r[1] for r in rows], axis=0))
    return (rows[0], rows[1], rows[3]), table


def _block_scan(br, bi, steps, shift):
    yr, yi = br, bi
    for s, (pr, pi) in zip((1, 2, 4), steps):
        sr, si = shift(yr, s), shift(yi, s)
        yr, yi = yr + pr * sr - pi * si, yi + pr * si + pi * sr
    return yr, yi


def s5_core_fwd(proj, a_re, a_im, wb_re, wb_im, wc_re, wc_im, *, name):
    L = proj.shape[0]
    parts, cu, W = wb_re.shape

    def body(u_ref, ar_ref, ai_ref, wbr_ref, wbi_ref, wcr_ref, wci_ref, y_ref, xr_ref, xi_ref, br_ref, bi_ref):
        u = u_ref[...]
        br_ref[...] = _dot(u, wbr_ref[...])
        bi_ref[...] = _dot(u, wbi_ref[...])
        steps, (tr, ti) = _powers(ar_ref[...], ai_ref[...])
        row = lax.broadcasted_iota(jnp.int32, (8, W), 0)

        def shift(y, s):
            return jnp.where(row >= s, pltpu.roll(y, s, 0), 0.0)

        def step(t8, carry):
            cr, ci = carry
            base = pl.multiple_of(t8 * 8, 8)
            yr, yi = _block_scan(br_ref[pl.ds(base, 8), :], bi_ref[pl.ds(base, 8), :], steps, shift)
            xr = yr + tr * cr - ti * ci
            xi = yi + tr * ci + ti * cr
            xr_ref[pl.ds(base, 8), :] = xr
            xi_ref[pl.ds(base, 8), :] = xi
            return jnp.broadcast_to(xr[7:8, :], (8, W)), jnp.broadcast_to(xi[7:8, :], (8, W))

        zero = jnp.zeros((8, W), F32)
        lax.fori_loop(0, L // 8, step, (zero, zero), unroll=2)
        y_ref[...] = _dot(xr_ref[...], wcr_ref[...]) + _dot(xi_ref[...], wci_ref[...])

    ucol = pl.BlockSpec((L, cu), lambda t: (0, t))
    vec = pl.BlockSpec((1, W), lambda t: (0, t))
    col = pl.BlockSpec((L, W), lambda t: (0, t))
    wb = pl.BlockSpec((None, cu, W), lambda t: (t, 0, 0))
    wc = pl.BlockSpec((None, W, cu), lambda t: (t, 0, 0))
    return pl.pallas_call(body, grid=(parts,), in_specs=[ucol, vec, vec, wb, wb, wc, wc], out_specs=[ucol, col, col],
                          out_shape=[jax.ShapeDtypeStruct((L, parts * cu), F32)]
                          + [jax.ShapeDtypeStruct((L, parts * W), F32)] * 2,
                          scratch_shapes=[pltpu.VMEM((L, W), F32)] * 2,
                          compiler_params=_params(dimension_semantics=("parallel",)), name=name)(
        proj, a_re, a_im, wb_re, wb_im, wc_re, wc_im)


def s5_core_bwd(dy, du_d, proj, xs_re, xs_im, a_re, a_im, wb_re, wb_im, wc_re, wc_im, *, name):
    L = proj.shape[0]
    parts, cu, W = wb_re.shape

    def body(dy_ref, dud_ref, u_ref, xr_ref, xi_ref, ar_ref, ai_ref, wbr_ref, wbi_ref, wcr_ref, wci_ref,
             du_ref, dwbr_ref, dwbi_ref, dwcr_ref, dwci_ref, dar_ref, dai_ref, lr_ref, li_ref):
        dy = dy_ref[...]
        lr_ref[...] = _dot_nt(dy, wcr_ref[...])
        li_ref[...] = _dot_nt(dy, wci_ref[...])
        dwcr_ref[...] = _dot_tn(xr_ref[...], dy)
        dwci_ref[...] = _dot_tn(xi_ref[...], dy)
        ar, ai = ar_ref[...], -ai_ref[...]
        steps, (tr, ti) = _powers(ar, ai)
        tr = jnp.concatenate([tr[j:j + 1, :] for j in range(7, -1, -1)], axis=0)
        ti = jnp.concatenate([ti[j:j + 1, :] for j in range(7, -1, -1)], axis=0)
        row8 = lax.broadcasted_iota(jnp.int32, (8, W), 0)
        nblk = L // 8

        def shift(y, s):
            return jnp.where(row8 < 8 - s, pltpu.roll(y, 8 - s, 0), 0.0)

        def step(s, carry):
            cr, ci = carry
            base = pl.multiple_of((nblk - 1 - s) * 8, 8)
            yr, yi = _block_scan(lr_ref[pl.ds(base, 8), :], li_ref[pl.ds(base, 8), :], steps, shift)
            lr = yr + tr * cr - ti * ci
            li = yi + tr * ci + ti * cr
            lr_ref[pl.ds(base, 8), :] = lr
            li_ref[pl.ds(base, 8), :] = li
            return jnp.broadcast_to(lr[0:1, :], (8, W)), jnp.broadcast_to(li[0:1, :], (8, W))

        zero = jnp.zeros((8, W), F32)
        lax.fori_loop(0, nblk, step, (zero, zero), unroll=2)
        row = lax.broadcasted_iota(jnp.int32, (L, W), 0)
        xpr = jnp.where(row >= 1, pltpu.roll(xr_ref[...], 1, 0), 0.0)
        xpi = jnp.where(row >= 1, pltpu.roll(xi_ref[...], 1, 0), 0.0)
        lr, li = lr_ref[...], li_ref[...]
        dar_ref[...] = jnp.sum(lr * xpr + li * xpi, axis=0, keepdims=True)
        dai_ref[...] = jnp.sum(li * xpr - lr * xpi, axis=0, keepdims=True)
        u = u_ref[...]
        dwbr_ref[...] = _dot_tn(u, lr)
        dwbi_ref[...] = _dot_tn(u, li)
        du_ref[...] = (dud_ref[...] + _dot_nt(lr, wbr_ref[...]) + _dot_nt(li, wbi_ref[...])).astype(du_ref.dtype)

    ucol = pl.BlockSpec((L, cu), lambda t: (0, t))
    vec = pl.BlockSpec((1, W), lambda t: (0, t))
    col = pl.BlockSpec((L, W), lambda t: (0, t))
    wb = pl.BlockSpec((None, cu, W), lambda t: (t, 0, 0))
    wc = pl.BlockSpec((None, W, cu), lambda t: (t, 0, 0))
    return pl.pallas_call(
        body, grid=(parts,), in_specs=[ucol, ucol, ucol, col, col, vec, vec, wb, wb, wc, wc],
        out_specs=[ucol, wb, wb, wc, wc, vec, vec],
        out_shape=[jax.ShapeDtypeStruct((L, parts * cu), BF16)] + [jax.ShapeDtypeStruct((parts, cu, W), F32)] * 2
        + [jax.ShapeDtypeStruct((parts, W, cu), F32)] * 2 + [jax.ShapeDtypeStruct((1, parts * W), F32)] * 2,
        scratch_shapes=[pltpu.VMEM((L, W), F32)] * 2,
        compiler_params=_params(dimension_semantics=("parallel",)), name=name,
    )(dy, du_d, proj, xs_re, xs_im, a_re, a_im, wb_re, wb_im, wc_re, wc_im)


def _gelu(y):
    c = math.sqrt(2.0 / math.pi)
    t = jnp.tanh(c * (y + 0.044715 * y * y * y))
    return 0.5 * y * (1.0 + t), t


def s5_out_fwd(y0, proj, dvec, glu_w, glu_b, *, name):
    L, C = y0.shape
    tr = _pick(L, (256, 128))

    def body(y_ref, u_ref, d_ref, w_ref, b_ref, o_ref):
        z, _ = _gelu(y_ref[...] + d_ref[...] * u_ref[...])
        zg = _dot(z, w_ref[...]) + b_ref[...]
        o_ref[...] = (z * _sigmoid(zg)).astype(o_ref.dtype)

    row = pl.BlockSpec((tr, C), lambda i: (i, 0))
    vec = pl.BlockSpec((1, C), lambda i: (0, 0))
    wsp = pl.BlockSpec((C, C), lambda i: (0, 0))
    return pl.pallas_call(body, grid=(L // tr,), in_specs=[row, row, vec, wsp, vec], out_specs=row,
                          out_shape=jax.ShapeDtypeStruct((L, 2 * C), BF16), name=name)(
        y0, proj, dvec, glu_w, glu_b)


def s5_out_bwd(y0, proj, dvec, glu_w, glu_b, dcat, *, name):
    L, C = y0.shape
    tr = _pick(L, (256, 128))

    def body(y_ref, u_ref, d_ref, w_ref, b_ref, do_ref, dy_ref, dud_ref, z_ref, dzg_ref, db_ref, dd_ref):
        u = u_ref[...]
        y = y_ref[...] + d_ref[...] * u
        z, t = _gelu(y)
        zg = _dot(z, w_ref[...]) + b_ref[...]
        s = _sigmoid(zg)
        do = do_ref[...]
        dzg = do * z * s * (1.0 - s)
        dz = do * s + _dot_nt(dzg, w_ref[...])
        c = math.sqrt(2.0 / math.pi)
        dgelu = 0.5 * (1.0 + t) + 0.5 * y * (1.0 - t * t) * c * (1.0 + 3.0 * 0.044715 * y * y)
        dy = dz * dgelu

        @pl.when(pl.program_id(0) == 0)
        def _():
            db_ref[...] = jnp.zeros_like(db_ref)
            dd_ref[...] = jnp.zeros_like(dd_ref)

        db_ref[...] += jnp.sum(dzg, axis=0, keepdims=True)
        dd_ref[...] += jnp.sum(dy * u, axis=0, keepdims=True)
        dy_ref[...] = dy
        dud_ref[...] = dy * d_ref[...]
        z_ref[...] = z.astype(BF16)
        dzg_ref[...] = dzg.astype(BF16)

    row = pl.BlockSpec((tr, C), lambda i: (i, 0))
    vec = pl.BlockSpec((1, C), lambda i: (0, 0))
    wsp = pl.BlockSpec((C, C), lambda i: (0, 0))
    return pl.pallas_call(body, grid=(L // tr,), in_specs=[row, row, vec, wsp, vec, row],
                          out_specs=[row, row, row, row, vec, vec],
                          out_shape=[jax.ShapeDtypeStruct((L, C), F32), jax.ShapeDtypeStruct((L, C), F32),
                                     jax.ShapeDtypeStruct((L, C), BF16), jax.ShapeDtypeStruct((L, C), BF16),
                                     jax.ShapeDtypeStruct((1, C), F32), jax.ShapeDtypeStruct((1, C), F32)],
                          compiler_params=_params(dimension_semantics=("arbitrary",)), name=name)(
        y0, proj, dvec, glu_w, glu_b, dcat)


def _dot_tri(tri, x, tri_left=True):
    t = tri.astype(BF16)
    x1 = x.astype(BF16)
    r1 = x - x1.astype(F32)
    x2 = r1.astype(BF16)
    x3 = (r1 - x2.astype(F32)).astype(BF16)
    dot = (lambda p: jnp.dot(t, p, preferred_element_type=F32)) if tri_left else (
        lambda p: jnp.dot(p, t, preferred_element_type=F32))
    return dot(x1) + dot(x2) + dot(x3)


def _hg_gates(xq, xf, lb, tri):
    C = xq.shape[0]
    sq = _sigmoid(xq)
    q = xq * sq
    sg = _sigmoid(xf)
    f = lb + (1.0 - lb) * sg
    kk = 1.0 - f
    b = _dot_tri(tri, jnp.log(f))
    bm = b[C // 2 - 1:C // 2, :]
    bl = b[C - 1:C, :]
    eb = jnp.exp(b)
    eqm, ekm, ekl = jnp.exp(b - bm), jnp.exp(bm - b), jnp.exp(bl - b)
    return dict(sq=sq, q=q, sg=sg, f=f, kk=kk, eb=eb, ebl=jnp.exp(bl), eqm=eqm, ekm=ekm, ekl=ekl,
                qb=q * eb, qt=q * eqm, kt=kk * ekm, kh=kk * ekl)


def _tri(C, lower):
    r = lax.broadcasted_iota(jnp.int32, (C, C), 0)
    c = lax.broadcasted_iota(jnp.int32, (C, C), 1)
    return (r >= c) if lower else (c >= r)


def hgrn_fwd(proj, lb, norm_g, cat, *, name):
    L = proj.shape[0]
    C, H, K = HG_CHUNK, HG_HEADS, HG_DIM
    HK = H * K
    nc = L // C

    def body(q_ref, f_ref, i_ref, g_ref, lb_ref, ng_ref, cat_ref, o_ref, sall_ref, st_ref):
        @pl.when(pl.program_id(0) == 0)
        def _():
            st_ref[...] = jnp.zeros_like(st_ref)

        mask = _tri(C, True)
        sts = [st_ref[h] for h in range(H)]
        for s in range(S):
            rs = slice(s * C, (s + 1) * C)
            gt = _hg_gates(q_ref[rs, :], f_ref[rs, :], lb_ref[...], mask.astype(F32))
            v_all = i_ref[rs, :]
            outs = []
            for h in range(H):
                sl = slice(h * K, (h + 1) * K)
                v, st = v_all[:, sl], sts[h]
                sall_ref[s, h] = st
                att = jnp.where(mask, _dot_nt(gt["qt"][:, sl], gt["kt"][:, sl]), 0.0)
                o = _dot(att, v) + _dot_nt(gt["qb"][:, sl], st)
                sts[h] = st * gt["ebl"][:, sl] + _dot_tn(v, gt["kh"][:, sl])
                outs.append(o * lax.rsqrt(jnp.mean(o * o, axis=-1, keepdims=True) + NORM_EPS))
            xg = g_ref[rs, :]
            o_ref[rs, :] = (jnp.concatenate(outs, axis=1) * ng_ref[...] * (xg * _sigmoid(xg))).astype(o_ref.dtype)
        for h in range(H):
            st_ref[h] = sts[h]

    S = HG_STEP_CHUNKS

    def blk(cb):
        return pl.BlockSpec((S * C, HK), lambda i: (i, cb))

    vec = pl.BlockSpec((1, HK), lambda i: (0, 0))
    return pl.pallas_call(
        body, grid=(nc // S,), in_specs=[blk(1), blk(2), blk(3), blk(4), vec, vec, pl.BlockSpec(memory_space=pl.ANY)],
        out_specs=[pl.BlockSpec((S * C, HK), lambda i: (i, 1)), pl.BlockSpec((S, H, K, K), lambda i: (i, 0, 0, 0))],
        out_shape=[jax.ShapeDtypeStruct((L, 2 * HK), BF16), jax.ShapeDtypeStruct((nc, H, K, K), F32)],
        input_output_aliases={6: 0},
        scratch_shapes=[pltpu.VMEM((H, K, K), F32)],
        compiler_params=_params(dimension_semantics=("arbitrary",)), name=name,
    )(proj, proj, proj, proj, lb, norm_g, cat)


def hgrn_bwd(proj, lb, norm_g, sall, dcat, du, *, name):
    L = proj.shape[0]
    C, H, K = HG_CHUNK, HG_HEADS, HG_DIM
    HK = H * K
    nc = L // C

    def body(q_ref, f_ref, i_ref, g_ref, lb_ref, ng_ref, sall_ref, do_ref, du_ref, dx_ref, dlb_ref, dng_ref, dst_ref):
        @pl.when(pl.program_id(0) == 0)
        def _():
            dst_ref[...] = jnp.zeros_like(dst_ref)
            dlb_ref[...] = jnp.zeros_like(dlb_ref)
            dng_ref[...] = jnp.zeros_like(dng_ref)

        mask = _tri(C, True)
        lb_all, ng = lb_ref[...], ng_ref[...]
        dx_ref[:, 0:HK] = du_ref[...]
        dsts = [dst_ref[h] for h in range(H)]
        for s in reversed(range(S)):
            rs = slice(s * C, (s + 1) * C)
            dsts = chunk_bwd(rs, s, dsts, mask, lb_all, ng, q_ref, f_ref, i_ref, g_ref, sall_ref, do_ref,
                             dx_ref, dlb_ref, dng_ref)
        for h in range(H):
            dst_ref[h] = dsts[h]

    def chunk_bwd(rs, s, dsts, mask, lb_all, ng, q_ref, f_ref, i_ref, g_ref, sall_ref, do_ref, dx_ref, dlb_ref, dng_ref):
        xq, xg, v_all = q_ref[rs, :], g_ref[rs, :], i_ref[rs, :]
        gt = _hg_gates(xq, f_ref[rs, :], lb_all, mask.astype(F32))
        sgg = _sigmoid(xg)
        d_ob = do_ref[rs, :]
        d_on = d_ob * (xg * sgg)
        doh = d_on * ng
        ohs, d_qts, d_qbs, d_kts, d_khs, dvs, d_bls, new_dsts = [], [], [], [], [], [], [], []
        for h in range(H):
            sl = slice(h * K, (h + 1) * K)
            v, st, dst = v_all[:, sl], sall_ref[s, h], dsts[h]
            qt, kt, kh, qb = gt["qt"][:, sl], gt["kt"][:, sl], gt["kh"][:, sl], gt["qb"][:, sl]
            att = jnp.where(mask, _dot_nt(qt, kt), 0.0)
            o = _dot(att, v) + _dot_nt(qb, st)
            r = lax.rsqrt(jnp.mean(o * o, axis=-1, keepdims=True) + NORM_EPS)
            oh = o * r
            do = r * (doh[:, sl] - oh * jnp.mean(doh[:, sl] * oh, axis=-1, keepdims=True))
            datt = jnp.where(mask, _dot_nt(do, v), 0.0)
            dvs.append(_dot_tn(att, do) + _dot_nt(kh, dst))
            d_qbs.append(_dot_x3(do, st))
            d_qts.append(_dot_x3(datt, kt))
            d_kts.append(_dot_x3(datt, qt, ((0,), (0,))))
            d_kh = _dot_x3(v, dst)
            d_khs.append(d_kh)
            d_bls.append(jnp.sum(dst * st, axis=0, keepdims=True) * gt["ebl"][:, sl]
                         + jnp.sum(d_kh * kh, axis=0, keepdims=True))
            new_dsts.append(dst * gt["ebl"][:, sl] + _dot_tn(do, qb))
            ohs.append(oh)
        oh, d_qt, d_qb, d_kt, d_kh, dv, d_bl = (jnp.concatenate(p, axis=1) for p in
                                                (ohs, d_qts, d_qbs, d_kts, d_khs, dvs, d_bls))
        dxg = d_ob * (oh * ng) * (sgg * (1.0 + xg * (1.0 - sgg)))
        dng_ref[...] += jnp.sum(d_on * oh, axis=0, keepdims=True)
        dq = d_qt * gt["eqm"] + d_qb * gt["eb"]
        db = d_qt * gt["qt"] + d_qb * gt["qb"] - d_kt * gt["kt"] - d_kh * gt["kh"]
        rowi = lax.broadcasted_iota(jnp.int32, (C, HK), 0)
        db = db + jnp.where(rowi == C - 1, d_bl, 0.0)
        dkk = d_kt * gt["ekm"] + d_kh * gt["ekl"]
        dlg = _dot_tri(_tri(C, False).astype(F32), db)
        df = dlg / gt["f"] - dkk
        sg, sq = gt["sg"], gt["sq"]
        dlb_ref[...] += jnp.sum(df * (1.0 - sg), axis=0, keepdims=True)
        dx_ref[rs, HK:2 * HK] = (dq * (sq * (1.0 + xq * (1.0 - sq)))).astype(dx_ref.dtype)
        dx_ref[rs, 2 * HK:3 * HK] = (df * (1.0 - lb_all) * sg * (1.0 - sg)).astype(dx_ref.dtype)
        dx_ref[rs, 3 * HK:4 * HK] = dv.astype(dx_ref.dtype)
        dx_ref[rs, 4 * HK:5 * HK] = dxg.astype(dx_ref.dtype)
        return new_dsts

    S = HG_STEP_CHUNKS
    ns = nc // S

    def blk(cb):
        return pl.BlockSpec((S * C, HK), lambda i: (ns - 1 - i, cb))

    vec = pl.BlockSpec((1, HK), lambda i: (0, 0))
    return pl.pallas_call(
        body, grid=(ns,),
        in_specs=[blk(1), blk(2), blk(3), blk(4), vec, vec,
                  pl.BlockSpec((S, H, K, K), lambda i: (ns - 1 - i, 0, 0, 0)), blk(1), blk(0)],
        out_specs=[pl.BlockSpec((S * C, 5 * HK), lambda i: (ns - 1 - i, 0)), vec, vec],
        out_shape=[jax.ShapeDtypeStruct((L, 5 * HK), BF16), jax.ShapeDtypeStruct((1, HK), F32),
                   jax.ShapeDtypeStruct((1, HK), F32)],
        scratch_shapes=[pltpu.VMEM((H, K, K), F32)],
        compiler_params=_params(dimension_semantics=("arbitrary",)), name=name,
    )(proj, proj, proj, proj, lb, norm_g, sall, dcat, du)


def _shift_down(x, k, row):
    return jnp.where(row >= k, pltpu.roll(x, k, 0), 0.0)


def _shift_up(x, k, row):
    n = x.shape[0]
    return jnp.where(row < n - k, pltpu.roll(x, n - k, 0), 0.0)


def convgate_fwd(hu, conv_w, conv_b, *, name):
    L, C2 = hu.shape
    C = C2 // 2
    tc = _pick(C, (256, 128))
    nb = C // tc

    def body(a_ref, b_ref, wa_ref, wb_ref, ba_ref, bb_ref, o_ref):
        row = lax.broadcasted_iota(jnp.int32, (L, tc), 0)

        def conv(x, w, bias):
            return w[2:3, :] * x + w[1:2, :] * _shift_down(x, 1, row) + w[0:1, :] * _shift_down(x, 2, row) + bias

        ca = conv(a_ref[...], wa_ref[...], ba_ref[...])
        cb = conv(b_ref[...], wb_ref[...], bb_ref[...])
        o_ref[...] = (ca * _sigmoid(ca) * cb).astype(o_ref.dtype)

    def col(off, rows):
        return pl.BlockSpec((rows, tc), lambda j: (0, j + off))

    return pl.pallas_call(
        body, grid=(nb,), in_specs=[col(0, L), col(nb, L), col(0, 3), col(nb, 3), col(0, 1), col(nb, 1)],
        out_specs=col(0, L), out_shape=jax.ShapeDtypeStruct((L, C), BF16),
        compiler_params=_params(dimension_semantics=("parallel",)), name=name,
    )(hu, hu, conv_w, conv_w, conv_b, conv_b)


def convgate_bwd(hu, conv_w, conv_b, dact, *, name):
    L, C2 = hu.shape
    C = C2 // 2
    tc = _pick(C, (256, 128))
    nb = C // tc

    def body(a_ref, b_ref, wa_ref, wb_ref, ba_ref, bb_ref, d_ref, dxa_ref, dxb_ref, dwa_ref, dwb_ref, dba_ref, dbb_ref):
        row = lax.broadcasted_iota(jnp.int32, (L, tc), 0)

        def conv(x, w, bias):
            x1 = _shift_down(x, 1, row)
            x2 = _shift_down(x, 2, row)
            return w[2:3, :] * x + w[1:2, :] * x1 + w[0:1, :] * x2 + bias, x1, x2

        xa, xb = a_ref[...], b_ref[...]
        wa, wb = wa_ref[...], wb_ref[...]
        ca, xa1, xa2 = conv(xa, wa, ba_ref[...])
        cb, xb1, xb2 = conv(xb, wb, bb_ref[...])
        d = d_ref[...]
        sa = _sigmoid(ca)
        dca = d * cb * (sa * (1.0 + ca * (1.0 - sa)))
        dcb = d * (ca * sa)

        def back(dc, w, x, x1, x2, dx_ref, dw_ref, db_ref):
            dx = w[2:3, :] * dc + w[1:2, :] * _shift_up(dc, 1, row) + w[0:1, :] * _shift_up(dc, 2, row)
            dx_ref[...] = dx.astype(dx_ref.dtype)
            dw_ref[...] = jnp.concatenate([jnp.sum(dc * x2, axis=0, keepdims=True),
                                           jnp.sum(dc * x1, axis=0, keepdims=True),
                                           jnp.sum(dc * x, axis=0, keepdims=True)], axis=0)
            db_ref[...] = jnp.sum(dc, axis=0, keepdims=True)

        back(dca, wa, xa, xa1, xa2, dxa_ref, dwa_ref, dba_ref)
        back(dcb, wb, xb, xb1, xb2, dxb_ref, dwb_ref, dbb_ref)

    def col(off, rows):
        return pl.BlockSpec((rows, tc), lambda j: (0, j + off))

    outs = pl.pallas_call(
        body, grid=(nb,),
        in_specs=[col(0, L), col(nb, L), col(0, 3), col(nb, 3), col(0, 1), col(nb, 1), col(0, L)],
        out_specs=[col(0, L), col(0, L), col(0, 3), col(0, 3), col(0, 1), col(0, 1)],
        out_shape=[jax.ShapeDtypeStruct((L, C), BF16)] * 2 + [jax.ShapeDtypeStruct((3, C), F32)] * 2
        + [jax.ShapeDtypeStruct((1, C), F32)] * 2,
        compiler_params=_params(dimension_semantics=("parallel",)), name=name,
    )(hu, hu, conv_w, conv_w, conv_b, conv_b, dact)
    dxa, dxb, dwa, dwb, dba, dbb = outs
    return (dxa, dxb), jnp.concatenate([dwa, dwb], axis=1), jnp.concatenate([dba, dbb], axis=1)


def rope_tables(positions):
    half = ROT_DIM // 2
    inv_freq = ROPE_THETA ** (-jnp.arange(half, dtype=F32) * 2.0 / ROT_DIM)
    ang = positions.astype(F32)[:, None] * inv_freq
    cos, sin = jnp.cos(ang), jnp.sin(ang)
    L = positions.shape[0]
    one = jnp.ones((L, ATT_E - ROT_DIM), F32)
    zero = jnp.zeros((L, ATT_E - ROT_DIM), F32)
    zh = jnp.zeros((L, half), F32)
    tc = jnp.concatenate([cos, cos, one], axis=1)
    ts1 = jnp.concatenate([zh, sin, zero], axis=1)
    ts2 = jnp.concatenate([-sin, zh, zero], axis=1)
    return tuple(jnp.concatenate([t, t], axis=1) for t in (tc, ts1, ts2))


def norm_mm(x, g, w_t, *, tabs=None, name):
    L, D = x.shape
    N = w_t.shape[0]
    W = 512
    tm = _pick(L, (1024, 512, 256, 128))
    nq = N // (3 * W)
    scale = ATT_E ** -0.5
    rope = tabs is not None

    def body(x_ref, g_ref, b_ref, *rest):
        hn_ref, o_ref, hn_scr = rest[-3:]
        j = pl.program_id(1)

        @pl.when(j == 0)
        def _():
            xv = x_ref[...]
            r = lax.rsqrt(jnp.mean(xv * xv, axis=-1, keepdims=True) + NORM_EPS)
            hn = (xv * r * g_ref[...]).astype(BF16)
            hn_scr[...] = hn
            hn_ref[...] = hn

        out = _dot_nt(hn_scr[...], b_ref[...])
        if rope:
            c_ref, s1_ref, s2_ref = rest[:3]
            c = jnp.concatenate([c_ref[...]] * 4, axis=1)
            s1 = jnp.concatenate([s1_ref[...]] * 4, axis=1)
            s2 = jnp.concatenate([s2_ref[...]] * 4, axis=1)
            rot = out * c + pltpu.roll(out, 8, 1) * s1 + pltpu.roll(out, W - 8, 1) * s2
            out = jnp.where(j < 2 * nq, rot * jnp.where(j < nq, scale, 1.0), out)
        o_ref[...] = out

    row = pl.BlockSpec((tm, D), lambda i, j: (i, 0))
    tab = pl.BlockSpec((tm, 128), lambda i, j: (i, 0))
    return pl.pallas_call(body, grid=(L // tm, N // W),
                          in_specs=[row, pl.BlockSpec((1, D), lambda i, j: (0, 0)), pl.BlockSpec((W, D), lambda i, j: (j, 0))]
                          + ([tab, tab, tab] if rope else []),
                          out_specs=[row, pl.BlockSpec((tm, W), lambda i, j: (i, j))],
                          out_shape=[jax.ShapeDtypeStruct((L, D), BF16), jax.ShapeDtypeStruct((L, N), F32)],
                          scratch_shapes=[pltpu.VMEM((tm, D), BF16)],
                          compiler_params=_params(dimension_semantics=("parallel", "arbitrary")), name=name)(
        x, g.reshape(1, D), w_t, *(tabs or ()))


def rope_bwd(slabs, tabs, *, name):
    L, W = slabs[0].shape
    tr = _pick(L, (256, 128))
    nq = len(slabs) // 3
    scale = ATT_E ** -0.5

    def body(*refs):
        d_refs, (c_ref, s1_ref, s2_ref, o_ref) = refs[:3 * nq], refs[3 * nq:]
        c = jnp.concatenate([c_ref[...]] * 4, axis=1)
        s1 = jnp.concatenate([s1_ref[...]] * 4, axis=1)
        s2 = jnp.concatenate([s2_ref[...]] * 4, axis=1)
        for j, d_ref in enumerate(d_refs):
            dy = d_ref[...]
            if j < 2 * nq:
                dy = dy * c + pltpu.roll(dy * s1, W - 8, 1) + pltpu.roll(dy * s2, 8, 1)
            if j < nq:
                dy = dy * scale
            o_ref[:, j * W:(j + 1) * W] = dy.astype(o_ref.dtype)

    slab = pl.BlockSpec((tr, W), lambda i: (i, 0))
    tab = pl.BlockSpec((tr, 128), lambda i: (i, 0))
    return pl.pallas_call(body, grid=(L // tr,), in_specs=[slab] * (3 * nq) + [tab, tab, tab],
                          out_specs=pl.BlockSpec((tr, 3 * nq * W), lambda i: (i, 0)),
                          out_shape=jax.ShapeDtypeStruct((L, 3 * nq * W), BF16),
                          compiler_params=_params(dimension_semantics=("parallel",)), name=name)(*slabs, *tabs)


def _att_masks(has_prev):
    qi = lax.broadcasted_iota(jnp.int32, (ATT_BLOCK, ATT_BLOCK), 0)
    kj = lax.broadcasted_iota(jnp.int32, (ATT_BLOCK, ATT_BLOCK), 1)
    return qi >= kj, (kj >= qi) & has_prev


ATT_COLS = 128


def _att_rows(j, d, nb):
    B = ATT_BLOCK
    r, n = j // nb, j % nb
    start = r + d * B * n
    has_prev = n > 0
    pstart = jnp.where(has_prev, start - d * B, start)
    if d == 1:
        return pl.ds(pl.multiple_of(start, B), B), pl.ds(pl.multiple_of(pstart, B), B), has_prev
    return pl.ds(start, B, stride=d), pl.ds(pstart, B, stride=d), has_prev


def _qkv_specs(L, g):
    per = ATT_HPG * ATT_E // ATT_COLS
    third = len(ATT_DILATIONS) * per
    return [pl.BlockSpec((L, ATT_COLS), lambda c, base=base: (0, base + c))
            for base in (g * per, third + g * per, 2 * third + g * per)]


def attn_fwd(qkv, g, d, *, name):
    L, W = qkv.shape[0], ATT_HPG * ATT_E
    B, E = ATT_BLOCK, ATT_E
    nblk = L // B
    nb = nblk // d

    def body(q_ref, k_ref, v_ref, o_ref, l_ref):
        def step(j, carry):
            cur, prv, has_prev = _att_rows(j, d, nb)
            mc, mp = _att_masks(has_prev)
            qb, kc, kp, vc, vp = q_ref[cur, :], k_ref[cur, :], k_ref[prv, :], v_ref[cur, :], v_ref[prv, :]
            outs, lses = [], []
            for h in range(ATT_COLS // E):
                sl = slice(h * E, (h + 1) * E)
                sc = jnp.where(mc, _dot_nt(qb[:, sl], kc[:, sl]), NEG_BIG)
                sp = jnp.where(mp, _dot_nt(qb[:, sl], kp[:, sl]), NEG_BIG)
                m = jnp.maximum(jnp.max(sc, axis=-1, keepdims=True), jnp.max(sp, axis=-1, keepdims=True))
                pc = jnp.exp(sc - m)
                pp = jnp.exp(sp - m)
                den = jnp.sum(pc, axis=-1, keepdims=True) + jnp.sum(pp, axis=-1, keepdims=True)
                outs.append((_dot(pc, vc[:, sl]) + _dot(pp, vp[:, sl])) / den)
                lses.append(jnp.broadcast_to(m + jnp.log(den), (B, E)))
            o_ref[cur, :] = jnp.concatenate(outs, axis=1)
            l_ref[cur, :] = jnp.concatenate(lses, axis=1)
            return carry

        lax.fori_loop(0, nblk, step, 0, unroll=4)

    col = pl.BlockSpec((L, ATT_COLS), lambda c: (0, c))
    return pl.pallas_call(body, grid=(W // ATT_COLS,), in_specs=_qkv_specs(L, g), out_specs=[col] * 2,
                          out_shape=[jax.ShapeDtypeStruct((L, W), F32)] * 2,
                          compiler_params=_params(dimension_semantics=("parallel",)), name=name)(qkv, qkv, qkv)


def attn_bwd(qkv, g, lse, do, dl, d, *, name):
    L, W = qkv.shape[0], ATT_HPG * ATT_E
    B, E = ATT_BLOCK, ATT_E
    nblk = L // B
    nb = nblk // d

    def body(q_ref, k_ref, v_ref, l_ref, do_ref, dl_ref, dq_ref, dk_ref, dv_ref):
        dk_ref[...] = jnp.zeros_like(dk_ref)
        dv_ref[...] = jnp.zeros_like(dv_ref)

        def step(j, carry):
            cur, prv, has_prev = _att_rows(j, d, nb)
            mc, mp = _att_masks(has_prev)
            qb, kc, kp, vc, vp = q_ref[cur, :], k_ref[cur, :], k_ref[prv, :], v_ref[cur, :], v_ref[prv, :]
            lb, dob, dlb = l_ref[cur, :], do_ref[cur, :], dl_ref[cur, :]
            dqs, dkc, dkp, dvc, dvp = [], [], [], [], []
            for h in range(ATT_COLS // E):
                sl = slice(h * E, (h + 1) * E)
                qh, doh = qb[:, sl], dob[:, sl]
                lse_h, dl_h = lb[:, h * E:h * E + 1], dlb[:, h * E:h * E + 1]
                pc = jnp.where(mc, jnp.exp(_dot_nt(qh, kc[:, sl]) - lse_h), 0.0)
                pp = jnp.where(mp, jnp.exp(_dot_nt(qh, kp[:, sl]) - lse_h), 0.0)
                dsc = pc * (_dot_nt(doh, vc[:, sl]) - dl_h)
                dsp = pp * (_dot_nt(doh, vp[:, sl]) - dl_h)
                dqs.append(_dot(dsc, kc[:, sl]) + _dot(dsp, kp[:, sl]))
                dkc.append(_dot_tn(dsc, qh))
                dkp.append(_dot_tn(dsp, qh))
                dvc.append(_dot_tn(pc, doh))
                dvp.append(_dot_tn(pp, doh))
            dq_ref[cur, :] = jnp.concatenate(dqs, axis=1)
            dk_ref[cur, :] = dk_ref[cur, :] + jnp.concatenate(dkc, axis=1)
            dv_ref[cur, :] = dv_ref[cur, :] + jnp.concatenate(dvc, axis=1)
            dk_ref[prv, :] = dk_ref[prv, :] + jnp.concatenate(dkp, axis=1)
            dv_ref[prv, :] = dv_ref[prv, :] + jnp.concatenate(dvp, axis=1)
            return carry

        lax.fori_loop(0, nblk, step, 0, unroll=4)

    col = pl.BlockSpec((L, ATT_COLS), lambda c: (0, c))
    return pl.pallas_call(body, grid=(W // ATT_COLS,), in_specs=_qkv_specs(L, g) + [col] * 3, out_specs=[col] * 3,
                          out_shape=[jax.ShapeDtypeStruct((L, W), F32)] * 3,
                          compiler_params=_params(dimension_semantics=("parallel",)), name=name)(
        qkv, qkv, qkv, lse, do, dl)


def _merge_alpha(l_refs):
    ls = [r[...] for r in l_refs]
    m = jnp.maximum(jnp.maximum(ls[0], ls[1]), ls[2])
    es = [jnp.exp(l - m) for l in ls]
    den = es[0] + es[1] + es[2]
    return [e / den for e in es]


def merge_fwd(os_, ls_, *, name):
    L, W = os_[0].shape
    tr = _pick(L, (256, 128))

    def body(o0, o1, o2, l0, l1, l2, out_ref):
        al = _merge_alpha((l0, l1, l2))
        out_ref[...] = (al[0] * o0[...] + al[1] * o1[...] + al[2] * o2[...]).astype(out_ref.dtype)

    row = pl.BlockSpec((tr, W), lambda i: (i, 0))
    return pl.pallas_call(body, grid=(L // tr,), in_specs=[row] * 6, out_specs=row,
                          out_shape=jax.ShapeDtypeStruct((L, W), BF16), name=name)(*os_, *ls_)


def merge_bwd(os_, ls_, do, *, name):
    L, W = do.shape
    tr = _pick(L, (256, 128))

    def body(o0, o1, o2, l0, l1, l2, do_ref, d0, d1, d2, e0, e1, e2):
        al = _merge_alpha((l0, l1, l2))
        dov = do_ref[...]
        r = lax.broadcasted_iota(jnp.int32, (W, W), 0) // ATT_E
        c = lax.broadcasted_iota(jnp.int32, (W, W), 1) // ATT_E
        ones_blk = (r == c).astype(F32)
        t = jnp.zeros_like(dov)
        for a, o in zip(al, (o0, o1, o2)):
            t = t + a * _dot_tri(ones_blk, dov * o[...], tri_left=False)
        for a, d_ref, e_ref in zip(al, (d0, d1, d2), (e0, e1, e2)):
            d_ref[...] = a * dov
            e_ref[...] = a * t

    row = pl.BlockSpec((tr, W), lambda i: (i, 0))
    return pl.pallas_call(body, grid=(L // tr,), in_specs=[row] * 7, out_specs=[row] * 6,
                          out_shape=[jax.ShapeDtypeStruct((L, W), F32)] * 6, name=name)(*os_, *ls_, do)


def _me_and_peers():
    x, y, c = lax.axis_index("x"), lax.axis_index("y"), lax.axis_index("c")
    peers = []
    for k in range(1, N_DEV):
        px = 1 - x if k & 4 else x
        py = 1 - y if k & 2 else y
        pc = 1 - c if k & 1 else c
        peers.append((px, py, pc))
    return (x, y, c), peers


def _index(dev):
    return 4 * dev[0] + 2 * dev[1] + dev[2]


def _hbm(a):
    return pltpu.with_memory_space_constraint(a, pltpu.HBM)


HBM_SPEC = pl.BlockSpec(memory_space=pltpu.HBM)
SEM_SPEC = pl.BlockSpec(memory_space=pltpu.SEMAPHORE)
DATAFLOW = pltpu.SideEffectType.DATAFLOW_SIDE_EFFECTING


def _remote(src_ref, land_ref, slotted, me, peer, src_is_mine, send_sem, recv_sem, k):
    sender, receiver = (me, peer) if src_is_mine else (peer, me)
    src = src_ref.at[_index(receiver)] if slotted else src_ref
    return pltpu.make_async_remote_copy(src_ref=src, dst_ref=land_ref.at[_index(sender)], send_sem=send_sem.at[k],
                                        recv_sem=recv_sem.at[k], device_id=peer, device_id_type=MESH_ID)


SIBLING = 0
SAME_CORE = (1, 3, 5)
OTHER_CORE = (2, 4, 6)


def copies_start(arrays, mode, *, name):
    n = len(arrays)
    slotted = mode == "exchange"
    lands = [lax.empty(a.shape if slotted else (N_DEV,) + a.shape, a.dtype) for a in arrays]
    targets = (SIBLING,) + SAME_CORE if mode == "gather2" else tuple(range(N_DEV - 1))

    def body(*refs):
        x_refs, land_refs = refs[:n], refs[n:2 * n]
        send, recv = refs[2 * n:3 * n], refs[3 * n:4 * n]
        token = refs[-1]
        me, peers = _me_and_peers()
        for w in range(n):
            for k in targets:
                _remote(x_refs[w], land_refs[w], slotted, me, peers[k], True, send[w], recv[w], k).start()
            if not slotted:
                pltpu.make_async_copy(x_refs[w], land_refs[w].at[_index(me)], recv[w].at[N_DEV - 1]).start()
        token[...] = jnp.zeros_like(token)

    sem = pltpu.SemaphoreType.DMA((N_DEV,))
    out_shape = ([sem] * (2 * n) + [pltpu.HBM(a.shape, a.dtype) for a in arrays]
                 + [pltpu.HBM(l.shape, l.dtype) for l in lands] + [jax.ShapeDtypeStruct((8, 128), F32)])
    outs = pl.pallas_call(
        body, name=name, out_shape=out_shape, in_specs=[HBM_SPEC] * (2 * n),
        out_specs=[SEM_SPEC] * (2 * n) + [HBM_SPEC] * (2 * n) + [pl.BlockSpec(memory_space=pltpu.VMEM)],
        input_output_aliases={i: 2 * n + i for i in range(2 * n)},
        compiler_params=pltpu.CompilerParams(has_side_effects=DATAFLOW),
    )(*[_hbm(a) for a in arrays], *[_hbm(l) for l in lands])
    handles = [(outs[w], outs[n + w], outs[2 * n + w], outs[3 * n + w]) for w in range(n)]
    return handles, outs[-1]


def _forward(land_ref, me, peers, j, fsend, frecv, mine):
    block = _index(peers[SAME_CORE[j]] if mine else peers[OTHER_CORE[j]])
    return pltpu.make_async_remote_copy(src_ref=land_ref.at[block], dst_ref=land_ref.at[block], send_sem=fsend.at[j],
                                        recv_sem=frecv.at[j], device_id=peers[SIBLING], device_id_type=MESH_ID)


def copies_forward(handles, after, *, name):
    n = len(handles)

    def body(*refs):
        land_refs, recv = refs[:n], refs[n:2 * n]
        fsend, frecv = refs[2 * n + 1:3 * n + 1], refs[3 * n + 1:4 * n + 1]
        token = refs[-1]
        me, peers = _me_and_peers()
        for w in range(n):
            for j, k in enumerate(SAME_CORE):
                block = land_refs[w].at[_index(peers[k])]
                pltpu.make_async_remote_copy(src_ref=block, dst_ref=block, send_sem=recv[w].at[N_DEV - 1],
                                             recv_sem=recv[w].at[k], device_id=peers[k], device_id_type=MESH_ID).wait_recv()
                _forward(land_refs[w], me, peers, j, fsend[w], frecv[w], True).start()
        token[...] = jnp.zeros_like(token)

    sem = pltpu.SemaphoreType.DMA((len(SAME_CORE),))
    lands = [h[3] for h in handles]
    outs = pl.pallas_call(
        body, name=name,
        out_shape=[sem] * (2 * n) + [pltpu.HBM(l.shape, l.dtype) for l in lands] + [jax.ShapeDtypeStruct((8, 128), F32)],
        in_specs=[HBM_SPEC] * n + [SEM_SPEC] * n + [pl.BlockSpec(memory_space=pl.ANY)],
        out_specs=[SEM_SPEC] * (2 * n) + [HBM_SPEC] * n + [pl.BlockSpec(memory_space=pltpu.VMEM)],
        input_output_aliases={w: 2 * n + w for w in range(n)},
        compiler_params=pltpu.CompilerParams(has_side_effects=DATAFLOW),
    )(*lands, *[h[1] for h in handles], after)
    new = [(h[0], h[1], h[2], outs[2 * n + w], outs[w], outs[n + w]) for w, h in enumerate(handles)]
    return new, outs[-1]


def copies_wait(handle, mode, after, *, name):
    slotted = mode == "exchange"
    two_level = mode == "gather2"
    send_sem, recv_sem, x_thru, land_thru = handle[:4]
    targets = (SIBLING,) + SAME_CORE if two_level else tuple(range(N_DEV - 1))
    arrivals = (SIBLING,) if two_level else targets

    def body(x_ref, land_ref, send_ref, recv_ref, *rest):
        me, peers = _me_and_peers()
        for k in targets:
            _remote(x_ref, land_ref, slotted, me, peers[k], True, send_ref, recv_ref, k).wait_send()
        for k in arrivals:
            _remote(x_ref, land_ref, slotted, me, peers[k], False, send_ref, recv_ref, k).wait_recv()
        if not slotted:
            pltpu.make_async_copy(x_ref, land_ref.at[_index(me)], recv_ref.at[N_DEV - 1]).wait()
        if two_level:
            fsend, frecv = rest[0], rest[1]
            for j in range(len(SAME_CORE)):
                _forward(land_ref, me, peers, j, fsend, frecv, True).wait_send()
                _forward(land_ref, me, peers, j, fsend, frecv, False).wait_recv()

    extra = list(handle[4:])
    return pl.pallas_call(
        body, name=name, out_shape=(pltpu.HBM(x_thru.shape, x_thru.dtype), pltpu.HBM(land_thru.shape, land_thru.dtype)),
        in_specs=[HBM_SPEC, HBM_SPEC, SEM_SPEC, SEM_SPEC] + [SEM_SPEC] * len(extra) + [pl.BlockSpec(memory_space=pl.ANY)],
        out_specs=(HBM_SPEC, HBM_SPEC), input_output_aliases={0: 0, 1: 1},
        compiler_params=pltpu.CompilerParams(has_side_effects=DATAFLOW),
    )(x_thru, land_thru, send_sem, recv_sem, *extra, after)


def cast_bf16(x, *, dep=None, name):
    R, C = x.shape
    tr = _pick(R, (512, 352, 256, 128, 64))
    deps = [] if dep is None else [dep]

    def body(x_ref, *rest):
        rest[-1][...] = x_ref[...].astype(BF16)

    row = pl.BlockSpec((tr, C), lambda i: (i, 0))
    return pl.pallas_call(body, grid=(R // tr,), in_specs=[row] + [pl.BlockSpec((8, 128), lambda i: (0, 0))] * len(deps),
                          out_specs=row, out_shape=jax.ShapeDtypeStruct((R, C), BF16), name=name)(x, *deps)


def cast_bf16_layer(x3, layer, *, name):
    _, R, C = x3.shape
    tr = _pick(R, (512, 352, 256, 128, 64))

    def body(x_ref, o_ref):
        o_ref[...] = x_ref[...].astype(BF16)

    return pl.pallas_call(body, grid=(R // tr,), in_specs=[pl.BlockSpec((None, tr, C), lambda i: (layer, i, 0))],
                          out_specs=pl.BlockSpec((tr, C), lambda i: (i, 0)),
                          out_shape=jax.ShapeDtypeStruct((R, C), BF16), name=name)(x3)


BD_PARTS = 4


def _blockdiag_call(b, build, G, r, c, name):
    gp = G // BD_PARTS

    def body_build(b_ref, o_ref):
        o_ref[...] = jnp.zeros_like(o_ref)
        for g in range(G):
            o_ref[g // gp, (g % gp) * r:(g % gp + 1) * r, (g % gp) * c:(g % gp + 1) * c] = b_ref[g]

    def body_extract(d_ref, o_ref):
        for g in range(G):
            o_ref[g] = d_ref[g // gp, (g % gp) * r:(g % gp + 1) * r, (g % gp) * c:(g % gp + 1) * c]

    out = jax.ShapeDtypeStruct((BD_PARTS, gp * r, gp * c) if build else (G, r, c), F32)
    return pl.pallas_call(body_build if build else body_extract, out_shape=out, name=name)(b)


def make_blockdiag(G, r, c, name):
    @jax.custom_vjp
    def blockdiag(b):
        return _blockdiag_call(b, True, G, r, c, name + "_build")

    def fwd(b):
        return blockdiag(b), None

    def bwd(_, g):
        return (_blockdiag_call(g, False, G, r, c, name + "_extract"),)

    blockdiag.defvjp(fwd, bwd)
    return blockdiag


def cols_from_shards(g, *, name):
    _, K, n = g.shape
    tk = _pick(K, (256, 128))

    def body(g_ref, o_ref):
        for i in range(N_DEV):
            o_ref[:, i * n:(i + 1) * n] = g_ref[i]

    return pl.pallas_call(body, grid=(K // tk,), in_specs=[pl.BlockSpec((N_DEV, tk, n), lambda i: (0, i, 0))],
                          out_specs=pl.BlockSpec((tk, N_DEV * n), lambda i: (i, 0)),
                          out_shape=jax.ShapeDtypeStruct((K, N_DEV * n), g.dtype), name=name)(g)


def shards_from_cols(w, *, name):
    K, N = w.shape
    n = N // N_DEV
    tk = _pick(K, (256, 128))

    def body(w_ref, o_ref):
        for i in range(N_DEV):
            o_ref[i] = w_ref[:, i * n:(i + 1) * n].astype(o_ref.dtype)

    return pl.pallas_call(body, grid=(K // tk,), in_specs=[pl.BlockSpec((tk, N), lambda i: (i, 0))],
                          out_specs=pl.BlockSpec((N_DEV, tk, n), lambda i: (0, i, 0)),
                          out_shape=jax.ShapeDtypeStruct((N_DEV, K, n), BF16), name=name)(w)


def _adamw(w, g, m, v):
    m = ADAM_B1 * m + (1.0 - ADAM_B1) * g
    v = ADAM_B2 * v + (1.0 - ADAM_B2) * (g * g)
    m_hat = m / (1.0 - ADAM_B1 ** ADAM_STEP)
    v_hat = v / (1.0 - ADAM_B2 ** ADAM_STEP)
    delta = -ADAM_LR * (m_hat / (jnp.sqrt(v_hat) + ADAM_EPS) + ADAM_WD * w)
    return delta, m, v


def reduce_adamw(recv, own, own_slotted, me, w, m, v, *, layer=0, n_layers=1, into=None, name):
    _, R, C = recv.shape
    tr = _pick(R, (352, 320, 288, 256, 128, 64, 32, 16, 8))
    off = layer * (R // tr)

    def body(me_ref, r_ref, own_ref, w_ref, m_ref, v_ref, *rest):
        g_ref, d_ref, nm_ref, nv_ref = rest[-4:]
        mine = me_ref[0]
        g = None
        for i in range(N_DEV):
            part = jnp.where(mine == i, own_ref[...], r_ref[i]).astype(F32)
            g = part if g is None else g + part
        delta, nm, nv = _adamw(w_ref[...], g, m_ref[...], v_ref[...])
        g_ref[...] = g
        d_ref[...] = delta
        nm_ref[...] = nm
        nv_ref[...] = nv

    row = pl.BlockSpec((tr, C), lambda i, me_ref: (i + off, 0))
    own_spec = (pl.BlockSpec((None, tr, C), lambda i, me_ref: (me_ref[0], i, 0)) if own_slotted
                else pl.BlockSpec((tr, C), lambda i, me_ref: (i, 0)))
    rest = [] if into is None else list(into)
    grid_spec = pltpu.PrefetchScalarGridSpec(
        num_scalar_prefetch=1, grid=(R // tr,),
        in_specs=[pl.BlockSpec((N_DEV, tr, C), lambda i, me_ref: (0, i, 0)), own_spec, row, row, row]
        + [pl.BlockSpec(memory_space=pl.ANY)] * len(rest),
        out_specs=[row] * 4)
    return pl.pallas_call(body, grid_spec=grid_spec, out_shape=[jax.ShapeDtypeStruct((n_layers * R, C), F32)] * 4,
                          input_output_aliases={6 + k: k for k in range(len(rest))},
                          compiler_params=_params(dimension_semantics=("parallel",)), name=name)(
        me.reshape(1).astype(jnp.int32), recv, own, w, m, v, *rest)


def _s5_prepare(A_re, A_im, log_dt, B_re, B_im, C_re, C_im):
    G, P, Cg = S5_GROUPS, S5_STATE, S5_GROUP
    dt = jnp.exp(log_dt)[:, None]
    mag = jnp.exp(A_re * dt)
    ab_re = mag * jnp.cos(A_im * dt)
    ab_im = mag * jnp.sin(A_im * dt)
    den = A_re * A_re + A_im * A_im
    nr, ni = ab_re - 1.0, ab_im
    c_re = (nr * A_re + ni * A_im) / den
    c_im = (ni * A_re - nr * A_im) / den
    Bb_re = c_re[..., None] * B_re - c_im[..., None] * B_im
    Bb_im = c_re[..., None] * B_im + c_im[..., None] * B_re
    def dense_in(b, name):
        return make_blockdiag(G, Cg, P, name)(b.transpose(0, 2, 1))

    def dense_out(c, name):
        return make_blockdiag(G, P, Cg, name)(c.transpose(0, 2, 1))

    return (ab_re.reshape(1, G * P), ab_im.reshape(1, G * P), dense_in(Bb_re, "s5_wb_re"), dense_in(Bb_im, "s5_wb_im"),
            dense_out(C_re, "s5_wc_re"), dense_out(-C_im, "s5_wc_im"))


def _lower_bound(gamma):
    return jnp.cumsum(jax.nn.softmax(gamma, axis=0), axis=0)[0:1]


def _ffn_fwd(h, g_norm, get_w_in, conv_w, conv_b, get_w_out, tag, final=None):
    w_in = get_w_in(h)
    hn, hu = norm_mm(h, g_norm, w_in, name=tag + "_in")
    act = convgate_fwd(hu, conv_w, conv_b, name=tag + "_gate")
    w_out = get_w_out(act)
    if final is None:
        h_out = mm(act, w_out, res=h, name=tag + "_out")
    else:
        h_out = mm_final_loss(act, w_out, h, final[0], final[1], name=tag + "_out_loss")
    return h_out, (hn, hu, act), w_in, w_out


def _ffn_bwd(h, g_norm, w_in, conv_w, conv_b, w_out, saved, dh, tag, send_dw_in, send_dw_out):
    hn, hu, act = saved
    sent = send_dw_out(mm(act, dh, ta=True, out_dtype=BF16, name=tag + "_dwout"))
    dact = mm(dh, w_out, tb=True, dep=sent, name=tag + "_dact")
    (dhu_a, dhu_b), dconv_w, dconv_b = convgate_bwd(hu, conv_w, conv_b, dact, name=tag + "_dgate")
    rows = 2 * dhu_a.shape[1]
    dw_in = mm(dhu_a, hn, ta=True, out_dtype=BF16, out_rows=rows, name=tag + "_dwin_a")
    dw_in = mm(dhu_b, hn, ta=True, out_dtype=BF16, out_rows=rows, out_off=rows // 2, into=dw_in, name=tag + "_dwin_b")
    sent = send_dw_in(dw_in)
    dh_in, dg = mm_drms((dhu_a, dhu_b), w_in, h, g_norm, dh, dep=sent, name=tag + "_dhn")
    return dh_in, dg, dconv_w, dconv_b


def kernel(x, positions, norm_mix, norm_ffn, norm_final, mix_w_in, mix_w_out, s5_A_re, s5_A_im, s5_log_dt, s5_B_re, s5_B_im, s5_C_re, s5_C_im, s5_D, s5_glu_w, s5_glu_b, hgrn_gamma, hgrn_norm, att_w_qkv, att_w_o, ffn_w_in, ffn_conv_w, ffn_conv_b, ffn_w_out, loss_target, m_norm_mix, m_norm_ffn, m_norm_final, m_mix_w_in, m_mix_w_out, m_s5_A_re, m_s5_A_im, m_s5_log_dt, m_s5_B_re, m_s5_B_im, m_s5_C_re, m_s5_C_im, m_s5_D, m_s5_glu_w, m_s5_glu_b, m_hgrn_gamma, m_hgrn_norm, m_att_w_qkv, m_att_w_o, m_ffn_w_in, m_ffn_conv_w, m_ffn_conv_b, m_ffn_w_out, v_norm_mix, v_norm_ffn, v_norm_final, v_mix_w_in, v_mix_w_out, v_s5_A_re, v_s5_A_im, v_s5_log_dt, v_s5_B_re, v_s5_B_im, v_s5_C_re, v_s5_C_im, v_s5_D, v_s5_glu_w, v_s5_glu_b, v_hgrn_gamma, v_hgrn_norm, v_att_w_qkv, v_att_w_o, v_ffn_w_in, v_ffn_conv_w, v_ffn_conv_b, v_ffn_w_out):
    W = dict(norm_mix=norm_mix, norm_ffn=norm_ffn, norm_final=norm_final, mix_w_in=mix_w_in, mix_w_out=mix_w_out,
             s5_A_re=s5_A_re, s5_A_im=s5_A_im, s5_log_dt=s5_log_dt, s5_B_re=s5_B_re, s5_B_im=s5_B_im,
             s5_C_re=s5_C_re, s5_C_im=s5_C_im, s5_D=s5_D, s5_glu_w=s5_glu_w, s5_glu_b=s5_glu_b,
             hgrn_gamma=hgrn_gamma, hgrn_norm=hgrn_norm, att_w_qkv=att_w_qkv, att_w_o=att_w_o, ffn_w_in=ffn_w_in,
             ffn_conv_w=ffn_conv_w, ffn_conv_b=ffn_conv_b, ffn_w_out=ffn_w_out)
    M = dict(norm_mix=m_norm_mix, norm_ffn=m_norm_ffn, norm_final=m_norm_final, mix_w_in=m_mix_w_in,
             mix_w_out=m_mix_w_out, s5_A_re=m_s5_A_re, s5_A_im=m_s5_A_im, s5_log_dt=m_s5_log_dt, s5_B_re=m_s5_B_re,
             s5_B_im=m_s5_B_im, s5_C_re=m_s5_C_re, s5_C_im=m_s5_C_im, s5_D=m_s5_D, s5_glu_w=m_s5_glu_w,
             s5_glu_b=m_s5_glu_b, hgrn_gamma=m_hgrn_gamma, hgrn_norm=m_hgrn_norm, att_w_qkv=m_att_w_qkv,
             att_w_o=m_att_w_o, ffn_w_in=m_ffn_w_in, ffn_conv_w=m_ffn_conv_w, ffn_conv_b=m_ffn_conv_b,
             ffn_w_out=m_ffn_w_out)
    V = dict(norm_mix=v_norm_mix, norm_ffn=v_norm_ffn, norm_final=v_norm_final, mix_w_in=v_mix_w_in,
             mix_w_out=v_mix_w_out, s5_A_re=v_s5_A_re, s5_A_im=v_s5_A_im, s5_log_dt=v_s5_log_dt, s5_B_re=v_s5_B_re,
             s5_B_im=v_s5_B_im, s5_C_re=v_s5_C_re, s5_C_im=v_s5_C_im, s5_D=v_s5_D, s5_glu_w=v_s5_glu_w,
             s5_glu_b=v_s5_glu_b, hgrn_gamma=v_hgrn_gamma, hgrn_norm=v_hgrn_norm, att_w_qkv=v_att_w_qkv,
             att_w_o=v_att_w_o, ffn_w_in=v_ffn_w_in, ffn_conv_w=v_ffn_conv_w, ffn_conv_b=v_ffn_conv_b,
             ffn_w_out=v_ffn_w_out)
    return _step(x[0], positions[0], loss_target[0], W, M, V)


TRANSPOSED = ("mix_w_in", "att_w_qkv", "ffn_w_in")
SMALL = ("norm_mix", "norm_ffn", "norm_final", "s5_A_re", "s5_A_im", "s5_log_dt", "s5_B_re", "s5_B_im", "s5_C_re",
         "s5_C_im", "s5_D", "s5_glu_b", "hgrn_gamma", "hgrn_norm", "ffn_conv_b")
ORDER = ("norm_mix", "norm_ffn", "norm_final", "mix_w_in", "mix_w_out", "s5_A_re", "s5_A_im", "s5_log_dt", "s5_B_re",
         "s5_B_im", "s5_C_re", "s5_C_im", "s5_D", "s5_glu_w", "s5_glu_b", "hgrn_gamma", "hgrn_norm", "att_w_qkv",
         "att_w_o", "ffn_w_in", "ffn_conv_w", "ffn_conv_b", "ffn_w_out")
PACK_COLS = 1024


def _step(x, positions, target, W, M, V):
    L, D = x.shape
    me = 4 * lax.axis_index("x") + 2 * lax.axis_index("y") + lax.axis_index("c")
    n_cw = W["ffn_conv_w"].shape[-1]
    T = {n: tuple(jnp.swapaxes(d[n], -1, -2) for d in (W, M, V)) for n in TRANSPOSED}
    first = {
        "mix_w_in": cast_bf16(T["mix_w_in"][0][0], name="mix_w_in_cast"),
        "conv_w": W["ffn_conv_w"].reshape(6, n_cw),
        "s5_glu_w": cast_bf16(W["s5_glu_w"][0], name="s5_glu_w_cast"),
    }
    first_handles, token = copies_start(list(first.values()), "gather2", name="gather_start_first")
    shards = {
        "mix_w_out": cast_bf16(W["mix_w_out"][0], dep=token, name="mix_w_out_cast"),
        "ffn_w_in0": cast_bf16_layer(T["ffn_w_in"][0], 0, name="ffn_w_in0_cast"),
        "ffn_w_out0": cast_bf16_layer(W["ffn_w_out"], 0, name="ffn_w_out0_cast"),
        "att_w_qkv": cast_bf16(T["att_w_qkv"][0][0], name="att_w_qkv_cast"),
        "att_w_o": cast_bf16(W["att_w_o"][0], name="att_w_o_cast"),
        "ffn_w_in1": cast_bf16_layer(T["ffn_w_in"][0], 1, name="ffn_w_in1_cast"),
        "ffn_w_out1": cast_bf16_layer(W["ffn_w_out"], 1, name="ffn_w_out1_cast"),
    }
    gather_handles, token = copies_start(list(shards.values()), "gather2", name="gather_start")
    gather_handle = dict(zip(list(first) + list(shards), first_handles + gather_handles))

    def forward(keys, after, name):
        new, sent = copies_forward([gather_handle[k] for k in keys], after, name=name)
        gather_handle.update(zip(keys, new))
        return sent

    def gathered(key, after, cols):
        _, land = copies_wait(gather_handle[key], "gather2", after, name=key + "_gwait")
        return cols_from_shards(land, name=key + "_asm") if cols else land.reshape(-1, land.shape[-1])

    conv_b = W["ffn_conv_b"].reshape(2, 1, -1)

    s5_params = (W["s5_A_re"][0], W["s5_A_im"][0], W["s5_log_dt"][0], W["s5_B_re"][0], W["s5_B_im"][0],
                 W["s5_C_re"][0], W["s5_C_im"][0])
    (a_re, a_im, wb_re, wb_im, wc_re, wc_im), s5_prep_vjp = jax.vjp(_s5_prepare, *s5_params)
    dvec = W["s5_D"].reshape(1, S5_WIDTH)
    glu_b = W["s5_glu_b"].reshape(1, S5_WIDTH)
    lb, lb_vjp = jax.vjp(_lower_bound, W["hgrn_gamma"])
    hg_norm = W["hgrn_norm"].reshape(1, -1)
    tabs = rope_tables(positions)

    sent = forward(["mix_w_in", "conv_w", "s5_glu_w"], token, "forward_a")
    w_mix_in = gathered("mix_w_in", sent, False)
    hn0, proj = norm_mm(x, W["norm_mix"][0], w_mix_in, name="l0_proj")
    y0, xs_re, xs_im = s5_core_fwd(proj, a_re, a_im, wb_re, wb_im, wc_re, wc_im, name="s5_core")
    w_glu = gathered("s5_glu_w", y0, False)
    cat = s5_out_fwd(y0, proj, dvec, w_glu, glu_b, name="s5_out")
    cat, hg_states = hgrn_fwd(proj, lb, hg_norm, cat, name="hgrn_fwd")
    forward(["mix_w_out"], cat, "forward_b")
    w_mix_out = gathered("mix_w_out", cat, False)
    h1 = mm(cat, w_mix_out, res=x, name="l0_mix_out")
    _, cw_all = copies_wait(gather_handle["conv_w"], "gather2", h1, name="conv_w_gwait")
    conv_w = cw_all.transpose(1, 0, 2).reshape(2, 3, N_DEV * n_cw)
    w_ffn_in, w_ffn_out = [None, None], [None, None]
    h2, ffn0_saved, w_ffn_in[0], w_ffn_out[0] = _ffn_fwd(
        h1, W["norm_ffn"][0],
        lambda a: (forward(["ffn_w_in0"], a, "forward_b2"), gathered("ffn_w_in0", a, False))[1], conv_w[0], conv_b[0],
        lambda a: (forward(["ffn_w_out0"], a, "forward_c"), gathered("ffn_w_out0", a, False))[1], "ffn0")

    forward(["att_w_qkv", "att_w_o"], h2, "forward_d")
    w_qkv = gathered("att_w_qkv", h2, False)
    hn2, qkv_r = norm_mm(h2, W["norm_mix"][1], w_qkv, tabs=tabs, name="l1_qkv")
    att_o, att_l = [], []
    for g, d in enumerate(ATT_DILATIONS):
        o_g, l_g = attn_fwd(qkv_r, g, d, name=f"attn_fwd{g}")
        att_o.append(o_g)
        att_l.append(l_g)
    o_att = merge_fwd(att_o, att_l, name="merge_fwd")
    forward(["ffn_w_in1", "ffn_w_out1"], o_att, "forward_e")
    w_o = gathered("att_w_o", o_att, True)
    h3 = mm(o_att, w_o, res=h2, name="l1_mix_out")
    (loss, dh4, dg_final), ffn1_saved, w_ffn_in[1], w_ffn_out[1] = _ffn_fwd(
        h3, W["norm_ffn"][1], lambda a: gathered("ffn_w_in1", a, False), conv_w[1], conv_b[1],
        lambda a: gathered("ffn_w_out1", a, False), "ffn1", final=(W["norm_final"], target))

    exchanges = {}

    pending = []

    def send_grad(key, g, cols, flush=True):
        if cols:
            parts = shards_from_cols(g, name=key + "_split")
        else:
            parts = g.reshape(N_DEV, g.shape[0] // N_DEV, g.shape[1])
        pending.append((key, parts))
        if not flush:
            return None
        handles, sent = copies_start([p for _, p in pending], "exchange", name=key + "_xstart")
        exchanges.update(zip([k for k, _ in pending], handles))
        pending.clear()
        return sent

    dh3, dg_ffn1, dcw1, dcb1 = _ffn_bwd(h3, W["norm_ffn"][1], w_ffn_in[1], conv_w[1], conv_b[1], w_ffn_out[1],
                                        ffn1_saved, dh4, "ffn1", lambda g: send_grad("ffn_w_in1", g, False),
                                        lambda g: send_grad("ffn_w_out1", g, False, flush=False))
    sent = send_grad("att_w_o", mm(o_att, dh3, ta=True, name="l1_dwo"), True, flush=False)
    d_oatt = mm(dh3, w_o, tb=True, dep=sent, name="l1_dmix")
    mb = merge_bwd(att_o, att_l, d_oatt, name="merge_bwd")
    d_slabs = [attn_bwd(qkv_r, g, att_l[g], mb[g], mb[3 + g], d, name=f"attn_bwd{g}")
               for g, d in enumerate(ATT_DILATIONS)]
    d_qkv = rope_bwd([s[0] for s in d_slabs] + [s[1] for s in d_slabs] + [s[2] for s in d_slabs], tabs,
                     name="rope_bwd")
    sent = send_grad("att_w_qkv", mm(d_qkv, hn2, ta=True, out_dtype=BF16, name="l1_dwqkv"), False)
    dh2, dg_mix1 = mm_drms(d_qkv, w_qkv, h2, W["norm_mix"][1], dh3, dep=sent, name="l1_dhn")

    dh1, dg_ffn0, dcw0, dcb0 = _ffn_bwd(h1, W["norm_ffn"][0], w_ffn_in[0], conv_w[0], conv_b[0], w_ffn_out[0],
                                        ffn0_saved, dh2, "ffn0", lambda g: send_grad("ffn_w_in0", g, False),
                                        lambda g: send_grad("ffn_w_out0", g, False, flush=False))
    sent = send_grad("mix_w_out", mm(cat, dh1, ta=True, out_dtype=BF16, name="l0_dwout"), False)
    dcat = mm(dh1, w_mix_out, tb=True, dep=sent, name="l0_dcat")
    dy, du_d, z_bf, dzg, dglu_b, dD = s5_out_bwd(y0, proj, dvec, w_glu, glu_b, dcat, name="s5_dout")
    sent_glu = send_grad("s5_glu_w", mm(z_bf, dzg, ta=True, out_dtype=BF16, name="s5_dglu"), False, flush=False)
    du, dwb_re, dwb_im, dwc_re, dwc_im, da_re, da_im = s5_core_bwd(
        dy, du_d, proj, xs_re, xs_im, a_re, a_im, wb_re, wb_im, wc_re, wc_im, name="s5_dcore")
    s5_small = s5_prep_vjp((da_re, da_im, dwb_re, dwb_im, dwc_re, dwc_im))
    d_proj, dlb, dhg_norm = hgrn_bwd(proj, lb, hg_norm, hg_states, dcat, du, name="hgrn_bwd")
    sent = send_grad("mix_w_in", mm(d_proj, hn0, ta=True, out_dtype=BF16, dep=sent_glu, name="l0_dwin"), False)
    grad_x, dg_mix0 = mm_drms(d_proj, w_mix_in, x, W["norm_mix"][0], dh1, dep=sent, name="l0_dhn")
    (d_gamma,) = lb_vjp(dlb)
    out = {}

    dA_re, dA_im, dlog_dt, dB_re, dB_im, dC_re, dC_im = s5_small
    small_g = dict(norm_mix=jnp.concatenate([dg_mix0, dg_mix1], axis=0), norm_ffn=jnp.concatenate([dg_ffn0, dg_ffn1], axis=0),
                   norm_final=dg_final, s5_A_re=dA_re, s5_A_im=dA_im, s5_log_dt=dlog_dt, s5_B_re=dB_re, s5_B_im=dB_im,
                   s5_C_re=dC_re, s5_C_im=dC_im, s5_D=dD, s5_glu_b=dglu_b, hgrn_gamma=d_gamma, hgrn_norm=dhg_norm,
                   ffn_conv_b=jnp.concatenate([dcb0, dcb1], axis=0))
    conv_w_g = jnp.stack([dcw0, dcw1], axis=0)
    sizes = [math.prod(W[n].shape) for n in SMALL]
    n_conv = conv_w_g.size
    total = sum(sizes) + n_conv + 1
    rows = -(-total // PACK_COLS)
    rows = -(-rows // 8) * 8
    pad = rows * PACK_COLS - total

    def pack(vals, conv_part, last):
        flat = [v.reshape(-1).astype(F32) for v in vals] + [conv_part.reshape(-1), last.reshape(-1),
                                                            jnp.zeros((pad,), F32)]
        return jnp.concatenate(flat).reshape(rows, PACK_COLS)

    def conv_full(shard):
        col_owner = lax.broadcasted_iota(jnp.int32, (2, 3, N_DEV * n_cw), 2) // n_cw
        return jnp.where(col_owner == me, jnp.tile(shard, (1, 1, N_DEV)), 0.0)

    zero1 = jnp.zeros((1,), F32)
    g_pack = pack([small_g[n] for n in SMALL], conv_w_g, loss)
    w_pack = pack([W[n] for n in SMALL], conv_full(W["ffn_conv_w"]), zero1)
    m_pack = pack([M[n] for n in SMALL], conv_full(M["ffn_conv_w"]), zero1)
    v_pack = pack([V[n] for n in SMALL], conv_full(V["ffn_conv_w"]), zero1 + 1.0)
    (small_handle,), small_sent = copies_start([g_pack], "gather", name="small_xstart")

    def finish(name, n_layers):
        w3, m3, v3 = T[name] if name in TRANSPOSED else (W[name], M[name], V[name])
        res = None
        for layer in reversed(range(n_layers)):
            key = name if n_layers == 1 else f"{name}{layer}"
            own, recv = copies_wait(exchanges[key], "exchange", small_sent, name=key + "_xwait")
            _, R, Cn = recv.shape
            res = reduce_adamw(recv, own, True, me, w3.reshape(n_layers * R, Cn), m3.reshape(n_layers * R, Cn),
                               v3.reshape(n_layers * R, Cn), layer=layer, n_layers=n_layers, into=res,
                               name=key + "_adamw")
        res = [r.reshape(w3.shape) for r in res]
        return tuple(jnp.swapaxes(r, -1, -2) for r in res) if name in TRANSPOSED else tuple(res)

    for name in ("ffn_w_out", "ffn_w_in"):
        out[name] = finish(name, 2)
    for name in ("att_w_o", "att_w_qkv", "mix_w_out", "s5_glu_w", "mix_w_in"):
        out[name] = finish(name, 1)

    small_own, small_recv = copies_wait(small_handle, "gather", out["s5_glu_w"][0], name="small_xwait")
    res = reduce_adamw(small_recv, small_own, False, me, w_pack, m_pack, v_pack, name="small_adamw")
    flat = [r.reshape(-1) for r in res]
    off = 0
    for n, sz in zip(SMALL, sizes):
        out[n] = tuple(f[off:off + sz].reshape(W[n].shape) for f in flat)
        off += sz
    conv_res = [f[off:off + n_conv].reshape(2, 3, N_DEV * n_cw) for f in flat]
    out["ffn_conv_w"] = tuple(lax.dynamic_slice(c, (0, 0, me * n_cw), (2, 3, n_cw)) for c in conv_res)
    off += n_conv
    loss_total = flat[0][off]

    result = [loss_total, grad_x[None]]
    for k in range(4):
        result += [out[n][k] for n in ORDER]
    return tuple(result)
```

```python
import math

import jax
import jax.numpy as jnp
from jax import lax
from jax.experimental import pallas as pl
from jax.experimental.pallas import tpu as pltpu

F32 = jnp.float32
BF16 = jnp.bfloat16
MESH_ID = pl.DeviceIdType.MESH
N_DEV = 8
VMEM_LIMIT_BYTES = 56 * 1024 * 1024

NORM_EPS = 1e-6
S5_WIDTH, S5_GROUP, S5_GROUPS, S5_STATE = 512, 16, 32, 64
HG_HEADS, HG_DIM, HG_CHUNK = 4, 128, 64
HG_STEP_CHUNKS = 4
ATT_E, ATT_HPG, ATT_BLOCK = 64, 8, 128
ATT_DILATIONS = (1, 4, 16)
ROT_DIM, ROPE_THETA = 16, 500000.0
D_FF = 2816
ADAM_LR, ADAM_B1, ADAM_B2, ADAM_EPS, ADAM_WD, ADAM_STEP = 0.001, 0.9, 0.999, 1e-08, 0.01, 10
NEG_BIG = -1e30


def _params(**kw):
    return pltpu.CompilerParams(vmem_limit_bytes=VMEM_LIMIT_BYTES, **kw)


def _pick(n, cands):
    for c in cands:
        if n % c == 0:
            return c
    return n


def _dot(a, b):
    return jnp.dot(a.astype(BF16), b.astype(BF16), preferred_element_type=F32)


def _dot_nt(a, b):
    return lax.dot_general(a.astype(BF16), b.astype(BF16), (((1,), (1,)), ((), ())), preferred_element_type=F32)


def _dot_tn(a, b):
    return lax.dot_general(a.astype(BF16), b.astype(BF16), (((0,), (0,)), ((), ())), preferred_element_type=F32)


def _split2(x):
    hi = x.astype(BF16)
    return hi, (x - hi.astype(F32)).astype(BF16)


def _dot_x3(a, b, contract=((1,), (0,))):
    dn = (contract, ((), ()))
    a1, a2 = _split2(a)
    b1, b2 = _split2(b)
    return (lax.dot_general(a1, b1, dn, preferred_element_type=F32) + lax.dot_general(a1, b2, dn, preferred_element_type=F32)
            + lax.dot_general(a2, b1, dn, preferred_element_type=F32))


def _sigmoid(x):
    return 1.0 / (1.0 + jnp.exp(-x))


V7X_HBM_BYTES_PER_S = 3.2e12
V7X_MXU_FLOPS_PER_S = 0.7e15
GRID_STEP_S = 0.35e-6
MM_VMEM_BUDGET = 40 * 1024 * 1024


def _divisors(n, cands):
    return [c for c in cands if c <= n and n % c == 0] or [n]


def _mm_tiles(m, n, k, sa, sb, so, sr):
    best = None
    for tm in _divisors(m, (2816, 2048, 1408, 1024, 512, 256, 128)):
        for tn in _divisors(n, (2816, 2048, 1408, 1024, 512, 256, 128)):
            for tk in _divisors(k, (k, 2816, 2560, 2304, 2048, 1536, 1408, 1280, 1024, 512, 256, 128)):
                nk = k // tk
                vmem = 2 * (tm * tk * sa + tk * tn * sb + tm * tn * (so + sr)) + (tm * tn * 4 if nk > 1 else 0)
                vmem += tm * tk * 2 * (sa > 2) + tk * tn * 2 * (sb > 2) + tm * tn * 4
                if vmem > MM_VMEM_BUDGET:
                    continue
                ni, nj = m // tm, n // tn
                for i_outer in (True, False):
                    if i_outer:
                        a_reads = 1 if nk == 1 else nj
                        b_reads = 1 if (nk == 1 and nj == 1) else ni
                    else:
                        b_reads = 1 if nk == 1 else ni
                        a_reads = 1 if (nk == 1 and ni == 1) else nj
                    traffic = a_reads * m * k * sa + b_reads * k * n * sb + m * n * (so + sr)
                    t = max(traffic / V7X_HBM_BYTES_PER_S, 2.0 * m * n * k / V7X_MXU_FLOPS_PER_S)
                    t += ni * nj * nk * GRID_STEP_S
                    t += (tm * tk * sa + tk * tn * sb + tm * tn * so) / V7X_HBM_BYTES_PER_S
                    if best is None or t < best[0]:
                        best = (t, tm, tn, tk, i_outer)
    assert best is not None, (m, n, k)
    return best[1:]


def mm(a, b, *, ta=False, tb=False, res=None, out_dtype=F32, dep=None, out_rows=None, out_off=0, into=None, name):
    m, k = (a.shape[1], a.shape[0]) if ta else a.shape
    n = b.shape[0] if tb else b.shape[1]
    assert (b.shape[1] if tb else b.shape[0]) == k
    has_res = res is not None
    tm, tn, tk, i_outer = _mm_tiles(m, n, k, a.dtype.itemsize, b.dtype.itemsize, jnp.dtype(out_dtype).itemsize,
                                    res.dtype.itemsize if has_res else 0)
    nk = k // tk
    deps = [] if dep is None else [dep]
    dn = (((0 if ta else 1,), (1 if tb else 0,)), ((), ()))

    def body_single(*refs):
        a_ref, b_ref = refs[:2]
        o_ref = refs[-1]
        out = lax.dot_general(a_ref[...].astype(BF16), b_ref[...].astype(BF16), dn, preferred_element_type=F32)
        if has_res:
            out = out + refs[2][...].astype(F32)
        o_ref[...] = out.astype(o_ref.dtype)

    def body(*refs):
        a_ref, b_ref = refs[:2]
        r_ref = refs[2] if has_res else None
        o_ref, acc_ref = refs[-2:]
        kk = pl.program_id(2)
        part = lax.dot_general(a_ref[...].astype(BF16), b_ref[...].astype(BF16), dn, preferred_element_type=F32)

        @pl.when(kk == 0)
        def _():
            acc_ref[...] = part

        @pl.when(kk > 0)
        def _():
            acc_ref[...] += part

        @pl.when(kk == nk - 1)
        def _():
            out = acc_ref[...]
            if has_res:
                out = out + r_ref[...].astype(F32)
            o_ref[...] = out.astype(o_ref.dtype)

    def ij(f):
        return (lambda g0, g1, q: f(g0, g1, q)) if i_outer else (lambda g0, g1, q: f(g1, g0, q))

    a_spec = pl.BlockSpec((tk, tm), ij(lambda i, j, q: (q, i))) if ta else pl.BlockSpec((tm, tk), ij(lambda i, j, q: (i, q)))
    b_spec = pl.BlockSpec((tn, tk), ij(lambda i, j, q: (j, q))) if tb else pl.BlockSpec((tk, tn), ij(lambda i, j, q: (q, j)))
    assert out_off % tm == 0
    off = out_off // tm
    r_spec = pl.BlockSpec((tm, tn), ij(lambda i, j, q: (i, j)))
    o_spec = pl.BlockSpec((tm, tn), ij(lambda i, j, q: (i + off, j)))
    rest = [] if into is None else [into]
    in_specs = ([a_spec, b_spec] + ([r_spec] if has_res else []) + [pl.BlockSpec((8, 128), lambda g0, g1, q: (0, 0))] * len(deps)
                + [pl.BlockSpec(memory_space=pl.ANY)] * len(rest))
    args = (a, b) + ((res,) if has_res else ()) + tuple(deps) + tuple(rest)
    grid = (m // tm, n // tn, nk) if i_outer else (n // tn, m // tm, nk)
    return pl.pallas_call(
        body_single if nk == 1 else body, grid=grid, in_specs=in_specs, out_specs=o_spec,
        out_shape=jax.ShapeDtypeStruct((out_rows or m, n), out_dtype),
        input_output_aliases={len(args) - 1: 0} if rest else {},
        scratch_shapes=[] if nk == 1 else [pltpu.VMEM((tm, tn), F32)],
        compiler_params=_params(dimension_semantics=("parallel", "parallel", "arbitrary")), name=name,
    )(*args)


def mm_drms(dy_in, w, x, g, dres, *, dep=None, name):
    halves = dy_in if isinstance(dy_in, (tuple, list)) else (dy_in,)
    m, kh = halves[0].shape
    k = kh * len(halves)
    D = w.shape[1]
    tm = _pick(m, (512, 256, 128))
    tk = max(_divisors(kh, (1536, 1408, 1280, 1024, 512, 256, 128)))
    nk, nh = k // tk, kh // tk
    deps = [] if dep is None else [dep]

    def body(*refs):
        a_refs, (b_ref, x_ref, g_ref, dres_ref) = refs[:len(halves)], refs[len(halves):len(halves) + 4]
        dx_ref, dg_ref, acc_ref = refs[-3:]
        i, q = pl.program_id(0), pl.program_id(1)
        a = a_refs[0][...] if len(halves) == 1 else jnp.where(q < nh, a_refs[0][...], a_refs[1][...])
        part = jnp.dot(a, b_ref[...], preferred_element_type=F32)

        @pl.when(q == 0)
        def _():
            acc_ref[...] = part

        @pl.when(q > 0)
        def _():
            acc_ref[...] += part

        @pl.when((i == 0) & (q == 0))
        def _():
            dg_ref[...] = jnp.zeros_like(dg_ref)

        @pl.when(q == nk - 1)
        def _():
            dyv = acc_ref[...]
            xv = x_ref[...]
            r = lax.rsqrt(jnp.mean(xv * xv, axis=-1, keepdims=True) + NORM_EPS)
            xh = xv * r
            dg_ref[...] += jnp.sum(dyv * xh, axis=0, keepdims=True)
            dxh = dyv * g_ref[...]
            dx_ref[...] = dres_ref[...] + r * (dxh - xh * jnp.mean(dxh * xh, axis=-1, keepdims=True))

    row = pl.BlockSpec((tm, D), lambda i, q: (i, 0))
    vec = pl.BlockSpec((1, D), lambda i, q: (0, 0))
    a_specs = [pl.BlockSpec((tm, tk), lambda i, q, h=h: (i, jnp.clip(q - h * nh, 0, nh - 1))) for h in range(len(halves))]
    in_specs = a_specs + [pl.BlockSpec((tk, D), lambda i, q: (q, 0)), row, vec, row]
    in_specs += [pl.BlockSpec((8, 128), lambda i, q: (0, 0))] * len(deps)
    return pl.pallas_call(
        body, grid=(m // tm, nk), in_specs=in_specs, out_specs=[row, vec],
        out_shape=[jax.ShapeDtypeStruct((m, D), F32), jax.ShapeDtypeStruct((1, D), F32)],
        scratch_shapes=[pltpu.VMEM((tm, D), F32)],
        compiler_params=_params(dimension_semantics=("arbitrary", "arbitrary")), name=name,
    )(*halves, w, x, g.reshape(1, D), dres, *deps)


def mm_final_loss(act, w, h_res, g, target, *, name):
    L, K = act.shape
    D = w.shape[1]
    tm = _pick(L, (512, 256, 128))
    tk = max(_divisors(K, (1536, 1408, 1280, 1024, 512, 256, 128)))
    nk = K // tk

    def body(a_ref, b_ref, r_ref, g_ref, t_ref, loss_ref, dx_ref, dg_ref, acc_ref):
        i, q = pl.program_id(0), pl.program_id(1)
        part = jnp.dot(a_ref[...], b_ref[...], preferred_element_type=F32)

        @pl.when(q == 0)
        def _():
            acc_ref[...] = part

        @pl.when(q > 0)
        def _():
            acc_ref[...] += part

        @pl.when((i == 0) & (q == 0))
        def _():
            dg_ref[...] = jnp.zeros_like(dg_ref)
            loss_ref[...] = jnp.zeros_like(loss_ref)

        @pl.when(q == nk - 1)
        def _():
            xv = acc_ref[...] + r_ref[...]
            gv = g_ref[...]
            r = lax.rsqrt(jnp.mean(xv * xv, axis=-1, keepdims=True) + NORM_EPS)
            xh = xv * r
            err = xh * gv - t_ref[...]
            loss_ref[...] += 0.5 * jnp.sum(jnp.mean(err * err, axis=-1, keepdims=True), axis=0, keepdims=True)
            dyv = err * (1.0 / D)
            dg_ref[...] += jnp.sum(dyv * xh, axis=0, keepdims=True)
            dxh = dyv * gv
            dx_ref[...] = r * (dxh - xh * jnp.mean(dxh * xh, axis=-1, keepdims=True))

    row = pl.BlockSpec((tm, D), lambda i, q: (i, 0))
    vec = pl.BlockSpec((1, D), lambda i, q: (0, 0))
    one = pl.BlockSpec((1, 1), lambda i, q: (0, 0))
    return pl.pallas_call(
        body, grid=(L // tm, nk),
        in_specs=[pl.BlockSpec((tm, tk), lambda i, q: (i, q)), pl.BlockSpec((tk, D), lambda i, q: (q, 0)), row, vec, row],
        out_specs=[one, row, vec],
        out_shape=[jax.ShapeDtypeStruct((1, 1), F32), jax.ShapeDtypeStruct((L, D), F32), jax.ShapeDtypeStruct((1, D), F32)],
        scratch_shapes=[pltpu.VMEM((tm, D), F32)],
        compiler_params=_params(dimension_semantics=("arbitrary", "arbitrary")), name=name,
    )(act, w, h_res, g.reshape(1, D), target)


def _cmul(ar, ai, br, bi):
    return ar * br - ai * bi, ar * bi + ai * br


def _powers(ar, ai):
    rows = [(ar, ai)]
    for _ in range(7):
        rows.append(_cmul(rows[-1][0], rows[-1][1], ar, ai))
    table = (jnp.concatenate([r[0] for r in rows], axis=0), jnp.concatenate([r[1] for r in rows], axis=0))
    return (rows[0], rows[1], rows[3]), table


def _block_scan(br, bi, steps, shift):
    yr, yi = br, bi
    for s, (pr, pi) in zip((1, 2, 4), steps):
        sr, si = shift(yr, s), shift(yi, s)
        yr, yi = yr + pr * sr - pi * si, yi + pr * si + pi * sr
    return yr, yi


def s5_core_fwd(proj, a_re, a_im, wb_re, wb_im, wc_re, wc_im, *, name):
    L = proj.shape[0]
    parts, cu, W = wb_re.shape

    def body(u_ref, ar_ref, ai_ref, wbr_ref, wbi_ref, wcr_ref, wci_ref, y_ref, xr_ref, xi_ref, br_ref, bi_ref):
        u = u_ref[...]
        br_ref[...] = _dot(u, wbr_ref[...])
        bi_ref[...] = _dot(u, wbi_ref[...])
        steps, (tr, ti) = _powers(ar_ref[...], ai_ref[...])
        row = lax.broadcasted_iota(jnp.int32, (8, W), 0)

        def shift(y, s):
            return jnp.where(row >= s, pltpu.roll(y, s, 0), 0.0)

        def step(t8, carry):
            cr, ci = carry
            base = pl.multiple_of(t8 * 8, 8)
            yr, yi = _block_scan(br_ref[pl.ds(base, 8), :], bi_ref[pl.ds(base, 8), :], steps, shift)
            xr = yr + tr * cr - ti * ci
            xi = yi + tr * ci + ti * cr
            xr_ref[pl.ds(base, 8), :] = xr
            xi_ref[pl.ds(base, 8), :] = xi
            return jnp.broadcast_to(xr[7:8, :], (8, W)), jnp.broadcast_to(xi[7:8, :], (8, W))

        zero = jnp.zeros((8, W), F32)
        lax.fori_loop(0, L // 8, step, (zero, zero), unroll=2)
        y_ref[...] = _dot(xr_ref[...], wcr_ref[...]) + _dot(xi_ref[...], wci_ref[...])

    ucol = pl.BlockSpec((L, cu), lambda t: (0, t))
    vec = pl.BlockSpec((1, W), lambda t: (0, t))
    col = pl.BlockSpec((L, W), lambda t: (0, t))
    wb = pl.BlockSpec((None, cu, W), lambda t: (t, 0, 0))
    wc = pl.BlockSpec((None, W, cu), lambda t: (t, 0, 0))
    return pl.pallas_call(body, grid=(parts,), in_specs=[ucol, vec, vec, wb, wb, wc, wc], out_specs=[ucol, col, col],
                          out_shape=[jax.ShapeDtypeStruct((L, parts * cu), F32)]
                          + [jax.ShapeDtypeStruct((L, parts * W), F32)] * 2,
                          scratch_shapes=[pltpu.VMEM((L, W), F32)] * 2,
                          compiler_params=_params(dimension_semantics=("parallel",)), name=name)(
        proj, a_re, a_im, wb_re, wb_im, wc_re, wc_im)


def s5_core_bwd(dy, du_d, proj, xs_re, xs_im, a_re, a_im, wb_re, wb_im, wc_re, wc_im, *, name):
    L = proj.shape[0]
    parts, cu, W = wb_re.shape

    def body(dy_ref, dud_ref, u_ref, xr_ref, xi_ref, ar_ref, ai_ref, wbr_ref, wbi_ref, wcr_ref, wci_ref,
             du_ref, dwbr_ref, dwbi_ref, dwcr_ref, dwci_ref, dar_ref, dai_ref, lr_ref, li_ref):
        dy = dy_ref[...]
        lr_ref[...] = _dot_nt(dy, wcr_ref[...])
        li_ref[...] = _dot_nt(dy, wci_ref[...])
        dwcr_ref[...] = _dot_tn(xr_ref[...], dy)
        dwci_ref[...] = _dot_tn(xi_ref[...], dy)
        ar, ai = ar_ref[...], -ai_ref[...]
        steps, (tr, ti) = _powers(ar, ai)
        tr = jnp.concatenate([tr[j:j + 1, :] for j in range(7, -1, -1)], axis=0)
        ti = jnp.concatenate([ti[j:j + 1, :] for j in range(7, -1, -1)], axis=0)
        row8 = lax.broadcasted_iota(jnp.int32, (8, W), 0)
        nblk = L // 8

        def shift(y, s):
            return jnp.where(row8 < 8 - s, pltpu.roll(y, 8 - s, 0), 0.0)

        def step(s, carry):
            cr, ci = carry
            base = pl.multiple_of((nblk - 1 - s) * 8, 8)
            yr, yi = _block_scan(lr_ref[pl.ds(base, 8), :], li_ref[pl.ds(base, 8), :], steps, shift)
            lr = yr + tr * cr - ti * ci
            li = yi + tr * ci + ti * cr
            lr_ref[pl.ds(base, 8), :] = lr
            li_ref[pl.ds(base, 8), :] = li
            return jnp.broadcast_to(lr[0:1, :], (8, W)), jnp.broadcast_to(li[0:1, :], (8, W))

        zero = jnp.zeros((8, W), F32)
        lax.fori_loop(0, nblk, step, (zero, zero), unroll=2)
        row = lax.broadcasted_iota(jnp.int32, (L, W), 0)
        xpr = jnp.where(row >= 1, pltpu.roll(xr_ref[...], 1, 0), 0.0)
        xpi = jnp.where(row >= 1, pltpu.roll(xi_ref[...], 1, 0), 0.0)
        lr, li = lr_ref[...], li_ref[...]
        dar_ref[...] = jnp.sum(lr * xpr + li * xpi, axis=0, keepdims=True)
        dai_ref[...] = jnp.sum(li * xpr - lr * xpi, axis=0, keepdims=True)
        u = u_ref[...]
        dwbr_ref[...] = _dot_tn(u, lr)
        dwbi_ref[...] = _dot_tn(u, li)
        du_ref[...] = (dud_ref[...] + _dot_nt(lr, wbr_ref[...]) + _dot_nt(li, wbi_ref[...])).astype(du_ref.dtype)

    ucol = pl.BlockSpec((L, cu), lambda t: (0, t))
    vec = pl.BlockSpec((1, W), lambda t: (0, t))
    col = pl.BlockSpec((L, W), lambda t: (0, t))
    wb = pl.BlockSpec((None, cu, W), lambda t: (t, 0, 0))
    wc = pl.BlockSpec((None, W, cu), lambda t: (t, 0, 0))
    return pl.pallas_call(
        body, grid=(parts,), in_specs=[ucol, ucol, ucol, col, col, vec, vec, wb, wb, wc, wc],
        out_specs=[ucol, wb, wb, wc, wc, vec, vec],
        out_shape=[jax.ShapeDtypeStruct((L, parts * cu), BF16)] + [jax.ShapeDtypeStruct((parts, cu, W), F32)] * 2
        + [jax.ShapeDtypeStruct((parts, W, cu), F32)] * 2 + [jax.ShapeDtypeStruct((1, parts * W), F32)] * 2,
        scratch_shapes=[pltpu.VMEM((L, W), F32)] * 2,
        compiler_params=_params(dimension_semantics=("parallel",)), name=name,
    )(dy, du_d, proj, xs_re, xs_im, a_re, a_im, wb_re, wb_im, wc_re, wc_im)


def _gelu(y):
    c = math.sqrt(2.0 / math.pi)
    t = jnp.tanh(c * (y + 0.044715 * y * y * y))
    return 0.5 * y * (1.0 + t), t


def s5_out_fwd(y0, proj, dvec, glu_w, glu_b, *, name):
    L, C = y0.shape
    tr = _pick(L, (256, 128))

    def body(y_ref, u_ref, d_ref, w_ref, b_ref, o_ref):
        z, _ = _gelu(y_ref[...] + d_ref[...] * u_ref[...])
        zg = _dot(z, w_ref[...]) + b_ref[...]
        o_ref[...] = (z * _sigmoid(zg)).astype(o_ref.dtype)

    row = pl.BlockSpec((tr, C), lambda i: (i, 0))
    vec = pl.BlockSpec((1, C), lambda i: (0, 0))
    wsp = pl.BlockSpec((C, C), lambda i: (0, 0))
    return pl.pallas_call(body, grid=(L // tr,), in_specs=[row, row, vec, wsp, vec], out_specs=row,
                          out_shape=jax.ShapeDtypeStruct((L, 2 * C), BF16), name=name)(
        y0, proj, dvec, glu_w, glu_b)


def s5_out_bwd(y0, proj, dvec, glu_w, glu_b, dcat, *, name):
    L, C = y0.shape
    tr = _pick(L, (256, 128))

    def body(y_ref, u_ref, d_ref, w_ref, b_ref, do_ref, dy_ref, dud_ref, z_ref, dzg_ref, db_ref, dd_ref):
        u = u_ref[...]
        y = y_ref[...] + d_ref[...] * u
        z, t = _gelu(y)
        zg = _dot(z, w_ref[...]) + b_ref[...]
        s = _sigmoid(zg)
        do = do_ref[...]
        dzg = do * z * s * (1.0 - s)
        dz = do * s + _dot_nt(dzg, w_ref[...])
        c = math.sqrt(2.0 / math.pi)
        dgelu = 0.5 * (1.0 + t) + 0.5 * y * (1.0 - t * t) * c * (1.0 + 3.0 * 0.044715 * y * y)
        dy = dz * dgelu

        @pl.when(pl.program_id(0) == 0)
        def _():
            db_ref[...] = jnp.zeros_like(db_ref)
            dd_ref[...] = jnp.zeros_like(dd_ref)

        db_ref[...] += jnp.sum(dzg, axis=0, keepdims=True)
        dd_ref[...] += jnp.sum(dy * u, axis=0, keepdims=True)
        dy_ref[...] = dy
        dud_ref[...] = dy * d_ref[...]
        z_ref[...] = z.astype(BF16)
        dzg_ref[...] = dzg.astype(BF16)

    row = pl.BlockSpec((tr, C), lambda i: (i, 0))
    vec = pl.BlockSpec((1, C), lambda i: (0, 0))
    wsp = pl.BlockSpec((C, C), lambda i: (0, 0))
    return pl.pallas_call(body, grid=(L // tr,), in_specs=[row, row, vec, wsp, vec, row],
                          out_specs=[row, row, row, row, vec, vec],
                          out_shape=[jax.ShapeDtypeStruct((L, C), F32), jax.ShapeDtypeStruct((L, C), F32),
                                     jax.ShapeDtypeStruct((L, C), BF16), jax.ShapeDtypeStruct((L, C), BF16),
                                     jax.ShapeDtypeStruct((1, C), F32), jax.ShapeDtypeStruct((1, C), F32)],
                          compiler_params=_params(dimension_semantics=("arbitrary",)), name=name)(
        y0, proj, dvec, glu_w, glu_b, dcat)


def _dot_tri(tri, x, tri_left=True):
    t = tri.astype(BF16)
    x1 = x.astype(BF16)
    r1 = x - x1.astype(F32)
    x2 = r1.astype(BF16)
    x3 = (r1 - x2.astype(F32)).astype(BF16)
    dot = (lambda p: jnp.dot(t, p, preferred_element_type=F32)) if tri_left else (
        lambda p: jnp.dot(p, t, preferred_element_type=F32))
    return dot(x1) + dot(x2) + dot(x3)


def _hg_gates(xq, xf, lb, tri):
    C = xq.shape[0]
    sq = _sigmoid(xq)
    q = xq * sq
    sg = _sigmoid(xf)
    f = lb + (1.0 - lb) * sg
    kk = 1.0 - f
    b = _dot_tri(tri, jnp.log(f))
    bm = b[C // 2 - 1:C // 2, :]
    bl = b[C - 1:C, :]
    eb = jnp.exp(b)
    eqm, ekm, ekl = jnp.exp(b - bm), jnp.exp(bm - b), jnp.exp(bl - b)
    return dict(sq=sq, q=q, sg=sg, f=f, kk=kk, eb=eb, ebl=jnp.exp(bl), eqm=eqm, ekm=ekm, ekl=ekl,
                qb=q * eb, qt=q * eqm, kt=kk * ekm, kh=kk * ekl)


def _tri(C, lower):
    r = lax.broadcasted_iota(jnp.int32, (C, C), 0)
    c = lax.broadcasted_iota(jnp.int32, (C, C), 1)
    return (r >= c) if lower else (c >= r)


def hgrn_fwd(proj, lb, norm_g, cat, *, name):
    L = proj.shape[0]
    C, H, K = HG_CHUNK, HG_HEADS, HG_DIM
    HK = H * K
    nc = L // C

    def body(q_ref, f_ref, i_ref, g_ref, lb_ref, ng_ref, cat_ref, o_ref, sall_ref, st_ref):
        @pl.when(pl.program_id(0) == 0)
        def _():
            st_ref[...] = jnp.zeros_like(st_ref)

        mask = _tri(C, True)
        sts = [st_ref[h] for h in range(H)]
        for s in range(S):
            rs = slice(s * C, (s + 1) * C)
            gt = _hg_gates(q_ref[rs, :], f_ref[rs, :], lb_ref[...], mask.astype(F32))
            v_all = i_ref[rs, :]
            outs = []
            for h in range(H):
                sl = slice(h * K, (h + 1) * K)
                v, st = v_all[:, sl], sts[h]
                sall_ref[s, h] = st
                att = jnp.where(mask, _dot_nt(gt["qt"][:, sl], gt["kt"][:, sl]), 0.0)
                o = _dot(att, v) + _dot_nt(gt["qb"][:, sl], st)
                sts[h] = st * gt["ebl"][:, sl] + _dot_tn(v, gt["kh"][:, sl])
                outs.append(o * lax.rsqrt(jnp.mean(o * o, axis=-1, keepdims=True) + NORM_EPS))
            xg = g_ref[rs, :]
            o_ref[rs, :] = (jnp.concatenate(outs, axis=1) * ng_ref[...] * (xg * _sigmoid(xg))).astype(o_ref.dtype)
        for h in range(H):
            st_ref[h] = sts[h]

    S = HG_STEP_CHUNKS

    def blk(cb):
        return pl.BlockSpec((S * C, HK), lambda i: (i, cb))

    vec = pl.BlockSpec((1, HK), lambda i: (0, 0))
    return pl.pallas_call(
        body, grid=(nc // S,), in_specs=[blk(1), blk(2), blk(3), blk(4), vec, vec, pl.BlockSpec(memory_space=pl.ANY)],
        out_specs=[pl.BlockSpec((S * C, HK), lambda i: (i, 1)), pl.BlockSpec((S, H, K, K), lambda i: (i, 0, 0, 0))],
        out_shape=[jax.ShapeDtypeStruct((L, 2 * HK), BF16), jax.ShapeDtypeStruct((nc, H, K, K), F32)],
        input_output_aliases={6: 0},
        scratch_shapes=[pltpu.VMEM((H, K, K), F32)],
        compiler_params=_params(dimension_semantics=("arbitrary",)), name=name,
    )(proj, proj, proj, proj, lb, norm_g, cat)


def hgrn_bwd(proj, lb, norm_g, sall, dcat, du, *, name):
    L = proj.shape[0]
    C, H, K = HG_CHUNK, HG_HEADS, HG_DIM
    HK = H * K
    nc = L // C

    def body(q_ref, f_ref, i_ref, g_ref, lb_ref, ng_ref, sall_ref, do_ref, du_ref, dx_ref, dlb_ref, dng_ref, dst_ref):
        @pl.when(pl.program_id(0) == 0)
        def _():
            dst_ref[...] = jnp.zeros_like(dst_ref)
            dlb_ref[...] = jnp.zeros_like(dlb_ref)
            dng_ref[...] = jnp.zeros_like(dng_ref)

        mask = _tri(C, True)
        lb_all, ng = lb_ref[...], ng_ref[...]
        dx_ref[:, 0:HK] = du_ref[...]
        dsts = [dst_ref[h] for h in range(H)]
        for s in reversed(range(S)):
            rs = slice(s * C, (s + 1) * C)
            dsts = chunk_bwd(rs, s, dsts, mask, lb_all, ng, q_ref, f_ref, i_ref, g_ref, sall_ref, do_ref,
                             dx_ref, dlb_ref, dng_ref)
        for h in range(H):
            dst_ref[h] = dsts[h]

    def chunk_bwd(rs, s, dsts, mask, lb_all, ng, q_ref, f_ref, i_ref, g_ref, sall_ref, do_ref, dx_ref, dlb_ref, dng_ref):
        xq, xg, v_all = q_ref[rs, :], g_ref[rs, :], i_ref[rs, :]
        gt = _hg_gates(xq, f_ref[rs, :], lb_all, mask.astype(F32))
        sgg = _sigmoid(xg)
        d_ob = do_ref[rs, :]
        d_on = d_ob * (xg * sgg)
        doh = d_on * ng
        ohs, d_qts, d_qbs, d_kts, d_khs, dvs, d_bls, new_dsts = [], [], [], [], [], [], [], []
        for h in range(H):
            sl = slice(h * K, (h + 1) * K)
            v, st, dst = v_all[:, sl], sall_ref[s, h], dsts[h]
            qt, kt, kh, qb = gt["qt"][:, sl], gt["kt"][:, sl], gt["kh"][:, sl], gt["qb"][:, sl]
            att = jnp.where(mask, _dot_nt(qt, kt), 0.0)
            o = _dot(att, v) + _dot_nt(qb, st)
            r = lax.rsqrt(jnp.mean(o * o, axis=-1, keepdims=True) + NORM_EPS)
            oh = o * r
            do = r * (doh[:, sl] - oh * jnp.mean(doh[:, sl] * oh, axis=-1, keepdims=True))
            datt = jnp.where(mask, _dot_nt(do, v), 0.0)
            dvs.append(_dot_tn(att, do) + _dot_nt(kh, dst))
            d_qbs.append(_dot_x3(do, st))
            d_qts.append(_dot_x3(datt, kt))
            d_kts.append(_dot_x3(datt, qt, ((0,), (0,))))
            d_kh = _dot_x3(v, dst)
            d_khs.append(d_kh)
            d_bls.append(jnp.sum(dst * st, axis=0, keepdims=True) * gt["ebl"][:, sl]
                         + jnp.sum(d_kh * kh, axis=0, keepdims=True))
            new_dsts.append(dst * gt["ebl"][:, sl] + _dot_tn(do, qb))
            ohs.append(oh)
        oh, d_qt, d_qb, d_kt, d_kh, dv, d_bl = (jnp.concatenate(p, axis=1) for p in
                                                (ohs, d_qts, d_qbs, d_kts, d_khs, dvs, d_bls))
        dxg = d_ob * (oh * ng) * (sgg * (1.0 + xg * (1.0 - sgg)))
        dng_ref[...] += jnp.sum(d_on * oh, axis=0, keepdims=True)
        dq = d_qt * gt["eqm"] + d_qb * gt["eb"]
        db = d_qt * gt["qt"] + d_qb * gt["qb"] - d_kt * gt["kt"] - d_kh * gt["kh"]
        rowi = lax.broadcasted_iota(jnp.int32, (C, HK), 0)
        db = db + jnp.where(rowi == C - 1, d_bl, 0.0)
        dkk = d_kt * gt["ekm"] + d_kh * gt["ekl"]
        dlg = _dot_tri(_tri(C, False).astype(F32), db)
        df = dlg / gt["f"] - dkk
        sg, sq = gt["sg"], gt["sq"]
        dlb_ref[...] += jnp.sum(df * (1.0 - sg), axis=0, keepdims=True)
        dx_ref[rs, HK:2 * HK] = (dq * (sq * (1.0 + xq * (1.0 - sq)))).astype(dx_ref.dtype)
        dx_ref[rs, 2 * HK:3 * HK] = (df * (1.0 - lb_all) * sg * (1.0 - sg)).astype(dx_ref.dtype)
        dx_ref[rs, 3 * HK:4 * HK] = dv.astype(dx_ref.dtype)
        dx_ref[rs, 4 * HK:5 * HK] = dxg.astype(dx_ref.dtype)
        return new_dsts

    S = HG_STEP_CHUNKS
    ns = nc // S

    def blk(cb):
        return pl.BlockSpec((S * C, HK), lambda i: (ns - 1 - i, cb))

    vec = pl.BlockSpec((1, HK), lambda i: (0, 0))
    return pl.pallas_call(
        body, grid=(ns,),
        in_specs=[blk(1), blk(2), blk(3), blk(4), vec, vec,
                  pl.BlockSpec((S, H, K, K), lambda i: (ns - 1 - i, 0, 0, 0)), blk(1), blk(0)],
        out_specs=[pl.BlockSpec((S * C, 5 * HK), lambda i: (ns - 1 - i, 0)), vec, vec],
        out_shape=[jax.ShapeDtypeStruct((L, 5 * HK), BF16), jax.ShapeDtypeStruct((1, HK), F32),
                   jax.ShapeDtypeStruct((1, HK), F32)],
        scratch_shapes=[pltpu.VMEM((H, K, K), F32)],
        compiler_params=_params(dimension_semantics=("arbitrary",)), name=name,
    )(proj, proj, proj, proj, lb, norm_g, sall, dcat, du)


def _shift_down(x, k, row):
    return jnp.where(row >= k, pltpu.roll(x, k, 0), 0.0)


def _shift_up(x, k, row):
    n = x.shape[0]
    return jnp.where(row < n - k, pltpu.roll(x, n - k, 0), 0.0)


def convgate_fwd(hu, conv_w, conv_b, *, name):
    L, C2 = hu.shape
    C = C2 // 2
    tc = _pick(C, (256, 128))
    nb = C // tc

    def body(a_ref, b_ref, wa_ref, wb_ref, ba_ref, bb_ref, o_ref):
        row = lax.broadcasted_iota(jnp.int32, (L, tc), 0)

        def conv(x, w, bias):
            return w[2:3, :] * x + w[1:2, :] * _shift_down(x, 1, row) + w[0:1, :] * _shift_down(x, 2, row) + bias

        ca = conv(a_ref[...], wa_ref[...], ba_ref[...])
        cb = conv(b_ref[...], wb_ref[...], bb_ref[...])
        o_ref[...] = (ca * _sigmoid(ca) * cb).astype(o_ref.dtype)

    def col(off, rows):
        return pl.BlockSpec((rows, tc), lambda j: (0, j + off))

    return pl.pallas_call(
        body, grid=(nb,), in_specs=[col(0, L), col(nb, L), col(0, 3), col(nb, 3), col(0, 1), col(nb, 1)],
        out_specs=col(0, L), out_shape=jax.ShapeDtypeStruct((L, C), BF16),
        compiler_params=_params(dimension_semantics=("parallel",)), name=name,
    )(hu, hu, conv_w, conv_w, conv_b, conv_b)


def convgate_bwd(hu, conv_w, conv_b, dact, *, name):
    L, C2 = hu.shape
    C = C2 // 2
    tc = _pick(C, (256, 128))
    nb = C // tc

    def body(a_ref, b_ref, wa_ref, wb_ref, ba_ref, bb_ref, d_ref, dxa_ref, dxb_ref, dwa_ref, dwb_ref, dba_ref, dbb_ref):
        row = lax.broadcasted_iota(jnp.int32, (L, tc), 0)

        def conv(x, w, bias):
            x1 = _shift_down(x, 1, row)
            x2 = _shift_down(x, 2, row)
            return w[2:3, :] * x + w[1:2, :] * x1 + w[0:1, :] * x2 + bias, x1, x2

        xa, xb = a_ref[...], b_ref[...]
        wa, wb = wa_ref[...], wb_ref[...]
        ca, xa1, xa2 = conv(xa, wa, ba_ref[...])
        cb, xb1, xb2 = conv(xb, wb, bb_ref[...])
        d = d_ref[...]
        sa = _sigmoid(ca)
        dca = d * cb * (sa * (1.0 + ca * (1.0 - sa)))
        dcb = d * (ca * sa)

        def back(dc, w, x, x1, x2, dx_ref, dw_ref, db_ref):
            dx = w[2:3, :] * dc + w[1:2, :] * _shift_up(dc, 1, row) + w[0:1, :] * _shift_up(dc, 2, row)
            dx_ref[...] = dx.astype(dx_ref.dtype)
            dw_ref[...] = jnp.concatenate([jnp.sum(dc * x2, axis=0, keepdims=True),
                                           jnp.sum(dc * x1, axis=0, keepdims=True),
                                           jnp.sum(dc * x, axis=0, keepdims=True)], axis=0)
            db_ref[...] = jnp.sum(dc, axis=0, keepdims=True)

        back(dca, wa, xa, xa1, xa2, dxa_ref, dwa_ref, dba_ref)
        back(dcb, wb, xb, xb1, xb2, dxb_ref, dwb_ref, dbb_ref)

    def col(off, rows):
        return pl.BlockSpec((rows, tc), lambda j: (0, j + off))

    outs = pl.pallas_call(
        body, grid=(nb,),
        in_specs=[col(0, L), col(nb, L), col(0, 3), col(nb, 3), col(0, 1), col(nb, 1), col(0, L)],
        out_specs=[col(0, L), col(0, L), col(0, 3), col(0, 3), col(0, 1), col(0, 1)],
        out_shape=[jax.ShapeDtypeStruct((L, C), BF16)] * 2 + [jax.ShapeDtypeStruct((3, C), F32)] * 2
        + [jax.ShapeDtypeStruct((1, C), F32)] * 2,
        compiler_params=_params(dimension_semantics=("parallel",)), name=name,
    )(hu, hu, conv_w, conv_w, conv_b, conv_b, dact)
    dxa, dxb, dwa, dwb, dba, dbb = outs
    return (dxa, dxb), jnp.concatenate([dwa, dwb], axis=1), jnp.concatenate([dba, dbb], axis=1)


def rope_tables(positions):
    half = ROT_DIM // 2
    inv_freq = ROPE_THETA ** (-jnp.arange(half, dtype=F32) * 2.0 / ROT_DIM)
    ang = positions.astype(F32)[:, None] * inv_freq
    cos, sin = jnp.cos(ang), jnp.sin(ang)
    L = positions.shape[0]
    one = jnp.ones((L, ATT_E - ROT_DIM), F32)
    zero = jnp.zeros((L, ATT_E - ROT_DIM), F32)
    zh = jnp.zeros((L, half), F32)
    tc = jnp.concatenate([cos, cos, one], axis=1)
    ts1 = jnp.concatenate([zh, sin, zero], axis=1)
    ts2 = jnp.concatenate([-sin, zh, zero], axis=1)
    return tuple(jnp.concatenate([t, t], axis=1) for t in (tc, ts1, ts2))


def norm_mm(x, g, w_t, *, tabs=None, name):
    L, D = x.shape
    N = w_t.shape[0]
    W = 512
    tm = _pick(L, (1024, 512, 256, 128))
    nq = N // (3 * W)
    scale = ATT_E ** -0.5
    rope = tabs is not None

    def body(x_ref, g_ref, b_ref, *rest):
        hn_ref, o_ref, hn_scr = rest[-3:]
        j = pl.program_id(1)

        @pl.when(j == 0)
        def _():
            xv = x_ref[...]
            r = lax.rsqrt(jnp.mean(xv * xv, axis=-1, keepdims=True) + NORM_EPS)
            hn = (xv * r * g_ref[...]).astype(BF16)
            hn_scr[...] = hn
            hn_ref[...] = hn

        out = _dot_nt(hn_scr[...], b_ref[...])
        if rope:
            c_ref, s1_ref, s2_ref = rest[:3]
            c = jnp.concatenate([c_ref[...]] * 4, axis=1)
            s1 = jnp.concatenate([s1_ref[...]] * 4, axis=1)
            s2 = jnp.concatenate([s2_ref[...]] * 4, axis=1)
            rot = out * c + pltpu.roll(out, 8, 1) * s1 + pltpu.roll(out, W - 8, 1) * s2
            out = jnp.where(j < 2 * nq, rot * jnp.where(j < nq, scale, 1.0), out)
        o_ref[...] = out

    row = pl.BlockSpec((tm, D), lambda i, j: (i, 0))
    tab = pl.BlockSpec((tm, 128), lambda i, j: (i, 0))
    return pl.pallas_call(body, grid=(L // tm, N // W),
                          in_specs=[row, pl.BlockSpec((1, D), lambda i, j: (0, 0)), pl.BlockSpec((W, D), lambda i, j: (j, 0))]
                          + ([tab, tab, tab] if rope else []),
                          out_specs=[row, pl.BlockSpec((tm, W), lambda i, j: (i, j))],
                          out_shape=[jax.ShapeDtypeStruct((L, D), BF16), jax.ShapeDtypeStruct((L, N), F32)],
                          scratch_shapes=[pltpu.VMEM((tm, D), BF16)],
                          compiler_params=_params(dimension_semantics=("parallel", "arbitrary")), name=name)(
        x, g.reshape(1, D), w_t, *(tabs or ()))


def rope_bwd(slabs, tabs, *, name):
    L, W = slabs[0].shape
    tr = _pick(L, (256, 128))
    nq = len(slabs) // 3
    scale = ATT_E ** -0.5

    def body(*refs):
        d_refs, (c_ref, s1_ref, s2_ref, o_ref) = refs[:3 * nq], refs[3 * nq:]
        c = jnp.concatenate([c_ref[...]] * 4, axis=1)
        s1 = jnp.concatenate([s1_ref[...]] * 4, axis=1)
        s2 = jnp.concatenate([s2_ref[...]] * 4, axis=1)
        for j, d_ref in enumerate(d_refs):
            dy = d_ref[...]
            if j < 2 * nq:
                dy = dy * c + pltpu.roll(dy * s1, W - 8, 1) + pltpu.roll(dy * s2, 8, 1)
            if j < nq:
                dy = dy * scale
            o_ref[:, j * W:(j + 1) * W] = dy.astype(o_ref.dtype)

    slab = pl.BlockSpec((tr, W), lambda i: (i, 0))
    tab = pl.BlockSpec((tr, 128), lambda i: (i, 0))
    return pl.pallas_call(body, grid=(L // tr,), in_specs=[slab] * (3 * nq) + [tab, tab, tab],
                          out_specs=pl.BlockSpec((tr, 3 * nq * W), lambda i: (i, 0)),
                          out_shape=jax.ShapeDtypeStruct((L, 3 * nq * W), BF16),
                          compiler_params=_params(dimension_semantics=("parallel",)), name=name)(*slabs, *tabs)


def _att_masks(has_prev):
    qi = lax.broadcasted_iota(jnp.int32, (ATT_BLOCK, ATT_BLOCK), 0)
    kj = lax.broadcasted_iota(jnp.int32, (ATT_BLOCK, ATT_BLOCK), 1)
    return qi >= kj, (kj >= qi) & has_prev


ATT_COLS = 128


def _att_rows(j, d, nb):
    B = ATT_BLOCK
    r, n = j // nb, j % nb
    start = r + d * B * n
    has_prev = n > 0
    pstart = jnp.where(has_prev, start - d * B, start)
    if d == 1:
        return pl.ds(pl.multiple_of(start, B), B), pl.ds(pl.multiple_of(pstart, B), B), has_prev
    return pl.ds(start, B, stride=d), pl.ds(pstart, B, stride=d), has_prev


def _qkv_specs(L, g):
    per = ATT_HPG * ATT_E // ATT_COLS
    third = len(ATT_DILATIONS) * per
    return [pl.BlockSpec((L, ATT_COLS), lambda c, base=base: (0, base + c))
            for base in (g * per, third + g * per, 2 * third + g * per)]


def attn_fwd(qkv, g, d, *, name):
    L, W = qkv.shape[0], ATT_HPG * ATT_E
    B, E = ATT_BLOCK, ATT_E
    nblk = L // B
    nb = nblk // d

    def body(q_ref, k_ref, v_ref, o_ref, l_ref):
        def step(j, carry):
            cur, prv, has_prev = _att_rows(j, d, nb)
            mc, mp = _att_masks(has_prev)
            qb, kc, kp, vc, vp = q_ref[cur, :], k_ref[cur, :], k_ref[prv, :], v_ref[cur, :], v_ref[prv, :]
            outs, lses = [], []
            for h in range(ATT_COLS // E):
                sl = slice(h * E, (h + 1) * E)
                sc = jnp.where(mc, _dot_nt(qb[:, sl], kc[:, sl]), NEG_BIG)
                sp = jnp.where(mp, _dot_nt(qb[:, sl], kp[:, sl]), NEG_BIG)
                m = jnp.maximum(jnp.max(sc, axis=-1, keepdims=True), jnp.max(sp, axis=-1, keepdims=True))
                pc = jnp.exp(sc - m)
                pp = jnp.exp(sp - m)
                den = jnp.sum(pc, axis=-1, keepdims=True) + jnp.sum(pp, axis=-1, keepdims=True)
                outs.append((_dot(pc, vc[:, sl]) + _dot(pp, vp[:, sl])) / den)
                lses.append(jnp.broadcast_to(m + jnp.log(den), (B, E)))
            o_ref[cur, :] = jnp.concatenate(outs, axis=1)
            l_ref[cur, :] = jnp.concatenate(lses, axis=1)
            return carry

        lax.fori_loop(0, nblk, step, 0, unroll=8)

    col = pl.BlockSpec((L, ATT_COLS), lambda c: (0, c))
    return pl.pallas_call(body, grid=(W // ATT_COLS,), in_specs=_qkv_specs(L, g), out_specs=[col] * 2,
                          out_shape=[jax.ShapeDtypeStruct((L, W), F32)] * 2,
                          compiler_params=_params(dimension_semantics=("parallel",)), name=name)(qkv, qkv, qkv)


def attn_bwd(qkv, g, lse, do, dl, d, *, name):
    L, W = qkv.shape[0], ATT_HPG * ATT_E
    B, E = ATT_BLOCK, ATT_E
    nblk = L // B
    nb = nblk // d

    def body(q_ref, k_ref, v_ref, l_ref, do_ref, dl_ref, dq_ref, dk_ref, dv_ref):
        dk_ref[...] = jnp.zeros_like(dk_ref)
        dv_ref[...] = jnp.zeros_like(dv_ref)

        def step(j, carry):
            cur, prv, has_prev = _att_rows(j, d, nb)
            mc, mp = _att_masks(has_prev)
            qb, kc, kp, vc, vp = q_ref[cur, :], k_ref[cur, :], k_ref[prv, :], v_ref[cur, :], v_ref[prv, :]
            lb, dob, dlb = l_ref[cur, :], do_ref[cur, :], dl_ref[cur, :]
            dqs, dkc, dkp, dvc, dvp = [], [], [], [], []
            for h in range(ATT_COLS // E):
                sl = slice(h * E, (h + 1) * E)
                qh, doh = qb[:, sl], dob[:, sl]
                lse_h, dl_h = lb[:, h * E:h * E + 1], dlb[:, h * E:h * E + 1]
                pc = jnp.where(mc, jnp.exp(_dot_nt(qh, kc[:, sl]) - lse_h), 0.0)
                pp = jnp.where(mp, jnp.exp(_dot_nt(qh, kp[:, sl]) - lse_h), 0.0)
                dsc = pc * (_dot_nt(doh, vc[:, sl]) - dl_h)
                dsp = pp * (_dot_nt(doh, vp[:, sl]) - dl_h)
                dqs.append(_dot(dsc, kc[:, sl]) + _dot(dsp, kp[:, sl]))
                dkc.append(_dot_tn(dsc, qh))
                dkp.append(_dot_tn(dsp, qh))
                dvc.append(_dot_tn(pc, doh))
                dvp.append(_dot_tn(pp, doh))
            dq_ref[cur, :] = jnp.concatenate(dqs, axis=1)
            dk_ref[cur, :] = dk_ref[cur, :] + jnp.concatenate(dkc, axis=1)
            dv_ref[cur, :] = dv_ref[cur, :] + jnp.concatenate(dvc, axis=1)
            dk_ref[prv, :] = dk_ref[prv, :] + jnp.concatenate(dkp, axis=1)
            dv_ref[prv, :] = dv_ref[prv, :] + jnp.concatenate(dvp, axis=1)
            return carry

        lax.fori_loop(0, nblk, step, 0, unroll=4)

    col = pl.BlockSpec((L, ATT_COLS), lambda c: (0, c))
    return pl.pallas_call(body, grid=(W // ATT_COLS,), in_specs=_qkv_specs(L, g) + [col] * 3, out_specs=[col] * 3,
                          out_shape=[jax.ShapeDtypeStruct((L, W), F32)] * 3,
                          compiler_params=_params(dimension_semantics=("parallel",)), name=name)(
        qkv, qkv, qkv, lse, do, dl)


def _merge_alpha(l_refs):
    ls = [r[...] for r in l_refs]
    m = jnp.maximum(jnp.maximum(ls[0], ls[1]), ls[2])
    es = [jnp.exp(l - m) for l in ls]
    den = es[0] + es[1] + es[2]
    return [e / den for e in es]


def merge_fwd(os_, ls_, *, name):
    L, W = os_[0].shape
    tr = _pick(L, (256, 128))

    def body(o0, o1, o2, l0, l1, l2, out_ref):
        al = _merge_alpha((l0, l1, l2))
        out_ref[...] = (al[0] * o0[...] + al[1] * o1[...] + al[2] * o2[...]).astype(out_ref.dtype)

    row = pl.BlockSpec((tr, W), lambda i: (i, 0))
    return pl.pallas_call(body, grid=(L // tr,), in_specs=[row] * 6, out_specs=row,
                          out_shape=jax.ShapeDtypeStruct((L, W), BF16), name=name)(*os_, *ls_)


def merge_bwd(os_, ls_, do, *, name):
    L, W = do.shape
    tr = _pick(L, (256, 128))

    def body(o0, o1, o2, l0, l1, l2, do_ref, d0, d1, d2, e0, e1, e2):
        al = _merge_alpha((l0, l1, l2))
        dov = do_ref[...]
        r = lax.broadcasted_iota(jnp.int32, (W, W), 0) // ATT_E
        c = lax.broadcasted_iota(jnp.int32, (W, W), 1) // ATT_E
        ones_blk = (r == c).astype(F32)
        t = jnp.zeros_like(dov)
        for a, o in zip(al, (o0, o1, o2)):
            t = t + a * _dot_tri(ones_blk, dov * o[...], tri_left=False)
        for a, d_ref, e_ref in zip(al, (d0, d1, d2), (e0, e1, e2)):
            d_ref[...] = a * dov
            e_ref[...] = a * t

    row = pl.BlockSpec((tr, W), lambda i: (i, 0))
    return pl.pallas_call(body, grid=(L // tr,), in_specs=[row] * 7, out_specs=[row] * 6,
                          out_shape=[jax.ShapeDtypeStruct((L, W), F32)] * 6, name=name)(*os_, *ls_, do)


def _me_and_peers():
    x, y, c = lax.axis_index("x"), lax.axis_index("y"), lax.axis_index("c")
    peers = []
    for k in range(1, N_DEV):
        px = 1 - x if k & 4 else x
        py = 1 - y if k & 2 else y
        pc = 1 - c if k & 1 else c
        peers.append((px, py, pc))
    return (x, y, c), peers


def _index(dev):
    return 4 * dev[0] + 2 * dev[1] + dev[2]


def _hbm(a):
    return pltpu.with_memory_space_constraint(a, pltpu.HBM)


HBM_SPEC = pl.BlockSpec(memory_space=pltpu.HBM)
SEM_SPEC = pl.BlockSpec(memory_space=pltpu.SEMAPHORE)
DATAFLOW = pltpu.SideEffectType.DATAFLOW_SIDE_EFFECTING


def _remote(src_ref, land_ref, slotted, me, peer, src_is_mine, send_sem, recv_sem, k):
    sender, receiver = (me, peer) if src_is_mine else (peer, me)
    src = src_ref.at[_index(receiver)] if slotted else src_ref
    return pltpu.make_async_remote_copy(src_ref=src, dst_ref=land_ref.at[_index(sender)], send_sem=send_sem.at[k],
                                        recv_sem=recv_sem.at[k], device_id=peer, device_id_type=MESH_ID)


SIBLING = 0
SAME_CORE = (1, 3, 5)
OTHER_CORE = (2, 4, 6)


def copies_start(arrays, mode, *, name):
    n = len(arrays)
    slotted = mode == "exchange"
    lands = [lax.empty(a.shape if slotted else (N_DEV,) + a.shape, a.dtype) for a in arrays]
    targets = (SIBLING,) + SAME_CORE if mode == "gather2" else tuple(range(N_DEV - 1))

    def body(*refs):
        x_refs, land_refs = refs[:n], refs[n:2 * n]
        send, recv = refs[2 * n:3 * n], refs[3 * n:4 * n]
        token = refs[-1]
        me, peers = _me_and_peers()
        for w in range(n):
            for k in targets:
                _remote(x_refs[w], land_refs[w], slotted, me, peers[k], True, send[w], recv[w], k).start()
            if not slotted:
                pltpu.make_async_copy(x_refs[w], land_refs[w].at[_index(me)], recv[w].at[N_DEV - 1]).start()
        token[...] = jnp.zeros_like(token)

    sem = pltpu.SemaphoreType.DMA((N_DEV,))
    out_shape = ([sem] * (2 * n) + [pltpu.HBM(a.shape, a.dtype) for a in arrays]
                 + [pltpu.HBM(l.shape, l.dtype) for l in lands] + [jax.ShapeDtypeStruct((8, 128), F32)])
    outs = pl.pallas_call(
        body, name=name, out_shape=out_shape, in_specs=[HBM_SPEC] * (2 * n),
        out_specs=[SEM_SPEC] * (2 * n) + [HBM_SPEC] * (2 * n) + [pl.BlockSpec(memory_space=pltpu.VMEM)],
        input_output_aliases={i: 2 * n + i for i in range(2 * n)},
        compiler_params=pltpu.CompilerParams(has_side_effects=DATAFLOW),
    )(*[_hbm(a) for a in arrays], *[_hbm(l) for l in lands])
    handles = [(outs[w], outs[n + w], outs[2 * n + w], outs[3 * n + w]) for w in range(n)]
    return handles, outs[-1]


def _forward(land_ref, me, peers, j, fsend, frecv, mine):
    block = _index(peers[SAME_CORE[j]] if mine else peers[OTHER_CORE[j]])
    return pltpu.make_async_remote_copy(src_ref=land_ref.at[block], dst_ref=land_ref.at[block], send_sem=fsend.at[j],
                                        recv_sem=frecv.at[j], device_id=peers[SIBLING], device_id_type=MESH_ID)


def copies_forward(handles, after, *, name):
    n = len(handles)

    def body(*refs):
        land_refs, recv = refs[:n], refs[n:2 * n]
        fsend, frecv = refs[2 * n + 1:3 * n + 1], refs[3 * n + 1:4 * n + 1]
        token = refs[-1]
        me, peers = _me_and_peers()
        for w in range(n):
            for j, k in enumerate(SAME_CORE):
                block = land_refs[w].at[_index(peers[k])]
                pltpu.make_async_remote_copy(src_ref=block, dst_ref=block, send_sem=recv[w].at[N_DEV - 1],
                                             recv_sem=recv[w].at[k], device_id=peers[k], device_id_type=MESH_ID).wait_recv()
                _forward(land_refs[w], me, peers, j, fsend[w], frecv[w], True).start()
        token[...] = jnp.zeros_like(token)

    sem = pltpu.SemaphoreType.DMA((len(SAME_CORE),))
    lands = [h[3] for h in handles]
    outs = pl.pallas_call(
        body, name=name,
        out_shape=[sem] * (2 * n) + [pltpu.HBM(l.shape, l.dtype) for l in lands] + [jax.ShapeDtypeStruct((8, 128), F32)],
        in_specs=[HBM_SPEC] * n + [SEM_SPEC] * n + [pl.BlockSpec(memory_space=pl.ANY)],
        out_specs=[SEM_SPEC] * (2 * n) + [HBM_SPEC] * n + [pl.BlockSpec(memory_space=pltpu.VMEM)],
        input_output_aliases={w: 2 * n + w for w in range(n)},
        compiler_params=pltpu.CompilerParams(has_side_effects=DATAFLOW),
    )(*lands, *[h[1] for h in handles], after)
    new = [(h[0], h[1], h[2], outs[2 * n + w], outs[w], outs[n + w]) for w, h in enumerate(handles)]
    return new, outs[-1]


def copies_wait(handle, mode, after, *, name):
    slotted = mode == "exchange"
    two_level = mode == "gather2"
    send_sem, recv_sem, x_thru, land_thru = handle[:4]
    targets = (SIBLING,) + SAME_CORE if two_level else tuple(range(N_DEV - 1))
    arrivals = (SIBLING,) if two_level else targets

    def body(x_ref, land_ref, send_ref, recv_ref, *rest):
        me, peers = _me_and_peers()
        for k in targets:
            _remote(x_ref, land_ref, slotted, me, peers[k], True, send_ref, recv_ref, k).wait_send()
        for k in arrivals:
            _remote(x_ref, land_ref, slotted, me, peers[k], False, send_ref, recv_ref, k).wait_recv()
        if not slotted:
            pltpu.make_async_copy(x_ref, land_ref.at[_index(me)], recv_ref.at[N_DEV - 1]).wait()
        if two_level:
            fsend, frecv = rest[0], rest[1]
            for j in range(len(SAME_CORE)):
                _forward(land_ref, me, peers, j, fsend, frecv, True).wait_send()
                _forward(land_ref, me, peers, j, fsend, frecv, False).wait_recv()

    extra = list(handle[4:])
    return pl.pallas_call(
        body, name=name, out_shape=(pltpu.HBM(x_thru.shape, x_thru.dtype), pltpu.HBM(land_thru.shape, land_thru.dtype)),
        in_specs=[HBM_SPEC, HBM_SPEC, SEM_SPEC, SEM_SPEC] + [SEM_SPEC] * len(extra) + [pl.BlockSpec(memory_space=pl.ANY)],
        out_specs=(HBM_SPEC, HBM_SPEC), input_output_aliases={0: 0, 1: 1},
        compiler_params=pltpu.CompilerParams(has_side_effects=DATAFLOW),
    )(x_thru, land_thru, send_sem, recv_sem, *extra, after)


def cast_bf16(x, *, dep=None, name):
    R, C = x.shape
    tr = _pick(R, (512, 352, 256, 128, 64))
    deps = [] if dep is None else [dep]

    def body(x_ref, *rest):
        rest[-1][...] = x_ref[...].astype(BF16)

    row = pl.BlockSpec((tr, C), lambda i: (i, 0))
    return pl.pallas_call(body, grid=(R // tr,), in_specs=[row] + [pl.BlockSpec((8, 128), lambda i: (0, 0))] * len(deps),
                          out_specs=row, out_shape=jax.ShapeDtypeStruct((R, C), BF16), name=name)(x, *deps)


def cast_bf16_layer(x3, layer, *, name):
    _, R, C = x3.shape
    tr = _pick(R, (512, 352, 256, 128, 64))

    def body(x_ref, o_ref):
        o_ref[...] = x_ref[...].astype(BF16)

    return pl.pallas_call(body, grid=(R // tr,), in_specs=[pl.BlockSpec((None, tr, C), lambda i: (layer, i, 0))],
                          out_specs=pl.BlockSpec((tr, C), lambda i: (i, 0)),
                          out_shape=jax.ShapeDtypeStruct((R, C), BF16), name=name)(x3)


BD_PARTS = 4


def _blockdiag_call(b, build, G, r, c, name):
    gp = G // BD_PARTS

    def body_build(b_ref, o_ref):
        o_ref[...] = jnp.zeros_like(o_ref)
        for g in range(G):
            o_ref[g // gp, (g % gp) * r:(g % gp + 1) * r, (g % gp) * c:(g % gp + 1) * c] = b_ref[g]

    def body_extract(d_ref, o_ref):
        for g in range(G):
            o_ref[g] = d_ref[g // gp, (g % gp) * r:(g % gp + 1) * r, (g % gp) * c:(g % gp + 1) * c]

    out = jax.ShapeDtypeStruct((BD_PARTS, gp * r, gp * c) if build else (G, r, c), F32)
    return pl.pallas_call(body_build if build else body_extract, out_shape=out, name=name)(b)


def make_blockdiag(G, r, c, name):
    @jax.custom_vjp
    def blockdiag(b):
        return _blockdiag_call(b, True, G, r, c, name + "_build")

    def fwd(b):
        return blockdiag(b), None

    def bwd(_, g):
        return (_blockdiag_call(g, False, G, r, c, name + "_extract"),)

    blockdiag.defvjp(fwd, bwd)
    return blockdiag


def cols_from_shards(g, *, name):
    _, K, n = g.shape
    tk = _pick(K, (256, 128))

    def body(g_ref, o_ref):
        for i in range(N_DEV):
            o_ref[:, i * n:(i + 1) * n] = g_ref[i]

    return pl.pallas_call(body, grid=(K // tk,), in_specs=[pl.BlockSpec((N_DEV, tk, n), lambda i: (0, i, 0))],
                          out_specs=pl.BlockSpec((tk, N_DEV * n), lambda i: (i, 0)),
                          out_shape=jax.ShapeDtypeStruct((K, N_DEV * n), g.dtype), name=name)(g)


def shards_from_cols(w, *, name):
    K, N = w.shape
    n = N // N_DEV
    tk = _pick(K, (256, 128))

    def body(w_ref, o_ref):
        for i in range(N_DEV):
            o_ref[i] = w_ref[:, i * n:(i + 1) * n].astype(o_ref.dtype)

    return pl.pallas_call(body, grid=(K // tk,), in_specs=[pl.BlockSpec((tk, N), lambda i: (i, 0))],
                          out_specs=pl.BlockSpec((N_DEV, tk, n), lambda i: (0, i, 0)),
                          out_shape=jax.ShapeDtypeStruct((N_DEV, K, n), BF16), name=name)(w)


def _adamw(w, g, m, v):
    m = ADAM_B1 * m + (1.0 - ADAM_B1) * g
    v = ADAM_B2 * v + (1.0 - ADAM_B2) * (g * g)
    m_hat = m / (1.0 - ADAM_B1 ** ADAM_STEP)
    v_hat = v / (1.0 - ADAM_B2 ** ADAM_STEP)
    delta = -ADAM_LR * (m_hat / (jnp.sqrt(v_hat) + ADAM_EPS) + ADAM_WD * w)
    return delta, m, v


def reduce_adamw(recv, own, own_slotted, me, w, m, v, *, layer=0, n_layers=1, into=None, name):
    _, R, C = recv.shape
    tr = _pick(R, (176, 192, 160, 184, 128, 64, 32, 16, 8))
    off = layer * (R // tr)

    def body(me_ref, r_ref, own_ref, w_ref, m_ref, v_ref, *rest):
        g_ref, d_ref, nm_ref, nv_ref = rest[-4:]
        mine = me_ref[0]
        g = None
        for i in range(N_DEV):
            part = jnp.where(mine == i, own_ref[...], r_ref[i]).astype(F32)
            g = part if g is None else g + part
        delta, nm, nv = _adamw(w_ref[...], g, m_ref[...], v_ref[...])
        g_ref[...] = g
        d_ref[...] = delta
        nm_ref[...] = nm
        nv_ref[...] = nv

    row = pl.BlockSpec((tr, C), lambda i, me_ref: (i + off, 0))
    own_spec = (pl.BlockSpec((None, tr, C), lambda i, me_ref: (me_ref[0], i, 0)) if own_slotted
                else pl.BlockSpec((tr, C), lambda i, me_ref: (i, 0)))
    rest = [] if into is None else list(into)
    grid_spec = pltpu.PrefetchScalarGridSpec(
        num_scalar_prefetch=1, grid=(R // tr,),
        in_specs=[pl.BlockSpec((N_DEV, tr, C), lambda i, me_ref: (0, i, 0)), own_spec, row, row, row]
        + [pl.BlockSpec(memory_space=pl.ANY)] * len(rest),
        out_specs=[row] * 4)
    return pl.pallas_call(body, grid_spec=grid_spec, out_shape=[jax.ShapeDtypeStruct((n_layers * R, C), F32)] * 4,
                          input_output_aliases={6 + k: k for k in range(len(rest))},
                          compiler_params=_params(dimension_semantics=("parallel",)), name=name)(
        me.reshape(1).astype(jnp.int32), recv, own, w, m, v, *rest)


def _s5_prepare(A_re, A_im, log_dt, B_re, B_im, C_re, C_im):
    G, P, Cg = S5_GROUPS, S5_STATE, S5_GROUP
    dt = jnp.exp(log_dt)[:, None]
    mag = jnp.exp(A_re * dt)
    ab_re = mag * jnp.cos(A_im * dt)
    ab_im = mag * jnp.sin(A_im * dt)
    den = A_re * A_re + A_im * A_im
    nr, ni = ab_re - 1.0, ab_im
    c_re = (nr * A_re + ni * A_im) / den
    c_im = (ni * A_re - nr * A_im) / den
    Bb_re = c_re[..., None] * B_re - c_im[..., None] * B_im
    Bb_im = c_re[..., None] * B_im + c_im[..., None] * B_re
    def dense_in(b, name):
        return make_blockdiag(G, Cg, P, name)(b.transpose(0, 2, 1))

    def dense_out(c, name):
        return make_blockdiag(G, P, Cg, name)(c.transpose(0, 2, 1))

    return (ab_re.reshape(1, G * P), ab_im.reshape(1, G * P), dense_in(Bb_re, "s5_wb_re"), dense_in(Bb_im, "s5_wb_im"),
            dense_out(C_re, "s5_wc_re"), dense_out(-C_im, "s5_wc_im"))


def _lower_bound(gamma):
    return jnp.cumsum(jax.nn.softmax(gamma, axis=0), axis=0)[0:1]


def _ffn_fwd(h, g_norm, get_w_in, conv_w, conv_b, get_w_out, tag, final=None):
    w_in = get_w_in(h)
    hn, hu = norm_mm(h, g_norm, w_in, name=tag + "_in")
    act = convgate_fwd(hu, conv_w, conv_b, name=tag + "_gate")
    w_out = get_w_out(act)
    if final is None:
        h_out = mm(act, w_out, res=h, name=tag + "_out")
    else:
        h_out = mm_final_loss(act, w_out, h, final[0], final[1], name=tag + "_out_loss")
    return h_out, (hn, hu, act), w_in, w_out


def _ffn_bwd(h, g_norm, w_in, conv_w, conv_b, w_out, saved, dh, tag, send_dw_in, send_dw_out):
    hn, hu, act = saved
    sent = send_dw_out(mm(act, dh, ta=True, out_dtype=BF16, name=tag + "_dwout"))
    dact = mm(dh, w_out, tb=True, dep=sent, name=tag + "_dact")
    (dhu_a, dhu_b), dconv_w, dconv_b = convgate_bwd(hu, conv_w, conv_b, dact, name=tag + "_dgate")
    rows = 2 * dhu_a.shape[1]
    dw_in = mm(dhu_a, hn, ta=True, out_dtype=BF16, out_rows=rows, name=tag + "_dwin_a")
    dw_in = mm(dhu_b, hn, ta=True, out_dtype=BF16, out_rows=rows, out_off=rows // 2, into=dw_in, name=tag + "_dwin_b")
    sent = send_dw_in(dw_in)
    dh_in, dg = mm_drms((dhu_a, dhu_b), w_in, h, g_norm, dh, dep=sent, name=tag + "_dhn")
    return dh_in, dg, dconv_w, dconv_b


def kernel(x, positions, norm_mix, norm_ffn, norm_final, mix_w_in, mix_w_out, s5_A_re, s5_A_im, s5_log_dt, s5_B_re, s5_B_im, s5_C_re, s5_C_im, s5_D, s5_glu_w, s5_glu_b, hgrn_gamma, hgrn_norm, att_w_qkv, att_w_o, ffn_w_in, ffn_conv_w, ffn_conv_b, ffn_w_out, loss_target, m_norm_mix, m_norm_ffn, m_norm_final, m_mix_w_in, m_mix_w_out, m_s5_A_re, m_s5_A_im, m_s5_log_dt, m_s5_B_re, m_s5_B_im, m_s5_C_re, m_s5_C_im, m_s5_D, m_s5_glu_w, m_s5_glu_b, m_hgrn_gamma, m_hgrn_norm, m_att_w_qkv, m_att_w_o, m_ffn_w_in, m_ffn_conv_w, m_ffn_conv_b, m_ffn_w_out, v_norm_mix, v_norm_ffn, v_norm_final, v_mix_w_in, v_mix_w_out, v_s5_A_re, v_s5_A_im, v_s5_log_dt, v_s5_B_re, v_s5_B_im, v_s5_C_re, v_s5_C_im, v_s5_D, v_s5_glu_w, v_s5_glu_b, v_hgrn_gamma, v_hgrn_norm, v_att_w_qkv, v_att_w_o, v_ffn_w_in, v_ffn_conv_w, v_ffn_conv_b, v_ffn_w_out):
    W = dict(norm_mix=norm_mix, norm_ffn=norm_ffn, norm_final=norm_final, mix_w_in=mix_w_in, mix_w_out=mix_w_out,
             s5_A_re=s5_A_re, s5_A_im=s5_A_im, s5_log_dt=s5_log_dt, s5_B_re=s5_B_re, s5_B_im=s5_B_im,
             s5_C_re=s5_C_re, s5_C_im=s5_C_im, s5_D=s5_D, s5_glu_w=s5_glu_w, s5_glu_b=s5_glu_b,
             hgrn_gamma=hgrn_gamma, hgrn_norm=hgrn_norm, att_w_qkv=att_w_qkv, att_w_o=att_w_o, ffn_w_in=ffn_w_in,
             ffn_conv_w=ffn_conv_w, ffn_conv_b=ffn_conv_b, ffn_w_out=ffn_w_out)
    M = dict(norm_mix=m_norm_mix, norm_ffn=m_norm_ffn, norm_final=m_norm_final, mix_w_in=m_mix_w_in,
             mix_w_out=m_mix_w_out, s5_A_re=m_s5_A_re, s5_A_im=m_s5_A_im, s5_log_dt=m_s5_log_dt, s5_B_re=m_s5_B_re,
             s5_B_im=m_s5_B_im, s5_C_re=m_s5_C_re, s5_C_im=m_s5_C_im, s5_D=m_s5_D, s5_glu_w=m_s5_glu_w,
             s5_glu_b=m_s5_glu_b, hgrn_gamma=m_hgrn_gamma, hgrn_norm=m_hgrn_norm, att_w_qkv=m_att_w_qkv,
             att_w_o=m_att_w_o, ffn_w_in=m_ffn_w_in, ffn_conv_w=m_ffn_conv_w, ffn_conv_b=m_ffn_conv_b,
             ffn_w_out=m_ffn_w_out)
    V = dict(norm_mix=v_norm_mix, norm_ffn=v_norm_ffn, norm_final=v_norm_final, mix_w_in=v_mix_w_in,
             mix_w_out=v_mix_w_out, s5_A_re=v_s5_A_re, s5_A_im=v_s5_A_im, s5_log_dt=v_s5_log_dt, s5_B_re=v_s5_B_re,
             s5_B_im=v_s5_B_im, s5_C_re=v_s5_C_re, s5_C_im=v_s5_C_im, s5_D=v_s5_D, s5_glu_w=v_s5_glu_w,
             s5_glu_b=v_s5_glu_b, hgrn_gamma=v_hgrn_gamma, hgrn_norm=v_hgrn_norm, att_w_qkv=v_att_w_qkv,
             att_w_o=v_att_w_o, ffn_w_in=v_ffn_w_in, ffn_conv_w=v_ffn_conv_w, ffn_conv_b=v_ffn_conv_b,
             ffn_w_out=v_ffn_w_out)
    return _step(x[0], positions[0], loss_target[0], W, M, V)


TRANSPOSED = ("mix_w_in", "att_w_qkv", "ffn_w_in")
SMALL = ("norm_mix", "norm_ffn", "norm_final", "s5_A_re", "s5_A_im", "s5_log_dt", "s5_B_re", "s5_B_im", "s5_C_re",
         "s5_C_im", "s5_D", "s5_glu_b", "hgrn_gamma", "hgrn_norm", "ffn_conv_b")
ORDER = ("norm_mix", "norm_ffn", "norm_final", "mix_w_in", "mix_w_out", "s5_A_re", "s5_A_im", "s5_log_dt", "s5_B_re",
         "s5_B_im", "s5_C_re", "s5_C_im", "s5_D", "s5_glu_w", "s5_glu_b", "hgrn_gamma", "hgrn_norm", "att_w_qkv",
         "att_w_o", "ffn_w_in", "ffn_conv_w", "ffn_conv_b", "ffn_w_out")
PACK_COLS = 1024


def _step(x, positions, target, W, M, V):
    L, D = x.shape
    me = 4 * lax.axis_index("x") + 2 * lax.axis_index("y") + lax.axis_index("c")
    n_cw = W["ffn_conv_w"].shape[-1]
    T = {n: tuple(jnp.swapaxes(d[n], -1, -2) for d in (W, M, V)) for n in TRANSPOSED}
    first = {
        "mix_w_in": cast_bf16(T["mix_w_in"][0][0], name="mix_w_in_cast"),
        "conv_w": W["ffn_conv_w"].reshape(6, n_cw),
        "s5_glu_w": cast_bf16(W["s5_glu_w"][0], name="s5_glu_w_cast"),
    }
    first_handles, token = copies_start(list(first.values()), "gather2", name="gather_start_first")
    shards = {
        "mix_w_out": cast_bf16(W["mix_w_out"][0], dep=token, name="mix_w_out_cast"),
        "ffn_w_in0": cast_bf16_layer(T["ffn_w_in"][0], 0, name="ffn_w_in0_cast"),
        "ffn_w_out0": cast_bf16_layer(W["ffn_w_out"], 0, name="ffn_w_out0_cast"),
        "att_w_qkv": cast_bf16(T["att_w_qkv"][0][0], name="att_w_qkv_cast"),
        "att_w_o": cast_bf16(W["att_w_o"][0], name="att_w_o_cast"),
        "ffn_w_in1": cast_bf16_layer(T["ffn_w_in"][0], 1, name="ffn_w_in1_cast"),
        "ffn_w_out1": cast_bf16_layer(W["ffn_w_out"], 1, name="ffn_w_out1_cast"),
    }
    gather_handles, token = copies_start(list(shards.values()), "gather2", name="gather_start")
    gather_handle = dict(zip(list(first) + list(shards), first_handles + gather_handles))

    def forward(keys, after, name):
        new, sent = copies_forward([gather_handle[k] for k in keys], after, name=name)
        gather_handle.update(zip(keys, new))
        return sent

    def gathered(key, after, cols):
        _, land = copies_wait(gather_handle[key], "gather2", after, name=key + "_gwait")
        return cols_from_shards(land, name=key + "_asm") if cols else land.reshape(-1, land.shape[-1])

    conv_b = W["ffn_conv_b"].reshape(2, 1, -1)

    s5_params = (W["s5_A_re"][0], W["s5_A_im"][0], W["s5_log_dt"][0], W["s5_B_re"][0], W["s5_B_im"][0],
                 W["s5_C_re"][0], W["s5_C_im"][0])
    (a_re, a_im, wb_re, wb_im, wc_re, wc_im), s5_prep_vjp = jax.vjp(_s5_prepare, *s5_params)
    dvec = W["s5_D"].reshape(1, S5_WIDTH)
    glu_b = W["s5_glu_b"].reshape(1, S5_WIDTH)
    lb, lb_vjp = jax.vjp(_lower_bound, W["hgrn_gamma"])
    hg_norm = W["hgrn_norm"].reshape(1, -1)
    tabs = rope_tables(positions)

    sent = forward(["mix_w_in", "conv_w", "s5_glu_w"], token, "forward_a")
    w_mix_in = gathered("mix_w_in", sent, False)
    hn0, proj = norm_mm(x, W["norm_mix"][0], w_mix_in, name="l0_proj")
    y0, xs_re, xs_im = s5_core_fwd(proj, a_re, a_im, wb_re, wb_im, wc_re, wc_im, name="s5_core")
    w_glu = gathered("s5_glu_w", y0, False)
    cat = s5_out_fwd(y0, proj, dvec, w_glu, glu_b, name="s5_out")
    cat, hg_states = hgrn_fwd(proj, lb, hg_norm, cat, name="hgrn_fwd")
    forward(["mix_w_out"], cat, "forward_b")
    w_mix_out = gathered("mix_w_out", cat, False)
    h1 = mm(cat, w_mix_out, res=x, name="l0_mix_out")
    _, cw_all = copies_wait(gather_handle["conv_w"], "gather2", h1, name="conv_w_gwait")
    conv_w = cw_all.transpose(1, 0, 2).reshape(2, 3, N_DEV * n_cw)
    w_ffn_in, w_ffn_out = [None, None], [None, None]
    h2, ffn0_saved, w_ffn_in[0], w_ffn_out[0] = _ffn_fwd(
        h1, W["norm_ffn"][0],
        lambda a: (forward(["ffn_w_in0"], a, "forward_b2"), gathered("ffn_w_in0", a, False))[1], conv_w[0], conv_b[0],
        lambda a: (forward(["ffn_w_out0"], a, "forward_c"), gathered("ffn_w_out0", a, False))[1], "ffn0")

    forward(["att_w_qkv", "att_w_o"], h2, "forward_d")
    w_qkv = gathered("att_w_qkv", h2, False)
    hn2, qkv_r = norm_mm(h2, W["norm_mix"][1], w_qkv, tabs=tabs, name="l1_qkv")
    att_o, att_l = [], []
    for g, d in enumerate(ATT_DILATIONS):
        o_g, l_g = attn_fwd(qkv_r, g, d, name=f"attn_fwd{g}")
        att_o.append(o_g)
        att_l.append(l_g)
    o_att = merge_fwd(att_o, att_l, name="merge_fwd")
    forward(["ffn_w_in1", "ffn_w_out1"], o_att, "forward_e")
    w_o = gathered("att_w_o", o_att, True)
    h3 = mm(o_att, w_o, res=h2, name="l1_mix_out")
    (loss, dh4, dg_final), ffn1_saved, w_ffn_in[1], w_ffn_out[1] = _ffn_fwd(
        h3, W["norm_ffn"][1], lambda a: gathered("ffn_w_in1", a, False), conv_w[1], conv_b[1],
        lambda a: gathered("ffn_w_out1", a, False), "ffn1", final=(W["norm_final"], target))

    exchanges = {}

    pending = []

    def send_grad(key, g, cols, flush=True):
        if cols:
            parts = shards_from_cols(g, name=key + "_split")
        else:
            parts = g.reshape(N_DEV, g.shape[0] // N_DEV, g.shape[1])
        pending.append((key, parts))
        if not flush:
            return None
        handles, sent = copies_start([p for _, p in pending], "exchange", name=key + "_xstart")
        exchanges.update(zip([k for k, _ in pending], handles))
        pending.clear()
        return sent

    dh3, dg_ffn1, dcw1, dcb1 = _ffn_bwd(h3, W["norm_ffn"][1], w_ffn_in[1], conv_w[1], conv_b[1], w_ffn_out[1],
                                        ffn1_saved, dh4, "ffn1", lambda g: send_grad("ffn_w_in1", g, False),
                                        lambda g: send_grad("ffn_w_out1", g, False, flush=False))
    sent = send_grad("att_w_o", mm(o_att, dh3, ta=True, name="l1_dwo"), True, flush=False)
    d_oatt = mm(dh3, w_o, tb=True, dep=sent, name="l1_dmix")
    mb = merge_bwd(att_o, att_l, d_oatt, name="merge_bwd")
    d_slabs = [attn_bwd(qkv_r, g, att_l[g], mb[g], mb[3 + g], d, name=f"attn_bwd{g}")
               for g, d in enumerate(ATT_DILATIONS)]
    d_qkv = rope_bwd([s[0] for s in d_slabs] + [s[1] for s in d_slabs] + [s[2] for s in d_slabs], tabs,
                     name="rope_bwd")
    sent = send_grad("att_w_qkv", mm(d_qkv, hn2, ta=True, out_dtype=BF16, name="l1_dwqkv"), False)
    dh2, dg_mix1 = mm_drms(d_qkv, w_qkv, h2, W["norm_mix"][1], dh3, dep=sent, name="l1_dhn")

    dh1, dg_ffn0, dcw0, dcb0 = _ffn_bwd(h1, W["norm_ffn"][0], w_ffn_in[0], conv_w[0], conv_b[0], w_ffn_out[0],
                                        ffn0_saved, dh2, "ffn0", lambda g: send_grad("ffn_w_in0", g, False),
                                        lambda g: send_grad("ffn_w_out0", g, False, flush=False))
    sent = send_grad("mix_w_out", mm(cat, dh1, ta=True, out_dtype=BF16, name="l0_dwout"), False)
    dcat = mm(dh1, w_mix_out, tb=True, dep=sent, name="l0_dcat")
    dy, du_d, z_bf, dzg, dglu_b, dD = s5_out_bwd(y0, proj, dvec, w_glu, glu_b, dcat, name="s5_dout")
    sent_glu = send_grad("s5_glu_w", mm(z_bf, dzg, ta=True, out_dtype=BF16, name="s5_dglu"), False, flush=False)
    du, dwb_re, dwb_im, dwc_re, dwc_im, da_re, da_im = s5_core_bwd(
        dy, du_d, proj, xs_re, xs_im, a_re, a_im, wb_re, wb_im, wc_re, wc_im, name="s5_dcore")
    s5_small = s5_prep_vjp((da_re, da_im, dwb_re, dwb_im, dwc_re, dwc_im))
    d_proj, dlb, dhg_norm = hgrn_bwd(proj, lb, hg_norm, hg_states, dcat, du, name="hgrn_bwd")
    sent = send_grad("mix_w_in", mm(d_proj, hn0, ta=True, out_dtype=BF16, dep=sent_glu, name="l0_dwin"), False)
    grad_x, dg_mix0 = mm_drms(d_proj, w_mix_in, x, W["norm_mix"][0], dh1, dep=sent, name="l0_dhn")
    (d_gamma,) = lb_vjp(dlb)
    out = {}

    dA_re, dA_im, dlog_dt, dB_re, dB_im, dC_re, dC_im = s5_small
    small_g = dict(norm_mix=jnp.concatenate([dg_mix0, dg_mix1], axis=0), norm_ffn=jnp.concatenate([dg_ffn0, dg_ffn1], axis=0),
                   norm_final=dg_final, s5_A_re=dA_re, s5_A_im=dA_im, s5_log_dt=dlog_dt, s5_B_re=dB_re, s5_B_im=dB_im,
                   s5_C_re=dC_re, s5_C_im=dC_im, s5_D=dD, s5_glu_b=dglu_b, hgrn_gamma=d_gamma, hgrn_norm=dhg_norm,
                   ffn_conv_b=jnp.concatenate([dcb0, dcb1], axis=0))
    conv_w_g = jnp.stack([dcw0, dcw1], axis=0)
    sizes = [math.prod(W[n].shape) for n in SMALL]
    n_conv = conv_w_g.size
    total = sum(sizes) + n_conv + 1
    rows = -(-total // PACK_COLS)
    rows = -(-rows // 8) * 8
    pad = rows * PACK_COLS - total

    def pack(vals, conv_part, last):
        flat = [v.reshape(-1).astype(F32) for v in vals] + [conv_part.reshape(-1), last.reshape(-1),
                                                            jnp.zeros((pad,), F32)]
        return jnp.concatenate(flat).reshape(rows, PACK_COLS)

    def conv_full(shard):
        col_owner = lax.broadcasted_iota(jnp.int32, (2, 3, N_DEV * n_cw), 2) // n_cw
        return jnp.where(col_owner == me, jnp.tile(shard, (1, 1, N_DEV)), 0.0)

    zero1 = jnp.zeros((1,), F32)
    g_pack = pack([small_g[n] for n in SMALL], conv_w_g, loss)
    w_pack = pack([W[n] for n in SMALL], conv_full(W["ffn_conv_w"]), zero1)
    m_pack = pack([M[n] for n in SMALL], conv_full(M["ffn_conv_w"]), zero1)
    v_pack = pack([V[n] for n in SMALL], conv_full(V["ffn_conv_w"]), zero1 + 1.0)
    (small_handle,), small_sent = copies_start([g_pack], "gather", name="small_xstart")

    def finish(name, n_layers):
        w3, m3, v3 = T[name] if name in TRANSPOSED else (W[name], M[name], V[name])
        res = None
        for layer in reversed(range(n_layers)):
            key = name if n_layers == 1 else f"{name}{layer}"
            own, recv = copies_wait(exchanges[key], "exchange", small_sent, name=key + "_xwait")
            _, R, Cn = recv.shape
            res = reduce_adamw(recv, own, True, me, w3.reshape(n_layers * R, Cn), m3.reshape(n_layers * R, Cn),
                               v3.reshape(n_layers * R, Cn), layer=layer, n_layers=n_layers, into=res,
                               name=key + "_adamw")
        res = [r.reshape(w3.shape) for r in res]
        return tuple(jnp.swapaxes(r, -1, -2) for r in res) if name in TRANSPOSED else tuple(res)

    for name in ("ffn_w_out", "ffn_w_in"):
        out[name] = finish(name, 2)
    for name in ("att_w_o", "att_w_qkv", "mix_w_out", "s5_glu_w", "mix_w_in"):
        out[name] = finish(name, 1)

    small_own, small_recv = copies_wait(small_handle, "gather", out["s5_glu_w"][0], name="small_xwait")
    res = reduce_adamw(small_recv, small_own, False, me, w_pack, m_pack, v_pack, name="small_adamw")
    flat = [r.reshape(-1) for r in res]
    off = 0
    for n, sz in zip(SMALL, sizes):
        out[n] = tuple(f[off:off + sz].reshape(W[n].shape) for f in flat)
        off += sz
    conv_res = [f[off:off + n_conv].reshape(2, 3, N_DEV * n_cw) for f in flat]
    out["ffn_conv_w"] = tuple(lax.dynamic_slice(c, (0, 0, me * n_cw), (2, 3, n_cw)) for c in conv_res)
    off += n_conv
    loss_total = flat[0][off]

    result = [loss_total, grad_x[None]]
    for k in range(4):
        result += [out[n][k] for n in ORDER]
    return tuple(result)
```

```python
import math

import jax
import jax.numpy as jnp
from jax import lax
from jax.experimental import pallas as pl
from jax.experimental.pallas import tpu as pltpu

F32 = jnp.float32
BF16 = jnp.bfloat16
MESH_ID = pl.DeviceIdType.MESH
N_DEV = 8
VMEM_LIMIT_BYTES = 56 * 1024 * 1024

NORM_EPS = 1e-6
S5_WIDTH, S5_GROUP, S5_GROUPS, S5_STATE = 512, 16, 32, 64
HG_HEADS, HG_DIM, HG_CHUNK = 4, 128, 64
HG_STEP_CHUNKS = 4
ATT_E, ATT_HPG, ATT_BLOCK = 64, 8, 128
ATT_DILATIONS = (1, 4, 16)
ROT_DIM, ROPE_THETA = 16, 500000.0
D_FF = 2816
ADAM_LR, ADAM_B1, ADAM_B2, ADAM_EPS, ADAM_WD, ADAM_STEP = 0.001, 0.9, 0.999, 1e-08, 0.01, 10
NEG_BIG = -1e30


def _params(**kw):
    return pltpu.CompilerParams(vmem_limit_bytes=VMEM_LIMIT_BYTES, **kw)


def _pick(n, cands):
    for c in cands:
        if n % c == 0:
            return c
    return n


def _dot(a, b):
    return jnp.dot(a.astype(BF16), b.astype(BF16), preferred_element_type=F32)


def _dot_nt(a, b):
    return lax.dot_general(a.astype(BF16), b.astype(BF16), (((1,), (1,)), ((), ())), preferred_element_type=F32)


def _dot_tn(a, b):
    return lax.dot_general(a.astype(BF16), b.astype(BF16), (((0,), (0,)), ((), ())), preferred_element_type=F32)


def _split2(x):
    hi = x.astype(BF16)
    return hi, (x - hi.astype(F32)).astype(BF16)


def _dot_x3(a, b, contract=((1,), (0,))):
    dn = (contract, ((), ()))
    a1, a2 = _split2(a)
    b1, b2 = _split2(b)
    return (lax.dot_general(a1, b1, dn, preferred_element_type=F32) + lax.dot_general(a1, b2, dn, preferred_element_type=F32)
            + lax.dot_general(a2, b1, dn, preferred_element_type=F32))


def _sigmoid(x):
    return 1.0 / (1.0 + jnp.exp(-x))


V7X_HBM_BYTES_PER_S = 3.2e12
V7X_MXU_FLOPS_PER_S = 0.7e15
GRID_STEP_S = 0.35e-6
MM_VMEM_BUDGET = 40 * 1024 * 1024


def _divisors(n, cands):
    return [c for c in cands if c <= n and n % c == 0] or [n]


def _mm_tiles(m, n, k, sa, sb, so, sr):
    best = None
    for tm in _divisors(m, (2816, 2048, 1408, 1024, 512, 256, 128)):
        for tn in _divisors(n, (2816, 2048, 1408, 1024, 512, 256, 128)):
            for tk in _divisors(k, (k, 2816, 2560, 2304, 2048, 1536, 1408, 1280, 1024, 512, 256, 128)):
                nk = k // tk
                vmem = 2 * (tm * tk * sa + tk * tn * sb + tm * tn * (so + sr)) + (tm * tn * 4 if nk > 1 else 0)
                vmem += tm * tk * 2 * (sa > 2) + tk * tn * 2 * (sb > 2) + tm * tn * 4
                if vmem > MM_VMEM_BUDGET:
                    continue
                ni, nj = m // tm, n // tn
                for i_outer in (True, False):
                    if i_outer:
                        a_reads = 1 if nk == 1 else nj
                        b_reads = 1 if (nk == 1 and nj == 1) else ni
                    else:
                        b_reads = 1 if nk == 1 else ni
                        a_reads = 1 if (nk == 1 and ni == 1) else nj
                    traffic = a_reads * m * k * sa + b_reads * k * n * sb + m * n * (so + sr)
                    t = max(traffic / V7X_HBM_BYTES_PER_S, 2.0 * m * n * k / V7X_MXU_FLOPS_PER_S)
                    t += ni * nj * nk * GRID_STEP_S
                    t += (tm * tk * sa + tk * tn * sb + tm * tn * so) / V7X_HBM_BYTES_PER_S
                    if best is None or t < best[0]:
                        best = (t, tm, tn, tk, i_outer)
    assert best is not None, (m, n, k)
    return best[1:]


def mm(a, b, *, ta=False, tb=False, res=None, out_dtype=F32, dep=None, out_rows=None, out_off=0, into=None, name):
    m, k = (a.shape[1], a.shape[0]) if ta else a.shape
    n = b.shape[0] if tb else b.shape[1]
    assert (b.shape[1] if tb else b.shape[0]) == k
    has_res = res is not None
    tm, tn, tk, i_outer = _mm_tiles(m, n, k, a.dtype.itemsize, b.dtype.itemsize, jnp.dtype(out_dtype).itemsize,
                                    res.dtype.itemsize if has_res else 0)
    nk = k // tk
    deps = [] if dep is None else [dep]
    dn = (((0 if ta else 1,), (1 if tb else 0,)), ((), ()))

    def body_single(*refs):
        a_ref, b_ref = refs[:2]
        o_ref = refs[-1]
        out = lax.dot_general(a_ref[...].astype(BF16), b_ref[...].astype(BF16), dn, preferred_element_type=F32)
        if has_res:
            out = out + refs[2][...].astype(F32)
        o_ref[...] = out.astype(o_ref.dtype)

    def body(*refs):
        a_ref, b_ref = refs[:2]
        r_ref = refs[2] if has_res else None
        o_ref, acc_ref = refs[-2:]
        kk = pl.program_id(2)
        part = lax.dot_general(a_ref[...].astype(BF16), b_ref[...].astype(BF16), dn, preferred_element_type=F32)

        @pl.when(kk == 0)
        def _():
            acc_ref[...] = part

        @pl.when(kk > 0)
        def _():
            acc_ref[...] += part

        @pl.when(kk == nk - 1)
        def _():
            out = acc_ref[...]
            if has_res:
                out = out + r_ref[...].astype(F32)
            o_ref[...] = out.astype(o_ref.dtype)

    def ij(f):
        return (lambda g0, g1, q: f(g0, g1, q)) if i_outer else (lambda g0, g1, q: f(g1, g0, q))

    a_spec = pl.BlockSpec((tk, tm), ij(lambda i, j, q: (q, i))) if ta else pl.BlockSpec((tm, tk), ij(lambda i, j, q: (i, q)))
    b_spec = pl.BlockSpec((tn, tk), ij(lambda i, j, q: (j, q))) if tb else pl.BlockSpec((tk, tn), ij(lambda i, j, q: (q, j)))
    assert out_off % tm == 0
    off = out_off // tm
    r_spec = pl.BlockSpec((tm, tn), ij(lambda i, j, q: (i, j)))
    o_spec = pl.BlockSpec((tm, tn), ij(lambda i, j, q: (i + off, j)))
    rest = [] if into is None else [into]
    in_specs = ([a_spec, b_spec] + ([r_spec] if has_res else []) + [pl.BlockSpec((8, 128), lambda g0, g1, q: (0, 0))] * len(deps)
                + [pl.BlockSpec(memory_space=pl.ANY)] * len(rest))
    args = (a, b) + ((res,) if has_res else ()) + tuple(deps) + tuple(rest)
    grid = (m // tm, n // tn, nk) if i_outer else (n // tn, m // tm, nk)
    return pl.pallas_call(
        body_single if nk == 1 else body, grid=grid, in_specs=in_specs, out_specs=o_spec,
        out_shape=jax.ShapeDtypeStruct((out_rows or m, n), out_dtype),
        input_output_aliases={len(args) - 1: 0} if rest else {},
        scratch_shapes=[] if nk == 1 else [pltpu.VMEM((tm, tn), F32)],
        compiler_params=_params(dimension_semantics=("parallel", "parallel", "arbitrary")), name=name,
    )(*args)


def mm_drms(dy_in, w, x, g, dres, *, dep=None, name):
    halves = dy_in if isinstance(dy_in, (tuple, list)) else (dy_in,)
    m, kh = halves[0].shape
    k = kh * len(halves)
    D = w.shape[1]
    tm = _pick(m, (512, 256, 128))
    tk = max(_divisors(kh, (1536, 1408, 1280, 1024, 512, 256, 128)))
    nk, nh = k // tk, kh // tk
    deps = [] if dep is None else [dep]

    def body(*refs):
        a_refs, (b_ref, x_ref, g_ref, dres_ref) = refs[:len(halves)], refs[len(halves):len(halves) + 4]
        dx_ref, dg_ref, acc_ref = refs[-3:]
        i, q = pl.program_id(0), pl.program_id(1)
        a = a_refs[0][...] if len(halves) == 1 else jnp.where(q < nh, a_refs[0][...], a_refs[1][...])
        part = jnp.dot(a, b_ref[...], preferred_element_type=F32)

        @pl.when(q == 0)
        def _():
            acc_ref[...] = part

        @pl.when(q > 0)
        def _():
            acc_ref[...] += part

        @pl.when((i == 0) & (q == 0))
        def _():
            dg_ref[...] = jnp.zeros_like(dg_ref)

        @pl.when(q == nk - 1)
        def _():
            dyv = acc_ref[...]
            xv = x_ref[...]
            r = lax.rsqrt(jnp.mean(xv * xv, axis=-1, keepdims=True) + NORM_EPS)
            xh = xv * r
            dg_ref[...] += jnp.sum(dyv * xh, axis=0, keepdims=True)
            dxh = dyv * g_ref[...]
            dx_ref[...] = dres_ref[...] + r * (dxh - xh * jnp.mean(dxh * xh, axis=-1, keepdims=True))

    row = pl.BlockSpec((tm, D), lambda i, q: (i, 0))
    vec = pl.BlockSpec((1, D), lambda i, q: (0, 0))
    a_specs = [pl.BlockSpec((tm, tk), lambda i, q, h=h: (i, jnp.clip(q - h * nh, 0, nh - 1))) for h in range(len(halves))]
    in_specs = a_specs + [pl.BlockSpec((tk, D), lambda i, q: (q, 0)), row, vec, row]
    in_specs += [pl.BlockSpec((8, 128), lambda i, q: (0, 0))] * len(deps)
    return pl.pallas_call(
        body, grid=(m // tm, nk), in_specs=in_specs, out_specs=[row, vec],
        out_shape=[jax.ShapeDtypeStruct((m, D), F32), jax.ShapeDtypeStruct((1, D), F32)],
        scratch_shapes=[pltpu.VMEM((tm, D), F32)],
        compiler_params=_params(dimension_semantics=("arbitrary", "arbitrary")), name=name,
    )(*halves, w, x, g.reshape(1, D), dres, *deps)


def mm_final_loss(act, w, h_res, g, target, *, name):
    L, K = act.shape
    D = w.shape[1]
    tm = _pick(L, (512, 256, 128))
    tk = max(_divisors(K, (1536, 1408, 1280, 1024, 512, 256, 128)))
    nk = K // tk

    def body(a_ref, b_ref, r_ref, g_ref, t_ref, loss_ref, dx_ref, dg_ref, acc_ref):
        i, q = pl.program_id(0), pl.program_id(1)
        part = jnp.dot(a_ref[...], b_ref[...], preferred_element_type=F32)

        @pl.when(q == 0)
        def _():
            acc_ref[...] = part

        @pl.when(q > 0)
        def _():
            acc_ref[...] += part

        @pl.when((i == 0) & (q == 0))
        def _():
            dg_ref[...] = jnp.zeros_like(dg_ref)
            loss_ref[...] = jnp.zeros_like(loss_ref)

        @pl.when(q == nk - 1)
        def _():
            xv = acc_ref[...] + r_ref[...]
            gv = g_ref[...]
            r = lax.rsqrt(jnp.mean(xv * xv, axis=-1, keepdims=True) + NORM_EPS)
            xh = xv * r
            err = xh * gv - t_ref[...]
            loss_ref[...] += 0.5 * jnp.sum(jnp.mean(err * err, axis=-1, keepdims=True), axis=0, keepdims=True)
            dyv = err * (1.0 / D)
            dg_ref[...] += jnp.sum(dyv * xh, axis=0, keepdims=True)
            dxh = dyv * gv
            dx_ref[...] = r * (dxh - xh * jnp.mean(dxh * xh, axis=-1, keepdims=True))

    row = pl.BlockSpec((tm, D), lambda i, q: (i, 0))
    vec = pl.BlockSpec((1, D), lambda i, q: (0, 0))
    one = pl.BlockSpec((1, 1), lambda i, q: (0, 0))
    return pl.pallas_call(
        body, grid=(L // tm, nk),
        in_specs=[pl.BlockSpec((tm, tk), lambda i, q: (i, q)), pl.BlockSpec((tk, D), lambda i, q: (q, 0)), row, vec, row],
        out_specs=[one, row, vec],
        out_shape=[jax.ShapeDtypeStruct((1, 1), F32), jax.ShapeDtypeStruct((L, D), F32), jax.ShapeDtypeStruct((1, D), F32)],
        scratch_shapes=[pltpu.VMEM((tm, D), F32)],
        compiler_params=_params(dimension_semantics=("arbitrary", "arbitrary")), name=name,
    )(act, w, h_res, g.reshape(1, D), target)


def _cmul(ar, ai, br, bi):
    return ar * br - ai * bi, ar * bi + ai * br


def _powers(ar, ai):
    rows = [(ar, ai)]
    for _ in range(7):
        rows.append(_cmul(rows[-1][0], rows[-1][1], ar, ai))
    table = (jnp.concatenate([r[0] for r in rows], axis=0), jnp.concatenate([r[1] for r in rows], axis=0))
    return (rows[0], rows[1], rows[3]), table


def _block_scan(br, bi, steps, shift):
    yr, yi = br, bi
    for s, (pr, pi) in zip((1, 2, 4), steps):
        sr, si = shift(yr, s), shift(yi, s)
        yr, yi = yr + pr * sr - pi * si, yi + pr * si + pi * sr
    return yr, yi


def s5_core_fwd(proj, a_re, a_im, wb_re, wb_im, wc_re, wc_im, *, name):
    L = proj.shape[0]
    parts, cu, W = wb_re.shape

    def body(u_ref, ar_ref, ai_ref, wbr_ref, wbi_ref, wcr_ref, wci_ref, y_ref, xr_ref, xi_ref, br_ref, bi_ref):
        u = u_ref[...]
        br_ref[...] = _dot(u, wbr_ref[...])
        bi_ref[...] = _dot(u, wbi_ref[...])
        steps, (tr, ti) = _powers(ar_ref[...], ai_ref[...])
        row = lax.broadcasted_iota(jnp.int32, (8, W), 0)

        def shift(y, s):
            return jnp.where(row >= s, pltpu.roll(y, s, 0), 0.0)

        def step(t8, carry):
            cr, ci = carry
            base = pl.multiple_of(t8 * 8, 8)
            yr, yi = _block_scan(br_ref[pl.ds(base, 8), :], bi_ref[pl.ds(base, 8), :], steps, shift)
            xr = yr + tr * cr - ti * ci
            xi = yi + tr * ci + ti * cr
            xr_ref[pl.ds(base, 8), :] = xr
            xi_ref[pl.ds(base, 8), :] = xi
            return jnp.broadcast_to(xr[7:8, :], (8, W)), jnp.broadcast_to(xi[7:8, :], (8, W))

        zero = jnp.zeros((8, W), F32)
        lax.fori_loop(0, L // 8, step, (zero, zero), unroll=2)
        y_ref[...] = _dot(xr_ref[...], wcr_ref[...]) + _dot(xi_ref[...], wci_ref[...])

    ucol = pl.BlockSpec((L, cu), lambda t: (0, t))
    vec = pl.BlockSpec((1, W), lambda t: (0, t))
    col = pl.BlockSpec((L, W), lambda t: (0, t))
    wb = pl.BlockSpec((None, cu, W), lambda t: (t, 0, 0))
    wc = pl.BlockSpec((None, W, cu), lambda t: (t, 0, 0))
    return pl.pallas_call(body, grid=(parts,), in_specs=[ucol, vec, vec, wb, wb, wc, wc], out_specs=[ucol, col, col],
                          out_shape=[jax.ShapeDtypeStruct((L, parts * cu), F32)]
                          + [jax.ShapeDtypeStruct((L, parts * W), F32)] * 2,
                          scratch_shapes=[pltpu.VMEM((L, W), F32)] * 2,
                          compiler_params=_params(dimension_semantics=("parallel",)), name=name)(
        proj, a_re, a_im, wb_re, wb_im, wc_re, wc_im)


def s5_core_bwd(dy, du_d, proj, xs_re, xs_im, a_re, a_im, wb_re, wb_im, wc_re, wc_im, *, name):
    L = proj.shape[0]
    parts, cu, W = wb_re.shape

    def body(dy_ref, dud_ref, u_ref, xr_ref, xi_ref, ar_ref, ai_ref, wbr_ref, wbi_ref, wcr_ref, wci_ref,
             du_ref, dwbr_ref, dwbi_ref, dwcr_ref, dwci_ref, dar_ref, dai_ref, lr_ref, li_ref):
        dy = dy_ref[...]
        lr_ref[...] = _dot_nt(dy, wcr_ref[...])
        li_ref[...] = _dot_nt(dy, wci_ref[...])
        dwcr_ref[...] = _dot_tn(xr_ref[...], dy)
        dwci_ref[...] = _dot_tn(xi_ref[...], dy)
        ar, ai = ar_ref[...], -ai_ref[...]
        steps, (tr, ti) = _powers(ar, ai)
        tr = jnp.concatenate([tr[j:j + 1, :] for j in range(7, -1, -1)], axis=0)
        ti = jnp.concatenate([ti[j:j + 1, :] for j in range(7, -1, -1)], axis=0)
        row8 = lax.broadcasted_iota(jnp.int32, (8, W), 0)
        nblk = L // 8

        def shift(y, s):
            return jnp.where(row8 < 8 - s, pltpu.roll(y, 8 - s, 0), 0.0)

        def step(s, carry):
            cr, ci = carry
            base = pl.multiple_of((nblk - 1 - s) * 8, 8)
            yr, yi = _block_scan(lr_ref[pl.ds(base, 8), :], li_ref[pl.ds(base, 8), :], steps, shift)
            lr = yr + tr * cr - ti * ci
            li = yi + tr * ci + ti * cr
            lr_ref[pl.ds(base, 8), :] = lr
            li_ref[pl.ds(base, 8), :] = li
            return jnp.broadcast_to(lr[0:1, :], (8, W)), jnp.broadcast_to(li[0:1, :], (8, W))

        zero = jnp.zeros((8, W), F32)
        lax.fori_loop(0, nblk, step, (zero, zero), unroll=2)
        row = lax.broadcasted_iota(jnp.int32, (L, W), 0)
        xpr = jnp.where(row >= 1, pltpu.roll(xr_ref[...], 1, 0), 0.0)
        xpi = jnp.where(row >= 1, pltpu.roll(xi_ref[...], 1, 0), 0.0)
        lr, li = lr_ref[...], li_ref[...]
        dar_ref[...] = jnp.sum(lr * xpr + li * xpi, axis=0, keepdims=True)
        dai_ref[...] = jnp.sum(li * xpr - lr * xpi, axis=0, keepdims=True)
        u = u_ref[...]
        dwbr_ref[...] = _dot_tn(u, lr)
        dwbi_ref[...] = _dot_tn(u, li)
        du_ref[...] = (dud_ref[...] + _dot_nt(lr, wbr_ref[...]) + _dot_nt(li, wbi_ref[...])).astype(du_ref.dtype)

    ucol = pl.BlockSpec((L, cu), lambda t: (0, t))
    vec = pl.BlockSpec((1, W), lambda t: (0, t))
    col = pl.BlockSpec((L, W), lambda t: (0, t))
    wb = pl.BlockSpec((None, cu, W), lambda t: (t, 0, 0))
    wc = pl.BlockSpec((None, W, cu), lambda t: (t, 0, 0))
    return pl.pallas_call(
        body, grid=(parts,), in_specs=[ucol, ucol, ucol, col, col, vec, vec, wb, wb, wc, wc],
        out_specs=[ucol, wb, wb, wc, wc, vec, vec],
        out_shape=[jax.ShapeDtypeStruct((L, parts * cu), BF16)] + [jax.ShapeDtypeStruct((parts, cu, W), F32)] * 2
        + [jax.ShapeDtypeStruct((parts, W, cu), F32)] * 2 + [jax.ShapeDtypeStruct((1, parts * W), F32)] * 2,
        scratch_shapes=[pltpu.VMEM((L, W), F32)] * 2,
        compiler_params=_params(dimension_semantics=("parallel",)), name=name,
    )(dy, du_d, proj, xs_re, xs_im, a_re, a_im, wb_re, wb_im, wc_re, wc_im)


def _gelu(y):
    c = math.sqrt(2.0 / math.pi)
    t = jnp.tanh(c * (y + 0.044715 * y * y * y))
    return 0.5 * y * (1.0 + t), t


def s5_out_fwd(y0, proj, dvec, glu_w, glu_b, *, name):
    L, C = y0.shape
    tr = _pick(L, (256, 128))

    def body(y_ref, u_ref, d_ref, w_ref, b_ref, o_ref):
        z, _ = _gelu(y_ref[...] + d_ref[...] * u_ref[...])
        zg = _dot(z, w_ref[...]) + b_ref[...]
        o_ref[...] = (z * _sigmoid(zg)).astype(o_ref.dtype)

    row = pl.BlockSpec((tr, C), lambda i: (i, 0))
    vec = pl.BlockSpec((1, C), lambda i: (0, 0))
    wsp = pl.BlockSpec((C, C), lambda i: (0, 0))
    return pl.pallas_call(body, grid=(L // tr,), in_specs=[row, row, vec, wsp, vec], out_specs=row,
                          out_shape=jax.ShapeDtypeStruct((L, 2 * C), BF16), name=name)(
        y0, proj, dvec, glu_w, glu_b)


def s5_out_bwd(y0, proj, dvec, glu_w, glu_b, dcat, *, name):
    L, C = y0.shape
    tr = _pick(L, (256, 128))

    def body(y_ref, u_ref, d_ref, w_ref, b_ref, do_ref, dy_ref, dud_ref, z_ref, dzg_ref, db_ref, dd_ref):
        u = u_ref[...]
        y = y_ref[...] + d_ref[...] * u
        z, t = _gelu(y)
        zg = _dot(z, w_ref[...]) + b_ref[...]
        s = _sigmoid(zg)
        do = do_ref[...]
        dzg = do * z * s * (1.0 - s)
        dz = do * s + _dot_nt(dzg, w_ref[...])
        c = math.sqrt(2.0 / math.pi)
        dgelu = 0.5 * (1.0 + t) + 0.5 * y * (1.0 - t * t) * c * (1.0 + 3.0 * 0.044715 * y * y)
        dy = dz * dgelu

        @pl.when(pl.program_id(0) == 0)
        def _():
            db_ref[...] = jnp.zeros_like(db_ref)
            dd_ref[...] = jnp.zeros_like(dd_ref)

        db_ref[...] += jnp.sum(dzg, axis=0, keepdims=True)
        dd_ref[...] += jnp.sum(dy * u, axis=0, keepdims=True)
        dy_ref[...] = dy
        dud_ref[...] = dy * d_ref[...]
        z_ref[...] = z.astype(BF16)
        dzg_ref[...] = dzg.astype(BF16)

    row = pl.BlockSpec((tr, C), lambda i: (i, 0))
    vec = pl.BlockSpec((1, C), lambda i: (0, 0))
    wsp = pl.BlockSpec((C, C), lambda i: (0, 0))
    return pl.pallas_call(body, grid=(L // tr,), in_specs=[row, row, vec, wsp, vec, row],
                          out_specs=[row, row, row, row, vec, vec],
                          out_shape=[jax.ShapeDtypeStruct((L, C), F32), jax.ShapeDtypeStruct((L, C), F32),
                                     jax.ShapeDtypeStruct((L, C), BF16), jax.ShapeDtypeStruct((L, C), BF16),
                                     jax.ShapeDtypeStruct((1, C), F32), jax.ShapeDtypeStruct((1, C), F32)],
                          compiler_params=_params(dimension_semantics=("arbitrary",)), name=name)(
        y0, proj, dvec, glu_w, glu_b, dcat)


def _dot_tri(tri, x, tri_left=True):
    t = tri.astype(BF16)
    x1 = x.astype(BF16)
    r1 = x - x1.astype(F32)
    x2 = r1.astype(BF16)
    x3 = (r1 - x2.astype(F32)).astype(BF16)
    dot = (lambda p: jnp.dot(t, p, preferred_element_type=F32)) if tri_left else (
        lambda p: jnp.dot(p, t, preferred_element_type=F32))
    return dot(x1) + dot(x2) + dot(x3)


def _hg_gates(xq, xf, lb, tri):
    C = xq.shape[0]
    sq = _sigmoid(xq)
    q = xq * sq
    sg = _sigmoid(xf)
    f = lb + (1.0 - lb) * sg
    kk = 1.0 - f
    b = _dot_tri(tri, jnp.log(f))
    bm = b[C // 2 - 1:C // 2, :]
    bl = b[C - 1:C, :]
    eb = jnp.exp(b)
    eqm, ekm, ekl = jnp.exp(b - bm), jnp.exp(bm - b), jnp.exp(bl - b)
    return dict(sq=sq, q=q, sg=sg, f=f, kk=kk, eb=eb, ebl=jnp.exp(bl), eqm=eqm, ekm=ekm, ekl=ekl,
                qb=q * eb, qt=q * eqm, kt=kk * ekm, kh=kk * ekl)


def _tri(C, lower):
    r = lax.broadcasted_iota(jnp.int32, (C, C), 0)
    c = lax.broadcasted_iota(jnp.int32, (C, C), 1)
    return (r >= c) if lower else (c >= r)


def hgrn_fwd(proj, lb, norm_g, cat, *, name):
    L = proj.shape[0]
    C, H, K = HG_CHUNK, HG_HEADS, HG_DIM
    HK = H * K
    nc = L // C

    def body(q_ref, f_ref, i_ref, g_ref, lb_ref, ng_ref, cat_ref, o_ref, sall_ref, st_ref):
        @pl.when(pl.program_id(0) == 0)
        def _():
            st_ref[...] = jnp.zeros_like(st_ref)

        mask = _tri(C, True)
        sts = [st_ref[h] for h in range(H)]
        for s in range(S):
            rs = slice(s * C, (s + 1) * C)
            gt = _hg_gates(q_ref[rs, :], f_ref[rs, :], lb_ref[...], mask.astype(F32))
            v_all = i_ref[rs, :]
            outs = []
            for h in range(H):
                sl = slice(h * K, (h + 1) * K)
                v, st = v_all[:, sl], sts[h]
                sall_ref[s, h] = st
                att = jnp.where(mask, _dot_nt(gt["qt"][:, sl], gt["kt"][:, sl]), 0.0)
                o = _dot(att, v) + _dot_nt(gt["qb"][:, sl], st)
                sts[h] = st * gt["ebl"][:, sl] + _dot_tn(v, gt["kh"][:, sl])
                outs.append(o * lax.rsqrt(jnp.mean(o * o, axis=-1, keepdims=True) + NORM_EPS))
            xg = g_ref[rs, :]
            o_ref[rs, :] = (jnp.concatenate(outs, axis=1) * ng_ref[...] * (xg * _sigmoid(xg))).astype(o_ref.dtype)
        for h in range(H):
            st_ref[h] = sts[h]

    S = HG_STEP_CHUNKS

    def blk(cb):
        return pl.BlockSpec((S * C, HK), lambda i: (i, cb))

    vec = pl.BlockSpec((1, HK), lambda i: (0, 0))
    return pl.pallas_call(
        body, grid=(nc // S,), in_specs=[blk(1), blk(2), blk(3), blk(4), vec, vec, pl.BlockSpec(memory_space=pl.ANY)],
        out_specs=[pl.BlockSpec((S * C, HK), lambda i: (i, 1)), pl.BlockSpec((S, H, K, K), lambda i: (i, 0, 0, 0))],
        out_shape=[jax.ShapeDtypeStruct((L, 2 * HK), BF16), jax.ShapeDtypeStruct((nc, H, K, K), F32)],
        input_output_aliases={6: 0},
        scratch_shapes=[pltpu.VMEM((H, K, K), F32)],
        compiler_params=_params(dimension_semantics=("arbitrary",)), name=name,
    )(proj, proj, proj, proj, lb, norm_g, cat)


def hgrn_bwd(proj, lb, norm_g, sall, dcat, du, *, name):
    L = proj.shape[0]
    C, H, K = HG_CHUNK, HG_HEADS, HG_DIM
    HK = H * K
    nc = L // C

    def body(q_ref, f_ref, i_ref, g_ref, lb_ref, ng_ref, sall_ref, do_ref, du_ref, dx_ref, dlb_ref, dng_ref, dst_ref):
        @pl.when(pl.program_id(0) == 0)
        def _():
            dst_ref[...] = jnp.zeros_like(dst_ref)
            dlb_ref[...] = jnp.zeros_like(dlb_ref)
            dng_ref[...] = jnp.zeros_like(dng_ref)

        mask = _tri(C, True)
        lb_all, ng = lb_ref[...], ng_ref[...]
        dx_ref[:, 0:HK] = du_ref[...]
        dsts = [dst_ref[h] for h in range(H)]
        for s in reversed(range(S)):
            rs = slice(s * C, (s + 1) * C)
            dsts = chunk_bwd(rs, s, dsts, mask, lb_all, ng, q_ref, f_ref, i_ref, g_ref, sall_ref, do_ref,
                             dx_ref, dlb_ref, dng_ref)
        for h in range(H):
            dst_ref[h] = dsts[h]

    def chunk_bwd(rs, s, dsts, mask, lb_all, ng, q_ref, f_ref, i_ref, g_ref, sall_ref, do_ref, dx_ref, dlb_ref, dng_ref):
        xq, xg, v_all = q_ref[rs, :], g_ref[rs, :], i_ref[rs, :]
        gt = _hg_gates(xq, f_ref[rs, :], lb_all, mask.astype(F32))
        sgg = _sigmoid(xg)
        d_ob = do_ref[rs, :]
        d_on = d_ob * (xg * sgg)
        doh = d_on * ng
        ohs, d_qts, d_qbs, d_kts, d_khs, dvs, d_bls, new_dsts = [], [], [], [], [], [], [], []
        for h in range(H):
            sl = slice(h * K, (h + 1) * K)
            v, st, dst = v_all[:, sl], sall_ref[s, h], dsts[h]
            qt, kt, kh, qb = gt["qt"][:, sl], gt["kt"][:, sl], gt["kh"][:, sl], gt["qb"][:, sl]
            att = jnp.where(mask, _dot_nt(qt, kt), 0.0)
            o = _dot(att, v) + _dot_nt(qb, st)
            r = lax.rsqrt(jnp.mean(o * o, axis=-1, keepdims=True) + NORM_EPS)
            oh = o * r
            do = r * (doh[:, sl] - oh * jnp.mean(doh[:, sl] * oh, axis=-1, keepdims=True))
            datt = jnp.where(mask, _dot_nt(do, v), 0.0)
            dvs.append(_dot_tn(att, do) + _dot_nt(kh, dst))
            d_qbs.append(_dot_x3(do, st))
            d_qts.append(_dot_x3(datt, kt))
            d_kts.append(_dot_x3(datt, qt, ((0,), (0,))))
            d_kh = _dot_x3(v, dst)
            d_khs.append(d_kh)
            d_bls.append(jnp.sum(dst * st, axis=0, keepdims=True) * gt["ebl"][:, sl]
                         + jnp.sum(d_kh * kh, axis=0, keepdims=True))
            new_dsts.append(dst * gt["ebl"][:, sl] + _dot_tn(do, qb))
            ohs.append(oh)
        oh, d_qt, d_qb, d_kt, d_kh, dv, d_bl = (jnp.concatenate(p, axis=1) for p in
                                                (ohs, d_qts, d_qbs, d_kts, d_khs, dvs, d_bls))
        dxg = d_ob * (oh * ng) * (sgg * (1.0 + xg * (1.0 - sgg)))
        dng_ref[...] += jnp.sum(d_on * oh, axis=0, keepdims=True)
        dq = d_qt * gt["eqm"] + d_qb * gt["eb"]
        db = d_qt * gt["qt"] + d_qb * gt["qb"] - d_kt * gt["kt"] - d_kh * gt["kh"]
        rowi = lax.broadcasted_iota(jnp.int32, (C, HK), 0)
        db = db + jnp.where(rowi == C - 1, d_bl, 0.0)
        dkk = d_kt * gt["ekm"] + d_kh * gt["ekl"]
        dlg = _dot_tri(_tri(C, False).astype(F32), db)
        df = dlg / gt["f"] - dkk
        sg, sq = gt["sg"], gt["sq"]
        dlb_ref[...] += jnp.sum(df * (1.0 - sg), axis=0, keepdims=True)
        dx_ref[rs, HK:2 * HK] = (dq * (sq * (1.0 + xq * (1.0 - sq)))).astype(dx_ref.dtype)
        dx_ref[rs, 2 * HK:3 * HK] = (df * (1.0 - lb_all) * sg * (1.0 - sg)).astype(dx_ref.dtype)
        dx_ref[rs, 3 * HK:4 * HK] = dv.astype(dx_ref.dtype)
        dx_ref[rs, 4 * HK:5 * HK] = dxg.astype(dx_ref.dtype)
        return new_dsts

    S = HG_STEP_CHUNKS
    ns = nc // S

    def blk(cb):
        return pl.BlockSpec((S * C, HK), lambda i: (ns - 1 - i, cb))

    vec = pl.BlockSpec((1, HK), lambda i: (0, 0))
    return pl.pallas_call(
        body, grid=(ns,),
        in_specs=[blk(1), blk(2), blk(3), blk(4), vec, vec,
                  pl.BlockSpec((S, H, K, K), lambda i: (ns - 1 - i, 0, 0, 0)), blk(1), blk(0)],
        out_specs=[pl.BlockSpec((S * C, 5 * HK), lambda i: (ns - 1 - i, 0)), vec, vec],
        out_shape=[jax.ShapeDtypeStruct((L, 5 * HK), BF16), jax.ShapeDtypeStruct((1, HK), F32),
                   jax.ShapeDtypeStruct((1, HK), F32)],
        scratch_shapes=[pltpu.VMEM((H, K, K), F32)],
        compiler_params=_params(dimension_semantics=("arbitrary",)), name=name,
    )(proj, proj, proj, proj, lb, norm_g, sall, dcat, du)


def _shift_down(x, k, row):
    return jnp.where(row >= k, pltpu.roll(x, k, 0), 0.0)


def _shift_up(x, k, row):
    n = x.shape[0]
    return jnp.where(row < n - k, pltpu.roll(x, n - k, 0), 0.0)


def convgate_fwd(hu, conv_w, conv_b, *, name):
    L, C2 = hu.shape
    C = C2 // 2
    tc = _pick(C, (256, 128))
    nb = C // tc

    def body(a_ref, b_ref, wa_ref, wb_ref, ba_ref, bb_ref, o_ref):
        row = lax.broadcasted_iota(jnp.int32, (L, tc), 0)

        def conv(x, w, bias):
            return w[2:3, :] * x + w[1:2, :] * _shift_down(x, 1, row) + w[0:1, :] * _shift_down(x, 2, row) + bias

        ca = conv(a_ref[...], wa_ref[...], ba_ref[...])
        cb = conv(b_ref[...], wb_ref[...], bb_ref[...])
        o_ref[...] = (ca * _sigmoid(ca) * cb).astype(o_ref.dtype)

    def col(off, rows):
        return pl.BlockSpec((rows, tc), lambda j: (0, j + off))

    return pl.pallas_call(
        body, grid=(nb,), in_specs=[col(0, L), col(nb, L), col(0, 3), col(nb, 3), col(0, 1), col(nb, 1)],
        out_specs=col(0, L), out_shape=jax.ShapeDtypeStruct((L, C), BF16),
        compiler_params=_params(dimension_semantics=("parallel",)), name=name,
    )(hu, hu, conv_w, conv_w, conv_b, conv_b)


def convgate_bwd(hu, conv_w, conv_b, dact, *, name):
    L, C2 = hu.shape
    C = C2 // 2
    tc = _pick(C, (256, 128))
    nb = C // tc

    def body(a_ref, b_ref, wa_ref, wb_ref, ba_ref, bb_ref, d_ref, dxa_ref, dxb_ref, dwa_ref, dwb_ref, dba_ref, dbb_ref):
        row = lax.broadcasted_iota(jnp.int32, (L, tc), 0)

        def conv(x, w, bias):
            x1 = _shift_down(x, 1, row)
            x2 = _shift_down(x, 2, row)
            return w[2:3, :] * x + w[1:2, :] * x1 + w[0:1, :] * x2 + bias, x1, x2

        xa, xb = a_ref[...], b_ref[...]
        wa, wb = wa_ref[...], wb_ref[...]
        ca, xa1, xa2 = conv(xa, wa, ba_ref[...])
        cb, xb1, xb2 = conv(xb, wb, bb_ref[...])
        d = d_ref[...]
        sa = _sigmoid(ca)
        dca = d * cb * (sa * (1.0 + ca * (1.0 - sa)))
        dcb = d * (ca * sa)

        def back(dc, w, x, x1, x2, dx_ref, dw_ref, db_ref):
            dx = w[2:3, :] * dc + w[1:2, :] * _shift_up(dc, 1, row) + w[0:1, :] * _shift_up(dc, 2, row)
            dx_ref[...] = dx.astype(dx_ref.dtype)
            dw_ref[...] = jnp.concatenate([jnp.sum(dc * x2, axis=0, keepdims=True),
                                           jnp.sum(dc * x1, axis=0, keepdims=True),
                                           jnp.sum(dc * x, axis=0, keepdims=True)], axis=0)
            db_ref[...] = jnp.sum(dc, axis=0, keepdims=True)

        back(dca, wa, xa, xa1, xa2, dxa_ref, dwa_ref, dba_ref)
        back(dcb, wb, xb, xb1, xb2, dxb_ref, dwb_ref, dbb_ref)

    def col(off, rows):
        return pl.BlockSpec((rows, tc), lambda j: (0, j + off))

    outs = pl.pallas_call(
        body, grid=(nb,),
        in_specs=[col(0, L), col(nb, L), col(0, 3), col(nb, 3), col(0, 1), col(nb, 1), col(0, L)],
        out_specs=[col(0, L), col(0, L), col(0, 3), col(0, 3), col(0, 1), col(0, 1)],
        out_shape=[jax.ShapeDtypeStruct((L, C), BF16)] * 2 + [jax.ShapeDtypeStruct((3, C), F32)] * 2
        + [jax.ShapeDtypeStruct((1, C), F32)] * 2,
        compiler_params=_params(dimension_semantics=("parallel",)), name=name,
    )(hu, hu, conv_w, conv_w, conv_b, conv_b, dact)
    dxa, dxb, dwa, dwb, dba, dbb = outs
    return (dxa, dxb), jnp.concatenate([dwa, dwb], axis=1), jnp.concatenate([dba, dbb], axis=1)


def rope_tables(positions):
    half = ROT_DIM // 2
    inv_freq = ROPE_THETA ** (-jnp.arange(half, dtype=F32) * 2.0 / ROT_DIM)
    ang = positions.astype(F32)[:, None] * inv_freq
    cos, sin = jnp.cos(ang), jnp.sin(ang)
    L = positions.shape[0]
    one = jnp.ones((L, ATT_E - ROT_DIM), F32)
    zero = jnp.zeros((L, ATT_E - ROT_DIM), F32)
    zh = jnp.zeros((L, half), F32)
    tc = jnp.concatenate([cos, cos, one], axis=1)
    ts1 = jnp.concatenate([zh, sin, zero], axis=1)
    ts2 = jnp.concatenate([-sin, zh, zero], axis=1)
    return tuple(jnp.concatenate([t, t], axis=1) for t in (tc, ts1, ts2))


def norm_mm(x, g, w_t, *, tabs=None, name):
    L, D = x.shape
    N = w_t.shape[0]
    W = 512
    tm = _pick(L, (1024, 512, 256, 128))
    nq = N // (3 * W)
    scale = ATT_E ** -0.5
    rope = tabs is not None

    def body(x_ref, g_ref, b_ref, *rest):
        hn_ref, o_ref, hn_scr = rest[-3:]
        j = pl.program_id(1)

        @pl.when(j == 0)
        def _():
            xv = x_ref[...]
            r = lax.rsqrt(jnp.mean(xv * xv, axis=-1, keepdims=True) + NORM_EPS)
            hn = (xv * r * g_ref[...]).astype(BF16)
            hn_scr[...] = hn
            hn_ref[...] = hn

        out = _dot_nt(hn_scr[...], b_ref[...])
        if rope:
            c_ref, s1_ref, s2_ref = rest[:3]
            c = jnp.concatenate([c_ref[...]] * 4, axis=1)
            s1 = jnp.concatenate([s1_ref[...]] * 4, axis=1)
            s2 = jnp.concatenate([s2_ref[...]] * 4, axis=1)
            rot = out * c + pltpu.roll(out, 8, 1) * s1 + pltpu.roll(out, W - 8, 1) * s2
            out = jnp.where(j < 2 * nq, rot * jnp.where(j < nq, scale, 1.0), out)
        o_ref[...] = out

    row = pl.BlockSpec((tm, D), lambda i, j: (i, 0))
    tab = pl.BlockSpec((tm, 128), lambda i, j: (i, 0))
    return pl.pallas_call(body, grid=(L // tm, N // W),
                          in_specs=[row, pl.BlockSpec((1, D), lambda i, j: (0, 0)), pl.BlockSpec((W, D), lambda i, j: (j, 0))]
                          + ([tab, tab, tab] if rope else []),
                          out_specs=[row, pl.BlockSpec((tm, W), lambda i, j: (i, j))],
                          out_shape=[jax.ShapeDtypeStruct((L, D), BF16), jax.ShapeDtypeStruct((L, N), F32)],
                          scratch_shapes=[pltpu.VMEM((tm, D), BF16)],
                          compiler_params=_params(dimension_semantics=("parallel", "arbitrary")), name=name)(
        x, g.reshape(1, D), w_t, *(tabs or ()))


def rope_bwd(slabs, tabs, *, name):
    L, W = slabs[0].shape
    tr = _pick(L, (256, 128))
    nq = len(slabs) // 3
    scale = ATT_E ** -0.5

    def body(*refs):
        d_refs, (c_ref, s1_ref, s2_ref, o_ref) = refs[:3 * nq], refs[3 * nq:]
        c = jnp.concatenate([c_ref[...]] * 4, axis=1)
        s1 = jnp.concatenate([s1_ref[...]] * 4, axis=1)
        s2 = jnp.concatenate([s2_ref[...]] * 4, axis=1)
        for j, d_ref in enumerate(d_refs):
            dy = d_ref[...]
            if j < 2 * nq:
                dy = dy * c + pltpu.roll(dy * s1, W - 8, 1) + pltpu.roll(dy * s2, 8, 1)
            if j < nq:
                dy = dy * scale
            o_ref[:, j * W:(j + 1) * W] = dy.astype(o_ref.dtype)

    slab = pl.BlockSpec((tr, W), lambda i: (i, 0))
    tab = pl.BlockSpec((tr, 128), lambda i: (i, 0))
    return pl.pallas_call(body, grid=(L // tr,), in_specs=[slab] * (3 * nq) + [tab, tab, tab],
                          out_specs=pl.BlockSpec((tr, 3 * nq * W), lambda i: (i, 0)),
                          out_shape=jax.ShapeDtypeStruct((L, 3 * nq * W), BF16),
                          compiler_params=_params(dimension_semantics=("parallel",)), name=name)(*slabs, *tabs)


def _att_masks(has_prev):
    qi = lax.broadcasted_iota(jnp.int32, (ATT_BLOCK, ATT_BLOCK), 0)
    kj = lax.broadcasted_iota(jnp.int32, (ATT_BLOCK, ATT_BLOCK), 1)
    return qi >= kj, (kj >= qi) & has_prev


ATT_COLS = 128


def _att_rows(j, d, nb):
    B = ATT_BLOCK
    r, n = j // nb, j % nb
    start = r + d * B * n
    has_prev = n > 0
    pstart = jnp.where(has_prev, start - d * B, start)
    if d == 1:
        return pl.ds(pl.multiple_of(start, B), B), pl.ds(pl.multiple_of(pstart, B), B), has_prev
    return pl.ds(start, B, stride=d), pl.ds(pstart, B, stride=d), has_prev


def _qkv_specs(L, g):
    per = ATT_HPG * ATT_E // ATT_COLS
    third = len(ATT_DILATIONS) * per
    return [pl.BlockSpec((L, ATT_COLS), lambda c, base=base: (0, base + c))
            for base in (g * per, third + g * per, 2 * third + g * per)]


def attn_fwd(qkv, g, d, *, name):
    L, W = qkv.shape[0], ATT_HPG * ATT_E
    B, E = ATT_BLOCK, ATT_E
    nblk = L // B
    nb = nblk // d

    def body(q_ref, k_ref, v_ref, o_ref, l_ref):
        def step(j, carry):
            cur, prv, has_prev = _att_rows(j, d, nb)
            mc, mp = _att_masks(has_prev)
            qb, kc, kp, vc, vp = q_ref[cur, :], k_ref[cur, :], k_ref[prv, :], v_ref[cur, :], v_ref[prv, :]
            outs, lses = [], []
            for h in range(ATT_COLS // E):
                sl = slice(h * E, (h + 1) * E)
                sc = jnp.where(mc, _dot_nt(qb[:, sl], kc[:, sl]), NEG_BIG)
                sp = jnp.where(mp, _dot_nt(qb[:, sl], kp[:, sl]), NEG_BIG)
                m = jnp.maximum(jnp.max(sc, axis=-1, keepdims=True), jnp.max(sp, axis=-1, keepdims=True))
                pc = jnp.exp(sc - m)
                pp = jnp.exp(sp - m)
                den = jnp.sum(pc, axis=-1, keepdims=True) + jnp.sum(pp, axis=-1, keepdims=True)
                outs.append((_dot(pc, vc[:, sl]) + _dot(pp, vp[:, sl])) / den)
                lses.append(jnp.broadcast_to(m + jnp.log(den), (B, E)))
            o_ref[cur, :] = jnp.concatenate(outs, axis=1)
            l_ref[cur, :] = jnp.concatenate(lses, axis=1)
            return carry

        lax.fori_loop(0, nblk, step, 0, unroll=4)

    col = pl.BlockSpec((L, ATT_COLS), lambda c: (0, c))
    return pl.pallas_call(body, grid=(W // ATT_COLS,), in_specs=_qkv_specs(L, g), out_specs=[col] * 2,
                          out_shape=[jax.ShapeDtypeStruct((L, W), F32)] * 2,
                          compiler_params=_params(dimension_semantics=("parallel",)), name=name)(qkv, qkv, qkv)


def attn_bwd(qkv, g, lse, do, dl, d, *, name):
    L, W = qkv.shape[0], ATT_HPG * ATT_E
    B, E = ATT_BLOCK, ATT_E
    nblk = L // B
    nb = nblk // d

    def body(q_ref, k_ref, v_ref, l_ref, do_ref, dl_ref, dq_ref, dk_ref, dv_ref):
        dk_ref[...] = jnp.zeros_like(dk_ref)
        dv_ref[...] = jnp.zeros_like(dv_ref)

        def step(j, carry):
            cur, prv, has_prev = _att_rows(j, d, nb)
            mc, mp = _att_masks(has_prev)
            qb, kc, kp, vc, vp = q_ref[cur, :], k_ref[cur, :], k_ref[prv, :], v_ref[cur, :], v_ref[prv, :]
            lb, dob, dlb = l_ref[cur, :], do_ref[cur, :], dl_ref[cur, :]
            dqs, dkc, dkp, dvc, dvp = [], [], [], [], []
            for h in range(ATT_COLS // E):
                sl = slice(h * E, (h + 1) * E)
                qh, doh = qb[:, sl], dob[:, sl]
                lse_h, dl_h = lb[:, h * E:h * E + 1], dlb[:, h * E:h * E + 1]
                pc = jnp.where(mc, jnp.exp(_dot_nt(qh, kc[:, sl]) - lse_h), 0.0)
                pp = jnp.where(mp, jnp.exp(_dot_nt(qh, kp[:, sl]) - lse_h), 0.0)
                dsc = pc * (_dot_nt(doh, vc[:, sl]) - dl_h)
                dsp = pp * (_dot_nt(doh, vp[:, sl]) - dl_h)
                dqs.append(_dot(dsc, kc[:, sl]) + _dot(dsp, kp[:, sl]))
                dkc.append(_dot_tn(dsc, qh))
                dkp.append(_dot_tn(dsp, qh))
                dvc.append(_dot_tn(pc, doh))
                dvp.append(_dot_tn(pp, doh))
            dq_ref[cur, :] = jnp.concatenate(dqs, axis=1)
            dk_ref[cur, :] = dk_ref[cur, :] + jnp.concatenate(dkc, axis=1)
            dv_ref[cur, :] = dv_ref[cur, :] + jnp.concatenate(dvc, axis=1)
            dk_ref[prv, :] = dk_ref[prv, :] + jnp.concatenate(dkp, axis=1)
            dv_ref[prv, :] = dv_ref[prv, :] + jnp.concatenate(dvp, axis=1)
            return carry

        lax.fori_loop(0, nblk, step, 0, unroll=4)

    col = pl.BlockSpec((L, ATT_COLS), lambda c: (0, c))
    return pl.pallas_call(body, grid=(W // ATT_COLS,), in_specs=_qkv_specs(L, g) + [col] * 3, out_specs=[col] * 3,
                          out_shape=[jax.ShapeDtypeStruct((L, W), F32)] * 3,
                          compiler_params=_params(dimension_semantics=("parallel",)), name=name)(
        qkv, qkv, qkv, lse, do, dl)


def _merge_alpha(l_refs):
    ls = [r[...] for r in l_refs]
    m = jnp.maximum(jnp.maximum(ls[0], ls[1]), ls[2])
    es = [jnp.exp(l - m) for l in ls]
    den = es[0] + es[1] + es[2]
    return [e / den for e in es]


def merge_fwd(os_, ls_, *, name):
    L, W = os_[0].shape
    tr = _pick(L, (256, 128))

    def body(o0, o1, o2, l0, l1, l2, out_ref):
        al = _merge_alpha((l0, l1, l2))
        out_ref[...] = (al[0] * o0[...] + al[1] * o1[...] + al[2] * o2[...]).astype(out_ref.dtype)

    row = pl.BlockSpec((tr, W), lambda i: (i, 0))
    return pl.pallas_call(body, grid=(L // tr,), in_specs=[row] * 6, out_specs=row,
                          out_shape=jax.ShapeDtypeStruct((L, W), BF16), name=name)(*os_, *ls_)


def merge_bwd(os_, ls_, do, *, name):
    L, W = do.shape
    tr = _pick(L, (256, 128))

    def body(o0, o1, o2, l0, l1, l2, do_ref, d0, d1, d2, e0, e1, e2):
        al = _merge_alpha((l0, l1, l2))
        dov = do_ref[...]
        r = lax.broadcasted_iota(jnp.int32, (W, W), 0) // ATT_E
        c = lax.broadcasted_iota(jnp.int32, (W, W), 1) // ATT_E
        ones_blk = (r == c).astype(F32)
        t = jnp.zeros_like(dov)
        for a, o in zip(al, (o0, o1, o2)):
            t = t + a * _dot_tri(ones_blk, dov * o[...], tri_left=False)
        for a, d_ref, e_ref in zip(al, (d0, d1, d2), (e0, e1, e2)):
            d_ref[...] = a * dov
            e_ref[...] = a * t

    row = pl.BlockSpec((tr, W), lambda i: (i, 0))
    return pl.pallas_call(body, grid=(L // tr,), in_specs=[row] * 7, out_specs=[row] * 6,
                          out_shape=[jax.ShapeDtypeStruct((L, W), F32)] * 6, name=name)(*os_, *ls_, do)


def _me_and_peers():
    x, y, c = lax.axis_index("x"), lax.axis_index("y"), lax.axis_index("c")
    peers = []
    for k in range(1, N_DEV):
        px = 1 - x if k & 4 else x
        py = 1 - y if k & 2 else y
        pc = 1 - c if k & 1 else c
        peers.append((px, py, pc))
    return (x, y, c), peers


def _index(dev):
    return 4 * dev[0] + 2 * dev[1] + dev[2]


def _hbm(a):
    return pltpu.with_memory_space_constraint(a, pltpu.HBM)


HBM_SPEC = pl.BlockSpec(memory_space=pltpu.HBM)
SEM_SPEC = pl.BlockSpec(memory_space=pltpu.SEMAPHORE)
DATAFLOW = pltpu.SideEffectType.DATAFLOW_SIDE_EFFECTING


def _remote(src_ref, land_ref, slotted, me, peer, src_is_mine, send_sem, recv_sem, k):
    sender, receiver = (me, peer) if src_is_mine else (peer, me)
    src = src_ref.at[_index(receiver)] if slotted else src_ref
    return pltpu.make_async_remote_copy(src_ref=src, dst_ref=land_ref.at[_index(sender)], send_sem=send_sem.at[k],
                                        recv_sem=recv_sem.at[k], device_id=peer, device_id_type=MESH_ID)


SIBLING = 0
SAME_CORE = (1, 3, 5)
OTHER_CORE = (2, 4, 6)


def copies_start(arrays, mode, *, name):
    n = len(arrays)
    slotted = mode == "exchange"
    lands = [lax.empty(a.shape if slotted else (N_DEV,) + a.shape, a.dtype) for a in arrays]
    targets = (SIBLING,) + SAME_CORE if mode == "gather2" else tuple(range(N_DEV - 1))

    def body(*refs):
        x_refs, land_refs = refs[:n], refs[n:2 * n]
        send, recv = refs[2 * n:3 * n], refs[3 * n:4 * n]
        token = refs[-1]
        me, peers = _me_and_peers()
        for w in range(n):
            for k in targets:
                _remote(x_refs[w], land_refs[w], slotted, me, peers[k], True, send[w], recv[w], k).start()
            if not slotted:
                pltpu.make_async_copy(x_refs[w], land_refs[w].at[_index(me)], recv[w].at[N_DEV - 1]).start()
        token[...] = jnp.zeros_like(token)

    sem = pltpu.SemaphoreType.DMA((N_DEV,))
    out_shape = ([sem] * (2 * n) + [pltpu.HBM(a.shape, a.dtype) for a in arrays]
                 + [pltpu.HBM(l.shape, l.dtype) for l in lands] + [jax.ShapeDtypeStruct((8, 128), F32)])
    outs = pl.pallas_call(
        body, name=name, out_shape=out_shape, in_specs=[HBM_SPEC] * (2 * n),
        out_specs=[SEM_SPEC] * (2 * n) + [HBM_SPEC] * (2 * n) + [pl.BlockSpec(memory_space=pltpu.VMEM)],
        input_output_aliases={i: 2 * n + i for i in range(2 * n)},
        compiler_params=pltpu.CompilerParams(has_side_effects=DATAFLOW),
    )(*[_hbm(a) for a in arrays], *[_hbm(l) for l in lands])
    handles = [(outs[w], outs[n + w], outs[2 * n + w], outs[3 * n + w]) for w in range(n)]
    return handles, outs[-1]


def _forward(land_ref, me, peers, j, fsend, frecv, mine):
    block = _index(peers[SAME_CORE[j]] if mine else peers[OTHER_CORE[j]])
    return pltpu.make_async_remote_copy(src_ref=land_ref.at[block], dst_ref=land_ref.at[block], send_sem=fsend.at[j],
                                        recv_sem=frecv.at[j], device_id=peers[SIBLING], device_id_type=MESH_ID)


def copies_forward(handles, after, *, name):
    n = len(handles)

    def body(*refs):
        land_refs, recv = refs[:n], refs[n:2 * n]
        fsend, frecv = refs[2 * n + 1:3 * n + 1], refs[3 * n + 1:4 * n + 1]
        token = refs[-1]
        me, peers = _me_and_peers()
        for w in range(n):
            for j, k in enumerate(SAME_CORE):
                block = land_refs[w].at[_index(peers[k])]
                pltpu.make_async_remote_copy(src_ref=block, dst_ref=block, send_sem=recv[w].at[N_DEV - 1],
                                             recv_sem=recv[w].at[k], device_id=peers[k], device_id_type=MESH_ID).wait_recv()
                _forward(land_refs[w], me, peers, j, fsend[w], frecv[w], True).start()
        token[...] = jnp.zeros_like(token)

    sem = pltpu.SemaphoreType.DMA((len(SAME_CORE),))
    lands = [h[3] for h in handles]
    outs = pl.pallas_call(
        body, name=name,
        out_shape=[sem] * (2 * n) + [pltpu.HBM(l.shape, l.dtype) for l in lands] + [jax.ShapeDtypeStruct((8, 128), F32)],
        in_specs=[HBM_SPEC] * n + [SEM_SPEC] * n + [pl.BlockSpec(memory_space=pl.ANY)],
        out_specs=[SEM_SPEC] * (2 * n) + [HBM_SPEC] * n + [pl.BlockSpec(memory_space=pltpu.VMEM)],
        input_output_aliases={w: 2 * n + w for w in range(n)},
        compiler_params=pltpu.CompilerParams(has_side_effects=DATAFLOW),
    )(*lands, *[h[1] for h in handles], after)
    new = [(h[0], h[1], h[2], outs[2 * n + w], outs[w], outs[n + w]) for w, h in enumerate(handles)]
    return new, outs[-1]


def copies_wait(handle, mode, after, *, name):
    slotted = mode == "exchange"
    two_level = mode == "gather2"
    send_sem, recv_sem, x_thru, land_thru = handle[:4]
    targets = (SIBLING,) + SAME_CORE if two_level else tuple(range(N_DEV - 1))
    arrivals = (SIBLING,) if two_level else targets

    def body(x_ref, land_ref, send_ref, recv_ref, *rest):
        me, peers = _me_and_peers()
        for k in targets:
            _remote(x_ref, land_ref, slotted, me, peers[k], True, send_ref, recv_ref, k).wait_send()
        for k in arrivals:
            _remote(x_ref, land_ref, slotted, me, peers[k], False, send_ref, recv_ref, k).wait_recv()
        if not slotted:
            pltpu.make_async_copy(x_ref, land_ref.at[_index(me)], recv_ref.at[N_DEV - 1]).wait()
        if two_level:
            fsend, frecv = rest[0], rest[1]
            for j in range(len(SAME_CORE)):
                _forward(land_ref, me, peers, j, fsend, frecv, True).wait_send()
                _forward(land_ref, me, peers, j, fsend, frecv, False).wait_recv()

    extra = list(handle[4:])
    return pl.pallas_call(
        body, name=name, out_shape=(pltpu.HBM(x_thru.shape, x_thru.dtype), pltpu.HBM(land_thru.shape, land_thru.dtype)),
        in_specs=[HBM_SPEC, HBM_SPEC, SEM_SPEC, SEM_SPEC] + [SEM_SPEC] * len(extra) + [pl.BlockSpec(memory_space=pl.ANY)],
        out_specs=(HBM_SPEC, HBM_SPEC), input_output_aliases={0: 0, 1: 1},
        compiler_params=pltpu.CompilerParams(has_side_effects=DATAFLOW),
    )(x_thru, land_thru, send_sem, recv_sem, *extra, after)


def cast_bf16(x, *, dep=None, name):
    R, C = x.shape
    tr = _pick(R, (512, 352, 256, 128, 64))
    deps = [] if dep is None else [dep]

    def body(x_ref, *rest):
        rest[-1][...] = x_ref[...].astype(BF16)

    row = pl.BlockSpec((tr, C), lambda i: (i, 0))
    return pl.pallas_call(body, grid=(R // tr,), in_specs=[row] + [pl.BlockSpec((8, 128), lambda i: (0, 0))] * len(deps),
                          out_specs=row, out_shape=jax.ShapeDtypeStruct((R, C), BF16), name=name)(x, *deps)


def cast_bf16_layer(x3, layer, *, name):
    _, R, C = x3.shape
    tr = _pick(R, (512, 352, 256, 128, 64))

    def body(x_ref, o_ref):
        o_ref[...] = x_ref[...].astype(BF16)

    return pl.pallas_call(body, grid=(R // tr,), in_specs=[pl.BlockSpec((None, tr, C), lambda i: (layer, i, 0))],
                          out_specs=pl.BlockSpec((tr, C), lambda i: (i, 0)),
                          out_shape=jax.ShapeDtypeStruct((R, C), BF16), name=name)(x3)


BD_PARTS = 4


def _blockdiag_call(b, build, G, r, c, name):
    gp = G // BD_PARTS

    def body_build(b_ref, o_ref):
        o_ref[...] = jnp.zeros_like(o_ref)
        for g in range(G):
            o_ref[g // gp, (g % gp) * r:(g % gp + 1) * r, (g % gp) * c:(g % gp + 1) * c] = b_ref[g]

    def body_extract(d_ref, o_ref):
        for g in range(G):
            o_ref[g] = d_ref[g // gp, (g % gp) * r:(g % gp + 1) * r, (g % gp) * c:(g % gp + 1) * c]

    out = jax.ShapeDtypeStruct((BD_PARTS, gp * r, gp * c) if build else (G, r, c), F32)
    return pl.pallas_call(body_build if build else body_extract, out_shape=out, name=name)(b)


def make_blockdiag(G, r, c, name):
    @jax.custom_vjp
    def blockdiag(b):
        return _blockdiag_call(b, True, G, r, c, name + "_build")

    def fwd(b):
        return blockdiag(b), None

    def bwd(_, g):
        return (_blockdiag_call(g, False, G, r, c, name + "_extract"),)

    blockdiag.defvjp(fwd, bwd)
    return blockdiag


def cols_from_shards(g, *, name):
    _, K, n = g.shape
    tk = _pick(K, (256, 128))

    def body(g_ref, o_ref):
        for i in range(N_DEV):
            o_ref[:, i * n:(i + 1) * n] = g_ref[i]

    return pl.pallas_call(body, grid=(K // tk,), in_specs=[pl.BlockSpec((N_DEV, tk, n), lambda i: (0, i, 0))],
                          out_specs=pl.BlockSpec((tk, N_DEV * n), lambda i: (i, 0)),
                          out_shape=jax.ShapeDtypeStruct((K, N_DEV * n), g.dtype), name=name)(g)


def shards_from_cols(w, *, name):
    K, N = w.shape
    n = N // N_DEV
    tk = _pick(K, (256, 128))

    def body(w_ref, o_ref):
        for i in range(N_DEV):
            o_ref[i] = w_ref[:, i * n:(i + 1) * n].astype(o_ref.dtype)

    return pl.pallas_call(body, grid=(K // tk,), in_specs=[pl.BlockSpec((tk, N), lambda i: (i, 0))],
                          out_specs=pl.BlockSpec((N_DEV, tk, n), lambda i: (0, i, 0)),
                          out_shape=jax.ShapeDtypeStruct((N_DEV, K, n), BF16), name=name)(w)


def _adamw(w, g, m, v):
    m = ADAM_B1 * m + (1.0 - ADAM_B1) * g
    v = ADAM_B2 * v + (1.0 - ADAM_B2) * (g * g)
    m_hat = m / (1.0 - ADAM_B1 ** ADAM_STEP)
    v_hat = v / (1.0 - ADAM_B2 ** ADAM_STEP)
    delta = -ADAM_LR * (m_hat / (jnp.sqrt(v_hat) + ADAM_EPS) + ADAM_WD * w)
    return delta, m, v


def reduce_adamw(recv, own, own_slotted, me, w, m, v, *, layer=0, n_layers=1, into=None, name):
    _, R, C = recv.shape
    tr = _pick(R, (176, 192, 160, 184, 128, 64, 32, 16, 8))
    off = layer * (R // tr)

    def body(me_ref, r_ref, own_ref, w_ref, m_ref, v_ref, *rest):
        g_ref, d_ref, nm_ref, nv_ref = rest[-4:]
        mine = me_ref[0]
        g = None
        for i in range(N_DEV):
            part = jnp.where(mine == i, own_ref[...], r_ref[i]).astype(F32)
            g = part if g is None else g + part
        delta, nm, nv = _adamw(w_ref[...], g, m_ref[...], v_ref[...])
        g_ref[...] = g
        d_ref[...] = delta
        nm_ref[...] = nm
        nv_ref[...] = nv

    row = pl.BlockSpec((tr, C), lambda i, me_ref: (i + off, 0))
    own_spec = (pl.BlockSpec((None, tr, C), lambda i, me_ref: (me_ref[0], i, 0)) if own_slotted
                else pl.BlockSpec((tr, C), lambda i, me_ref: (i, 0)))
    rest = [] if into is None else list(into)
    grid_spec = pltpu.PrefetchScalarGridSpec(
        num_scalar_prefetch=1, grid=(R // tr,),
        in_specs=[pl.BlockSpec((N_DEV, tr, C), lambda i, me_ref: (0, i, 0)), own_spec, row, row, row]
        + [pl.BlockSpec(memory_space=pl.ANY)] * len(rest),
        out_specs=[row] * 4)
    return pl.pallas_call(body, grid_spec=grid_spec, out_shape=[jax.ShapeDtypeStruct((n_layers * R, C), F32)] * 4,
                          input_output_aliases={6 + k: k for k in range(len(rest))},
                          compiler_params=_params(dimension_semantics=("parallel",)), name=name)(
        me.reshape(1).astype(jnp.int32), recv, own, w, m, v, *rest)


def _s5_prepare(A_re, A_im, log_dt, B_re, B_im, C_re, C_im):
    G, P, Cg = S5_GROUPS, S5_STATE, S5_GROUP
    dt = jnp.exp(log_dt)[:, None]
    mag = jnp.exp(A_re * dt)
    ab_re = mag * jnp.cos(A_im * dt)
    ab_im = mag * jnp.sin(A_im * dt)
    den = A_re * A_re + A_im * A_im
    nr, ni = ab_re - 1.0, ab_im
    c_re = (nr * A_re + ni * A_im) / den
    c_im = (ni * A_re - nr * A_im) / den
    Bb_re = c_re[..., None] * B_re - c_im[..., None] * B_im
    Bb_im = c_re[..., None] * B_im + c_im[..., None] * B_re
    def dense_in(b, name):
        return make_blockdiag(G, Cg, P, name)(b.transpose(0, 2, 1))

    def dense_out(c, name):
        return make_blockdiag(G, P, Cg, name)(c.transpose(0, 2, 1))

    return (ab_re.reshape(1, G * P), ab_im.reshape(1, G * P), dense_in(Bb_re, "s5_wb_re"), dense_in(Bb_im, "s5_wb_im"),
            dense_out(C_re, "s5_wc_re"), dense_out(-C_im, "s5_wc_im"))


def _lower_bound(gamma):
    return jnp.cumsum(jax.nn.softmax(gamma, axis=0), axis=0)[0:1]


def _ffn_fwd(h, g_norm, get_w_in, conv_w, conv_b, get_w_out, tag, final=None):
    w_in = get_w_in(h)
    hn, hu = norm_mm(h, g_norm, w_in, name=tag + "_in")
    act = convgate_fwd(hu, conv_w, conv_b, name=tag + "_gate")
    w_out = get_w_out(act)
    if final is None:
        h_out = mm(act, w_out, res=h, name=tag + "_out")
    else:
        h_out = mm_final_loss(act, w_out, h, final[0], final[1], name=tag + "_out_loss")
    return h_out, (hn, hu, act), w_in, w_out


def _ffn_bwd(h, g_norm, w_in, conv_w, conv_b, w_out, saved, dh, tag, send_dw_in, send_dw_out):
    hn, hu, act = saved
    sent = send_dw_out(mm(act, dh, ta=True, out_dtype=BF16, name=tag + "_dwout"))
    dact = mm(dh, w_out, tb=True, dep=sent, name=tag + "_dact")
    (dhu_a, dhu_b), dconv_w, dconv_b = convgate_bwd(hu, conv_w, conv_b, dact, name=tag + "_dgate")
    rows = 2 * dhu_a.shape[1]
    dw_in = mm(dhu_a, hn, ta=True, out_dtype=BF16, out_rows=rows, name=tag + "_dwin_a")
    dw_in = mm(dhu_b, hn, ta=True, out_dtype=BF16, out_rows=rows, out_off=rows // 2, into=dw_in, name=tag + "_dwin_b")
    sent = send_dw_in(dw_in)
    dh_in, dg = mm_drms((dhu_a, dhu_b), w_in, h, g_norm, dh, dep=sent, name=tag + "_dhn")
    return dh_in, dg, dconv_w, dconv_b


def kernel(x, positions, norm_mix, norm_ffn, norm_final, mix_w_in, mix_w_out, s5_A_re, s5_A_im, s5_log_dt, s5_B_re, s5_B_im, s5_C_re, s5_C_im, s5_D, s5_glu_w, s5_glu_b, hgrn_gamma, hgrn_norm, att_w_qkv, att_w_o, ffn_w_in, ffn_conv_w, ffn_conv_b, ffn_w_out, loss_target, m_norm_mix, m_norm_ffn, m_norm_final, m_mix_w_in, m_mix_w_out, m_s5_A_re, m_s5_A_im, m_s5_log_dt, m_s5_B_re, m_s5_B_im, m_s5_C_re, m_s5_C_im, m_s5_D, m_s5_glu_w, m_s5_glu_b, m_hgrn_gamma, m_hgrn_norm, m_att_w_qkv, m_att_w_o, m_ffn_w_in, m_ffn_conv_w, m_ffn_conv_b, m_ffn_w_out, v_norm_mix, v_norm_ffn, v_norm_final, v_mix_w_in, v_mix_w_out, v_s5_A_re, v_s5_A_im, v_s5_log_dt, v_s5_B_re, v_s5_B_im, v_s5_C_re, v_s5_C_im, v_s5_D, v_s5_glu_w, v_s5_glu_b, v_hgrn_gamma, v_hgrn_norm, v_att_w_qkv, v_att_w_o, v_ffn_w_in, v_ffn_conv_w, v_ffn_conv_b, v_ffn_w_out):
    W = dict(norm_mix=norm_mix, norm_ffn=norm_ffn, norm_final=norm_final, mix_w_in=mix_w_in, mix_w_out=mix_w_out,
             s5_A_re=s5_A_re, s5_A_im=s5_A_im, s5_log_dt=s5_log_dt, s5_B_re=s5_B_re, s5_B_im=s5_B_im,
             s5_C_re=s5_C_re, s5_C_im=s5_C_im, s5_D=s5_D, s5_glu_w=s5_glu_w, s5_glu_b=s5_glu_b,
             hgrn_gamma=hgrn_gamma, hgrn_norm=hgrn_norm, att_w_qkv=att_w_qkv, att_w_o=att_w_o, ffn_w_in=ffn_w_in,
             ffn_conv_w=ffn_conv_w, ffn_conv_b=ffn_conv_b, ffn_w_out=ffn_w_out)
    M = dict(norm_mix=m_norm_mix, norm_ffn=m_norm_ffn, norm_final=m_norm_final, mix_w_in=m_mix_w_in,
             mix_w_out=m_mix_w_out, s5_A_re=m_s5_A_re, s5_A_im=m_s5_A_im, s5_log_dt=m_s5_log_dt, s5_B_re=m_s5_B_re,
             s5_B_im=m_s5_B_im, s5_C_re=m_s5_C_re, s5_C_im=m_s5_C_im, s5_D=m_s5_D, s5_glu_w=m_s5_glu_w,
             s5_glu_b=m_s5_glu_b, hgrn_gamma=m_hgrn_gamma, hgrn_norm=m_hgrn_norm, att_w_qkv=m_att_w_qkv,
             att_w_o=m_att_w_o, ffn_w_in=m_ffn_w_in, ffn_conv_w=m_ffn_conv_w, ffn_conv_b=m_ffn_conv_b,
             ffn_w_out=m_ffn_w_out)
    V = dict(norm_mix=v_norm_mix, norm_ffn=v_norm_ffn, norm_final=v_norm_final, mix_w_in=v_mix_w_in,
             mix_w_out=v_mix_w_out, s5_A_re=v_s5_A_re, s5_A_im=v_s5_A_im, s5_log_dt=v_s5_log_dt, s5_B_re=v_s5_B_re,
             s5_B_im=v_s5_B_im, s5_C_re=v_s5_C_re, s5_C_im=v_s5_C_im, s5_D=v_s5_D, s5_glu_w=v_s5_glu_w,
             s5_glu_b=v_s5_glu_b, hgrn_gamma=v_hgrn_gamma, hgrn_norm=v_hgrn_norm, att_w_qkv=v_att_w_qkv,
             att_w_o=v_att_w_o, ffn_w_in=v_ffn_w_in, ffn_conv_w=v_ffn_conv_w, ffn_conv_b=v_ffn_conv_b,
             ffn_w_out=v_ffn_w_out)
    return _step(x[0], positions[0], loss_target[0], W, M, V)


TRANSPOSED = ("mix_w_in", "att_w_qkv", "ffn_w_in")
SMALL = ("norm_mix", "norm_ffn", "norm_final", "s5_A_re", "s5_A_im", "s5_log_dt", "s5_B_re", "s5_B_im", "s5_C_re",
         "s5_C_im", "s5_D", "s5_glu_b", "hgrn_gamma", "hgrn_norm", "ffn_conv_b")
ORDER = ("norm_mix", "norm_ffn", "norm_final", "mix_w_in", "mix_w_out", "s5_A_re", "s5_A_im", "s5_log_dt", "s5_B_re",
         "s5_B_im", "s5_C_re", "s5_C_im", "s5_D", "s5_glu_w", "s5_glu_b", "hgrn_gamma", "hgrn_norm", "att_w_qkv",
         "att_w_o", "ffn_w_in", "ffn_conv_w", "ffn_conv_b", "ffn_w_out")
PACK_COLS = 1024


def _step(x, positions, target, W, M, V):
    L, D = x.shape
    me = 4 * lax.axis_index("x") + 2 * lax.axis_index("y") + lax.axis_index("c")
    n_cw = W["ffn_conv_w"].shape[-1]
    T = {n: tuple(jnp.swapaxes(d[n], -1, -2) for d in (W, M, V)) for n in TRANSPOSED}
    first = {
        "mix_w_in": cast_bf16(T["mix_w_in"][0][0], name="mix_w_in_cast"),
        "conv_w": W["ffn_conv_w"].reshape(6, n_cw),
        "s5_glu_w": cast_bf16(W["s5_glu_w"][0], name="s5_glu_w_cast"),
    }
    first_handles, token = copies_start(list(first.values()), "gather2", name="gather_start_first")
    shards = {
        "mix_w_out": cast_bf16(W["mix_w_out"][0], dep=token, name="mix_w_out_cast"),
        "ffn_w_in0": cast_bf16_layer(T["ffn_w_in"][0], 0, name="ffn_w_in0_cast"),
        "ffn_w_out0": cast_bf16_layer(W["ffn_w_out"], 0, name="ffn_w_out0_cast"),
        "att_w_qkv": cast_bf16(T["att_w_qkv"][0][0], name="att_w_qkv_cast"),
        "att_w_o": cast_bf16(W["att_w_o"][0], name="att_w_o_cast"),
        "ffn_w_in1": cast_bf16_layer(T["ffn_w_in"][0], 1, name="ffn_w_in1_cast"),
        "ffn_w_out1": cast_bf16_layer(W["ffn_w_out"], 1, name="ffn_w_out1_cast"),
    }
    gather_handles, token = copies_start(list(shards.values()), "gather2", name="gather_start")
    gather_handle = dict(zip(list(first) + list(shards), first_handles + gather_handles))

    def forward(keys, after, name):
        new, sent = copies_forward([gather_handle[k] for k in keys], after, name=name)
        gather_handle.update(zip(keys, new))
        return sent

    def gathered(key, after, cols):
        _, land = copies_wait(gather_handle[key], "gather2", after, name=key + "_gwait")
        return cols_from_shards(land, name=key + "_asm") if cols else land.reshape(-1, land.shape[-1])

    conv_b = W["ffn_conv_b"].reshape(2, 1, -1)

    s5_params = (W["s5_A_re"][0], W["s5_A_im"][0], W["s5_log_dt"][0], W["s5_B_re"][0], W["s5_B_im"][0],
                 W["s5_C_re"][0], W["s5_C_im"][0])
    (a_re, a_im, wb_re, wb_im, wc_re, wc_im), s5_prep_vjp = jax.vjp(_s5_prepare, *s5_params)
    dvec = W["s5_D"].reshape(1, S5_WIDTH)
    glu_b = W["s5_glu_b"].reshape(1, S5_WIDTH)
    lb, lb_vjp = jax.vjp(_lower_bound, W["hgrn_gamma"])
    hg_norm = W["hgrn_norm"].reshape(1, -1)
    tabs = rope_tables(positions)

    sent = forward(["mix_w_in", "conv_w", "s5_glu_w"], token, "forward_a")
    w_mix_in = gathered("mix_w_in", sent, False)
    hn0, proj = norm_mm(x, W["norm_mix"][0], w_mix_in, name="l0_proj")
    y0, xs_re, xs_im = s5_core_fwd(proj, a_re, a_im, wb_re, wb_im, wc_re, wc_im, name="s5_core")
    w_glu = gathered("s5_glu_w", y0, False)
    cat = s5_out_fwd(y0, proj, dvec, w_glu, glu_b, name="s5_out")
    cat, hg_states = hgrn_fwd(proj, lb, hg_norm, cat, name="hgrn_fwd")
    forward(["mix_w_out"], cat, "forward_b")
    w_mix_out = gathered("mix_w_out", cat, False)
    h1 = mm(cat, w_mix_out, res=x, name="l0_mix_out")
    _, cw_all = copies_wait(gather_handle["conv_w"], "gather2", h1, name="conv_w_gwait")
    conv_w = cw_all.transpose(1, 0, 2).reshape(2, 3, N_DEV * n_cw)
    w_ffn_in, w_ffn_out = [None, None], [None, None]
    h2, ffn0_saved, w_ffn_in[0], w_ffn_out[0] = _ffn_fwd(
        h1, W["norm_ffn"][0],
        lambda a: (forward(["ffn_w_in0"], a, "forward_b2"), gathered("ffn_w_in0", a, False))[1], conv_w[0], conv_b[0],
        lambda a: (forward(["ffn_w_out0"], a, "forward_c"), gathered("ffn_w_out0", a, False))[1], "ffn0")

    forward(["att_w_qkv", "att_w_o"], h2, "forward_d")
    w_qkv = gathered("att_w_qkv", h2, False)
    hn2, qkv_r = norm_mm(h2, W["norm_mix"][1], w_qkv, tabs=tabs, name="l1_qkv")
    att_o, att_l = [], []
    for g, d in enumerate(ATT_DILATIONS):
        o_g, l_g = attn_fwd(qkv_r, g, d, name=f"attn_fwd{g}")
        att_o.append(o_g)
        att_l.append(l_g)
    o_att = merge_fwd(att_o, att_l, name="merge_fwd")
    forward(["ffn_w_in1", "ffn_w_out1"], o_att, "forward_e")
    w_o = gathered("att_w_o", o_att, True)
    h3 = mm(o_att, w_o, res=h2, name="l1_mix_out")
    (loss, dh4, dg_final), ffn1_saved, w_ffn_in[1], w_ffn_out[1] = _ffn_fwd(
        h3, W["norm_ffn"][1], lambda a: gathered("ffn_w_in1", a, False), conv_w[1], conv_b[1],
        lambda a: gathered("ffn_w_out1", a, False), "ffn1", final=(W["norm_final"], target))

    exchanges = {}

    pending = []

    def send_grad(key, g, cols, flush=True):
        if cols:
            parts = shards_from_cols(g, name=key + "_split")
        else:
            parts = g.reshape(N_DEV, g.shape[0] // N_DEV, g.shape[1])
        pending.append((key, parts))
        if not flush:
            return None
        handles, sent = copies_start([p for _, p in pending], "exchange", name=key + "_xstart")
        exchanges.update(zip([k for k, _ in pending], handles))
        pending.clear()
        return sent

    dh3, dg_ffn1, dcw1, dcb1 = _ffn_bwd(h3, W["norm_ffn"][1], w_ffn_in[1], conv_w[1], conv_b[1], w_ffn_out[1],
                                        ffn1_saved, dh4, "ffn1", lambda g: send_grad("ffn_w_in1", g, False),
                                        lambda g: send_grad("ffn_w_out1", g, False, flush=False))
    sent = send_grad("att_w_o", mm(o_att, dh3, ta=True, name="l1_dwo"), True, flush=False)
    d_oatt = mm(dh3, w_o, tb=True, dep=sent, name="l1_dmix")
    mb = merge_bwd(att_o, att_l, d_oatt, name="merge_bwd")
    d_slabs = [attn_bwd(qkv_r, g, att_l[g], mb[g], mb[3 + g], d, name=f"attn_bwd{g}")
               for g, d in enumerate(ATT_DILATIONS)]
    d_qkv = rope_bwd([s[0] for s in d_slabs] + [s[1] for s in d_slabs] + [s[2] for s in d_slabs], tabs,
                     name="rope_bwd")
    sent = send_grad("att_w_qkv", mm(d_qkv, hn2, ta=True, out_dtype=BF16, name="l1_dwqkv"), False)
    dh2, dg_mix1 = mm_drms(d_qkv, w_qkv, h2, W["norm_mix"][1], dh3, dep=sent, name="l1_dhn")

    dh1, dg_ffn0, dcw0, dcb0 = _ffn_bwd(h1, W["norm_ffn"][0], w_ffn_in[0], conv_w[0], conv_b[0], w_ffn_out[0],
                                        ffn0_saved, dh2, "ffn0", lambda g: send_grad("ffn_w_in0", g, False),
                                        lambda g: send_grad("ffn_w_out0", g, False, flush=False))
    sent = send_grad("mix_w_out", mm(cat, dh1, ta=True, out_dtype=BF16, name="l0_dwout"), False)
    dcat = mm(dh1, w_mix_out, tb=True, dep=sent, name="l0_dcat")
    dy, du_d, z_bf, dzg, dglu_b, dD = s5_out_bwd(y0, proj, dvec, w_glu, glu_b, dcat, name="s5_dout")
    sent_glu = send_grad("s5_glu_w", mm(z_bf, dzg, ta=True, out_dtype=BF16, name="s5_dglu"), False, flush=False)
    du, dwb_re, dwb_im, dwc_re, dwc_im, da_re, da_im = s5_core_bwd(
        dy, du_d, proj, xs_re, xs_im, a_re, a_im, wb_re, wb_im, wc_re, wc_im, name="s5_dcore")
    s5_small = s5_prep_vjp((da_re, da_im, dwb_re, dwb_im, dwc_re, dwc_im))
    d_proj, dlb, dhg_norm = hgrn_bwd(proj, lb, hg_norm, hg_states, dcat, du, name="hgrn_bwd")
    sent = send_grad("mix_w_in", mm(d_proj, hn0, ta=True, out_dtype=BF16, dep=sent_glu, name="l0_dwin"), False)
    grad_x, dg_mix0 = mm_drms(d_proj, w_mix_in, x, W["norm_mix"][0], dh1, dep=sent, name="l0_dhn")
    (d_gamma,) = lb_vjp(dlb)
    out = {}

    dA_re, dA_im, dlog_dt, dB_re, dB_im, dC_re, dC_im = s5_small
    small_g = dict(norm_mix=jnp.concatenate([dg_mix0, dg_mix1], axis=0), norm_ffn=jnp.concatenate([dg_ffn0, dg_ffn1], axis=0),
                   norm_final=dg_final, s5_A_re=dA_re, s5_A_im=dA_im, s5_log_dt=dlog_dt, s5_B_re=dB_re, s5_B_im=dB_im,
                   s5_C_re=dC_re, s5_C_im=dC_im, s5_D=dD, s5_glu_b=dglu_b, hgrn_gamma=d_gamma, hgrn_norm=dhg_norm,
                   ffn_conv_b=jnp.concatenate([dcb0, dcb1], axis=0))
    conv_w_g = jnp.stack([dcw0, dcw1], axis=0)
    sizes = [math.prod(W[n].shape) for n in SMALL]
    n_conv = conv_w_g.size
    total = sum(sizes) + n_conv + 1
    rows = -(-total // PACK_COLS)
    rows = -(-rows // 8) * 8
    pad = rows * PACK_COLS - total

    def pack(vals, conv_part, last):
        flat = [v.reshape(-1).astype(F32) for v in vals] + [conv_part.reshape(-1), last.reshape(-1),
                                                            jnp.zeros((pad,), F32)]
        return jnp.concatenate(flat).reshape(rows, PACK_COLS)

    def conv_full(shard):
        col_owner = lax.broadcasted_iota(jnp.int32, (2, 3, N_DEV * n_cw), 2) // n_cw
        return jnp.where(col_owner == me, jnp.tile(shard, (1, 1, N_DEV)), 0.0)

    zero1 = jnp.zeros((1,), F32)
    g_pack = pack([small_g[n] for n in SMALL], conv_w_g, loss)
    w_pack = pack([W[n] for n in SMALL], conv_full(W["ffn_conv_w"]), zero1)
    m_pack = pack([M[n] for n in SMALL], conv_full(M["ffn_conv_w"]), zero1)
    v_pack = pack([V[n] for n in SMALL], conv_full(V["ffn_conv_w"]), zero1 + 1.0)
    (small_handle,), small_sent = copies_start([g_pack], "gather", name="small_xstart")

    def finish(name, n_layers):
        w3, m3, v3 = T[name] if name in TRANSPOSED else (W[name], M[name], V[name])
        res = None
        for layer in reversed(range(n_layers)):
            key = name if n_layers == 1 else f"{name}{layer}"
            own, recv = copies_wait(exchanges[key], "exchange", small_sent, name=key + "_xwait")
            _, R, Cn = recv.shape
            res = reduce_adamw(recv, own, True, me, w3.reshape(n_layers * R, Cn), m3.reshape(n_layers * R, Cn),
                               v3.reshape(n_layers * R, Cn), layer=layer, n_layers=n_layers, into=res,
                               name=key + "_adamw")
        res = [r.reshape(w3.shape) for r in res]
        return tuple(jnp.swapaxes(r, -1, -2) for r in res) if name in TRANSPOSED else tuple(res)

    for name in ("ffn_w_out", "ffn_w_in"):
        out[name] = finish(name, 2)
    for name in ("att_w_o", "att_w_qkv", "mix_w_out", "s5_glu_w", "mix_w_in"):
        out[name] = finish(name, 1)

    small_own, small_recv = copies_wait(small_handle, "gather", out["s5_glu_w"][0], name="small_xwait")
    res = reduce_adamw(small_recv, small_own, False, me, w_pack, m_pack, v_pack, name="small_adamw")
    flat = [r.reshape(-1) for r in res]
    off = 0
    for n, sz in zip(SMALL, sizes):
        out[n] = tuple(f[off:off + sz].reshape(W[n].shape) for f in flat)
        off += sz
    conv_res = [f[off:off + n_conv].reshape(2, 3, N_DEV * n_cw) for f in flat]
    out["ffn_conv_w"] = tuple(lax.dynamic_slice(c, (0, 0, me * n_cw), (2, 3, n_cw)) for c in conv_res)
    off += n_conv
    loss_total = flat[0][off]

    result = [loss_total, grad_x[None]]
    for k in range(4):
        result += [out[n][k] for n in ORDER]
    return tuple(result)
```

```python
import math

import jax
import jax.numpy as jnp
from jax import lax
from jax.experimental import pallas as pl
from jax.experimental.pallas import tpu as pltpu

F32 = jnp.float32
BF16 = jnp.bfloat16
MESH_ID = pl.DeviceIdType.MESH
N_DEV = 8
VMEM_LIMIT_BYTES = 56 * 1024 * 1024

NORM_EPS = 1e-6
S5_WIDTH, S5_GROUP, S5_GROUPS, S5_STATE = 512, 16, 32, 64
HG_HEADS, HG_DIM, HG_CHUNK = 4, 128, 64
HG_STEP_CHUNKS = 4
ATT_E, ATT_HPG, ATT_BLOCK = 64, 8, 128
ATT_DILATIONS = (1, 4, 16)
ROT_DIM, ROPE_THETA = 16, 500000.0
D_FF = 2816
ADAM_LR, ADAM_B1, ADAM_B2, ADAM_EPS, ADAM_WD, ADAM_STEP = 0.001, 0.9, 0.999, 1e-08, 0.01, 10
NEG_BIG = -1e30


def _params(**kw):
    return pltpu.CompilerParams(vmem_limit_bytes=VMEM_LIMIT_BYTES, **kw)


def _pick(n, cands):
    for c in cands:
        if n % c == 0:
            return c
    return n


def _dot(a, b):
    return jnp.dot(a.astype(BF16), b.astype(BF16), preferred_element_type=F32)


def _dot_nt(a, b):
    return lax.dot_general(a.astype(BF16), b.astype(BF16), (((1,), (1,)), ((), ())), preferred_element_type=F32)


def _dot_tn(a, b):
    return lax.dot_general(a.astype(BF16), b.astype(BF16), (((0,), (0,)), ((), ())), preferred_element_type=F32)


def _split2(x):
    hi = x.astype(BF16)
    return hi, (x - hi.astype(F32)).astype(BF16)


def _dot_x3(a, b, contract=((1,), (0,))):
    dn = (contract, ((), ()))
    a1, a2 = _split2(a)
    b1, b2 = _split2(b)
    return (lax.dot_general(a1, b1, dn, preferred_element_type=F32) + lax.dot_general(a1, b2, dn, preferred_element_type=F32)
            + lax.dot_general(a2, b1, dn, preferred_element_type=F32))


def _sigmoid(x):
    return 1.0 / (1.0 + jnp.exp(-x))


V7X_HBM_BYTES_PER_S = 3.2e12
V7X_MXU_FLOPS_PER_S = 0.7e15
GRID_STEP_S = 0.35e-6
MM_VMEM_BUDGET = 40 * 1024 * 1024


def _divisors(n, cands):
    return [c for c in cands if c <= n and n % c == 0] or [n]


def _mm_tiles(m, n, k, sa, sb, so, sr):
    best = None
    for tm in _divisors(m, (2816, 2048, 1408, 1024, 512, 256, 128)):
        for tn in _divisors(n, (2816, 2048, 1408, 1024, 512, 256, 128)):
            for tk in _divisors(k, (k, 2816, 2560, 2304, 2048, 1536, 1408, 1280, 1024, 512, 256, 128)):
                nk = k // tk
                vmem = 2 * (tm * tk * sa + tk * tn * sb + tm * tn * (so + sr)) + (tm * tn * 4 if nk > 1 else 0)
                vmem += tm * tk * 2 * (sa > 2) + tk * tn * 2 * (sb > 2) + tm * tn * 4
                if vmem > MM_VMEM_BUDGET:
                    continue
                ni, nj = m // tm, n // tn
                for i_outer in (True, False):
                    if i_outer:
                        a_reads = 1 if nk == 1 else nj
                        b_reads = 1 if (nk == 1 and nj == 1) else ni
                    else:
                        b_reads = 1 if nk == 1 else ni
                        a_reads = 1 if (nk == 1 and ni == 1) else nj
                    traffic = a_reads * m * k * sa + b_reads * k * n * sb + m * n * (so + sr)
                    t = max(traffic / V7X_HBM_BYTES_PER_S, 2.0 * m * n * k / V7X_MXU_FLOPS_PER_S)
                    t += ni * nj * nk * GRID_STEP_S
                    t += (tm * tk * sa + tk * tn * sb + tm * tn * so) / V7X_HBM_BYTES_PER_S
                    if best is None or t < best[0]:
                        best = (t, tm, tn, tk, i_outer)
    assert best is not None, (m, n, k)
    return best[1:]


def mm(a, b, *, ta=False, tb=False, res=None, out_dtype=F32, dep=None, out_rows=None, out_off=0, into=None, name):
    m, k = (a.shape[1], a.shape[0]) if ta else a.shape
    n = b.shape[0] if tb else b.shape[1]
    assert (b.shape[1] if tb else b.shape[0]) == k
    has_res = res is not None
    tm, tn, tk, i_outer = _mm_tiles(m, n, k, a.dtype.itemsize, b.dtype.itemsize, jnp.dtype(out_dtype).itemsize,
                                    res.dtype.itemsize if has_res else 0)
    nk = k // tk
    deps = [] if dep is None else [dep]
    dn = (((0 if ta else 1,), (1 if tb else 0,)), ((), ()))

    def body_single(*refs):
        a_ref, b_ref = refs[:2]
        o_ref = refs[-1]
        out = lax.dot_general(a_ref[...].astype(BF16), b_ref[...].astype(BF16), dn, preferred_element_type=F32)
        if has_res:
            out = out + refs[2][...].astype(F32)
        o_ref[...] = out.astype(o_ref.dtype)

    def body(*refs):
        a_ref, b_ref = refs[:2]
        r_ref = refs[2] if has_res else None
        o_ref, acc_ref = refs[-2:]
        kk = pl.program_id(2)
        part = lax.dot_general(a_ref[...].astype(BF16), b_ref[...].astype(BF16), dn, preferred_element_type=F32)

        @pl.when(kk == 0)
        def _():
            acc_ref[...] = part

        @pl.when(kk > 0)
        def _():
            acc_ref[...] += part

        @pl.when(kk == nk - 1)
        def _():
            out = acc_ref[...]
            if has_res:
                out = out + r_ref[...].astype(F32)
            o_ref[...] = out.astype(o_ref.dtype)

    def ij(f):
        return (lambda g0, g1, q: f(g0, g1, q)) if i_outer else (lambda g0, g1, q: f(g1, g0, q))

    a_spec = pl.BlockSpec((tk, tm), ij(lambda i, j, q: (q, i))) if ta else pl.BlockSpec((tm, tk), ij(lambda i, j, q: (i, q)))
    b_spec = pl.BlockSpec((tn, tk), ij(lambda i, j, q: (j, q))) if tb else pl.BlockSpec((tk, tn), ij(lambda i, j, q: (q, j)))
    assert out_off % tm == 0
    off = out_off // tm
    r_spec = pl.BlockSpec((tm, tn), ij(lambda i, j, q: (i, j)))
    o_spec = pl.BlockSpec((tm, tn), ij(lambda i, j, q: (i + off, j)))
    rest = [] if into is None else [into]
    in_specs = ([a_spec, b_spec] + ([r_spec] if has_res else []) + [pl.BlockSpec((8, 128), lambda g0, g1, q: (0, 0))] * len(deps)
                + [pl.BlockSpec(memory_space=pl.ANY)] * len(rest))
    args = (a, b) + ((res,) if has_res else ()) + tuple(deps) + tuple(rest)
    grid = (m // tm, n // tn, nk) if i_outer else (n // tn, m // tm, nk)
    return pl.pallas_call(
        body_single if nk == 1 else body, grid=grid, in_specs=in_specs, out_specs=o_spec,
        out_shape=jax.ShapeDtypeStruct((out_rows or m, n), out_dtype),
        input_output_aliases={len(args) - 1: 0} if rest else {},
        scratch_shapes=[] if nk == 1 else [pltpu.VMEM((tm, tn), F32)],
        compiler_params=_params(dimension_semantics=("parallel", "parallel", "arbitrary")), name=name,
    )(*args)


def mm_drms(dy_in, w, x, g, dres, *, dep=None, name):
    halves = dy_in if isinstance(dy_in, (tuple, list)) else (dy_in,)
    m, kh = halves[0].shape
    k = kh * len(halves)
    D = w.shape[1]
    tm = _pick(m, (1024, 512, 256, 128))
    tk = max(_divisors(kh, (1536, 1408, 1280, 1024, 512, 256, 128)))
    nk, nh = k // tk, kh // tk
    deps = [] if dep is None else [dep]

    def body(*refs):
        a_refs, (b_ref, x_ref, g_ref, dres_ref) = refs[:len(halves)], refs[len(halves):len(halves) + 4]
        dx_ref, dg_ref, acc_ref = refs[-3:]
        i, q = pl.program_id(0), pl.program_id(1)
        a = a_refs[0][...] if len(halves) == 1 else jnp.where(q < nh, a_refs[0][...], a_refs[1][...])
        part = jnp.dot(a, b_ref[...], preferred_element_type=F32)

        @pl.when(q == 0)
        def _():
            acc_ref[...] = part

        @pl.when(q > 0)
        def _():
            acc_ref[...] += part

        @pl.when((i == 0) & (q == 0))
        def _():
            dg_ref[...] = jnp.zeros_like(dg_ref)

        @pl.when(q == nk - 1)
        def _():
            dyv = acc_ref[...]
            xv = x_ref[...]
            r = lax.rsqrt(jnp.mean(xv * xv, axis=-1, keepdims=True) + NORM_EPS)
            xh = xv * r
            dg_ref[...] += jnp.sum(dyv * xh, axis=0, keepdims=True)
            dxh = dyv * g_ref[...]
            dx_ref[...] = dres_ref[...] + r * (dxh - xh * jnp.mean(dxh * xh, axis=-1, keepdims=True))

    row = pl.BlockSpec((tm, D), lambda i, q: (i, 0))
    vec = pl.BlockSpec((1, D), lambda i, q: (0, 0))
    a_specs = [pl.BlockSpec((tm, tk), lambda i, q, h=h: (i, jnp.clip(q - h * nh, 0, nh - 1))) for h in range(len(halves))]
    in_specs = a_specs + [pl.BlockSpec((tk, D), lambda i, q: (q, 0)), row, vec, row]
    in_specs += [pl.BlockSpec((8, 128), lambda i, q: (0, 0))] * len(deps)
    return pl.pallas_call(
        body, grid=(m // tm, nk), in_specs=in_specs, out_specs=[row, vec],
        out_shape=[jax.ShapeDtypeStruct((m, D), F32), jax.ShapeDtypeStruct((1, D), F32)],
        scratch_shapes=[pltpu.VMEM((tm, D), F32)],
        compiler_params=_params(dimension_semantics=("arbitrary", "arbitrary")), name=name,
    )(*halves, w, x, g.reshape(1, D), dres, *deps)


def mm_final_loss(act, w, h_res, g, target, *, name):
    L, K = act.shape
    D = w.shape[1]
    tm = _pick(L, (1024, 512, 256, 128))
    tk = max(_divisors(K, (1536, 1408, 1280, 1024, 512, 256, 128)))
    nk = K // tk

    def body(a_ref, b_ref, r_ref, g_ref, t_ref, loss_ref, dx_ref, dg_ref, acc_ref):
        i, q = pl.program_id(0), pl.program_id(1)
        part = jnp.dot(a_ref[...], b_ref[...], preferred_element_type=F32)

        @pl.when(q == 0)
        def _():
            acc_ref[...] = part

        @pl.when(q > 0)
        def _():
            acc_ref[...] += part

        @pl.when((i == 0) & (q == 0))
        def _():
            dg_ref[...] = jnp.zeros_like(dg_ref)
            loss_ref[...] = jnp.zeros_like(loss_ref)

        @pl.when(q == nk - 1)
        def _():
            xv = acc_ref[...] + r_ref[...]
            gv = g_ref[...]
            r = lax.rsqrt(jnp.mean(xv * xv, axis=-1, keepdims=True) + NORM_EPS)
            xh = xv * r
            err = xh * gv - t_ref[...]
            loss_ref[...] += 0.5 * jnp.sum(jnp.mean(err * err, axis=-1, keepdims=True), axis=0, keepdims=True)
            dyv = err * (1.0 / D)
            dg_ref[...] += jnp.sum(dyv * xh, axis=0, keepdims=True)
            dxh = dyv * gv
            dx_ref[...] = r * (dxh - xh * jnp.mean(dxh * xh, axis=-1, keepdims=True))

    row = pl.BlockSpec((tm, D), lambda i, q: (i, 0))
    vec = pl.BlockSpec((1, D), lambda i, q: (0, 0))
    one = pl.BlockSpec((1, 1), lambda i, q: (0, 0))
    return pl.pallas_call(
        body, grid=(L // tm, nk),
        in_specs=[pl.BlockSpec((tm, tk), lambda i, q: (i, q)), pl.BlockSpec((tk, D), lambda i, q: (q, 0)), row, vec, row],
        out_specs=[one, row, vec],
        out_shape=[jax.ShapeDtypeStruct((1, 1), F32), jax.ShapeDtypeStruct((L, D), F32), jax.ShapeDtypeStruct((1, D), F32)],
        scratch_shapes=[pltpu.VMEM((tm, D), F32)],
        compiler_params=_params(dimension_semantics=("arbitrary", "arbitrary")), name=name,
    )(act, w, h_res, g.reshape(1, D), target)


def _cmul(ar, ai, br, bi):
    return ar * br - ai * bi, ar * bi + ai * br


def _powers(ar, ai):
    rows = [(ar, ai)]
    for _ in range(7):
        rows.append(_cmul(rows[-1][0], rows[-1][1], ar, ai))
    table = (jnp.concatenate([r[0] for r in rows], axis=0), jnp.concatenate([r[1] for r in rows], axis=0))
    return (rows[0], rows[1], rows[3]), table


def _block_scan(br, bi, steps, shift):
    yr, yi = br, bi
    for s, (pr, pi) in zip((1, 2, 4), steps):
        sr, si = shift(yr, s), shift(yi, s)
        yr, yi = yr + pr * sr - pi * si, yi + pr * si + pi * sr
    return yr, yi


def s5_core_fwd(proj, a_re, a_im, wb_re, wb_im, wc_re, wc_im, *, name):
    L = proj.shape[0]
    parts, cu, W = wb_re.shape

    def body(u_ref, ar_ref, ai_ref, wbr_ref, wbi_ref, wcr_ref, wci_ref, y_ref, xr_ref, xi_ref, br_ref, bi_ref):
        u = u_ref[...]
        br_ref[...] = _dot(u, wbr_ref[...])
        bi_ref[...] = _dot(u, wbi_ref[...])
        steps, (tr, ti) = _powers(ar_ref[...], ai_ref[...])
        row = lax.broadcasted_iota(jnp.int32, (8, W), 0)

        def shift(y, s):
            return jnp.where(row >= s, pltpu.roll(y, s, 0), 0.0)

        def step(t8, carry):
            cr, ci = carry
            base = pl.multiple_of(t8 * 8, 8)
            yr, yi = _block_scan(br_ref[pl.ds(base, 8), :], bi_ref[pl.ds(base, 8), :], steps, shift)
            xr = yr + tr * cr - ti * ci
            xi = yi + tr * ci + ti * cr
            xr_ref[pl.ds(base, 8), :] = xr
            xi_ref[pl.ds(base, 8), :] = xi
            return jnp.broadcast_to(xr[7:8, :], (8, W)), jnp.broadcast_to(xi[7:8, :], (8, W))

        zero = jnp.zeros((8, W), F32)
        lax.fori_loop(0, L // 8, step, (zero, zero), unroll=2)
        y_ref[...] = _dot(xr_ref[...], wcr_ref[...]) + _dot(xi_ref[...], wci_ref[...])

    ucol = pl.BlockSpec((L, cu), lambda t: (0, t))
    vec = pl.BlockSpec((1, W), lambda t: (0, t))
    col = pl.BlockSpec((L, W), lambda t: (0, t))
    wb = pl.BlockSpec((None, cu, W), lambda t: (t, 0, 0))
    wc = pl.BlockSpec((None, W, cu), lambda t: (t, 0, 0))
    return pl.pallas_call(body, grid=(parts,), in_specs=[ucol, vec, vec, wb, wb, wc, wc], out_specs=[ucol, col, col],
                          out_shape=[jax.ShapeDtypeStruct((L, parts * cu), F32)]
                          + [jax.ShapeDtypeStruct((L, parts * W), F32)] * 2,
                          scratch_shapes=[pltpu.VMEM((L, W), F32)] * 2,
                          compiler_params=_params(dimension_semantics=("parallel",)), name=name)(
        proj, a_re, a_im, wb_re, wb_im, wc_re, wc_im)


def s5_core_bwd(dy, du_d, proj, xs_re, xs_im, a_re, a_im, wb_re, wb_im, wc_re, wc_im, *, name):
    L = proj.shape[0]
    parts, cu, W = wb_re.shape

    def body(dy_ref, dud_ref, u_ref, xr_ref, xi_ref, ar_ref, ai_ref, wbr_ref, wbi_ref, wcr_ref, wci_ref,
             du_ref, dwbr_ref, dwbi_ref, dwcr_ref, dwci_ref, dar_ref, dai_ref, lr_ref, li_ref):
        dy = dy_ref[...]
        lr_ref[...] = _dot_nt(dy, wcr_ref[...])
        li_ref[...] = _dot_nt(dy, wci_ref[...])
        dwcr_ref[...] = _dot_tn(xr_ref[...], dy)
        dwci_ref[...] = _dot_tn(xi_ref[...], dy)
        ar, ai = ar_ref[...], -ai_ref[...]
        steps, (tr, ti) = _powers(ar, ai)
        tr = jnp.concatenate([tr[j:j + 1, :] for j in range(7, -1, -1)], axis=0)
        ti = jnp.concatenate([ti[j:j + 1, :] for j in range(7, -1, -1)], axis=0)
        row8 = lax.broadcasted_iota(jnp.int32, (8, W), 0)
        nblk = L // 8

        def shift(y, s):
            return jnp.where(row8 < 8 - s, pltpu.roll(y, 8 - s, 0), 0.0)

        def step(s, carry):
            cr, ci = carry
            base = pl.multiple_of((nblk - 1 - s) * 8, 8)
            yr, yi = _block_scan(lr_ref[pl.ds(base, 8), :], li_ref[pl.ds(base, 8), :], steps, shift)
            lr = yr + tr * cr - ti * ci
            li = yi + tr * ci + ti * cr
            lr_ref[pl.ds(base, 8), :] = lr
            li_ref[pl.ds(base, 8), :] = li
            return jnp.broadcast_to(lr[0:1, :], (8, W)), jnp.broadcast_to(li[0:1, :], (8, W))

        zero = jnp.zeros((8, W), F32)
        lax.fori_loop(0, nblk, step, (zero, zero), unroll=2)
        row = lax.broadcasted_iota(jnp.int32, (L, W), 0)
        xpr = jnp.where(row >= 1, pltpu.roll(xr_ref[...], 1, 0), 0.0)
        xpi = jnp.where(row >= 1, pltpu.roll(xi_ref[...], 1, 0), 0.0)
        lr, li = lr_ref[...], li_ref[...]
        dar_ref[...] = jnp.sum(lr * xpr + li * xpi, axis=0, keepdims=True)
        dai_ref[...] = jnp.sum(li * xpr - lr * xpi, axis=0, keepdims=True)
        u = u_ref[...]
        dwbr_ref[...] = _dot_tn(u, lr)
        dwbi_ref[...] = _dot_tn(u, li)
        du_ref[...] = (dud_ref[...] + _dot_nt(lr, wbr_ref[...]) + _dot_nt(li, wbi_ref[...])).astype(du_ref.dtype)

    ucol = pl.BlockSpec((L, cu), lambda t: (0, t))
    vec = pl.BlockSpec((1, W), lambda t: (0, t))
    col = pl.BlockSpec((L, W), lambda t: (0, t))
    wb = pl.BlockSpec((None, cu, W), lambda t: (t, 0, 0))
    wc = pl.BlockSpec((None, W, cu), lambda t: (t, 0, 0))
    return pl.pallas_call(
        body, grid=(parts,), in_specs=[ucol, ucol, ucol, col, col, vec, vec, wb, wb, wc, wc],
        out_specs=[ucol, wb, wb, wc, wc, vec, vec],
        out_shape=[jax.ShapeDtypeStruct((L, parts * cu), BF16)] + [jax.ShapeDtypeStruct((parts, cu, W), F32)] * 2
        + [jax.ShapeDtypeStruct((parts, W, cu), F32)] * 2 + [jax.ShapeDtypeStruct((1, parts * W), F32)] * 2,
        scratch_shapes=[pltpu.VMEM((L, W), F32)] * 2,
        compiler_params=_params(dimension_semantics=("parallel",)), name=name,
    )(dy, du_d, proj, xs_re, xs_im, a_re, a_im, wb_re, wb_im, wc_re, wc_im)


def _gelu(y):
    c = math.sqrt(2.0 / math.pi)
    t = jnp.tanh(c * (y + 0.044715 * y * y * y))
    return 0.5 * y * (1.0 + t), t


def s5_out_fwd(y0, proj, dvec, glu_w, glu_b, *, name):
    L, C = y0.shape
    tr = _pick(L, (256, 128))

    def body(y_ref, u_ref, d_ref, w_ref, b_ref, o_ref):
        z, _ = _gelu(y_ref[...] + d_ref[...] * u_ref[...])
        zg = _dot(z, w_ref[...]) + b_ref[...]
        o_ref[...] = (z * _sigmoid(zg)).astype(o_ref.dtype)

    row = pl.BlockSpec((tr, C), lambda i: (i, 0))
    vec = pl.BlockSpec((1, C), lambda i: (0, 0))
    wsp = pl.BlockSpec((C, C), lambda i: (0, 0))
    return pl.pallas_call(body, grid=(L // tr,), in_specs=[row, row, vec, wsp, vec], out_specs=row,
                          out_shape=jax.ShapeDtypeStruct((L, 2 * C), BF16), name=name)(
        y0, proj, dvec, glu_w, glu_b)


def s5_out_bwd(y0, proj, dvec, glu_w, glu_b, dcat, *, name):
    L, C = y0.shape
    tr = _pick(L, (256, 128))

    def body(y_ref, u_ref, d_ref, w_ref, b_ref, do_ref, dy_ref, dud_ref, z_ref, dzg_ref, db_ref, dd_ref):
        u = u_ref[...]
        y = y_ref[...] + d_ref[...] * u
        z, t = _gelu(y)
        zg = _dot(z, w_ref[...]) + b_ref[...]
        s = _sigmoid(zg)
        do = do_ref[...]
        dzg = do * z * s * (1.0 - s)
        dz = do * s + _dot_nt(dzg, w_ref[...])
        c = math.sqrt(2.0 / math.pi)
        dgelu = 0.5 * (1.0 + t) + 0.5 * y * (1.0 - t * t) * c * (1.0 + 3.0 * 0.044715 * y * y)
        dy = dz * dgelu

        @pl.when(pl.program_id(0) == 0)
        def _():
            db_ref[...] = jnp.zeros_like(db_ref)
            dd_ref[...] = jnp.zeros_like(dd_ref)

        db_ref[...] += jnp.sum(dzg, axis=0, keepdims=True)
        dd_ref[...] += jnp.sum(dy * u, axis=0, keepdims=True)
        dy_ref[...] = dy
        dud_ref[...] = dy * d_ref[...]
        z_ref[...] = z.astype(BF16)
        dzg_ref[...] = dzg.astype(BF16)

    row = pl.BlockSpec((tr, C), lambda i: (i, 0))
    vec = pl.BlockSpec((1, C), lambda i: (0, 0))
    wsp = pl.BlockSpec((C, C), lambda i: (0, 0))
    return pl.pallas_call(body, grid=(L // tr,), in_specs=[row, row, vec, wsp, vec, row],
                          out_specs=[row, row, row, row, vec, vec],
                          out_shape=[jax.ShapeDtypeStruct((L, C), F32), jax.ShapeDtypeStruct((L, C), F32),
                                     jax.ShapeDtypeStruct((L, C), BF16), jax.ShapeDtypeStruct((L, C), BF16),
                                     jax.ShapeDtypeStruct((1, C), F32), jax.ShapeDtypeStruct((1, C), F32)],
                          compiler_params=_params(dimension_semantics=("arbitrary",)), name=name)(
        y0, proj, dvec, glu_w, glu_b, dcat)


def _dot_tri(tri, x, tri_left=True):
    t = tri.astype(BF16)
    x1 = x.astype(BF16)
    r1 = x - x1.astype(F32)
    x2 = r1.astype(BF16)
    x3 = (r1 - x2.astype(F32)).astype(BF16)
    dot = (lambda p: jnp.dot(t, p, preferred_element_type=F32)) if tri_left else (
        lambda p: jnp.dot(p, t, preferred_element_type=F32))
    return dot(x1) + dot(x2) + dot(x3)


def _hg_gates(xq, xf, lb, tri):
    C = xq.shape[0]
    sq = _sigmoid(xq)
    q = xq * sq
    sg = _sigmoid(xf)
    f = lb + (1.0 - lb) * sg
    kk = 1.0 - f
    b = _dot_tri(tri, jnp.log(f))
    bm = b[C // 2 - 1:C // 2, :]
    bl = b[C - 1:C, :]
    eb = jnp.exp(b)
    eqm, ekm, ekl = jnp.exp(b - bm), jnp.exp(bm - b), jnp.exp(bl - b)
    return dict(sq=sq, q=q, sg=sg, f=f, kk=kk, eb=eb, ebl=jnp.exp(bl), eqm=eqm, ekm=ekm, ekl=ekl,
                qb=q * eb, qt=q * eqm, kt=kk * ekm, kh=kk * ekl)


def _tri(C, lower):
    r = lax.broadcasted_iota(jnp.int32, (C, C), 0)
    c = lax.broadcasted_iota(jnp.int32, (C, C), 1)
    return (r >= c) if lower else (c >= r)


def hgrn_fwd(proj, lb, norm_g, cat, *, name):
    L = proj.shape[0]
    C, H, K = HG_CHUNK, HG_HEADS, HG_DIM
    HK = H * K
    nc = L // C

    def body(q_ref, f_ref, i_ref, g_ref, lb_ref, ng_ref, cat_ref, o_ref, sall_ref, st_ref):
        @pl.when(pl.program_id(0) == 0)
        def _():
            st_ref[...] = jnp.zeros_like(st_ref)

        mask = _tri(C, True)
        sts = [st_ref[h] for h in range(H)]
        for s in range(S):
            rs = slice(s * C, (s + 1) * C)
            gt = _hg_gates(q_ref[rs, :], f_ref[rs, :], lb_ref[...], mask.astype(F32))
            v_all = i_ref[rs, :]
            outs = []
            for h in range(H):
                sl = slice(h * K, (h + 1) * K)
                v, st = v_all[:, sl], sts[h]
                sall_ref[s, h] = st
                att = jnp.where(mask, _dot_nt(gt["qt"][:, sl], gt["kt"][:, sl]), 0.0)
                o = _dot(att, v) + _dot_nt(gt["qb"][:, sl], st)
                sts[h] = st * gt["ebl"][:, sl] + _dot_tn(v, gt["kh"][:, sl])
                outs.append(o * lax.rsqrt(jnp.mean(o * o, axis=-1, keepdims=True) + NORM_EPS))
            xg = g_ref[rs, :]
            o_ref[rs, :] = (jnp.concatenate(outs, axis=1) * ng_ref[...] * (xg * _sigmoid(xg))).astype(o_ref.dtype)
        for h in range(H):
            st_ref[h] = sts[h]

    S = HG_STEP_CHUNKS

    def blk(cb):
        return pl.BlockSpec((S * C, HK), lambda i: (i, cb))

    vec = pl.BlockSpec((1, HK), lambda i: (0, 0))
    return pl.pallas_call(
        body, grid=(nc // S,), in_specs=[blk(1), blk(2), blk(3), blk(4), vec, vec, pl.BlockSpec(memory_space=pl.ANY)],
        out_specs=[pl.BlockSpec((S * C, HK), lambda i: (i, 1)), pl.BlockSpec((S, H, K, K), lambda i: (i, 0, 0, 0))],
        out_shape=[jax.ShapeDtypeStruct((L, 2 * HK), BF16), jax.ShapeDtypeStruct((nc, H, K, K), F32)],
        input_output_aliases={6: 0},
        scratch_shapes=[pltpu.VMEM((H, K, K), F32)],
        compiler_params=_params(dimension_semantics=("arbitrary",)), name=name,
    )(proj, proj, proj, proj, lb, norm_g, cat)


def hgrn_bwd(proj, lb, norm_g, sall, dcat, du, *, name):
    L = proj.shape[0]
    C, H, K = HG_CHUNK, HG_HEADS, HG_DIM
    HK = H * K
    nc = L // C

    def body(q_ref, f_ref, i_ref, g_ref, lb_ref, ng_ref, sall_ref, do_ref, du_ref, dx_ref, dlb_ref, dng_ref, dst_ref):
        @pl.when(pl.program_id(0) == 0)
        def _():
            dst_ref[...] = jnp.zeros_like(dst_ref)
            dlb_ref[...] = jnp.zeros_like(dlb_ref)
            dng_ref[...] = jnp.zeros_like(dng_ref)

        mask = _tri(C, True)
        lb_all, ng = lb_ref[...], ng_ref[...]
        dx_ref[:, 0:HK] = du_ref[...]
        dsts = [dst_ref[h] for h in range(H)]
        for s in reversed(range(S)):
            rs = slice(s * C, (s + 1) * C)
            dsts = chunk_bwd(rs, s, dsts, mask, lb_all, ng, q_ref, f_ref, i_ref, g_ref, sall_ref, do_ref,
                             dx_ref, dlb_ref, dng_ref)
        for h in range(H):
            dst_ref[h] = dsts[h]

    def chunk_bwd(rs, s, dsts, mask, lb_all, ng, q_ref, f_ref, i_ref, g_ref, sall_ref, do_ref, dx_ref, dlb_ref, dng_ref):
        xq, xg, v_all = q_ref[rs, :], g_ref[rs, :], i_ref[rs, :]
        gt = _hg_gates(xq, f_ref[rs, :], lb_all, mask.astype(F32))
        sgg = _sigmoid(xg)
        d_ob = do_ref[rs, :]
        d_on = d_ob * (xg * sgg)
        doh = d_on * ng
        ohs, d_qts, d_qbs, d_kts, d_khs, dvs, d_bls, new_dsts = [], [], [], [], [], [], [], []
        for h in range(H):
            sl = slice(h * K, (h + 1) * K)
            v, st, dst = v_all[:, sl], sall_ref[s, h], dsts[h]
            qt, kt, kh, qb = gt["qt"][:, sl], gt["kt"][:, sl], gt["kh"][:, sl], gt["qb"][:, sl]
            att = jnp.where(mask, _dot_nt(qt, kt), 0.0)
            o = _dot(att, v) + _dot_nt(qb, st)
            r = lax.rsqrt(jnp.mean(o * o, axis=-1, keepdims=True) + NORM_EPS)
            oh = o * r
            do = r * (doh[:, sl] - oh * jnp.mean(doh[:, sl] * oh, axis=-1, keepdims=True))
            datt = jnp.where(mask, _dot_nt(do, v), 0.0)
            dvs.append(_dot_tn(att, do) + _dot_nt(kh, dst))
            d_qbs.append(_dot_x3(do, st))
            d_qts.append(_dot_x3(datt, kt))
            d_kts.append(_dot_x3(datt, qt, ((0,), (0,))))
            d_kh = _dot_x3(v, dst)
            d_khs.append(d_kh)
            d_bls.append(jnp.sum(dst * st, axis=0, keepdims=True) * gt["ebl"][:, sl]
                         + jnp.sum(d_kh * kh, axis=0, keepdims=True))
            new_dsts.append(dst * gt["ebl"][:, sl] + _dot_tn(do, qb))
            ohs.append(oh)
        oh, d_qt, d_qb, d_kt, d_kh, dv, d_bl = (jnp.concatenate(p, axis=1) for p in
                                                (ohs, d_qts, d_qbs, d_kts, d_khs, dvs, d_bls))
        dxg = d_ob * (oh * ng) * (sgg * (1.0 + xg * (1.0 - sgg)))
        dng_ref[...] += jnp.sum(d_on * oh, axis=0, keepdims=True)
        dq = d_qt * gt["eqm"] + d_qb * gt["eb"]
        db = d_qt * gt["qt"] + d_qb * gt["qb"] - d_kt * gt["kt"] - d_kh * gt["kh"]
        rowi = lax.broadcasted_iota(jnp.int32, (C, HK), 0)
        db = db + jnp.where(rowi == C - 1, d_bl, 0.0)
        dkk = d_kt * gt["ekm"] + d_kh * gt["ekl"]
        dlg = _dot_tri(_tri(C, False).astype(F32), db)
        df = dlg / gt["f"] - dkk
        sg, sq = gt["sg"], gt["sq"]
        dlb_ref[...] += jnp.sum(df * (1.0 - sg), axis=0, keepdims=True)
        dx_ref[rs, HK:2 * HK] = (dq * (sq * (1.0 + xq * (1.0 - sq)))).astype(dx_ref.dtype)
        dx_ref[rs, 2 * HK:3 * HK] = (df * (1.0 - lb_all) * sg * (1.0 - sg)).astype(dx_ref.dtype)
        dx_ref[rs, 3 * HK:4 * HK] = dv.astype(dx_ref.dtype)
        dx_ref[rs, 4 * HK:5 * HK] = dxg.astype(dx_ref.dtype)
        return new_dsts

    S = HG_STEP_CHUNKS
    ns = nc // S

    def blk(cb):
        return pl.BlockSpec((S * C, HK), lambda i: (ns - 1 - i, cb))

    vec = pl.BlockSpec((1, HK), lambda i: (0, 0))
    return pl.pallas_call(
        body, grid=(ns,),
        in_specs=[blk(1), blk(2), blk(3), blk(4), vec, vec,
                  pl.BlockSpec((S, H, K, K), lambda i: (ns - 1 - i, 0, 0, 0)), blk(1), blk(0)],
        out_specs=[pl.BlockSpec((S * C, 5 * HK), lambda i: (ns - 1 - i, 0)), vec, vec],
        out_shape=[jax.ShapeDtypeStruct((L, 5 * HK), BF16), jax.ShapeDtypeStruct((1, HK), F32),
                   jax.ShapeDtypeStruct((1, HK), F32)],
        scratch_shapes=[pltpu.VMEM((H, K, K), F32)],
        compiler_params=_params(dimension_semantics=("arbitrary",)), name=name,
    )(proj, proj, proj, proj, lb, norm_g, sall, dcat, du)


def _shift_down(x, k, row):
    return jnp.where(row >= k, pltpu.roll(x, k, 0), 0.0)


def _shift_up(x, k, row):
    n = x.shape[0]
    return jnp.where(row < n - k, pltpu.roll(x, n - k, 0), 0.0)


def convgate_fwd(hu, conv_w, conv_b, *, name):
    L, C2 = hu.shape
    C = C2 // 2
    tc = _pick(C, (256, 128))
    nb = C // tc

    def body(a_ref, b_ref, wa_ref, wb_ref, ba_ref, bb_ref, o_ref):
        row = lax.broadcasted_iota(jnp.int32, (L, tc), 0)

        def conv(x, w, bias):
            return w[2:3, :] * x + w[1:2, :] * _shift_down(x, 1, row) + w[0:1, :] * _shift_down(x, 2, row) + bias

        ca = conv(a_ref[...], wa_ref[...], ba_ref[...])
        cb = conv(b_ref[...], wb_ref[...], bb_ref[...])
        o_ref[...] = (ca * _sigmoid(ca) * cb).astype(o_ref.dtype)

    def col(off, rows):
        return pl.BlockSpec((rows, tc), lambda j: (0, j + off))

    return pl.pallas_call(
        body, grid=(nb,), in_specs=[col(0, L), col(nb, L), col(0, 3), col(nb, 3), col(0, 1), col(nb, 1)],
        out_specs=col(0, L), out_shape=jax.ShapeDtypeStruct((L, C), BF16),
        compiler_params=_params(dimension_semantics=("parallel",)), name=name,
    )(hu, hu, conv_w, conv_w, conv_b, conv_b)


def convgate_bwd(hu, conv_w, conv_b, dact, *, name):
    L, C2 = hu.shape
    C = C2 // 2
    tc = _pick(C, (256, 128))
    nb = C // tc

    def body(a_ref, b_ref, wa_ref, wb_ref, ba_ref, bb_ref, d_ref, dxa_ref, dxb_ref, dwa_ref, dwb_ref, dba_ref, dbb_ref):
        row = lax.broadcasted_iota(jnp.int32, (L, tc), 0)

        def conv(x, w, bias):
            x1 = _shift_down(x, 1, row)
            x2 = _shift_down(x, 2, row)
            return w[2:3, :] * x + w[1:2, :] * x1 + w[0:1, :] * x2 + bias, x1, x2

        xa, xb = a_ref[...], b_ref[...]
        wa, wb = wa_ref[...], wb_ref[...]
        ca, xa1, xa2 = conv(xa, wa, ba_ref[...])
        cb, xb1, xb2 = conv(xb, wb, bb_ref[...])
        d = d_ref[...]
        sa = _sigmoid(ca)
        dca = d * cb * (sa * (1.0 + ca * (1.0 - sa)))
        dcb = d * (ca * sa)

        def back(dc, w, x, x1, x2, dx_ref, dw_ref, db_ref):
            dx = w[2:3, :] * dc + w[1:2, :] * _shift_up(dc, 1, row) + w[0:1, :] * _shift_up(dc, 2, row)
            dx_ref[...] = dx.astype(dx_ref.dtype)
            dw_ref[...] = jnp.concatenate([jnp.sum(dc * x2, axis=0, keepdims=True),
                                           jnp.sum(dc * x1, axis=0, keepdims=True),
                                           jnp.sum(dc * x, axis=0, keepdims=True)], axis=0)
            db_ref[...] = jnp.sum(dc, axis=0, keepdims=True)

        back(dca, wa, xa, xa1, xa2, dxa_ref, dwa_ref, dba_ref)
        back(dcb, wb, xb, xb1, xb2, dxb_ref, dwb_ref, dbb_ref)

    def col(off, rows):
        return pl.BlockSpec((rows, tc), lambda j: (0, j + off))

    outs = pl.pallas_call(
        body, grid=(nb,),
        in_specs=[col(0, L), col(nb, L), col(0, 3), col(nb, 3), col(0, 1), col(nb, 1), col(0, L)],
        out_specs=[col(0, L), col(0, L), col(0, 3), col(0, 3), col(0, 1), col(0, 1)],
        out_shape=[jax.ShapeDtypeStruct((L, C), BF16)] * 2 + [jax.ShapeDtypeStruct((3, C), F32)] * 2
        + [jax.ShapeDtypeStruct((1, C), F32)] * 2,
        compiler_params=_params(dimension_semantics=("parallel",)), name=name,
    )(hu, hu, conv_w, conv_w, conv_b, conv_b, dact)
    dxa, dxb, dwa, dwb, dba, dbb = outs
    return (dxa, dxb), jnp.concatenate([dwa, dwb], axis=1), jnp.concatenate([dba, dbb], axis=1)


def rope_tables(positions):
    half = ROT_DIM // 2
    inv_freq = ROPE_THETA ** (-jnp.arange(half, dtype=F32) * 2.0 / ROT_DIM)
    ang = positions.astype(F32)[:, None] * inv_freq
    cos, sin = jnp.cos(ang), jnp.sin(ang)
    L = positions.shape[0]
    one = jnp.ones((L, ATT_E - ROT_DIM), F32)
    zero = jnp.zeros((L, ATT_E - ROT_DIM), F32)
    zh = jnp.zeros((L, half), F32)
    tc = jnp.concatenate([cos, cos, one], axis=1)
    ts1 = jnp.concatenate([zh, sin, zero], axis=1)
    ts2 = jnp.concatenate([-sin, zh, zero], axis=1)
    return tuple(jnp.concatenate([t, t], axis=1) for t in (tc, ts1, ts2))


def norm_mm(x, g, w_t, *, tabs=None, name):
    L, D = x.shape
    N = w_t.shape[0]
    W = 512
    tm = _pick(L, (1024, 512, 256, 128))
    nq = N // (3 * W)
    scale = ATT_E ** -0.5
    rope = tabs is not None

    def body(x_ref, g_ref, b_ref, *rest):
        hn_ref, o_ref, hn_scr = rest[-3:]
        j = pl.program_id(1)

        @pl.when(j == 0)
        def _():
            xv = x_ref[...]
            r = lax.rsqrt(jnp.mean(xv * xv, axis=-1, keepdims=True) + NORM_EPS)
            hn = (xv * r * g_ref[...]).astype(BF16)
            hn_scr[...] = hn
            hn_ref[...] = hn

        out = _dot_nt(hn_scr[...], b_ref[...])
        if rope:
            c_ref, s1_ref, s2_ref = rest[:3]
            c = jnp.concatenate([c_ref[...]] * 4, axis=1)
            s1 = jnp.concatenate([s1_ref[...]] * 4, axis=1)
            s2 = jnp.concatenate([s2_ref[...]] * 4, axis=1)
            rot = out * c + pltpu.roll(out, 8, 1) * s1 + pltpu.roll(out, W - 8, 1) * s2
            out = jnp.where(j < 2 * nq, rot * jnp.where(j < nq, scale, 1.0), out)
        o_ref[...] = out

    row = pl.BlockSpec((tm, D), lambda i, j: (i, 0))
    tab = pl.BlockSpec((tm, 128), lambda i, j: (i, 0))
    return pl.pallas_call(body, grid=(L // tm, N // W),
                          in_specs=[row, pl.BlockSpec((1, D), lambda i, j: (0, 0)), pl.BlockSpec((W, D), lambda i, j: (j, 0))]
                          + ([tab, tab, tab] if rope else []),
                          out_specs=[row, pl.BlockSpec((tm, W), lambda i, j: (i, j))],
                          out_shape=[jax.ShapeDtypeStruct((L, D), BF16), jax.ShapeDtypeStruct((L, N), F32)],
                          scratch_shapes=[pltpu.VMEM((tm, D), BF16)],
                          compiler_params=_params(dimension_semantics=("parallel", "arbitrary")), name=name)(
        x, g.reshape(1, D), w_t, *(tabs or ()))


def rope_bwd(slabs, tabs, *, name):
    L, W = slabs[0].shape
    tr = _pick(L, (256, 128))
    nq = len(slabs) // 3
    scale = ATT_E ** -0.5

    def body(*refs):
        d_refs, (c_ref, s1_ref, s2_ref, o_ref) = refs[:3 * nq], refs[3 * nq:]
        c = jnp.concatenate([c_ref[...]] * 4, axis=1)
        s1 = jnp.concatenate([s1_ref[...]] * 4, axis=1)
        s2 = jnp.concatenate([s2_ref[...]] * 4, axis=1)
        for j, d_ref in enumerate(d_refs):
            dy = d_ref[...]
            if j < 2 * nq:
                dy = dy * c + pltpu.roll(dy * s1, W - 8, 1) + pltpu.roll(dy * s2, 8, 1)
            if j < nq:
                dy = dy * scale
            o_ref[:, j * W:(j + 1) * W] = dy.astype(o_ref.dtype)

    slab = pl.BlockSpec((tr, W), lambda i: (i, 0))
    tab = pl.BlockSpec((tr, 128), lambda i: (i, 0))
    return pl.pallas_call(body, grid=(L // tr,), in_specs=[slab] * (3 * nq) + [tab, tab, tab],
                          out_specs=pl.BlockSpec((tr, 3 * nq * W), lambda i: (i, 0)),
                          out_shape=jax.ShapeDtypeStruct((L, 3 * nq * W), BF16),
                          compiler_params=_params(dimension_semantics=("parallel",)), name=name)(*slabs, *tabs)


def _att_masks(has_prev):
    qi = lax.broadcasted_iota(jnp.int32, (ATT_BLOCK, ATT_BLOCK), 0)
    kj = lax.broadcasted_iota(jnp.int32, (ATT_BLOCK, ATT_BLOCK), 1)
    return qi >= kj, (kj >= qi) & has_prev


ATT_COLS = 128


def _att_rows(j, d, nb):
    B = ATT_BLOCK
    r, n = j // nb, j % nb
    start = r + d * B * n
    has_prev = n > 0
    pstart = jnp.where(has_prev, start - d * B, start)
    if d == 1:
        return pl.ds(pl.multiple_of(start, B), B), pl.ds(pl.multiple_of(pstart, B), B), has_prev
    return pl.ds(start, B, stride=d), pl.ds(pstart, B, stride=d), has_prev


def _qkv_specs(L, g):
    per = ATT_HPG * ATT_E // ATT_COLS
    third = len(ATT_DILATIONS) * per
    return [pl.BlockSpec((L, ATT_COLS), lambda c, base=base: (0, base + c))
            for base in (g * per, third + g * per, 2 * third + g * per)]


def attn_fwd(qkv, g, d, *, name):
    L, W = qkv.shape[0], ATT_HPG * ATT_E
    B, E = ATT_BLOCK, ATT_E
    nblk = L // B
    nb = nblk // d

    def body(q_ref, k_ref, v_ref, o_ref, l_ref):
        def step(j, carry):
            cur, prv, has_prev = _att_rows(j, d, nb)
            mc, mp = _att_masks(has_prev)
            qb, kc, kp, vc, vp = q_ref[cur, :], k_ref[cur, :], k_ref[prv, :], v_ref[cur, :], v_ref[prv, :]
            outs, lses = [], []
            for h in range(ATT_COLS // E):
                sl = slice(h * E, (h + 1) * E)
                sc = jnp.where(mc, _dot_nt(qb[:, sl], kc[:, sl]), NEG_BIG)
                sp = jnp.where(mp, _dot_nt(qb[:, sl], kp[:, sl]), NEG_BIG)
                m = jnp.maximum(jnp.max(sc, axis=-1, keepdims=True), jnp.max(sp, axis=-1, keepdims=True))
                pc = jnp.exp(sc - m)
                pp = jnp.exp(sp - m)
                den = jnp.sum(pc, axis=-1, keepdims=True) + jnp.sum(pp, axis=-1, keepdims=True)
                outs.append((_dot(pc, vc[:, sl]) + _dot(pp, vp[:, sl])) / den)
                lses.append(jnp.broadcast_to(m + jnp.log(den), (B, E)))
            o_ref[cur, :] = jnp.concatenate(outs, axis=1)
            l_ref[cur, :] = jnp.concatenate(lses, axis=1)
            return carry

        lax.fori_loop(0, nblk, step, 0, unroll=4)

    col = pl.BlockSpec((L, ATT_COLS), lambda c: (0, c))
    return pl.pallas_call(body, grid=(W // ATT_COLS,), in_specs=_qkv_specs(L, g), out_specs=[col] * 2,
                          out_shape=[jax.ShapeDtypeStruct((L, W), F32)] * 2,
                          compiler_params=_params(dimension_semantics=("parallel",)), name=name)(qkv, qkv, qkv)


def attn_bwd(qkv, g, lse, do, dl, d, *, name):
    L, W = qkv.shape[0], ATT_HPG * ATT_E
    B, E = ATT_BLOCK, ATT_E
    nblk = L // B
    nb = nblk // d

    def body(q_ref, k_ref, v_ref, l_ref, do_ref, dl_ref, dq_ref, dk_ref, dv_ref):
        dk_ref[...] = jnp.zeros_like(dk_ref)
        dv_ref[...] = jnp.zeros_like(dv_ref)

        def step(j, carry):
            cur, prv, has_prev = _att_rows(j, d, nb)
            mc, mp = _att_masks(has_prev)
            qb, kc, kp, vc, vp = q_ref[cur, :], k_ref[cur, :], k_ref[prv, :], v_ref[cur, :], v_ref[prv, :]
            lb, dob, dlb = l_ref[cur, :], do_ref[cur, :], dl_ref[cur, :]
            dqs, dkc, dkp, dvc, dvp = [], [], [], [], []
            for h in range(ATT_COLS // E):
                sl = slice(h * E, (h + 1) * E)
                qh, doh = qb[:, sl], dob[:, sl]
                lse_h, dl_h = lb[:, h * E:h * E + 1], dlb[:, h * E:h * E + 1]
                pc = jnp.where(mc, jnp.exp(_dot_nt(qh, kc[:, sl]) - lse_h), 0.0)
                pp = jnp.where(mp, jnp.exp(_dot_nt(qh, kp[:, sl]) - lse_h), 0.0)
                dsc = pc * (_dot_nt(doh, vc[:, sl]) - dl_h)
                dsp = pp * (_dot_nt(doh, vp[:, sl]) - dl_h)
                dqs.append(_dot(dsc, kc[:, sl]) + _dot(dsp, kp[:, sl]))
                dkc.append(_dot_tn(dsc, qh))
                dkp.append(_dot_tn(dsp, qh))
                dvc.append(_dot_tn(pc, doh))
                dvp.append(_dot_tn(pp, doh))
            dq_ref[cur, :] = jnp.concatenate(dqs, axis=1)
            dk_ref[cur, :] = dk_ref[cur, :] + jnp.concatenate(dkc, axis=1)
            dv_ref[cur, :] = dv_ref[cur, :] + jnp.concatenate(dvc, axis=1)
            dk_ref[prv, :] = dk_ref[prv, :] + jnp.concatenate(dkp, axis=1)
            dv_ref[prv, :] = dv_ref[prv, :] + jnp.concatenate(dvp, axis=1)
            return carry

        lax.fori_loop(0, nblk, step, 0, unroll=4)

    col = pl.BlockSpec((L, ATT_COLS), lambda c: (0, c))
    return pl.pallas_call(body, grid=(W // ATT_COLS,), in_specs=_qkv_specs(L, g) + [col] * 3, out_specs=[col] * 3,
                          out_shape=[jax.ShapeDtypeStruct((L, W), F32)] * 3,
                          compiler_params=_params(dimension_semantics=("parallel",)), name=name)(
        qkv, qkv, qkv, lse, do, dl)


def _merge_alpha(l_refs):
    ls = [r[...] for r in l_refs]
    m = jnp.maximum(jnp.maximum(ls[0], ls[1]), ls[2])
    es = [jnp.exp(l - m) for l in ls]
    den = es[0] + es[1] + es[2]
    return [e / den for e in es]


def merge_fwd(os_, ls_, *, name):
    L, W = os_[0].shape
    tr = _pick(L, (256, 128))

    def body(o0, o1, o2, l0, l1, l2, out_ref):
        al = _merge_alpha((l0, l1, l2))
        out_ref[...] = (al[0] * o0[...] + al[1] * o1[...] + al[2] * o2[...]).astype(out_ref.dtype)

    row = pl.BlockSpec((tr, W), lambda i: (i, 0))
    return pl.pallas_call(body, grid=(L // tr,), in_specs=[row] * 6, out_specs=row,
                          out_shape=jax.ShapeDtypeStruct((L, W), BF16), name=name)(*os_, *ls_)


def merge_bwd(os_, ls_, do, *, name):
    L, W = do.shape
    tr = _pick(L, (256, 128))

    def body(o0, o1, o2, l0, l1, l2, do_ref, d0, d1, d2, e0, e1, e2):
        al = _merge_alpha((l0, l1, l2))
        dov = do_ref[...]
        r = lax.broadcasted_iota(jnp.int32, (W, W), 0) // ATT_E
        c = lax.broadcasted_iota(jnp.int32, (W, W), 1) // ATT_E
        ones_blk = (r == c).astype(F32)
        t = jnp.zeros_like(dov)
        for a, o in zip(al, (o0, o1, o2)):
            t = t + a * _dot_tri(ones_blk, dov * o[...], tri_left=False)
        for a, d_ref, e_ref in zip(al, (d0, d1, d2), (e0, e1, e2)):
            d_ref[...] = a * dov
            e_ref[...] = a * t

    row = pl.BlockSpec((tr, W), lambda i: (i, 0))
    return pl.pallas_call(body, grid=(L // tr,), in_specs=[row] * 7, out_specs=[row] * 6,
                          out_shape=[jax.ShapeDtypeStruct((L, W), F32)] * 6, name=name)(*os_, *ls_, do)


def _me_and_peers():
    x, y, c = lax.axis_index("x"), lax.axis_index("y"), lax.axis_index("c")
    peers = []
    for k in range(1, N_DEV):
        px = 1 - x if k & 4 else x
        py = 1 - y if k & 2 else y
        pc = 1 - c if k & 1 else c
        peers.append((px, py, pc))
    return (x, y, c), peers


def _index(dev):
    return 4 * dev[0] + 2 * dev[1] + dev[2]


def _hbm(a):
    return pltpu.with_memory_space_constraint(a, pltpu.HBM)


HBM_SPEC = pl.BlockSpec(memory_space=pltpu.HBM)
SEM_SPEC = pl.BlockSpec(memory_space=pltpu.SEMAPHORE)
DATAFLOW = pltpu.SideEffectType.DATAFLOW_SIDE_EFFECTING


def _remote(src_ref, land_ref, slotted, me, peer, src_is_mine, send_sem, recv_sem, k):
    sender, receiver = (me, peer) if src_is_mine else (peer, me)
    src = src_ref.at[_index(receiver)] if slotted else src_ref
    return pltpu.make_async_remote_copy(src_ref=src, dst_ref=land_ref.at[_index(sender)], send_sem=send_sem.at[k],
                                        recv_sem=recv_sem.at[k], device_id=peer, device_id_type=MESH_ID)


SIBLING = 0
SAME_CORE = (1, 3, 5)
OTHER_CORE = (2, 4, 6)


def copies_start(arrays, mode, *, name):
    n = len(arrays)
    slotted = mode == "exchange"
    lands = [lax.empty(a.shape if slotted else (N_DEV,) + a.shape, a.dtype) for a in arrays]
    targets = (SIBLING,) + SAME_CORE if mode == "gather2" else tuple(range(N_DEV - 1))

    def body(*refs):
        x_refs, land_refs = refs[:n], refs[n:2 * n]
        send, recv = refs[2 * n:3 * n], refs[3 * n:4 * n]
        token = refs[-1]
        me, peers = _me_and_peers()
        for w in range(n):
            for k in targets:
                _remote(x_refs[w], land_refs[w], slotted, me, peers[k], True, send[w], recv[w], k).start()
            if not slotted:
                pltpu.make_async_copy(x_refs[w], land_refs[w].at[_index(me)], recv[w].at[N_DEV - 1]).start()
        token[...] = jnp.zeros_like(token)

    sem = pltpu.SemaphoreType.DMA((N_DEV,))
    out_shape = ([sem] * (2 * n) + [pltpu.HBM(a.shape, a.dtype) for a in arrays]
                 + [pltpu.HBM(l.shape, l.dtype) for l in lands] + [jax.ShapeDtypeStruct((8, 128), F32)])
    outs = pl.pallas_call(
        body, name=name, out_shape=out_shape, in_specs=[HBM_SPEC] * (2 * n),
        out_specs=[SEM_SPEC] * (2 * n) + [HBM_SPEC] * (2 * n) + [pl.BlockSpec(memory_space=pltpu.VMEM)],
        input_output_aliases={i: 2 * n + i for i in range(2 * n)},
        compiler_params=pltpu.CompilerParams(has_side_effects=DATAFLOW),
    )(*[_hbm(a) for a in arrays], *[_hbm(l) for l in lands])
    handles = [(outs[w], outs[n + w], outs[2 * n + w], outs[3 * n + w]) for w in range(n)]
    return handles, outs[-1]


def _forward(land_ref, me, peers, j, fsend, frecv, mine):
    block = _index(peers[SAME_CORE[j]] if mine else peers[OTHER_CORE[j]])
    return pltpu.make_async_remote_copy(src_ref=land_ref.at[block], dst_ref=land_ref.at[block], send_sem=fsend.at[j],
                                        recv_sem=frecv.at[j], device_id=peers[SIBLING], device_id_type=MESH_ID)


def copies_forward(handles, after, *, name):
    n = len(handles)

    def body(*refs):
        land_refs, recv = refs[:n], refs[n:2 * n]
        fsend, frecv = refs[2 * n + 1:3 * n + 1], refs[3 * n + 1:4 * n + 1]
        token = refs[-1]
        me, peers = _me_and_peers()
        for w in range(n):
            for j, k in enumerate(SAME_CORE):
                block = land_refs[w].at[_index(peers[k])]
                pltpu.make_async_remote_copy(src_ref=block, dst_ref=block, send_sem=recv[w].at[N_DEV - 1],
                                             recv_sem=recv[w].at[k], device_id=peers[k], device_id_type=MESH_ID).wait_recv()
                _forward(land_refs[w], me, peers, j, fsend[w], frecv[w], True).start()
        token[...] = jnp.zeros_like(token)

    sem = pltpu.SemaphoreType.DMA((len(SAME_CORE),))
    lands = [h[3] for h in handles]
    outs = pl.pallas_call(
        body, name=name,
        out_shape=[sem] * (2 * n) + [pltpu.HBM(l.shape, l.dtype) for l in lands] + [jax.ShapeDtypeStruct((8, 128), F32)],
        in_specs=[HBM_SPEC] * n + [SEM_SPEC] * n + [pl.BlockSpec(memory_space=pl.ANY)],
        out_specs=[SEM_SPEC] * (2 * n) + [HBM_SPEC] * n + [pl.BlockSpec(memory_space=pltpu.VMEM)],
        input_output_aliases={w: 2 * n + w for w in range(n)},
        compiler_params=pltpu.CompilerParams(has_side_effects=DATAFLOW),
    )(*lands, *[h[1] for h in handles], after)
    new = [(h[0], h[1], h[2], outs[2 * n + w], outs[w], outs[n + w]) for w, h in enumerate(handles)]
    return new, outs[-1]


def copies_wait(handle, mode, after, *, name):
    slotted = mode == "exchange"
    two_level = mode == "gather2"
    send_sem, recv_sem, x_thru, land_thru = handle[:4]
    targets = (SIBLING,) + SAME_CORE if two_level else tuple(range(N_DEV - 1))
    arrivals = (SIBLING,) if two_level else targets

    def body(x_ref, land_ref, send_ref, recv_ref, *rest):
        me, peers = _me_and_peers()
        for k in targets:
            _remote(x_ref, land_ref, slotted, me, peers[k], True, send_ref, recv_ref, k).wait_send()
        for k in arrivals:
            _remote(x_ref, land_ref, slotted, me, peers[k], False, send_ref, recv_ref, k).wait_recv()
        if not slotted:
            pltpu.make_async_copy(x_ref, land_ref.at[_index(me)], recv_ref.at[N_DEV - 1]).wait()
        if two_level:
            fsend, frecv = rest[0], rest[1]
            for j in range(len(SAME_CORE)):
                _forward(land_ref, me, peers, j, fsend, frecv, True).wait_send()
                _forward(land_ref, me, peers, j, fsend, frecv, False).wait_recv()

    extra = list(handle[4:])
    return pl.pallas_call(
        body, name=name, out_shape=(pltpu.HBM(x_thru.shape, x_thru.dtype), pltpu.HBM(land_thru.shape, land_thru.dtype)),
        in_specs=[HBM_SPEC, HBM_SPEC, SEM_SPEC, SEM_SPEC] + [SEM_SPEC] * len(extra) + [pl.BlockSpec(memory_space=pl.ANY)],
        out_specs=(HBM_SPEC, HBM_SPEC), input_output_aliases={0: 0, 1: 1},
        compiler_params=pltpu.CompilerParams(has_side_effects=DATAFLOW),
    )(x_thru, land_thru, send_sem, recv_sem, *extra, after)


def cast_bf16(x, *, dep=None, name):
    R, C = x.shape
    tr = _pick(R, (512, 352, 256, 128, 64))
    deps = [] if dep is None else [dep]

    def body(x_ref, *rest):
        rest[-1][...] = x_ref[...].astype(BF16)

    row = pl.BlockSpec((tr, C), lambda i: (i, 0))
    return pl.pallas_call(body, grid=(R // tr,), in_specs=[row] + [pl.BlockSpec((8, 128), lambda i: (0, 0))] * len(deps),
                          out_specs=row, out_shape=jax.ShapeDtypeStruct((R, C), BF16), name=name)(x, *deps)


def cast_bf16_layer(x3, layer, *, name):
    _, R, C = x3.shape
    tr = _pick(R, (512, 352, 256, 128, 64))

    def body(x_ref, o_ref):
        o_ref[...] = x_ref[...].astype(BF16)

    return pl.pallas_call(body, grid=(R // tr,), in_specs=[pl.BlockSpec((None, tr, C), lambda i: (layer, i, 0))],
                          out_specs=pl.BlockSpec((tr, C), lambda i: (i, 0)),
                          out_shape=jax.ShapeDtypeStruct((R, C), BF16), name=name)(x3)


BD_PARTS = 4


def _blockdiag_call(b, build, G, r, c, name):
    gp = G // BD_PARTS

    def body_build(b_ref, o_ref):
        o_ref[...] = jnp.zeros_like(o_ref)
        for g in range(G):
            o_ref[g // gp, (g % gp) * r:(g % gp + 1) * r, (g % gp) * c:(g % gp + 1) * c] = b_ref[g]

    def body_extract(d_ref, o_ref):
        for g in range(G):
            o_ref[g] = d_ref[g // gp, (g % gp) * r:(g % gp + 1) * r, (g % gp) * c:(g % gp + 1) * c]

    out = jax.ShapeDtypeStruct((BD_PARTS, gp * r, gp * c) if build else (G, r, c), F32)
    return pl.pallas_call(body_build if build else body_extract, out_shape=out, name=name)(b)


def make_blockdiag(G, r, c, name):
    @jax.custom_vjp
    def blockdiag(b):
        return _blockdiag_call(b, True, G, r, c, name + "_build")

    def fwd(b):
        return blockdiag(b), None

    def bwd(_, g):
        return (_blockdiag_call(g, False, G, r, c, name + "_extract"),)

    blockdiag.defvjp(fwd, bwd)
    return blockdiag


def cols_from_shards(g, *, name):
    _, K, n = g.shape
    tk = _pick(K, (256, 128))

    def body(g_ref, o_ref):
        for i in range(N_DEV):
            o_ref[:, i * n:(i + 1) * n] = g_ref[i]

    return pl.pallas_call(body, grid=(K // tk,), in_specs=[pl.BlockSpec((N_DEV, tk, n), lambda i: (0, i, 0))],
                          out_specs=pl.BlockSpec((tk, N_DEV * n), lambda i: (i, 0)),
                          out_shape=jax.ShapeDtypeStruct((K, N_DEV * n), g.dtype), name=name)(g)


def shards_from_cols(w, *, name):
    K, N = w.shape
    n = N // N_DEV
    tk = _pick(K, (256, 128))

    def body(w_ref, o_ref):
        for i in range(N_DEV):
            o_ref[i] = w_ref[:, i * n:(i + 1) * n].astype(o_ref.dtype)

    return pl.pallas_call(body, grid=(K // tk,), in_specs=[pl.BlockSpec((tk, N), lambda i: (i, 0))],
                          out_specs=pl.BlockSpec((N_DEV, tk, n), lambda i: (0, i, 0)),
                          out_shape=jax.ShapeDtypeStruct((N_DEV, K, n), BF16), name=name)(w)


def _adamw(w, g, m, v):
    m = ADAM_B1 * m + (1.0 - ADAM_B1) * g
    v = ADAM_B2 * v + (1.0 - ADAM_B2) * (g * g)
    m_hat = m / (1.0 - ADAM_B1 ** ADAM_STEP)
    v_hat = v / (1.0 - ADAM_B2 ** ADAM_STEP)
    delta = -ADAM_LR * (m_hat / (jnp.sqrt(v_hat) + ADAM_EPS) + ADAM_WD * w)
    return delta, m, v


def reduce_adamw(recv, own, own_slotted, me, w, m, v, *, layer=0, n_layers=1, into=None, name):
    _, R, C = recv.shape
    tr = _pick(R, (176, 192, 160, 184, 128, 64, 32, 16, 8))
    off = layer * (R // tr)

    def body(me_ref, r_ref, own_ref, w_ref, m_ref, v_ref, *rest):
        g_ref, d_ref, nm_ref, nv_ref = rest[-4:]
        mine = me_ref[0]
        g = None
        for i in range(N_DEV):
            part = jnp.where(mine == i, own_ref[...], r_ref[i]).astype(F32)
            g = part if g is None else g + part
        delta, nm, nv = _adamw(w_ref[...], g, m_ref[...], v_ref[...])
        g_ref[...] = g
        d_ref[...] = delta
        nm_ref[...] = nm
        nv_ref[...] = nv

    row = pl.BlockSpec((tr, C), lambda i, me_ref: (i + off, 0))
    own_spec = (pl.BlockSpec((None, tr, C), lambda i, me_ref: (me_ref[0], i, 0)) if own_slotted
                else pl.BlockSpec((tr, C), lambda i, me_ref: (i, 0)))
    rest = [] if into is None else list(into)
    grid_spec = pltpu.PrefetchScalarGridSpec(
        num_scalar_prefetch=1, grid=(R // tr,),
        in_specs=[pl.BlockSpec((N_DEV, tr, C), lambda i, me_ref: (0, i, 0)), own_spec, row, row, row]
        + [pl.BlockSpec(memory_space=pl.ANY)] * len(rest),
        out_specs=[row] * 4)
    return pl.pallas_call(body, grid_spec=grid_spec, out_shape=[jax.ShapeDtypeStruct((n_layers * R, C), F32)] * 4,
                          input_output_aliases={6 + k: k for k in range(len(rest))},
                          compiler_params=_params(dimension_semantics=("parallel",)), name=name)(
        me.reshape(1).astype(jnp.int32), recv, own, w, m, v, *rest)


def _s5_prepare(A_re, A_im, log_dt, B_re, B_im, C_re, C_im):
    G, P, Cg = S5_GROUPS, S5_STATE, S5_GROUP
    dt = jnp.exp(log_dt)[:, None]
    mag = jnp.exp(A_re * dt)
    ab_re = mag * jnp.cos(A_im * dt)
    ab_im = mag * jnp.sin(A_im * dt)
    den = A_re * A_re + A_im * A_im
    nr, ni = ab_re - 1.0, ab_im
    c_re = (nr * A_re + ni * A_im) / den
    c_im = (ni * A_re - nr * A_im) / den
    Bb_re = c_re[..., None] * B_re - c_im[..., None] * B_im
    Bb_im = c_re[..., None] * B_im + c_im[..., None] * B_re
    def dense_in(b, name):
        return make_blockdiag(G, Cg, P, name)(b.transpose(0, 2, 1))

    def dense_out(c, name):
        return make_blockdiag(G, P, Cg, name)(c.transpose(0, 2, 1))

    return (ab_re.reshape(1, G * P), ab_im.reshape(1, G * P), dense_in(Bb_re, "s5_wb_re"), dense_in(Bb_im, "s5_wb_im"),
            dense_out(C_re, "s5_wc_re"), dense_out(-C_im, "s5_wc_im"))


def _lower_bound(gamma):
    return jnp.cumsum(jax.nn.softmax(gamma, axis=0), axis=0)[0:1]


def _ffn_fwd(h, g_norm, get_w_in, conv_w, conv_b, get_w_out, tag, final=None):
    w_in = get_w_in(h)
    hn, hu = norm_mm(h, g_norm, w_in, name=tag + "_in")
    act = convgate_fwd(hu, conv_w, conv_b, name=tag + "_gate")
    w_out = get_w_out(act)
    if final is None:
        h_out = mm(act, w_out, res=h, name=tag + "_out")
    else:
        h_out = mm_final_loss(act, w_out, h, final[0], final[1], name=tag + "_out_loss")
    return h_out, (hn, hu, act), w_in, w_out


def _ffn_bwd(h, g_norm, w_in, conv_w, conv_b, w_out, saved, dh, tag, send_dw_in, send_dw_out):
    hn, hu, act = saved
    sent = send_dw_out(mm(act, dh, ta=True, out_dtype=BF16, name=tag + "_dwout"))
    dact = mm(dh, w_out, tb=True, dep=sent, name=tag + "_dact")
    (dhu_a, dhu_b), dconv_w, dconv_b = convgate_bwd(hu, conv_w, conv_b, dact, name=tag + "_dgate")
    rows = 2 * dhu_a.shape[1]
    dw_in = mm(dhu_a, hn, ta=True, out_dtype=BF16, out_rows=rows, name=tag + "_dwin_a")
    dw_in = mm(dhu_b, hn, ta=True, out_dtype=BF16, out_rows=rows, out_off=rows // 2, into=dw_in, name=tag + "_dwin_b")
    sent = send_dw_in(dw_in)
    dh_in, dg = mm_drms((dhu_a, dhu_b), w_in, h, g_norm, dh, dep=sent, name=tag + "_dhn")
    return dh_in, dg, dconv_w, dconv_b


def kernel(x, positions, norm_mix, norm_ffn, norm_final, mix_w_in, mix_w_out, s5_A_re, s5_A_im, s5_log_dt, s5_B_re, s5_B_im, s5_C_re, s5_C_im, s5_D, s5_glu_w, s5_glu_b, hgrn_gamma, hgrn_norm, att_w_qkv, att_w_o, ffn_w_in, ffn_conv_w, ffn_conv_b, ffn_w_out, loss_target, m_norm_mix, m_norm_ffn, m_norm_final, m_mix_w_in, m_mix_w_out, m_s5_A_re, m_s5_A_im, m_s5_log_dt, m_s5_B_re, m_s5_B_im, m_s5_C_re, m_s5_C_im, m_s5_D, m_s5_glu_w, m_s5_glu_b, m_hgrn_gamma, m_hgrn_norm, m_att_w_qkv, m_att_w_o, m_ffn_w_in, m_ffn_conv_w, m_ffn_conv_b, m_ffn_w_out, v_norm_mix, v_norm_ffn, v_norm_final, v_mix_w_in, v_mix_w_out, v_s5_A_re, v_s5_A_im, v_s5_log_dt, v_s5_B_re, v_s5_B_im, v_s5_C_re, v_s5_C_im, v_s5_D, v_s5_glu_w, v_s5_glu_b, v_hgrn_gamma, v_hgrn_norm, v_att_w_qkv, v_att_w_o, v_ffn_w_in, v_ffn_conv_w, v_ffn_conv_b, v_ffn_w_out):
    W = dict(norm_mix=norm_mix, norm_ffn=norm_ffn, norm_final=norm_final, mix_w_in=mix_w_in, mix_w_out=mix_w_out,
             s5_A_re=s5_A_re, s5_A_im=s5_A_im, s5_log_dt=s5_log_dt, s5_B_re=s5_B_re, s5_B_im=s5_B_im,
             s5_C_re=s5_C_re, s5_C_im=s5_C_im, s5_D=s5_D, s5_glu_w=s5_glu_w, s5_glu_b=s5_glu_b,
             hgrn_gamma=hgrn_gamma, hgrn_norm=hgrn_norm, att_w_qkv=att_w_qkv, att_w_o=att_w_o, ffn_w_in=ffn_w_in,
             ffn_conv_w=ffn_conv_w, ffn_conv_b=ffn_conv_b, ffn_w_out=ffn_w_out)
    M = dict(norm_mix=m_norm_mix, norm_ffn=m_norm_ffn, norm_final=m_norm_final, mix_w_in=m_mix_w_in,
             mix_w_out=m_mix_w_out, s5_A_re=m_s5_A_re, s5_A_im=m_s5_A_im, s5_log_dt=m_s5_log_dt, s5_B_re=m_s5_B_re,
             s5_B_im=m_s5_B_im, s5_C_re=m_s5_C_re, s5_C_im=m_s5_C_im, s5_D=m_s5_D, s5_glu_w=m_s5_glu_w,
             s5_glu_b=m_s5_glu_b, hgrn_gamma=m_hgrn_gamma, hgrn_norm=m_hgrn_norm, att_w_qkv=m_att_w_qkv,
             att_w_o=m_att_w_o, ffn_w_in=m_ffn_w_in, ffn_conv_w=m_ffn_conv_w, ffn_conv_b=m_ffn_conv_b,
             ffn_w_out=m_ffn_w_out)
    V = dict(norm_mix=v_norm_mix, norm_ffn=v_norm_ffn, norm_final=v_norm_final, mix_w_in=v_mix_w_in,
             mix_w_out=v_mix_w_out, s5_A_re=v_s5_A_re, s5_A_im=v_s5_A_im, s5_log_dt=v_s5_log_dt, s5_B_re=v_s5_B_re,
             s5_B_im=v_s5_B_im, s5_C_re=v_s5_C_re, s5_C_im=v_s5_C_im, s5_D=v_s5_D, s5_glu_w=v_s5_glu_w,
             s5_glu_b=v_s5_glu_b, hgrn_gamma=v_hgrn_gamma, hgrn_norm=v_hgrn_norm, att_w_qkv=v_att_w_qkv,
             att_w_o=v_att_w_o, ffn_w_in=v_ffn_w_in, ffn_conv_w=v_ffn_conv_w, ffn_conv_b=v_ffn_conv_b,
             ffn_w_out=v_ffn_w_out)
    return _step(x[0], positions[0], loss_target[0], W, M, V)


TRANSPOSED = ("mix_w_in", "att_w_qkv", "ffn_w_in")
SMALL = ("norm_mix", "norm_ffn", "norm_final", "s5_A_re", "s5_A_im", "s5_log_dt", "s5_B_re", "s5_B_im", "s5_C_re",
         "s5_C_im", "s5_D", "s5_glu_b", "hgrn_gamma", "hgrn_norm", "ffn_conv_b")
ORDER = ("norm_mix", "norm_ffn", "norm_final", "mix_w_in", "mix_w_out", "s5_A_re", "s5_A_im", "s5_log_dt", "s5_B_re",
         "s5_B_im", "s5_C_re", "s5_C_im", "s5_D", "s5_glu_w", "s5_glu_b", "hgrn_gamma", "hgrn_norm", "att_w_qkv",
         "att_w_o", "ffn_w_in", "ffn_conv_w", "ffn_conv_b", "ffn_w_out")
PACK_COLS = 1024


def _step(x, positions, target, W, M, V):
    L, D = x.shape
    me = 4 * lax.axis_index("x") + 2 * lax.axis_index("y") + lax.axis_index("c")
    n_cw = W["ffn_conv_w"].shape[-1]
    T = {n: tuple(jnp.swapaxes(d[n], -1, -2) for d in (W, M, V)) for n in TRANSPOSED}
    first = {
        "mix_w_in": cast_bf16(T["mix_w_in"][0][0], name="mix_w_in_cast"),
        "conv_w": W["ffn_conv_w"].reshape(6, n_cw),
        "s5_glu_w": cast_bf16(W["s5_glu_w"][0], name="s5_glu_w_cast"),
    }
    first_handles, token = copies_start(list(first.values()), "gather2", name="gather_start_first")
    shards = {
        "mix_w_out": cast_bf16(W["mix_w_out"][0], dep=token, name="mix_w_out_cast"),
        "ffn_w_in0": cast_bf16_layer(T["ffn_w_in"][0], 0, name="ffn_w_in0_cast"),
        "ffn_w_out0": cast_bf16_layer(W["ffn_w_out"], 0, name="ffn_w_out0_cast"),
        "att_w_qkv": cast_bf16(T["att_w_qkv"][0][0], name="att_w_qkv_cast"),
        "att_w_o": cast_bf16(W["att_w_o"][0], name="att_w_o_cast"),
        "ffn_w_in1": cast_bf16_layer(T["ffn_w_in"][0], 1, name="ffn_w_in1_cast"),
        "ffn_w_out1": cast_bf16_layer(W["ffn_w_out"], 1, name="ffn_w_out1_cast"),
    }
    gather_handles, token = copies_start(list(shards.values()), "gather2", name="gather_start")
    gather_handle = dict(zip(list(first) + list(shards), first_handles + gather_handles))

    def forward(keys, after, name):
        new, sent = copies_forward([gather_handle[k] for k in keys], after, name=name)
        gather_handle.update(zip(keys, new))
        return sent

    def gathered(key, after, cols):
        _, land = copies_wait(gather_handle[key], "gather2", after, name=key + "_gwait")
        return cols_from_shards(land, name=key + "_asm") if cols else land.reshape(-1, land.shape[-1])

    conv_b = W["ffn_conv_b"].reshape(2, 1, -1)

    s5_params = (W["s5_A_re"][0], W["s5_A_im"][0], W["s5_log_dt"][0], W["s5_B_re"][0], W["s5_B_im"][0],
                 W["s5_C_re"][0], W["s5_C_im"][0])
    (a_re, a_im, wb_re, wb_im, wc_re, wc_im), s5_prep_vjp = jax.vjp(_s5_prepare, *s5_params)
    dvec = W["s5_D"].reshape(1, S5_WIDTH)
    glu_b = W["s5_glu_b"].reshape(1, S5_WIDTH)
    lb, lb_vjp = jax.vjp(_lower_bound, W["hgrn_gamma"])
    hg_norm = W["hgrn_norm"].reshape(1, -1)
    tabs = rope_tables(positions)

    sent = forward(["mix_w_in", "conv_w", "s5_glu_w"], token, "forward_a")
    w_mix_in = gathered("mix_w_in", sent, False)
    hn0, proj = norm_mm(x, W["norm_mix"][0], w_mix_in, name="l0_proj")
    y0, xs_re, xs_im = s5_core_fwd(proj, a_re, a_im, wb_re, wb_im, wc_re, wc_im, name="s5_core")
    w_glu = gathered("s5_glu_w", y0, False)
    cat = s5_out_fwd(y0, proj, dvec, w_glu, glu_b, name="s5_out")
    cat, hg_states = hgrn_fwd(proj, lb, hg_norm, cat, name="hgrn_fwd")
    forward(["mix_w_out"], cat, "forward_b")
    w_mix_out = gathered("mix_w_out", cat, False)
    h1 = mm(cat, w_mix_out, res=x, name="l0_mix_out")
    _, cw_all = copies_wait(gather_handle["conv_w"], "gather2", h1, name="conv_w_gwait")
    conv_w = cw_all.transpose(1, 0, 2).reshape(2, 3, N_DEV * n_cw)
    w_ffn_in, w_ffn_out = [None, None], [None, None]
    h2, ffn0_saved, w_ffn_in[0], w_ffn_out[0] = _ffn_fwd(
        h1, W["norm_ffn"][0],
        lambda a: (forward(["ffn_w_in0"], a, "forward_b2"), gathered("ffn_w_in0", a, False))[1], conv_w[0], conv_b[0],
        lambda a: (forward(["ffn_w_out0"], a, "forward_c"), gathered("ffn_w_out0", a, False))[1], "ffn0")

    forward(["att_w_qkv", "att_w_o"], h2, "forward_d")
    w_qkv = gathered("att_w_qkv", h2, False)
    hn2, qkv_r = norm_mm(h2, W["norm_mix"][1], w_qkv, tabs=tabs, name="l1_qkv")
    att_o, att_l = [], []
    for g, d in enumerate(ATT_DILATIONS):
        o_g, l_g = attn_fwd(qkv_r, g, d, name=f"attn_fwd{g}")
        att_o.append(o_g)
        att_l.append(l_g)
    o_att = merge_fwd(att_o, att_l, name="merge_fwd")
    forward(["ffn_w_in1", "ffn_w_out1"], o_att, "forward_e")
    w_o = gathered("att_w_o", o_att, True)
    h3 = mm(o_att, w_o, res=h2, name="l1_mix_out")
    (loss, dh4, dg_final), ffn1_saved, w_ffn_in[1], w_ffn_out[1] = _ffn_fwd(
        h3, W["norm_ffn"][1], lambda a: gathered("ffn_w_in1", a, False), conv_w[1], conv_b[1],
        lambda a: gathered("ffn_w_out1", a, False), "ffn1", final=(W["norm_final"], target))

    exchanges = {}

    pending = []

    def send_grad(key, g, cols, flush=True):
        if cols:
            parts = shards_from_cols(g, name=key + "_split")
        else:
            parts = g.reshape(N_DEV, g.shape[0] // N_DEV, g.shape[1])
        pending.append((key, parts))
        if not flush:
            return None
        handles, sent = copies_start([p for _, p in pending], "exchange", name=key + "_xstart")
        exchanges.update(zip([k for k, _ in pending], handles))
        pending.clear()
        return sent

    dh3, dg_ffn1, dcw1, dcb1 = _ffn_bwd(h3, W["norm_ffn"][1], w_ffn_in[1], conv_w[1], conv_b[1], w_ffn_out[1],
                                        ffn1_saved, dh4, "ffn1", lambda g: send_grad("ffn_w_in1", g, False),
                                        lambda g: send_grad("ffn_w_out1", g, False, flush=False))
    sent = send_grad("att_w_o", mm(o_att, dh3, ta=True, name="l1_dwo"), True, flush=False)
    d_oatt = mm(dh3, w_o, tb=True, dep=sent, name="l1_dmix")
    mb = merge_bwd(att_o, att_l, d_oatt, name="merge_bwd")
    d_slabs = [attn_bwd(qkv_r, g, att_l[g], mb[g], mb[3 + g], d, name=f"attn_bwd{g}")
               for g, d in enumerate(ATT_DILATIONS)]
    d_qkv = rope_bwd([s[0] for s in d_slabs] + [s[1] for s in d_slabs] + [s[2] for s in d_slabs], tabs,
                     name="rope_bwd")
    sent = send_grad("att_w_qkv", mm(d_qkv, hn2, ta=True, out_dtype=BF16, name="l1_dwqkv"), False)
    dh2, dg_mix1 = mm_drms(d_qkv, w_qkv, h2, W["norm_mix"][1], dh3, dep=sent, name="l1_dhn")

    dh1, dg_ffn0, dcw0, dcb0 = _ffn_bwd(h1, W["norm_ffn"][0], w_ffn_in[0], conv_w[0], conv_b[0], w_ffn_out[0],
                                        ffn0_saved, dh2, "ffn0", lambda g: send_grad("ffn_w_in0", g, False),
                                        lambda g: send_grad("ffn_w_out0", g, False, flush=False))
    sent = send_grad("mix_w_out", mm(cat, dh1, ta=True, out_dtype=BF16, name="l0_dwout"), False)
    dcat = mm(dh1, w_mix_out, tb=True, dep=sent, name="l0_dcat")
    dy, du_d, z_bf, dzg, dglu_b, dD = s5_out_bwd(y0, proj, dvec, w_glu, glu_b, dcat, name="s5_dout")
    sent_glu = send_grad("s5_glu_w", mm(z_bf, dzg, ta=True, out_dtype=BF16, name="s5_dglu"), False, flush=False)
    du, dwb_re, dwb_im, dwc_re, dwc_im, da_re, da_im = s5_core_bwd(
        dy, du_d, proj, xs_re, xs_im, a_re, a_im, wb_re, wb_im, wc_re, wc_im, name="s5_dcore")
    s5_small = s5_prep_vjp((da_re, da_im, dwb_re, dwb_im, dwc_re, dwc_im))
    d_proj, dlb, dhg_norm = hgrn_bwd(proj, lb, hg_norm, hg_states, dcat, du, name="hgrn_bwd")
    sent = send_grad("mix_w_in", mm(d_proj, hn0, ta=True, out_dtype=BF16, dep=sent_glu, name="l0_dwin"), False)
    grad_x, dg_mix0 = mm_drms(d_proj, w_mix_in, x, W["norm_mix"][0], dh1, dep=sent, name="l0_dhn")
    (d_gamma,) = lb_vjp(dlb)
    out = {}

    dA_re, dA_im, dlog_dt, dB_re, dB_im, dC_re, dC_im = s5_small
    small_g = dict(norm_mix=jnp.concatenate([dg_mix0, dg_mix1], axis=0), norm_ffn=jnp.concatenate([dg_ffn0, dg_ffn1], axis=0),
                   norm_final=dg_final, s5_A_re=dA_re, s5_A_im=dA_im, s5_log_dt=dlog_dt, s5_B_re=dB_re, s5_B_im=dB_im,
                   s5_C_re=dC_re, s5_C_im=dC_im, s5_D=dD, s5_glu_b=dglu_b, hgrn_gamma=d_gamma, hgrn_norm=dhg_norm,
                   ffn_conv_b=jnp.concatenate([dcb0, dcb1], axis=0))
    conv_w_g = jnp.stack([dcw0, dcw1], axis=0)
    sizes = [math.prod(W[n].shape) for n in SMALL]
    n_conv = conv_w_g.size
    total = sum(sizes) + n_conv + 1
    rows = -(-total // PACK_COLS)
    rows = -(-rows // 8) * 8
    pad = rows * PACK_COLS - total

    def pack(vals, conv_part, last):
        flat = [v.reshape(-1).astype(F32) for v in vals] + [conv_part.reshape(-1), last.reshape(-1),
                                                            jnp.zeros((pad,), F32)]
        return jnp.concatenate(flat).reshape(rows, PACK_COLS)

    def conv_full(shard):
        col_owner = lax.broadcasted_iota(jnp.int32, (2, 3, N_DEV * n_cw), 2) // n_cw
        return jnp.where(col_owner == me, jnp.tile(shard, (1, 1, N_DEV)), 0.0)

    zero1 = jnp.zeros((1,), F32)
    g_pack = pack([small_g[n] for n in SMALL], conv_w_g, loss)
    w_pack = pack([W[n] for n in SMALL], conv_full(W["ffn_conv_w"]), zero1)
    m_pack = pack([M[n] for n in SMALL], conv_full(M["ffn_conv_w"]), zero1)
    v_pack = pack([V[n] for n in SMALL], conv_full(V["ffn_conv_w"]), zero1 + 1.0)
    (small_handle,), small_sent = copies_start([g_pack], "gather", name="small_xstart")

    def finish(name, n_layers):
        w3, m3, v3 = T[name] if name in TRANSPOSED else (W[name], M[name], V[name])
        res = None
        for layer in reversed(range(n_layers)):
            key = name if n_layers == 1 else f"{name}{layer}"
            own, recv = copies_wait(exchanges[key], "exchange", small_sent, name=key + "_xwait")
            _, R, Cn = recv.shape
            res = reduce_adamw(recv, own, True, me, w3.reshape(n_layers * R, Cn), m3.reshape(n_layers * R, Cn),
                               v3.reshape(n_layers * R, Cn), layer=layer, n_layers=n_layers, into=res,
                               name=key + "_adamw")
        res = [r.reshape(w3.shape) for r in res]
        return tuple(jnp.swapaxes(r, -1, -2) for r in res) if name in TRANSPOSED else tuple(res)

    for name in ("ffn_w_out", "ffn_w_in"):
        out[name] = finish(name, 2)
    for name in ("att_w_o", "att_w_qkv", "mix_w_out", "s5_glu_w", "mix_w_in"):
        out[name] = finish(name, 1)

    small_own, small_recv = copies_wait(small_handle, "gather", out["s5_glu_w"][0], name="small_xwait")
    res = reduce_adamw(small_recv, small_own, False, me, w_pack, m_pack, v_pack, name="small_adamw")
    flat = [r.reshape(-1) for r in res]
    off = 0
    for n, sz in zip(SMALL, sizes):
        out[n] = tuple(f[off:off + sz].reshape(W[n].shape) for f in flat)
        off += sz
    conv_res = [f[off:off + n_conv].reshape(2, 3, N_DEV * n_cw) for f in flat]
    out["ffn_conv_w"] = tuple(lax.dynamic_slice(c, (0, 0, me * n_cw), (2, 3, n_cw)) for c in conv_res)
    off += n_conv
    loss_total = flat[0][off]

    result = [loss_total, grad_x[None]]
    for k in range(4):
        result += [out[n][k] for n in ORDER]
    return tuple(result)
```

```python
import math

import jax
import jax.numpy as jnp
from jax import lax
from jax.experimental import pallas as pl
from jax.experimental.pallas import tpu as pltpu

F32 = jnp.float32
BF16 = jnp.bfloat16
MESH_ID = pl.DeviceIdType.MESH
N_DEV = 8
VMEM_LIMIT_BYTES = 56 * 1024 * 1024

NORM_EPS = 1e-6
S5_WIDTH, S5_GROUP, S5_GROUPS, S5_STATE = 512, 16, 32, 64
HG_HEADS, HG_DIM, HG_CHUNK = 4, 128, 64
HG_STEP_CHUNKS = 4
ATT_E, ATT_HPG, ATT_BLOCK = 64, 8, 128
ATT_DILATIONS = (1, 4, 16)
ROT_DIM, ROPE_THETA = 16, 500000.0
D_FF = 2816
ADAM_LR, ADAM_B1, ADAM_B2, ADAM_EPS, ADAM_WD, ADAM_STEP = 0.001, 0.9, 0.999, 1e-08, 0.01, 10
NEG_BIG = -1e30


def _params(**kw):
    return pltpu.CompilerParams(vmem_limit_bytes=VMEM_LIMIT_BYTES, **kw)


def _pick(n, cands):
    for c in cands:
        if n % c == 0:
            return c
    return n


def _dot(a, b):
    return jnp.dot(a.astype(BF16), b.astype(BF16), preferred_element_type=F32)


def _dot_nt(a, b):
    return lax.dot_general(a.astype(BF16), b.astype(BF16), (((1,), (1,)), ((), ())), preferred_element_type=F32)


def _dot_tn(a, b):
    return lax.dot_general(a.astype(BF16), b.astype(BF16), (((0,), (0,)), ((), ())), preferred_element_type=F32)


def _split2(x):
    hi = x.astype(BF16)
    return hi, (x - hi.astype(F32)).astype(BF16)


def _dot_x3(a, b, contract=((1,), (0,))):
    dn = (contract, ((), ()))
    a1, a2 = _split2(a)
    b1, b2 = _split2(b)
    return (lax.dot_general(a1, b1, dn, preferred_element_type=F32) + lax.dot_general(a1, b2, dn, preferred_element_type=F32)
            + lax.dot_general(a2, b1, dn, preferred_element_type=F32))


def _sigmoid(x):
    return 1.0 / (1.0 + jnp.exp(-x))


V7X_HBM_BYTES_PER_S = 3.2e12
V7X_MXU_FLOPS_PER_S = 0.7e15
GRID_STEP_S = 0.35e-6
MM_VMEM_BUDGET = 40 * 1024 * 1024


def _divisors(n, cands):
    return [c for c in cands if c <= n and n % c == 0] or [n]


def _mm_tiles(m, n, k, sa, sb, so, sr):
    best = None
    for tm in _divisors(m, (2816, 2048, 1408, 1024, 512, 256, 128)):
        for tn in _divisors(n, (2816, 2048, 1408, 1024, 512, 256, 128)):
            for tk in _divisors(k, (k, 2816, 2560, 2304, 2048, 1536, 1408, 1280, 1024, 512, 256, 128)):
                nk = k // tk
                vmem = 2 * (tm * tk * sa + tk * tn * sb + tm * tn * (so + sr)) + (tm * tn * 4 if nk > 1 else 0)
                vmem += tm * tk * 2 * (sa > 2) + tk * tn * 2 * (sb > 2) + tm * tn * 4
                if vmem > MM_VMEM_BUDGET:
                    continue
                ni, nj = m // tm, n // tn
                for i_outer in (True, False):
                    if i_outer:
                        a_reads = 1 if nk == 1 else nj
                        b_reads = 1 if (nk == 1 and nj == 1) else ni
                    else:
                        b_reads = 1 if nk == 1 else ni
                        a_reads = 1 if (nk == 1 and ni == 1) else nj
                    traffic = a_reads * m * k * sa + b_reads * k * n * sb + m * n * (so + sr)
                    t = max(traffic / V7X_HBM_BYTES_PER_S, 2.0 * m * n * k / V7X_MXU_FLOPS_PER_S)
                    t += ni * nj * nk * GRID_STEP_S
                    t += (tm * tk * sa + tk * tn * sb + tm * tn * so) / V7X_HBM_BYTES_PER_S
                    if best is None or t < best[0]:
                        best = (t, tm, tn, tk, i_outer)
    assert best is not None, (m, n, k)
    return best[1:]


def mm(a, b, *, ta=False, tb=False, res=None, out_dtype=F32, dep=None, out_rows=None, out_off=0, into=None, name):
    m, k = (a.shape[1], a.shape[0]) if ta else a.shape
    n = b.shape[0] if tb else b.shape[1]
    assert (b.shape[1] if tb else b.shape[0]) == k
    has_res = res is not None
    tm, tn, tk, i_outer = _mm_tiles(m, n, k, a.dtype.itemsize, b.dtype.itemsize, jnp.dtype(out_dtype).itemsize,
                                    res.dtype.itemsize if has_res else 0)
    nk = k // tk
    deps = [] if dep is None else [dep]
    dn = (((0 if ta else 1,), (1 if tb else 0,)), ((), ()))

    def body_single(*refs):
        a_ref, b_ref = refs[:2]
        o_ref = refs[-1]
        out = lax.dot_general(a_ref[...].astype(BF16), b_ref[...].astype(BF16), dn, preferred_element_type=F32)
        if has_res:
            out = out + refs[2][...].astype(F32)
        o_ref[...] = out.astype(o_ref.dtype)

    def body(*refs):
        a_ref, b_ref = refs[:2]
        r_ref = refs[2] if has_res else None
        o_ref, acc_ref = refs[-2:]
        kk = pl.program_id(2)
        part = lax.dot_general(a_ref[...].astype(BF16), b_ref[...].astype(BF16), dn, preferred_element_type=F32)

        @pl.when(kk == 0)
        def _():
            acc_ref[...] = part

        @pl.when(kk > 0)
        def _():
            acc_ref[...] += part

        @pl.when(kk == nk - 1)
        def _():
            out = acc_ref[...]
            if has_res:
                out = out + r_ref[...].astype(F32)
            o_ref[...] = out.astype(o_ref.dtype)

    def ij(f):
        return (lambda g0, g1, q: f(g0, g1, q)) if i_outer else (lambda g0, g1, q: f(g1, g0, q))

    a_spec = pl.BlockSpec((tk, tm), ij(lambda i, j, q: (q, i))) if ta else pl.BlockSpec((tm, tk), ij(lambda i, j, q: (i, q)))
    b_spec = pl.BlockSpec((tn, tk), ij(lambda i, j, q: (j, q))) if tb else pl.BlockSpec((tk, tn), ij(lambda i, j, q: (q, j)))
    assert out_off % tm == 0
    off = out_off // tm
    r_spec = pl.BlockSpec((tm, tn), ij(lambda i, j, q: (i, j)))
    o_spec = pl.BlockSpec((tm, tn), ij(lambda i, j, q: (i + off, j)))
    rest = [] if into is None else [into]
    in_specs = ([a_spec, b_spec] + ([r_spec] if has_res else []) + [pl.BlockSpec((8, 128), lambda g0, g1, q: (0, 0))] * len(deps)
                + [pl.BlockSpec(memory_space=pl.ANY)] * len(rest))
    args = (a, b) + ((res,) if has_res else ()) + tuple(deps) + tuple(rest)
    grid = (m // tm, n // tn, nk) if i_outer else (n // tn, m // tm, nk)
    return pl.pallas_call(
        body_single if nk == 1 else body, grid=grid, in_specs=in_specs, out_specs=o_spec,
        out_shape=jax.ShapeDtypeStruct((out_rows or m, n), out_dtype),
        input_output_aliases={len(args) - 1: 0} if rest else {},
        scratch_shapes=[] if nk == 1 else [pltpu.VMEM((tm, tn), F32)],
        compiler_params=_params(dimension_semantics=("parallel", "parallel", "arbitrary")), name=name,
    )(*args)


def mm_drms(dy_in, w, x, g, dres, *, dep=None, name):
    halves = dy_in if isinstance(dy_in, (tuple, list)) else (dy_in,)
    m, kh = halves[0].shape
    k = kh * len(halves)
    D = w.shape[1]
    tm = _pick(m, (1024, 512, 256, 128))
    tk = max(_divisors(kh, (1536, 1408, 1280, 1024, 512, 256, 128)))
    nk, nh = k // tk, kh // tk
    deps = [] if dep is None else [dep]

    def body(*refs):
        a_refs, (b_ref, x_ref, g_ref, dres_ref) = refs[:len(halves)], refs[len(halves):len(halves) + 4]
        dx_ref, dg_ref, acc_ref = refs[-3:]
        i, q = pl.program_id(0), pl.program_id(1)
        a = a_refs[0][...] if len(halves) == 1 else jnp.where(q < nh, a_refs[0][...], a_refs[1][...])
        part = jnp.dot(a, b_ref[...], preferred_element_type=F32)

        @pl.when(q == 0)
        def _():
            acc_ref[...] = part

        @pl.when(q > 0)
        def _():
            acc_ref[...] += part

        @pl.when((i == 0) & (q == 0))
        def _():
            dg_ref[...] = jnp.zeros_like(dg_ref)

        @pl.when(q == nk - 1)
        def _():
            dyv = acc_ref[...]
            xv = x_ref[...]
            r = lax.rsqrt(jnp.mean(xv * xv, axis=-1, keepdims=True) + NORM_EPS)
            xh = xv * r
            dg_ref[...] += jnp.sum(dyv * xh, axis=0, keepdims=True)
            dxh = dyv * g_ref[...]
            dx_ref[...] = dres_ref[...] + r * (dxh - xh * jnp.mean(dxh * xh, axis=-1, keepdims=True))

    row = pl.BlockSpec((tm, D), lambda i, q: (i, 0))
    vec = pl.BlockSpec((1, D), lambda i, q: (0, 0))
    a_specs = [pl.BlockSpec((tm, tk), lambda i, q, h=h: (i, jnp.clip(q - h * nh, 0, nh - 1))) for h in range(len(halves))]
    in_specs = a_specs + [pl.BlockSpec((tk, D), lambda i, q: (q, 0)), row, vec, row]
    in_specs += [pl.BlockSpec((8, 128), lambda i, q: (0, 0))] * len(deps)
    return pl.pallas_call(
        body, grid=(m // tm, nk), in_specs=in_specs, out_specs=[row, vec],
        out_shape=[jax.ShapeDtypeStruct((m, D), F32), jax.ShapeDtypeStruct((1, D), F32)],
        scratch_shapes=[pltpu.VMEM((tm, D), F32)],
        compiler_params=_params(dimension_semantics=("arbitrary", "arbitrary")), name=name,
    )(*halves, w, x, g.reshape(1, D), dres, *deps)


def mm_final_loss(act, w, h_res, g, target, *, name):
    L, K = act.shape
    D = w.shape[1]
    tm = _pick(L, (1024, 512, 256, 128))
    tk = max(_divisors(K, (1536, 1408, 1280, 1024, 512, 256, 128)))
    nk = K // tk

    def body(a_ref, b_ref, r_ref, g_ref, t_ref, loss_ref, dx_ref, dg_ref, acc_ref):
        i, q = pl.program_id(0), pl.program_id(1)
        part = jnp.dot(a_ref[...], b_ref[...], preferred_element_type=F32)

        @pl.when(q == 0)
        def _():
            acc_ref[...] = part

        @pl.when(q > 0)
        def _():
            acc_ref[...] += part

        @pl.when((i == 0) & (q == 0))
        def _():
            dg_ref[...] = jnp.zeros_like(dg_ref)
            loss_ref[...] = jnp.zeros_like(loss_ref)

        @pl.when(q == nk - 1)
        def _():
            xv = acc_ref[...] + r_ref[...]
            gv = g_ref[...]
            r = lax.rsqrt(jnp.mean(xv * xv, axis=-1, keepdims=True) + NORM_EPS)
            xh = xv * r
            err = xh * gv - t_ref[...]
            loss_ref[...] += 0.5 * jnp.sum(jnp.mean(err * err, axis=-1, keepdims=True), axis=0, keepdims=True)
            dyv = err * (1.0 / D)
            dg_ref[...] += jnp.sum(dyv * xh, axis=0, keepdims=True)
            dxh = dyv * gv
            dx_ref[...] = r * (dxh - xh * jnp.mean(dxh * xh, axis=-1, keepdims=True))

    row = pl.BlockSpec((tm, D), lambda i, q: (i, 0))
    vec = pl.BlockSpec((1, D), lambda i, q: (0, 0))
    one = pl.BlockSpec((1, 1), lambda i, q: (0, 0))
    return pl.pallas_call(
        body, grid=(L // tm, nk),
        in_specs=[pl.BlockSpec((tm, tk), lambda i, q: (i, q)), pl.BlockSpec((tk, D), lambda i, q: (q, 0)), row, vec, row],
        out_specs=[one, row, vec],
        out_shape=[jax.ShapeDtypeStruct((1, 1), F32), jax.ShapeDtypeStruct((L, D), F32), jax.ShapeDtypeStruct((1, D), F32)],
        scratch_shapes=[pltpu.VMEM((tm, D), F32)],
        compiler_params=_params(dimension_semantics=("arbitrary", "arbitrary")), name=name,
    )(act, w, h_res, g.reshape(1, D), target)


def _cmul(ar, ai, br, bi):
    return ar * br - ai * bi, ar * bi + ai * br


def _powers(ar, ai):
    rows = [(ar, ai)]
    for _ in range(7):
        rows.append(_cmul(rows[-1][0], rows[-1][1], ar, ai))
    table = (jnp.concatenate([r[0] for r in rows], axis=0), jnp.concatenate([r[1] for r in rows], axis=0))
    return (rows[0], rows[1], rows[3]), table


def _block_scan(br, bi, steps, shift):
    yr, yi = br, bi
    for s, (pr, pi) in zip((1, 2, 4), steps):
        sr, si = shift(yr, s), shift(yi, s)
        yr, yi = yr + pr * sr - pi * si, yi + pr * si + pi * sr
    return yr, yi


def s5_core_fwd(proj, a_re, a_im, wb_re, wb_im, wc_re, wc_im, *, name):
    L = proj.shape[0]
    parts, cu, W = wb_re.shape

    def body(u_ref, ar_ref, ai_ref, wbr_ref, wbi_ref, wcr_ref, wci_ref, y_ref, xr_ref, xi_ref, br_ref, bi_ref):
        u = u_ref[...]
        br_ref[...] = _dot(u, wbr_ref[...])
        bi_ref[...] = _dot(u, wbi_ref[...])
        steps, (tr, ti) = _powers(ar_ref[...], ai_ref[...])
        row = lax.broadcasted_iota(jnp.int32, (8, W), 0)

        def shift(y, s):
            return jnp.where(row >= s, pltpu.roll(y, s, 0), 0.0)

        def step(t8, carry):
            cr, ci = carry
            base = pl.multiple_of(t8 * 8, 8)
            yr, yi = _block_scan(br_ref[pl.ds(base, 8), :], bi_ref[pl.ds(base, 8), :], steps, shift)
            xr = yr + tr * cr - ti * ci
            xi = yi + tr * ci + ti * cr
            xr_ref[pl.ds(base, 8), :] = xr
            xi_ref[pl.ds(base, 8), :] = xi
            return jnp.broadcast_to(xr[7:8, :], (8, W)), jnp.broadcast_to(xi[7:8, :], (8, W))

        zero = jnp.zeros((8, W), F32)
        lax.fori_loop(0, L // 8, step, (zero, zero), unroll=2)
        y_ref[...] = _dot(xr_ref[...], wcr_ref[...]) + _dot(xi_ref[...], wci_ref[...])

    ucol = pl.BlockSpec((L, cu), lambda t: (0, t))
    vec = pl.BlockSpec((1, W), lambda t: (0, t))
    col = pl.BlockSpec((L, W), lambda t: (0, t))
    wb = pl.BlockSpec((None, cu, W), lambda t: (t, 0, 0))
    wc = pl.BlockSpec((None, W, cu), lambda t: (t, 0, 0))
    return pl.pallas_call(body, grid=(parts,), in_specs=[ucol, vec, vec, wb, wb, wc, wc], out_specs=[ucol, col, col],
                          out_shape=[jax.ShapeDtypeStruct((L, parts * cu), F32)]
                          + [jax.ShapeDtypeStruct((L, parts * W), F32)] * 2,
                          scratch_shapes=[pltpu.VMEM((L, W), F32)] * 2,
                          compiler_params=_params(dimension_semantics=("parallel",)), name=name)(
        proj, a_re, a_im, wb_re, wb_im, wc_re, wc_im)


def s5_core_bwd(dy, du_d, proj, xs_re, xs_im, a_re, a_im, wb_re, wb_im, wc_re, wc_im, *, name):
    L = proj.shape[0]
    parts, cu, W = wb_re.shape

    def body(dy_ref, dud_ref, u_ref, xr_ref, xi_ref, ar_ref, ai_ref, wbr_ref, wbi_ref, wcr_ref, wci_ref,
             du_ref, dwbr_ref, dwbi_ref, dwcr_ref, dwci_ref, dar_ref, dai_ref, lr_ref, li_ref):
        dy = dy_ref[...]
        lr_ref[...] = _dot_nt(dy, wcr_ref[...])
        li_ref[...] = _dot_nt(dy, wci_ref[...])
        dwcr_ref[...] = _dot_tn(xr_ref[...], dy)
        dwci_ref[...] = _dot_tn(xi_ref[...], dy)
        ar, ai = ar_ref[...], -ai_ref[...]
        steps, (tr, ti) = _powers(ar, ai)
        tr = jnp.concatenate([tr[j:j + 1, :] for j in range(7, -1, -1)], axis=0)
        ti = jnp.concatenate([ti[j:j + 1, :] for j in range(7, -1, -1)], axis=0)
        row8 = lax.broadcasted_iota(jnp.int32, (8, W), 0)
        nblk = L // 8

        def shift(y, s):
            return jnp.where(row8 < 8 - s, pltpu.roll(y, 8 - s, 0), 0.0)

        def step(s, carry):
            cr, ci = carry
            base = pl.multiple_of((nblk - 1 - s) * 8, 8)
            yr, yi = _block_scan(lr_ref[pl.ds(base, 8), :], li_ref[pl.ds(base, 8), :], steps, shift)
            lr = yr + tr * cr - ti * ci
            li = yi + tr * ci + ti * cr
            lr_ref[pl.ds(base, 8), :] = lr
            li_ref[pl.ds(base, 8), :] = li
            return jnp.broadcast_to(lr[0:1, :], (8, W)), jnp.broadcast_to(li[0:1, :], (8, W))

        zero = jnp.zeros((8, W), F32)
        lax.fori_loop(0, nblk, step, (zero, zero), unroll=2)
        row = lax.broadcasted_iota(jnp.int32, (L, W), 0)
        xpr = jnp.where(row >= 1, pltpu.roll(xr_ref[...], 1, 0), 0.0)
        xpi = jnp.where(row >= 1, pltpu.roll(xi_ref[...], 1, 0), 0.0)
        lr, li = lr_ref[...], li_ref[...]
        dar_ref[...] = jnp.sum(lr * xpr + li * xpi, axis=0, keepdims=True)
        dai_ref[...] = jnp.sum(li * xpr - lr * xpi, axis=0, keepdims=True)
        u = u_ref[...]
        dwbr_ref[...] = _dot_tn(u, lr)
        dwbi_ref[...] = _dot_tn(u, li)
        du_ref[...] = (dud_ref[...] + _dot_nt(lr, wbr_ref[...]) + _dot_nt(li, wbi_ref[...])).astype(du_ref.dtype)

    ucol = pl.BlockSpec((L, cu), lambda t: (0, t))
    vec = pl.BlockSpec((1, W), lambda t: (0, t))
    col = pl.BlockSpec((L, W), lambda t: (0, t))
    wb = pl.BlockSpec((None, cu, W), lambda t: (t, 0, 0))
    wc = pl.BlockSpec((None, W, cu), lambda t: (t, 0, 0))
    return pl.pallas_call(
        body, grid=(parts,), in_specs=[ucol, ucol, ucol, col, col, vec, vec, wb, wb, wc, wc],
        out_specs=[ucol, wb, wb, wc, wc, vec, vec],
        out_shape=[jax.ShapeDtypeStruct((L, parts * cu), BF16)] + [jax.ShapeDtypeStruct((parts, cu, W), F32)] * 2
        + [jax.ShapeDtypeStruct((parts, W, cu), F32)] * 2 + [jax.ShapeDtypeStruct((1, parts * W), F32)] * 2,
        scratch_shapes=[pltpu.VMEM((L, W), F32)] * 2,
        compiler_params=_params(dimension_semantics=("parallel",)), name=name,
    )(dy, du_d, proj, xs_re, xs_im, a_re, a_im, wb_re, wb_im, wc_re, wc_im)


def _gelu(y):
    c = math.sqrt(2.0 / math.pi)
    t = jnp.tanh(c * (y + 0.044715 * y * y * y))
    return 0.5 * y * (1.0 + t), t


def s5_out_fwd(y0, proj, dvec, glu_w, glu_b, *, name):
    L, C = y0.shape
    tr = _pick(L, (256, 128))

    def body(y_ref, u_ref, d_ref, w_ref, b_ref, o_ref):
        z, _ = _gelu(y_ref[...] + d_ref[...] * u_ref[...])
        zg = _dot(z, w_ref[...]) + b_ref[...]
        o_ref[...] = (z * _sigmoid(zg)).astype(o_ref.dtype)

    row = pl.BlockSpec((tr, C), lambda i: (i, 0))
    vec = pl.BlockSpec((1, C), lambda i: (0, 0))
    wsp = pl.BlockSpec((C, C), lambda i: (0, 0))
    return pl.pallas_call(body, grid=(L // tr,), in_specs=[row, row, vec, wsp, vec], out_specs=row,
                          out_shape=jax.ShapeDtypeStruct((L, 2 * C), BF16), name=name)(
        y0, proj, dvec, glu_w, glu_b)


def s5_out_bwd(y0, proj, dvec, glu_w, glu_b, dcat, *, name):
    L, C = y0.shape
    tr = _pick(L, (256, 128))

    def body(y_ref, u_ref, d_ref, w_ref, b_ref, do_ref, dy_ref, dud_ref, z_ref, dzg_ref, db_ref, dd_ref):
        u = u_ref[...]
        y = y_ref[...] + d_ref[...] * u
        z, t = _gelu(y)
        zg = _dot(z, w_ref[...]) + b_ref[...]
        s = _sigmoid(zg)
        do = do_ref[...]
        dzg = do * z * s * (1.0 - s)
        dz = do * s + _dot_nt(dzg, w_ref[...])
        c = math.sqrt(2.0 / math.pi)
        dgelu = 0.5 * (1.0 + t) + 0.5 * y * (1.0 - t * t) * c * (1.0 + 3.0 * 0.044715 * y * y)
        dy = dz * dgelu

        @pl.when(pl.program_id(0) == 0)
        def _():
            db_ref[...] = jnp.zeros_like(db_ref)
            dd_ref[...] = jnp.zeros_like(dd_ref)

        db_ref[...] += jnp.sum(dzg, axis=0, keepdims=True)
        dd_ref[...] += jnp.sum(dy * u, axis=0, keepdims=True)
        dy_ref[...] = dy
        dud_ref[...] = dy * d_ref[...]
        z_ref[...] = z.astype(BF16)
        dzg_ref[...] = dzg.astype(BF16)

    row = pl.BlockSpec((tr, C), lambda i: (i, 0))
    vec = pl.BlockSpec((1, C), lambda i: (0, 0))
    wsp = pl.BlockSpec((C, C), lambda i: (0, 0))
    return pl.pallas_call(body, grid=(L // tr,), in_specs=[row, row, vec, wsp, vec, row],
                          out_specs=[row, row, row, row, vec, vec],
                          out_shape=[jax.ShapeDtypeStruct((L, C), F32), jax.ShapeDtypeStruct((L, C), F32),
                                     jax.ShapeDtypeStruct((L, C), BF16), jax.ShapeDtypeStruct((L, C), BF16),
                                     jax.ShapeDtypeStruct((1, C), F32), jax.ShapeDtypeStruct((1, C), F32)],
                          compiler_params=_params(dimension_semantics=("arbitrary",)), name=name)(
        y0, proj, dvec, glu_w, glu_b, dcat)


def _dot_tri(tri, x, tri_left=True):
    t = tri.astype(BF16)
    x1 = x.astype(BF16)
    r1 = x - x1.astype(F32)
    x2 = r1.astype(BF16)
    x3 = (r1 - x2.astype(F32)).astype(BF16)
    dot = (lambda p: jnp.dot(t, p, preferred_element_type=F32)) if tri_left else (
        lambda p: jnp.dot(p, t, preferred_element_type=F32))
    return dot(x1) + dot(x2) + dot(x3)


def _hg_gates(xq, xf, lb, tri):
    C = xq.shape[0]
    sq = _sigmoid(xq)
    q = xq * sq
    sg = _sigmoid(xf)
    f = lb + (1.0 - lb) * sg
    kk = 1.0 - f
    b = _dot_tri(tri, jnp.log(f))
    bm = b[C // 2 - 1:C // 2, :]
    bl = b[C - 1:C, :]
    eb = jnp.exp(b)
    eqm, ekm, ekl = jnp.exp(b - bm), jnp.exp(bm - b), jnp.exp(bl - b)
    return dict(sq=sq, q=q, sg=sg, f=f, kk=kk, eb=eb, ebl=jnp.exp(bl), eqm=eqm, ekm=ekm, ekl=ekl,
                qb=q * eb, qt=q * eqm, kt=kk * ekm, kh=kk * ekl)


def _tri(C, lower):
    r = lax.broadcasted_iota(jnp.int32, (C, C), 0)
    c = lax.broadcasted_iota(jnp.int32, (C, C), 1)
    return (r >= c) if lower else (c >= r)


def hgrn_fwd(proj, lb, norm_g, cat, *, name):
    L = proj.shape[0]
    C, H, K = HG_CHUNK, HG_HEADS, HG_DIM
    HK = H * K
    nc = L // C

    def body(q_ref, f_ref, i_ref, g_ref, lb_ref, ng_ref, cat_ref, o_ref, sall_ref, st_ref):
        @pl.when(pl.program_id(0) == 0)
        def _():
            st_ref[...] = jnp.zeros_like(st_ref)

        mask = _tri(C, True)
        sts = [st_ref[h] for h in range(H)]
        for s in range(S):
            rs = slice(s * C, (s + 1) * C)
            gt = _hg_gates(q_ref[rs, :], f_ref[rs, :], lb_ref[...], mask.astype(F32))
            v_all = i_ref[rs, :]
            outs = []
            for h in range(H):
                sl = slice(h * K, (h + 1) * K)
                v, st = v_all[:, sl], sts[h]
                sall_ref[s, h] = st
                att = jnp.where(mask, _dot_nt(gt["qt"][:, sl], gt["kt"][:, sl]), 0.0)
                o = _dot(att, v) + _dot_nt(gt["qb"][:, sl], st)
                sts[h] = st * gt["ebl"][:, sl] + _dot_tn(v, gt["kh"][:, sl])
                outs.append(o * lax.rsqrt(jnp.mean(o * o, axis=-1, keepdims=True) + NORM_EPS))
            xg = g_ref[rs, :]
            o_ref[rs, :] = (jnp.concatenate(outs, axis=1) * ng_ref[...] * (xg * _sigmoid(xg))).astype(o_ref.dtype)
        for h in range(H):
            st_ref[h] = sts[h]

    S = HG_STEP_CHUNKS

    def blk(cb):
        return pl.BlockSpec((S * C, HK), lambda i: (i, cb))

    vec = pl.BlockSpec((1, HK), lambda i: (0, 0))
    return pl.pallas_call(
        body, grid=(nc // S,), in_specs=[blk(1), blk(2), blk(3), blk(4), vec, vec, pl.BlockSpec(memory_space=pl.ANY)],
        out_specs=[pl.BlockSpec((S * C, HK), lambda i: (i, 1)), pl.BlockSpec((S, H, K, K), lambda i: (i, 0, 0, 0))],
        out_shape=[jax.ShapeDtypeStruct((L, 2 * HK), BF16), jax.ShapeDtypeStruct((nc, H, K, K), F32)],
        input_output_aliases={6: 0},
        scratch_shapes=[pltpu.VMEM((H, K, K), F32)],
        compiler_params=_params(dimension_semantics=("arbitrary",)), name=name,
    )(proj, proj, proj, proj, lb, norm_g, cat)


def hgrn_bwd(proj, lb, norm_g, sall, dcat, du, *, name):
    L = proj.shape[0]
    C, H, K = HG_CHUNK, HG_HEADS, HG_DIM
    HK = H * K
    nc = L // C

    def body(q_ref, f_ref, i_ref, g_ref, lb_ref, ng_ref, sall_ref, do_ref, du_ref, dx_ref, dlb_ref, dng_ref, dst_ref):
        @pl.when(pl.program_id(0) == 0)
        def _():
            dst_ref[...] = jnp.zeros_like(dst_ref)
            dlb_ref[...] = jnp.zeros_like(dlb_ref)
            dng_ref[...] = jnp.zeros_like(dng_ref)

        mask = _tri(C, True)
        lb_all, ng = lb_ref[...], ng_ref[...]
        dx_ref[:, 0:HK] = du_ref[...]
        dsts = [dst_ref[h] for h in range(H)]
        for s in reversed(range(S)):
            rs = slice(s * C, (s + 1) * C)
            dsts = chunk_bwd(rs, s, dsts, mask, lb_all, ng, q_ref, f_ref, i_ref, g_ref, sall_ref, do_ref,
                             dx_ref, dlb_ref, dng_ref)
        for h in range(H):
            dst_ref[h] = dsts[h]

    def chunk_bwd(rs, s, dsts, mask, lb_all, ng, q_ref, f_ref, i_ref, g_ref, sall_ref, do_ref, dx_ref, dlb_ref, dng_ref):
        xq, xg, v_all = q_ref[rs, :], g_ref[rs, :], i_ref[rs, :]
        gt = _hg_gates(xq, f_ref[rs, :], lb_all, mask.astype(F32))
        sgg = _sigmoid(xg)
        d_ob = do_ref[rs, :]
        d_on = d_ob * (xg * sgg)
        doh = d_on * ng
        ohs, d_qts, d_qbs, d_kts, d_khs, dvs, d_bls, new_dsts = [], [], [], [], [], [], [], []
        for h in range(H):
            sl = slice(h * K, (h + 1) * K)
            v, st, dst = v_all[:, sl], sall_ref[s, h], dsts[h]
            qt, kt, kh, qb = gt["qt"][:, sl], gt["kt"][:, sl], gt["kh"][:, sl], gt["qb"][:, sl]
            att = jnp.where(mask, _dot_nt(qt, kt), 0.0)
            o = _dot(att, v) + _dot_nt(qb, st)
            r = lax.rsqrt(jnp.mean(o * o, axis=-1, keepdims=True) + NORM_EPS)
            oh = o * r
            do = r * (doh[:, sl] - oh * jnp.mean(doh[:, sl] * oh, axis=-1, keepdims=True))
            datt = jnp.where(mask, _dot_nt(do, v), 0.0)
            dvs.append(_dot_tn(att, do) + _dot_nt(kh, dst))
            d_qbs.append(_dot_x3(do, st))
            d_qts.append(_dot_x3(datt, kt))
            d_kts.append(_dot_x3(datt, qt, ((0,), (0,))))
            d_kh = _dot_x3(v, dst)
            d_khs.append(d_kh)
            d_bls.append(jnp.sum(dst * st, axis=0, keepdims=True) * gt["ebl"][:, sl]
                         + jnp.sum(d_kh * kh, axis=0, keepdims=True))
            new_dsts.append(dst * gt["ebl"][:, sl] + _dot_tn(do, qb))
            ohs.append(oh)
        oh, d_qt, d_qb, d_kt, d_kh, dv, d_bl = (jnp.concatenate(p, axis=1) for p in
                                                (ohs, d_qts, d_qbs, d_kts, d_khs, dvs, d_bls))
        dxg = d_ob * (oh * ng) * (sgg * (1.0 + xg * (1.0 - sgg)))
        dng_ref[...] += jnp.sum(d_on * oh, axis=0, keepdims=True)
        dq = d_qt * gt["eqm"] + d_qb * gt["eb"]
        db = d_qt * gt["qt"] + d_qb * gt["qb"] - d_kt * gt["kt"] - d_kh * gt["kh"]
        rowi = lax.broadcasted_iota(jnp.int32, (C, HK), 0)
        db = db + jnp.where(rowi == C - 1, d_bl, 0.0)
        dkk = d_kt * gt["ekm"] + d_kh * gt["ekl"]
        dlg = _dot_tri(_tri(C, False).astype(F32), db)
        df = dlg / gt["f"] - dkk
        sg, sq = gt["sg"], gt["sq"]
        dlb_ref[...] += jnp.sum(df * (1.0 - sg), axis=0, keepdims=True)
        dx_ref[rs, HK:2 * HK] = (dq * (sq * (1.0 + xq * (1.0 - sq)))).astype(dx_ref.dtype)
        dx_ref[rs, 2 * HK:3 * HK] = (df * (1.0 - lb_all) * sg * (1.0 - sg)).astype(dx_ref.dtype)
        dx_ref[rs, 3 * HK:4 * HK] = dv.astype(dx_ref.dtype)
        dx_ref[rs, 4 * HK:5 * HK] = dxg.astype(dx_ref.dtype)
        return new_dsts

    S = HG_STEP_CHUNKS
    ns = nc // S

    def blk(cb):
        return pl.BlockSpec((S * C, HK), lambda i: (ns - 1 - i, cb))

    vec = pl.BlockSpec((1, HK), lambda i: (0, 0))
    return pl.pallas_call(
        body, grid=(ns,),
        in_specs=[blk(1), blk(2), blk(3), blk(4), vec, vec,
                  pl.BlockSpec((S, H, K, K), lambda i: (ns - 1 - i, 0, 0, 0)), blk(1), blk(0)],
        out_specs=[pl.BlockSpec((S * C, 5 * HK), lambda i: (ns - 1 - i, 0)), vec, vec],
        out_shape=[jax.ShapeDtypeStruct((L, 5 * HK), BF16), jax.ShapeDtypeStruct((1, HK), F32),
                   jax.ShapeDtypeStruct((1, HK), F32)],
        scratch_shapes=[pltpu.VMEM((H, K, K), F32)],
        compiler_params=_params(dimension_semantics=("arbitrary",)), name=name,
    )(proj, proj, proj, proj, lb, norm_g, sall, dcat, du)


def _shift_down(x, k, row):
    return jnp.where(row >= k, pltpu.roll(x, k, 0), 0.0)


def _shift_up(x, k, row):
    n = x.shape[0]
    return jnp.where(row < n - k, pltpu.roll(x, n - k, 0), 0.0)


def convgate_fwd(hu, conv_w, conv_b, *, name):
    L, C2 = hu.shape
    C = C2 // 2
    tc = _pick(C, (256, 128))
    nb = C // tc

    def body(a_ref, b_ref, wa_ref, wb_ref, ba_ref, bb_ref, o_ref):
        row = lax.broadcasted_iota(jnp.int32, (L, tc), 0)

        def conv(x, w, bias):
            return w[2:3, :] * x + w[1:2, :] * _shift_down(x, 1, row) + w[0:1, :] * _shift_down(x, 2, row) + bias

        ca = conv(a_ref[...], wa_ref[...], ba_ref[...])
        cb = conv(b_ref[...], wb_ref[...], bb_ref[...])
        o_ref[...] = (ca * _sigmoid(ca) * cb).astype(o_ref.dtype)

    def col(off, rows):
        return pl.BlockSpec((rows, tc), lambda j: (0, j + off))

    return pl.pallas_call(
        body, grid=(nb,), in_specs=[col(0, L), col(nb, L), col(0, 3), col(nb, 3), col(0, 1), col(nb, 1)],
        out_specs=col(0, L), out_shape=jax.ShapeDtypeStruct((L, C), BF16),
        compiler_params=_params(dimension_semantics=("parallel",)), name=name,
    )(hu, hu, conv_w, conv_w, conv_b, conv_b)


def convgate_bwd(hu, conv_w, conv_b, dact, *, name):
    L, C2 = hu.shape
    C = C2 // 2
    tc = _pick(C, (256, 128))
    nb = C // tc

    def body(a_ref, b_ref, wa_ref, wb_ref, ba_ref, bb_ref, d_ref, dxa_ref, dxb_ref, dwa_ref, dwb_ref, dba_ref, dbb_ref):
        row = lax.broadcasted_iota(jnp.int32, (L, tc), 0)

        def conv(x, w, bias):
            x1 = _shift_down(x, 1, row)
            x2 = _shift_down(x, 2, row)
            return w[2:3, :] * x + w[1:2, :] * x1 + w[0:1, :] * x2 + bias, x1, x2

        xa, xb = a_ref[...], b_ref[...]
        wa, wb = wa_ref[...], wb_ref[...]
        ca, xa1, xa2 = conv(xa, wa, ba_ref[...])
        cb, xb1, xb2 = conv(xb, wb, bb_ref[...])
        d = d_ref[...]
        sa = _sigmoid(ca)
        dca = d * cb * (sa * (1.0 + ca * (1.0 - sa)))
        dcb = d * (ca * sa)

        def back(dc, w, x, x1, x2, dx_ref, dw_ref, db_ref):
            dx = w[2:3, :] * dc + w[1:2, :] * _shift_up(dc, 1, row) + w[0:1, :] * _shift_up(dc, 2, row)
            dx_ref[...] = dx.astype(dx_ref.dtype)
            dw_ref[...] = jnp.concatenate([jnp.sum(dc * x2, axis=0, keepdims=True),
                                           jnp.sum(dc * x1, axis=0, keepdims=True),
                                           jnp.sum(dc * x, axis=0, keepdims=True)], axis=0)
            db_ref[...] = jnp.sum(dc, axis=0, keepdims=True)

        back(dca, wa, xa, xa1, xa2, dxa_ref, dwa_ref, dba_ref)
        back(dcb, wb, xb, xb1, xb2, dxb_ref, dwb_ref, dbb_ref)

    def col(off, rows):
        return pl.BlockSpec((rows, tc), lambda j: (0, j + off))

    outs = pl.pallas_call(
        body, grid=(nb,),
        in_specs=[col(0, L), col(nb, L), col(0, 3), col(nb, 3), col(0, 1), col(nb, 1), col(0, L)],
        out_specs=[col(0, L), col(0, L), col(0, 3), col(0, 3), col(0, 1), col(0, 1)],
        out_shape=[jax.ShapeDtypeStruct((L, C), BF16)] * 2 + [jax.ShapeDtypeStruct((3, C), F32)] * 2
        + [jax.ShapeDtypeStruct((1, C), F32)] * 2,
        compiler_params=_params(dimension_semantics=("parallel",)), name=name,
    )(hu, hu, conv_w, conv_w, conv_b, conv_b, dact)
    dxa, dxb, dwa, dwb, dba, dbb = outs
    return (dxa, dxb), jnp.concatenate([dwa, dwb], axis=1), jnp.concatenate([dba, dbb], axis=1)


def rope_tables(positions):
    half = ROT_DIM // 2
    inv_freq = ROPE_THETA ** (-jnp.arange(half, dtype=F32) * 2.0 / ROT_DIM)
    ang = positions.astype(F32)[:, None] * inv_freq
    cos, sin = jnp.cos(ang), jnp.sin(ang)
    L = positions.shape[0]
    one = jnp.ones((L, ATT_E - ROT_DIM), F32)
    zero = jnp.zeros((L, ATT_E - ROT_DIM), F32)
    zh = jnp.zeros((L, half), F32)
    tc = jnp.concatenate([cos, cos, one], axis=1)
    ts1 = jnp.concatenate([zh, sin, zero], axis=1)
    ts2 = jnp.concatenate([-sin, zh, zero], axis=1)
    return tuple(jnp.concatenate([t, t], axis=1) for t in (tc, ts1, ts2))


def norm_mm(x, g, w_t, *, tabs=None, name):
    L, D = x.shape
    N = w_t.shape[0]
    W = 512
    tm = _pick(L, (2048, 1024, 512, 256, 128))
    nq = N // (3 * W)
    scale = ATT_E ** -0.5
    rope = tabs is not None

    def body(x_ref, g_ref, b_ref, *rest):
        hn_ref, o_ref, hn_scr = rest[-3:]
        j = pl.program_id(1)

        @pl.when(j == 0)
        def _():
            xv = x_ref[...]
            r = lax.rsqrt(jnp.mean(xv * xv, axis=-1, keepdims=True) + NORM_EPS)
            hn = (xv * r * g_ref[...]).astype(BF16)
            hn_scr[...] = hn
            hn_ref[...] = hn

        out = _dot_nt(hn_scr[...], b_ref[...])
        if rope:
            c_ref, s1_ref, s2_ref = rest[:3]
            c = jnp.concatenate([c_ref[...]] * 4, axis=1)
            s1 = jnp.concatenate([s1_ref[...]] * 4, axis=1)
            s2 = jnp.concatenate([s2_ref[...]] * 4, axis=1)
            rot = out * c + pltpu.roll(out, 8, 1) * s1 + pltpu.roll(out, W - 8, 1) * s2
            out = jnp.where(j < 2 * nq, rot * jnp.where(j < nq, scale, 1.0), out)
        o_ref[...] = out

    row = pl.BlockSpec((tm, D), lambda i, j: (i, 0))
    tab = pl.BlockSpec((tm, 128), lambda i, j: (i, 0))
    return pl.pallas_call(body, grid=(L // tm, N // W),
                          in_specs=[row, pl.BlockSpec((1, D), lambda i, j: (0, 0)), pl.BlockSpec((W, D), lambda i, j: (j, 0))]
                          + ([tab, tab, tab] if rope else []),
                          out_specs=[row, pl.BlockSpec((tm, W), lambda i, j: (i, j))],
                          out_shape=[jax.ShapeDtypeStruct((L, D), BF16), jax.ShapeDtypeStruct((L, N), F32)],
                          scratch_shapes=[pltpu.VMEM((tm, D), BF16)],
                          compiler_params=_params(dimension_semantics=("parallel", "arbitrary")), name=name)(
        x, g.reshape(1, D), w_t, *(tabs or ()))


def rope_bwd(slabs, tabs, *, name):
    L, W = slabs[0].shape
    tr = _pick(L, (256, 128))
    nq = len(slabs) // 3
    scale = ATT_E ** -0.5

    def body(*refs):
        d_refs, (c_ref, s1_ref, s2_ref, o_ref) = refs[:3 * nq], refs[3 * nq:]
        c = jnp.concatenate([c_ref[...]] * 4, axis=1)
        s1 = jnp.concatenate([s1_ref[...]] * 4, axis=1)
        s2 = jnp.concatenate([s2_ref[...]] * 4, axis=1)
        for j, d_ref in enumerate(d_refs):
            dy = d_ref[...]
            if j < 2 * nq:
                dy = dy * c + pltpu.roll(dy * s1, W - 8, 1) + pltpu.roll(dy * s2, 8, 1)
            if j < nq:
                dy = dy * scale
            o_ref[:, j * W:(j + 1) * W] = dy.astype(o_ref.dtype)

    slab = pl.BlockSpec((tr, W), lambda i: (i, 0))
    tab = pl.BlockSpec((tr, 128), lambda i: (i, 0))
    return pl.pallas_call(body, grid=(L // tr,), in_specs=[slab] * (3 * nq) + [tab, tab, tab],
                          out_specs=pl.BlockSpec((tr, 3 * nq * W), lambda i: (i, 0)),
                          out_shape=jax.ShapeDtypeStruct((L, 3 * nq * W), BF16),
                          compiler_params=_params(dimension_semantics=("parallel",)), name=name)(*slabs, *tabs)


def _att_masks(has_prev):
    qi = lax.broadcasted_iota(jnp.int32, (ATT_BLOCK, ATT_BLOCK), 0)
    kj = lax.broadcasted_iota(jnp.int32, (ATT_BLOCK, ATT_BLOCK), 1)
    return qi >= kj, (kj >= qi) & has_prev


ATT_COLS = 128


def _att_rows(j, d, nb):
    B = ATT_BLOCK
    r, n = j // nb, j % nb
    start = r + d * B * n
    has_prev = n > 0
    pstart = jnp.where(has_prev, start - d * B, start)
    if d == 1:
        return pl.ds(pl.multiple_of(start, B), B), pl.ds(pl.multiple_of(pstart, B), B), has_prev
    return pl.ds(start, B, stride=d), pl.ds(pstart, B, stride=d), has_prev


def _qkv_specs(L, g):
    per = ATT_HPG * ATT_E // ATT_COLS
    third = len(ATT_DILATIONS) * per
    return [pl.BlockSpec((L, ATT_COLS), lambda c, base=base: (0, base + c))
            for base in (g * per, third + g * per, 2 * third + g * per)]


def attn_fwd(qkv, g, d, *, name):
    L, W = qkv.shape[0], ATT_HPG * ATT_E
    B, E = ATT_BLOCK, ATT_E
    nblk = L // B
    nb = nblk // d

    def body(q_ref, k_ref, v_ref, o_ref, l_ref):
        def step(j, carry):
            cur, prv, has_prev = _att_rows(j, d, nb)
            mc, mp = _att_masks(has_prev)
            qb, kc, kp, vc, vp = q_ref[cur, :], k_ref[cur, :], k_ref[prv, :], v_ref[cur, :], v_ref[prv, :]
            outs, lses = [], []
            for h in range(ATT_COLS // E):
                sl = slice(h * E, (h + 1) * E)
                sc = jnp.where(mc, _dot_nt(qb[:, sl], kc[:, sl]), NEG_BIG)
                sp = jnp.where(mp, _dot_nt(qb[:, sl], kp[:, sl]), NEG_BIG)
                m = jnp.maximum(jnp.max(sc, axis=-1, keepdims=True), jnp.max(sp, axis=-1, keepdims=True))
                pc = jnp.exp(sc - m)
                pp = jnp.exp(sp - m)
                den = jnp.sum(pc, axis=-1, keepdims=True) + jnp.sum(pp, axis=-1, keepdims=True)
                outs.append((_dot(pc, vc[:, sl]) + _dot(pp, vp[:, sl])) / den)
                lses.append(jnp.broadcast_to(m + jnp.log(den), (B, E)))
            o_ref[cur, :] = jnp.concatenate(outs, axis=1)
            l_ref[cur, :] = jnp.concatenate(lses, axis=1)
            return carry

        lax.fori_loop(0, nblk, step, 0, unroll=4)

    col = pl.BlockSpec((L, ATT_COLS), lambda c: (0, c))
    return pl.pallas_call(body, grid=(W // ATT_COLS,), in_specs=_qkv_specs(L, g), out_specs=[col] * 2,
                          out_shape=[jax.ShapeDtypeStruct((L, W), F32)] * 2,
                          compiler_params=_params(dimension_semantics=("parallel",)), name=name)(qkv, qkv, qkv)


def attn_bwd(qkv, g, lse, do, dl, d, *, name):
    L, W = qkv.shape[0], ATT_HPG * ATT_E
    B, E = ATT_BLOCK, ATT_E
    nblk = L // B
    nb = nblk // d

    def body(q_ref, k_ref, v_ref, l_ref, do_ref, dl_ref, dq_ref, dk_ref, dv_ref):
        dk_ref[...] = jnp.zeros_like(dk_ref)
        dv_ref[...] = jnp.zeros_like(dv_ref)

        def step(j, carry):
            cur, prv, has_prev = _att_rows(j, d, nb)
            mc, mp = _att_masks(has_prev)
            qb, kc, kp, vc, vp = q_ref[cur, :], k_ref[cur, :], k_ref[prv, :], v_ref[cur, :], v_ref[prv, :]
            lb, dob, dlb = l_ref[cur, :], do_ref[cur, :], dl_ref[cur, :]
            dqs, dkc, dkp, dvc, dvp = [], [], [], [], []
            for h in range(ATT_COLS // E):
                sl = slice(h * E, (h + 1) * E)
                qh, doh = qb[:, sl], dob[:, sl]
                lse_h, dl_h = lb[:, h * E:h * E + 1], dlb[:, h * E:h * E + 1]
                pc = jnp.where(mc, jnp.exp(_dot_nt(qh, kc[:, sl]) - lse_h), 0.0)
                pp = jnp.where(mp, jnp.exp(_dot_nt(qh, kp[:, sl]) - lse_h), 0.0)
                dsc = pc * (_dot_nt(doh, vc[:, sl]) - dl_h)
                dsp = pp * (_dot_nt(doh, vp[:, sl]) - dl_h)
                dqs.append(_dot(dsc, kc[:, sl]) + _dot(dsp, kp[:, sl]))
                dkc.append(_dot_tn(dsc, qh))
                dkp.append(_dot_tn(dsp, qh))
                dvc.append(_dot_tn(pc, doh))
                dvp.append(_dot_tn(pp, doh))
            dq_ref[cur, :] = jnp.concatenate(dqs, axis=1)
            dk_ref[cur, :] = dk_ref[cur, :] + jnp.concatenate(dkc, axis=1)
            dv_ref[cur, :] = dv_ref[cur, :] + jnp.concatenate(dvc, axis=1)
            dk_ref[prv, :] = dk_ref[prv, :] + jnp.concatenate(dkp, axis=1)
            dv_ref[prv, :] = dv_ref[prv, :] + jnp.concatenate(dvp, axis=1)
            return carry

        lax.fori_loop(0, nblk, step, 0, unroll=4)

    col = pl.BlockSpec((L, ATT_COLS), lambda c: (0, c))
    return pl.pallas_call(body, grid=(W // ATT_COLS,), in_specs=_qkv_specs(L, g) + [col] * 3, out_specs=[col] * 3,
                          out_shape=[jax.ShapeDtypeStruct((L, W), F32)] * 3,
                          compiler_params=_params(dimension_semantics=("parallel",)), name=name)(
        qkv, qkv, qkv, lse, do, dl)


def _merge_alpha(l_refs):
    ls = [r[...] for r in l_refs]
    m = jnp.maximum(jnp.maximum(ls[0], ls[1]), ls[2])
    es = [jnp.exp(l - m) for l in ls]
    den = es[0] + es[1] + es[2]
    return [e / den for e in es]


def merge_fwd(os_, ls_, *, name):
    L, W = os_[0].shape
    tr = _pick(L, (256, 128))

    def body(o0, o1, o2, l0, l1, l2, out_ref):
        al = _merge_alpha((l0, l1, l2))
        out_ref[...] = (al[0] * o0[...] + al[1] * o1[...] + al[2] * o2[...]).astype(out_ref.dtype)

    row = pl.BlockSpec((tr, W), lambda i: (i, 0))
    return pl.pallas_call(body, grid=(L // tr,), in_specs=[row] * 6, out_specs=row,
                          out_shape=jax.ShapeDtypeStruct((L, W), BF16), name=name)(*os_, *ls_)


def merge_bwd(os_, ls_, do, *, name):
    L, W = do.shape
    tr = _pick(L, (256, 128))

    def body(o0, o1, o2, l0, l1, l2, do_ref, d0, d1, d2, e0, e1, e2):
        al = _merge_alpha((l0, l1, l2))
        dov = do_ref[...]
        r = lax.broadcasted_iota(jnp.int32, (W, W), 0) // ATT_E
        c = lax.broadcasted_iota(jnp.int32, (W, W), 1) // ATT_E
        ones_blk = (r == c).astype(F32)
        t = jnp.zeros_like(dov)
        for a, o in zip(al, (o0, o1, o2)):
            t = t + a * _dot_tri(ones_blk, dov * o[...], tri_left=False)
        for a, d_ref, e_ref in zip(al, (d0, d1, d2), (e0, e1, e2)):
            d_ref[...] = a * dov
            e_ref[...] = a * t

    row = pl.BlockSpec((tr, W), lambda i: (i, 0))
    return pl.pallas_call(body, grid=(L // tr,), in_specs=[row] * 7, out_specs=[row] * 6,
                          out_shape=[jax.ShapeDtypeStruct((L, W), F32)] * 6, name=name)(*os_, *ls_, do)


def _me_and_peers():
    x, y, c = lax.axis_index("x"), lax.axis_index("y"), lax.axis_index("c")
    peers = []
    for k in range(1, N_DEV):
        px = 1 - x if k & 4 else x
        py = 1 - y if k & 2 else y
        pc = 1 - c if k & 1 else c
        peers.append((px, py, pc))
    return (x, y, c), peers


def _index(dev):
    return 4 * dev[0] + 2 * dev[1] + dev[2]


def _hbm(a):
    return pltpu.with_memory_space_constraint(a, pltpu.HBM)


HBM_SPEC = pl.BlockSpec(memory_space=pltpu.HBM)
SEM_SPEC = pl.BlockSpec(memory_space=pltpu.SEMAPHORE)
DATAFLOW = pltpu.SideEffectType.DATAFLOW_SIDE_EFFECTING


def _remote(src_ref, land_ref, slotted, me, peer, src_is_mine, send_sem, recv_sem, k):
    sender, receiver = (me, peer) if src_is_mine else (peer, me)
    src = src_ref.at[_index(receiver)] if slotted else src_ref
    return pltpu.make_async_remote_copy(src_ref=src, dst_ref=land_ref.at[_index(sender)], send_sem=send_sem.at[k],
                                        recv_sem=recv_sem.at[k], device_id=peer, device_id_type=MESH_ID)


SIBLING = 0
SAME_CORE = (1, 3, 5)
OTHER_CORE = (2, 4, 6)


def copies_start(arrays, mode, *, name):
    n = len(arrays)
    slotted = mode == "exchange"
    lands = [lax.empty(a.shape if slotted else (N_DEV,) + a.shape, a.dtype) for a in arrays]
    targets = (SIBLING,) + SAME_CORE if mode == "gather2" else tuple(range(N_DEV - 1))

    def body(*refs):
        x_refs, land_refs = refs[:n], refs[n:2 * n]
        send, recv = refs[2 * n:3 * n], refs[3 * n:4 * n]
        token = refs[-1]
        me, peers = _me_and_peers()
        for w in range(n):
            for k in targets:
                _remote(x_refs[w], land_refs[w], slotted, me, peers[k], True, send[w], recv[w], k).start()
            if not slotted:
                pltpu.make_async_copy(x_refs[w], land_refs[w].at[_index(me)], recv[w].at[N_DEV - 1]).start()
        token[...] = jnp.zeros_like(token)

    sem = pltpu.SemaphoreType.DMA((N_DEV,))
    out_shape = ([sem] * (2 * n) + [pltpu.HBM(a.shape, a.dtype) for a in arrays]
                 + [pltpu.HBM(l.shape, l.dtype) for l in lands] + [jax.ShapeDtypeStruct((8, 128), F32)])
    outs = pl.pallas_call(
        body, name=name, out_shape=out_shape, in_specs=[HBM_SPEC] * (2 * n),
        out_specs=[SEM_SPEC] * (2 * n) + [HBM_SPEC] * (2 * n) + [pl.BlockSpec(memory_space=pltpu.VMEM)],
        input_output_aliases={i: 2 * n + i for i in range(2 * n)},
        compiler_params=pltpu.CompilerParams(has_side_effects=DATAFLOW),
    )(*[_hbm(a) for a in arrays], *[_hbm(l) for l in lands])
    handles = [(outs[w], outs[n + w], outs[2 * n + w], outs[3 * n + w]) for w in range(n)]
    return handles, outs[-1]


def _forward(land_ref, me, peers, j, fsend, frecv, mine):
    block = _index(peers[SAME_CORE[j]] if mine else peers[OTHER_CORE[j]])
    return pltpu.make_async_remote_copy(src_ref=land_ref.at[block], dst_ref=land_ref.at[block], send_sem=fsend.at[j],
                                        recv_sem=frecv.at[j], device_id=peers[SIBLING], device_id_type=MESH_ID)


def copies_forward(handles, after, *, name):
    n = len(handles)

    def body(*refs):
        land_refs, recv = refs[:n], refs[n:2 * n]
        fsend, frecv = refs[2 * n + 1:3 * n + 1], refs[3 * n + 1:4 * n + 1]
        token = refs[-1]
        me, peers = _me_and_peers()
        for w in range(n):
            for j, k in enumerate(SAME_CORE):
                block = land_refs[w].at[_index(peers[k])]
                pltpu.make_async_remote_copy(src_ref=block, dst_ref=block, send_sem=recv[w].at[N_DEV - 1],
                                             recv_sem=recv[w].at[k], device_id=peers[k], device_id_type=MESH_ID).wait_recv()
                _forward(land_refs[w], me, peers, j, fsend[w], frecv[w], True).start()
        token[...] = jnp.zeros_like(token)

    sem = pltpu.SemaphoreType.DMA((len(SAME_CORE),))
    lands = [h[3] for h in handles]
    outs = pl.pallas_call(
        body, name=name,
        out_shape=[sem] * (2 * n) + [pltpu.HBM(l.shape, l.dtype) for l in lands] + [jax.ShapeDtypeStruct((8, 128), F32)],
        in_specs=[HBM_SPEC] * n + [SEM_SPEC] * n + [pl.BlockSpec(memory_space=pl.ANY)],
        out_specs=[SEM_SPEC] * (2 * n) + [HBM_SPEC] * n + [pl.BlockSpec(memory_space=pltpu.VMEM)],
        input_output_aliases={w: 2 * n + w for w in range(n)},
        compiler_params=pltpu.CompilerParams(has_side_effects=DATAFLOW),
    )(*lands, *[h[1] for h in handles], after)
    new = [(h[0], h[1], h[2], outs[2 * n + w], outs[w], outs[n + w]) for w, h in enumerate(handles)]
    return new, outs[-1]


def copies_wait(handle, mode, after, *, name):
    slotted = mode == "exchange"
    two_level = mode == "gather2"
    send_sem, recv_sem, x_thru, land_thru = handle[:4]
    targets = (SIBLING,) + SAME_CORE if two_level else tuple(range(N_DEV - 1))
    arrivals = (SIBLING,) if two_level else targets

    def body(x_ref, land_ref, send_ref, recv_ref, *rest):
        me, peers = _me_and_peers()
        for k in targets:
            _remote(x_ref, land_ref, slotted, me, peers[k], True, send_ref, recv_ref, k).wait_send()
        for k in arrivals:
            _remote(x_ref, land_ref, slotted, me, peers[k], False, send_ref, recv_ref, k).wait_recv()
        if not slotted:
            pltpu.make_async_copy(x_ref, land_ref.at[_index(me)], recv_ref.at[N_DEV - 1]).wait()
        if two_level:
            fsend, frecv = rest[0], rest[1]
            for j in range(len(SAME_CORE)):
                _forward(land_ref, me, peers, j, fsend, frecv, True).wait_send()
                _forward(land_ref, me, peers, j, fsend, frecv, False).wait_recv()

    extra = list(handle[4:])
    return pl.pallas_call(
        body, name=name, out_shape=(pltpu.HBM(x_thru.shape, x_thru.dtype), pltpu.HBM(land_thru.shape, land_thru.dtype)),
        in_specs=[HBM_SPEC, HBM_SPEC, SEM_SPEC, SEM_SPEC] + [SEM_SPEC] * len(extra) + [pl.BlockSpec(memory_space=pl.ANY)],
        out_specs=(HBM_SPEC, HBM_SPEC), input_output_aliases={0: 0, 1: 1},
        compiler_params=pltpu.CompilerParams(has_side_effects=DATAFLOW),
    )(x_thru, land_thru, send_sem, recv_sem, *extra, after)


def cast_bf16(x, *, dep=None, name):
    R, C = x.shape
    tr = _pick(R, (512, 352, 256, 128, 64))
    deps = [] if dep is None else [dep]

    def body(x_ref, *rest):
        rest[-1][...] = x_ref[...].astype(BF16)

    row = pl.BlockSpec((tr, C), lambda i: (i, 0))
    return pl.pallas_call(body, grid=(R // tr,), in_specs=[row] + [pl.BlockSpec((8, 128), lambda i: (0, 0))] * len(deps),
                          out_specs=row, out_shape=jax.ShapeDtypeStruct((R, C), BF16), name=name)(x, *deps)


def cast_bf16_layer(x3, layer, *, name):
    _, R, C = x3.shape
    tr = _pick(R, (512, 352, 256, 128, 64))

    def body(x_ref, o_ref):
        o_ref[...] = x_ref[...].astype(BF16)

    return pl.pallas_call(body, grid=(R // tr,), in_specs=[pl.BlockSpec((None, tr, C), lambda i: (layer, i, 0))],
                          out_specs=pl.BlockSpec((tr, C), lambda i: (i, 0)),
                          out_shape=jax.ShapeDtypeStruct((R, C), BF16), name=name)(x3)


BD_PARTS = 4


def _blockdiag_call(b, build, G, r, c, name):
    gp = G // BD_PARTS

    def body_build(b_ref, o_ref):
        o_ref[...] = jnp.zeros_like(o_ref)
        for g in range(G):
            o_ref[g // gp, (g % gp) * r:(g % gp + 1) * r, (g % gp) * c:(g % gp + 1) * c] = b_ref[g]

    def body_extract(d_ref, o_ref):
        for g in range(G):
            o_ref[g] = d_ref[g // gp, (g % gp) * r:(g % gp + 1) * r, (g % gp) * c:(g % gp + 1) * c]

    out = jax.ShapeDtypeStruct((BD_PARTS, gp * r, gp * c) if build else (G, r, c), F32)
    return pl.pallas_call(body_build if build else body_extract, out_shape=out, name=name)(b)


def make_blockdiag(G, r, c, name):
    @jax.custom_vjp
    def blockdiag(b):
        return _blockdiag_call(b, True, G, r, c, name + "_build")

    def fwd(b):
        return blockdiag(b), None

    def bwd(_, g):
        return (_blockdiag_call(g, False, G, r, c, name + "_extract"),)

    blockdiag.defvjp(fwd, bwd)
    return blockdiag


def cols_from_shards(g, *, name):
    _, K, n = g.shape
    tk = _pick(K, (256, 128))

    def body(g_ref, o_ref):
        for i in range(N_DEV):
            o_ref[:, i * n:(i + 1) * n] = g_ref[i]

    return pl.pallas_call(body, grid=(K // tk,), in_specs=[pl.BlockSpec((N_DEV, tk, n), lambda i: (0, i, 0))],
                          out_specs=pl.BlockSpec((tk, N_DEV * n), lambda i: (i, 0)),
                          out_shape=jax.ShapeDtypeStruct((K, N_DEV * n), g.dtype), name=name)(g)


def shards_from_cols(w, *, name):
    K, N = w.shape
    n = N // N_DEV
    tk = _pick(K, (256, 128))

    def body(w_ref, o_ref):
        for i in range(N_DEV):
            o_ref[i] = w_ref[:, i * n:(i + 1) * n].astype(o_ref.dtype)

    return pl.pallas_call(body, grid=(K // tk,), in_specs=[pl.BlockSpec((tk, N), lambda i: (i, 0))],
                          out_specs=pl.BlockSpec((N_DEV, tk, n), lambda i: (0, i, 0)),
                          out_shape=jax.ShapeDtypeStruct((N_DEV, K, n), BF16), name=name)(w)


def _adamw(w, g, m, v):
    m = ADAM_B1 * m + (1.0 - ADAM_B1) * g
    v = ADAM_B2 * v + (1.0 - ADAM_B2) * (g * g)
    m_hat = m / (1.0 - ADAM_B1 ** ADAM_STEP)
    v_hat = v / (1.0 - ADAM_B2 ** ADAM_STEP)
    delta = -ADAM_LR * (m_hat / (jnp.sqrt(v_hat) + ADAM_EPS) + ADAM_WD * w)
    return delta, m, v


def reduce_adamw(recv, own, own_slotted, me, w, m, v, *, layer=0, n_layers=1, into=None, name):
    _, R, C = recv.shape
    tr = _pick(R, (176, 192, 160, 184, 128, 64, 32, 16, 8))
    off = layer * (R // tr)

    def body(me_ref, r_ref, own_ref, w_ref, m_ref, v_ref, *rest):
        g_ref, d_ref, nm_ref, nv_ref = rest[-4:]
        mine = me_ref[0]
        g = None
        for i in range(N_DEV):
            part = jnp.where(mine == i, own_ref[...], r_ref[i]).astype(F32)
            g = part if g is None else g + part
        delta, nm, nv = _adamw(w_ref[...], g, m_ref[...], v_ref[...])
        g_ref[...] = g
        d_ref[...] = delta
        nm_ref[...] = nm
        nv_ref[...] = nv

    row = pl.BlockSpec((tr, C), lambda i, me_ref: (i + off, 0))
    own_spec = (pl.BlockSpec((None, tr, C), lambda i, me_ref: (me_ref[0], i, 0)) if own_slotted
                else pl.BlockSpec((tr, C), lambda i, me_ref: (i, 0)))
    rest = [] if into is None else list(into)
    grid_spec = pltpu.PrefetchScalarGridSpec(
        num_scalar_prefetch=1, grid=(R // tr,),
        in_specs=[pl.BlockSpec((N_DEV, tr, C), lambda i, me_ref: (0, i, 0)), own_spec, row, row, row]
        + [pl.BlockSpec(memory_space=pl.ANY)] * len(rest),
        out_specs=[row] * 4)
    return pl.pallas_call(body, grid_spec=grid_spec, out_shape=[jax.ShapeDtypeStruct((n_layers * R, C), F32)] * 4,
                          input_output_aliases={6 + k: k for k in range(len(rest))},
                          compiler_params=_params(dimension_semantics=("parallel",)), name=name)(
        me.reshape(1).astype(jnp.int32), recv, own, w, m, v, *rest)


def _s5_prepare(A_re, A_im, log_dt, B_re, B_im, C_re, C_im):
    G, P, Cg = S5_GROUPS, S5_STATE, S5_GROUP
    dt = jnp.exp(log_dt)[:, None]
    mag = jnp.exp(A_re * dt)
    ab_re = mag * jnp.cos(A_im * dt)
    ab_im = mag * jnp.sin(A_im * dt)
    den = A_re * A_re + A_im * A_im
    nr, ni = ab_re - 1.0, ab_im
    c_re = (nr * A_re + ni * A_im) / den
    c_im = (ni * A_re - nr * A_im) / den
    Bb_re = c_re[..., None] * B_re - c_im[..., None] * B_im
    Bb_im = c_re[..., None] * B_im + c_im[..., None] * B_re
    def dense_in(b, name):
        return make_blockdiag(G, Cg, P, name)(b.transpose(0, 2, 1))

    def dense_out(c, name):
        return make_blockdiag(G, P, Cg, name)(c.transpose(0, 2, 1))

    return (ab_re.reshape(1, G * P), ab_im.reshape(1, G * P), dense_in(Bb_re, "s5_wb_re"), dense_in(Bb_im, "s5_wb_im"),
            dense_out(C_re, "s5_wc_re"), dense_out(-C_im, "s5_wc_im"))


def _lower_bound(gamma):
    return jnp.cumsum(jax.nn.softmax(gamma, axis=0), axis=0)[0:1]


def _ffn_fwd(h, g_norm, get_w_in, conv_w, conv_b, get_w_out, tag, final=None):
    w_in = get_w_in(h)
    hn, hu = norm_mm(h, g_norm, w_in, name=tag + "_in")
    act = convgate_fwd(hu, conv_w, conv_b, name=tag + "_gate")
    w_out = get_w_out(act)
    if final is None:
        h_out = mm(act, w_out, res=h, name=tag + "_out")
    else:
        h_out = mm_final_loss(act, w_out, h, final[0], final[1], name=tag + "_out_loss")
    return h_out, (hn, hu, act), w_in, w_out


def _ffn_bwd(h, g_norm, w_in, conv_w, conv_b, w_out, saved, dh, tag, send_dw_in, send_dw_out):
    hn, hu, act = saved
    sent = send_dw_out(mm(act, dh, ta=True, out_dtype=BF16, name=tag + "_dwout"))
    dact = mm(dh, w_out, tb=True, dep=sent, name=tag + "_dact")
    (dhu_a, dhu_b), dconv_w, dconv_b = convgate_bwd(hu, conv_w, conv_b, dact, name=tag + "_dgate")
    rows = 2 * dhu_a.shape[1]
    dw_in = mm(dhu_a, hn, ta=True, out_dtype=BF16, out_rows=rows, name=tag + "_dwin_a")
    dw_in = mm(dhu_b, hn, ta=True, out_dtype=BF16, out_rows=rows, out_off=rows // 2, into=dw_in, name=tag + "_dwin_b")
    sent = send_dw_in(dw_in)
    dh_in, dg = mm_drms((dhu_a, dhu_b), w_in, h, g_norm, dh, dep=sent, name=tag + "_dhn")
    return dh_in, dg, dconv_w, dconv_b


def kernel(x, positions, norm_mix, norm_ffn, norm_final, mix_w_in, mix_w_out, s5_A_re, s5_A_im, s5_log_dt, s5_B_re, s5_B_im, s5_C_re, s5_C_im, s5_D, s5_glu_w, s5_glu_b, hgrn_gamma, hgrn_norm, att_w_qkv, att_w_o, ffn_w_in, ffn_conv_w, ffn_conv_b, ffn_w_out, loss_target, m_norm_mix, m_norm_ffn, m_norm_final, m_mix_w_in, m_mix_w_out, m_s5_A_re, m_s5_A_im, m_s5_log_dt, m_s5_B_re, m_s5_B_im, m_s5_C_re, m_s5_C_im, m_s5_D, m_s5_glu_w, m_s5_glu_b, m_hgrn_gamma, m_hgrn_norm, m_att_w_qkv, m_att_w_o, m_ffn_w_in, m_ffn_conv_w, m_ffn_conv_b, m_ffn_w_out, v_norm_mix, v_norm_ffn, v_norm_final, v_mix_w_in, v_mix_w_out, v_s5_A_re, v_s5_A_im, v_s5_log_dt, v_s5_B_re, v_s5_B_im, v_s5_C_re, v_s5_C_im, v_s5_D, v_s5_glu_w, v_s5_glu_b, v_hgrn_gamma, v_hgrn_norm, v_att_w_qkv, v_att_w_o, v_ffn_w_in, v_ffn_conv_w, v_ffn_conv_b, v_ffn_w_out):
    W = dict(norm_mix=norm_mix, norm_ffn=norm_ffn, norm_final=norm_final, mix_w_in=mix_w_in, mix_w_out=mix_w_out,
             s5_A_re=s5_A_re, s5_A_im=s5_A_im, s5_log_dt=s5_log_dt, s5_B_re=s5_B_re, s5_B_im=s5_B_im,
             s5_C_re=s5_C_re, s5_C_im=s5_C_im, s5_D=s5_D, s5_glu_w=s5_glu_w, s5_glu_b=s5_glu_b,
             hgrn_gamma=hgrn_gamma, hgrn_norm=hgrn_norm, att_w_qkv=att_w_qkv, att_w_o=att_w_o, ffn_w_in=ffn_w_in,
             ffn_conv_w=ffn_conv_w, ffn_conv_b=ffn_conv_b, ffn_w_out=ffn_w_out)
    M = dict(norm_mix=m_norm_mix, norm_ffn=m_norm_ffn, norm_final=m_norm_final, mix_w_in=m_mix_w_in,
             mix_w_out=m_mix_w_out, s5_A_re=m_s5_A_re, s5_A_im=m_s5_A_im, s5_log_dt=m_s5_log_dt, s5_B_re=m_s5_B_re,
             s5_B_im=m_s5_B_im, s5_C_re=m_s5_C_re, s5_C_im=m_s5_C_im, s5_D=m_s5_D, s5_glu_w=m_s5_glu_w,
             s5_glu_b=m_s5_glu_b, hgrn_gamma=m_hgrn_gamma, hgrn_norm=m_hgrn_norm, att_w_qkv=m_att_w_qkv,
             att_w_o=m_att_w_o, ffn_w_in=m_ffn_w_in, ffn_conv_w=m_ffn_conv_w, ffn_conv_b=m_ffn_conv_b,
             ffn_w_out=m_ffn_w_out)
    V = dict(norm_mix=v_norm_mix, norm_ffn=v_norm_ffn, norm_final=v_norm_final, mix_w_in=v_mix_w_in,
             mix_w_out=v_mix_w_out, s5_A_re=v_s5_A_re, s5_A_im=v_s5_A_im, s5_log_dt=v_s5_log_dt, s5_B_re=v_s5_B_re,
             s5_B_im=v_s5_B_im, s5_C_re=v_s5_C_re, s5_C_im=v_s5_C_im, s5_D=v_s5_D, s5_glu_w=v_s5_glu_w,
             s5_glu_b=v_s5_glu_b, hgrn_gamma=v_hgrn_gamma, hgrn_norm=v_hgrn_norm, att_w_qkv=v_att_w_qkv,
             att_w_o=v_att_w_o, ffn_w_in=v_ffn_w_in, ffn_conv_w=v_ffn_conv_w, ffn_conv_b=v_ffn_conv_b,
             ffn_w_out=v_ffn_w_out)
    return _step(x[0], positions[0], loss_target[0], W, M, V)


TRANSPOSED = ("mix_w_in", "att_w_qkv", "ffn_w_in")
SMALL = ("norm_mix", "norm_ffn", "norm_final", "s5_A_re", "s5_A_im", "s5_log_dt", "s5_B_re", "s5_B_im", "s5_C_re",
         "s5_C_im", "s5_D", "s5_glu_b", "hgrn_gamma", "hgrn_norm", "ffn_conv_b")
ORDER = ("norm_mix", "norm_ffn", "norm_final", "mix_w_in", "mix_w_out", "s5_A_re", "s5_A_im", "s5_log_dt", "s5_B_re",
         "s5_B_im", "s5_C_re", "s5_C_im", "s5_D", "s5_glu_w", "s5_glu_b", "hgrn_gamma", "hgrn_norm", "att_w_qkv",
         "att_w_o", "ffn_w_in", "ffn_conv_w", "ffn_conv_b", "ffn_w_out")
PACK_COLS = 1024


def _step(x, positions, target, W, M, V):
    L, D = x.shape
    me = 4 * lax.axis_index("x") + 2 * lax.axis_index("y") + lax.axis_index("c")
    n_cw = W["ffn_conv_w"].shape[-1]
    T = {n: tuple(jnp.swapaxes(d[n], -1, -2) for d in (W, M, V)) for n in TRANSPOSED}
    first = {
        "mix_w_in": cast_bf16(T["mix_w_in"][0][0], name="mix_w_in_cast"),
        "conv_w": W["ffn_conv_w"].reshape(6, n_cw),
        "s5_glu_w": cast_bf16(W["s5_glu_w"][0], name="s5_glu_w_cast"),
    }
    first_handles, token = copies_start(list(first.values()), "gather2", name="gather_start_first")
    shards = {
        "mix_w_out": cast_bf16(W["mix_w_out"][0], dep=token, name="mix_w_out_cast"),
        "ffn_w_in0": cast_bf16_layer(T["ffn_w_in"][0], 0, name="ffn_w_in0_cast"),
        "ffn_w_out0": cast_bf16_layer(W["ffn_w_out"], 0, name="ffn_w_out0_cast"),
        "att_w_qkv": cast_bf16(T["att_w_qkv"][0][0], name="att_w_qkv_cast"),
        "att_w_o": cast_bf16(W["att_w_o"][0], name="att_w_o_cast"),
        "ffn_w_in1": cast_bf16_layer(T["ffn_w_in"][0], 1, name="ffn_w_in1_cast"),
        "ffn_w_out1": cast_bf16_layer(W["ffn_w_out"], 1, name="ffn_w_out1_cast"),
    }
    gather_handles, token = copies_start(list(shards.values()), "gather2", name="gather_start")
    gather_handle = dict(zip(list(first) + list(shards), first_handles + gather_handles))

    def forward(keys, after, name):
        new, sent = copies_forward([gather_handle[k] for k in keys], after, name=name)
        gather_handle.update(zip(keys, new))
        return sent

    def gathered(key, after, cols):
        _, land = copies_wait(gather_handle[key], "gather2", after, name=key + "_gwait")
        return cols_from_shards(land, name=key + "_asm") if cols else land.reshape(-1, land.shape[-1])

    conv_b = W["ffn_conv_b"].reshape(2, 1, -1)

    s5_params = (W["s5_A_re"][0], W["s5_A_im"][0], W["s5_log_dt"][0], W["s5_B_re"][0], W["s5_B_im"][0],
                 W["s5_C_re"][0], W["s5_C_im"][0])
    (a_re, a_im, wb_re, wb_im, wc_re, wc_im), s5_prep_vjp = jax.vjp(_s5_prepare, *s5_params)
    dvec = W["s5_D"].reshape(1, S5_WIDTH)
    glu_b = W["s5_glu_b"].reshape(1, S5_WIDTH)
    lb, lb_vjp = jax.vjp(_lower_bound, W["hgrn_gamma"])
    hg_norm = W["hgrn_norm"].reshape(1, -1)
    tabs = rope_tables(positions)

    sent = forward(["mix_w_in", "conv_w", "s5_glu_w"], token, "forward_a")
    w_mix_in = gathered("mix_w_in", sent, False)
    hn0, proj = norm_mm(x, W["norm_mix"][0], w_mix_in, name="l0_proj")
    y0, xs_re, xs_im = s5_core_fwd(proj, a_re, a_im, wb_re, wb_im, wc_re, wc_im, name="s5_core")
    w_glu = gathered("s5_glu_w", y0, False)
    cat = s5_out_fwd(y0, proj, dvec, w_glu, glu_b, name="s5_out")
    cat, hg_states = hgrn_fwd(proj, lb, hg_norm, cat, name="hgrn_fwd")
    forward(["mix_w_out"], cat, "forward_b")
    w_mix_out = gathered("mix_w_out", cat, False)
    h1 = mm(cat, w_mix_out, res=x, name="l0_mix_out")
    _, cw_all = copies_wait(gather_handle["conv_w"], "gather2", h1, name="conv_w_gwait")
    conv_w = cw_all.transpose(1, 0, 2).reshape(2, 3, N_DEV * n_cw)
    w_ffn_in, w_ffn_out = [None, None], [None, None]
    h2, ffn0_saved, w_ffn_in[0], w_ffn_out[0] = _ffn_fwd(
        h1, W["norm_ffn"][0],
        lambda a: (forward(["ffn_w_in0"], a, "forward_b2"), gathered("ffn_w_in0", a, False))[1], conv_w[0], conv_b[0],
        lambda a: (forward(["ffn_w_out0"], a, "forward_c"), gathered("ffn_w_out0", a, False))[1], "ffn0")

    forward(["att_w_qkv", "att_w_o"], h2, "forward_d")
    w_qkv = gathered("att_w_qkv", h2, False)
    hn2, qkv_r = norm_mm(h2, W["norm_mix"][1], w_qkv, tabs=tabs, name="l1_qkv")
    att_o, att_l = [], []
    for g, d in enumerate(ATT_DILATIONS):
        o_g, l_g = attn_fwd(qkv_r, g, d, name=f"attn_fwd{g}")
        att_o.append(o_g)
        att_l.append(l_g)
    o_att = merge_fwd(att_o, att_l, name="merge_fwd")
    forward(["ffn_w_in1", "ffn_w_out1"], o_att, "forward_e")
    w_o = gathered("att_w_o", o_att, True)
    h3 = mm(o_att, w_o, res=h2, name="l1_mix_out")
    (loss, dh4, dg_final), ffn1_saved, w_ffn_in[1], w_ffn_out[1] = _ffn_fwd(
        h3, W["norm_ffn"][1], lambda a: gathered("ffn_w_in1", a, False), conv_w[1], conv_b[1],
        lambda a: gathered("ffn_w_out1", a, False), "ffn1", final=(W["norm_final"], target))

    exchanges = {}

    pending = []

    def send_grad(key, g, cols, flush=True):
        if cols:
            parts = shards_from_cols(g, name=key + "_split")
        else:
            parts = g.reshape(N_DEV, g.shape[0] // N_DEV, g.shape[1])
        pending.append((key, parts))
        if not flush:
            return None
        handles, sent = copies_start([p for _, p in pending], "exchange", name=key + "_xstart")
        exchanges.update(zip([k for k, _ in pending], handles))
        pending.clear()
        return sent

    dh3, dg_ffn1, dcw1, dcb1 = _ffn_bwd(h3, W["norm_ffn"][1], w_ffn_in[1], conv_w[1], conv_b[1], w_ffn_out[1],
                                        ffn1_saved, dh4, "ffn1", lambda g: send_grad("ffn_w_in1", g, False),
                                        lambda g: send_grad("ffn_w_out1", g, False, flush=False))
    sent = send_grad("att_w_o", mm(o_att, dh3, ta=True, name="l1_dwo"), True, flush=False)
    d_oatt = mm(dh3, w_o, tb=True, dep=sent, name="l1_dmix")
    mb = merge_bwd(att_o, att_l, d_oatt, name="merge_bwd")
    d_slabs = [attn_bwd(qkv_r, g, att_l[g], mb[g], mb[3 + g], d, name=f"attn_bwd{g}")
               for g, d in enumerate(ATT_DILATIONS)]
    d_qkv = rope_bwd([s[0] for s in d_slabs] + [s[1] for s in d_slabs] + [s[2] for s in d_slabs], tabs,
                     name="rope_bwd")
    sent = send_grad("att_w_qkv", mm(d_qkv, hn2, ta=True, out_dtype=BF16, name="l1_dwqkv"), False)
    dh2, dg_mix1 = mm_drms(d_qkv, w_qkv, h2, W["norm_mix"][1], dh3, dep=sent, name="l1_dhn")

    dh1, dg_ffn0, dcw0, dcb0 = _ffn_bwd(h1, W["norm_ffn"][0], w_ffn_in[0], conv_w[0], conv_b[0], w_ffn_out[0],
                                        ffn0_saved, dh2, "ffn0", lambda g: send_grad("ffn_w_in0", g, False),
                                        lambda g: send_grad("ffn_w_out0", g, False, flush=False))
    sent = send_grad("mix_w_out", mm(cat, dh1, ta=True, out_dtype=BF16, name="l0_dwout"), False)
    dcat = mm(dh1, w_mix_out, tb=True, dep=sent, name="l0_dcat")
    dy, du_d, z_bf, dzg, dglu_b, dD = s5_out_bwd(y0, proj, dvec, w_glu, glu_b, dcat, name="s5_dout")
    sent_glu = send_grad("s5_glu_w", mm(z_bf, dzg, ta=True, out_dtype=BF16, name="s5_dglu"), False, flush=False)
    du, dwb_re, dwb_im, dwc_re, dwc_im, da_re, da_im = s5_core_bwd(
        dy, du_d, proj, xs_re, xs_im, a_re, a_im, wb_re, wb_im, wc_re, wc_im, name="s5_dcore")
    s5_small = s5_prep_vjp((da_re, da_im, dwb_re, dwb_im, dwc_re, dwc_im))
    d_proj, dlb, dhg_norm = hgrn_bwd(proj, lb, hg_norm, hg_states, dcat, du, name="hgrn_bwd")
    sent = send_grad("mix_w_in", mm(d_proj, hn0, ta=True, out_dtype=BF16, dep=sent_glu, name="l0_dwin"), False)
    grad_x, dg_mix0 = mm_drms(d_proj, w_mix_in, x, W["norm_mix"][0], dh1, dep=sent, name="l0_dhn")
    (d_gamma,) = lb_vjp(dlb)
    out = {}

    dA_re, dA_im, dlog_dt, dB_re, dB_im, dC_re, dC_im = s5_small
    small_g = dict(norm_mix=jnp.concatenate([dg_mix0, dg_mix1], axis=0), norm_ffn=jnp.concatenate([dg_ffn0, dg_ffn1], axis=0),
                   norm_final=dg_final, s5_A_re=dA_re, s5_A_im=dA_im, s5_log_dt=dlog_dt, s5_B_re=dB_re, s5_B_im=dB_im,
                   s5_C_re=dC_re, s5_C_im=dC_im, s5_D=dD, s5_glu_b=dglu_b, hgrn_gamma=d_gamma, hgrn_norm=dhg_norm,
                   ffn_conv_b=jnp.concatenate([dcb0, dcb1], axis=0))
    conv_w_g = jnp.stack([dcw0, dcw1], axis=0)
    sizes = [math.prod(W[n].shape) for n in SMALL]
    n_conv = conv_w_g.size
    total = sum(sizes) + n_conv + 1
    rows = -(-total // PACK_COLS)
    rows = -(-rows // 8) * 8
    pad = rows * PACK_COLS - total

    def pack(vals, conv_part, last):
        flat = [v.reshape(-1).astype(F32) for v in vals] + [conv_part.reshape(-1), last.reshape(-1),
                                                            jnp.zeros((pad,), F32)]
        return jnp.concatenate(flat).reshape(rows, PACK_COLS)

    def conv_full(shard):
        col_owner = lax.broadcasted_iota(jnp.int32, (2, 3, N_DEV * n_cw), 2) // n_cw
        return jnp.where(col_owner == me, jnp.tile(shard, (1, 1, N_DEV)), 0.0)

    zero1 = jnp.zeros((1,), F32)
    g_pack = pack([small_g[n] for n in SMALL], conv_w_g, loss)
    w_pack = pack([W[n] for n in SMALL], conv_full(W["ffn_conv_w"]), zero1)
    m_pack = pack([M[n] for n in SMALL], conv_full(M["ffn_conv_w"]), zero1)
    v_pack = pack([V[n] for n in SMALL], conv_full(V["ffn_conv_w"]), zero1 + 1.0)
    (small_handle,), small_sent = copies_start([g_pack], "gather", name="small_xstart")

    def finish(name, n_layers):
        w3, m3, v3 = T[name] if name in TRANSPOSED else (W[name], M[name], V[name])
        res = None
        for layer in reversed(range(n_layers)):
            key = name if n_layers == 1 else f"{name}{layer}"
            own, recv = copies_wait(exchanges[key], "exchange", small_sent, name=key + "_xwait")
            _, R, Cn = recv.shape
            res = reduce_adamw(recv, own, True, me, w3.reshape(n_layers * R, Cn), m3.reshape(n_layers * R, Cn),
                               v3.reshape(n_layers * R, Cn), layer=layer, n_layers=n_layers, into=res,
                               name=key + "_adamw")
        res = [r.reshape(w3.shape) for r in res]
        return tuple(jnp.swapaxes(r, -1, -2) for r in res) if name in TRANSPOSED else tuple(res)

    for name in ("ffn_w_out", "ffn_w_in"):
        out[name] = finish(name, 2)
    for name in ("att_w_o", "att_w_qkv", "mix_w_out", "s5_glu_w", "mix_w_in"):
        out[name] = finish(name, 1)

    small_own, small_recv = copies_wait(small_handle, "gather", out["s5_glu_w"][0], name="small_xwait")
    res = reduce_adamw(small_recv, small_own, False, me, w_pack, m_pack, v_pack, name="small_adamw")
    flat = [r.reshape(-1) for r in res]
    off = 0
    for n, sz in zip(SMALL, sizes):
        out[n] = tuple(f[off:off + sz].reshape(W[n].shape) for f in flat)
        off += sz
    conv_res = [f[off:off + n_conv].reshape(2, 3, N_DEV * n_cw) for f in flat]
    out["ffn_conv_w"] = tuple(lax.dynamic_slice(c, (0, 0, me * n_cw), (2, 3, n_cw)) for c in conv_res)
    off += n_conv
    loss_total = flat[0][off]

    result = [loss_total, grad_x[None]]
    for k in range(4):
        result += [out[n][k] for n in ORDER]
    return tuple(result)
```

```python
import math

import jax
import jax.numpy as jnp
from jax import lax
from jax.experimental import pallas as pl
from jax.experimental.pallas import tpu as pltpu

F32 = jnp.float32
BF16 = jnp.bfloat16
MESH_ID = pl.DeviceIdType.MESH
N_DEV = 8
VMEM_LIMIT_BYTES = 56 * 1024 * 1024

NORM_EPS = 1e-6
S5_WIDTH, S5_GROUP, S5_GROUPS, S5_STATE = 512, 16, 32, 64
HG_HEADS, HG_DIM, HG_CHUNK = 4, 128, 64
HG_STEP_CHUNKS = 4
ATT_E, ATT_HPG, ATT_BLOCK = 64, 8, 128
ATT_DILATIONS = (1, 4, 16)
ROT_DIM, ROPE_THETA = 16, 500000.0
D_FF = 2816
CONV_ROWS, CONV_HALO = 128, 8
ADAM_LR, ADAM_B1, ADAM_B2, ADAM_EPS, ADAM_WD, ADAM_STEP = 0.001, 0.9, 0.999, 1e-08, 0.01, 10
NEG_BIG = -1e30


def _params(**kw):
    return pltpu.CompilerParams(vmem_limit_bytes=VMEM_LIMIT_BYTES, **kw)


def _pick(n, cands):
    for c in cands:
        if n % c == 0:
            return c
    return n


def _dot(a, b):
    return jnp.dot(a.astype(BF16), b.astype(BF16), preferred_element_type=F32)


def _dot_nt(a, b):
    return lax.dot_general(a.astype(BF16), b.astype(BF16), (((1,), (1,)), ((), ())), preferred_element_type=F32)


def _dot_tn(a, b):
    return lax.dot_general(a.astype(BF16), b.astype(BF16), (((0,), (0,)), ((), ())), preferred_element_type=F32)


def _split2(x):
    hi = x.astype(BF16)
    return hi, (x - hi.astype(F32)).astype(BF16)


def _dot_x3(a, b, contract=((1,), (0,))):
    dn = (contract, ((), ()))
    a1, a2 = _split2(a)
    b1, b2 = _split2(b)
    return (lax.dot_general(a1, b1, dn, preferred_element_type=F32) + lax.dot_general(a1, b2, dn, preferred_element_type=F32)
            + lax.dot_general(a2, b1, dn, preferred_element_type=F32))


def _sigmoid(x):
    return 1.0 / (1.0 + jnp.exp(-x))


V7X_HBM_BYTES_PER_S = 3.2e12
V7X_MXU_FLOPS_PER_S = 0.7e15
GRID_STEP_S = 0.35e-6
MM_VMEM_BUDGET = 40 * 1024 * 1024


def _divisors(n, cands):
    return [c for c in cands if c <= n and n % c == 0] or [n]


def _mm_tiles(m, n, k, sa, sb, so, sr):
    best = None
    for tm in _divisors(m, (2816, 2048, 1408, 1024, 512, 256, 128)):
        for tn in _divisors(n, (2816, 2048, 1408, 1024, 512, 256, 128)):
            for tk in _divisors(k, (k, 2816, 2560, 2304, 2048, 1536, 1408, 1280, 1024, 512, 256, 128)):
                nk = k // tk
                vmem = 2 * (tm * tk * sa + tk * tn * sb + tm * tn * (so + sr)) + (tm * tn * 4 if nk > 1 else 0)
                vmem += tm * tk * 2 * (sa > 2) + tk * tn * 2 * (sb > 2) + tm * tn * 4
                if vmem > MM_VMEM_BUDGET:
                    continue
                ni, nj = m // tm, n // tn
                for i_outer in (True, False):
                    if i_outer:
                        a_reads = 1 if nk == 1 else nj
                        b_reads = 1 if (nk == 1 and nj == 1) else ni
                    else:
                        b_reads = 1 if nk == 1 else ni
                        a_reads = 1 if (nk == 1 and ni == 1) else nj
                    traffic = a_reads * m * k * sa + b_reads * k * n * sb + m * n * (so + sr)
                    t = max(traffic / V7X_HBM_BYTES_PER_S, 2.0 * m * n * k / V7X_MXU_FLOPS_PER_S)
                    t += ni * nj * nk * GRID_STEP_S
                    t += (tm * tk * sa + tk * tn * sb + tm * tn * so) / V7X_HBM_BYTES_PER_S
                    if best is None or t < best[0]:
                        best = (t, tm, tn, tk, i_outer)
    assert best is not None, (m, n, k)
    return best[1:]


def mm(a, b, *, ta=False, tb=False, res=None, out_dtype=F32, dep=None, out_rows=None, out_off=0, into=None, name):
    m, k = (a.shape[1], a.shape[0]) if ta else a.shape
    n = b.shape[0] if tb else b.shape[1]
    assert (b.shape[1] if tb else b.shape[0]) == k
    has_res = res is not None
    tm, tn, tk, i_outer = _mm_tiles(m, n, k, a.dtype.itemsize, b.dtype.itemsize, jnp.dtype(out_dtype).itemsize,
                                    res.dtype.itemsize if has_res else 0)
    nk = k // tk
    deps = [] if dep is None else [dep]
    dn = (((0 if ta else 1,), (1 if tb else 0,)), ((), ()))

    def body_single(*refs):
        a_ref, b_ref = refs[:2]
        o_ref = refs[-1]
        out = lax.dot_general(a_ref[...].astype(BF16), b_ref[...].astype(BF16), dn, preferred_element_type=F32)
        if has_res:
            out = out + refs[2][...].astype(F32)
        o_ref[...] = out.astype(o_ref.dtype)

    def body(*refs):
        a_ref, b_ref = refs[:2]
        r_ref = refs[2] if has_res else None
        o_ref, acc_ref = refs[-2:]
        kk = pl.program_id(2)
        part = lax.dot_general(a_ref[...].astype(BF16), b_ref[...].astype(BF16), dn, preferred_element_type=F32)

        @pl.when(kk == 0)
        def _():
            acc_ref[...] = part

        @pl.when(kk > 0)
        def _():
            acc_ref[...] += part

        @pl.when(kk == nk - 1)
        def _():
            out = acc_ref[...]
            if has_res:
                out = out + r_ref[...].astype(F32)
            o_ref[...] = out.astype(o_ref.dtype)

    def ij(f):
        return (lambda g0, g1, q: f(g0, g1, q)) if i_outer else (lambda g0, g1, q: f(g1, g0, q))

    a_spec = pl.BlockSpec((tk, tm), ij(lambda i, j, q: (q, i))) if ta else pl.BlockSpec((tm, tk), ij(lambda i, j, q: (i, q)))
    b_spec = pl.BlockSpec((tn, tk), ij(lambda i, j, q: (j, q))) if tb else pl.BlockSpec((tk, tn), ij(lambda i, j, q: (q, j)))
    assert out_off % tm == 0
    off = out_off // tm
    r_spec = pl.BlockSpec((tm, tn), ij(lambda i, j, q: (i, j)))
    o_spec = pl.BlockSpec((tm, tn), ij(lambda i, j, q: (i + off, j)))
    rest = [] if into is None else [into]
    in_specs = ([a_spec, b_spec] + ([r_spec] if has_res else []) + [pl.BlockSpec((8, 128), lambda g0, g1, q: (0, 0))] * len(deps)
                + [pl.BlockSpec(memory_space=pl.ANY)] * len(rest))
    args = (a, b) + ((res,) if has_res else ()) + tuple(deps) + tuple(rest)
    grid = (m // tm, n // tn, nk) if i_outer else (n // tn, m // tm, nk)
    return pl.pallas_call(
        body_single if nk == 1 else body, grid=grid, in_specs=in_specs, out_specs=o_spec,
        out_shape=jax.ShapeDtypeStruct((out_rows or m, n), out_dtype),
        input_output_aliases={len(args) - 1: 0} if rest else {},
        scratch_shapes=[] if nk == 1 else [pltpu.VMEM((tm, tn), F32)],
        compiler_params=_params(dimension_semantics=("parallel", "parallel", "arbitrary")), name=name,
    )(*args)


def mm_drms(dy_in, w, x, g, dres, *, dep=None, name):
    halves = dy_in if isinstance(dy_in, (tuple, list)) else (dy_in,)
    m, kh = halves[0].shape
    k = kh * len(halves)
    D = w.shape[1]
    tm = _pick(m, (1024, 512, 256, 128))
    tk = max(_divisors(kh, (1536, 1408, 1280, 1024, 512, 256, 128)))
    nk, nh = k // tk, kh // tk
    deps = [] if dep is None else [dep]

    def body(*refs):
        a_refs, (b_ref, x_ref, g_ref, dres_ref) = refs[:len(halves)], refs[len(halves):len(halves) + 4]
        dx_ref, dg_ref, acc_ref = refs[-3:]
        i, q = pl.program_id(0), pl.program_id(1)
        a = a_refs[0][...] if len(halves) == 1 else jnp.where(q < nh, a_refs[0][...], a_refs[1][...])
        part = jnp.dot(a, b_ref[...], preferred_element_type=F32)

        @pl.when(q == 0)
        def _():
            acc_ref[...] = part

        @pl.when(q > 0)
        def _():
            acc_ref[...] += part

        @pl.when((i == 0) & (q == 0))
        def _():
            dg_ref[...] = jnp.zeros_like(dg_ref)

        @pl.when(q == nk - 1)
        def _():
            dyv = acc_ref[...]
            xv = x_ref[...]
            r = lax.rsqrt(jnp.mean(xv * xv, axis=-1, keepdims=True) + NORM_EPS)
            xh = xv * r
            dg_ref[...] += jnp.sum(dyv * xh, axis=0, keepdims=True)
            dxh = dyv * g_ref[...]
            dx_ref[...] = dres_ref[...] + r * (dxh - xh * jnp.mean(dxh * xh, axis=-1, keepdims=True))

    row = pl.BlockSpec((tm, D), lambda i, q: (i, 0))
    vec = pl.BlockSpec((1, D), lambda i, q: (0, 0))
    a_specs = [pl.BlockSpec((tm, tk), lambda i, q, h=h: (i, jnp.clip(q - h * nh, 0, nh - 1))) for h in range(len(halves))]
    in_specs = a_specs + [pl.BlockSpec((tk, D), lambda i, q: (q, 0)), row, vec, row]
    in_specs += [pl.BlockSpec((8, 128), lambda i, q: (0, 0))] * len(deps)
    return pl.pallas_call(
        body, grid=(m // tm, nk), in_specs=in_specs, out_specs=[row, vec],
        out_shape=[jax.ShapeDtypeStruct((m, D), F32), jax.ShapeDtypeStruct((1, D), F32)],
        scratch_shapes=[pltpu.VMEM((tm, D), F32)],
        compiler_params=_params(dimension_semantics=("arbitrary", "arbitrary")), name=name,
    )(*halves, w, x, g.reshape(1, D), dres, *deps)


def mm_final_loss(act, w, h_res, g, target, *, name):
    L, K = act.shape
    D = w.shape[1]
    tm = _pick(L, (1024, 512, 256, 128))
    tk = max(_divisors(K, (1536, 1408, 1280, 1024, 512, 256, 128)))
    nk = K // tk

    def body(a_ref, b_ref, r_ref, g_ref, t_ref, loss_ref, dx_ref, dg_ref, acc_ref):
        i, q = pl.program_id(0), pl.program_id(1)
        part = jnp.dot(a_ref[...], b_ref[...], preferred_element_type=F32)

        @pl.when(q == 0)
        def _():
            acc_ref[...] = part

        @pl.when(q > 0)
        def _():
            acc_ref[...] += part

        @pl.when((i == 0) & (q == 0))
        def _():
            dg_ref[...] = jnp.zeros_like(dg_ref)
            loss_ref[...] = jnp.zeros_like(loss_ref)

        @pl.when(q == nk - 1)
        def _():
            xv = acc_ref[...] + r_ref[...]
            gv = g_ref[...]
            r = lax.rsqrt(jnp.mean(xv * xv, axis=-1, keepdims=True) + NORM_EPS)
            xh = xv * r
            err = xh * gv - t_ref[...]
            loss_ref[...] += 0.5 * jnp.sum(jnp.mean(err * err, axis=-1, keepdims=True), axis=0, keepdims=True)
            dyv = err * (1.0 / D)
            dg_ref[...] += jnp.sum(dyv * xh, axis=0, keepdims=True)
            dxh = dyv * gv
            dx_ref[...] = r * (dxh - xh * jnp.mean(dxh * xh, axis=-1, keepdims=True))

    row = pl.BlockSpec((tm, D), lambda i, q: (i, 0))
    vec = pl.BlockSpec((1, D), lambda i, q: (0, 0))
    one = pl.BlockSpec((1, 1), lambda i, q: (0, 0))
    return pl.pallas_call(
        body, grid=(L // tm, nk),
        in_specs=[pl.BlockSpec((tm, tk), lambda i, q: (i, q)), pl.BlockSpec((tk, D), lambda i, q: (q, 0)), row, vec, row],
        out_specs=[one, row, vec],
        out_shape=[jax.ShapeDtypeStruct((1, 1), F32), jax.ShapeDtypeStruct((L, D), F32), jax.ShapeDtypeStruct((1, D), F32)],
        scratch_shapes=[pltpu.VMEM((tm, D), F32)],
        compiler_params=_params(dimension_semantics=("arbitrary", "arbitrary")), name=name,
    )(act, w, h_res, g.reshape(1, D), target)


def _cmul(ar, ai, br, bi):
    return ar * br - ai * bi, ar * bi + ai * br


def _powers(ar, ai):
    rows = [(ar, ai)]
    for _ in range(7):
        rows.append(_cmul(rows[-1][0], rows[-1][1], ar, ai))
    table = (jnp.concatenate([r[0] for r in rows], axis=0), jnp.concatenate([r[1] for r in rows], axis=0))
    return (rows[0], rows[1], rows[3]), table


def _block_scan(br, bi, steps, shift):
    yr, yi = br, bi
    for s, (pr, pi) in zip((1, 2, 4), steps):
        sr, si = shift(yr, s), shift(yi, s)
        yr, yi = yr + pr * sr - pi * si, yi + pr * si + pi * sr
    return yr, yi


def s5_core_fwd(proj, a_re, a_im, wb_re, wb_im, wc_re, wc_im, *, name):
    L = proj.shape[0]
    parts, cu, W = wb_re.shape

    def body(u_ref, ar_ref, ai_ref, wbr_ref, wbi_ref, wcr_ref, wci_ref, y_ref, xr_ref, xi_ref, br_ref, bi_ref):
        u = u_ref[...]
        br_ref[...] = _dot(u, wbr_ref[...])
        bi_ref[...] = _dot(u, wbi_ref[...])
        steps, (tr, ti) = _powers(ar_ref[...], ai_ref[...])
        row = lax.broadcasted_iota(jnp.int32, (8, W), 0)

        def shift(y, s):
            return jnp.where(row >= s, pltpu.roll(y, s, 0), 0.0)

        def step(t8, carry):
            cr, ci = carry
            base = pl.multiple_of(t8 * 8, 8)
            yr, yi = _block_scan(br_ref[pl.ds(base, 8), :], bi_ref[pl.ds(base, 8), :], steps, shift)
            xr = yr + tr * cr - ti * ci
            xi = yi + tr * ci + ti * cr
            xr_ref[pl.ds(base, 8), :] = xr
            xi_ref[pl.ds(base, 8), :] = xi
            return jnp.broadcast_to(xr[7:8, :], (8, W)), jnp.broadcast_to(xi[7:8, :], (8, W))

        zero = jnp.zeros((8, W), F32)
        lax.fori_loop(0, L // 8, step, (zero, zero), unroll=2)
        y_ref[...] = _dot(xr_ref[...], wcr_ref[...]) + _dot(xi_ref[...], wci_ref[...])

    ucol = pl.BlockSpec((L, cu), lambda t: (0, t))
    vec = pl.BlockSpec((1, W), lambda t: (0, t))
    col = pl.BlockSpec((L, W), lambda t: (0, t))
    wb = pl.BlockSpec((None, cu, W), lambda t: (t, 0, 0))
    wc = pl.BlockSpec((None, W, cu), lambda t: (t, 0, 0))
    return pl.pallas_call(body, grid=(parts,), in_specs=[ucol, vec, vec, wb, wb, wc, wc], out_specs=[ucol, col, col],
                          out_shape=[jax.ShapeDtypeStruct((L, parts * cu), F32)]
                          + [jax.ShapeDtypeStruct((L, parts * W), F32)] * 2,
                          scratch_shapes=[pltpu.VMEM((L, W), F32)] * 2,
                          compiler_params=_params(dimension_semantics=("parallel",)), name=name)(
        proj, a_re, a_im, wb_re, wb_im, wc_re, wc_im)


def s5_core_bwd(dy, du_d, proj, xs_re, xs_im, a_re, a_im, wb_re, wb_im, wc_re, wc_im, *, name):
    L = proj.shape[0]
    parts, cu, W = wb_re.shape

    def body(dy_ref, dud_ref, u_ref, xr_ref, xi_ref, ar_ref, ai_ref, wbr_ref, wbi_ref, wcr_ref, wci_ref,
             du_ref, dwbr_ref, dwbi_ref, dwcr_ref, dwci_ref, dar_ref, dai_ref, lr_ref, li_ref):
        dy = dy_ref[...]
        lr_ref[...] = _dot_nt(dy, wcr_ref[...])
        li_ref[...] = _dot_nt(dy, wci_ref[...])
        dwcr_ref[...] = _dot_tn(xr_ref[...], dy)
        dwci_ref[...] = _dot_tn(xi_ref[...], dy)
        ar, ai = ar_ref[...], -ai_ref[...]
        steps, (tr, ti) = _powers(ar, ai)
        tr = jnp.concatenate([tr[j:j + 1, :] for j in range(7, -1, -1)], axis=0)
        ti = jnp.concatenate([ti[j:j + 1, :] for j in range(7, -1, -1)], axis=0)
        row8 = lax.broadcasted_iota(jnp.int32, (8, W), 0)
        nblk = L // 8

        def shift(y, s):
            return jnp.where(row8 < 8 - s, pltpu.roll(y, 8 - s, 0), 0.0)

        def step(s, carry):
            cr, ci = carry
            base = pl.multiple_of((nblk - 1 - s) * 8, 8)
            yr, yi = _block_scan(lr_ref[pl.ds(base, 8), :], li_ref[pl.ds(base, 8), :], steps, shift)
            lr = yr + tr * cr - ti * ci
            li = yi + tr * ci + ti * cr
            lr_ref[pl.ds(base, 8), :] = lr
            li_ref[pl.ds(base, 8), :] = li
            return jnp.broadcast_to(lr[0:1, :], (8, W)), jnp.broadcast_to(li[0:1, :], (8, W))

        zero = jnp.zeros((8, W), F32)
        lax.fori_loop(0, nblk, step, (zero, zero), unroll=2)
        row = lax.broadcasted_iota(jnp.int32, (L, W), 0)
        xpr = jnp.where(row >= 1, pltpu.roll(xr_ref[...], 1, 0), 0.0)
        xpi = jnp.where(row >= 1, pltpu.roll(xi_ref[...], 1, 0), 0.0)
        lr, li = lr_ref[...], li_ref[...]
        dar_ref[...] = jnp.sum(lr * xpr + li * xpi, axis=0, keepdims=True)
        dai_ref[...] = jnp.sum(li * xpr - lr * xpi, axis=0, keepdims=True)
        u = u_ref[...]
        dwbr_ref[...] = _dot_tn(u, lr)
        dwbi_ref[...] = _dot_tn(u, li)
        du_ref[...] = (dud_ref[...] + _dot_nt(lr, wbr_ref[...]) + _dot_nt(li, wbi_ref[...])).astype(du_ref.dtype)

    ucol = pl.BlockSpec((L, cu), lambda t: (0, t))
    vec = pl.BlockSpec((1, W), lambda t: (0, t))
    col = pl.BlockSpec((L, W), lambda t: (0, t))
    wb = pl.BlockSpec((None, cu, W), lambda t: (t, 0, 0))
    wc = pl.BlockSpec((None, W, cu), lambda t: (t, 0, 0))
    return pl.pallas_call(
        body, grid=(parts,), in_specs=[ucol, ucol, ucol, col, col, vec, vec, wb, wb, wc, wc],
        out_specs=[ucol, wb, wb, wc, wc, vec, vec],
        out_shape=[jax.ShapeDtypeStruct((L, parts * cu), BF16)] + [jax.ShapeDtypeStruct((parts, cu, W), F32)] * 2
        + [jax.ShapeDtypeStruct((parts, W, cu), F32)] * 2 + [jax.ShapeDtypeStruct((1, parts * W), F32)] * 2,
        scratch_shapes=[pltpu.VMEM((L, W), F32)] * 2,
        compiler_params=_params(dimension_semantics=("parallel",)), name=name,
    )(dy, du_d, proj, xs_re, xs_im, a_re, a_im, wb_re, wb_im, wc_re, wc_im)


def _gelu(y):
    c = math.sqrt(2.0 / math.pi)
    t = jnp.tanh(c * (y + 0.044715 * y * y * y))
    return 0.5 * y * (1.0 + t), t


def s5_out_fwd(y0, proj, dvec, glu_w, glu_b, *, name):
    L, C = y0.shape
    tr = _pick(L, (256, 128))

    def body(y_ref, u_ref, d_ref, w_ref, b_ref, o_ref):
        z, _ = _gelu(y_ref[...] + d_ref[...] * u_ref[...])
        zg = _dot(z, w_ref[...]) + b_ref[...]
        o_ref[...] = (z * _sigmoid(zg)).astype(o_ref.dtype)

    row = pl.BlockSpec((tr, C), lambda i: (i, 0))
    vec = pl.BlockSpec((1, C), lambda i: (0, 0))
    wsp = pl.BlockSpec((C, C), lambda i: (0, 0))
    return pl.pallas_call(body, grid=(L // tr,), in_specs=[row, row, vec, wsp, vec], out_specs=row,
                          out_shape=jax.ShapeDtypeStruct((L, 2 * C), BF16), name=name)(
        y0, proj, dvec, glu_w, glu_b)


def s5_out_bwd(y0, proj, dvec, glu_w, glu_b, dcat, *, name):
    L, C = y0.shape
    tr = _pick(L, (256, 128))

    def body(y_ref, u_ref, d_ref, w_ref, b_ref, do_ref, dy_ref, dud_ref, z_ref, dzg_ref, db_ref, dd_ref):
        u = u_ref[...]
        y = y_ref[...] + d_ref[...] * u
        z, t = _gelu(y)
        zg = _dot(z, w_ref[...]) + b_ref[...]
        s = _sigmoid(zg)
        do = do_ref[...]
        dzg = do * z * s * (1.0 - s)
        dz = do * s + _dot_nt(dzg, w_ref[...])
        c = math.sqrt(2.0 / math.pi)
        dgelu = 0.5 * (1.0 + t) + 0.5 * y * (1.0 - t * t) * c * (1.0 + 3.0 * 0.044715 * y * y)
        dy = dz * dgelu

        @pl.when(pl.program_id(0) == 0)
        def _():
            db_ref[...] = jnp.zeros_like(db_ref)
            dd_ref[...] = jnp.zeros_like(dd_ref)

        db_ref[...] += jnp.sum(dzg, axis=0, keepdims=True)
        dd_ref[...] += jnp.sum(dy * u, axis=0, keepdims=True)
        dy_ref[...] = dy
        dud_ref[...] = dy * d_ref[...]
        z_ref[...] = z.astype(BF16)
        dzg_ref[...] = dzg.astype(BF16)

    row = pl.BlockSpec((tr, C), lambda i: (i, 0))
    vec = pl.BlockSpec((1, C), lambda i: (0, 0))
    wsp = pl.BlockSpec((C, C), lambda i: (0, 0))
    return pl.pallas_call(body, grid=(L // tr,), in_specs=[row, row, vec, wsp, vec, row],
                          out_specs=[row, row, row, row, vec, vec],
                          out_shape=[jax.ShapeDtypeStruct((L, C), F32), jax.ShapeDtypeStruct((L, C), F32),
                                     jax.ShapeDtypeStruct((L, C), BF16), jax.ShapeDtypeStruct((L, C), BF16),
                                     jax.ShapeDtypeStruct((1, C), F32), jax.ShapeDtypeStruct((1, C), F32)],
                          compiler_params=_params(dimension_semantics=("arbitrary",)), name=name)(
        y0, proj, dvec, glu_w, glu_b, dcat)


def _dot_tri(tri, x, tri_left=True):
    t = tri.astype(BF16)
    x1 = x.astype(BF16)
    r1 = x - x1.astype(F32)
    x2 = r1.astype(BF16)
    x3 = (r1 - x2.astype(F32)).astype(BF16)
    dot = (lambda p: jnp.dot(t, p, preferred_element_type=F32)) if tri_left else (
        lambda p: jnp.dot(p, t, preferred_element_type=F32))
    return dot(x1) + dot(x2) + dot(x3)


def _hg_gates(xq, xf, lb, tri):
    C = xq.shape[0]
    sq = _sigmoid(xq)
    q = xq * sq
    sg = _sigmoid(xf)
    f = lb + (1.0 - lb) * sg
    kk = 1.0 - f
    b = _dot_tri(tri, jnp.log(f))
    bm = b[C // 2 - 1:C // 2, :]
    bl = b[C - 1:C, :]
    eb = jnp.exp(b)
    eqm, ekm, ekl = jnp.exp(b - bm), jnp.exp(bm - b), jnp.exp(bl - b)
    return dict(sq=sq, q=q, sg=sg, f=f, kk=kk, eb=eb, ebl=jnp.exp(bl), eqm=eqm, ekm=ekm, ekl=ekl,
                qb=q * eb, qt=q * eqm, kt=kk * ekm, kh=kk * ekl)


def _tri(C, lower):
    r = lax.broadcasted_iota(jnp.int32, (C, C), 0)
    c = lax.broadcasted_iota(jnp.int32, (C, C), 1)
    return (r >= c) if lower else (c >= r)


def hgrn_fwd(proj, lb, norm_g, cat, *, name):
    L = proj.shape[0]
    C, H, K = HG_CHUNK, HG_HEADS, HG_DIM
    HK = H * K
    nc = L // C

    def body(q_ref, f_ref, i_ref, g_ref, lb_ref, ng_ref, cat_ref, o_ref, sall_ref, st_ref):
        @pl.when(pl.program_id(0) == 0)
        def _():
            st_ref[...] = jnp.zeros_like(st_ref)

        mask = _tri(C, True)
        sts = [st_ref[h] for h in range(H)]
        for s in range(S):
            rs = slice(s * C, (s + 1) * C)
            gt = _hg_gates(q_ref[rs, :], f_ref[rs, :], lb_ref[...], mask.astype(F32))
            v_all = i_ref[rs, :]
            outs = []
            for h in range(H):
                sl = slice(h * K, (h + 1) * K)
                v, st = v_all[:, sl], sts[h]
                sall_ref[s, h] = st
                att = jnp.where(mask, _dot_nt(gt["qt"][:, sl], gt["kt"][:, sl]), 0.0)
                o = _dot(att, v) + _dot_nt(gt["qb"][:, sl], st)
                sts[h] = st * gt["ebl"][:, sl] + _dot_tn(v, gt["kh"][:, sl])
                outs.append(o * lax.rsqrt(jnp.mean(o * o, axis=-1, keepdims=True) + NORM_EPS))
            xg = g_ref[rs, :]
            o_ref[rs, :] = (jnp.concatenate(outs, axis=1) * ng_ref[...] * (xg * _sigmoid(xg))).astype(o_ref.dtype)
        for h in range(H):
            st_ref[h] = sts[h]

    S = HG_STEP_CHUNKS

    def blk(cb):
        return pl.BlockSpec((S * C, HK), lambda i: (i, cb))

    vec = pl.BlockSpec((1, HK), lambda i: (0, 0))
    return pl.pallas_call(
        body, grid=(nc // S,), in_specs=[blk(1), blk(2), blk(3), blk(4), vec, vec, pl.BlockSpec(memory_space=pl.ANY)],
        out_specs=[pl.BlockSpec((S * C, HK), lambda i: (i, 1)), pl.BlockSpec((S, H, K, K), lambda i: (i, 0, 0, 0))],
        out_shape=[jax.ShapeDtypeStruct((L, 2 * HK), BF16), jax.ShapeDtypeStruct((nc, H, K, K), F32)],
        input_output_aliases={6: 0},
        scratch_shapes=[pltpu.VMEM((H, K, K), F32)],
        compiler_params=_params(dimension_semantics=("arbitrary",)), name=name,
    )(proj, proj, proj, proj, lb, norm_g, cat)


def hgrn_bwd(proj, lb, norm_g, sall, dcat, du, *, name):
    L = proj.shape[0]
    C, H, K = HG_CHUNK, HG_HEADS, HG_DIM
    HK = H * K
    nc = L // C

    def body(q_ref, f_ref, i_ref, g_ref, lb_ref, ng_ref, sall_ref, do_ref, du_ref, dx_ref, dlb_ref, dng_ref, dst_ref):
        @pl.when(pl.program_id(0) == 0)
        def _():
            dst_ref[...] = jnp.zeros_like(dst_ref)
            dlb_ref[...] = jnp.zeros_like(dlb_ref)
            dng_ref[...] = jnp.zeros_like(dng_ref)

        mask = _tri(C, True)
        lb_all, ng = lb_ref[...], ng_ref[...]
        dx_ref[:, 0:HK] = du_ref[...]
        dsts = [dst_ref[h] for h in range(H)]
        for s in reversed(range(S)):
            rs = slice(s * C, (s + 1) * C)
            dsts = chunk_bwd(rs, s, dsts, mask, lb_all, ng, q_ref, f_ref, i_ref, g_ref, sall_ref, do_ref,
                             dx_ref, dlb_ref, dng_ref)
        for h in range(H):
            dst_ref[h] = dsts[h]

    def chunk_bwd(rs, s, dsts, mask, lb_all, ng, q_ref, f_ref, i_ref, g_ref, sall_ref, do_ref, dx_ref, dlb_ref, dng_ref):
        xq, xg, v_all = q_ref[rs, :], g_ref[rs, :], i_ref[rs, :]
        gt = _hg_gates(xq, f_ref[rs, :], lb_all, mask.astype(F32))
        sgg = _sigmoid(xg)
        d_ob = do_ref[rs, :]
        d_on = d_ob * (xg * sgg)
        doh = d_on * ng
        ohs, d_qts, d_qbs, d_kts, d_khs, dvs, d_bls, new_dsts = [], [], [], [], [], [], [], []
        for h in range(H):
            sl = slice(h * K, (h + 1) * K)
            v, st, dst = v_all[:, sl], sall_ref[s, h], dsts[h]
            qt, kt, kh, qb = gt["qt"][:, sl], gt["kt"][:, sl], gt["kh"][:, sl], gt["qb"][:, sl]
            att = jnp.where(mask, _dot_nt(qt, kt), 0.0)
            o = _dot(att, v) + _dot_nt(qb, st)
            r = lax.rsqrt(jnp.mean(o * o, axis=-1, keepdims=True) + NORM_EPS)
            oh = o * r
            do = r * (doh[:, sl] - oh * jnp.mean(doh[:, sl] * oh, axis=-1, keepdims=True))
            datt = jnp.where(mask, _dot_nt(do, v), 0.0)
            dvs.append(_dot_tn(att, do) + _dot_nt(kh, dst))
            d_qbs.append(_dot_x3(do, st))
            d_qts.append(_dot_x3(datt, kt))
            d_kts.append(_dot_x3(datt, qt, ((0,), (0,))))
            d_kh = _dot_x3(v, dst)
            d_khs.append(d_kh)
            d_bls.append(jnp.sum(dst * st, axis=0, keepdims=True) * gt["ebl"][:, sl]
                         + jnp.sum(d_kh * kh, axis=0, keepdims=True))
            new_dsts.append(dst * gt["ebl"][:, sl] + _dot_tn(do, qb))
            ohs.append(oh)
        oh, d_qt, d_qb, d_kt, d_kh, dv, d_bl = (jnp.concatenate(p, axis=1) for p in
                                                (ohs, d_qts, d_qbs, d_kts, d_khs, dvs, d_bls))
        dxg = d_ob * (oh * ng) * (sgg * (1.0 + xg * (1.0 - sgg)))
        dng_ref[...] += jnp.sum(d_on * oh, axis=0, keepdims=True)
        dq = d_qt * gt["eqm"] + d_qb * gt["eb"]
        db = d_qt * gt["qt"] + d_qb * gt["qb"] - d_kt * gt["kt"] - d_kh * gt["kh"]
        rowi = lax.broadcasted_iota(jnp.int32, (C, HK), 0)
        db = db + jnp.where(rowi == C - 1, d_bl, 0.0)
        dkk = d_kt * gt["ekm"] + d_kh * gt["ekl"]
        dlg = _dot_tri(_tri(C, False).astype(F32), db)
        df = dlg / gt["f"] - dkk
        sg, sq = gt["sg"], gt["sq"]
        dlb_ref[...] += jnp.sum(df * (1.0 - sg), axis=0, keepdims=True)
        dx_ref[rs, HK:2 * HK] = (dq * (sq * (1.0 + xq * (1.0 - sq)))).astype(dx_ref.dtype)
        dx_ref[rs, 2 * HK:3 * HK] = (df * (1.0 - lb_all) * sg * (1.0 - sg)).astype(dx_ref.dtype)
        dx_ref[rs, 3 * HK:4 * HK] = dv.astype(dx_ref.dtype)
        dx_ref[rs, 4 * HK:5 * HK] = dxg.astype(dx_ref.dtype)
        return new_dsts

    S = HG_STEP_CHUNKS
    ns = nc // S

    def blk(cb):
        return pl.BlockSpec((S * C, HK), lambda i: (ns - 1 - i, cb))

    vec = pl.BlockSpec((1, HK), lambda i: (0, 0))
    return pl.pallas_call(
        body, grid=(ns,),
        in_specs=[blk(1), blk(2), blk(3), blk(4), vec, vec,
                  pl.BlockSpec((S, H, K, K), lambda i: (ns - 1 - i, 0, 0, 0)), blk(1), blk(0)],
        out_specs=[pl.BlockSpec((S * C, 5 * HK), lambda i: (ns - 1 - i, 0)), vec, vec],
        out_shape=[jax.ShapeDtypeStruct((L, 5 * HK), BF16), jax.ShapeDtypeStruct((1, HK), F32),
                   jax.ShapeDtypeStruct((1, HK), F32)],
        scratch_shapes=[pltpu.VMEM((H, K, K), F32)],
        compiler_params=_params(dimension_semantics=("arbitrary",)), name=name,
    )(proj, proj, proj, proj, lb, norm_g, sall, dcat, du)


def _shift_down(x, k, row):
    return jnp.where(row >= k, pltpu.roll(x, k, 0), 0.0)


def _shift_up(x, k, row):
    n = x.shape[0]
    return jnp.where(row < n - k, pltpu.roll(x, n - k, 0), 0.0)


def convgate_fwd(hu, conv_w, conv_b, *, name):
    L, C2 = hu.shape
    C = C2 // 2
    tc = _pick(C, (256, 128))
    nb = C // tc

    def body(a_ref, b_ref, wa_ref, wb_ref, ba_ref, bb_ref, o_ref):
        row = lax.broadcasted_iota(jnp.int32, (L, tc), 0)

        def conv(x, w, bias):
            return w[2:3, :] * x + w[1:2, :] * _shift_down(x, 1, row) + w[0:1, :] * _shift_down(x, 2, row) + bias

        ca = conv(a_ref[...], wa_ref[...], ba_ref[...])
        cb = conv(b_ref[...], wb_ref[...], bb_ref[...])
        o_ref[...] = (ca * _sigmoid(ca) * cb).astype(o_ref.dtype)

    def col(off, rows):
        return pl.BlockSpec((rows, tc), lambda j: (0, j + off))

    return pl.pallas_call(
        body, grid=(nb,), in_specs=[col(0, L), col(nb, L), col(0, 3), col(nb, 3), col(0, 1), col(nb, 1)],
        out_specs=col(0, L), out_shape=jax.ShapeDtypeStruct((L, C), BF16),
        compiler_params=_params(dimension_semantics=("parallel",)), name=name,
    )(hu, hu, conv_w, conv_w, conv_b, conv_b)


def convgate_bwd(hu, conv_w, conv_b, dact, *, name):
    L, C2 = hu.shape
    C = C2 // 2
    tc = _pick(C, (256, 128))
    nb = C // tc

    R = CONV_ROWS
    Wn = R + 2 * CONV_HALO

    def body(a_ref, b_ref, wa_ref, wb_ref, ba_ref, bb_ref, d_ref, dxa_ref, dxb_ref, dwa_ref, dwb_ref, dba_ref, dbb_ref,
             sa_ref, sb_ref):
        wa, wb, ba, bb = wa_ref[...], wb_ref[...], ba_ref[...], bb_ref[...]
        wrow = lax.broadcasted_iota(jnp.int32, (Wn, tc), 0)

        def chunk(i, acc):
            r0 = pl.multiple_of(i * R, R)
            ws = pl.multiple_of(jnp.clip(r0 - CONV_HALO, 0, L - Wn), 8)
            co = pl.multiple_of(r0 - ws, 8)
            g = wrow + ws

            def down(x, k):
                return jnp.where(g >= k, pltpu.roll(x, k, 0), 0.0)

            def up(x, k):
                return jnp.where(g < L - k, pltpu.roll(x, Wn - k, 0), 0.0)

            win = pl.ds(ws, Wn)
            xa, xb, d = a_ref[win, :], b_ref[win, :], d_ref[win, :]
            xa1, xa2, xb1, xb2 = down(xa, 1), down(xa, 2), down(xb, 1), down(xb, 2)
            ca = wa[2:3, :] * xa + wa[1:2, :] * xa1 + wa[0:1, :] * xa2 + ba
            cb = wb[2:3, :] * xb + wb[1:2, :] * xb1 + wb[0:1, :] * xb2 + bb
            sa = _sigmoid(ca)
            dca = d * cb * (sa * (1.0 + ca * (1.0 - sa)))
            dcb = d * (ca * sa)
            central = (wrow >= co) & (wrow < co + R)
            sums = []
            for dc, w, x, x1, x2, s_ref in ((dca, wa, xa, xa1, xa2, sa_ref), (dcb, wb, xb, xb1, xb2, sb_ref)):
                s_ref[...] = w[2:3, :] * dc + w[1:2, :] * up(dc, 1) + w[0:1, :] * up(dc, 2)
                dcm = jnp.where(central, dc, 0.0)
                sums += [jnp.sum((dcm * t).reshape(Wn // 8, 8, tc), axis=0) for t in (x2, x1, x)]
                sums.append(jnp.sum(dcm.reshape(Wn // 8, 8, tc), axis=0))
            dxa_ref[pl.ds(r0, R), :] = sa_ref[pl.ds(co, R), :].astype(dxa_ref.dtype)
            dxb_ref[pl.ds(r0, R), :] = sb_ref[pl.ds(co, R), :].astype(dxb_ref.dtype)
            return tuple(a + s for a, s in zip(acc, sums))

        zero = jnp.zeros((8, tc), F32)
        acc = lax.fori_loop(0, L // R, chunk, (zero,) * 8, unroll=2)
        acc = [jnp.sum(a, axis=0, keepdims=True) for a in acc]
        dwa_ref[...] = jnp.concatenate(acc[0:3], axis=0)
        dba_ref[...] = acc[3]
        dwb_ref[...] = jnp.concatenate(acc[4:7], axis=0)
        dbb_ref[...] = acc[7]

    def col(off, rows):
        return pl.BlockSpec((rows, tc), lambda j: (0, j + off))

    outs = pl.pallas_call(
        body, grid=(nb,),
        in_specs=[col(0, L), col(nb, L), col(0, 3), col(nb, 3), col(0, 1), col(nb, 1), col(0, L)],
        out_specs=[col(0, L), col(0, L), col(0, 3), col(0, 3), col(0, 1), col(0, 1)],
        out_shape=[jax.ShapeDtypeStruct((L, C), BF16)] * 2 + [jax.ShapeDtypeStruct((3, C), F32)] * 2
        + [jax.ShapeDtypeStruct((1, C), F32)] * 2,
        scratch_shapes=[pltpu.VMEM((Wn, tc), F32)] * 2,
        compiler_params=_params(dimension_semantics=("parallel",)), name=name,
    )(hu, hu, conv_w, conv_w, conv_b, conv_b, dact)
    dxa, dxb, dwa, dwb, dba, dbb = outs
    return (dxa, dxb), jnp.concatenate([dwa, dwb], axis=1), jnp.concatenate([dba, dbb], axis=1)


def rope_tables(positions):
    half = ROT_DIM // 2
    inv_freq = ROPE_THETA ** (-jnp.arange(half, dtype=F32) * 2.0 / ROT_DIM)
    ang = positions.astype(F32)[:, None] * inv_freq
    cos, sin = jnp.cos(ang), jnp.sin(ang)
    L = positions.shape[0]
    one = jnp.ones((L, ATT_E - ROT_DIM), F32)
    zero = jnp.zeros((L, ATT_E - ROT_DIM), F32)
    zh = jnp.zeros((L, half), F32)
    tc = jnp.concatenate([cos, cos, one], axis=1)
    ts1 = jnp.concatenate([zh, sin, zero], axis=1)
    ts2 = jnp.concatenate([-sin, zh, zero], axis=1)
    return tuple(jnp.concatenate([t, t], axis=1) for t in (tc, ts1, ts2))


def norm_mm(x, g, w_t, *, tabs=None, name):
    L, D = x.shape
    N = w_t.shape[0]
    W = 512
    tm = _pick(L, (2048, 1024, 512, 256, 128))
    nq = N // (3 * W)
    scale = ATT_E ** -0.5
    rope = tabs is not None

    def body(x_ref, g_ref, b_ref, *rest):
        hn_ref, o_ref, hn_scr = rest[-3:]
        j = pl.program_id(1)

        @pl.when(j == 0)
        def _():
            xv = x_ref[...]
            r = lax.rsqrt(jnp.mean(xv * xv, axis=-1, keepdims=True) + NORM_EPS)
            hn = (xv * r * g_ref[...]).astype(BF16)
            hn_scr[...] = hn
            hn_ref[...] = hn

        out = _dot_nt(hn_scr[...], b_ref[...])
        if rope:
            c_ref, s1_ref, s2_ref = rest[:3]
            c = jnp.concatenate([c_ref[...]] * 4, axis=1)
            s1 = jnp.concatenate([s1_ref[...]] * 4, axis=1)
            s2 = jnp.concatenate([s2_ref[...]] * 4, axis=1)
            rot = out * c + pltpu.roll(out, 8, 1) * s1 + pltpu.roll(out, W - 8, 1) * s2
            out = jnp.where(j < 2 * nq, rot * jnp.where(j < nq, scale, 1.0), out)
        o_ref[...] = out

    row = pl.BlockSpec((tm, D), lambda i, j: (i, 0))
    tab = pl.BlockSpec((tm, 128), lambda i, j: (i, 0))
    return pl.pallas_call(body, grid=(L // tm, N // W),
                          in_specs=[row, pl.BlockSpec((1, D), lambda i, j: (0, 0)), pl.BlockSpec((W, D), lambda i, j: (j, 0))]
                          + ([tab, tab, tab] if rope else []),
                          out_specs=[row, pl.BlockSpec((tm, W), lambda i, j: (i, j))],
                          out_shape=[jax.ShapeDtypeStruct((L, D), BF16), jax.ShapeDtypeStruct((L, N), F32)],
                          scratch_shapes=[pltpu.VMEM((tm, D), BF16)],
                          compiler_params=_params(dimension_semantics=("parallel", "arbitrary")), name=name)(
        x, g.reshape(1, D), w_t, *(tabs or ()))


def rope_bwd(slabs, tabs, *, name):
    L, W = slabs[0].shape
    tr = _pick(L, (256, 128))
    nq = len(slabs) // 3
    scale = ATT_E ** -0.5

    def body(*refs):
        d_refs, (c_ref, s1_ref, s2_ref, o_ref) = refs[:3 * nq], refs[3 * nq:]
        c = jnp.concatenate([c_ref[...]] * 4, axis=1)
        s1 = jnp.concatenate([s1_ref[...]] * 4, axis=1)
        s2 = jnp.concatenate([s2_ref[...]] * 4, axis=1)
        for j, d_ref in enumerate(d_refs):
            dy = d_ref[...]
            if j < 2 * nq:
                dy = dy * c + pltpu.roll(dy * s1, W - 8, 1) + pltpu.roll(dy * s2, 8, 1)
            if j < nq:
                dy = dy * scale
            o_ref[:, j * W:(j + 1) * W] = dy.astype(o_ref.dtype)

    slab = pl.BlockSpec((tr, W), lambda i: (i, 0))
    tab = pl.BlockSpec((tr, 128), lambda i: (i, 0))
    return pl.pallas_call(body, grid=(L // tr,), in_specs=[slab] * (3 * nq) + [tab, tab, tab],
                          out_specs=pl.BlockSpec((tr, 3 * nq * W), lambda i: (i, 0)),
                          out_shape=jax.ShapeDtypeStruct((L, 3 * nq * W), BF16),
                          compiler_params=_params(dimension_semantics=("parallel",)), name=name)(*slabs, *tabs)


def _att_masks(has_prev):
    qi = lax.broadcasted_iota(jnp.int32, (ATT_BLOCK, ATT_BLOCK), 0)
    kj = lax.broadcasted_iota(jnp.int32, (ATT_BLOCK, ATT_BLOCK), 1)
    return qi >= kj, (kj >= qi) & has_prev


ATT_COLS = 128


def _att_rows(j, d, nb):
    B = ATT_BLOCK
    r, n = j // nb, j % nb
    start = r + d * B * n
    has_prev = n > 0
    pstart = jnp.where(has_prev, start - d * B, start)
    if d == 1:
        return pl.ds(pl.multiple_of(start, B), B), pl.ds(pl.multiple_of(pstart, B), B), has_prev
    return pl.ds(start, B, stride=d), pl.ds(pstart, B, stride=d), has_prev


def _qkv_specs(L, g):
    per = ATT_HPG * ATT_E // ATT_COLS
    third = len(ATT_DILATIONS) * per
    return [pl.BlockSpec((L, ATT_COLS), lambda c, base=base: (0, base + c))
            for base in (g * per, third + g * per, 2 * third + g * per)]


def attn_fwd(qkv, g, d, *, name):
    L, W = qkv.shape[0], ATT_HPG * ATT_E
    B, E = ATT_BLOCK, ATT_E
    nblk = L // B
    nb = nblk // d

    def body(q_ref, k_ref, v_ref, o_ref, l_ref):
        def step(j, carry):
            cur, prv, has_prev = _att_rows(j, d, nb)
            mc, mp = _att_masks(has_prev)
            qb, kc, kp, vc, vp = q_ref[cur, :], k_ref[cur, :], k_ref[prv, :], v_ref[cur, :], v_ref[prv, :]
            outs, lses = [], []
            for h in range(ATT_COLS // E):
                sl = slice(h * E, (h + 1) * E)
                sc = jnp.where(mc, _dot_nt(qb[:, sl], kc[:, sl]), NEG_BIG)
                sp = jnp.where(mp, _dot_nt(qb[:, sl], kp[:, sl]), NEG_BIG)
                m = jnp.maximum(jnp.max(sc, axis=-1, keepdims=True), jnp.max(sp, axis=-1, keepdims=True))
                pc = jnp.exp(sc - m)
                pp = jnp.exp(sp - m)
                den = jnp.sum(pc, axis=-1, keepdims=True) + jnp.sum(pp, axis=-1, keepdims=True)
                outs.append((_dot(pc, vc[:, sl]) + _dot(pp, vp[:, sl])) / den)
                lses.append(jnp.broadcast_to(m + jnp.log(den), (B, E)))
            o_ref[cur, :] = jnp.concatenate(outs, axis=1)
            l_ref[cur, :] = jnp.concatenate(lses, axis=1)
            return carry

        lax.fori_loop(0, nblk, step, 0, unroll=4)

    col = pl.BlockSpec((L, ATT_COLS), lambda c: (0, c))
    return pl.pallas_call(body, grid=(W // ATT_COLS,), in_specs=_qkv_specs(L, g), out_specs=[col] * 2,
                          out_shape=[jax.ShapeDtypeStruct((L, W), F32)] * 2,
                          compiler_params=_params(dimension_semantics=("parallel",)), name=name)(qkv, qkv, qkv)


def attn_bwd(qkv, g, lse, do, dl, d, *, name):
    L, W = qkv.shape[0], ATT_HPG * ATT_E
    B, E = ATT_BLOCK, ATT_E
    nblk = L // B
    nb = nblk // d

    def body(q_ref, k_ref, v_ref, l_ref, do_ref, dl_ref, dq_ref, dk_ref, dv_ref):
        dk_ref[...] = jnp.zeros_like(dk_ref)
        dv_ref[...] = jnp.zeros_like(dv_ref)

        def step(j, carry):
            cur, prv, has_prev = _att_rows(j, d, nb)
            mc, mp = _att_masks(has_prev)
            qb, kc, kp, vc, vp = q_ref[cur, :], k_ref[cur, :], k_ref[prv, :], v_ref[cur, :], v_ref[prv, :]
            lb, dob, dlb = l_ref[cur, :], do_ref[cur, :], dl_ref[cur, :]
            dqs, dkc, dkp, dvc, dvp = [], [], [], [], []
            for h in range(ATT_COLS // E):
                sl = slice(h * E, (h + 1) * E)
                qh, doh = qb[:, sl], dob[:, sl]
                lse_h, dl_h = lb[:, h * E:h * E + 1], dlb[:, h * E:h * E + 1]
                pc = jnp.where(mc, jnp.exp(_dot_nt(qh, kc[:, sl]) - lse_h), 0.0)
                pp = jnp.where(mp, jnp.exp(_dot_nt(qh, kp[:, sl]) - lse_h), 0.0)
                dsc = pc * (_dot_nt(doh, vc[:, sl]) - dl_h)
                dsp = pp * (_dot_nt(doh, vp[:, sl]) - dl_h)
                dqs.append(_dot(dsc, kc[:, sl]) + _dot(dsp, kp[:, sl]))
                dkc.append(_dot_tn(dsc, qh))
                dkp.append(_dot_tn(dsp, qh))
                dvc.append(_dot_tn(pc, doh))
                dvp.append(_dot_tn(pp, doh))
            dq_ref[cur, :] = jnp.concatenate(dqs, axis=1)
            dk_ref[cur, :] = dk_ref[cur, :] + jnp.concatenate(dkc, axis=1)
            dv_ref[cur, :] = dv_ref[cur, :] + jnp.concatenate(dvc, axis=1)
            dk_ref[prv, :] = dk_ref[prv, :] + jnp.concatenate(dkp, axis=1)
            dv_ref[prv, :] = dv_ref[prv, :] + jnp.concatenate(dvp, axis=1)
            return carry

        lax.fori_loop(0, nblk, step, 0, unroll=4)

    col = pl.BlockSpec((L, ATT_COLS), lambda c: (0, c))
    return pl.pallas_call(body, grid=(W // ATT_COLS,), in_specs=_qkv_specs(L, g) + [col] * 3, out_specs=[col] * 3,
                          out_shape=[jax.ShapeDtypeStruct((L, W), F32)] * 3,
                          compiler_params=_params(dimension_semantics=("parallel",)), name=name)(
        qkv, qkv, qkv, lse, do, dl)


def _merge_alpha(l_refs):
    ls = [r[...] for r in l_refs]
    m = jnp.maximum(jnp.maximum(ls[0], ls[1]), ls[2])
    es = [jnp.exp(l - m) for l in ls]
    den = es[0] + es[1] + es[2]
    return [e / den for e in es]


def merge_fwd(os_, ls_, *, name):
    L, W = os_[0].shape
    tr = _pick(L, (256, 128))

    def body(o0, o1, o2, l0, l1, l2, out_ref):
        al = _merge_alpha((l0, l1, l2))
        out_ref[...] = (al[0] * o0[...] + al[1] * o1[...] + al[2] * o2[...]).astype(out_ref.dtype)

    row = pl.BlockSpec((tr, W), lambda i: (i, 0))
    return pl.pallas_call(body, grid=(L // tr,), in_specs=[row] * 6, out_specs=row,
                          out_shape=jax.ShapeDtypeStruct((L, W), BF16), name=name)(*os_, *ls_)


def merge_bwd(os_, ls_, do, *, name):
    L, W = do.shape
    tr = _pick(L, (256, 128))

    def body(o0, o1, o2, l0, l1, l2, do_ref, d0, d1, d2, e0, e1, e2):
        al = _merge_alpha((l0, l1, l2))
        dov = do_ref[...]
        r = lax.broadcasted_iota(jnp.int32, (W, W), 0) // ATT_E
        c = lax.broadcasted_iota(jnp.int32, (W, W), 1) // ATT_E
        ones_blk = (r == c).astype(F32)
        t = jnp.zeros_like(dov)
        for a, o in zip(al, (o0, o1, o2)):
            t = t + a * _dot_tri(ones_blk, dov * o[...], tri_left=False)
        for a, d_ref, e_ref in zip(al, (d0, d1, d2), (e0, e1, e2)):
            d_ref[...] = a * dov
            e_ref[...] = a * t

    row = pl.BlockSpec((tr, W), lambda i: (i, 0))
    return pl.pallas_call(body, grid=(L // tr,), in_specs=[row] * 7, out_specs=[row] * 6,
                          out_shape=[jax.ShapeDtypeStruct((L, W), F32)] * 6, name=name)(*os_, *ls_, do)


def _me_and_peers():
    x, y, c = lax.axis_index("x"), lax.axis_index("y"), lax.axis_index("c")
    peers = []
    for k in range(1, N_DEV):
        px = 1 - x if k & 4 else x
        py = 1 - y if k & 2 else y
        pc = 1 - c if k & 1 else c
        peers.append((px, py, pc))
    return (x, y, c), peers


def _index(dev):
    return 4 * dev[0] + 2 * dev[1] + dev[2]


def _hbm(a):
    return pltpu.with_memory_space_constraint(a, pltpu.HBM)


HBM_SPEC = pl.BlockSpec(memory_space=pltpu.HBM)
SEM_SPEC = pl.BlockSpec(memory_space=pltpu.SEMAPHORE)
DATAFLOW = pltpu.SideEffectType.DATAFLOW_SIDE_EFFECTING


def _remote(src_ref, land_ref, slotted, me, peer, src_is_mine, send_sem, recv_sem, k):
    sender, receiver = (me, peer) if src_is_mine else (peer, me)
    src = src_ref.at[_index(receiver)] if slotted else src_ref
    return pltpu.make_async_remote_copy(src_ref=src, dst_ref=land_ref.at[_index(sender)], send_sem=send_sem.at[k],
                                        recv_sem=recv_sem.at[k], device_id=peer, device_id_type=MESH_ID)


SIBLING = 0
SAME_CORE = (1, 3, 5)
OTHER_CORE = (2, 4, 6)


def copies_start(arrays, mode, *, name):
    n = len(arrays)
    slotted = mode == "exchange"
    lands = [lax.empty(a.shape if slotted else (N_DEV,) + a.shape, a.dtype) for a in arrays]
    targets = (SIBLING,) + SAME_CORE if mode == "gather2" else tuple(range(N_DEV - 1))

    def body(*refs):
        x_refs, land_refs = refs[:n], refs[n:2 * n]
        send, recv = refs[2 * n:3 * n], refs[3 * n:4 * n]
        token = refs[-1]
        me, peers = _me_and_peers()
        for w in range(n):
            for k in targets:
                _remote(x_refs[w], land_refs[w], slotted, me, peers[k], True, send[w], recv[w], k).start()
            if not slotted:
                pltpu.make_async_copy(x_refs[w], land_refs[w].at[_index(me)], recv[w].at[N_DEV - 1]).start()
        token[...] = jnp.zeros_like(token)

    sem = pltpu.SemaphoreType.DMA((N_DEV,))
    out_shape = ([sem] * (2 * n) + [pltpu.HBM(a.shape, a.dtype) for a in arrays]
                 + [pltpu.HBM(l.shape, l.dtype) for l in lands] + [jax.ShapeDtypeStruct((8, 128), F32)])
    outs = pl.pallas_call(
        body, name=name, out_shape=out_shape, in_specs=[HBM_SPEC] * (2 * n),
        out_specs=[SEM_SPEC] * (2 * n) + [HBM_SPEC] * (2 * n) + [pl.BlockSpec(memory_space=pltpu.VMEM)],
        input_output_aliases={i: 2 * n + i for i in range(2 * n)},
        compiler_params=pltpu.CompilerParams(has_side_effects=DATAFLOW),
    )(*[_hbm(a) for a in arrays], *[_hbm(l) for l in lands])
    handles = [(outs[w], outs[n + w], outs[2 * n + w], outs[3 * n + w]) for w in range(n)]
    return handles, outs[-1]


def _forward(land_ref, me, peers, j, fsend, frecv, mine):
    block = _index(peers[SAME_CORE[j]] if mine else peers[OTHER_CORE[j]])
    return pltpu.make_async_remote_copy(src_ref=land_ref.at[block], dst_ref=land_ref.at[block], send_sem=fsend.at[j],
                                        recv_sem=frecv.at[j], device_id=peers[SIBLING], device_id_type=MESH_ID)


def copies_forward(handles, after, *, name):
    n = len(handles)

    def body(*refs):
        land_refs, recv = refs[:n], refs[n:2 * n]
        fsend, frecv = refs[2 * n + 1:3 * n + 1], refs[3 * n + 1:4 * n + 1]
        token = refs[-1]
        me, peers = _me_and_peers()
        for w in range(n):
            for j, k in enumerate(SAME_CORE):
                block = land_refs[w].at[_index(peers[k])]
                pltpu.make_async_remote_copy(src_ref=block, dst_ref=block, send_sem=recv[w].at[N_DEV - 1],
                                             recv_sem=recv[w].at[k], device_id=peers[k], device_id_type=MESH_ID).wait_recv()
                _forward(land_refs[w], me, peers, j, fsend[w], frecv[w], True).start()
        token[...] = jnp.zeros_like(token)

    sem = pltpu.SemaphoreType.DMA((len(SAME_CORE),))
    lands = [h[3] for h in handles]
    outs = pl.pallas_call(
        body, name=name,
        out_shape=[sem] * (2 * n) + [pltpu.HBM(l.shape, l.dtype) for l in lands] + [jax.ShapeDtypeStruct((8, 128), F32)],
        in_specs=[HBM_SPEC] * n + [SEM_SPEC] * n + [pl.BlockSpec(memory_space=pl.ANY)],
        out_specs=[SEM_SPEC] * (2 * n) + [HBM_SPEC] * n + [pl.BlockSpec(memory_space=pltpu.VMEM)],
        input_output_aliases={w: 2 * n + w for w in range(n)},
        compiler_params=pltpu.CompilerParams(has_side_effects=DATAFLOW),
    )(*lands, *[h[1] for h in handles], after)
    new = [(h[0], h[1], h[2], outs[2 * n + w], outs[w], outs[n + w]) for w, h in enumerate(handles)]
    return new, outs[-1]


def copies_wait(handle, mode, after, *, name):
    slotted = mode == "exchange"
    two_level = mode == "gather2"
    send_sem, recv_sem, x_thru, land_thru = handle[:4]
    targets = (SIBLING,) + SAME_CORE if two_level else tuple(range(N_DEV - 1))
    arrivals = (SIBLING,) if two_level else targets

    def body(x_ref, land_ref, send_ref, recv_ref, *rest):
        me, peers = _me_and_peers()
        for k in targets:
            _remote(x_ref, land_ref, slotted, me, peers[k], True, send_ref, recv_ref, k).wait_send()
        for k in arrivals:
            _remote(x_ref, land_ref, slotted, me, peers[k], False, send_ref, recv_ref, k).wait_recv()
        if not slotted:
            pltpu.make_async_copy(x_ref, land_ref.at[_index(me)], recv_ref.at[N_DEV - 1]).wait()
        if two_level:
            fsend, frecv = rest[0], rest[1]
            for j in range(len(SAME_CORE)):
                _forward(land_ref, me, peers, j, fsend, frecv, True).wait_send()
                _forward(land_ref, me, peers, j, fsend, frecv, False).wait_recv()

    extra = list(handle[4:])
    return pl.pallas_call(
        body, name=name, out_shape=(pltpu.HBM(x_thru.shape, x_thru.dtype), pltpu.HBM(land_thru.shape, land_thru.dtype)),
        in_specs=[HBM_SPEC, HBM_SPEC, SEM_SPEC, SEM_SPEC] + [SEM_SPEC] * len(extra) + [pl.BlockSpec(memory_space=pl.ANY)],
        out_specs=(HBM_SPEC, HBM_SPEC), input_output_aliases={0: 0, 1: 1},
        compiler_params=pltpu.CompilerParams(has_side_effects=DATAFLOW),
    )(x_thru, land_thru, send_sem, recv_sem, *extra, after)


def cast_bf16(x, *, dep=None, name):
    R, C = x.shape
    tr = _pick(R, (512, 352, 256, 128, 64))
    deps = [] if dep is None else [dep]

    def body(x_ref, *rest):
        rest[-1][...] = x_ref[...].astype(BF16)

    row = pl.BlockSpec((tr, C), lambda i: (i, 0))
    return pl.pallas_call(body, grid=(R // tr,), in_specs=[row] + [pl.BlockSpec((8, 128), lambda i: (0, 0))] * len(deps),
                          out_specs=row, out_shape=jax.ShapeDtypeStruct((R, C), BF16), name=name)(x, *deps)


def cast_bf16_layer(x3, layer, *, name):
    _, R, C = x3.shape
    tr = _pick(R, (512, 352, 256, 128, 64))

    def body(x_ref, o_ref):
        o_ref[...] = x_ref[...].astype(BF16)

    return pl.pallas_call(body, grid=(R // tr,), in_specs=[pl.BlockSpec((None, tr, C), lambda i: (layer, i, 0))],
                          out_specs=pl.BlockSpec((tr, C), lambda i: (i, 0)),
                          out_shape=jax.ShapeDtypeStruct((R, C), BF16), name=name)(x3)


BD_PARTS = 4


def _blockdiag_call(b, build, G, r, c, name):
    gp = G // BD_PARTS

    def body_build(b_ref, o_ref):
        o_ref[...] = jnp.zeros_like(o_ref)
        for g in range(G):
            o_ref[g // gp, (g % gp) * r:(g % gp + 1) * r, (g % gp) * c:(g % gp + 1) * c] = b_ref[g]

    def body_extract(d_ref, o_ref):
        for g in range(G):
            o_ref[g] = d_ref[g // gp, (g % gp) * r:(g % gp + 1) * r, (g % gp) * c:(g % gp + 1) * c]

    out = jax.ShapeDtypeStruct((BD_PARTS, gp * r, gp * c) if build else (G, r, c), F32)
    return pl.pallas_call(body_build if build else body_extract, out_shape=out, name=name)(b)


def make_blockdiag(G, r, c, name):
    @jax.custom_vjp
    def blockdiag(b):
        return _blockdiag_call(b, True, G, r, c, name + "_build")

    def fwd(b):
        return blockdiag(b), None

    def bwd(_, g):
        return (_blockdiag_call(g, False, G, r, c, name + "_extract"),)

    blockdiag.defvjp(fwd, bwd)
    return blockdiag


def cols_from_shards(g, *, name):
    _, K, n = g.shape
    tk = _pick(K, (256, 128))

    def body(g_ref, o_ref):
        for i in range(N_DEV):
            o_ref[:, i * n:(i + 1) * n] = g_ref[i]

    return pl.pallas_call(body, grid=(K // tk,), in_specs=[pl.BlockSpec((N_DEV, tk, n), lambda i: (0, i, 0))],
                          out_specs=pl.BlockSpec((tk, N_DEV * n), lambda i: (i, 0)),
                          out_shape=jax.ShapeDtypeStruct((K, N_DEV * n), g.dtype), name=name)(g)


def shards_from_cols(w, *, name):
    K, N = w.shape
    n = N // N_DEV
    tk = _pick(K, (256, 128))

    def body(w_ref, o_ref):
        for i in range(N_DEV):
            o_ref[i] = w_ref[:, i * n:(i + 1) * n].astype(o_ref.dtype)

    return pl.pallas_call(body, grid=(K // tk,), in_specs=[pl.BlockSpec((tk, N), lambda i: (i, 0))],
                          out_specs=pl.BlockSpec((N_DEV, tk, n), lambda i: (0, i, 0)),
                          out_shape=jax.ShapeDtypeStruct((N_DEV, K, n), BF16), name=name)(w)


def _adamw(w, g, m, v):
    m = ADAM_B1 * m + (1.0 - ADAM_B1) * g
    v = ADAM_B2 * v + (1.0 - ADAM_B2) * (g * g)
    m_hat = m / (1.0 - ADAM_B1 ** ADAM_STEP)
    v_hat = v / (1.0 - ADAM_B2 ** ADAM_STEP)
    delta = -ADAM_LR * (m_hat / (jnp.sqrt(v_hat) + ADAM_EPS) + ADAM_WD * w)
    return delta, m, v


def reduce_adamw(recv, own, own_slotted, me, w, m, v, *, layer=0, n_layers=1, into=None, name):
    _, R, C = recv.shape
    tr = _pick(R, (176, 192, 160, 184, 128, 64, 32, 16, 8))
    off = layer * (R // tr)

    def body(me_ref, r_ref, own_ref, w_ref, m_ref, v_ref, *rest):
        g_ref, d_ref, nm_ref, nv_ref = rest[-4:]
        mine = me_ref[0]
        g = None
        for i in range(N_DEV):
            part = jnp.where(mine == i, own_ref[...], r_ref[i]).astype(F32)
            g = part if g is None else g + part
        delta, nm, nv = _adamw(w_ref[...], g, m_ref[...], v_ref[...])
        g_ref[...] = g
        d_ref[...] = delta
        nm_ref[...] = nm
        nv_ref[...] = nv

    row = pl.BlockSpec((tr, C), lambda i, me_ref: (i + off, 0))
    own_spec = (pl.BlockSpec((None, tr, C), lambda i, me_ref: (me_ref[0], i, 0)) if own_slotted
                else pl.BlockSpec((tr, C), lambda i, me_ref: (i, 0)))
    rest = [] if into is None else list(into)
    grid_spec = pltpu.PrefetchScalarGridSpec(
        num_scalar_prefetch=1, grid=(R // tr,),
        in_specs=[pl.BlockSpec((N_DEV, tr, C), lambda i, me_ref: (0, i, 0)), own_spec, row, row, row]
        + [pl.BlockSpec(memory_space=pl.ANY)] * len(rest),
        out_specs=[row] * 4)
    return pl.pallas_call(body, grid_spec=grid_spec, out_shape=[jax.ShapeDtypeStruct((n_layers * R, C), F32)] * 4,
                          input_output_aliases={6 + k: k for k in range(len(rest))},
                          compiler_params=_params(dimension_semantics=("parallel",)), name=name)(
        me.reshape(1).astype(jnp.int32), recv, own, w, m, v, *rest)


def _s5_prepare(A_re, A_im, log_dt, B_re, B_im, C_re, C_im):
    G, P, Cg = S5_GROUPS, S5_STATE, S5_GROUP
    dt = jnp.exp(log_dt)[:, None]
    mag = jnp.exp(A_re * dt)
    ab_re = mag * jnp.cos(A_im * dt)
    ab_im = mag * jnp.sin(A_im * dt)
    den = A_re * A_re + A_im * A_im
    nr, ni = ab_re - 1.0, ab_im
    c_re = (nr * A_re + ni * A_im) / den
    c_im = (ni * A_re - nr * A_im) / den
    Bb_re = c_re[..., None] * B_re - c_im[..., None] * B_im
    Bb_im = c_re[..., None] * B_im + c_im[..., None] * B_re
    def dense_in(b, name):
        return make_blockdiag(G, Cg, P, name)(b.transpose(0, 2, 1))

    def dense_out(c, name):
        return make_blockdiag(G, P, Cg, name)(c.transpose(0, 2, 1))

    return (ab_re.reshape(1, G * P), ab_im.reshape(1, G * P), dense_in(Bb_re, "s5_wb_re"), dense_in(Bb_im, "s5_wb_im"),
            dense_out(C_re, "s5_wc_re"), dense_out(-C_im, "s5_wc_im"))


def _lower_bound(gamma):
    return jnp.cumsum(jax.nn.softmax(gamma, axis=0), axis=0)[0:1]


def _ffn_fwd(h, g_norm, get_w_in, conv_w, conv_b, get_w_out, tag, final=None):
    w_in = get_w_in(h)
    hn, hu = norm_mm(h, g_norm, w_in, name=tag + "_in")
    act = convgate_fwd(hu, conv_w, conv_b, name=tag + "_gate")
    w_out = get_w_out(act)
    if final is None:
        h_out = mm(act, w_out, res=h, name=tag + "_out")
    else:
        h_out = mm_final_loss(act, w_out, h, final[0], final[1], name=tag + "_out_loss")
    return h_out, (hn, hu, act), w_in, w_out


def _ffn_bwd(h, g_norm, w_in, conv_w, conv_b, w_out, saved, dh, tag, send_dw_in, send_dw_out):
    hn, hu, act = saved
    sent = send_dw_out(mm(act, dh, ta=True, out_dtype=BF16, name=tag + "_dwout"))
    dact = mm(dh, w_out, tb=True, dep=sent, name=tag + "_dact")
    (dhu_a, dhu_b), dconv_w, dconv_b = convgate_bwd(hu, conv_w, conv_b, dact, name=tag + "_dgate")
    rows = 2 * dhu_a.shape[1]
    dw_in = mm(dhu_a, hn, ta=True, out_dtype=BF16, out_rows=rows, name=tag + "_dwin_a")
    dw_in = mm(dhu_b, hn, ta=True, out_dtype=BF16, out_rows=rows, out_off=rows // 2, into=dw_in, name=tag + "_dwin_b")
    sent = send_dw_in(dw_in)
    dh_in, dg = mm_drms((dhu_a, dhu_b), w_in, h, g_norm, dh, dep=sent, name=tag + "_dhn")
    return dh_in, dg, dconv_w, dconv_b


def kernel(x, positions, norm_mix, norm_ffn, norm_final, mix_w_in, mix_w_out, s5_A_re, s5_A_im, s5_log_dt, s5_B_re, s5_B_im, s5_C_re, s5_C_im, s5_D, s5_glu_w, s5_glu_b, hgrn_gamma, hgrn_norm, att_w_qkv, att_w_o, ffn_w_in, ffn_conv_w, ffn_conv_b, ffn_w_out, loss_target, m_norm_mix, m_norm_ffn, m_norm_final, m_mix_w_in, m_mix_w_out, m_s5_A_re, m_s5_A_im, m_s5_log_dt, m_s5_B_re, m_s5_B_im, m_s5_C_re, m_s5_C_im, m_s5_D, m_s5_glu_w, m_s5_glu_b, m_hgrn_gamma, m_hgrn_norm, m_att_w_qkv, m_att_w_o, m_ffn_w_in, m_ffn_conv_w, m_ffn_conv_b, m_ffn_w_out, v_norm_mix, v_norm_ffn, v_norm_final, v_mix_w_in, v_mix_w_out, v_s5_A_re, v_s5_A_im, v_s5_log_dt, v_s5_B_re, v_s5_B_im, v_s5_C_re, v_s5_C_im, v_s5_D, v_s5_glu_w, v_s5_glu_b, v_hgrn_gamma, v_hgrn_norm, v_att_w_qkv, v_att_w_o, v_ffn_w_in, v_ffn_conv_w, v_ffn_conv_b, v_ffn_w_out):
    W = dict(norm_mix=norm_mix, norm_ffn=norm_ffn, norm_final=norm_final, mix_w_in=mix_w_in, mix_w_out=mix_w_out,
             s5_A_re=s5_A_re, s5_A_im=s5_A_im, s5_log_dt=s5_log_dt, s5_B_re=s5_B_re, s5_B_im=s5_B_im,
             s5_C_re=s5_C_re, s5_C_im=s5_C_im, s5_D=s5_D, s5_glu_w=s5_glu_w, s5_glu_b=s5_glu_b,
             hgrn_gamma=hgrn_gamma, hgrn_norm=hgrn_norm, att_w_qkv=att_w_qkv, att_w_o=att_w_o, ffn_w_in=ffn_w_in,
             ffn_conv_w=ffn_conv_w, ffn_conv_b=ffn_conv_b, ffn_w_out=ffn_w_out)
    M = dict(norm_mix=m_norm_mix, norm_ffn=m_norm_ffn, norm_final=m_norm_final, mix_w_in=m_mix_w_in,
             mix_w_out=m_mix_w_out, s5_A_re=m_s5_A_re, s5_A_im=m_s5_A_im, s5_log_dt=m_s5_log_dt, s5_B_re=m_s5_B_re,
             s5_B_im=m_s5_B_im, s5_C_re=m_s5_C_re, s5_C_im=m_s5_C_im, s5_D=m_s5_D, s5_glu_w=m_s5_glu_w,
             s5_glu_b=m_s5_glu_b, hgrn_gamma=m_hgrn_gamma, hgrn_norm=m_hgrn_norm, att_w_qkv=m_att_w_qkv,
             att_w_o=m_att_w_o, ffn_w_in=m_ffn_w_in, ffn_conv_w=m_ffn_conv_w, ffn_conv_b=m_ffn_conv_b,
             ffn_w_out=m_ffn_w_out)
    V = dict(norm_mix=v_norm_mix, norm_ffn=v_norm_ffn, norm_final=v_norm_final, mix_w_in=v_mix_w_in,
             mix_w_out=v_mix_w_out, s5_A_re=v_s5_A_re, s5_A_im=v_s5_A_im, s5_log_dt=v_s5_log_dt, s5_B_re=v_s5_B_re,
             s5_B_im=v_s5_B_im, s5_C_re=v_s5_C_re, s5_C_im=v_s5_C_im, s5_D=v_s5_D, s5_glu_w=v_s5_glu_w,
             s5_glu_b=v_s5_glu_b, hgrn_gamma=v_hgrn_gamma, hgrn_norm=v_hgrn_norm, att_w_qkv=v_att_w_qkv,
             att_w_o=v_att_w_o, ffn_w_in=v_ffn_w_in, ffn_conv_w=v_ffn_conv_w, ffn_conv_b=v_ffn_conv_b,
             ffn_w_out=v_ffn_w_out)
    return _step(x[0], positions[0], loss_target[0], W, M, V)


TRANSPOSED = ("mix_w_in", "att_w_qkv", "ffn_w_in")
SMALL = ("norm_mix", "norm_ffn", "norm_final", "s5_A_re", "s5_A_im", "s5_log_dt", "s5_B_re", "s5_B_im", "s5_C_re",
         "s5_C_im", "s5_D", "s5_glu_b", "hgrn_gamma", "hgrn_norm", "ffn_conv_b")
ORDER = ("norm_mix", "norm_ffn", "norm_final", "mix_w_in", "mix_w_out", "s5_A_re", "s5_A_im", "s5_log_dt", "s5_B_re",
         "s5_B_im", "s5_C_re", "s5_C_im", "s5_D", "s5_glu_w", "s5_glu_b", "hgrn_gamma", "hgrn_norm", "att_w_qkv",
         "att_w_o", "ffn_w_in", "ffn_conv_w", "ffn_conv_b", "ffn_w_out")
PACK_COLS = 1024


def _step(x, positions, target, W, M, V):
    L, D = x.shape
    me = 4 * lax.axis_index("x") + 2 * lax.axis_index("y") + lax.axis_index("c")
    n_cw = W["ffn_conv_w"].shape[-1]
    T = {n: tuple(jnp.swapaxes(d[n], -1, -2) for d in (W, M, V)) for n in TRANSPOSED}
    first = {
        "mix_w_in": cast_bf16(T["mix_w_in"][0][0], name="mix_w_in_cast"),
        "conv_w": W["ffn_conv_w"].reshape(6, n_cw),
        "s5_glu_w": cast_bf16(W["s5_glu_w"][0], name="s5_glu_w_cast"),
    }
    first_handles, token = copies_start(list(first.values()), "gather2", name="gather_start_first")
    shards = {
        "mix_w_out": cast_bf16(W["mix_w_out"][0], dep=token, name="mix_w_out_cast"),
        "ffn_w_in0": cast_bf16_layer(T["ffn_w_in"][0], 0, name="ffn_w_in0_cast"),
        "ffn_w_out0": cast_bf16_layer(W["ffn_w_out"], 0, name="ffn_w_out0_cast"),
        "att_w_qkv": cast_bf16(T["att_w_qkv"][0][0], name="att_w_qkv_cast"),
        "att_w_o": cast_bf16(W["att_w_o"][0], name="att_w_o_cast"),
        "ffn_w_in1": cast_bf16_layer(T["ffn_w_in"][0], 1, name="ffn_w_in1_cast"),
        "ffn_w_out1": cast_bf16_layer(W["ffn_w_out"], 1, name="ffn_w_out1_cast"),
    }
    gather_handles, token = copies_start(list(shards.values()), "gather2", name="gather_start")
    gather_handle = dict(zip(list(first) + list(shards), first_handles + gather_handles))

    def forward(keys, after, name):
        new, sent = copies_forward([gather_handle[k] for k in keys], after, name=name)
        gather_handle.update(zip(keys, new))
        return sent

    def gathered(key, after, cols):
        _, land = copies_wait(gather_handle[key], "gather2", after, name=key + "_gwait")
        return cols_from_shards(land, name=key + "_asm") if cols else land.reshape(-1, land.shape[-1])

    conv_b = W["ffn_conv_b"].reshape(2, 1, -1)

    s5_params = (W["s5_A_re"][0], W["s5_A_im"][0], W["s5_log_dt"][0], W["s5_B_re"][0], W["s5_B_im"][0],
                 W["s5_C_re"][0], W["s5_C_im"][0])
    (a_re, a_im, wb_re, wb_im, wc_re, wc_im), s5_prep_vjp = jax.vjp(_s5_prepare, *s5_params)
    dvec = W["s5_D"].reshape(1, S5_WIDTH)
    glu_b = W["s5_glu_b"].reshape(1, S5_WIDTH)
    lb, lb_vjp = jax.vjp(_lower_bound, W["hgrn_gamma"])
    hg_norm = W["hgrn_norm"].reshape(1, -1)
    tabs = rope_tables(positions)

    sent = forward(["mix_w_in", "conv_w", "s5_glu_w"], token, "forward_a")
    w_mix_in = gathered("mix_w_in", sent, False)
    hn0, proj = norm_mm(x, W["norm_mix"][0], w_mix_in, name="l0_proj")
    y0, xs_re, xs_im = s5_core_fwd(proj, a_re, a_im, wb_re, wb_im, wc_re, wc_im, name="s5_core")
    w_glu = gathered("s5_glu_w", y0, False)
    cat = s5_out_fwd(y0, proj, dvec, w_glu, glu_b, name="s5_out")
    cat, hg_states = hgrn_fwd(proj, lb, hg_norm, cat, name="hgrn_fwd")
    forward(["mix_w_out"], cat, "forward_b")
    w_mix_out = gathered("mix_w_out", cat, False)
    h1 = mm(cat, w_mix_out, res=x, name="l0_mix_out")
    _, cw_all = copies_wait(gather_handle["conv_w"], "gather2", h1, name="conv_w_gwait")
    conv_w = cw_all.transpose(1, 0, 2).reshape(2, 3, N_DEV * n_cw)
    w_ffn_in, w_ffn_out = [None, None], [None, None]
    h2, ffn0_saved, w_ffn_in[0], w_ffn_out[0] = _ffn_fwd(
        h1, W["norm_ffn"][0],
        lambda a: (forward(["ffn_w_in0"], a, "forward_b2"), gathered("ffn_w_in0", a, False))[1], conv_w[0], conv_b[0],
        lambda a: (forward(["ffn_w_out0"], a, "forward_c"), gathered("ffn_w_out0", a, False))[1], "ffn0")

    forward(["att_w_qkv", "att_w_o"], h2, "forward_d")
    w_qkv = gathered("att_w_qkv", h2, False)
    hn2, qkv_r = norm_mm(h2, W["norm_mix"][1], w_qkv, tabs=tabs, name="l1_qkv")
    att_o, att_l = [], []
    for g, d in enumerate(ATT_DILATIONS):
        o_g, l_g = attn_fwd(qkv_r, g, d, name=f"attn_fwd{g}")
        att_o.append(o_g)
        att_l.append(l_g)
    o_att = merge_fwd(att_o, att_l, name="merge_fwd")
    forward(["ffn_w_in1", "ffn_w_out1"], o_att, "forward_e")
    w_o = gathered("att_w_o", o_att, True)
    h3 = mm(o_att, w_o, res=h2, name="l1_mix_out")
    (loss, dh4, dg_final), ffn1_saved, w_ffn_in[1], w_ffn_out[1] = _ffn_fwd(
        h3, W["norm_ffn"][1], lambda a: gathered("ffn_w_in1", a, False), conv_w[1], conv_b[1],
        lambda a: gathered("ffn_w_out1", a, False), "ffn1", final=(W["norm_final"], target))

    exchanges = {}

    pending = []

    def send_grad(key, g, cols, flush=True):
        if cols:
            parts = shards_from_cols(g, name=key + "_split")
        else:
            parts = g.reshape(N_DEV, g.shape[0] // N_DEV, g.shape[1])
        pending.append((key, parts))
        if not flush:
            return None
        handles, sent = copies_start([p for _, p in pending], "exchange", name=key + "_xstart")
        exchanges.update(zip([k for k, _ in pending], handles))
        pending.clear()
        return sent

    dh3, dg_ffn1, dcw1, dcb1 = _ffn_bwd(h3, W["norm_ffn"][1], w_ffn_in[1], conv_w[1], conv_b[1], w_ffn_out[1],
                                        ffn1_saved, dh4, "ffn1", lambda g: send_grad("ffn_w_in1", g, False),
                                        lambda g: send_grad("ffn_w_out1", g, False, flush=False))
    sent = send_grad("att_w_o", mm(o_att, dh3, ta=True, name="l1_dwo"), True, flush=False)
    d_oatt = mm(dh3, w_o, tb=True, dep=sent, name="l1_dmix")
    mb = merge_bwd(att_o, att_l, d_oatt, name="merge_bwd")
    d_slabs = [attn_bwd(qkv_r, g, att_l[g], mb[g], mb[3 + g], d, name=f"attn_bwd{g}")
               for g, d in enumerate(ATT_DILATIONS)]
    d_qkv = rope_bwd([s[0] for s in d_slabs] + [s[1] for s in d_slabs] + [s[2] for s in d_slabs], tabs,
                     name="rope_bwd")
    sent = send_grad("att_w_qkv", mm(d_qkv, hn2, ta=True, out_dtype=BF16, name="l1_dwqkv"), False)
    dh2, dg_mix1 = mm_drms(d_qkv, w_qkv, h2, W["norm_mix"][1], dh3, dep=sent, name="l1_dhn")

    dh1, dg_ffn0, dcw0, dcb0 = _ffn_bwd(h1, W["norm_ffn"][0], w_ffn_in[0], conv_w[0], conv_b[0], w_ffn_out[0],
                                        ffn0_saved, dh2, "ffn0", lambda g: send_grad("ffn_w_in0", g, False),
                                        lambda g: send_grad("ffn_w_out0", g, False, flush=False))
    sent = send_grad("mix_w_out", mm(cat, dh1, ta=True, out_dtype=BF16, name="l0_dwout"), False)
    dcat = mm(dh1, w_mix_out, tb=True, dep=sent, name="l0_dcat")
    dy, du_d, z_bf, dzg, dglu_b, dD = s5_out_bwd(y0, proj, dvec, w_glu, glu_b, dcat, name="s5_dout")
    sent_glu = send_grad("s5_glu_w", mm(z_bf, dzg, ta=True, out_dtype=BF16, name="s5_dglu"), False, flush=False)
    du, dwb_re, dwb_im, dwc_re, dwc_im, da_re, da_im = s5_core_bwd(
        dy, du_d, proj, xs_re, xs_im, a_re, a_im, wb_re, wb_im, wc_re, wc_im, name="s5_dcore")
    s5_small = s5_prep_vjp((da_re, da_im, dwb_re, dwb_im, dwc_re, dwc_im))
    d_proj, dlb, dhg_norm = hgrn_bwd(proj, lb, hg_norm, hg_states, dcat, du, name="hgrn_bwd")
    sent = send_grad("mix_w_in", mm(d_proj, hn0, ta=True, out_dtype=BF16, dep=sent_glu, name="l0_dwin"), False)
    grad_x, dg_mix0 = mm_drms(d_proj, w_mix_in, x, W["norm_mix"][0], dh1, dep=sent, name="l0_dhn")
    (d_gamma,) = lb_vjp(dlb)
    out = {}

    dA_re, dA_im, dlog_dt, dB_re, dB_im, dC_re, dC_im = s5_small
    small_g = dict(norm_mix=jnp.concatenate([dg_mix0, dg_mix1], axis=0), norm_ffn=jnp.concatenate([dg_ffn0, dg_ffn1], axis=0),
                   norm_final=dg_final, s5_A_re=dA_re, s5_A_im=dA_im, s5_log_dt=dlog_dt, s5_B_re=dB_re, s5_B_im=dB_im,
                   s5_C_re=dC_re, s5_C_im=dC_im, s5_D=dD, s5_glu_b=dglu_b, hgrn_gamma=d_gamma, hgrn_norm=dhg_norm,
                   ffn_conv_b=jnp.concatenate([dcb0, dcb1], axis=0))
    conv_w_g = jnp.stack([dcw0, dcw1], axis=0)
    sizes = [math.prod(W[n].shape) for n in SMALL]
    n_conv = conv_w_g.size
    total = sum(sizes) + n_conv + 1
    rows = -(-total // PACK_COLS)
    rows = -(-rows // 8) * 8
    pad = rows * PACK_COLS - total

    def pack(vals, conv_part, last):
        flat = [v.reshape(-1).astype(F32) for v in vals] + [conv_part.reshape(-1), last.reshape(-1),
                                                            jnp.zeros((pad,), F32)]
        return jnp.concatenate(flat).reshape(rows, PACK_COLS)

    def conv_full(shard):
        col_owner = lax.broadcasted_iota(jnp.int32, (2, 3, N_DEV * n_cw), 2) // n_cw
        return jnp.where(col_owner == me, jnp.tile(shard, (1, 1, N_DEV)), 0.0)

    zero1 = jnp.zeros((1,), F32)
    g_pack = pack([small_g[n] for n in SMALL], conv_w_g, loss)
    w_pack = pack([W[n] for n in SMALL], conv_full(W["ffn_conv_w"]), zero1)
    m_pack = pack([M[n] for n in SMALL], conv_full(M["ffn_conv_w"]), zero1)
    v_pack = pack([V[n] for n in SMALL], conv_full(V["ffn_conv_w"]), zero1 + 1.0)
    (small_handle,), small_sent = copies_start([g_pack], "gather", name="small_xstart")

    def finish(name, n_layers):
        w3, m3, v3 = T[name] if name in TRANSPOSED else (W[name], M[name], V[name])
        res = None
        for layer in reversed(range(n_layers)):
            key = name if n_layers == 1 else f"{name}{layer}"
            own, recv = copies_wait(exchanges[key], "exchange", small_sent, name=key + "_xwait")
            _, R, Cn = recv.shape
            res = reduce_adamw(recv, own, True, me, w3.reshape(n_layers * R, Cn), m3.reshape(n_layers * R, Cn),
                               v3.reshape(n_layers * R, Cn), layer=layer, n_layers=n_layers, into=res,
                               name=key + "_adamw")
        res = [r.reshape(w3.shape) for r in res]
        return tuple(jnp.swapaxes(r, -1, -2) for r in res) if name in TRANSPOSED else tuple(res)

    for name in ("ffn_w_out", "ffn_w_in"):
        out[name] = finish(name, 2)
    for name in ("att_w_o", "att_w_qkv", "mix_w_out", "s5_glu_w", "mix_w_in"):
        out[name] = finish(name, 1)

    small_own, small_recv = copies_wait(small_handle, "gather", out["s5_glu_w"][0], name="small_xwait")
    res = reduce_adamw(small_recv, small_own, False, me, w_pack, m_pack, v_pack, name="small_adamw")
    flat = [r.reshape(-1) for r in res]
    off = 0
    for n, sz in zip(SMALL, sizes):
        out[n] = tuple(f[off:off + sz].reshape(W[n].shape) for f in flat)
        off += sz
    conv_res = [f[off:off + n_conv].reshape(2, 3, N_DEV * n_cw) for f in flat]
    out["ffn_conv_w"] = tuple(lax.dynamic_slice(c, (0, 0, me * n_cw), (2, 3, n_cw)) for c in conv_res)
    off += n_conv
    loss_total = flat[0][off]

    result = [loss_total, grad_x[None]]
    for k in range(4):
        result += [out[n][k] for n in ORDER]
    return tuple(result)
```
